```python
import jax, jax.numpy as jnp
from jax import lax
import numpy as np

D_MODEL = 1024
BATCH = 8
SEQ = 2048
DEPTH = 4

CONV_WIDTH = D_MODEL // 2
CONV_HEADS = 8
CONV_K = 3
POOL_WIDTH = D_MODEL // 2
POOL_WINDOWS = (2, 4, 8, 16)
POOL_GROUPS = len(POOL_WINDOWS)
POOL_GROUP_DIM = POOL_WIDTH // POOL_GROUPS
MIX_WIDTH = CONV_WIDTH + POOL_WIDTH
IN_COLS = 4 * CONV_WIDTH + 2 * POOL_WIDTH
NORM_EPS = 1e-6

kernel_name = "hybrid_shortconv_pool_parallel_adaln"


def rms_norm(x, g):
    xf = x.astype(jnp.float32)
    y = xf * lax.rsqrt(jnp.mean(xf * xf, axis=-1, keepdims=True) + NORM_EPS)
    return (y * g.astype(jnp.float32)).astype(x.dtype)


def causal_depthwise_conv(z, w):
    C = z.shape[-1]
    return lax.conv_general_dilated(
        z, w[:, None, :].astype(z.dtype),
        window_strides=(1,), padding=[(CONV_K - 1, 0)],
        dimension_numbers=("NWC", "WIO", "NWC"),
        feature_group_count=C)


def causal_multiscale_pool(p, w_pool, pool_scale):
    B, T, _ = p.shape
    pf = p.astype(jnp.float32)
    pos = jnp.arange(1, T + 1, dtype=jnp.float32)[None, :, None]
    outs = []
    for g, w in enumerate(POOL_WINDOWS):
        pg = pf[..., g * POOL_GROUP_DIM:(g + 1) * POOL_GROUP_DIM]
        s = jnp.cumsum(pg, axis=1)
        lag = jnp.pad(s, ((0, 0), (w, 0), (0, 0)))[:, :T]
        mean = (s - lag) / jnp.minimum(pos, float(w))
        outs.append(mean - pg)
    pooled = jnp.stack(outs, axis=2).astype(p.dtype)
    mixed = jnp.einsum("btgc,gcd->btgd", pooled, w_pool)
    return mixed.reshape(B, T, POOL_WIDTH) * pool_scale


def _fwd_setup_inputs(seed: int = 0) -> dict:
    key = jax.random.key(seed)
    ks = jax.random.split(key, 12)
    f32 = jnp.float32
    x = jax.random.normal(ks[0], (BATCH, SEQ, D_MODEL), f32)
    c = jax.random.normal(ks[1], (BATCH, D_MODEL), f32)
    w_ada = jax.random.normal(ks[2], (DEPTH, D_MODEL, 3 * D_MODEL), f32) * D_MODEL ** -0.5
    b_ada = 0.01 * jax.random.normal(ks[3], (DEPTH, 3 * D_MODEL), f32)
    g_pre = 1.0 + 0.02 * jax.random.normal(ks[4], (DEPTH, D_MODEL), f32)
    w_in = jax.random.normal(ks[5], (DEPTH, D_MODEL, IN_COLS), f32) * D_MODEL ** -0.5
    w_conv = jax.random.normal(ks[6], (DEPTH, CONV_K, CONV_WIDTH), f32) * CONV_K ** -0.5
    w_pool = jax.random.normal(ks[7], (DEPTH, POOL_GROUPS, POOL_GROUP_DIM, POOL_GROUP_DIM), f32) * POOL_GROUP_DIM ** -0.5
    pool_scale = 1.0 + 0.1 * jax.random.normal(ks[8], (DEPTH, POOL_WIDTH), f32)
    w_out = jax.random.normal(ks[9], (DEPTH, MIX_WIDTH, D_MODEL), f32) * MIX_WIDTH ** -0.5
    g_post = 1.0 + 0.02 * jax.random.normal(ks[10], (DEPTH, D_MODEL), f32)
    return {"x": x, "c": c, "w_ada": w_ada, "b_ada": b_ada, "g_pre": g_pre, "w_in": w_in,
            "w_conv": w_conv, "w_pool": w_pool, "pool_scale": pool_scale,
            "w_out": w_out, "g_post": g_post}


def _fwd_reference(x, c, w_ada, b_ada, g_pre, w_in, w_conv, w_pool, pool_scale, w_out, g_post):
    B, T, D = x.shape
    c_act = jax.nn.silu(c)
    cw = CONV_WIDTH
    for l in range(DEPTH):
        mod = c_act @ w_ada[l] + b_ada[l]
        shift, scale, gate = jnp.split(mod, 3, axis=-1)
        h = rms_norm(x, g_pre[l]) * (1.0 + scale[:, None, :]) + shift[:, None, :]
        proj = h @ w_in[l]
        u_a = proj[..., 0 * cw:1 * cw]
        b_a = proj[..., 1 * cw:2 * cw]
        c_a = proj[..., 2 * cw:3 * cw]
        gate_a = proj[..., 3 * cw:4 * cw]
        o = 4 * cw
        u_p = proj[..., o:o + POOL_WIDTH]
        gate_p = proj[..., o + POOL_WIDTH:o + 2 * POOL_WIDTH]
        y_a = b_a * causal_depthwise_conv(c_a * u_a, w_conv[l]) * jax.nn.silu(gate_a)
        y_p = causal_multiscale_pool(u_p, w_pool[l], pool_scale[l]) * jax.nn.silu(gate_p)
        y = jnp.concatenate([y_a, y_p], axis=-1) @ w_out[l]
        x = x + gate[:, None, :] * rms_norm(y, g_post[l])
    return x


import jax as _jax
import jax.numpy as _jnp

TWIN_FORMAT = 'train_step'
FWD_PARAMS = ['x', 'c', 'w_ada', 'b_ada', 'g_pre', 'w_in', 'w_conv', 'w_pool', 'pool_scale', 'w_out', 'g_post']
TWIN_WEIGHTS = ['w_ada', 'b_ada', 'g_pre', 'w_in', 'w_conv', 'w_pool', 'pool_scale', 'w_out', 'g_post']
TWIN_DIFF_INPUT = 'x'
TWIN_INPUTS = ['x', 'c', 'w_ada', 'b_ada', 'g_pre', 'w_in', 'w_conv', 'w_pool', 'pool_scale', 'w_out', 'g_post', 'loss_target', 'm_w_ada', 'm_b_ada', 'm_g_pre', 'm_w_in', 'm_w_conv', 'm_w_pool', 'm_pool_scale', 'm_w_out', 'm_g_post', 'v_w_ada', 'v_b_ada', 'v_g_pre', 'v_w_in', 'v_w_conv', 'v_w_pool', 'v_pool_scale', 'v_w_out', 'v_g_post']
TWIN_OUTPUTS = ['loss', 'grad_x', 'grad_w_ada', 'grad_b_ada', 'grad_g_pre', 'grad_w_in', 'grad_w_conv', 'grad_w_pool', 'grad_pool_scale', 'grad_w_out', 'grad_g_post', 'delta_w_ada', 'delta_b_ada', 'delta_g_pre', 'delta_w_in', 'delta_w_conv', 'delta_w_pool', 'delta_pool_scale', 'delta_w_out', 'delta_g_post', 'new_m_w_ada', 'new_m_b_ada', 'new_m_g_pre', 'new_m_w_in', 'new_m_w_conv', 'new_m_w_pool', 'new_m_pool_scale', 'new_m_w_out', 'new_m_g_post', 'new_v_w_ada', 'new_v_b_ada', 'new_v_g_pre', 'new_v_w_in', 'new_v_w_conv', 'new_v_w_pool', 'new_v_pool_scale', 'new_v_w_out', 'new_v_g_post']
TWIN_LEAF_KINDS = {'loss': 'loss', 'grad_x': 'grad_x', 'grad_w_ada': 'grad_w', 'grad_b_ada': 'grad_w', 'grad_g_pre': 'grad_w', 'grad_w_in': 'grad_w', 'grad_w_conv': 'grad_w', 'grad_w_pool': 'grad_w', 'grad_pool_scale': 'grad_w', 'grad_w_out': 'grad_w', 'grad_g_post': 'grad_w', 'delta_w_ada': 'delta_w', 'delta_b_ada': 'delta_w', 'delta_g_pre': 'delta_w', 'delta_w_in': 'delta_w', 'delta_w_conv': 'delta_w', 'delta_w_pool': 'delta_w', 'delta_pool_scale': 'delta_w', 'delta_w_out': 'delta_w', 'delta_g_post': 'delta_w', 'new_m_w_ada': 'new_m', 'new_m_b_ada': 'new_m', 'new_m_g_pre': 'new_m', 'new_m_w_in': 'new_m', 'new_m_w_conv': 'new_m', 'new_m_w_pool': 'new_m', 'new_m_pool_scale': 'new_m', 'new_m_w_out': 'new_m', 'new_m_g_post': 'new_m', 'new_v_w_ada': 'new_v', 'new_v_b_ada': 'new_v', 'new_v_g_pre': 'new_v', 'new_v_w_in': 'new_v', 'new_v_w_conv': 'new_v', 'new_v_w_pool': 'new_v', 'new_v_pool_scale': 'new_v', 'new_v_w_out': 'new_v', 'new_v_g_post': 'new_v'}


def _forward(args):
    return _fwd_reference(*[args[k] for k in FWD_PARAMS])


def _output_shape():
    out = _jax.eval_shape(lambda: _forward(_fwd_setup_inputs(0)))
    return out.shape, out.dtype

N_MICROBATCH = 1
ADAM_LR = 0.001
ADAM_B1 = 0.9
ADAM_B2 = 0.999
ADAM_EPS = 1e-08
ADAM_WD = 0.01
ADAM_STEP = 10
PER_EXAMPLE_BATCH_AXIS = {'x': 0, 'c': 0, 'loss_target': 0}
SHARED_INPUTS = []
_WEIGHT_DTYPES = {'w_ada': _jnp.float32, 'b_ada': _jnp.float32, 'g_pre': _jnp.float32, 'w_in': _jnp.float32, 'w_conv': _jnp.float32, 'w_pool': _jnp.float32, 'pool_scale': _jnp.float32, 'w_out': _jnp.float32, 'g_post': _jnp.float32}
MOMENT_SCALE = {'w_ada': 1.714570e+00, 'b_ada': 3.170460e+00, 'g_pre': 4.307999e-01, 'w_in': 2.796189e-01, 'w_conv': 3.400671e-01, 'w_pool': 1.566136e-01, 'pool_scale': 1.808928e-01, 'w_out': 3.067608e-01, 'g_post': 7.402807e+00}


def _to_microbatches(a, axis):
    t = _jnp.moveaxis(a, axis, 0)
    t = t.reshape((N_MICROBATCH, t.shape[0] // N_MICROBATCH) + t.shape[1:])
    return _jnp.moveaxis(t, 1, axis + 1)


def setup_inputs(seed: int = 0) -> dict:
    inp = _fwd_setup_inputs(seed)
    key = _jax.random.fold_in(_jax.random.key(seed), 7919)
    shape, _ = _output_shape()
    out = dict(inp)
    out["loss_target"] = _jax.random.normal(_jax.random.fold_in(key, 0), shape, _jnp.float32)
    for i, name in enumerate(TWIN_WEIGHTS):
        w = inp[name].astype(_jnp.float32)
        if MOMENT_SCALE is None:
            s = _jnp.sqrt(_jnp.mean(_jnp.square(w)) + 1e-30)
        else:
            s = MOMENT_SCALE[name]
        km, kv = _jax.random.split(_jax.random.fold_in(key, i + 1))
        out[name] = w
        out["m_" + name] = s * _jax.random.normal(km, w.shape, _jnp.float32)
        out["v_" + name] = (s * s) * _jax.random.uniform(kv, w.shape, _jnp.float32, 0.5, 1.5)
    if N_MICROBATCH > 1:
        for name, axis in PER_EXAMPLE_BATCH_AXIS.items():
            out[name] = _to_microbatches(out[name], axis)
    return {'x': out['x'], 'c': out['c'], 'w_ada': out['w_ada'], 'b_ada': out['b_ada'], 'g_pre': out['g_pre'], 'w_in': out['w_in'], 'w_conv': out['w_conv'], 'w_pool': out['w_pool'], 'pool_scale': out['pool_scale'], 'w_out': out['w_out'], 'g_post': out['g_post'], 'loss_target': out['loss_target'], 'm_w_ada': out['m_w_ada'], 'm_b_ada': out['m_b_ada'], 'm_g_pre': out['m_g_pre'], 'm_w_in': out['m_w_in'], 'm_w_conv': out['m_w_conv'], 'm_w_pool': out['m_w_pool'], 'm_pool_scale': out['m_pool_scale'], 'm_w_out': out['m_w_out'], 'm_g_post': out['m_g_post'], 'v_w_ada': out['v_w_ada'], 'v_b_ada': out['v_b_ada'], 'v_g_pre': out['v_g_pre'], 'v_w_in': out['v_w_in'], 'v_w_conv': out['v_w_conv'], 'v_w_pool': out['v_w_pool'], 'v_pool_scale': out['v_pool_scale'], 'v_w_out': out['v_w_out'], 'v_g_post': out['v_g_post']}


def _loss(weights, diff, rest, loss_target):
    with _jax.named_scope("forward"):
        args = {**rest, TWIN_DIFF_INPUT: diff, **{k: w.astype(_WEIGHT_DTYPES[k]) for k, w in weights.items()}}
        y = _forward(args)
    with _jax.named_scope("loss_head"):
        err = _jnp.square(y.astype(_jnp.float32) - loss_target)
        return 0.5 * _jnp.sum(_jnp.mean(err, axis=-1)) if err.ndim else 0.5 * err


def _adamw(w, g, m, v):
    m = ADAM_B1 * m + (1.0 - ADAM_B1) * g
    v = ADAM_B2 * v + (1.0 - ADAM_B2) * _jnp.square(g)
    m_hat = m / (1.0 - ADAM_B1 ** ADAM_STEP)
    v_hat = v / (1.0 - ADAM_B2 ** ADAM_STEP)
    delta = -ADAM_LR * (m_hat / (_jnp.sqrt(v_hat) + ADAM_EPS) + ADAM_WD * w)
    return delta, m, v


def reference(x, c, w_ada, b_ada, g_pre, w_in, w_conv, w_pool, pool_scale, w_out, g_post, loss_target, m_w_ada, m_b_ada, m_g_pre, m_w_in, m_w_conv, m_w_pool, m_pool_scale, m_w_out, m_g_post, v_w_ada, v_b_ada, v_g_pre, v_w_in, v_w_conv, v_w_pool, v_pool_scale, v_w_out, v_g_post):
    given = dict(x=x, c=c, w_ada=w_ada, b_ada=b_ada, g_pre=g_pre, w_in=w_in, w_conv=w_conv, w_pool=w_pool, pool_scale=pool_scale, w_out=w_out, g_post=g_post, loss_target=loss_target, m_w_ada=m_w_ada, m_b_ada=m_b_ada, m_g_pre=m_g_pre, m_w_in=m_w_in, m_w_conv=m_w_conv, m_w_pool=m_w_pool, m_pool_scale=m_pool_scale, m_w_out=m_w_out, m_g_post=m_g_post, v_w_ada=v_w_ada, v_b_ada=v_b_ada, v_g_pre=v_g_pre, v_w_in=v_w_in, v_w_conv=v_w_conv, v_w_pool=v_w_pool, v_pool_scale=v_pool_scale, v_w_out=v_w_out, v_g_post=v_g_post)
    weights = {n: given[n] for n in TWIN_WEIGHTS}
    shared = {n: given[n] for n in SHARED_INPUTS}
    per_example = {n: given[n] for n in ['x', 'c']}
    grad_fn = _jax.value_and_grad(_loss, argnums=(0, 1))

    def one_microbatch(ex, loss_target):
        ex = dict(ex)
        diff = ex.pop(TWIN_DIFF_INPUT)
        return grad_fn(weights, diff, {**shared, **ex}, loss_target)

    if N_MICROBATCH == 1:
        loss, (grad_w, grad_x) = one_microbatch(per_example, given["loss_target"])
    else:
        def body(carry, xs):
            loss_sum, grad_sum = carry
            l_k, (gw_k, gx_k) = one_microbatch(xs[0], xs[1])
            with _jax.named_scope("update"):
                return (loss_sum + l_k, _jax.tree.map(_jnp.add, grad_sum, gw_k)), gx_k

        init = (_jnp.zeros((), _jnp.float32), _jax.tree.map(_jnp.zeros_like, weights))
        (loss, grad_w), grad_x = _jax.lax.scan(body, init, (per_example, given["loss_target"]))
    with _jax.named_scope("update"):
        delta_w, new_m, new_v = {}, {}, {}
        for n in TWIN_WEIGHTS:
            delta_w[n], new_m[n], new_v[n] = _adamw(weights[n], grad_w[n], given["m_" + n], given["v_" + n])
    return (loss, grad_x, *[grad_w[n] for n in TWIN_WEIGHTS], *[delta_w[n] for n in TWIN_WEIGHTS],
            *[new_m[n] for n in TWIN_WEIGHTS], *[new_v[n] for n in TWIN_WEIGHTS])
```

```python
import functools

import jax
import jax.numpy as jnp
from jax import lax
from jax.experimental import pallas as pl
from jax.experimental.pallas import tpu as pltpu

F32 = jnp.float32
BF16 = jnp.bfloat16
MESH = pl.DeviceIdType.MESH
ANY = pl.BlockSpec(memory_space=pl.ANY)

NORM_EPS = 1e-6
POOL_WINDOWS = (2, 4, 8, 16)
ADAM_LR = 0.001
ADAM_B1 = 0.9
ADAM_B2 = 0.999
ADAM_EPS = 1e-08
ADAM_WD = 0.01
ADAM_STEP = 10

N_CHIPS = 4
N_DEV = 8
LANES = 128
SUBLANES = 8
VMEM_BIG = 56 * 1024 * 1024
HIST = 16
R_CONV = 32
R_POOL = 64
PACK_ROWS = 64

NT = (((1,), (1,)), ((), ()))
TN = (((0,), (0,)), ((), ()))


def _params(vmem=None, n_grid=1):
    kw = {}
    if n_grid:
        kw["dimension_semantics"] = ("arbitrary",) * n_grid
    if vmem is not None:
        kw["vmem_limit_bytes"] = vmem
    return pltpu.CompilerParams(**kw)


def _colsum8(v):
    n, d = v.shape
    return v.reshape(n // SUBLANES, SUBLANES, d).sum(axis=0)


def _rms(v):
    return lax.rsqrt(jnp.mean(v * v, axis=-1, keepdims=True) + NORM_EPS)


def _shift_down(ext, k, rows):
    if k == 0:
        return ext[HIST:HIST + rows]
    return pltpu.roll(ext, k, 0)[HIST:HIST + rows]


def _shift_up(ext, k, rows):
    if k == 0:
        return ext[0:rows]
    return pltpu.roll(ext, ext.shape[0] - k, 0)[0:rows]


def _load_ext(ref, r0, h0, first, rows):
    hist = ref[pl.ds(h0, HIST), :].astype(F32)
    hist = jnp.where(first, 0.0, hist)
    cur = ref[pl.ds(r0, rows), :].astype(F32)
    return jnp.concatenate([hist, cur], axis=0)


def _me():
    return lax.axis_index("x"), lax.axis_index("y"), lax.axis_index("c")


def cast_weights(w_in, w_out):
    L = w_in.shape[0]

    def body(wi, wo, oi, oo):
        oi[...] = wi[...].astype(BF16)
        oo[...] = wo[...].astype(BF16)

    bi = (1,) + w_in.shape[1:]
    bo = (1,) + w_out.shape[1:]
    return pl.pallas_call(
        body, name="cast_w", grid=(L,),
        in_specs=[pl.BlockSpec(bi, lambda l: (l, 0, 0)), pl.BlockSpec(bo, lambda l: (l, 0, 0))],
        out_specs=[pl.BlockSpec(bi, lambda l: (l, 0, 0)), pl.BlockSpec(bo, lambda l: (l, 0, 0))],
        out_shape=[jax.ShapeDtypeStruct(w_in.shape, BF16), jax.ShapeDtypeStruct(w_out.shape, BF16)],
        compiler_params=_params(VMEM_BIG),
    )(w_in, w_out)


def mod_part(c_all, w_ada, b_my):
    L, D, CW = w_ada.shape

    def body(c_ref, w_ref, b_ref, o_ref):
        cv = c_ref[...]
        ca = (cv * jax.nn.sigmoid(cv)).astype(BF16)
        o_ref[0] = jnp.dot(ca, w_ref[0].astype(BF16), preferred_element_type=F32) + b_ref[0]

    return pl.pallas_call(
        body, name="mod_part", grid=(L,),
        in_specs=[pl.BlockSpec((N_DEV, D), lambda l: (0, 0)),
                  pl.BlockSpec((1, D, CW), lambda l: (l, 0, 0)),
                  pl.BlockSpec((1, 1, CW), lambda l: (l, 0, 0))],
        out_specs=pl.BlockSpec((1, N_DEV, CW), lambda l: (l, 0, 0)),
        out_shape=jax.ShapeDtypeStruct((L, N_DEV, CW), F32),
        compiler_params=_params(VMEM_BIG),
    )(c_all, w_ada, b_my.reshape(L, 1, CW))


def proj_fwd(x, shift, scale, g_pre, wg):
    T, D = x.shape
    NB, _, CW = wg.shape
    tm = 512

    def body(x_ref, sh_ref, sc_ref, g_ref, w_ref, o_ref):
        xv = x_ref[...]
        h = (xv * _rms(xv) * g_ref[...]) * (1.0 + sc_ref[...]) + sh_ref[...]
        hb = h.astype(BF16)
        for j in range(NB):
            o_ref[:, j * CW:(j + 1) * CW] = jnp.dot(hb, w_ref[j], preferred_element_type=F32).astype(BF16)

    row = pl.BlockSpec((1, D), lambda i: (0, 0))
    return pl.pallas_call(
        body, name="proj_fwd", grid=(T // tm,),
        in_specs=[pl.BlockSpec((tm, D), lambda i: (i, 0)), row, row, row,
                  pl.BlockSpec((NB, D, CW), lambda i: (0, 0, 0))],
        out_specs=pl.BlockSpec((tm, NB * CW), lambda i: (i, 0)),
        out_shape=jax.ShapeDtypeStruct((T, NB * CW), BF16),
        compiler_params=_params(VMEM_BIG),
    )(x, shift, scale, g_pre, wg)


def conv_fwd(proj, wconv):
    T = proj.shape[0]
    R = R_CONV
    nblk = 4

    def body(u_ref, b_ref, c_ref, g_ref, w_ref, o_ref):
        w0 = w_ref[pl.ds(0, 1), :]
        w1 = w_ref[pl.ds(1, 1), :]
        w2 = w_ref[pl.ds(2, 1), :]

        def chunk(i, carry):
            r0 = pl.multiple_of(i * R, R)
            h0 = pl.multiple_of(jnp.maximum(r0 - HIST, 0), HIST)
            first = i == 0
            ca = _load_ext(c_ref, r0, h0, first, R) * _load_ext(u_ref, r0, h0, first, R)
            conv = w2 * ca[HIST:] + w1 * _shift_down(ca, 1, R) + w0 * _shift_down(ca, 2, R)
            g = g_ref[pl.ds(r0, R), :].astype(F32)
            b = b_ref[pl.ds(r0, R), :].astype(F32)
            o_ref[pl.ds(r0, R), :] = (b * conv * (g * jax.nn.sigmoid(g))).astype(BF16)
            return carry

        lax.fori_loop(0, T // R, chunk, 0)

    def col(off):
        return pl.BlockSpec((T, LANES), lambda j: (0, j + off))

    return pl.pallas_call(
        body, name="conv_fwd", grid=(nblk,),
        in_specs=[col(0), col(4), col(8), col(12), pl.BlockSpec((None, 3, LANES), lambda j: (j, 0, 0))],
        out_specs=pl.BlockSpec((T, LANES), lambda j: (0, j)),
        out_shape=jax.ShapeDtypeStruct((T, nblk * LANES), BF16),
        compiler_params=_params(),
    )(proj, proj, proj, proj, wconv)


def _causal_window_sum(ext, w):
    s, k = ext, 1
    while k < w:
        s = s + pltpu.roll(s, k, 0)
        k *= 2
    return s


def _anticausal_window_sum(ext, w):
    s, k = ext, 1
    n = ext.shape[0]
    while k < w:
        s = s + pltpu.roll(s, n - k, 0)
        k *= 2
    return s


def _count(r0, rows, w):
    t = r0 + lax.broadcasted_iota(jnp.int32, (rows, LANES), 0)
    return jnp.minimum(t + 1, w).astype(F32)


def _pooled_loop(p_ref, pooled_s, w, T):
    R = R_POOL

    def chunk(i, carry):
        r0 = pl.multiple_of(i * R, R)
        h0 = pl.multiple_of(jnp.maximum(r0 - HIST, 0), HIST)
        ext = _load_ext(p_ref, r0, h0, i == 0, R)
        ws = _causal_window_sum(ext, w)[HIST:]
        pooled_s[pl.ds(r0, R), :] = (ws / _count(r0, R, w) - ext[HIST:]).astype(BF16)
        return carry

    lax.fori_loop(0, T // R, chunk, 0)


def pool_fwd(proj, wpool, pscale):
    T = proj.shape[0]
    R = R_POOL
    ngrp = len(POOL_WINDOWS)

    def body(p_ref, g_ref, w_ref, s_ref, o_ref, pooled_s, mixed_s):
        grp = pl.program_id(0)

        def group(w):
            _pooled_loop(p_ref, pooled_s, w, T)
            mixed_s[...] = jnp.dot(pooled_s[...], w_ref[...].astype(BF16), preferred_element_type=F32)
            sc = s_ref[...]

            def chunk(i, carry):
                r0 = pl.multiple_of(i * R, R)
                g = g_ref[pl.ds(r0, R), :].astype(F32)
                o_ref[pl.ds(r0, R), :] = (mixed_s[pl.ds(r0, R), :] * sc * (g * jax.nn.sigmoid(g))).astype(BF16)
                return carry

            lax.fori_loop(0, T // R, chunk, 0)

        for k, w in enumerate(POOL_WINDOWS):
            pl.when(grp == k)(functools.partial(group, w))

    return pl.pallas_call(
        body, name="pool_fwd", grid=(ngrp,),
        in_specs=[pl.BlockSpec((T, LANES), lambda j: (0, j + 16)), pl.BlockSpec((T, LANES), lambda j: (0, j + 20)),
                  pl.BlockSpec((None, LANES, LANES), lambda j: (j, 0, 0)), pl.BlockSpec((1, LANES), lambda j: (0, j))],
        out_specs=pl.BlockSpec((T, LANES), lambda j: (0, j)),
        out_shape=jax.ShapeDtypeStruct((T, ngrp * LANES), BF16),
        scratch_shapes=[pltpu.VMEM((T, LANES), BF16), pltpu.VMEM((T, LANES), F32)],
        compiler_params=_params(),
    )(proj, proj, wpool, pscale)


def out_fwd(ya, yp, wo, x, gate, g_post):
    T, D = x.shape
    H = ya.shape[1]
    tm = 512

    def body(ya_ref, yp_ref, wo_ref, x_ref, gt_ref, g_ref, xn_ref, y_ref):
        y = (jnp.dot(ya_ref[...], wo_ref[0:H, :], preferred_element_type=F32)
             + jnp.dot(yp_ref[...], wo_ref[H:2 * H, :], preferred_element_type=F32))
        xn_ref[...] = x_ref[...] + gt_ref[...] * (y * _rms(y) * g_ref[...])
        y_ref[...] = y

    row = pl.BlockSpec((1, D), lambda i: (0, 0))
    tile = pl.BlockSpec((tm, D), lambda i: (i, 0))
    half = pl.BlockSpec((tm, H), lambda i: (i, 0))
    return pl.pallas_call(
        body, name="out_fwd", grid=(T // tm,),
        in_specs=[half, half, pl.BlockSpec((2 * H, D), lambda i: (0, 0)), tile, row, row],
        out_specs=[tile, tile],
        out_shape=[jax.ShapeDtypeStruct((T, D), F32), jax.ShapeDtypeStruct((T, D), F32)],
        compiler_params=_params(VMEM_BIG),
    )(ya, yp, wo, x, gate, g_post)


def loss_head(xl, target):
    T, D = xl.shape
    tm = 512
    nt = T // tm

    def body(x_ref, t_ref, dx_ref, l_ref, acc):
        i = pl.program_id(0)

        @pl.when(i == 0)
        def _():
            acc[...] = jnp.zeros_like(acc)

        d = x_ref[...] - t_ref[...]
        dx_ref[...] = d * (1.0 / D)
        acc[...] += _colsum8(d * d)

        @pl.when(i == nt - 1)
        def _():
            l_ref[...] = jnp.zeros_like(l_ref) + jnp.sum(acc[...]) * (0.5 / D)

    tile = pl.BlockSpec((tm, D), lambda i: (i, 0))
    return pl.pallas_call(
        body, name="loss_head", grid=(nt,),
        in_specs=[tile, tile],
        out_specs=[tile, pl.BlockSpec((SUBLANES, LANES), lambda i: (0, 0))],
        out_shape=[jax.ShapeDtypeStruct((T, D), F32), jax.ShapeDtypeStruct((SUBLANES, LANES), F32)],
        scratch_shapes=[pltpu.VMEM((SUBLANES, D), F32)],
        compiler_params=_params(VMEM_BIG),
    )(xl, target)


def out_bwd(dx, y, ya, yp, wo, gate, g_post):
    T, D = dx.shape
    H = ya.shape[1]
    tm = 512
    nt = T // tm

    def body(dx_ref, y_ref, ya_ref, yp_ref, wo_ref, gt_ref, g_ref,
             dya_ref, dyp_ref, dwo_ref, dgt_ref, dg_ref, acc_w, acc_gt, acc_g):
        i = pl.program_id(0)

        @pl.when(i == 0)
        def _():
            acc_w[...] = jnp.zeros_like(acc_w)
            acc_gt[...] = jnp.zeros_like(acc_gt)
            acc_g[...] = jnp.zeros_like(acc_g)

        yv = y_ref[...]
        dxv = dx_ref[...]
        g = g_ref[...]
        r = _rms(yv)
        yn = yv * r
        acc_gt[...] += _colsum8(dxv * (yn * g))
        dn = dxv * gt_ref[...]
        acc_g[...] += _colsum8(dn * yn)
        a = dn * g
        dy = r * (a - yn * jnp.mean(a * yn, axis=-1, keepdims=True))
        dyb = dy.astype(BF16)
        dyc = lax.dot_general(dyb, wo_ref[...], NT, preferred_element_type=F32)
        dya_ref[...] = dyc[:, 0:H].astype(BF16)
        dyp_ref[...] = dyc[:, H:2 * H].astype(BF16)
        acc_w[0:H, :] += lax.dot_general(ya_ref[...], dyb, TN, preferred_element_type=F32)
        acc_w[H:2 * H, :] += lax.dot_general(yp_ref[...], dyb, TN, preferred_element_type=F32)

        @pl.when(i == nt - 1)
        def _():
            dwo_ref[...] = acc_w[...].astype(BF16)
            dgt_ref[...] = jnp.sum(acc_gt[...], axis=0, keepdims=True)
            dg_ref[...] = jnp.sum(acc_g[...], axis=0, keepdims=True)

    row = pl.BlockSpec((1, D), lambda i: (0, 0))
    tile = pl.BlockSpec((tm, D), lambda i: (i, 0))
    half = pl.BlockSpec((tm, H), lambda i: (i, 0))
    full = pl.BlockSpec((2 * H, D), lambda i: (0, 0))
    return pl.pallas_call(
        body, name="out_bwd", grid=(nt,),
        in_specs=[tile, tile, half, half, full, row, row],
        out_specs=[half, half, full, row, row],
        out_shape=[jax.ShapeDtypeStruct((T, H), BF16), jax.ShapeDtypeStruct((T, H), BF16),
                   jax.ShapeDtypeStruct((2 * H, D), BF16),
                   jax.ShapeDtypeStruct((1, D), F32), jax.ShapeDtypeStruct((1, D), F32)],
        scratch_shapes=[pltpu.VMEM((2 * H, D), F32), pltpu.VMEM((SUBLANES, D), F32), pltpu.VMEM((SUBLANES, D), F32)],
        compiler_params=_params(VMEM_BIG),
    )(dx, y, ya, yp, wo, gate, g_post)


def conv_bwd(proj, dya, wconv):
    T = proj.shape[0]
    R = R_CONV
    nblk = 4
    nchunk = T // R

    def body(u_ref, b_ref, c_ref, g_ref, dy_ref, w_ref, du_ref, db_ref, dc_ref, dg_ref, dw_ref):
        w0 = w_ref[pl.ds(0, 1), :]
        w1 = w_ref[pl.ds(1, 1), :]
        w2 = w_ref[pl.ds(2, 1), :]

        def chunk(k, carry):
            head, a0, a1, a2 = carry
            i = nchunk - 1 - k
            r0 = pl.multiple_of(i * R, R)
            h0 = pl.multiple_of(jnp.maximum(r0 - HIST, 0), HIST)
            first = i == 0
            ue = _load_ext(u_ref, r0, h0, first, R)
            ce = _load_ext(c_ref, r0, h0, first, R)
            ca = ce * ue
            ca0 = ca[HIST:]
            ca1 = _shift_down(ca, 1, R)
            ca2 = _shift_down(ca, 2, R)
            conv = w2 * ca0 + w1 * ca1 + w0 * ca2
            g = g_ref[pl.ds(r0, R), :].astype(F32)
            b = b_ref[pl.ds(r0, R), :].astype(F32)
            dy = dy_ref[pl.ds(r0, R), :].astype(F32)
            sg = jax.nn.sigmoid(g)
            sl = g * sg
            t = dy * conv
            db_ref[pl.ds(r0, R), :] = (t * sl).astype(BF16)
            dg_ref[pl.ds(r0, R), :] = (t * b * (sg * (1.0 + g * (1.0 - sg)))).astype(BF16)
            dconv = dy * b * sl
            a2 = a2 + _colsum8(dconv * ca0)
            a1 = a1 + _colsum8(dconv * ca1)
            a0 = a0 + _colsum8(dconv * ca2)
            e = jnp.concatenate([dconv, head], axis=0)
            dca = w2 * dconv + w1 * _shift_up(e, 1, R) + w0 * _shift_up(e, 2, R)
            du_ref[pl.ds(r0, R), :] = (dca * ce[HIST:]).astype(BF16)
            dc_ref[pl.ds(r0, R), :] = (dca * ue[HIST:]).astype(BF16)
            return dconv[0:SUBLANES], a0, a1, a2

        z = jnp.zeros((SUBLANES, LANES), F32)
        _, a0, a1, a2 = lax.fori_loop(0, nchunk, chunk, (z, z, z, z))
        dw_ref[pl.ds(0, 1), :] = jnp.sum(a0, axis=0, keepdims=True)
        dw_ref[pl.ds(1, 1), :] = jnp.sum(a1, axis=0, keepdims=True)
        dw_ref[pl.ds(2, 1), :] = jnp.sum(a2, axis=0, keepdims=True)

    def col(off):
        return pl.BlockSpec((T, LANES), lambda j: (0, j + off))

    sec = jax.ShapeDtypeStruct((T, nblk * LANES), BF16)
    return pl.pallas_call(
        body, name="conv_bwd", grid=(nblk,),
        in_specs=[col(0), col(4), col(8), col(12), col(0), pl.BlockSpec((None, 3, LANES), lambda j: (j, 0, 0))],
        out_specs=[col(0), col(0), col(0), col(0), pl.BlockSpec((None, 3, LANES), lambda j: (j, 0, 0))],
        out_shape=[sec, sec, sec, sec, jax.ShapeDtypeStruct((nblk, 3, LANES), F32)],
        compiler_params=_params(),
    )(proj, proj, proj, proj, dya, wconv)


def pool_bwd(proj, dyp, wpool, pscale):
    T = proj.shape[0]
    R = R_POOL
    ngrp = len(POOL_WINDOWS)
    nchunk = T // R

    def body(p_ref, g_ref, dy_ref, w_ref, s_ref, du_ref, dg_ref, dw_ref, ds_ref,
             pooled_s, mixed_s, dmix_s, dpool_s):
        grp = pl.program_id(0)

        def group(w):
            wb = w_ref[...].astype(BF16)
            _pooled_loop(p_ref, pooled_s, w, T)
            mixed_s[...] = jnp.dot(pooled_s[...], wb, preferred_element_type=F32)
            sc = s_ref[...]

            def gate_chunk(i, acc):
                r0 = pl.multiple_of(i * R, R)
                g = g_ref[pl.ds(r0, R), :].astype(F32)
                dy = dy_ref[pl.ds(r0, R), :].astype(F32)
                mixed = mixed_s[pl.ds(r0, R), :]
                sg = jax.nn.sigmoid(g)
                dg_ref[pl.ds(r0, R), :] = (dy * mixed * sc * (sg * (1.0 + g * (1.0 - sg)))).astype(BF16)
                dms = dy * (g * sg)
                dmix_s[pl.ds(r0, R), :] = (dms * sc).astype(BF16)
                return acc + _colsum8(dms * mixed)

            acc = lax.fori_loop(0, nchunk, gate_chunk, jnp.zeros((SUBLANES, LANES), F32))
            ds_ref[...] = jnp.sum(acc, axis=0, keepdims=True)
            dpool_s[pl.ds(0, T), :] = lax.dot_general(dmix_s[...], wb, NT, preferred_element_type=F32)
            dpool_s[pl.ds(T, HIST), :] = jnp.zeros((HIST, LANES), F32)
            dw_ref[...] = lax.dot_general(pooled_s[...], dmix_s[...], TN, preferred_element_type=F32).astype(BF16)

            def back_chunk(i, carry):
                r0 = pl.multiple_of(i * R, R)
                dpe = dpool_s[pl.ds(r0, R + HIST), :]
                e = dpe / _count(r0, R + HIST, w)
                du_ref[pl.ds(r0, R), :] = (_anticausal_window_sum(e, w)[0:R] - dpe[0:R]).astype(BF16)
                return carry

            lax.fori_loop(0, nchunk, back_chunk, 0)

        for k, w in enumerate(POOL_WINDOWS):
            pl.when(grp == k)(functools.partial(group, w))

    def col(off):
        return pl.BlockSpec((T, LANES), lambda j: (0, j + off))

    sec = jax.ShapeDtypeStruct((T, ngrp * LANES), BF16)
    wspec = pl.BlockSpec((None, LANES, LANES), lambda j: (j, 0, 0))
    sspec = pl.BlockSpec((1, LANES), lambda j: (0, j))
    return pl.pallas_call(
        body, name="pool_bwd", grid=(ngrp,),
        in_specs=[col(16), col(20), col(0), wspec, sspec],
        out_specs=[col(0), col(0), wspec, sspec],
        out_shape=[sec, sec, jax.ShapeDtypeStruct((ngrp, LANES, LANES), BF16),
                   jax.ShapeDtypeStruct((1, ngrp * LANES), F32)],
        scratch_shapes=[pltpu.VMEM((T, LANES), BF16), pltpu.VMEM((T, LANES), F32),
                        pltpu.VMEM((T, LANES), BF16), pltpu.VMEM((T + HIST, LANES), F32)],
        compiler_params=_params(),
    )(proj, proj, dyp, wpool, pscale)


def in_bwd(dsecs, wg, x, dxo, shift, scale, g_pre):
    T, D = x.shape
    NB, _, CW = wg.shape
    SW = dsecs[0].shape[1]
    nsec = len(dsecs)
    tm = 256
    nt = T // tm

    def body(*refs):
        d_refs = refs[0:nsec]
        w_ref, x_ref, dxo_ref, sh_ref, sc_ref, g_ref = refs[nsec:nsec + 6]
        dxi_ref, dw_ref, dsh_ref, dsc_ref, dg_ref = refs[nsec + 6:nsec + 11]
        dp_s, acc_w, acc_sh, acc_sc, acc_g = refs[nsec + 11:]
        i = pl.program_id(0)

        @pl.when(i == 0)
        def _():
            acc_w[...] = jnp.zeros_like(acc_w)
            acc_sh[...] = jnp.zeros_like(acc_sh)
            acc_sc[...] = jnp.zeros_like(acc_sc)
            acc_g[...] = jnp.zeros_like(acc_g)

        for s in range(nsec):
            dp_s[:, s * SW:(s + 1) * SW] = d_refs[s][...]
        xv = x_ref[...]
        g = g_ref[...]
        r = _rms(xv)
        xh = xv * r
        n = xh * g
        sc1 = 1.0 + sc_ref[...]
        hb = (n * sc1 + sh_ref[...]).astype(BF16)
        dh = lax.dot_general(dp_s[:, 0:CW], w_ref[0], NT, preferred_element_type=F32)
        for j in range(1, NB):
            dh = dh + lax.dot_general(dp_s[:, j * CW:(j + 1) * CW], w_ref[j], NT, preferred_element_type=F32)
        for j in range(NB):
            acc_w[j] += lax.dot_general(hb, dp_s[:, j * CW:(j + 1) * CW], TN, preferred_element_type=F32)
        acc_sh[...] += _colsum8(dh)
        acc_sc[...] += _colsum8(dh * n)
        dnp = dh * sc1
        acc_g[...] += _colsum8(dnp * xh)
        a = dnp * g
        dxi_ref[...] = dxo_ref[...] + r * (a - xh * jnp.mean(a * xh, axis=-1, keepdims=True))

        @pl.when(i == nt - 1)
        def _():
            dw_ref[...] = acc_w[...].astype(BF16)
            dsh_ref[...] = jnp.sum(acc_sh[...], axis=0, keepdims=True)
            dsc_ref[...] = jnp.sum(acc_sc[...], axis=0, keepdims=True)
            dg_ref[...] = jnp.sum(acc_g[...], axis=0, keepdims=True)

    row = pl.BlockSpec((1, D), lambda i: (0, 0))
    tile = pl.BlockSpec((tm, D), lambda i: (i, 0))
    sect = pl.BlockSpec((tm, SW), lambda i: (i, 0))
    wspec = pl.BlockSpec((NB, D, CW), lambda i: (0, 0, 0))
    rowshape = jax.ShapeDtypeStruct((1, D), F32)
    return pl.pallas_call(
        body, name="in_bwd", grid=(nt,),
        in_specs=[sect] * nsec + [wspec, tile, tile, row, row, row],
        out_specs=[tile, wspec, row, row, row],
        out_shape=[jax.ShapeDtypeStruct((T, D), F32), jax.ShapeDtypeStruct((NB, D, CW), BF16),
                   rowshape, rowshape, rowshape],
        scratch_shapes=[pltpu.VMEM((tm, nsec * SW), BF16), pltpu.VMEM((NB, D, CW), F32),
                        pltpu.VMEM((SUBLANES, D), F32), pltpu.VMEM((SUBLANES, D), F32), pltpu.VMEM((SUBLANES, D), F32)],
        compiler_params=_params(VMEM_BIG),
    )(*dsecs, wg, x, dxo, shift, scale, g_pre)


def _rcopy(src, dst, ssem, rsem, dev):
    return pltpu.make_async_remote_copy(src_ref=src, dst_ref=dst, send_sem=ssem, recv_sem=rsem,
                                        device_id=dev, device_id_type=MESH)


def _peers7(x, y, c):
    out = []
    for m in range(1, N_DEV):
        bx, by, bc = (m >> 2) & 1, (m >> 1) & 1, m & 1
        out.append(((1 - x) if bx else x, (1 - y) if by else y, (1 - c) if bc else c))
    return out


def gather_weights(wi, wo, c8, wc):
    L, D, CW = wi.shape
    RO = wo.shape[1]
    hD, hR = D // 2, RO // 2
    n_w = 3 * 2 * L
    n_rem = n_w + 7 + 3

    def body(wi_ref, wo_ref, c_ref, wc_ref, *rest):
        gis = rest[0:L]
        gos = rest[L:2 * L]
        call, wcall = rest[2 * L], rest[2 * L + 1]
        ssem, rsem, fss, frs, lsem = rest[2 * L + 2:]
        x, y, c = _me()
        myc = 2 * x + y
        me_lin = 4 * x + 2 * y + c
        me = (x, y, c)
        sib = (x, y, 1 - c)
        chips = [(1 - x, y), (x, 1 - y), (1 - x, 1 - y)]
        hi, ho = pl.ds(c * hD, hD), pl.ds(c * hR, hR)
        ohi, oho = pl.ds((1 - c) * hD, hD), pl.ds((1 - c) * hR, hR)

        local = []
        for l in range(L):
            local.append(pltpu.make_async_copy(wi_ref.at[l], gis[l].at[myc], lsem.at[2 * l]))
            local.append(pltpu.make_async_copy(wo_ref.at[l], gos[l].at[myc], lsem.at[2 * l + 1]))
        local.append(pltpu.make_async_copy(c_ref, call.at[me_lin], lsem.at[2 * L]))
        local.append(pltpu.make_async_copy(wc_ref, wcall.at[myc], lsem.at[2 * L + 1]))
        for cp in local:
            cp.start()

        sends, recvs, fwds, frecvs = [], [], [], []
        k = 0
        for (px, py) in chips:
            pc = 2 * px + py
            for l in range(L):
                for src, dst_arr, rows, orows in ((wi_ref, gis[l], hi, ohi), (wo_ref, gos[l], ho, oho)):
                    sends.append(_rcopy(src.at[l, rows], dst_arr.at[myc, rows], ssem.at[k], rsem.at[k], (px, py, c)))
                    landed = dst_arr.at[pc, rows]
                    recvs.append(_rcopy(landed, landed, ssem.at[k], rsem.at[k], me))
                    fwds.append(_rcopy(landed, landed, fss.at[k], frs.at[k], sib))
                    other = dst_arr.at[pc, orows]
                    frecvs.append(_rcopy(other, other, fss.at[k], frs.at[k], me))
                    k += 1
        for m, peer in enumerate(_peers7(x, y, c)):
            plin = 4 * peer[0] + 2 * peer[1] + peer[2]
            sends.append(_rcopy(c_ref, call.at[me_lin], ssem.at[n_w + m], rsem.at[n_w + m], peer))
            recvs.append(_rcopy(call.at[plin], call.at[plin], ssem.at[n_w + m], rsem.at[n_w + m], me))
        for j, (px, py) in enumerate(chips):
            pc = 2 * px + py
            sends.append(_rcopy(wc_ref, wcall.at[myc], ssem.at[n_w + 7 + j], rsem.at[n_w + 7 + j], (px, py, c)))
            recvs.append(_rcopy(wcall.at[pc], wcall.at[pc], ssem.at[n_w + 7 + j], rsem.at[n_w + 7 + j], me))

        for cp in sends:
            cp.start()
        for k in range(n_w):
            recvs[k].wait_recv()
            fwds[k].start()
        for k in range(n_w, n_rem):
            recvs[k].wait_recv()
        for cp in frecvs:
            cp.wait_recv()
        for cp in sends + fwds:
            cp.wait_send()
        for cp in local:
            cp.wait()

    out_shape = ([jax.ShapeDtypeStruct((N_CHIPS, D, CW), BF16)] * L + [jax.ShapeDtypeStruct((N_CHIPS, RO, D), BF16)] * L
                 + [jax.ShapeDtypeStruct((N_DEV, SUBLANES, LANES), F32), jax.ShapeDtypeStruct((N_CHIPS, L, 3, LANES), F32)])
    outs = pl.pallas_call(
        body, name="gather_weights",
        in_specs=[ANY] * 4, out_specs=[ANY] * (2 * L + 2), out_shape=out_shape,
        scratch_shapes=[pltpu.SemaphoreType.DMA((n_rem,)), pltpu.SemaphoreType.DMA((n_rem,)),
                        pltpu.SemaphoreType.DMA((n_w,)), pltpu.SemaphoreType.DMA((n_w,)),
                        pltpu.SemaphoreType.DMA((2 * L + 2,))],
        compiler_params=_params(n_grid=0),
    )(wi, wo, c8, wc)
    return outs[0:L], outs[L:2 * L], outs[2 * L], outs[2 * L + 1]


def exchange_mod(part):
    def body(p_ref, o_ref, ssem, rsem, lsem):
        x, y, c = _me()
        myc = 2 * x + y
        chips = [(1 - x, y), (x, 1 - y), (1 - x, 1 - y)]
        loc = pltpu.make_async_copy(p_ref, o_ref.at[myc], lsem)
        loc.start()
        sends = [_rcopy(p_ref, o_ref.at[myc], ssem.at[j], rsem.at[j], (px, py, c)) for j, (px, py) in enumerate(chips)]
        for cp in sends:
            cp.start()
        for j, (px, py) in enumerate(chips):
            slot = o_ref.at[2 * px + py]
            _rcopy(slot, slot, ssem.at[j], rsem.at[j], (x, y, c)).wait_recv()
        for cp in sends:
            cp.wait_send()
        loc.wait()

    return pl.pallas_call(
        body, name="exchange_mod", in_specs=[ANY], out_specs=ANY,
        out_shape=jax.ShapeDtypeStruct((N_CHIPS,) + part.shape, part.dtype),
        scratch_shapes=[pltpu.SemaphoreType.DMA((3,)), pltpu.SemaphoreType.DMA((3,)), pltpu.SemaphoreType.DMA],
        compiler_params=_params(n_grid=0),
    )(part)


def sibling_exchange(dwi, dwo, dwp, pack):
    L = len(dwi)
    groups = (dwi, dwo, dwp)
    n_big = 3 * L

    def body(*refs):
        ins = [refs[g * L:(g + 1) * L] for g in range(3)]
        pack_ref = refs[3 * L]
        outs = [refs[3 * L + 1 + g * L:3 * L + 1 + (g + 1) * L] for g in range(3)]
        packs = refs[6 * L + 1]
        ssem, rsem, lsem = refs[6 * L + 2:]
        x, y, c = _me()
        me = (x, y, c)
        me_lin = 4 * x + 2 * y + c
        sib = (x, y, 1 - c)
        loc = pltpu.make_async_copy(pack_ref, packs.at[me_lin], lsem)
        loc.start()
        sends, recvs = [], []
        k = 0
        for g in range(3):
            for l in range(L):
                half = ins[g][l].shape[1] // 2
                sends.append(_rcopy(ins[g][l].at[:, pl.ds((1 - c) * half, half)], outs[g][l],
                                    ssem.at[k], rsem.at[k], sib))
                recvs.append(_rcopy(outs[g][l], outs[g][l], ssem.at[k], rsem.at[k], me))
                k += 1
        for m, peer in enumerate(_peers7(x, y, c)):
            plin = 4 * peer[0] + 2 * peer[1] + peer[2]
            sends.append(_rcopy(pack_ref, packs.at[me_lin], ssem.at[n_big + m], rsem.at[n_big + m], peer))
            recvs.append(_rcopy(packs.at[plin], packs.at[plin], ssem.at[n_big + m], rsem.at[n_big + m], me))
        for cp in sends:
            cp.start()
        for cp in recvs:
            cp.wait_recv()
        for cp in sends:
            cp.wait_send()
        loc.wait()

    def halves(arrs):
        return [jax.ShapeDtypeStruct((a.shape[0], a.shape[1] // 2) + a.shape[2:], a.dtype) for a in arrs]

    out_shape = halves(dwi) + halves(dwo) + halves(dwp) + [jax.ShapeDtypeStruct((N_DEV,) + pack.shape, pack.dtype)]
    outs = pl.pallas_call(
        body, name="sibling_exchange",
        in_specs=[ANY] * (3 * L + 1), out_specs=[ANY] * (3 * L + 1), out_shape=out_shape,
        scratch_shapes=[pltpu.SemaphoreType.DMA((n_big + 7,)), pltpu.SemaphoreType.DMA((n_big + 7,)),
                        pltpu.SemaphoreType.DMA],
        compiler_params=_params(n_grid=0),
    )(*dwi, *dwo, *dwp, pack)
    return outs[0:L], outs[L:2 * L], outs[2 * L:3 * L], outs[3 * L]


def chip_exchange(chi, cho, chp):
    L = len(chi)
    n = 3 * L * 3

    def body(*refs):
        ins = refs[0:3 * L]
        outs = refs[3 * L:6 * L]
        ssem, rsem, lsem = refs[6 * L:]
        x, y, c = _me()
        myc = 2 * x + y
        chips = [(1 - x, y), (x, 1 - y), (1 - x, 1 - y)]
        local = [pltpu.make_async_copy(ins[a].at[myc], outs[a].at[myc], lsem.at[a]) for a in range(3 * L)]
        for cp in local:
            cp.start()
        sends, recvs = [], []
        k = 0
        for (px, py) in chips:
            pc = 2 * px + py
            for a in range(3 * L):
                sends.append(_rcopy(ins[a].at[pc], outs[a].at[myc], ssem.at[k], rsem.at[k], (px, py, c)))
                recvs.append(_rcopy(outs[a].at[pc], outs[a].at[pc], ssem.at[k], rsem.at[k], (x, y, c)))
                k += 1
        for cp in sends:
            cp.start()
        for cp in recvs:
            cp.wait_recv()
        for cp in sends:
            cp.wait_send()
        for cp in local:
            cp.wait()

    arrs = list(chi) + list(cho) + list(chp)
    outs = pl.pallas_call(
        body, name="chip_exchange",
        in_specs=[ANY] * (3 * L), out_specs=[ANY] * (3 * L),
        out_shape=[jax.ShapeDtypeStruct(a.shape, a.dtype) for a in arrs],
        scratch_shapes=[pltpu.SemaphoreType.DMA((n,)), pltpu.SemaphoreType.DMA((n,)), pltpu.SemaphoreType.DMA((3 * L,))],
        compiler_params=_params(n_grid=0),
    )(*arrs)
    return outs[0:L], outs[L:2 * L], outs[2 * L:3 * L]


def spread_reduced(ri, ro, rp, L_shapes):
    L = len(ri)
    (D, CW), (RO, D2) = L_shapes
    n = 2 * L + 7 * L

    def body(*refs):
        ri_r, ro_r, rp_r = refs[0:L], refs[L:2 * L], refs[2 * L:3 * L]
        gi, go, gp = refs[3 * L:3 * L + 3]
        ssem, rsem, lsem = refs[3 * L + 3:]
        x, y, c = _me()
        me = (x, y, c)
        myc = 2 * x + y
        sib = (x, y, 1 - c)
        peers = _peers7(x, y, c)
        hi, ho, hp = pl.ds(c * (D // 2), D // 2), pl.ds(c * (RO // 2), RO // 2), pl.ds(c * 64, 64)
        ohi, oho = pl.ds((1 - c) * (D // 2), D // 2), pl.ds((1 - c) * (RO // 2), RO // 2)
        local, sends, recvs = [], [], []
        k = 0
        for l in range(L):
            local.append(pltpu.make_async_copy(ri_r[l], gi.at[l, hi], lsem.at[3 * l]))
            local.append(pltpu.make_async_copy(ro_r[l], go.at[l, ho], lsem.at[3 * l + 1]))
            local.append(pltpu.make_async_copy(rp_r[l], gp.at[l, myc, hp], lsem.at[3 * l + 2]))
            sends.append(_rcopy(ri_r[l], gi.at[l, hi], ssem.at[k], rsem.at[k], sib))
            recvs.append(_rcopy(gi.at[l, ohi], gi.at[l, ohi], ssem.at[k], rsem.at[k], me))
            k += 1
            sends.append(_rcopy(ro_r[l], go.at[l, ho], ssem.at[k], rsem.at[k], sib))
            recvs.append(_rcopy(go.at[l, oho], go.at[l, oho], ssem.at[k], rsem.at[k], me))
            k += 1
            for peer in peers:
                pchip = 2 * peer[0] + peer[1]
                prow = pl.ds(peer[2] * 64, 64)
                sends.append(_rcopy(rp_r[l], gp.at[l, myc, hp], ssem.at[k], rsem.at[k], peer))
                recvs.append(_rcopy(gp.at[l, pchip, prow], gp.at[l, pchip, prow], ssem.at[k], rsem.at[k], me))
                k += 1
        for cp in local + sends:
            cp.start()
        for cp in recvs:
            cp.wait_recv()
        for cp in sends:
            cp.wait_send()
        for cp in local:
            cp.wait()

    return pl.pallas_call(
        body, name="spread_reduced",
        in_specs=[ANY] * (3 * L), out_specs=[ANY] * 3,
        out_shape=[jax.ShapeDtypeStruct((L, D, CW), F32), jax.ShapeDtypeStruct((L, RO, D2), F32),
                   jax.ShapeDtypeStruct((L, N_CHIPS, LANES, LANES), F32)],
        scratch_shapes=[pltpu.SemaphoreType.DMA((n,)), pltpu.SemaphoreType.DMA((n,)), pltpu.SemaphoreType.DMA((3 * L,))],
        compiler_params=_params(n_grid=0),
    )(*ri, *ro, *rp)


def add_sibling(cidx, mine, sib):
    def body(c_ref, *refs):
        for a in range(3):
            m, s, o = refs[a], refs[3 + a], refs[6 + a]
            o[...] = (m[...].astype(F32) + s[...].astype(F32)).astype(BF16)

    def mine_spec(a):
        h = a.shape[1] // 2
        return pl.BlockSpec((None, h, a.shape[2]), lambda j, c_ref: (j, c_ref[0], 0))

    def sib_spec(a):
        return pl.BlockSpec((None,) + a.shape[1:], lambda j, c_ref: (j, 0, 0))

    return pl.pallas_call(
        body, name="add_sibling",
        grid_spec=pltpu.PrefetchScalarGridSpec(
            num_scalar_prefetch=1, grid=(N_CHIPS,),
            in_specs=[mine_spec(a) for a in mine] + [sib_spec(a) for a in sib],
            out_specs=[sib_spec(a) for a in sib]),
        out_shape=[jax.ShapeDtypeStruct(a.shape, BF16) for a in sib],
        compiler_params=_params(VMEM_BIG),
    )(cidx, *mine, *sib)


def sum_chips(bufs):
    nq = 4

    def body(*refs):
        for a in range(3):
            b, o = refs[a], refs[3 + a]
            s = b[0].astype(F32)
            for j in range(1, N_CHIPS):
                s = s + b[j].astype(F32)
            o[...] = s

    def in_spec(a):
        return pl.BlockSpec((N_CHIPS, a.shape[1] // nq, a.shape[2]), lambda q: (0, q, 0))

    def out_spec(a):
        return pl.BlockSpec((a.shape[1] // nq, a.shape[2]), lambda q: (q, 0))

    return pl.pallas_call(
        body, name="sum_chips", grid=(nq,),
        in_specs=[in_spec(a) for a in bufs], out_specs=[out_spec(a) for a in bufs],
        out_shape=[jax.ShapeDtypeStruct(a.shape[1:], F32) for a in bufs],
        compiler_params=_params(VMEM_BIG),
    )(*bufs)


def sum_packs(packs):
    def body(p_ref, o_ref):
        s = p_ref[0]
        for d in range(1, N_DEV):
            s = s + p_ref[d]
        o_ref[...] = s

    return pl.pallas_call(
        body, name="sum_packs",
        out_shape=jax.ShapeDtypeStruct(packs.shape[1:], F32),
        compiler_params=_params(n_grid=0),
    )(packs)


def _adamw_math(w, g, m, v):
    m = ADAM_B1 * m + (1.0 - ADAM_B1) * g
    v = ADAM_B2 * v + (1.0 - ADAM_B2) * (g * g)
    m_hat = m / (1.0 - ADAM_B1 ** ADAM_STEP)
    v_hat = v / (1.0 - ADAM_B2 ** ADAM_STEP)
    delta = -ADAM_LR * (m_hat / (jnp.sqrt(v_hat) + ADAM_EPS) + ADAM_WD * w)
    return delta, m, v


def adamw(w, g, m, v, block, name):
    grid = tuple(s // b for s, b in zip(w.shape, block))

    def body(w_ref, g_ref, m_ref, v_ref, d_ref, mo_ref, vo_ref):
        d, mm, vv = _adamw_math(w_ref[...], g_ref[...], m_ref[...], v_ref[...])
        d_ref[...] = d
        mo_ref[...] = mm
        vo_ref[...] = vv

    spec = pl.BlockSpec(block, lambda *idx: idx)
    shape = jax.ShapeDtypeStruct(w.shape, F32)
    return pl.pallas_call(
        body, name=name, grid=grid,
        in_specs=[spec] * 4, out_specs=[spec] * 3, out_shape=[shape] * 3,
        compiler_params=_params(VMEM_BIG, n_grid=len(grid)),
    )(w, g, m, v)


def ada_finish(c_all, dmod, w, m, v):
    L, D, CW = w.shape
    hD = D // 2

    def body(c_ref, d_ref, w_ref, m_ref, v_ref, g_ref, dl_ref, mo_ref, vo_ref):
        cv = c_ref[...]
        z = jnp.zeros_like(cv)
        ca = jnp.concatenate([cv * jax.nn.sigmoid(cv), z], axis=0).astype(BF16)
        dm = jnp.concatenate([d_ref[0], jnp.zeros_like(d_ref[0])], axis=0).astype(BF16)
        g = lax.dot_general(ca, dm, TN, preferred_element_type=F32)
        g_ref[0] = g
        d, mm, vv = _adamw_math(w_ref[0], g, m_ref[0], v_ref[0])
        dl_ref[0] = d
        mo_ref[0] = mm
        vo_ref[0] = vv

    big = pl.BlockSpec((1, hD, CW), lambda l, h: (l, h, 0))
    shape = jax.ShapeDtypeStruct(w.shape, F32)
    return pl.pallas_call(
        body, name="ada_finish", grid=(L, 2),
        in_specs=[pl.BlockSpec((N_DEV, hD), lambda l, h: (0, h)), pl.BlockSpec((1, N_DEV, CW), lambda l, h: (l, 0, 0)),
                  big, big, big],
        out_specs=[big] * 4, out_shape=[shape] * 4,
        compiler_params=_params(VMEM_BIG, n_grid=2),
    )(c_all, dmod, w, m, v)


def _pad_rows(a, rows):
    return jnp.concatenate([a, jnp.zeros((rows - a.shape[0], LANES), F32)], axis=0)


def kernel(x, c, w_ada, b_ada, g_pre, w_in, w_conv, w_pool, pool_scale, w_out, g_post, loss_target, m_w_ada, m_b_ada, m_g_pre, m_w_in, m_w_conv, m_w_pool, m_pool_scale, m_w_out, m_g_post, v_w_ada, v_b_ada, v_g_pre, v_w_in, v_w_conv, v_w_pool, v_pool_scale, v_w_out, v_g_post):
    L, D, CW = w_in.shape
    RO = w_out.shape[1]
    T = x.shape[1]
    ix, iy, ic = _me()
    chip = 2 * ix + iy
    me_lin = 4 * ix + 2 * iy + ic

    wi_bf, wo_bf = cast_weights(w_in, w_out)
    wg_in, wg_out, c_all3, wconv_all = gather_weights(wi_bf, wo_bf, c.reshape(SUBLANES, LANES), w_conv)
    c_all = c_all3.reshape(N_DEV, D)
    b_my = lax.dynamic_slice_in_dim(b_ada, chip * CW, CW, axis=1)
    mod_all = exchange_mod(mod_part(c_all, w_ada, b_my))
    mod = lax.dynamic_index_in_dim(mod_all, me_lin, axis=2, keepdims=False)
    mod = jnp.transpose(mod, (1, 0, 2)).reshape(L, 3 * D)
    wg_out = [w.reshape(N_CHIPS * RO, D) for w in wg_out]

    xs, projs, yas, yps, ys = [x.reshape(T, D)], [], [], [], []
    for l in range(L):
        shift, scale = mod[l:l + 1, 0:D], mod[l:l + 1, D:2 * D]
        gate = mod[l:l + 1, 2 * D:3 * D]
        proj = proj_fwd(xs[l], shift, scale, g_pre[l:l + 1], wg_in[l])
        ya = conv_fwd(proj, wconv_all[:, l])
        yp = pool_fwd(proj, w_pool[l], pool_scale[l:l + 1])
        xn, yv = out_fwd(ya, yp, wg_out[l], xs[l], gate, g_post[l:l + 1])
        xs.append(xn)
        projs.append(proj)
        yas.append(ya)
        yps.append(yp)
        ys.append(yv)

    dx, loss_blk = loss_head(xs[L], loss_target.reshape(T, D))
    loss = lax.psum(loss_blk[0, 0], ("x", "y", "c"))

    dwi, dwo, dwp, pack_rows = [None] * L, [None] * L, [None] * L, [None] * L
    for l in reversed(range(L)):
        shift, scale = mod[l:l + 1, 0:D], mod[l:l + 1, D:2 * D]
        gate = mod[l:l + 1, 2 * D:3 * D]
        dya, dyp, dwo_l, dgate, dgpost = out_bwd(dx, ys[l], yas[l], yps[l], wg_out[l], gate, g_post[l:l + 1])
        du_a, db_a, dc_a, dg_a, dwc = conv_bwd(projs[l], dya, wconv_all[:, l])
        du_p, dg_p, dwp_l, dps = pool_bwd(projs[l], dyp, w_pool[l], pool_scale[l:l + 1])
        dx, dwi_l, dshift, dscale, dgpre = in_bwd([du_a, db_a, dc_a, dg_a, du_p, dg_p], wg_in[l], xs[l], dx,
                                                  shift, scale, g_pre[l:l + 1])
        dwi[l], dwo[l], dwp[l] = dwi_l, dwo_l.reshape(N_CHIPS, RO, D), dwp_l
        pack_rows[l] = jnp.concatenate(
            [dgpre.reshape(8, LANES), dgpost.reshape(8, LANES), dshift.reshape(8, LANES), dscale.reshape(8, LANES),
             dgate.reshape(8, LANES), _pad_rows(dps.reshape(4, LANES), 8), _pad_rows(dwc.reshape(12, LANES), 16)], axis=0)
    grad_x = dx.reshape(1, T, D)
    pack = jnp.concatenate(pack_rows, axis=0)

    cidx = jnp.reshape(ic, (1,)).astype(jnp.int32)
    sb_i, sb_o, sb_p, packs_all = sibling_exchange(dwi, dwo, dwp, pack)
    ch_i, ch_o, ch_p = [], [], []
    for l in range(L):
        a, b, p = add_sibling(cidx, (dwi[l], dwo[l], dwp[l]), (sb_i[l], sb_o[l], sb_p[l]))
        ch_i.append(a)
        ch_o.append(b)
        ch_p.append(p)
    rb_i, rb_o, rb_p = chip_exchange(ch_i, ch_o, ch_p)
    rd_i, rd_o, rd_p = [], [], []
    for l in range(L):
        a, b, p = sum_chips((rb_i[l], rb_o[l], rb_p[l]))
        rd_i.append(a)
        rd_o.append(b)
        rd_p.append(p)
    g_w_in, g_w_out, g_w_pool = spread_reduced(rd_i, rd_o, rd_p, ((D, CW), (RO, D)))

    small = sum_packs(packs_all).reshape(L, PACK_ROWS, LANES)
    g_g_pre = small[:, 0:8].reshape(L, D)
    g_g_post = small[:, 8:16].reshape(L, D)
    g_b_ada = small[:, 16:40].reshape(L, 3 * D)
    g_pscale = small[:, 40:44].reshape(L, 4 * LANES)
    g_w_conv = lax.dynamic_index_in_dim(small[:, 48:60].reshape(L, N_CHIPS, 3, LANES), chip, axis=1, keepdims=False)
    dmod_all = packs_all.reshape(N_DEV, L, PACK_ROWS, LANES)[:, :, 16:40]
    dmod_my = lax.dynamic_slice_in_dim(dmod_all, chip * (CW // LANES), CW // LANES, axis=2)
    dmod_my = jnp.transpose(dmod_my, (1, 0, 2, 3)).reshape(L, N_DEV, CW)

    g_w_ada, d_w_ada, nm_w_ada, nv_w_ada = ada_finish(c_all, dmod_my, w_ada, m_w_ada, v_w_ada)
    d_w_in, nm_w_in, nv_w_in = adamw(w_in, g_w_in, m_w_in, v_w_in, (1, D // 2, CW), "adamw_w_in")
    d_w_out, nm_w_out, nv_w_out = adamw(w_out, g_w_out, m_w_out, v_w_out, (1, RO, D), "adamw_w_out")
    pshape = (L, N_CHIPS * LANES, LANES)
    d_w_pool, nm_w_pool, nv_w_pool = adamw(w_pool.reshape(pshape), g_w_pool.reshape(pshape), m_w_pool.reshape(pshape),
                                           v_w_pool.reshape(pshape), (1,) + pshape[1:], "adamw_w_pool")
    d_w_pool, nm_w_pool, nv_w_pool = [a.reshape(w_pool.shape) for a in (d_w_pool, nm_w_pool, nv_w_pool)]

    def small_adamw(w, g, m, v, name):
        shp = (1,) + w.shape if w.ndim == 2 else w.shape
        outs = adamw(w.reshape(shp), g.reshape(shp), m.reshape(shp), v.reshape(shp), shp, name)
        return [a.reshape(w.shape) for a in outs]

    d_b_ada, nm_b_ada, nv_b_ada = small_adamw(b_ada, g_b_ada, m_b_ada, v_b_ada, "adamw_b_ada")
    d_g_pre, nm_g_pre, nv_g_pre = small_adamw(g_pre, g_g_pre, m_g_pre, v_g_pre, "adamw_g_pre")
    d_w_conv, nm_w_conv, nv_w_conv = small_adamw(w_conv, g_w_conv, m_w_conv, v_w_conv, "adamw_w_conv")
    d_pscale, nm_pscale, nv_pscale = small_adamw(pool_scale, g_pscale, m_pool_scale, v_pool_scale, "adamw_pool_scale")
    d_g_post, nm_g_post, nv_g_post = small_adamw(g_post, g_g_post, m_g_post, v_g_post, "adamw_g_post")

    return (loss, grad_x,
            g_w_ada, g_b_ada, g_g_pre, g_w_in, g_w_conv, g_w_pool, g_pscale, g_w_out, g_g_post,
            d_w_ada, d_b_ada, d_g_pre, d_w_in, d_w_conv, d_w_pool, d_pscale, d_w_out, d_g_post,
            nm_w_ada, nm_b_ada, nm_g_pre, nm_w_in, nm_w_conv, nm_w_pool, nm_pscale, nm_w_out, nm_g_post,
            nv_w_ada, nv_b_ada, nv_g_pre, nv_w_in, nv_w_conv, nv_w_pool, nv_pscale, nv_w_out, nv_g_post)
```

```python
import functools

import jax
import jax.numpy as jnp
from jax import lax
from jax.experimental import pallas as pl
from jax.experimental.pallas import tpu as pltpu

F32 = jnp.float32
BF16 = jnp.bfloat16
MESH = pl.DeviceIdType.MESH
ANY = pl.BlockSpec(memory_space=pl.ANY)

NORM_EPS = 1e-6
POOL_WINDOWS = (2, 4, 8, 16)
ADAM_LR = 0.001
ADAM_B1 = 0.9
ADAM_B2 = 0.999
ADAM_EPS = 1e-08
ADAM_WD = 0.01
ADAM_STEP = 10

N_CHIPS = 4
N_DEV = 8
LANES = 128
SUBLANES = 8
VMEM_BIG = 56 * 1024 * 1024
HIST = 16
R_CONV = 32
R_POOL = 64

NT = (((1,), (1,)), ((), ()))
TN = (((0,), (0,)), ((), ()))


def _params(vmem=None, n_grid=1):
    kw = {}
    if n_grid:
        kw["dimension_semantics"] = ("arbitrary",) * n_grid
    if vmem is not None:
        kw["vmem_limit_bytes"] = vmem
    return pltpu.CompilerParams(**kw)


def _colsum8(v):
    n, d = v.shape
    return v.reshape(n // SUBLANES, SUBLANES, d).sum(axis=0)


def _rms(v):
    return lax.rsqrt(jnp.mean(v * v, axis=-1, keepdims=True) + NORM_EPS)


def _shift_down(ext, k, rows):
    if k == 0:
        return ext[HIST:HIST + rows]
    return pltpu.roll(ext, k, 0)[HIST:HIST + rows]


def _shift_up(ext, k, rows):
    if k == 0:
        return ext[0:rows]
    return pltpu.roll(ext, ext.shape[0] - k, 0)[0:rows]


def _load_ext(ref, r0, h0, first, rows):
    hist = ref[pl.ds(h0, HIST), :].astype(F32)
    hist = jnp.where(first, 0.0, hist)
    cur = ref[pl.ds(r0, rows), :].astype(F32)
    return jnp.concatenate([hist, cur], axis=0)


def _me():
    return lax.axis_index("x"), lax.axis_index("y"), lax.axis_index("c")


def cast_weights(pos, w_in, w_out, l):
    _, D, CW = w_in.shape
    RO = w_out.shape[1]

    def body(pos_ref, wi, wo, oi, oo):
        oi[...] = wi[...].astype(BF16)
        oo[...] = wo[...].astype(BF16)

    return pl.pallas_call(
        body, name="cast_w",
        grid_spec=pltpu.PrefetchScalarGridSpec(
            num_scalar_prefetch=1, grid=(2,),
            in_specs=[pl.BlockSpec((None, D // 2, CW), lambda h, p: (l, h, 0)),
                      pl.BlockSpec((None, RO // 2, D), lambda h, p: (l, h, 0))],
            out_specs=[pl.BlockSpec((None, D // 2, CW), lambda h, p: (p[1], h, 0)),
                       pl.BlockSpec((None, RO // 2, D), lambda h, p: (p[1], h, 0))]),
        out_shape=[jax.ShapeDtypeStruct((N_CHIPS, D, CW), BF16), jax.ShapeDtypeStruct((N_CHIPS, RO, D), BF16)],
        compiler_params=_params(),
    )(pos, w_in, w_out)


def mod_part(c_all, w_ada, b_my):
    L, D, CW = w_ada.shape

    def body(c_ref, w_ref, b_ref, o_ref):
        cv = c_ref[...]
        ca = (cv * jax.nn.sigmoid(cv)).astype(BF16)
        o_ref[0] = jnp.dot(ca, w_ref[0].astype(BF16), preferred_element_type=F32) + b_ref[0]

    return pl.pallas_call(
        body, name="mod_part", grid=(L,),
        in_specs=[pl.BlockSpec((N_DEV, D), lambda l: (0, 0)),
                  pl.BlockSpec((1, D, CW), lambda l: (l, 0, 0)),
                  pl.BlockSpec((1, 1, CW), lambda l: (l, 0, 0))],
        out_specs=pl.BlockSpec((1, N_DEV, CW), lambda l: (l, 0, 0)),
        out_shape=jax.ShapeDtypeStruct((L, N_DEV, CW), F32),
        compiler_params=_params(VMEM_BIG),
    )(c_all, w_ada, b_my.reshape(L, 1, CW))


def _mod_row(l, k, D):
    return pl.BlockSpec((None, None, 1, D), lambda *_: (l, k, 0, 0))


def _layer_row(l, D):
    return pl.BlockSpec((None, 1, D), lambda *_: (l, 0, 0))


def proj_fwd(x, mod4, g_pre3, wg, l):
    T, D = x.shape
    NB, _, CW = wg.shape
    tm = 512

    def body(x_ref, sh_ref, sc_ref, g_ref, w_ref, o_ref):
        xv = x_ref[...]
        h = (xv * _rms(xv) * g_ref[...]) * (1.0 + sc_ref[...]) + sh_ref[...]
        hb = h.astype(BF16)
        for j in range(NB):
            o_ref[:, j * CW:(j + 1) * CW] = jnp.dot(hb, w_ref[j], preferred_element_type=F32).astype(BF16)

    return pl.pallas_call(
        body, name="proj_fwd", grid=(T // tm,),
        in_specs=[pl.BlockSpec((tm, D), lambda i: (i, 0)), _mod_row(l, 0, D), _mod_row(l, 1, D), _layer_row(l, D),
                  pl.BlockSpec((NB, D, CW), lambda i: (0, 0, 0))],
        out_specs=pl.BlockSpec((tm, NB * CW), lambda i: (i, 0)),
        out_shape=jax.ShapeDtypeStruct((T, NB * CW), BF16),
        compiler_params=_params(VMEM_BIG),
    )(x, mod4, mod4, g_pre3, wg)


def conv_fwd(proj, wconv, l):
    T = proj.shape[0]
    R = R_CONV
    nblk = 4

    def body(u_ref, b_ref, c_ref, g_ref, w_ref, o_ref):
        w0 = w_ref[pl.ds(0, 1), :]
        w1 = w_ref[pl.ds(1, 1), :]
        w2 = w_ref[pl.ds(2, 1), :]

        def chunk(i, carry):
            r0 = pl.multiple_of(i * R, R)
            h0 = pl.multiple_of(jnp.maximum(r0 - HIST, 0), HIST)
            first = i == 0
            ca = _load_ext(c_ref, r0, h0, first, R) * _load_ext(u_ref, r0, h0, first, R)
            conv = w2 * ca[HIST:] + w1 * _shift_down(ca, 1, R) + w0 * _shift_down(ca, 2, R)
            g = g_ref[pl.ds(r0, R), :].astype(F32)
            b = b_ref[pl.ds(r0, R), :].astype(F32)
            o_ref[pl.ds(r0, R), :] = (b * conv * (g * jax.nn.sigmoid(g))).astype(BF16)
            return carry

        lax.fori_loop(0, T // R, chunk, 0)

    def col(off):
        return pl.BlockSpec((T, LANES), lambda j: (0, j + off))

    return pl.pallas_call(
        body, name="conv_fwd", grid=(nblk,),
        in_specs=[col(0), col(4), col(8), col(12), pl.BlockSpec((None, None, 3, LANES), lambda j: (j, l, 0, 0))],
        out_specs=pl.BlockSpec((T, LANES), lambda j: (0, j)),
        out_shape=jax.ShapeDtypeStruct((T, nblk * LANES), BF16),
        compiler_params=_params(),
    )(proj, proj, proj, proj, wconv)


def _causal_window_sum(ext, w):
    s, k = ext, 1
    while k < w:
        s = s + pltpu.roll(s, k, 0)
        k *= 2
    return s


def _anticausal_window_sum(ext, w):
    s, k = ext, 1
    n = ext.shape[0]
    while k < w:
        s = s + pltpu.roll(s, n - k, 0)
        k *= 2
    return s


def _count(r0, rows, w):
    t = r0 + lax.broadcasted_iota(jnp.int32, (rows, LANES), 0)
    return jnp.minimum(t + 1, w).astype(F32)


def _pooled_loop(p_ref, pooled_s, w, T):
    R = R_POOL

    def chunk(i, carry):
        r0 = pl.multiple_of(i * R, R)
        h0 = pl.multiple_of(jnp.maximum(r0 - HIST, 0), HIST)
        ext = _load_ext(p_ref, r0, h0, i == 0, R)
        ws = _causal_window_sum(ext, w)[HIST:]
        pooled_s[pl.ds(r0, R), :] = (ws / _count(r0, R, w) - ext[HIST:]).astype(BF16)
        return carry

    lax.fori_loop(0, T // R, chunk, 0)


def _pool_w_spec(l):
    return pl.BlockSpec((None, None, LANES, LANES), lambda j: (l, j, 0, 0))


def _pool_s_spec(l):
    return pl.BlockSpec((None, 1, LANES), lambda j: (l, 0, j))


def pool_fwd(proj, wpool, pscale3, l):
    T = proj.shape[0]
    R = R_POOL
    ngrp = len(POOL_WINDOWS)

    def body(p_ref, g_ref, w_ref, s_ref, o_ref, pooled_s, mixed_s):
        grp = pl.program_id(0)

        def group(w):
            _pooled_loop(p_ref, pooled_s, w, T)
            mixed_s[...] = jnp.dot(pooled_s[...], w_ref[...].astype(BF16), preferred_element_type=F32)
            sc = s_ref[...]

            def chunk(i, carry):
                r0 = pl.multiple_of(i * R, R)
                g = g_ref[pl.ds(r0, R), :].astype(F32)
                o_ref[pl.ds(r0, R), :] = (mixed_s[pl.ds(r0, R), :] * sc * (g * jax.nn.sigmoid(g))).astype(BF16)
                return carry

            lax.fori_loop(0, T // R, chunk, 0)

        for k, w in enumerate(POOL_WINDOWS):
            pl.when(grp == k)(functools.partial(group, w))

    return pl.pallas_call(
        body, name="pool_fwd", grid=(ngrp,),
        in_specs=[pl.BlockSpec((T, LANES), lambda j: (0, j + 16)), pl.BlockSpec((T, LANES), lambda j: (0, j + 20)),
                  _pool_w_spec(l), _pool_s_spec(l)],
        out_specs=pl.BlockSpec((T, LANES), lambda j: (0, j)),
        out_shape=jax.ShapeDtypeStruct((T, ngrp * LANES), BF16),
        scratch_shapes=[pltpu.VMEM((T, LANES), BF16), pltpu.VMEM((T, LANES), F32)],
        compiler_params=_params(),
    )(proj, proj, wpool, pscale3)


def out_fwd(ya, yp, wo, x, mod4, g_post3, l):
    T, D = x.shape
    H = ya.shape[1]
    tm = 512

    def body(ya_ref, yp_ref, wo_ref, x_ref, gt_ref, g_ref, xn_ref, y_ref):
        y = (jnp.dot(ya_ref[...], wo_ref[0:H, :], preferred_element_type=F32)
             + jnp.dot(yp_ref[...], wo_ref[H:2 * H, :], preferred_element_type=F32))
        xn_ref[...] = x_ref[...] + gt_ref[...] * (y * _rms(y) * g_ref[...])
        y_ref[...] = y

    tile = pl.BlockSpec((tm, D), lambda i: (i, 0))
    half = pl.BlockSpec((tm, H), lambda i: (i, 0))
    return pl.pallas_call(
        body, name="out_fwd", grid=(T // tm,),
        in_specs=[half, half, pl.BlockSpec((2 * H, D), lambda i: (0, 0)), tile, _mod_row(l, 2, D), _layer_row(l, D)],
        out_specs=[tile, tile],
        out_shape=[jax.ShapeDtypeStruct((T, D), F32), jax.ShapeDtypeStruct((T, D), F32)],
        compiler_params=_params(VMEM_BIG),
    )(ya, yp, wo, x, mod4, g_post3)


def loss_head(xl, target):
    T, D = xl.shape
    tm = 512
    nt = T // tm

    def body(x_ref, t_ref, dx_ref, l_ref, acc):
        i = pl.program_id(0)

        @pl.when(i == 0)
        def _():
            acc[...] = jnp.zeros_like(acc)

        d = x_ref[...] - t_ref[...]
        dx_ref[...] = d * (1.0 / D)
        acc[...] += _colsum8(d * d)

        @pl.when(i == nt - 1)
        def _():
            l_ref[...] = jnp.zeros_like(l_ref) + jnp.sum(acc[...]) * (0.5 / D)

    tile = pl.BlockSpec((tm, D), lambda i: (i, 0))
    return pl.pallas_call(
        body, name="loss_head", grid=(nt,),
        in_specs=[tile, tile],
        out_specs=[tile, pl.BlockSpec((SUBLANES, LANES), lambda i: (0, 0))],
        out_shape=[jax.ShapeDtypeStruct((T, D), F32), jax.ShapeDtypeStruct((SUBLANES, LANES), F32)],
        scratch_shapes=[pltpu.VMEM((SUBLANES, D), F32)],
        compiler_params=_params(VMEM_BIG),
    )(xl, target)


def out_bwd(dx, y, ya, yp, wo, mod4, g_post3, l):
    T, D = dx.shape
    H = ya.shape[1]
    tm = 512
    nt = T // tm

    def body(dx_ref, y_ref, ya_ref, yp_ref, wo_ref, gt_ref, g_ref,
             dya_ref, dyp_ref, dwo_ref, dgt_ref, dg_ref, acc_w, acc_gt, acc_g):
        i = pl.program_id(0)

        @pl.when(i == 0)
        def _():
            acc_w[...] = jnp.zeros_like(acc_w)
            acc_gt[...] = jnp.zeros_like(acc_gt)
            acc_g[...] = jnp.zeros_like(acc_g)

        yv = y_ref[...]
        dxv = dx_ref[...]
        g = g_ref[...]
        r = _rms(yv)
        yn = yv * r
        acc_gt[...] += _colsum8(dxv * (yn * g))
        dn = dxv * gt_ref[...]
        acc_g[...] += _colsum8(dn * yn)
        a = dn * g
        dy = r * (a - yn * jnp.mean(a * yn, axis=-1, keepdims=True))
        dyb = dy.astype(BF16)
        dyc = lax.dot_general(dyb, wo_ref[...], NT, preferred_element_type=F32)
        dya_ref[...] = dyc[:, 0:H].astype(BF16)
        dyp_ref[...] = dyc[:, H:2 * H].astype(BF16)
        acc_w[0:H, :] += lax.dot_general(ya_ref[...], dyb, TN, preferred_element_type=F32)
        acc_w[H:2 * H, :] += lax.dot_general(yp_ref[...], dyb, TN, preferred_element_type=F32)

        @pl.when(i == nt - 1)
        def _():
            dwo_ref[...] = acc_w[...].astype(BF16)
            dgt_ref[...] = jnp.sum(acc_gt[...], axis=0, keepdims=True)
            dg_ref[...] = jnp.sum(acc_g[...], axis=0, keepdims=True)

    row = pl.BlockSpec((1, D), lambda i: (0, 0))
    tile = pl.BlockSpec((tm, D), lambda i: (i, 0))
    half = pl.BlockSpec((tm, H), lambda i: (i, 0))
    full = pl.BlockSpec((2 * H, D), lambda i: (0, 0))
    return pl.pallas_call(
        body, name="out_bwd", grid=(nt,),
        in_specs=[tile, tile, half, half, full, _mod_row(l, 2, D), _layer_row(l, D)],
        out_specs=[half, half, full, row, row],
        out_shape=[jax.ShapeDtypeStruct((T, H), BF16), jax.ShapeDtypeStruct((T, H), BF16),
                   jax.ShapeDtypeStruct((2 * H, D), BF16),
                   jax.ShapeDtypeStruct((1, D), F32), jax.ShapeDtypeStruct((1, D), F32)],
        scratch_shapes=[pltpu.VMEM((2 * H, D), F32), pltpu.VMEM((SUBLANES, D), F32), pltpu.VMEM((SUBLANES, D), F32)],
        compiler_params=_params(VMEM_BIG),
    )(dx, y, ya, yp, wo, mod4, g_post3)


def conv_bwd(proj, dya, wconv, l):
    T = proj.shape[0]
    R = R_CONV
    nblk = 4
    nchunk = T // R

    def body(u_ref, b_ref, c_ref, g_ref, dy_ref, w_ref, du_ref, db_ref, dc_ref, dg_ref, dw_ref):
        w0 = w_ref[pl.ds(0, 1), :]
        w1 = w_ref[pl.ds(1, 1), :]
        w2 = w_ref[pl.ds(2, 1), :]

        def chunk(k, carry):
            head, a0, a1, a2 = carry
            i = nchunk - 1 - k
            r0 = pl.multiple_of(i * R, R)
            h0 = pl.multiple_of(jnp.maximum(r0 - HIST, 0), HIST)
            first = i == 0
            ue = _load_ext(u_ref, r0, h0, first, R)
            ce = _load_ext(c_ref, r0, h0, first, R)
            ca = ce * ue
            ca0 = ca[HIST:]
            ca1 = _shift_down(ca, 1, R)
            ca2 = _shift_down(ca, 2, R)
            conv = w2 * ca0 + w1 * ca1 + w0 * ca2
            g = g_ref[pl.ds(r0, R), :].astype(F32)
            b = b_ref[pl.ds(r0, R), :].astype(F32)
            dy = dy_ref[pl.ds(r0, R), :].astype(F32)
            sg = jax.nn.sigmoid(g)
            sl = g * sg
            t = dy * conv
            db_ref[pl.ds(r0, R), :] = (t * sl).astype(BF16)
            dg_ref[pl.ds(r0, R), :] = (t * b * (sg * (1.0 + g * (1.0 - sg)))).astype(BF16)
            dconv = dy * b * sl
            a2 = a2 + _colsum8(dconv * ca0)
            a1 = a1 + _colsum8(dconv * ca1)
            a0 = a0 + _colsum8(dconv * ca2)
            e = jnp.concatenate([dconv, head], axis=0)
            dca = w2 * dconv + w1 * _shift_up(e, 1, R) + w0 * _shift_up(e, 2, R)
            du_ref[pl.ds(r0, R), :] = (dca * ce[HIST:]).astype(BF16)
            dc_ref[pl.ds(r0, R), :] = (dca * ue[HIST:]).astype(BF16)
            return dconv[0:SUBLANES], a0, a1, a2

        z = jnp.zeros((SUBLANES, LANES), F32)
        _, a0, a1, a2 = lax.fori_loop(0, nchunk, chunk, (z, z, z, z))
        dw_ref[pl.ds(0, 1), :] = jnp.sum(a0, axis=0, keepdims=True)
        dw_ref[pl.ds(1, 1), :] = jnp.sum(a1, axis=0, keepdims=True)
        dw_ref[pl.ds(2, 1), :] = jnp.sum(a2, axis=0, keepdims=True)

    def col(off):
        return pl.BlockSpec((T, LANES), lambda j: (0, j + off))

    sec = jax.ShapeDtypeStruct((T, nblk * LANES), BF16)
    return pl.pallas_call(
        body, name="conv_bwd", grid=(nblk,),
        in_specs=[col(0), col(4), col(8), col(12), col(0), pl.BlockSpec((None, None, 3, LANES), lambda j: (j, l, 0, 0))],
        out_specs=[col(0), col(0), col(0), col(0), pl.BlockSpec((None, 3, LANES), lambda j: (j, 0, 0))],
        out_shape=[sec, sec, sec, sec, jax.ShapeDtypeStruct((nblk, 3, LANES), F32)],
        compiler_params=_params(),
    )(proj, proj, proj, proj, dya, wconv)


def pool_bwd(proj, dyp, wpool, pscale3, l):
    T = proj.shape[0]
    R = R_POOL
    ngrp = len(POOL_WINDOWS)
    nchunk = T // R

    def body(p_ref, g_ref, dy_ref, w_ref, s_ref, du_ref, dg_ref, dw_ref, ds_ref,
             pooled_s, mixed_s, dmix_s, dpool_s):
        grp = pl.program_id(0)

        def group(w):
            wb = w_ref[...].astype(BF16)
            _pooled_loop(p_ref, pooled_s, w, T)
            mixed_s[...] = jnp.dot(pooled_s[...], wb, preferred_element_type=F32)
            sc = s_ref[...]

            def gate_chunk(i, acc):
                r0 = pl.multiple_of(i * R, R)
                g = g_ref[pl.ds(r0, R), :].astype(F32)
                dy = dy_ref[pl.ds(r0, R), :].astype(F32)
                mixed = mixed_s[pl.ds(r0, R), :]
                sg = jax.nn.sigmoid(g)
                dg_ref[pl.ds(r0, R), :] = (dy * mixed * sc * (sg * (1.0 + g * (1.0 - sg)))).astype(BF16)
                dms = dy * (g * sg)
                dmix_s[pl.ds(r0, R), :] = (dms * sc).astype(BF16)
                return acc + _colsum8(dms * mixed)

            acc = lax.fori_loop(0, nchunk, gate_chunk, jnp.zeros((SUBLANES, LANES), F32))
            ds_ref[...] = jnp.sum(acc, axis=0, keepdims=True)
            dpool_s[pl.ds(0, T), :] = lax.dot_general(dmix_s[...], wb, NT, preferred_element_type=F32)
            dpool_s[pl.ds(T, HIST), :] = jnp.zeros((HIST, LANES), F32)
            dw_ref[...] = lax.dot_general(pooled_s[...], dmix_s[...], TN, preferred_element_type=F32).astype(BF16)

            def back_chunk(i, carry):
                r0 = pl.multiple_of(i * R, R)
                dpe = dpool_s[pl.ds(r0, R + HIST), :]
                e = dpe / _count(r0, R + HIST, w)
                du_ref[pl.ds(r0, R), :] = (_anticausal_window_sum(e, w)[0:R] - dpe[0:R]).astype(BF16)
                return carry

            lax.fori_loop(0, nchunk, back_chunk, 0)

        for k, w in enumerate(POOL_WINDOWS):
            pl.when(grp == k)(functools.partial(group, w))

    def col(off):
        return pl.BlockSpec((T, LANES), lambda j: (0, j + off))

    sec = jax.ShapeDtypeStruct((T, ngrp * LANES), BF16)
    wspec = pl.BlockSpec((None, LANES, LANES), lambda j: (j, 0, 0))
    sspec = pl.BlockSpec((1, LANES), lambda j: (0, j))
    return pl.pallas_call(
        body, name="pool_bwd", grid=(ngrp,),
        in_specs=[col(16), col(20), col(0), _pool_w_spec(l), _pool_s_spec(l)],
        out_specs=[col(0), col(0), wspec, sspec],
        out_shape=[sec, sec, jax.ShapeDtypeStruct((ngrp, LANES, LANES), BF16),
                   jax.ShapeDtypeStruct((1, ngrp * LANES), F32)],
        scratch_shapes=[pltpu.VMEM((T, LANES), BF16), pltpu.VMEM((T, LANES), F32),
                        pltpu.VMEM((T, LANES), BF16), pltpu.VMEM((T + HIST, LANES), F32)],
        compiler_params=_params(),
    )(proj, proj, dyp, wpool, pscale3)


def in_bwd(dsecs, wg, x, dxo, mod4, g_pre3, l):
    T, D = x.shape
    NB, _, CW = wg.shape
    SW = dsecs[0].shape[1]
    nsec = len(dsecs)
    tm = 256
    nt = T // tm

    def body(*refs):
        d_refs = refs[0:nsec]
        w_ref, x_ref, dxo_ref, sh_ref, sc_ref, g_ref = refs[nsec:nsec + 6]
        dxi_ref, dw_ref, dsh_ref, dsc_ref, dg_ref = refs[nsec + 6:nsec + 11]
        dp_s, acc_w, acc_sh, acc_sc, acc_g = refs[nsec + 11:]
        i = pl.program_id(0)

        @pl.when(i == 0)
        def _():
            acc_w[...] = jnp.zeros_like(acc_w)
            acc_sh[...] = jnp.zeros_like(acc_sh)
            acc_sc[...] = jnp.zeros_like(acc_sc)
            acc_g[...] = jnp.zeros_like(acc_g)

        for s in range(nsec):
            dp_s[:, s * SW:(s + 1) * SW] = d_refs[s][...]
        xv = x_ref[...]
        g = g_ref[...]
        r = _rms(xv)
        xh = xv * r
        n = xh * g
        sc1 = 1.0 + sc_ref[...]
        hb = (n * sc1 + sh_ref[...]).astype(BF16)
        dh = lax.dot_general(dp_s[:, 0:CW], w_ref[0], NT, preferred_element_type=F32)
        for j in range(1, NB):
            dh = dh + lax.dot_general(dp_s[:, j * CW:(j + 1) * CW], w_ref[j], NT, preferred_element_type=F32)
        for j in range(NB):
            acc_w[j] += lax.dot_general(hb, dp_s[:, j * CW:(j + 1) * CW], TN, preferred_element_type=F32)
        acc_sh[...] += _colsum8(dh)
        acc_sc[...] += _colsum8(dh * n)
        dnp = dh * sc1
        acc_g[...] += _colsum8(dnp * xh)
        a = dnp * g
        dxi_ref[...] = dxo_ref[...] + r * (a - xh * jnp.mean(a * xh, axis=-1, keepdims=True))

        @pl.when(i == nt - 1)
        def _():
            dw_ref[...] = acc_w[...].astype(BF16)
            dsh_ref[...] = jnp.sum(acc_sh[...], axis=0, keepdims=True)
            dsc_ref[...] = jnp.sum(acc_sc[...], axis=0, keepdims=True)
            dg_ref[...] = jnp.sum(acc_g[...], axis=0, keepdims=True)

    row = pl.BlockSpec((1, D), lambda i: (0, 0))
    tile = pl.BlockSpec((tm, D), lambda i: (i, 0))
    sect = pl.BlockSpec((tm, SW), lambda i: (i, 0))
    wspec = pl.BlockSpec((NB, D, CW), lambda i: (0, 0, 0))
    rowshape = jax.ShapeDtypeStruct((1, D), F32)
    return pl.pallas_call(
        body, name="in_bwd", grid=(nt,),
        in_specs=[sect] * nsec + [wspec, tile, tile, _mod_row(l, 0, D), _mod_row(l, 1, D), _layer_row(l, D)],
        out_specs=[tile, wspec, row, row, row],
        out_shape=[jax.ShapeDtypeStruct((T, D), F32), jax.ShapeDtypeStruct((NB, D, CW), BF16),
                   rowshape, rowshape, rowshape],
        scratch_shapes=[pltpu.VMEM((tm, nsec * SW), BF16), pltpu.VMEM((NB, D, CW), F32),
                        pltpu.VMEM((SUBLANES, D), F32), pltpu.VMEM((SUBLANES, D), F32), pltpu.VMEM((SUBLANES, D), F32)],
        compiler_params=_params(VMEM_BIG),
    )(*dsecs, wg, x, dxo, mod4, mod4, g_pre3)


def _rcopy(src, dst, ssem, rsem, dev):
    return pltpu.make_async_remote_copy(src_ref=src, dst_ref=dst, send_sem=ssem, recv_sem=rsem,
                                        device_id=dev, device_id_type=MESH)


def _peers7(x, y, c):
    out = []
    for m in range(1, N_DEV):
        bx, by, bc = (m >> 2) & 1, (m >> 1) & 1, m & 1
        out.append(((1 - x) if bx else x, (1 - y) if by else y, (1 - c) if bc else c))
    return out


def gather_weights(gis, gos, c8, wc):
    L = len(gis)
    n_w = 3 * 2 * L
    n_rem = n_w + 7 + 3

    def body(*refs):
        c_ref, wc_ref = refs[2 * L], refs[2 * L + 1]
        outs = refs[2 * L + 2:]
        go_all = outs[0:2 * L]
        call, wcall = outs[2 * L], outs[2 * L + 1]
        ssem, rsem, fss, frs, lsem = outs[2 * L + 2:]
        x, y, c = _me()
        myc = 2 * x + y
        me_lin = 4 * x + 2 * y + c
        me = (x, y, c)
        sib = (x, y, 1 - c)
        chips = [(1 - x, y), (x, 1 - y), (1 - x, 1 - y)]

        local = [pltpu.make_async_copy(c_ref, call.at[me_lin], lsem.at[0]),
                 pltpu.make_async_copy(wc_ref, wcall.at[myc], lsem.at[1])]
        for cp in local:
            cp.start()

        sends, recvs, fwds, frecvs = [], [], [], []
        k = 0
        for (px, py) in chips:
            pc = 2 * px + py
            for a in range(2 * L):
                buf = go_all[a]
                h = buf.shape[1] // 2
                rows, orows = pl.ds(c * h, h), pl.ds((1 - c) * h, h)
                sends.append(_rcopy(buf.at[myc, rows], buf.at[myc, rows], ssem.at[k], rsem.at[k], (px, py, c)))
                landed = buf.at[pc, rows]
                recvs.append(_rcopy(landed, landed, ssem.at[k], rsem.at[k], me))
                fwds.append(_rcopy(landed, landed, fss.at[k], frs.at[k], sib))
                other = buf.at[pc, orows]
                frecvs.append(_rcopy(other, other, fss.at[k], frs.at[k], me))
                k += 1
        for m, peer in enumerate(_peers7(x, y, c)):
            plin = 4 * peer[0] + 2 * peer[1] + peer[2]
            sends.append(_rcopy(c_ref, call.at[me_lin], ssem.at[n_w + m], rsem.at[n_w + m], peer))
            recvs.append(_rcopy(call.at[plin], call.at[plin], ssem.at[n_w + m], rsem.at[n_w + m], me))
        for j, (px, py) in enumerate(chips):
            pc = 2 * px + py
            sends.append(_rcopy(wc_ref, wcall.at[myc], ssem.at[n_w + 7 + j], rsem.at[n_w + 7 + j], (px, py, c)))
            recvs.append(_rcopy(wcall.at[pc], wcall.at[pc], ssem.at[n_w + 7 + j], rsem.at[n_w + 7 + j], me))

        for cp in sends:
            cp.start()
        for k in range(n_w):
            recvs[k].wait_recv()
            fwds[k].start()
        for k in range(n_w, n_rem):
            recvs[k].wait_recv()
        for cp in frecvs:
            cp.wait_recv()
        for cp in sends + fwds:
            cp.wait_send()
        for cp in local:
            cp.wait()

    bufs = [b for l in range(L) for b in (gis[l], gos[l])]
    out_shape = ([jax.ShapeDtypeStruct(b.shape, b.dtype) for b in bufs]
                 + [jax.ShapeDtypeStruct((N_DEV, SUBLANES, LANES), F32),
                    jax.ShapeDtypeStruct((N_CHIPS, wc.shape[0], 3, LANES), F32)])
    outs = pl.pallas_call(
        body, name="gather_weights",
        in_specs=[ANY] * (2 * L + 2), out_specs=[ANY] * (2 * L + 2), out_shape=out_shape,
        input_output_aliases={a: a for a in range(2 * L)},
        scratch_shapes=[pltpu.SemaphoreType.DMA((n_rem,)), pltpu.SemaphoreType.DMA((n_rem,)),
                        pltpu.SemaphoreType.DMA((n_w,)), pltpu.SemaphoreType.DMA((n_w,)),
                        pltpu.SemaphoreType.DMA((2,))],
        compiler_params=_params(n_grid=0),
    )(*bufs, c8, wc)
    return outs[0:2 * L:2], outs[1:2 * L:2], outs[2 * L], outs[2 * L + 1]


def exchange_mod(part):
    def body(p_ref, o_ref, ssem, rsem, lsem):
        x, y, c = _me()
        myc = 2 * x + y
        chips = [(1 - x, y), (x, 1 - y), (1 - x, 1 - y)]
        loc = pltpu.make_async_copy(p_ref, o_ref.at[myc], lsem)
        loc.start()
        sends = [_rcopy(p_ref, o_ref.at[myc], ssem.at[j], rsem.at[j], (px, py, c)) for j, (px, py) in enumerate(chips)]
        for cp in sends:
            cp.start()
        for j, (px, py) in enumerate(chips):
            slot = o_ref.at[2 * px + py]
            _rcopy(slot, slot, ssem.at[j], rsem.at[j], (x, y, c)).wait_recv()
        for cp in sends:
            cp.wait_send()
        loc.wait()

    return pl.pallas_call(
        body, name="exchange_mod", in_specs=[ANY], out_specs=ANY,
        out_shape=jax.ShapeDtypeStruct((N_CHIPS,) + part.shape, part.dtype),
        scratch_shapes=[pltpu.SemaphoreType.DMA((3,)), pltpu.SemaphoreType.DMA((3,)), pltpu.SemaphoreType.DMA],
        compiler_params=_params(n_grid=0),
    )(part)


def sibling_exchange(dwi, dwo, dwp, pack):
    L = len(dwi)
    groups = (dwi, dwo, dwp)
    n_big = 3 * L

    def body(*refs):
        ins = [refs[g * L:(g + 1) * L] for g in range(3)]
        pack_ref = refs[3 * L]
        outs = [refs[3 * L + 1 + g * L:3 * L + 1 + (g + 1) * L] for g in range(3)]
        packs = refs[6 * L + 1]
        ssem, rsem, lsem = refs[6 * L + 2:]
        x, y, c = _me()
        me = (x, y, c)
        me_lin = 4 * x + 2 * y + c
        sib = (x, y, 1 - c)
        loc = pltpu.make_async_copy(pack_ref, packs.at[me_lin], lsem)
        loc.start()
        sends, recvs = [], []
        k = 0
        for g in range(3):
            for l in range(L):
                half = ins[g][l].shape[1] // 2
                sends.append(_rcopy(ins[g][l].at[:, pl.ds((1 - c) * half, half)], outs[g][l],
                                    ssem.at[k], rsem.at[k], sib))
                recvs.append(_rcopy(outs[g][l], outs[g][l], ssem.at[k], rsem.at[k], me))
                k += 1
        for m, peer in enumerate(_peers7(x, y, c)):
            plin = 4 * peer[0] + 2 * peer[1] + peer[2]
            sends.append(_rcopy(pack_ref, packs.at[me_lin], ssem.at[n_big + m], rsem.at[n_big + m], peer))
            recvs.append(_rcopy(packs.at[plin], packs.at[plin], ssem.at[n_big + m], rsem.at[n_big + m], me))
        for cp in sends:
            cp.start()
        for cp in recvs:
            cp.wait_recv()
        for cp in sends:
            cp.wait_send()
        loc.wait()

    def halves(arrs):
        return [jax.ShapeDtypeStruct((a.shape[0], a.shape[1] // 2) + a.shape[2:], a.dtype) for a in arrs]

    out_shape = halves(dwi) + halves(dwo) + halves(dwp) + [jax.ShapeDtypeStruct((N_DEV,) + pack.shape, pack.dtype)]
    outs = pl.pallas_call(
        body, name="sibling_exchange",
        in_specs=[ANY] * (3 * L + 1), out_specs=[ANY] * (3 * L + 1), out_shape=out_shape,
        scratch_shapes=[pltpu.SemaphoreType.DMA((n_big + 7,)), pltpu.SemaphoreType.DMA((n_big + 7,)),
                        pltpu.SemaphoreType.DMA],
        compiler_params=_params(n_grid=0),
    )(*dwi, *dwo, *dwp, pack)
    return outs[0:L], outs[L:2 * L], outs[2 * L:3 * L], outs[3 * L]


def chip_exchange(chi, cho, chp):
    L = len(chi)
    n = 3 * L * 3

    def body(*refs):
        ins = refs[0:3 * L]
        outs = refs[3 * L:6 * L]
        ssem, rsem = refs[6 * L:]
        x, y, c = _me()
        chips = [(1 - x, y), (x, 1 - y), (1 - x, 1 - y)]
        sends, recvs = [], []
        k = 0
        for j, (px, py) in enumerate(chips):
            pc = 2 * px + py
            for a in range(3 * L):
                sends.append(_rcopy(ins[a].at[pc], outs[a].at[j], ssem.at[k], rsem.at[k], (px, py, c)))
                recvs.append(_rcopy(outs[a].at[j], outs[a].at[j], ssem.at[k], rsem.at[k], (x, y, c)))
                k += 1
        for cp in sends:
            cp.start()
        for cp in recvs:
            cp.wait_recv()
        for cp in sends:
            cp.wait_send()

    arrs = [a for l in range(L) for a in (chi[l], cho[l], chp[l])]
    outs = pl.pallas_call(
        body, name="chip_exchange",
        in_specs=[ANY] * (3 * L), out_specs=[ANY] * (3 * L),
        out_shape=[jax.ShapeDtypeStruct((3,) + a.shape[1:], a.dtype) for a in arrs],
        scratch_shapes=[pltpu.SemaphoreType.DMA((n,)), pltpu.SemaphoreType.DMA((n,))],
        compiler_params=_params(n_grid=0),
    )(*arrs)
    return outs[0::3], outs[1::3], outs[2::3]


def spread_reduced(gi, go, gp):
    L, D, _ = gi.shape
    RO = go.shape[1]
    HP = LANES // 2
    n = 2 * L + 7 * L

    def body(gi_in, go_in, gp_in, gi, go, gp, ssem, rsem):
        x, y, c = _me()
        me = (x, y, c)
        myc = 2 * x + y
        sib = (x, y, 1 - c)
        peers = _peers7(x, y, c)
        hi, ho, hp = pl.ds(c * (D // 2), D // 2), pl.ds(c * (RO // 2), RO // 2), pl.ds(c * HP, HP)
        ohi, oho = pl.ds((1 - c) * (D // 2), D // 2), pl.ds((1 - c) * (RO // 2), RO // 2)
        sends, recvs = [], []
        k = 0
        for l in range(L):
            sends.append(_rcopy(gi.at[l, hi], gi.at[l, hi], ssem.at[k], rsem.at[k], sib))
            recvs.append(_rcopy(gi.at[l, ohi], gi.at[l, ohi], ssem.at[k], rsem.at[k], me))
            k += 1
            sends.append(_rcopy(go.at[l, ho], go.at[l, ho], ssem.at[k], rsem.at[k], sib))
            recvs.append(_rcopy(go.at[l, oho], go.at[l, oho], ssem.at[k], rsem.at[k], me))
            k += 1
            for peer in peers:
                pchip = 2 * peer[0] + peer[1]
                prow = pl.ds(peer[2] * HP, HP)
                sends.append(_rcopy(gp.at[l, myc, hp], gp.at[l, myc, hp], ssem.at[k], rsem.at[k], peer))
                recvs.append(_rcopy(gp.at[l, pchip, prow], gp.at[l, pchip, prow], ssem.at[k], rsem.at[k], me))
                k += 1
        for cp in sends:
            cp.start()
        for cp in recvs:
            cp.wait_recv()
        for cp in sends:
            cp.wait_send()

    return pl.pallas_call(
        body, name="spread_reduced",
        in_specs=[ANY] * 3, out_specs=[ANY] * 3,
        out_shape=[jax.ShapeDtypeStruct(a.shape, a.dtype) for a in (gi, go, gp)],
        input_output_aliases={0: 0, 1: 1, 2: 2},
        scratch_shapes=[pltpu.SemaphoreType.DMA((n,)), pltpu.SemaphoreType.DMA((n,))],
        compiler_params=_params(n_grid=0),
    )(gi, go, gp)


def add_sibling(cidx, mine, sib):
    def body(c_ref, *refs):
        for a in range(3):
            m, s, o = refs[a], refs[3 + a], refs[6 + a]
            o[...] = (m[...].astype(F32) + s[...].astype(F32)).astype(BF16)

    def mine_spec(a):
        h = a.shape[1] // 2
        return pl.BlockSpec((None, h, a.shape[2]), lambda j, c_ref: (j, c_ref[0], 0))

    def sib_spec(a):
        return pl.BlockSpec((None,) + a.shape[1:], lambda j, c_ref: (j, 0, 0))

    return pl.pallas_call(
        body, name="add_sibling",
        grid_spec=pltpu.PrefetchScalarGridSpec(
            num_scalar_prefetch=1, grid=(N_CHIPS,),
            in_specs=[mine_spec(a) for a in mine] + [sib_spec(a) for a in sib],
            out_specs=[sib_spec(a) for a in sib]),
        out_shape=[jax.ShapeDtypeStruct(a.shape, BF16) for a in sib],
        compiler_params=_params(VMEM_BIG),
    )(cidx, *mine, *sib)


def sum_chips(pos, own, rb, acc, l, shapes):
    nq = 4
    n_in = 6 + (3 if acc is not None else 0)

    def body(pos_ref, *refs):
        for a in range(3):
            m, b, o = refs[a], refs[3 + a], refs[n_in + a]
            s = m[...].astype(F32)
            for j in range(3):
                s = s + b[j].astype(F32)
            o[...] = s

    def own_spec(a):
        return pl.BlockSpec((None, a.shape[1] // nq, a.shape[2]), lambda q, p: (p[1], q, 0))

    def rb_spec(a):
        return pl.BlockSpec((3, a.shape[1] // nq, a.shape[2]), lambda q, p: (0, q, 0))

    hi, ho, hp = own[0].shape[1] // nq, own[1].shape[1] // nq, own[2].shape[1] // nq
    out_specs = [pl.BlockSpec((None, hi, shapes[0][2]), lambda q, p: (l, p[0] * nq + q, 0)),
                 pl.BlockSpec((None, ho, shapes[1][2]), lambda q, p: (l, p[0] * nq + q, 0)),
                 pl.BlockSpec((None, None, hp, LANES), lambda q, p: (l, p[1], p[0] * nq + q, 0))]
    in_specs = [own_spec(a) for a in own] + [rb_spec(a) for a in rb]
    args = list(own) + list(rb)
    aliases = {}
    if acc is not None:
        in_specs += [ANY] * 3
        args += list(acc)
        aliases = {7: 0, 8: 1, 9: 2}
    return pl.pallas_call(
        body, name="sum_chips",
        grid_spec=pltpu.PrefetchScalarGridSpec(num_scalar_prefetch=1, grid=(nq,), in_specs=in_specs, out_specs=out_specs),
        out_shape=[jax.ShapeDtypeStruct(s, F32) for s in shapes],
        input_output_aliases=aliases,
        compiler_params=_params(VMEM_BIG),
    )(pos, *args)


def pack_small(per_layer, loss_blk):
    L = len(per_layer)
    D = per_layer[0][0].shape[1]

    def body(*refs):
        o = refs[-1]
        lb = refs[-2]
        o[...] = jnp.zeros_like(o)
        for l in range(L):
            dgpre, dgpost, dsh, dsc, dgt, dps, dwc = refs[7 * l:7 * l + 7]
            base = SUBLANES * l
            for r, src in enumerate((dgpre, dgpost, dsh, dsc, dgt)):
                o[pl.ds(base + r, 1), :] = src[...]
            o[pl.ds(base + 5, 1), 0:dps.shape[1]] = dps[...]
            for j in range(dwc.shape[0]):
                for k in range(3):
                    idx = 3 * j + k
                    o[pl.ds(base + 6 + idx // 8, 1), (idx % 8) * LANES:(idx % 8 + 1) * LANES] = dwc[j, pl.ds(k, 1), :]
        o[pl.ds(5, 1), 4 * LANES:5 * LANES] = lb[pl.ds(0, 1), :]

    flat = [a for layer in per_layer for a in layer]
    return pl.pallas_call(
        body, name="pack_small",
        out_shape=jax.ShapeDtypeStruct((L * SUBLANES, D), F32),
        compiler_params=_params(n_grid=0),
    )(*flat, loss_blk)


def sum_packs(packs):
    def body(p_ref, o_ref):
        s = p_ref[0]
        for d in range(1, N_DEV):
            s = s + p_ref[d]
        o_ref[...] = s

    return pl.pallas_call(
        body, name="sum_packs",
        out_shape=jax.ShapeDtypeStruct(packs.shape[1:], F32),
        compiler_params=_params(n_grid=0),
    )(packs)


def _adamw_math(w, g, m, v):
    m = ADAM_B1 * m + (1.0 - ADAM_B1) * g
    v = ADAM_B2 * v + (1.0 - ADAM_B2) * (g * g)
    m_hat = m / (1.0 - ADAM_B1 ** ADAM_STEP)
    v_hat = v / (1.0 - ADAM_B2 ** ADAM_STEP)
    delta = -ADAM_LR * (m_hat / (jnp.sqrt(v_hat) + ADAM_EPS) + ADAM_WD * w)
    return delta, m, v


def adamw(w, g, m, v, block, name):
    grid = tuple(s // b for s, b in zip(w.shape, block))

    def body(w_ref, g_ref, m_ref, v_ref, d_ref, mo_ref, vo_ref):
        d, mm, vv = _adamw_math(w_ref[...], g_ref[...], m_ref[...], v_ref[...])
        d_ref[...] = d
        mo_ref[...] = mm
        vo_ref[...] = vv

    spec = pl.BlockSpec(block, lambda *idx: idx)
    shape = jax.ShapeDtypeStruct(w.shape, F32)
    return pl.pallas_call(
        body, name=name, grid=grid,
        in_specs=[spec] * 4, out_specs=[spec] * 3, out_shape=[shape] * 3,
        compiler_params=_params(VMEM_BIG, n_grid=len(grid)),
    )(w, g, m, v)


def ada_finish(c_all, dmod, w, m, v):
    L, D, CW = w.shape
    hD = D // 2

    def body(c_ref, d_ref, w_ref, m_ref, v_ref, g_ref, dl_ref, mo_ref, vo_ref):
        cv = c_ref[...]
        z = jnp.zeros_like(cv)
        ca = jnp.concatenate([cv * jax.nn.sigmoid(cv), z], axis=0).astype(BF16)
        dm = jnp.concatenate([d_ref[0], jnp.zeros_like(d_ref[0])], axis=0).astype(BF16)
        g = lax.dot_general(ca, dm, TN, preferred_element_type=F32)
        g_ref[0] = g
        d, mm, vv = _adamw_math(w_ref[0], g, m_ref[0], v_ref[0])
        dl_ref[0] = d
        mo_ref[0] = mm
        vo_ref[0] = vv

    big = pl.BlockSpec((1, hD, CW), lambda l, h: (l, h, 0))
    shape = jax.ShapeDtypeStruct(w.shape, F32)
    return pl.pallas_call(
        body, name="ada_finish", grid=(L, 2),
        in_specs=[pl.BlockSpec((N_DEV, hD), lambda l, h: (0, h)), pl.BlockSpec((1, N_DEV, CW), lambda l, h: (l, 0, 0)),
                  big, big, big],
        out_specs=[big] * 4, out_shape=[shape] * 4,
        compiler_params=_params(VMEM_BIG, n_grid=2),
    )(c_all, dmod, w, m, v)


def kernel(x, c, w_ada, b_ada, g_pre, w_in, w_conv, w_pool, pool_scale, w_out, g_post, loss_target, m_w_ada, m_b_ada, m_g_pre, m_w_in, m_w_conv, m_w_pool, m_pool_scale, m_w_out, m_g_post, v_w_ada, v_b_ada, v_g_pre, v_w_in, v_w_conv, v_w_pool, v_pool_scale, v_w_out, v_g_post):
    L, D, CW = w_in.shape
    RO = w_out.shape[1]
    T = x.shape[1]
    ix, iy, ic = _me()
    chip = 2 * ix + iy
    me_lin = 4 * ix + 2 * iy + ic

    pos = jnp.stack([ic, chip]).astype(jnp.int32)
    g_pre3, g_post3 = g_pre.reshape(L, 1, D), g_post.reshape(L, 1, D)
    pscale3 = pool_scale.reshape(L, 1, pool_scale.shape[1])

    own = [cast_weights(pos, w_in, w_out, l) for l in range(L)]
    wg_in, wg_out, c_all3, wconv_all = gather_weights([o[0] for o in own], [o[1] for o in own],
                                                      c.reshape(SUBLANES, LANES), w_conv)
    c_all = c_all3.reshape(N_DEV, D)
    b_my = lax.dynamic_slice_in_dim(b_ada, chip * CW, CW, axis=1)
    mod_all = exchange_mod(mod_part(c_all, w_ada, b_my))
    mod = lax.dynamic_index_in_dim(mod_all, me_lin, axis=2, keepdims=False)
    mod4 = jnp.transpose(mod, (1, 0, 2)).reshape(L, 3, 1, D)
    wg_out = [w.reshape(N_CHIPS * RO, D) for w in wg_out]

    xs, projs, yas, yps, ys = [x.reshape(T, D)], [], [], [], []
    for l in range(L):
        proj = proj_fwd(xs[l], mod4, g_pre3, wg_in[l], l)
        ya = conv_fwd(proj, wconv_all, l)
        yp = pool_fwd(proj, w_pool, pscale3, l)
        xn, yv = out_fwd(ya, yp, wg_out[l], xs[l], mod4, g_post3, l)
        xs.append(xn)
        projs.append(proj)
        yas.append(ya)
        yps.append(yp)
        ys.append(yv)

    dx, loss_blk = loss_head(xs[L], loss_target.reshape(T, D))

    dwi, dwo, dwp, smalls = [None] * L, [None] * L, [None] * L, [None] * L
    for l in reversed(range(L)):
        dya, dyp, dwo_l, dgate, dgpost = out_bwd(dx, ys[l], yas[l], yps[l], wg_out[l], mod4, g_post3, l)
        du_a, db_a, dc_a, dg_a, dwc = conv_bwd(projs[l], dya, wconv_all, l)
        du_p, dg_p, dwp_l, dps = pool_bwd(projs[l], dyp, w_pool, pscale3, l)
        dx, dwi_l, dshift, dscale, dgpre = in_bwd([du_a, db_a, dc_a, dg_a, du_p, dg_p], wg_in[l], xs[l], dx,
                                                  mod4, g_pre3, l)
        dwi[l], dwo[l], dwp[l] = dwi_l, dwo_l.reshape(N_CHIPS, RO, D), dwp_l
        smalls[l] = (dgpre, dgpost, dshift, dscale, dgate, dps, dwc)
    grad_x = dx.reshape(1, T, D)
    pack = pack_small(smalls, loss_blk)

    sb_i, sb_o, sb_p, packs_all = sibling_exchange(dwi, dwo, dwp, pack)
    ch_i, ch_o, ch_p = [], [], []
    for l in range(L):
        a, b, p = add_sibling(pos, (dwi[l], dwo[l], dwp[l]), (sb_i[l], sb_o[l], sb_p[l]))
        ch_i.append(a)
        ch_o.append(b)
        ch_p.append(p)
    rb_i, rb_o, rb_p = chip_exchange(ch_i, ch_o, ch_p)
    acc = None
    for l in range(L):
        acc = sum_chips(pos, (ch_i[l], ch_o[l], ch_p[l]), (rb_i[l], rb_o[l], rb_p[l]), acc, l,
                        (w_in.shape, w_out.shape, w_pool.shape))
    g_w_in, g_w_out, g_w_pool = spread_reduced(*acc)

    small = sum_packs(packs_all).reshape(L, SUBLANES, D)
    loss = small[0, 5, 4 * LANES]
    g_g_pre = small[:, 0]
    g_g_post = small[:, 1]
    g_b_ada = small[:, 2:5].reshape(L, 3 * D)
    g_pscale = small[:, 5, 0:pool_scale.shape[1]]
    g_w_conv = small[:, 6:8].reshape(L, 2 * D)[:, 0:N_CHIPS * 3 * LANES].reshape(L, N_CHIPS, 3, LANES)
    g_w_conv = lax.dynamic_index_in_dim(g_w_conv, chip, axis=1, keepdims=False)
    dmod_all = packs_all.reshape(N_DEV, L, SUBLANES, D)[:, :, 2:5].reshape(N_DEV, L, 3 * D)
    dmod_my = jnp.transpose(lax.dynamic_slice_in_dim(dmod_all, chip * CW, CW, axis=2), (1, 0, 2))

    g_w_ada, d_w_ada, nm_w_ada, nv_w_ada = ada_finish(c_all, dmod_my, w_ada, m_w_ada, v_w_ada)
    d_w_in, nm_w_in, nv_w_in = adamw(w_in, g_w_in, m_w_in, v_w_in, (1, D // 2, CW), "adamw_w_in")
    d_w_out, nm_w_out, nv_w_out = adamw(w_out, g_w_out, m_w_out, v_w_out, (1, RO, D), "adamw_w_out")
    pshape = (L, N_CHIPS * LANES, LANES)
    d_w_pool, nm_w_pool, nv_w_pool = adamw(w_pool.reshape(pshape), g_w_pool.reshape(pshape), m_w_pool.reshape(pshape),
                                           v_w_pool.reshape(pshape), (1,) + pshape[1:], "adamw_w_pool")
    d_w_pool, nm_w_pool, nv_w_pool = [a.reshape(w_pool.shape) for a in (d_w_pool, nm_w_pool, nv_w_pool)]

    def small_adamw(w, g, m, v, name):
        shp = (1,) + w.shape if w.ndim == 2 else w.shape
        outs = adamw(w.reshape(shp), g.reshape(shp), m.reshape(shp), v.reshape(shp), shp, name)
        return [a.reshape(w.shape) for a in outs]

    d_b_ada, nm_b_ada, nv_b_ada = small_adamw(b_ada, g_b_ada, m_b_ada, v_b_ada, "adamw_b_ada")
    d_g_pre, nm_g_pre, nv_g_pre = small_adamw(g_pre, g_g_pre, m_g_pre, v_g_pre, "adamw_g_pre")
    d_w_conv, nm_w_conv, nv_w_conv = small_adamw(w_conv, g_w_conv, m_w_conv, v_w_conv, "adamw_w_conv")
    d_pscale, nm_pscale, nv_pscale = small_adamw(pool_scale, g_pscale, m_pool_scale, v_pool_scale, "adamw_pool_scale")
    d_g_post, nm_g_post, nv_g_post = small_adamw(g_post, g_g_post, m_g_post, v_g_post, "adamw_g_post")

    return (loss, grad_x,
            g_w_ada, g_b_ada, g_g_pre, g_w_in, g_w_conv, g_w_pool, g_pscale, g_w_out, g_g_post,
            d_w_ada, d_b_ada, d_g_pre, d_w_in, d_w_conv, d_w_pool, d_pscale, d_w_out, d_g_post,
            nm_w_ada, nm_b_ada, nm_g_pre, nm_w_in, nm_w_conv, nm_w_pool, nm_pscale, nm_w_out, nm_g_post,
            nv_w_ada, nv_b_ada, nv_g_pre, nv_w_in, nv_w_conv, nv_w_pool, nv_pscale, nv_w_out, nv_g_post)
```

```python
import functools

import jax
import jax.numpy as jnp
from jax import lax
from jax.experimental import pallas as pl
from jax.experimental.pallas import tpu as pltpu

F32 = jnp.float32
BF16 = jnp.bfloat16
MESH = pl.DeviceIdType.MESH
ANY = pl.BlockSpec(memory_space=pl.ANY)

NORM_EPS = 1e-6
POOL_WINDOWS = (2, 4, 8, 16)
ADAM_LR = 0.001
ADAM_B1 = 0.9
ADAM_B2 = 0.999
ADAM_EPS = 1e-08
ADAM_WD = 0.01
ADAM_STEP = 10

N_CHIPS = 4
N_DEV = 8
LANES = 128
SUBLANES = 8
VMEM_BIG = 56 * 1024 * 1024
HIST = 16
R_CONV = 32
R_POOL = 64

NT = (((1,), (1,)), ((), ()))
TN = (((0,), (0,)), ((), ()))


def _params(vmem=None, n_grid=1):
    kw = {}
    if n_grid:
        kw["dimension_semantics"] = ("arbitrary",) * n_grid
    if vmem is not None:
        kw["vmem_limit_bytes"] = vmem
    return pltpu.CompilerParams(**kw)


def _colsum8(v):
    n, d = v.shape
    return v.reshape(n // SUBLANES, SUBLANES, d).sum(axis=0)


def _rms(v):
    return lax.rsqrt(jnp.mean(v * v, axis=-1, keepdims=True) + NORM_EPS)


def _shift_down(ext, k, rows):
    if k == 0:
        return ext[HIST:HIST + rows]
    return pltpu.roll(ext, k, 0)[HIST:HIST + rows]


def _shift_up(ext, k, rows):
    if k == 0:
        return ext[0:rows]
    return pltpu.roll(ext, ext.shape[0] - k, 0)[0:rows]


def _load_ext(ref, r0, h0, first, rows):
    hist = ref[pl.ds(h0, HIST), :].astype(F32)
    hist = jnp.where(first, 0.0, hist)
    cur = ref[pl.ds(r0, rows), :].astype(F32)
    return jnp.concatenate([hist, cur], axis=0)


def _me():
    return lax.axis_index("x"), lax.axis_index("y"), lax.axis_index("c")


def cast_weights(pos, w_in, w_out, l):
    _, D, CW = w_in.shape
    RO = w_out.shape[1]

    def body(pos_ref, wi, wo, oi, oo):
        oi[...] = wi[...].astype(BF16)
        oo[...] = wo[...].astype(BF16)

    return pl.pallas_call(
        body, name="cast_w",
        grid_spec=pltpu.PrefetchScalarGridSpec(
            num_scalar_prefetch=1, grid=(2,),
            in_specs=[pl.BlockSpec((None, D // 2, CW), lambda h, p: (l, h, 0)),
                      pl.BlockSpec((None, RO // 2, D), lambda h, p: (l, h, 0))],
            out_specs=[pl.BlockSpec((None, D // 2, CW), lambda h, p: (p[1], h, 0)),
                       pl.BlockSpec((None, RO // 2, D), lambda h, p: (p[1], h, 0))]),
        out_shape=[jax.ShapeDtypeStruct((N_CHIPS, D, CW), BF16), jax.ShapeDtypeStruct((N_CHIPS, RO, D), BF16)],
        compiler_params=_params(),
    )(pos, w_in, w_out)


def mod_part(c_all, w_ada, b_my):
    L, D, CW = w_ada.shape

    def body(c_ref, w_ref, b_ref, o_ref):
        cv = c_ref[...]
        ca = (cv * jax.nn.sigmoid(cv)).astype(BF16)
        o_ref[0] = jnp.dot(ca, w_ref[0].astype(BF16), preferred_element_type=F32) + b_ref[0]

    return pl.pallas_call(
        body, name="mod_part", grid=(L,),
        in_specs=[pl.BlockSpec((N_DEV, D), lambda l: (0, 0)),
                  pl.BlockSpec((1, D, CW), lambda l: (l, 0, 0)),
                  pl.BlockSpec((1, 1, CW), lambda l: (l, 0, 0))],
        out_specs=pl.BlockSpec((1, N_DEV, CW), lambda l: (l, 0, 0)),
        out_shape=jax.ShapeDtypeStruct((L, N_DEV, CW), F32),
        compiler_params=_params(VMEM_BIG),
    )(c_all, w_ada, b_my.reshape(L, 1, CW))


def _mod_row(l, k, D):
    return pl.BlockSpec((None, None, 1, D), lambda *_: (l, k, 0, 0))


def _layer_row(l, D):
    return pl.BlockSpec((None, 1, D), lambda *_: (l, 0, 0))


def proj_fwd(x, mod4, g_pre3, wg, l):
    T, D = x.shape
    NB, _, CW = wg.shape
    tm = 512

    def body(x_ref, sh_ref, sc_ref, g_ref, w_ref, o_ref):
        xv = x_ref[...]
        h = (xv * _rms(xv) * g_ref[...]) * (1.0 + sc_ref[...]) + sh_ref[...]
        hb = h.astype(BF16)
        for j in range(NB):
            o_ref[:, j * CW:(j + 1) * CW] = jnp.dot(hb, w_ref[j], preferred_element_type=F32).astype(BF16)

    return pl.pallas_call(
        body, name="proj_fwd", grid=(T // tm,),
        in_specs=[pl.BlockSpec((tm, D), lambda i: (i, 0)), _mod_row(l, 0, D), _mod_row(l, 1, D), _layer_row(l, D),
                  pl.BlockSpec((NB, D, CW), lambda i: (0, 0, 0))],
        out_specs=pl.BlockSpec((tm, NB * CW), lambda i: (i, 0)),
        out_shape=jax.ShapeDtypeStruct((T, NB * CW), BF16),
        compiler_params=_params(VMEM_BIG),
    )(x, mod4, mod4, g_pre3, wg)


def conv_fwd(proj, wconv, l):
    T = proj.shape[0]
    R = R_CONV
    nblk = 4

    def body(u_ref, b_ref, c_ref, g_ref, w_ref, o_ref):
        w0 = w_ref[pl.ds(0, 1), :]
        w1 = w_ref[pl.ds(1, 1), :]
        w2 = w_ref[pl.ds(2, 1), :]

        def chunk(i, carry):
            r0 = pl.multiple_of(i * R, R)
            h0 = pl.multiple_of(jnp.maximum(r0 - HIST, 0), HIST)
            first = i == 0
            ca = _load_ext(c_ref, r0, h0, first, R) * _load_ext(u_ref, r0, h0, first, R)
            conv = w2 * ca[HIST:] + w1 * _shift_down(ca, 1, R) + w0 * _shift_down(ca, 2, R)
            g = g_ref[pl.ds(r0, R), :].astype(F32)
            b = b_ref[pl.ds(r0, R), :].astype(F32)
            o_ref[pl.ds(r0, R), :] = (b * conv * (g * jax.nn.sigmoid(g))).astype(BF16)
            return carry

        lax.fori_loop(0, T // R, chunk, 0)

    def col(off):
        return pl.BlockSpec((T, LANES), lambda j: (0, j + off))

    return pl.pallas_call(
        body, name="conv_fwd", grid=(nblk,),
        in_specs=[col(0), col(4), col(8), col(12), pl.BlockSpec((None, None, 3, LANES), lambda j: (j, l, 0, 0))],
        out_specs=pl.BlockSpec((T, LANES), lambda j: (0, j)),
        out_shape=jax.ShapeDtypeStruct((T, nblk * LANES), BF16),
        compiler_params=_params(),
    )(proj, proj, proj, proj, wconv)


def _causal_window_sum(ext, w):
    s, k = ext, 1
    while k < w:
        s = s + pltpu.roll(s, k, 0)
        k *= 2
    return s


def _anticausal_window_sum(ext, w):
    s, k = ext, 1
    n = ext.shape[0]
    while k < w:
        s = s + pltpu.roll(s, n - k, 0)
        k *= 2
    return s


def _count(r0, rows, w):
    t = r0 + lax.broadcasted_iota(jnp.int32, (rows, LANES), 0)
    return jnp.minimum(t + 1, w).astype(F32)


def _pooled_loop(p_ref, pooled_s, w, T):
    R = R_POOL

    def chunk(i, carry):
        r0 = pl.multiple_of(i * R, R)
        h0 = pl.multiple_of(jnp.maximum(r0 - HIST, 0), HIST)
        ext = _load_ext(p_ref, r0, h0, i == 0, R)
        ws = _causal_window_sum(ext, w)[HIST:]
        pooled_s[pl.ds(r0, R), :] = (ws / _count(r0, R, w) - ext[HIST:]).astype(BF16)
        return carry

    lax.fori_loop(0, T // R, chunk, 0)


def _pool_w_spec(l):
    return pl.BlockSpec((None, None, LANES, LANES), lambda j: (l, j, 0, 0))


def _pool_s_spec(l):
    return pl.BlockSpec((None, 1, LANES), lambda j: (l, 0, j))


def pool_fwd(proj, wpool, pscale3, l):
    T = proj.shape[0]
    R = R_POOL
    ngrp = len(POOL_WINDOWS)

    def body(p_ref, g_ref, w_ref, s_ref, o_ref, pooled_s, mixed_s):
        grp = pl.program_id(0)

        def group(w):
            _pooled_loop(p_ref, pooled_s, w, T)
            mixed_s[...] = jnp.dot(pooled_s[...], w_ref[...].astype(BF16), preferred_element_type=F32)
            sc = s_ref[...]

            def chunk(i, carry):
                r0 = pl.multiple_of(i * R, R)
                g = g_ref[pl.ds(r0, R), :].astype(F32)
                o_ref[pl.ds(r0, R), :] = (mixed_s[pl.ds(r0, R), :] * sc * (g * jax.nn.sigmoid(g))).astype(BF16)
                return carry

            lax.fori_loop(0, T // R, chunk, 0)

        for k, w in enumerate(POOL_WINDOWS):
            pl.when(grp == k)(functools.partial(group, w))

    return pl.pallas_call(
        body, name="pool_fwd", grid=(ngrp,),
        in_specs=[pl.BlockSpec((T, LANES), lambda j: (0, j + 16)), pl.BlockSpec((T, LANES), lambda j: (0, j + 20)),
                  _pool_w_spec(l), _pool_s_spec(l)],
        out_specs=pl.BlockSpec((T, LANES), lambda j: (0, j)),
        out_shape=jax.ShapeDtypeStruct((T, ngrp * LANES), BF16),
        scratch_shapes=[pltpu.VMEM((T, LANES), BF16), pltpu.VMEM((T, LANES), F32)],
        compiler_params=_params(),
    )(proj, proj, wpool, pscale3)


def out_fwd(ya, yp, wo, x, mod4, g_post3, l):
    T, D = x.shape
    H = ya.shape[1]
    tm = 512

    def body(ya_ref, yp_ref, wo_ref, x_ref, gt_ref, g_ref, xn_ref, y_ref):
        y = (jnp.dot(ya_ref[...], wo_ref[0:H, :], preferred_element_type=F32)
             + jnp.dot(yp_ref[...], wo_ref[H:2 * H, :], preferred_element_type=F32))
        xn_ref[...] = x_ref[...] + gt_ref[...] * (y * _rms(y) * g_ref[...])
        y_ref[...] = y

    tile = pl.BlockSpec((tm, D), lambda i: (i, 0))
    half = pl.BlockSpec((tm, H), lambda i: (i, 0))
    return pl.pallas_call(
        body, name="out_fwd", grid=(T // tm,),
        in_specs=[half, half, pl.BlockSpec((2 * H, D), lambda i: (0, 0)), tile, _mod_row(l, 2, D), _layer_row(l, D)],
        out_specs=[tile, tile],
        out_shape=[jax.ShapeDtypeStruct((T, D), F32), jax.ShapeDtypeStruct((T, D), F32)],
        compiler_params=_params(VMEM_BIG),
    )(ya, yp, wo, x, mod4, g_post3)


def loss_head(xl, target):
    T, D = xl.shape
    tm = 512
    nt = T // tm

    def body(x_ref, t_ref, dx_ref, l_ref, acc):
        i = pl.program_id(0)

        @pl.when(i == 0)
        def _():
            acc[...] = jnp.zeros_like(acc)

        d = x_ref[...] - t_ref[...]
        dx_ref[...] = d * (1.0 / D)
        acc[...] += _colsum8(d * d)

        @pl.when(i == nt - 1)
        def _():
            l_ref[...] = jnp.zeros_like(l_ref) + jnp.sum(acc[...]) * (0.5 / D)

    tile = pl.BlockSpec((tm, D), lambda i: (i, 0))
    return pl.pallas_call(
        body, name="loss_head", grid=(nt,),
        in_specs=[tile, tile],
        out_specs=[tile, pl.BlockSpec((SUBLANES, LANES), lambda i: (0, 0))],
        out_shape=[jax.ShapeDtypeStruct((T, D), F32), jax.ShapeDtypeStruct((SUBLANES, LANES), F32)],
        scratch_shapes=[pltpu.VMEM((SUBLANES, D), F32)],
        compiler_params=_params(VMEM_BIG),
    )(xl, target)


def out_bwd(dx, y, ya, yp, wo, mod4, g_post3, l):
    T, D = dx.shape
    H = ya.shape[1]
    tm = 512
    nt = T // tm

    def body(dx_ref, y_ref, ya_ref, yp_ref, wo_ref, gt_ref, g_ref,
             dya_ref, dyp_ref, dwo_ref, dgt_ref, dg_ref, acc_w, acc_gt, acc_g):
        i = pl.program_id(0)

        @pl.when(i == 0)
        def _():
            acc_w[...] = jnp.zeros_like(acc_w)
            acc_gt[...] = jnp.zeros_like(acc_gt)
            acc_g[...] = jnp.zeros_like(acc_g)

        yv = y_ref[...]
        dxv = dx_ref[...]
        g = g_ref[...]
        r = _rms(yv)
        yn = yv * r
        acc_gt[...] += _colsum8(dxv * (yn * g))
        dn = dxv * gt_ref[...]
        acc_g[...] += _colsum8(dn * yn)
        a = dn * g
        dy = r * (a - yn * jnp.mean(a * yn, axis=-1, keepdims=True))
        dyb = dy.astype(BF16)
        dyc = lax.dot_general(dyb, wo_ref[...], NT, preferred_element_type=F32)
        dya_ref[...] = dyc[:, 0:H].astype(BF16)
        dyp_ref[...] = dyc[:, H:2 * H].astype(BF16)
        acc_w[0:H, :] += lax.dot_general(ya_ref[...], dyb, TN, preferred_element_type=F32)
        acc_w[H:2 * H, :] += lax.dot_general(yp_ref[...], dyb, TN, preferred_element_type=F32)

        @pl.when(i == nt - 1)
        def _():
            dwo_ref[...] = acc_w[...].astype(BF16)
            dgt_ref[...] = jnp.sum(acc_gt[...], axis=0, keepdims=True)
            dg_ref[...] = jnp.sum(acc_g[...], axis=0, keepdims=True)

    row = pl.BlockSpec((1, D), lambda i: (0, 0))
    tile = pl.BlockSpec((tm, D), lambda i: (i, 0))
    half = pl.BlockSpec((tm, H), lambda i: (i, 0))
    full = pl.BlockSpec((2 * H, D), lambda i: (0, 0))
    return pl.pallas_call(
        body, name="out_bwd", grid=(nt,),
        in_specs=[tile, tile, half, half, full, _mod_row(l, 2, D), _layer_row(l, D)],
        out_specs=[half, half, full, row, row],
        out_shape=[jax.ShapeDtypeStruct((T, H), BF16), jax.ShapeDtypeStruct((T, H), BF16),
                   jax.ShapeDtypeStruct((2 * H, D), BF16),
                   jax.ShapeDtypeStruct((1, D), F32), jax.ShapeDtypeStruct((1, D), F32)],
        scratch_shapes=[pltpu.VMEM((2 * H, D), F32), pltpu.VMEM((SUBLANES, D), F32), pltpu.VMEM((SUBLANES, D), F32)],
        compiler_params=_params(VMEM_BIG),
    )(dx, y, ya, yp, wo, mod4, g_post3)


def conv_bwd(proj, dya, wconv, l):
    T = proj.shape[0]
    R = R_CONV
    nblk = 4
    nchunk = T // R

    def body(u_ref, b_ref, c_ref, g_ref, dy_ref, w_ref, du_ref, db_ref, dc_ref, dg_ref, dw_ref):
        w0 = w_ref[pl.ds(0, 1), :]
        w1 = w_ref[pl.ds(1, 1), :]
        w2 = w_ref[pl.ds(2, 1), :]

        def chunk(k, carry):
            head, a0, a1, a2 = carry
            i = nchunk - 1 - k
            r0 = pl.multiple_of(i * R, R)
            h0 = pl.multiple_of(jnp.maximum(r0 - HIST, 0), HIST)
            first = i == 0
            ue = _load_ext(u_ref, r0, h0, first, R)
            ce = _load_ext(c_ref, r0, h0, first, R)
            ca = ce * ue
            ca0 = ca[HIST:]
            ca1 = _shift_down(ca, 1, R)
            ca2 = _shift_down(ca, 2, R)
            conv = w2 * ca0 + w1 * ca1 + w0 * ca2
            g = g_ref[pl.ds(r0, R), :].astype(F32)
            b = b_ref[pl.ds(r0, R), :].astype(F32)
            dy = dy_ref[pl.ds(r0, R), :].astype(F32)
            sg = jax.nn.sigmoid(g)
            sl = g * sg
            t = dy * conv
            db_ref[pl.ds(r0, R), :] = (t * sl).astype(BF16)
            dg_ref[pl.ds(r0, R), :] = (t * b * (sg * (1.0 + g * (1.0 - sg)))).astype(BF16)
            dconv = dy * b * sl
            a2 = a2 + _colsum8(dconv * ca0)
            a1 = a1 + _colsum8(dconv * ca1)
            a0 = a0 + _colsum8(dconv * ca2)
            e = jnp.concatenate([dconv, head], axis=0)
            dca = w2 * dconv + w1 * _shift_up(e, 1, R) + w0 * _shift_up(e, 2, R)
            du_ref[pl.ds(r0, R), :] = (dca * ce[HIST:]).astype(BF16)
            dc_ref[pl.ds(r0, R), :] = (dca * ue[HIST:]).astype(BF16)
            return dconv[0:SUBLANES], a0, a1, a2

        z = jnp.zeros((SUBLANES, LANES), F32)
        _, a0, a1, a2 = lax.fori_loop(0, nchunk, chunk, (z, z, z, z))
        dw_ref[pl.ds(0, 1), :] = jnp.sum(a0, axis=0, keepdims=True)
        dw_ref[pl.ds(1, 1), :] = jnp.sum(a1, axis=0, keepdims=True)
        dw_ref[pl.ds(2, 1), :] = jnp.sum(a2, axis=0, keepdims=True)

    def col(off):
        return pl.BlockSpec((T, LANES), lambda j: (0, j + off))

    sec = jax.ShapeDtypeStruct((T, nblk * LANES), BF16)
    return pl.pallas_call(
        body, name="conv_bwd", grid=(nblk,),
        in_specs=[col(0), col(4), col(8), col(12), col(0), pl.BlockSpec((None, None, 3, LANES), lambda j: (j, l, 0, 0))],
        out_specs=[col(0), col(0), col(0), col(0), pl.BlockSpec((None, 3, LANES), lambda j: (j, 0, 0))],
        out_shape=[sec, sec, sec, sec, jax.ShapeDtypeStruct((nblk, 3, LANES), F32)],
        compiler_params=_params(),
    )(proj, proj, proj, proj, dya, wconv)


def pool_bwd(proj, dyp, wpool, pscale3, l):
    T = proj.shape[0]
    R = R_POOL
    ngrp = len(POOL_WINDOWS)
    nchunk = T // R

    def body(p_ref, g_ref, dy_ref, w_ref, s_ref, du_ref, dg_ref, dw_ref, ds_ref,
             pooled_s, mixed_s, dmix_s, dpool_s):
        grp = pl.program_id(0)

        def group(w):
            wb = w_ref[...].astype(BF16)
            _pooled_loop(p_ref, pooled_s, w, T)
            mixed_s[...] = jnp.dot(pooled_s[...], wb, preferred_element_type=F32)
            sc = s_ref[...]

            def gate_chunk(i, acc):
                r0 = pl.multiple_of(i * R, R)
                g = g_ref[pl.ds(r0, R), :].astype(F32)
                dy = dy_ref[pl.ds(r0, R), :].astype(F32)
                mixed = mixed_s[pl.ds(r0, R), :]
                sg = jax.nn.sigmoid(g)
                dg_ref[pl.ds(r0, R), :] = (dy * mixed * sc * (sg * (1.0 + g * (1.0 - sg)))).astype(BF16)
                dms = dy * (g * sg)
                dmix_s[pl.ds(r0, R), :] = (dms * sc).astype(BF16)
                return acc + _colsum8(dms * mixed)

            acc = lax.fori_loop(0, nchunk, gate_chunk, jnp.zeros((SUBLANES, LANES), F32))
            ds_ref[...] = jnp.sum(acc, axis=0, keepdims=True)
            dpool_s[pl.ds(0, T), :] = lax.dot_general(dmix_s[...], wb, NT, preferred_element_type=F32)
            dpool_s[pl.ds(T, HIST), :] = jnp.zeros((HIST, LANES), F32)
            dw_ref[...] = lax.dot_general(pooled_s[...], dmix_s[...], TN, preferred_element_type=F32).astype(BF16)

            def back_chunk(i, carry):
                r0 = pl.multiple_of(i * R, R)
                dpe = dpool_s[pl.ds(r0, R + HIST), :]
                e = dpe / _count(r0, R + HIST, w)
                du_ref[pl.ds(r0, R), :] = (_anticausal_window_sum(e, w)[0:R] - dpe[0:R]).astype(BF16)
                return carry

            lax.fori_loop(0, nchunk, back_chunk, 0)

        for k, w in enumerate(POOL_WINDOWS):
            pl.when(grp == k)(functools.partial(group, w))

    def col(off):
        return pl.BlockSpec((T, LANES), lambda j: (0, j + off))

    sec = jax.ShapeDtypeStruct((T, ngrp * LANES), BF16)
    wspec = pl.BlockSpec((None, LANES, LANES), lambda j: (j, 0, 0))
    sspec = pl.BlockSpec((1, LANES), lambda j: (0, j))
    return pl.pallas_call(
        body, name="pool_bwd", grid=(ngrp,),
        in_specs=[col(16), col(20), col(0), _pool_w_spec(l), _pool_s_spec(l)],
        out_specs=[col(0), col(0), wspec, sspec],
        out_shape=[sec, sec, jax.ShapeDtypeStruct((ngrp, LANES, LANES), BF16),
                   jax.ShapeDtypeStruct((1, ngrp * LANES), F32)],
        scratch_shapes=[pltpu.VMEM((T, LANES), BF16), pltpu.VMEM((T, LANES), F32),
                        pltpu.VMEM((T, LANES), BF16), pltpu.VMEM((T + HIST, LANES), F32)],
        compiler_params=_params(),
    )(proj, proj, dyp, wpool, pscale3)


def in_bwd(dsecs, wg, x, dxo, mod4, g_pre3, l):
    T, D = x.shape
    NB, _, CW = wg.shape
    SW = dsecs[0].shape[1]
    nsec = len(dsecs)
    tm = 256
    nt = T // tm

    def body(*refs):
        d_refs = refs[0:nsec]
        w_ref, x_ref, dxo_ref, sh_ref, sc_ref, g_ref = refs[nsec:nsec + 6]
        dxi_ref, dw_ref, dsh_ref, dsc_ref, dg_ref = refs[nsec + 6:nsec + 11]
        dp_s, acc_w, acc_sh, acc_sc, acc_g = refs[nsec + 11:]
        i = pl.program_id(0)

        @pl.when(i == 0)
        def _():
            acc_w[...] = jnp.zeros_like(acc_w)
            acc_sh[...] = jnp.zeros_like(acc_sh)
            acc_sc[...] = jnp.zeros_like(acc_sc)
            acc_g[...] = jnp.zeros_like(acc_g)

        for s in range(nsec):
            dp_s[:, s * SW:(s + 1) * SW] = d_refs[s][...]
        xv = x_ref[...]
        g = g_ref[...]
        r = _rms(xv)
        xh = xv * r
        n = xh * g
        sc1 = 1.0 + sc_ref[...]
        hb = (n * sc1 + sh_ref[...]).astype(BF16)
        dh = lax.dot_general(dp_s[:, 0:CW], w_ref[0], NT, preferred_element_type=F32)
        for j in range(1, NB):
            dh = dh + lax.dot_general(dp_s[:, j * CW:(j + 1) * CW], w_ref[j], NT, preferred_element_type=F32)
        for j in range(NB):
            acc_w[j] += lax.dot_general(hb, dp_s[:, j * CW:(j + 1) * CW], TN, preferred_element_type=F32)
        acc_sh[...] += _colsum8(dh)
        acc_sc[...] += _colsum8(dh * n)
        dnp = dh * sc1
        acc_g[...] += _colsum8(dnp * xh)
        a = dnp * g
        dxi_ref[...] = dxo_ref[...] + r * (a - xh * jnp.mean(a * xh, axis=-1, keepdims=True))

        @pl.when(i == nt - 1)
        def _():
            dw_ref[...] = acc_w[...].astype(BF16)
            dsh_ref[...] = jnp.sum(acc_sh[...], axis=0, keepdims=True)
            dsc_ref[...] = jnp.sum(acc_sc[...], axis=0, keepdims=True)
            dg_ref[...] = jnp.sum(acc_g[...], axis=0, keepdims=True)

    row = pl.BlockSpec((1, D), lambda i: (0, 0))
    tile = pl.BlockSpec((tm, D), lambda i: (i, 0))
    sect = pl.BlockSpec((tm, SW), lambda i: (i, 0))
    wspec = pl.BlockSpec((NB, D, CW), lambda i: (0, 0, 0))
    rowshape = jax.ShapeDtypeStruct((1, D), F32)
    return pl.pallas_call(
        body, name="in_bwd", grid=(nt,),
        in_specs=[sect] * nsec + [wspec, tile, tile, _mod_row(l, 0, D), _mod_row(l, 1, D), _layer_row(l, D)],
        out_specs=[tile, wspec, row, row, row],
        out_shape=[jax.ShapeDtypeStruct((T, D), F32), jax.ShapeDtypeStruct((NB, D, CW), BF16),
                   rowshape, rowshape, rowshape],
        scratch_shapes=[pltpu.VMEM((tm, nsec * SW), BF16), pltpu.VMEM((NB, D, CW), F32),
                        pltpu.VMEM((SUBLANES, D), F32), pltpu.VMEM((SUBLANES, D), F32), pltpu.VMEM((SUBLANES, D), F32)],
        compiler_params=_params(VMEM_BIG),
    )(*dsecs, wg, x, dxo, mod4, mod4, g_pre3)


def _rcopy(src, dst, ssem, rsem, dev):
    return pltpu.make_async_remote_copy(src_ref=src, dst_ref=dst, send_sem=ssem, recv_sem=rsem,
                                        device_id=dev, device_id_type=MESH)


def _peers7(x, y, c):
    out = []
    for m in range(1, N_DEV):
        bx, by, bc = (m >> 2) & 1, (m >> 1) & 1, m & 1
        out.append(((1 - x) if bx else x, (1 - y) if by else y, (1 - c) if bc else c))
    return out


def gather_weights(gis, gos, c8, wc):
    L = len(gis)
    n_w = 3 * 2 * L
    n_rem = n_w + 7 + 3

    def body(*refs):
        c_ref, wc_ref = refs[2 * L], refs[2 * L + 1]
        outs = refs[2 * L + 2:]
        go_all = outs[0:2 * L]
        call, wcall = outs[2 * L], outs[2 * L + 1]
        ssem, rsem, fss, frs, lsem = outs[2 * L + 2:]
        x, y, c = _me()
        myc = 2 * x + y
        me_lin = 4 * x + 2 * y + c
        me = (x, y, c)
        sib = (x, y, 1 - c)
        chips = [(1 - x, y), (x, 1 - y), (1 - x, 1 - y)]

        local = [pltpu.make_async_copy(c_ref, call.at[me_lin], lsem.at[0]),
                 pltpu.make_async_copy(wc_ref, wcall.at[myc], lsem.at[1])]
        for cp in local:
            cp.start()

        sends, recvs, fwds, frecvs = [], [], [], []
        k = 0
        for (px, py) in chips:
            pc = 2 * px + py
            for a in range(2 * L):
                buf = go_all[a]
                h = buf.shape[1] // 2
                rows, orows = pl.ds(c * h, h), pl.ds((1 - c) * h, h)
                sends.append(_rcopy(buf.at[myc, rows], buf.at[myc, rows], ssem.at[k], rsem.at[k], (px, py, c)))
                landed = buf.at[pc, rows]
                recvs.append(_rcopy(landed, landed, ssem.at[k], rsem.at[k], me))
                fwds.append(_rcopy(landed, landed, fss.at[k], frs.at[k], sib))
                other = buf.at[pc, orows]
                frecvs.append(_rcopy(other, other, fss.at[k], frs.at[k], me))
                k += 1
        for m, peer in enumerate(_peers7(x, y, c)):
            plin = 4 * peer[0] + 2 * peer[1] + peer[2]
            sends.append(_rcopy(c_ref, call.at[me_lin], ssem.at[n_w + m], rsem.at[n_w + m], peer))
            recvs.append(_rcopy(call.at[plin], call.at[plin], ssem.at[n_w + m], rsem.at[n_w + m], me))
        for j, (px, py) in enumerate(chips):
            pc = 2 * px + py
            sends.append(_rcopy(wc_ref, wcall.at[myc], ssem.at[n_w + 7 + j], rsem.at[n_w + 7 + j], (px, py, c)))
            recvs.append(_rcopy(wcall.at[pc], wcall.at[pc], ssem.at[n_w + 7 + j], rsem.at[n_w + 7 + j], me))

        for cp in sends:
            cp.start()
        for k in range(n_w):
            recvs[k].wait_recv()
            fwds[k].start()
        for k in range(n_w, n_rem):
            recvs[k].wait_recv()
        for cp in frecvs:
            cp.wait_recv()
        for cp in sends + fwds:
            cp.wait_send()
        for cp in local:
            cp.wait()

    bufs = [b for l in range(L) for b in (gis[l], gos[l])]
    out_shape = ([jax.ShapeDtypeStruct(b.shape, b.dtype) for b in bufs]
                 + [jax.ShapeDtypeStruct((N_DEV, SUBLANES, LANES), F32),
                    jax.ShapeDtypeStruct((N_CHIPS, wc.shape[0], 3, LANES), F32)])
    outs = pl.pallas_call(
        body, name="gather_weights",
        in_specs=[ANY] * (2 * L + 2), out_specs=[ANY] * (2 * L + 2), out_shape=out_shape,
        input_output_aliases={a: a for a in range(2 * L)},
        scratch_shapes=[pltpu.SemaphoreType.DMA((n_rem,)), pltpu.SemaphoreType.DMA((n_rem,)),
                        pltpu.SemaphoreType.DMA((n_w,)), pltpu.SemaphoreType.DMA((n_w,)),
                        pltpu.SemaphoreType.DMA((2,))],
        compiler_params=_params(n_grid=0),
    )(*bufs, c8, wc)
    return outs[0:2 * L:2], outs[1:2 * L:2], outs[2 * L], outs[2 * L + 1]


HBM = pl.BlockSpec(memory_space=pltpu.HBM)
SEM = pl.BlockSpec(memory_space=pltpu.SEMAPHORE)
N_XCHG = 6


def _hbm(a):
    return pltpu.with_memory_space_constraint(a, pltpu.HBM)


def gather_start(bufs):
    n = len(bufs)
    L = n // 2

    def body(*refs):
        ins = refs[0:n]
        ssems, rsems = refs[n:n + L], refs[n + L:n + 2 * L]
        token = refs[-1]
        x, y, c = _me()
        myc = 2 * x + y
        for l in range(L):
            k = 0
            for (px, py) in [(1 - x, y), (x, 1 - y), (1 - x, 1 - y)]:
                for buf in (ins[2 * l], ins[2 * l + 1]):
                    h = buf.shape[1] // 2
                    own = buf.at[myc, pl.ds(c * h, h)]
                    _rcopy(own, own, ssems[l].at[k], rsems[l].at[k], (px, py, c)).start()
                    k += 1
        token[...] = jnp.zeros_like(token)

    sems = [pltpu.SemaphoreType.DMA((N_XCHG,))] * (2 * L)
    outs = pl.pallas_call(
        body, name="gather_start",
        in_specs=[HBM] * n,
        out_specs=[SEM] * (2 * L) + [HBM] * n + [pl.BlockSpec(memory_space=pltpu.VMEM)],
        out_shape=sems + [pltpu.HBM(b.shape, b.dtype) for b in bufs] + [jax.ShapeDtypeStruct((SUBLANES, LANES), F32)],
        input_output_aliases={a: 2 * L + a for a in range(n)},
        compiler_params=pltpu.CompilerParams(has_side_effects=pltpu.SideEffectType.DATAFLOW_SIDE_EFFECTING),
    )(*[_hbm(b) for b in bufs])
    return outs[0:L], outs[L:2 * L], outs[2 * L:2 * L + n], outs[-1]


def gather_wait(gi, go, ssem, rsem, after):
    def body(gi_ref, go_ref, ssem_ref, rsem_ref, after_ref, gi_out, go_out):
        x, y, c = _me()
        myc = 2 * x + y
        k = 0
        for (px, py) in [(1 - x, y), (x, 1 - y), (1 - x, 1 - y)]:
            for buf in (gi_ref, go_ref):
                h = buf.shape[1] // 2
                rows = pl.ds(c * h, h)
                cp = _rcopy(buf.at[myc, rows], buf.at[2 * px + py, rows], ssem_ref.at[k], rsem_ref.at[k], (px, py, c))
                cp.wait_send()
                cp.wait_recv()
                k += 1

    return pl.pallas_call(
        body, name="gather_wait",
        in_specs=[HBM, HBM, SEM, SEM, ANY], out_specs=[HBM, HBM],
        out_shape=[pltpu.HBM(gi.shape, gi.dtype), pltpu.HBM(go.shape, go.dtype)],
        input_output_aliases={0: 0, 1: 1},
        compiler_params=pltpu.CompilerParams(has_side_effects=pltpu.SideEffectType.DATAFLOW_SIDE_EFFECTING),
    )(gi, go, ssem, rsem, after)


def forward_sibling(gi, go):
    def body(gi_in, go_in, gi_ref, go_ref, ssem, rsem):
        x, y, c = _me()
        me, sib = (x, y, c), (x, y, 1 - c)
        sends, recvs = [], []
        k = 0
        for (px, py) in [(1 - x, y), (x, 1 - y), (1 - x, 1 - y)]:
            pc = 2 * px + py
            for buf in (gi_ref, go_ref):
                h = buf.shape[1] // 2
                landed = buf.at[pc, pl.ds(c * h, h)]
                other = buf.at[pc, pl.ds((1 - c) * h, h)]
                sends.append(_rcopy(landed, landed, ssem.at[k], rsem.at[k], sib))
                recvs.append(_rcopy(other, other, ssem.at[k], rsem.at[k], me))
                k += 1
        for cp in sends:
            cp.start()
        for cp in recvs:
            cp.wait_recv()
        for cp in sends:
            cp.wait_send()

    return pl.pallas_call(
        body, name="forward_sibling",
        in_specs=[ANY, ANY], out_specs=[ANY, ANY],
        out_shape=[jax.ShapeDtypeStruct(gi.shape, gi.dtype), jax.ShapeDtypeStruct(go.shape, go.dtype)],
        input_output_aliases={0: 0, 1: 1},
        scratch_shapes=[pltpu.SemaphoreType.DMA((N_XCHG,)), pltpu.SemaphoreType.DMA((N_XCHG,))],
        compiler_params=_params(n_grid=0),
    )(gi, go)


def gather_small(c8, wc, token):
    def body(c_ref, wc_ref, token_ref, call, wcall, ssem, rsem, lsem):
        x, y, c = _me()
        myc = 2 * x + y
        me_lin = 4 * x + 2 * y + c
        me = (x, y, c)
        local = [pltpu.make_async_copy(c_ref, call.at[me_lin], lsem.at[0]),
                 pltpu.make_async_copy(wc_ref, wcall.at[myc], lsem.at[1])]
        for cp in local:
            cp.start()
        sends, recvs = [], []
        for m, peer in enumerate(_peers7(x, y, c)):
            plin = 4 * peer[0] + 2 * peer[1] + peer[2]
            sends.append(_rcopy(c_ref, call.at[me_lin], ssem.at[m], rsem.at[m], peer))
            recvs.append(_rcopy(call.at[plin], call.at[plin], ssem.at[m], rsem.at[m], me))
        for j, (px, py) in enumerate([(1 - x, y), (x, 1 - y), (1 - x, 1 - y)]):
            pc = 2 * px + py
            sends.append(_rcopy(wc_ref, wcall.at[myc], ssem.at[7 + j], rsem.at[7 + j], (px, py, c)))
            recvs.append(_rcopy(wcall.at[pc], wcall.at[pc], ssem.at[7 + j], rsem.at[7 + j], me))
        for cp in sends:
            cp.start()
        for cp in recvs:
            cp.wait_recv()
        for cp in sends:
            cp.wait_send()
        for cp in local:
            cp.wait()

    return pl.pallas_call(
        body, name="gather_small",
        in_specs=[ANY] * 3, out_specs=[ANY] * 2,
        out_shape=[jax.ShapeDtypeStruct((N_DEV, SUBLANES, LANES), F32),
                   jax.ShapeDtypeStruct((N_CHIPS, wc.shape[0], 3, LANES), F32)],
        scratch_shapes=[pltpu.SemaphoreType.DMA((10,)), pltpu.SemaphoreType.DMA((10,)), pltpu.SemaphoreType.DMA((2,))],
        compiler_params=_params(n_grid=0),
    )(c8, wc, token)


def exchange_mod(part):
    def body(p_ref, o_ref, ssem, rsem, lsem):
        x, y, c = _me()
        myc = 2 * x + y
        chips = [(1 - x, y), (x, 1 - y), (1 - x, 1 - y)]
        loc = pltpu.make_async_copy(p_ref, o_ref.at[myc], lsem)
        loc.start()
        sends = [_rcopy(p_ref, o_ref.at[myc], ssem.at[j], rsem.at[j], (px, py, c)) for j, (px, py) in enumerate(chips)]
        for cp in sends:
            cp.start()
        for j, (px, py) in enumerate(chips):
            slot = o_ref.at[2 * px + py]
            _rcopy(slot, slot, ssem.at[j], rsem.at[j], (x, y, c)).wait_recv()
        for cp in sends:
            cp.wait_send()
        loc.wait()

    return pl.pallas_call(
        body, name="exchange_mod", in_specs=[ANY], out_specs=ANY,
        out_shape=jax.ShapeDtypeStruct((N_CHIPS,) + part.shape, part.dtype),
        scratch_shapes=[pltpu.SemaphoreType.DMA((3,)), pltpu.SemaphoreType.DMA((3,)), pltpu.SemaphoreType.DMA],
        compiler_params=_params(n_grid=0),
    )(part)


def sibling_exchange(dwi, dwo, dwp, pack):
    L = len(dwi)
    groups = (dwi, dwo, dwp)
    n_big = 3 * L

    def body(*refs):
        ins = [refs[g * L:(g + 1) * L] for g in range(3)]
        pack_ref = refs[3 * L]
        outs = [refs[3 * L + 1 + g * L:3 * L + 1 + (g + 1) * L] for g in range(3)]
        packs = refs[6 * L + 1]
        ssem, rsem, lsem = refs[6 * L + 2:]
        x, y, c = _me()
        me = (x, y, c)
        me_lin = 4 * x + 2 * y + c
        sib = (x, y, 1 - c)
        loc = pltpu.make_async_copy(pack_ref, packs.at[me_lin], lsem)
        loc.start()
        sends, recvs = [], []
        k = 0
        for g in range(3):
            for l in range(L):
                half = ins[g][l].shape[1] // 2
                sends.append(_rcopy(ins[g][l].at[:, pl.ds((1 - c) * half, half)], outs[g][l],
                                    ssem.at[k], rsem.at[k], sib))
                recvs.append(_rcopy(outs[g][l], outs[g][l], ssem.at[k], rsem.at[k], me))
                k += 1
        for m, peer in enumerate(_peers7(x, y, c)):
            plin = 4 * peer[0] + 2 * peer[1] + peer[2]
            sends.append(_rcopy(pack_ref, packs.at[me_lin], ssem.at[n_big + m], rsem.at[n_big + m], peer))
            recvs.append(_rcopy(packs.at[plin], packs.at[plin], ssem.at[n_big + m], rsem.at[n_big + m], me))
        for cp in sends:
            cp.start()
        for cp in recvs:
            cp.wait_recv()
        for cp in sends:
            cp.wait_send()
        loc.wait()

    def halves(arrs):
        return [jax.ShapeDtypeStruct((a.shape[0], a.shape[1] // 2) + a.shape[2:], a.dtype) for a in arrs]

    out_shape = halves(dwi) + halves(dwo) + halves(dwp) + [jax.ShapeDtypeStruct((N_DEV,) + pack.shape, pack.dtype)]
    outs = pl.pallas_call(
        body, name="sibling_exchange",
        in_specs=[ANY] * (3 * L + 1), out_specs=[ANY] * (3 * L + 1), out_shape=out_shape,
        scratch_shapes=[pltpu.SemaphoreType.DMA((n_big + 7,)), pltpu.SemaphoreType.DMA((n_big + 7,)),
                        pltpu.SemaphoreType.DMA],
        compiler_params=_params(n_grid=0),
    )(*dwi, *dwo, *dwp, pack)
    return outs[0:L], outs[L:2 * L], outs[2 * L:3 * L], outs[3 * L]


def chip_exchange(chi, cho, chp):
    L = len(chi)
    n = 3 * L * 3

    def body(*refs):
        ins = refs[0:3 * L]
        outs = refs[3 * L:6 * L]
        ssem, rsem = refs[6 * L:]
        x, y, c = _me()
        chips = [(1 - x, y), (x, 1 - y), (1 - x, 1 - y)]
        sends, recvs = [], []
        k = 0
        for j, (px, py) in enumerate(chips):
            pc = 2 * px + py
            for a in range(3 * L):
                sends.append(_rcopy(ins[a].at[pc], outs[a].at[j], ssem.at[k], rsem.at[k], (px, py, c)))
                recvs.append(_rcopy(outs[a].at[j], outs[a].at[j], ssem.at[k], rsem.at[k], (x, y, c)))
                k += 1
        for cp in sends:
            cp.start()
        for cp in recvs:
            cp.wait_recv()
        for cp in sends:
            cp.wait_send()

    arrs = [a for l in range(L) for a in (chi[l], cho[l], chp[l])]
    outs = pl.pallas_call(
        body, name="chip_exchange",
        in_specs=[ANY] * (3 * L), out_specs=[ANY] * (3 * L),
        out_shape=[jax.ShapeDtypeStruct((3,) + a.shape[1:], a.dtype) for a in arrs],
        scratch_shapes=[pltpu.SemaphoreType.DMA((n,)), pltpu.SemaphoreType.DMA((n,))],
        compiler_params=_params(n_grid=0),
    )(*arrs)
    return outs[0::3], outs[1::3], outs[2::3]


def spread_reduced(gi, go, gp):
    L, D, _ = gi.shape
    RO = go.shape[1]
    HP = LANES // 2
    n = 2 * L + 7 * L

    def body(gi_in, go_in, gp_in, gi, go, gp, ssem, rsem):
        x, y, c = _me()
        me = (x, y, c)
        myc = 2 * x + y
        sib = (x, y, 1 - c)
        peers = _peers7(x, y, c)
        hi, ho, hp = pl.ds(c * (D // 2), D // 2), pl.ds(c * (RO // 2), RO // 2), pl.ds(c * HP, HP)
        ohi, oho = pl.ds((1 - c) * (D // 2), D // 2), pl.ds((1 - c) * (RO // 2), RO // 2)
        sends, recvs = [], []
        k = 0
        for l in range(L):
            sends.append(_rcopy(gi.at[l, hi], gi.at[l, hi], ssem.at[k], rsem.at[k], sib))
            recvs.append(_rcopy(gi.at[l, ohi], gi.at[l, ohi], ssem.at[k], rsem.at[k], me))
            k += 1
            sends.append(_rcopy(go.at[l, ho], go.at[l, ho], ssem.at[k], rsem.at[k], sib))
            recvs.append(_rcopy(go.at[l, oho], go.at[l, oho], ssem.at[k], rsem.at[k], me))
            k += 1
            for peer in peers:
                pchip = 2 * peer[0] + peer[1]
                prow = pl.ds(peer[2] * HP, HP)
                sends.append(_rcopy(gp.at[l, myc, hp], gp.at[l, myc, hp], ssem.at[k], rsem.at[k], peer))
                recvs.append(_rcopy(gp.at[l, pchip, prow], gp.at[l, pchip, prow], ssem.at[k], rsem.at[k], me))
                k += 1
        for cp in sends:
            cp.start()
        for cp in recvs:
            cp.wait_recv()
        for cp in sends:
            cp.wait_send()

    return pl.pallas_call(
        body, name="spread_reduced",
        in_specs=[ANY] * 3, out_specs=[ANY] * 3,
        out_shape=[jax.ShapeDtypeStruct(a.shape, a.dtype) for a in (gi, go, gp)],
        input_output_aliases={0: 0, 1: 1, 2: 2},
        scratch_shapes=[pltpu.SemaphoreType.DMA((n,)), pltpu.SemaphoreType.DMA((n,))],
        compiler_params=_params(n_grid=0),
    )(gi, go, gp)


def add_sibling(cidx, mine, sib):
    def body(c_ref, *refs):
        for a in range(3):
            m, s, o = refs[a], refs[3 + a], refs[6 + a]
            o[...] = (m[...].astype(F32) + s[...].astype(F32)).astype(BF16)

    def mine_spec(a):
        h = a.shape[1] // 2
        return pl.BlockSpec((None, h, a.shape[2]), lambda j, c_ref: (j, c_ref[0], 0))

    def sib_spec(a):
        return pl.BlockSpec((None,) + a.shape[1:], lambda j, c_ref: (j, 0, 0))

    return pl.pallas_call(
        body, name="add_sibling",
        grid_spec=pltpu.PrefetchScalarGridSpec(
            num_scalar_prefetch=1, grid=(N_CHIPS,),
            in_specs=[mine_spec(a) for a in mine] + [sib_spec(a) for a in sib],
            out_specs=[sib_spec(a) for a in sib]),
        out_shape=[jax.ShapeDtypeStruct(a.shape, BF16) for a in sib],
        compiler_params=_params(VMEM_BIG),
    )(cidx, *mine, *sib)


def sum_chips(pos, own, rb, acc, l, shapes):
    nq = 4
    n_in = 6 + (3 if acc is not None else 0)

    def body(pos_ref, *refs):
        for a in range(3):
            m, b, o = refs[a], refs[3 + a], refs[n_in + a]
            s = m[...].astype(F32)
            for j in range(3):
                s = s + b[j].astype(F32)
            o[...] = s

    def own_spec(a):
        return pl.BlockSpec((None, a.shape[1] // nq, a.shape[2]), lambda q, p: (p[1], q, 0))

    def rb_spec(a):
        return pl.BlockSpec((3, a.shape[1] // nq, a.shape[2]), lambda q, p: (0, q, 0))

    hi, ho, hp = own[0].shape[1] // nq, own[1].shape[1] // nq, own[2].shape[1] // nq
    out_specs = [pl.BlockSpec((None, hi, shapes[0][2]), lambda q, p: (l, p[0] * nq + q, 0)),
                 pl.BlockSpec((None, ho, shapes[1][2]), lambda q, p: (l, p[0] * nq + q, 0)),
                 pl.BlockSpec((None, None, hp, LANES), lambda q, p: (l, p[1], p[0] * nq + q, 0))]
    in_specs = [own_spec(a) for a in own] + [rb_spec(a) for a in rb]
    args = list(own) + list(rb)
    aliases = {}
    if acc is not None:
        in_specs += [ANY] * 3
        args += list(acc)
        aliases = {7: 0, 8: 1, 9: 2}
    return pl.pallas_call(
        body, name="sum_chips",
        grid_spec=pltpu.PrefetchScalarGridSpec(num_scalar_prefetch=1, grid=(nq,), in_specs=in_specs, out_specs=out_specs),
        out_shape=[jax.ShapeDtypeStruct(s, F32) for s in shapes],
        input_output_aliases=aliases,
        compiler_params=_params(VMEM_BIG),
    )(pos, *args)


def pack_small(per_layer, loss_blk):
    L = len(per_layer)
    D = per_layer[0][0].shape[1]

    def body(*refs):
        o = refs[-1]
        lb = refs[-2]
        o[...] = jnp.zeros_like(o)
        for l in range(L):
            dgpre, dgpost, dsh, dsc, dgt, dps, dwc = refs[7 * l:7 * l + 7]
            base = SUBLANES * l
            for r, src in enumerate((dgpre, dgpost, dsh, dsc, dgt)):
                o[pl.ds(base + r, 1), :] = src[...]
            o[pl.ds(base + 5, 1), 0:dps.shape[1]] = dps[...]
            for j in range(dwc.shape[0]):
                for k in range(3):
                    idx = 3 * j + k
                    o[pl.ds(base + 6 + idx // 8, 1), (idx % 8) * LANES:(idx % 8 + 1) * LANES] = dwc[j, pl.ds(k, 1), :]
        o[pl.ds(5, 1), 4 * LANES:5 * LANES] = lb[pl.ds(0, 1), :]

    flat = [a for layer in per_layer for a in layer]
    return pl.pallas_call(
        body, name="pack_small",
        out_shape=jax.ShapeDtypeStruct((L * SUBLANES, D), F32),
        compiler_params=_params(n_grid=0),
    )(*flat, loss_blk)


def sum_packs(packs):
    def body(p_ref, o_ref):
        s = p_ref[0]
        for d in range(1, N_DEV):
            s = s + p_ref[d]
        o_ref[...] = s

    return pl.pallas_call(
        body, name="sum_packs",
        out_shape=jax.ShapeDtypeStruct(packs.shape[1:], F32),
        compiler_params=_params(n_grid=0),
    )(packs)


def _adamw_math(w, g, m, v):
    m = ADAM_B1 * m + (1.0 - ADAM_B1) * g
    v = ADAM_B2 * v + (1.0 - ADAM_B2) * (g * g)
    m_hat = m / (1.0 - ADAM_B1 ** ADAM_STEP)
    v_hat = v / (1.0 - ADAM_B2 ** ADAM_STEP)
    delta = -ADAM_LR * (m_hat / (jnp.sqrt(v_hat) + ADAM_EPS) + ADAM_WD * w)
    return delta, m, v


def adamw(w, g, m, v, block, name):
    grid = tuple(s // b for s, b in zip(w.shape, block))

    def body(w_ref, g_ref, m_ref, v_ref, d_ref, mo_ref, vo_ref):
        d, mm, vv = _adamw_math(w_ref[...], g_ref[...], m_ref[...], v_ref[...])
        d_ref[...] = d
        mo_ref[...] = mm
        vo_ref[...] = vv

    spec = pl.BlockSpec(block, lambda *idx: idx)
    shape = jax.ShapeDtypeStruct(w.shape, F32)
    return pl.pallas_call(
        body, name=name, grid=grid,
        in_specs=[spec] * 4, out_specs=[spec] * 3, out_shape=[shape] * 3,
        compiler_params=_params(VMEM_BIG, n_grid=len(grid)),
    )(w, g, m, v)


def ada_finish(c_all, dmod, w, m, v):
    L, D, CW = w.shape
    hD = D // 2

    def body(c_ref, d_ref, w_ref, m_ref, v_ref, g_ref, dl_ref, mo_ref, vo_ref):
        cv = c_ref[...]
        z = jnp.zeros_like(cv)
        ca = jnp.concatenate([cv * jax.nn.sigmoid(cv), z], axis=0).astype(BF16)
        dm = jnp.concatenate([d_ref[0], jnp.zeros_like(d_ref[0])], axis=0).astype(BF16)
        g = lax.dot_general(ca, dm, TN, preferred_element_type=F32)
        g_ref[0] = g
        d, mm, vv = _adamw_math(w_ref[0], g, m_ref[0], v_ref[0])
        dl_ref[0] = d
        mo_ref[0] = mm
        vo_ref[0] = vv

    big = pl.BlockSpec((1, hD, CW), lambda l, h: (l, h, 0))
    shape = jax.ShapeDtypeStruct(w.shape, F32)
    return pl.pallas_call(
        body, name="ada_finish", grid=(L, 2),
        in_specs=[pl.BlockSpec((N_DEV, hD), lambda l, h: (0, h)), pl.BlockSpec((1, N_DEV, CW), lambda l, h: (l, 0, 0)),
                  big, big, big],
        out_specs=[big] * 4, out_shape=[shape] * 4,
        compiler_params=_params(VMEM_BIG, n_grid=2),
    )(c_all, dmod, w, m, v)


def kernel(x, c, w_ada, b_ada, g_pre, w_in, w_conv, w_pool, pool_scale, w_out, g_post, loss_target, m_w_ada, m_b_ada, m_g_pre, m_w_in, m_w_conv, m_w_pool, m_pool_scale, m_w_out, m_g_post, v_w_ada, v_b_ada, v_g_pre, v_w_in, v_w_conv, v_w_pool, v_pool_scale, v_w_out, v_g_post):
    L, D, CW = w_in.shape
    RO = w_out.shape[1]
    T = x.shape[1]
    ix, iy, ic = _me()
    chip = 2 * ix + iy
    me_lin = 4 * ix + 2 * iy + ic

    pos = jnp.stack([ic, chip]).astype(jnp.int32)
    g_pre3, g_post3 = g_pre.reshape(L, 1, D), g_post.reshape(L, 1, D)
    pscale3 = pool_scale.reshape(L, 1, pool_scale.shape[1])

    own = [b for l in range(L) for b in cast_weights(pos, w_in, w_out, l)]
    g_ssems, g_rsems, g_bufs, token = gather_start(own)
    c_all3, wconv_all = gather_small(c.reshape(SUBLANES, LANES), w_conv, token)
    c_all = c_all3.reshape(N_DEV, D)
    b_my = lax.dynamic_slice_in_dim(b_ada, chip * CW, CW, axis=1)
    mod_all = exchange_mod(mod_part(c_all, w_ada, b_my))
    mod = lax.dynamic_index_in_dim(mod_all, me_lin, axis=2, keepdims=False)
    mod4 = jnp.transpose(mod, (1, 0, 2)).reshape(L, 3, 1, D)

    xs, projs, yas, yps, ys = [x.reshape(T, D)], [], [], [], []
    wg_in, wg_out = [], []
    for l in range(L):
        gi, go = gather_wait(g_bufs[2 * l], g_bufs[2 * l + 1], g_ssems[l], g_rsems[l], mod4 if l == 0 else xs[l])
        gi, go = forward_sibling(gi, go)
        wg_in.append(gi)
        wg_out.append(go.reshape(N_CHIPS * RO, D))
        proj = proj_fwd(xs[l], mod4, g_pre3, wg_in[l], l)
        ya = conv_fwd(proj, wconv_all, l)
        yp = pool_fwd(proj, w_pool, pscale3, l)
        xn, yv = out_fwd(ya, yp, wg_out[l], xs[l], mod4, g_post3, l)
        xs.append(xn)
        projs.append(proj)
        yas.append(ya)
        yps.append(yp)
        ys.append(yv)

    dx, loss_blk = loss_head(xs[L], loss_target.reshape(T, D))

    dwi, dwo, dwp, smalls = [None] * L, [None] * L, [None] * L, [None] * L
    for l in reversed(range(L)):
        dya, dyp, dwo_l, dgate, dgpost = out_bwd(dx, ys[l], yas[l], yps[l], wg_out[l], mod4, g_post3, l)
        du_a, db_a, dc_a, dg_a, dwc = conv_bwd(projs[l], dya, wconv_all, l)
        du_p, dg_p, dwp_l, dps = pool_bwd(projs[l], dyp, w_pool, pscale3, l)
        dx, dwi_l, dshift, dscale, dgpre = in_bwd([du_a, db_a, dc_a, dg_a, du_p, dg_p], wg_in[l], xs[l], dx,
                                                  mod4, g_pre3, l)
        dwi[l], dwo[l], dwp[l] = dwi_l, dwo_l.reshape(N_CHIPS, RO, D), dwp_l
        smalls[l] = (dgpre, dgpost, dshift, dscale, dgate, dps, dwc)
    grad_x = dx.reshape(1, T, D)
    pack = pack_small(smalls, loss_blk)

    sb_i, sb_o, sb_p, packs_all = sibling_exchange(dwi, dwo, dwp, pack)
    ch_i, ch_o, ch_p = [], [], []
    for l in range(L):
        a, b, p = add_sibling(pos, (dwi[l], dwo[l], dwp[l]), (sb_i[l], sb_o[l], sb_p[l]))
        ch_i.append(a)
        ch_o.append(b)
        ch_p.append(p)
    rb_i, rb_o, rb_p = chip_exchange(ch_i, ch_o, ch_p)
    acc = None
    for l in range(L):
        acc = sum_chips(pos, (ch_i[l], ch_o[l], ch_p[l]), (rb_i[l], rb_o[l], rb_p[l]), acc, l,
                        (w_in.shape, w_out.shape, w_pool.shape))
    g_w_in, g_w_out, g_w_pool = spread_reduced(*acc)

    small = sum_packs(packs_all).reshape(L, SUBLANES, D)
    loss = small[0, 5, 4 * LANES]
    g_g_pre = small[:, 0]
    g_g_post = small[:, 1]
    g_b_ada = small[:, 2:5].reshape(L, 3 * D)
    g_pscale = small[:, 5, 0:pool_scale.shape[1]]
    g_w_conv = small[:, 6:8].reshape(L, 2 * D)[:, 0:N_CHIPS * 3 * LANES].reshape(L, N_CHIPS, 3, LANES)
    g_w_conv = lax.dynamic_index_in_dim(g_w_conv, chip, axis=1, keepdims=False)
    dmod_all = packs_all.reshape(N_DEV, L, SUBLANES, D)[:, :, 2:5].reshape(N_DEV, L, 3 * D)
    dmod_my = jnp.transpose(lax.dynamic_slice_in_dim(dmod_all, chip * CW, CW, axis=2), (1, 0, 2))

    g_w_ada, d_w_ada, nm_w_ada, nv_w_ada = ada_finish(c_all, dmod_my, w_ada, m_w_ada, v_w_ada)
    d_w_in, nm_w_in, nv_w_in = adamw(w_in, g_w_in, m_w_in, v_w_in, (1, D // 2, CW), "adamw_w_in")
    d_w_out, nm_w_out, nv_w_out = adamw(w_out, g_w_out, m_w_out, v_w_out, (1, RO, D), "adamw_w_out")
    pshape = (L, N_CHIPS * LANES, LANES)
    d_w_pool, nm_w_pool, nv_w_pool = adamw(w_pool.reshape(pshape), g_w_pool.reshape(pshape), m_w_pool.reshape(pshape),
                                           v_w_pool.reshape(pshape), (1,) + pshape[1:], "adamw_w_pool")
    d_w_pool, nm_w_pool, nv_w_pool = [a.reshape(w_pool.shape) for a in (d_w_pool, nm_w_pool, nv_w_pool)]

    def small_adamw(w, g, m, v, name):
        shp = (1,) + w.shape if w.ndim == 2 else w.shape
        outs = adamw(w.reshape(shp), g.reshape(shp), m.reshape(shp), v.reshape(shp), shp, name)
        return [a.reshape(w.shape) for a in outs]

    d_b_ada, nm_b_ada, nv_b_ada = small_adamw(b_ada, g_b_ada, m_b_ada, v_b_ada, "adamw_b_ada")
    d_g_pre, nm_g_pre, nv_g_pre = small_adamw(g_pre, g_g_pre, m_g_pre, v_g_pre, "adamw_g_pre")
    d_w_conv, nm_w_conv, nv_w_conv = small_adamw(w_conv, g_w_conv, m_w_conv, v_w_conv, "adamw_w_conv")
    d_pscale, nm_pscale, nv_pscale = small_adamw(pool_scale, g_pscale, m_pool_scale, v_pool_scale, "adamw_pool_scale")
    d_g_post, nm_g_post, nv_g_post = small_adamw(g_post, g_g_post, m_g_post, v_g_post, "adamw_g_post")

    return (loss, grad_x,
            g_w_ada, g_b_ada, g_g_pre, g_w_in, g_w_conv, g_w_pool, g_pscale, g_w_out, g_g_post,
            d_w_ada, d_b_ada, d_g_pre, d_w_in, d_w_conv, d_w_pool, d_pscale, d_w_out, d_g_post,
            nm_w_ada, nm_b_ada, nm_g_pre, nm_w_in, nm_w_conv, nm_w_pool, nm_pscale, nm_w_out, nm_g_post,
            nv_w_ada, nv_b_ada, nv_g_pre, nv_w_in, nv_w_conv, nv_w_pool, nv_pscale, nv_w_out, nv_g_post)
```

```python
import functools

import jax
import jax.numpy as jnp
from jax import lax
from jax.experimental import pallas as pl
from jax.experimental.pallas import tpu as pltpu

F32 = jnp.float32
BF16 = jnp.bfloat16
MESH = pl.DeviceIdType.MESH
ANY = pl.BlockSpec(memory_space=pl.ANY)

NORM_EPS = 1e-6
POOL_WINDOWS = (2, 4, 8, 16)
ADAM_LR = 0.001
ADAM_B1 = 0.9
ADAM_B2 = 0.999
ADAM_EPS = 1e-08
ADAM_WD = 0.01
ADAM_STEP = 10

N_CHIPS = 4
N_DEV = 8
LANES = 128
SUBLANES = 8
VMEM_BIG = 56 * 1024 * 1024
HIST = 16
R_CONV = 32
R_POOL = 64

NT = (((1,), (1,)), ((), ()))
TN = (((0,), (0,)), ((), ()))


def _params(vmem=None, n_grid=1):
    kw = {}
    if n_grid:
        kw["dimension_semantics"] = ("arbitrary",) * n_grid
    if vmem is not None:
        kw["vmem_limit_bytes"] = vmem
    return pltpu.CompilerParams(**kw)


def _colsum8(v):
    n, d = v.shape
    return v.reshape(n // SUBLANES, SUBLANES, d).sum(axis=0)


def _rms(v):
    return lax.rsqrt(jnp.mean(v * v, axis=-1, keepdims=True) + NORM_EPS)


def _shift_down(ext, k, rows):
    if k == 0:
        return ext[HIST:HIST + rows]
    return pltpu.roll(ext, k, 0)[HIST:HIST + rows]


def _shift_up(ext, k, rows):
    if k == 0:
        return ext[0:rows]
    return pltpu.roll(ext, ext.shape[0] - k, 0)[0:rows]


def _load_ext(ref, r0, h0, first, rows):
    hist = ref[pl.ds(h0, HIST), :].astype(F32)
    hist = jnp.where(first, 0.0, hist)
    cur = ref[pl.ds(r0, rows), :].astype(F32)
    return jnp.concatenate([hist, cur], axis=0)


def _me():
    return lax.axis_index("x"), lax.axis_index("y"), lax.axis_index("c")


def cast_weights(pos, w_in, w_out, l, after):
    _, D, CW = w_in.shape
    RO = w_out.shape[1]

    def body(pos_ref, wi, wo, after_ref, oi, oo):
        oi[...] = wi[...].astype(BF16)
        oo[...] = wo[...].astype(BF16)

    return pl.pallas_call(
        body, name="cast_w",
        grid_spec=pltpu.PrefetchScalarGridSpec(
            num_scalar_prefetch=1, grid=(2,),
            in_specs=[pl.BlockSpec((None, D // 2, CW), lambda h, p: (l, h, 0)),
                      pl.BlockSpec((None, RO // 2, D), lambda h, p: (l, h, 0)), ANY],
            out_specs=[pl.BlockSpec((None, D // 2, CW), lambda h, p: (p[1], h, 0)),
                       pl.BlockSpec((None, RO // 2, D), lambda h, p: (p[1], h, 0))]),
        out_shape=[jax.ShapeDtypeStruct((N_CHIPS, D, CW), BF16), jax.ShapeDtypeStruct((N_CHIPS, RO, D), BF16)],
        compiler_params=_params(),
    )(pos, w_in, w_out, after)


def mod_part(c_all, w_ada, b_my, after):
    L, D, CW = w_ada.shape

    def body(c_ref, w_ref, b_ref, after_ref, o_ref):
        cv = c_ref[...]
        ca = (cv * jax.nn.sigmoid(cv)).astype(BF16)
        o_ref[0] = jnp.dot(ca, w_ref[0].astype(BF16), preferred_element_type=F32) + b_ref[0]

    return pl.pallas_call(
        body, name="mod_part", grid=(L,),
        in_specs=[pl.BlockSpec((N_DEV, D), lambda l: (0, 0)),
                  pl.BlockSpec((1, D, CW), lambda l: (l, 0, 0)),
                  pl.BlockSpec((1, 1, CW), lambda l: (l, 0, 0)), ANY],
        out_specs=pl.BlockSpec((1, N_DEV, CW), lambda l: (l, 0, 0)),
        out_shape=jax.ShapeDtypeStruct((L, N_DEV, CW), F32),
        compiler_params=_params(VMEM_BIG),
    )(c_all, w_ada, b_my.reshape(L, 1, CW), after)


def _mod_row(l, k, D):
    return pl.BlockSpec((None, None, 1, D), lambda *_: (l, k, 0, 0))


def _layer_row(l, D):
    return pl.BlockSpec((None, 1, D), lambda *_: (l, 0, 0))


def proj_fwd(x, mod4, g_pre3, wg, l):
    T, D = x.shape
    NB, _, CW = wg.shape
    tm = 512

    def body(x_ref, sh_ref, sc_ref, g_ref, w_ref, o_ref):
        xv = x_ref[...]
        h = (xv * _rms(xv) * g_ref[...]) * (1.0 + sc_ref[...]) + sh_ref[...]
        hb = h.astype(BF16)
        for j in range(NB):
            o_ref[:, j * CW:(j + 1) * CW] = jnp.dot(hb, w_ref[j], preferred_element_type=F32).astype(BF16)

    return pl.pallas_call(
        body, name="proj_fwd", grid=(T // tm,),
        in_specs=[pl.BlockSpec((tm, D), lambda i: (i, 0)), _mod_row(l, 0, D), _mod_row(l, 1, D), _layer_row(l, D),
                  pl.BlockSpec((NB, D, CW), lambda i: (0, 0, 0))],
        out_specs=pl.BlockSpec((tm, NB * CW), lambda i: (i, 0)),
        out_shape=jax.ShapeDtypeStruct((T, NB * CW), BF16),
        compiler_params=_params(VMEM_BIG),
    )(x, mod4, mod4, g_pre3, wg)


def conv_fwd(proj, wconv, l):
    T = proj.shape[0]
    R = R_CONV
    nblk = 4

    def body(u_ref, b_ref, c_ref, g_ref, w_ref, o_ref):
        w0 = w_ref[pl.ds(0, 1), :]
        w1 = w_ref[pl.ds(1, 1), :]
        w2 = w_ref[pl.ds(2, 1), :]

        def chunk(i, carry):
            r0 = pl.multiple_of(i * R, R)
            h0 = pl.multiple_of(jnp.maximum(r0 - HIST, 0), HIST)
            first = i == 0
            ca = _load_ext(c_ref, r0, h0, first, R) * _load_ext(u_ref, r0, h0, first, R)
            conv = w2 * ca[HIST:] + w1 * _shift_down(ca, 1, R) + w0 * _shift_down(ca, 2, R)
            g = g_ref[pl.ds(r0, R), :].astype(F32)
            b = b_ref[pl.ds(r0, R), :].astype(F32)
            o_ref[pl.ds(r0, R), :] = (b * conv * (g * jax.nn.sigmoid(g))).astype(BF16)
            return carry

        lax.fori_loop(0, T // R, chunk, 0)

    def col(off):
        return pl.BlockSpec((T, LANES), lambda j: (0, j + off))

    return pl.pallas_call(
        body, name="conv_fwd", grid=(nblk,),
        in_specs=[col(0), col(4), col(8), col(12), pl.BlockSpec((None, None, 3, LANES), lambda j: (j, l, 0, 0))],
        out_specs=pl.BlockSpec((T, LANES), lambda j: (0, j)),
        out_shape=jax.ShapeDtypeStruct((T, nblk * LANES), BF16),
        compiler_params=_params(),
    )(proj, proj, proj, proj, wconv)


def _causal_window_sum(ext, w):
    s, k = ext, 1
    while k < w:
        s = s + pltpu.roll(s, k, 0)
        k *= 2
    return s


def _anticausal_window_sum(ext, w):
    s, k = ext, 1
    n = ext.shape[0]
    while k < w:
        s = s + pltpu.roll(s, n - k, 0)
        k *= 2
    return s


def _count(r0, rows, w):
    t = r0 + lax.broadcasted_iota(jnp.int32, (rows, LANES), 0)
    return jnp.minimum(t + 1, w).astype(F32)


def _pooled_loop(p_ref, pooled_s, w, T):
    R = R_POOL

    def chunk(i, carry):
        r0 = pl.multiple_of(i * R, R)
        h0 = pl.multiple_of(jnp.maximum(r0 - HIST, 0), HIST)
        ext = _load_ext(p_ref, r0, h0, i == 0, R)
        ws = _causal_window_sum(ext, w)[HIST:]
        pooled_s[pl.ds(r0, R), :] = (ws / _count(r0, R, w) - ext[HIST:]).astype(BF16)
        return carry

    lax.fori_loop(0, T // R, chunk, 0)


def _pool_w_spec(l):
    return pl.BlockSpec((None, None, LANES, LANES), lambda j: (l, j, 0, 0))


def _pool_s_spec(l):
    return pl.BlockSpec((None, 1, LANES), lambda j: (l, 0, j))


def pool_fwd(proj, wpool, pscale3, l):
    T = proj.shape[0]
    R = R_POOL
    ngrp = len(POOL_WINDOWS)

    def body(p_ref, g_ref, w_ref, s_ref, o_ref, pooled_s, mixed_s):
        grp = pl.program_id(0)

        def group(w):
            _pooled_loop(p_ref, pooled_s, w, T)
            mixed_s[...] = jnp.dot(pooled_s[...], w_ref[...].astype(BF16), preferred_element_type=F32)
            sc = s_ref[...]

            def chunk(i, carry):
                r0 = pl.multiple_of(i * R, R)
                g = g_ref[pl.ds(r0, R), :].astype(F32)
                o_ref[pl.ds(r0, R), :] = (mixed_s[pl.ds(r0, R), :] * sc * (g * jax.nn.sigmoid(g))).astype(BF16)
                return carry

            lax.fori_loop(0, T // R, chunk, 0)

        for k, w in enumerate(POOL_WINDOWS):
            pl.when(grp == k)(functools.partial(group, w))

    return pl.pallas_call(
        body, name="pool_fwd", grid=(ngrp,),
        in_specs=[pl.BlockSpec((T, LANES), lambda j: (0, j + 16)), pl.BlockSpec((T, LANES), lambda j: (0, j + 20)),
                  _pool_w_spec(l), _pool_s_spec(l)],
        out_specs=pl.BlockSpec((T, LANES), lambda j: (0, j)),
        out_shape=jax.ShapeDtypeStruct((T, ngrp * LANES), BF16),
        scratch_shapes=[pltpu.VMEM((T, LANES), BF16), pltpu.VMEM((T, LANES), F32)],
        compiler_params=_params(),
    )(proj, proj, wpool, pscale3)


def out_fwd(ya, yp, wo, x, mod4, g_post3, l):
    T, D = x.shape
    H = ya.shape[1]
    tm = 512

    def body(ya_ref, yp_ref, wo_ref, x_ref, gt_ref, g_ref, xn_ref, y_ref):
        y = (jnp.dot(ya_ref[...], wo_ref[0:H, :], preferred_element_type=F32)
             + jnp.dot(yp_ref[...], wo_ref[H:2 * H, :], preferred_element_type=F32))
        xn_ref[...] = x_ref[...] + gt_ref[...] * (y * _rms(y) * g_ref[...])
        y_ref[...] = y

    tile = pl.BlockSpec((tm, D), lambda i: (i, 0))
    half = pl.BlockSpec((tm, H), lambda i: (i, 0))
    return pl.pallas_call(
        body, name="out_fwd", grid=(T // tm,),
        in_specs=[half, half, pl.BlockSpec((2 * H, D), lambda i: (0, 0)), tile, _mod_row(l, 2, D), _layer_row(l, D)],
        out_specs=[tile, tile],
        out_shape=[jax.ShapeDtypeStruct((T, D), F32), jax.ShapeDtypeStruct((T, D), F32)],
        compiler_params=_params(VMEM_BIG),
    )(ya, yp, wo, x, mod4, g_post3)


def loss_head(xl, target):
    T, D = xl.shape
    tm = 512
    nt = T // tm

    def body(x_ref, t_ref, dx_ref, l_ref, acc):
        i = pl.program_id(0)

        @pl.when(i == 0)
        def _():
            acc[...] = jnp.zeros_like(acc)

        d = x_ref[...] - t_ref[...]
        dx_ref[...] = d * (1.0 / D)
        acc[...] += _colsum8(d * d)

        @pl.when(i == nt - 1)
        def _():
            l_ref[...] = jnp.zeros_like(l_ref) + jnp.sum(acc[...]) * (0.5 / D)

    tile = pl.BlockSpec((tm, D), lambda i: (i, 0))
    return pl.pallas_call(
        body, name="loss_head", grid=(nt,),
        in_specs=[tile, tile],
        out_specs=[tile, pl.BlockSpec((SUBLANES, LANES), lambda i: (0, 0))],
        out_shape=[jax.ShapeDtypeStruct((T, D), F32), jax.ShapeDtypeStruct((SUBLANES, LANES), F32)],
        scratch_shapes=[pltpu.VMEM((SUBLANES, D), F32)],
        compiler_params=_params(VMEM_BIG),
    )(xl, target)


def out_bwd(dx, y, ya, yp, wo, mod4, g_post3, l, after):
    T, D = dx.shape
    H = ya.shape[1]
    tm = 512
    nt = T // tm

    def body(dx_ref, y_ref, ya_ref, yp_ref, wo_ref, gt_ref, g_ref, after_ref,
             dya_ref, dyp_ref, dwo_ref, dgt_ref, dg_ref, acc_w, acc_gt, acc_g):
        i = pl.program_id(0)

        @pl.when(i == 0)
        def _():
            acc_w[...] = jnp.zeros_like(acc_w)
            acc_gt[...] = jnp.zeros_like(acc_gt)
            acc_g[...] = jnp.zeros_like(acc_g)

        yv = y_ref[...]
        dxv = dx_ref[...]
        g = g_ref[...]
        r = _rms(yv)
        yn = yv * r
        acc_gt[...] += _colsum8(dxv * (yn * g))
        dn = dxv * gt_ref[...]
        acc_g[...] += _colsum8(dn * yn)
        a = dn * g
        dy = r * (a - yn * jnp.mean(a * yn, axis=-1, keepdims=True))
        dyb = dy.astype(BF16)
        dyc = lax.dot_general(dyb, wo_ref[...], NT, preferred_element_type=F32)
        dya_ref[...] = dyc[:, 0:H].astype(BF16)
        dyp_ref[...] = dyc[:, H:2 * H].astype(BF16)
        acc_w[0:H, :] += lax.dot_general(ya_ref[...], dyb, TN, preferred_element_type=F32)
        acc_w[H:2 * H, :] += lax.dot_general(yp_ref[...], dyb, TN, preferred_element_type=F32)

        @pl.when(i == nt - 1)
        def _():
            dwo_ref[...] = acc_w[...].astype(BF16)
            dgt_ref[...] = jnp.sum(acc_gt[...], axis=0, keepdims=True)
            dg_ref[...] = jnp.sum(acc_g[...], axis=0, keepdims=True)

    row = pl.BlockSpec((1, D), lambda i: (0, 0))
    tile = pl.BlockSpec((tm, D), lambda i: (i, 0))
    half = pl.BlockSpec((tm, H), lambda i: (i, 0))
    full = pl.BlockSpec((2 * H, D), lambda i: (0, 0))
    return pl.pallas_call(
        body, name="out_bwd", grid=(nt,),
        in_specs=[tile, tile, half, half, full, _mod_row(l, 2, D), _layer_row(l, D), ANY],
        out_specs=[half, half, full, row, row],
        out_shape=[jax.ShapeDtypeStruct((T, H), BF16), jax.ShapeDtypeStruct((T, H), BF16),
                   jax.ShapeDtypeStruct((2 * H, D), BF16),
                   jax.ShapeDtypeStruct((1, D), F32), jax.ShapeDtypeStruct((1, D), F32)],
        scratch_shapes=[pltpu.VMEM((2 * H, D), F32), pltpu.VMEM((SUBLANES, D), F32), pltpu.VMEM((SUBLANES, D), F32)],
        compiler_params=_params(VMEM_BIG),
    )(dx, y, ya, yp, wo, mod4, g_post3, after)


def conv_bwd(proj, dya, wconv, l):
    T = proj.shape[0]
    R = R_CONV
    nblk = 4
    nchunk = T // R

    def body(u_ref, b_ref, c_ref, g_ref, dy_ref, w_ref, du_ref, db_ref, dc_ref, dg_ref, dw_ref):
        w0 = w_ref[pl.ds(0, 1), :]
        w1 = w_ref[pl.ds(1, 1), :]
        w2 = w_ref[pl.ds(2, 1), :]

        def chunk(k, carry):
            head, a0, a1, a2 = carry
            i = nchunk - 1 - k
            r0 = pl.multiple_of(i * R, R)
            h0 = pl.multiple_of(jnp.maximum(r0 - HIST, 0), HIST)
            first = i == 0
            ue = _load_ext(u_ref, r0, h0, first, R)
            ce = _load_ext(c_ref, r0, h0, first, R)
            ca = ce * ue
            ca0 = ca[HIST:]
            ca1 = _shift_down(ca, 1, R)
            ca2 = _shift_down(ca, 2, R)
            conv = w2 * ca0 + w1 * ca1 + w0 * ca2
            g = g_ref[pl.ds(r0, R), :].astype(F32)
            b = b_ref[pl.ds(r0, R), :].astype(F32)
            dy = dy_ref[pl.ds(r0, R), :].astype(F32)
            sg = jax.nn.sigmoid(g)
            sl = g * sg
            t = dy * conv
            db_ref[pl.ds(r0, R), :] = (t * sl).astype(BF16)
            dg_ref[pl.ds(r0, R), :] = (t * b * (sg * (1.0 + g * (1.0 - sg)))).astype(BF16)
            dconv = dy * b * sl
            a2 = a2 + _colsum8(dconv * ca0)
            a1 = a1 + _colsum8(dconv * ca1)
            a0 = a0 + _colsum8(dconv * ca2)
            e = jnp.concatenate([dconv, head], axis=0)
            dca = w2 * dconv + w1 * _shift_up(e, 1, R) + w0 * _shift_up(e, 2, R)
            du_ref[pl.ds(r0, R), :] = (dca * ce[HIST:]).astype(BF16)
            dc_ref[pl.ds(r0, R), :] = (dca * ue[HIST:]).astype(BF16)
            return dconv[0:SUBLANES], a0, a1, a2

        z = jnp.zeros((SUBLANES, LANES), F32)
        _, a0, a1, a2 = lax.fori_loop(0, nchunk, chunk, (z, z, z, z))
        dw_ref[pl.ds(0, 1), :] = jnp.sum(a0, axis=0, keepdims=True)
        dw_ref[pl.ds(1, 1), :] = jnp.sum(a1, axis=0, keepdims=True)
        dw_ref[pl.ds(2, 1), :] = jnp.sum(a2, axis=0, keepdims=True)

    def col(off):
        return pl.BlockSpec((T, LANES), lambda j: (0, j + off))

    sec = jax.ShapeDtypeStruct((T, nblk * LANES), BF16)
    return pl.pallas_call(
        body, name="conv_bwd", grid=(nblk,),
        in_specs=[col(0), col(4), col(8), col(12), col(0), pl.BlockSpec((None, None, 3, LANES), lambda j: (j, l, 0, 0))],
        out_specs=[col(0), col(0), col(0), col(0), pl.BlockSpec((None, 3, LANES), lambda j: (j, 0, 0))],
        out_shape=[sec, sec, sec, sec, jax.ShapeDtypeStruct((nblk, 3, LANES), F32)],
        compiler_params=_params(),
    )(proj, proj, proj, proj, dya, wconv)


def pool_bwd(proj, dyp, wpool, pscale3, l):
    T = proj.shape[0]
    R = R_POOL
    ngrp = len(POOL_WINDOWS)
    nchunk = T // R

    def body(p_ref, g_ref, dy_ref, w_ref, s_ref, du_ref, dg_ref, dw_ref, ds_ref,
             pooled_s, mixed_s, dmix_s, dpool_s):
        grp = pl.program_id(0)

        def group(w):
            wb = w_ref[...].astype(BF16)
            _pooled_loop(p_ref, pooled_s, w, T)
            mixed_s[...] = jnp.dot(pooled_s[...], wb, preferred_element_type=F32)
            sc = s_ref[...]

            def gate_chunk(i, acc):
                r0 = pl.multiple_of(i * R, R)
                g = g_ref[pl.ds(r0, R), :].astype(F32)
                dy = dy_ref[pl.ds(r0, R), :].astype(F32)
                mixed = mixed_s[pl.ds(r0, R), :]
                sg = jax.nn.sigmoid(g)
                dg_ref[pl.ds(r0, R), :] = (dy * mixed * sc * (sg * (1.0 + g * (1.0 - sg)))).astype(BF16)
                dms = dy * (g * sg)
                dmix_s[pl.ds(r0, R), :] = (dms * sc).astype(BF16)
                return acc + _colsum8(dms * mixed)

            acc = lax.fori_loop(0, nchunk, gate_chunk, jnp.zeros((SUBLANES, LANES), F32))
            ds_ref[...] = jnp.sum(acc, axis=0, keepdims=True)
            dpool_s[pl.ds(0, T), :] = lax.dot_general(dmix_s[...], wb, NT, preferred_element_type=F32)
            dpool_s[pl.ds(T, HIST), :] = jnp.zeros((HIST, LANES), F32)
            dw_ref[...] = lax.dot_general(pooled_s[...], dmix_s[...], TN, preferred_element_type=F32).astype(BF16)

            def back_chunk(i, carry):
                r0 = pl.multiple_of(i * R, R)
                dpe = dpool_s[pl.ds(r0, R + HIST), :]
                e = dpe / _count(r0, R + HIST, w)
                du_ref[pl.ds(r0, R), :] = (_anticausal_window_sum(e, w)[0:R] - dpe[0:R]).astype(BF16)
                return carry

            lax.fori_loop(0, nchunk, back_chunk, 0)

        for k, w in enumerate(POOL_WINDOWS):
            pl.when(grp == k)(functools.partial(group, w))

    def col(off):
        return pl.BlockSpec((T, LANES), lambda j: (0, j + off))

    sec = jax.ShapeDtypeStruct((T, ngrp * LANES), BF16)
    wspec = pl.BlockSpec((None, LANES, LANES), lambda j: (j, 0, 0))
    sspec = pl.BlockSpec((1, LANES), lambda j: (0, j))
    return pl.pallas_call(
        body, name="pool_bwd", grid=(ngrp,),
        in_specs=[col(16), col(20), col(0), _pool_w_spec(l), _pool_s_spec(l)],
        out_specs=[col(0), col(0), wspec, sspec],
        out_shape=[sec, sec, jax.ShapeDtypeStruct((ngrp, LANES, LANES), BF16),
                   jax.ShapeDtypeStruct((1, ngrp * LANES), F32)],
        scratch_shapes=[pltpu.VMEM((T, LANES), BF16), pltpu.VMEM((T, LANES), F32),
                        pltpu.VMEM((T, LANES), BF16), pltpu.VMEM((T + HIST, LANES), F32)],
        compiler_params=_params(),
    )(proj, proj, dyp, wpool, pscale3)


def in_bwd(dsecs, wg, x, dxo, mod4, g_pre3, l):
    T, D = x.shape
    NB, _, CW = wg.shape
    SW = dsecs[0].shape[1]
    nsec = len(dsecs)
    tm = 256
    nt = T // tm

    def body(*refs):
        d_refs = refs[0:nsec]
        w_ref, x_ref, dxo_ref, sh_ref, sc_ref, g_ref = refs[nsec:nsec + 6]
        dxi_ref, dw_ref, dsh_ref, dsc_ref, dg_ref = refs[nsec + 6:nsec + 11]
        dp_s, acc_w, acc_sh, acc_sc, acc_g = refs[nsec + 11:]
        i = pl.program_id(0)

        @pl.when(i == 0)
        def _():
            acc_w[...] = jnp.zeros_like(acc_w)
            acc_sh[...] = jnp.zeros_like(acc_sh)
            acc_sc[...] = jnp.zeros_like(acc_sc)
            acc_g[...] = jnp.zeros_like(acc_g)

        for s in range(nsec):
            dp_s[:, s * SW:(s + 1) * SW] = d_refs[s][...]
        xv = x_ref[...]
        g = g_ref[...]
        r = _rms(xv)
        xh = xv * r
        n = xh * g
        sc1 = 1.0 + sc_ref[...]
        hb = (n * sc1 + sh_ref[...]).astype(BF16)
        dh = lax.dot_general(dp_s[:, 0:CW], w_ref[0], NT, preferred_element_type=F32)
        for j in range(1, NB):
            dh = dh + lax.dot_general(dp_s[:, j * CW:(j + 1) * CW], w_ref[j], NT, preferred_element_type=F32)
        for j in range(NB):
            acc_w[j] += lax.dot_general(hb, dp_s[:, j * CW:(j + 1) * CW], TN, preferred_element_type=F32)
        acc_sh[...] += _colsum8(dh)
        acc_sc[...] += _colsum8(dh * n)
        dnp = dh * sc1
        acc_g[...] += _colsum8(dnp * xh)
        a = dnp * g
        dxi_ref[...] = dxo_ref[...] + r * (a - xh * jnp.mean(a * xh, axis=-1, keepdims=True))

        @pl.when(i == nt - 1)
        def _():
            dw_ref[...] = acc_w[...].astype(BF16)
            dsh_ref[...] = jnp.sum(acc_sh[...], axis=0, keepdims=True)
            dsc_ref[...] = jnp.sum(acc_sc[...], axis=0, keepdims=True)
            dg_ref[...] = jnp.sum(acc_g[...], axis=0, keepdims=True)

    row = pl.BlockSpec((1, D), lambda i: (0, 0))
    tile = pl.BlockSpec((tm, D), lambda i: (i, 0))
    sect = pl.BlockSpec((tm, SW), lambda i: (i, 0))
    wspec = pl.BlockSpec((NB, D, CW), lambda i: (0, 0, 0))
    rowshape = jax.ShapeDtypeStruct((1, D), F32)
    return pl.pallas_call(
        body, name="in_bwd", grid=(nt,),
        in_specs=[sect] * nsec + [wspec, tile, tile, _mod_row(l, 0, D), _mod_row(l, 1, D), _layer_row(l, D)],
        out_specs=[tile, wspec, row, row, row],
        out_shape=[jax.ShapeDtypeStruct((T, D), F32), jax.ShapeDtypeStruct((NB, D, CW), BF16),
                   rowshape, rowshape, rowshape],
        scratch_shapes=[pltpu.VMEM((tm, nsec * SW), BF16), pltpu.VMEM((NB, D, CW), F32),
                        pltpu.VMEM((SUBLANES, D), F32), pltpu.VMEM((SUBLANES, D), F32), pltpu.VMEM((SUBLANES, D), F32)],
        compiler_params=_params(VMEM_BIG),
    )(*dsecs, wg, x, dxo, mod4, mod4, g_pre3)


def _rcopy(src, dst, ssem, rsem, dev):
    return pltpu.make_async_remote_copy(src_ref=src, dst_ref=dst, send_sem=ssem, recv_sem=rsem,
                                        device_id=dev, device_id_type=MESH)


def _peers7(x, y, c):
    out = []
    for m in range(1, N_DEV):
        bx, by, bc = (m >> 2) & 1, (m >> 1) & 1, m & 1
        out.append(((1 - x) if bx else x, (1 - y) if by else y, (1 - c) if bc else c))
    return out


def gather_weights(gis, gos, c8, wc):
    L = len(gis)
    n_w = 3 * 2 * L
    n_rem = n_w + 7 + 3

    def body(*refs):
        c_ref, wc_ref = refs[2 * L], refs[2 * L + 1]
        outs = refs[2 * L + 2:]
        go_all = outs[0:2 * L]
        call, wcall = outs[2 * L], outs[2 * L + 1]
        ssem, rsem, fss, frs, lsem = outs[2 * L + 2:]
        x, y, c = _me()
        myc = 2 * x + y
        me_lin = 4 * x + 2 * y + c
        me = (x, y, c)
        sib = (x, y, 1 - c)
        chips = [(1 - x, y), (x, 1 - y), (1 - x, 1 - y)]

        local = [pltpu.make_async_copy(c_ref, call.at[me_lin], lsem.at[0]),
                 pltpu.make_async_copy(wc_ref, wcall.at[myc], lsem.at[1])]
        for cp in local:
            cp.start()

        sends, recvs, fwds, frecvs = [], [], [], []
        k = 0
        for (px, py) in chips:
            pc = 2 * px + py
            for a in range(2 * L):
                buf = go_all[a]
                h = buf.shape[1] // 2
                rows, orows = pl.ds(c * h, h), pl.ds((1 - c) * h, h)
                sends.append(_rcopy(buf.at[myc, rows], buf.at[myc, rows], ssem.at[k], rsem.at[k], (px, py, c)))
                landed = buf.at[pc, rows]
                recvs.append(_rcopy(landed, landed, ssem.at[k], rsem.at[k], me))
                fwds.append(_rcopy(landed, landed, fss.at[k], frs.at[k], sib))
                other = buf.at[pc, orows]
                frecvs.append(_rcopy(other, other, fss.at[k], frs.at[k], me))
                k += 1
        for m, peer in enumerate(_peers7(x, y, c)):
            plin = 4 * peer[0] + 2 * peer[1] + peer[2]
            sends.append(_rcopy(c_ref, call.at[me_lin], ssem.at[n_w + m], rsem.at[n_w + m], peer))
            recvs.append(_rcopy(call.at[plin], call.at[plin], ssem.at[n_w + m], rsem.at[n_w + m], me))
        for j, (px, py) in enumerate(chips):
            pc = 2 * px + py
            sends.append(_rcopy(wc_ref, wcall.at[myc], ssem.at[n_w + 7 + j], rsem.at[n_w + 7 + j], (px, py, c)))
            recvs.append(_rcopy(wcall.at[pc], wcall.at[pc], ssem.at[n_w + 7 + j], rsem.at[n_w + 7 + j], me))

        for cp in sends:
            cp.start()
        for k in range(n_w):
            recvs[k].wait_recv()
            fwds[k].start()
        for k in range(n_w, n_rem):
            recvs[k].wait_recv()
        for cp in frecvs:
            cp.wait_recv()
        for cp in sends + fwds:
            cp.wait_send()
        for cp in local:
            cp.wait()

    bufs = [b for l in range(L) for b in (gis[l], gos[l])]
    out_shape = ([jax.ShapeDtypeStruct(b.shape, b.dtype) for b in bufs]
                 + [jax.ShapeDtypeStruct((N_DEV, SUBLANES, LANES), F32),
                    jax.ShapeDtypeStruct((N_CHIPS, wc.shape[0], 3, LANES), F32)])
    outs = pl.pallas_call(
        body, name="gather_weights",
        in_specs=[ANY] * (2 * L + 2), out_specs=[ANY] * (2 * L + 2), out_shape=out_shape,
        input_output_aliases={a: a for a in range(2 * L)},
        scratch_shapes=[pltpu.SemaphoreType.DMA((n_rem,)), pltpu.SemaphoreType.DMA((n_rem,)),
                        pltpu.SemaphoreType.DMA((n_w,)), pltpu.SemaphoreType.DMA((n_w,)),
                        pltpu.SemaphoreType.DMA((2,))],
        compiler_params=_params(n_grid=0),
    )(*bufs, c8, wc)
    return outs[0:2 * L:2], outs[1:2 * L:2], outs[2 * L], outs[2 * L + 1]


HBM = pl.BlockSpec(memory_space=pltpu.HBM)
SEM = pl.BlockSpec(memory_space=pltpu.SEMAPHORE)
N_XCHG = 6


def _hbm(a):
    return pltpu.with_memory_space_constraint(a, pltpu.HBM)


def gather_start(bufs):
    n = len(bufs)
    L = n // 2

    def body(*refs):
        ins = refs[0:n]
        ssems, rsems = refs[n:n + L], refs[n + L:n + 2 * L]
        token = refs[-1]
        x, y, c = _me()
        myc = 2 * x + y
        for l in range(L):
            k = 0
            for (px, py) in [(1 - x, y), (x, 1 - y), (1 - x, 1 - y)]:
                for buf in (ins[2 * l], ins[2 * l + 1]):
                    h = buf.shape[1] // 2
                    own = buf.at[myc, pl.ds(c * h, h)]
                    _rcopy(own, own, ssems[l].at[k], rsems[l].at[k], (px, py, c)).start()
                    k += 1
        token[...] = jnp.zeros_like(token)

    sems = [pltpu.SemaphoreType.DMA((N_XCHG,))] * (2 * L)
    outs = pl.pallas_call(
        body, name="gather_start",
        in_specs=[HBM] * n,
        out_specs=[SEM] * (2 * L) + [HBM] * n + [pl.BlockSpec(memory_space=pltpu.VMEM)],
        out_shape=sems + [pltpu.HBM(b.shape, b.dtype) for b in bufs] + [jax.ShapeDtypeStruct((SUBLANES, LANES), F32)],
        input_output_aliases={a: 2 * L + a for a in range(n)},
        compiler_params=pltpu.CompilerParams(has_side_effects=pltpu.SideEffectType.DATAFLOW_SIDE_EFFECTING),
    )(*[_hbm(b) for b in bufs])
    return outs[0:L], outs[L:2 * L], outs[2 * L:2 * L + n], outs[-1]


def gather_wait(gi, go, ssem, rsem, after):
    def body(gi_ref, go_ref, ssem_ref, rsem_ref, after_ref, gi_out, go_out):
        x, y, c = _me()
        myc = 2 * x + y
        k = 0
        for (px, py) in [(1 - x, y), (x, 1 - y), (1 - x, 1 - y)]:
            for buf in (gi_ref, go_ref):
                h = buf.shape[1] // 2
                rows = pl.ds(c * h, h)
                cp = _rcopy(buf.at[myc, rows], buf.at[2 * px + py, rows], ssem_ref.at[k], rsem_ref.at[k], (px, py, c))
                cp.wait_send()
                cp.wait_recv()
                k += 1

    return pl.pallas_call(
        body, name="gather_wait",
        in_specs=[HBM, HBM, SEM, SEM, ANY], out_specs=[HBM, HBM],
        out_shape=[pltpu.HBM(gi.shape, gi.dtype), pltpu.HBM(go.shape, go.dtype)],
        input_output_aliases={0: 0, 1: 1},
        compiler_params=pltpu.CompilerParams(has_side_effects=pltpu.SideEffectType.DATAFLOW_SIDE_EFFECTING),
    )(gi, go, ssem, rsem, after)


def forward_sibling(gi, go):
    def body(gi_in, go_in, gi_ref, go_ref, ssem, rsem):
        x, y, c = _me()
        me, sib = (x, y, c), (x, y, 1 - c)
        sends, recvs = [], []
        k = 0
        for (px, py) in [(1 - x, y), (x, 1 - y), (1 - x, 1 - y)]:
            pc = 2 * px + py
            for buf in (gi_ref, go_ref):
                h = buf.shape[1] // 2
                landed = buf.at[pc, pl.ds(c * h, h)]
                other = buf.at[pc, pl.ds((1 - c) * h, h)]
                sends.append(_rcopy(landed, landed, ssem.at[k], rsem.at[k], sib))
                recvs.append(_rcopy(other, other, ssem.at[k], rsem.at[k], me))
                k += 1
        for cp in sends:
            cp.start()
        for cp in recvs:
            cp.wait_recv()
        for cp in sends:
            cp.wait_send()

    return pl.pallas_call(
        body, name="forward_sibling",
        in_specs=[ANY, ANY], out_specs=[ANY, ANY],
        out_shape=[jax.ShapeDtypeStruct(gi.shape, gi.dtype), jax.ShapeDtypeStruct(go.shape, go.dtype)],
        input_output_aliases={0: 0, 1: 1},
        scratch_shapes=[pltpu.SemaphoreType.DMA((N_XCHG,)), pltpu.SemaphoreType.DMA((N_XCHG,))],
        compiler_params=_params(n_grid=0),
    )(gi, go)


def gather_small(c8, wc, token):
    def body(c_ref, wc_ref, token_ref, call, wcall, ssem, rsem, lsem):
        x, y, c = _me()
        myc = 2 * x + y
        me_lin = 4 * x + 2 * y + c
        me = (x, y, c)
        local = [pltpu.make_async_copy(c_ref, call.at[me_lin], lsem.at[0]),
                 pltpu.make_async_copy(wc_ref, wcall.at[myc], lsem.at[1])]
        for cp in local:
            cp.start()
        sends, recvs = [], []
        for m, peer in enumerate(_peers7(x, y, c)):
            plin = 4 * peer[0] + 2 * peer[1] + peer[2]
            sends.append(_rcopy(c_ref, call.at[me_lin], ssem.at[m], rsem.at[m], peer))
            recvs.append(_rcopy(call.at[plin], call.at[plin], ssem.at[m], rsem.at[m], me))
        for j, (px, py) in enumerate([(1 - x, y), (x, 1 - y), (1 - x, 1 - y)]):
            pc = 2 * px + py
            sends.append(_rcopy(wc_ref, wcall.at[myc], ssem.at[7 + j], rsem.at[7 + j], (px, py, c)))
            recvs.append(_rcopy(wcall.at[pc], wcall.at[pc], ssem.at[7 + j], rsem.at[7 + j], me))
        for cp in sends:
            cp.start()
        for cp in recvs:
            cp.wait_recv()
        for cp in sends:
            cp.wait_send()
        for cp in local:
            cp.wait()

    return pl.pallas_call(
        body, name="gather_small",
        in_specs=[ANY] * 3, out_specs=[ANY] * 2,
        out_shape=[jax.ShapeDtypeStruct((N_DEV, SUBLANES, LANES), F32),
                   jax.ShapeDtypeStruct((N_CHIPS, wc.shape[0], 3, LANES), F32)],
        scratch_shapes=[pltpu.SemaphoreType.DMA((10,)), pltpu.SemaphoreType.DMA((10,)), pltpu.SemaphoreType.DMA((2,))],
        compiler_params=_params(n_grid=0),
    )(c8, wc, token)


def exchange_mod(part):
    def body(p_ref, o_ref, ssem, rsem, lsem):
        x, y, c = _me()
        myc = 2 * x + y
        chips = [(1 - x, y), (x, 1 - y), (1 - x, 1 - y)]
        loc = pltpu.make_async_copy(p_ref, o_ref.at[myc], lsem)
        loc.start()
        sends = [_rcopy(p_ref, o_ref.at[myc], ssem.at[j], rsem.at[j], (px, py, c)) for j, (px, py) in enumerate(chips)]
        for cp in sends:
            cp.start()
        for j, (px, py) in enumerate(chips):
            slot = o_ref.at[2 * px + py]
            _rcopy(slot, slot, ssem.at[j], rsem.at[j], (x, y, c)).wait_recv()
        for cp in sends:
            cp.wait_send()
        loc.wait()

    return pl.pallas_call(
        body, name="exchange_mod", in_specs=[ANY], out_specs=ANY,
        out_shape=jax.ShapeDtypeStruct((N_CHIPS,) + part.shape, part.dtype),
        scratch_shapes=[pltpu.SemaphoreType.DMA((3,)), pltpu.SemaphoreType.DMA((3,)), pltpu.SemaphoreType.DMA],
        compiler_params=_params(n_grid=0),
    )(part)


def sibling_exchange(parts, after):
    n = len(parts)

    def body(*refs):
        ins, outs = refs[0:n], refs[n + 1:2 * n + 1]
        ssem, rsem = refs[2 * n + 1:]
        x, y, c = _me()
        sends, recvs = [], []
        for k in range(n):
            half = ins[k].shape[1] // 2
            sends.append(_rcopy(ins[k].at[:, pl.ds((1 - c) * half, half)], outs[k], ssem.at[k], rsem.at[k], (x, y, 1 - c)))
            recvs.append(_rcopy(outs[k], outs[k], ssem.at[k], rsem.at[k], (x, y, c)))
        for cp in sends:
            cp.start()
        for cp in recvs:
            cp.wait_recv()
        for cp in sends:
            cp.wait_send()

    return pl.pallas_call(
        body, name="sibling_exchange",
        in_specs=[ANY] * (n + 1), out_specs=[ANY] * n,
        out_shape=[jax.ShapeDtypeStruct((a.shape[0], a.shape[1] // 2) + a.shape[2:], a.dtype) for a in parts],
        scratch_shapes=[pltpu.SemaphoreType.DMA((n,)), pltpu.SemaphoreType.DMA((n,))],
        compiler_params=_params(n_grid=0),
    )(*parts, after)


def gather_pack(pack):
    def body(pack_ref, packs, ssem, rsem, lsem):
        x, y, c = _me()
        me_lin = 4 * x + 2 * y + c
        loc = pltpu.make_async_copy(pack_ref, packs.at[me_lin], lsem)
        loc.start()
        sends, recvs = [], []
        for m, peer in enumerate(_peers7(x, y, c)):
            plin = 4 * peer[0] + 2 * peer[1] + peer[2]
            sends.append(_rcopy(pack_ref, packs.at[me_lin], ssem.at[m], rsem.at[m], peer))
            recvs.append(_rcopy(packs.at[plin], packs.at[plin], ssem.at[m], rsem.at[m], (x, y, c)))
        for cp in sends:
            cp.start()
        for cp in recvs:
            cp.wait_recv()
        for cp in sends:
            cp.wait_send()
        loc.wait()

    return pl.pallas_call(
        body, name="gather_pack", in_specs=[ANY], out_specs=ANY,
        out_shape=jax.ShapeDtypeStruct((N_DEV,) + pack.shape, pack.dtype),
        scratch_shapes=[pltpu.SemaphoreType.DMA((7,)), pltpu.SemaphoreType.DMA((7,)), pltpu.SemaphoreType.DMA],
        compiler_params=_params(n_grid=0),
    )(pack)


def chip_start(parts):
    n = len(parts)

    def body(*refs):
        ins, lands = refs[0:n], refs[n:2 * n]
        ssem, rsem = refs[2 * n], refs[2 * n + 1]
        token = refs[-1]
        x, y, c = _me()
        k = 0
        for j, (px, py) in enumerate([(1 - x, y), (x, 1 - y), (1 - x, 1 - y)]):
            for a in range(n):
                _rcopy(ins[a].at[2 * px + py], lands[a].at[j], ssem.at[k], rsem.at[k], (px, py, c)).start()
                k += 1
        token[...] = jnp.zeros_like(token)

    lands = [lax.empty((3,) + a.shape[1:], a.dtype) for a in parts]
    outs = pl.pallas_call(
        body, name="chip_start",
        in_specs=[HBM] * (2 * n),
        out_specs=[SEM, SEM] + [HBM] * (2 * n) + [pl.BlockSpec(memory_space=pltpu.VMEM)],
        out_shape=([pltpu.SemaphoreType.DMA((3 * n,))] * 2 + [pltpu.HBM(a.shape, a.dtype) for a in parts]
                   + [pltpu.HBM(a.shape, a.dtype) for a in lands] + [jax.ShapeDtypeStruct((SUBLANES, LANES), F32)]),
        input_output_aliases={a: 2 + a for a in range(2 * n)},
        compiler_params=pltpu.CompilerParams(has_side_effects=pltpu.SideEffectType.DATAFLOW_SIDE_EFFECTING),
    )(*[_hbm(a) for a in parts], *[_hbm(a) for a in lands])
    return outs[0], outs[1], outs[2:2 + n], outs[2 + n:2 + 2 * n], outs[-1]


def chip_wait(parts, lands, ssem, rsem, after):
    n = len(parts)
    after = list(after)

    def body(*refs):
        ins, lnd = refs[0:n], refs[n:2 * n]
        ssem_ref, rsem_ref = refs[2 * n], refs[2 * n + 1]
        x, y, c = _me()
        k = 0
        for j, (px, py) in enumerate([(1 - x, y), (x, 1 - y), (1 - x, 1 - y)]):
            for a in range(n):
                cp = _rcopy(ins[a].at[2 * px + py], lnd[a].at[j], ssem_ref.at[k], rsem_ref.at[k], (px, py, c))
                cp.wait_send()
                cp.wait_recv()
                k += 1

    outs = pl.pallas_call(
        body, name="chip_wait",
        in_specs=[HBM] * (2 * n) + [SEM, SEM] + [ANY] * len(after), out_specs=[HBM] * (2 * n),
        out_shape=[pltpu.HBM(a.shape, a.dtype) for a in list(parts) + list(lands)],
        input_output_aliases={a: a for a in range(2 * n)},
        compiler_params=pltpu.CompilerParams(has_side_effects=pltpu.SideEffectType.DATAFLOW_SIDE_EFFECTING),
    )(*parts, *lands, ssem, rsem, *after)
    return outs[0:n], outs[n:2 * n]


def chip_exchange(chi, cho, chp):
    L = len(chi)
    n = 3 * L * 3

    def body(*refs):
        ins = refs[0:3 * L]
        outs = refs[3 * L:6 * L]
        ssem, rsem = refs[6 * L:]
        x, y, c = _me()
        chips = [(1 - x, y), (x, 1 - y), (1 - x, 1 - y)]
        sends, recvs = [], []
        k = 0
        for j, (px, py) in enumerate(chips):
            pc = 2 * px + py
            for a in range(3 * L):
                sends.append(_rcopy(ins[a].at[pc], outs[a].at[j], ssem.at[k], rsem.at[k], (px, py, c)))
                recvs.append(_rcopy(outs[a].at[j], outs[a].at[j], ssem.at[k], rsem.at[k], (x, y, c)))
                k += 1
        for cp in sends:
            cp.start()
        for cp in recvs:
            cp.wait_recv()
        for cp in sends:
            cp.wait_send()

    arrs = [a for l in range(L) for a in (chi[l], cho[l], chp[l])]
    outs = pl.pallas_call(
        body, name="chip_exchange",
        in_specs=[ANY] * (3 * L), out_specs=[ANY] * (3 * L),
        out_shape=[jax.ShapeDtypeStruct((3,) + a.shape[1:], a.dtype) for a in arrs],
        scratch_shapes=[pltpu.SemaphoreType.DMA((n,)), pltpu.SemaphoreType.DMA((n,))],
        compiler_params=_params(n_grid=0),
    )(*arrs)
    return outs[0::3], outs[1::3], outs[2::3]


def spread_reduced(gi, go, gp, layers, wp_layers, after):
    _, D, _ = gi.shape
    RO = go.shape[1]
    HP = LANES // 2
    n = 2 * len(layers) + 7 * len(wp_layers)

    def body(gi_in, go_in, gp_in, after_ref, gi, go, gp, ssem, rsem):
        x, y, c = _me()
        me = (x, y, c)
        myc = 2 * x + y
        sib = (x, y, 1 - c)
        peers = _peers7(x, y, c)
        hi, ho, hp = pl.ds(c * (D // 2), D // 2), pl.ds(c * (RO // 2), RO // 2), pl.ds(c * HP, HP)
        ohi, oho = pl.ds((1 - c) * (D // 2), D // 2), pl.ds((1 - c) * (RO // 2), RO // 2)
        sends, recvs = [], []
        k = 0
        for l in layers:
            sends.append(_rcopy(gi.at[l, hi], gi.at[l, hi], ssem.at[k], rsem.at[k], sib))
            recvs.append(_rcopy(gi.at[l, ohi], gi.at[l, ohi], ssem.at[k], rsem.at[k], me))
            k += 1
            sends.append(_rcopy(go.at[l, ho], go.at[l, ho], ssem.at[k], rsem.at[k], sib))
            recvs.append(_rcopy(go.at[l, oho], go.at[l, oho], ssem.at[k], rsem.at[k], me))
            k += 1
        for l in wp_layers:
            for peer in peers:
                pchip = 2 * peer[0] + peer[1]
                prow = pl.ds(peer[2] * HP, HP)
                sends.append(_rcopy(gp.at[l, myc, hp], gp.at[l, myc, hp], ssem.at[k], rsem.at[k], peer))
                recvs.append(_rcopy(gp.at[l, pchip, prow], gp.at[l, pchip, prow], ssem.at[k], rsem.at[k], me))
                k += 1
        for cp in sends:
            cp.start()
        for cp in recvs:
            cp.wait_recv()
        for cp in sends:
            cp.wait_send()

    return pl.pallas_call(
        body, name="spread_reduced",
        in_specs=[ANY] * 4, out_specs=[ANY] * 3,
        out_shape=[jax.ShapeDtypeStruct(a.shape, a.dtype) for a in (gi, go, gp)],
        input_output_aliases={0: 0, 1: 1, 2: 2},
        scratch_shapes=[pltpu.SemaphoreType.DMA((n,)), pltpu.SemaphoreType.DMA((n,))],
        compiler_params=_params(n_grid=0),
    )(gi, go, gp, after)


def add_sibling(cidx, mine, sib):
    def body(c_ref, *refs):
        for a in range(3):
            m, s, o = refs[a], refs[3 + a], refs[6 + a]
            o[...] = (m[...].astype(F32) + s[...].astype(F32)).astype(BF16)

    def mine_spec(a):
        h = a.shape[1] // 2
        return pl.BlockSpec((None, h, a.shape[2]), lambda j, c_ref: (j, c_ref[0], 0))

    def sib_spec(a):
        return pl.BlockSpec((None,) + a.shape[1:], lambda j, c_ref: (j, 0, 0))

    return pl.pallas_call(
        body, name="add_sibling",
        grid_spec=pltpu.PrefetchScalarGridSpec(
            num_scalar_prefetch=1, grid=(N_CHIPS,),
            in_specs=[mine_spec(a) for a in mine] + [sib_spec(a) for a in sib],
            out_specs=[sib_spec(a) for a in sib]),
        out_shape=[jax.ShapeDtypeStruct(a.shape, BF16) for a in sib],
        compiler_params=_params(VMEM_BIG),
    )(cidx, *mine, *sib)


def sum_chips(pos, own, rb, acc, l, shapes):
    nq = 4
    n_in = 6 + (3 if acc is not None else 0)

    def body(pos_ref, *refs):
        for a in range(3):
            m, b, o = refs[a], refs[3 + a], refs[n_in + a]
            s = m[...].astype(F32)
            for j in range(3):
                s = s + b[j].astype(F32)
            o[...] = s

    def own_spec(a):
        return pl.BlockSpec((None, a.shape[1] // nq, a.shape[2]), lambda q, p: (p[1], q, 0))

    def rb_spec(a):
        return pl.BlockSpec((3, a.shape[1] // nq, a.shape[2]), lambda q, p: (0, q, 0))

    hi, ho, hp = own[0].shape[1] // nq, own[1].shape[1] // nq, own[2].shape[1] // nq
    out_specs = [pl.BlockSpec((None, hi, shapes[0][2]), lambda q, p: (l, p[0] * nq + q, 0)),
                 pl.BlockSpec((None, ho, shapes[1][2]), lambda q, p: (l, p[0] * nq + q, 0)),
                 pl.BlockSpec((None, None, hp, LANES), lambda q, p: (l, p[1], p[0] * nq + q, 0))]
    in_specs = [own_spec(a) for a in own] + [rb_spec(a) for a in rb]
    args = list(own) + list(rb)
    aliases = {}
    if acc is not None:
        in_specs += [ANY] * 3
        args += list(acc)
        aliases = {7: 0, 8: 1, 9: 2}
    return pl.pallas_call(
        body, name="sum_chips",
        grid_spec=pltpu.PrefetchScalarGridSpec(num_scalar_prefetch=1, grid=(nq,), in_specs=in_specs, out_specs=out_specs),
        out_shape=[jax.ShapeDtypeStruct(s, F32) for s in shapes],
        input_output_aliases=aliases,
        compiler_params=_params(VMEM_BIG),
    )(pos, *args)


def pack_small(per_layer, loss_blk):
    L = len(per_layer)
    D = per_layer[0][0].shape[1]

    def body(*refs):
        o = refs[-1]
        lb = refs[-2]
        o[...] = jnp.zeros_like(o)
        for l in range(L):
            dgpre, dgpost, dsh, dsc, dgt, dps, dwc = refs[7 * l:7 * l + 7]
            base = SUBLANES * l
            for r, src in enumerate((dgpre, dgpost, dsh, dsc, dgt)):
                o[pl.ds(base + r, 1), :] = src[...]
            o[pl.ds(base + 5, 1), 0:dps.shape[1]] = dps[...]
            for j in range(dwc.shape[0]):
                for k in range(3):
                    idx = 3 * j + k
                    o[pl.ds(base + 6 + idx // 8, 1), (idx % 8) * LANES:(idx % 8 + 1) * LANES] = dwc[j, pl.ds(k, 1), :]
        o[pl.ds(5, 1), 4 * LANES:5 * LANES] = lb[pl.ds(0, 1), :]

    flat = [a for layer in per_layer for a in layer]
    return pl.pallas_call(
        body, name="pack_small",
        out_shape=jax.ShapeDtypeStruct((L * SUBLANES, D), F32),
        compiler_params=_params(n_grid=0),
    )(*flat, loss_blk)


def sum_packs(packs):
    def body(p_ref, o_ref):
        s = p_ref[0]
        for d in range(1, N_DEV):
            s = s + p_ref[d]
        o_ref[...] = s

    return pl.pallas_call(
        body, name="sum_packs",
        out_shape=jax.ShapeDtypeStruct(packs.shape[1:], F32),
        compiler_params=_params(n_grid=0),
    )(packs)


def _adamw_math(w, g, m, v):
    m = ADAM_B1 * m + (1.0 - ADAM_B1) * g
    v = ADAM_B2 * v + (1.0 - ADAM_B2) * (g * g)
    m_hat = m / (1.0 - ADAM_B1 ** ADAM_STEP)
    v_hat = v / (1.0 - ADAM_B2 ** ADAM_STEP)
    delta = -ADAM_LR * (m_hat / (jnp.sqrt(v_hat) + ADAM_EPS) + ADAM_WD * w)
    return delta, m, v


def adamw(w, g, m, v, block, name, first=0, count=None, acc=None):
    grid = tuple(s // b for s, b in zip(w.shape, block))
    if count is not None:
        grid = (count,) + grid[1:]

    def body(w_ref, g_ref, m_ref, v_ref, *rest):
        d_ref, mo_ref, vo_ref = rest[-3:]
        d, mm, vv = _adamw_math(w_ref[...], g_ref[...], m_ref[...], v_ref[...])
        d_ref[...] = d
        mo_ref[...] = mm
        vo_ref[...] = vv

    spec = pl.BlockSpec(block, lambda i, *rest: (first + i,) + rest)
    shape = jax.ShapeDtypeStruct(w.shape, F32)
    extra = [] if acc is None else list(acc)
    return pl.pallas_call(
        body, name=name, grid=grid,
        in_specs=[spec] * 4 + [ANY] * len(extra), out_specs=[spec] * 3, out_shape=[shape] * 3,
        input_output_aliases={4 + a: a for a in range(len(extra))},
        compiler_params=_params(VMEM_BIG, n_grid=len(grid)),
    )(w, g, m, v, *extra)


def ada_finish(c_all, dmod, w, m, v):
    L, D, CW = w.shape
    hD = D // 2

    def body(c_ref, d_ref, w_ref, m_ref, v_ref, g_ref, dl_ref, mo_ref, vo_ref):
        cv = c_ref[...]
        z = jnp.zeros_like(cv)
        ca = jnp.concatenate([cv * jax.nn.sigmoid(cv), z], axis=0).astype(BF16)
        dm = jnp.concatenate([d_ref[0], jnp.zeros_like(d_ref[0])], axis=0).astype(BF16)
        g = lax.dot_general(ca, dm, TN, preferred_element_type=F32)
        g_ref[0] = g
        d, mm, vv = _adamw_math(w_ref[0], g, m_ref[0], v_ref[0])
        dl_ref[0] = d
        mo_ref[0] = mm
        vo_ref[0] = vv

    big = pl.BlockSpec((1, hD, CW), lambda l, h: (l, h, 0))
    shape = jax.ShapeDtypeStruct(w.shape, F32)
    return pl.pallas_call(
        body, name="ada_finish", grid=(L, 2),
        in_specs=[pl.BlockSpec((N_DEV, hD), lambda l, h: (0, h)), pl.BlockSpec((1, N_DEV, CW), lambda l, h: (l, 0, 0)),
                  big, big, big],
        out_specs=[big] * 4, out_shape=[shape] * 4,
        compiler_params=_params(VMEM_BIG, n_grid=2),
    )(c_all, dmod, w, m, v)


def kernel(x, c, w_ada, b_ada, g_pre, w_in, w_conv, w_pool, pool_scale, w_out, g_post, loss_target, m_w_ada, m_b_ada, m_g_pre, m_w_in, m_w_conv, m_w_pool, m_pool_scale, m_w_out, m_g_post, v_w_ada, v_b_ada, v_g_pre, v_w_in, v_w_conv, v_w_pool, v_pool_scale, v_w_out, v_g_post):
    L, D, CW = w_in.shape
    RO = w_out.shape[1]
    T = x.shape[1]
    ix, iy, ic = _me()
    chip = 2 * ix + iy
    me_lin = 4 * ix + 2 * iy + ic

    pos = jnp.stack([ic, chip]).astype(jnp.int32)
    g_pre3, g_post3 = g_pre.reshape(L, 1, D), g_post.reshape(L, 1, D)
    pscale3 = pool_scale.reshape(L, 1, pool_scale.shape[1])

    c_all3, wconv_all = gather_small(c.reshape(SUBLANES, LANES), w_conv, pos)
    c_all = c_all3.reshape(N_DEV, D)
    ss0, rs0, bufs0, token0 = gather_start(list(cast_weights(pos, w_in, w_out, 0, c_all3)))
    b_my = lax.dynamic_slice_in_dim(b_ada, chip * CW, CW, axis=1)
    mod_all = exchange_mod(mod_part(c_all, w_ada, b_my, token0))
    mod = lax.dynamic_index_in_dim(mod_all, me_lin, axis=2, keepdims=False)
    mod4 = jnp.transpose(mod, (1, 0, 2)).reshape(L, 3, 1, D)
    rest = [b for l in range(1, L) for b in cast_weights(pos, w_in, w_out, l, mod_all)]
    ss1, rs1, bufs1, token1 = gather_start(rest)
    g_ssems, g_rsems, g_bufs = list(ss0) + list(ss1), list(rs0) + list(rs1), list(bufs0) + list(bufs1)

    xs, projs, yas, yps, ys = [x.reshape(T, D)], [], [], [], []
    wg_in, wg_out = [], []
    for l in range(L):
        gi, go = gather_wait(g_bufs[2 * l], g_bufs[2 * l + 1], g_ssems[l], g_rsems[l], token1 if l == 0 else xs[l])
        gi, go = forward_sibling(gi, go)
        wg_in.append(gi)
        wg_out.append(go.reshape(N_CHIPS * RO, D))
        proj = proj_fwd(xs[l], mod4, g_pre3, wg_in[l], l)
        ya = conv_fwd(proj, wconv_all, l)
        yp = pool_fwd(proj, w_pool, pscale3, l)
        xn, yv = out_fwd(ya, yp, wg_out[l], xs[l], mod4, g_post3, l)
        xs.append(xn)
        projs.append(proj)
        yas.append(ya)
        yps.append(yp)
        ys.append(yv)

    dx, loss_blk = loss_head(xs[L], loss_target.reshape(T, D))

    shapes = (w_in.shape, w_out.shape, w_pool.shape)
    smalls = [None] * L
    acc, flying, token, packs_all = None, None, loss_blk, None
    for l in reversed(range(L)):
        dya, dyp, dwo_l, dgate, dgpost = out_bwd(dx, ys[l], yas[l], yps[l], wg_out[l], mod4, g_post3, l, token)
        du_a, db_a, dc_a, dg_a, dwc = conv_bwd(projs[l], dya, wconv_all, l)
        du_p, dg_p, dwp_l, dps = pool_bwd(projs[l], dyp, w_pool, pscale3, l)
        dx, dwi_l, dshift, dscale, dgpre = in_bwd([du_a, db_a, dc_a, dg_a, du_p, dg_p], wg_in[l], xs[l], dx,
                                                  mod4, g_pre3, l)
        smalls[l] = (dgpre, dgpost, dshift, dscale, dgate, dps, dwc)
        parts = [dwi_l, dwo_l.reshape(N_CHIPS, RO, D), dwp_l]
        before = dx
        if l == 0:
            packs_all = gather_pack(pack_small(smalls, loss_blk))
            before = packs_all
        chip_parts = add_sibling(pos, parts, sibling_exchange(parts, before))
        ssem, rsem, chip_parts, lands, token = chip_start(chip_parts)
        if flying is not None:
            fl, f_parts, f_lands, f_ssem, f_rsem = flying
            f_parts, f_lands = chip_wait(f_parts, f_lands, f_ssem, f_rsem, [token])
            acc = sum_chips(pos, f_parts, f_lands, acc, fl, shapes)
        flying = (l, chip_parts, lands, ssem, rsem)
    grad_x = dx.reshape(1, T, D)

    g_w_in, g_w_out, g_w_pool = spread_reduced(*acc, tuple(range(1, L)), (), acc[0])
    small = sum_packs(packs_all).reshape(L, SUBLANES, D)
    loss = small[0, 5, 4 * LANES]
    g_g_pre = small[:, 0]
    g_g_post = small[:, 1]
    g_b_ada = small[:, 2:5].reshape(L, 3 * D)
    g_pscale = small[:, 5, 0:pool_scale.shape[1]]
    g_w_conv = small[:, 6:8].reshape(L, 2 * D)[:, 0:N_CHIPS * 3 * LANES].reshape(L, N_CHIPS, 3, LANES)
    g_w_conv = lax.dynamic_index_in_dim(g_w_conv, chip, axis=1, keepdims=False)
    dmod_all = packs_all.reshape(N_DEV, L, SUBLANES, D)[:, :, 2:5].reshape(N_DEV, L, 3 * D)
    dmod_my = jnp.transpose(lax.dynamic_slice_in_dim(dmod_all, chip * CW, CW, axis=2), (1, 0, 2))

    g_w_ada, d_w_ada, nm_w_ada, nv_w_ada = ada_finish(c_all, dmod_my, w_ada, m_w_ada, v_w_ada)
    in_blk, out_blk = (1, D // 2, CW), (1, RO, D)
    upd_in = adamw(w_in, g_w_in, m_w_in, v_w_in, in_blk, "adamw_w_in", 1, L - 1)
    upd_out = adamw(w_out, g_w_out, m_w_out, v_w_out, out_blk, "adamw_w_out", 1, L - 1)

    def small_adamw(w, g, m, v, name):
        shp = (1,) + w.shape if w.ndim == 2 else w.shape
        outs = adamw(w.reshape(shp), g.reshape(shp), m.reshape(shp), v.reshape(shp), shp, name)
        return [a.reshape(w.shape) for a in outs]

    d_b_ada, nm_b_ada, nv_b_ada = small_adamw(b_ada, g_b_ada, m_b_ada, v_b_ada, "adamw_b_ada")
    d_g_pre, nm_g_pre, nv_g_pre = small_adamw(g_pre, g_g_pre, m_g_pre, v_g_pre, "adamw_g_pre")
    d_w_conv, nm_w_conv, nv_w_conv = small_adamw(w_conv, g_w_conv, m_w_conv, v_w_conv, "adamw_w_conv")
    d_pscale, nm_pscale, nv_pscale = small_adamw(pool_scale, g_pscale, m_pool_scale, v_pool_scale, "adamw_pool_scale")
    d_g_post, nm_g_post, nv_g_post = small_adamw(g_post, g_g_post, m_g_post, v_g_post, "adamw_g_post")

    fl, f_parts, f_lands, f_ssem, f_rsem = flying
    done = [nv_w_ada, upd_in[2], upd_out[2], nv_b_ada, nv_g_pre, nv_w_conv, nv_pscale, nv_g_post]
    f_parts, f_lands = chip_wait(f_parts, f_lands, f_ssem, f_rsem, done)
    acc = sum_chips(pos, f_parts, f_lands, (g_w_in, g_w_out, g_w_pool), fl, shapes)
    g_w_in, g_w_out, g_w_pool = spread_reduced(*acc, (0,), tuple(range(L)), acc[0])
    d_w_in, nm_w_in, nv_w_in = adamw(w_in, g_w_in, m_w_in, v_w_in, in_blk, "adamw_w_in", 0, 1, upd_in)
    d_w_out, nm_w_out, nv_w_out = adamw(w_out, g_w_out, m_w_out, v_w_out, out_blk, "adamw_w_out", 0, 1, upd_out)
    pshape = (L, N_CHIPS * LANES, LANES)
    d_w_pool, nm_w_pool, nv_w_pool = adamw(w_pool.reshape(pshape), g_w_pool.reshape(pshape), m_w_pool.reshape(pshape),
                                           v_w_pool.reshape(pshape), (1,) + pshape[1:], "adamw_w_pool")
    d_w_pool, nm_w_pool, nv_w_pool = [a.reshape(w_pool.shape) for a in (d_w_pool, nm_w_pool, nv_w_pool)]

    return (loss, grad_x,
            g_w_ada, g_b_ada, g_g_pre, g_w_in, g_w_conv, g_w_pool, g_pscale, g_w_out, g_g_post,
            d_w_ada, d_b_ada, d_g_pre, d_w_in, d_w_conv, d_w_pool, d_pscale, d_w_out, d_g_post,
            nm_w_ada, nm_b_ada, nm_g_pre, nm_w_in, nm_w_conv, nm_w_pool, nm_pscale, nm_w_out, nm_g_post,
            nv_w_ada, nv_b_ada, nv_g_pre, nv_w_in, nv_w_conv, nv_w_pool, nv_pscale, nv_w_out, nv_g_post)
```

```python
import functools

import jax
import jax.numpy as jnp
from jax import lax
from jax.experimental import pallas as pl
from jax.experimental.pallas import tpu as pltpu

F32 = jnp.float32
BF16 = jnp.bfloat16
MESH = pl.DeviceIdType.MESH
ANY = pl.BlockSpec(memory_space=pl.ANY)

NORM_EPS = 1e-6
POOL_WINDOWS = (2, 4, 8, 16)
ADAM_LR = 0.001
ADAM_B1 = 0.9
ADAM_B2 = 0.999
ADAM_EPS = 1e-08
ADAM_WD = 0.01
ADAM_STEP = 10

N_CHIPS = 4
N_DEV = 8
LANES = 128
SUBLANES = 8
VMEM_BIG = 56 * 1024 * 1024
HIST = 16
R_CONV = 32
R_POOL = 64

NT = (((1,), (1,)), ((), ()))
TN = (((0,), (0,)), ((), ()))


def _params(vmem=None, n_grid=1):
    kw = {}
    if n_grid:
        kw["dimension_semantics"] = ("arbitrary",) * n_grid
    if vmem is not None:
        kw["vmem_limit_bytes"] = vmem
    return pltpu.CompilerParams(**kw)


def _colsum8(v):
    n, d = v.shape
    return v.reshape(n // SUBLANES, SUBLANES, d).sum(axis=0)


def _rms(v):
    return lax.rsqrt(jnp.mean(v * v, axis=-1, keepdims=True) + NORM_EPS)


def _shift_down(ext, k, rows):
    if k == 0:
        return ext[HIST:HIST + rows]
    return pltpu.roll(ext, k, 0)[HIST:HIST + rows]


def _shift_up(ext, k, rows):
    if k == 0:
        return ext[0:rows]
    return pltpu.roll(ext, ext.shape[0] - k, 0)[0:rows]


def _load_ext(ref, r0, h0, first, rows):
    hist = ref[pl.ds(h0, HIST), :].astype(F32)
    hist = jnp.where(first, 0.0, hist)
    cur = ref[pl.ds(r0, rows), :].astype(F32)
    return jnp.concatenate([hist, cur], axis=0)


def _me():
    return lax.axis_index("x"), lax.axis_index("y"), lax.axis_index("c")


def cast_weights(pos, w_in, w_out, l, after):
    _, D, CW = w_in.shape
    RO = w_out.shape[1]

    def body(pos_ref, wi, wo, after_ref, oi, oo):
        oi[...] = wi[...].astype(BF16)
        oo[...] = wo[...].astype(BF16)

    return pl.pallas_call(
        body, name="cast_w",
        grid_spec=pltpu.PrefetchScalarGridSpec(
            num_scalar_prefetch=1, grid=(2,),
            in_specs=[pl.BlockSpec((None, D // 2, CW), lambda h, p: (l, h, 0)),
                      pl.BlockSpec((None, RO // 2, D), lambda h, p: (l, h, 0)), ANY],
            out_specs=[pl.BlockSpec((None, D // 2, CW), lambda h, p: (p[1], h, 0)),
                       pl.BlockSpec((None, RO // 2, D), lambda h, p: (p[1], h, 0))]),
        out_shape=[jax.ShapeDtypeStruct((N_CHIPS, D, CW), BF16), jax.ShapeDtypeStruct((N_CHIPS, RO, D), BF16)],
        compiler_params=_params(),
    )(pos, w_in, w_out, after)


def mod_part(c_all, w_ada, b_my, after):
    L, D, CW = w_ada.shape

    def body(c_ref, w_ref, b_ref, after_ref, o_ref):
        cv = c_ref[...]
        ca = (cv * jax.nn.sigmoid(cv)).astype(BF16)
        o_ref[0] = jnp.dot(ca, w_ref[0].astype(BF16), preferred_element_type=F32) + b_ref[0]

    return pl.pallas_call(
        body, name="mod_part", grid=(L,),
        in_specs=[pl.BlockSpec((N_DEV, D), lambda l: (0, 0)),
                  pl.BlockSpec((1, D, CW), lambda l: (l, 0, 0)),
                  pl.BlockSpec((1, 1, CW), lambda l: (l, 0, 0)), ANY],
        out_specs=pl.BlockSpec((1, N_DEV, CW), lambda l: (l, 0, 0)),
        out_shape=jax.ShapeDtypeStruct((L, N_DEV, CW), F32),
        compiler_params=_params(VMEM_BIG),
    )(c_all, w_ada, b_my.reshape(L, 1, CW), after)


def _mod_row(l, k, D):
    return pl.BlockSpec((None, None, 1, D), lambda *_: (l, k, 0, 0))


def _layer_row(l, D):
    return pl.BlockSpec((None, 1, D), lambda *_: (l, 0, 0))


def proj_fwd(x, mod4, g_pre3, wg, l):
    T, D = x.shape
    NB, _, CW = wg.shape
    tm = 512

    def body(x_ref, sh_ref, sc_ref, g_ref, w_ref, o_ref):
        xv = x_ref[...]
        h = (xv * _rms(xv) * g_ref[...]) * (1.0 + sc_ref[...]) + sh_ref[...]
        hb = h.astype(BF16)
        for j in range(NB):
            o_ref[:, j * CW:(j + 1) * CW] = jnp.dot(hb, w_ref[j], preferred_element_type=F32).astype(BF16)

    return pl.pallas_call(
        body, name="proj_fwd", grid=(T // tm,),
        in_specs=[pl.BlockSpec((tm, D), lambda i: (i, 0)), _mod_row(l, 0, D), _mod_row(l, 1, D), _layer_row(l, D),
                  pl.BlockSpec((NB, D, CW), lambda i: (0, 0, 0))],
        out_specs=pl.BlockSpec((tm, NB * CW), lambda i: (i, 0)),
        out_shape=jax.ShapeDtypeStruct((T, NB * CW), BF16),
        compiler_params=_params(VMEM_BIG),
    )(x, mod4, mod4, g_pre3, wg)


def conv_fwd(proj, wconv, l):
    T = proj.shape[0]
    R = R_CONV
    nblk = 4

    def body(u_ref, b_ref, c_ref, g_ref, w_ref, o_ref):
        w0 = w_ref[pl.ds(0, 1), :]
        w1 = w_ref[pl.ds(1, 1), :]
        w2 = w_ref[pl.ds(2, 1), :]

        def chunk(i, carry):
            r0 = pl.multiple_of(i * R, R)
            h0 = pl.multiple_of(jnp.maximum(r0 - HIST, 0), HIST)
            first = i == 0
            ca = _load_ext(c_ref, r0, h0, first, R) * _load_ext(u_ref, r0, h0, first, R)
            conv = w2 * ca[HIST:] + w1 * _shift_down(ca, 1, R) + w0 * _shift_down(ca, 2, R)
            g = g_ref[pl.ds(r0, R), :].astype(F32)
            b = b_ref[pl.ds(r0, R), :].astype(F32)
            o_ref[pl.ds(r0, R), :] = (b * conv * (g * jax.nn.sigmoid(g))).astype(BF16)
            return carry

        lax.fori_loop(0, T // R, chunk, 0)

    def col(off):
        return pl.BlockSpec((T, LANES), lambda j: (0, j + off))

    return pl.pallas_call(
        body, name="conv_fwd", grid=(nblk,),
        in_specs=[col(0), col(4), col(8), col(12), pl.BlockSpec((None, None, 3, LANES), lambda j: (j, l, 0, 0))],
        out_specs=pl.BlockSpec((T, LANES), lambda j: (0, j)),
        out_shape=jax.ShapeDtypeStruct((T, nblk * LANES), BF16),
        compiler_params=_params(),
    )(proj, proj, proj, proj, wconv)


def _causal_window_sum(ext, w):
    s, k = ext, 1
    while k < w:
        s = s + pltpu.roll(s, k, 0)
        k *= 2
    return s


def _anticausal_window_sum(ext, w):
    s, k = ext, 1
    n = ext.shape[0]
    while k < w:
        s = s + pltpu.roll(s, n - k, 0)
        k *= 2
    return s


def _count(r0, rows, w):
    t = r0 + lax.broadcasted_iota(jnp.int32, (rows, LANES), 0)
    return jnp.minimum(t + 1, w).astype(F32)


def _pooled_loop(p_ref, pooled_s, w, T):
    R = R_POOL

    def chunk(i, carry):
        r0 = pl.multiple_of(i * R, R)
        h0 = pl.multiple_of(jnp.maximum(r0 - HIST, 0), HIST)
        ext = _load_ext(p_ref, r0, h0, i == 0, R)
        ws = _causal_window_sum(ext, w)[HIST:]
        pooled_s[pl.ds(r0, R), :] = (ws / _count(r0, R, w) - ext[HIST:]).astype(BF16)
        return carry

    lax.fori_loop(0, T // R, chunk, 0)


def _pool_w_spec(l):
    return pl.BlockSpec((None, None, LANES, LANES), lambda j: (l, j, 0, 0))


def _pool_s_spec(l):
    return pl.BlockSpec((None, 1, LANES), lambda j: (l, 0, j))


def pool_fwd(proj, wpool, pscale3, l):
    T = proj.shape[0]
    R = R_POOL
    ngrp = len(POOL_WINDOWS)

    def body(p_ref, g_ref, w_ref, s_ref, o_ref, pooled_s, mixed_s):
        grp = pl.program_id(0)

        def group(w):
            _pooled_loop(p_ref, pooled_s, w, T)
            mixed_s[...] = jnp.dot(pooled_s[...], w_ref[...].astype(BF16), preferred_element_type=F32)
            sc = s_ref[...]

            def chunk(i, carry):
                r0 = pl.multiple_of(i * R, R)
                g = g_ref[pl.ds(r0, R), :].astype(F32)
                o_ref[pl.ds(r0, R), :] = (mixed_s[pl.ds(r0, R), :] * sc * (g * jax.nn.sigmoid(g))).astype(BF16)
                return carry

            lax.fori_loop(0, T // R, chunk, 0)

        for k, w in enumerate(POOL_WINDOWS):
            pl.when(grp == k)(functools.partial(group, w))

    return pl.pallas_call(
        body, name="pool_fwd", grid=(ngrp,),
        in_specs=[pl.BlockSpec((T, LANES), lambda j: (0, j + 16)), pl.BlockSpec((T, LANES), lambda j: (0, j + 20)),
                  _pool_w_spec(l), _pool_s_spec(l)],
        out_specs=pl.BlockSpec((T, LANES), lambda j: (0, j)),
        out_shape=jax.ShapeDtypeStruct((T, ngrp * LANES), BF16),
        scratch_shapes=[pltpu.VMEM((T, LANES), BF16), pltpu.VMEM((T, LANES), F32)],
        compiler_params=_params(),
    )(proj, proj, wpool, pscale3)


def out_fwd(ya, yp, wo, x, mod4, g_post3, l, after):
    T, D = x.shape
    H = ya.shape[1]
    tm = 512

    def body(ya_ref, yp_ref, wo_ref, x_ref, gt_ref, g_ref, after_ref, xn_ref, y_ref):
        y = (jnp.dot(ya_ref[...], wo_ref[0:H, :], preferred_element_type=F32)
             + jnp.dot(yp_ref[...], wo_ref[H:2 * H, :], preferred_element_type=F32))
        xn_ref[...] = x_ref[...] + gt_ref[...] * (y * _rms(y) * g_ref[...])
        y_ref[...] = y

    tile = pl.BlockSpec((tm, D), lambda i: (i, 0))
    half = pl.BlockSpec((tm, H), lambda i: (i, 0))
    return pl.pallas_call(
        body, name="out_fwd", grid=(T // tm,),
        in_specs=[half, half, pl.BlockSpec((2 * H, D), lambda i: (0, 0)), tile, _mod_row(l, 2, D), _layer_row(l, D),
                  ANY],
        out_specs=[tile, tile],
        out_shape=[jax.ShapeDtypeStruct((T, D), F32), jax.ShapeDtypeStruct((T, D), F32)],
        compiler_params=_params(VMEM_BIG),
    )(ya, yp, wo, x, mod4, g_post3, after)


def loss_head(xl, target):
    T, D = xl.shape
    tm = 512
    nt = T // tm

    def body(x_ref, t_ref, dx_ref, l_ref, acc):
        i = pl.program_id(0)

        @pl.when(i == 0)
        def _():
            acc[...] = jnp.zeros_like(acc)

        d = x_ref[...] - t_ref[...]
        dx_ref[...] = d * (1.0 / D)
        acc[...] += _colsum8(d * d)

        @pl.when(i == nt - 1)
        def _():
            l_ref[...] = jnp.zeros_like(l_ref) + jnp.sum(acc[...]) * (0.5 / D)

    tile = pl.BlockSpec((tm, D), lambda i: (i, 0))
    return pl.pallas_call(
        body, name="loss_head", grid=(nt,),
        in_specs=[tile, tile],
        out_specs=[tile, pl.BlockSpec((SUBLANES, LANES), lambda i: (0, 0))],
        out_shape=[jax.ShapeDtypeStruct((T, D), F32), jax.ShapeDtypeStruct((SUBLANES, LANES), F32)],
        scratch_shapes=[pltpu.VMEM((SUBLANES, D), F32)],
        compiler_params=_params(VMEM_BIG),
    )(xl, target)


def out_bwd(dx, y, ya, yp, wo, mod4, g_post3, l, after):
    T, D = dx.shape
    H = ya.shape[1]
    tm = 512
    nt = T // tm

    def body(dx_ref, y_ref, ya_ref, yp_ref, wo_ref, gt_ref, g_ref, after_ref,
             dya_ref, dyp_ref, dwo_ref, dgt_ref, dg_ref, acc_w, acc_gt, acc_g):
        i = pl.program_id(0)

        @pl.when(i == 0)
        def _():
            acc_w[...] = jnp.zeros_like(acc_w)
            acc_gt[...] = jnp.zeros_like(acc_gt)
            acc_g[...] = jnp.zeros_like(acc_g)

        yv = y_ref[...]
        dxv = dx_ref[...]
        g = g_ref[...]
        r = _rms(yv)
        yn = yv * r
        acc_gt[...] += _colsum8(dxv * (yn * g))
        dn = dxv * gt_ref[...]
        acc_g[...] += _colsum8(dn * yn)
        a = dn * g
        dy = r * (a - yn * jnp.mean(a * yn, axis=-1, keepdims=True))
        dyb = dy.astype(BF16)
        dyc = lax.dot_general(dyb, wo_ref[...], NT, preferred_element_type=F32)
        dya_ref[...] = dyc[:, 0:H].astype(BF16)
        dyp_ref[...] = dyc[:, H:2 * H].astype(BF16)
        acc_w[0:H, :] += lax.dot_general(ya_ref[...], dyb, TN, preferred_element_type=F32)
        acc_w[H:2 * H, :] += lax.dot_general(yp_ref[...], dyb, TN, preferred_element_type=F32)

        @pl.when(i == nt - 1)
        def _():
            dwo_ref[...] = acc_w[...].astype(BF16)
            dgt_ref[...] = jnp.sum(acc_gt[...], axis=0, keepdims=True)
            dg_ref[...] = jnp.sum(acc_g[...], axis=0, keepdims=True)

    row = pl.BlockSpec((1, D), lambda i: (0, 0))
    tile = pl.BlockSpec((tm, D), lambda i: (i, 0))
    half = pl.BlockSpec((tm, H), lambda i: (i, 0))
    full = pl.BlockSpec((2 * H, D), lambda i: (0, 0))
    return pl.pallas_call(
        body, name="out_bwd", grid=(nt,),
        in_specs=[tile, tile, half, half, full, _mod_row(l, 2, D), _layer_row(l, D), ANY],
        out_specs=[half, half, full, row, row],
        out_shape=[jax.ShapeDtypeStruct((T, H), BF16), jax.ShapeDtypeStruct((T, H), BF16),
                   jax.ShapeDtypeStruct((2 * H, D), BF16),
                   jax.ShapeDtypeStruct((1, D), F32), jax.ShapeDtypeStruct((1, D), F32)],
        scratch_shapes=[pltpu.VMEM((2 * H, D), F32), pltpu.VMEM((SUBLANES, D), F32), pltpu.VMEM((SUBLANES, D), F32)],
        compiler_params=_params(VMEM_BIG),
    )(dx, y, ya, yp, wo, mod4, g_post3, after)


def conv_bwd(proj, dya, wconv, l):
    T = proj.shape[0]
    R = R_CONV
    nblk = 4
    nchunk = T // R

    def body(u_ref, b_ref, c_ref, g_ref, dy_ref, w_ref, du_ref, db_ref, dc_ref, dg_ref, dw_ref):
        w0 = w_ref[pl.ds(0, 1), :]
        w1 = w_ref[pl.ds(1, 1), :]
        w2 = w_ref[pl.ds(2, 1), :]

        def chunk(k, carry):
            head, a0, a1, a2 = carry
            i = nchunk - 1 - k
            r0 = pl.multiple_of(i * R, R)
            h0 = pl.multiple_of(jnp.maximum(r0 - HIST, 0), HIST)
            first = i == 0
            ue = _load_ext(u_ref, r0, h0, first, R)
            ce = _load_ext(c_ref, r0, h0, first, R)
            ca = ce * ue
            ca0 = ca[HIST:]
            ca1 = _shift_down(ca, 1, R)
            ca2 = _shift_down(ca, 2, R)
            conv = w2 * ca0 + w1 * ca1 + w0 * ca2
            g = g_ref[pl.ds(r0, R), :].astype(F32)
            b = b_ref[pl.ds(r0, R), :].astype(F32)
            dy = dy_ref[pl.ds(r0, R), :].astype(F32)
            sg = jax.nn.sigmoid(g)
            sl = g * sg
            t = dy * conv
            db_ref[pl.ds(r0, R), :] = (t * sl).astype(BF16)
            dg_ref[pl.ds(r0, R), :] = (t * b * (sg * (1.0 + g * (1.0 - sg)))).astype(BF16)
            dconv = dy * b * sl
            a2 = a2 + _colsum8(dconv * ca0)
            a1 = a1 + _colsum8(dconv * ca1)
            a0 = a0 + _colsum8(dconv * ca2)
            e = jnp.concatenate([dconv, head], axis=0)
            dca = w2 * dconv + w1 * _shift_up(e, 1, R) + w0 * _shift_up(e, 2, R)
            du_ref[pl.ds(r0, R), :] = (dca * ce[HIST:]).astype(BF16)
            dc_ref[pl.ds(r0, R), :] = (dca * ue[HIST:]).astype(BF16)
            return dconv[0:SUBLANES], a0, a1, a2

        z = jnp.zeros((SUBLANES, LANES), F32)
        _, a0, a1, a2 = lax.fori_loop(0, nchunk, chunk, (z, z, z, z))
        dw_ref[pl.ds(0, 1), :] = jnp.sum(a0, axis=0, keepdims=True)
        dw_ref[pl.ds(1, 1), :] = jnp.sum(a1, axis=0, keepdims=True)
        dw_ref[pl.ds(2, 1), :] = jnp.sum(a2, axis=0, keepdims=True)

    def col(off):
        return pl.BlockSpec((T, LANES), lambda j: (0, j + off))

    sec = jax.ShapeDtypeStruct((T, nblk * LANES), BF16)
    return pl.pallas_call(
        body, name="conv_bwd", grid=(nblk,),
        in_specs=[col(0), col(4), col(8), col(12), col(0), pl.BlockSpec((None, None, 3, LANES), lambda j: (j, l, 0, 0))],
        out_specs=[col(0), col(0), col(0), col(0), pl.BlockSpec((None, 3, LANES), lambda j: (j, 0, 0))],
        out_shape=[sec, sec, sec, sec, jax.ShapeDtypeStruct((nblk, 3, LANES), F32)],
        compiler_params=_params(),
    )(proj, proj, proj, proj, dya, wconv)


def pool_bwd(proj, dyp, wpool, pscale3, l, after):
    T = proj.shape[0]
    R = R_POOL
    ngrp = len(POOL_WINDOWS)
    nchunk = T // R

    def body(p_ref, g_ref, dy_ref, w_ref, s_ref, after_ref, du_ref, dg_ref, dw_ref, ds_ref,
             pooled_s, mixed_s, dmix_s, dpool_s):
        grp = pl.program_id(0)

        def group(w):
            wb = w_ref[...].astype(BF16)
            _pooled_loop(p_ref, pooled_s, w, T)
            mixed_s[...] = jnp.dot(pooled_s[...], wb, preferred_element_type=F32)
            sc = s_ref[...]

            def gate_chunk(i, acc):
                r0 = pl.multiple_of(i * R, R)
                g = g_ref[pl.ds(r0, R), :].astype(F32)
                dy = dy_ref[pl.ds(r0, R), :].astype(F32)
                mixed = mixed_s[pl.ds(r0, R), :]
                sg = jax.nn.sigmoid(g)
                dg_ref[pl.ds(r0, R), :] = (dy * mixed * sc * (sg * (1.0 + g * (1.0 - sg)))).astype(BF16)
                dms = dy * (g * sg)
                dmix_s[pl.ds(r0, R), :] = (dms * sc).astype(BF16)
                return acc + _colsum8(dms * mixed)

            acc = lax.fori_loop(0, nchunk, gate_chunk, jnp.zeros((SUBLANES, LANES), F32))
            ds_ref[...] = jnp.sum(acc, axis=0, keepdims=True)
            dpool_s[pl.ds(0, T), :] = lax.dot_general(dmix_s[...], wb, NT, preferred_element_type=F32)
            dpool_s[pl.ds(T, HIST), :] = jnp.zeros((HIST, LANES), F32)
            dw_ref[...] = lax.dot_general(pooled_s[...], dmix_s[...], TN, preferred_element_type=F32).astype(BF16)

            def back_chunk(i, carry):
                r0 = pl.multiple_of(i * R, R)
                dpe = dpool_s[pl.ds(r0, R + HIST), :]
                e = dpe / _count(r0, R + HIST, w)
                du_ref[pl.ds(r0, R), :] = (_anticausal_window_sum(e, w)[0:R] - dpe[0:R]).astype(BF16)
                return carry

            lax.fori_loop(0, nchunk, back_chunk, 0)

        for k, w in enumerate(POOL_WINDOWS):
            pl.when(grp == k)(functools.partial(group, w))

    def col(off):
        return pl.BlockSpec((T, LANES), lambda j: (0, j + off))

    sec = jax.ShapeDtypeStruct((T, ngrp * LANES), BF16)
    wspec = pl.BlockSpec((None, LANES, LANES), lambda j: (j, 0, 0))
    sspec = pl.BlockSpec((1, LANES), lambda j: (0, j))
    return pl.pallas_call(
        body, name="pool_bwd", grid=(ngrp,),
        in_specs=[col(16), col(20), col(0), _pool_w_spec(l), _pool_s_spec(l), ANY],
        out_specs=[col(0), col(0), wspec, sspec],
        out_shape=[sec, sec, jax.ShapeDtypeStruct((ngrp, LANES, LANES), BF16),
                   jax.ShapeDtypeStruct((1, ngrp * LANES), F32)],
        scratch_shapes=[pltpu.VMEM((T, LANES), BF16), pltpu.VMEM((T, LANES), F32),
                        pltpu.VMEM((T, LANES), BF16), pltpu.VMEM((T + HIST, LANES), F32)],
        compiler_params=_params(),
    )(proj, proj, dyp, wpool, pscale3, after)


def in_bwd(dsecs, wg, x, dxo, mod4, g_pre3, l):
    T, D = x.shape
    NB, _, CW = wg.shape
    SW = dsecs[0].shape[1]
    nsec = len(dsecs)
    tm = 256
    nt = T // tm

    def body(*refs):
        d_refs = refs[0:nsec]
        w_ref, x_ref, dxo_ref, sh_ref, sc_ref, g_ref = refs[nsec:nsec + 6]
        dxi_ref, dw_ref, dsh_ref, dsc_ref, dg_ref = refs[nsec + 6:nsec + 11]
        dp_s, acc_w, acc_sh, acc_sc, acc_g = refs[nsec + 11:]
        i = pl.program_id(0)

        @pl.when(i == 0)
        def _():
            acc_w[...] = jnp.zeros_like(acc_w)
            acc_sh[...] = jnp.zeros_like(acc_sh)
            acc_sc[...] = jnp.zeros_like(acc_sc)
            acc_g[...] = jnp.zeros_like(acc_g)

        for s in range(nsec):
            dp_s[:, s * SW:(s + 1) * SW] = d_refs[s][...]
        xv = x_ref[...]
        g = g_ref[...]
        r = _rms(xv)
        xh = xv * r
        n = xh * g
        sc1 = 1.0 + sc_ref[...]
        hb = (n * sc1 + sh_ref[...]).astype(BF16)
        dh = lax.dot_general(dp_s[:, 0:CW], w_ref[0], NT, preferred_element_type=F32)
        for j in range(1, NB):
            dh = dh + lax.dot_general(dp_s[:, j * CW:(j + 1) * CW], w_ref[j], NT, preferred_element_type=F32)
        for j in range(NB):
            acc_w[j] += lax.dot_general(hb, dp_s[:, j * CW:(j + 1) * CW], TN, preferred_element_type=F32)
        acc_sh[...] += _colsum8(dh)
        acc_sc[...] += _colsum8(dh * n)
        dnp = dh * sc1
        acc_g[...] += _colsum8(dnp * xh)
        a = dnp * g
        dxi_ref[...] = dxo_ref[...] + r * (a - xh * jnp.mean(a * xh, axis=-1, keepdims=True))

        @pl.when(i == nt - 1)
        def _():
            dw_ref[...] = acc_w[...].astype(BF16)
            dsh_ref[...] = jnp.sum(acc_sh[...], axis=0, keepdims=True)
            dsc_ref[...] = jnp.sum(acc_sc[...], axis=0, keepdims=True)
            dg_ref[...] = jnp.sum(acc_g[...], axis=0, keepdims=True)

    row = pl.BlockSpec((1, D), lambda i: (0, 0))
    tile = pl.BlockSpec((tm, D), lambda i: (i, 0))
    sect = pl.BlockSpec((tm, SW), lambda i: (i, 0))
    wspec = pl.BlockSpec((NB, D, CW), lambda i: (0, 0, 0))
    rowshape = jax.ShapeDtypeStruct((1, D), F32)
    return pl.pallas_call(
        body, name="in_bwd", grid=(nt,),
        in_specs=[sect] * nsec + [wspec, tile, tile, _mod_row(l, 0, D), _mod_row(l, 1, D), _layer_row(l, D)],
        out_specs=[tile, wspec, row, row, row],
        out_shape=[jax.ShapeDtypeStruct((T, D), F32), jax.ShapeDtypeStruct((NB, D, CW), BF16),
                   rowshape, rowshape, rowshape],
        scratch_shapes=[pltpu.VMEM((tm, nsec * SW), BF16), pltpu.VMEM((NB, D, CW), F32),
                        pltpu.VMEM((SUBLANES, D), F32), pltpu.VMEM((SUBLANES, D), F32), pltpu.VMEM((SUBLANES, D), F32)],
        compiler_params=_params(VMEM_BIG),
    )(*dsecs, wg, x, dxo, mod4, mod4, g_pre3)


def _rcopy(src, dst, ssem, rsem, dev):
    return pltpu.make_async_remote_copy(src_ref=src, dst_ref=dst, send_sem=ssem, recv_sem=rsem,
                                        device_id=dev, device_id_type=MESH)


def _peers7(x, y, c):
    out = []
    for m in range(1, N_DEV):
        bx, by, bc = (m >> 2) & 1, (m >> 1) & 1, m & 1
        out.append(((1 - x) if bx else x, (1 - y) if by else y, (1 - c) if bc else c))
    return out


HBM = pl.BlockSpec(memory_space=pltpu.HBM)
SEM = pl.BlockSpec(memory_space=pltpu.SEMAPHORE)
SPLIT = pltpu.CompilerParams(has_side_effects=pltpu.SideEffectType.DATAFLOW_SIDE_EFFECTING)


def _hbm(a):
    return pltpu.with_memory_space_constraint(a, pltpu.HBM)


def _chips(x, y):
    return [(1 - x, y), (x, 1 - y), (1 - x, 1 - y)]


def xchg_start(name, bufs, n_copies, plan):
    n = len(bufs)

    def body(*refs):
        ssem, rsem, token = refs[n], refs[n + 1], refs[-1]
        copies = plan(refs[0:n], *_me())
        assert len(copies) == n_copies
        for k, (src, dst, peer, _) in enumerate(copies):
            _rcopy(src, dst, ssem.at[k], rsem.at[k], peer).start()
        token[...] = jnp.zeros_like(token)

    outs = pl.pallas_call(
        body, name=name,
        in_specs=[HBM] * n,
        out_specs=[SEM, SEM] + [HBM] * n + [pl.BlockSpec(memory_space=pltpu.VMEM)],
        out_shape=([pltpu.SemaphoreType.DMA((n_copies,))] * 2 + [pltpu.HBM(b.shape, b.dtype) for b in bufs]
                   + [jax.ShapeDtypeStruct((SUBLANES, LANES), F32)]),
        input_output_aliases={a: 2 + a for a in range(n)},
        compiler_params=SPLIT,
    )(*[_hbm(b) for b in bufs])
    return outs[0], outs[1], list(outs[2:2 + n]), outs[-1]


def xchg_wait(name, bufs, ssem, rsem, n_copies, plan, after):
    n = len(bufs)
    after = list(after)

    def body(*refs):
        ssem_ref, rsem_ref = refs[n], refs[n + 1]
        copies = plan(refs[0:n], *_me())
        assert len(copies) == n_copies
        for k, (src, _, peer, land) in enumerate(copies):
            cp = _rcopy(src, land, ssem_ref.at[k], rsem_ref.at[k], peer)
            cp.wait_send()
            cp.wait_recv()

    outs = pl.pallas_call(
        body, name=name,
        in_specs=[HBM] * n + [SEM, SEM] + [ANY] * len(after), out_specs=[HBM] * n,
        out_shape=[pltpu.HBM(b.shape, b.dtype) for b in bufs],
        input_output_aliases={a: a for a in range(n)},
        compiler_params=SPLIT,
    )(*bufs, ssem, rsem, *after)
    return list(outs)


def plan_gather(refs, x, y, c):
    out = []
    for (px, py) in _chips(x, y):
        for buf in refs:
            h = buf.shape[1] // 2
            rows = pl.ds(c * h, h)
            own = buf.at[2 * x + y, rows]
            out.append((own, own, (px, py, c), buf.at[2 * px + py, rows]))
    return out


def plan_forward(refs, x, y, c):
    out = []
    for (px, py) in _chips(x, y):
        for buf in refs:
            h = buf.shape[1] // 2
            landed = buf.at[2 * px + py, pl.ds(c * h, h)]
            out.append((landed, landed, (x, y, 1 - c), buf.at[2 * px + py, pl.ds((1 - c) * h, h)]))
    return out


def plan_sibling(refs, x, y, c):
    n = len(refs) // 2
    out = []
    for a in range(n):
        h = refs[a].shape[1] // 2
        out.append((refs[a].at[:, pl.ds((1 - c) * h, h)], refs[n + a], (x, y, 1 - c), refs[n + a]))
    return out


def plan_chip(refs, x, y, c):
    n = len(refs) // 2
    out = []
    for j, (px, py) in enumerate(_chips(x, y)):
        for a in range(n):
            out.append((refs[a].at[2 * px + py], refs[n + a].at[j], (px, py, c), refs[n + a].at[j]))
    return out


def plan_pack(refs, x, y, c):
    (packs,) = refs
    mine = packs.at[4 * x + 2 * y + c]
    return [(mine, mine, peer, packs.at[4 * peer[0] + 2 * peer[1] + peer[2]]) for peer in _peers7(x, y, c)]


def plan_spread(layers, wp_layers):
    def plan(refs, x, y, c):
        gi, go, gp = refs
        hD, hR, hP = gi.shape[1] // 2, go.shape[1] // 2, gp.shape[2] // 2
        sib = (x, y, 1 - c)
        out = []
        for l in layers:
            mine = gi.at[l, pl.ds(c * hD, hD)]
            out.append((mine, mine, sib, gi.at[l, pl.ds((1 - c) * hD, hD)]))
            mine = go.at[l, pl.ds(c * hR, hR)]
            out.append((mine, mine, sib, go.at[l, pl.ds((1 - c) * hR, hR)]))
        for l in wp_layers:
            mine = gp.at[l, 2 * x + y, pl.ds(c * hP, hP)]
            for peer in _peers7(x, y, c):
                out.append((mine, mine, peer, gp.at[l, 2 * peer[0] + peer[1], pl.ds(peer[2] * hP, hP)]))
        return out

    return plan


def gather_small(c8, wc, token):
    def body(c_ref, wc_ref, token_ref, call, wcall, ssem, rsem, lsem):
        x, y, c = _me()
        myc = 2 * x + y
        me_lin = 4 * x + 2 * y + c
        me = (x, y, c)
        local = [pltpu.make_async_copy(c_ref, call.at[me_lin], lsem.at[0]),
                 pltpu.make_async_copy(wc_ref, wcall.at[myc], lsem.at[1])]
        for cp in local:
            cp.start()
        sends, recvs = [], []
        for m, peer in enumerate(_peers7(x, y, c)):
            plin = 4 * peer[0] + 2 * peer[1] + peer[2]
            sends.append(_rcopy(c_ref, call.at[me_lin], ssem.at[m], rsem.at[m], peer))
            recvs.append(_rcopy(call.at[plin], call.at[plin], ssem.at[m], rsem.at[m], me))
        for j, (px, py) in enumerate([(1 - x, y), (x, 1 - y), (1 - x, 1 - y)]):
            pc = 2 * px + py
            sends.append(_rcopy(wc_ref, wcall.at[myc], ssem.at[7 + j], rsem.at[7 + j], (px, py, c)))
            recvs.append(_rcopy(wcall.at[pc], wcall.at[pc], ssem.at[7 + j], rsem.at[7 + j], me))
        for cp in sends:
            cp.start()
        for cp in recvs:
            cp.wait_recv()
        for cp in sends:
            cp.wait_send()
        for cp in local:
            cp.wait()

    return pl.pallas_call(
        body, name="gather_small",
        in_specs=[ANY] * 3, out_specs=[ANY] * 2,
        out_shape=[jax.ShapeDtypeStruct((N_DEV, SUBLANES, LANES), F32),
                   jax.ShapeDtypeStruct((N_CHIPS, wc.shape[0], 3, LANES), F32)],
        scratch_shapes=[pltpu.SemaphoreType.DMA((10,)), pltpu.SemaphoreType.DMA((10,)), pltpu.SemaphoreType.DMA((2,))],
        compiler_params=_params(n_grid=0),
    )(c8, wc, token)


def exchange_mod(part):
    def body(p_ref, o_ref, ssem, rsem, lsem):
        x, y, c = _me()
        myc = 2 * x + y
        chips = [(1 - x, y), (x, 1 - y), (1 - x, 1 - y)]
        loc = pltpu.make_async_copy(p_ref, o_ref.at[myc], lsem)
        loc.start()
        sends = [_rcopy(p_ref, o_ref.at[myc], ssem.at[j], rsem.at[j], (px, py, c)) for j, (px, py) in enumerate(chips)]
        for cp in sends:
            cp.start()
        for j, (px, py) in enumerate(chips):
            slot = o_ref.at[2 * px + py]
            _rcopy(slot, slot, ssem.at[j], rsem.at[j], (x, y, c)).wait_recv()
        for cp in sends:
            cp.wait_send()
        loc.wait()

    return pl.pallas_call(
        body, name="exchange_mod", in_specs=[ANY], out_specs=ANY,
        out_shape=jax.ShapeDtypeStruct((N_CHIPS,) + part.shape, part.dtype),
        scratch_shapes=[pltpu.SemaphoreType.DMA((3,)), pltpu.SemaphoreType.DMA((3,)), pltpu.SemaphoreType.DMA],
        compiler_params=_params(n_grid=0),
    )(part)


def spread_now(gi, go, gp, layers, wp_layers):
    plan = plan_spread(layers, wp_layers)
    n = 2 * len(layers) + 7 * len(wp_layers)

    def body(gi_in, go_in, gp_in, gi, go, gp, ssem, rsem):
        copies = plan((gi, go, gp), *_me())
        me = _me()
        sends = [_rcopy(src, dst, ssem.at[k], rsem.at[k], peer) for k, (src, dst, peer, _) in enumerate(copies)]
        for cp in sends:
            cp.start()
        for k, (_, _, _, land) in enumerate(copies):
            _rcopy(land, land, ssem.at[k], rsem.at[k], me).wait_recv()
        for cp in sends:
            cp.wait_send()

    return pl.pallas_call(
        body, name="spread_now",
        in_specs=[ANY] * 3, out_specs=[ANY] * 3,
        out_shape=[jax.ShapeDtypeStruct(a.shape, a.dtype) for a in (gi, go, gp)],
        input_output_aliases={0: 0, 1: 1, 2: 2},
        scratch_shapes=[pltpu.SemaphoreType.DMA((n,)), pltpu.SemaphoreType.DMA((n,))],
        compiler_params=_params(n_grid=0),
    )(gi, go, gp)


def add_sibling(cidx, mine, sib):
    def body(c_ref, *refs):
        for a in range(3):
            m, s, o = refs[a], refs[3 + a], refs[6 + a]
            o[...] = (m[...].astype(F32) + s[...].astype(F32)).astype(BF16)

    def mine_spec(a):
        h = a.shape[1] // 2
        return pl.BlockSpec((None, h, a.shape[2]), lambda j, c_ref: (j, c_ref[0], 0))

    def sib_spec(a):
        return pl.BlockSpec((None,) + a.shape[1:], lambda j, c_ref: (j, 0, 0))

    return pl.pallas_call(
        body, name="add_sibling",
        grid_spec=pltpu.PrefetchScalarGridSpec(
            num_scalar_prefetch=1, grid=(N_CHIPS,),
            in_specs=[mine_spec(a) for a in mine] + [sib_spec(a) for a in sib],
            out_specs=[sib_spec(a) for a in sib]),
        out_shape=[jax.ShapeDtypeStruct(a.shape, BF16) for a in sib],
        compiler_params=_params(VMEM_BIG),
    )(cidx, *mine, *sib)


def sum_chips(pos, own, rb, acc, l, shapes):
    nq = 4
    n_in = 6 + (3 if acc is not None else 0)

    def body(pos_ref, *refs):
        for a in range(3):
            m, b, o = refs[a], refs[3 + a], refs[n_in + a]
            s = m[...].astype(F32)
            for j in range(3):
                s = s + b[j].astype(F32)
            o[...] = s

    def own_spec(a):
        return pl.BlockSpec((None, a.shape[1] // nq, a.shape[2]), lambda q, p: (p[1], q, 0))

    def rb_spec(a):
        return pl.BlockSpec((3, a.shape[1] // nq, a.shape[2]), lambda q, p: (0, q, 0))

    hi, ho, hp = own[0].shape[1] // nq, own[1].shape[1] // nq, own[2].shape[1] // nq
    out_specs = [pl.BlockSpec((None, hi, shapes[0][2]), lambda q, p: (l, p[0] * nq + q, 0)),
                 pl.BlockSpec((None, ho, shapes[1][2]), lambda q, p: (l, p[0] * nq + q, 0)),
                 pl.BlockSpec((None, None, hp, LANES), lambda q, p: (l, p[1], p[0] * nq + q, 0))]
    in_specs = [own_spec(a) for a in own] + [rb_spec(a) for a in rb]
    args = list(own) + list(rb)
    aliases = {}
    if acc is not None:
        in_specs += [ANY] * 3
        args += list(acc)
        aliases = {7: 0, 8: 1, 9: 2}
    return pl.pallas_call(
        body, name="sum_chips",
        grid_spec=pltpu.PrefetchScalarGridSpec(num_scalar_prefetch=1, grid=(nq,), in_specs=in_specs, out_specs=out_specs),
        out_shape=[jax.ShapeDtypeStruct(s, F32) for s in shapes],
        input_output_aliases=aliases,
        compiler_params=_params(VMEM_BIG),
    )(pos, *args)


def pack_small(pos, per_layer, loss_blk):
    L = len(per_layer)
    D = per_layer[0][0].shape[1]

    def body(pos_ref, *refs):
        o = refs[-1]
        lb = refs[-2]
        o[...] = jnp.zeros_like(o)
        for l in range(L):
            dgpre, dgpost, dsh, dsc, dgt, dps, dwc = refs[7 * l:7 * l + 7]
            base = SUBLANES * l
            for r, src in enumerate((dgpre, dgpost, dsh, dsc, dgt)):
                o[pl.ds(base + r, 1), :] = src[...]
            o[pl.ds(base + 5, 1), 0:dps.shape[1]] = dps[...]
            for j in range(dwc.shape[0]):
                for k in range(3):
                    idx = 3 * j + k
                    o[pl.ds(base + 6 + idx // 8, 1), (idx % 8) * LANES:(idx % 8 + 1) * LANES] = dwc[j, pl.ds(k, 1), :]
        o[pl.ds(5, 1), 4 * LANES:5 * LANES] = lb[pl.ds(0, 1), :]

    flat = [a for layer in per_layer for a in layer] + [loss_blk]

    def whole(a):
        return pl.BlockSpec(a.shape, lambda i, p: (0,) * a.ndim)

    return pl.pallas_call(
        body, name="pack_small",
        grid_spec=pltpu.PrefetchScalarGridSpec(
            num_scalar_prefetch=1, grid=(1,), in_specs=[whole(a) for a in flat],
            out_specs=pl.BlockSpec((None, L * SUBLANES, D), lambda i, p: (p[2], 0, 0))),
        out_shape=jax.ShapeDtypeStruct((N_DEV, L * SUBLANES, D), F32),
        compiler_params=_params(),
    )(pos, *flat)


def sum_packs(packs):
    def body(p_ref, o_ref):
        s = p_ref[0]
        for d in range(1, N_DEV):
            s = s + p_ref[d]
        o_ref[...] = s

    return pl.pallas_call(
        body, name="sum_packs",
        out_shape=jax.ShapeDtypeStruct(packs.shape[1:], F32),
        compiler_params=_params(n_grid=0),
    )(packs)


def _adamw_math(w, g, m, v):
    m = ADAM_B1 * m + (1.0 - ADAM_B1) * g
    v = ADAM_B2 * v + (1.0 - ADAM_B2) * (g * g)
    m_hat = m / (1.0 - ADAM_B1 ** ADAM_STEP)
    v_hat = v / (1.0 - ADAM_B2 ** ADAM_STEP)
    delta = -ADAM_LR * (m_hat / (jnp.sqrt(v_hat) + ADAM_EPS) + ADAM_WD * w)
    return delta, m, v


def adamw(w, g, m, v, block, name, first=0, count=None, acc=None):
    grid = tuple(s // b for s, b in zip(w.shape, block))
    if count is not None:
        grid = (count,) + grid[1:]

    def body(w_ref, g_ref, m_ref, v_ref, *rest):
        d_ref, mo_ref, vo_ref = rest[-3:]
        d, mm, vv = _adamw_math(w_ref[...], g_ref[...], m_ref[...], v_ref[...])
        d_ref[...] = d
        mo_ref[...] = mm
        vo_ref[...] = vv

    spec = pl.BlockSpec(block, lambda i, *rest: (first + i,) + rest)
    shape = jax.ShapeDtypeStruct(w.shape, F32)
    extra = [] if acc is None else list(acc)
    return pl.pallas_call(
        body, name=name, grid=grid,
        in_specs=[spec] * 4 + [ANY] * len(extra), out_specs=[spec] * 3, out_shape=[shape] * 3,
        input_output_aliases={4 + a: a for a in range(len(extra))},
        compiler_params=_params(VMEM_BIG, n_grid=len(grid)),
    )(w, g, m, v, *extra)


def ada_finish(c_all, dmod, w, m, v):
    L, D, CW = w.shape
    hD = D // 2

    def body(c_ref, d_ref, w_ref, m_ref, v_ref, g_ref, dl_ref, mo_ref, vo_ref):
        cv = c_ref[...]
        z = jnp.zeros_like(cv)
        ca = jnp.concatenate([cv * jax.nn.sigmoid(cv), z], axis=0).astype(BF16)
        dm = jnp.concatenate([d_ref[0], jnp.zeros_like(d_ref[0])], axis=0).astype(BF16)
        g = lax.dot_general(ca, dm, TN, preferred_element_type=F32)
        g_ref[0] = g
        d, mm, vv = _adamw_math(w_ref[0], g, m_ref[0], v_ref[0])
        dl_ref[0] = d
        mo_ref[0] = mm
        vo_ref[0] = vv

    big = pl.BlockSpec((1, hD, CW), lambda l, h: (l, h, 0))
    shape = jax.ShapeDtypeStruct(w.shape, F32)
    return pl.pallas_call(
        body, name="ada_finish", grid=(L, 2),
        in_specs=[pl.BlockSpec((N_DEV, hD), lambda l, h: (0, h)), pl.BlockSpec((1, N_DEV, CW), lambda l, h: (l, 0, 0)),
                  big, big, big],
        out_specs=[big] * 4, out_shape=[shape] * 4,
        compiler_params=_params(VMEM_BIG, n_grid=2),
    )(c_all, dmod, w, m, v)


def kernel(x, c, w_ada, b_ada, g_pre, w_in, w_conv, w_pool, pool_scale, w_out, g_post, loss_target, m_w_ada, m_b_ada, m_g_pre, m_w_in, m_w_conv, m_w_pool, m_pool_scale, m_w_out, m_g_post, v_w_ada, v_b_ada, v_g_pre, v_w_in, v_w_conv, v_w_pool, v_pool_scale, v_w_out, v_g_post):
    L, D, CW = w_in.shape
    RO = w_out.shape[1]
    T = x.shape[1]
    ix, iy, ic = _me()
    chip = 2 * ix + iy
    me_lin = 4 * ix + 2 * iy + ic

    pos = jnp.stack([ic, chip, me_lin]).astype(jnp.int32)
    g_pre3, g_post3 = g_pre.reshape(L, 1, D), g_post.reshape(L, 1, D)
    pscale3 = pool_scale.reshape(L, 1, pool_scale.shape[1])
    n_g, n_s, n_c = 6, 3, 9

    c_all3, wconv_all = gather_small(c.reshape(SUBLANES, LANES), w_conv, pos)
    c_all = c_all3.reshape(N_DEV, D)
    gath = [None] * L
    ss, rs, bufs, token = xchg_start("gather_start", list(cast_weights(pos, w_in, w_out, 0, c_all3)), n_g, plan_gather)
    gath[0] = (ss, rs, bufs)
    b_my = lax.dynamic_slice_in_dim(b_ada, chip * CW, CW, axis=1)
    mod_all = exchange_mod(mod_part(c_all, w_ada, b_my, token))
    mod = lax.dynamic_index_in_dim(mod_all, me_lin, axis=2, keepdims=False)
    mod4 = jnp.transpose(mod, (1, 0, 2)).reshape(L, 3, 1, D)
    token = mod_all
    for l in range(1, L):
        ss, rs, bufs, token = xchg_start("gather_start", list(cast_weights(pos, w_in, w_out, l, token)), n_g, plan_gather)
        gath[l] = (ss, rs, bufs)

    def arrive(l, after):
        ss, rs, bufs = gath[l]
        bufs = xchg_wait("gather_wait", bufs, ss, rs, n_g, plan_gather, after)
        return xchg_start("forward_start", bufs, n_g, plan_forward)

    xs, projs, yas, yps, ys = [x.reshape(T, D)], [], [], [], []
    wg_in, wg_out = [], []
    fwd = arrive(0, [token])
    for l in range(L):
        fss, frs, bufs, ftoken = fwd
        gi, go = xchg_wait("forward_wait", bufs, fss, frs, n_g, plan_forward, [ftoken if l == 0 else xs[l]])
        wg_in.append(gi)
        wg_out.append(go.reshape(N_CHIPS * RO, D))
        proj = proj_fwd(xs[l], mod4, g_pre3, wg_in[l], l)
        ya = conv_fwd(proj, wconv_all, l)
        yp = pool_fwd(proj, w_pool, pscale3, l)
        token = yp
        if l + 1 < L:
            fwd = arrive(l + 1, [ya, yp])
            token = fwd[3]
        xn, yv = out_fwd(ya, yp, wg_out[l], xs[l], mod4, g_post3, l, token)
        xs.append(xn)
        projs.append(proj)
        yas.append(ya)
        yps.append(yp)
        ys.append(yv)

    dx, loss_blk = loss_head(xs[L], loss_target.reshape(T, D))

    shapes = (w_in.shape, w_out.shape, w_pool.shape)
    smalls = [None] * L
    acc, flying, sib, token = None, None, None, loss_blk

    def to_chips(sib, after):
        sl, s_ss, s_rs, s_bufs = sib
        s_bufs = xchg_wait("sibling_wait", s_bufs, s_ss, s_rs, n_s, plan_sibling, after)
        chip_parts = add_sibling(pos, s_bufs[0:3], s_bufs[3:6])
        lands = [lax.empty((3,) + a.shape[1:], a.dtype) for a in chip_parts]
        c_ss, c_rs, c_bufs, ctoken = xchg_start("chip_start", list(chip_parts) + lands, n_c, plan_chip)
        return (sl, c_ss, c_rs, c_bufs), ctoken

    def landed(flying, acc, after):
        fl, f_ss, f_rs, f_bufs = flying
        f_bufs = xchg_wait("chip_wait", f_bufs, f_ss, f_rs, n_c, plan_chip, after)
        return sum_chips(pos, f_bufs[0:3], f_bufs[3:6], acc, fl, shapes)

    for l in reversed(range(L)):
        dya, dyp, dwo_l, dgate, dgpost = out_bwd(dx, ys[l], yas[l], yps[l], wg_out[l], mod4, g_post3, l, token)
        token = dya
        if sib is not None:
            arrived = flying
            flying, token = to_chips(sib, [dya])
            if arrived is not None:
                acc = landed(arrived, acc, [token])
                token = acc[0]
        du_p, dg_p, dwp_l, dps = pool_bwd(projs[l], dyp, w_pool, pscale3, l, token)
        du_a, db_a, dc_a, dg_a, dwc = conv_bwd(projs[l], dya, wconv_all, l)
        dx, dwi_l, dshift, dscale, dgpre = in_bwd([du_a, db_a, dc_a, dg_a, du_p, dg_p], wg_in[l], xs[l], dx,
                                                  mod4, g_pre3, l)
        smalls[l] = (dgpre, dgpost, dshift, dscale, dgate, dps, dwc)
        parts = [dwi_l, dwo_l.reshape(N_CHIPS, RO, D), dwp_l]
        s_lands = [lax.empty((a.shape[0], a.shape[1] // 2) + a.shape[2:], a.dtype) for a in parts]
        s_ss, s_rs, s_bufs, token = xchg_start("sibling_start", parts + s_lands, n_s, plan_sibling)
        sib = (l, s_ss, s_rs, s_bufs)
    grad_x = dx.reshape(1, T, D)

    p_ss, p_rs, packs, ptoken = xchg_start("pack_start", [pack_small(pos, smalls, loss_blk)], N_DEV - 1, plan_pack)
    acc = landed(flying, acc, [ptoken, token])
    flying, token = to_chips(sib, [acc[0]])
    n_sp = 2 * (L - 1)
    spread = plan_spread(tuple(range(1, L)), ())
    sp_ss, sp_rs, acc, sp_token = xchg_start("spread_start", list(acc), n_sp, spread)
    (packs_all,) = xchg_wait("pack_wait", packs, p_ss, p_rs, N_DEV - 1, plan_pack, [sp_token, token])
    small = sum_packs(packs_all).reshape(L, SUBLANES, D)
    loss = small[0, 5, 4 * LANES]
    g_g_pre = small[:, 0]
    g_g_post = small[:, 1]
    g_b_ada = small[:, 2:5].reshape(L, 3 * D)
    g_pscale = small[:, 5, 0:pool_scale.shape[1]]
    g_w_conv = small[:, 6:8].reshape(L, 2 * D)[:, 0:N_CHIPS * 3 * LANES].reshape(L, N_CHIPS, 3, LANES)
    g_w_conv = lax.dynamic_index_in_dim(g_w_conv, chip, axis=1, keepdims=False)
    dmod_all = packs_all.reshape(N_DEV, L, SUBLANES, D)[:, :, 2:5].reshape(N_DEV, L, 3 * D)
    dmod_my = jnp.transpose(lax.dynamic_slice_in_dim(dmod_all, chip * CW, CW, axis=2), (1, 0, 2))

    g_w_ada, d_w_ada, nm_w_ada, nv_w_ada = ada_finish(c_all, dmod_my, w_ada, m_w_ada, v_w_ada)

    def small_adamw(w, g, m, v, name):
        shp = (1,) + w.shape if w.ndim == 2 else w.shape
        outs = adamw(w.reshape(shp), g.reshape(shp), m.reshape(shp), v.reshape(shp), shp, name)
        return [a.reshape(w.shape) for a in outs]

    d_b_ada, nm_b_ada, nv_b_ada = small_adamw(b_ada, g_b_ada, m_b_ada, v_b_ada, "adamw_b_ada")
    d_g_pre, nm_g_pre, nv_g_pre = small_adamw(g_pre, g_g_pre, m_g_pre, v_g_pre, "adamw_g_pre")
    d_w_conv, nm_w_conv, nv_w_conv = small_adamw(w_conv, g_w_conv, m_w_conv, v_w_conv, "adamw_w_conv")
    d_pscale, nm_pscale, nv_pscale = small_adamw(pool_scale, g_pscale, m_pool_scale, v_pool_scale, "adamw_pool_scale")
    d_g_post, nm_g_post, nv_g_post = small_adamw(g_post, g_g_post, m_g_post, v_g_post, "adamw_g_post")

    done = [nv_w_ada, nv_b_ada, nv_g_pre, nv_w_conv, nv_pscale, nv_g_post]
    g_w_in, g_w_out, g_w_pool = xchg_wait("spread_wait", acc, sp_ss, sp_rs, n_sp, spread, done)
    in_blk, out_blk = (1, D // 2, CW), (1, RO, D)
    upd_in = adamw(w_in, g_w_in, m_w_in, v_w_in, in_blk, "adamw_w_in", 1, L - 1)
    upd_out = adamw(w_out, g_w_out, m_w_out, v_w_out, out_blk, "adamw_w_out", 1, L - 1)

    acc = landed(flying, (g_w_in, g_w_out, g_w_pool), [upd_in[2], upd_out[2]])
    g_w_in, g_w_out, g_w_pool = spread_now(*acc, (0,), tuple(range(L)))
    d_w_in, nm_w_in, nv_w_in = adamw(w_in, g_w_in, m_w_in, v_w_in, in_blk, "adamw_w_in", 0, 1, upd_in)
    d_w_out, nm_w_out, nv_w_out = adamw(w_out, g_w_out, m_w_out, v_w_out, out_blk, "adamw_w_out", 0, 1, upd_out)
    pshape = (L, N_CHIPS * LANES, LANES)
    d_w_pool, nm_w_pool, nv_w_pool = adamw(w_pool.reshape(pshape), g_w_pool.reshape(pshape), m_w_pool.reshape(pshape),
                                           v_w_pool.reshape(pshape), (1,) + pshape[1:], "adamw_w_pool")
    d_w_pool, nm_w_pool, nv_w_pool = [a.reshape(w_pool.shape) for a in (d_w_pool, nm_w_pool, nv_w_pool)]

    return (loss, grad_x,
            g_w_ada, g_b_ada, g_g_pre, g_w_in, g_w_conv, g_w_pool, g_pscale, g_w_out, g_g_post,
            d_w_ada, d_b_ada, d_g_pre, d_w_in, d_w_conv, d_w_pool, d_pscale, d_w_out, d_g_post,
            nm_w_ada, nm_b_ada, nm_g_pre, nm_w_in, nm_w_conv, nm_w_pool, nm_pscale, nm_w_out, nm_g_post,
            nv_w_ada, nv_b_ada, nv_g_pre, nv_w_in, nv_w_conv, nv_w_pool, nv_pscale, nv_w_out, nv_g_post)
```

```python
import functools

import jax
import jax.numpy as jnp
from jax import lax
from jax.experimental import pallas as pl
from jax.experimental.pallas import tpu as pltpu

F32 = jnp.float32
BF16 = jnp.bfloat16
MESH = pl.DeviceIdType.MESH
ANY = pl.BlockSpec(memory_space=pl.ANY)

NORM_EPS = 1e-6
POOL_WINDOWS = (2, 4, 8, 16)
ADAM_LR = 0.001
ADAM_B1 = 0.9
ADAM_B2 = 0.999
ADAM_EPS = 1e-08
ADAM_WD = 0.01
ADAM_STEP = 10

N_CHIPS = 4
N_DEV = 8
LANES = 128
SUBLANES = 8
VMEM_BIG = 56 * 1024 * 1024
HIST = 16
R_CONV = 32
R_POOL = 64

NT = (((1,), (1,)), ((), ()))
TN = (((0,), (0,)), ((), ()))


def _params(vmem=None, n_grid=1):
    kw = {}
    if n_grid:
        kw["dimension_semantics"] = ("arbitrary",) * n_grid
    if vmem is not None:
        kw["vmem_limit_bytes"] = vmem
    return pltpu.CompilerParams(**kw)


def _colsum8(v):
    n, d = v.shape
    return v.reshape(n // SUBLANES, SUBLANES, d).sum(axis=0)


def _rms(v):
    return lax.rsqrt(jnp.mean(v * v, axis=-1, keepdims=True) + NORM_EPS)


def _sigmoid(v):
    return 0.5 * jnp.tanh(0.5 * v) + 0.5


def _shift_down(ext, k, rows):
    if k == 0:
        return ext[HIST:HIST + rows]
    return pltpu.roll(ext, k, 0)[HIST:HIST + rows]


def _shift_up(ext, k, rows):
    if k == 0:
        return ext[0:rows]
    return pltpu.roll(ext, ext.shape[0] - k, 0)[0:rows]


def _load_ext(ref, r0, h0, first, rows):
    hist = ref[pl.ds(h0, HIST), :].astype(F32)
    hist = jnp.where(first, 0.0, hist)
    cur = ref[pl.ds(r0, rows), :].astype(F32)
    return jnp.concatenate([hist, cur], axis=0)


def _me():
    return lax.axis_index("x"), lax.axis_index("y"), lax.axis_index("c")


def cast_weights(pos, w_in, w_out, l, after):
    _, D, CW = w_in.shape
    RO = w_out.shape[1]

    def body(pos_ref, wi, wo, after_ref, oi, oo):
        oi[...] = wi[...].astype(BF16)
        oo[...] = wo[...].astype(BF16)

    return pl.pallas_call(
        body, name="cast_w",
        grid_spec=pltpu.PrefetchScalarGridSpec(
            num_scalar_prefetch=1, grid=(2,),
            in_specs=[pl.BlockSpec((None, D // 2, CW), lambda h, p: (l, h, 0)),
                      pl.BlockSpec((None, RO // 2, D), lambda h, p: (l, h, 0)), ANY],
            out_specs=[pl.BlockSpec((D // 2, CW), lambda h, p: (h, p[1])),
                       pl.BlockSpec((None, RO // 2, D), lambda h, p: (p[1], h, 0))]),
        out_shape=[jax.ShapeDtypeStruct((D, N_CHIPS * CW), BF16), jax.ShapeDtypeStruct((N_CHIPS, RO, D), BF16)],
        compiler_params=_params(),
    )(pos, w_in, w_out, after)


def mod_part(c_all, w_ada, b_my, after):
    L, D, CW = w_ada.shape

    def body(c_ref, w_ref, b_ref, after_ref, o_ref):
        cv = c_ref[...]
        ca = (cv * jax.nn.sigmoid(cv)).astype(BF16)
        o_ref[0] = jnp.dot(ca, w_ref[0].astype(BF16), preferred_element_type=F32) + b_ref[0]

    return pl.pallas_call(
        body, name="mod_part", grid=(L,),
        in_specs=[pl.BlockSpec((N_DEV, D), lambda l: (0, 0)),
                  pl.BlockSpec((1, D, CW), lambda l: (l, 0, 0)),
                  pl.BlockSpec((1, 1, CW), lambda l: (l, 0, 0)), ANY],
        out_specs=pl.BlockSpec((1, N_DEV, CW), lambda l: (l, 0, 0)),
        out_shape=jax.ShapeDtypeStruct((L, N_DEV, CW), F32),
        compiler_params=_params(VMEM_BIG),
    )(c_all, w_ada, b_my.reshape(L, 1, CW), after)


def _mod_row(l, k, D):
    return pl.BlockSpec((None, None, 1, D), lambda *_: (l, k, 0, 0))


def _layer_row(l, D):
    return pl.BlockSpec((None, 1, D), lambda *_: (l, 0, 0))


def proj_fwd(x, mod4, g_pre3, wg, l):
    T, D = x.shape
    NC = wg.shape[1]
    NB = N_CHIPS
    CW = NC // NB
    tm = 512

    def body(x_ref, sh_ref, sc_ref, g_ref, w_ref, o_ref):
        xv = x_ref[...]
        h = (xv * _rms(xv) * g_ref[...]) * (1.0 + sc_ref[...]) + sh_ref[...]
        hb = h.astype(BF16)
        for j in range(NB):
            cols = slice(j * CW, (j + 1) * CW)
            o_ref[:, cols] = jnp.dot(hb, w_ref[:, cols], preferred_element_type=F32).astype(BF16)

    return pl.pallas_call(
        body, name="proj_fwd", grid=(T // tm,),
        in_specs=[pl.BlockSpec((tm, D), lambda i: (i, 0)), _mod_row(l, 0, D), _mod_row(l, 1, D), _layer_row(l, D),
                  pl.BlockSpec((D, NC), lambda i: (0, 0))],
        out_specs=pl.BlockSpec((tm, NC), lambda i: (i, 0)),
        out_shape=jax.ShapeDtypeStruct((T, NC), BF16),
        compiler_params=_params(VMEM_BIG),
    )(x, mod4, mod4, g_pre3, wg)


def conv_fwd(proj, wconv, l):
    T = proj.shape[0]
    R = R_CONV
    nblk = 4

    def body(u_ref, b_ref, c_ref, g_ref, w_ref, o_ref):
        w0 = w_ref[pl.ds(0, 1), :]
        w1 = w_ref[pl.ds(1, 1), :]
        w2 = w_ref[pl.ds(2, 1), :]

        def chunk(i, carry):
            r0 = pl.multiple_of(i * R, R)
            h0 = pl.multiple_of(jnp.maximum(r0 - HIST, 0), HIST)
            first = i == 0
            ca = _load_ext(c_ref, r0, h0, first, R) * _load_ext(u_ref, r0, h0, first, R)
            conv = w2 * ca[HIST:] + w1 * _shift_down(ca, 1, R) + w0 * _shift_down(ca, 2, R)
            g = g_ref[pl.ds(r0, R), :].astype(F32)
            b = b_ref[pl.ds(r0, R), :].astype(F32)
            o_ref[pl.ds(r0, R), :] = (b * conv * (g * _sigmoid(g))).astype(BF16)
            return carry

        lax.fori_loop(0, T // R, chunk, 0)

    def col(off):
        return pl.BlockSpec((T, LANES), lambda j: (0, j + off))

    return pl.pallas_call(
        body, name="conv_fwd", grid=(nblk,),
        in_specs=[col(0), col(4), col(8), col(12), pl.BlockSpec((None, None, 3, LANES), lambda j: (j, l, 0, 0))],
        out_specs=pl.BlockSpec((T, LANES), lambda j: (0, j)),
        out_shape=jax.ShapeDtypeStruct((T, nblk * LANES), BF16),
        compiler_params=_params(),
    )(proj, proj, proj, proj, wconv)


def _causal_window_sum(ext, w):
    s, k = ext, 1
    while k < w:
        s = s + pltpu.roll(s, k, 0)
        k *= 2
    return s


def _anticausal_window_sum(ext, w):
    s, k = ext, 1
    n = ext.shape[0]
    while k < w:
        s = s + pltpu.roll(s, n - k, 0)
        k *= 2
    return s


def _count(r0, rows, w):
    t = r0 + lax.broadcasted_iota(jnp.int32, (rows, LANES), 0)
    return jnp.minimum(t + 1, w).astype(F32)


def _pooled_loop(p_ref, pooled_s, w, T):
    R = R_POOL

    def chunk(i, carry):
        r0 = pl.multiple_of(i * R, R)
        h0 = pl.multiple_of(jnp.maximum(r0 - HIST, 0), HIST)
        ext = _load_ext(p_ref, r0, h0, i == 0, R)
        ws = _causal_window_sum(ext, w)[HIST:]
        pooled_s[pl.ds(r0, R), :] = (ws / _count(r0, R, w) - ext[HIST:]).astype(BF16)
        return carry

    lax.fori_loop(0, T // R, chunk, 0)


def _pool_w_spec(l):
    return pl.BlockSpec((None, None, LANES, LANES), lambda j: (l, j, 0, 0))


def _pool_s_spec(l):
    return pl.BlockSpec((None, 1, LANES), lambda j: (l, 0, j))


def pool_fwd(proj, wpool, pscale3, l):
    T = proj.shape[0]
    R = R_POOL
    ngrp = len(POOL_WINDOWS)

    def body(p_ref, g_ref, w_ref, s_ref, o_ref, pooled_s, mixed_s):
        grp = pl.program_id(0)

        def group(w):
            _pooled_loop(p_ref, pooled_s, w, T)
            mixed_s[...] = jnp.dot(pooled_s[...], w_ref[...].astype(BF16), preferred_element_type=F32)
            sc = s_ref[...]

            def chunk(i, carry):
                r0 = pl.multiple_of(i * R, R)
                g = g_ref[pl.ds(r0, R), :].astype(F32)
                o_ref[pl.ds(r0, R), :] = (mixed_s[pl.ds(r0, R), :] * sc * (g * _sigmoid(g))).astype(BF16)
                return carry

            lax.fori_loop(0, T // R, chunk, 0)

        for k, w in enumerate(POOL_WINDOWS):
            pl.when(grp == k)(functools.partial(group, w))

    return pl.pallas_call(
        body, name="pool_fwd", grid=(ngrp,),
        in_specs=[pl.BlockSpec((T, LANES), lambda j: (0, j + 16)), pl.BlockSpec((T, LANES), lambda j: (0, j + 20)),
                  _pool_w_spec(l), _pool_s_spec(l)],
        out_specs=pl.BlockSpec((T, LANES), lambda j: (0, j)),
        out_shape=jax.ShapeDtypeStruct((T, ngrp * LANES), BF16),
        scratch_shapes=[pltpu.VMEM((T, LANES), BF16), pltpu.VMEM((T, LANES), F32)],
        compiler_params=_params(),
    )(proj, proj, wpool, pscale3)


def out_fwd(ya, yp, wo, x, mod4, g_post3, l, after):
    T, D = x.shape
    H = ya.shape[1]
    tm = 512

    def body(ya_ref, yp_ref, wo_ref, x_ref, gt_ref, g_ref, after_ref, xn_ref, y_ref):
        y = (jnp.dot(ya_ref[...], wo_ref[0:H, :], preferred_element_type=F32)
             + jnp.dot(yp_ref[...], wo_ref[H:2 * H, :], preferred_element_type=F32))
        xn_ref[...] = x_ref[...] + gt_ref[...] * (y * _rms(y) * g_ref[...])
        y_ref[...] = y

    tile = pl.BlockSpec((tm, D), lambda i: (i, 0))
    half = pl.BlockSpec((tm, H), lambda i: (i, 0))
    return pl.pallas_call(
        body, name="out_fwd", grid=(T // tm,),
        in_specs=[half, half, pl.BlockSpec((2 * H, D), lambda i: (0, 0)), tile, _mod_row(l, 2, D), _layer_row(l, D),
                  ANY],
        out_specs=[tile, tile],
        out_shape=[jax.ShapeDtypeStruct((T, D), F32), jax.ShapeDtypeStruct((T, D), F32)],
        compiler_params=_params(VMEM_BIG),
    )(ya, yp, wo, x, mod4, g_post3, after)


def loss_head(xl, target):
    T, D = xl.shape
    tm = 512
    nt = T // tm

    def body(x_ref, t_ref, dx_ref, l_ref, acc):
        i = pl.program_id(0)

        @pl.when(i == 0)
        def _():
            acc[...] = jnp.zeros_like(acc)

        d = x_ref[...] - t_ref[...]
        dx_ref[...] = d * (1.0 / D)
        acc[...] += _colsum8(d * d)

        @pl.when(i == nt - 1)
        def _():
            l_ref[...] = jnp.zeros_like(l_ref) + jnp.sum(acc[...]) * (0.5 / D)

    tile = pl.BlockSpec((tm, D), lambda i: (i, 0))
    return pl.pallas_call(
        body, name="loss_head", grid=(nt,),
        in_specs=[tile, tile],
        out_specs=[tile, pl.BlockSpec((SUBLANES, LANES), lambda i: (0, 0))],
        out_shape=[jax.ShapeDtypeStruct((T, D), F32), jax.ShapeDtypeStruct((SUBLANES, LANES), F32)],
        scratch_shapes=[pltpu.VMEM((SUBLANES, D), F32)],
        compiler_params=_params(VMEM_BIG),
    )(xl, target)


def out_bwd(dx, y, ya, yp, wo, mod4, g_post3, l, after):
    T, D = dx.shape
    H = ya.shape[1]
    tm = 512
    nt = T // tm

    def body(dx_ref, y_ref, ya_ref, yp_ref, wo_ref, gt_ref, g_ref, after_ref,
             dya_ref, dyp_ref, dwo_ref, dgt_ref, dg_ref, acc_w, acc_gt, acc_g):
        i = pl.program_id(0)

        @pl.when(i == 0)
        def _():
            acc_w[...] = jnp.zeros_like(acc_w)
            acc_gt[...] = jnp.zeros_like(acc_gt)
            acc_g[...] = jnp.zeros_like(acc_g)

        yv = y_ref[...]
        dxv = dx_ref[...]
        g = g_ref[...]
        r = _rms(yv)
        yn = yv * r
        acc_gt[...] += _colsum8(dxv * (yn * g))
        dn = dxv * gt_ref[...]
        acc_g[...] += _colsum8(dn * yn)
        a = dn * g
        dy = r * (a - yn * jnp.mean(a * yn, axis=-1, keepdims=True))
        dyb = dy.astype(BF16)
        dyc = lax.dot_general(dyb, wo_ref[...], NT, preferred_element_type=F32)
        dya_ref[...] = dyc[:, 0:H].astype(BF16)
        dyp_ref[...] = dyc[:, H:2 * H].astype(BF16)
        acc_w[0:H, :] += lax.dot_general(ya_ref[...], dyb, TN, preferred_element_type=F32)
        acc_w[H:2 * H, :] += lax.dot_general(yp_ref[...], dyb, TN, preferred_element_type=F32)

        @pl.when(i == nt - 1)
        def _():
            dwo_ref[...] = acc_w[...].astype(BF16)
            dgt_ref[...] = jnp.sum(acc_gt[...], axis=0, keepdims=True)
            dg_ref[...] = jnp.sum(acc_g[...], axis=0, keepdims=True)

    row = pl.BlockSpec((1, D), lambda i: (0, 0))
    tile = pl.BlockSpec((tm, D), lambda i: (i, 0))
    half = pl.BlockSpec((tm, H), lambda i: (i, 0))
    full = pl.BlockSpec((2 * H, D), lambda i: (0, 0))
    return pl.pallas_call(
        body, name="out_bwd", grid=(nt,),
        in_specs=[tile, tile, half, half, full, _mod_row(l, 2, D), _layer_row(l, D), ANY],
        out_specs=[half, half, full, row, row],
        out_shape=[jax.ShapeDtypeStruct((T, H), BF16), jax.ShapeDtypeStruct((T, H), BF16),
                   jax.ShapeDtypeStruct((2 * H, D), BF16),
                   jax.ShapeDtypeStruct((1, D), F32), jax.ShapeDtypeStruct((1, D), F32)],
        scratch_shapes=[pltpu.VMEM((2 * H, D), F32), pltpu.VMEM((SUBLANES, D), F32), pltpu.VMEM((SUBLANES, D), F32)],
        compiler_params=_params(VMEM_BIG),
    )(dx, y, ya, yp, wo, mod4, g_post3, after)


def conv_bwd(proj, dya, wconv, l):
    T = proj.shape[0]
    R = R_CONV
    nblk = 4
    nchunk = T // R

    def body(u_ref, b_ref, c_ref, g_ref, dy_ref, w_ref, du_ref, db_ref, dc_ref, dg_ref, dw_ref):
        w0 = w_ref[pl.ds(0, 1), :]
        w1 = w_ref[pl.ds(1, 1), :]
        w2 = w_ref[pl.ds(2, 1), :]

        def chunk(k, carry):
            head, a0, a1, a2 = carry
            i = nchunk - 1 - k
            r0 = pl.multiple_of(i * R, R)
            h0 = pl.multiple_of(jnp.maximum(r0 - HIST, 0), HIST)
            first = i == 0
            ue = _load_ext(u_ref, r0, h0, first, R)
            ce = _load_ext(c_ref, r0, h0, first, R)
            ca = ce * ue
            ca0 = ca[HIST:]
            ca1 = _shift_down(ca, 1, R)
            ca2 = _shift_down(ca, 2, R)
            conv = w2 * ca0 + w1 * ca1 + w0 * ca2
            g = g_ref[pl.ds(r0, R), :].astype(F32)
            b = b_ref[pl.ds(r0, R), :].astype(F32)
            dy = dy_ref[pl.ds(r0, R), :].astype(F32)
            sg = _sigmoid(g)
            sl = g * sg
            t = dy * conv
            db_ref[pl.ds(r0, R), :] = (t * sl).astype(BF16)
            dg_ref[pl.ds(r0, R), :] = (t * b * (sg * (1.0 + g * (1.0 - sg)))).astype(BF16)
            dconv = dy * b * sl
            a2 = a2 + _colsum8(dconv * ca0)
            a1 = a1 + _colsum8(dconv * ca1)
            a0 = a0 + _colsum8(dconv * ca2)
            e = jnp.concatenate([dconv, head], axis=0)
            dca = w2 * dconv + w1 * _shift_up(e, 1, R) + w0 * _shift_up(e, 2, R)
            du_ref[pl.ds(r0, R), :] = (dca * ce[HIST:]).astype(BF16)
            dc_ref[pl.ds(r0, R), :] = (dca * ue[HIST:]).astype(BF16)
            return dconv[0:SUBLANES], a0, a1, a2

        z = jnp.zeros((SUBLANES, LANES), F32)
        _, a0, a1, a2 = lax.fori_loop(0, nchunk, chunk, (z, z, z, z))
        dw_ref[pl.ds(0, 1), :] = jnp.sum(a0, axis=0, keepdims=True)
        dw_ref[pl.ds(1, 1), :] = jnp.sum(a1, axis=0, keepdims=True)
        dw_ref[pl.ds(2, 1), :] = jnp.sum(a2, axis=0, keepdims=True)

    def col(off):
        return pl.BlockSpec((T, LANES), lambda j: (0, j + off))

    sec = jax.ShapeDtypeStruct((T, nblk * LANES), BF16)
    return pl.pallas_call(
        body, name="conv_bwd", grid=(nblk,),
        in_specs=[col(0), col(4), col(8), col(12), col(0), pl.BlockSpec((None, None, 3, LANES), lambda j: (j, l, 0, 0))],
        out_specs=[col(0), col(0), col(0), col(0), pl.BlockSpec((None, 3, LANES), lambda j: (j, 0, 0))],
        out_shape=[sec, sec, sec, sec, jax.ShapeDtypeStruct((nblk, 3, LANES), F32)],
        compiler_params=_params(),
    )(proj, proj, proj, proj, dya, wconv)


def pool_bwd(proj, dyp, wpool, pscale3, l, after):
    T = proj.shape[0]
    R = R_POOL
    ngrp = len(POOL_WINDOWS)
    nchunk = T // R

    def body(p_ref, g_ref, dy_ref, w_ref, s_ref, after_ref, du_ref, dg_ref, dw_ref, ds_ref,
             pooled_s, mixed_s, dmix_s, dpool_s):
        grp = pl.program_id(0)

        def group(w):
            wb = w_ref[...].astype(BF16)
            _pooled_loop(p_ref, pooled_s, w, T)
            mixed_s[...] = jnp.dot(pooled_s[...], wb, preferred_element_type=F32)
            sc = s_ref[...]

            def gate_chunk(i, acc):
                r0 = pl.multiple_of(i * R, R)
                g = g_ref[pl.ds(r0, R), :].astype(F32)
                dy = dy_ref[pl.ds(r0, R), :].astype(F32)
                mixed = mixed_s[pl.ds(r0, R), :]
                sg = _sigmoid(g)
                dg_ref[pl.ds(r0, R), :] = (dy * mixed * sc * (sg * (1.0 + g * (1.0 - sg)))).astype(BF16)
                dms = dy * (g * sg)
                dmix_s[pl.ds(r0, R), :] = (dms * sc).astype(BF16)
                return acc + _colsum8(dms * mixed)

            acc = lax.fori_loop(0, nchunk, gate_chunk, jnp.zeros((SUBLANES, LANES), F32))
            ds_ref[...] = jnp.sum(acc, axis=0, keepdims=True)
            dpool_s[pl.ds(0, T), :] = lax.dot_general(dmix_s[...], wb, NT, preferred_element_type=F32)
            dpool_s[pl.ds(T, HIST), :] = jnp.zeros((HIST, LANES), F32)
            dw_ref[...] = lax.dot_general(pooled_s[...], dmix_s[...], TN, preferred_element_type=F32).astype(BF16)

            def back_chunk(i, carry):
                r0 = pl.multiple_of(i * R, R)
                dpe = dpool_s[pl.ds(r0, R + HIST), :]
                e = dpe / _count(r0, R + HIST, w)
                du_ref[pl.ds(r0, R), :] = (_anticausal_window_sum(e, w)[0:R] - dpe[0:R]).astype(BF16)
                return carry

            lax.fori_loop(0, nchunk, back_chunk, 0)

        for k, w in enumerate(POOL_WINDOWS):
            pl.when(grp == k)(functools.partial(group, w))

    def col(off):
        return pl.BlockSpec((T, LANES), lambda j: (0, j + off))

    sec = jax.ShapeDtypeStruct((T, ngrp * LANES), BF16)
    wspec = pl.BlockSpec((None, LANES, LANES), lambda j: (j, 0, 0))
    sspec = pl.BlockSpec((1, LANES), lambda j: (0, j))
    return pl.pallas_call(
        body, name="pool_bwd", grid=(ngrp,),
        in_specs=[col(16), col(20), col(0), _pool_w_spec(l), _pool_s_spec(l), ANY],
        out_specs=[col(0), col(0), wspec, sspec],
        out_shape=[sec, sec, jax.ShapeDtypeStruct((ngrp, LANES, LANES), BF16),
                   jax.ShapeDtypeStruct((1, ngrp * LANES), F32)],
        scratch_shapes=[pltpu.VMEM((T, LANES), BF16), pltpu.VMEM((T, LANES), F32),
                        pltpu.VMEM((T, LANES), BF16), pltpu.VMEM((T + HIST, LANES), F32)],
        compiler_params=_params(),
    )(proj, proj, dyp, wpool, pscale3, after)


def in_bwd(dsecs, wg, x, dxo, mod4, g_pre3, l):
    T, D = x.shape
    NB = N_CHIPS
    CW = wg.shape[1] // NB
    SW = dsecs[0].shape[1]
    nsec = len(dsecs)
    PW = 256
    assert SW % PW == 0 and CW % PW == 0
    tm = 256
    nt = T // tm

    def body(*refs):
        d_refs = refs[0:nsec]
        w_ref, x_ref, dxo_ref, sh_ref, sc_ref, g_ref = refs[nsec:nsec + 6]
        dxi_ref, dw_ref, dsh_ref, dsc_ref, dg_ref = refs[nsec + 6:nsec + 11]
        acc_w, acc_sh, acc_sc, acc_g = refs[nsec + 11:]
        i = pl.program_id(0)

        @pl.when(i == 0)
        def _():
            acc_w[...] = jnp.zeros_like(acc_w)
            acc_sh[...] = jnp.zeros_like(acc_sh)
            acc_sc[...] = jnp.zeros_like(acc_sc)
            acc_g[...] = jnp.zeros_like(acc_g)

        xv = x_ref[...]
        g = g_ref[...]
        r = _rms(xv)
        xh = xv * r
        n = xh * g
        sc1 = 1.0 + sc_ref[...]
        hb = (n * sc1 + sh_ref[...]).astype(BF16)
        dh = lax.dot_general(d_refs[0][...], w_ref[:, 0:SW], NT, preferred_element_type=F32)
        for s in range(1, nsec):
            dh = dh + lax.dot_general(d_refs[s][...], w_ref[:, s * SW:(s + 1) * SW], NT, preferred_element_type=F32)
        for p in range(nsec * SW // PW):
            col = p * PW
            s, so = col // SW, col % SW
            j, jo = col // CW, col % CW
            acc_w[j, :, jo:jo + PW] += lax.dot_general(hb, d_refs[s][:, so:so + PW], TN, preferred_element_type=F32)
        acc_sh[...] += _colsum8(dh)
        acc_sc[...] += _colsum8(dh * n)
        dnp = dh * sc1
        acc_g[...] += _colsum8(dnp * xh)
        a = dnp * g
        dxi_ref[...] = dxo_ref[...] + r * (a - xh * jnp.mean(a * xh, axis=-1, keepdims=True))

        @pl.when(i == nt - 1)
        def _():
            dw_ref[...] = acc_w[...].astype(BF16)
            dsh_ref[...] = jnp.sum(acc_sh[...], axis=0, keepdims=True)
            dsc_ref[...] = jnp.sum(acc_sc[...], axis=0, keepdims=True)
            dg_ref[...] = jnp.sum(acc_g[...], axis=0, keepdims=True)

    row = pl.BlockSpec((1, D), lambda i: (0, 0))
    tile = pl.BlockSpec((tm, D), lambda i: (i, 0))
    sect = pl.BlockSpec((tm, SW), lambda i: (i, 0))
    wspec = pl.BlockSpec((NB, D, CW), lambda i: (0, 0, 0))
    rowshape = jax.ShapeDtypeStruct((1, D), F32)
    return pl.pallas_call(
        body, name="in_bwd", grid=(nt,),
        in_specs=[sect] * nsec + [pl.BlockSpec((D, NB * CW), lambda i: (0, 0)), tile, tile,
                                  _mod_row(l, 0, D), _mod_row(l, 1, D), _layer_row(l, D)],
        out_specs=[tile, wspec, row, row, row],
        out_shape=[jax.ShapeDtypeStruct((T, D), F32), jax.ShapeDtypeStruct((NB, D, CW), BF16),
                   rowshape, rowshape, rowshape],
        scratch_shapes=[pltpu.VMEM((NB, D, CW), F32),
                        pltpu.VMEM((SUBLANES, D), F32), pltpu.VMEM((SUBLANES, D), F32), pltpu.VMEM((SUBLANES, D), F32)],
        compiler_params=_params(VMEM_BIG),
    )(*dsecs, wg, x, dxo, mod4, mod4, g_pre3)


def _rcopy(src, dst, ssem, rsem, dev):
    return pltpu.make_async_remote_copy(src_ref=src, dst_ref=dst, send_sem=ssem, recv_sem=rsem,
                                        device_id=dev, device_id_type=MESH)


def _peers7(x, y, c):
    out = []
    for m in range(1, N_DEV):
        bx, by, bc = (m >> 2) & 1, (m >> 1) & 1, m & 1
        out.append(((1 - x) if bx else x, (1 - y) if by else y, (1 - c) if bc else c))
    return out


HBM = pl.BlockSpec(memory_space=pltpu.HBM)
SEM = pl.BlockSpec(memory_space=pltpu.SEMAPHORE)
SPLIT = pltpu.CompilerParams(has_side_effects=pltpu.SideEffectType.DATAFLOW_SIDE_EFFECTING)


def _hbm(a):
    return pltpu.with_memory_space_constraint(a, pltpu.HBM)


def _chips(x, y):
    return [(1 - x, y), (x, 1 - y), (1 - x, 1 - y)]


def xchg_start(name, bufs, n_copies, plan):
    n = len(bufs)

    def body(*refs):
        ssem, rsem, token = refs[n], refs[n + 1], refs[-1]
        copies = plan(refs[0:n], *_me())
        assert len(copies) == n_copies
        for k, (src, dst, peer, _) in enumerate(copies):
            _rcopy(src, dst, ssem.at[k], rsem.at[k], peer).start()
        token[...] = jnp.zeros_like(token)

    outs = pl.pallas_call(
        body, name=name,
        in_specs=[HBM] * n,
        out_specs=[SEM, SEM] + [HBM] * n + [pl.BlockSpec(memory_space=pltpu.VMEM)],
        out_shape=([pltpu.SemaphoreType.DMA((n_copies,))] * 2 + [pltpu.HBM(b.shape, b.dtype) for b in bufs]
                   + [jax.ShapeDtypeStruct((SUBLANES, LANES), F32)]),
        input_output_aliases={a: 2 + a for a in range(n)},
        compiler_params=SPLIT,
    )(*[_hbm(b) for b in bufs])
    return outs[0], outs[1], list(outs[2:2 + n]), outs[-1]


def xchg_wait(name, bufs, ssem, rsem, n_copies, plan, after):
    n = len(bufs)
    after = list(after)

    def body(*refs):
        ssem_ref, rsem_ref = refs[n], refs[n + 1]
        copies = plan(refs[0:n], *_me())
        assert len(copies) == n_copies
        for k, (src, _, peer, land) in enumerate(copies):
            cp = _rcopy(src, land, ssem_ref.at[k], rsem_ref.at[k], peer)
            cp.wait_send()
            cp.wait_recv()

    outs = pl.pallas_call(
        body, name=name,
        in_specs=[HBM] * n + [SEM, SEM] + [ANY] * len(after), out_specs=[HBM] * n,
        out_shape=[pltpu.HBM(b.shape, b.dtype) for b in bufs],
        input_output_aliases={a: a for a in range(n)},
        compiler_params=SPLIT,
    )(*bufs, ssem, rsem, *after)
    return list(outs)


def _shard_half(buf, chip, half):
    if len(buf.shape) == 2:
        h, w = buf.shape[0] // 2, buf.shape[1] // N_CHIPS
        return buf.at[pl.ds(half * h, h), pl.ds(chip * w, w)]
    h = buf.shape[1] // 2
    return buf.at[chip, pl.ds(half * h, h)]


def plan_gather(refs, x, y, c):
    out = []
    for (px, py) in _chips(x, y):
        for buf in refs:
            own = _shard_half(buf, 2 * x + y, c)
            out.append((own, own, (px, py, c), _shard_half(buf, 2 * px + py, c)))
    return out


def plan_forward(refs, x, y, c):
    out = []
    for (px, py) in _chips(x, y):
        for buf in refs:
            landed = _shard_half(buf, 2 * px + py, c)
            out.append((landed, landed, (x, y, 1 - c), _shard_half(buf, 2 * px + py, 1 - c)))
    return out


def plan_sibling(refs, x, y, c):
    n = len(refs) // 2
    out = []
    for a in range(n):
        h = refs[a].shape[1] // 2
        out.append((refs[a].at[:, pl.ds((1 - c) * h, h)], refs[n + a], (x, y, 1 - c), refs[n + a]))
    return out


def plan_chip(refs, x, y, c):
    n = len(refs) // 2
    out = []
    for j, (px, py) in enumerate(_chips(x, y)):
        for a in range(n):
            out.append((refs[a].at[2 * px + py], refs[n + a].at[j], (px, py, c), refs[n + a].at[j]))
    return out


def plan_pack(refs, x, y, c):
    (packs,) = refs
    mine = packs.at[4 * x + 2 * y + c]
    return [(mine, mine, peer, packs.at[4 * peer[0] + 2 * peer[1] + peer[2]]) for peer in _peers7(x, y, c)]


def plan_spread(layers, wp_layers):
    def plan(refs, x, y, c):
        gi, go, gp = refs
        hD, hR, hP = gi.shape[1] // 2, go.shape[1] // 2, gp.shape[2] // 2
        sib = (x, y, 1 - c)
        out = []
        for l in layers:
            mine = gi.at[l, pl.ds(c * hD, hD)]
            out.append((mine, mine, sib, gi.at[l, pl.ds((1 - c) * hD, hD)]))
            mine = go.at[l, pl.ds(c * hR, hR)]
            out.append((mine, mine, sib, go.at[l, pl.ds((1 - c) * hR, hR)]))
        for l in wp_layers:
            mine = gp.at[l, 2 * x + y, pl.ds(c * hP, hP)]
            for peer in _peers7(x, y, c):
                out.append((mine, mine, peer, gp.at[l, 2 * peer[0] + peer[1], pl.ds(peer[2] * hP, hP)]))
        return out

    return plan


def gather_small(c8, wc, token):
    def body(c_ref, wc_ref, token_ref, call, wcall, ssem, rsem, lsem):
        x, y, c = _me()
        myc = 2 * x + y
        me_lin = 4 * x + 2 * y + c
        me = (x, y, c)
        local = [pltpu.make_async_copy(c_ref, call.at[me_lin], lsem.at[0]),
                 pltpu.make_async_copy(wc_ref, wcall.at[myc], lsem.at[1])]
        for cp in local:
            cp.start()
        sends, recvs = [], []
        for m, peer in enumerate(_peers7(x, y, c)):
            plin = 4 * peer[0] + 2 * peer[1] + peer[2]
            sends.append(_rcopy(c_ref, call.at[me_lin], ssem.at[m], rsem.at[m], peer))
            recvs.append(_rcopy(call.at[plin], call.at[plin], ssem.at[m], rsem.at[m], me))
        for j, (px, py) in enumerate([(1 - x, y), (x, 1 - y), (1 - x, 1 - y)]):
            pc = 2 * px + py
            sends.append(_rcopy(wc_ref, wcall.at[myc], ssem.at[7 + j], rsem.at[7 + j], (px, py, c)))
            recvs.append(_rcopy(wcall.at[pc], wcall.at[pc], ssem.at[7 + j], rsem.at[7 + j], me))
        for cp in sends:
            cp.start()
        for cp in recvs:
            cp.wait_recv()
        for cp in sends:
            cp.wait_send()
        for cp in local:
            cp.wait()

    return pl.pallas_call(
        body, name="gather_small",
        in_specs=[ANY] * 3, out_specs=[ANY] * 2,
        out_shape=[jax.ShapeDtypeStruct((N_DEV, SUBLANES, LANES), F32),
                   jax.ShapeDtypeStruct((N_CHIPS, wc.shape[0], 3, LANES), F32)],
        scratch_shapes=[pltpu.SemaphoreType.DMA((10,)), pltpu.SemaphoreType.DMA((10,)), pltpu.SemaphoreType.DMA((2,))],
        compiler_params=_params(n_grid=0),
    )(c8, wc, token)


def exchange_mod(part):
    def body(p_ref, o_ref, ssem, rsem, lsem):
        x, y, c = _me()
        myc = 2 * x + y
        chips = [(1 - x, y), (x, 1 - y), (1 - x, 1 - y)]
        loc = pltpu.make_async_copy(p_ref, o_ref.at[myc], lsem)
        loc.start()
        sends = [_rcopy(p_ref, o_ref.at[myc], ssem.at[j], rsem.at[j], (px, py, c)) for j, (px, py) in enumerate(chips)]
        for cp in sends:
            cp.start()
        for j, (px, py) in enumerate(chips):
            slot = o_ref.at[2 * px + py]
            _rcopy(slot, slot, ssem.at[j], rsem.at[j], (x, y, c)).wait_recv()
        for cp in sends:
            cp.wait_send()
        loc.wait()

    return pl.pallas_call(
        body, name="exchange_mod", in_specs=[ANY], out_specs=ANY,
        out_shape=jax.ShapeDtypeStruct((N_CHIPS,) + part.shape, part.dtype),
        scratch_shapes=[pltpu.SemaphoreType.DMA((3,)), pltpu.SemaphoreType.DMA((3,)), pltpu.SemaphoreType.DMA],
        compiler_params=_params(n_grid=0),
    )(part)


def spread_now(gi, go, gp, layers, wp_layers):
    plan = plan_spread(layers, wp_layers)
    n = 2 * len(layers) + 7 * len(wp_layers)

    def body(gi_in, go_in, gp_in, gi, go, gp, ssem, rsem):
        copies = plan((gi, go, gp), *_me())
        me = _me()
        sends = [_rcopy(src, dst, ssem.at[k], rsem.at[k], peer) for k, (src, dst, peer, _) in enumerate(copies)]
        for cp in sends:
            cp.start()
        for k, (_, _, _, land) in enumerate(copies):
            _rcopy(land, land, ssem.at[k], rsem.at[k], me).wait_recv()
        for cp in sends:
            cp.wait_send()

    return pl.pallas_call(
        body, name="spread_now",
        in_specs=[ANY] * 3, out_specs=[ANY] * 3,
        out_shape=[jax.ShapeDtypeStruct(a.shape, a.dtype) for a in (gi, go, gp)],
        input_output_aliases={0: 0, 1: 1, 2: 2},
        scratch_shapes=[pltpu.SemaphoreType.DMA((n,)), pltpu.SemaphoreType.DMA((n,))],
        compiler_params=_params(n_grid=0),
    )(gi, go, gp)


def add_sibling(cidx, mine, sib):
    def body(c_ref, *refs):
        for a in range(3):
            m, s, o = refs[a], refs[3 + a], refs[6 + a]
            o[...] = (m[...].astype(F32) + s[...].astype(F32)).astype(BF16)

    def mine_spec(a):
        h = a.shape[1] // 2
        return pl.BlockSpec((None, h, a.shape[2]), lambda j, c_ref: (j, c_ref[0], 0))

    def sib_spec(a):
        return pl.BlockSpec((None,) + a.shape[1:], lambda j, c_ref: (j, 0, 0))

    return pl.pallas_call(
        body, name="add_sibling",
        grid_spec=pltpu.PrefetchScalarGridSpec(
            num_scalar_prefetch=1, grid=(N_CHIPS,),
            in_specs=[mine_spec(a) for a in mine] + [sib_spec(a) for a in sib],
            out_specs=[sib_spec(a) for a in sib]),
        out_shape=[jax.ShapeDtypeStruct(a.shape, BF16) for a in sib],
        compiler_params=_params(VMEM_BIG),
    )(cidx, *mine, *sib)


def sum_chips(pos, own, rb, acc, l, shapes):
    nq = 4
    n_in = 6 + (3 if acc is not None else 0)

    def body(pos_ref, *refs):
        for a in range(3):
            m, b, o = refs[a], refs[3 + a], refs[n_in + a]
            s = m[...].astype(F32)
            for j in range(3):
                s = s + b[j].astype(F32)
            o[...] = s

    def own_spec(a):
        return pl.BlockSpec((None, a.shape[1] // nq, a.shape[2]), lambda q, p: (p[1], q, 0))

    def rb_spec(a):
        return pl.BlockSpec((3, a.shape[1] // nq, a.shape[2]), lambda q, p: (0, q, 0))

    hi, ho, hp = own[0].shape[1] // nq, own[1].shape[1] // nq, own[2].shape[1] // nq
    out_specs = [pl.BlockSpec((None, hi, shapes[0][2]), lambda q, p: (l, p[0] * nq + q, 0)),
                 pl.BlockSpec((None, ho, shapes[1][2]), lambda q, p: (l, p[0] * nq + q, 0)),
                 pl.BlockSpec((None, None, hp, LANES), lambda q, p: (l, p[1], p[0] * nq + q, 0))]
    in_specs = [own_spec(a) for a in own] + [rb_spec(a) for a in rb]
    args = list(own) + list(rb)
    aliases = {}
    if acc is not None:
        in_specs += [ANY] * 3
        args += list(acc)
        aliases = {7: 0, 8: 1, 9: 2}
    return pl.pallas_call(
        body, name="sum_chips",
        grid_spec=pltpu.PrefetchScalarGridSpec(num_scalar_prefetch=1, grid=(nq,), in_specs=in_specs, out_specs=out_specs),
        out_shape=[jax.ShapeDtypeStruct(s, F32) for s in shapes],
        input_output_aliases=aliases,
        compiler_params=_params(VMEM_BIG),
    )(pos, *args)


def pack_small(pos, per_layer, loss_blk):
    L = len(per_layer)
    D = per_layer[0][0].shape[1]

    def body(pos_ref, *refs):
        o = refs[-1]
        lb = refs[-2]
        o[...] = jnp.zeros_like(o)
        for l in range(L):
            dgpre, dgpost, dsh, dsc, dgt, dps, dwc = refs[7 * l:7 * l + 7]
            base = SUBLANES * l
            for r, src in enumerate((dgpre, dgpost, dsh, dsc, dgt)):
                o[pl.ds(base + r, 1), :] = src[...]
            o[pl.ds(base + 5, 1), 0:dps.shape[1]] = dps[...]
            for j in range(dwc.shape[0]):
                for k in range(3):
                    idx = 3 * j + k
                    o[pl.ds(base + 6 + idx // 8, 1), (idx % 8) * LANES:(idx % 8 + 1) * LANES] = dwc[j, pl.ds(k, 1), :]
        o[pl.ds(5, 1), 4 * LANES:5 * LANES] = lb[pl.ds(0, 1), :]

    flat = [a for layer in per_layer for a in layer] + [loss_blk]

    def whole(a):
        return pl.BlockSpec(a.shape, lambda i, p: (0,) * a.ndim)

    return pl.pallas_call(
        body, name="pack_small",
        grid_spec=pltpu.PrefetchScalarGridSpec(
            num_scalar_prefetch=1, grid=(1,), in_specs=[whole(a) for a in flat],
            out_specs=pl.BlockSpec((None, L * SUBLANES, D), lambda i, p: (p[2], 0, 0))),
        out_shape=jax.ShapeDtypeStruct((N_DEV, L * SUBLANES, D), F32),
        compiler_params=_params(),
    )(pos, *flat)


def sum_packs(packs):
    def body(p_ref, o_ref):
        s = p_ref[0]
        for d in range(1, N_DEV):
            s = s + p_ref[d]
        o_ref[...] = s

    return pl.pallas_call(
        body, name="sum_packs",
        out_shape=jax.ShapeDtypeStruct(packs.shape[1:], F32),
        compiler_params=_params(n_grid=0),
    )(packs)


def _adamw_math(w, g, m, v):
    m = ADAM_B1 * m + (1.0 - ADAM_B1) * g
    v = ADAM_B2 * v + (1.0 - ADAM_B2) * (g * g)
    m_hat = m / (1.0 - ADAM_B1 ** ADAM_STEP)
    v_hat = v / (1.0 - ADAM_B2 ** ADAM_STEP)
    delta = -ADAM_LR * (m_hat / (jnp.sqrt(v_hat) + ADAM_EPS) + ADAM_WD * w)
    return delta, m, v


def adamw(w, g, m, v, block, name, first=0, count=None, acc=None):
    grid = tuple(s // b for s, b in zip(w.shape, block))
    if count is not None:
        grid = (count,) + grid[1:]

    def body(w_ref, g_ref, m_ref, v_ref, *rest):
        d_ref, mo_ref, vo_ref = rest[-3:]
        d, mm, vv = _adamw_math(w_ref[...], g_ref[...], m_ref[...], v_ref[...])
        d_ref[...] = d
        mo_ref[...] = mm
        vo_ref[...] = vv

    spec = pl.BlockSpec(block, lambda i, *rest: (first + i,) + rest)
    shape = jax.ShapeDtypeStruct(w.shape, F32)
    extra = [] if acc is None else list(acc)
    return pl.pallas_call(
        body, name=name, grid=grid,
        in_specs=[spec] * 4 + [ANY] * len(extra), out_specs=[spec] * 3, out_shape=[shape] * 3,
        input_output_aliases={4 + a: a for a in range(len(extra))},
        compiler_params=_params(VMEM_BIG, n_grid=len(grid)),
    )(w, g, m, v, *extra)


def ada_finish(c_all, dmod, w, m, v):
    L, D, CW = w.shape
    hD = D // 2

    def body(c_ref, d_ref, w_ref, m_ref, v_ref, g_ref, dl_ref, mo_ref, vo_ref):
        cv = c_ref[...]
        z = jnp.zeros_like(cv)
        ca = jnp.concatenate([cv * jax.nn.sigmoid(cv), z], axis=0).astype(BF16)
        dm = jnp.concatenate([d_ref[0], jnp.zeros_like(d_ref[0])], axis=0).astype(BF16)
        g = lax.dot_general(ca, dm, TN, preferred_element_type=F32)
        g_ref[0] = g
        d, mm, vv = _adamw_math(w_ref[0], g, m_ref[0], v_ref[0])
        dl_ref[0] = d
        mo_ref[0] = mm
        vo_ref[0] = vv

    big = pl.BlockSpec((1, hD, CW), lambda l, h: (l, h, 0))
    shape = jax.ShapeDtypeStruct(w.shape, F32)
    return pl.pallas_call(
        body, name="ada_finish", grid=(L, 2),
        in_specs=[pl.BlockSpec((N_DEV, hD), lambda l, h: (0, h)), pl.BlockSpec((1, N_DEV, CW), lambda l, h: (l, 0, 0)),
                  big, big, big],
        out_specs=[big] * 4, out_shape=[shape] * 4,
        compiler_params=_params(VMEM_BIG, n_grid=2),
    )(c_all, dmod, w, m, v)


def kernel(x, c, w_ada, b_ada, g_pre, w_in, w_conv, w_pool, pool_scale, w_out, g_post, loss_target, m_w_ada, m_b_ada, m_g_pre, m_w_in, m_w_conv, m_w_pool, m_pool_scale, m_w_out, m_g_post, v_w_ada, v_b_ada, v_g_pre, v_w_in, v_w_conv, v_w_pool, v_pool_scale, v_w_out, v_g_post):
    L, D, CW = w_in.shape
    RO = w_out.shape[1]
    T = x.shape[1]
    ix, iy, ic = _me()
    chip = 2 * ix + iy
    me_lin = 4 * ix + 2 * iy + ic

    pos = jnp.stack([ic, chip, me_lin]).astype(jnp.int32)
    g_pre3, g_post3 = g_pre.reshape(L, 1, D), g_post.reshape(L, 1, D)
    pscale3 = pool_scale.reshape(L, 1, pool_scale.shape[1])
    n_g, n_s, n_c = 6, 3, 9

    c_all3, wconv_all = gather_small(c.reshape(SUBLANES, LANES), w_conv, pos)
    c_all = c_all3.reshape(N_DEV, D)
    gath = [None] * L
    ss, rs, bufs, token = xchg_start("gather_start", list(cast_weights(pos, w_in, w_out, 0, c_all3)), n_g, plan_gather)
    gath[0] = (ss, rs, bufs)
    b_my = lax.dynamic_slice_in_dim(b_ada, chip * CW, CW, axis=1)
    mod_all = exchange_mod(mod_part(c_all, w_ada, b_my, token))
    mod = lax.dynamic_index_in_dim(mod_all, me_lin, axis=2, keepdims=False)
    mod4 = jnp.transpose(mod, (1, 0, 2)).reshape(L, 3, 1, D)
    token = mod_all
    for l in range(1, L):
        ss, rs, bufs, token = xchg_start("gather_start", list(cast_weights(pos, w_in, w_out, l, token)), n_g, plan_gather)
        gath[l] = (ss, rs, bufs)

    def arrive(l, after):
        ss, rs, bufs = gath[l]
        bufs = xchg_wait("gather_wait", bufs, ss, rs, n_g, plan_gather, after)
        return xchg_start("forward_start", bufs, n_g, plan_forward)

    xs, projs, yas, yps, ys = [x.reshape(T, D)], [], [], [], []
    wg_in, wg_out = [], []
    fwd = arrive(0, [token])
    for l in range(L):
        fss, frs, bufs, ftoken = fwd
        gi, go = xchg_wait("forward_wait", bufs, fss, frs, n_g, plan_forward, [ftoken if l == 0 else xs[l]])
        wg_in.append(gi)
        wg_out.append(go.reshape(N_CHIPS * RO, D))
        proj = proj_fwd(xs[l], mod4, g_pre3, wg_in[l], l)
        ya = conv_fwd(proj, wconv_all, l)
        yp = pool_fwd(proj, w_pool, pscale3, l)
        token = yp
        if l + 1 < L:
            fwd = arrive(l + 1, [ya, yp])
            token = fwd[3]
        xn, yv = out_fwd(ya, yp, wg_out[l], xs[l], mod4, g_post3, l, token)
        xs.append(xn)
        projs.append(proj)
        yas.append(ya)
        yps.append(yp)
        ys.append(yv)

    dx, loss_blk = loss_head(xs[L], loss_target.reshape(T, D))

    shapes = (w_in.shape, w_out.shape, w_pool.shape)
    smalls = [None] * L
    acc, flying, sib, token = None, None, None, loss_blk

    def to_chips(sib, after):
        sl, s_ss, s_rs, s_bufs = sib
        s_bufs = xchg_wait("sibling_wait", s_bufs, s_ss, s_rs, n_s, plan_sibling, after)
        chip_parts = add_sibling(pos, s_bufs[0:3], s_bufs[3:6])
        lands = [lax.empty((3,) + a.shape[1:], a.dtype) for a in chip_parts]
        c_ss, c_rs, c_bufs, ctoken = xchg_start("chip_start", list(chip_parts) + lands, n_c, plan_chip)
        return (sl, c_ss, c_rs, c_bufs), ctoken

    def landed(flying, acc, after):
        fl, f_ss, f_rs, f_bufs = flying
        f_bufs = xchg_wait("chip_wait", f_bufs, f_ss, f_rs, n_c, plan_chip, after)
        return sum_chips(pos, f_bufs[0:3], f_bufs[3:6], acc, fl, shapes)

    for l in reversed(range(L)):
        dya, dyp, dwo_l, dgate, dgpost = out_bwd(dx, ys[l], yas[l], yps[l], wg_out[l], mod4, g_post3, l, token)
        token = dya
        if sib is not None:
            arrived = flying
            flying, token = to_chips(sib, [dya])
            if arrived is not None:
                acc = landed(arrived, acc, [token])
                token = acc[0]
        du_p, dg_p, dwp_l, dps = pool_bwd(projs[l], dyp, w_pool, pscale3, l, token)
        du_a, db_a, dc_a, dg_a, dwc = conv_bwd(projs[l], dya, wconv_all, l)
        dx, dwi_l, dshift, dscale, dgpre = in_bwd([du_a, db_a, dc_a, dg_a, du_p, dg_p], wg_in[l], xs[l], dx,
                                                  mod4, g_pre3, l)
        smalls[l] = (dgpre, dgpost, dshift, dscale, dgate, dps, dwc)
        parts = [dwi_l, dwo_l.reshape(N_CHIPS, RO, D), dwp_l]
        s_lands = [lax.empty((a.shape[0], a.shape[1] // 2) + a.shape[2:], a.dtype) for a in parts]
        s_ss, s_rs, s_bufs, token = xchg_start("sibling_start", parts + s_lands, n_s, plan_sibling)
        sib = (l, s_ss, s_rs, s_bufs)
    grad_x = dx.reshape(1, T, D)

    p_ss, p_rs, packs, ptoken = xchg_start("pack_start", [pack_small(pos, smalls, loss_blk)], N_DEV - 1, plan_pack)
    acc = landed(flying, acc, [ptoken, token])
    flying, token = to_chips(sib, [acc[0]])
    n_sp = (2 + N_DEV - 1) * (L - 1)
    spread = plan_spread(tuple(range(1, L)), tuple(range(1, L)))
    sp_ss, sp_rs, acc, sp_token = xchg_start("spread_start", list(acc), n_sp, spread)
    (packs_all,) = xchg_wait("pack_wait", packs, p_ss, p_rs, N_DEV - 1, plan_pack, [sp_token, token])
    small = sum_packs(packs_all).reshape(L, SUBLANES, D)
    loss = small[0, 5, 4 * LANES]
    g_g_pre = small[:, 0]
    g_g_post = small[:, 1]
    g_b_ada = small[:, 2:5].reshape(L, 3 * D)
    g_pscale = small[:, 5, 0:pool_scale.shape[1]]
    g_w_conv = small[:, 6:8].reshape(L, 2 * D)[:, 0:N_CHIPS * 3 * LANES].reshape(L, N_CHIPS, 3, LANES)
    g_w_conv = lax.dynamic_index_in_dim(g_w_conv, chip, axis=1, keepdims=False)
    dmod_all = packs_all.reshape(N_DEV, L, SUBLANES, D)[:, :, 2:5].reshape(N_DEV, L, 3 * D)
    dmod_my = jnp.transpose(lax.dynamic_slice_in_dim(dmod_all, chip * CW, CW, axis=2), (1, 0, 2))

    g_w_ada, d_w_ada, nm_w_ada, nv_w_ada = ada_finish(c_all, dmod_my, w_ada, m_w_ada, v_w_ada)

    def small_adamw(w, g, m, v, name):
        shp = (1,) + w.shape if w.ndim == 2 else w.shape
        outs = adamw(w.reshape(shp), g.reshape(shp), m.reshape(shp), v.reshape(shp), shp, name)
        return [a.reshape(w.shape) for a in outs]

    d_b_ada, nm_b_ada, nv_b_ada = small_adamw(b_ada, g_b_ada, m_b_ada, v_b_ada, "adamw_b_ada")
    d_g_pre, nm_g_pre, nv_g_pre = small_adamw(g_pre, g_g_pre, m_g_pre, v_g_pre, "adamw_g_pre")
    d_w_conv, nm_w_conv, nv_w_conv = small_adamw(w_conv, g_w_conv, m_w_conv, v_w_conv, "adamw_w_conv")
    d_pscale, nm_pscale, nv_pscale = small_adamw(pool_scale, g_pscale, m_pool_scale, v_pool_scale, "adamw_pool_scale")
    d_g_post, nm_g_post, nv_g_post = small_adamw(g_post, g_g_post, m_g_post, v_g_post, "adamw_g_post")

    done = [nv_w_ada, nv_b_ada, nv_g_pre, nv_w_conv, nv_pscale, nv_g_post]
    g_w_in, g_w_out, g_w_pool = xchg_wait("spread_wait", acc, sp_ss, sp_rs, n_sp, spread, done)
    in_blk, out_blk = (1, D // 2, CW), (1, RO, D)
    upd_in = adamw(w_in, g_w_in, m_w_in, v_w_in, in_blk, "adamw_w_in", 1, L - 1)
    upd_out = adamw(w_out, g_w_out, m_w_out, v_w_out, out_blk, "adamw_w_out", 1, L - 1)

    acc = landed(flying, (g_w_in, g_w_out, g_w_pool), [upd_in[2], upd_out[2]])
    g_w_in, g_w_out, g_w_pool = spread_now(*acc, (0,), (0,))
    d_w_in, nm_w_in, nv_w_in = adamw(w_in, g_w_in, m_w_in, v_w_in, in_blk, "adamw_w_in", 0, 1, upd_in)
    d_w_out, nm_w_out, nv_w_out = adamw(w_out, g_w_out, m_w_out, v_w_out, out_blk, "adamw_w_out", 0, 1, upd_out)
    pshape = (L, N_CHIPS * LANES, LANES)
    d_w_pool, nm_w_pool, nv_w_pool = adamw(w_pool.reshape(pshape), g_w_pool.reshape(pshape), m_w_pool.reshape(pshape),
                                           v_w_pool.reshape(pshape), (1,) + pshape[1:], "adamw_w_pool")
    d_w_pool, nm_w_pool, nv_w_pool = [a.reshape(w_pool.shape) for a in (d_w_pool, nm_w_pool, nv_w_pool)]

    return (loss, grad_x,
            g_w_ada, g_b_ada, g_g_pre, g_w_in, g_w_conv, g_w_pool, g_pscale, g_w_out, g_g_post,
            d_w_ada, d_b_ada, d_g_pre, d_w_in, d_w_conv, d_w_pool, d_pscale, d_w_out, d_g_post,
            nm_w_ada, nm_b_ada, nm_g_pre, nm_w_in, nm_w_conv, nm_w_pool, nm_pscale, nm_w_out, nm_g_post,
            nv_w_ada, nv_b_ada, nv_g_pre, nv_w_in, nv_w_conv, nv_w_pool, nv_pscale, nv_w_out, nv_g_post)
```

```python
import functools

import jax
import jax.numpy as jnp
from jax import lax
from jax.experimental import pallas as pl
from jax.experimental.pallas import tpu as pltpu

F32 = jnp.float32
BF16 = jnp.bfloat16
MESH = pl.DeviceIdType.MESH
ANY = pl.BlockSpec(memory_space=pl.ANY)

NORM_EPS = 1e-6
POOL_WINDOWS = (2, 4, 8, 16)
ADAM_LR = 0.001
ADAM_B1 = 0.9
ADAM_B2 = 0.999
ADAM_EPS = 1e-08
ADAM_WD = 0.01
ADAM_STEP = 10

N_CHIPS = 4
N_DEV = 8
LANES = 128
SUBLANES = 8
VMEM_BIG = 56 * 1024 * 1024
HIST = 16
R_CONV = 32
R_POOL = 64

NT = (((1,), (1,)), ((), ()))
TN = (((0,), (0,)), ((), ()))


def _params(vmem=None, n_grid=1):
    kw = {}
    if n_grid:
        kw["dimension_semantics"] = ("arbitrary",) * n_grid
    if vmem is not None:
        kw["vmem_limit_bytes"] = vmem
    return pltpu.CompilerParams(**kw)


def _colsum8(v):
    n, d = v.shape
    return v.reshape(n // SUBLANES, SUBLANES, d).sum(axis=0)


def _rms(v):
    return lax.rsqrt(jnp.mean(v * v, axis=-1, keepdims=True) + NORM_EPS)


def _sigmoid(v):
    return 0.5 * jnp.tanh(0.5 * v) + 0.5


def _shift_down(ext, k, rows):
    if k == 0:
        return ext[HIST:HIST + rows]
    return pltpu.roll(ext, k, 0)[HIST:HIST + rows]


def _shift_up(ext, k, rows):
    if k == 0:
        return ext[0:rows]
    return pltpu.roll(ext, ext.shape[0] - k, 0)[0:rows]


def _load_ext(ref, r0, h0, first, rows):
    hist = ref[pl.ds(h0, HIST), :].astype(F32)
    hist = jnp.where(first, 0.0, hist)
    cur = ref[pl.ds(r0, rows), :].astype(F32)
    return jnp.concatenate([hist, cur], axis=0)


def _me():
    return lax.axis_index("x"), lax.axis_index("y"), lax.axis_index("c")


def cast_weights(pos, w_in, w_out, l, after):
    _, D, CW = w_in.shape
    RO = w_out.shape[1]

    def body(pos_ref, wi, wo, after_ref, oi, oo):
        oi[...] = wi[...].astype(BF16)
        oo[...] = wo[...].astype(BF16)

    return pl.pallas_call(
        body, name="cast_w",
        grid_spec=pltpu.PrefetchScalarGridSpec(
            num_scalar_prefetch=1, grid=(2,),
            in_specs=[pl.BlockSpec((None, D // 2, CW), lambda h, p: (l, h, 0)),
                      pl.BlockSpec((None, RO // 2, D), lambda h, p: (l, h, 0)), ANY],
            out_specs=[pl.BlockSpec((D // 2, CW), lambda h, p: (h, p[1])),
                       pl.BlockSpec((None, RO // 2, D), lambda h, p: (p[1], h, 0))]),
        out_shape=[jax.ShapeDtypeStruct((D, N_CHIPS * CW), BF16), jax.ShapeDtypeStruct((N_CHIPS, RO, D), BF16)],
        compiler_params=_params(),
    )(pos, w_in, w_out, after)


def mod_part(pos, c_all, w_ada, b_my, after):
    L, D, CW = w_ada.shape

    def body(pos_ref, c_ref, w_ref, b_ref, after_ref, o_ref):
        cv = c_ref[...]
        ca = (cv * jax.nn.sigmoid(cv)).astype(BF16)
        o_ref[...] = jnp.dot(ca, w_ref[0].astype(BF16), preferred_element_type=F32) + b_ref[0]

    return pl.pallas_call(
        body, name="mod_part",
        grid_spec=pltpu.PrefetchScalarGridSpec(
            num_scalar_prefetch=1, grid=(L,),
            in_specs=[pl.BlockSpec((N_DEV, D), lambda l, p: (0, 0)),
                      pl.BlockSpec((1, D, CW), lambda l, p: (l, 0, 0)),
                      pl.BlockSpec((1, 1, CW), lambda l, p: (l, 0, 0)), ANY],
            out_specs=pl.BlockSpec((None, None, N_DEV, CW), lambda l, p: (p[1], l, 0, 0))),
        out_shape=jax.ShapeDtypeStruct((N_CHIPS, L, N_DEV, CW), F32),
        compiler_params=_params(VMEM_BIG),
    )(pos, c_all, w_ada, b_my.reshape(L, 1, CW), after)


def _mod_row(l, k, D):
    return pl.BlockSpec((None, None, 1, D), lambda *_: (l, k, 0, 0))


def _layer_row(l, D):
    return pl.BlockSpec((None, 1, D), lambda *_: (l, 0, 0))


def proj_fwd(x, mod4, g_pre3, wg, l):
    T, D = x.shape
    NC = wg.shape[1]
    NB = N_CHIPS
    CW = NC // NB
    tm = 512

    def body(x_ref, sh_ref, sc_ref, g_ref, w_ref, o_ref):
        xv = x_ref[...]
        h = (xv * _rms(xv) * g_ref[...]) * (1.0 + sc_ref[...]) + sh_ref[...]
        hb = h.astype(BF16)
        for j in range(NB):
            cols = slice(j * CW, (j + 1) * CW)
            o_ref[:, cols] = jnp.dot(hb, w_ref[:, cols], preferred_element_type=F32).astype(BF16)

    return pl.pallas_call(
        body, name="proj_fwd", grid=(T // tm,),
        in_specs=[pl.BlockSpec((tm, D), lambda i: (i, 0)), _mod_row(l, 0, D), _mod_row(l, 1, D), _layer_row(l, D),
                  pl.BlockSpec((D, NC), lambda i: (0, 0))],
        out_specs=pl.BlockSpec((tm, NC), lambda i: (i, 0)),
        out_shape=jax.ShapeDtypeStruct((T, NC), BF16),
        compiler_params=_params(VMEM_BIG),
    )(x, mod4, mod4, g_pre3, wg)


def conv_fwd(proj, wconv, l):
    T = proj.shape[0]
    R = R_CONV
    nblk = 4

    def body(u_ref, b_ref, c_ref, g_ref, w_ref, o_ref):
        w0 = w_ref[pl.ds(0, 1), :]
        w1 = w_ref[pl.ds(1, 1), :]
        w2 = w_ref[pl.ds(2, 1), :]

        def chunk(i, carry):
            r0 = pl.multiple_of(i * R, R)
            h0 = pl.multiple_of(jnp.maximum(r0 - HIST, 0), HIST)
            first = i == 0
            ca = _load_ext(c_ref, r0, h0, first, R) * _load_ext(u_ref, r0, h0, first, R)
            conv = w2 * ca[HIST:] + w1 * _shift_down(ca, 1, R) + w0 * _shift_down(ca, 2, R)
            g = g_ref[pl.ds(r0, R), :].astype(F32)
            b = b_ref[pl.ds(r0, R), :].astype(F32)
            o_ref[pl.ds(r0, R), :] = (b * conv * (g * _sigmoid(g))).astype(BF16)
            return carry

        lax.fori_loop(0, T // R, chunk, 0)

    def col(off):
        return pl.BlockSpec((T, LANES), lambda j: (0, j + off))

    return pl.pallas_call(
        body, name="conv_fwd", grid=(nblk,),
        in_specs=[col(0), col(4), col(8), col(12), pl.BlockSpec((None, None, 3, LANES), lambda j: (j, l, 0, 0))],
        out_specs=pl.BlockSpec((T, LANES), lambda j: (0, j)),
        out_shape=jax.ShapeDtypeStruct((T, nblk * LANES), BF16),
        compiler_params=_params(),
    )(proj, proj, proj, proj, wconv)


def _causal_window_sum(ext, w):
    s, k = ext, 1
    while k < w:
        s = s + pltpu.roll(s, k, 0)
        k *= 2
    return s


def _anticausal_window_sum(ext, w):
    s, k = ext, 1
    n = ext.shape[0]
    while k < w:
        s = s + pltpu.roll(s, n - k, 0)
        k *= 2
    return s


def _count(r0, rows, w):
    t = r0 + lax.broadcasted_iota(jnp.int32, (rows, LANES), 0)
    return jnp.minimum(t + 1, w).astype(F32)


def _pooled_loop(p_ref, pooled_s, w, T):
    R = R_POOL

    def chunk(i, carry):
        r0 = pl.multiple_of(i * R, R)
        h0 = pl.multiple_of(jnp.maximum(r0 - HIST, 0), HIST)
        ext = _load_ext(p_ref, r0, h0, i == 0, R)
        ws = _causal_window_sum(ext, w)[HIST:]
        pooled_s[pl.ds(r0, R), :] = (ws / _count(r0, R, w) - ext[HIST:]).astype(BF16)
        return carry

    lax.fori_loop(0, T // R, chunk, 0)


def _pool_w_spec(l):
    return pl.BlockSpec((None, None, LANES, LANES), lambda j: (l, j, 0, 0))


def _pool_s_spec(l):
    return pl.BlockSpec((None, 1, LANES), lambda j: (l, 0, j))


def pool_fwd(proj, wpool, pscale3, l):
    T = proj.shape[0]
    R = R_POOL
    ngrp = len(POOL_WINDOWS)

    def body(p_ref, g_ref, w_ref, s_ref, o_ref, pooled_s, mixed_s):
        grp = pl.program_id(0)

        def group(w):
            _pooled_loop(p_ref, pooled_s, w, T)
            mixed_s[...] = jnp.dot(pooled_s[...], w_ref[...].astype(BF16), preferred_element_type=F32)
            sc = s_ref[...]

            def chunk(i, carry):
                r0 = pl.multiple_of(i * R, R)
                g = g_ref[pl.ds(r0, R), :].astype(F32)
                o_ref[pl.ds(r0, R), :] = (mixed_s[pl.ds(r0, R), :] * sc * (g * _sigmoid(g))).astype(BF16)
                return carry

            lax.fori_loop(0, T // R, chunk, 0)

        for k, w in enumerate(POOL_WINDOWS):
            pl.when(grp == k)(functools.partial(group, w))

    return pl.pallas_call(
        body, name="pool_fwd", grid=(ngrp,),
        in_specs=[pl.BlockSpec((T, LANES), lambda j: (0, j + 16)), pl.BlockSpec((T, LANES), lambda j: (0, j + 20)),
                  _pool_w_spec(l), _pool_s_spec(l)],
        out_specs=pl.BlockSpec((T, LANES), lambda j: (0, j)),
        out_shape=jax.ShapeDtypeStruct((T, ngrp * LANES), BF16),
        scratch_shapes=[pltpu.VMEM((T, LANES), BF16), pltpu.VMEM((T, LANES), F32)],
        compiler_params=_params(),
    )(proj, proj, wpool, pscale3)


def out_fwd(ya, yp, wo, x, mod4, g_post3, l, after):
    T, D = x.shape
    H = ya.shape[1]
    tm = 512

    def body(ya_ref, yp_ref, wo_ref, x_ref, gt_ref, g_ref, after_ref, xn_ref, y_ref):
        y = (jnp.dot(ya_ref[...], wo_ref[0:H, :], preferred_element_type=F32)
             + jnp.dot(yp_ref[...], wo_ref[H:2 * H, :], preferred_element_type=F32))
        xn_ref[...] = x_ref[...] + gt_ref[...] * (y * _rms(y) * g_ref[...])
        y_ref[...] = y

    tile = pl.BlockSpec((tm, D), lambda i: (i, 0))
    half = pl.BlockSpec((tm, H), lambda i: (i, 0))
    return pl.pallas_call(
        body, name="out_fwd", grid=(T // tm,),
        in_specs=[half, half, pl.BlockSpec((2 * H, D), lambda i: (0, 0)), tile, _mod_row(l, 2, D), _layer_row(l, D),
                  ANY],
        out_specs=[tile, tile],
        out_shape=[jax.ShapeDtypeStruct((T, D), F32), jax.ShapeDtypeStruct((T, D), F32)],
        compiler_params=_params(VMEM_BIG),
    )(ya, yp, wo, x, mod4, g_post3, after)


def loss_head(xl, target):
    T, D = xl.shape
    tm = 512
    nt = T // tm

    def body(x_ref, t_ref, dx_ref, l_ref, acc):
        i = pl.program_id(0)

        @pl.when(i == 0)
        def _():
            acc[...] = jnp.zeros_like(acc)

        d = x_ref[...] - t_ref[...]
        dx_ref[...] = d * (1.0 / D)
        acc[...] += _colsum8(d * d)

        @pl.when(i == nt - 1)
        def _():
            l_ref[...] = jnp.zeros_like(l_ref) + jnp.sum(acc[...]) * (0.5 / D)

    tile = pl.BlockSpec((tm, D), lambda i: (i, 0))
    return pl.pallas_call(
        body, name="loss_head", grid=(nt,),
        in_specs=[tile, tile],
        out_specs=[tile, pl.BlockSpec((SUBLANES, LANES), lambda i: (0, 0))],
        out_shape=[jax.ShapeDtypeStruct((T, D), F32), jax.ShapeDtypeStruct((SUBLANES, LANES), F32)],
        scratch_shapes=[pltpu.VMEM((SUBLANES, D), F32)],
        compiler_params=_params(VMEM_BIG),
    )(xl, target)


def out_bwd(dx, y, ya, yp, wo, mod4, g_post3, l, after):
    T, D = dx.shape
    H = ya.shape[1]
    tm = 512
    nt = T // tm

    def body(dx_ref, y_ref, ya_ref, yp_ref, wo_ref, gt_ref, g_ref, after_ref,
             dya_ref, dyp_ref, dwo_ref, dgt_ref, dg_ref, acc_w, acc_gt, acc_g):
        i = pl.program_id(0)

        @pl.when(i == 0)
        def _():
            acc_w[...] = jnp.zeros_like(acc_w)
            acc_gt[...] = jnp.zeros_like(acc_gt)
            acc_g[...] = jnp.zeros_like(acc_g)

        yv = y_ref[...]
        dxv = dx_ref[...]
        g = g_ref[...]
        r = _rms(yv)
        yn = yv * r
        acc_gt[...] += _colsum8(dxv * (yn * g))
        dn = dxv * gt_ref[...]
        acc_g[...] += _colsum8(dn * yn)
        a = dn * g
        dy = r * (a - yn * jnp.mean(a * yn, axis=-1, keepdims=True))
        dyb = dy.astype(BF16)
        dyc = lax.dot_general(dyb, wo_ref[...], NT, preferred_element_type=F32)
        dya_ref[...] = dyc[:, 0:H].astype(BF16)
        dyp_ref[...] = dyc[:, H:2 * H].astype(BF16)
        acc_w[0:H, :] += lax.dot_general(ya_ref[...], dyb, TN, preferred_element_type=F32)
        acc_w[H:2 * H, :] += lax.dot_general(yp_ref[...], dyb, TN, preferred_element_type=F32)

        @pl.when(i == nt - 1)
        def _():
            dwo_ref[...] = acc_w[...].astype(BF16)
            dgt_ref[...] = jnp.sum(acc_gt[...], axis=0, keepdims=True)
            dg_ref[...] = jnp.sum(acc_g[...], axis=0, keepdims=True)

    row = pl.BlockSpec((1, D), lambda i: (0, 0))
    tile = pl.BlockSpec((tm, D), lambda i: (i, 0))
    half = pl.BlockSpec((tm, H), lambda i: (i, 0))
    full = pl.BlockSpec((2 * H, D), lambda i: (0, 0))
    return pl.pallas_call(
        body, name="out_bwd", grid=(nt,),
        in_specs=[tile, tile, half, half, full, _mod_row(l, 2, D), _layer_row(l, D), ANY],
        out_specs=[half, half, full, row, row],
        out_shape=[jax.ShapeDtypeStruct((T, H), BF16), jax.ShapeDtypeStruct((T, H), BF16),
                   jax.ShapeDtypeStruct((2 * H, D), BF16),
                   jax.ShapeDtypeStruct((1, D), F32), jax.ShapeDtypeStruct((1, D), F32)],
        scratch_shapes=[pltpu.VMEM((2 * H, D), F32), pltpu.VMEM((SUBLANES, D), F32), pltpu.VMEM((SUBLANES, D), F32)],
        compiler_params=_params(VMEM_BIG),
    )(dx, y, ya, yp, wo, mod4, g_post3, after)


def conv_bwd(proj, dya, wconv, l):
    T = proj.shape[0]
    R = R_CONV
    nblk = 4
    nchunk = T // R

    def body(u_ref, b_ref, c_ref, g_ref, dy_ref, w_ref, du_ref, db_ref, dc_ref, dg_ref, dw_ref):
        w0 = w_ref[pl.ds(0, 1), :]
        w1 = w_ref[pl.ds(1, 1), :]
        w2 = w_ref[pl.ds(2, 1), :]

        def chunk(k, carry):
            head, a0, a1, a2 = carry
            i = nchunk - 1 - k
            r0 = pl.multiple_of(i * R, R)
            h0 = pl.multiple_of(jnp.maximum(r0 - HIST, 0), HIST)
            first = i == 0
            ue = _load_ext(u_ref, r0, h0, first, R)
            ce = _load_ext(c_ref, r0, h0, first, R)
            ca = ce * ue
            ca0 = ca[HIST:]
            ca1 = _shift_down(ca, 1, R)
            ca2 = _shift_down(ca, 2, R)
            conv = w2 * ca0 + w1 * ca1 + w0 * ca2
            g = g_ref[pl.ds(r0, R), :].astype(F32)
            b = b_ref[pl.ds(r0, R), :].astype(F32)
            dy = dy_ref[pl.ds(r0, R), :].astype(F32)
            sg = _sigmoid(g)
            sl = g * sg
            t = dy * conv
            db_ref[pl.ds(r0, R), :] = (t * sl).astype(BF16)
            dg_ref[pl.ds(r0, R), :] = (t * b * (sg * (1.0 + g * (1.0 - sg)))).astype(BF16)
            dconv = dy * b * sl
            a2 = a2 + _colsum8(dconv * ca0)
            a1 = a1 + _colsum8(dconv * ca1)
            a0 = a0 + _colsum8(dconv * ca2)
            e = jnp.concatenate([dconv, head], axis=0)
            dca = w2 * dconv + w1 * _shift_up(e, 1, R) + w0 * _shift_up(e, 2, R)
            du_ref[pl.ds(r0, R), :] = (dca * ce[HIST:]).astype(BF16)
            dc_ref[pl.ds(r0, R), :] = (dca * ue[HIST:]).astype(BF16)
            return dconv[0:SUBLANES], a0, a1, a2

        z = jnp.zeros((SUBLANES, LANES), F32)
        _, a0, a1, a2 = lax.fori_loop(0, nchunk, chunk, (z, z, z, z))
        dw_ref[pl.ds(0, 1), :] = jnp.sum(a0, axis=0, keepdims=True)
        dw_ref[pl.ds(1, 1), :] = jnp.sum(a1, axis=0, keepdims=True)
        dw_ref[pl.ds(2, 1), :] = jnp.sum(a2, axis=0, keepdims=True)

    def col(off):
        return pl.BlockSpec((T, LANES), lambda j: (0, j + off))

    sec = jax.ShapeDtypeStruct((T, nblk * LANES), BF16)
    return pl.pallas_call(
        body, name="conv_bwd", grid=(nblk,),
        in_specs=[col(0), col(4), col(8), col(12), col(0), pl.BlockSpec((None, None, 3, LANES), lambda j: (j, l, 0, 0))],
        out_specs=[col(0), col(0), col(0), col(0), pl.BlockSpec((None, 3, LANES), lambda j: (j, 0, 0))],
        out_shape=[sec, sec, sec, sec, jax.ShapeDtypeStruct((nblk, 3, LANES), F32)],
        compiler_params=_params(),
    )(proj, proj, proj, proj, dya, wconv)


def pool_bwd(proj, dyp, wpool, pscale3, l, after):
    T = proj.shape[0]
    R = R_POOL
    ngrp = len(POOL_WINDOWS)
    nchunk = T // R

    def body(p_ref, g_ref, dy_ref, w_ref, s_ref, after_ref, du_ref, dg_ref, dw_ref, ds_ref,
             pooled_s, mixed_s, dmix_s, dpool_s):
        grp = pl.program_id(0)

        def group(w):
            wb = w_ref[...].astype(BF16)
            _pooled_loop(p_ref, pooled_s, w, T)
            mixed_s[...] = jnp.dot(pooled_s[...], wb, preferred_element_type=F32)
            sc = s_ref[...]

            def gate_chunk(i, acc):
                r0 = pl.multiple_of(i * R, R)
                g = g_ref[pl.ds(r0, R), :].astype(F32)
                dy = dy_ref[pl.ds(r0, R), :].astype(F32)
                mixed = mixed_s[pl.ds(r0, R), :]
                sg = _sigmoid(g)
                dg_ref[pl.ds(r0, R), :] = (dy * mixed * sc * (sg * (1.0 + g * (1.0 - sg)))).astype(BF16)
                dms = dy * (g * sg)
                dmix_s[pl.ds(r0, R), :] = (dms * sc).astype(BF16)
                return acc + _colsum8(dms * mixed)

            acc = lax.fori_loop(0, nchunk, gate_chunk, jnp.zeros((SUBLANES, LANES), F32))
            ds_ref[...] = jnp.sum(acc, axis=0, keepdims=True)
            dpool_s[pl.ds(0, T), :] = lax.dot_general(dmix_s[...], wb, NT, preferred_element_type=F32)
            dpool_s[pl.ds(T, HIST), :] = jnp.zeros((HIST, LANES), F32)
            dw_ref[...] = lax.dot_general(pooled_s[...], dmix_s[...], TN, preferred_element_type=F32).astype(BF16)

            def back_chunk(i, carry):
                r0 = pl.multiple_of(i * R, R)
                dpe = dpool_s[pl.ds(r0, R + HIST), :]
                e = dpe / _count(r0, R + HIST, w)
                du_ref[pl.ds(r0, R), :] = (_anticausal_window_sum(e, w)[0:R] - dpe[0:R]).astype(BF16)
                return carry

            lax.fori_loop(0, nchunk, back_chunk, 0)

        for k, w in enumerate(POOL_WINDOWS):
            pl.when(grp == k)(functools.partial(group, w))

    def col(off):
        return pl.BlockSpec((T, LANES), lambda j: (0, j + off))

    sec = jax.ShapeDtypeStruct((T, ngrp * LANES), BF16)
    wspec = pl.BlockSpec((None, LANES, LANES), lambda j: (j, 0, 0))
    sspec = pl.BlockSpec((1, LANES), lambda j: (0, j))
    return pl.pallas_call(
        body, name="pool_bwd", grid=(ngrp,),
        in_specs=[col(16), col(20), col(0), _pool_w_spec(l), _pool_s_spec(l), ANY],
        out_specs=[col(0), col(0), wspec, sspec],
        out_shape=[sec, sec, jax.ShapeDtypeStruct((ngrp, LANES, LANES), BF16),
                   jax.ShapeDtypeStruct((1, ngrp * LANES), F32)],
        scratch_shapes=[pltpu.VMEM((T, LANES), BF16), pltpu.VMEM((T, LANES), F32),
                        pltpu.VMEM((T, LANES), BF16), pltpu.VMEM((T + HIST, LANES), F32)],
        compiler_params=_params(),
    )(proj, proj, dyp, wpool, pscale3, after)


def in_bwd(dsecs, wg, x, dxo, mod4, g_pre3, l):
    T, D = x.shape
    NB = N_CHIPS
    CW = wg.shape[1] // NB
    SW = dsecs[0].shape[1]
    nsec = len(dsecs)
    PW = 256
    assert SW % PW == 0 and CW % PW == 0
    tm = 512
    nt = T // tm

    def body(*refs):
        d_refs = refs[0:nsec]
        w_ref, x_ref, dxo_ref, sh_ref, sc_ref, g_ref = refs[nsec:nsec + 6]
        dxi_ref, dw_ref, dsh_ref, dsc_ref, dg_ref = refs[nsec + 6:nsec + 11]
        acc_w, acc_sh, acc_sc, acc_g, stage, dw_sem = refs[nsec + 11:]
        i = pl.program_id(0)

        @pl.when(i == 0)
        def _():
            acc_w[...] = jnp.zeros_like(acc_w)
            acc_sh[...] = jnp.zeros_like(acc_sh)
            acc_sc[...] = jnp.zeros_like(acc_sc)
            acc_g[...] = jnp.zeros_like(acc_g)

        xv = x_ref[...]
        g = g_ref[...]
        r = _rms(xv)
        xh = xv * r
        n = xh * g
        sc1 = 1.0 + sc_ref[...]
        hb = (n * sc1 + sh_ref[...]).astype(BF16)
        dh = lax.dot_general(d_refs[0][...], w_ref[:, 0:SW], NT, preferred_element_type=F32)
        for s in range(1, nsec):
            dh = dh + lax.dot_general(d_refs[s][...], w_ref[:, s * SW:(s + 1) * SW], NT, preferred_element_type=F32)
        for p in range(nsec * SW // PW):
            col = p * PW
            s, so = col // SW, col % SW
            j, jo = col // CW, col % CW
            acc_w[j, :, jo:jo + PW] += lax.dot_general(hb, d_refs[s][:, so:so + PW], TN, preferred_element_type=F32)
        acc_sh[...] += _colsum8(dh)
        acc_sc[...] += _colsum8(dh * n)
        dnp = dh * sc1
        acc_g[...] += _colsum8(dnp * xh)
        a = dnp * g
        dxi_ref[...] = dxo_ref[...] + r * (a - xh * jnp.mean(a * xh, axis=-1, keepdims=True))

        @pl.when(i == nt - 1)
        def _():
            for j in range(NB):
                stage[...] = acc_w[j].astype(BF16)
                out = pltpu.make_async_copy(stage, dw_ref.at[j], dw_sem)
                out.start()
                out.wait()
            dsh_ref[...] = jnp.sum(acc_sh[...], axis=0, keepdims=True)
            dsc_ref[...] = jnp.sum(acc_sc[...], axis=0, keepdims=True)
            dg_ref[...] = jnp.sum(acc_g[...], axis=0, keepdims=True)

    row = pl.BlockSpec((1, D), lambda i: (0, 0))
    tile = pl.BlockSpec((tm, D), lambda i: (i, 0))
    sect = pl.BlockSpec((tm, SW), lambda i: (i, 0))
    rowshape = jax.ShapeDtypeStruct((1, D), F32)
    return pl.pallas_call(
        body, name="in_bwd", grid=(nt,),
        in_specs=[sect] * nsec + [pl.BlockSpec((D, NB * CW), lambda i: (0, 0)), tile, tile,
                                  _mod_row(l, 0, D), _mod_row(l, 1, D), _layer_row(l, D)],
        out_specs=[tile, ANY, row, row, row],
        out_shape=[jax.ShapeDtypeStruct((T, D), F32), jax.ShapeDtypeStruct((NB, D, CW), BF16),
                   rowshape, rowshape, rowshape],
        scratch_shapes=[pltpu.VMEM((NB, D, CW), F32),
                        pltpu.VMEM((SUBLANES, D), F32), pltpu.VMEM((SUBLANES, D), F32), pltpu.VMEM((SUBLANES, D), F32),
                        pltpu.VMEM((D, CW), BF16), pltpu.SemaphoreType.DMA],
        compiler_params=_params(VMEM_BIG),
    )(*dsecs, wg, x, dxo, mod4, mod4, g_pre3)


def _rcopy(src, dst, ssem, rsem, dev):
    return pltpu.make_async_remote_copy(src_ref=src, dst_ref=dst, send_sem=ssem, recv_sem=rsem,
                                        device_id=dev, device_id_type=MESH)


def _peers7(x, y, c):
    out = []
    for m in range(1, N_DEV):
        bx, by, bc = (m >> 2) & 1, (m >> 1) & 1, m & 1
        out.append(((1 - x) if bx else x, (1 - y) if by else y, (1 - c) if bc else c))
    return out


HBM = pl.BlockSpec(memory_space=pltpu.HBM)
SEM = pl.BlockSpec(memory_space=pltpu.SEMAPHORE)
SPLIT = pltpu.CompilerParams(has_side_effects=pltpu.SideEffectType.DATAFLOW_SIDE_EFFECTING)


def _hbm(a):
    return pltpu.with_memory_space_constraint(a, pltpu.HBM)


def _chips(x, y):
    return [(1 - x, y), (x, 1 - y), (1 - x, 1 - y)]


SIBLING_BARRIER_ID = 0


def xchg_start(name, bufs, n_copies, plan, sibling_only=False):
    n = len(bufs)

    def body(*refs):
        ssem, rsem, token = refs[n], refs[n + 1], refs[-1]
        x, y, c = _me()
        if sibling_only:
            barrier = pltpu.get_barrier_semaphore()
            pl.semaphore_signal(barrier, inc=1, device_id=(x, y, 1 - c), device_id_type=MESH)
            pl.semaphore_wait(barrier, 1)
        copies = plan(refs[0:n], x, y, c)
        assert len(copies) == n_copies
        for k, (src, dst, peer, _) in enumerate(copies):
            _rcopy(src, dst, ssem.at[k], rsem.at[k], peer).start()
        token[...] = jnp.zeros_like(token)

    params = dict(has_side_effects=pltpu.SideEffectType.DATAFLOW_SIDE_EFFECTING)
    if sibling_only:
        params["collective_id"] = SIBLING_BARRIER_ID
    outs = pl.pallas_call(
        body, name=name,
        in_specs=[HBM] * n,
        out_specs=[SEM, SEM] + [HBM] * n + [pl.BlockSpec(memory_space=pltpu.VMEM)],
        out_shape=([pltpu.SemaphoreType.DMA((n_copies,))] * 2 + [pltpu.HBM(b.shape, b.dtype) for b in bufs]
                   + [jax.ShapeDtypeStruct((SUBLANES, LANES), F32)]),
        input_output_aliases={a: 2 + a for a in range(n)},
        compiler_params=pltpu.CompilerParams(**params),
    )(*[_hbm(b) for b in bufs])
    return outs[0], outs[1], list(outs[2:2 + n]), outs[-1]


def xchg_wait(name, bufs, ssem, rsem, n_copies, plan, after):
    n = len(bufs)
    after = list(after)

    def body(*refs):
        ssem_ref, rsem_ref = refs[n], refs[n + 1]
        copies = plan(refs[0:n], *_me())
        assert len(copies) == n_copies
        for k, (src, _, peer, land) in enumerate(copies):
            cp = _rcopy(src, land, ssem_ref.at[k], rsem_ref.at[k], peer)
            cp.wait_send()
            cp.wait_recv()

    outs = pl.pallas_call(
        body, name=name,
        in_specs=[HBM] * n + [SEM, SEM] + [ANY] * len(after), out_specs=[HBM] * n,
        out_shape=[pltpu.HBM(b.shape, b.dtype) for b in bufs],
        input_output_aliases={a: a for a in range(n)},
        compiler_params=SPLIT,
    )(*bufs, ssem, rsem, *after)
    return list(outs)


def _shard_half(buf, chip, half):
    if len(buf.shape) == 2:
        h, w = buf.shape[0] // 2, buf.shape[1] // N_CHIPS
        return buf.at[pl.ds(half * h, h), pl.ds(chip * w, w)]
    h = buf.shape[1] // 2
    return buf.at[chip, pl.ds(half * h, h)]


def plan_gather(refs, x, y, c):
    out = []
    for (px, py) in _chips(x, y):
        for buf in refs:
            own = _shard_half(buf, 2 * x + y, c)
            out.append((own, own, (px, py, c), _shard_half(buf, 2 * px + py, c)))
    return out


def plan_forward(refs, x, y, c):
    out = []
    for (px, py) in _chips(x, y):
        for buf in refs:
            landed = _shard_half(buf, 2 * px + py, c)
            out.append((landed, landed, (x, y, 1 - c), _shard_half(buf, 2 * px + py, 1 - c)))
    return out


def plan_sibling(refs, x, y, c):
    n = len(refs) // 2
    out = []
    for a in range(n):
        h = refs[a].shape[1] // 2
        out.append((refs[a].at[:, pl.ds((1 - c) * h, h)], refs[n + a], (x, y, 1 - c), refs[n + a]))
    return out


def plan_chip(refs, x, y, c):
    n = len(refs) // 2
    out = []
    for j, (px, py) in enumerate(_chips(x, y)):
        for a in range(n):
            out.append((refs[a].at[2 * px + py], refs[n + a].at[j], (px, py, c), refs[n + a].at[j]))
    return out


def plan_mod(refs, x, y, c):
    (mods,) = refs
    mine = mods.at[2 * x + y]
    return [(mine, mine, (px, py, c), mods.at[2 * px + py]) for (px, py) in _chips(x, y)]


def plan_pack(refs, x, y, c):
    (packs,) = refs
    mine = packs.at[4 * x + 2 * y + c]
    return [(mine, mine, peer, packs.at[4 * peer[0] + 2 * peer[1] + peer[2]]) for peer in _peers7(x, y, c)]


def plan_spread(layers, wp_layers):
    def plan(refs, x, y, c):
        gi, go, gp = refs
        hD, hR, hP = gi.shape[1] // 2, go.shape[1] // 2, gp.shape[2] // 2
        sib = (x, y, 1 - c)
        out = []
        for l in layers:
            mine = gi.at[l, pl.ds(c * hD, hD)]
            out.append((mine, mine, sib, gi.at[l, pl.ds((1 - c) * hD, hD)]))
            mine = go.at[l, pl.ds(c * hR, hR)]
            out.append((mine, mine, sib, go.at[l, pl.ds((1 - c) * hR, hR)]))
        for l in wp_layers:
            mine = gp.at[l, 2 * x + y, pl.ds(c * hP, hP)]
            for peer in _peers7(x, y, c):
                out.append((mine, mine, peer, gp.at[l, 2 * peer[0] + peer[1], pl.ds(peer[2] * hP, hP)]))
        return out

    return plan


def gather_small(c8, wc, token):
    def body(c_ref, wc_ref, token_ref, call, wcall, ssem, rsem, lsem):
        x, y, c = _me()
        myc = 2 * x + y
        me_lin = 4 * x + 2 * y + c
        me = (x, y, c)
        local = [pltpu.make_async_copy(c_ref, call.at[me_lin], lsem.at[0]),
                 pltpu.make_async_copy(wc_ref, wcall.at[myc], lsem.at[1])]
        for cp in local:
            cp.start()
        sends, recvs = [], []
        for m, peer in enumerate(_peers7(x, y, c)):
            plin = 4 * peer[0] + 2 * peer[1] + peer[2]
            sends.append(_rcopy(c_ref, call.at[me_lin], ssem.at[m], rsem.at[m], peer))
            recvs.append(_rcopy(call.at[plin], call.at[plin], ssem.at[m], rsem.at[m], me))
        for j, (px, py) in enumerate([(1 - x, y), (x, 1 - y), (1 - x, 1 - y)]):
            pc = 2 * px + py
            sends.append(_rcopy(wc_ref, wcall.at[myc], ssem.at[7 + j], rsem.at[7 + j], (px, py, c)))
            recvs.append(_rcopy(wcall.at[pc], wcall.at[pc], ssem.at[7 + j], rsem.at[7 + j], me))
        for cp in sends:
            cp.start()
        for cp in recvs:
            cp.wait_recv()
        for cp in sends:
            cp.wait_send()
        for cp in local:
            cp.wait()

    return pl.pallas_call(
        body, name="gather_small",
        in_specs=[ANY] * 3, out_specs=[ANY] * 2,
        out_shape=[jax.ShapeDtypeStruct((N_DEV, SUBLANES, LANES), F32),
                   jax.ShapeDtypeStruct((N_CHIPS, wc.shape[0], 3, LANES), F32)],
        scratch_shapes=[pltpu.SemaphoreType.DMA((10,)), pltpu.SemaphoreType.DMA((10,)), pltpu.SemaphoreType.DMA((2,))],
        compiler_params=_params(n_grid=0),
    )(c8, wc, token)


def spread_now(gi, go, gp, layers, wp_layers):
    plan = plan_spread(layers, wp_layers)
    n = 2 * len(layers) + 7 * len(wp_layers)

    def body(gi_in, go_in, gp_in, gi, go, gp, ssem, rsem):
        copies = plan((gi, go, gp), *_me())
        me = _me()
        sends = [_rcopy(src, dst, ssem.at[k], rsem.at[k], peer) for k, (src, dst, peer, _) in enumerate(copies)]
        for cp in sends:
            cp.start()
        for k, (_, _, _, land) in enumerate(copies):
            _rcopy(land, land, ssem.at[k], rsem.at[k], me).wait_recv()
        for cp in sends:
            cp.wait_send()

    return pl.pallas_call(
        body, name="spread_now",
        in_specs=[ANY] * 3, out_specs=[ANY] * 3,
        out_shape=[jax.ShapeDtypeStruct(a.shape, a.dtype) for a in (gi, go, gp)],
        input_output_aliases={0: 0, 1: 1, 2: 2},
        scratch_shapes=[pltpu.SemaphoreType.DMA((n,)), pltpu.SemaphoreType.DMA((n,))],
        compiler_params=_params(n_grid=0),
    )(gi, go, gp)


def add_sibling(cidx, mine, sib):
    def body(c_ref, *refs):
        for a in range(3):
            m, s, o = refs[a], refs[3 + a], refs[6 + a]
            o[...] = (m[...].astype(F32) + s[...].astype(F32)).astype(BF16)

    def mine_spec(a):
        h = a.shape[1] // 2
        return pl.BlockSpec((None, h, a.shape[2]), lambda j, c_ref: (j, c_ref[0], 0))

    def sib_spec(a):
        return pl.BlockSpec((None,) + a.shape[1:], lambda j, c_ref: (j, 0, 0))

    return pl.pallas_call(
        body, name="add_sibling",
        grid_spec=pltpu.PrefetchScalarGridSpec(
            num_scalar_prefetch=1, grid=(N_CHIPS,),
            in_specs=[mine_spec(a) for a in mine] + [sib_spec(a) for a in sib],
            out_specs=[sib_spec(a) for a in sib]),
        out_shape=[jax.ShapeDtypeStruct(a.shape, BF16) for a in sib],
        compiler_params=_params(VMEM_BIG),
    )(cidx, *mine, *sib)


def sum_chips(pos, own, rb, acc, l, shapes):
    nq = 4
    n_in = 6 + (3 if acc is not None else 0)

    def body(pos_ref, *refs):
        for a in range(3):
            m, b, o = refs[a], refs[3 + a], refs[n_in + a]
            s = m[...].astype(F32)
            for j in range(3):
                s = s + b[j].astype(F32)
            o[...] = s

    def own_spec(a):
        return pl.BlockSpec((None, a.shape[1] // nq, a.shape[2]), lambda q, p: (p[1], q, 0))

    def rb_spec(a):
        return pl.BlockSpec((3, a.shape[1] // nq, a.shape[2]), lambda q, p: (0, q, 0))

    hi, ho, hp = own[0].shape[1] // nq, own[1].shape[1] // nq, own[2].shape[1] // nq
    out_specs = [pl.BlockSpec((None, hi, shapes[0][2]), lambda q, p: (l, p[0] * nq + q, 0)),
                 pl.BlockSpec((None, ho, shapes[1][2]), lambda q, p: (l, p[0] * nq + q, 0)),
                 pl.BlockSpec((None, None, hp, LANES), lambda q, p: (l, p[1], p[0] * nq + q, 0))]
    in_specs = [own_spec(a) for a in own] + [rb_spec(a) for a in rb]
    args = list(own) + list(rb)
    aliases = {}
    if acc is not None:
        in_specs += [ANY] * 3
        args += list(acc)
        aliases = {7: 0, 8: 1, 9: 2}
    return pl.pallas_call(
        body, name="sum_chips",
        grid_spec=pltpu.PrefetchScalarGridSpec(num_scalar_prefetch=1, grid=(nq,), in_specs=in_specs, out_specs=out_specs),
        out_shape=[jax.ShapeDtypeStruct(s, F32) for s in shapes],
        input_output_aliases=aliases,
        compiler_params=_params(VMEM_BIG),
    )(pos, *args)


def pack_small(pos, per_layer, loss_blk):
    L = len(per_layer)
    D = per_layer[0][0].shape[1]

    def body(pos_ref, *refs):
        o = refs[-1]
        lb = refs[-2]
        o[...] = jnp.zeros_like(o)
        for l in range(L):
            dgpre, dgpost, dsh, dsc, dgt, dps, dwc = refs[7 * l:7 * l + 7]
            base = SUBLANES * l
            for r, src in enumerate((dgpre, dgpost, dsh, dsc, dgt)):
                o[pl.ds(base + r, 1), :] = src[...]
            o[pl.ds(base + 5, 1), 0:dps.shape[1]] = dps[...]
            for j in range(dwc.shape[0]):
                for k in range(3):
                    idx = 3 * j + k
                    o[pl.ds(base + 6 + idx // 8, 1), (idx % 8) * LANES:(idx % 8 + 1) * LANES] = dwc[j, pl.ds(k, 1), :]
        o[pl.ds(5, 1), 4 * LANES:5 * LANES] = lb[pl.ds(0, 1), :]

    flat = [a for layer in per_layer for a in layer] + [loss_blk]

    def whole(a):
        return pl.BlockSpec(a.shape, lambda i, p: (0,) * a.ndim)

    return pl.pallas_call(
        body, name="pack_small",
        grid_spec=pltpu.PrefetchScalarGridSpec(
            num_scalar_prefetch=1, grid=(1,), in_specs=[whole(a) for a in flat],
            out_specs=pl.BlockSpec((None, L * SUBLANES, D), lambda i, p: (p[2], 0, 0))),
        out_shape=jax.ShapeDtypeStruct((N_DEV, L * SUBLANES, D), F32),
        compiler_params=_params(),
    )(pos, *flat)


def sum_packs(packs):
    def body(p_ref, o_ref):
        s = p_ref[0]
        for d in range(1, N_DEV):
            s = s + p_ref[d]
        o_ref[...] = s

    return pl.pallas_call(
        body, name="sum_packs",
        out_shape=jax.ShapeDtypeStruct(packs.shape[1:], F32),
        compiler_params=_params(n_grid=0),
    )(packs)


def _adamw_math(w, g, m, v):
    m = ADAM_B1 * m + (1.0 - ADAM_B1) * g
    v = ADAM_B2 * v + (1.0 - ADAM_B2) * (g * g)
    m_hat = m / (1.0 - ADAM_B1 ** ADAM_STEP)
    v_hat = v / (1.0 - ADAM_B2 ** ADAM_STEP)
    delta = -ADAM_LR * (m_hat / (jnp.sqrt(v_hat) + ADAM_EPS) + ADAM_WD * w)
    return delta, m, v


def adamw(w, g, m, v, block, name, first=0, count=None, acc=None):
    grid = tuple(s // b for s, b in zip(w.shape, block))
    if count is not None:
        grid = (count,) + grid[1:]

    def body(w_ref, g_ref, m_ref, v_ref, *rest):
        d_ref, mo_ref, vo_ref = rest[-3:]
        d, mm, vv = _adamw_math(w_ref[...], g_ref[...], m_ref[...], v_ref[...])
        d_ref[...] = d
        mo_ref[...] = mm
        vo_ref[...] = vv

    spec = pl.BlockSpec(block, lambda i, *rest: (first + i,) + rest)
    shape = jax.ShapeDtypeStruct(w.shape, F32)
    extra = [] if acc is None else list(acc)
    return pl.pallas_call(
        body, name=name, grid=grid,
        in_specs=[spec] * 4 + [ANY] * len(extra), out_specs=[spec] * 3, out_shape=[shape] * 3,
        input_output_aliases={4 + a: a for a in range(len(extra))},
        compiler_params=_params(VMEM_BIG, n_grid=len(grid)),
    )(w, g, m, v, *extra)


def ada_finish(c_all, dmod, w, m, v):
    L, D, CW = w.shape
    hD = D // 2

    def body(c_ref, d_ref, w_ref, m_ref, v_ref, g_ref, dl_ref, mo_ref, vo_ref):
        cv = c_ref[...]
        z = jnp.zeros_like(cv)
        ca = jnp.concatenate([cv * jax.nn.sigmoid(cv), z], axis=0).astype(BF16)
        dm = jnp.concatenate([d_ref[0], jnp.zeros_like(d_ref[0])], axis=0).astype(BF16)
        g = lax.dot_general(ca, dm, TN, preferred_element_type=F32)
        g_ref[0] = g
        d, mm, vv = _adamw_math(w_ref[0], g, m_ref[0], v_ref[0])
        dl_ref[0] = d
        mo_ref[0] = mm
        vo_ref[0] = vv

    big = pl.BlockSpec((1, hD, CW), lambda l, h: (l, h, 0))
    shape = jax.ShapeDtypeStruct(w.shape, F32)
    return pl.pallas_call(
        body, name="ada_finish", grid=(L, 2),
        in_specs=[pl.BlockSpec((N_DEV, hD), lambda l, h: (0, h)), pl.BlockSpec((1, N_DEV, CW), lambda l, h: (l, 0, 0)),
                  big, big, big],
        out_specs=[big] * 4, out_shape=[shape] * 4,
        compiler_params=_params(VMEM_BIG, n_grid=2),
    )(c_all, dmod, w, m, v)


def kernel(x, c, w_ada, b_ada, g_pre, w_in, w_conv, w_pool, pool_scale, w_out, g_post, loss_target, m_w_ada, m_b_ada, m_g_pre, m_w_in, m_w_conv, m_w_pool, m_pool_scale, m_w_out, m_g_post, v_w_ada, v_b_ada, v_g_pre, v_w_in, v_w_conv, v_w_pool, v_pool_scale, v_w_out, v_g_post):
    L, D, CW = w_in.shape
    RO = w_out.shape[1]
    T = x.shape[1]
    ix, iy, ic = _me()
    chip = 2 * ix + iy
    me_lin = 4 * ix + 2 * iy + ic

    pos = jnp.stack([ic, chip, me_lin]).astype(jnp.int32)
    g_pre3, g_post3 = g_pre.reshape(L, 1, D), g_post.reshape(L, 1, D)
    pscale3 = pool_scale.reshape(L, 1, pool_scale.shape[1])
    n_g, n_s, n_c = 6, 3, 9

    c_all3, wconv_all = gather_small(c.reshape(SUBLANES, LANES), w_conv, pos)
    c_all = c_all3.reshape(N_DEV, D)
    gath = [None] * L
    ss, rs, bufs, token = xchg_start("gather_start", list(cast_weights(pos, w_in, w_out, 0, c_all3)), n_g, plan_gather)
    gath[0] = (ss, rs, bufs)
    b_my = lax.dynamic_slice_in_dim(b_ada, chip * CW, CW, axis=1)
    m_ss, m_rs, mods, token = xchg_start("mod_start", [mod_part(pos, c_all, w_ada, b_my, token)], 3, plan_mod)
    for l in range(1, L):
        ss, rs, bufs, token = xchg_start("gather_start", list(cast_weights(pos, w_in, w_out, l, token)), n_g, plan_gather)
        gath[l] = (ss, rs, bufs)
    (mod_all,) = xchg_wait("mod_wait", mods, m_ss, m_rs, 3, plan_mod, [token])
    mod = lax.dynamic_index_in_dim(mod_all, me_lin, axis=2, keepdims=False)
    mod4 = jnp.transpose(mod, (1, 0, 2)).reshape(L, 3, 1, D)
    token = mod4

    def arrive(l, after):
        ss, rs, bufs = gath[l]
        bufs = xchg_wait("gather_wait", bufs, ss, rs, n_g, plan_gather, after)
        return xchg_start("forward_start", bufs, n_g, plan_forward, sibling_only=True)

    xs, projs, yas, yps, ys = [x.reshape(T, D)], [], [], [], []
    wg_in, wg_out = [], []
    fwd = arrive(0, [token])
    for l in range(L):
        fss, frs, bufs, ftoken = fwd
        gi, go = xchg_wait("forward_wait", bufs, fss, frs, n_g, plan_forward, [ftoken if l == 0 else xs[l]])
        wg_in.append(gi)
        wg_out.append(go.reshape(N_CHIPS * RO, D))
        proj = proj_fwd(xs[l], mod4, g_pre3, wg_in[l], l)
        ya = conv_fwd(proj, wconv_all, l)
        yp = pool_fwd(proj, w_pool, pscale3, l)
        token = yp
        if l + 1 < L:
            fwd = arrive(l + 1, [ya, yp])
            token = fwd[3]
        xn, yv = out_fwd(ya, yp, wg_out[l], xs[l], mod4, g_post3, l, token)
        xs.append(xn)
        projs.append(proj)
        yas.append(ya)
        yps.append(yp)
        ys.append(yv)

    dx, loss_blk = loss_head(xs[L], loss_target.reshape(T, D))

    shapes = (w_in.shape, w_out.shape, w_pool.shape)
    smalls = [None] * L
    acc, flying, sib, token = None, None, None, loss_blk

    def to_chips(sib, after):
        sl, s_ss, s_rs, s_bufs = sib
        s_bufs = xchg_wait("sibling_wait", s_bufs, s_ss, s_rs, n_s, plan_sibling, after)
        chip_parts = add_sibling(pos, s_bufs[0:3], s_bufs[3:6])
        lands = [lax.empty((3,) + a.shape[1:], a.dtype) for a in chip_parts]
        c_ss, c_rs, c_bufs, ctoken = xchg_start("chip_start", list(chip_parts) + lands, n_c, plan_chip)
        return (sl, c_ss, c_rs, c_bufs), ctoken

    def landed(flying, acc, after):
        fl, f_ss, f_rs, f_bufs = flying
        f_bufs = xchg_wait("chip_wait", f_bufs, f_ss, f_rs, n_c, plan_chip, after)
        return sum_chips(pos, f_bufs[0:3], f_bufs[3:6], acc, fl, shapes)

    for l in reversed(range(L)):
        dya, dyp, dwo_l, dgate, dgpost = out_bwd(dx, ys[l], yas[l], yps[l], wg_out[l], mod4, g_post3, l, token)
        token = dya
        if sib is not None:
            arrived = flying
            flying, token = to_chips(sib, [dya])
            if arrived is not None:
                acc = landed(arrived, acc, [token])
                token = acc[0]
        du_p, dg_p, dwp_l, dps = pool_bwd(projs[l], dyp, w_pool, pscale3, l, token)
        du_a, db_a, dc_a, dg_a, dwc = conv_bwd(projs[l], dya, wconv_all, l)
        dx, dwi_l, dshift, dscale, dgpre = in_bwd([du_a, db_a, dc_a, dg_a, du_p, dg_p], wg_in[l], xs[l], dx,
                                                  mod4, g_pre3, l)
        smalls[l] = (dgpre, dgpost, dshift, dscale, dgate, dps, dwc)
        parts = [dwi_l, dwo_l.reshape(N_CHIPS, RO, D), dwp_l]
        s_lands = [lax.empty((a.shape[0], a.shape[1] // 2) + a.shape[2:], a.dtype) for a in parts]
        s_ss, s_rs, s_bufs, token = xchg_start("sibling_start", parts + s_lands, n_s, plan_sibling, sibling_only=True)
        sib = (l, s_ss, s_rs, s_bufs)
    grad_x = dx.reshape(1, T, D)

    p_ss, p_rs, packs, ptoken = xchg_start("pack_start", [pack_small(pos, smalls, loss_blk)], N_DEV - 1, plan_pack)
    acc = landed(flying, acc, [ptoken, token])
    flying, token = to_chips(sib, [acc[0]])
    n_sp = (2 + N_DEV - 1) * (L - 1)
    spread = plan_spread(tuple(range(1, L)), tuple(range(1, L)))
    sp_ss, sp_rs, acc, sp_token = xchg_start("spread_start", list(acc), n_sp, spread)
    (packs_all,) = xchg_wait("pack_wait", packs, p_ss, p_rs, N_DEV - 1, plan_pack, [sp_token, token])
    small = sum_packs(packs_all).reshape(L, SUBLANES, D)
    loss = small[0, 5, 4 * LANES]
    g_g_pre = small[:, 0]
    g_g_post = small[:, 1]
    g_b_ada = small[:, 2:5].reshape(L, 3 * D)
    g_pscale = small[:, 5, 0:pool_scale.shape[1]]
    g_w_conv = small[:, 6:8].reshape(L, 2 * D)[:, 0:N_CHIPS * 3 * LANES].reshape(L, N_CHIPS, 3, LANES)
    g_w_conv = lax.dynamic_index_in_dim(g_w_conv, chip, axis=1, keepdims=False)
    dmod_all = packs_all.reshape(N_DEV, L, SUBLANES, D)[:, :, 2:5].reshape(N_DEV, L, 3 * D)
    dmod_my = jnp.transpose(lax.dynamic_slice_in_dim(dmod_all, chip * CW, CW, axis=2), (1, 0, 2))

    g_w_ada, d_w_ada, nm_w_ada, nv_w_ada = ada_finish(c_all, dmod_my, w_ada, m_w_ada, v_w_ada)

    def small_adamw(w, g, m, v, name):
        shp = (1,) + w.shape if w.ndim == 2 else w.shape
        outs = adamw(w.reshape(shp), g.reshape(shp), m.reshape(shp), v.reshape(shp), shp, name)
        return [a.reshape(w.shape) for a in outs]

    d_b_ada, nm_b_ada, nv_b_ada = small_adamw(b_ada, g_b_ada, m_b_ada, v_b_ada, "adamw_b_ada")
    d_g_pre, nm_g_pre, nv_g_pre = small_adamw(g_pre, g_g_pre, m_g_pre, v_g_pre, "adamw_g_pre")
    d_w_conv, nm_w_conv, nv_w_conv = small_adamw(w_conv, g_w_conv, m_w_conv, v_w_conv, "adamw_w_conv")
    d_pscale, nm_pscale, nv_pscale = small_adamw(pool_scale, g_pscale, m_pool_scale, v_pool_scale, "adamw_pool_scale")
    d_g_post, nm_g_post, nv_g_post = small_adamw(g_post, g_g_post, m_g_post, v_g_post, "adamw_g_post")

    done = [nv_w_ada, nv_b_ada, nv_g_pre, nv_w_conv, nv_pscale, nv_g_post]
    g_w_in, g_w_out, g_w_pool = xchg_wait("spread_wait", acc, sp_ss, sp_rs, n_sp, spread, done)
    in_blk, out_blk = (1, D // 2, CW), (1, RO, D)
    upd_in = adamw(w_in, g_w_in, m_w_in, v_w_in, in_blk, "adamw_w_in", 1, L - 1)
    upd_out = adamw(w_out, g_w_out, m_w_out, v_w_out, out_blk, "adamw_w_out", 1, L - 1)

    acc = landed(flying, (g_w_in, g_w_out, g_w_pool), [upd_in[2], upd_out[2]])
    g_w_in, g_w_out, g_w_pool = spread_now(*acc, (0,), (0,))
    d_w_in, nm_w_in, nv_w_in = adamw(w_in, g_w_in, m_w_in, v_w_in, in_blk, "adamw_w_in", 0, 1, upd_in)
    d_w_out, nm_w_out, nv_w_out = adamw(w_out, g_w_out, m_w_out, v_w_out, out_blk, "adamw_w_out", 0, 1, upd_out)
    pshape = (L, N_CHIPS * LANES, LANES)
    d_w_pool, nm_w_pool, nv_w_pool = adamw(w_pool.reshape(pshape), g_w_pool.reshape(pshape), m_w_pool.reshape(pshape),
                                           v_w_pool.reshape(pshape), (1,) + pshape[1:], "adamw_w_pool")
    d_w_pool, nm_w_pool, nv_w_pool = [a.reshape(w_pool.shape) for a in (d_w_pool, nm_w_pool, nv_w_pool)]

    return (loss, grad_x,
            g_w_ada, g_b_ada, g_g_pre, g_w_in, g_w_conv, g_w_pool, g_pscale, g_w_out, g_g_post,
            d_w_ada, d_b_ada, d_g_pre, d_w_in, d_w_conv, d_w_pool, d_pscale, d_w_out, d_g_post,
            nm_w_ada, nm_b_ada, nm_g_pre, nm_w_in, nm_w_conv, nm_w_pool, nm_pscale, nm_w_out, nm_g_post,
            nv_w_ada, nv_b_ada, nv_g_pre, nv_w_in, nv_w_conv, nv_w_pool, nv_pscale, nv_w_out, nv_g_post)
```

```python
import functools

import jax
import jax.numpy as jnp
from jax import lax
from jax.experimental import pallas as pl
from jax.experimental.pallas import tpu as pltpu

F32 = jnp.float32
BF16 = jnp.bfloat16
MESH = pl.DeviceIdType.MESH
ANY = pl.BlockSpec(memory_space=pl.ANY)

NORM_EPS = 1e-6
POOL_WINDOWS = (2, 4, 8, 16)
ADAM_LR = 0.001
ADAM_B1 = 0.9
ADAM_B2 = 0.999
ADAM_EPS = 1e-08
ADAM_WD = 0.01
ADAM_STEP = 10

N_CHIPS = 4
N_DEV = 8
LANES = 128
SUBLANES = 8
VMEM_BIG = 56 * 1024 * 1024
HIST = 16
R_CONV = 32
R_POOL = 64

NT = (((1,), (1,)), ((), ()))
TN = (((0,), (0,)), ((), ()))


def _params(vmem=None, n_grid=1):
    kw = {}
    if n_grid:
        kw["dimension_semantics"] = ("arbitrary",) * n_grid
    if vmem is not None:
        kw["vmem_limit_bytes"] = vmem
    return pltpu.CompilerParams(**kw)


def _colsum8(v):
    n, d = v.shape
    return v.reshape(n // SUBLANES, SUBLANES, d).sum(axis=0)


def _rms(v):
    return lax.rsqrt(jnp.mean(v * v, axis=-1, keepdims=True) + NORM_EPS)


def _sigmoid(v):
    return 0.5 * jnp.tanh(0.5 * v) + 0.5


def _shift_down(ext, k, rows):
    if k == 0:
        return ext[HIST:HIST + rows]
    return pltpu.roll(ext, k, 0)[HIST:HIST + rows]


def _shift_up(ext, k, rows):
    if k == 0:
        return ext[0:rows]
    return pltpu.roll(ext, ext.shape[0] - k, 0)[0:rows]


def _load_ext(ref, r0, h0, first, rows):
    hist = ref[pl.ds(h0, HIST), :].astype(F32)
    hist = jnp.where(first, 0.0, hist)
    cur = ref[pl.ds(r0, rows), :].astype(F32)
    return jnp.concatenate([hist, cur], axis=0)


def _me():
    return lax.axis_index("x"), lax.axis_index("y"), lax.axis_index("c")


def cast_weights(pos, w_in, w_out, l, after):
    _, D, CW = w_in.shape
    RO = w_out.shape[1]

    def body(pos_ref, wi, wo, after_ref, oi, oo):
        oi[...] = wi[...].astype(BF16)
        oo[...] = wo[...].astype(BF16)

    return pl.pallas_call(
        body, name="cast_w",
        grid_spec=pltpu.PrefetchScalarGridSpec(
            num_scalar_prefetch=1, grid=(2,),
            in_specs=[pl.BlockSpec((None, D // 2, CW), lambda h, p: (l, h, 0)),
                      pl.BlockSpec((None, RO // 2, D), lambda h, p: (l, h, 0)), ANY],
            out_specs=[pl.BlockSpec((D // 2, CW), lambda h, p: (h, p[1])),
                       pl.BlockSpec((None, RO // 2, D), lambda h, p: (p[1], h, 0))]),
        out_shape=[jax.ShapeDtypeStruct((D, N_CHIPS * CW), BF16), jax.ShapeDtypeStruct((N_CHIPS, RO, D), BF16)],
        compiler_params=_params(),
    )(pos, w_in, w_out, after)


def mod_part(pos, c_all, w_ada, b_my, after):
    L, D, CW = w_ada.shape

    def body(pos_ref, c_ref, w_ref, b_ref, after_ref, o_ref):
        cv = c_ref[...]
        ca = (cv * jax.nn.sigmoid(cv)).astype(BF16)
        o_ref[...] = jnp.dot(ca, w_ref[0].astype(BF16), preferred_element_type=F32) + b_ref[0]

    return pl.pallas_call(
        body, name="mod_part",
        grid_spec=pltpu.PrefetchScalarGridSpec(
            num_scalar_prefetch=1, grid=(L,),
            in_specs=[pl.BlockSpec((N_DEV, D), lambda l, p: (0, 0)),
                      pl.BlockSpec((1, D, CW), lambda l, p: (l, 0, 0)),
                      pl.BlockSpec((1, 1, CW), lambda l, p: (l, 0, 0)), ANY],
            out_specs=pl.BlockSpec((None, None, N_DEV, CW), lambda l, p: (p[1], l, 0, 0))),
        out_shape=jax.ShapeDtypeStruct((N_CHIPS, L, N_DEV, CW), F32),
        compiler_params=_params(VMEM_BIG),
    )(pos, c_all, w_ada, b_my.reshape(L, 1, CW), after)


def _mod_row(l, k, D):
    return pl.BlockSpec((None, None, 1, D), lambda *_: (l, k, 0, 0))


def _layer_row(l, D):
    return pl.BlockSpec((None, 1, D), lambda *_: (l, 0, 0))


def proj_fwd(x, mod4, g_pre3, wg, l):
    T, D = x.shape
    NC = wg.shape[1]
    NB = N_CHIPS
    CW = NC // NB
    tm = 512

    def body(x_ref, sh_ref, sc_ref, g_ref, w_ref, o_ref):
        xv = x_ref[...]
        h = (xv * _rms(xv) * g_ref[...]) * (1.0 + sc_ref[...]) + sh_ref[...]
        hb = h.astype(BF16)
        for j in range(NB):
            cols = slice(j * CW, (j + 1) * CW)
            o_ref[:, cols] = jnp.dot(hb, w_ref[:, cols], preferred_element_type=F32).astype(BF16)

    return pl.pallas_call(
        body, name="proj_fwd", grid=(T // tm,),
        in_specs=[pl.BlockSpec((tm, D), lambda i: (i, 0)), _mod_row(l, 0, D), _mod_row(l, 1, D), _layer_row(l, D),
                  pl.BlockSpec((D, NC), lambda i: (0, 0))],
        out_specs=pl.BlockSpec((tm, NC), lambda i: (i, 0)),
        out_shape=jax.ShapeDtypeStruct((T, NC), BF16),
        compiler_params=_params(VMEM_BIG),
    )(x, mod4, mod4, g_pre3, wg)


def conv_fwd(proj, wconv, l):
    T = proj.shape[0]
    R = R_CONV
    nblk = 4

    def body(u_ref, b_ref, c_ref, g_ref, w_ref, o_ref):
        w0 = w_ref[pl.ds(0, 1), :]
        w1 = w_ref[pl.ds(1, 1), :]
        w2 = w_ref[pl.ds(2, 1), :]

        def chunk(i, carry):
            r0 = pl.multiple_of(i * R, R)
            h0 = pl.multiple_of(jnp.maximum(r0 - HIST, 0), HIST)
            first = i == 0
            ca = _load_ext(c_ref, r0, h0, first, R) * _load_ext(u_ref, r0, h0, first, R)
            conv = w2 * ca[HIST:] + w1 * _shift_down(ca, 1, R) + w0 * _shift_down(ca, 2, R)
            g = g_ref[pl.ds(r0, R), :].astype(F32)
            b = b_ref[pl.ds(r0, R), :].astype(F32)
            o_ref[pl.ds(r0, R), :] = (b * conv * (g * _sigmoid(g))).astype(BF16)
            return carry

        lax.fori_loop(0, T // R, chunk, 0)

    def col(off):
        return pl.BlockSpec((T, LANES), lambda j: (0, j + off))

    return pl.pallas_call(
        body, name="conv_fwd", grid=(nblk,),
        in_specs=[col(0), col(4), col(8), col(12), pl.BlockSpec((None, None, 3, LANES), lambda j: (j, l, 0, 0))],
        out_specs=pl.BlockSpec((T, LANES), lambda j: (0, j)),
        out_shape=jax.ShapeDtypeStruct((T, nblk * LANES), BF16),
        compiler_params=_params(),
    )(proj, proj, proj, proj, wconv)


def _causal_window_sum(ext, w):
    s, k = ext, 1
    while k < w:
        s = s + pltpu.roll(s, k, 0)
        k *= 2
    return s


def _anticausal_window_sum(ext, w):
    s, k = ext, 1
    n = ext.shape[0]
    while k < w:
        s = s + pltpu.roll(s, n - k, 0)
        k *= 2
    return s


def _count(r0, rows, w):
    t = r0 + lax.broadcasted_iota(jnp.int32, (rows, LANES), 0)
    return jnp.minimum(t + 1, w).astype(F32)


def _pooled_loop(p_ref, pooled_s, w, T):
    R = R_POOL

    def chunk(i, carry):
        r0 = pl.multiple_of(i * R, R)
        h0 = pl.multiple_of(jnp.maximum(r0 - HIST, 0), HIST)
        ext = _load_ext(p_ref, r0, h0, i == 0, R)
        ws = _causal_window_sum(ext, w)[HIST:]
        pooled_s[pl.ds(r0, R), :] = (ws / _count(r0, R, w) - ext[HIST:]).astype(BF16)
        return carry

    lax.fori_loop(0, T // R, chunk, 0)


def _pool_w_spec(l):
    return pl.BlockSpec((None, None, LANES, LANES), lambda j: (l, j, 0, 0))


def _pool_s_spec(l):
    return pl.BlockSpec((None, 1, LANES), lambda j: (l, 0, j))


def pool_fwd(proj, wpool, pscale3, l):
    T = proj.shape[0]
    R = R_POOL
    ngrp = len(POOL_WINDOWS)

    def body(p_ref, g_ref, w_ref, s_ref, o_ref, pooled_s, mixed_s):
        grp = pl.program_id(0)

        def group(w):
            _pooled_loop(p_ref, pooled_s, w, T)
            mixed_s[...] = jnp.dot(pooled_s[...], w_ref[...].astype(BF16), preferred_element_type=F32)
            sc = s_ref[...]

            def chunk(i, carry):
                r0 = pl.multiple_of(i * R, R)
                g = g_ref[pl.ds(r0, R), :].astype(F32)
                o_ref[pl.ds(r0, R), :] = (mixed_s[pl.ds(r0, R), :] * sc * (g * _sigmoid(g))).astype(BF16)
                return carry

            lax.fori_loop(0, T // R, chunk, 0)

        for k, w in enumerate(POOL_WINDOWS):
            pl.when(grp == k)(functools.partial(group, w))

    return pl.pallas_call(
        body, name="pool_fwd", grid=(ngrp,),
        in_specs=[pl.BlockSpec((T, LANES), lambda j: (0, j + 16)), pl.BlockSpec((T, LANES), lambda j: (0, j + 20)),
                  _pool_w_spec(l), _pool_s_spec(l)],
        out_specs=pl.BlockSpec((T, LANES), lambda j: (0, j)),
        out_shape=jax.ShapeDtypeStruct((T, ngrp * LANES), BF16),
        scratch_shapes=[pltpu.VMEM((T, LANES), BF16), pltpu.VMEM((T, LANES), F32)],
        compiler_params=_params(),
    )(proj, proj, wpool, pscale3)


def out_fwd(ya, yp, wo, x, mod4, g_post3, l, after):
    T, D = x.shape
    H = ya.shape[1]
    tm = 512

    def body(ya_ref, yp_ref, wo_ref, x_ref, gt_ref, g_ref, after_ref, xn_ref, y_ref):
        y = (jnp.dot(ya_ref[...], wo_ref[0:H, :], preferred_element_type=F32)
             + jnp.dot(yp_ref[...], wo_ref[H:2 * H, :], preferred_element_type=F32))
        xn_ref[...] = x_ref[...] + gt_ref[...] * (y * _rms(y) * g_ref[...])
        y_ref[...] = y

    tile = pl.BlockSpec((tm, D), lambda i: (i, 0))
    half = pl.BlockSpec((tm, H), lambda i: (i, 0))
    return pl.pallas_call(
        body, name="out_fwd", grid=(T // tm,),
        in_specs=[half, half, pl.BlockSpec((2 * H, D), lambda i: (0, 0)), tile, _mod_row(l, 2, D), _layer_row(l, D),
                  ANY],
        out_specs=[tile, tile],
        out_shape=[jax.ShapeDtypeStruct((T, D), F32), jax.ShapeDtypeStruct((T, D), F32)],
        compiler_params=_params(VMEM_BIG),
    )(ya, yp, wo, x, mod4, g_post3, after)


def out_fwd_loss(ya, yp, wo, x, mod4, g_post3, l, target):
    T, D = x.shape
    H = ya.shape[1]
    tm = 512
    nt = T // tm

    def body(ya_ref, yp_ref, wo_ref, x_ref, gt_ref, g_ref, t_ref, dx_ref, y_ref, l_ref, acc):
        i = pl.program_id(0)

        @pl.when(i == 0)
        def _():
            acc[...] = jnp.zeros_like(acc)

        y = (jnp.dot(ya_ref[...], wo_ref[0:H, :], preferred_element_type=F32)
             + jnp.dot(yp_ref[...], wo_ref[H:2 * H, :], preferred_element_type=F32))
        y_ref[...] = y
        d = (x_ref[...] + gt_ref[...] * (y * _rms(y) * g_ref[...])) - t_ref[...]
        dx_ref[...] = d * (1.0 / D)
        acc[...] += _colsum8(d * d)

        @pl.when(i == nt - 1)
        def _():
            l_ref[...] = jnp.zeros_like(l_ref) + jnp.sum(acc[...]) * (0.5 / D)

    tile = pl.BlockSpec((tm, D), lambda i: (i, 0))
    half = pl.BlockSpec((tm, H), lambda i: (i, 0))
    return pl.pallas_call(
        body, name="out_fwd_loss", grid=(nt,),
        in_specs=[half, half, pl.BlockSpec((2 * H, D), lambda i: (0, 0)), tile, _mod_row(l, 2, D), _layer_row(l, D),
                  tile],
        out_specs=[tile, tile, pl.BlockSpec((SUBLANES, LANES), lambda i: (0, 0))],
        out_shape=[jax.ShapeDtypeStruct((T, D), F32), jax.ShapeDtypeStruct((T, D), F32),
                   jax.ShapeDtypeStruct((SUBLANES, LANES), F32)],
        scratch_shapes=[pltpu.VMEM((SUBLANES, D), F32)],
        compiler_params=_params(VMEM_BIG),
    )(ya, yp, wo, x, mod4, g_post3, target)


def out_bwd(dx, y, ya, yp, wo, mod4, g_post3, l, after):
    T, D = dx.shape
    H = ya.shape[1]
    tm = 512
    nt = T // tm

    def body(dx_ref, y_ref, ya_ref, yp_ref, wo_ref, gt_ref, g_ref, after_ref,
             dya_ref, dyp_ref, dwo_ref, dgt_ref, dg_ref, acc_w, acc_gt, acc_g):
        i = pl.program_id(0)

        @pl.when(i == 0)
        def _():
            acc_w[...] = jnp.zeros_like(acc_w)
            acc_gt[...] = jnp.zeros_like(acc_gt)
            acc_g[...] = jnp.zeros_like(acc_g)

        yv = y_ref[...]
        dxv = dx_ref[...]
        g = g_ref[...]
        r = _rms(yv)
        yn = yv * r
        acc_gt[...] += _colsum8(dxv * (yn * g))
        dn = dxv * gt_ref[...]
        acc_g[...] += _colsum8(dn * yn)
        a = dn * g
        dy = r * (a - yn * jnp.mean(a * yn, axis=-1, keepdims=True))
        dyb = dy.astype(BF16)
        dyc = lax.dot_general(dyb, wo_ref[...], NT, preferred_element_type=F32)
        dya_ref[...] = dyc[:, 0:H].astype(BF16)
        dyp_ref[...] = dyc[:, H:2 * H].astype(BF16)
        acc_w[0:H, :] += lax.dot_general(ya_ref[...], dyb, TN, preferred_element_type=F32)
        acc_w[H:2 * H, :] += lax.dot_general(yp_ref[...], dyb, TN, preferred_element_type=F32)

        @pl.when(i == nt - 1)
        def _():
            dwo_ref[...] = acc_w[...].astype(BF16)
            dgt_ref[...] = jnp.sum(acc_gt[...], axis=0, keepdims=True)
            dg_ref[...] = jnp.sum(acc_g[...], axis=0, keepdims=True)

    row = pl.BlockSpec((1, D), lambda i: (0, 0))
    tile = pl.BlockSpec((tm, D), lambda i: (i, 0))
    half = pl.BlockSpec((tm, H), lambda i: (i, 0))
    full = pl.BlockSpec((2 * H, D), lambda i: (0, 0))
    return pl.pallas_call(
        body, name="out_bwd", grid=(nt,),
        in_specs=[tile, tile, half, half, full, _mod_row(l, 2, D), _layer_row(l, D), ANY],
        out_specs=[half, half, full, row, row],
        out_shape=[jax.ShapeDtypeStruct((T, H), BF16), jax.ShapeDtypeStruct((T, H), BF16),
                   jax.ShapeDtypeStruct((2 * H, D), BF16),
                   jax.ShapeDtypeStruct((1, D), F32), jax.ShapeDtypeStruct((1, D), F32)],
        scratch_shapes=[pltpu.VMEM((2 * H, D), F32), pltpu.VMEM((SUBLANES, D), F32), pltpu.VMEM((SUBLANES, D), F32)],
        compiler_params=_params(VMEM_BIG),
    )(dx, y, ya, yp, wo, mod4, g_post3, after)


def conv_bwd(proj, dya, wconv, l):
    T = proj.shape[0]
    R = R_CONV
    nblk = 4
    nchunk = T // R

    def body(u_ref, b_ref, c_ref, g_ref, dy_ref, w_ref, du_ref, db_ref, dc_ref, dg_ref, dw_ref):
        w0 = w_ref[pl.ds(0, 1), :]
        w1 = w_ref[pl.ds(1, 1), :]
        w2 = w_ref[pl.ds(2, 1), :]

        def chunk(k, carry):
            head, a0, a1, a2 = carry
            i = nchunk - 1 - k
            r0 = pl.multiple_of(i * R, R)
            h0 = pl.multiple_of(jnp.maximum(r0 - HIST, 0), HIST)
            first = i == 0
            ue = _load_ext(u_ref, r0, h0, first, R)
            ce = _load_ext(c_ref, r0, h0, first, R)
            ca = ce * ue
            ca0 = ca[HIST:]
            ca1 = _shift_down(ca, 1, R)
            ca2 = _shift_down(ca, 2, R)
            conv = w2 * ca0 + w1 * ca1 + w0 * ca2
            g = g_ref[pl.ds(r0, R), :].astype(F32)
            b = b_ref[pl.ds(r0, R), :].astype(F32)
            dy = dy_ref[pl.ds(r0, R), :].astype(F32)
            sg = _sigmoid(g)
            sl = g * sg
            t = dy * conv
            db_ref[pl.ds(r0, R), :] = (t * sl).astype(BF16)
            dg_ref[pl.ds(r0, R), :] = (t * b * (sg * (1.0 + g * (1.0 - sg)))).astype(BF16)
            dconv = dy * b * sl
            a2 = a2 + _colsum8(dconv * ca0)
            a1 = a1 + _colsum8(dconv * ca1)
            a0 = a0 + _colsum8(dconv * ca2)
            e = jnp.concatenate([dconv, head], axis=0)
            dca = w2 * dconv + w1 * _shift_up(e, 1, R) + w0 * _shift_up(e, 2, R)
            du_ref[pl.ds(r0, R), :] = (dca * ce[HIST:]).astype(BF16)
            dc_ref[pl.ds(r0, R), :] = (dca * ue[HIST:]).astype(BF16)
            return dconv[0:SUBLANES], a0, a1, a2

        z = jnp.zeros((SUBLANES, LANES), F32)
        _, a0, a1, a2 = lax.fori_loop(0, nchunk, chunk, (z, z, z, z))
        dw_ref[pl.ds(0, 1), :] = jnp.sum(a0, axis=0, keepdims=True)
        dw_ref[pl.ds(1, 1), :] = jnp.sum(a1, axis=0, keepdims=True)
        dw_ref[pl.ds(2, 1), :] = jnp.sum(a2, axis=0, keepdims=True)

    def col(off):
        return pl.BlockSpec((T, LANES), lambda j: (0, j + off))

    sec = jax.ShapeDtypeStruct((T, nblk * LANES), BF16)
    return pl.pallas_call(
        body, name="conv_bwd", grid=(nblk,),
        in_specs=[col(0), col(4), col(8), col(12), col(0), pl.BlockSpec((None, None, 3, LANES), lambda j: (j, l, 0, 0))],
        out_specs=[col(0), col(0), col(0), col(0), pl.BlockSpec((None, 3, LANES), lambda j: (j, 0, 0))],
        out_shape=[sec, sec, sec, sec, jax.ShapeDtypeStruct((nblk, 3, LANES), F32)],
        compiler_params=_params(),
    )(proj, proj, proj, proj, dya, wconv)


def pool_bwd(proj, dyp, wpool, pscale3, l, after):
    T = proj.shape[0]
    R = R_POOL
    ngrp = len(POOL_WINDOWS)
    nchunk = T // R

    def body(p_ref, g_ref, dy_ref, w_ref, s_ref, after_ref, du_ref, dg_ref, dw_ref, ds_ref,
             pooled_s, mixed_s, dmix_s, dpool_s):
        grp = pl.program_id(0)

        def group(w):
            wb = w_ref[...].astype(BF16)
            _pooled_loop(p_ref, pooled_s, w, T)
            mixed_s[...] = jnp.dot(pooled_s[...], wb, preferred_element_type=F32)
            sc = s_ref[...]

            def gate_chunk(i, acc):
                r0 = pl.multiple_of(i * R, R)
                g = g_ref[pl.ds(r0, R), :].astype(F32)
                dy = dy_ref[pl.ds(r0, R), :].astype(F32)
                mixed = mixed_s[pl.ds(r0, R), :]
                sg = _sigmoid(g)
                dg_ref[pl.ds(r0, R), :] = (dy * mixed * sc * (sg * (1.0 + g * (1.0 - sg)))).astype(BF16)
                dms = dy * (g * sg)
                dmix_s[pl.ds(r0, R), :] = (dms * sc).astype(BF16)
                return acc + _colsum8(dms * mixed)

            acc = lax.fori_loop(0, nchunk, gate_chunk, jnp.zeros((SUBLANES, LANES), F32))
            ds_ref[...] = jnp.sum(acc, axis=0, keepdims=True)
            dpool_s[pl.ds(0, T), :] = lax.dot_general(dmix_s[...], wb, NT, preferred_element_type=F32)
            dpool_s[pl.ds(T, HIST), :] = jnp.zeros((HIST, LANES), F32)
            dw_ref[...] = lax.dot_general(pooled_s[...], dmix_s[...], TN, preferred_element_type=F32).astype(BF16)

            def back_chunk(i, carry):
                r0 = pl.multiple_of(i * R, R)
                dpe = dpool_s[pl.ds(r0, R + HIST), :]
                e = dpe / _count(r0, R + HIST, w)
                du_ref[pl.ds(r0, R), :] = (_anticausal_window_sum(e, w)[0:R] - dpe[0:R]).astype(BF16)
                return carry

            lax.fori_loop(0, nchunk, back_chunk, 0)

        for k, w in enumerate(POOL_WINDOWS):
            pl.when(grp == k)(functools.partial(group, w))

    def col(off):
        return pl.BlockSpec((T, LANES), lambda j: (0, j + off))

    sec = jax.ShapeDtypeStruct((T, ngrp * LANES), BF16)
    wspec = pl.BlockSpec((None, LANES, LANES), lambda j: (j, 0, 0))
    sspec = pl.BlockSpec((1, LANES), lambda j: (0, j))
    return pl.pallas_call(
        body, name="pool_bwd", grid=(ngrp,),
        in_specs=[col(16), col(20), col(0), _pool_w_spec(l), _pool_s_spec(l), ANY],
        out_specs=[col(0), col(0), wspec, sspec],
        out_shape=[sec, sec, jax.ShapeDtypeStruct((ngrp, LANES, LANES), BF16),
                   jax.ShapeDtypeStruct((1, ngrp * LANES), F32)],
        scratch_shapes=[pltpu.VMEM((T, LANES), BF16), pltpu.VMEM((T, LANES), F32),
                        pltpu.VMEM((T, LANES), BF16), pltpu.VMEM((T + HIST, LANES), F32)],
        compiler_params=_params(),
    )(proj, proj, dyp, wpool, pscale3, after)


def in_bwd(dsecs, wg, x, dxo, mod4, g_pre3, l):
    T, D = x.shape
    NB = N_CHIPS
    CW = wg.shape[1] // NB
    SW = dsecs[0].shape[1]
    nsec = len(dsecs)
    PW = 256
    assert SW % PW == 0 and CW % PW == 0
    tm = 256
    nt = T // tm

    def body(*refs):
        d_refs = refs[0:nsec]
        w_ref, x_ref, dxo_ref, sh_ref, sc_ref, g_ref = refs[nsec:nsec + 6]
        dxi_ref, dw_ref, dsh_ref, dsc_ref, dg_ref = refs[nsec + 6:nsec + 11]
        acc_w, acc_sh, acc_sc, acc_g = refs[nsec + 11:]
        i = pl.program_id(0)

        @pl.when(i == 0)
        def _():
            acc_w[...] = jnp.zeros_like(acc_w)
            acc_sh[...] = jnp.zeros_like(acc_sh)
            acc_sc[...] = jnp.zeros_like(acc_sc)
            acc_g[...] = jnp.zeros_like(acc_g)

        xv = x_ref[...]
        g = g_ref[...]
        r = _rms(xv)
        xh = xv * r
        n = xh * g
        sc1 = 1.0 + sc_ref[...]
        hb = (n * sc1 + sh_ref[...]).astype(BF16)
        dh = lax.dot_general(d_refs[0][...], w_ref[:, 0:SW], NT, preferred_element_type=F32)
        for s in range(1, nsec):
            dh = dh + lax.dot_general(d_refs[s][...], w_ref[:, s * SW:(s + 1) * SW], NT, preferred_element_type=F32)
        for p in range(nsec * SW // PW):
            col = p * PW
            s, so = col // SW, col % SW
            j, jo = col // CW, col % CW
            acc_w[j, :, jo:jo + PW] += lax.dot_general(hb, d_refs[s][:, so:so + PW], TN, preferred_element_type=F32)
        acc_sh[...] += _colsum8(dh)
        acc_sc[...] += _colsum8(dh * n)
        dnp = dh * sc1
        acc_g[...] += _colsum8(dnp * xh)
        a = dnp * g
        dxi_ref[...] = dxo_ref[...] + r * (a - xh * jnp.mean(a * xh, axis=-1, keepdims=True))

        @pl.when(i == nt - 1)
        def _():
            dw_ref[...] = acc_w[...].astype(BF16)
            dsh_ref[...] = jnp.sum(acc_sh[...], axis=0, keepdims=True)
            dsc_ref[...] = jnp.sum(acc_sc[...], axis=0, keepdims=True)
            dg_ref[...] = jnp.sum(acc_g[...], axis=0, keepdims=True)

    row = pl.BlockSpec((1, D), lambda i: (0, 0))
    tile = pl.BlockSpec((tm, D), lambda i: (i, 0))
    sect = pl.BlockSpec((tm, SW), lambda i: (i, 0))
    rowshape = jax.ShapeDtypeStruct((1, D), F32)
    return pl.pallas_call(
        body, name="in_bwd", grid=(nt,),
        in_specs=[sect] * nsec + [pl.BlockSpec((D, NB * CW), lambda i: (0, 0)), tile, tile,
                                  _mod_row(l, 0, D), _mod_row(l, 1, D), _layer_row(l, D)],
        out_specs=[tile, pl.BlockSpec((NB, D, CW), lambda i: (0, 0, 0)), row, row, row],
        out_shape=[jax.ShapeDtypeStruct((T, D), F32), jax.ShapeDtypeStruct((NB, D, CW), BF16),
                   rowshape, rowshape, rowshape],
        scratch_shapes=[pltpu.VMEM((NB, D, CW), F32),
                        pltpu.VMEM((SUBLANES, D), F32), pltpu.VMEM((SUBLANES, D), F32), pltpu.VMEM((SUBLANES, D), F32)],
        compiler_params=_params(VMEM_BIG),
    )(*dsecs, wg, x, dxo, mod4, mod4, g_pre3)


def _rcopy(src, dst, ssem, rsem, dev):
    return pltpu.make_async_remote_copy(src_ref=src, dst_ref=dst, send_sem=ssem, recv_sem=rsem,
                                        device_id=dev, device_id_type=MESH)


def _peers7(x, y, c):
    out = []
    for m in range(1, N_DEV):
        bx, by, bc = (m >> 2) & 1, (m >> 1) & 1, m & 1
        out.append(((1 - x) if bx else x, (1 - y) if by else y, (1 - c) if bc else c))
    return out


HBM = pl.BlockSpec(memory_space=pltpu.HBM)
SEM = pl.BlockSpec(memory_space=pltpu.SEMAPHORE)
SPLIT = pltpu.CompilerParams(has_side_effects=pltpu.SideEffectType.DATAFLOW_SIDE_EFFECTING)


def _hbm(a):
    return pltpu.with_memory_space_constraint(a, pltpu.HBM)


def _chips(x, y):
    return [(1 - x, y), (x, 1 - y), (1 - x, 1 - y)]


SIBLING_BARRIER_ID = 0


def xchg_start(name, bufs, n_copies, plan, sibling_only=False):
    n = len(bufs)

    def body(*refs):
        ssem, rsem, token = refs[n], refs[n + 1], refs[-1]
        x, y, c = _me()
        if sibling_only:
            barrier = pltpu.get_barrier_semaphore()
            pl.semaphore_signal(barrier, inc=1, device_id=(x, y, 1 - c), device_id_type=MESH)
            pl.semaphore_wait(barrier, 1)
        copies = plan(refs[0:n], x, y, c)
        assert len(copies) == n_copies
        for k, (src, dst, peer, _) in enumerate(copies):
            _rcopy(src, dst, ssem.at[k], rsem.at[k], peer).start()
        token[...] = jnp.zeros_like(token)

    params = dict(has_side_effects=pltpu.SideEffectType.DATAFLOW_SIDE_EFFECTING)
    if sibling_only:
        params["collective_id"] = SIBLING_BARRIER_ID
    outs = pl.pallas_call(
        body, name=name,
        in_specs=[HBM] * n,
        out_specs=[SEM, SEM] + [HBM] * n + [pl.BlockSpec(memory_space=pltpu.VMEM)],
        out_shape=([pltpu.SemaphoreType.DMA((n_copies,))] * 2 + [pltpu.HBM(b.shape, b.dtype) for b in bufs]
                   + [jax.ShapeDtypeStruct((SUBLANES, LANES), F32)]),
        input_output_aliases={a: 2 + a for a in range(n)},
        compiler_params=pltpu.CompilerParams(**params),
    )(*[_hbm(b) for b in bufs])
    return outs[0], outs[1], list(outs[2:2 + n]), outs[-1]


def xchg_wait(name, bufs, ssem, rsem, n_copies, plan, after):
    n = len(bufs)
    after = list(after)

    def body(*refs):
        ssem_ref, rsem_ref = refs[n], refs[n + 1]
        copies = plan(refs[0:n], *_me())
        assert len(copies) == n_copies
        for k, (src, _, peer, land) in enumerate(copies):
            cp = _rcopy(src, land, ssem_ref.at[k], rsem_ref.at[k], peer)
            cp.wait_send()
            cp.wait_recv()

    outs = pl.pallas_call(
        body, name=name,
        in_specs=[HBM] * n + [SEM, SEM] + [ANY] * len(after), out_specs=[HBM] * n,
        out_shape=[pltpu.HBM(b.shape, b.dtype) for b in bufs],
        input_output_aliases={a: a for a in range(n)},
        compiler_params=SPLIT,
    )(*bufs, ssem, rsem, *after)
    return list(outs)


def _shard_half(buf, chip, half):
    if len(buf.shape) == 2:
        h, w = buf.shape[0] // 2, buf.shape[1] // N_CHIPS
        return buf.at[pl.ds(half * h, h), pl.ds(chip * w, w)]
    h = buf.shape[1] // 2
    return buf.at[chip, pl.ds(half * h, h)]


def plan_gather(refs, x, y, c):
    out = []
    for (px, py) in _chips(x, y):
        for buf in refs:
            own = _shard_half(buf, 2 * x + y, c)
            out.append((own, own, (px, py, c), _shard_half(buf, 2 * px + py, c)))
    return out


def plan_forward(refs, x, y, c):
    out = []
    for (px, py) in _chips(x, y):
        for buf in refs:
            landed = _shard_half(buf, 2 * px + py, c)
            out.append((landed, landed, (x, y, 1 - c), _shard_half(buf, 2 * px + py, 1 - c)))
    return out


def plan_sibling(refs, x, y, c):
    n = len(refs) // 2
    out = []
    for a in range(n):
        h = refs[a].shape[1] // 2
        out.append((refs[a].at[:, pl.ds((1 - c) * h, h)], refs[n + a], (x, y, 1 - c), refs[n + a]))
    return out


def plan_chip(refs, x, y, c):
    n = len(refs) // 2
    out = []
    for j, (px, py) in enumerate(_chips(x, y)):
        for a in range(n):
            out.append((refs[a].at[2 * px + py], refs[n + a].at[j], (px, py, c), refs[n + a].at[j]))
    return out


def plan_mod(refs, x, y, c):
    (mods,) = refs
    mine = mods.at[2 * x + y]
    return [(mine, mine, (px, py, c), mods.at[2 * px + py]) for (px, py) in _chips(x, y)]


def plan_pack(refs, x, y, c):
    (packs,) = refs
    mine = packs.at[4 * x + 2 * y + c]
    return [(mine, mine, peer, packs.at[4 * peer[0] + 2 * peer[1] + peer[2]]) for peer in _peers7(x, y, c)]


def plan_spread(layers, wp_layers):
    def plan(refs, x, y, c):
        gi, go, gp = refs
        hD, hR, hP = gi.shape[1] // 2, go.shape[1] // 2, gp.shape[2] // 2
        sib = (x, y, 1 - c)
        out = []
        for l in layers:
            mine = gi.at[l, pl.ds(c * hD, hD)]
            out.append((mine, mine, sib, gi.at[l, pl.ds((1 - c) * hD, hD)]))
            mine = go.at[l, pl.ds(c * hR, hR)]
            out.append((mine, mine, sib, go.at[l, pl.ds((1 - c) * hR, hR)]))
        for l in wp_layers:
            mine = gp.at[l, 2 * x + y, pl.ds(c * hP, hP)]
            for peer in _peers7(x, y, c):
                out.append((mine, mine, peer, gp.at[l, 2 * peer[0] + peer[1], pl.ds(peer[2] * hP, hP)]))
        return out

    return plan


def gather_small(c8, wc, token):
    def body(c_ref, wc_ref, token_ref, call, wcall, ssem, rsem, lsem):
        x, y, c = _me()
        myc = 2 * x + y
        me_lin = 4 * x + 2 * y + c
        me = (x, y, c)
        local = [pltpu.make_async_copy(c_ref, call.at[me_lin], lsem.at[0]),
                 pltpu.make_async_copy(wc_ref, wcall.at[myc], lsem.at[1])]
        for cp in local:
            cp.start()
        sends, recvs = [], []
        for m, peer in enumerate(_peers7(x, y, c)):
            plin = 4 * peer[0] + 2 * peer[1] + peer[2]
            sends.append(_rcopy(c_ref, call.at[me_lin], ssem.at[m], rsem.at[m], peer))
            recvs.append(_rcopy(call.at[plin], call.at[plin], ssem.at[m], rsem.at[m], me))
        for j, (px, py) in enumerate([(1 - x, y), (x, 1 - y), (1 - x, 1 - y)]):
            pc = 2 * px + py
            sends.append(_rcopy(wc_ref, wcall.at[myc], ssem.at[7 + j], rsem.at[7 + j], (px, py, c)))
            recvs.append(_rcopy(wcall.at[pc], wcall.at[pc], ssem.at[7 + j], rsem.at[7 + j], me))
        for cp in sends:
            cp.start()
        for cp in recvs:
            cp.wait_recv()
        for cp in sends:
            cp.wait_send()
        for cp in local:
            cp.wait()

    return pl.pallas_call(
        body, name="gather_small",
        in_specs=[ANY] * 3, out_specs=[ANY] * 2,
        out_shape=[jax.ShapeDtypeStruct((N_DEV, SUBLANES, LANES), F32),
                   jax.ShapeDtypeStruct((N_CHIPS, wc.shape[0], 3, LANES), F32)],
        scratch_shapes=[pltpu.SemaphoreType.DMA((10,)), pltpu.SemaphoreType.DMA((10,)), pltpu.SemaphoreType.DMA((2,))],
        compiler_params=_params(n_grid=0),
    )(c8, wc, token)


def spread_now(gi, go, gp, layers, wp_layers):
    plan = plan_spread(layers, wp_layers)
    n = 2 * len(layers) + 7 * len(wp_layers)

    def body(gi_in, go_in, gp_in, gi, go, gp, ssem, rsem):
        copies = plan((gi, go, gp), *_me())
        me = _me()
        sends = [_rcopy(src, dst, ssem.at[k], rsem.at[k], peer) for k, (src, dst, peer, _) in enumerate(copies)]
        for cp in sends:
            cp.start()
        for k, (_, _, _, land) in enumerate(copies):
            _rcopy(land, land, ssem.at[k], rsem.at[k], me).wait_recv()
        for cp in sends:
            cp.wait_send()

    return pl.pallas_call(
        body, name="spread_now",
        in_specs=[ANY] * 3, out_specs=[ANY] * 3,
        out_shape=[jax.ShapeDtypeStruct(a.shape, a.dtype) for a in (gi, go, gp)],
        input_output_aliases={0: 0, 1: 1, 2: 2},
        scratch_shapes=[pltpu.SemaphoreType.DMA((n,)), pltpu.SemaphoreType.DMA((n,))],
        compiler_params=_params(n_grid=0),
    )(gi, go, gp)


def add_sibling(cidx, mine, sib):
    def body(c_ref, *refs):
        for a in range(3):
            m, s, o = refs[a], refs[3 + a], refs[6 + a]
            o[...] = (m[...].astype(F32) + s[...].astype(F32)).astype(BF16)

    def mine_spec(a):
        h = a.shape[1] // 2
        return pl.BlockSpec((None, h, a.shape[2]), lambda j, c_ref: (j, c_ref[0], 0))

    def sib_spec(a):
        return pl.BlockSpec((None,) + a.shape[1:], lambda j, c_ref: (j, 0, 0))

    return pl.pallas_call(
        body, name="add_sibling",
        grid_spec=pltpu.PrefetchScalarGridSpec(
            num_scalar_prefetch=1, grid=(N_CHIPS,),
            in_specs=[mine_spec(a) for a in mine] + [sib_spec(a) for a in sib],
            out_specs=[sib_spec(a) for a in sib]),
        out_shape=[jax.ShapeDtypeStruct(a.shape, BF16) for a in sib],
        compiler_params=_params(VMEM_BIG),
    )(cidx, *mine, *sib)


def sum_chips(pos, own, rb, acc, l, shapes):
    nq = 4
    n_in = 6 + (3 if acc is not None else 0)

    def body(pos_ref, *refs):
        for a in range(3):
            m, b, o = refs[a], refs[3 + a], refs[n_in + a]
            s = m[...].astype(F32)
            for j in range(3):
                s = s + b[j].astype(F32)
            o[...] = s

    def own_spec(a):
        return pl.BlockSpec((None, a.shape[1] // nq, a.shape[2]), lambda q, p: (p[1], q, 0))

    def rb_spec(a):
        return pl.BlockSpec((3, a.shape[1] // nq, a.shape[2]), lambda q, p: (0, q, 0))

    hi, ho, hp = own[0].shape[1] // nq, own[1].shape[1] // nq, own[2].shape[1] // nq
    out_specs = [pl.BlockSpec((None, hi, shapes[0][2]), lambda q, p: (l, p[0] * nq + q, 0)),
                 pl.BlockSpec((None, ho, shapes[1][2]), lambda q, p: (l, p[0] * nq + q, 0)),
                 pl.BlockSpec((None, None, hp, LANES), lambda q, p: (l, p[1], p[0] * nq + q, 0))]
    in_specs = [own_spec(a) for a in own] + [rb_spec(a) for a in rb]
    args = list(own) + list(rb)
    aliases = {}
    if acc is not None:
        in_specs += [ANY] * 3
        args += list(acc)
        aliases = {7: 0, 8: 1, 9: 2}
    return pl.pallas_call(
        body, name="sum_chips",
        grid_spec=pltpu.PrefetchScalarGridSpec(num_scalar_prefetch=1, grid=(nq,), in_specs=in_specs, out_specs=out_specs),
        out_shape=[jax.ShapeDtypeStruct(s, F32) for s in shapes],
        input_output_aliases=aliases,
        compiler_params=_params(VMEM_BIG),
    )(pos, *args)


def pack_small(pos, per_layer, loss_blk):
    L = len(per_layer)
    D = per_layer[0][0].shape[1]

    def body(pos_ref, *refs):
        o = refs[-1]
        lb = refs[-2]
        o[...] = jnp.zeros_like(o)
        for l in range(L):
            dgpre, dgpost, dsh, dsc, dgt, dps, dwc = refs[7 * l:7 * l + 7]
            base = SUBLANES * l
            for r, src in enumerate((dgpre, dgpost, dsh, dsc, dgt)):
                o[pl.ds(base + r, 1), :] = src[...]
            o[pl.ds(base + 5, 1), 0:dps.shape[1]] = dps[...]
            for j in range(dwc.shape[0]):
                for k in range(3):
                    idx = 3 * j + k
                    o[pl.ds(base + 6 + idx // 8, 1), (idx % 8) * LANES:(idx % 8 + 1) * LANES] = dwc[j, pl.ds(k, 1), :]
        o[pl.ds(5, 1), 4 * LANES:5 * LANES] = lb[pl.ds(0, 1), :]

    flat = [a for layer in per_layer for a in layer] + [loss_blk]

    def whole(a):
        return pl.BlockSpec(a.shape, lambda i, p: (0,) * a.ndim)

    return pl.pallas_call(
        body, name="pack_small",
        grid_spec=pltpu.PrefetchScalarGridSpec(
            num_scalar_prefetch=1, grid=(1,), in_specs=[whole(a) for a in flat],
            out_specs=pl.BlockSpec((None, L * SUBLANES, D), lambda i, p: (p[2], 0, 0))),
        out_shape=jax.ShapeDtypeStruct((N_DEV, L * SUBLANES, D), F32),
        compiler_params=_params(),
    )(pos, *flat)


def small_update(pos, packs, params, moments_m, moments_v):
    n = len(params)
    L, D = params[1].shape
    PS = params[3].shape[1]

    def body(pos_ref, p_ref, *refs):
        ws, ms, vs = refs[0:n], refs[n:2 * n], refs[2 * n:3 * n]
        loss_ref = refs[3 * n]
        outs = [refs[3 * n + 1 + 4 * t:3 * n + 5 + 4 * t] for t in range(n)]
        summed = refs[-1]
        s = p_ref[0]
        for d in range(1, N_DEV):
            s = s + p_ref[d]
        summed[...] = s
        loss_ref[...] = summed[pl.ds(5, 1), 4 * LANES:5 * LANES]
        chip = pos_ref[1]

        def update(t, idx, g):
            d, mm, vv = _adamw_math(ws[t][idx], g, ms[t][idx], vs[t][idx])
            g_ref, d_ref, mo_ref, vo_ref = outs[t]
            g_ref[idx] = g
            d_ref[idx] = d
            mo_ref[idx] = mm
            vo_ref[idx] = vv

        for l in range(L):
            base = SUBLANES * l
            row = pl.ds(l, 1)
            for k in range(3):
                update(0, (row, slice(k * D, (k + 1) * D)), summed[pl.ds(base + 2 + k, 1), :])
            update(1, (row, slice(None)), summed[pl.ds(base, 1), :])
            update(2, (row, slice(None)), summed[pl.ds(base + 1, 1), :])
            update(3, (row, slice(None)), summed[pl.ds(base + 5, 1), 0:PS])
            for k in range(3):
                g = None
                for j in range(N_CHIPS):
                    idx = 3 * j + k
                    cand = summed[pl.ds(base + 6 + idx // 8, 1), (idx % 8) * LANES:(idx % 8 + 1) * LANES]
                    g = cand if g is None else jnp.where(chip == j, cand, g)
                update(4, (l, pl.ds(k, 1), slice(None)), g)

    def whole(a):
        return pl.BlockSpec(a.shape, lambda i, p: (0,) * a.ndim)

    ins = [packs] + list(params) + list(moments_m) + list(moments_v)
    out_shape = [jax.ShapeDtypeStruct((1, LANES), F32)]
    for w in params:
        out_shape += [jax.ShapeDtypeStruct(w.shape, F32)] * 4
    outs = pl.pallas_call(
        body, name="small_update",
        grid_spec=pltpu.PrefetchScalarGridSpec(
            num_scalar_prefetch=1, grid=(1,), in_specs=[whole(a) for a in ins],
            out_specs=[whole(a) for a in out_shape],
            scratch_shapes=[pltpu.VMEM(packs.shape[1:], F32)]),
        out_shape=out_shape,
        compiler_params=_params(),
    )(pos, *ins)
    return outs[0], [outs[1 + 4 * t:5 + 4 * t] for t in range(n)]


def _adamw_math(w, g, m, v):
    m = ADAM_B1 * m + (1.0 - ADAM_B1) * g
    v = ADAM_B2 * v + (1.0 - ADAM_B2) * (g * g)
    m_hat = m / (1.0 - ADAM_B1 ** ADAM_STEP)
    v_hat = v / (1.0 - ADAM_B2 ** ADAM_STEP)
    delta = -ADAM_LR * (m_hat / (jnp.sqrt(v_hat) + ADAM_EPS) + ADAM_WD * w)
    return delta, m, v


def adamw(w, g, m, v, block, name, first=0, count=None, acc=None):
    grid = tuple(s // b for s, b in zip(w.shape, block))
    if count is not None:
        grid = (count,) + grid[1:]

    def body(w_ref, g_ref, m_ref, v_ref, *rest):
        d_ref, mo_ref, vo_ref = rest[-3:]
        d, mm, vv = _adamw_math(w_ref[...], g_ref[...], m_ref[...], v_ref[...])
        d_ref[...] = d
        mo_ref[...] = mm
        vo_ref[...] = vv

    spec = pl.BlockSpec(block, lambda i, *rest: (first + i,) + rest)
    shape = jax.ShapeDtypeStruct(w.shape, F32)
    extra = [] if acc is None else list(acc)
    return pl.pallas_call(
        body, name=name, grid=grid,
        in_specs=[spec] * 4 + [ANY] * len(extra), out_specs=[spec] * 3, out_shape=[shape] * 3,
        input_output_aliases={4 + a: a for a in range(len(extra))},
        compiler_params=_params(VMEM_BIG, n_grid=len(grid)),
    )(w, g, m, v, *extra)


def ada_finish(c_all, dmod, w, m, v):
    L, D, CW = w.shape
    hD = D // 2

    def body(c_ref, d_ref, w_ref, m_ref, v_ref, g_ref, dl_ref, mo_ref, vo_ref):
        cv = c_ref[...]
        z = jnp.zeros_like(cv)
        ca = jnp.concatenate([cv * jax.nn.sigmoid(cv), z], axis=0).astype(BF16)
        dm = jnp.concatenate([d_ref[0], jnp.zeros_like(d_ref[0])], axis=0).astype(BF16)
        g = lax.dot_general(ca, dm, TN, preferred_element_type=F32)
        g_ref[0] = g
        d, mm, vv = _adamw_math(w_ref[0], g, m_ref[0], v_ref[0])
        dl_ref[0] = d
        mo_ref[0] = mm
        vo_ref[0] = vv

    big = pl.BlockSpec((1, hD, CW), lambda l, h: (l, h, 0))
    shape = jax.ShapeDtypeStruct(w.shape, F32)
    return pl.pallas_call(
        body, name="ada_finish", grid=(L, 2),
        in_specs=[pl.BlockSpec((N_DEV, hD), lambda l, h: (0, h)), pl.BlockSpec((1, N_DEV, CW), lambda l, h: (l, 0, 0)),
                  big, big, big],
        out_specs=[big] * 4, out_shape=[shape] * 4,
        compiler_params=_params(VMEM_BIG, n_grid=2),
    )(c_all, dmod, w, m, v)


def kernel(x, c, w_ada, b_ada, g_pre, w_in, w_conv, w_pool, pool_scale, w_out, g_post, loss_target, m_w_ada, m_b_ada, m_g_pre, m_w_in, m_w_conv, m_w_pool, m_pool_scale, m_w_out, m_g_post, v_w_ada, v_b_ada, v_g_pre, v_w_in, v_w_conv, v_w_pool, v_pool_scale, v_w_out, v_g_post):
    L, D, CW = w_in.shape
    RO = w_out.shape[1]
    T = x.shape[1]
    ix, iy, ic = _me()
    chip = 2 * ix + iy
    me_lin = 4 * ix + 2 * iy + ic

    pos = jnp.stack([ic, chip, me_lin]).astype(jnp.int32)
    g_pre3, g_post3 = g_pre.reshape(L, 1, D), g_post.reshape(L, 1, D)
    pscale3 = pool_scale.reshape(L, 1, pool_scale.shape[1])
    n_g, n_s, n_c = 6, 3, 9

    c_all3, wconv_all = gather_small(c.reshape(SUBLANES, LANES), w_conv, pos)
    c_all = c_all3.reshape(N_DEV, D)
    gath = [None] * L
    ss, rs, bufs, token = xchg_start("gather_start", list(cast_weights(pos, w_in, w_out, 0, c_all3)), n_g, plan_gather)
    gath[0] = (ss, rs, bufs)
    b_my = lax.dynamic_slice_in_dim(b_ada, chip * CW, CW, axis=1)
    m_ss, m_rs, mods, token = xchg_start("mod_start", [mod_part(pos, c_all, w_ada, b_my, token)], 3, plan_mod)
    for l in range(1, L):
        ss, rs, bufs, token = xchg_start("gather_start", list(cast_weights(pos, w_in, w_out, l, token)), n_g, plan_gather)
        gath[l] = (ss, rs, bufs)
    (mod_all,) = xchg_wait("mod_wait", mods, m_ss, m_rs, 3, plan_mod, [token])
    mod = lax.dynamic_index_in_dim(mod_all, me_lin, axis=2, keepdims=False)
    mod4 = jnp.transpose(mod, (1, 0, 2)).reshape(L, 3, 1, D)
    token = mod4

    def arrive(l, after):
        ss, rs, bufs = gath[l]
        bufs = xchg_wait("gather_wait", bufs, ss, rs, n_g, plan_gather, after)
        return xchg_start("forward_start", bufs, n_g, plan_forward, sibling_only=True)

    xs, projs, yas, yps, ys = [x.reshape(T, D)], [], [], [], []
    wg_in, wg_out = [], []
    fwd = arrive(0, [token])
    for l in range(L):
        fss, frs, bufs, ftoken = fwd
        gi, go = xchg_wait("forward_wait", bufs, fss, frs, n_g, plan_forward, [ftoken if l == 0 else xs[l]])
        wg_in.append(gi)
        wg_out.append(go.reshape(N_CHIPS * RO, D))
        proj = proj_fwd(xs[l], mod4, g_pre3, wg_in[l], l)
        ya = conv_fwd(proj, wconv_all, l)
        yp = pool_fwd(proj, w_pool, pscale3, l)
        projs.append(proj)
        yas.append(ya)
        yps.append(yp)
        if l + 1 < L:
            fwd = arrive(l + 1, [ya, yp])
            xn, yv = out_fwd(ya, yp, wg_out[l], xs[l], mod4, g_post3, l, fwd[3])
            xs.append(xn)
        else:
            dx, yv, loss_blk = out_fwd_loss(ya, yp, wg_out[l], xs[l], mod4, g_post3, l, loss_target.reshape(T, D))
        ys.append(yv)

    shapes = (w_in.shape, w_out.shape, w_pool.shape)
    smalls = [None] * L
    acc, flying, sib, token = None, None, None, loss_blk

    def to_chips(sib, after):
        sl, s_ss, s_rs, s_bufs = sib
        s_bufs = xchg_wait("sibling_wait", s_bufs, s_ss, s_rs, n_s, plan_sibling, after)
        chip_parts = add_sibling(pos, s_bufs[0:3], s_bufs[3:6])
        lands = [lax.empty((3,) + a.shape[1:], a.dtype) for a in chip_parts]
        c_ss, c_rs, c_bufs, ctoken = xchg_start("chip_start", list(chip_parts) + lands, n_c, plan_chip)
        return (sl, c_ss, c_rs, c_bufs), ctoken

    def landed(flying, acc, after):
        fl, f_ss, f_rs, f_bufs = flying
        f_bufs = xchg_wait("chip_wait", f_bufs, f_ss, f_rs, n_c, plan_chip, after)
        return sum_chips(pos, f_bufs[0:3], f_bufs[3:6], acc, fl, shapes)

    for l in reversed(range(L)):
        dya, dyp, dwo_l, dgate, dgpost = out_bwd(dx, ys[l], yas[l], yps[l], wg_out[l], mod4, g_post3, l, token)
        token = dya
        if sib is not None:
            arrived = flying
            flying, token = to_chips(sib, [dya])
            if arrived is not None:
                acc = landed(arrived, acc, [token])
                token = acc[0]
        du_p, dg_p, dwp_l, dps = pool_bwd(projs[l], dyp, w_pool, pscale3, l, token)
        du_a, db_a, dc_a, dg_a, dwc = conv_bwd(projs[l], dya, wconv_all, l)
        dx, dwi_l, dshift, dscale, dgpre = in_bwd([du_a, db_a, dc_a, dg_a, du_p, dg_p], wg_in[l], xs[l], dx,
                                                  mod4, g_pre3, l)
        smalls[l] = (dgpre, dgpost, dshift, dscale, dgate, dps, dwc)
        parts = [dwi_l, dwo_l.reshape(N_CHIPS, RO, D), dwp_l]
        s_lands = [lax.empty((a.shape[0], a.shape[1] // 2) + a.shape[2:], a.dtype) for a in parts]
        s_ss, s_rs, s_bufs, token = xchg_start("sibling_start", parts + s_lands, n_s, plan_sibling, sibling_only=True)
        sib = (l, s_ss, s_rs, s_bufs)
    grad_x = dx.reshape(1, T, D)

    p_ss, p_rs, packs, ptoken = xchg_start("pack_start", [pack_small(pos, smalls, loss_blk)], N_DEV - 1, plan_pack)
    acc = landed(flying, acc, [ptoken, token])
    flying, token = to_chips(sib, [acc[0]])
    n_sp = (2 + N_DEV - 1) * (L - 1)
    spread = plan_spread(tuple(range(1, L)), tuple(range(1, L)))
    sp_ss, sp_rs, acc, sp_token = xchg_start("spread_start", list(acc), n_sp, spread)
    (packs_all,) = xchg_wait("pack_wait", packs, p_ss, p_rs, N_DEV - 1, plan_pack, [sp_token, token])
    dmod_all = packs_all.reshape(N_DEV, L, SUBLANES, D)[:, :, 2:5].reshape(N_DEV, L, 3 * D)
    dmod_my = jnp.transpose(lax.dynamic_slice_in_dim(dmod_all, chip * CW, CW, axis=2), (1, 0, 2))

    g_w_ada, d_w_ada, nm_w_ada, nv_w_ada = ada_finish(c_all, dmod_my, w_ada, m_w_ada, v_w_ada)
    loss_row, upd = small_update(pos, packs_all, [b_ada, g_pre, g_post, pool_scale, w_conv],
                                 [m_b_ada, m_g_pre, m_g_post, m_pool_scale, m_w_conv],
                                 [v_b_ada, v_g_pre, v_g_post, v_pool_scale, v_w_conv])
    loss = loss_row[0, 0]
    (g_b_ada, d_b_ada, nm_b_ada, nv_b_ada), (g_g_pre, d_g_pre, nm_g_pre, nv_g_pre) = upd[0], upd[1]
    (g_g_post, d_g_post, nm_g_post, nv_g_post), (g_pscale, d_pscale, nm_pscale, nv_pscale) = upd[2], upd[3]
    g_w_conv, d_w_conv, nm_w_conv, nv_w_conv = upd[4]

    done = [nv_w_ada, nv_w_conv]
    g_w_in, g_w_out, g_w_pool = xchg_wait("spread_wait", acc, sp_ss, sp_rs, n_sp, spread, done)
    in_blk, out_blk = (1, D // 2, CW), (1, RO, D)
    upd_in = adamw(w_in, g_w_in, m_w_in, v_w_in, in_blk, "adamw_w_in", 1, L - 1)
    upd_out = adamw(w_out, g_w_out, m_w_out, v_w_out, out_blk, "adamw_w_out", 1, L - 1)

    acc = landed(flying, (g_w_in, g_w_out, g_w_pool), [upd_in[2], upd_out[2]])
    g_w_in, g_w_out, g_w_pool = spread_now(*acc, (0,), (0,))
    d_w_in, nm_w_in, nv_w_in = adamw(w_in, g_w_in, m_w_in, v_w_in, in_blk, "adamw_w_in", 0, 1, upd_in)
    d_w_out, nm_w_out, nv_w_out = adamw(w_out, g_w_out, m_w_out, v_w_out, out_blk, "adamw_w_out", 0, 1, upd_out)
    pshape = (L, N_CHIPS * LANES, LANES)
    d_w_pool, nm_w_pool, nv_w_pool = adamw(w_pool.reshape(pshape), g_w_pool.reshape(pshape), m_w_pool.reshape(pshape),
                                           v_w_pool.reshape(pshape), (1,) + pshape[1:], "adamw_w_pool")
    d_w_pool, nm_w_pool, nv_w_pool = [a.reshape(w_pool.shape) for a in (d_w_pool, nm_w_pool, nv_w_pool)]

    return (loss, grad_x,
            g_w_ada, g_b_ada, g_g_pre, g_w_in, g_w_conv, g_w_pool, g_pscale, g_w_out, g_g_post,
            d_w_ada, d_b_ada, d_g_pre, d_w_in, d_w_conv, d_w_pool, d_pscale, d_w_out, d_g_post,
            nm_w_ada, nm_b_ada, nm_g_pre, nm_w_in, nm_w_conv, nm_w_pool, nm_pscale, nm_w_out, nm_g_post,
            nv_w_ada, nv_b_ada, nv_g_pre, nv_w_in, nv_w_conv, nv_w_pool, nv_pscale, nv_w_out, nv_g_post)
```

```python
import functools

import jax
import jax.numpy as jnp
from jax import lax
from jax.experimental import pallas as pl
from jax.experimental.pallas import tpu as pltpu

F32 = jnp.float32
BF16 = jnp.bfloat16
MESH = pl.DeviceIdType.MESH
ANY = pl.BlockSpec(memory_space=pl.ANY)

NORM_EPS = 1e-6
POOL_WINDOWS = (2, 4, 8, 16)
ADAM_LR = 0.001
ADAM_B1 = 0.9
ADAM_B2 = 0.999
ADAM_EPS = 1e-08
ADAM_WD = 0.01
ADAM_STEP = 10

N_CHIPS = 4
N_DEV = 8
LANES = 128
SUBLANES = 8
VMEM_BIG = 56 * 1024 * 1024
HIST = 16
R_CONV = 32
R_POOL = 64

NT = (((1,), (1,)), ((), ()))
TN = (((0,), (0,)), ((), ()))


def _params(vmem=None, n_grid=1):
    kw = {}
    if n_grid:
        kw["dimension_semantics"] = ("arbitrary",) * n_grid
    if vmem is not None:
        kw["vmem_limit_bytes"] = vmem
    return pltpu.CompilerParams(**kw)


def _colsum8(v):
    n, d = v.shape
    return v.reshape(n // SUBLANES, SUBLANES, d).sum(axis=0)


def _rms(v):
    return lax.rsqrt(jnp.mean(v * v, axis=-1, keepdims=True) + NORM_EPS)


def _sigmoid(v):
    return 0.5 * jnp.tanh(0.5 * v) + 0.5


def _shift_down(ext, k, rows):
    if k == 0:
        return ext[HIST:HIST + rows]
    return pltpu.roll(ext, k, 0)[HIST:HIST + rows]


def _shift_up(ext, k, rows):
    if k == 0:
        return ext[0:rows]
    return pltpu.roll(ext, ext.shape[0] - k, 0)[0:rows]


def _load_ext(ref, r0, h0, first, rows):
    hist = ref[pl.ds(h0, HIST), :].astype(F32)
    hist = jnp.where(first, 0.0, hist)
    cur = ref[pl.ds(r0, rows), :].astype(F32)
    return jnp.concatenate([hist, cur], axis=0)


def _me():
    return lax.axis_index("x"), lax.axis_index("y"), lax.axis_index("c")


def cast_weights(pos, w_in, w_out, l, after):
    _, D, CW = w_in.shape
    RO = w_out.shape[1]

    def body(pos_ref, wi, wo, after_ref, oi, oo):
        oi[...] = wi[...].astype(BF16)
        oo[...] = wo[...].astype(BF16)

    return pl.pallas_call(
        body, name="cast_w",
        grid_spec=pltpu.PrefetchScalarGridSpec(
            num_scalar_prefetch=1, grid=(2,),
            in_specs=[pl.BlockSpec((None, D // 2, CW), lambda h, p: (l, h, 0)),
                      pl.BlockSpec((None, RO // 2, D), lambda h, p: (l, h, 0)), ANY],
            out_specs=[pl.BlockSpec((D // 2, CW), lambda h, p: (h, p[1])),
                       pl.BlockSpec((None, RO // 2, D), lambda h, p: (p[1], h, 0))]),
        out_shape=[jax.ShapeDtypeStruct((D, N_CHIPS * CW), BF16), jax.ShapeDtypeStruct((N_CHIPS, RO, D), BF16)],
        compiler_params=_params(),
    )(pos, w_in, w_out, after)


def mod_part(pos, c_all, w_ada, b_my, after):
    L, D, CW = w_ada.shape

    def body(pos_ref, c_ref, w_ref, b_ref, after_ref, o_ref):
        cv = c_ref[...]
        ca = (cv * jax.nn.sigmoid(cv)).astype(BF16)
        o_ref[...] = jnp.dot(ca, w_ref[0].astype(BF16), preferred_element_type=F32) + b_ref[0]

    return pl.pallas_call(
        body, name="mod_part",
        grid_spec=pltpu.PrefetchScalarGridSpec(
            num_scalar_prefetch=1, grid=(L,),
            in_specs=[pl.BlockSpec((N_DEV, D), lambda l, p: (0, 0)),
                      pl.BlockSpec((1, D, CW), lambda l, p: (l, 0, 0)),
                      pl.BlockSpec((1, 1, CW), lambda l, p: (l, 0, 0)), ANY],
            out_specs=pl.BlockSpec((None, None, N_DEV, CW), lambda l, p: (p[1], l, 0, 0))),
        out_shape=jax.ShapeDtypeStruct((N_CHIPS, L, N_DEV, CW), F32),
        compiler_params=_params(VMEM_BIG),
    )(pos, c_all, w_ada, b_my.reshape(L, 1, CW), after)


def _mod_row(l, k, D):
    return pl.BlockSpec((None, None, 1, D), lambda *_: (l, k, 0, 0))


def _layer_row(l, D):
    return pl.BlockSpec((None, 1, D), lambda *_: (l, 0, 0))


def proj_fwd(x, mod4, g_pre3, wg, l):
    T, D = x.shape
    NC = wg.shape[1]
    NB = N_CHIPS
    CW = NC // NB
    tm = 512

    def body(x_ref, sh_ref, sc_ref, g_ref, w_ref, o_ref):
        xv = x_ref[...]
        h = (xv * _rms(xv) * g_ref[...]) * (1.0 + sc_ref[...]) + sh_ref[...]
        hb = h.astype(BF16)
        for j in range(NB):
            cols = slice(j * CW, (j + 1) * CW)
            o_ref[:, cols] = jnp.dot(hb, w_ref[:, cols], preferred_element_type=F32).astype(BF16)

    return pl.pallas_call(
        body, name="proj_fwd", grid=(T // tm,),
        in_specs=[pl.BlockSpec((tm, D), lambda i: (i, 0)), _mod_row(l, 0, D), _mod_row(l, 1, D), _layer_row(l, D),
                  pl.BlockSpec((D, NC), lambda i: (0, 0))],
        out_specs=pl.BlockSpec((tm, NC), lambda i: (i, 0)),
        out_shape=jax.ShapeDtypeStruct((T, NC), BF16),
        compiler_params=_params(VMEM_BIG),
    )(x, mod4, mod4, g_pre3, wg)


def conv_fwd(proj, wconv, l):
    T = proj.shape[0]
    R = R_CONV
    nblk = 4

    def body(u_ref, b_ref, c_ref, g_ref, w_ref, o_ref):
        w0 = w_ref[pl.ds(0, 1), :]
        w1 = w_ref[pl.ds(1, 1), :]
        w2 = w_ref[pl.ds(2, 1), :]

        def chunk(i, carry):
            r0 = pl.multiple_of(i * R, R)
            h0 = pl.multiple_of(jnp.maximum(r0 - HIST, 0), HIST)
            first = i == 0
            ca = _load_ext(c_ref, r0, h0, first, R) * _load_ext(u_ref, r0, h0, first, R)
            conv = w2 * ca[HIST:] + w1 * _shift_down(ca, 1, R) + w0 * _shift_down(ca, 2, R)
            g = g_ref[pl.ds(r0, R), :].astype(F32)
            b = b_ref[pl.ds(r0, R), :].astype(F32)
            o_ref[pl.ds(r0, R), :] = (b * conv * (g * _sigmoid(g))).astype(BF16)
            return carry

        lax.fori_loop(0, T // R, chunk, 0)

    def col(off):
        return pl.BlockSpec((T, LANES), lambda j: (0, j + off))

    return pl.pallas_call(
        body, name="conv_fwd", grid=(nblk,),
        in_specs=[col(0), col(4), col(8), col(12), pl.BlockSpec((None, None, 3, LANES), lambda j: (j, l, 0, 0))],
        out_specs=pl.BlockSpec((T, LANES), lambda j: (0, j)),
        out_shape=jax.ShapeDtypeStruct((T, nblk * LANES), BF16),
        compiler_params=_params(),
    )(proj, proj, proj, proj, wconv)


def _causal_window_sum(ext, w):
    s, k = ext, 1
    while k < w:
        s = s + pltpu.roll(s, k, 0)
        k *= 2
    return s


def _anticausal_window_sum(ext, w):
    s, k = ext, 1
    n = ext.shape[0]
    while k < w:
        s = s + pltpu.roll(s, n - k, 0)
        k *= 2
    return s


def _count(r0, rows, w):
    t = r0 + lax.broadcasted_iota(jnp.int32, (rows, LANES), 0)
    return jnp.minimum(t + 1, w).astype(F32)


def _pooled_loop(p_ref, pooled_s, w, T):
    R = R_POOL

    def chunk(i, carry):
        r0 = pl.multiple_of(i * R, R)
        h0 = pl.multiple_of(jnp.maximum(r0 - HIST, 0), HIST)
        ext = _load_ext(p_ref, r0, h0, i == 0, R)
        ws = _causal_window_sum(ext, w)[HIST:]
        pooled_s[pl.ds(r0, R), :] = (ws / _count(r0, R, w) - ext[HIST:]).astype(BF16)
        return carry

    lax.fori_loop(0, T // R, chunk, 0)


def _pool_w_spec(l):
    return pl.BlockSpec((None, None, LANES, LANES), lambda j: (l, j, 0, 0))


def _pool_s_spec(l):
    return pl.BlockSpec((None, 1, LANES), lambda j: (l, 0, j))


def pool_fwd(proj, wpool, pscale3, l):
    T = proj.shape[0]
    R = R_POOL
    ngrp = len(POOL_WINDOWS)

    def body(p_ref, g_ref, w_ref, s_ref, o_ref, pooled_s, mixed_s):
        grp = pl.program_id(0)

        def group(w):
            _pooled_loop(p_ref, pooled_s, w, T)
            mixed_s[...] = jnp.dot(pooled_s[...], w_ref[...].astype(BF16), preferred_element_type=F32)
            sc = s_ref[...]

            def chunk(i, carry):
                r0 = pl.multiple_of(i * R, R)
                g = g_ref[pl.ds(r0, R), :].astype(F32)
                o_ref[pl.ds(r0, R), :] = (mixed_s[pl.ds(r0, R), :] * sc * (g * _sigmoid(g))).astype(BF16)
                return carry

            lax.fori_loop(0, T // R, chunk, 0)

        for k, w in enumerate(POOL_WINDOWS):
            pl.when(grp == k)(functools.partial(group, w))

    return pl.pallas_call(
        body, name="pool_fwd", grid=(ngrp,),
        in_specs=[pl.BlockSpec((T, LANES), lambda j: (0, j + 16)), pl.BlockSpec((T, LANES), lambda j: (0, j + 20)),
                  _pool_w_spec(l), _pool_s_spec(l)],
        out_specs=pl.BlockSpec((T, LANES), lambda j: (0, j)),
        out_shape=jax.ShapeDtypeStruct((T, ngrp * LANES), BF16),
        scratch_shapes=[pltpu.VMEM((T, LANES), BF16), pltpu.VMEM((T, LANES), F32)],
        compiler_params=_params(),
    )(proj, proj, wpool, pscale3)


def out_fwd(ya, yp, wo, x, mod4, g_post3, l, after):
    T, D = x.shape
    H = ya.shape[1]
    tm = 512

    def body(ya_ref, yp_ref, wo_ref, x_ref, gt_ref, g_ref, after_ref, xn_ref, y_ref):
        y = (jnp.dot(ya_ref[...], wo_ref[0:H, :], preferred_element_type=F32)
             + jnp.dot(yp_ref[...], wo_ref[H:2 * H, :], preferred_element_type=F32))
        xn_ref[...] = x_ref[...] + gt_ref[...] * (y * _rms(y) * g_ref[...])
        y_ref[...] = y

    tile = pl.BlockSpec((tm, D), lambda i: (i, 0))
    half = pl.BlockSpec((tm, H), lambda i: (i, 0))
    return pl.pallas_call(
        body, name="out_fwd", grid=(T // tm,),
        in_specs=[half, half, pl.BlockSpec((2 * H, D), lambda i: (0, 0)), tile, _mod_row(l, 2, D), _layer_row(l, D),
                  ANY],
        out_specs=[tile, tile],
        out_shape=[jax.ShapeDtypeStruct((T, D), F32), jax.ShapeDtypeStruct((T, D), F32)],
        compiler_params=_params(VMEM_BIG),
    )(ya, yp, wo, x, mod4, g_post3, after)


def out_fwd_loss(ya, yp, wo, x, mod4, g_post3, l, target):
    T, D = x.shape
    H = ya.shape[1]
    tm = 512
    nt = T // tm

    def body(ya_ref, yp_ref, wo_ref, x_ref, gt_ref, g_ref, t_ref, dx_ref, y_ref, l_ref, acc):
        i = pl.program_id(0)

        @pl.when(i == 0)
        def _():
            acc[...] = jnp.zeros_like(acc)

        y = (jnp.dot(ya_ref[...], wo_ref[0:H, :], preferred_element_type=F32)
             + jnp.dot(yp_ref[...], wo_ref[H:2 * H, :], preferred_element_type=F32))
        y_ref[...] = y
        d = (x_ref[...] + gt_ref[...] * (y * _rms(y) * g_ref[...])) - t_ref[...]
        dx_ref[...] = d * (1.0 / D)
        acc[...] += _colsum8(d * d)

        @pl.when(i == nt - 1)
        def _():
            l_ref[...] = jnp.zeros_like(l_ref) + jnp.sum(acc[...]) * (0.5 / D)

    tile = pl.BlockSpec((tm, D), lambda i: (i, 0))
    half = pl.BlockSpec((tm, H), lambda i: (i, 0))
    return pl.pallas_call(
        body, name="out_fwd_loss", grid=(nt,),
        in_specs=[half, half, pl.BlockSpec((2 * H, D), lambda i: (0, 0)), tile, _mod_row(l, 2, D), _layer_row(l, D),
                  tile],
        out_specs=[tile, tile, pl.BlockSpec((SUBLANES, LANES), lambda i: (0, 0))],
        out_shape=[jax.ShapeDtypeStruct((T, D), F32), jax.ShapeDtypeStruct((T, D), F32),
                   jax.ShapeDtypeStruct((SUBLANES, LANES), F32)],
        scratch_shapes=[pltpu.VMEM((SUBLANES, D), F32)],
        compiler_params=_params(VMEM_BIG),
    )(ya, yp, wo, x, mod4, g_post3, target)


def out_bwd(dx, y, ya, yp, wo, mod4, g_post3, l, after):
    T, D = dx.shape
    H = ya.shape[1]
    tm = 512
    nt = T // tm

    def body(dx_ref, y_ref, ya_ref, yp_ref, wo_ref, gt_ref, g_ref, after_ref,
             dya_ref, dyp_ref, dwo_ref, dgt_ref, dg_ref, acc_w, acc_gt, acc_g):
        i = pl.program_id(0)

        @pl.when(i == 0)
        def _():
            acc_w[...] = jnp.zeros_like(acc_w)
            acc_gt[...] = jnp.zeros_like(acc_gt)
            acc_g[...] = jnp.zeros_like(acc_g)

        yv = y_ref[...]
        dxv = dx_ref[...]
        g = g_ref[...]
        r = _rms(yv)
        yn = yv * r
        acc_gt[...] += _colsum8(dxv * (yn * g))
        dn = dxv * gt_ref[...]
        acc_g[...] += _colsum8(dn * yn)
        a = dn * g
        dy = r * (a - yn * jnp.mean(a * yn, axis=-1, keepdims=True))
        dyb = dy.astype(BF16)
        dyc = lax.dot_general(dyb, wo_ref[...], NT, preferred_element_type=F32)
        dya_ref[...] = dyc[:, 0:H].astype(BF16)
        dyp_ref[...] = dyc[:, H:2 * H].astype(BF16)
        acc_w[0:H, :] += lax.dot_general(ya_ref[...], dyb, TN, preferred_element_type=F32)
        acc_w[H:2 * H, :] += lax.dot_general(yp_ref[...], dyb, TN, preferred_element_type=F32)

        @pl.when(i == nt - 1)
        def _():
            dwo_ref[...] = acc_w[...].astype(BF16)
            dgt_ref[...] = jnp.sum(acc_gt[...], axis=0, keepdims=True)
            dg_ref[...] = jnp.sum(acc_g[...], axis=0, keepdims=True)

    row = pl.BlockSpec((1, D), lambda i: (0, 0))
    tile = pl.BlockSpec((tm, D), lambda i: (i, 0))
    half = pl.BlockSpec((tm, H), lambda i: (i, 0))
    full = pl.BlockSpec((2 * H, D), lambda i: (0, 0))
    return pl.pallas_call(
        body, name="out_bwd", grid=(nt,),
        in_specs=[tile, tile, half, half, full, _mod_row(l, 2, D), _layer_row(l, D), ANY],
        out_specs=[half, half, full, row, row],
        out_shape=[jax.ShapeDtypeStruct((T, H), BF16), jax.ShapeDtypeStruct((T, H), BF16),
                   jax.ShapeDtypeStruct((2 * H, D), BF16),
                   jax.ShapeDtypeStruct((1, D), F32), jax.ShapeDtypeStruct((1, D), F32)],
        scratch_shapes=[pltpu.VMEM((2 * H, D), F32), pltpu.VMEM((SUBLANES, D), F32), pltpu.VMEM((SUBLANES, D), F32)],
        compiler_params=_params(VMEM_BIG),
    )(dx, y, ya, yp, wo, mod4, g_post3, after)


def conv_bwd(proj, dya, wconv, l):
    T = proj.shape[0]
    R = R_CONV
    nblk = 4
    nchunk = T // R

    def body(u_ref, b_ref, c_ref, g_ref, dy_ref, w_ref, du_ref, db_ref, dc_ref, dg_ref, dw_ref):
        w0 = w_ref[pl.ds(0, 1), :]
        w1 = w_ref[pl.ds(1, 1), :]
        w2 = w_ref[pl.ds(2, 1), :]

        def chunk(k, carry):
            head, a0, a1, a2 = carry
            i = nchunk - 1 - k
            r0 = pl.multiple_of(i * R, R)
            h0 = pl.multiple_of(jnp.maximum(r0 - HIST, 0), HIST)
            first = i == 0
            ue = _load_ext(u_ref, r0, h0, first, R)
            ce = _load_ext(c_ref, r0, h0, first, R)
            ca = ce * ue
            ca0 = ca[HIST:]
            ca1 = _shift_down(ca, 1, R)
            ca2 = _shift_down(ca, 2, R)
            conv = w2 * ca0 + w1 * ca1 + w0 * ca2
            g = g_ref[pl.ds(r0, R), :].astype(F32)
            b = b_ref[pl.ds(r0, R), :].astype(F32)
            dy = dy_ref[pl.ds(r0, R), :].astype(F32)
            sg = _sigmoid(g)
            sl = g * sg
            t = dy * conv
            db_ref[pl.ds(r0, R), :] = (t * sl).astype(BF16)
            dg_ref[pl.ds(r0, R), :] = (t * b * (sg * (1.0 + g * (1.0 - sg)))).astype(BF16)
            dconv = dy * b * sl
            a2 = a2 + _colsum8(dconv * ca0)
            a1 = a1 + _colsum8(dconv * ca1)
            a0 = a0 + _colsum8(dconv * ca2)
            e = jnp.concatenate([dconv, head], axis=0)
            dca = w2 * dconv + w1 * _shift_up(e, 1, R) + w0 * _shift_up(e, 2, R)
            du_ref[pl.ds(r0, R), :] = (dca * ce[HIST:]).astype(BF16)
            dc_ref[pl.ds(r0, R), :] = (dca * ue[HIST:]).astype(BF16)
            return dconv[0:SUBLANES], a0, a1, a2

        z = jnp.zeros((SUBLANES, LANES), F32)
        _, a0, a1, a2 = lax.fori_loop(0, nchunk, chunk, (z, z, z, z))
        dw_ref[pl.ds(0, 1), :] = jnp.sum(a0, axis=0, keepdims=True)
        dw_ref[pl.ds(1, 1), :] = jnp.sum(a1, axis=0, keepdims=True)
        dw_ref[pl.ds(2, 1), :] = jnp.sum(a2, axis=0, keepdims=True)

    def col(off):
        return pl.BlockSpec((T, LANES), lambda j: (0, j + off))

    sec = jax.ShapeDtypeStruct((T, nblk * LANES), BF16)
    return pl.pallas_call(
        body, name="conv_bwd", grid=(nblk,),
        in_specs=[col(0), col(4), col(8), col(12), col(0), pl.BlockSpec((None, None, 3, LANES), lambda j: (j, l, 0, 0))],
        out_specs=[col(0), col(0), col(0), col(0), pl.BlockSpec((None, 3, LANES), lambda j: (j, 0, 0))],
        out_shape=[sec, sec, sec, sec, jax.ShapeDtypeStruct((nblk, 3, LANES), F32)],
        compiler_params=_params(),
    )(proj, proj, proj, proj, dya, wconv)


def pool_bwd(proj, dyp, wpool, pscale3, l, after):
    T = proj.shape[0]
    R = R_POOL
    ngrp = len(POOL_WINDOWS)
    nchunk = T // R

    def body(p_ref, g_ref, dy_ref, w_ref, s_ref, after_ref, du_ref, dg_ref, dw_ref, ds_ref,
             pooled_s, mixed_s, dmix_s, dpool_s):
        grp = pl.program_id(0)

        def group(w):
            wb = w_ref[...].astype(BF16)
            _pooled_loop(p_ref, pooled_s, w, T)
            mixed_s[...] = jnp.dot(pooled_s[...], wb, preferred_element_type=F32)
            sc = s_ref[...]

            def gate_chunk(i, acc):
                r0 = pl.multiple_of(i * R, R)
                g = g_ref[pl.ds(r0, R), :].astype(F32)
                dy = dy_ref[pl.ds(r0, R), :].astype(F32)
                mixed = mixed_s[pl.ds(r0, R), :]
                sg = _sigmoid(g)
                dg_ref[pl.ds(r0, R), :] = (dy * mixed * sc * (sg * (1.0 + g * (1.0 - sg)))).astype(BF16)
                dms = dy * (g * sg)
                dmix_s[pl.ds(r0, R), :] = (dms * sc).astype(BF16)
                return acc + _colsum8(dms * mixed)

            acc = lax.fori_loop(0, nchunk, gate_chunk, jnp.zeros((SUBLANES, LANES), F32))
            ds_ref[...] = jnp.sum(acc, axis=0, keepdims=True)
            dpool_s[pl.ds(0, T), :] = lax.dot_general(dmix_s[...], wb, NT, preferred_element_type=F32)
            dpool_s[pl.ds(T, HIST), :] = jnp.zeros((HIST, LANES), F32)
            dw_ref[...] = lax.dot_general(pooled_s[...], dmix_s[...], TN, preferred_element_type=F32).astype(BF16)

            def back_chunk(i, carry):
                r0 = pl.multiple_of(i * R, R)
                dpe = dpool_s[pl.ds(r0, R + HIST), :]
                e = dpe / _count(r0, R + HIST, w)
                du_ref[pl.ds(r0, R), :] = (_anticausal_window_sum(e, w)[0:R] - dpe[0:R]).astype(BF16)
                return carry

            lax.fori_loop(0, nchunk, back_chunk, 0)

        for k, w in enumerate(POOL_WINDOWS):
            pl.when(grp == k)(functools.partial(group, w))

    def col(off):
        return pl.BlockSpec((T, LANES), lambda j: (0, j + off))

    sec = jax.ShapeDtypeStruct((T, ngrp * LANES), BF16)
    wspec = pl.BlockSpec((None, LANES, LANES), lambda j: (j, 0, 0))
    sspec = pl.BlockSpec((1, LANES), lambda j: (0, j))
    return pl.pallas_call(
        body, name="pool_bwd", grid=(ngrp,),
        in_specs=[col(16), col(20), col(0), _pool_w_spec(l), _pool_s_spec(l), ANY],
        out_specs=[col(0), col(0), wspec, sspec],
        out_shape=[sec, sec, jax.ShapeDtypeStruct((ngrp, LANES, LANES), BF16),
                   jax.ShapeDtypeStruct((1, ngrp * LANES), F32)],
        scratch_shapes=[pltpu.VMEM((T, LANES), BF16), pltpu.VMEM((T, LANES), F32),
                        pltpu.VMEM((T, LANES), BF16), pltpu.VMEM((T + HIST, LANES), F32)],
        compiler_params=_params(),
    )(proj, proj, dyp, wpool, pscale3, after)


def in_bwd(dsecs, wg, x, dxo, mod4, g_pre3, l):
    T, D = x.shape
    NB = N_CHIPS
    CW = wg.shape[1] // NB
    SW = dsecs[0].shape[1]
    nsec = len(dsecs)
    PW = 256
    assert SW % PW == 0 and CW % PW == 0
    tm = 256
    nt = T // tm

    def body(*refs):
        d_refs = refs[0:nsec]
        w_ref, x_ref, dxo_ref, sh_ref, sc_ref, g_ref = refs[nsec:nsec + 6]
        dxi_ref, dw_ref, dsh_ref, dsc_ref, dg_ref = refs[nsec + 6:nsec + 11]
        acc_w, acc_sh, acc_sc, acc_g = refs[nsec + 11:]
        i = pl.program_id(0)

        @pl.when(i == 0)
        def _():
            acc_w[...] = jnp.zeros_like(acc_w)
            acc_sh[...] = jnp.zeros_like(acc_sh)
            acc_sc[...] = jnp.zeros_like(acc_sc)
            acc_g[...] = jnp.zeros_like(acc_g)

        xv = x_ref[...]
        g = g_ref[...]
        r = _rms(xv)
        xh = xv * r
        n = xh * g
        sc1 = 1.0 + sc_ref[...]
        hb = (n * sc1 + sh_ref[...]).astype(BF16)
        dh = lax.dot_general(d_refs[0][...], w_ref[:, 0:SW], NT, preferred_element_type=F32)
        for s in range(1, nsec):
            dh = dh + lax.dot_general(d_refs[s][...], w_ref[:, s * SW:(s + 1) * SW], NT, preferred_element_type=F32)
        for p in range(nsec * SW // PW):
            col = p * PW
            s, so = col // SW, col % SW
            j, jo = col // CW, col % CW
            acc_w[j, :, jo:jo + PW] += lax.dot_general(hb, d_refs[s][:, so:so + PW], TN, preferred_element_type=F32)
        acc_sh[...] += _colsum8(dh)
        acc_sc[...] += _colsum8(dh * n)
        dnp = dh * sc1
        acc_g[...] += _colsum8(dnp * xh)
        a = dnp * g
        dxi_ref[...] = dxo_ref[...] + r * (a - xh * jnp.mean(a * xh, axis=-1, keepdims=True))

        @pl.when(i == nt - 1)
        def _():
            dw_ref[...] = acc_w[...].astype(BF16)
            dsh_ref[...] = jnp.sum(acc_sh[...], axis=0, keepdims=True)
            dsc_ref[...] = jnp.sum(acc_sc[...], axis=0, keepdims=True)
            dg_ref[...] = jnp.sum(acc_g[...], axis=0, keepdims=True)

    row = pl.BlockSpec((1, D), lambda i: (0, 0))
    tile = pl.BlockSpec((tm, D), lambda i: (i, 0))
    sect = pl.BlockSpec((tm, SW), lambda i: (i, 0))
    rowshape = jax.ShapeDtypeStruct((1, D), F32)
    return pl.pallas_call(
        body, name="in_bwd", grid=(nt,),
        in_specs=[sect] * nsec + [pl.BlockSpec((D, NB * CW), lambda i: (0, 0)), tile, tile,
                                  _mod_row(l, 0, D), _mod_row(l, 1, D), _layer_row(l, D)],
        out_specs=[tile, pl.BlockSpec((NB, D, CW), lambda i: (0, 0, 0)), row, row, row],
        out_shape=[jax.ShapeDtypeStruct((T, D), F32), jax.ShapeDtypeStruct((NB, D, CW), BF16),
                   rowshape, rowshape, rowshape],
        scratch_shapes=[pltpu.VMEM((NB, D, CW), F32),
                        pltpu.VMEM((SUBLANES, D), F32), pltpu.VMEM((SUBLANES, D), F32), pltpu.VMEM((SUBLANES, D), F32)],
        compiler_params=_params(VMEM_BIG),
    )(*dsecs, wg, x, dxo, mod4, mod4, g_pre3)


def _rcopy(src, dst, ssem, rsem, dev):
    return pltpu.make_async_remote_copy(src_ref=src, dst_ref=dst, send_sem=ssem, recv_sem=rsem,
                                        device_id=dev, device_id_type=MESH)


def _peers7(x, y, c):
    out = []
    for m in range(1, N_DEV):
        bx, by, bc = (m >> 2) & 1, (m >> 1) & 1, m & 1
        out.append(((1 - x) if bx else x, (1 - y) if by else y, (1 - c) if bc else c))
    return out


HBM = pl.BlockSpec(memory_space=pltpu.HBM)
SEM = pl.BlockSpec(memory_space=pltpu.SEMAPHORE)
SPLIT = pltpu.CompilerParams(has_side_effects=pltpu.SideEffectType.DATAFLOW_SIDE_EFFECTING)


def _hbm(a):
    return pltpu.with_memory_space_constraint(a, pltpu.HBM)


def _chips(x, y):
    return [(1 - x, y), (x, 1 - y), (1 - x, 1 - y)]


SIBLING_BARRIER_ID = 0


def xchg_start(name, bufs, n_copies, plan, sibling_only=False):
    n = len(bufs)

    def body(*refs):
        ssem, rsem, token = refs[n], refs[n + 1], refs[-1]
        x, y, c = _me()
        if sibling_only:
            barrier = pltpu.get_barrier_semaphore()
            pl.semaphore_signal(barrier, inc=1, device_id=(x, y, 1 - c), device_id_type=MESH)
            pl.semaphore_wait(barrier, 1)
        copies = plan(refs[0:n], x, y, c)
        assert len(copies) == n_copies
        for k, (src, dst, peer, _) in enumerate(copies):
            _rcopy(src, dst, ssem.at[k], rsem.at[k], peer).start()
        token[...] = jnp.zeros_like(token)

    params = dict(has_side_effects=pltpu.SideEffectType.DATAFLOW_SIDE_EFFECTING)
    if sibling_only:
        params["collective_id"] = SIBLING_BARRIER_ID
    outs = pl.pallas_call(
        body, name=name,
        in_specs=[HBM] * n,
        out_specs=[SEM, SEM] + [HBM] * n + [pl.BlockSpec(memory_space=pltpu.VMEM)],
        out_shape=([pltpu.SemaphoreType.DMA((n_copies,))] * 2 + [pltpu.HBM(b.shape, b.dtype) for b in bufs]
                   + [jax.ShapeDtypeStruct((SUBLANES, LANES), F32)]),
        input_output_aliases={a: 2 + a for a in range(n)},
        compiler_params=pltpu.CompilerParams(**params),
    )(*[_hbm(b) for b in bufs])
    return outs[0], outs[1], list(outs[2:2 + n]), outs[-1]


def xchg_wait(name, bufs, ssem, rsem, n_copies, plan, after):
    n = len(bufs)
    after = list(after)

    def body(*refs):
        ssem_ref, rsem_ref = refs[n], refs[n + 1]
        copies = plan(refs[0:n], *_me())
        assert len(copies) == n_copies
        for k, (src, _, peer, land) in enumerate(copies):
            cp = _rcopy(src, land, ssem_ref.at[k], rsem_ref.at[k], peer)
            cp.wait_send()
            cp.wait_recv()

    outs = pl.pallas_call(
        body, name=name,
        in_specs=[HBM] * n + [SEM, SEM] + [ANY] * len(after), out_specs=[HBM] * n,
        out_shape=[pltpu.HBM(b.shape, b.dtype) for b in bufs],
        input_output_aliases={a: a for a in range(n)},
        compiler_params=SPLIT,
    )(*bufs, ssem, rsem, *after)
    return list(outs)


def _shard_half(buf, chip, half):
    if len(buf.shape) == 2:
        h, w = buf.shape[0] // 2, buf.shape[1] // N_CHIPS
        return buf.at[pl.ds(half * h, h), pl.ds(chip * w, w)]
    h = buf.shape[1] // 2
    return buf.at[chip, pl.ds(half * h, h)]


def plan_gather(refs, x, y, c):
    out = []
    for (px, py) in _chips(x, y):
        for buf in refs:
            own = _shard_half(buf, 2 * x + y, c)
            out.append((own, own, (px, py, c), _shard_half(buf, 2 * px + py, c)))
    return out


def plan_forward(refs, x, y, c):
    out = []
    for (px, py) in _chips(x, y):
        for buf in refs:
            landed = _shard_half(buf, 2 * px + py, c)
            out.append((landed, landed, (x, y, 1 - c), _shard_half(buf, 2 * px + py, 1 - c)))
    return out


def plan_sibling(refs, x, y, c):
    n = len(refs) // 2
    out = []
    for a in range(n):
        h = refs[a].shape[1] // 2
        out.append((refs[a].at[:, pl.ds((1 - c) * h, h)], refs[n + a], (x, y, 1 - c), refs[n + a]))
    return out


def plan_chip(refs, x, y, c):
    n = len(refs) // 2
    out = []
    for j, (px, py) in enumerate(_chips(x, y)):
        for a in range(n):
            out.append((refs[a].at[2 * px + py], refs[n + a].at[j], (px, py, c), refs[n + a].at[j]))
    return out


def plan_mod(refs, x, y, c):
    (mods,) = refs
    mine = mods.at[2 * x + y]
    return [(mine, mine, (px, py, c), mods.at[2 * px + py]) for (px, py) in _chips(x, y)]


def plan_pack(refs, x, y, c):
    (packs,) = refs
    mine = packs.at[4 * x + 2 * y + c]
    return [(mine, mine, peer, packs.at[4 * peer[0] + 2 * peer[1] + peer[2]]) for peer in _peers7(x, y, c)]


def plan_spread(layers, wp_layers):
    def plan(refs, x, y, c):
        gi, go, gp = refs
        hD, hR, hP = gi.shape[1] // 2, go.shape[1] // 2, gp.shape[2] // 2
        sib = (x, y, 1 - c)
        out = []
        for l in layers:
            mine = gi.at[l, pl.ds(c * hD, hD)]
            out.append((mine, mine, sib, gi.at[l, pl.ds((1 - c) * hD, hD)]))
            mine = go.at[l, pl.ds(c * hR, hR)]
            out.append((mine, mine, sib, go.at[l, pl.ds((1 - c) * hR, hR)]))
        for l in wp_layers:
            mine = gp.at[l, 2 * x + y, pl.ds(c * hP, hP)]
            for peer in _peers7(x, y, c):
                out.append((mine, mine, peer, gp.at[l, 2 * peer[0] + peer[1], pl.ds(peer[2] * hP, hP)]))
        return out

    return plan


def gather_small(c8, wc, token):
    def body(c_ref, wc_ref, token_ref, call, wcall, ssem, rsem, lsem):
        x, y, c = _me()
        myc = 2 * x + y
        me_lin = 4 * x + 2 * y + c
        me = (x, y, c)
        local = [pltpu.make_async_copy(c_ref, call.at[me_lin], lsem.at[0]),
                 pltpu.make_async_copy(wc_ref, wcall.at[myc], lsem.at[1])]
        for cp in local:
            cp.start()
        sends, recvs = [], []
        for m, peer in enumerate(_peers7(x, y, c)):
            plin = 4 * peer[0] + 2 * peer[1] + peer[2]
            sends.append(_rcopy(c_ref, call.at[me_lin], ssem.at[m], rsem.at[m], peer))
            recvs.append(_rcopy(call.at[plin], call.at[plin], ssem.at[m], rsem.at[m], me))
        for j, (px, py) in enumerate([(1 - x, y), (x, 1 - y), (1 - x, 1 - y)]):
            pc = 2 * px + py
            sends.append(_rcopy(wc_ref, wcall.at[myc], ssem.at[7 + j], rsem.at[7 + j], (px, py, c)))
            recvs.append(_rcopy(wcall.at[pc], wcall.at[pc], ssem.at[7 + j], rsem.at[7 + j], me))
        for cp in sends:
            cp.start()
        for cp in recvs:
            cp.wait_recv()
        for cp in sends:
            cp.wait_send()
        for cp in local:
            cp.wait()

    return pl.pallas_call(
        body, name="gather_small",
        in_specs=[ANY] * 3, out_specs=[ANY] * 2,
        out_shape=[jax.ShapeDtypeStruct((N_DEV, SUBLANES, LANES), F32),
                   jax.ShapeDtypeStruct((N_CHIPS, wc.shape[0], 3, LANES), F32)],
        scratch_shapes=[pltpu.SemaphoreType.DMA((10,)), pltpu.SemaphoreType.DMA((10,)), pltpu.SemaphoreType.DMA((2,))],
        compiler_params=_params(n_grid=0),
    )(c8, wc, token)


def spread_now(gi, go, gp, layers, wp_layers):
    plan = plan_spread(layers, wp_layers)
    n = 2 * len(layers) + 7 * len(wp_layers)

    def body(gi_in, go_in, gp_in, gi, go, gp, ssem, rsem):
        copies = plan((gi, go, gp), *_me())
        me = _me()
        sends = [_rcopy(src, dst, ssem.at[k], rsem.at[k], peer) for k, (src, dst, peer, _) in enumerate(copies)]
        for cp in sends:
            cp.start()
        for k, (_, _, _, land) in enumerate(copies):
            _rcopy(land, land, ssem.at[k], rsem.at[k], me).wait_recv()
        for cp in sends:
            cp.wait_send()

    return pl.pallas_call(
        body, name="spread_now",
        in_specs=[ANY] * 3, out_specs=[ANY] * 3,
        out_shape=[jax.ShapeDtypeStruct(a.shape, a.dtype) for a in (gi, go, gp)],
        input_output_aliases={0: 0, 1: 1, 2: 2},
        scratch_shapes=[pltpu.SemaphoreType.DMA((n,)), pltpu.SemaphoreType.DMA((n,))],
        compiler_params=_params(n_grid=0),
    )(gi, go, gp)


def add_sibling(cidx, mine, sib):
    def body(c_ref, *refs):
        for a in range(3):
            m, s, o = refs[a], refs[3 + a], refs[6 + a]
            o[...] = (m[...].astype(F32) + s[...].astype(F32)).astype(BF16)

    def mine_spec(a):
        h = a.shape[1] // 2
        return pl.BlockSpec((None, h, a.shape[2]), lambda j, c_ref: (j, c_ref[0], 0))

    def sib_spec(a):
        return pl.BlockSpec((None,) + a.shape[1:], lambda j, c_ref: (j, 0, 0))

    return pl.pallas_call(
        body, name="add_sibling",
        grid_spec=pltpu.PrefetchScalarGridSpec(
            num_scalar_prefetch=1, grid=(N_CHIPS,),
            in_specs=[mine_spec(a) for a in mine] + [sib_spec(a) for a in sib],
            out_specs=[sib_spec(a) for a in sib]),
        out_shape=[jax.ShapeDtypeStruct(a.shape, BF16) for a in sib],
        compiler_params=_params(VMEM_BIG),
    )(cidx, *mine, *sib)


def sum_chips(pos, own, rb, acc, l, shapes):
    nq = 4
    n_in = 6 + (3 if acc is not None else 0)

    def body(pos_ref, *refs):
        for a in range(3):
            m, b, o = refs[a], refs[3 + a], refs[n_in + a]
            s = m[...].astype(F32)
            for j in range(3):
                s = s + b[j].astype(F32)
            o[...] = s

    def own_spec(a):
        return pl.BlockSpec((None, a.shape[1] // nq, a.shape[2]), lambda q, p: (p[1], q, 0))

    def rb_spec(a):
        return pl.BlockSpec((3, a.shape[1] // nq, a.shape[2]), lambda q, p: (0, q, 0))

    hi, ho, hp = own[0].shape[1] // nq, own[1].shape[1] // nq, own[2].shape[1] // nq
    out_specs = [pl.BlockSpec((None, hi, shapes[0][2]), lambda q, p: (l, p[0] * nq + q, 0)),
                 pl.BlockSpec((None, ho, shapes[1][2]), lambda q, p: (l, p[0] * nq + q, 0)),
                 pl.BlockSpec((None, None, hp, LANES), lambda q, p: (l, p[1], p[0] * nq + q, 0))]
    in_specs = [own_spec(a) for a in own] + [rb_spec(a) for a in rb]
    args = list(own) + list(rb)
    aliases = {}
    if acc is not None:
        in_specs += [ANY] * 3
        args += list(acc)
        aliases = {7: 0, 8: 1, 9: 2}
    return pl.pallas_call(
        body, name="sum_chips",
        grid_spec=pltpu.PrefetchScalarGridSpec(num_scalar_prefetch=1, grid=(nq,), in_specs=in_specs, out_specs=out_specs),
        out_shape=[jax.ShapeDtypeStruct(s, F32) for s in shapes],
        input_output_aliases=aliases,
        compiler_params=_params(VMEM_BIG),
    )(pos, *args)


def pack_small(pos, per_layer, loss_blk):
    L = len(per_layer)
    D = per_layer[0][0].shape[1]

    def body(pos_ref, *refs):
        o = refs[-1]
        lb = refs[-2]
        o[...] = jnp.zeros_like(o)
        for l in range(L):
            dgpre, dgpost, dsh, dsc, dgt, dps, dwc = refs[7 * l:7 * l + 7]
            base = SUBLANES * l
            for r, src in enumerate((dgpre, dgpost, dsh, dsc, dgt)):
                o[pl.ds(base + r, 1), :] = src[...]
            o[pl.ds(base + 5, 1), 0:dps.shape[1]] = dps[...]
            for j in range(dwc.shape[0]):
                for k in range(3):
                    idx = 3 * j + k
                    o[pl.ds(base + 6 + idx // 8, 1), (idx % 8) * LANES:(idx % 8 + 1) * LANES] = dwc[j, pl.ds(k, 1), :]
        o[pl.ds(5, 1), 4 * LANES:5 * LANES] = lb[pl.ds(0, 1), :]

    flat = [a for layer in per_layer for a in layer] + [loss_blk]

    def whole(a):
        return pl.BlockSpec(a.shape, lambda i, p: (0,) * a.ndim)

    return pl.pallas_call(
        body, name="pack_small",
        grid_spec=pltpu.PrefetchScalarGridSpec(
            num_scalar_prefetch=1, grid=(1,), in_specs=[whole(a) for a in flat],
            out_specs=pl.BlockSpec((None, L * SUBLANES, D), lambda i, p: (p[2], 0, 0))),
        out_shape=jax.ShapeDtypeStruct((N_DEV, L * SUBLANES, D), F32),
        compiler_params=_params(),
    )(pos, *flat)


def small_update(pos, packs, params, moments_m, moments_v):
    n = len(params)
    L, D = params[1].shape
    PS = params[3].shape[1]

    def body(pos_ref, p_ref, *refs):
        ws, ms, vs = refs[0:n], refs[n:2 * n], refs[2 * n:3 * n]
        loss_ref = refs[3 * n]
        outs = [refs[3 * n + 1 + 4 * t:3 * n + 5 + 4 * t] for t in range(n)]
        summed = refs[-1]
        s = p_ref[0]
        for d in range(1, N_DEV):
            s = s + p_ref[d]
        summed[...] = s
        loss_ref[...] = summed[pl.ds(5, 1), 4 * LANES:5 * LANES]
        chip = pos_ref[1]

        def update(t, idx, g):
            d, mm, vv = _adamw_math(ws[t][idx], g, ms[t][idx], vs[t][idx])
            g_ref, d_ref, mo_ref, vo_ref = outs[t]
            g_ref[idx] = g
            d_ref[idx] = d
            mo_ref[idx] = mm
            vo_ref[idx] = vv

        for l in range(L):
            base = SUBLANES * l
            row = pl.ds(l, 1)
            for k in range(3):
                update(0, (row, slice(k * D, (k + 1) * D)), summed[pl.ds(base + 2 + k, 1), :])
            update(1, (row, slice(None)), summed[pl.ds(base, 1), :])
            update(2, (row, slice(None)), summed[pl.ds(base + 1, 1), :])
            update(3, (row, slice(None)), summed[pl.ds(base + 5, 1), 0:PS])
            for k in range(3):
                g = None
                for j in range(N_CHIPS):
                    idx = 3 * j + k
                    cand = summed[pl.ds(base + 6 + idx // 8, 1), (idx % 8) * LANES:(idx % 8 + 1) * LANES]
                    g = cand if g is None else jnp.where(chip == j, cand, g)
                update(4, (l, pl.ds(k, 1), slice(None)), g)

    def whole(a):
        return pl.BlockSpec(a.shape, lambda i, p: (0,) * a.ndim)

    ins = [packs] + list(params) + list(moments_m) + list(moments_v)
    out_shape = [jax.ShapeDtypeStruct((1, LANES), F32)]
    for w in params:
        out_shape += [jax.ShapeDtypeStruct(w.shape, F32)] * 4
    outs = pl.pallas_call(
        body, name="small_update",
        grid_spec=pltpu.PrefetchScalarGridSpec(
            num_scalar_prefetch=1, grid=(1,), in_specs=[whole(a) for a in ins],
            out_specs=[whole(a) for a in out_shape],
            scratch_shapes=[pltpu.VMEM(packs.shape[1:], F32)]),
        out_shape=out_shape,
        compiler_params=_params(),
    )(pos, *ins)
    return outs[0], [outs[1 + 4 * t:5 + 4 * t] for t in range(n)]


def _adamw_math(w, g, m, v):
    m = ADAM_B1 * m + (1.0 - ADAM_B1) * g
    v = ADAM_B2 * v + (1.0 - ADAM_B2) * (g * g)
    m_hat = m / (1.0 - ADAM_B1 ** ADAM_STEP)
    v_hat = v / (1.0 - ADAM_B2 ** ADAM_STEP)
    delta = -ADAM_LR * (m_hat / (jnp.sqrt(v_hat) + ADAM_EPS) + ADAM_WD * w)
    return delta, m, v


def adamw(w, g, m, v, block, name, first=0, count=None, acc=None):
    grid = tuple(s // b for s, b in zip(w.shape, block))
    if count is not None:
        grid = (count,) + grid[1:]

    def body(w_ref, g_ref, m_ref, v_ref, *rest):
        go_ref, d_ref, mo_ref, vo_ref = rest[-4:]
        gv = g_ref[...]
        d, mm, vv = _adamw_math(w_ref[...], gv, m_ref[...], v_ref[...])
        go_ref[...] = gv
        d_ref[...] = d
        mo_ref[...] = mm
        vo_ref[...] = vv

    spec = pl.BlockSpec(block, lambda i, *rest: (first + i,) + rest)
    shape = jax.ShapeDtypeStruct(w.shape, F32)
    extra = [] if acc is None else list(acc)
    return pl.pallas_call(
        body, name=name, grid=grid,
        in_specs=[spec] * 4 + [ANY] * len(extra), out_specs=[spec] * 4, out_shape=[shape] * 4,
        input_output_aliases={4 + a: a for a in range(len(extra))},
        compiler_params=_params(VMEM_BIG, n_grid=len(grid)),
    )(w, g, m, v, *extra)


def ada_finish(c_all, dmod, w, m, v):
    L, D, CW = w.shape
    hD = D // 2

    def body(c_ref, d_ref, w_ref, m_ref, v_ref, g_ref, dl_ref, mo_ref, vo_ref):
        cv = c_ref[...]
        z = jnp.zeros_like(cv)
        ca = jnp.concatenate([cv * jax.nn.sigmoid(cv), z], axis=0).astype(BF16)
        dm = jnp.concatenate([d_ref[0], jnp.zeros_like(d_ref[0])], axis=0).astype(BF16)
        g = lax.dot_general(ca, dm, TN, preferred_element_type=F32)
        g_ref[0] = g
        d, mm, vv = _adamw_math(w_ref[0], g, m_ref[0], v_ref[0])
        dl_ref[0] = d
        mo_ref[0] = mm
        vo_ref[0] = vv

    big = pl.BlockSpec((1, hD, CW), lambda l, h: (l, h, 0))
    shape = jax.ShapeDtypeStruct(w.shape, F32)
    return pl.pallas_call(
        body, name="ada_finish", grid=(L, 2),
        in_specs=[pl.BlockSpec((N_DEV, hD), lambda l, h: (0, h)), pl.BlockSpec((1, N_DEV, CW), lambda l, h: (l, 0, 0)),
                  big, big, big],
        out_specs=[big] * 4, out_shape=[shape] * 4,
        compiler_params=_params(VMEM_BIG, n_grid=2),
    )(c_all, dmod, w, m, v)


def kernel(x, c, w_ada, b_ada, g_pre, w_in, w_conv, w_pool, pool_scale, w_out, g_post, loss_target, m_w_ada, m_b_ada, m_g_pre, m_w_in, m_w_conv, m_w_pool, m_pool_scale, m_w_out, m_g_post, v_w_ada, v_b_ada, v_g_pre, v_w_in, v_w_conv, v_w_pool, v_pool_scale, v_w_out, v_g_post):
    L, D, CW = w_in.shape
    RO = w_out.shape[1]
    T = x.shape[1]
    ix, iy, ic = _me()
    chip = 2 * ix + iy
    me_lin = 4 * ix + 2 * iy + ic

    pos = jnp.stack([ic, chip, me_lin]).astype(jnp.int32)
    g_pre3, g_post3 = g_pre.reshape(L, 1, D), g_post.reshape(L, 1, D)
    pscale3 = pool_scale.reshape(L, 1, pool_scale.shape[1])
    n_g, n_s, n_c = 6, 3, 9

    c_all3, wconv_all = gather_small(c.reshape(SUBLANES, LANES), w_conv, pos)
    c_all = c_all3.reshape(N_DEV, D)
    gath = [None] * L
    ss, rs, bufs, token = xchg_start("gather_start", list(cast_weights(pos, w_in, w_out, 0, c_all3)), n_g, plan_gather)
    gath[0] = (ss, rs, bufs)
    b_my = lax.dynamic_slice_in_dim(b_ada, chip * CW, CW, axis=1)
    m_ss, m_rs, mods, token = xchg_start("mod_start", [mod_part(pos, c_all, w_ada, b_my, token)], 3, plan_mod)
    for l in range(1, L):
        ss, rs, bufs, token = xchg_start("gather_start", list(cast_weights(pos, w_in, w_out, l, token)), n_g, plan_gather)
        gath[l] = (ss, rs, bufs)
    (mod_all,) = xchg_wait("mod_wait", mods, m_ss, m_rs, 3, plan_mod, [token])
    mod = lax.dynamic_index_in_dim(mod_all, me_lin, axis=2, keepdims=False)
    mod4 = jnp.transpose(mod, (1, 0, 2)).reshape(L, 3, 1, D)
    token = mod4

    def arrive(l, after):
        ss, rs, bufs = gath[l]
        bufs = xchg_wait("gather_wait", bufs, ss, rs, n_g, plan_gather, after)
        return xchg_start("forward_start", bufs, n_g, plan_forward, sibling_only=True)

    xs, projs, yas, yps, ys = [x.reshape(T, D)], [], [], [], []
    wg_in, wg_out = [], []
    fwd = arrive(0, [token])
    for l in range(L):
        fss, frs, bufs, ftoken = fwd
        gi, go = xchg_wait("forward_wait", bufs, fss, frs, n_g, plan_forward, [ftoken if l == 0 else xs[l]])
        wg_in.append(gi)
        wg_out.append(go.reshape(N_CHIPS * RO, D))
        proj = proj_fwd(xs[l], mod4, g_pre3, wg_in[l], l)
        ya = conv_fwd(proj, wconv_all, l)
        yp = pool_fwd(proj, w_pool, pscale3, l)
        projs.append(proj)
        yas.append(ya)
        yps.append(yp)
        if l + 1 < L:
            fwd = arrive(l + 1, [ya, yp])
            xn, yv = out_fwd(ya, yp, wg_out[l], xs[l], mod4, g_post3, l, fwd[3])
            xs.append(xn)
        else:
            dx, yv, loss_blk = out_fwd_loss(ya, yp, wg_out[l], xs[l], mod4, g_post3, l, loss_target.reshape(T, D))
        ys.append(yv)

    shapes = (w_in.shape, w_out.shape, w_pool.shape)
    smalls = [None] * L
    acc, flying, sib, token = None, None, None, loss_blk

    def to_chips(sib, after):
        sl, s_ss, s_rs, s_bufs = sib
        s_bufs = xchg_wait("sibling_wait", s_bufs, s_ss, s_rs, n_s, plan_sibling, after)
        chip_parts = add_sibling(pos, s_bufs[0:3], s_bufs[3:6])
        lands = [lax.empty((3,) + a.shape[1:], a.dtype) for a in chip_parts]
        c_ss, c_rs, c_bufs, ctoken = xchg_start("chip_start", list(chip_parts) + lands, n_c, plan_chip)
        return (sl, c_ss, c_rs, c_bufs), ctoken

    def landed(flying, acc, after):
        fl, f_ss, f_rs, f_bufs = flying
        f_bufs = xchg_wait("chip_wait", f_bufs, f_ss, f_rs, n_c, plan_chip, after)
        return sum_chips(pos, f_bufs[0:3], f_bufs[3:6], acc, fl, shapes)

    for l in reversed(range(L)):
        dya, dyp, dwo_l, dgate, dgpost = out_bwd(dx, ys[l], yas[l], yps[l], wg_out[l], mod4, g_post3, l, token)
        token = dya
        if sib is not None:
            arrived = flying
            flying, token = to_chips(sib, [dya])
            if arrived is not None:
                acc = landed(arrived, acc, [token])
                token = acc[0]
        du_p, dg_p, dwp_l, dps = pool_bwd(projs[l], dyp, w_pool, pscale3, l, token)
        du_a, db_a, dc_a, dg_a, dwc = conv_bwd(projs[l], dya, wconv_all, l)
        dx, dwi_l, dshift, dscale, dgpre = in_bwd([du_a, db_a, dc_a, dg_a, du_p, dg_p], wg_in[l], xs[l], dx,
                                                  mod4, g_pre3, l)
        smalls[l] = (dgpre, dgpost, dshift, dscale, dgate, dps, dwc)
        parts = [dwi_l, dwo_l.reshape(N_CHIPS, RO, D), dwp_l]
        s_lands = [lax.empty((a.shape[0], a.shape[1] // 2) + a.shape[2:], a.dtype) for a in parts]
        s_ss, s_rs, s_bufs, token = xchg_start("sibling_start", parts + s_lands, n_s, plan_sibling, sibling_only=True)
        sib = (l, s_ss, s_rs, s_bufs)
    grad_x = dx.reshape(1, T, D)

    p_ss, p_rs, packs, ptoken = xchg_start("pack_start", [pack_small(pos, smalls, loss_blk)], N_DEV - 1, plan_pack)
    acc = landed(flying, acc, [ptoken, token])
    n_sp = (2 + N_DEV - 1) * (L - 1)
    spread = plan_spread(tuple(range(1, L)), tuple(range(1, L)))
    sp_ss, sp_rs, acc, sp_token = xchg_start("spread_start", list(acc), n_sp, spread)
    flying, token = to_chips(sib, [sp_token])
    (packs_all,) = xchg_wait("pack_wait", packs, p_ss, p_rs, N_DEV - 1, plan_pack, [token])
    dmod_all = packs_all.reshape(N_DEV, L, SUBLANES, D)[:, :, 2:5].reshape(N_DEV, L, 3 * D)
    dmod_my = jnp.transpose(lax.dynamic_slice_in_dim(dmod_all, chip * CW, CW, axis=2), (1, 0, 2))

    g_w_ada, d_w_ada, nm_w_ada, nv_w_ada = ada_finish(c_all, dmod_my, w_ada, m_w_ada, v_w_ada)
    loss_row, upd = small_update(pos, packs_all, [b_ada, g_pre, g_post, pool_scale, w_conv],
                                 [m_b_ada, m_g_pre, m_g_post, m_pool_scale, m_w_conv],
                                 [v_b_ada, v_g_pre, v_g_post, v_pool_scale, v_w_conv])
    loss = loss_row[0, 0]
    (g_b_ada, d_b_ada, nm_b_ada, nv_b_ada), (g_g_pre, d_g_pre, nm_g_pre, nv_g_pre) = upd[0], upd[1]
    (g_g_post, d_g_post, nm_g_post, nv_g_post), (g_pscale, d_pscale, nm_pscale, nv_pscale) = upd[2], upd[3]
    g_w_conv, d_w_conv, nm_w_conv, nv_w_conv = upd[4]

    done = [nv_w_ada, nv_w_conv]
    g_w_in, g_w_out, g_w_pool = xchg_wait("spread_wait", acc, sp_ss, sp_rs, n_sp, spread, done)
    in_blk, out_blk = (1, D // 2, CW), (1, RO, D)
    upd_in = adamw(w_in, g_w_in, m_w_in, v_w_in, in_blk, "adamw_w_in", 1, L - 1)
    upd_out = adamw(w_out, g_w_out, m_w_out, v_w_out, out_blk, "adamw_w_out", 1, L - 1)

    acc = landed(flying, (g_w_in, g_w_out, g_w_pool), [upd_in[3], upd_out[3]])
    r_w_in, r_w_out, r_w_pool = spread_now(*acc, (0,), (0,))
    g_w_in, d_w_in, nm_w_in, nv_w_in = adamw(w_in, r_w_in, m_w_in, v_w_in, in_blk, "adamw_w_in", 0, 1, upd_in)
    g_w_out, d_w_out, nm_w_out, nv_w_out = adamw(w_out, r_w_out, m_w_out, v_w_out, out_blk, "adamw_w_out", 0, 1, upd_out)
    pshape = (L, N_CHIPS * LANES, LANES)
    upd_pool = adamw(w_pool.reshape(pshape), r_w_pool.reshape(pshape), m_w_pool.reshape(pshape),
                     v_w_pool.reshape(pshape), (1,) + pshape[1:], "adamw_w_pool")
    g_w_pool, d_w_pool, nm_w_pool, nv_w_pool = [a.reshape(w_pool.shape) for a in upd_pool]

    return (loss, grad_x,
            g_w_ada, g_b_ada, g_g_pre, g_w_in, g_w_conv, g_w_pool, g_pscale, g_w_out, g_g_post,
            d_w_ada, d_b_ada, d_g_pre, d_w_in, d_w_conv, d_w_pool, d_pscale, d_w_out, d_g_post,
            nm_w_ada, nm_b_ada, nm_g_pre, nm_w_in, nm_w_conv, nm_w_pool, nm_pscale, nm_w_out, nm_g_post,
            nv_w_ada, nv_b_ada, nv_g_pre, nv_w_in, nv_w_conv, nv_w_pool, nv_pscale, nv_w_out, nv_g_post)
```

```python
import functools

import jax
import jax.numpy as jnp
from jax import lax
from jax.experimental import pallas as pl
from jax.experimental.pallas import tpu as pltpu

F32 = jnp.float32
BF16 = jnp.bfloat16
MESH = pl.DeviceIdType.MESH
ANY = pl.BlockSpec(memory_space=pl.ANY)

NORM_EPS = 1e-6
POOL_WINDOWS = (2, 4, 8, 16)
ADAM_LR = 0.001
ADAM_B1 = 0.9
ADAM_B2 = 0.999
ADAM_EPS = 1e-08
ADAM_WD = 0.01
ADAM_STEP = 10

N_CHIPS = 4
N_DEV = 8
LANES = 128
SUBLANES = 8
VMEM_BIG = 56 * 1024 * 1024
HIST = 16
R_CONV = 64
R_POOL = 128

NT = (((1,), (1,)), ((), ()))
TN = (((0,), (0,)), ((), ()))


def _params(vmem=None, n_grid=1):
    kw = {}
    if n_grid:
        kw["dimension_semantics"] = ("arbitrary",) * n_grid
    if vmem is not None:
        kw["vmem_limit_bytes"] = vmem
    return pltpu.CompilerParams(**kw)


def _colsum8(v):
    n, d = v.shape
    return v.reshape(n // SUBLANES, SUBLANES, d).sum(axis=0)


def _rms(v):
    return lax.rsqrt(jnp.mean(v * v, axis=-1, keepdims=True) + NORM_EPS)


def _sigmoid(v):
    return 0.5 * jnp.tanh(0.5 * v) + 0.5


def _shift_down(ext, k, rows):
    if k == 0:
        return ext[HIST:HIST + rows]
    return pltpu.roll(ext, k, 0)[HIST:HIST + rows]


def _shift_up(ext, k, rows):
    if k == 0:
        return ext[0:rows]
    return pltpu.roll(ext, ext.shape[0] - k, 0)[0:rows]


def _load_ext(ref, r0, h0, first, rows):
    hist = ref[pl.ds(h0, HIST), :].astype(F32)
    hist = jnp.where(first, 0.0, hist)
    cur = ref[pl.ds(r0, rows), :].astype(F32)
    return jnp.concatenate([hist, cur], axis=0)


def _me():
    return lax.axis_index("x"), lax.axis_index("y"), lax.axis_index("c")


def cast_weights(pos, w_in, w_out, l, after):
    _, D, CW = w_in.shape
    RO = w_out.shape[1]

    def body(pos_ref, wi, wo, after_ref, oi, oo):
        oi[...] = wi[...].astype(BF16)
        oo[...] = wo[...].astype(BF16)

    return pl.pallas_call(
        body, name="cast_w",
        grid_spec=pltpu.PrefetchScalarGridSpec(
            num_scalar_prefetch=1, grid=(2,),
            in_specs=[pl.BlockSpec((None, D // 2, CW), lambda h, p: (l, h, 0)),
                      pl.BlockSpec((None, RO // 2, D), lambda h, p: (l, h, 0)), ANY],
            out_specs=[pl.BlockSpec((D // 2, CW), lambda h, p: (h, p[1])),
                       pl.BlockSpec((None, RO // 2, D), lambda h, p: (p[1], h, 0))]),
        out_shape=[jax.ShapeDtypeStruct((D, N_CHIPS * CW), BF16), jax.ShapeDtypeStruct((N_CHIPS, RO, D), BF16)],
        compiler_params=_params(),
    )(pos, w_in, w_out, after)


def mod_part(pos, c_all, w_ada, b_my, after):
    L, D, CW = w_ada.shape

    def body(pos_ref, c_ref, w_ref, b_ref, after_ref, o_ref):
        cv = c_ref[...]
        ca = (cv * jax.nn.sigmoid(cv)).astype(BF16)
        o_ref[...] = jnp.dot(ca, w_ref[0].astype(BF16), preferred_element_type=F32) + b_ref[0]

    return pl.pallas_call(
        body, name="mod_part",
        grid_spec=pltpu.PrefetchScalarGridSpec(
            num_scalar_prefetch=1, grid=(L,),
            in_specs=[pl.BlockSpec((N_DEV, D), lambda l, p: (0, 0)),
                      pl.BlockSpec((1, D, CW), lambda l, p: (l, 0, 0)),
                      pl.BlockSpec((1, 1, CW), lambda l, p: (l, 0, 0)), ANY],
            out_specs=pl.BlockSpec((None, None, N_DEV, CW), lambda l, p: (p[1], l, 0, 0))),
        out_shape=jax.ShapeDtypeStruct((N_CHIPS, L, N_DEV, CW), F32),
        compiler_params=_params(VMEM_BIG),
    )(pos, c_all, w_ada, b_my.reshape(L, 1, CW), after)


def _mod_row(l, k, D):
    return pl.BlockSpec((None, None, 1, D), lambda *_: (l, k, 0, 0))


def _layer_row(l, D):
    return pl.BlockSpec((None, 1, D), lambda *_: (l, 0, 0))


def proj_fwd(x, mod4, g_pre3, wg, l):
    T, D = x.shape
    NC = wg.shape[1]
    NB = N_CHIPS
    CW = NC // NB
    tm = 512

    def body(x_ref, sh_ref, sc_ref, g_ref, w_ref, o_ref):
        xv = x_ref[...]
        h = (xv * _rms(xv) * g_ref[...]) * (1.0 + sc_ref[...]) + sh_ref[...]
        hb = h.astype(BF16)
        for j in range(NB):
            cols = slice(j * CW, (j + 1) * CW)
            o_ref[:, cols] = jnp.dot(hb, w_ref[:, cols], preferred_element_type=F32).astype(BF16)

    return pl.pallas_call(
        body, name="proj_fwd", grid=(T // tm,),
        in_specs=[pl.BlockSpec((tm, D), lambda i: (i, 0)), _mod_row(l, 0, D), _mod_row(l, 1, D), _layer_row(l, D),
                  pl.BlockSpec((D, NC), lambda i: (0, 0))],
        out_specs=pl.BlockSpec((tm, NC), lambda i: (i, 0)),
        out_shape=jax.ShapeDtypeStruct((T, NC), BF16),
        compiler_params=_params(VMEM_BIG),
    )(x, mod4, mod4, g_pre3, wg)


def conv_fwd(proj, wconv, l):
    T = proj.shape[0]
    R = R_CONV
    nblk = 4

    def body(u_ref, b_ref, c_ref, g_ref, w_ref, o_ref):
        w0 = w_ref[pl.ds(0, 1), :]
        w1 = w_ref[pl.ds(1, 1), :]
        w2 = w_ref[pl.ds(2, 1), :]

        def chunk(i, carry):
            r0 = pl.multiple_of(i * R, R)
            h0 = pl.multiple_of(jnp.maximum(r0 - HIST, 0), HIST)
            first = i == 0
            ca = _load_ext(c_ref, r0, h0, first, R) * _load_ext(u_ref, r0, h0, first, R)
            conv = w2 * ca[HIST:] + w1 * _shift_down(ca, 1, R) + w0 * _shift_down(ca, 2, R)
            g = g_ref[pl.ds(r0, R), :].astype(F32)
            b = b_ref[pl.ds(r0, R), :].astype(F32)
            o_ref[pl.ds(r0, R), :] = (b * conv * (g * _sigmoid(g))).astype(BF16)
            return carry

        lax.fori_loop(0, T // R, chunk, 0)

    def col(off):
        return pl.BlockSpec((T, LANES), lambda j: (0, j + off))

    return pl.pallas_call(
        body, name="conv_fwd", grid=(nblk,),
        in_specs=[col(0), col(4), col(8), col(12), pl.BlockSpec((None, None, 3, LANES), lambda j: (j, l, 0, 0))],
        out_specs=pl.BlockSpec((T, LANES), lambda j: (0, j)),
        out_shape=jax.ShapeDtypeStruct((T, nblk * LANES), BF16),
        compiler_params=_params(),
    )(proj, proj, proj, proj, wconv)


def _causal_window_sum(ext, w):
    s, k = ext, 1
    while k < w:
        s = s + pltpu.roll(s, k, 0)
        k *= 2
    return s


def _anticausal_window_sum(ext, w):
    s, k = ext, 1
    n = ext.shape[0]
    while k < w:
        s = s + pltpu.roll(s, n - k, 0)
        k *= 2
    return s


def _count(r0, rows, w):
    t = r0 + lax.broadcasted_iota(jnp.int32, (rows, LANES), 0)
    return jnp.minimum(t + 1, w).astype(F32)


def _pooled_loop(p_ref, pooled_s, w, T):
    R = R_POOL

    def chunk(i, carry):
        r0 = pl.multiple_of(i * R, R)
        h0 = pl.multiple_of(jnp.maximum(r0 - HIST, 0), HIST)
        ext = _load_ext(p_ref, r0, h0, i == 0, R)
        ws = _causal_window_sum(ext, w)[HIST:]
        pooled_s[pl.ds(r0, R), :] = (ws / _count(r0, R, w) - ext[HIST:]).astype(BF16)
        return carry

    lax.fori_loop(0, T // R, chunk, 0)


def _pool_w_spec(l):
    return pl.BlockSpec((None, None, LANES, LANES), lambda j: (l, j, 0, 0))


def _pool_s_spec(l):
    return pl.BlockSpec((None, 1, LANES), lambda j: (l, 0, j))


def pool_fwd(proj, wpool, pscale3, l):
    T = proj.shape[0]
    R = R_POOL
    ngrp = len(POOL_WINDOWS)

    def body(p_ref, g_ref, w_ref, s_ref, o_ref, pooled_s, mixed_s):
        grp = pl.program_id(0)

        def group(w):
            _pooled_loop(p_ref, pooled_s, w, T)
            mixed_s[...] = jnp.dot(pooled_s[...], w_ref[...].astype(BF16), preferred_element_type=F32)
            sc = s_ref[...]

            def chunk(i, carry):
                r0 = pl.multiple_of(i * R, R)
                g = g_ref[pl.ds(r0, R), :].astype(F32)
                o_ref[pl.ds(r0, R), :] = (mixed_s[pl.ds(r0, R), :] * sc * (g * _sigmoid(g))).astype(BF16)
                return carry

            lax.fori_loop(0, T // R, chunk, 0)

        for k, w in enumerate(POOL_WINDOWS):
            pl.when(grp == k)(functools.partial(group, w))

    return pl.pallas_call(
        body, name="pool_fwd", grid=(ngrp,),
        in_specs=[pl.BlockSpec((T, LANES), lambda j: (0, j + 16)), pl.BlockSpec((T, LANES), lambda j: (0, j + 20)),
                  _pool_w_spec(l), _pool_s_spec(l)],
        out_specs=pl.BlockSpec((T, LANES), lambda j: (0, j)),
        out_shape=jax.ShapeDtypeStruct((T, ngrp * LANES), BF16),
        scratch_shapes=[pltpu.VMEM((T, LANES), BF16), pltpu.VMEM((T, LANES), F32)],
        compiler_params=_params(),
    )(proj, proj, wpool, pscale3)


def out_fwd(ya, yp, wo, x, mod4, g_post3, l, after):
    T, D = x.shape
    H = ya.shape[1]
    tm = 512

    def body(ya_ref, yp_ref, wo_ref, x_ref, gt_ref, g_ref, after_ref, xn_ref, y_ref):
        y = (jnp.dot(ya_ref[...], wo_ref[0:H, :], preferred_element_type=F32)
             + jnp.dot(yp_ref[...], wo_ref[H:2 * H, :], preferred_element_type=F32))
        xn_ref[...] = x_ref[...] + gt_ref[...] * (y * _rms(y) * g_ref[...])
        y_ref[...] = y

    tile = pl.BlockSpec((tm, D), lambda i: (i, 0))
    half = pl.BlockSpec((tm, H), lambda i: (i, 0))
    return pl.pallas_call(
        body, name="out_fwd", grid=(T // tm,),
        in_specs=[half, half, pl.BlockSpec((2 * H, D), lambda i: (0, 0)), tile, _mod_row(l, 2, D), _layer_row(l, D),
                  ANY],
        out_specs=[tile, tile],
        out_shape=[jax.ShapeDtypeStruct((T, D), F32), jax.ShapeDtypeStruct((T, D), F32)],
        compiler_params=_params(VMEM_BIG),
    )(ya, yp, wo, x, mod4, g_post3, after)


def out_fwd_loss(ya, yp, wo, x, mod4, g_post3, l, target):
    T, D = x.shape
    H = ya.shape[1]
    tm = 512
    nt = T // tm

    def body(ya_ref, yp_ref, wo_ref, x_ref, gt_ref, g_ref, t_ref, dx_ref, y_ref, l_ref, acc):
        i = pl.program_id(0)

        @pl.when(i == 0)
        def _():
            acc[...] = jnp.zeros_like(acc)

        y = (jnp.dot(ya_ref[...], wo_ref[0:H, :], preferred_element_type=F32)
             + jnp.dot(yp_ref[...], wo_ref[H:2 * H, :], preferred_element_type=F32))
        y_ref[...] = y
        d = (x_ref[...] + gt_ref[...] * (y * _rms(y) * g_ref[...])) - t_ref[...]
        dx_ref[...] = d * (1.0 / D)
        acc[...] += _colsum8(d * d)

        @pl.when(i == nt - 1)
        def _():
            l_ref[...] = jnp.zeros_like(l_ref) + jnp.sum(acc[...]) * (0.5 / D)

    tile = pl.BlockSpec((tm, D), lambda i: (i, 0))
    half = pl.BlockSpec((tm, H), lambda i: (i, 0))
    return pl.pallas_call(
        body, name="out_fwd_loss", grid=(nt,),
        in_specs=[half, half, pl.BlockSpec((2 * H, D), lambda i: (0, 0)), tile, _mod_row(l, 2, D), _layer_row(l, D),
                  tile],
        out_specs=[tile, tile, pl.BlockSpec((SUBLANES, LANES), lambda i: (0, 0))],
        out_shape=[jax.ShapeDtypeStruct((T, D), F32), jax.ShapeDtypeStruct((T, D), F32),
                   jax.ShapeDtypeStruct((SUBLANES, LANES), F32)],
        scratch_shapes=[pltpu.VMEM((SUBLANES, D), F32)],
        compiler_params=_params(VMEM_BIG),
    )(ya, yp, wo, x, mod4, g_post3, target)


def out_bwd(dx, y, ya, yp, wo, mod4, g_post3, l, after):
    T, D = dx.shape
    H = ya.shape[1]
    tm = 512
    nt = T // tm

    def body(dx_ref, y_ref, ya_ref, yp_ref, wo_ref, gt_ref, g_ref, after_ref,
             dya_ref, dyp_ref, dwo_ref, dgt_ref, dg_ref, acc_w, acc_gt, acc_g):
        i = pl.program_id(0)

        @pl.when(i == 0)
        def _():
            acc_w[...] = jnp.zeros_like(acc_w)
            acc_gt[...] = jnp.zeros_like(acc_gt)
            acc_g[...] = jnp.zeros_like(acc_g)

        yv = y_ref[...]
        dxv = dx_ref[...]
        g = g_ref[...]
        r = _rms(yv)
        yn = yv * r
        acc_gt[...] += _colsum8(dxv * (yn * g))
        dn = dxv * gt_ref[...]
        acc_g[...] += _colsum8(dn * yn)
        a = dn * g
        dy = r * (a - yn * jnp.mean(a * yn, axis=-1, keepdims=True))
        dyb = dy.astype(BF16)
        dyc = lax.dot_general(dyb, wo_ref[...], NT, preferred_element_type=F32)
        dya_ref[...] = dyc[:, 0:H].astype(BF16)
        dyp_ref[...] = dyc[:, H:2 * H].astype(BF16)
        acc_w[0:H, :] += lax.dot_general(ya_ref[...], dyb, TN, preferred_element_type=F32)
        acc_w[H:2 * H, :] += lax.dot_general(yp_ref[...], dyb, TN, preferred_element_type=F32)

        @pl.when(i == nt - 1)
        def _():
            dwo_ref[...] = acc_w[...].astype(BF16)
            dgt_ref[...] = jnp.sum(acc_gt[...], axis=0, keepdims=True)
            dg_ref[...] = jnp.sum(acc_g[...], axis=0, keepdims=True)

    row = pl.BlockSpec((1, D), lambda i: (0, 0))
    tile = pl.BlockSpec((tm, D), lambda i: (i, 0))
    half = pl.BlockSpec((tm, H), lambda i: (i, 0))
    full = pl.BlockSpec((2 * H, D), lambda i: (0, 0))
    return pl.pallas_call(
        body, name="out_bwd", grid=(nt,),
        in_specs=[tile, tile, half, half, full, _mod_row(l, 2, D), _layer_row(l, D), ANY],
        out_specs=[half, half, full, row, row],
        out_shape=[jax.ShapeDtypeStruct((T, H), BF16), jax.ShapeDtypeStruct((T, H), BF16),
                   jax.ShapeDtypeStruct((2 * H, D), BF16),
                   jax.ShapeDtypeStruct((1, D), F32), jax.ShapeDtypeStruct((1, D), F32)],
        scratch_shapes=[pltpu.VMEM((2 * H, D), F32), pltpu.VMEM((SUBLANES, D), F32), pltpu.VMEM((SUBLANES, D), F32)],
        compiler_params=_params(VMEM_BIG),
    )(dx, y, ya, yp, wo, mod4, g_post3, after)


def conv_bwd(proj, dya, wconv, l):
    T = proj.shape[0]
    R = R_CONV
    nblk = 4
    nchunk = T // R

    def body(u_ref, b_ref, c_ref, g_ref, dy_ref, w_ref, du_ref, db_ref, dc_ref, dg_ref, dw_ref):
        w0 = w_ref[pl.ds(0, 1), :]
        w1 = w_ref[pl.ds(1, 1), :]
        w2 = w_ref[pl.ds(2, 1), :]

        def chunk(k, carry):
            head, a0, a1, a2 = carry
            i = nchunk - 1 - k
            r0 = pl.multiple_of(i * R, R)
            h0 = pl.multiple_of(jnp.maximum(r0 - HIST, 0), HIST)
            first = i == 0
            ue = _load_ext(u_ref, r0, h0, first, R)
            ce = _load_ext(c_ref, r0, h0, first, R)
            ca = ce * ue
            ca0 = ca[HIST:]
            ca1 = _shift_down(ca, 1, R)
            ca2 = _shift_down(ca, 2, R)
            conv = w2 * ca0 + w1 * ca1 + w0 * ca2
            g = g_ref[pl.ds(r0, R), :].astype(F32)
            b = b_ref[pl.ds(r0, R), :].astype(F32)
            dy = dy_ref[pl.ds(r0, R), :].astype(F32)
            sg = _sigmoid(g)
            sl = g * sg
            t = dy * conv
            db_ref[pl.ds(r0, R), :] = (t * sl).astype(BF16)
            dg_ref[pl.ds(r0, R), :] = (t * b * (sg * (1.0 + g * (1.0 - sg)))).astype(BF16)
            dconv = dy * b * sl
            a2 = a2 + _colsum8(dconv * ca0)
            a1 = a1 + _colsum8(dconv * ca1)
            a0 = a0 + _colsum8(dconv * ca2)
            e = jnp.concatenate([dconv, head], axis=0)
            dca = w2 * dconv + w1 * _shift_up(e, 1, R) + w0 * _shift_up(e, 2, R)
            du_ref[pl.ds(r0, R), :] = (dca * ce[HIST:]).astype(BF16)
            dc_ref[pl.ds(r0, R), :] = (dca * ue[HIST:]).astype(BF16)
            return dconv[0:SUBLANES], a0, a1, a2

        z = jnp.zeros((SUBLANES, LANES), F32)
        _, a0, a1, a2 = lax.fori_loop(0, nchunk, chunk, (z, z, z, z))
        dw_ref[pl.ds(0, 1), :] = jnp.sum(a0, axis=0, keepdims=True)
        dw_ref[pl.ds(1, 1), :] = jnp.sum(a1, axis=0, keepdims=True)
        dw_ref[pl.ds(2, 1), :] = jnp.sum(a2, axis=0, keepdims=True)

    def col(off):
        return pl.BlockSpec((T, LANES), lambda j: (0, j + off))

    sec = jax.ShapeDtypeStruct((T, nblk * LANES), BF16)
    return pl.pallas_call(
        body, name="conv_bwd", grid=(nblk,),
        in_specs=[col(0), col(4), col(8), col(12), col(0), pl.BlockSpec((None, None, 3, LANES), lambda j: (j, l, 0, 0))],
        out_specs=[col(0), col(0), col(0), col(0), pl.BlockSpec((None, 3, LANES), lambda j: (j, 0, 0))],
        out_shape=[sec, sec, sec, sec, jax.ShapeDtypeStruct((nblk, 3, LANES), F32)],
        compiler_params=_params(),
    )(proj, proj, proj, proj, dya, wconv)


def pool_bwd(proj, dyp, wpool, pscale3, l, after):
    T = proj.shape[0]
    R = R_POOL
    ngrp = len(POOL_WINDOWS)
    nchunk = T // R

    def body(p_ref, g_ref, dy_ref, w_ref, s_ref, after_ref, du_ref, dg_ref, dw_ref, ds_ref,
             pooled_s, mixed_s, dmix_s, dpool_s):
        grp = pl.program_id(0)

        def group(w):
            wb = w_ref[...].astype(BF16)
            _pooled_loop(p_ref, pooled_s, w, T)
            mixed_s[...] = jnp.dot(pooled_s[...], wb, preferred_element_type=F32)
            sc = s_ref[...]

            def gate_chunk(i, acc):
                r0 = pl.multiple_of(i * R, R)
                g = g_ref[pl.ds(r0, R), :].astype(F32)
                dy = dy_ref[pl.ds(r0, R), :].astype(F32)
                mixed = mixed_s[pl.ds(r0, R), :]
                sg = _sigmoid(g)
                dg_ref[pl.ds(r0, R), :] = (dy * mixed * sc * (sg * (1.0 + g * (1.0 - sg)))).astype(BF16)
                dms = dy * (g * sg)
                dmix_s[pl.ds(r0, R), :] = (dms * sc).astype(BF16)
                return acc + _colsum8(dms * mixed)

            acc = lax.fori_loop(0, nchunk, gate_chunk, jnp.zeros((SUBLANES, LANES), F32))
            ds_ref[...] = jnp.sum(acc, axis=0, keepdims=True)
            dpool_s[pl.ds(0, T), :] = lax.dot_general(dmix_s[...], wb, NT, preferred_element_type=F32)
            dpool_s[pl.ds(T, HIST), :] = jnp.zeros((HIST, LANES), F32)
            dw_ref[...] = lax.dot_general(pooled_s[...], dmix_s[...], TN, preferred_element_type=F32).astype(BF16)

            def back_chunk(i, carry):
                r0 = pl.multiple_of(i * R, R)
                dpe = dpool_s[pl.ds(r0, R + HIST), :]
                e = dpe / _count(r0, R + HIST, w)
                du_ref[pl.ds(r0, R), :] = (_anticausal_window_sum(e, w)[0:R] - dpe[0:R]).astype(BF16)
                return carry

            lax.fori_loop(0, nchunk, back_chunk, 0)

        for k, w in enumerate(POOL_WINDOWS):
            pl.when(grp == k)(functools.partial(group, w))

    def col(off):
        return pl.BlockSpec((T, LANES), lambda j: (0, j + off))

    sec = jax.ShapeDtypeStruct((T, ngrp * LANES), BF16)
    wspec = pl.BlockSpec((None, LANES, LANES), lambda j: (j, 0, 0))
    sspec = pl.BlockSpec((1, LANES), lambda j: (0, j))
    return pl.pallas_call(
        body, name="pool_bwd", grid=(ngrp,),
        in_specs=[col(16), col(20), col(0), _pool_w_spec(l), _pool_s_spec(l), ANY],
        out_specs=[col(0), col(0), wspec, sspec],
        out_shape=[sec, sec, jax.ShapeDtypeStruct((ngrp, LANES, LANES), BF16),
                   jax.ShapeDtypeStruct((1, ngrp * LANES), F32)],
        scratch_shapes=[pltpu.VMEM((T, LANES), BF16), pltpu.VMEM((T, LANES), F32),
                        pltpu.VMEM((T, LANES), BF16), pltpu.VMEM((T + HIST, LANES), F32)],
        compiler_params=_params(),
    )(proj, proj, dyp, wpool, pscale3, after)


def in_bwd(dsecs, wg, x, dxo, mod4, g_pre3, l):
    T, D = x.shape
    NB = N_CHIPS
    CW = wg.shape[1] // NB
    SW = dsecs[0].shape[1]
    nsec = len(dsecs)
    PW = 256
    assert SW % PW == 0 and CW % PW == 0
    tm = 256
    nt = T // tm

    def body(*refs):
        d_refs = refs[0:nsec]
        w_ref, x_ref, dxo_ref, sh_ref, sc_ref, g_ref = refs[nsec:nsec + 6]
        dxi_ref, dw_ref, dsh_ref, dsc_ref, dg_ref = refs[nsec + 6:nsec + 11]
        acc_w, acc_sh, acc_sc, acc_g = refs[nsec + 11:]
        i = pl.program_id(0)

        @pl.when(i == 0)
        def _():
            acc_w[...] = jnp.zeros_like(acc_w)
            acc_sh[...] = jnp.zeros_like(acc_sh)
            acc_sc[...] = jnp.zeros_like(acc_sc)
            acc_g[...] = jnp.zeros_like(acc_g)

        xv = x_ref[...]
        g = g_ref[...]
        r = _rms(xv)
        xh = xv * r
        n = xh * g
        sc1 = 1.0 + sc_ref[...]
        hb = (n * sc1 + sh_ref[...]).astype(BF16)
        dh = lax.dot_general(d_refs[0][...], w_ref[:, 0:SW], NT, preferred_element_type=F32)
        for s in range(1, nsec):
            dh = dh + lax.dot_general(d_refs[s][...], w_ref[:, s * SW:(s + 1) * SW], NT, preferred_element_type=F32)
        for p in range(nsec * SW // PW):
            col = p * PW
            s, so = col // SW, col % SW
            j, jo = col // CW, col % CW
            acc_w[j, :, jo:jo + PW] += lax.dot_general(hb, d_refs[s][:, so:so + PW], TN, preferred_element_type=F32)
        acc_sh[...] += _colsum8(dh)
        acc_sc[...] += _colsum8(dh * n)
        dnp = dh * sc1
        acc_g[...] += _colsum8(dnp * xh)
        a = dnp * g
        dxi_ref[...] = dxo_ref[...] + r * (a - xh * jnp.mean(a * xh, axis=-1, keepdims=True))

        @pl.when(i == nt - 1)
        def _():
            dw_ref[...] = acc_w[...].astype(BF16)
            dsh_ref[...] = jnp.sum(acc_sh[...], axis=0, keepdims=True)
            dsc_ref[...] = jnp.sum(acc_sc[...], axis=0, keepdims=True)
            dg_ref[...] = jnp.sum(acc_g[...], axis=0, keepdims=True)

    row = pl.BlockSpec((1, D), lambda i: (0, 0))
    tile = pl.BlockSpec((tm, D), lambda i: (i, 0))
    sect = pl.BlockSpec((tm, SW), lambda i: (i, 0))
    rowshape = jax.ShapeDtypeStruct((1, D), F32)
    return pl.pallas_call(
        body, name="in_bwd", grid=(nt,),
        in_specs=[sect] * nsec + [pl.BlockSpec((D, NB * CW), lambda i: (0, 0)), tile, tile,
                                  _mod_row(l, 0, D), _mod_row(l, 1, D), _layer_row(l, D)],
        out_specs=[tile, pl.BlockSpec((NB, D, CW), lambda i: (0, 0, 0)), row, row, row],
        out_shape=[jax.ShapeDtypeStruct((T, D), F32), jax.ShapeDtypeStruct((NB, D, CW), BF16),
                   rowshape, rowshape, rowshape],
        scratch_shapes=[pltpu.VMEM((NB, D, CW), F32),
                        pltpu.VMEM((SUBLANES, D), F32), pltpu.VMEM((SUBLANES, D), F32), pltpu.VMEM((SUBLANES, D), F32)],
        compiler_params=_params(VMEM_BIG),
    )(*dsecs, wg, x, dxo, mod4, mod4, g_pre3)


def _rcopy(src, dst, ssem, rsem, dev):
    return pltpu.make_async_remote_copy(src_ref=src, dst_ref=dst, send_sem=ssem, recv_sem=rsem,
                                        device_id=dev, device_id_type=MESH)


def _peers7(x, y, c):
    out = []
    for m in range(1, N_DEV):
        bx, by, bc = (m >> 2) & 1, (m >> 1) & 1, m & 1
        out.append(((1 - x) if bx else x, (1 - y) if by else y, (1 - c) if bc else c))
    return out


HBM = pl.BlockSpec(memory_space=pltpu.HBM)
SEM = pl.BlockSpec(memory_space=pltpu.SEMAPHORE)
SPLIT = pltpu.CompilerParams(has_side_effects=pltpu.SideEffectType.DATAFLOW_SIDE_EFFECTING)


def _hbm(a):
    return pltpu.with_memory_space_constraint(a, pltpu.HBM)


def _chips(x, y):
    return [(1 - x, y), (x, 1 - y), (1 - x, 1 - y)]


SIBLING_BARRIER_ID = 0


def xchg_start(name, bufs, n_copies, plan, sibling_only=False):
    n = len(bufs)

    def body(*refs):
        ssem, rsem, token = refs[n], refs[n + 1], refs[-1]
        x, y, c = _me()
        if sibling_only:
            barrier = pltpu.get_barrier_semaphore()
            pl.semaphore_signal(barrier, inc=1, device_id=(x, y, 1 - c), device_id_type=MESH)
            pl.semaphore_wait(barrier, 1)
        copies = plan(refs[0:n], x, y, c)
        assert len(copies) == n_copies
        for k, (src, dst, peer, _) in enumerate(copies):
            _rcopy(src, dst, ssem.at[k], rsem.at[k], peer).start()
        token[...] = jnp.zeros_like(token)

    params = dict(has_side_effects=pltpu.SideEffectType.DATAFLOW_SIDE_EFFECTING)
    if sibling_only:
        params["collective_id"] = SIBLING_BARRIER_ID
    outs = pl.pallas_call(
        body, name=name,
        in_specs=[HBM] * n,
        out_specs=[SEM, SEM] + [HBM] * n + [pl.BlockSpec(memory_space=pltpu.VMEM)],
        out_shape=([pltpu.SemaphoreType.DMA((n_copies,))] * 2 + [pltpu.HBM(b.shape, b.dtype) for b in bufs]
                   + [jax.ShapeDtypeStruct((SUBLANES, LANES), F32)]),
        input_output_aliases={a: 2 + a for a in range(n)},
        compiler_params=pltpu.CompilerParams(**params),
    )(*[_hbm(b) for b in bufs])
    return outs[0], outs[1], list(outs[2:2 + n]), outs[-1]


def xchg_wait(name, bufs, ssem, rsem, n_copies, plan, after):
    n = len(bufs)
    after = list(after)

    def body(*refs):
        ssem_ref, rsem_ref = refs[n], refs[n + 1]
        copies = plan(refs[0:n], *_me())
        assert len(copies) == n_copies
        for k, (src, _, peer, land) in enumerate(copies):
            cp = _rcopy(src, land, ssem_ref.at[k], rsem_ref.at[k], peer)
            cp.wait_send()
            cp.wait_recv()

    outs = pl.pallas_call(
        body, name=name,
        in_specs=[HBM] * n + [SEM, SEM] + [ANY] * len(after), out_specs=[HBM] * n,
        out_shape=[pltpu.HBM(b.shape, b.dtype) for b in bufs],
        input_output_aliases={a: a for a in range(n)},
        compiler_params=SPLIT,
    )(*bufs, ssem, rsem, *after)
    return list(outs)


def _shard_half(buf, chip, half):
    if len(buf.shape) == 2:
        h, w = buf.shape[0] // 2, buf.shape[1] // N_CHIPS
        return buf.at[pl.ds(half * h, h), pl.ds(chip * w, w)]
    h = buf.shape[1] // 2
    return buf.at[chip, pl.ds(half * h, h)]


def plan_gather(refs, x, y, c):
    out = []
    for (px, py) in _chips(x, y):
        for buf in refs:
            own = _shard_half(buf, 2 * x + y, c)
            out.append((own, own, (px, py, c), _shard_half(buf, 2 * px + py, c)))
    return out


def plan_forward(refs, x, y, c):
    out = []
    for (px, py) in _chips(x, y):
        for buf in refs:
            landed = _shard_half(buf, 2 * px + py, c)
            out.append((landed, landed, (x, y, 1 - c), _shard_half(buf, 2 * px + py, 1 - c)))
    return out


def plan_sibling(refs, x, y, c):
    n = len(refs) // 2
    out = []
    for a in range(n):
        h = refs[a].shape[1] // 2
        out.append((refs[a].at[:, pl.ds((1 - c) * h, h)], refs[n + a], (x, y, 1 - c), refs[n + a]))
    return out


def plan_chip(refs, x, y, c):
    n = len(refs) // 2
    out = []
    for j, (px, py) in enumerate(_chips(x, y)):
        for a in range(n):
            out.append((refs[a].at[2 * px + py], refs[n + a].at[j], (px, py, c), refs[n + a].at[j]))
    return out


def plan_mod(refs, x, y, c):
    (mods,) = refs
    mine = mods.at[2 * x + y]
    return [(mine, mine, (px, py, c), mods.at[2 * px + py]) for (px, py) in _chips(x, y)]


def plan_pack(refs, x, y, c):
    (packs,) = refs
    mine = packs.at[4 * x + 2 * y + c]
    return [(mine, mine, peer, packs.at[4 * peer[0] + 2 * peer[1] + peer[2]]) for peer in _peers7(x, y, c)]


def plan_spread(layers, wp_layers):
    def plan(refs, x, y, c):
        gi, go, gp = refs
        hD, hR, hP = gi.shape[1] // 2, go.shape[1] // 2, gp.shape[2] // 2
        sib = (x, y, 1 - c)
        out = []
        for l in layers:
            mine = gi.at[l, pl.ds(c * hD, hD)]
            out.append((mine, mine, sib, gi.at[l, pl.ds((1 - c) * hD, hD)]))
            mine = go.at[l, pl.ds(c * hR, hR)]
            out.append((mine, mine, sib, go.at[l, pl.ds((1 - c) * hR, hR)]))
        for l in wp_layers:
            mine = gp.at[l, 2 * x + y, pl.ds(c * hP, hP)]
            for peer in _peers7(x, y, c):
                out.append((mine, mine, peer, gp.at[l, 2 * peer[0] + peer[1], pl.ds(peer[2] * hP, hP)]))
        return out

    return plan


def gather_small(c8, wc, token):
    def body(c_ref, wc_ref, token_ref, call, wcall, ssem, rsem, lsem):
        x, y, c = _me()
        myc = 2 * x + y
        me_lin = 4 * x + 2 * y + c
        me = (x, y, c)
        local = [pltpu.make_async_copy(c_ref, call.at[me_lin], lsem.at[0]),
                 pltpu.make_async_copy(wc_ref, wcall.at[myc], lsem.at[1])]
        for cp in local:
            cp.start()
        sends, recvs = [], []
        for m, peer in enumerate(_peers7(x, y, c)):
            plin = 4 * peer[0] + 2 * peer[1] + peer[2]
            sends.append(_rcopy(c_ref, call.at[me_lin], ssem.at[m], rsem.at[m], peer))
            recvs.append(_rcopy(call.at[plin], call.at[plin], ssem.at[m], rsem.at[m], me))
        for j, (px, py) in enumerate([(1 - x, y), (x, 1 - y), (1 - x, 1 - y)]):
            pc = 2 * px + py
            sends.append(_rcopy(wc_ref, wcall.at[myc], ssem.at[7 + j], rsem.at[7 + j], (px, py, c)))
            recvs.append(_rcopy(wcall.at[pc], wcall.at[pc], ssem.at[7 + j], rsem.at[7 + j], me))
        for cp in sends:
            cp.start()
        for cp in recvs:
            cp.wait_recv()
        for cp in sends:
            cp.wait_send()
        for cp in local:
            cp.wait()

    return pl.pallas_call(
        body, name="gather_small",
        in_specs=[ANY] * 3, out_specs=[ANY] * 2,
        out_shape=[jax.ShapeDtypeStruct((N_DEV, SUBLANES, LANES), F32),
                   jax.ShapeDtypeStruct((N_CHIPS, wc.shape[0], 3, LANES), F32)],
        scratch_shapes=[pltpu.SemaphoreType.DMA((10,)), pltpu.SemaphoreType.DMA((10,)), pltpu.SemaphoreType.DMA((2,))],
        compiler_params=_params(n_grid=0),
    )(c8, wc, token)


def spread_now(gi, go, gp, layers, wp_layers):
    plan = plan_spread(layers, wp_layers)
    n = 2 * len(layers) + 7 * len(wp_layers)

    def body(gi_in, go_in, gp_in, gi, go, gp, ssem, rsem):
        copies = plan((gi, go, gp), *_me())
        me = _me()
        sends = [_rcopy(src, dst, ssem.at[k], rsem.at[k], peer) for k, (src, dst, peer, _) in enumerate(copies)]
        for cp in sends:
            cp.start()
        for k, (_, _, _, land) in enumerate(copies):
            _rcopy(land, land, ssem.at[k], rsem.at[k], me).wait_recv()
        for cp in sends:
            cp.wait_send()

    return pl.pallas_call(
        body, name="spread_now",
        in_specs=[ANY] * 3, out_specs=[ANY] * 3,
        out_shape=[jax.ShapeDtypeStruct(a.shape, a.dtype) for a in (gi, go, gp)],
        input_output_aliases={0: 0, 1: 1, 2: 2},
        scratch_shapes=[pltpu.SemaphoreType.DMA((n,)), pltpu.SemaphoreType.DMA((n,))],
        compiler_params=_params(n_grid=0),
    )(gi, go, gp)


def add_sibling(cidx, mine, sib):
    def body(c_ref, *refs):
        for a in range(3):
            m, s, o = refs[a], refs[3 + a], refs[6 + a]
            o[...] = (m[...].astype(F32) + s[...].astype(F32)).astype(BF16)

    def mine_spec(a):
        h = a.shape[1] // 2
        return pl.BlockSpec((None, h, a.shape[2]), lambda j, c_ref: (j, c_ref[0], 0))

    def sib_spec(a):
        return pl.BlockSpec((None,) + a.shape[1:], lambda j, c_ref: (j, 0, 0))

    return pl.pallas_call(
        body, name="add_sibling",
        grid_spec=pltpu.PrefetchScalarGridSpec(
            num_scalar_prefetch=1, grid=(N_CHIPS,),
            in_specs=[mine_spec(a) for a in mine] + [sib_spec(a) for a in sib],
            out_specs=[sib_spec(a) for a in sib]),
        out_shape=[jax.ShapeDtypeStruct(a.shape, BF16) for a in sib],
        compiler_params=_params(VMEM_BIG),
    )(cidx, *mine, *sib)


def sum_chips(pos, own, rb, acc, l, shapes):
    nq = 4
    n_in = 6 + (3 if acc is not None else 0)

    def body(pos_ref, *refs):
        for a in range(3):
            m, b, o = refs[a], refs[3 + a], refs[n_in + a]
            s = m[...].astype(F32)
            for j in range(3):
                s = s + b[j].astype(F32)
            o[...] = s

    def own_spec(a):
        return pl.BlockSpec((None, a.shape[1] // nq, a.shape[2]), lambda q, p: (p[1], q, 0))

    def rb_spec(a):
        return pl.BlockSpec((3, a.shape[1] // nq, a.shape[2]), lambda q, p: (0, q, 0))

    hi, ho, hp = own[0].shape[1] // nq, own[1].shape[1] // nq, own[2].shape[1] // nq
    out_specs = [pl.BlockSpec((None, hi, shapes[0][2]), lambda q, p: (l, p[0] * nq + q, 0)),
                 pl.BlockSpec((None, ho, shapes[1][2]), lambda q, p: (l, p[0] * nq + q, 0)),
                 pl.BlockSpec((None, None, hp, LANES), lambda q, p: (l, p[1], p[0] * nq + q, 0))]
    in_specs = [own_spec(a) for a in own] + [rb_spec(a) for a in rb]
    args = list(own) + list(rb)
    aliases = {}
    if acc is not None:
        in_specs += [ANY] * 3
        args += list(acc)
        aliases = {7: 0, 8: 1, 9: 2}
    return pl.pallas_call(
        body, name="sum_chips",
        grid_spec=pltpu.PrefetchScalarGridSpec(num_scalar_prefetch=1, grid=(nq,), in_specs=in_specs, out_specs=out_specs),
        out_shape=[jax.ShapeDtypeStruct(s, F32) for s in shapes],
        input_output_aliases=aliases,
        compiler_params=_params(VMEM_BIG),
    )(pos, *args)


def pack_small(pos, per_layer, loss_blk):
    L = len(per_layer)
    D = per_layer[0][0].shape[1]

    def body(pos_ref, *refs):
        o = refs[-1]
        lb = refs[-2]
        o[...] = jnp.zeros_like(o)
        for l in range(L):
            dgpre, dgpost, dsh, dsc, dgt, dps, dwc = refs[7 * l:7 * l + 7]
            base = SUBLANES * l
            for r, src in enumerate((dgpre, dgpost, dsh, dsc, dgt)):
                o[pl.ds(base + r, 1), :] = src[...]
            o[pl.ds(base + 5, 1), 0:dps.shape[1]] = dps[...]
            for j in range(dwc.shape[0]):
                for k in range(3):
                    idx = 3 * j + k
                    o[pl.ds(base + 6 + idx // 8, 1), (idx % 8) * LANES:(idx % 8 + 1) * LANES] = dwc[j, pl.ds(k, 1), :]
        o[pl.ds(5, 1), 4 * LANES:5 * LANES] = lb[pl.ds(0, 1), :]

    flat = [a for layer in per_layer for a in layer] + [loss_blk]

    def whole(a):
        return pl.BlockSpec(a.shape, lambda i, p: (0,) * a.ndim)

    return pl.pallas_call(
        body, name="pack_small",
        grid_spec=pltpu.PrefetchScalarGridSpec(
            num_scalar_prefetch=1, grid=(1,), in_specs=[whole(a) for a in flat],
            out_specs=pl.BlockSpec((None, L * SUBLANES, D), lambda i, p: (p[2], 0, 0))),
        out_shape=jax.ShapeDtypeStruct((N_DEV, L * SUBLANES, D), F32),
        compiler_params=_params(),
    )(pos, *flat)


def small_update(pos, packs, params, moments_m, moments_v):
    n = len(params)
    L, D = params[1].shape
    PS = params[3].shape[1]

    def body(pos_ref, p_ref, *refs):
        ws, ms, vs = refs[0:n], refs[n:2 * n], refs[2 * n:3 * n]
        loss_ref = refs[3 * n]
        outs = [refs[3 * n + 1 + 4 * t:3 * n + 5 + 4 * t] for t in range(n)]
        summed = refs[-1]
        s = p_ref[0]
        for d in range(1, N_DEV):
            s = s + p_ref[d]
        summed[...] = s
        loss_ref[...] = summed[pl.ds(5, 1), 4 * LANES:5 * LANES]
        chip = pos_ref[1]

        def update(t, idx, g):
            d, mm, vv = _adamw_math(ws[t][idx], g, ms[t][idx], vs[t][idx])
            g_ref, d_ref, mo_ref, vo_ref = outs[t]
            g_ref[idx] = g
            d_ref[idx] = d
            mo_ref[idx] = mm
            vo_ref[idx] = vv

        for l in range(L):
            base = SUBLANES * l
            row = pl.ds(l, 1)
            for k in range(3):
                update(0, (row, slice(k * D, (k + 1) * D)), summed[pl.ds(base + 2 + k, 1), :])
            update(1, (row, slice(None)), summed[pl.ds(base, 1), :])
            update(2, (row, slice(None)), summed[pl.ds(base + 1, 1), :])
            update(3, (row, slice(None)), summed[pl.ds(base + 5, 1), 0:PS])
            for k in range(3):
                g = None
                for j in range(N_CHIPS):
                    idx = 3 * j + k
                    cand = summed[pl.ds(base + 6 + idx // 8, 1), (idx % 8) * LANES:(idx % 8 + 1) * LANES]
                    g = cand if g is None else jnp.where(chip == j, cand, g)
                update(4, (l, pl.ds(k, 1), slice(None)), g)

    def whole(a):
        return pl.BlockSpec(a.shape, lambda i, p: (0,) * a.ndim)

    ins = [packs] + list(params) + list(moments_m) + list(moments_v)
    out_shape = [jax.ShapeDtypeStruct((1, LANES), F32)]
    for w in params:
        out_shape += [jax.ShapeDtypeStruct(w.shape, F32)] * 4
    outs = pl.pallas_call(
        body, name="small_update",
        grid_spec=pltpu.PrefetchScalarGridSpec(
            num_scalar_prefetch=1, grid=(1,), in_specs=[whole(a) for a in ins],
            out_specs=[whole(a) for a in out_shape],
            scratch_shapes=[pltpu.VMEM(packs.shape[1:], F32)]),
        out_shape=out_shape,
        compiler_params=_params(),
    )(pos, *ins)
    return outs[0], [outs[1 + 4 * t:5 + 4 * t] for t in range(n)]


def _adamw_math(w, g, m, v):
    m = ADAM_B1 * m + (1.0 - ADAM_B1) * g
    v = ADAM_B2 * v + (1.0 - ADAM_B2) * (g * g)
    m_hat = m / (1.0 - ADAM_B1 ** ADAM_STEP)
    v_hat = v / (1.0 - ADAM_B2 ** ADAM_STEP)
    delta = -ADAM_LR * (m_hat / (jnp.sqrt(v_hat) + ADAM_EPS) + ADAM_WD * w)
    return delta, m, v


def adamw(w, g, m, v, block, name, first=0, count=None, acc=None):
    grid = tuple(s // b for s, b in zip(w.shape, block))
    if count is not None:
        grid = (count,) + grid[1:]

    def body(w_ref, g_ref, m_ref, v_ref, *rest):
        go_ref, d_ref, mo_ref, vo_ref = rest[-4:]
        gv = g_ref[...]
        d, mm, vv = _adamw_math(w_ref[...], gv, m_ref[...], v_ref[...])
        go_ref[...] = gv
        d_ref[...] = d
        mo_ref[...] = mm
        vo_ref[...] = vv

    spec = pl.BlockSpec(block, lambda i, *rest: (first + i,) + rest)
    shape = jax.ShapeDtypeStruct(w.shape, F32)
    extra = [] if acc is None else list(acc)
    return pl.pallas_call(
        body, name=name, grid=grid,
        in_specs=[spec] * 4 + [ANY] * len(extra), out_specs=[spec] * 4, out_shape=[shape] * 4,
        input_output_aliases={4 + a: a for a in range(len(extra))},
        compiler_params=_params(VMEM_BIG, n_grid=len(grid)),
    )(w, g, m, v, *extra)


def ada_finish(c_all, dmod, w, m, v):
    L, D, CW = w.shape
    hD = D // 2

    def body(c_ref, d_ref, w_ref, m_ref, v_ref, g_ref, dl_ref, mo_ref, vo_ref):
        cv = c_ref[...]
        z = jnp.zeros_like(cv)
        ca = jnp.concatenate([cv * jax.nn.sigmoid(cv), z], axis=0).astype(BF16)
        dm = jnp.concatenate([d_ref[0], jnp.zeros_like(d_ref[0])], axis=0).astype(BF16)
        g = lax.dot_general(ca, dm, TN, preferred_element_type=F32)
        g_ref[0] = g
        d, mm, vv = _adamw_math(w_ref[0], g, m_ref[0], v_ref[0])
        dl_ref[0] = d
        mo_ref[0] = mm
        vo_ref[0] = vv

    big = pl.BlockSpec((1, hD, CW), lambda l, h: (l, h, 0))
    shape = jax.ShapeDtypeStruct(w.shape, F32)
    return pl.pallas_call(
        body, name="ada_finish", grid=(L, 2),
        in_specs=[pl.BlockSpec((N_DEV, hD), lambda l, h: (0, h)), pl.BlockSpec((1, N_DEV, CW), lambda l, h: (l, 0, 0)),
                  big, big, big],
        out_specs=[big] * 4, out_shape=[shape] * 4,
        compiler_params=_params(VMEM_BIG, n_grid=2),
    )(c_all, dmod, w, m, v)


def kernel(x, c, w_ada, b_ada, g_pre, w_in, w_conv, w_pool, pool_scale, w_out, g_post, loss_target, m_w_ada, m_b_ada, m_g_pre, m_w_in, m_w_conv, m_w_pool, m_pool_scale, m_w_out, m_g_post, v_w_ada, v_b_ada, v_g_pre, v_w_in, v_w_conv, v_w_pool, v_pool_scale, v_w_out, v_g_post):
    L, D, CW = w_in.shape
    RO = w_out.shape[1]
    T = x.shape[1]
    ix, iy, ic = _me()
    chip = 2 * ix + iy
    me_lin = 4 * ix + 2 * iy + ic

    pos = jnp.stack([ic, chip, me_lin]).astype(jnp.int32)
    g_pre3, g_post3 = g_pre.reshape(L, 1, D), g_post.reshape(L, 1, D)
    pscale3 = pool_scale.reshape(L, 1, pool_scale.shape[1])
    n_g, n_s, n_c = 6, 3, 9

    c_all3, wconv_all = gather_small(c.reshape(SUBLANES, LANES), w_conv, pos)
    c_all = c_all3.reshape(N_DEV, D)
    gath = [None] * L
    ss, rs, bufs, token = xchg_start("gather_start", list(cast_weights(pos, w_in, w_out, 0, c_all3)), n_g, plan_gather)
    gath[0] = (ss, rs, bufs)
    b_my = lax.dynamic_slice_in_dim(b_ada, chip * CW, CW, axis=1)
    m_ss, m_rs, mods, token = xchg_start("mod_start", [mod_part(pos, c_all, w_ada, b_my, token)], 3, plan_mod)
    for l in range(1, L):
        ss, rs, bufs, token = xchg_start("gather_start", list(cast_weights(pos, w_in, w_out, l, token)), n_g, plan_gather)
        gath[l] = (ss, rs, bufs)
    (mod_all,) = xchg_wait("mod_wait", mods, m_ss, m_rs, 3, plan_mod, [token])
    mod = lax.dynamic_index_in_dim(mod_all, me_lin, axis=2, keepdims=False)
    mod4 = jnp.transpose(mod, (1, 0, 2)).reshape(L, 3, 1, D)
    token = mod4

    def arrive(l, after):
        ss, rs, bufs = gath[l]
        bufs = xchg_wait("gather_wait", bufs, ss, rs, n_g, plan_gather, after)
        return xchg_start("forward_start", bufs, n_g, plan_forward, sibling_only=True)

    xs, projs, yas, yps, ys = [x.reshape(T, D)], [], [], [], []
    wg_in, wg_out = [], []
    fwd = arrive(0, [token])
    for l in range(L):
        fss, frs, bufs, ftoken = fwd
        gi, go = xchg_wait("forward_wait", bufs, fss, frs, n_g, plan_forward, [ftoken if l == 0 else xs[l]])
        wg_in.append(gi)
        wg_out.append(go.reshape(N_CHIPS * RO, D))
        proj = proj_fwd(xs[l], mod4, g_pre3, wg_in[l], l)
        ya = conv_fwd(proj, wconv_all, l)
        yp = pool_fwd(proj, w_pool, pscale3, l)
        projs.append(proj)
        yas.append(ya)
        yps.append(yp)
        if l + 1 < L:
            fwd = arrive(l + 1, [ya, yp])
            xn, yv = out_fwd(ya, yp, wg_out[l], xs[l], mod4, g_post3, l, fwd[3])
            xs.append(xn)
        else:
            dx, yv, loss_blk = out_fwd_loss(ya, yp, wg_out[l], xs[l], mod4, g_post3, l, loss_target.reshape(T, D))
        ys.append(yv)

    shapes = (w_in.shape, w_out.shape, w_pool.shape)
    smalls = [None] * L
    acc, flying, sib, token = None, None, None, loss_blk

    def to_chips(sib, after):
        sl, s_ss, s_rs, s_bufs = sib
        s_bufs = xchg_wait("sibling_wait", s_bufs, s_ss, s_rs, n_s, plan_sibling, after)
        chip_parts = add_sibling(pos, s_bufs[0:3], s_bufs[3:6])
        lands = [lax.empty((3,) + a.shape[1:], a.dtype) for a in chip_parts]
        c_ss, c_rs, c_bufs, ctoken = xchg_start("chip_start", list(chip_parts) + lands, n_c, plan_chip)
        return (sl, c_ss, c_rs, c_bufs), ctoken

    def landed(flying, acc, after):
        fl, f_ss, f_rs, f_bufs = flying
        f_bufs = xchg_wait("chip_wait", f_bufs, f_ss, f_rs, n_c, plan_chip, after)
        return sum_chips(pos, f_bufs[0:3], f_bufs[3:6], acc, fl, shapes)

    for l in reversed(range(L)):
        dya, dyp, dwo_l, dgate, dgpost = out_bwd(dx, ys[l], yas[l], yps[l], wg_out[l], mod4, g_post3, l, token)
        token = dya
        if sib is not None:
            arrived = flying
            flying, token = to_chips(sib, [dya])
            if arrived is not None:
                acc = landed(arrived, acc, [token])
                token = acc[0]
        du_p, dg_p, dwp_l, dps = pool_bwd(projs[l], dyp, w_pool, pscale3, l, token)
        du_a, db_a, dc_a, dg_a, dwc = conv_bwd(projs[l], dya, wconv_all, l)
        dx, dwi_l, dshift, dscale, dgpre = in_bwd([du_a, db_a, dc_a, dg_a, du_p, dg_p], wg_in[l], xs[l], dx,
                                                  mod4, g_pre3, l)
        smalls[l] = (dgpre, dgpost, dshift, dscale, dgate, dps, dwc)
        parts = [dwi_l, dwo_l.reshape(N_CHIPS, RO, D), dwp_l]
        s_lands = [lax.empty((a.shape[0], a.shape[1] // 2) + a.shape[2:], a.dtype) for a in parts]
        s_ss, s_rs, s_bufs, token = xchg_start("sibling_start", parts + s_lands, n_s, plan_sibling, sibling_only=True)
        sib = (l, s_ss, s_rs, s_bufs)
    grad_x = dx.reshape(1, T, D)

    p_ss, p_rs, packs, ptoken = xchg_start("pack_start", [pack_small(pos, smalls, loss_blk)], N_DEV - 1, plan_pack)
    acc = landed(flying, acc, [ptoken, token])
    n_sp = (2 + N_DEV - 1) * (L - 1)
    spread = plan_spread(tuple(range(1, L)), tuple(range(1, L)))
    sp_ss, sp_rs, acc, sp_token = xchg_start("spread_start", list(acc), n_sp, spread)
    flying, token = to_chips(sib, [sp_token])
    (packs_all,) = xchg_wait("pack_wait", packs, p_ss, p_rs, N_DEV - 1, plan_pack, [token])
    dmod_all = packs_all.reshape(N_DEV, L, SUBLANES, D)[:, :, 2:5].reshape(N_DEV, L, 3 * D)
    dmod_my = jnp.transpose(lax.dynamic_slice_in_dim(dmod_all, chip * CW, CW, axis=2), (1, 0, 2))

    g_w_ada, d_w_ada, nm_w_ada, nv_w_ada = ada_finish(c_all, dmod_my, w_ada, m_w_ada, v_w_ada)
    loss_row, upd = small_update(pos, packs_all, [b_ada, g_pre, g_post, pool_scale, w_conv],
                                 [m_b_ada, m_g_pre, m_g_post, m_pool_scale, m_w_conv],
                                 [v_b_ada, v_g_pre, v_g_post, v_pool_scale, v_w_conv])
    loss = loss_row[0, 0]
    (g_b_ada, d_b_ada, nm_b_ada, nv_b_ada), (g_g_pre, d_g_pre, nm_g_pre, nv_g_pre) = upd[0], upd[1]
    (g_g_post, d_g_post, nm_g_post, nv_g_post), (g_pscale, d_pscale, nm_pscale, nv_pscale) = upd[2], upd[3]
    g_w_conv, d_w_conv, nm_w_conv, nv_w_conv = upd[4]

    done = [nv_w_ada, nv_w_conv]
    g_w_in, g_w_out, g_w_pool = xchg_wait("spread_wait", acc, sp_ss, sp_rs, n_sp, spread, done)
    in_blk, out_blk = (1, D // 2, CW), (1, RO, D)
    upd_in = adamw(w_in, g_w_in, m_w_in, v_w_in, in_blk, "adamw_w_in", 1, L - 1)
    upd_out = adamw(w_out, g_w_out, m_w_out, v_w_out, out_blk, "adamw_w_out", 1, L - 1)

    acc = landed(flying, (g_w_in, g_w_out, g_w_pool), [upd_in[3], upd_out[3]])
    r_w_in, r_w_out, r_w_pool = spread_now(*acc, (0,), (0,))
    g_w_in, d_w_in, nm_w_in, nv_w_in = adamw(w_in, r_w_in, m_w_in, v_w_in, in_blk, "adamw_w_in", 0, 1, upd_in)
    g_w_out, d_w_out, nm_w_out, nv_w_out = adamw(w_out, r_w_out, m_w_out, v_w_out, out_blk, "adamw_w_out", 0, 1, upd_out)
    pshape = (L, N_CHIPS * LANES, LANES)
    upd_pool = adamw(w_pool.reshape(pshape), r_w_pool.reshape(pshape), m_w_pool.reshape(pshape),
                     v_w_pool.reshape(pshape), (1,) + pshape[1:], "adamw_w_pool")
    g_w_pool, d_w_pool, nm_w_pool, nv_w_pool = [a.reshape(w_pool.shape) for a in upd_pool]

    return (loss, grad_x,
            g_w_ada, g_b_ada, g_g_pre, g_w_in, g_w_conv, g_w_pool, g_pscale, g_w_out, g_g_post,
            d_w_ada, d_b_ada, d_g_pre, d_w_in, d_w_conv, d_w_pool, d_pscale, d_w_out, d_g_post,
            nm_w_ada, nm_b_ada, nm_g_pre, nm_w_in, nm_w_conv, nm_w_pool, nm_pscale, nm_w_out, nm_g_post,
            nv_w_ada, nv_b_ada, nv_g_pre, nv_w_in, nv_w_conv, nv_w_pool, nv_pscale, nv_w_out, nv_g_post)
```

```python
import functools

import jax
import jax.numpy as jnp
from jax import lax
from jax.experimental import pallas as pl
from jax.experimental.pallas import tpu as pltpu

F32 = jnp.float32
BF16 = jnp.bfloat16
MESH = pl.DeviceIdType.MESH
ANY = pl.BlockSpec(memory_space=pl.ANY)

NORM_EPS = 1e-6
POOL_WINDOWS = (2, 4, 8, 16)
ADAM_LR = 0.001
ADAM_B1 = 0.9
ADAM_B2 = 0.999
ADAM_EPS = 1e-08
ADAM_WD = 0.01
ADAM_STEP = 10

N_CHIPS = 4
N_DEV = 8
LANES = 128
SUBLANES = 8
VMEM_BIG = 56 * 1024 * 1024
HIST = 16
R_CONV = 64
R_POOL = 128

NT = (((1,), (1,)), ((), ()))
TN = (((0,), (0,)), ((), ()))


def _params(vmem=None, n_grid=1):
    kw = {}
    if n_grid:
        kw["dimension_semantics"] = ("arbitrary",) * n_grid
    if vmem is not None:
        kw["vmem_limit_bytes"] = vmem
    return pltpu.CompilerParams(**kw)


def _colsum8(v):
    n, d = v.shape
    return v.reshape(n // SUBLANES, SUBLANES, d).sum(axis=0)


def _rms(v):
    return lax.rsqrt(jnp.mean(v * v, axis=-1, keepdims=True) + NORM_EPS)


def _sigmoid(v):
    return 0.5 * jnp.tanh(0.5 * v) + 0.5


def _shift_down(ext, k, rows):
    if k == 0:
        return ext[HIST:HIST + rows]
    return pltpu.roll(ext, k, 0)[HIST:HIST + rows]


def _shift_up(ext, k, rows):
    if k == 0:
        return ext[0:rows]
    return pltpu.roll(ext, ext.shape[0] - k, 0)[0:rows]


def _load_ext(ref, r0, h0, first, rows):
    hist = ref[pl.ds(h0, HIST), :].astype(F32)
    hist = jnp.where(first, 0.0, hist)
    cur = ref[pl.ds(r0, rows), :].astype(F32)
    return jnp.concatenate([hist, cur], axis=0)


def _me():
    return lax.axis_index("x"), lax.axis_index("y"), lax.axis_index("c")


def cast_weights(pos, w_in, w_out, l, after):
    _, D, CW = w_in.shape
    RO = w_out.shape[1]

    def body(pos_ref, wi, wo, after_ref, oi, oo):
        oi[...] = wi[...].astype(BF16)
        oo[...] = wo[...].astype(BF16)

    return pl.pallas_call(
        body, name="cast_w",
        grid_spec=pltpu.PrefetchScalarGridSpec(
            num_scalar_prefetch=1, grid=(2,),
            in_specs=[pl.BlockSpec((None, D // 2, CW), lambda h, p: (l, h, 0)),
                      pl.BlockSpec((None, RO // 2, D), lambda h, p: (l, h, 0)), ANY],
            out_specs=[pl.BlockSpec((D // 2, CW), lambda h, p: (h, p[1])),
                       pl.BlockSpec((None, RO // 2, D), lambda h, p: (p[1], h, 0))]),
        out_shape=[jax.ShapeDtypeStruct((D, N_CHIPS * CW), BF16), jax.ShapeDtypeStruct((N_CHIPS, RO, D), BF16)],
        compiler_params=_params(),
    )(pos, w_in, w_out, after)


def mod_part(pos, c_all, w_ada, b_my, after):
    L, D, CW = w_ada.shape

    def body(pos_ref, c_ref, w_ref, b_ref, after_ref, o_ref):
        cv = c_ref[...]
        ca = (cv * jax.nn.sigmoid(cv)).astype(BF16)
        o_ref[...] = jnp.dot(ca, w_ref[0].astype(BF16), preferred_element_type=F32) + b_ref[0]

    return pl.pallas_call(
        body, name="mod_part",
        grid_spec=pltpu.PrefetchScalarGridSpec(
            num_scalar_prefetch=1, grid=(L,),
            in_specs=[pl.BlockSpec((N_DEV, D), lambda l, p: (0, 0)),
                      pl.BlockSpec((1, D, CW), lambda l, p: (l, 0, 0)),
                      pl.BlockSpec((1, 1, CW), lambda l, p: (l, 0, 0)), ANY],
            out_specs=pl.BlockSpec((None, None, N_DEV, CW), lambda l, p: (p[1], l, 0, 0))),
        out_shape=jax.ShapeDtypeStruct((N_CHIPS, L, N_DEV, CW), F32),
        compiler_params=_params(VMEM_BIG),
    )(pos, c_all, w_ada, b_my.reshape(L, 1, CW), after)


def _mod_row(l, k, D):
    return pl.BlockSpec((None, None, 1, D), lambda *_: (l, k, 0, 0))


def _layer_row(l, D):
    return pl.BlockSpec((None, 1, D), lambda *_: (l, 0, 0))


def proj_fwd(x, mod4, g_pre3, wg, l):
    T, D = x.shape
    NC = wg.shape[1]
    NB = N_CHIPS
    CW = NC // NB
    tm = 512

    def body(x_ref, sh_ref, sc_ref, g_ref, w_ref, o_ref):
        xv = x_ref[...]
        h = (xv * _rms(xv) * g_ref[...]) * (1.0 + sc_ref[...]) + sh_ref[...]
        hb = h.astype(BF16)
        for j in range(NB):
            cols = slice(j * CW, (j + 1) * CW)
            o_ref[:, cols] = jnp.dot(hb, w_ref[:, cols], preferred_element_type=F32).astype(BF16)

    return pl.pallas_call(
        body, name="proj_fwd", grid=(T // tm,),
        in_specs=[pl.BlockSpec((tm, D), lambda i: (i, 0)), _mod_row(l, 0, D), _mod_row(l, 1, D), _layer_row(l, D),
                  pl.BlockSpec((D, NC), lambda i: (0, 0))],
        out_specs=pl.BlockSpec((tm, NC), lambda i: (i, 0)),
        out_shape=jax.ShapeDtypeStruct((T, NC), BF16),
        compiler_params=_params(VMEM_BIG),
    )(x, mod4, mod4, g_pre3, wg)


def conv_fwd(proj, wconv, l):
    T = proj.shape[0]
    R = R_CONV
    nblk = 4

    def body(u_ref, b_ref, c_ref, g_ref, w_ref, o_ref):
        w0 = w_ref[pl.ds(0, 1), :]
        w1 = w_ref[pl.ds(1, 1), :]
        w2 = w_ref[pl.ds(2, 1), :]

        def chunk(i, carry):
            r0 = pl.multiple_of(i * R, R)
            h0 = pl.multiple_of(jnp.maximum(r0 - HIST, 0), HIST)
            first = i == 0
            ca = _load_ext(c_ref, r0, h0, first, R) * _load_ext(u_ref, r0, h0, first, R)
            conv = w2 * ca[HIST:] + w1 * _shift_down(ca, 1, R) + w0 * _shift_down(ca, 2, R)
            g = g_ref[pl.ds(r0, R), :].astype(F32)
            b = b_ref[pl.ds(r0, R), :].astype(F32)
            o_ref[pl.ds(r0, R), :] = (b * conv * (g * _sigmoid(g))).astype(BF16)
            return carry

        lax.fori_loop(0, T // R, chunk, 0)

    def col(off):
        return pl.BlockSpec((T, LANES), lambda j: (0, j + off))

    return pl.pallas_call(
        body, name="conv_fwd", grid=(nblk,),
        in_specs=[col(0), col(4), col(8), col(12), pl.BlockSpec((None, None, 3, LANES), lambda j: (j, l, 0, 0))],
        out_specs=pl.BlockSpec((T, LANES), lambda j: (0, j)),
        out_shape=jax.ShapeDtypeStruct((T, nblk * LANES), BF16),
        compiler_params=_params(),
    )(proj, proj, proj, proj, wconv)


def _causal_window_sum(ext, w):
    s, k = ext, 1
    while k < w:
        s = s + pltpu.roll(s, k, 0)
        k *= 2
    return s


def _anticausal_window_sum(ext, w):
    s, k = ext, 1
    n = ext.shape[0]
    while k < w:
        s = s + pltpu.roll(s, n - k, 0)
        k *= 2
    return s


def _count(r0, rows, w):
    t = r0 + lax.broadcasted_iota(jnp.int32, (rows, LANES), 0)
    return jnp.minimum(t + 1, w).astype(F32)


def _pooled_loop(p_ref, pooled_s, w, T):
    R = R_POOL

    def chunk(i, carry):
        r0 = pl.multiple_of(i * R, R)
        h0 = pl.multiple_of(jnp.maximum(r0 - HIST, 0), HIST)
        ext = _load_ext(p_ref, r0, h0, i == 0, R)
        ws = _causal_window_sum(ext, w)[HIST:]
        pooled_s[pl.ds(r0, R), :] = (ws / _count(r0, R, w) - ext[HIST:]).astype(BF16)
        return carry

    lax.fori_loop(0, T // R, chunk, 0)


def _pool_w_spec(l):
    return pl.BlockSpec((None, None, LANES, LANES), lambda j: (l, j, 0, 0))


def _pool_s_spec(l):
    return pl.BlockSpec((None, 1, LANES), lambda j: (l, 0, j))


def pool_fwd(proj, wpool, pscale3, l):
    T = proj.shape[0]
    R = R_POOL
    ngrp = len(POOL_WINDOWS)

    def body(p_ref, g_ref, w_ref, s_ref, o_ref, pooled_s, mixed_s):
        grp = pl.program_id(0)

        def group(w):
            _pooled_loop(p_ref, pooled_s, w, T)
            mixed_s[...] = jnp.dot(pooled_s[...], w_ref[...].astype(BF16), preferred_element_type=F32)
            sc = s_ref[...]

            def chunk(i, carry):
                r0 = pl.multiple_of(i * R, R)
                g = g_ref[pl.ds(r0, R), :].astype(F32)
                o_ref[pl.ds(r0, R), :] = (mixed_s[pl.ds(r0, R), :] * sc * (g * _sigmoid(g))).astype(BF16)
                return carry

            lax.fori_loop(0, T // R, chunk, 0)

        for k, w in enumerate(POOL_WINDOWS):
            pl.when(grp == k)(functools.partial(group, w))

    return pl.pallas_call(
        body, name="pool_fwd", grid=(ngrp,),
        in_specs=[pl.BlockSpec((T, LANES), lambda j: (0, j + 16)), pl.BlockSpec((T, LANES), lambda j: (0, j + 20)),
                  _pool_w_spec(l), _pool_s_spec(l)],
        out_specs=pl.BlockSpec((T, LANES), lambda j: (0, j)),
        out_shape=jax.ShapeDtypeStruct((T, ngrp * LANES), BF16),
        scratch_shapes=[pltpu.VMEM((T, LANES), BF16), pltpu.VMEM((T, LANES), F32)],
        compiler_params=_params(),
    )(proj, proj, wpool, pscale3)


def out_fwd(ya, yp, wo, x, mod4, g_post3, l, after):
    T, D = x.shape
    H = ya.shape[1]
    tm = 512

    def body(ya_ref, yp_ref, wo_ref, x_ref, gt_ref, g_ref, after_ref, xn_ref, y_ref):
        y = (jnp.dot(ya_ref[...], wo_ref[0:H, :], preferred_element_type=F32)
             + jnp.dot(yp_ref[...], wo_ref[H:2 * H, :], preferred_element_type=F32))
        xn_ref[...] = x_ref[...] + gt_ref[...] * (y * _rms(y) * g_ref[...])
        y_ref[...] = y.astype(BF16)

    tile = pl.BlockSpec((tm, D), lambda i: (i, 0))
    half = pl.BlockSpec((tm, H), lambda i: (i, 0))
    return pl.pallas_call(
        body, name="out_fwd", grid=(T // tm,),
        in_specs=[half, half, pl.BlockSpec((2 * H, D), lambda i: (0, 0)), tile, _mod_row(l, 2, D), _layer_row(l, D),
                  ANY],
        out_specs=[tile, tile],
        out_shape=[jax.ShapeDtypeStruct((T, D), F32), jax.ShapeDtypeStruct((T, D), BF16)],
        compiler_params=_params(VMEM_BIG),
    )(ya, yp, wo, x, mod4, g_post3, after)


def out_fwd_loss(ya, yp, wo, x, mod4, g_post3, l, target):
    T, D = x.shape
    H = ya.shape[1]
    tm = 512
    nt = T // tm

    def body(ya_ref, yp_ref, wo_ref, x_ref, gt_ref, g_ref, t_ref, dx_ref, y_ref, l_ref, acc):
        i = pl.program_id(0)

        @pl.when(i == 0)
        def _():
            acc[...] = jnp.zeros_like(acc)

        y = (jnp.dot(ya_ref[...], wo_ref[0:H, :], preferred_element_type=F32)
             + jnp.dot(yp_ref[...], wo_ref[H:2 * H, :], preferred_element_type=F32))
        y_ref[...] = y.astype(BF16)
        d = (x_ref[...] + gt_ref[...] * (y * _rms(y) * g_ref[...])) - t_ref[...]
        dx_ref[...] = d * (1.0 / D)
        acc[...] += _colsum8(d * d)

        @pl.when(i == nt - 1)
        def _():
            l_ref[...] = jnp.zeros_like(l_ref) + jnp.sum(acc[...]) * (0.5 / D)

    tile = pl.BlockSpec((tm, D), lambda i: (i, 0))
    half = pl.BlockSpec((tm, H), lambda i: (i, 0))
    return pl.pallas_call(
        body, name="out_fwd_loss", grid=(nt,),
        in_specs=[half, half, pl.BlockSpec((2 * H, D), lambda i: (0, 0)), tile, _mod_row(l, 2, D), _layer_row(l, D),
                  tile],
        out_specs=[tile, tile, pl.BlockSpec((SUBLANES, LANES), lambda i: (0, 0))],
        out_shape=[jax.ShapeDtypeStruct((T, D), F32), jax.ShapeDtypeStruct((T, D), BF16),
                   jax.ShapeDtypeStruct((SUBLANES, LANES), F32)],
        scratch_shapes=[pltpu.VMEM((SUBLANES, D), F32)],
        compiler_params=_params(VMEM_BIG),
    )(ya, yp, wo, x, mod4, g_post3, target)


def out_bwd(dx, y, ya, yp, wo, mod4, g_post3, l, after):
    T, D = dx.shape
    H = ya.shape[1]
    tm = 512
    nt = T // tm

    def body(dx_ref, y_ref, ya_ref, yp_ref, wo_ref, gt_ref, g_ref, after_ref,
             dya_ref, dyp_ref, dwo_ref, dgt_ref, dg_ref, acc_w, acc_gt, acc_g):
        i = pl.program_id(0)

        @pl.when(i == 0)
        def _():
            acc_w[...] = jnp.zeros_like(acc_w)
            acc_gt[...] = jnp.zeros_like(acc_gt)
            acc_g[...] = jnp.zeros_like(acc_g)

        yv = y_ref[...].astype(F32)
        dxv = dx_ref[...]
        g = g_ref[...]
        r = _rms(yv)
        yn = yv * r
        acc_gt[...] += _colsum8(dxv * (yn * g))
        dn = dxv * gt_ref[...]
        acc_g[...] += _colsum8(dn * yn)
        a = dn * g
        dy = r * (a - yn * jnp.mean(a * yn, axis=-1, keepdims=True))
        dyb = dy.astype(BF16)
        dyc = lax.dot_general(dyb, wo_ref[...], NT, preferred_element_type=F32)
        dya_ref[...] = dyc[:, 0:H].astype(BF16)
        dyp_ref[...] = dyc[:, H:2 * H].astype(BF16)
        acc_w[0:H, :] += lax.dot_general(ya_ref[...], dyb, TN, preferred_element_type=F32)
        acc_w[H:2 * H, :] += lax.dot_general(yp_ref[...], dyb, TN, preferred_element_type=F32)

        @pl.when(i == nt - 1)
        def _():
            dwo_ref[...] = acc_w[...].astype(BF16)
            dgt_ref[...] = jnp.sum(acc_gt[...], axis=0, keepdims=True)
            dg_ref[...] = jnp.sum(acc_g[...], axis=0, keepdims=True)

    row = pl.BlockSpec((1, D), lambda i: (0, 0))
    tile = pl.BlockSpec((tm, D), lambda i: (i, 0))
    half = pl.BlockSpec((tm, H), lambda i: (i, 0))
    full = pl.BlockSpec((2 * H, D), lambda i: (0, 0))
    return pl.pallas_call(
        body, name="out_bwd", grid=(nt,),
        in_specs=[tile, tile, half, half, full, _mod_row(l, 2, D), _layer_row(l, D), ANY],
        out_specs=[half, half, full, row, row],
        out_shape=[jax.ShapeDtypeStruct((T, H), BF16), jax.ShapeDtypeStruct((T, H), BF16),
                   jax.ShapeDtypeStruct((2 * H, D), BF16),
                   jax.ShapeDtypeStruct((1, D), F32), jax.ShapeDtypeStruct((1, D), F32)],
        scratch_shapes=[pltpu.VMEM((2 * H, D), F32), pltpu.VMEM((SUBLANES, D), F32), pltpu.VMEM((SUBLANES, D), F32)],
        compiler_params=_params(VMEM_BIG),
    )(dx, y, ya, yp, wo, mod4, g_post3, after)


def conv_bwd(proj, dya, wconv, l):
    T = proj.shape[0]
    R = R_CONV
    nblk = 4
    nchunk = T // R

    def body(u_ref, b_ref, c_ref, g_ref, dy_ref, w_ref, du_ref, db_ref, dc_ref, dg_ref, dw_ref):
        w0 = w_ref[pl.ds(0, 1), :]
        w1 = w_ref[pl.ds(1, 1), :]
        w2 = w_ref[pl.ds(2, 1), :]

        def chunk(k, carry):
            head, a0, a1, a2 = carry
            i = nchunk - 1 - k
            r0 = pl.multiple_of(i * R, R)
            h0 = pl.multiple_of(jnp.maximum(r0 - HIST, 0), HIST)
            first = i == 0
            ue = _load_ext(u_ref, r0, h0, first, R)
            ce = _load_ext(c_ref, r0, h0, first, R)
            ca = ce * ue
            ca0 = ca[HIST:]
            ca1 = _shift_down(ca, 1, R)
            ca2 = _shift_down(ca, 2, R)
            conv = w2 * ca0 + w1 * ca1 + w0 * ca2
            g = g_ref[pl.ds(r0, R), :].astype(F32)
            b = b_ref[pl.ds(r0, R), :].astype(F32)
            dy = dy_ref[pl.ds(r0, R), :].astype(F32)
            sg = _sigmoid(g)
            sl = g * sg
            t = dy * conv
            db_ref[pl.ds(r0, R), :] = (t * sl).astype(BF16)
            dg_ref[pl.ds(r0, R), :] = (t * b * (sg * (1.0 + g * (1.0 - sg)))).astype(BF16)
            dconv = dy * b * sl
            a2 = a2 + _colsum8(dconv * ca0)
            a1 = a1 + _colsum8(dconv * ca1)
            a0 = a0 + _colsum8(dconv * ca2)
            e = jnp.concatenate([dconv, head], axis=0)
            dca = w2 * dconv + w1 * _shift_up(e, 1, R) + w0 * _shift_up(e, 2, R)
            du_ref[pl.ds(r0, R), :] = (dca * ce[HIST:]).astype(BF16)
            dc_ref[pl.ds(r0, R), :] = (dca * ue[HIST:]).astype(BF16)
            return dconv[0:SUBLANES], a0, a1, a2

        z = jnp.zeros((SUBLANES, LANES), F32)
        _, a0, a1, a2 = lax.fori_loop(0, nchunk, chunk, (z, z, z, z))
        dw_ref[pl.ds(0, 1), :] = jnp.sum(a0, axis=0, keepdims=True)
        dw_ref[pl.ds(1, 1), :] = jnp.sum(a1, axis=0, keepdims=True)
        dw_ref[pl.ds(2, 1), :] = jnp.sum(a2, axis=0, keepdims=True)

    def col(off):
        return pl.BlockSpec((T, LANES), lambda j: (0, j + off))

    sec = jax.ShapeDtypeStruct((T, nblk * LANES), BF16)
    return pl.pallas_call(
        body, name="conv_bwd", grid=(nblk,),
        in_specs=[col(0), col(4), col(8), col(12), col(0), pl.BlockSpec((None, None, 3, LANES), lambda j: (j, l, 0, 0))],
        out_specs=[col(0), col(0), col(0), col(0), pl.BlockSpec((None, 3, LANES), lambda j: (j, 0, 0))],
        out_shape=[sec, sec, sec, sec, jax.ShapeDtypeStruct((nblk, 3, LANES), F32)],
        compiler_params=_params(),
    )(proj, proj, proj, proj, dya, wconv)


def pool_bwd(proj, dyp, wpool, pscale3, l, after):
    T = proj.shape[0]
    R = R_POOL
    ngrp = len(POOL_WINDOWS)
    nchunk = T // R

    def body(p_ref, g_ref, dy_ref, w_ref, s_ref, after_ref, du_ref, dg_ref, dw_ref, ds_ref,
             pooled_s, mixed_s, dmix_s, dpool_s):
        grp = pl.program_id(0)

        def group(w):
            wb = w_ref[...].astype(BF16)
            _pooled_loop(p_ref, pooled_s, w, T)
            mixed_s[...] = jnp.dot(pooled_s[...], wb, preferred_element_type=F32)
            sc = s_ref[...]

            def gate_chunk(i, acc):
                r0 = pl.multiple_of(i * R, R)
                g = g_ref[pl.ds(r0, R), :].astype(F32)
                dy = dy_ref[pl.ds(r0, R), :].astype(F32)
                mixed = mixed_s[pl.ds(r0, R), :]
                sg = _sigmoid(g)
                dg_ref[pl.ds(r0, R), :] = (dy * mixed * sc * (sg * (1.0 + g * (1.0 - sg)))).astype(BF16)
                dms = dy * (g * sg)
                dmix_s[pl.ds(r0, R), :] = (dms * sc).astype(BF16)
                return acc + _colsum8(dms * mixed)

            acc = lax.fori_loop(0, nchunk, gate_chunk, jnp.zeros((SUBLANES, LANES), F32))
            ds_ref[...] = jnp.sum(acc, axis=0, keepdims=True)
            dpool_s[pl.ds(0, T), :] = lax.dot_general(dmix_s[...], wb, NT, preferred_element_type=F32)
            dpool_s[pl.ds(T, HIST), :] = jnp.zeros((HIST, LANES), F32)
            dw_ref[...] = lax.dot_general(pooled_s[...], dmix_s[...], TN, preferred_element_type=F32).astype(BF16)

            def back_chunk(i, carry):
                r0 = pl.multiple_of(i * R, R)
                dpe = dpool_s[pl.ds(r0, R + HIST), :]
                e = dpe / _count(r0, R + HIST, w)
                du_ref[pl.ds(r0, R), :] = (_anticausal_window_sum(e, w)[0:R] - dpe[0:R]).astype(BF16)
                return carry

            lax.fori_loop(0, nchunk, back_chunk, 0)

        for k, w in enumerate(POOL_WINDOWS):
            pl.when(grp == k)(functools.partial(group, w))

    def col(off):
        return pl.BlockSpec((T, LANES), lambda j: (0, j + off))

    sec = jax.ShapeDtypeStruct((T, ngrp * LANES), BF16)
    wspec = pl.BlockSpec((None, LANES, LANES), lambda j: (j, 0, 0))
    sspec = pl.BlockSpec((1, LANES), lambda j: (0, j))
    return pl.pallas_call(
        body, name="pool_bwd", grid=(ngrp,),
        in_specs=[col(16), col(20), col(0), _pool_w_spec(l), _pool_s_spec(l), ANY],
        out_specs=[col(0), col(0), wspec, sspec],
        out_shape=[sec, sec, jax.ShapeDtypeStruct((ngrp, LANES, LANES), BF16),
                   jax.ShapeDtypeStruct((1, ngrp * LANES), F32)],
        scratch_shapes=[pltpu.VMEM((T, LANES), BF16), pltpu.VMEM((T, LANES), F32),
                        pltpu.VMEM((T, LANES), BF16), pltpu.VMEM((T + HIST, LANES), F32)],
        compiler_params=_params(),
    )(proj, proj, dyp, wpool, pscale3, after)


def in_bwd(dsecs, wg, x, dxo, mod4, g_pre3, l):
    T, D = x.shape
    NB = N_CHIPS
    CW = wg.shape[1] // NB
    SW = dsecs[0].shape[1]
    nsec = len(dsecs)
    PW = 256
    assert SW % PW == 0 and CW % PW == 0
    tm = 256
    nt = T // tm

    def body(*refs):
        d_refs = refs[0:nsec]
        w_ref, x_ref, dxo_ref, sh_ref, sc_ref, g_ref = refs[nsec:nsec + 6]
        dxi_ref, dw_ref, dsh_ref, dsc_ref, dg_ref = refs[nsec + 6:nsec + 11]
        acc_w, acc_sh, acc_sc, acc_g = refs[nsec + 11:]
        i = pl.program_id(0)

        @pl.when(i == 0)
        def _():
            acc_w[...] = jnp.zeros_like(acc_w)
            acc_sh[...] = jnp.zeros_like(acc_sh)
            acc_sc[...] = jnp.zeros_like(acc_sc)
            acc_g[...] = jnp.zeros_like(acc_g)

        xv = x_ref[...]
        g = g_ref[...]
        r = _rms(xv)
        xh = xv * r
        n = xh * g
        sc1 = 1.0 + sc_ref[...]
        hb = (n * sc1 + sh_ref[...]).astype(BF16)
        dh = lax.dot_general(d_refs[0][...], w_ref[:, 0:SW], NT, preferred_element_type=F32)
        for s in range(1, nsec):
            dh = dh + lax.dot_general(d_refs[s][...], w_ref[:, s * SW:(s + 1) * SW], NT, preferred_element_type=F32)
        for p in range(nsec * SW // PW):
            col = p * PW
            s, so = col // SW, col % SW
            j, jo = col // CW, col % CW
            acc_w[j, :, jo:jo + PW] += lax.dot_general(hb, d_refs[s][:, so:so + PW], TN, preferred_element_type=F32)
        acc_sh[...] += _colsum8(dh)
        acc_sc[...] += _colsum8(dh * n)
        dnp = dh * sc1
        acc_g[...] += _colsum8(dnp * xh)
        a = dnp * g
        dxi_ref[...] = dxo_ref[...] + r * (a - xh * jnp.mean(a * xh, axis=-1, keepdims=True))

        @pl.when(i == nt - 1)
        def _():
            dw_ref[...] = acc_w[...].astype(BF16)
            dsh_ref[...] = jnp.sum(acc_sh[...], axis=0, keepdims=True)
            dsc_ref[...] = jnp.sum(acc_sc[...], axis=0, keepdims=True)
            dg_ref[...] = jnp.sum(acc_g[...], axis=0, keepdims=True)

    row = pl.BlockSpec((1, D), lambda i: (0, 0))
    tile = pl.BlockSpec((tm, D), lambda i: (i, 0))
    sect = pl.BlockSpec((tm, SW), lambda i: (i, 0))
    rowshape = jax.ShapeDtypeStruct((1, D), F32)
    return pl.pallas_call(
        body, name="in_bwd", grid=(nt,),
        in_specs=[sect] * nsec + [pl.BlockSpec((D, NB * CW), lambda i: (0, 0)), tile, tile,
                                  _mod_row(l, 0, D), _mod_row(l, 1, D), _layer_row(l, D)],
        out_specs=[tile, pl.BlockSpec((NB, D, CW), lambda i: (0, 0, 0)), row, row, row],
        out_shape=[jax.ShapeDtypeStruct((T, D), F32), jax.ShapeDtypeStruct((NB, D, CW), BF16),
                   rowshape, rowshape, rowshape],
        scratch_shapes=[pltpu.VMEM((NB, D, CW), F32),
                        pltpu.VMEM((SUBLANES, D), F32), pltpu.VMEM((SUBLANES, D), F32), pltpu.VMEM((SUBLANES, D), F32)],
        compiler_params=_params(VMEM_BIG),
    )(*dsecs, wg, x, dxo, mod4, mod4, g_pre3)


def _rcopy(src, dst, ssem, rsem, dev):
    return pltpu.make_async_remote_copy(src_ref=src, dst_ref=dst, send_sem=ssem, recv_sem=rsem,
                                        device_id=dev, device_id_type=MESH)


def _peers7(x, y, c):
    out = []
    for m in range(1, N_DEV):
        bx, by, bc = (m >> 2) & 1, (m >> 1) & 1, m & 1
        out.append(((1 - x) if bx else x, (1 - y) if by else y, (1 - c) if bc else c))
    return out


HBM = pl.BlockSpec(memory_space=pltpu.HBM)
SEM = pl.BlockSpec(memory_space=pltpu.SEMAPHORE)
SPLIT = pltpu.CompilerParams(has_side_effects=pltpu.SideEffectType.DATAFLOW_SIDE_EFFECTING)


def _hbm(a):
    return pltpu.with_memory_space_constraint(a, pltpu.HBM)


def _chips(x, y):
    return [(1 - x, y), (x, 1 - y), (1 - x, 1 - y)]


SIBLING_BARRIER_ID = 0


def xchg_start(name, bufs, n_copies, plan, sibling_only=False, after=()):
    n = len(bufs)
    after = list(after)

    def body(*refs):
        ssem, rsem, token = refs[n + len(after)], refs[n + len(after) + 1], refs[-1]
        x, y, c = _me()
        if sibling_only:
            barrier = pltpu.get_barrier_semaphore()
            pl.semaphore_signal(barrier, inc=1, device_id=(x, y, 1 - c), device_id_type=MESH)
            pl.semaphore_wait(barrier, 1)
        copies = plan(refs[0:n], x, y, c)
        assert len(copies) == n_copies
        for k, (src, dst, peer, _) in enumerate(copies):
            _rcopy(src, dst, ssem.at[k], rsem.at[k], peer).start()
        token[...] = jnp.zeros_like(token)

    params = dict(has_side_effects=pltpu.SideEffectType.DATAFLOW_SIDE_EFFECTING)
    if sibling_only:
        params["collective_id"] = SIBLING_BARRIER_ID
    outs = pl.pallas_call(
        body, name=name,
        in_specs=[HBM] * n + [ANY] * len(after),
        out_specs=[SEM, SEM] + [HBM] * n + [pl.BlockSpec(memory_space=pltpu.VMEM)],
        out_shape=([pltpu.SemaphoreType.DMA((n_copies,))] * 2 + [pltpu.HBM(b.shape, b.dtype) for b in bufs]
                   + [jax.ShapeDtypeStruct((SUBLANES, LANES), F32)]),
        input_output_aliases={a: 2 + a for a in range(n)},
        compiler_params=pltpu.CompilerParams(**params),
    )(*[_hbm(b) for b in bufs], *after)
    return outs[0], outs[1], list(outs[2:2 + n]), outs[-1]


def xchg_wait(name, bufs, ssem, rsem, n_copies, plan, after):
    n = len(bufs)
    after = list(after)

    def body(*refs):
        ssem_ref, rsem_ref = refs[n], refs[n + 1]
        copies = plan(refs[0:n], *_me())
        assert len(copies) == n_copies
        for k, (src, _, peer, land) in enumerate(copies):
            cp = _rcopy(src, land, ssem_ref.at[k], rsem_ref.at[k], peer)
            cp.wait_send()
            cp.wait_recv()

    outs = pl.pallas_call(
        body, name=name,
        in_specs=[HBM] * n + [SEM, SEM] + [ANY] * len(after), out_specs=[HBM] * n,
        out_shape=[pltpu.HBM(b.shape, b.dtype) for b in bufs],
        input_output_aliases={a: a for a in range(n)},
        compiler_params=SPLIT,
    )(*bufs, ssem, rsem, *after)
    return list(outs)


def _shard_half(buf, chip, half):
    if len(buf.shape) == 2:
        h, w = buf.shape[0] // 2, buf.shape[1] // N_CHIPS
        return buf.at[pl.ds(half * h, h), pl.ds(chip * w, w)]
    h = buf.shape[1] // 2
    return buf.at[chip, pl.ds(half * h, h)]


def plan_gather(refs, x, y, c):
    out = []
    for (px, py) in _chips(x, y):
        for buf in refs:
            own = _shard_half(buf, 2 * x + y, c)
            out.append((own, own, (px, py, c), _shard_half(buf, 2 * px + py, c)))
    return out


def plan_forward(refs, x, y, c):
    out = []
    for (px, py) in _chips(x, y):
        for buf in refs:
            landed = _shard_half(buf, 2 * px + py, c)
            out.append((landed, landed, (x, y, 1 - c), _shard_half(buf, 2 * px + py, 1 - c)))
    return out


def plan_sibling(refs, x, y, c):
    n = len(refs) // 2
    out = []
    for a in range(n):
        h = refs[a].shape[1] // 2
        out.append((refs[a].at[:, pl.ds((1 - c) * h, h)], refs[n + a], (x, y, 1 - c), refs[n + a]))
    return out


def plan_chip(refs, x, y, c):
    n = len(refs) // 2
    out = []
    for j, (px, py) in enumerate(_chips(x, y)):
        for a in range(n):
            out.append((refs[a].at[2 * px + py], refs[n + a].at[j], (px, py, c), refs[n + a].at[j]))
    return out


def plan_mod(refs, x, y, c):
    (mods,) = refs
    mine = mods.at[2 * x + y]
    return [(mine, mine, (px, py, c), mods.at[2 * px + py]) for (px, py) in _chips(x, y)]


def plan_pack(refs, x, y, c):
    (packs,) = refs
    mine = packs.at[4 * x + 2 * y + c]
    return [(mine, mine, peer, packs.at[4 * peer[0] + 2 * peer[1] + peer[2]]) for peer in _peers7(x, y, c)]


def plan_spread(layers, wp_layers):
    def plan(refs, x, y, c):
        gi, go, gp = refs
        hD, hR, hP = gi.shape[1] // 2, go.shape[1] // 2, gp.shape[2] // 2
        sib = (x, y, 1 - c)
        out = []
        for l in layers:
            mine = gi.at[l, pl.ds(c * hD, hD)]
            out.append((mine, mine, sib, gi.at[l, pl.ds((1 - c) * hD, hD)]))
            mine = go.at[l, pl.ds(c * hR, hR)]
            out.append((mine, mine, sib, go.at[l, pl.ds((1 - c) * hR, hR)]))
        for l in wp_layers:
            mine = gp.at[l, 2 * x + y, pl.ds(c * hP, hP)]
            for peer in _peers7(x, y, c):
                out.append((mine, mine, peer, gp.at[l, 2 * peer[0] + peer[1], pl.ds(peer[2] * hP, hP)]))
        return out

    return plan


def gather_small(c8, wc, token):
    def body(c_ref, wc_ref, token_ref, call, wcall, ssem, rsem, lsem):
        x, y, c = _me()
        myc = 2 * x + y
        me_lin = 4 * x + 2 * y + c
        me = (x, y, c)
        local = [pltpu.make_async_copy(c_ref, call.at[me_lin], lsem.at[0]),
                 pltpu.make_async_copy(wc_ref, wcall.at[myc], lsem.at[1])]
        for cp in local:
            cp.start()
        sends, recvs = [], []
        for m, peer in enumerate(_peers7(x, y, c)):
            plin = 4 * peer[0] + 2 * peer[1] + peer[2]
            sends.append(_rcopy(c_ref, call.at[me_lin], ssem.at[m], rsem.at[m], peer))
            recvs.append(_rcopy(call.at[plin], call.at[plin], ssem.at[m], rsem.at[m], me))
        for j, (px, py) in enumerate([(1 - x, y), (x, 1 - y), (1 - x, 1 - y)]):
            pc = 2 * px + py
            sends.append(_rcopy(wc_ref, wcall.at[myc], ssem.at[7 + j], rsem.at[7 + j], (px, py, c)))
            recvs.append(_rcopy(wcall.at[pc], wcall.at[pc], ssem.at[7 + j], rsem.at[7 + j], me))
        for cp in sends:
            cp.start()
        for cp in recvs:
            cp.wait_recv()
        for cp in sends:
            cp.wait_send()
        for cp in local:
            cp.wait()

    return pl.pallas_call(
        body, name="gather_small",
        in_specs=[ANY] * 3, out_specs=[ANY] * 2,
        out_shape=[jax.ShapeDtypeStruct((N_DEV, SUBLANES, LANES), F32),
                   jax.ShapeDtypeStruct((N_CHIPS, wc.shape[0], 3, LANES), F32)],
        scratch_shapes=[pltpu.SemaphoreType.DMA((10,)), pltpu.SemaphoreType.DMA((10,)), pltpu.SemaphoreType.DMA((2,))],
        compiler_params=_params(n_grid=0),
    )(c8, wc, token)


def spread_now(gi, go, gp, layers, wp_layers):
    plan = plan_spread(layers, wp_layers)
    n = 2 * len(layers) + 7 * len(wp_layers)

    def body(gi_in, go_in, gp_in, gi, go, gp, ssem, rsem):
        copies = plan((gi, go, gp), *_me())
        me = _me()
        sends = [_rcopy(src, dst, ssem.at[k], rsem.at[k], peer) for k, (src, dst, peer, _) in enumerate(copies)]
        for cp in sends:
            cp.start()
        for k, (_, _, _, land) in enumerate(copies):
            _rcopy(land, land, ssem.at[k], rsem.at[k], me).wait_recv()
        for cp in sends:
            cp.wait_send()

    return pl.pallas_call(
        body, name="spread_now",
        in_specs=[ANY] * 3, out_specs=[ANY] * 3,
        out_shape=[jax.ShapeDtypeStruct(a.shape, a.dtype) for a in (gi, go, gp)],
        input_output_aliases={0: 0, 1: 1, 2: 2},
        scratch_shapes=[pltpu.SemaphoreType.DMA((n,)), pltpu.SemaphoreType.DMA((n,))],
        compiler_params=_params(n_grid=0),
    )(gi, go, gp)


def add_sibling(cidx, mine, sib):
    def body(c_ref, *refs):
        for a in range(3):
            m, s, o = refs[a], refs[3 + a], refs[6 + a]
            o[...] = (m[...].astype(F32) + s[...].astype(F32)).astype(BF16)

    def mine_spec(a):
        h = a.shape[1] // 2
        return pl.BlockSpec((None, h, a.shape[2]), lambda j, c_ref: (j, c_ref[0], 0))

    def sib_spec(a):
        return pl.BlockSpec((None,) + a.shape[1:], lambda j, c_ref: (j, 0, 0))

    return pl.pallas_call(
        body, name="add_sibling",
        grid_spec=pltpu.PrefetchScalarGridSpec(
            num_scalar_prefetch=1, grid=(N_CHIPS,),
            in_specs=[mine_spec(a) for a in mine] + [sib_spec(a) for a in sib],
            out_specs=[sib_spec(a) for a in sib]),
        out_shape=[jax.ShapeDtypeStruct(a.shape, BF16) for a in sib],
        compiler_params=_params(VMEM_BIG),
    )(cidx, *mine, *sib)


def sum_chips(pos, own, rb, acc, l, shapes):
    nq = 4
    n_in = 6 + (3 if acc is not None else 0)

    def body(pos_ref, *refs):
        for a in range(3):
            m, b, o = refs[a], refs[3 + a], refs[n_in + a]
            s = m[...].astype(F32)
            for j in range(3):
                s = s + b[j].astype(F32)
            o[...] = s

    def own_spec(a):
        return pl.BlockSpec((None, a.shape[1] // nq, a.shape[2]), lambda q, p: (p[1], q, 0))

    def rb_spec(a):
        return pl.BlockSpec((3, a.shape[1] // nq, a.shape[2]), lambda q, p: (0, q, 0))

    hi, ho, hp = own[0].shape[1] // nq, own[1].shape[1] // nq, own[2].shape[1] // nq
    out_specs = [pl.BlockSpec((None, hi, shapes[0][2]), lambda q, p: (l, p[0] * nq + q, 0)),
                 pl.BlockSpec((None, ho, shapes[1][2]), lambda q, p: (l, p[0] * nq + q, 0)),
                 pl.BlockSpec((None, None, hp, LANES), lambda q, p: (l, p[1], p[0] * nq + q, 0))]
    in_specs = [own_spec(a) for a in own] + [rb_spec(a) for a in rb]
    args = list(own) + list(rb)
    aliases = {}
    if acc is not None:
        in_specs += [ANY] * 3
        args += list(acc)
        aliases = {7: 0, 8: 1, 9: 2}
    return pl.pallas_call(
        body, name="sum_chips",
        grid_spec=pltpu.PrefetchScalarGridSpec(num_scalar_prefetch=1, grid=(nq,), in_specs=in_specs, out_specs=out_specs),
        out_shape=[jax.ShapeDtypeStruct(s, F32) for s in shapes],
        input_output_aliases=aliases,
        compiler_params=_params(VMEM_BIG),
    )(pos, *args)


def pack_small(pos, per_layer, loss_blk):
    L = len(per_layer)
    D = per_layer[0][0].shape[1]

    def body(pos_ref, *refs):
        o = refs[-1]
        lb = refs[-2]
        o[...] = jnp.zeros_like(o)
        for l in range(L):
            dgpre, dgpost, dsh, dsc, dgt, dps, dwc = refs[7 * l:7 * l + 7]
            base = SUBLANES * l
            for r, src in enumerate((dgpre, dgpost, dsh, dsc, dgt)):
                o[pl.ds(base + r, 1), :] = src[...]
            o[pl.ds(base + 5, 1), 0:dps.shape[1]] = dps[...]
            for j in range(dwc.shape[0]):
                for k in range(3):
                    idx = 3 * j + k
                    o[pl.ds(base + 6 + idx // 8, 1), (idx % 8) * LANES:(idx % 8 + 1) * LANES] = dwc[j, pl.ds(k, 1), :]
        o[pl.ds(5, 1), 4 * LANES:5 * LANES] = lb[pl.ds(0, 1), :]

    flat = [a for layer in per_layer for a in layer] + [loss_blk]

    def whole(a):
        return pl.BlockSpec(a.shape, lambda i, p: (0,) * a.ndim)

    return pl.pallas_call(
        body, name="pack_small",
        grid_spec=pltpu.PrefetchScalarGridSpec(
            num_scalar_prefetch=1, grid=(1,), in_specs=[whole(a) for a in flat],
            out_specs=pl.BlockSpec((None, L * SUBLANES, D), lambda i, p: (p[2], 0, 0))),
        out_shape=jax.ShapeDtypeStruct((N_DEV, L * SUBLANES, D), F32),
        compiler_params=_params(),
    )(pos, *flat)


def small_update(pos, packs, params, moments_m, moments_v):
    n = len(params)
    L, D = params[1].shape
    PS = params[3].shape[1]

    def body(pos_ref, p_ref, *refs):
        ws, ms, vs = refs[0:n], refs[n:2 * n], refs[2 * n:3 * n]
        loss_ref = refs[3 * n]
        outs = [refs[3 * n + 1 + 4 * t:3 * n + 5 + 4 * t] for t in range(n)]
        summed = refs[-1]
        s = p_ref[0]
        for d in range(1, N_DEV):
            s = s + p_ref[d]
        summed[...] = s
        loss_ref[...] = summed[pl.ds(5, 1), 4 * LANES:5 * LANES]
        chip = pos_ref[1]

        def update(t, idx, g):
            d, mm, vv = _adamw_math(ws[t][idx], g, ms[t][idx], vs[t][idx])
            g_ref, d_ref, mo_ref, vo_ref = outs[t]
            g_ref[idx] = g
            d_ref[idx] = d
            mo_ref[idx] = mm
            vo_ref[idx] = vv

        for l in range(L):
            base = SUBLANES * l
            row = pl.ds(l, 1)
            for k in range(3):
                update(0, (row, slice(k * D, (k + 1) * D)), summed[pl.ds(base + 2 + k, 1), :])
            update(1, (row, slice(None)), summed[pl.ds(base, 1), :])
            update(2, (row, slice(None)), summed[pl.ds(base + 1, 1), :])
            update(3, (row, slice(None)), summed[pl.ds(base + 5, 1), 0:PS])
            for k in range(3):
                g = None
                for j in range(N_CHIPS):
                    idx = 3 * j + k
                    cand = summed[pl.ds(base + 6 + idx // 8, 1), (idx % 8) * LANES:(idx % 8 + 1) * LANES]
                    g = cand if g is None else jnp.where(chip == j, cand, g)
                update(4, (l, pl.ds(k, 1), slice(None)), g)

    def whole(a):
        return pl.BlockSpec(a.shape, lambda i, p: (0,) * a.ndim)

    ins = [packs] + list(params) + list(moments_m) + list(moments_v)
    out_shape = [jax.ShapeDtypeStruct((1, LANES), F32)]
    for w in params:
        out_shape += [jax.ShapeDtypeStruct(w.shape, F32)] * 4
    outs = pl.pallas_call(
        body, name="small_update",
        grid_spec=pltpu.PrefetchScalarGridSpec(
            num_scalar_prefetch=1, grid=(1,), in_specs=[whole(a) for a in ins],
            out_specs=[whole(a) for a in out_shape],
            scratch_shapes=[pltpu.VMEM(packs.shape[1:], F32)]),
        out_shape=out_shape,
        compiler_params=_params(),
    )(pos, *ins)
    return outs[0], [outs[1 + 4 * t:5 + 4 * t] for t in range(n)]


def _adamw_math(w, g, m, v):
    m = ADAM_B1 * m + (1.0 - ADAM_B1) * g
    v = ADAM_B2 * v + (1.0 - ADAM_B2) * (g * g)
    m_hat = m / (1.0 - ADAM_B1 ** ADAM_STEP)
    v_hat = v / (1.0 - ADAM_B2 ** ADAM_STEP)
    delta = -ADAM_LR * (m_hat / (jnp.sqrt(v_hat) + ADAM_EPS) + ADAM_WD * w)
    return delta, m, v


def adamw(w, g, m, v, block, name, first=0, count=None, acc=None):
    grid = tuple(s // b for s, b in zip(w.shape, block))
    if count is not None:
        grid = (count,) + grid[1:]

    def body(w_ref, g_ref, m_ref, v_ref, *rest):
        go_ref, d_ref, mo_ref, vo_ref = rest[-4:]
        gv = g_ref[...]
        d, mm, vv = _adamw_math(w_ref[...], gv, m_ref[...], v_ref[...])
        go_ref[...] = gv
        d_ref[...] = d
        mo_ref[...] = mm
        vo_ref[...] = vv

    spec = pl.BlockSpec(block, lambda i, *rest: (first + i,) + rest)
    shape = jax.ShapeDtypeStruct(w.shape, F32)
    extra = [] if acc is None else list(acc)
    return pl.pallas_call(
        body, name=name, grid=grid,
        in_specs=[spec] * 4 + [ANY] * len(extra), out_specs=[spec] * 4, out_shape=[shape] * 4,
        input_output_aliases={4 + a: a for a in range(len(extra))},
        compiler_params=_params(VMEM_BIG, n_grid=len(grid)),
    )(w, g, m, v, *extra)


def ada_finish(c_all, dmod, w, m, v):
    L, D, CW = w.shape
    hD = D // 2

    def body(c_ref, d_ref, w_ref, m_ref, v_ref, g_ref, dl_ref, mo_ref, vo_ref):
        cv = c_ref[...]
        z = jnp.zeros_like(cv)
        ca = jnp.concatenate([cv * jax.nn.sigmoid(cv), z], axis=0).astype(BF16)
        dm = jnp.concatenate([d_ref[0], jnp.zeros_like(d_ref[0])], axis=0).astype(BF16)
        g = lax.dot_general(ca, dm, TN, preferred_element_type=F32)
        g_ref[0] = g
        d, mm, vv = _adamw_math(w_ref[0], g, m_ref[0], v_ref[0])
        dl_ref[0] = d
        mo_ref[0] = mm
        vo_ref[0] = vv

    big = pl.BlockSpec((1, hD, CW), lambda l, h: (l, h, 0))
    shape = jax.ShapeDtypeStruct(w.shape, F32)
    return pl.pallas_call(
        body, name="ada_finish", grid=(L, 2),
        in_specs=[pl.BlockSpec((N_DEV, hD), lambda l, h: (0, h)), pl.BlockSpec((1, N_DEV, CW), lambda l, h: (l, 0, 0)),
                  big, big, big],
        out_specs=[big] * 4, out_shape=[shape] * 4,
        compiler_params=_params(VMEM_BIG, n_grid=2),
    )(c_all, dmod, w, m, v)


def kernel(x, c, w_ada, b_ada, g_pre, w_in, w_conv, w_pool, pool_scale, w_out, g_post, loss_target, m_w_ada, m_b_ada, m_g_pre, m_w_in, m_w_conv, m_w_pool, m_pool_scale, m_w_out, m_g_post, v_w_ada, v_b_ada, v_g_pre, v_w_in, v_w_conv, v_w_pool, v_pool_scale, v_w_out, v_g_post):
    L, D, CW = w_in.shape
    RO = w_out.shape[1]
    T = x.shape[1]
    ix, iy, ic = _me()
    chip = 2 * ix + iy
    me_lin = 4 * ix + 2 * iy + ic

    pos = jnp.stack([ic, chip, me_lin]).astype(jnp.int32)
    g_pre3, g_post3 = g_pre.reshape(L, 1, D), g_post.reshape(L, 1, D)
    pscale3 = pool_scale.reshape(L, 1, pool_scale.shape[1])
    n_s, n_c = 3, 9

    def gather(bufs, after):
        ss, rs, bufs, tok = xchg_start("gather_start", bufs, 3 * len(bufs), plan_gather, after=after)
        return (ss, rs, bufs), tok

    def arrive(flight, after):
        ss, rs, bufs = flight
        bufs = xchg_wait("gather_wait", bufs, ss, rs, 3 * len(bufs), plan_gather, after)
        fss, frs, bufs, tok = xchg_start("forward_start", bufs, 3 * len(bufs), plan_forward, sibling_only=True)
        return (fss, frs, bufs), tok

    def ready(flight, after):
        fss, frs, bufs = flight
        return xchg_wait("forward_wait", bufs, fss, frs, 3 * len(bufs), plan_forward, after)

    c_all3, wconv_all = gather_small(c.reshape(SUBLANES, LANES), w_conv, pos)
    c_all = c_all3.reshape(N_DEV, D)
    gi0, go0 = cast_weights(pos, w_in, w_out, 0, c_all3)
    fly_in0, token = gather([gi0], [])
    b_my = lax.dynamic_slice_in_dim(b_ada, chip * CW, CW, axis=1)
    m_ss, m_rs, mods, token = xchg_start("mod_start", [mod_part(pos, c_all, w_ada, b_my, token)], 3, plan_mod)
    fly_out0, token = gather([go0], [token])
    fwd_in0, token = arrive(fly_in0, [token])
    flying_w = [None] * L
    flying_w[1], token = gather(list(cast_weights(pos, w_in, w_out, 1, token)), [])
    (mod_all,) = xchg_wait("mod_wait", mods, m_ss, m_rs, 3, plan_mod, [token])
    mod = lax.dynamic_index_in_dim(mod_all, me_lin, axis=2, keepdims=False)
    mod4 = jnp.transpose(mod, (1, 0, 2)).reshape(L, 3, 1, D)

    xs, projs, yas, yps, ys = [x.reshape(T, D)], [], [], [], []
    wg_in, wg_out = [], []
    fwd_next = None
    for l in range(L):
        if l == 0:
            (gi,) = ready(fwd_in0, [mod4])
        else:
            gi, go = ready(fwd_next, [xs[l]])
        proj = proj_fwd(xs[l], mod4, g_pre3, gi, l)
        if l + 2 < L:
            flying_w[l + 2], _ = gather(list(cast_weights(pos, w_in, w_out, l + 2, proj)), [])
        ya = conv_fwd(proj, wconv_all, l)
        yp = pool_fwd(proj, w_pool, pscale3, l)
        after = [ya, yp]
        if l == 0:
            fwd_out0, token = arrive(fly_out0, after)
            after = [token]
        if l + 1 < L:
            fwd_next, token = arrive(flying_w[l + 1], after)
            after = [token]
        if l == 0:
            (go,) = ready(fwd_out0, after)
        wg_in.append(gi)
        wg_out.append(go.reshape(N_CHIPS * RO, D))
        projs.append(proj)
        yas.append(ya)
        yps.append(yp)
        if l + 1 < L:
            xn, yv = out_fwd(ya, yp, wg_out[l], xs[l], mod4, g_post3, l, after[0])
            xs.append(xn)
        else:
            dx, yv, loss_blk = out_fwd_loss(ya, yp, wg_out[l], xs[l], mod4, g_post3, l, loss_target.reshape(T, D))
        ys.append(yv)

    shapes = (w_in.shape, w_out.shape, w_pool.shape)
    smalls = [None] * L
    acc, flying, sib, token = None, None, None, loss_blk

    def to_chips(sib, after):
        sl, s_ss, s_rs, s_bufs = sib
        s_bufs = xchg_wait("sibling_wait", s_bufs, s_ss, s_rs, n_s, plan_sibling, after)
        chip_parts = add_sibling(pos, s_bufs[0:3], s_bufs[3:6])
        lands = [lax.empty((3,) + a.shape[1:], a.dtype) for a in chip_parts]
        c_ss, c_rs, c_bufs, ctoken = xchg_start("chip_start", list(chip_parts) + lands, n_c, plan_chip)
        return (sl, c_ss, c_rs, c_bufs), ctoken

    def landed(flying, acc, after):
        fl, f_ss, f_rs, f_bufs = flying
        f_bufs = xchg_wait("chip_wait", f_bufs, f_ss, f_rs, n_c, plan_chip, after)
        return sum_chips(pos, f_bufs[0:3], f_bufs[3:6], acc, fl, shapes)

    for l in reversed(range(L)):
        dya, dyp, dwo_l, dgate, dgpost = out_bwd(dx, ys[l], yas[l], yps[l], wg_out[l], mod4, g_post3, l, token)
        token = dya
        if sib is not None:
            arrived = flying
            flying, token = to_chips(sib, [dya])
            if arrived is not None:
                acc = landed(arrived, acc, [token])
                token = acc[0]
        du_p, dg_p, dwp_l, dps = pool_bwd(projs[l], dyp, w_pool, pscale3, l, token)
        du_a, db_a, dc_a, dg_a, dwc = conv_bwd(projs[l], dya, wconv_all, l)
        dx, dwi_l, dshift, dscale, dgpre = in_bwd([du_a, db_a, dc_a, dg_a, du_p, dg_p], wg_in[l], xs[l], dx,
                                                  mod4, g_pre3, l)
        smalls[l] = (dgpre, dgpost, dshift, dscale, dgate, dps, dwc)
        parts = [dwi_l, dwo_l.reshape(N_CHIPS, RO, D), dwp_l]
        s_lands = [lax.empty((a.shape[0], a.shape[1] // 2) + a.shape[2:], a.dtype) for a in parts]
        s_ss, s_rs, s_bufs, token = xchg_start("sibling_start", parts + s_lands, n_s, plan_sibling, sibling_only=True)
        sib = (l, s_ss, s_rs, s_bufs)
    grad_x = dx.reshape(1, T, D)

    p_ss, p_rs, packs, ptoken = xchg_start("pack_start", [pack_small(pos, smalls, loss_blk)], N_DEV - 1, plan_pack)
    acc = landed(flying, acc, [ptoken, token])
    n_sp = (2 + N_DEV - 1) * (L - 1)
    spread = plan_spread(tuple(range(1, L)), tuple(range(1, L)))
    sp_ss, sp_rs, acc, sp_token = xchg_start("spread_start", list(acc), n_sp, spread)
    flying, token = to_chips(sib, [sp_token])
    (packs_all,) = xchg_wait("pack_wait", packs, p_ss, p_rs, N_DEV - 1, plan_pack, [token])
    dmod_all = packs_all.reshape(N_DEV, L, SUBLANES, D)[:, :, 2:5].reshape(N_DEV, L, 3 * D)
    dmod_my = jnp.transpose(lax.dynamic_slice_in_dim(dmod_all, chip * CW, CW, axis=2), (1, 0, 2))

    g_w_ada, d_w_ada, nm_w_ada, nv_w_ada = ada_finish(c_all, dmod_my, w_ada, m_w_ada, v_w_ada)
    loss_row, upd = small_update(pos, packs_all, [b_ada, g_pre, g_post, pool_scale, w_conv],
                                 [m_b_ada, m_g_pre, m_g_post, m_pool_scale, m_w_conv],
                                 [v_b_ada, v_g_pre, v_g_post, v_pool_scale, v_w_conv])
    loss = loss_row[0, 0]
    (g_b_ada, d_b_ada, nm_b_ada, nv_b_ada), (g_g_pre, d_g_pre, nm_g_pre, nv_g_pre) = upd[0], upd[1]
    (g_g_post, d_g_post, nm_g_post, nv_g_post), (g_pscale, d_pscale, nm_pscale, nv_pscale) = upd[2], upd[3]
    g_w_conv, d_w_conv, nm_w_conv, nv_w_conv = upd[4]

    done = [nv_w_ada, nv_w_conv]
    g_w_in, g_w_out, g_w_pool = xchg_wait("spread_wait", acc, sp_ss, sp_rs, n_sp, spread, done)
    in_blk, out_blk = (1, D // 2, CW), (1, RO, D)
    upd_in = adamw(w_in, g_w_in, m_w_in, v_w_in, in_blk, "adamw_w_in", 1, L - 1)
    upd_out = adamw(w_out, g_w_out, m_w_out, v_w_out, out_blk, "adamw_w_out", 1, L - 1)

    acc = landed(flying, (g_w_in, g_w_out, g_w_pool), [upd_in[3], upd_out[3]])
    r_w_in, r_w_out, r_w_pool = spread_now(*acc, (0,), (0,))
    g_w_in, d_w_in, nm_w_in, nv_w_in = adamw(w_in, r_w_in, m_w_in, v_w_in, in_blk, "adamw_w_in", 0, 1, upd_in)
    g_w_out, d_w_out, nm_w_out, nv_w_out = adamw(w_out, r_w_out, m_w_out, v_w_out, out_blk, "adamw_w_out", 0, 1, upd_out)
    pshape = (L, N_CHIPS * LANES, LANES)
    upd_pool = adamw(w_pool.reshape(pshape), r_w_pool.reshape(pshape), m_w_pool.reshape(pshape),
                     v_w_pool.reshape(pshape), (1,) + pshape[1:], "adamw_w_pool")
    g_w_pool, d_w_pool, nm_w_pool, nv_w_pool = [a.reshape(w_pool.shape) for a in upd_pool]

    return (loss, grad_x,
            g_w_ada, g_b_ada, g_g_pre, g_w_in, g_w_conv, g_w_pool, g_pscale, g_w_out, g_g_post,
            d_w_ada, d_b_ada, d_g_pre, d_w_in, d_w_conv, d_w_pool, d_pscale, d_w_out, d_g_post,
            nm_w_ada, nm_b_ada, nm_g_pre, nm_w_in, nm_w_conv, nm_w_pool, nm_pscale, nm_w_out, nm_g_post,
            nv_w_ada, nv_b_ada, nv_g_pre, nv_w_in, nv_w_conv, nv_w_pool, nv_pscale, nv_w_out, nv_g_post)
```

```python
import functools

import jax
import jax.numpy as jnp
from jax import lax
from jax.experimental import pallas as pl
from jax.experimental.pallas import tpu as pltpu

F32 = jnp.float32
BF16 = jnp.bfloat16
MESH = pl.DeviceIdType.MESH
ANY = pl.BlockSpec(memory_space=pl.ANY)

NORM_EPS = 1e-6
POOL_WINDOWS = (2, 4, 8, 16)
ADAM_LR = 0.001
ADAM_B1 = 0.9
ADAM_B2 = 0.999
ADAM_EPS = 1e-08
ADAM_WD = 0.01
ADAM_STEP = 10

N_CHIPS = 4
N_DEV = 8
LANES = 128
SUBLANES = 8
VMEM_BIG = 56 * 1024 * 1024
HIST = 16
R_CONV = 64
R_POOL = 128

NT = (((1,), (1,)), ((), ()))
TN = (((0,), (0,)), ((), ()))


def _params(vmem=None, n_grid=1):
    kw = {}
    if n_grid:
        kw["dimension_semantics"] = ("arbitrary",) * n_grid
    if vmem is not None:
        kw["vmem_limit_bytes"] = vmem
    return pltpu.CompilerParams(**kw)


def _colsum8(v):
    n, d = v.shape
    return v.reshape(n // SUBLANES, SUBLANES, d).sum(axis=0)


def _rms(v):
    return lax.rsqrt(jnp.mean(v * v, axis=-1, keepdims=True) + NORM_EPS)


def _sigmoid(v):
    return 0.5 * jnp.tanh(0.5 * v) + 0.5


def _shift_down(ext, k, rows):
    if k == 0:
        return ext[HIST:HIST + rows]
    return pltpu.roll(ext, k, 0)[HIST:HIST + rows]


def _shift_up(ext, k, rows):
    if k == 0:
        return ext[0:rows]
    return pltpu.roll(ext, ext.shape[0] - k, 0)[0:rows]


def _load_ext(ref, r0, h0, first, rows):
    hist = ref[pl.ds(h0, HIST), :].astype(F32)
    hist = jnp.where(first, 0.0, hist)
    cur = ref[pl.ds(r0, rows), :].astype(F32)
    return jnp.concatenate([hist, cur], axis=0)


def _me():
    return lax.axis_index("x"), lax.axis_index("y"), lax.axis_index("c")


def cast_weights(pos, w_in, w_out, l, after):
    _, D, CW = w_in.shape
    RO = w_out.shape[1]

    def body(pos_ref, wi, wo, after_ref, oi, oo):
        oi[...] = wi[...].astype(BF16)
        oo[...] = wo[...].astype(BF16)

    return pl.pallas_call(
        body, name="cast_w",
        grid_spec=pltpu.PrefetchScalarGridSpec(
            num_scalar_prefetch=1, grid=(2,),
            in_specs=[pl.BlockSpec((None, D // 2, CW), lambda h, p: (l, h, 0)),
                      pl.BlockSpec((None, RO // 2, D), lambda h, p: (l, h, 0)), ANY],
            out_specs=[pl.BlockSpec((D // 2, CW), lambda h, p: (h, p[1])),
                       pl.BlockSpec((None, RO // 2, D), lambda h, p: (p[1], h, 0))]),
        out_shape=[jax.ShapeDtypeStruct((D, N_CHIPS * CW), BF16), jax.ShapeDtypeStruct((N_CHIPS, RO, D), BF16)],
        compiler_params=_params(),
    )(pos, w_in, w_out, after)


def mod_part(pos, c_all, w_ada, b_my, after):
    L, D, CW = w_ada.shape

    def body(pos_ref, c_ref, w_ref, b_ref, after_ref, o_ref):
        cv = c_ref[...]
        ca = (cv * jax.nn.sigmoid(cv)).astype(BF16)
        o_ref[...] = jnp.dot(ca, w_ref[0].astype(BF16), preferred_element_type=F32) + b_ref[0]

    return pl.pallas_call(
        body, name="mod_part",
        grid_spec=pltpu.PrefetchScalarGridSpec(
            num_scalar_prefetch=1, grid=(L,),
            in_specs=[pl.BlockSpec((N_DEV, D), lambda l, p: (0, 0)),
                      pl.BlockSpec((1, D, CW), lambda l, p: (l, 0, 0)),
                      pl.BlockSpec((1, 1, CW), lambda l, p: (l, 0, 0)), ANY],
            out_specs=pl.BlockSpec((None, None, N_DEV, CW), lambda l, p: (p[1], l, 0, 0))),
        out_shape=jax.ShapeDtypeStruct((N_CHIPS, L, N_DEV, CW), F32),
        compiler_params=_params(VMEM_BIG),
    )(pos, c_all, w_ada, b_my.reshape(L, 1, CW), after)


def _mod_row(l, k, D):
    return pl.BlockSpec((None, None, 1, D), lambda *_: (l, k, 0, 0))


def _layer_row(l, D):
    return pl.BlockSpec((None, 1, D), lambda *_: (l, 0, 0))


def proj_fwd(x, mod4, g_pre3, wg, l):
    T, D = x.shape
    NC = wg.shape[1]
    NB = N_CHIPS
    CW = NC // NB
    tm = 512

    def body(x_ref, sh_ref, sc_ref, g_ref, w_ref, o_ref):
        xv = x_ref[...]
        h = (xv * _rms(xv) * g_ref[...]) * (1.0 + sc_ref[...]) + sh_ref[...]
        hb = h.astype(BF16)
        for j in range(NB):
            cols = slice(j * CW, (j + 1) * CW)
            o_ref[:, cols] = jnp.dot(hb, w_ref[:, cols], preferred_element_type=F32).astype(BF16)

    return pl.pallas_call(
        body, name="proj_fwd", grid=(T // tm,),
        in_specs=[pl.BlockSpec((tm, D), lambda i: (i, 0)), _mod_row(l, 0, D), _mod_row(l, 1, D), _layer_row(l, D),
                  pl.BlockSpec((D, NC), lambda i: (0, 0))],
        out_specs=pl.BlockSpec((tm, NC), lambda i: (i, 0)),
        out_shape=jax.ShapeDtypeStruct((T, NC), BF16),
        compiler_params=_params(VMEM_BIG),
    )(x, mod4, mod4, g_pre3, wg)


def conv_fwd(proj, wconv, l):
    T = proj.shape[0]
    R = R_CONV
    nblk = 4

    def body(u_ref, b_ref, c_ref, g_ref, w_ref, o_ref):
        w0 = w_ref[pl.ds(0, 1), :]
        w1 = w_ref[pl.ds(1, 1), :]
        w2 = w_ref[pl.ds(2, 1), :]

        def chunk(i, carry):
            r0 = pl.multiple_of(i * R, R)
            h0 = pl.multiple_of(jnp.maximum(r0 - HIST, 0), HIST)
            first = i == 0
            ca = _load_ext(c_ref, r0, h0, first, R) * _load_ext(u_ref, r0, h0, first, R)
            conv = w2 * ca[HIST:] + w1 * _shift_down(ca, 1, R) + w0 * _shift_down(ca, 2, R)
            g = g_ref[pl.ds(r0, R), :].astype(F32)
            b = b_ref[pl.ds(r0, R), :].astype(F32)
            o_ref[pl.ds(r0, R), :] = (b * conv * (g * _sigmoid(g))).astype(BF16)
            return carry

        lax.fori_loop(0, T // R, chunk, 0)

    def col(off):
        return pl.BlockSpec((T, LANES), lambda j: (0, j + off))

    return pl.pallas_call(
        body, name="conv_fwd", grid=(nblk,),
        in_specs=[col(0), col(4), col(8), col(12), pl.BlockSpec((None, None, 3, LANES), lambda j: (j, l, 0, 0))],
        out_specs=pl.BlockSpec((T, LANES), lambda j: (0, j)),
        out_shape=jax.ShapeDtypeStruct((T, nblk * LANES), BF16),
        compiler_params=_params(),
    )(proj, proj, proj, proj, wconv)


def _causal_window_sum(ext, w):
    s, k = ext, 1
    while k < w:
        s = s + pltpu.roll(s, k, 0)
        k *= 2
    return s


def _anticausal_window_sum(ext, w):
    s, k = ext, 1
    n = ext.shape[0]
    while k < w:
        s = s + pltpu.roll(s, n - k, 0)
        k *= 2
    return s


def _count(r0, rows, w):
    t = r0 + lax.broadcasted_iota(jnp.int32, (rows, LANES), 0)
    return jnp.minimum(t + 1, w).astype(F32)


def _pooled_loop(p_ref, pooled_s, w, T):
    R = R_POOL

    def chunk(i, carry):
        r0 = pl.multiple_of(i * R, R)
        h0 = pl.multiple_of(jnp.maximum(r0 - HIST, 0), HIST)
        ext = _load_ext(p_ref, r0, h0, i == 0, R)
        ws = _causal_window_sum(ext, w)[HIST:]
        pooled_s[pl.ds(r0, R), :] = (ws / _count(r0, R, w) - ext[HIST:]).astype(BF16)
        return carry

    lax.fori_loop(0, T // R, chunk, 0)


def _pool_w_spec(l):
    return pl.BlockSpec((None, None, LANES, LANES), lambda j: (l, j, 0, 0))


def _pool_s_spec(l):
    return pl.BlockSpec((None, 1, LANES), lambda j: (l, 0, j))


def pool_fwd(proj, wpool, pscale3, l):
    T = proj.shape[0]
    R = R_POOL
    ngrp = len(POOL_WINDOWS)

    def body(p_ref, g_ref, w_ref, s_ref, o_ref, pooled_s, mixed_s):
        grp = pl.program_id(0)

        def group(w):
            _pooled_loop(p_ref, pooled_s, w, T)
            mixed_s[...] = jnp.dot(pooled_s[...], w_ref[...].astype(BF16), preferred_element_type=F32)
            sc = s_ref[...]

            def chunk(i, carry):
                r0 = pl.multiple_of(i * R, R)
                g = g_ref[pl.ds(r0, R), :].astype(F32)
                o_ref[pl.ds(r0, R), :] = (mixed_s[pl.ds(r0, R), :] * sc * (g * _sigmoid(g))).astype(BF16)
                return carry

            lax.fori_loop(0, T // R, chunk, 0)

        for k, w in enumerate(POOL_WINDOWS):
            pl.when(grp == k)(functools.partial(group, w))

    return pl.pallas_call(
        body, name="pool_fwd", grid=(ngrp,),
        in_specs=[pl.BlockSpec((T, LANES), lambda j: (0, j + 16)), pl.BlockSpec((T, LANES), lambda j: (0, j + 20)),
                  _pool_w_spec(l), _pool_s_spec(l)],
        out_specs=pl.BlockSpec((T, LANES), lambda j: (0, j)),
        out_shape=jax.ShapeDtypeStruct((T, ngrp * LANES), BF16),
        scratch_shapes=[pltpu.VMEM((T, LANES), BF16), pltpu.VMEM((T, LANES), F32)],
        compiler_params=_params(),
    )(proj, proj, wpool, pscale3)


def out_fwd(ya, yp, wo, x, mod4, g_post3, l, after):
    T, D = x.shape
    H = ya.shape[1]
    tm = 512

    def body(ya_ref, yp_ref, wo_ref, x_ref, gt_ref, g_ref, after_ref, xn_ref, y_ref):
        y = (jnp.dot(ya_ref[...], wo_ref[0:H, :], preferred_element_type=F32)
             + jnp.dot(yp_ref[...], wo_ref[H:2 * H, :], preferred_element_type=F32))
        xn_ref[...] = x_ref[...] + gt_ref[...] * (y * _rms(y) * g_ref[...])
        y_ref[...] = y.astype(BF16)

    tile = pl.BlockSpec((tm, D), lambda i: (i, 0))
    half = pl.BlockSpec((tm, H), lambda i: (i, 0))
    return pl.pallas_call(
        body, name="out_fwd", grid=(T // tm,),
        in_specs=[half, half, pl.BlockSpec((2 * H, D), lambda i: (0, 0)), tile, _mod_row(l, 2, D), _layer_row(l, D),
                  ANY],
        out_specs=[tile, tile],
        out_shape=[jax.ShapeDtypeStruct((T, D), F32), jax.ShapeDtypeStruct((T, D), BF16)],
        compiler_params=_params(VMEM_BIG),
    )(ya, yp, wo, x, mod4, g_post3, after)


def out_fwd_loss(ya, yp, wo, x, mod4, g_post3, l, target):
    T, D = x.shape
    H = ya.shape[1]
    tm = 512
    nt = T // tm

    def body(ya_ref, yp_ref, wo_ref, x_ref, gt_ref, g_ref, t_ref, dx_ref, y_ref, l_ref, acc):
        i = pl.program_id(0)

        @pl.when(i == 0)
        def _():
            acc[...] = jnp.zeros_like(acc)

        y = (jnp.dot(ya_ref[...], wo_ref[0:H, :], preferred_element_type=F32)
             + jnp.dot(yp_ref[...], wo_ref[H:2 * H, :], preferred_element_type=F32))
        y_ref[...] = y.astype(BF16)
        d = (x_ref[...] + gt_ref[...] * (y * _rms(y) * g_ref[...])) - t_ref[...]
        dx_ref[...] = d * (1.0 / D)
        acc[...] += _colsum8(d * d)

        @pl.when(i == nt - 1)
        def _():
            l_ref[...] = jnp.zeros_like(l_ref) + jnp.sum(acc[...]) * (0.5 / D)

    tile = pl.BlockSpec((tm, D), lambda i: (i, 0))
    half = pl.BlockSpec((tm, H), lambda i: (i, 0))
    return pl.pallas_call(
        body, name="out_fwd_loss", grid=(nt,),
        in_specs=[half, half, pl.BlockSpec((2 * H, D), lambda i: (0, 0)), tile, _mod_row(l, 2, D), _layer_row(l, D),
                  tile],
        out_specs=[tile, tile, pl.BlockSpec((SUBLANES, LANES), lambda i: (0, 0))],
        out_shape=[jax.ShapeDtypeStruct((T, D), F32), jax.ShapeDtypeStruct((T, D), BF16),
                   jax.ShapeDtypeStruct((SUBLANES, LANES), F32)],
        scratch_shapes=[pltpu.VMEM((SUBLANES, D), F32)],
        compiler_params=_params(VMEM_BIG),
    )(ya, yp, wo, x, mod4, g_post3, target)


def out_bwd(dx, y, ya, yp, wo, mod4, g_post3, l, after):
    T, D = dx.shape
    H = ya.shape[1]
    tm = 512
    nt = T // tm

    def body(dx_ref, y_ref, ya_ref, yp_ref, wo_ref, gt_ref, g_ref, after_ref,
             dya_ref, dyp_ref, dwo_ref, dgt_ref, dg_ref, acc_w, acc_gt, acc_g):
        i = pl.program_id(0)

        @pl.when(i == 0)
        def _():
            acc_w[...] = jnp.zeros_like(acc_w)
            acc_gt[...] = jnp.zeros_like(acc_gt)
            acc_g[...] = jnp.zeros_like(acc_g)

        yv = y_ref[...].astype(F32)
        dxv = dx_ref[...]
        g = g_ref[...]
        r = _rms(yv)
        yn = yv * r
        acc_gt[...] += _colsum8(dxv * (yn * g))
        dn = dxv * gt_ref[...]
        acc_g[...] += _colsum8(dn * yn)
        a = dn * g
        dy = r * (a - yn * jnp.mean(a * yn, axis=-1, keepdims=True))
        dyb = dy.astype(BF16)
        dyc = lax.dot_general(dyb, wo_ref[...], NT, preferred_element_type=F32)
        dya_ref[...] = dyc[:, 0:H].astype(BF16)
        dyp_ref[...] = dyc[:, H:2 * H].astype(BF16)
        acc_w[0:H, :] += lax.dot_general(ya_ref[...], dyb, TN, preferred_element_type=F32)
        acc_w[H:2 * H, :] += lax.dot_general(yp_ref[...], dyb, TN, preferred_element_type=F32)

        @pl.when(i == nt - 1)
        def _():
            dwo_ref[...] = acc_w[...].astype(BF16)
            dgt_ref[...] = jnp.sum(acc_gt[...], axis=0, keepdims=True)
            dg_ref[...] = jnp.sum(acc_g[...], axis=0, keepdims=True)

    row = pl.BlockSpec((1, D), lambda i: (0, 0))
    tile = pl.BlockSpec((tm, D), lambda i: (i, 0))
    half = pl.BlockSpec((tm, H), lambda i: (i, 0))
    full = pl.BlockSpec((2 * H, D), lambda i: (0, 0))
    return pl.pallas_call(
        body, name="out_bwd", grid=(nt,),
        in_specs=[tile, tile, half, half, full, _mod_row(l, 2, D), _layer_row(l, D), ANY],
        out_specs=[half, half, full, row, row],
        out_shape=[jax.ShapeDtypeStruct((T, H), BF16), jax.ShapeDtypeStruct((T, H), BF16),
                   jax.ShapeDtypeStruct((2 * H, D), BF16),
                   jax.ShapeDtypeStruct((1, D), F32), jax.ShapeDtypeStruct((1, D), F32)],
        scratch_shapes=[pltpu.VMEM((2 * H, D), F32), pltpu.VMEM((SUBLANES, D), F32), pltpu.VMEM((SUBLANES, D), F32)],
        compiler_params=_params(VMEM_BIG),
    )(dx, y, ya, yp, wo, mod4, g_post3, after)


def conv_bwd(proj, dya, wconv, l):
    T = proj.shape[0]
    R = R_CONV
    nblk = 4
    nchunk = T // R

    def body(u_ref, b_ref, c_ref, g_ref, dy_ref, w_ref, du_ref, db_ref, dc_ref, dg_ref, dw_ref):
        w0 = w_ref[pl.ds(0, 1), :]
        w1 = w_ref[pl.ds(1, 1), :]
        w2 = w_ref[pl.ds(2, 1), :]

        def chunk(k, carry):
            head, a0, a1, a2 = carry
            i = nchunk - 1 - k
            r0 = pl.multiple_of(i * R, R)
            h0 = pl.multiple_of(jnp.maximum(r0 - HIST, 0), HIST)
            first = i == 0
            ue = _load_ext(u_ref, r0, h0, first, R)
            ce = _load_ext(c_ref, r0, h0, first, R)
            ca = ce * ue
            ca0 = ca[HIST:]
            ca1 = _shift_down(ca, 1, R)
            ca2 = _shift_down(ca, 2, R)
            conv = w2 * ca0 + w1 * ca1 + w0 * ca2
            g = g_ref[pl.ds(r0, R), :].astype(F32)
            b = b_ref[pl.ds(r0, R), :].astype(F32)
            dy = dy_ref[pl.ds(r0, R), :].astype(F32)
            sg = _sigmoid(g)
            sl = g * sg
            t = dy * conv
            db_ref[pl.ds(r0, R), :] = (t * sl).astype(BF16)
            dg_ref[pl.ds(r0, R), :] = (t * b * (sg * (1.0 + g * (1.0 - sg)))).astype(BF16)
            dconv = dy * b * sl
            a2 = a2 + _colsum8(dconv * ca0)
            a1 = a1 + _colsum8(dconv * ca1)
            a0 = a0 + _colsum8(dconv * ca2)
            e = jnp.concatenate([dconv, head], axis=0)
            dca = w2 * dconv + w1 * _shift_up(e, 1, R) + w0 * _shift_up(e, 2, R)
            du_ref[pl.ds(r0, R), :] = (dca * ce[HIST:]).astype(BF16)
            dc_ref[pl.ds(r0, R), :] = (dca * ue[HIST:]).astype(BF16)
            return dconv[0:SUBLANES], a0, a1, a2

        z = jnp.zeros((SUBLANES, LANES), F32)
        _, a0, a1, a2 = lax.fori_loop(0, nchunk, chunk, (z, z, z, z))
        dw_ref[pl.ds(0, 1), :] = jnp.sum(a0, axis=0, keepdims=True)
        dw_ref[pl.ds(1, 1), :] = jnp.sum(a1, axis=0, keepdims=True)
        dw_ref[pl.ds(2, 1), :] = jnp.sum(a2, axis=0, keepdims=True)

    def col(off):
        return pl.BlockSpec((T, LANES), lambda j: (0, j + off))

    sec = jax.ShapeDtypeStruct((T, nblk * LANES), BF16)
    return pl.pallas_call(
        body, name="conv_bwd", grid=(nblk,),
        in_specs=[col(0), col(4), col(8), col(12), col(0), pl.BlockSpec((None, None, 3, LANES), lambda j: (j, l, 0, 0))],
        out_specs=[col(0), col(0), col(0), col(0), pl.BlockSpec((None, 3, LANES), lambda j: (j, 0, 0))],
        out_shape=[sec, sec, sec, sec, jax.ShapeDtypeStruct((nblk, 3, LANES), F32)],
        compiler_params=_params(),
    )(proj, proj, proj, proj, dya, wconv)


def pool_bwd(proj, dyp, wpool, pscale3, l, after):
    T = proj.shape[0]
    R = R_POOL
    ngrp = len(POOL_WINDOWS)
    nchunk = T // R

    def body(p_ref, g_ref, dy_ref, w_ref, s_ref, after_ref, du_ref, dg_ref, dw_ref, ds_ref,
             pooled_s, mixed_s, dmix_s, dpool_s):
        grp = pl.program_id(0)

        def group(w):
            wb = w_ref[...].astype(BF16)
            _pooled_loop(p_ref, pooled_s, w, T)
            mixed_s[...] = jnp.dot(pooled_s[...], wb, preferred_element_type=F32)
            sc = s_ref[...]

            def gate_chunk(i, acc):
                r0 = pl.multiple_of(i * R, R)
                g = g_ref[pl.ds(r0, R), :].astype(F32)
                dy = dy_ref[pl.ds(r0, R), :].astype(F32)
                mixed = mixed_s[pl.ds(r0, R), :]
                sg = _sigmoid(g)
                dg_ref[pl.ds(r0, R), :] = (dy * mixed * sc * (sg * (1.0 + g * (1.0 - sg)))).astype(BF16)
                dms = dy * (g * sg)
                dmix_s[pl.ds(r0, R), :] = (dms * sc).astype(BF16)
                return acc + _colsum8(dms * mixed)

            acc = lax.fori_loop(0, nchunk, gate_chunk, jnp.zeros((SUBLANES, LANES), F32))
            ds_ref[...] = jnp.sum(acc, axis=0, keepdims=True)
            dpool_s[pl.ds(0, T), :] = lax.dot_general(dmix_s[...], wb, NT, preferred_element_type=F32)
            dpool_s[pl.ds(T, HIST), :] = jnp.zeros((HIST, LANES), F32)
            dw_ref[...] = lax.dot_general(pooled_s[...], dmix_s[...], TN, preferred_element_type=F32).astype(BF16)

            def back_chunk(i, carry):
                r0 = pl.multiple_of(i * R, R)
                dpe = dpool_s[pl.ds(r0, R + HIST), :]
                e = dpe / _count(r0, R + HIST, w)
                du_ref[pl.ds(r0, R), :] = (_anticausal_window_sum(e, w)[0:R] - dpe[0:R]).astype(BF16)
                return carry

            lax.fori_loop(0, nchunk, back_chunk, 0)

        for k, w in enumerate(POOL_WINDOWS):
            pl.when(grp == k)(functools.partial(group, w))

    def col(off):
        return pl.BlockSpec((T, LANES), lambda j: (0, j + off))

    sec = jax.ShapeDtypeStruct((T, ngrp * LANES), BF16)
    wspec = pl.BlockSpec((None, LANES, LANES), lambda j: (j, 0, 0))
    sspec = pl.BlockSpec((1, LANES), lambda j: (0, j))
    return pl.pallas_call(
        body, name="pool_bwd", grid=(ngrp,),
        in_specs=[col(16), col(20), col(0), _pool_w_spec(l), _pool_s_spec(l), ANY],
        out_specs=[col(0), col(0), wspec, sspec],
        out_shape=[sec, sec, jax.ShapeDtypeStruct((ngrp, LANES, LANES), BF16),
                   jax.ShapeDtypeStruct((1, ngrp * LANES), F32)],
        scratch_shapes=[pltpu.VMEM((T, LANES), BF16), pltpu.VMEM((T, LANES), F32),
                        pltpu.VMEM((T, LANES), BF16), pltpu.VMEM((T + HIST, LANES), F32)],
        compiler_params=_params(),
    )(proj, proj, dyp, wpool, pscale3, after)


def in_bwd(dsecs, wg, x, dxo, mod4, g_pre3, l):
    T, D = x.shape
    NB = N_CHIPS
    CW = wg.shape[1] // NB
    SW = dsecs[0].shape[1]
    nsec = len(dsecs)
    PW = 256
    assert SW % PW == 0 and CW % PW == 0
    tm = 256
    nt = T // tm

    def body(*refs):
        d_refs = refs[0:nsec]
        w_ref, x_ref, dxo_ref, sh_ref, sc_ref, g_ref = refs[nsec:nsec + 6]
        dxi_ref, dw_ref, dsh_ref, dsc_ref, dg_ref = refs[nsec + 6:nsec + 11]
        acc_w, acc_sh, acc_sc, acc_g = refs[nsec + 11:]
        i = pl.program_id(0)

        @pl.when(i == 0)
        def _():
            acc_w[...] = jnp.zeros_like(acc_w)
            acc_sh[...] = jnp.zeros_like(acc_sh)
            acc_sc[...] = jnp.zeros_like(acc_sc)
            acc_g[...] = jnp.zeros_like(acc_g)

        xv = x_ref[...]
        g = g_ref[...]
        r = _rms(xv)
        xh = xv * r
        n = xh * g
        sc1 = 1.0 + sc_ref[...]
        hb = (n * sc1 + sh_ref[...]).astype(BF16)
        dh = lax.dot_general(d_refs[0][...], w_ref[:, 0:SW], NT, preferred_element_type=F32)
        for s in range(1, nsec):
            dh = dh + lax.dot_general(d_refs[s][...], w_ref[:, s * SW:(s + 1) * SW], NT, preferred_element_type=F32)
        for p in range(nsec * SW // PW):
            col = p * PW
            s, so = col // SW, col % SW
            j, jo = col // CW, col % CW
            acc_w[j, :, jo:jo + PW] += lax.dot_general(hb, d_refs[s][:, so:so + PW], TN, preferred_element_type=F32)
        acc_sh[...] += _colsum8(dh)
        acc_sc[...] += _colsum8(dh * n)
        dnp = dh * sc1
        acc_g[...] += _colsum8(dnp * xh)
        a = dnp * g
        dxi_ref[...] = dxo_ref[...] + r * (a - xh * jnp.mean(a * xh, axis=-1, keepdims=True))

        @pl.when(i == nt - 1)
        def _():
            dw_ref[...] = acc_w[...].astype(BF16)
            dsh_ref[...] = jnp.sum(acc_sh[...], axis=0, keepdims=True)
            dsc_ref[...] = jnp.sum(acc_sc[...], axis=0, keepdims=True)
            dg_ref[...] = jnp.sum(acc_g[...], axis=0, keepdims=True)

    row = pl.BlockSpec((1, D), lambda i: (0, 0))
    tile = pl.BlockSpec((tm, D), lambda i: (i, 0))
    sect = pl.BlockSpec((tm, SW), lambda i: (i, 0))
    rowshape = jax.ShapeDtypeStruct((1, D), F32)
    return pl.pallas_call(
        body, name="in_bwd", grid=(nt,),
        in_specs=[sect] * nsec + [pl.BlockSpec((D, NB * CW), lambda i: (0, 0)), tile, tile,
                                  _mod_row(l, 0, D), _mod_row(l, 1, D), _layer_row(l, D)],
        out_specs=[tile, pl.BlockSpec((NB, D, CW), lambda i: (0, 0, 0)), row, row, row],
        out_shape=[jax.ShapeDtypeStruct((T, D), F32), jax.ShapeDtypeStruct((NB, D, CW), BF16),
                   rowshape, rowshape, rowshape],
        scratch_shapes=[pltpu.VMEM((NB, D, CW), F32),
                        pltpu.VMEM((SUBLANES, D), F32), pltpu.VMEM((SUBLANES, D), F32), pltpu.VMEM((SUBLANES, D), F32)],
        compiler_params=_params(VMEM_BIG),
    )(*dsecs, wg, x, dxo, mod4, mod4, g_pre3)


def _rcopy(src, dst, ssem, rsem, dev):
    return pltpu.make_async_remote_copy(src_ref=src, dst_ref=dst, send_sem=ssem, recv_sem=rsem,
                                        device_id=dev, device_id_type=MESH)


def _peers7(x, y, c):
    out = []
    for m in range(1, N_DEV):
        bx, by, bc = (m >> 2) & 1, (m >> 1) & 1, m & 1
        out.append(((1 - x) if bx else x, (1 - y) if by else y, (1 - c) if bc else c))
    return out


HBM = pl.BlockSpec(memory_space=pltpu.HBM)
SEM = pl.BlockSpec(memory_space=pltpu.SEMAPHORE)
SPLIT = pltpu.CompilerParams(has_side_effects=pltpu.SideEffectType.DATAFLOW_SIDE_EFFECTING)


def _hbm(a):
    return pltpu.with_memory_space_constraint(a, pltpu.HBM)


def _chips(x, y):
    return [(1 - x, y), (x, 1 - y), (1 - x, 1 - y)]


SIBLING_BARRIER_ID = 0


def xchg_start(name, bufs, n_copies, plan, sibling_only=False, after=()):
    n = len(bufs)
    after = list(after)

    def body(*refs):
        ssem, rsem, token = refs[n + len(after)], refs[n + len(after) + 1], refs[-1]
        x, y, c = _me()
        if sibling_only:
            barrier = pltpu.get_barrier_semaphore()
            pl.semaphore_signal(barrier, inc=1, device_id=(x, y, 1 - c), device_id_type=MESH)
            pl.semaphore_wait(barrier, 1)
        copies = plan(refs[0:n], x, y, c)
        assert len(copies) == n_copies
        for k, (src, dst, peer, _) in enumerate(copies):
            _rcopy(src, dst, ssem.at[k], rsem.at[k], peer).start()
        token[...] = jnp.zeros_like(token)

    params = dict(has_side_effects=pltpu.SideEffectType.DATAFLOW_SIDE_EFFECTING)
    if sibling_only:
        params["collective_id"] = SIBLING_BARRIER_ID
    outs = pl.pallas_call(
        body, name=name,
        in_specs=[HBM] * n + [ANY] * len(after),
        out_specs=[SEM, SEM] + [HBM] * n + [pl.BlockSpec(memory_space=pltpu.VMEM)],
        out_shape=([pltpu.SemaphoreType.DMA((n_copies,))] * 2 + [pltpu.HBM(b.shape, b.dtype) for b in bufs]
                   + [jax.ShapeDtypeStruct((SUBLANES, LANES), F32)]),
        input_output_aliases={a: 2 + a for a in range(n)},
        compiler_params=pltpu.CompilerParams(**params),
    )(*[_hbm(b) for b in bufs], *after)
    return outs[0], outs[1], list(outs[2:2 + n]), outs[-1]


def xchg_wait(name, bufs, ssem, rsem, n_copies, plan, after):
    n = len(bufs)
    after = list(after)

    def body(*refs):
        ssem_ref, rsem_ref = refs[n], refs[n + 1]
        copies = plan(refs[0:n], *_me())
        assert len(copies) == n_copies
        for k, (src, _, peer, land) in enumerate(copies):
            cp = _rcopy(src, land, ssem_ref.at[k], rsem_ref.at[k], peer)
            cp.wait_send()
            cp.wait_recv()

    outs = pl.pallas_call(
        body, name=name,
        in_specs=[HBM] * n + [SEM, SEM] + [ANY] * len(after), out_specs=[HBM] * n,
        out_shape=[pltpu.HBM(b.shape, b.dtype) for b in bufs],
        input_output_aliases={a: a for a in range(n)},
        compiler_params=SPLIT,
    )(*bufs, ssem, rsem, *after)
    return list(outs)


def _shard_half(buf, chip, half):
    if len(buf.shape) == 2:
        h, w = buf.shape[0] // 2, buf.shape[1] // N_CHIPS
        return buf.at[pl.ds(half * h, h), pl.ds(chip * w, w)]
    h = buf.shape[1] // 2
    return buf.at[chip, pl.ds(half * h, h)]


def plan_gather(refs, x, y, c):
    out = []
    for (px, py) in _chips(x, y):
        for buf in refs:
            own = _shard_half(buf, 2 * x + y, c)
            out.append((own, own, (px, py, c), _shard_half(buf, 2 * px + py, c)))
    return out


def plan_forward(refs, x, y, c):
    out = []
    for (px, py) in _chips(x, y):
        for buf in refs:
            landed = _shard_half(buf, 2 * px + py, c)
            out.append((landed, landed, (x, y, 1 - c), _shard_half(buf, 2 * px + py, 1 - c)))
    return out


def plan_sibling(refs, x, y, c):
    n = len(refs) // 2
    out = []
    for a in range(n):
        h = refs[a].shape[1] // 2
        out.append((refs[a].at[:, pl.ds((1 - c) * h, h)], refs[n + a], (x, y, 1 - c), refs[n + a]))
    return out


def plan_chip(refs, x, y, c):
    n = len(refs) // 2
    out = []
    for j, (px, py) in enumerate(_chips(x, y)):
        for a in range(n):
            out.append((refs[a].at[2 * px + py], refs[n + a].at[j], (px, py, c), refs[n + a].at[j]))
    return out


def plan_mod(refs, x, y, c):
    (mods,) = refs
    mine = mods.at[2 * x + y]
    return [(mine, mine, (px, py, c), mods.at[2 * px + py]) for (px, py) in _chips(x, y)]


def plan_pack(refs, x, y, c):
    (packs,) = refs
    mine = packs.at[4 * x + 2 * y + c]
    return [(mine, mine, peer, packs.at[4 * peer[0] + 2 * peer[1] + peer[2]]) for peer in _peers7(x, y, c)]


def plan_spread(layers, wp_layers):
    def plan(refs, x, y, c):
        gi, go, gp = refs
        hD, hR, hP = gi.shape[1] // 2, go.shape[1] // 2, gp.shape[2] // 2
        sib = (x, y, 1 - c)
        out = []
        for l in layers:
            mine = gi.at[l, pl.ds(c * hD, hD)]
            out.append((mine, mine, sib, gi.at[l, pl.ds((1 - c) * hD, hD)]))
            mine = go.at[l, pl.ds(c * hR, hR)]
            out.append((mine, mine, sib, go.at[l, pl.ds((1 - c) * hR, hR)]))
        for l in wp_layers:
            mine = gp.at[l, 2 * x + y, pl.ds(c * hP, hP)]
            for peer in _peers7(x, y, c):
                out.append((mine, mine, peer, gp.at[l, 2 * peer[0] + peer[1], pl.ds(peer[2] * hP, hP)]))
        return out

    return plan


def gather_small(c8, wc, token):
    def body(c_ref, wc_ref, token_ref, call, wcall, ssem, rsem, lsem):
        x, y, c = _me()
        myc = 2 * x + y
        me_lin = 4 * x + 2 * y + c
        me = (x, y, c)
        local = [pltpu.make_async_copy(c_ref, call.at[me_lin], lsem.at[0]),
                 pltpu.make_async_copy(wc_ref, wcall.at[myc], lsem.at[1])]
        for cp in local:
            cp.start()
        sends, recvs = [], []
        for m, peer in enumerate(_peers7(x, y, c)):
            plin = 4 * peer[0] + 2 * peer[1] + peer[2]
            sends.append(_rcopy(c_ref, call.at[me_lin], ssem.at[m], rsem.at[m], peer))
            recvs.append(_rcopy(call.at[plin], call.at[plin], ssem.at[m], rsem.at[m], me))
        for j, (px, py) in enumerate([(1 - x, y), (x, 1 - y), (1 - x, 1 - y)]):
            pc = 2 * px + py
            sends.append(_rcopy(wc_ref, wcall.at[myc], ssem.at[7 + j], rsem.at[7 + j], (px, py, c)))
            recvs.append(_rcopy(wcall.at[pc], wcall.at[pc], ssem.at[7 + j], rsem.at[7 + j], me))
        for cp in sends:
            cp.start()
        for cp in recvs:
            cp.wait_recv()
        for cp in sends:
            cp.wait_send()
        for cp in local:
            cp.wait()

    return pl.pallas_call(
        body, name="gather_small",
        in_specs=[ANY] * 3, out_specs=[ANY] * 2,
        out_shape=[jax.ShapeDtypeStruct((N_DEV, SUBLANES, LANES), F32),
                   jax.ShapeDtypeStruct((N_CHIPS, wc.shape[0], 3, LANES), F32)],
        scratch_shapes=[pltpu.SemaphoreType.DMA((10,)), pltpu.SemaphoreType.DMA((10,)), pltpu.SemaphoreType.DMA((2,))],
        compiler_params=_params(n_grid=0),
    )(c8, wc, token)


def spread_now(gi, go, gp, layers, wp_layers):
    plan = plan_spread(layers, wp_layers)
    n = 2 * len(layers) + 7 * len(wp_layers)

    def body(gi_in, go_in, gp_in, gi, go, gp, ssem, rsem):
        copies = plan((gi, go, gp), *_me())
        me = _me()
        sends = [_rcopy(src, dst, ssem.at[k], rsem.at[k], peer) for k, (src, dst, peer, _) in enumerate(copies)]
        for cp in sends:
            cp.start()
        for k, (_, _, _, land) in enumerate(copies):
            _rcopy(land, land, ssem.at[k], rsem.at[k], me).wait_recv()
        for cp in sends:
            cp.wait_send()

    return pl.pallas_call(
        body, name="spread_now",
        in_specs=[ANY] * 3, out_specs=[ANY] * 3,
        out_shape=[jax.ShapeDtypeStruct(a.shape, a.dtype) for a in (gi, go, gp)],
        input_output_aliases={0: 0, 1: 1, 2: 2},
        scratch_shapes=[pltpu.SemaphoreType.DMA((n,)), pltpu.SemaphoreType.DMA((n,))],
        compiler_params=_params(n_grid=0),
    )(gi, go, gp)


def add_sibling(cidx, mine, sib):
    def body(c_ref, *refs):
        for a in range(3):
            m, s, o = refs[a], refs[3 + a], refs[6 + a]
            o[...] = (m[...].astype(F32) + s[...].astype(F32)).astype(BF16)

    def mine_spec(a):
        h = a.shape[1] // 2
        return pl.BlockSpec((None, h, a.shape[2]), lambda j, c_ref: (j, c_ref[0], 0))

    def sib_spec(a):
        return pl.BlockSpec((None,) + a.shape[1:], lambda j, c_ref: (j, 0, 0))

    return pl.pallas_call(
        body, name="add_sibling",
        grid_spec=pltpu.PrefetchScalarGridSpec(
            num_scalar_prefetch=1, grid=(N_CHIPS,),
            in_specs=[mine_spec(a) for a in mine] + [sib_spec(a) for a in sib],
            out_specs=[sib_spec(a) for a in sib]),
        out_shape=[jax.ShapeDtypeStruct(a.shape, BF16) for a in sib],
        compiler_params=_params(VMEM_BIG),
    )(cidx, *mine, *sib)


def sum_chips(pos, own, rb, acc, l, shapes):
    nq = 4
    n_in = 6 + (3 if acc is not None else 0)

    def body(pos_ref, *refs):
        for a in range(3):
            m, b, o = refs[a], refs[3 + a], refs[n_in + a]
            s = m[...].astype(F32)
            for j in range(3):
                s = s + b[j].astype(F32)
            o[...] = s

    def own_spec(a):
        return pl.BlockSpec((None, a.shape[1] // nq, a.shape[2]), lambda q, p: (p[1], q, 0))

    def rb_spec(a):
        return pl.BlockSpec((3, a.shape[1] // nq, a.shape[2]), lambda q, p: (0, q, 0))

    hi, ho, hp = own[0].shape[1] // nq, own[1].shape[1] // nq, own[2].shape[1] // nq
    out_specs = [pl.BlockSpec((None, hi, shapes[0][2]), lambda q, p: (l, p[0] * nq + q, 0)),
                 pl.BlockSpec((None, ho, shapes[1][2]), lambda q, p: (l, p[0] * nq + q, 0)),
                 pl.BlockSpec((None, None, hp, LANES), lambda q, p: (l, p[1], p[0] * nq + q, 0))]
    in_specs = [own_spec(a) for a in own] + [rb_spec(a) for a in rb]
    args = list(own) + list(rb)
    aliases = {}
    if acc is not None:
        in_specs += [ANY] * 3
        args += list(acc)
        aliases = {7: 0, 8: 1, 9: 2}
    return pl.pallas_call(
        body, name="sum_chips",
        grid_spec=pltpu.PrefetchScalarGridSpec(num_scalar_prefetch=1, grid=(nq,), in_specs=in_specs, out_specs=out_specs),
        out_shape=[jax.ShapeDtypeStruct(s, F32) for s in shapes],
        input_output_aliases=aliases,
        compiler_params=_params(VMEM_BIG),
    )(pos, *args)


def pack_small(pos, per_layer, loss_blk):
    L = len(per_layer)
    D = per_layer[0][0].shape[1]

    def body(pos_ref, *refs):
        o = refs[-1]
        lb = refs[-2]
        o[...] = jnp.zeros_like(o)
        for l in range(L):
            dgpre, dgpost, dsh, dsc, dgt, dps, dwc = refs[7 * l:7 * l + 7]
            base = SUBLANES * l
            for r, src in enumerate((dgpre, dgpost, dsh, dsc, dgt)):
                o[pl.ds(base + r, 1), :] = src[...]
            o[pl.ds(base + 5, 1), 0:dps.shape[1]] = dps[...]
            for j in range(dwc.shape[0]):
                for k in range(3):
                    idx = 3 * j + k
                    o[pl.ds(base + 6 + idx // 8, 1), (idx % 8) * LANES:(idx % 8 + 1) * LANES] = dwc[j, pl.ds(k, 1), :]
        o[pl.ds(5, 1), 4 * LANES:5 * LANES] = lb[pl.ds(0, 1), :]

    flat = [a for layer in per_layer for a in layer] + [loss_blk]

    def whole(a):
        return pl.BlockSpec(a.shape, lambda i, p: (0,) * a.ndim)

    return pl.pallas_call(
        body, name="pack_small",
        grid_spec=pltpu.PrefetchScalarGridSpec(
            num_scalar_prefetch=1, grid=(1,), in_specs=[whole(a) for a in flat],
            out_specs=pl.BlockSpec((None, L * SUBLANES, D), lambda i, p: (p[2], 0, 0))),
        out_shape=jax.ShapeDtypeStruct((N_DEV, L * SUBLANES, D), F32),
        compiler_params=_params(),
    )(pos, *flat)


def small_update(pos, packs, params, moments_m, moments_v):
    n = len(params)
    L, D = params[1].shape
    PS = params[3].shape[1]

    def body(pos_ref, p_ref, *refs):
        ws, ms, vs = refs[0:n], refs[n:2 * n], refs[2 * n:3 * n]
        loss_ref = refs[3 * n]
        outs = [refs[3 * n + 1 + 4 * t:3 * n + 5 + 4 * t] for t in range(n)]
        summed = refs[-1]
        s = p_ref[0]
        for d in range(1, N_DEV):
            s = s + p_ref[d]
        summed[...] = s
        loss_ref[...] = summed[pl.ds(5, 1), 4 * LANES:5 * LANES]
        chip = pos_ref[1]

        def update(t, idx, g):
            d, mm, vv = _adamw_math(ws[t][idx], g, ms[t][idx], vs[t][idx])
            g_ref, d_ref, mo_ref, vo_ref = outs[t]
            g_ref[idx] = g
            d_ref[idx] = d
            mo_ref[idx] = mm
            vo_ref[idx] = vv

        for l in range(L):
            base = SUBLANES * l
            row = pl.ds(l, 1)
            for k in range(3):
                update(0, (row, slice(k * D, (k + 1) * D)), summed[pl.ds(base + 2 + k, 1), :])
            update(1, (row, slice(None)), summed[pl.ds(base, 1), :])
            update(2, (row, slice(None)), summed[pl.ds(base + 1, 1), :])
            update(3, (row, slice(None)), summed[pl.ds(base + 5, 1), 0:PS])
            for k in range(3):
                g = None
                for j in range(N_CHIPS):
                    idx = 3 * j + k
                    cand = summed[pl.ds(base + 6 + idx // 8, 1), (idx % 8) * LANES:(idx % 8 + 1) * LANES]
                    g = cand if g is None else jnp.where(chip == j, cand, g)
                update(4, (l, pl.ds(k, 1), slice(None)), g)

    def whole(a):
        return pl.BlockSpec(a.shape, lambda i, p: (0,) * a.ndim)

    ins = [packs] + list(params) + list(moments_m) + list(moments_v)
    out_shape = [jax.ShapeDtypeStruct((1, LANES), F32)]
    for w in params:
        out_shape += [jax.ShapeDtypeStruct(w.shape, F32)] * 4
    outs = pl.pallas_call(
        body, name="small_update",
        grid_spec=pltpu.PrefetchScalarGridSpec(
            num_scalar_prefetch=1, grid=(1,), in_specs=[whole(a) for a in ins],
            out_specs=[whole(a) for a in out_shape],
            scratch_shapes=[pltpu.VMEM(packs.shape[1:], F32)]),
        out_shape=out_shape,
        compiler_params=_params(),
    )(pos, *ins)
    return outs[0], [outs[1 + 4 * t:5 + 4 * t] for t in range(n)]


def _adamw_math(w, g, m, v):
    m = ADAM_B1 * m + (1.0 - ADAM_B1) * g
    v = ADAM_B2 * v + (1.0 - ADAM_B2) * (g * g)
    m_hat = m / (1.0 - ADAM_B1 ** ADAM_STEP)
    v_hat = v / (1.0 - ADAM_B2 ** ADAM_STEP)
    delta = -ADAM_LR * (m_hat / (jnp.sqrt(v_hat) + ADAM_EPS) + ADAM_WD * w)
    return delta, m, v


def adamw(w, g, m, v, block, name, first=0, count=None, acc=None):
    grid = tuple(s // b for s, b in zip(w.shape, block))
    if count is not None:
        grid = (count,) + grid[1:]

    def body(w_ref, g_ref, m_ref, v_ref, *rest):
        go_ref, d_ref, mo_ref, vo_ref = rest[-4:]
        gv = g_ref[...]
        d, mm, vv = _adamw_math(w_ref[...], gv, m_ref[...], v_ref[...])
        go_ref[...] = gv
        d_ref[...] = d
        mo_ref[...] = mm
        vo_ref[...] = vv

    spec = pl.BlockSpec(block, lambda i, *rest: (first + i,) + rest)
    shape = jax.ShapeDtypeStruct(w.shape, F32)
    extra = [] if acc is None else list(acc)
    return pl.pallas_call(
        body, name=name, grid=grid,
        in_specs=[spec] * 4 + [ANY] * len(extra), out_specs=[spec] * 4, out_shape=[shape] * 4,
        input_output_aliases={4 + a: a for a in range(len(extra))},
        compiler_params=_params(VMEM_BIG, n_grid=len(grid)),
    )(w, g, m, v, *extra)


def ada_finish(c_all, dmod, w, m, v):
    L, D, CW = w.shape
    hD = D // 2

    def body(c_ref, d_ref, w_ref, m_ref, v_ref, g_ref, dl_ref, mo_ref, vo_ref):
        cv = c_ref[...]
        z = jnp.zeros_like(cv)
        ca = jnp.concatenate([cv * jax.nn.sigmoid(cv), z], axis=0).astype(BF16)
        dm = jnp.concatenate([d_ref[0], jnp.zeros_like(d_ref[0])], axis=0).astype(BF16)
        g = lax.dot_general(ca, dm, TN, preferred_element_type=F32)
        g_ref[0] = g
        d, mm, vv = _adamw_math(w_ref[0], g, m_ref[0], v_ref[0])
        dl_ref[0] = d
        mo_ref[0] = mm
        vo_ref[0] = vv

    big = pl.BlockSpec((1, hD, CW), lambda l, h: (l, h, 0))
    shape = jax.ShapeDtypeStruct(w.shape, F32)
    return pl.pallas_call(
        body, name="ada_finish", grid=(L, 2),
        in_specs=[pl.BlockSpec((N_DEV, hD), lambda l, h: (0, h)), pl.BlockSpec((1, N_DEV, CW), lambda l, h: (l, 0, 0)),
                  big, big, big],
        out_specs=[big] * 4, out_shape=[shape] * 4,
        compiler_params=_params(VMEM_BIG, n_grid=2),
    )(c_all, dmod, w, m, v)


def kernel(x, c, w_ada, b_ada, g_pre, w_in, w_conv, w_pool, pool_scale, w_out, g_post, loss_target, m_w_ada, m_b_ada, m_g_pre, m_w_in, m_w_conv, m_w_pool, m_pool_scale, m_w_out, m_g_post, v_w_ada, v_b_ada, v_g_pre, v_w_in, v_w_conv, v_w_pool, v_pool_scale, v_w_out, v_g_post):
    L, D, CW = w_in.shape
    RO = w_out.shape[1]
    T = x.shape[1]
    ix, iy, ic = _me()
    chip = 2 * ix + iy
    me_lin = 4 * ix + 2 * iy + ic

    pos = jnp.stack([ic, chip, me_lin]).astype(jnp.int32)
    g_pre3, g_post3 = g_pre.reshape(L, 1, D), g_post.reshape(L, 1, D)
    pscale3 = pool_scale.reshape(L, 1, pool_scale.shape[1])
    n_s, n_c = 3, 9

    def gather(bufs, after):
        ss, rs, bufs, tok = xchg_start("gather_start", bufs, 3 * len(bufs), plan_gather, after=after)
        return (ss, rs, bufs), tok

    def arrive(flight, after):
        ss, rs, bufs = flight
        bufs = xchg_wait("gather_wait", bufs, ss, rs, 3 * len(bufs), plan_gather, after)
        fss, frs, bufs, tok = xchg_start("forward_start", bufs, 3 * len(bufs), plan_forward, sibling_only=True)
        return (fss, frs, bufs), tok

    def ready(flight, after):
        fss, frs, bufs = flight
        return xchg_wait("forward_wait", bufs, fss, frs, 3 * len(bufs), plan_forward, after)

    c_all3, wconv_all = gather_small(c.reshape(SUBLANES, LANES), w_conv, pos)
    c_all = c_all3.reshape(N_DEV, D)
    gi0, go0 = cast_weights(pos, w_in, w_out, 0, c_all3)
    fly_in0, token = gather([gi0], [])
    b_my = lax.dynamic_slice_in_dim(b_ada, chip * CW, CW, axis=1)
    m_ss, m_rs, mods, token = xchg_start("mod_start", [mod_part(pos, c_all, w_ada, b_my, token)], 3, plan_mod)
    fly_out0, token = gather([go0], [token])
    flying_w = [None] * L
    for l in range(1, L):
        flying_w[l], token = gather(list(cast_weights(pos, w_in, w_out, l, token)), [])
    fwd_in0, token = arrive(fly_in0, [token])
    (mod_all,) = xchg_wait("mod_wait", mods, m_ss, m_rs, 3, plan_mod, [token])
    mod = lax.dynamic_index_in_dim(mod_all, me_lin, axis=2, keepdims=False)
    mod4 = jnp.transpose(mod, (1, 0, 2)).reshape(L, 3, 1, D)

    xs, projs, yas, yps, ys = [x.reshape(T, D)], [], [], [], []
    wg_in, wg_out = [], []
    fwd_next = None
    for l in range(L):
        if l == 0:
            (gi,) = ready(fwd_in0, [mod4])
        else:
            gi, go = ready(fwd_next, [xs[l]])
        proj = proj_fwd(xs[l], mod4, g_pre3, gi, l)
        ya = conv_fwd(proj, wconv_all, l)
        yp = pool_fwd(proj, w_pool, pscale3, l)
        after = [ya, yp]
        if l == 0:
            fwd_out0, token = arrive(fly_out0, after)
            after = [token]
        if l + 1 < L:
            fwd_next, token = arrive(flying_w[l + 1], after)
            after = [token]
        if l == 0:
            (go,) = ready(fwd_out0, after)
        wg_in.append(gi)
        wg_out.append(go.reshape(N_CHIPS * RO, D))
        projs.append(proj)
        yas.append(ya)
        yps.append(yp)
        if l + 1 < L:
            xn, yv = out_fwd(ya, yp, wg_out[l], xs[l], mod4, g_post3, l, after[0])
            xs.append(xn)
        else:
            dx, yv, loss_blk = out_fwd_loss(ya, yp, wg_out[l], xs[l], mod4, g_post3, l, loss_target.reshape(T, D))
        ys.append(yv)

    shapes = (w_in.shape, w_out.shape, w_pool.shape)
    smalls = [None] * L
    acc, flying, sib, token = None, None, None, loss_blk

    def to_chips(sib, after):
        sl, s_ss, s_rs, s_bufs = sib
        s_bufs = xchg_wait("sibling_wait", s_bufs, s_ss, s_rs, n_s, plan_sibling, after)
        chip_parts = add_sibling(pos, s_bufs[0:3], s_bufs[3:6])
        lands = [lax.empty((3,) + a.shape[1:], a.dtype) for a in chip_parts]
        c_ss, c_rs, c_bufs, ctoken = xchg_start("chip_start", list(chip_parts) + lands, n_c, plan_chip)
        return (sl, c_ss, c_rs, c_bufs), ctoken

    def landed(flying, acc, after):
        fl, f_ss, f_rs, f_bufs = flying
        f_bufs = xchg_wait("chip_wait", f_bufs, f_ss, f_rs, n_c, plan_chip, after)
        return sum_chips(pos, f_bufs[0:3], f_bufs[3:6], acc, fl, shapes)

    for l in reversed(range(L)):
        dya, dyp, dwo_l, dgate, dgpost = out_bwd(dx, ys[l], yas[l], yps[l], wg_out[l], mod4, g_post3, l, token)
        token = dya
        if sib is not None:
            arrived = flying
            flying, token = to_chips(sib, [dya])
            if arrived is not None:
                acc = landed(arrived, acc, [token])
                token = acc[0]
        du_p, dg_p, dwp_l, dps = pool_bwd(projs[l], dyp, w_pool, pscale3, l, token)
        du_a, db_a, dc_a, dg_a, dwc = conv_bwd(projs[l], dya, wconv_all, l)
        dx, dwi_l, dshift, dscale, dgpre = in_bwd([du_a, db_a, dc_a, dg_a, du_p, dg_p], wg_in[l], xs[l], dx,
                                                  mod4, g_pre3, l)
        smalls[l] = (dgpre, dgpost, dshift, dscale, dgate, dps, dwc)
        parts = [dwi_l, dwo_l.reshape(N_CHIPS, RO, D), dwp_l]
        s_lands = [lax.empty((a.shape[0], a.shape[1] // 2) + a.shape[2:], a.dtype) for a in parts]
        s_ss, s_rs, s_bufs, token = xchg_start("sibling_start", parts + s_lands, n_s, plan_sibling, sibling_only=True)
        sib = (l, s_ss, s_rs, s_bufs)
    grad_x = dx.reshape(1, T, D)

    p_ss, p_rs, packs, ptoken = xchg_start("pack_start", [pack_small(pos, smalls, loss_blk)], N_DEV - 1, plan_pack)
    acc = landed(flying, acc, [ptoken, token])
    n_sp = (2 + N_DEV - 1) * (L - 1)
    spread = plan_spread(tuple(range(1, L)), tuple(range(1, L)))
    sp_ss, sp_rs, acc, sp_token = xchg_start("spread_start", list(acc), n_sp, spread)
    flying, token = to_chips(sib, [sp_token])
    (packs_all,) = xchg_wait("pack_wait", packs, p_ss, p_rs, N_DEV - 1, plan_pack, [token])
    dmod_all = packs_all.reshape(N_DEV, L, SUBLANES, D)[:, :, 2:5].reshape(N_DEV, L, 3 * D)
    dmod_my = jnp.transpose(lax.dynamic_slice_in_dim(dmod_all, chip * CW, CW, axis=2), (1, 0, 2))

    g_w_ada, d_w_ada, nm_w_ada, nv_w_ada = ada_finish(c_all, dmod_my, w_ada, m_w_ada, v_w_ada)
    loss_row, upd = small_update(pos, packs_all, [b_ada, g_pre, g_post, pool_scale, w_conv],
                                 [m_b_ada, m_g_pre, m_g_post, m_pool_scale, m_w_conv],
                                 [v_b_ada, v_g_pre, v_g_post, v_pool_scale, v_w_conv])
    loss = loss_row[0, 0]
    (g_b_ada, d_b_ada, nm_b_ada, nv_b_ada), (g_g_pre, d_g_pre, nm_g_pre, nv_g_pre) = upd[0], upd[1]
    (g_g_post, d_g_post, nm_g_post, nv_g_post), (g_pscale, d_pscale, nm_pscale, nv_pscale) = upd[2], upd[3]
    g_w_conv, d_w_conv, nm_w_conv, nv_w_conv = upd[4]

    done = [nv_w_ada, nv_w_conv]
    g_w_in, g_w_out, g_w_pool = xchg_wait("spread_wait", acc, sp_ss, sp_rs, n_sp, spread, done)
    in_blk, out_blk = (1, D // 2, CW), (1, RO, D)
    upd_in = adamw(w_in, g_w_in, m_w_in, v_w_in, in_blk, "adamw_w_in", 1, L - 1)
    upd_out = adamw(w_out, g_w_out, m_w_out, v_w_out, out_blk, "adamw_w_out", 1, L - 1)

    acc = landed(flying, (g_w_in, g_w_out, g_w_pool), [upd_in[3], upd_out[3]])
    r_w_in, r_w_out, r_w_pool = spread_now(*acc, (0,), (0,))
    g_w_in, d_w_in, nm_w_in, nv_w_in = adamw(w_in, r_w_in, m_w_in, v_w_in, in_blk, "adamw_w_in", 0, 1, upd_in)
    g_w_out, d_w_out, nm_w_out, nv_w_out = adamw(w_out, r_w_out, m_w_out, v_w_out, out_blk, "adamw_w_out", 0, 1, upd_out)
    pshape = (L, N_CHIPS * LANES, LANES)
    upd_pool = adamw(w_pool.reshape(pshape), r_w_pool.reshape(pshape), m_w_pool.reshape(pshape),
                     v_w_pool.reshape(pshape), (1,) + pshape[1:], "adamw_w_pool")
    g_w_pool, d_w_pool, nm_w_pool, nv_w_pool = [a.reshape(w_pool.shape) for a in upd_pool]

    return (loss, grad_x,
            g_w_ada, g_b_ada, g_g_pre, g_w_in, g_w_conv, g_w_pool, g_pscale, g_w_out, g_g_post,
            d_w_ada, d_b_ada, d_g_pre, d_w_in, d_w_conv, d_w_pool, d_pscale, d_w_out, d_g_post,
            nm_w_ada, nm_b_ada, nm_g_pre, nm_w_in, nm_w_conv, nm_w_pool, nm_pscale, nm_w_out, nm_g_post,
            nv_w_ada, nv_b_ada, nv_g_pre, nv_w_in, nv_w_conv, nv_w_pool, nv_pscale, nv_w_out, nv_g_post)
```

```python
import functools

import jax
import jax.numpy as jnp
from jax import lax
from jax.experimental import pallas as pl
from jax.experimental.pallas import tpu as pltpu

F32 = jnp.float32
BF16 = jnp.bfloat16
MESH = pl.DeviceIdType.MESH
ANY = pl.BlockSpec(memory_space=pl.ANY)

NORM_EPS = 1e-6
POOL_WINDOWS = (2, 4, 8, 16)
ADAM_LR = 0.001
ADAM_B1 = 0.9
ADAM_B2 = 0.999
ADAM_EPS = 1e-08
ADAM_WD = 0.01
ADAM_STEP = 10

N_CHIPS = 4
N_DEV = 8
LANES = 128
SUBLANES = 8
VMEM_BIG = 56 * 1024 * 1024
HIST = 16
R_CONV = 64
R_POOL = 128

NT = (((1,), (1,)), ((), ()))
TN = (((0,), (0,)), ((), ()))


def _params(vmem=None, n_grid=1):
    kw = {}
    if n_grid:
        kw["dimension_semantics"] = ("arbitrary",) * n_grid
    if vmem is not None:
        kw["vmem_limit_bytes"] = vmem
    return pltpu.CompilerParams(**kw)


def _colsum8(v):
    n, d = v.shape
    return v.reshape(n // SUBLANES, SUBLANES, d).sum(axis=0)


def _rms(v):
    return lax.rsqrt(jnp.mean(v * v, axis=-1, keepdims=True) + NORM_EPS)


def _sigmoid(v):
    return 0.5 * jnp.tanh(0.5 * v) + 0.5


def _shift_down(ext, k, rows):
    if k == 0:
        return ext[HIST:HIST + rows]
    return pltpu.roll(ext, k, 0)[HIST:HIST + rows]


def _shift_up(ext, k, rows):
    if k == 0:
        return ext[0:rows]
    return pltpu.roll(ext, ext.shape[0] - k, 0)[0:rows]


def _load_ext(ref, r0, h0, first, rows):
    hist = ref[pl.ds(h0, HIST), :].astype(F32)
    hist = jnp.where(first, 0.0, hist)
    cur = ref[pl.ds(r0, rows), :].astype(F32)
    return jnp.concatenate([hist, cur], axis=0)


def _me():
    return lax.axis_index("x"), lax.axis_index("y"), lax.axis_index("c")


def cast_weights(pos, w_in, w_out, l, after):
    _, D, CW = w_in.shape
    RO = w_out.shape[1]

    def body(pos_ref, wi, wo, after_ref, oi, oo):
        oi[...] = wi[...].astype(BF16)
        oo[...] = wo[...].astype(BF16)

    return pl.pallas_call(
        body, name="cast_w",
        grid_spec=pltpu.PrefetchScalarGridSpec(
            num_scalar_prefetch=1, grid=(2,),
            in_specs=[pl.BlockSpec((None, D // 2, CW), lambda h, p: (l, h, 0)),
                      pl.BlockSpec((None, RO // 2, D), lambda h, p: (l, h, 0)), ANY],
            out_specs=[pl.BlockSpec((D // 2, CW), lambda h, p: (h, p[1])),
                       pl.BlockSpec((None, RO // 2, D), lambda h, p: (p[1], h, 0))]),
        out_shape=[jax.ShapeDtypeStruct((D, N_CHIPS * CW), BF16), jax.ShapeDtypeStruct((N_CHIPS, RO, D), BF16)],
        compiler_params=_params(),
    )(pos, w_in, w_out, after)


def mod_part(pos, c_all, w_ada, b_my, after):
    L, D, CW = w_ada.shape

    def body(pos_ref, c_ref, w_ref, b_ref, after_ref, o_ref):
        cv = c_ref[...]
        ca = (cv * jax.nn.sigmoid(cv)).astype(BF16)
        o_ref[...] = jnp.dot(ca, w_ref[0].astype(BF16), preferred_element_type=F32) + b_ref[0]

    return pl.pallas_call(
        body, name="mod_part",
        grid_spec=pltpu.PrefetchScalarGridSpec(
            num_scalar_prefetch=1, grid=(L,),
            in_specs=[pl.BlockSpec((N_DEV, D), lambda l, p: (0, 0)),
                      pl.BlockSpec((1, D, CW), lambda l, p: (l, 0, 0)),
                      pl.BlockSpec((1, 1, CW), lambda l, p: (l, 0, 0)), ANY],
            out_specs=pl.BlockSpec((None, None, N_DEV, CW), lambda l, p: (p[1], l, 0, 0))),
        out_shape=jax.ShapeDtypeStruct((N_CHIPS, L, N_DEV, CW), F32),
        compiler_params=_params(VMEM_BIG),
    )(pos, c_all, w_ada, b_my.reshape(L, 1, CW), after)


def _mod_row(l, k, D):
    return pl.BlockSpec((None, None, 1, D), lambda *_: (l, k, 0, 0))


def _layer_row(l, D):
    return pl.BlockSpec((None, 1, D), lambda *_: (l, 0, 0))


def proj_fwd(x, mod4, g_pre3, wg, l):
    T, D = x.shape
    NC = wg.shape[1]
    NB = N_CHIPS
    CW = NC // NB
    tm = 512

    def body(x_ref, sh_ref, sc_ref, g_ref, w_ref, o_ref):
        xv = x_ref[...]
        h = (xv * _rms(xv) * g_ref[...]) * (1.0 + sc_ref[...]) + sh_ref[...]
        hb = h.astype(BF16)
        for j in range(NB):
            cols = slice(j * CW, (j + 1) * CW)
            o_ref[:, cols] = jnp.dot(hb, w_ref[:, cols], preferred_element_type=F32).astype(BF16)

    return pl.pallas_call(
        body, name="proj_fwd", grid=(T // tm,),
        in_specs=[pl.BlockSpec((tm, D), lambda i: (i, 0)), _mod_row(l, 0, D), _mod_row(l, 1, D), _layer_row(l, D),
                  pl.BlockSpec((D, NC), lambda i: (0, 0))],
        out_specs=pl.BlockSpec((tm, NC), lambda i: (i, 0)),
        out_shape=jax.ShapeDtypeStruct((T, NC), BF16),
        compiler_params=_params(VMEM_BIG),
    )(x, mod4, mod4, g_pre3, wg)


def conv_fwd(proj, wconv, l):
    T = proj.shape[0]
    R = R_CONV
    nblk = 4

    def body(u_ref, b_ref, c_ref, g_ref, w_ref, o_ref):
        w0 = w_ref[pl.ds(0, 1), :]
        w1 = w_ref[pl.ds(1, 1), :]
        w2 = w_ref[pl.ds(2, 1), :]

        def chunk(i, carry):
            r0 = pl.multiple_of(i * R, R)
            h0 = pl.multiple_of(jnp.maximum(r0 - HIST, 0), HIST)
            first = i == 0
            ca = _load_ext(c_ref, r0, h0, first, R) * _load_ext(u_ref, r0, h0, first, R)
            conv = w2 * ca[HIST:] + w1 * _shift_down(ca, 1, R) + w0 * _shift_down(ca, 2, R)
            g = g_ref[pl.ds(r0, R), :].astype(F32)
            b = b_ref[pl.ds(r0, R), :].astype(F32)
            o_ref[pl.ds(r0, R), :] = (b * conv * (g * _sigmoid(g))).astype(BF16)
            return carry

        lax.fori_loop(0, T // R, chunk, 0)

    def col(off):
        return pl.BlockSpec((T, LANES), lambda j: (0, j + off))

    return pl.pallas_call(
        body, name="conv_fwd", grid=(nblk,),
        in_specs=[col(0), col(4), col(8), col(12), pl.BlockSpec((None, None, 3, LANES), lambda j: (j, l, 0, 0))],
        out_specs=pl.BlockSpec((T, LANES), lambda j: (0, j)),
        out_shape=jax.ShapeDtypeStruct((T, nblk * LANES), BF16),
        compiler_params=_params(),
    )(proj, proj, proj, proj, wconv)


def _causal_window_sum(ext, w):
    s, k = ext, 1
    while k < w:
        s = s + pltpu.roll(s, k, 0)
        k *= 2
    return s


def _anticausal_window_sum(ext, w):
    s, k = ext, 1
    n = ext.shape[0]
    while k < w:
        s = s + pltpu.roll(s, n - k, 0)
        k *= 2
    return s


def _count(r0, rows, w):
    t = r0 + lax.broadcasted_iota(jnp.int32, (rows, LANES), 0)
    return jnp.minimum(t + 1, w).astype(F32)


def _pooled_loop(p_ref, pooled_s, w, T):
    R = R_POOL

    def chunk(i, carry):
        r0 = pl.multiple_of(i * R, R)
        h0 = pl.multiple_of(jnp.maximum(r0 - HIST, 0), HIST)
        ext = _load_ext(p_ref, r0, h0, i == 0, R)
        ws = _causal_window_sum(ext, w)[HIST:]
        pooled_s[pl.ds(r0, R), :] = (ws / _count(r0, R, w) - ext[HIST:]).astype(BF16)
        return carry

    lax.fori_loop(0, T // R, chunk, 0)


def _pool_w_spec(l):
    return pl.BlockSpec((None, None, LANES, LANES), lambda j: (l, j, 0, 0))


def _pool_s_spec(l):
    return pl.BlockSpec((None, 1, LANES), lambda j: (l, 0, j))


def pool_fwd(proj, wpool, pscale3, l):
    T = proj.shape[0]
    R = R_POOL
    ngrp = len(POOL_WINDOWS)

    def body(p_ref, g_ref, w_ref, s_ref, o_ref, pooled_s, mixed_s):
        grp = pl.program_id(0)

        def group(w):
            _pooled_loop(p_ref, pooled_s, w, T)
            mixed_s[...] = jnp.dot(pooled_s[...], w_ref[...].astype(BF16), preferred_element_type=F32)
            sc = s_ref[...]

            def chunk(i, carry):
                r0 = pl.multiple_of(i * R, R)
                g = g_ref[pl.ds(r0, R), :].astype(F32)
                o_ref[pl.ds(r0, R), :] = (mixed_s[pl.ds(r0, R), :] * sc * (g * _sigmoid(g))).astype(BF16)
                return carry

            lax.fori_loop(0, T // R, chunk, 0)

        for k, w in enumerate(POOL_WINDOWS):
            pl.when(grp == k)(functools.partial(group, w))

    return pl.pallas_call(
        body, name="pool_fwd", grid=(ngrp,),
        in_specs=[pl.BlockSpec((T, LANES), lambda j: (0, j + 16)), pl.BlockSpec((T, LANES), lambda j: (0, j + 20)),
                  _pool_w_spec(l), _pool_s_spec(l)],
        out_specs=pl.BlockSpec((T, LANES), lambda j: (0, j)),
        out_shape=jax.ShapeDtypeStruct((T, ngrp * LANES), BF16),
        scratch_shapes=[pltpu.VMEM((T, LANES), BF16), pltpu.VMEM((T, LANES), F32)],
        compiler_params=_params(),
    )(proj, proj, wpool, pscale3)


def out_fwd(ya, yp, wo, x, mod4, g_post3, l, after):
    T, D = x.shape
    H = ya.shape[1]
    tm = 512

    def body(ya_ref, yp_ref, wo_ref, x_ref, gt_ref, g_ref, after_ref, xn_ref, y_ref):
        y = (jnp.dot(ya_ref[...], wo_ref[0:H, :], preferred_element_type=F32)
             + jnp.dot(yp_ref[...], wo_ref[H:2 * H, :], preferred_element_type=F32))
        xn_ref[...] = x_ref[...] + gt_ref[...] * (y * _rms(y) * g_ref[...])
        y_ref[...] = y.astype(BF16)

    tile = pl.BlockSpec((tm, D), lambda i: (i, 0))
    half = pl.BlockSpec((tm, H), lambda i: (i, 0))
    return pl.pallas_call(
        body, name="out_fwd", grid=(T // tm,),
        in_specs=[half, half, pl.BlockSpec((2 * H, D), lambda i: (0, 0)), tile, _mod_row(l, 2, D), _layer_row(l, D),
                  ANY],
        out_specs=[tile, tile],
        out_shape=[jax.ShapeDtypeStruct((T, D), F32), jax.ShapeDtypeStruct((T, D), BF16)],
        compiler_params=_params(VMEM_BIG),
    )(ya, yp, wo, x, mod4, g_post3, after)


def out_fwd_loss(ya, yp, wo, x, mod4, g_post3, l, target):
    T, D = x.shape
    H = ya.shape[1]
    tm = 512
    nt = T // tm

    def body(ya_ref, yp_ref, wo_ref, x_ref, gt_ref, g_ref, t_ref, dx_ref, y_ref, l_ref, acc):
        i = pl.program_id(0)

        @pl.when(i == 0)
        def _():
            acc[...] = jnp.zeros_like(acc)

        y = (jnp.dot(ya_ref[...], wo_ref[0:H, :], preferred_element_type=F32)
             + jnp.dot(yp_ref[...], wo_ref[H:2 * H, :], preferred_element_type=F32))
        y_ref[...] = y.astype(BF16)
        d = (x_ref[...] + gt_ref[...] * (y * _rms(y) * g_ref[...])) - t_ref[...]
        dx_ref[...] = d * (1.0 / D)
        acc[...] += _colsum8(d * d)

        @pl.when(i == nt - 1)
        def _():
            l_ref[...] = jnp.zeros_like(l_ref) + jnp.sum(acc[...]) * (0.5 / D)

    tile = pl.BlockSpec((tm, D), lambda i: (i, 0))
    half = pl.BlockSpec((tm, H), lambda i: (i, 0))
    return pl.pallas_call(
        body, name="out_fwd_loss", grid=(nt,),
        in_specs=[half, half, pl.BlockSpec((2 * H, D), lambda i: (0, 0)), tile, _mod_row(l, 2, D), _layer_row(l, D),
                  tile],
        out_specs=[tile, tile, pl.BlockSpec((SUBLANES, LANES), lambda i: (0, 0))],
        out_shape=[jax.ShapeDtypeStruct((T, D), F32), jax.ShapeDtypeStruct((T, D), BF16),
                   jax.ShapeDtypeStruct((SUBLANES, LANES), F32)],
        scratch_shapes=[pltpu.VMEM((SUBLANES, D), F32)],
        compiler_params=_params(VMEM_BIG),
    )(ya, yp, wo, x, mod4, g_post3, target)


def out_bwd(dx, y, ya, yp, wo, mod4, g_post3, l, after):
    T, D = dx.shape
    H = ya.shape[1]
    tm = 512
    nt = T // tm

    def body(dx_ref, y_ref, ya_ref, yp_ref, wo_ref, gt_ref, g_ref, after_ref,
             dya_ref, dyp_ref, dwo_ref, dgt_ref, dg_ref, acc_w, acc_p):
        i = pl.program_id(0)

        @pl.when(i == 0)
        def _():
            acc_w[...] = jnp.zeros_like(acc_w)
            acc_p[...] = jnp.zeros_like(acc_p)

        yv = y_ref[...].astype(F32)
        dxv = dx_ref[...]
        gg = gt_ref[...] * g_ref[...]
        r = _rms(yv)
        yn = yv * r
        p = dxv * yn
        acc_p[...] += _colsum8(p)
        dy = r * (dxv * gg - yn * jnp.mean(p * gg, axis=-1, keepdims=True))
        dyb = dy.astype(BF16)
        dyc = lax.dot_general(dyb, wo_ref[...], NT, preferred_element_type=F32)
        dya_ref[...] = dyc[:, 0:H].astype(BF16)
        dyp_ref[...] = dyc[:, H:2 * H].astype(BF16)
        acc_w[0:H, :] += lax.dot_general(ya_ref[...], dyb, TN, preferred_element_type=F32)
        acc_w[H:2 * H, :] += lax.dot_general(yp_ref[...], dyb, TN, preferred_element_type=F32)

        @pl.when(i == nt - 1)
        def _():
            dwo_ref[...] = acc_w[...].astype(BF16)
            sp = jnp.sum(acc_p[...], axis=0, keepdims=True)
            dgt_ref[...] = g_ref[...] * sp
            dg_ref[...] = gt_ref[...] * sp

    row = pl.BlockSpec((1, D), lambda i: (0, 0))
    tile = pl.BlockSpec((tm, D), lambda i: (i, 0))
    half = pl.BlockSpec((tm, H), lambda i: (i, 0))
    full = pl.BlockSpec((2 * H, D), lambda i: (0, 0))
    return pl.pallas_call(
        body, name="out_bwd", grid=(nt,),
        in_specs=[tile, tile, half, half, full, _mod_row(l, 2, D), _layer_row(l, D), ANY],
        out_specs=[half, half, full, row, row],
        out_shape=[jax.ShapeDtypeStruct((T, H), BF16), jax.ShapeDtypeStruct((T, H), BF16),
                   jax.ShapeDtypeStruct((2 * H, D), BF16),
                   jax.ShapeDtypeStruct((1, D), F32), jax.ShapeDtypeStruct((1, D), F32)],
        scratch_shapes=[pltpu.VMEM((2 * H, D), F32), pltpu.VMEM((SUBLANES, D), F32)],
        compiler_params=_params(VMEM_BIG),
    )(dx, y, ya, yp, wo, mod4, g_post3, after)


def conv_bwd(proj, dya, wconv, l):
    T = proj.shape[0]
    R = R_CONV
    nblk = 4
    nchunk = T // R

    def body(u_ref, b_ref, c_ref, g_ref, dy_ref, w_ref, du_ref, db_ref, dc_ref, dg_ref, dw_ref):
        w0 = w_ref[pl.ds(0, 1), :]
        w1 = w_ref[pl.ds(1, 1), :]
        w2 = w_ref[pl.ds(2, 1), :]

        def chunk(k, carry):
            head, a0, a1, a2 = carry
            i = nchunk - 1 - k
            r0 = pl.multiple_of(i * R, R)
            h0 = pl.multiple_of(jnp.maximum(r0 - HIST, 0), HIST)
            first = i == 0
            ue = _load_ext(u_ref, r0, h0, first, R)
            ce = _load_ext(c_ref, r0, h0, first, R)
            ca = ce * ue
            ca0 = ca[HIST:]
            ca1 = _shift_down(ca, 1, R)
            ca2 = _shift_down(ca, 2, R)
            conv = w2 * ca0 + w1 * ca1 + w0 * ca2
            g = g_ref[pl.ds(r0, R), :].astype(F32)
            b = b_ref[pl.ds(r0, R), :].astype(F32)
            dy = dy_ref[pl.ds(r0, R), :].astype(F32)
            sg = _sigmoid(g)
            sl = g * sg
            t = dy * conv
            db_ref[pl.ds(r0, R), :] = (t * sl).astype(BF16)
            dg_ref[pl.ds(r0, R), :] = (t * b * (sg * (1.0 + g * (1.0 - sg)))).astype(BF16)
            dconv = dy * b * sl
            a2 = a2 + _colsum8(dconv * ca0)
            a1 = a1 + _colsum8(dconv * ca1)
            a0 = a0 + _colsum8(dconv * ca2)
            e = jnp.concatenate([dconv, head], axis=0)
            dca = w2 * dconv + w1 * _shift_up(e, 1, R) + w0 * _shift_up(e, 2, R)
            du_ref[pl.ds(r0, R), :] = (dca * ce[HIST:]).astype(BF16)
            dc_ref[pl.ds(r0, R), :] = (dca * ue[HIST:]).astype(BF16)
            return dconv[0:SUBLANES], a0, a1, a2

        z = jnp.zeros((SUBLANES, LANES), F32)
        _, a0, a1, a2 = lax.fori_loop(0, nchunk, chunk, (z, z, z, z))
        dw_ref[pl.ds(0, 1), :] = jnp.sum(a0, axis=0, keepdims=True)
        dw_ref[pl.ds(1, 1), :] = jnp.sum(a1, axis=0, keepdims=True)
        dw_ref[pl.ds(2, 1), :] = jnp.sum(a2, axis=0, keepdims=True)

    def col(off):
        return pl.BlockSpec((T, LANES), lambda j: (0, j + off))

    sec = jax.ShapeDtypeStruct((T, nblk * LANES), BF16)
    return pl.pallas_call(
        body, name="conv_bwd", grid=(nblk,),
        in_specs=[col(0), col(4), col(8), col(12), col(0), pl.BlockSpec((None, None, 3, LANES), lambda j: (j, l, 0, 0))],
        out_specs=[col(0), col(0), col(0), col(0), pl.BlockSpec((None, 3, LANES), lambda j: (j, 0, 0))],
        out_shape=[sec, sec, sec, sec, jax.ShapeDtypeStruct((nblk, 3, LANES), F32)],
        compiler_params=_params(),
    )(proj, proj, proj, proj, dya, wconv)


def pool_bwd(proj, dyp, wpool, pscale3, l, after):
    T = proj.shape[0]
    R = R_POOL
    ngrp = len(POOL_WINDOWS)
    nchunk = T // R

    def body(p_ref, g_ref, dy_ref, w_ref, s_ref, after_ref, du_ref, dg_ref, dw_ref, ds_ref,
             pooled_s, mixed_s, dmix_s, dpool_s):
        grp = pl.program_id(0)

        def group(w):
            wb = w_ref[...].astype(BF16)
            _pooled_loop(p_ref, pooled_s, w, T)
            mixed_s[...] = jnp.dot(pooled_s[...], wb, preferred_element_type=F32)
            sc = s_ref[...]

            def gate_chunk(i, acc):
                r0 = pl.multiple_of(i * R, R)
                g = g_ref[pl.ds(r0, R), :].astype(F32)
                dy = dy_ref[pl.ds(r0, R), :].astype(F32)
                mixed = mixed_s[pl.ds(r0, R), :]
                sg = _sigmoid(g)
                dg_ref[pl.ds(r0, R), :] = (dy * mixed * sc * (sg * (1.0 + g * (1.0 - sg)))).astype(BF16)
                dms = dy * (g * sg)
                dmix_s[pl.ds(r0, R), :] = (dms * sc).astype(BF16)
                return acc + _colsum8(dms * mixed)

            acc = lax.fori_loop(0, nchunk, gate_chunk, jnp.zeros((SUBLANES, LANES), F32))
            ds_ref[...] = jnp.sum(acc, axis=0, keepdims=True)
            dpool_s[pl.ds(0, T), :] = lax.dot_general(dmix_s[...], wb, NT, preferred_element_type=F32)
            dpool_s[pl.ds(T, HIST), :] = jnp.zeros((HIST, LANES), F32)
            dw_ref[...] = lax.dot_general(pooled_s[...], dmix_s[...], TN, preferred_element_type=F32).astype(BF16)

            def back_chunk(i, carry):
                r0 = pl.multiple_of(i * R, R)
                dpe = dpool_s[pl.ds(r0, R + HIST), :]
                e = dpe / _count(r0, R + HIST, w)
                du_ref[pl.ds(r0, R), :] = (_anticausal_window_sum(e, w)[0:R] - dpe[0:R]).astype(BF16)
                return carry

            lax.fori_loop(0, nchunk, back_chunk, 0)

        for k, w in enumerate(POOL_WINDOWS):
            pl.when(grp == k)(functools.partial(group, w))

    def col(off):
        return pl.BlockSpec((T, LANES), lambda j: (0, j + off))

    sec = jax.ShapeDtypeStruct((T, ngrp * LANES), BF16)
    wspec = pl.BlockSpec((None, LANES, LANES), lambda j: (j, 0, 0))
    sspec = pl.BlockSpec((1, LANES), lambda j: (0, j))
    return pl.pallas_call(
        body, name="pool_bwd", grid=(ngrp,),
        in_specs=[col(16), col(20), col(0), _pool_w_spec(l), _pool_s_spec(l), ANY],
        out_specs=[col(0), col(0), wspec, sspec],
        out_shape=[sec, sec, jax.ShapeDtypeStruct((ngrp, LANES, LANES), BF16),
                   jax.ShapeDtypeStruct((1, ngrp * LANES), F32)],
        scratch_shapes=[pltpu.VMEM((T, LANES), BF16), pltpu.VMEM((T, LANES), F32),
                        pltpu.VMEM((T, LANES), BF16), pltpu.VMEM((T + HIST, LANES), F32)],
        compiler_params=_params(),
    )(proj, proj, dyp, wpool, pscale3, after)


def in_bwd(dsecs, wg, x, dxo, mod4, g_pre3, l):
    T, D = x.shape
    NB = N_CHIPS
    CW = wg.shape[1] // NB
    SW = dsecs[0].shape[1]
    nsec = len(dsecs)
    PW = 256
    assert SW % PW == 0 and CW % PW == 0
    tm = 256
    nt = T // tm

    def body(*refs):
        d_refs = refs[0:nsec]
        w_ref, x_ref, dxo_ref, sh_ref, sc_ref, g_ref = refs[nsec:nsec + 6]
        dxi_ref, dw_ref, dsh_ref, dsc_ref, dg_ref = refs[nsec + 6:nsec + 11]
        acc_w, acc_sh, acc_q = refs[nsec + 11:]
        i = pl.program_id(0)

        @pl.when(i == 0)
        def _():
            acc_w[...] = jnp.zeros_like(acc_w)
            acc_sh[...] = jnp.zeros_like(acc_sh)
            acc_q[...] = jnp.zeros_like(acc_q)

        xv = x_ref[...]
        r = _rms(xv)
        xh = xv * r
        sg = g_ref[...] * (1.0 + sc_ref[...])
        hb = (xh * sg + sh_ref[...]).astype(BF16)
        dh = lax.dot_general(d_refs[0][...], w_ref[:, 0:SW], NT, preferred_element_type=F32)
        for s in range(1, nsec):
            dh = dh + lax.dot_general(d_refs[s][...], w_ref[:, s * SW:(s + 1) * SW], NT, preferred_element_type=F32)
        for p in range(nsec * SW // PW):
            col = p * PW
            s, so = col // SW, col % SW
            j, jo = col // CW, col % CW
            acc_w[j, :, jo:jo + PW] += lax.dot_general(hb, d_refs[s][:, so:so + PW], TN, preferred_element_type=F32)
        q = dh * xh
        acc_sh[...] += _colsum8(dh)
        acc_q[...] += _colsum8(q)
        dxi_ref[...] = dxo_ref[...] + r * (dh * sg - xh * jnp.mean(q * sg, axis=-1, keepdims=True))

        @pl.when(i == nt - 1)
        def _():
            dw_ref[...] = acc_w[...].astype(BF16)
            sq = jnp.sum(acc_q[...], axis=0, keepdims=True)
            dsh_ref[...] = jnp.sum(acc_sh[...], axis=0, keepdims=True)
            dsc_ref[...] = g_ref[...] * sq
            dg_ref[...] = (1.0 + sc_ref[...]) * sq

    row = pl.BlockSpec((1, D), lambda i: (0, 0))
    tile = pl.BlockSpec((tm, D), lambda i: (i, 0))
    sect = pl.BlockSpec((tm, SW), lambda i: (i, 0))
    rowshape = jax.ShapeDtypeStruct((1, D), F32)
    return pl.pallas_call(
        body, name="in_bwd", grid=(nt,),
        in_specs=[sect] * nsec + [pl.BlockSpec((D, NB * CW), lambda i: (0, 0)), tile, tile,
                                  _mod_row(l, 0, D), _mod_row(l, 1, D), _layer_row(l, D)],
        out_specs=[tile, pl.BlockSpec((NB, D, CW), lambda i: (0, 0, 0)), row, row, row],
        out_shape=[jax.ShapeDtypeStruct((T, D), F32), jax.ShapeDtypeStruct((NB, D, CW), BF16),
                   rowshape, rowshape, rowshape],
        scratch_shapes=[pltpu.VMEM((NB, D, CW), F32),
                        pltpu.VMEM((SUBLANES, D), F32), pltpu.VMEM((SUBLANES, D), F32)],
        compiler_params=_params(VMEM_BIG),
    )(*dsecs, wg, x, dxo, mod4, mod4, g_pre3)


def _rcopy(src, dst, ssem, rsem, dev):
    return pltpu.make_async_remote_copy(src_ref=src, dst_ref=dst, send_sem=ssem, recv_sem=rsem,
                                        device_id=dev, device_id_type=MESH)


def _peers7(x, y, c):
    out = []
    for m in range(1, N_DEV):
        bx, by, bc = (m >> 2) & 1, (m >> 1) & 1, m & 1
        out.append(((1 - x) if bx else x, (1 - y) if by else y, (1 - c) if bc else c))
    return out


HBM = pl.BlockSpec(memory_space=pltpu.HBM)
SEM = pl.BlockSpec(memory_space=pltpu.SEMAPHORE)
SPLIT = pltpu.CompilerParams(has_side_effects=pltpu.SideEffectType.DATAFLOW_SIDE_EFFECTING)


def _hbm(a):
    return pltpu.with_memory_space_constraint(a, pltpu.HBM)


def _chips(x, y):
    return [(1 - x, y), (x, 1 - y), (1 - x, 1 - y)]


SIBLING_BARRIER_ID = 0


def xchg_start(name, bufs, n_copies, plan, sibling_only=False, after=()):
    n = len(bufs)
    after = list(after)

    def body(*refs):
        ssem, rsem, token = refs[n + len(after)], refs[n + len(after) + 1], refs[-1]
        x, y, c = _me()
        if sibling_only:
            barrier = pltpu.get_barrier_semaphore()
            pl.semaphore_signal(barrier, inc=1, device_id=(x, y, 1 - c), device_id_type=MESH)
            pl.semaphore_wait(barrier, 1)
        copies = plan(refs[0:n], x, y, c)
        assert len(copies) == n_copies
        for k, (src, dst, peer, _) in enumerate(copies):
            _rcopy(src, dst, ssem.at[k], rsem.at[k], peer).start()
        token[...] = jnp.zeros_like(token)

    params = dict(has_side_effects=pltpu.SideEffectType.DATAFLOW_SIDE_EFFECTING)
    if sibling_only:
        params["collective_id"] = SIBLING_BARRIER_ID
    outs = pl.pallas_call(
        body, name=name,
        in_specs=[HBM] * n + [ANY] * len(after),
        out_specs=[SEM, SEM] + [HBM] * n + [pl.BlockSpec(memory_space=pltpu.VMEM)],
        out_shape=([pltpu.SemaphoreType.DMA((n_copies,))] * 2 + [pltpu.HBM(b.shape, b.dtype) for b in bufs]
                   + [jax.ShapeDtypeStruct((SUBLANES, LANES), F32)]),
        input_output_aliases={a: 2 + a for a in range(n)},
        compiler_params=pltpu.CompilerParams(**params),
    )(*[_hbm(b) for b in bufs], *after)
    return outs[0], outs[1], list(outs[2:2 + n]), outs[-1]


def xchg_wait(name, bufs, ssem, rsem, n_copies, plan, after):
    n = len(bufs)
    after = list(after)

    def body(*refs):
        ssem_ref, rsem_ref = refs[n], refs[n + 1]
        copies = plan(refs[0:n], *_me())
        assert len(copies) == n_copies
        for k, (src, _, peer, land) in enumerate(copies):
            cp = _rcopy(src, land, ssem_ref.at[k], rsem_ref.at[k], peer)
            cp.wait_send()
            cp.wait_recv()

    outs = pl.pallas_call(
        body, name=name,
        in_specs=[HBM] * n + [SEM, SEM] + [ANY] * len(after), out_specs=[HBM] * n,
        out_shape=[pltpu.HBM(b.shape, b.dtype) for b in bufs],
        input_output_aliases={a: a for a in range(n)},
        compiler_params=SPLIT,
    )(*bufs, ssem, rsem, *after)
    return list(outs)


def _shard_half(buf, chip, half):
    if len(buf.shape) == 2:
        h, w = buf.shape[0] // 2, buf.shape[1] // N_CHIPS
        return buf.at[pl.ds(half * h, h), pl.ds(chip * w, w)]
    h = buf.shape[1] // 2
    return buf.at[chip, pl.ds(half * h, h)]


def plan_gather(refs, x, y, c):
    out = []
    for (px, py) in _chips(x, y):
        for buf in refs:
            own = _shard_half(buf, 2 * x + y, c)
            out.append((own, own, (px, py, c), _shard_half(buf, 2 * px + py, c)))
    return out


def plan_forward(refs, x, y, c):
    out = []
    for (px, py) in _chips(x, y):
        for buf in refs:
            landed = _shard_half(buf, 2 * px + py, c)
            out.append((landed, landed, (x, y, 1 - c), _shard_half(buf, 2 * px + py, 1 - c)))
    return out


def plan_sibling(refs, x, y, c):
    n = len(refs) // 2
    out = []
    for a in range(n):
        h = refs[a].shape[1] // 2
        out.append((refs[a].at[:, pl.ds((1 - c) * h, h)], refs[n + a], (x, y, 1 - c), refs[n + a]))
    return out


def plan_chip(refs, x, y, c):
    n = len(refs) // 2
    out = []
    for j, (px, py) in enumerate(_chips(x, y)):
        for a in range(n):
            out.append((refs[a].at[2 * px + py], refs[n + a].at[j], (px, py, c), refs[n + a].at[j]))
    return out


def plan_mod(refs, x, y, c):
    (mods,) = refs
    mine = mods.at[2 * x + y]
    return [(mine, mine, (px, py, c), mods.at[2 * px + py]) for (px, py) in _chips(x, y)]


def plan_pack(refs, x, y, c):
    (packs,) = refs
    mine = packs.at[4 * x + 2 * y + c]
    return [(mine, mine, peer, packs.at[4 * peer[0] + 2 * peer[1] + peer[2]]) for peer in _peers7(x, y, c)]


def plan_spread(layers, wp_layers):
    def plan(refs, x, y, c):
        gi, go, gp = refs
        hD, hR, hP = gi.shape[1] // 2, go.shape[1] // 2, gp.shape[2] // 2
        sib = (x, y, 1 - c)
        out = []
        for l in layers:
            mine = gi.at[l, pl.ds(c * hD, hD)]
            out.append((mine, mine, sib, gi.at[l, pl.ds((1 - c) * hD, hD)]))
            mine = go.at[l, pl.ds(c * hR, hR)]
            out.append((mine, mine, sib, go.at[l, pl.ds((1 - c) * hR, hR)]))
        for l in wp_layers:
            mine = gp.at[l, 2 * x + y, pl.ds(c * hP, hP)]
            for peer in _peers7(x, y, c):
                out.append((mine, mine, peer, gp.at[l, 2 * peer[0] + peer[1], pl.ds(peer[2] * hP, hP)]))
        return out

    return plan


def gather_small(c8, wc, token):
    def body(c_ref, wc_ref, token_ref, call, wcall, ssem, rsem, lsem):
        x, y, c = _me()
        myc = 2 * x + y
        me_lin = 4 * x + 2 * y + c
        me = (x, y, c)
        local = [pltpu.make_async_copy(c_ref, call.at[me_lin], lsem.at[0]),
                 pltpu.make_async_copy(wc_ref, wcall.at[myc], lsem.at[1])]
        for cp in local:
            cp.start()
        sends, recvs = [], []
        for m, peer in enumerate(_peers7(x, y, c)):
            plin = 4 * peer[0] + 2 * peer[1] + peer[2]
            sends.append(_rcopy(c_ref, call.at[me_lin], ssem.at[m], rsem.at[m], peer))
            recvs.append(_rcopy(call.at[plin], call.at[plin], ssem.at[m], rsem.at[m], me))
        for j, (px, py) in enumerate([(1 - x, y), (x, 1 - y), (1 - x, 1 - y)]):
            pc = 2 * px + py
            sends.append(_rcopy(wc_ref, wcall.at[myc], ssem.at[7 + j], rsem.at[7 + j], (px, py, c)))
            recvs.append(_rcopy(wcall.at[pc], wcall.at[pc], ssem.at[7 + j], rsem.at[7 + j], me))
        for cp in sends:
            cp.start()
        for cp in recvs:
            cp.wait_recv()
        for cp in sends:
            cp.wait_send()
        for cp in local:
            cp.wait()

    return pl.pallas_call(
        body, name="gather_small",
        in_specs=[ANY] * 3, out_specs=[ANY] * 2,
        out_shape=[jax.ShapeDtypeStruct((N_DEV, SUBLANES, LANES), F32),
                   jax.ShapeDtypeStruct((N_CHIPS, wc.shape[0], 3, LANES), F32)],
        scratch_shapes=[pltpu.SemaphoreType.DMA((10,)), pltpu.SemaphoreType.DMA((10,)), pltpu.SemaphoreType.DMA((2,))],
        compiler_params=_params(n_grid=0),
    )(c8, wc, token)


def spread_now(gi, go, gp, layers, wp_layers):
    plan = plan_spread(layers, wp_layers)
    n = 2 * len(layers) + 7 * len(wp_layers)

    def body(gi_in, go_in, gp_in, gi, go, gp, ssem, rsem):
        copies = plan((gi, go, gp), *_me())
        me = _me()
        sends = [_rcopy(src, dst, ssem.at[k], rsem.at[k], peer) for k, (src, dst, peer, _) in enumerate(copies)]
        for cp in sends:
            cp.start()
        for k, (_, _, _, land) in enumerate(copies):
            _rcopy(land, land, ssem.at[k], rsem.at[k], me).wait_recv()
        for cp in sends:
            cp.wait_send()

    return pl.pallas_call(
        body, name="spread_now",
        in_specs=[ANY] * 3, out_specs=[ANY] * 3,
        out_shape=[jax.ShapeDtypeStruct(a.shape, a.dtype) for a in (gi, go, gp)],
        input_output_aliases={0: 0, 1: 1, 2: 2},
        scratch_shapes=[pltpu.SemaphoreType.DMA((n,)), pltpu.SemaphoreType.DMA((n,))],
        compiler_params=_params(n_grid=0),
    )(gi, go, gp)


def add_sibling(cidx, mine, sib):
    def body(c_ref, *refs):
        for a in range(3):
            m, s, o = refs[a], refs[3 + a], refs[6 + a]
            o[...] = (m[...].astype(F32) + s[...].astype(F32)).astype(BF16)

    def mine_spec(a):
        h = a.shape[1] // 2
        return pl.BlockSpec((None, h, a.shape[2]), lambda j, c_ref: (j, c_ref[0], 0))

    def sib_spec(a):
        return pl.BlockSpec((None,) + a.shape[1:], lambda j, c_ref: (j, 0, 0))

    return pl.pallas_call(
        body, name="add_sibling",
        grid_spec=pltpu.PrefetchScalarGridSpec(
            num_scalar_prefetch=1, grid=(N_CHIPS,),
            in_specs=[mine_spec(a) for a in mine] + [sib_spec(a) for a in sib],
            out_specs=[sib_spec(a) for a in sib]),
        out_shape=[jax.ShapeDtypeStruct(a.shape, BF16) for a in sib],
        compiler_params=_params(VMEM_BIG),
    )(cidx, *mine, *sib)


def sum_chips(pos, own, rb, acc, l, shapes):
    nq = 4
    n_in = 6 + (3 if acc is not None else 0)

    def body(pos_ref, *refs):
        for a in range(3):
            m, b, o = refs[a], refs[3 + a], refs[n_in + a]
            s = m[...].astype(F32)
            for j in range(3):
                s = s + b[j].astype(F32)
            o[...] = s

    def own_spec(a):
        return pl.BlockSpec((None, a.shape[1] // nq, a.shape[2]), lambda q, p: (p[1], q, 0))

    def rb_spec(a):
        return pl.BlockSpec((3, a.shape[1] // nq, a.shape[2]), lambda q, p: (0, q, 0))

    hi, ho, hp = own[0].shape[1] // nq, own[1].shape[1] // nq, own[2].shape[1] // nq
    out_specs = [pl.BlockSpec((None, hi, shapes[0][2]), lambda q, p: (l, p[0] * nq + q, 0)),
                 pl.BlockSpec((None, ho, shapes[1][2]), lambda q, p: (l, p[0] * nq + q, 0)),
                 pl.BlockSpec((None, None, hp, LANES), lambda q, p: (l, p[1], p[0] * nq + q, 0))]
    in_specs = [own_spec(a) for a in own] + [rb_spec(a) for a in rb]
    args = list(own) + list(rb)
    aliases = {}
    if acc is not None:
        in_specs += [ANY] * 3
        args += list(acc)
        aliases = {7: 0, 8: 1, 9: 2}
    return pl.pallas_call(
        body, name="sum_chips",
        grid_spec=pltpu.PrefetchScalarGridSpec(num_scalar_prefetch=1, grid=(nq,), in_specs=in_specs, out_specs=out_specs),
        out_shape=[jax.ShapeDtypeStruct(s, F32) for s in shapes],
        input_output_aliases=aliases,
        compiler_params=_params(VMEM_BIG),
    )(pos, *args)


def pack_small(pos, per_layer, loss_blk):
    L = len(per_layer)
    D = per_layer[0][0].shape[1]

    def body(pos_ref, *refs):
        o = refs[-1]
        lb = refs[-2]
        o[...] = jnp.zeros_like(o)
        for l in range(L):
            dgpre, dgpost, dsh, dsc, dgt, dps, dwc = refs[7 * l:7 * l + 7]
            base = SUBLANES * l
            for r, src in enumerate((dgpre, dgpost, dsh, dsc, dgt)):
                o[pl.ds(base + r, 1), :] = src[...]
            o[pl.ds(base + 5, 1), 0:dps.shape[1]] = dps[...]
            for j in range(dwc.shape[0]):
                for k in range(3):
                    idx = 3 * j + k
                    o[pl.ds(base + 6 + idx // 8, 1), (idx % 8) * LANES:(idx % 8 + 1) * LANES] = dwc[j, pl.ds(k, 1), :]
        o[pl.ds(5, 1), 4 * LANES:5 * LANES] = lb[pl.ds(0, 1), :]

    flat = [a for layer in per_layer for a in layer] + [loss_blk]

    def whole(a):
        return pl.BlockSpec(a.shape, lambda i, p: (0,) * a.ndim)

    return pl.pallas_call(
        body, name="pack_small",
        grid_spec=pltpu.PrefetchScalarGridSpec(
            num_scalar_prefetch=1, grid=(1,), in_specs=[whole(a) for a in flat],
            out_specs=pl.BlockSpec((None, L * SUBLANES, D), lambda i, p: (p[2], 0, 0))),
        out_shape=jax.ShapeDtypeStruct((N_DEV, L * SUBLANES, D), F32),
        compiler_params=_params(),
    )(pos, *flat)


def small_update(pos, packs, params, moments_m, moments_v):
    n = len(params)
    L, D = params[1].shape
    PS = params[3].shape[1]

    def body(pos_ref, p_ref, *refs):
        ws, ms, vs = refs[0:n], refs[n:2 * n], refs[2 * n:3 * n]
        loss_ref = refs[3 * n]
        outs = [refs[3 * n + 1 + 4 * t:3 * n + 5 + 4 * t] for t in range(n)]
        summed = refs[-1]
        s = p_ref[0]
        for d in range(1, N_DEV):
            s = s + p_ref[d]
        summed[...] = s
        loss_ref[...] = summed[pl.ds(5, 1), 4 * LANES:5 * LANES]
        chip = pos_ref[1]

        def update(t, idx, g):
            d, mm, vv = _adamw_math(ws[t][idx], g, ms[t][idx], vs[t][idx])
            g_ref, d_ref, mo_ref, vo_ref = outs[t]
            g_ref[idx] = g
            d_ref[idx] = d
            mo_ref[idx] = mm
            vo_ref[idx] = vv

        for l in range(L):
            base = SUBLANES * l
            row = pl.ds(l, 1)
            for k in range(3):
                update(0, (row, slice(k * D, (k + 1) * D)), summed[pl.ds(base + 2 + k, 1), :])
            update(1, (row, slice(None)), summed[pl.ds(base, 1), :])
            update(2, (row, slice(None)), summed[pl.ds(base + 1, 1), :])
            update(3, (row, slice(None)), summed[pl.ds(base + 5, 1), 0:PS])
            for k in range(3):
                g = None
                for j in range(N_CHIPS):
                    idx = 3 * j + k
                    cand = summed[pl.ds(base + 6 + idx // 8, 1), (idx % 8) * LANES:(idx % 8 + 1) * LANES]
                    g = cand if g is None else jnp.where(chip == j, cand, g)
                update(4, (l, pl.ds(k, 1), slice(None)), g)

    def whole(a):
        return pl.BlockSpec(a.shape, lambda i, p: (0,) * a.ndim)

    ins = [packs] + list(params) + list(moments_m) + list(moments_v)
    out_shape = [jax.ShapeDtypeStruct((1, LANES), F32)]
    for w in params:
        out_shape += [jax.ShapeDtypeStruct(w.shape, F32)] * 4
    outs = pl.pallas_call(
        body, name="small_update",
        grid_spec=pltpu.PrefetchScalarGridSpec(
            num_scalar_prefetch=1, grid=(1,), in_specs=[whole(a) for a in ins],
            out_specs=[whole(a) for a in out_shape],
            scratch_shapes=[pltpu.VMEM(packs.shape[1:], F32)]),
        out_shape=out_shape,
        compiler_params=_params(),
    )(pos, *ins)
    return outs[0], [outs[1 + 4 * t:5 + 4 * t] for t in range(n)]


def _adamw_math(w, g, m, v):
    m = ADAM_B1 * m + (1.0 - ADAM_B1) * g
    v = ADAM_B2 * v + (1.0 - ADAM_B2) * (g * g)
    m_hat = m / (1.0 - ADAM_B1 ** ADAM_STEP)
    v_hat = v / (1.0 - ADAM_B2 ** ADAM_STEP)
    delta = -ADAM_LR * (m_hat / (jnp.sqrt(v_hat) + ADAM_EPS) + ADAM_WD * w)
    return delta, m, v


def adamw(w, g, m, v, block, name, first=0, count=None, acc=None):
    grid = tuple(s // b for s, b in zip(w.shape, block))
    if count is not None:
        grid = (count,) + grid[1:]

    def body(w_ref, g_ref, m_ref, v_ref, *rest):
        go_ref, d_ref, mo_ref, vo_ref = rest[-4:]
        gv = g_ref[...]
        d, mm, vv = _adamw_math(w_ref[...], gv, m_ref[...], v_ref[...])
        go_ref[...] = gv
        d_ref[...] = d
        mo_ref[...] = mm
        vo_ref[...] = vv

    spec = pl.BlockSpec(block, lambda i, *rest: (first + i,) + rest)
    shape = jax.ShapeDtypeStruct(w.shape, F32)
    extra = [] if acc is None else list(acc)
    return pl.pallas_call(
        body, name=name, grid=grid,
        in_specs=[spec] * 4 + [ANY] * len(extra), out_specs=[spec] * 4, out_shape=[shape] * 4,
        input_output_aliases={4 + a: a for a in range(len(extra))},
        compiler_params=_params(VMEM_BIG, n_grid=len(grid)),
    )(w, g, m, v, *extra)


def ada_finish(c_all, dmod, w, m, v):
    L, D, CW = w.shape
    hD = D // 2

    def body(c_ref, d_ref, w_ref, m_ref, v_ref, g_ref, dl_ref, mo_ref, vo_ref):
        cv = c_ref[...]
        z = jnp.zeros_like(cv)
        ca = jnp.concatenate([cv * jax.nn.sigmoid(cv), z], axis=0).astype(BF16)
        dm = jnp.concatenate([d_ref[0], jnp.zeros_like(d_ref[0])], axis=0).astype(BF16)
        g = lax.dot_general(ca, dm, TN, preferred_element_type=F32)
        g_ref[0] = g
        d, mm, vv = _adamw_math(w_ref[0], g, m_ref[0], v_ref[0])
        dl_ref[0] = d
        mo_ref[0] = mm
        vo_ref[0] = vv

    big = pl.BlockSpec((1, hD, CW), lambda l, h: (l, h, 0))
    shape = jax.ShapeDtypeStruct(w.shape, F32)
    return pl.pallas_call(
        body, name="ada_finish", grid=(L, 2),
        in_specs=[pl.BlockSpec((N_DEV, hD), lambda l, h: (0, h)), pl.BlockSpec((1, N_DEV, CW), lambda l, h: (l, 0, 0)),
                  big, big, big],
        out_specs=[big] * 4, out_shape=[shape] * 4,
        compiler_params=_params(VMEM_BIG, n_grid=2),
    )(c_all, dmod, w, m, v)


def kernel(x, c, w_ada, b_ada, g_pre, w_in, w_conv, w_pool, pool_scale, w_out, g_post, loss_target, m_w_ada, m_b_ada, m_g_pre, m_w_in, m_w_conv, m_w_pool, m_pool_scale, m_w_out, m_g_post, v_w_ada, v_b_ada, v_g_pre, v_w_in, v_w_conv, v_w_pool, v_pool_scale, v_w_out, v_g_post):
    L, D, CW = w_in.shape
    RO = w_out.shape[1]
    T = x.shape[1]
    ix, iy, ic = _me()
    chip = 2 * ix + iy
    me_lin = 4 * ix + 2 * iy + ic

    pos = jnp.stack([ic, chip, me_lin]).astype(jnp.int32)
    g_pre3, g_post3 = g_pre.reshape(L, 1, D), g_post.reshape(L, 1, D)
    pscale3 = pool_scale.reshape(L, 1, pool_scale.shape[1])
    n_s, n_c = 3, 9

    def gather(bufs, after):
        ss, rs, bufs, tok = xchg_start("gather_start", bufs, 3 * len(bufs), plan_gather, after=after)
        return (ss, rs, bufs), tok

    def arrive(flight, after):
        ss, rs, bufs = flight
        bufs = xchg_wait("gather_wait", bufs, ss, rs, 3 * len(bufs), plan_gather, after)
        fss, frs, bufs, tok = xchg_start("forward_start", bufs, 3 * len(bufs), plan_forward, sibling_only=True)
        return (fss, frs, bufs), tok

    def ready(flight, after):
        fss, frs, bufs = flight
        return xchg_wait("forward_wait", bufs, fss, frs, 3 * len(bufs), plan_forward, after)

    c_all3, wconv_all = gather_small(c.reshape(SUBLANES, LANES), w_conv, pos)
    c_all = c_all3.reshape(N_DEV, D)
    gi0, go0 = cast_weights(pos, w_in, w_out, 0, c_all3)
    fly_in0, token = gather([gi0], [])
    b_my = lax.dynamic_slice_in_dim(b_ada, chip * CW, CW, axis=1)
    m_ss, m_rs, mods, token = xchg_start("mod_start", [mod_part(pos, c_all, w_ada, b_my, token)], 3, plan_mod)
    fly_out0, token = gather([go0], [token])
    flying_w = [None] * L
    for l in range(1, L):
        flying_w[l], token = gather(list(cast_weights(pos, w_in, w_out, l, token)), [])
    fwd_in0, token = arrive(fly_in0, [token])
    (mod_all,) = xchg_wait("mod_wait", mods, m_ss, m_rs, 3, plan_mod, [token])
    mod = lax.dynamic_index_in_dim(mod_all, me_lin, axis=2, keepdims=False)
    mod4 = jnp.transpose(mod, (1, 0, 2)).reshape(L, 3, 1, D)

    xs, projs, yas, yps, ys = [x.reshape(T, D)], [], [], [], []
    wg_in, wg_out = [], []
    fwd_next = None
    for l in range(L):
        if l == 0:
            (gi,) = ready(fwd_in0, [mod4])
        else:
            gi, go = ready(fwd_next, [xs[l]])
        proj = proj_fwd(xs[l], mod4, g_pre3, gi, l)
        ya = conv_fwd(proj, wconv_all, l)
        yp = pool_fwd(proj, w_pool, pscale3, l)
        after = [ya, yp]
        if l == 0:
            fwd_out0, token = arrive(fly_out0, after)
            after = [token]
        if l + 1 < L:
            fwd_next, token = arrive(flying_w[l + 1], after)
            after = [token]
        if l == 0:
            (go,) = ready(fwd_out0, after)
        wg_in.append(gi)
        wg_out.append(go.reshape(N_CHIPS * RO, D))
        projs.append(proj)
        yas.append(ya)
        yps.append(yp)
        if l + 1 < L:
            xn, yv = out_fwd(ya, yp, wg_out[l], xs[l], mod4, g_post3, l, after[0])
            xs.append(xn)
        else:
            dx, yv, loss_blk = out_fwd_loss(ya, yp, wg_out[l], xs[l], mod4, g_post3, l, loss_target.reshape(T, D))
        ys.append(yv)

    shapes = (w_in.shape, w_out.shape, w_pool.shape)
    smalls = [None] * L
    acc, flying, sib, token = None, None, None, loss_blk

    def to_chips(sib, after):
        sl, s_ss, s_rs, s_bufs = sib
        s_bufs = xchg_wait("sibling_wait", s_bufs, s_ss, s_rs, n_s, plan_sibling, after)
        chip_parts = add_sibling(pos, s_bufs[0:3], s_bufs[3:6])
        lands = [lax.empty((3,) + a.shape[1:], a.dtype) for a in chip_parts]
        c_ss, c_rs, c_bufs, ctoken = xchg_start("chip_start", list(chip_parts) + lands, n_c, plan_chip)
        return (sl, c_ss, c_rs, c_bufs), ctoken

    def landed(flying, acc, after):
        fl, f_ss, f_rs, f_bufs = flying
        f_bufs = xchg_wait("chip_wait", f_bufs, f_ss, f_rs, n_c, plan_chip, after)
        return sum_chips(pos, f_bufs[0:3], f_bufs[3:6], acc, fl, shapes)

    for l in reversed(range(L)):
        dya, dyp, dwo_l, dgate, dgpost = out_bwd(dx, ys[l], yas[l], yps[l], wg_out[l], mod4, g_post3, l, token)
        token = dya
        if sib is not None:
            arrived = flying
            flying, token = to_chips(sib, [dya])
            if arrived is not None:
                acc = landed(arrived, acc, [token])
                token = acc[0]
        du_p, dg_p, dwp_l, dps = pool_bwd(projs[l], dyp, w_pool, pscale3, l, token)
        du_a, db_a, dc_a, dg_a, dwc = conv_bwd(projs[l], dya, wconv_all, l)
        dx, dwi_l, dshift, dscale, dgpre = in_bwd([du_a, db_a, dc_a, dg_a, du_p, dg_p], wg_in[l], xs[l], dx,
                                                  mod4, g_pre3, l)
        smalls[l] = (dgpre, dgpost, dshift, dscale, dgate, dps, dwc)
        parts = [dwi_l, dwo_l.reshape(N_CHIPS, RO, D), dwp_l]
        s_lands = [lax.empty((a.shape[0], a.shape[1] // 2) + a.shape[2:], a.dtype) for a in parts]
        s_ss, s_rs, s_bufs, token = xchg_start("sibling_start", parts + s_lands, n_s, plan_sibling, sibling_only=True)
        sib = (l, s_ss, s_rs, s_bufs)
    grad_x = dx.reshape(1, T, D)

    p_ss, p_rs, packs, ptoken = xchg_start("pack_start", [pack_small(pos, smalls, loss_blk)], N_DEV - 1, plan_pack)
    acc = landed(flying, acc, [ptoken, token])
    n_sp = (2 + N_DEV - 1) * (L - 1)
    spread = plan_spread(tuple(range(1, L)), tuple(range(1, L)))
    sp_ss, sp_rs, acc, sp_token = xchg_start("spread_start", list(acc), n_sp, spread)
    flying, token = to_chips(sib, [sp_token])
    (packs_all,) = xchg_wait("pack_wait", packs, p_ss, p_rs, N_DEV - 1, plan_pack, [token])
    dmod_all = packs_all.reshape(N_DEV, L, SUBLANES, D)[:, :, 2:5].reshape(N_DEV, L, 3 * D)
    dmod_my = jnp.transpose(lax.dynamic_slice_in_dim(dmod_all, chip * CW, CW, axis=2), (1, 0, 2))

    g_w_ada, d_w_ada, nm_w_ada, nv_w_ada = ada_finish(c_all, dmod_my, w_ada, m_w_ada, v_w_ada)
    loss_row, upd = small_update(pos, packs_all, [b_ada, g_pre, g_post, pool_scale, w_conv],
                                 [m_b_ada, m_g_pre, m_g_post, m_pool_scale, m_w_conv],
                                 [v_b_ada, v_g_pre, v_g_post, v_pool_scale, v_w_conv])
    loss = loss_row[0, 0]
    (g_b_ada, d_b_ada, nm_b_ada, nv_b_ada), (g_g_pre, d_g_pre, nm_g_pre, nv_g_pre) = upd[0], upd[1]
    (g_g_post, d_g_post, nm_g_post, nv_g_post), (g_pscale, d_pscale, nm_pscale, nv_pscale) = upd[2], upd[3]
    g_w_conv, d_w_conv, nm_w_conv, nv_w_conv = upd[4]

    done = [nv_w_ada, nv_w_conv]
    g_w_in, g_w_out, g_w_pool = xchg_wait("spread_wait", acc, sp_ss, sp_rs, n_sp, spread, done)
    in_blk, out_blk = (1, D // 2, CW), (1, RO, D)
    upd_in = adamw(w_in, g_w_in, m_w_in, v_w_in, in_blk, "adamw_w_in", 1, L - 1)
    upd_out = adamw(w_out, g_w_out, m_w_out, v_w_out, out_blk, "adamw_w_out", 1, L - 1)

    acc = landed(flying, (g_w_in, g_w_out, g_w_pool), [upd_in[3], upd_out[3]])
    r_w_in, r_w_out, r_w_pool = spread_now(*acc, (0,), (0,))
    g_w_in, d_w_in, nm_w_in, nv_w_in = adamw(w_in, r_w_in, m_w_in, v_w_in, in_blk, "adamw_w_in", 0, 1, upd_in)
    g_w_out, d_w_out, nm_w_out, nv_w_out = adamw(w_out, r_w_out, m_w_out, v_w_out, out_blk, "adamw_w_out", 0, 1, upd_out)
    pshape = (L, N_CHIPS * LANES, LANES)
    upd_pool = adamw(w_pool.reshape(pshape), r_w_pool.reshape(pshape), m_w_pool.reshape(pshape),
                     v_w_pool.reshape(pshape), (1,) + pshape[1:], "adamw_w_pool")
    g_w_pool, d_w_pool, nm_w_pool, nv_w_pool = [a.reshape(w_pool.shape) for a in upd_pool]

    return (loss, grad_x,
            g_w_ada, g_b_ada, g_g_pre, g_w_in, g_w_conv, g_w_pool, g_pscale, g_w_out, g_g_post,
            d_w_ada, d_b_ada, d_g_pre, d_w_in, d_w_conv, d_w_pool, d_pscale, d_w_out, d_g_post,
            nm_w_ada, nm_b_ada, nm_g_pre, nm_w_in, nm_w_conv, nm_w_pool, nm_pscale, nm_w_out, nm_g_post,
            nv_w_ada, nv_b_ada, nv_g_pre, nv_w_in, nv_w_conv, nv_w_pool, nv_pscale, nv_w_out, nv_g_post)
```

```python
import functools

import jax
import jax.numpy as jnp
from jax import lax
from jax.experimental import pallas as pl
from jax.experimental.pallas import tpu as pltpu

F32 = jnp.float32
BF16 = jnp.bfloat16
MESH = pl.DeviceIdType.MESH
ANY = pl.BlockSpec(memory_space=pl.ANY)

NORM_EPS = 1e-6
POOL_WINDOWS = (2, 4, 8, 16)
ADAM_LR = 0.001
ADAM_B1 = 0.9
ADAM_B2 = 0.999
ADAM_EPS = 1e-08
ADAM_WD = 0.01
ADAM_STEP = 10

N_CHIPS = 4
N_DEV = 8
LANES = 128
SUBLANES = 8
VMEM_BIG = 56 * 1024 * 1024
HIST = 16
R_CONV = 64
R_POOL = 128

NT = (((1,), (1,)), ((), ()))
TN = (((0,), (0,)), ((), ()))


def _params(vmem=None, n_grid=1):
    kw = {}
    if n_grid:
        kw["dimension_semantics"] = ("arbitrary",) * n_grid
    if vmem is not None:
        kw["vmem_limit_bytes"] = vmem
    return pltpu.CompilerParams(**kw)


def _colsum8(v):
    n, d = v.shape
    return v.reshape(n // SUBLANES, SUBLANES, d).sum(axis=0)


def _rms(v):
    return lax.rsqrt(jnp.mean(v * v, axis=-1, keepdims=True) + NORM_EPS)


def _sigmoid(v):
    return 0.5 * jnp.tanh(0.5 * v) + 0.5


def _shift_down(ext, k, rows):
    if k == 0:
        return ext[HIST:HIST + rows]
    return pltpu.roll(ext, k, 0)[HIST:HIST + rows]


def _shift_up(ext, k, rows):
    if k == 0:
        return ext[0:rows]
    return pltpu.roll(ext, ext.shape[0] - k, 0)[0:rows]


def _load_ext(ref, r0, h0, first, rows):
    hist = ref[pl.ds(h0, HIST), :].astype(F32)
    hist = jnp.where(first, 0.0, hist)
    cur = ref[pl.ds(r0, rows), :].astype(F32)
    return jnp.concatenate([hist, cur], axis=0)


def _me():
    return lax.axis_index("x"), lax.axis_index("y"), lax.axis_index("c")


def cast_weights(pos, w_in, w_out, l, after):
    _, D, CW = w_in.shape
    RO = w_out.shape[1]

    def body(pos_ref, wi, wo, after_ref, oi, oo):
        oi[...] = wi[...].astype(BF16)
        oo[...] = wo[...].astype(BF16)

    return pl.pallas_call(
        body, name="cast_w",
        grid_spec=pltpu.PrefetchScalarGridSpec(
            num_scalar_prefetch=1, grid=(2,),
            in_specs=[pl.BlockSpec((None, D // 2, CW), lambda h, p: (l, h, 0)),
                      pl.BlockSpec((None, RO // 2, D), lambda h, p: (l, h, 0)), ANY],
            out_specs=[pl.BlockSpec((D // 2, CW), lambda h, p: (h, p[1])),
                       pl.BlockSpec((None, RO // 2, D), lambda h, p: (p[1], h, 0))]),
        out_shape=[jax.ShapeDtypeStruct((D, N_CHIPS * CW), BF16), jax.ShapeDtypeStruct((N_CHIPS, RO, D), BF16)],
        compiler_params=_params(),
    )(pos, w_in, w_out, after)


def mod_part(pos, c_all, w_ada, b_my, after):
    L, D, CW = w_ada.shape

    def body(pos_ref, c_ref, w_ref, b_ref, after_ref, o_ref):
        cv = c_ref[...]
        ca = (cv * jax.nn.sigmoid(cv)).astype(BF16)
        o_ref[...] = jnp.dot(ca, w_ref[0].astype(BF16), preferred_element_type=F32) + b_ref[0]

    return pl.pallas_call(
        body, name="mod_part",
        grid_spec=pltpu.PrefetchScalarGridSpec(
            num_scalar_prefetch=1, grid=(L,),
            in_specs=[pl.BlockSpec((N_DEV, D), lambda l, p: (0, 0)),
                      pl.BlockSpec((1, D, CW), lambda l, p: (l, 0, 0)),
                      pl.BlockSpec((1, 1, CW), lambda l, p: (l, 0, 0)), ANY],
            out_specs=pl.BlockSpec((None, None, N_DEV, CW), lambda l, p: (p[1], l, 0, 0))),
        out_shape=jax.ShapeDtypeStruct((N_CHIPS, L, N_DEV, CW), F32),
        compiler_params=_params(VMEM_BIG),
    )(pos, c_all, w_ada, b_my.reshape(L, 1, CW), after)


def _mod_row(l, k, D):
    return pl.BlockSpec((None, None, 1, D), lambda *_: (l, k, 0, 0))


def _layer_row(l, D):
    return pl.BlockSpec((None, 1, D), lambda *_: (l, 0, 0))


def proj_fwd(x, mod4, g_pre3, wg, l):
    T, D = x.shape
    NC = wg.shape[1]
    NB = N_CHIPS
    CW = NC // NB
    tm = 512

    def body(x_ref, sh_ref, sc_ref, g_ref, w_ref, o_ref):
        xv = x_ref[...]
        h = (xv * _rms(xv) * g_ref[...]) * (1.0 + sc_ref[...]) + sh_ref[...]
        hb = h.astype(BF16)
        for j in range(NB):
            cols = slice(j * CW, (j + 1) * CW)
            o_ref[:, cols] = jnp.dot(hb, w_ref[:, cols], preferred_element_type=F32).astype(BF16)

    return pl.pallas_call(
        body, name="proj_fwd", grid=(T // tm,),
        in_specs=[pl.BlockSpec((tm, D), lambda i: (i, 0)), _mod_row(l, 0, D), _mod_row(l, 1, D), _layer_row(l, D),
                  pl.BlockSpec((D, NC), lambda i: (0, 0))],
        out_specs=pl.BlockSpec((tm, NC), lambda i: (i, 0)),
        out_shape=jax.ShapeDtypeStruct((T, NC), BF16),
        compiler_params=_params(VMEM_BIG),
    )(x, mod4, mod4, g_pre3, wg)


def conv_fwd(proj, wconv, l):
    T = proj.shape[0]
    R = R_CONV
    nblk = 4

    def body(u_ref, b_ref, c_ref, g_ref, w_ref, o_ref):
        w0 = w_ref[pl.ds(0, 1), :]
        w1 = w_ref[pl.ds(1, 1), :]
        w2 = w_ref[pl.ds(2, 1), :]

        def chunk(i, carry):
            r0 = pl.multiple_of(i * R, R)
            h0 = pl.multiple_of(jnp.maximum(r0 - HIST, 0), HIST)
            first = i == 0
            ca = _load_ext(c_ref, r0, h0, first, R) * _load_ext(u_ref, r0, h0, first, R)
            conv = w2 * ca[HIST:] + w1 * _shift_down(ca, 1, R) + w0 * _shift_down(ca, 2, R)
            g = g_ref[pl.ds(r0, R), :].astype(F32)
            b = b_ref[pl.ds(r0, R), :].astype(F32)
            o_ref[pl.ds(r0, R), :] = (b * conv * (g * _sigmoid(g))).astype(BF16)
            return carry

        lax.fori_loop(0, T // R, chunk, 0)

    def col(off):
        return pl.BlockSpec((T, LANES), lambda j: (0, j + off))

    return pl.pallas_call(
        body, name="conv_fwd", grid=(nblk,),
        in_specs=[col(0), col(4), col(8), col(12), pl.BlockSpec((None, None, 3, LANES), lambda j: (j, l, 0, 0))],
        out_specs=pl.BlockSpec((T, LANES), lambda j: (0, j)),
        out_shape=jax.ShapeDtypeStruct((T, 2 * nblk * LANES), BF16),
        compiler_params=_params(),
    )(proj, proj, proj, proj, wconv)


def _causal_window_sum(ext, w):
    s, k = ext, 1
    while k < w:
        s = s + pltpu.roll(s, k, 0)
        k *= 2
    return s


def _anticausal_window_sum(ext, w):
    s, k = ext, 1
    n = ext.shape[0]
    while k < w:
        s = s + pltpu.roll(s, n - k, 0)
        k *= 2
    return s


def _count(r0, rows, w):
    t = r0 + lax.broadcasted_iota(jnp.int32, (rows, LANES), 0)
    return jnp.minimum(t + 1, w).astype(F32)


def _pooled_loop(p_ref, pooled_s, w, T):
    R = R_POOL

    def chunk(i, carry):
        r0 = pl.multiple_of(i * R, R)
        h0 = pl.multiple_of(jnp.maximum(r0 - HIST, 0), HIST)
        ext = _load_ext(p_ref, r0, h0, i == 0, R)
        ws = _causal_window_sum(ext, w)[HIST:]
        pooled_s[pl.ds(r0, R), :] = (ws / _count(r0, R, w) - ext[HIST:]).astype(BF16)
        return carry

    lax.fori_loop(0, T // R, chunk, 0)


def _pool_w_spec(l):
    return pl.BlockSpec((None, None, LANES, LANES), lambda j: (l, j, 0, 0))


def _pool_s_spec(l):
    return pl.BlockSpec((None, 1, LANES), lambda j: (l, 0, j))


def pool_fwd(proj, wpool, pscale3, ycat, l):
    T = proj.shape[0]
    R = R_POOL
    ngrp = len(POOL_WINDOWS)

    def body(p_ref, g_ref, w_ref, s_ref, ycat_ref, o_ref, pooled_s, mixed_s):
        grp = pl.program_id(0)

        def group(w):
            _pooled_loop(p_ref, pooled_s, w, T)
            mixed_s[...] = jnp.dot(pooled_s[...], w_ref[...].astype(BF16), preferred_element_type=F32)
            sc = s_ref[...]

            def chunk(i, carry):
                r0 = pl.multiple_of(i * R, R)
                g = g_ref[pl.ds(r0, R), :].astype(F32)
                o_ref[pl.ds(r0, R), :] = (mixed_s[pl.ds(r0, R), :] * sc * (g * _sigmoid(g))).astype(BF16)
                return carry

            lax.fori_loop(0, T // R, chunk, 0)

        for k, w in enumerate(POOL_WINDOWS):
            pl.when(grp == k)(functools.partial(group, w))

    return pl.pallas_call(
        body, name="pool_fwd", grid=(ngrp,),
        in_specs=[pl.BlockSpec((T, LANES), lambda j: (0, j + 16)), pl.BlockSpec((T, LANES), lambda j: (0, j + 20)),
                  _pool_w_spec(l), _pool_s_spec(l), ANY],
        out_specs=pl.BlockSpec((T, LANES), lambda j: (0, j + ngrp)),
        out_shape=jax.ShapeDtypeStruct(ycat.shape, BF16),
        input_output_aliases={4: 0},
        scratch_shapes=[pltpu.VMEM((T, LANES), BF16), pltpu.VMEM((T, LANES), F32)],
        compiler_params=_params(),
    )(proj, proj, wpool, pscale3, ycat)


def out_fwd(ycat, wo, x, mod4, g_post3, l, after):
    T, D = x.shape
    K = ycat.shape[1]
    tm = 512

    def body(yc_ref, wo_ref, x_ref, gt_ref, g_ref, after_ref, xn_ref, y_ref):
        y = jnp.dot(yc_ref[...], wo_ref[...], preferred_element_type=F32)
        xn_ref[...] = x_ref[...] + gt_ref[...] * (y * _rms(y) * g_ref[...])
        y_ref[...] = y.astype(BF16)

    tile = pl.BlockSpec((tm, D), lambda i: (i, 0))
    return pl.pallas_call(
        body, name="out_fwd", grid=(T // tm,),
        in_specs=[pl.BlockSpec((tm, K), lambda i: (i, 0)), pl.BlockSpec((K, D), lambda i: (0, 0)), tile,
                  _mod_row(l, 2, D), _layer_row(l, D), ANY],
        out_specs=[tile, tile],
        out_shape=[jax.ShapeDtypeStruct((T, D), F32), jax.ShapeDtypeStruct((T, D), BF16)],
        compiler_params=_params(VMEM_BIG),
    )(ycat, wo, x, mod4, g_post3, after)


def out_fwd_loss(ycat, wo, x, mod4, g_post3, l, target):
    T, D = x.shape
    K = ycat.shape[1]
    tm = 512
    nt = T // tm

    def body(yc_ref, wo_ref, x_ref, gt_ref, g_ref, t_ref, dx_ref, y_ref, l_ref, acc):
        i = pl.program_id(0)

        @pl.when(i == 0)
        def _():
            acc[...] = jnp.zeros_like(acc)

        y = jnp.dot(yc_ref[...], wo_ref[...], preferred_element_type=F32)
        y_ref[...] = y.astype(BF16)
        d = (x_ref[...] + gt_ref[...] * (y * _rms(y) * g_ref[...])) - t_ref[...]
        dx_ref[...] = d * (1.0 / D)
        acc[...] += _colsum8(d * d)

        @pl.when(i == nt - 1)
        def _():
            l_ref[...] = jnp.zeros_like(l_ref) + jnp.sum(acc[...]) * (0.5 / D)

    tile = pl.BlockSpec((tm, D), lambda i: (i, 0))
    return pl.pallas_call(
        body, name="out_fwd_loss", grid=(nt,),
        in_specs=[pl.BlockSpec((tm, K), lambda i: (i, 0)), pl.BlockSpec((K, D), lambda i: (0, 0)), tile,
                  _mod_row(l, 2, D), _layer_row(l, D), tile],
        out_specs=[tile, tile, pl.BlockSpec((SUBLANES, LANES), lambda i: (0, 0))],
        out_shape=[jax.ShapeDtypeStruct((T, D), F32), jax.ShapeDtypeStruct((T, D), BF16),
                   jax.ShapeDtypeStruct((SUBLANES, LANES), F32)],
        scratch_shapes=[pltpu.VMEM((SUBLANES, D), F32)],
        compiler_params=_params(VMEM_BIG),
    )(ycat, wo, x, mod4, g_post3, target)


def out_bwd(dx, y, ycat, wo, mod4, g_post3, l, after):
    T, D = dx.shape
    K = ycat.shape[1]
    tm = 512
    nt = T // tm

    def body(dx_ref, y_ref, yc_ref, wo_ref, gt_ref, g_ref, after_ref,
             dyc_ref, dwo_ref, dgt_ref, dg_ref, acc_w, acc_p):
        i = pl.program_id(0)

        @pl.when(i == 0)
        def _():
            acc_w[...] = jnp.zeros_like(acc_w)
            acc_p[...] = jnp.zeros_like(acc_p)

        yv = y_ref[...].astype(F32)
        dxv = dx_ref[...]
        gg = gt_ref[...] * g_ref[...]
        r = _rms(yv)
        yn = yv * r
        p = dxv * yn
        acc_p[...] += _colsum8(p)
        dy = r * (dxv * gg - yn * jnp.mean(p * gg, axis=-1, keepdims=True))
        dyb = dy.astype(BF16)
        dyc_ref[...] = lax.dot_general(dyb, wo_ref[...], NT, preferred_element_type=F32).astype(BF16)
        acc_w[...] += lax.dot_general(yc_ref[...], dyb, TN, preferred_element_type=F32)

        @pl.when(i == nt - 1)
        def _():
            dwo_ref[...] = acc_w[...].astype(BF16)
            sp = jnp.sum(acc_p[...], axis=0, keepdims=True)
            dgt_ref[...] = g_ref[...] * sp
            dg_ref[...] = gt_ref[...] * sp

    row = pl.BlockSpec((1, D), lambda i: (0, 0))
    tile = pl.BlockSpec((tm, D), lambda i: (i, 0))
    ktile = pl.BlockSpec((tm, K), lambda i: (i, 0))
    full = pl.BlockSpec((K, D), lambda i: (0, 0))
    return pl.pallas_call(
        body, name="out_bwd", grid=(nt,),
        in_specs=[tile, tile, ktile, full, _mod_row(l, 2, D), _layer_row(l, D), ANY],
        out_specs=[ktile, full, row, row],
        out_shape=[jax.ShapeDtypeStruct((T, K), BF16), jax.ShapeDtypeStruct((K, D), BF16),
                   jax.ShapeDtypeStruct((1, D), F32), jax.ShapeDtypeStruct((1, D), F32)],
        scratch_shapes=[pltpu.VMEM((K, D), F32), pltpu.VMEM((SUBLANES, D), F32)],
        compiler_params=_params(VMEM_BIG),
    )(dx, y, ycat, wo, mod4, g_post3, after)


def conv_bwd(proj, dycat, wconv, l):
    T = proj.shape[0]
    R = R_CONV
    nblk = 4
    nchunk = T // R

    def body(u_ref, b_ref, c_ref, g_ref, dy_ref, w_ref, du_ref, db_ref, dc_ref, dg_ref, dw_ref):
        w0 = w_ref[pl.ds(0, 1), :]
        w1 = w_ref[pl.ds(1, 1), :]
        w2 = w_ref[pl.ds(2, 1), :]

        def chunk(k, carry):
            head, a0, a1, a2 = carry
            i = nchunk - 1 - k
            r0 = pl.multiple_of(i * R, R)
            h0 = pl.multiple_of(jnp.maximum(r0 - HIST, 0), HIST)
            first = i == 0
            ue = _load_ext(u_ref, r0, h0, first, R)
            ce = _load_ext(c_ref, r0, h0, first, R)
            ca = ce * ue
            ca0 = ca[HIST:]
            ca1 = _shift_down(ca, 1, R)
            ca2 = _shift_down(ca, 2, R)
            conv = w2 * ca0 + w1 * ca1 + w0 * ca2
            g = g_ref[pl.ds(r0, R), :].astype(F32)
            b = b_ref[pl.ds(r0, R), :].astype(F32)
            dy = dy_ref[pl.ds(r0, R), :].astype(F32)
            sg = _sigmoid(g)
            sl = g * sg
            t = dy * conv
            db_ref[pl.ds(r0, R), :] = (t * sl).astype(BF16)
            dg_ref[pl.ds(r0, R), :] = (t * b * (sg * (1.0 + g * (1.0 - sg)))).astype(BF16)
            dconv = dy * b * sl
            a2 = a2 + _colsum8(dconv * ca0)
            a1 = a1 + _colsum8(dconv * ca1)
            a0 = a0 + _colsum8(dconv * ca2)
            e = jnp.concatenate([dconv, head], axis=0)
            dca = w2 * dconv + w1 * _shift_up(e, 1, R) + w0 * _shift_up(e, 2, R)
            du_ref[pl.ds(r0, R), :] = (dca * ce[HIST:]).astype(BF16)
            dc_ref[pl.ds(r0, R), :] = (dca * ue[HIST:]).astype(BF16)
            return dconv[0:SUBLANES], a0, a1, a2

        z = jnp.zeros((SUBLANES, LANES), F32)
        _, a0, a1, a2 = lax.fori_loop(0, nchunk, chunk, (z, z, z, z))
        dw_ref[pl.ds(0, 1), :] = jnp.sum(a0, axis=0, keepdims=True)
        dw_ref[pl.ds(1, 1), :] = jnp.sum(a1, axis=0, keepdims=True)
        dw_ref[pl.ds(2, 1), :] = jnp.sum(a2, axis=0, keepdims=True)

    def col(off):
        return pl.BlockSpec((T, LANES), lambda j: (0, j + off))

    sec = jax.ShapeDtypeStruct((T, nblk * LANES), BF16)
    return pl.pallas_call(
        body, name="conv_bwd", grid=(nblk,),
        in_specs=[col(0), col(4), col(8), col(12), col(0), pl.BlockSpec((None, None, 3, LANES), lambda j: (j, l, 0, 0))],
        out_specs=[col(0), col(0), col(0), col(0), pl.BlockSpec((None, 3, LANES), lambda j: (j, 0, 0))],
        out_shape=[sec, sec, sec, sec, jax.ShapeDtypeStruct((nblk, 3, LANES), F32)],
        compiler_params=_params(),
    )(proj, proj, proj, proj, dycat, wconv)


def pool_bwd(proj, dycat, wpool, pscale3, l, after):
    T = proj.shape[0]
    R = R_POOL
    ngrp = len(POOL_WINDOWS)
    nchunk = T // R

    def body(p_ref, g_ref, dy_ref, w_ref, s_ref, after_ref, du_ref, dg_ref, dw_ref, ds_ref,
             pooled_s, mixed_s, dmix_s, dpool_s):
        grp = pl.program_id(0)

        def group(w):
            wb = w_ref[...].astype(BF16)
            _pooled_loop(p_ref, pooled_s, w, T)
            mixed_s[...] = jnp.dot(pooled_s[...], wb, preferred_element_type=F32)
            sc = s_ref[...]

            def gate_chunk(i, acc):
                r0 = pl.multiple_of(i * R, R)
                g = g_ref[pl.ds(r0, R), :].astype(F32)
                dy = dy_ref[pl.ds(r0, R), :].astype(F32)
                mixed = mixed_s[pl.ds(r0, R), :]
                sg = _sigmoid(g)
                dg_ref[pl.ds(r0, R), :] = (dy * mixed * sc * (sg * (1.0 + g * (1.0 - sg)))).astype(BF16)
                dms = dy * (g * sg)
                dmix_s[pl.ds(r0, R), :] = (dms * sc).astype(BF16)
                return acc + _colsum8(dms * mixed)

            acc = lax.fori_loop(0, nchunk, gate_chunk, jnp.zeros((SUBLANES, LANES), F32))
            ds_ref[...] = jnp.sum(acc, axis=0, keepdims=True)
            dpool_s[pl.ds(0, T), :] = lax.dot_general(dmix_s[...], wb, NT, preferred_element_type=F32)
            dpool_s[pl.ds(T, HIST), :] = jnp.zeros((HIST, LANES), F32)
            dw_ref[...] = lax.dot_general(pooled_s[...], dmix_s[...], TN, preferred_element_type=F32).astype(BF16)

            def back_chunk(i, carry):
                r0 = pl.multiple_of(i * R, R)
                dpe = dpool_s[pl.ds(r0, R + HIST), :]
                e = dpe / _count(r0, R + HIST, w)
                du_ref[pl.ds(r0, R), :] = (_anticausal_window_sum(e, w)[0:R] - dpe[0:R]).astype(BF16)
                return carry

            lax.fori_loop(0, nchunk, back_chunk, 0)

        for k, w in enumerate(POOL_WINDOWS):
            pl.when(grp == k)(functools.partial(group, w))

    def col(off):
        return pl.BlockSpec((T, LANES), lambda j: (0, j + off))

    sec = jax.ShapeDtypeStruct((T, ngrp * LANES), BF16)
    wspec = pl.BlockSpec((None, LANES, LANES), lambda j: (j, 0, 0))
    sspec = pl.BlockSpec((1, LANES), lambda j: (0, j))
    return pl.pallas_call(
        body, name="pool_bwd", grid=(ngrp,),
        in_specs=[col(16), col(20), col(ngrp), _pool_w_spec(l), _pool_s_spec(l), ANY],
        out_specs=[col(0), col(0), wspec, sspec],
        out_shape=[sec, sec, jax.ShapeDtypeStruct((ngrp, LANES, LANES), BF16),
                   jax.ShapeDtypeStruct((1, ngrp * LANES), F32)],
        scratch_shapes=[pltpu.VMEM((T, LANES), BF16), pltpu.VMEM((T, LANES), F32),
                        pltpu.VMEM((T, LANES), BF16), pltpu.VMEM((T + HIST, LANES), F32)],
        compiler_params=_params(),
    )(proj, proj, dycat, wpool, pscale3, after)


def in_bwd(dsecs, wg, x, dxo, mod4, g_pre3, l):
    T, D = x.shape
    NB = N_CHIPS
    CW = wg.shape[1] // NB
    SW = dsecs[0].shape[1]
    nsec = len(dsecs)
    assert nsec * SW == NB * CW
    tm = 256
    nt = T // tm

    def body(*refs):
        d_refs = refs[0:nsec]
        w_ref, x_ref, dxo_ref, sh_ref, sc_ref, g_ref = refs[nsec:nsec + 6]
        dxi_ref, dw_ref, dsh_ref, dsc_ref, dg_ref = refs[nsec + 6:nsec + 11]
        dp_s, acc_w, acc_sh, acc_q = refs[nsec + 11:]
        i = pl.program_id(0)

        @pl.when(i == 0)
        def _():
            acc_w[...] = jnp.zeros_like(acc_w)
            acc_sh[...] = jnp.zeros_like(acc_sh)
            acc_q[...] = jnp.zeros_like(acc_q)

        xv = x_ref[...]
        r = _rms(xv)
        xh = xv * r
        sg = g_ref[...] * (1.0 + sc_ref[...])
        hb = (xh * sg + sh_ref[...]).astype(BF16)
        for s in range(nsec):
            dp_s[:, s * SW:(s + 1) * SW] = d_refs[s][...]
        dh = lax.dot_general(dp_s[...], w_ref[...], NT, preferred_element_type=F32)
        for j in range(NB):
            acc_w[j] += lax.dot_general(hb, dp_s[:, j * CW:(j + 1) * CW], TN, preferred_element_type=F32)
        q = dh * xh
        acc_sh[...] += _colsum8(dh)
        acc_q[...] += _colsum8(q)
        dxi_ref[...] = dxo_ref[...] + r * (dh * sg - xh * jnp.mean(q * sg, axis=-1, keepdims=True))

        @pl.when(i == nt - 1)
        def _():
            dw_ref[...] = acc_w[...].astype(BF16)
            sq = jnp.sum(acc_q[...], axis=0, keepdims=True)
            dsh_ref[...] = jnp.sum(acc_sh[...], axis=0, keepdims=True)
            dsc_ref[...] = g_ref[...] * sq
            dg_ref[...] = (1.0 + sc_ref[...]) * sq

    row = pl.BlockSpec((1, D), lambda i: (0, 0))
    tile = pl.BlockSpec((tm, D), lambda i: (i, 0))
    sect = pl.BlockSpec((tm, SW), lambda i: (i, 0))
    rowshape = jax.ShapeDtypeStruct((1, D), F32)
    return pl.pallas_call(
        body, name="in_bwd", grid=(nt,),
        in_specs=[sect] * nsec + [pl.BlockSpec((D, NB * CW), lambda i: (0, 0)), tile, tile,
                                  _mod_row(l, 0, D), _mod_row(l, 1, D), _layer_row(l, D)],
        out_specs=[tile, pl.BlockSpec((NB, D, CW), lambda i: (0, 0, 0)), row, row, row],
        out_shape=[jax.ShapeDtypeStruct((T, D), F32), jax.ShapeDtypeStruct((NB, D, CW), BF16),
                   rowshape, rowshape, rowshape],
        scratch_shapes=[pltpu.VMEM((tm, nsec * SW), BF16), pltpu.VMEM((NB, D, CW), F32),
                        pltpu.VMEM((SUBLANES, D), F32), pltpu.VMEM((SUBLANES, D), F32)],
        compiler_params=_params(VMEM_BIG),
    )(*dsecs, wg, x, dxo, mod4, mod4, g_pre3)


def _rcopy(src, dst, ssem, rsem, dev):
    return pltpu.make_async_remote_copy(src_ref=src, dst_ref=dst, send_sem=ssem, recv_sem=rsem,
                                        device_id=dev, device_id_type=MESH)


def _peers7(x, y, c):
    out = []
    for m in range(1, N_DEV):
        bx, by, bc = (m >> 2) & 1, (m >> 1) & 1, m & 1
        out.append(((1 - x) if bx else x, (1 - y) if by else y, (1 - c) if bc else c))
    return out


HBM = pl.BlockSpec(memory_space=pltpu.HBM)
SEM = pl.BlockSpec(memory_space=pltpu.SEMAPHORE)
SPLIT = pltpu.CompilerParams(has_side_effects=pltpu.SideEffectType.DATAFLOW_SIDE_EFFECTING)


def _hbm(a):
    return pltpu.with_memory_space_constraint(a, pltpu.HBM)


def _chips(x, y):
    return [(1 - x, y), (x, 1 - y), (1 - x, 1 - y)]


SIBLING_BARRIER_ID = 0


def xchg_start(name, bufs, n_copies, plan, sibling_only=False, after=()):
    n = len(bufs)
    after = list(after)

    def body(*refs):
        ssem, rsem, token = refs[n + len(after)], refs[n + len(after) + 1], refs[-1]
        x, y, c = _me()
        if sibling_only:
            barrier = pltpu.get_barrier_semaphore()
            pl.semaphore_signal(barrier, inc=1, device_id=(x, y, 1 - c), device_id_type=MESH)
            pl.semaphore_wait(barrier, 1)
        copies = plan(refs[0:n], x, y, c)
        assert len(copies) == n_copies
        for k, (src, dst, peer, _) in enumerate(copies):
            _rcopy(src, dst, ssem.at[k], rsem.at[k], peer).start()
        token[...] = jnp.zeros_like(token)

    params = dict(has_side_effects=pltpu.SideEffectType.DATAFLOW_SIDE_EFFECTING)
    if sibling_only:
        params["collective_id"] = SIBLING_BARRIER_ID
    outs = pl.pallas_call(
        body, name=name,
        in_specs=[HBM] * n + [ANY] * len(after),
        out_specs=[SEM, SEM] + [HBM] * n + [pl.BlockSpec(memory_space=pltpu.VMEM)],
        out_shape=([pltpu.SemaphoreType.DMA((n_copies,))] * 2 + [pltpu.HBM(b.shape, b.dtype) for b in bufs]
                   + [jax.ShapeDtypeStruct((SUBLANES, LANES), F32)]),
        input_output_aliases={a: 2 + a for a in range(n)},
        compiler_params=pltpu.CompilerParams(**params),
    )(*[_hbm(b) for b in bufs], *after)
    return outs[0], outs[1], list(outs[2:2 + n]), outs[-1]


def xchg_wait(name, bufs, ssem, rsem, n_copies, plan, after):
    n = len(bufs)
    after = list(after)

    def body(*refs):
        ssem_ref, rsem_ref = refs[n], refs[n + 1]
        copies = plan(refs[0:n], *_me())
        assert len(copies) == n_copies
        for k, (src, _, peer, land) in enumerate(copies):
            cp = _rcopy(src, land, ssem_ref.at[k], rsem_ref.at[k], peer)
            cp.wait_send()
            cp.wait_recv()

    outs = pl.pallas_call(
        body, name=name,
        in_specs=[HBM] * n + [SEM, SEM] + [ANY] * len(after), out_specs=[HBM] * n,
        out_shape=[pltpu.HBM(b.shape, b.dtype) for b in bufs],
        input_output_aliases={a: a for a in range(n)},
        compiler_params=SPLIT,
    )(*bufs, ssem, rsem, *after)
    return list(outs)


def _shard_half(buf, chip, half):
    if len(buf.shape) == 2:
        h, w = buf.shape[0] // 2, buf.shape[1] // N_CHIPS
        return buf.at[pl.ds(half * h, h), pl.ds(chip * w, w)]
    h = buf.shape[1] // 2
    return buf.at[chip, pl.ds(half * h, h)]


def plan_gather(refs, x, y, c):
    out = []
    for (px, py) in _chips(x, y):
        for buf in refs:
            own = _shard_half(buf, 2 * x + y, c)
            out.append((own, own, (px, py, c), _shard_half(buf, 2 * px + py, c)))
    return out


def plan_forward(refs, x, y, c):
    out = []
    for (px, py) in _chips(x, y):
        for buf in refs:
            landed = _shard_half(buf, 2 * px + py, c)
            out.append((landed, landed, (x, y, 1 - c), _shard_half(buf, 2 * px + py, 1 - c)))
    return out


def plan_sibling(refs, x, y, c):
    n = len(refs) // 2
    out = []
    for a in range(n):
        h = refs[a].shape[1] // 2
        out.append((refs[a].at[:, pl.ds((1 - c) * h, h)], refs[n + a], (x, y, 1 - c), refs[n + a]))
    return out


def plan_chip(refs, x, y, c):
    n = len(refs) // 2
    out = []
    for j, (px, py) in enumerate(_chips(x, y)):
        for a in range(n):
            out.append((refs[a].at[2 * px + py], refs[n + a].at[j], (px, py, c), refs[n + a].at[j]))
    return out


def plan_mod(refs, x, y, c):
    (mods,) = refs
    mine = mods.at[2 * x + y]
    return [(mine, mine, (px, py, c), mods.at[2 * px + py]) for (px, py) in _chips(x, y)]


def plan_pack(refs, x, y, c):
    (packs,) = refs
    mine = packs.at[4 * x + 2 * y + c]
    return [(mine, mine, peer, packs.at[4 * peer[0] + 2 * peer[1] + peer[2]]) for peer in _peers7(x, y, c)]


def plan_spread(layers, wp_layers):
    def plan(refs, x, y, c):
        gi, go, gp = refs
        hD, hR, hP = gi.shape[1] // 2, go.shape[1] // 2, gp.shape[2] // 2
        sib = (x, y, 1 - c)
        out = []
        for l in layers:
            mine = gi.at[l, pl.ds(c * hD, hD)]
            out.append((mine, mine, sib, gi.at[l, pl.ds((1 - c) * hD, hD)]))
            mine = go.at[l, pl.ds(c * hR, hR)]
            out.append((mine, mine, sib, go.at[l, pl.ds((1 - c) * hR, hR)]))
        for l in wp_layers:
            mine = gp.at[l, 2 * x + y, pl.ds(c * hP, hP)]
            for peer in _peers7(x, y, c):
                out.append((mine, mine, peer, gp.at[l, 2 * peer[0] + peer[1], pl.ds(peer[2] * hP, hP)]))
        return out

    return plan


def gather_small(c8, wc, token):
    def body(c_ref, wc_ref, token_ref, call, wcall, ssem, rsem, lsem):
        x, y, c = _me()
        myc = 2 * x + y
        me_lin = 4 * x + 2 * y + c
        me = (x, y, c)
        local = [pltpu.make_async_copy(c_ref, call.at[me_lin], lsem.at[0]),
                 pltpu.make_async_copy(wc_ref, wcall.at[myc], lsem.at[1])]
        for cp in local:
            cp.start()
        sends, recvs = [], []
        for m, peer in enumerate(_peers7(x, y, c)):
            plin = 4 * peer[0] + 2 * peer[1] + peer[2]
            sends.append(_rcopy(c_ref, call.at[me_lin], ssem.at[m], rsem.at[m], peer))
            recvs.append(_rcopy(call.at[plin], call.at[plin], ssem.at[m], rsem.at[m], me))
        for j, (px, py) in enumerate([(1 - x, y), (x, 1 - y), (1 - x, 1 - y)]):
            pc = 2 * px + py
            sends.append(_rcopy(wc_ref, wcall.at[myc], ssem.at[7 + j], rsem.at[7 + j], (px, py, c)))
            recvs.append(_rcopy(wcall.at[pc], wcall.at[pc], ssem.at[7 + j], rsem.at[7 + j], me))
        for cp in sends:
            cp.start()
        for cp in recvs:
            cp.wait_recv()
        for cp in sends:
            cp.wait_send()
        for cp in local:
            cp.wait()

    return pl.pallas_call(
        body, name="gather_small",
        in_specs=[ANY] * 3, out_specs=[ANY] * 2,
        out_shape=[jax.ShapeDtypeStruct((N_DEV, SUBLANES, LANES), F32),
                   jax.ShapeDtypeStruct((N_CHIPS, wc.shape[0], 3, LANES), F32)],
        scratch_shapes=[pltpu.SemaphoreType.DMA((10,)), pltpu.SemaphoreType.DMA((10,)), pltpu.SemaphoreType.DMA((2,))],
        compiler_params=_params(n_grid=0),
    )(c8, wc, token)


def spread_now(gi, go, gp, layers, wp_layers):
    plan = plan_spread(layers, wp_layers)
    n = 2 * len(layers) + 7 * len(wp_layers)

    def body(gi_in, go_in, gp_in, gi, go, gp, ssem, rsem):
        copies = plan((gi, go, gp), *_me())
        me = _me()
        sends = [_rcopy(src, dst, ssem.at[k], rsem.at[k], peer) for k, (src, dst, peer, _) in enumerate(copies)]
        for cp in sends:
            cp.start()
        for k, (_, _, _, land) in enumerate(copies):
            _rcopy(land, land, ssem.at[k], rsem.at[k], me).wait_recv()
        for cp in sends:
            cp.wait_send()

    return pl.pallas_call(
        body, name="spread_now",
        in_specs=[ANY] * 3, out_specs=[ANY] * 3,
        out_shape=[jax.ShapeDtypeStruct(a.shape, a.dtype) for a in (gi, go, gp)],
        input_output_aliases={0: 0, 1: 1, 2: 2},
        scratch_shapes=[pltpu.SemaphoreType.DMA((n,)), pltpu.SemaphoreType.DMA((n,))],
        compiler_params=_params(n_grid=0),
    )(gi, go, gp)


def add_sibling(cidx, mine, sib):
    def body(c_ref, *refs):
        for a in range(3):
            m, s, o = refs[a], refs[3 + a], refs[6 + a]
            o[...] = (m[...].astype(F32) + s[...].astype(F32)).astype(BF16)

    def mine_spec(a):
        h = a.shape[1] // 2
        return pl.BlockSpec((None, h, a.shape[2]), lambda j, c_ref: (j, c_ref[0], 0))

    def sib_spec(a):
        return pl.BlockSpec((None,) + a.shape[1:], lambda j, c_ref: (j, 0, 0))

    return pl.pallas_call(
        body, name="add_sibling",
        grid_spec=pltpu.PrefetchScalarGridSpec(
            num_scalar_prefetch=1, grid=(N_CHIPS,),
            in_specs=[mine_spec(a) for a in mine] + [sib_spec(a) for a in sib],
            out_specs=[sib_spec(a) for a in sib]),
        out_shape=[jax.ShapeDtypeStruct(a.shape, BF16) for a in sib],
        compiler_params=_params(VMEM_BIG),
    )(cidx, *mine, *sib)


def sum_chips(pos, own, rb, acc, l, shapes):
    nq = 4
    n_in = 6 + (3 if acc is not None else 0)

    def body(pos_ref, *refs):
        for a in range(3):
            m, b, o = refs[a], refs[3 + a], refs[n_in + a]
            s = m[...].astype(F32)
            for j in range(3):
                s = s + b[j].astype(F32)
            o[...] = s

    def own_spec(a):
        return pl.BlockSpec((None, a.shape[1] // nq, a.shape[2]), lambda q, p: (p[1], q, 0))

    def rb_spec(a):
        return pl.BlockSpec((3, a.shape[1] // nq, a.shape[2]), lambda q, p: (0, q, 0))

    hi, ho, hp = own[0].shape[1] // nq, own[1].shape[1] // nq, own[2].shape[1] // nq
    out_specs = [pl.BlockSpec((None, hi, shapes[0][2]), lambda q, p: (l, p[0] * nq + q, 0)),
                 pl.BlockSpec((None, ho, shapes[1][2]), lambda q, p: (l, p[0] * nq + q, 0)),
                 pl.BlockSpec((None, None, hp, LANES), lambda q, p: (l, p[1], p[0] * nq + q, 0))]
    in_specs = [own_spec(a) for a in own] + [rb_spec(a) for a in rb]
    args = list(own) + list(rb)
    aliases = {}
    if acc is not None:
        in_specs += [ANY] * 3
        args += list(acc)
        aliases = {7: 0, 8: 1, 9: 2}
    return pl.pallas_call(
        body, name="sum_chips",
        grid_spec=pltpu.PrefetchScalarGridSpec(num_scalar_prefetch=1, grid=(nq,), in_specs=in_specs, out_specs=out_specs),
        out_shape=[jax.ShapeDtypeStruct(s, F32) for s in shapes],
        input_output_aliases=aliases,
        compiler_params=_params(VMEM_BIG),
    )(pos, *args)


def pack_small(pos, per_layer, loss_blk):
    L = len(per_layer)
    D = per_layer[0][0].shape[1]

    def body(pos_ref, *refs):
        o = refs[-1]
        lb = refs[-2]
        o[...] = jnp.zeros_like(o)
        for l in range(L):
            dgpre, dgpost, dsh, dsc, dgt, dps, dwc = refs[7 * l:7 * l + 7]
            base = SUBLANES * l
            for r, src in enumerate((dgpre, dgpost, dsh, dsc, dgt)):
                o[pl.ds(base + r, 1), :] = src[...]
            o[pl.ds(base + 5, 1), 0:dps.shape[1]] = dps[...]
            for j in range(dwc.shape[0]):
                for k in range(3):
                    idx = 3 * j + k
                    o[pl.ds(base + 6 + idx // 8, 1), (idx % 8) * LANES:(idx % 8 + 1) * LANES] = dwc[j, pl.ds(k, 1), :]
        o[pl.ds(5, 1), 4 * LANES:5 * LANES] = lb[pl.ds(0, 1), :]

    flat = [a for layer in per_layer for a in layer] + [loss_blk]

    def whole(a):
        return pl.BlockSpec(a.shape, lambda i, p: (0,) * a.ndim)

    return pl.pallas_call(
        body, name="pack_small",
        grid_spec=pltpu.PrefetchScalarGridSpec(
            num_scalar_prefetch=1, grid=(1,), in_specs=[whole(a) for a in flat],
            out_specs=pl.BlockSpec((None, L * SUBLANES, D), lambda i, p: (p[2], 0, 0))),
        out_shape=jax.ShapeDtypeStruct((N_DEV, L * SUBLANES, D), F32),
        compiler_params=_params(),
    )(pos, *flat)


def small_update(pos, packs, params, moments_m, moments_v):
    n = len(params)
    L, D = params[1].shape
    PS = params[3].shape[1]

    def body(pos_ref, p_ref, *refs):
        ws, ms, vs = refs[0:n], refs[n:2 * n], refs[2 * n:3 * n]
        loss_ref = refs[3 * n]
        outs = [refs[3 * n + 1 + 4 * t:3 * n + 5 + 4 * t] for t in range(n)]
        summed = refs[-1]
        s = p_ref[0]
        for d in range(1, N_DEV):
            s = s + p_ref[d]
        summed[...] = s
        loss_ref[...] = summed[pl.ds(5, 1), 4 * LANES:5 * LANES]
        chip = pos_ref[1]

        def update(t, idx, g):
            d, mm, vv = _adamw_math(ws[t][idx], g, ms[t][idx], vs[t][idx])
            g_ref, d_ref, mo_ref, vo_ref = outs[t]
            g_ref[idx] = g
            d_ref[idx] = d
            mo_ref[idx] = mm
            vo_ref[idx] = vv

        for l in range(L):
            base = SUBLANES * l
            row = pl.ds(l, 1)
            for k in range(3):
                update(0, (row, slice(k * D, (k + 1) * D)), summed[pl.ds(base + 2 + k, 1), :])
            update(1, (row, slice(None)), summed[pl.ds(base, 1), :])
            update(2, (row, slice(None)), summed[pl.ds(base + 1, 1), :])
            update(3, (row, slice(None)), summed[pl.ds(base + 5, 1), 0:PS])
            for k in range(3):
                g = None
                for j in range(N_CHIPS):
                    idx = 3 * j + k
                    cand = summed[pl.ds(base + 6 + idx // 8, 1), (idx % 8) * LANES:(idx % 8 + 1) * LANES]
                    g = cand if g is None else jnp.where(chip == j, cand, g)
                update(4, (l, pl.ds(k, 1), slice(None)), g)

    def whole(a):
        return pl.BlockSpec(a.shape, lambda i, p: (0,) * a.ndim)

    ins = [packs] + list(params) + list(moments_m) + list(moments_v)
    out_shape = [jax.ShapeDtypeStruct((1, LANES), F32)]
    for w in params:
        out_shape += [jax.ShapeDtypeStruct(w.shape, F32)] * 4
    outs = pl.pallas_call(
        body, name="small_update",
        grid_spec=pltpu.PrefetchScalarGridSpec(
            num_scalar_prefetch=1, grid=(1,), in_specs=[whole(a) for a in ins],
            out_specs=[whole(a) for a in out_shape],
            scratch_shapes=[pltpu.VMEM(packs.shape[1:], F32)]),
        out_shape=out_shape,
        compiler_params=_params(),
    )(pos, *ins)
    return outs[0], [outs[1 + 4 * t:5 + 4 * t] for t in range(n)]


def _adamw_math(w, g, m, v):
    m = ADAM_B1 * m + (1.0 - ADAM_B1) * g
    v = ADAM_B2 * v + (1.0 - ADAM_B2) * (g * g)
    m_hat = m / (1.0 - ADAM_B1 ** ADAM_STEP)
    v_hat = v / (1.0 - ADAM_B2 ** ADAM_STEP)
    delta = -ADAM_LR * (m_hat / (jnp.sqrt(v_hat) + ADAM_EPS) + ADAM_WD * w)
    return delta, m, v


def adamw(w, g, m, v, block, name, first=0, count=None, acc=None):
    grid = tuple(s // b for s, b in zip(w.shape, block))
    if count is not None:
        grid = (count,) + grid[1:]

    def body(w_ref, g_ref, m_ref, v_ref, *rest):
        go_ref, d_ref, mo_ref, vo_ref = rest[-4:]
        gv = g_ref[...]
        d, mm, vv = _adamw_math(w_ref[...], gv, m_ref[...], v_ref[...])
        go_ref[...] = gv
        d_ref[...] = d
        mo_ref[...] = mm
        vo_ref[...] = vv

    spec = pl.BlockSpec(block, lambda i, *rest: (first + i,) + rest)
    shape = jax.ShapeDtypeStruct(w.shape, F32)
    extra = [] if acc is None else list(acc)
    return pl.pallas_call(
        body, name=name, grid=grid,
        in_specs=[spec] * 4 + [ANY] * len(extra), out_specs=[spec] * 4, out_shape=[shape] * 4,
        input_output_aliases={4 + a: a for a in range(len(extra))},
        compiler_params=_params(VMEM_BIG, n_grid=len(grid)),
    )(w, g, m, v, *extra)


def ada_finish(c_all, dmod, w, m, v):
    L, D, CW = w.shape
    hD = D // 2

    def body(c_ref, d_ref, w_ref, m_ref, v_ref, g_ref, dl_ref, mo_ref, vo_ref):
        cv = c_ref[...]
        z = jnp.zeros_like(cv)
        ca = jnp.concatenate([cv * jax.nn.sigmoid(cv), z], axis=0).astype(BF16)
        dm = jnp.concatenate([d_ref[0], jnp.zeros_like(d_ref[0])], axis=0).astype(BF16)
        g = lax.dot_general(ca, dm, TN, preferred_element_type=F32)
        g_ref[0] = g
        d, mm, vv = _adamw_math(w_ref[0], g, m_ref[0], v_ref[0])
        dl_ref[0] = d
        mo_ref[0] = mm
        vo_ref[0] = vv

    big = pl.BlockSpec((1, hD, CW), lambda l, h: (l, h, 0))
    shape = jax.ShapeDtypeStruct(w.shape, F32)
    return pl.pallas_call(
        body, name="ada_finish", grid=(L, 2),
        in_specs=[pl.BlockSpec((N_DEV, hD), lambda l, h: (0, h)), pl.BlockSpec((1, N_DEV, CW), lambda l, h: (l, 0, 0)),
                  big, big, big],
        out_specs=[big] * 4, out_shape=[shape] * 4,
        compiler_params=_params(VMEM_BIG, n_grid=2),
    )(c_all, dmod, w, m, v)


def kernel(x, c, w_ada, b_ada, g_pre, w_in, w_conv, w_pool, pool_scale, w_out, g_post, loss_target, m_w_ada, m_b_ada, m_g_pre, m_w_in, m_w_conv, m_w_pool, m_pool_scale, m_w_out, m_g_post, v_w_ada, v_b_ada, v_g_pre, v_w_in, v_w_conv, v_w_pool, v_pool_scale, v_w_out, v_g_post):
    L, D, CW = w_in.shape
    RO = w_out.shape[1]
    T = x.shape[1]
    ix, iy, ic = _me()
    chip = 2 * ix + iy
    me_lin = 4 * ix + 2 * iy + ic

    pos = jnp.stack([ic, chip, me_lin]).astype(jnp.int32)
    g_pre3, g_post3 = g_pre.reshape(L, 1, D), g_post.reshape(L, 1, D)
    pscale3 = pool_scale.reshape(L, 1, pool_scale.shape[1])
    n_s, n_c = 3, 9

    def gather(bufs, after):
        ss, rs, bufs, tok = xchg_start("gather_start", bufs, 3 * len(bufs), plan_gather, after=after)
        return (ss, rs, bufs), tok

    def arrive(flight, after):
        ss, rs, bufs = flight
        bufs = xchg_wait("gather_wait", bufs, ss, rs, 3 * len(bufs), plan_gather, after)
        fss, frs, bufs, tok = xchg_start("forward_start", bufs, 3 * len(bufs), plan_forward, sibling_only=True)
        return (fss, frs, bufs), tok

    def ready(flight, after):
        fss, frs, bufs = flight
        return xchg_wait("forward_wait", bufs, fss, frs, 3 * len(bufs), plan_forward, after)

    c_all3, wconv_all = gather_small(c.reshape(SUBLANES, LANES), w_conv, pos)
    c_all = c_all3.reshape(N_DEV, D)
    gi0, go0 = cast_weights(pos, w_in, w_out, 0, c_all3)
    fly_in0, token = gather([gi0], [])
    b_my = lax.dynamic_slice_in_dim(b_ada, chip * CW, CW, axis=1)
    m_ss, m_rs, mods, token = xchg_start("mod_start", [mod_part(pos, c_all, w_ada, b_my, token)], 3, plan_mod)
    fly_out0, token = gather([go0], [token])
    flying_w = [None] * L
    for l in range(1, L):
        flying_w[l], token = gather(list(cast_weights(pos, w_in, w_out, l, token)), [])
    fwd_in0, token = arrive(fly_in0, [token])
    (mod_all,) = xchg_wait("mod_wait", mods, m_ss, m_rs, 3, plan_mod, [token])
    mod = lax.dynamic_index_in_dim(mod_all, me_lin, axis=2, keepdims=False)
    mod4 = jnp.transpose(mod, (1, 0, 2)).reshape(L, 3, 1, D)

    xs, projs, ycats, ys = [x.reshape(T, D)], [], [], []
    wg_in, wg_out = [], []
    fwd_next = None
    for l in range(L):
        if l == 0:
            (gi,) = ready(fwd_in0, [mod4])
        else:
            gi, go = ready(fwd_next, [xs[l]])
        proj = proj_fwd(xs[l], mod4, g_pre3, gi, l)
        ycat = pool_fwd(proj, w_pool, pscale3, conv_fwd(proj, wconv_all, l), l)
        after = [ycat]
        if l == 0:
            fwd_out0, token = arrive(fly_out0, after)
            after = [token]
        if l + 1 < L:
            fwd_next, token = arrive(flying_w[l + 1], after)
            after = [token]
        if l == 0:
            (go,) = ready(fwd_out0, after)
        wg_in.append(gi)
        wg_out.append(go.reshape(N_CHIPS * RO, D))
        projs.append(proj)
        ycats.append(ycat)
        if l + 1 < L:
            xn, yv = out_fwd(ycat, wg_out[l], xs[l], mod4, g_post3, l, after[0])
            xs.append(xn)
        else:
            dx, yv, loss_blk = out_fwd_loss(ycat, wg_out[l], xs[l], mod4, g_post3, l, loss_target.reshape(T, D))
        ys.append(yv)

    shapes = (w_in.shape, w_out.shape, w_pool.shape)
    smalls = [None] * L
    acc, flying, sib, token = None, None, None, loss_blk

    def to_chips(sib, after):
        sl, s_ss, s_rs, s_bufs = sib
        s_bufs = xchg_wait("sibling_wait", s_bufs, s_ss, s_rs, n_s, plan_sibling, after)
        chip_parts = add_sibling(pos, s_bufs[0:3], s_bufs[3:6])
        lands = [lax.empty((3,) + a.shape[1:], a.dtype) for a in chip_parts]
        c_ss, c_rs, c_bufs, ctoken = xchg_start("chip_start", list(chip_parts) + lands, n_c, plan_chip)
        return (sl, c_ss, c_rs, c_bufs), ctoken

    def landed(flying, acc, after):
        fl, f_ss, f_rs, f_bufs = flying
        f_bufs = xchg_wait("chip_wait", f_bufs, f_ss, f_rs, n_c, plan_chip, after)
        return sum_chips(pos, f_bufs[0:3], f_bufs[3:6], acc, fl, shapes)

    for l in reversed(range(L)):
        dycat, dwo_l, dgate, dgpost = out_bwd(dx, ys[l], ycats[l], wg_out[l], mod4, g_post3, l, token)
        token = dycat
        if sib is not None:
            arrived = flying
            flying, token = to_chips(sib, [dycat])
            if arrived is not None:
                acc = landed(arrived, acc, [token])
                token = acc[0]
        du_p, dg_p, dwp_l, dps = pool_bwd(projs[l], dycat, w_pool, pscale3, l, token)
        du_a, db_a, dc_a, dg_a, dwc = conv_bwd(projs[l], dycat, wconv_all, l)
        dx, dwi_l, dshift, dscale, dgpre = in_bwd([du_a, db_a, dc_a, dg_a, du_p, dg_p], wg_in[l], xs[l], dx,
                                                  mod4, g_pre3, l)
        smalls[l] = (dgpre, dgpost, dshift, dscale, dgate, dps, dwc)
        parts = [dwi_l, dwo_l.reshape(N_CHIPS, RO, D), dwp_l]
        s_lands = [lax.empty((a.shape[0], a.shape[1] // 2) + a.shape[2:], a.dtype) for a in parts]
        s_ss, s_rs, s_bufs, token = xchg_start("sibling_start", parts + s_lands, n_s, plan_sibling, sibling_only=True)
        sib = (l, s_ss, s_rs, s_bufs)
    grad_x = dx.reshape(1, T, D)

    p_ss, p_rs, packs, ptoken = xchg_start("pack_start", [pack_small(pos, smalls, loss_blk)], N_DEV - 1, plan_pack)
    acc = landed(flying, acc, [ptoken, token])
    n_sp = (2 + N_DEV - 1) * (L - 1)
    spread = plan_spread(tuple(range(1, L)), tuple(range(1, L)))
    sp_ss, sp_rs, acc, sp_token = xchg_start("spread_start", list(acc), n_sp, spread)
    flying, token = to_chips(sib, [sp_token])
    (packs_all,) = xchg_wait("pack_wait", packs, p_ss, p_rs, N_DEV - 1, plan_pack, [token])
    dmod_all = packs_all.reshape(N_DEV, L, SUBLANES, D)[:, :, 2:5].reshape(N_DEV, L, 3 * D)
    dmod_my = jnp.transpose(lax.dynamic_slice_in_dim(dmod_all, chip * CW, CW, axis=2), (1, 0, 2))

    g_w_ada, d_w_ada, nm_w_ada, nv_w_ada = ada_finish(c_all, dmod_my, w_ada, m_w_ada, v_w_ada)
    loss_row, upd = small_update(pos, packs_all, [b_ada, g_pre, g_post, pool_scale, w_conv],
                                 [m_b_ada, m_g_pre, m_g_post, m_pool_scale, m_w_conv],
                                 [v_b_ada, v_g_pre, v_g_post, v_pool_scale, v_w_conv])
    loss = loss_row[0, 0]
    (g_b_ada, d_b_ada, nm_b_ada, nv_b_ada), (g_g_pre, d_g_pre, nm_g_pre, nv_g_pre) = upd[0], upd[1]
    (g_g_post, d_g_post, nm_g_post, nv_g_post), (g_pscale, d_pscale, nm_pscale, nv_pscale) = upd[2], upd[3]
    g_w_conv, d_w_conv, nm_w_conv, nv_w_conv = upd[4]

    done = [nv_w_ada, nv_w_conv]
    g_w_in, g_w_out, g_w_pool = xchg_wait("spread_wait", acc, sp_ss, sp_rs, n_sp, spread, done)
    in_blk, out_blk = (1, D // 2, CW), (1, RO, D)
    upd_in = adamw(w_in, g_w_in, m_w_in, v_w_in, in_blk, "adamw_w_in", 1, L - 1)
    upd_out = adamw(w_out, g_w_out, m_w_out, v_w_out, out_blk, "adamw_w_out", 1, L - 1)

    acc = landed(flying, (g_w_in, g_w_out, g_w_pool), [upd_in[3], upd_out[3]])
    r_w_in, r_w_out, r_w_pool = spread_now(*acc, (0,), (0,))
    g_w_in, d_w_in, nm_w_in, nv_w_in = adamw(w_in, r_w_in, m_w_in, v_w_in, in_blk, "adamw_w_in", 0, 1, upd_in)
    g_w_out, d_w_out, nm_w_out, nv_w_out = adamw(w_out, r_w_out, m_w_out, v_w_out, out_blk, "adamw_w_out", 0, 1, upd_out)
    pshape = (L, N_CHIPS * LANES, LANES)
    upd_pool = adamw(w_pool.reshape(pshape), r_w_pool.reshape(pshape), m_w_pool.reshape(pshape),
                     v_w_pool.reshape(pshape), (1,) + pshape[1:], "adamw_w_pool")
    g_w_pool, d_w_pool, nm_w_pool, nv_w_pool = [a.reshape(w_pool.shape) for a in upd_pool]

    return (loss, grad_x,
            g_w_ada, g_b_ada, g_g_pre, g_w_in, g_w_conv, g_w_pool, g_pscale, g_w_out, g_g_post,
            d_w_ada, d_b_ada, d_g_pre, d_w_in, d_w_conv, d_w_pool, d_pscale, d_w_out, d_g_post,
            nm_w_ada, nm_b_ada, nm_g_pre, nm_w_in, nm_w_conv, nm_w_pool, nm_pscale, nm_w_out, nm_g_post,
            nv_w_ada, nv_b_ada, nv_g_pre, nv_w_in, nv_w_conv, nv_w_pool, nv_pscale, nv_w_out, nv_g_post)
```

```python
import functools

import jax
import jax.numpy as jnp
from jax import lax
from jax.experimental import pallas as pl
from jax.experimental.pallas import tpu as pltpu

F32 = jnp.float32
BF16 = jnp.bfloat16
MESH = pl.DeviceIdType.MESH
ANY = pl.BlockSpec(memory_space=pl.ANY)

NORM_EPS = 1e-6
POOL_WINDOWS = (2, 4, 8, 16)
ADAM_LR = 0.001
ADAM_B1 = 0.9
ADAM_B2 = 0.999
ADAM_EPS = 1e-08
ADAM_WD = 0.01
ADAM_STEP = 10

N_CHIPS = 4
N_DEV = 8
LANES = 128
SUBLANES = 8
VMEM_BIG = 56 * 1024 * 1024
HIST = 16
R_CONV = 64
R_POOL = 128

NT = (((1,), (1,)), ((), ()))
TN = (((0,), (0,)), ((), ()))


def _params(vmem=None, n_grid=1):
    kw = {}
    if n_grid:
        kw["dimension_semantics"] = ("arbitrary",) * n_grid
    if vmem is not None:
        kw["vmem_limit_bytes"] = vmem
    return pltpu.CompilerParams(**kw)


def _colsum8(v):
    n, d = v.shape
    return v.reshape(n // SUBLANES, SUBLANES, d).sum(axis=0)


def _rms(v):
    return lax.rsqrt(jnp.mean(v * v, axis=-1, keepdims=True) + NORM_EPS)


def _sigmoid(v):
    return 0.5 * jnp.tanh(0.5 * v) + 0.5


def _shift_down(ext, k, rows):
    if k == 0:
        return ext[HIST:HIST + rows]
    return pltpu.roll(ext, k, 0)[HIST:HIST + rows]


def _shift_up(ext, k, rows):
    if k == 0:
        return ext[0:rows]
    return pltpu.roll(ext, ext.shape[0] - k, 0)[0:rows]


def _load_ext(ref, r0, h0, first, rows):
    hist = ref[pl.ds(h0, HIST), :].astype(F32)
    hist = jnp.where(first, 0.0, hist)
    cur = ref[pl.ds(r0, rows), :].astype(F32)
    return jnp.concatenate([hist, cur], axis=0)


def _me():
    return lax.axis_index("x"), lax.axis_index("y"), lax.axis_index("c")


def cast_weights(pos, w_in, w_out, l, after):
    _, D, CW = w_in.shape
    RO = w_out.shape[1]

    def body(pos_ref, wi, wo, after_ref, oi, oo):
        oi[...] = wi[...].astype(BF16)
        oo[...] = wo[...].astype(BF16)

    return pl.pallas_call(
        body, name="cast_w",
        grid_spec=pltpu.PrefetchScalarGridSpec(
            num_scalar_prefetch=1, grid=(2,),
            in_specs=[pl.BlockSpec((None, D // 2, CW), lambda h, p: (l, h, 0)),
                      pl.BlockSpec((None, RO // 2, D), lambda h, p: (l, h, 0)), ANY],
            out_specs=[pl.BlockSpec((D // 2, CW), lambda h, p: (h, p[1])),
                       pl.BlockSpec((None, RO // 2, D), lambda h, p: (p[1], h, 0))]),
        out_shape=[jax.ShapeDtypeStruct((D, N_CHIPS * CW), BF16), jax.ShapeDtypeStruct((N_CHIPS, RO, D), BF16)],
        compiler_params=_params(),
    )(pos, w_in, w_out, after)


def mod_part(pos, c_all, w_ada, b_my, after):
    L, D, CW = w_ada.shape

    def body(pos_ref, c_ref, w_ref, b_ref, after_ref, o_ref):
        cv = c_ref[...]
        ca = (cv * jax.nn.sigmoid(cv)).astype(BF16)
        o_ref[...] = jnp.dot(ca, w_ref[0].astype(BF16), preferred_element_type=F32) + b_ref[0]

    return pl.pallas_call(
        body, name="mod_part",
        grid_spec=pltpu.PrefetchScalarGridSpec(
            num_scalar_prefetch=1, grid=(L,),
            in_specs=[pl.BlockSpec((N_DEV, D), lambda l, p: (0, 0)),
                      pl.BlockSpec((1, D, CW), lambda l, p: (l, 0, 0)),
                      pl.BlockSpec((1, 1, CW), lambda l, p: (l, 0, 0)), ANY],
            out_specs=pl.BlockSpec((None, None, N_DEV, CW), lambda l, p: (p[1], l, 0, 0))),
        out_shape=jax.ShapeDtypeStruct((N_CHIPS, L, N_DEV, CW), F32),
        compiler_params=_params(VMEM_BIG),
    )(pos, c_all, w_ada, b_my.reshape(L, 1, CW), after)


def _mod_row(l, k, D):
    return pl.BlockSpec((None, None, 1, D), lambda *_: (l, k, 0, 0))


def _layer_row(l, D):
    return pl.BlockSpec((None, 1, D), lambda *_: (l, 0, 0))


def proj_fwd(x, mod4, g_pre3, wg, l):
    T, D = x.shape
    NC = wg.shape[1]
    NB = N_CHIPS
    CW = NC // NB
    tm = 512

    def body(x_ref, sh_ref, sc_ref, g_ref, w_ref, o_ref):
        xv = x_ref[...]
        h = (xv * _rms(xv) * g_ref[...]) * (1.0 + sc_ref[...]) + sh_ref[...]
        hb = h.astype(BF16)
        for j in range(NB):
            cols = slice(j * CW, (j + 1) * CW)
            o_ref[:, cols] = jnp.dot(hb, w_ref[:, cols], preferred_element_type=F32).astype(BF16)

    return pl.pallas_call(
        body, name="proj_fwd", grid=(T // tm,),
        in_specs=[pl.BlockSpec((tm, D), lambda i: (i, 0)), _mod_row(l, 0, D), _mod_row(l, 1, D), _layer_row(l, D),
                  pl.BlockSpec((D, NC), lambda i: (0, 0))],
        out_specs=pl.BlockSpec((tm, NC), lambda i: (i, 0)),
        out_shape=jax.ShapeDtypeStruct((T, NC), BF16),
        compiler_params=_params(VMEM_BIG),
    )(x, mod4, mod4, g_pre3, wg)


def conv_fwd(proj, wconv, l):
    T = proj.shape[0]
    R = R_CONV
    nblk = 4

    def body(u_ref, b_ref, c_ref, g_ref, w_ref, o_ref):
        w0 = w_ref[pl.ds(0, 1), :]
        w1 = w_ref[pl.ds(1, 1), :]
        w2 = w_ref[pl.ds(2, 1), :]

        def chunk(i, carry):
            r0 = pl.multiple_of(i * R, R)
            h0 = pl.multiple_of(jnp.maximum(r0 - HIST, 0), HIST)
            first = i == 0
            ca = _load_ext(c_ref, r0, h0, first, R) * _load_ext(u_ref, r0, h0, first, R)
            conv = w2 * ca[HIST:] + w1 * _shift_down(ca, 1, R) + w0 * _shift_down(ca, 2, R)
            g = g_ref[pl.ds(r0, R), :].astype(F32)
            b = b_ref[pl.ds(r0, R), :].astype(F32)
            o_ref[pl.ds(r0, R), :] = (b * conv * (g * _sigmoid(g))).astype(BF16)
            return carry

        lax.fori_loop(0, T // R, chunk, 0)

    def col(off):
        return pl.BlockSpec((T, LANES), lambda j: (0, j + off))

    return pl.pallas_call(
        body, name="conv_fwd", grid=(nblk,),
        in_specs=[col(0), col(4), col(8), col(12), pl.BlockSpec((None, None, 3, LANES), lambda j: (j, l, 0, 0))],
        out_specs=pl.BlockSpec((T, LANES), lambda j: (0, j)),
        out_shape=jax.ShapeDtypeStruct((T, nblk * LANES), BF16),
        compiler_params=_params(),
    )(proj, proj, proj, proj, wconv)


def _causal_window_sum(ext, w):
    s, k = ext, 1
    while k < w:
        s = s + pltpu.roll(s, k, 0)
        k *= 2
    return s


def _anticausal_window_sum(ext, w):
    s, k = ext, 1
    n = ext.shape[0]
    while k < w:
        s = s + pltpu.roll(s, n - k, 0)
        k *= 2
    return s


def _count(r0, rows, w):
    t = r0 + lax.broadcasted_iota(jnp.int32, (rows, LANES), 0)
    return jnp.minimum(t + 1, w).astype(F32)


def _pooled_loop(p_ref, pooled_s, w, T):
    R = R_POOL

    def chunk(i, carry):
        r0 = pl.multiple_of(i * R, R)
        h0 = pl.multiple_of(jnp.maximum(r0 - HIST, 0), HIST)
        ext = _load_ext(p_ref, r0, h0, i == 0, R)
        ws = _causal_window_sum(ext, w)[HIST:]
        pooled_s[pl.ds(r0, R), :] = (ws / _count(r0, R, w) - ext[HIST:]).astype(BF16)
        return carry

    lax.fori_loop(0, T // R, chunk, 0)


def _pool_w_spec(l):
    return pl.BlockSpec((None, None, LANES, LANES), lambda j: (l, j, 0, 0))


def _pool_s_spec(l):
    return pl.BlockSpec((None, 1, LANES), lambda j: (l, 0, j))


def pool_fwd(proj, wpool, pscale3, l):
    T = proj.shape[0]
    R = R_POOL
    ngrp = len(POOL_WINDOWS)

    def body(p_ref, g_ref, w_ref, s_ref, o_ref, pooled_s, mixed_s):
        grp = pl.program_id(0)

        def group(w):
            _pooled_loop(p_ref, pooled_s, w, T)
            mixed_s[...] = jnp.dot(pooled_s[...], w_ref[...].astype(BF16), preferred_element_type=F32)
            sc = s_ref[...]

            def chunk(i, carry):
                r0 = pl.multiple_of(i * R, R)
                g = g_ref[pl.ds(r0, R), :].astype(F32)
                o_ref[pl.ds(r0, R), :] = (mixed_s[pl.ds(r0, R), :] * sc * (g * _sigmoid(g))).astype(BF16)
                return carry

            lax.fori_loop(0, T // R, chunk, 0)

        for k, w in enumerate(POOL_WINDOWS):
            pl.when(grp == k)(functools.partial(group, w))

    return pl.pallas_call(
        body, name="pool_fwd", grid=(ngrp,),
        in_specs=[pl.BlockSpec((T, LANES), lambda j: (0, j + 16)), pl.BlockSpec((T, LANES), lambda j: (0, j + 20)),
                  _pool_w_spec(l), _pool_s_spec(l)],
        out_specs=pl.BlockSpec((T, LANES), lambda j: (0, j)),
        out_shape=jax.ShapeDtypeStruct((T, ngrp * LANES), BF16),
        scratch_shapes=[pltpu.VMEM((T, LANES), BF16), pltpu.VMEM((T, LANES), F32)],
        compiler_params=_params(),
    )(proj, proj, wpool, pscale3)


def out_fwd(ya, yp, wo, x, mod4, g_post3, l, after):
    T, D = x.shape
    H = ya.shape[1]
    tm = 512

    def body(ya_ref, yp_ref, wo_ref, x_ref, gt_ref, g_ref, after_ref, xn_ref, y_ref):
        y = (jnp.dot(ya_ref[...], wo_ref[0:H, :], preferred_element_type=F32)
             + jnp.dot(yp_ref[...], wo_ref[H:2 * H, :], preferred_element_type=F32))
        xn_ref[...] = x_ref[...] + gt_ref[...] * (y * _rms(y) * g_ref[...])
        y_ref[...] = y.astype(BF16)

    tile = pl.BlockSpec((tm, D), lambda i: (i, 0))
    half = pl.BlockSpec((tm, H), lambda i: (i, 0))
    return pl.pallas_call(
        body, name="out_fwd", grid=(T // tm,),
        in_specs=[half, half, pl.BlockSpec((2 * H, D), lambda i: (0, 0)), tile, _mod_row(l, 2, D), _layer_row(l, D),
                  ANY],
        out_specs=[tile, tile],
        out_shape=[jax.ShapeDtypeStruct((T, D), F32), jax.ShapeDtypeStruct((T, D), BF16)],
        compiler_params=_params(VMEM_BIG),
    )(ya, yp, wo, x, mod4, g_post3, after)


def out_fwd_loss(ya, yp, wo, x, mod4, g_post3, l, target):
    T, D = x.shape
    H = ya.shape[1]
    tm = 512
    nt = T // tm

    def body(ya_ref, yp_ref, wo_ref, x_ref, gt_ref, g_ref, t_ref, dx_ref, y_ref, l_ref, acc):
        i = pl.program_id(0)

        @pl.when(i == 0)
        def _():
            acc[...] = jnp.zeros_like(acc)

        y = (jnp.dot(ya_ref[...], wo_ref[0:H, :], preferred_element_type=F32)
             + jnp.dot(yp_ref[...], wo_ref[H:2 * H, :], preferred_element_type=F32))
        y_ref[...] = y.astype(BF16)
        d = (x_ref[...] + gt_ref[...] * (y * _rms(y) * g_ref[...])) - t_ref[...]
        dx_ref[...] = d * (1.0 / D)
        acc[...] += _colsum8(d * d)

        @pl.when(i == nt - 1)
        def _():
            l_ref[...] = jnp.zeros_like(l_ref) + jnp.sum(acc[...]) * (0.5 / D)

    tile = pl.BlockSpec((tm, D), lambda i: (i, 0))
    half = pl.BlockSpec((tm, H), lambda i: (i, 0))
    return pl.pallas_call(
        body, name="out_fwd_loss", grid=(nt,),
        in_specs=[half, half, pl.BlockSpec((2 * H, D), lambda i: (0, 0)), tile, _mod_row(l, 2, D), _layer_row(l, D),
                  tile],
        out_specs=[tile, tile, pl.BlockSpec((SUBLANES, LANES), lambda i: (0, 0))],
        out_shape=[jax.ShapeDtypeStruct((T, D), F32), jax.ShapeDtypeStruct((T, D), BF16),
                   jax.ShapeDtypeStruct((SUBLANES, LANES), F32)],
        scratch_shapes=[pltpu.VMEM((SUBLANES, D), F32)],
        compiler_params=_params(VMEM_BIG),
    )(ya, yp, wo, x, mod4, g_post3, target)


def out_bwd(dx, y, ya, yp, wo, mod4, g_post3, l, after):
    T, D = dx.shape
    H = ya.shape[1]
    tm = 512
    nt = T // tm

    def body(dx_ref, y_ref, ya_ref, yp_ref, wo_ref, gt_ref, g_ref, after_ref,
             dya_ref, dyp_ref, dwo_ref, dgt_ref, dg_ref, acc_w, acc_p):
        i = pl.program_id(0)

        @pl.when(i == 0)
        def _():
            acc_w[...] = jnp.zeros_like(acc_w)
            acc_p[...] = jnp.zeros_like(acc_p)

        yv = y_ref[...].astype(F32)
        dxv = dx_ref[...]
        gg = gt_ref[...] * g_ref[...]
        r = _rms(yv)
        yn = yv * r
        p = dxv * yn
        acc_p[...] += _colsum8(p)
        dy = r * (dxv * gg - yn * jnp.mean(p * gg, axis=-1, keepdims=True))
        dyb = dy.astype(BF16)
        dyc = lax.dot_general(dyb, wo_ref[...], NT, preferred_element_type=F32)
        dya_ref[...] = dyc[:, 0:H].astype(BF16)
        dyp_ref[...] = dyc[:, H:2 * H].astype(BF16)
        acc_w[0:H, :] += lax.dot_general(ya_ref[...], dyb, TN, preferred_element_type=F32)
        acc_w[H:2 * H, :] += lax.dot_general(yp_ref[...], dyb, TN, preferred_element_type=F32)

        @pl.when(i == nt - 1)
        def _():
            dwo_ref[...] = acc_w[...].astype(BF16)
            sp = jnp.sum(acc_p[...], axis=0, keepdims=True)
            dgt_ref[...] = g_ref[...] * sp
            dg_ref[...] = gt_ref[...] * sp

    row = pl.BlockSpec((1, D), lambda i: (0, 0))
    tile = pl.BlockSpec((tm, D), lambda i: (i, 0))
    half = pl.BlockSpec((tm, H), lambda i: (i, 0))
    full = pl.BlockSpec((2 * H, D), lambda i: (0, 0))
    return pl.pallas_call(
        body, name="out_bwd", grid=(nt,),
        in_specs=[tile, tile, half, half, full, _mod_row(l, 2, D), _layer_row(l, D), ANY],
        out_specs=[half, half, full, row, row],
        out_shape=[jax.ShapeDtypeStruct((T, H), BF16), jax.ShapeDtypeStruct((T, H), BF16),
                   jax.ShapeDtypeStruct((2 * H, D), BF16),
                   jax.ShapeDtypeStruct((1, D), F32), jax.ShapeDtypeStruct((1, D), F32)],
        scratch_shapes=[pltpu.VMEM((2 * H, D), F32), pltpu.VMEM((SUBLANES, D), F32)],
        compiler_params=_params(VMEM_BIG),
    )(dx, y, ya, yp, wo, mod4, g_post3, after)


def conv_bwd(proj, dya, wconv, l):
    T = proj.shape[0]
    R = R_CONV
    nblk = 4
    nchunk = T // R

    def body(u_ref, b_ref, c_ref, g_ref, dy_ref, w_ref, du_ref, db_ref, dc_ref, dg_ref, dw_ref):
        w0 = w_ref[pl.ds(0, 1), :]
        w1 = w_ref[pl.ds(1, 1), :]
        w2 = w_ref[pl.ds(2, 1), :]

        def chunk(k, carry):
            head, a0, a1, a2 = carry
            i = nchunk - 1 - k
            r0 = pl.multiple_of(i * R, R)
            h0 = pl.multiple_of(jnp.maximum(r0 - HIST, 0), HIST)
            first = i == 0
            ue = _load_ext(u_ref, r0, h0, first, R)
            ce = _load_ext(c_ref, r0, h0, first, R)
            ca = ce * ue
            ca0 = ca[HIST:]
            ca1 = _shift_down(ca, 1, R)
            ca2 = _shift_down(ca, 2, R)
            conv = w2 * ca0 + w1 * ca1 + w0 * ca2
            g = g_ref[pl.ds(r0, R), :].astype(F32)
            b = b_ref[pl.ds(r0, R), :].astype(F32)
            dy = dy_ref[pl.ds(r0, R), :].astype(F32)
            sg = _sigmoid(g)
            sl = g * sg
            t = dy * conv
            db_ref[pl.ds(r0, R), :] = (t * sl).astype(BF16)
            dg_ref[pl.ds(r0, R), :] = (t * b * (sg * (1.0 + g * (1.0 - sg)))).astype(BF16)
            dconv = dy * b * sl
            a2 = a2 + _colsum8(dconv * ca0)
            a1 = a1 + _colsum8(dconv * ca1)
            a0 = a0 + _colsum8(dconv * ca2)
            e = jnp.concatenate([dconv, head], axis=0)
            dca = w2 * dconv + w1 * _shift_up(e, 1, R) + w0 * _shift_up(e, 2, R)
            du_ref[pl.ds(r0, R), :] = (dca * ce[HIST:]).astype(BF16)
            dc_ref[pl.ds(r0, R), :] = (dca * ue[HIST:]).astype(BF16)
            return dconv[0:SUBLANES], a0, a1, a2

        z = jnp.zeros((SUBLANES, LANES), F32)
        _, a0, a1, a2 = lax.fori_loop(0, nchunk, chunk, (z, z, z, z))
        dw_ref[pl.ds(0, 1), :] = jnp.sum(a0, axis=0, keepdims=True)
        dw_ref[pl.ds(1, 1), :] = jnp.sum(a1, axis=0, keepdims=True)
        dw_ref[pl.ds(2, 1), :] = jnp.sum(a2, axis=0, keepdims=True)

    def col(off):
        return pl.BlockSpec((T, LANES), lambda j: (0, j + off))

    sec = jax.ShapeDtypeStruct((T, nblk * LANES), BF16)
    return pl.pallas_call(
        body, name="conv_bwd", grid=(nblk,),
        in_specs=[col(0), col(4), col(8), col(12), col(0), pl.BlockSpec((None, None, 3, LANES), lambda j: (j, l, 0, 0))],
        out_specs=[col(0), col(0), col(0), col(0), pl.BlockSpec((None, 3, LANES), lambda j: (j, 0, 0))],
        out_shape=[sec, sec, sec, sec, jax.ShapeDtypeStruct((nblk, 3, LANES), F32)],
        compiler_params=_params(),
    )(proj, proj, proj, proj, dya, wconv)


def pool_bwd(proj, dyp, wpool, pscale3, l, after):
    T = proj.shape[0]
    R = R_POOL
    ngrp = len(POOL_WINDOWS)
    nchunk = T // R

    def body(p_ref, g_ref, dy_ref, w_ref, s_ref, after_ref, du_ref, dg_ref, dw_ref, ds_ref,
             pooled_s, mixed_s, dmix_s, dpool_s):
        grp = pl.program_id(0)

        def group(w):
            wb = w_ref[...].astype(BF16)
            _pooled_loop(p_ref, pooled_s, w, T)
            mixed_s[...] = jnp.dot(pooled_s[...], wb, preferred_element_type=F32)
            sc = s_ref[...]

            def gate_chunk(i, acc):
                r0 = pl.multiple_of(i * R, R)
                g = g_ref[pl.ds(r0, R), :].astype(F32)
                dy = dy_ref[pl.ds(r0, R), :].astype(F32)
                mixed = mixed_s[pl.ds(r0, R), :]
                sg = _sigmoid(g)
                dg_ref[pl.ds(r0, R), :] = (dy * mixed * sc * (sg * (1.0 + g * (1.0 - sg)))).astype(BF16)
                dms = dy * (g * sg)
                dmix_s[pl.ds(r0, R), :] = (dms * sc).astype(BF16)
                return acc + _colsum8(dms * mixed)

            acc = lax.fori_loop(0, nchunk, gate_chunk, jnp.zeros((SUBLANES, LANES), F32))
            ds_ref[...] = jnp.sum(acc, axis=0, keepdims=True)
            dpool_s[pl.ds(0, T), :] = lax.dot_general(dmix_s[...], wb, NT, preferred_element_type=F32)
            dpool_s[pl.ds(T, HIST), :] = jnp.zeros((HIST, LANES), F32)
            dw_ref[...] = lax.dot_general(pooled_s[...], dmix_s[...], TN, preferred_element_type=F32).astype(BF16)

            def back_chunk(i, carry):
                r0 = pl.multiple_of(i * R, R)
                dpe = dpool_s[pl.ds(r0, R + HIST), :]
                e = dpe / _count(r0, R + HIST, w)
                du_ref[pl.ds(r0, R), :] = (_anticausal_window_sum(e, w)[0:R] - dpe[0:R]).astype(BF16)
                return carry

            lax.fori_loop(0, nchunk, back_chunk, 0)

        for k, w in enumerate(POOL_WINDOWS):
            pl.when(grp == k)(functools.partial(group, w))

    def col(off):
        return pl.BlockSpec((T, LANES), lambda j: (0, j + off))

    sec = jax.ShapeDtypeStruct((T, ngrp * LANES), BF16)
    wspec = pl.BlockSpec((None, LANES, LANES), lambda j: (j, 0, 0))
    sspec = pl.BlockSpec((1, LANES), lambda j: (0, j))
    return pl.pallas_call(
        body, name="pool_bwd", grid=(ngrp,),
        in_specs=[col(16), col(20), col(0), _pool_w_spec(l), _pool_s_spec(l), ANY],
        out_specs=[col(0), col(0), wspec, sspec],
        out_shape=[sec, sec, jax.ShapeDtypeStruct((ngrp, LANES, LANES), BF16),
                   jax.ShapeDtypeStruct((1, ngrp * LANES), F32)],
        scratch_shapes=[pltpu.VMEM((T, LANES), BF16), pltpu.VMEM((T, LANES), F32),
                        pltpu.VMEM((T, LANES), BF16), pltpu.VMEM((T + HIST, LANES), F32)],
        compiler_params=_params(),
    )(proj, proj, dyp, wpool, pscale3, after)


def in_bwd(dsecs, wg, x, dxo, mod4, g_pre3, l):
    T, D = x.shape
    NB = N_CHIPS
    CW = wg.shape[1] // NB
    SW = dsecs[0].shape[1]
    nsec = len(dsecs)
    PW = 256
    assert SW % PW == 0 and CW % PW == 0
    tm = 256
    nt = T // tm

    def body(*refs):
        d_refs = refs[0:nsec]
        w_ref, x_ref, dxo_ref, sh_ref, sc_ref, g_ref = refs[nsec:nsec + 6]
        dxi_ref, dw_ref, dsh_ref, dsc_ref, dg_ref = refs[nsec + 6:nsec + 11]
        acc_w, acc_sh, acc_q = refs[nsec + 11:]
        i = pl.program_id(0)

        @pl.when(i == 0)
        def _():
            acc_w[...] = jnp.zeros_like(acc_w)
            acc_sh[...] = jnp.zeros_like(acc_sh)
            acc_q[...] = jnp.zeros_like(acc_q)

        xv = x_ref[...]
        r = _rms(xv)
        xh = xv * r
        sg = g_ref[...] * (1.0 + sc_ref[...])
        hb = (xh * sg + sh_ref[...]).astype(BF16)
        dh = lax.dot_general(d_refs[0][...], w_ref[:, 0:SW], NT, preferred_element_type=F32)
        for s in range(1, nsec):
            dh = dh + lax.dot_general(d_refs[s][...], w_ref[:, s * SW:(s + 1) * SW], NT, preferred_element_type=F32)
        for p in range(nsec * SW // PW):
            col = p * PW
            s, so = col // SW, col % SW
            j, jo = col // CW, col % CW
            acc_w[j, :, jo:jo + PW] += lax.dot_general(hb, d_refs[s][:, so:so + PW], TN, preferred_element_type=F32)
        q = dh * xh
        acc_sh[...] += _colsum8(dh)
        acc_q[...] += _colsum8(q)
        dxi_ref[...] = dxo_ref[...] + r * (dh * sg - xh * jnp.mean(q * sg, axis=-1, keepdims=True))

        @pl.when(i == nt - 1)
        def _():
            dw_ref[...] = acc_w[...].astype(BF16)
            sq = jnp.sum(acc_q[...], axis=0, keepdims=True)
            dsh_ref[...] = jnp.sum(acc_sh[...], axis=0, keepdims=True)
            dsc_ref[...] = g_ref[...] * sq
            dg_ref[...] = (1.0 + sc_ref[...]) * sq

    row = pl.BlockSpec((1, D), lambda i: (0, 0))
    tile = pl.BlockSpec((tm, D), lambda i: (i, 0))
    sect = pl.BlockSpec((tm, SW), lambda i: (i, 0))
    rowshape = jax.ShapeDtypeStruct((1, D), F32)
    return pl.pallas_call(
        body, name="in_bwd", grid=(nt,),
        in_specs=[sect] * nsec + [pl.BlockSpec((D, NB * CW), lambda i: (0, 0)), tile, tile,
                                  _mod_row(l, 0, D), _mod_row(l, 1, D), _layer_row(l, D)],
        out_specs=[tile, pl.BlockSpec((NB, D, CW), lambda i: (0, 0, 0)), row, row, row],
        out_shape=[jax.ShapeDtypeStruct((T, D), F32), jax.ShapeDtypeStruct((NB, D, CW), BF16),
                   rowshape, rowshape, rowshape],
        scratch_shapes=[pltpu.VMEM((NB, D, CW), F32),
                        pltpu.VMEM((SUBLANES, D), F32), pltpu.VMEM((SUBLANES, D), F32)],
        compiler_params=_params(VMEM_BIG),
    )(*dsecs, wg, x, dxo, mod4, mod4, g_pre3)


def _rcopy(src, dst, ssem, rsem, dev):
    return pltpu.make_async_remote_copy(src_ref=src, dst_ref=dst, send_sem=ssem, recv_sem=rsem,
                                        device_id=dev, device_id_type=MESH)


def _peers7(x, y, c):
    out = []
    for m in range(1, N_DEV):
        bx, by, bc = (m >> 2) & 1, (m >> 1) & 1, m & 1
        out.append(((1 - x) if bx else x, (1 - y) if by else y, (1 - c) if bc else c))
    return out


HBM = pl.BlockSpec(memory_space=pltpu.HBM)
SEM = pl.BlockSpec(memory_space=pltpu.SEMAPHORE)
SPLIT = pltpu.CompilerParams(has_side_effects=pltpu.SideEffectType.DATAFLOW_SIDE_EFFECTING)


def _hbm(a):
    return pltpu.with_memory_space_constraint(a, pltpu.HBM)


def _chips(x, y):
    return [(1 - x, y), (x, 1 - y), (1 - x, 1 - y)]


SIBLING_BARRIER_ID = 0


def xchg_start(name, bufs, n_copies, plan, sibling_only=False, after=()):
    n = len(bufs)
    after = list(after)

    def body(*refs):
        ssem, rsem, token = refs[n + len(after)], refs[n + len(after) + 1], refs[-1]
        x, y, c = _me()
        if sibling_only:
            barrier = pltpu.get_barrier_semaphore()
            pl.semaphore_signal(barrier, inc=1, device_id=(x, y, 1 - c), device_id_type=MESH)
            pl.semaphore_wait(barrier, 1)
        copies = plan(refs[0:n], x, y, c)
        assert len(copies) == n_copies
        for k, (src, dst, peer, _) in enumerate(copies):
            _rcopy(src, dst, ssem.at[k], rsem.at[k], peer).start()
        token[...] = jnp.zeros_like(token)

    params = dict(has_side_effects=pltpu.SideEffectType.DATAFLOW_SIDE_EFFECTING)
    if sibling_only:
        params["collective_id"] = SIBLING_BARRIER_ID
    outs = pl.pallas_call(
        body, name=name,
        in_specs=[HBM] * n + [ANY] * len(after),
        out_specs=[SEM, SEM] + [HBM] * n + [pl.BlockSpec(memory_space=pltpu.VMEM)],
        out_shape=([pltpu.SemaphoreType.DMA((n_copies,))] * 2 + [pltpu.HBM(b.shape, b.dtype) for b in bufs]
                   + [jax.ShapeDtypeStruct((SUBLANES, LANES), F32)]),
        input_output_aliases={a: 2 + a for a in range(n)},
        compiler_params=pltpu.CompilerParams(**params),
    )(*[_hbm(b) for b in bufs], *after)
    return outs[0], outs[1], list(outs[2:2 + n]), outs[-1]


def xchg_wait(name, bufs, ssem, rsem, n_copies, plan, after, sems=None):
    n = len(bufs)
    after = list(after)
    sems = tuple(range(n_copies)) if sems is None else tuple(sems)
    assert len(sems) == n_copies

    def body(*refs):
        ssem_ref, rsem_ref = refs[n], refs[n + 1]
        copies = plan(refs[0:n], *_me())
        assert len(copies) == n_copies
        for k, (src, _, peer, land) in zip(sems, copies):
            cp = _rcopy(src, land, ssem_ref.at[k], rsem_ref.at[k], peer)
            cp.wait_send()
            cp.wait_recv()

    outs = pl.pallas_call(
        body, name=name,
        in_specs=[HBM] * n + [SEM, SEM] + [ANY] * len(after), out_specs=[HBM] * n,
        out_shape=[pltpu.HBM(b.shape, b.dtype) for b in bufs],
        input_output_aliases={a: a for a in range(n)},
        compiler_params=SPLIT,
    )(*bufs, ssem, rsem, *after)
    return list(outs)


def _shard_half(buf, chip, half):
    if len(buf.shape) == 2:
        h, w = buf.shape[0] // 2, buf.shape[1] // N_CHIPS
        return buf.at[pl.ds(half * h, h), pl.ds(chip * w, w)]
    h = buf.shape[1] // 2
    return buf.at[chip, pl.ds(half * h, h)]


def plan_gather(refs, x, y, c):
    out = []
    for (px, py) in _chips(x, y):
        for buf in refs:
            own = _shard_half(buf, 2 * x + y, c)
            out.append((own, own, (px, py, c), _shard_half(buf, 2 * px + py, c)))
    return out


def plan_forward(refs, x, y, c):
    out = []
    for (px, py) in _chips(x, y):
        for buf in refs:
            landed = _shard_half(buf, 2 * px + py, c)
            out.append((landed, landed, (x, y, 1 - c), _shard_half(buf, 2 * px + py, 1 - c)))
    return out


def plan_sibling(refs, x, y, c):
    n = len(refs) // 2
    out = []
    for a in range(n):
        h = refs[a].shape[1] // 2
        out.append((refs[a].at[:, pl.ds((1 - c) * h, h)], refs[n + a], (x, y, 1 - c), refs[n + a]))
    return out


def plan_chip(refs, x, y, c):
    n = len(refs) // 2
    out = []
    for j, (px, py) in enumerate(_chips(x, y)):
        for a in range(n):
            out.append((refs[a].at[2 * px + py], refs[n + a].at[j], (px, py, c), refs[n + a].at[j]))
    return out


def plan_mod(refs, x, y, c):
    (mods,) = refs
    mine = mods.at[2 * x + y]
    return [(mine, mine, (px, py, c), mods.at[2 * px + py]) for (px, py) in _chips(x, y)]


def plan_pack(refs, x, y, c):
    (packs,) = refs
    mine = packs.at[4 * x + 2 * y + c]
    return [(mine, mine, peer, packs.at[4 * peer[0] + 2 * peer[1] + peer[2]]) for peer in _peers7(x, y, c)]


def plan_spread(layers, wp_layers):
    def plan(refs, x, y, c):
        gi, go, gp = refs
        hD, hR, hP = gi.shape[1] // 2, go.shape[1] // 2, gp.shape[2] // 2
        sib = (x, y, 1 - c)
        out = []
        for l in layers:
            mine = gi.at[l, pl.ds(c * hD, hD)]
            out.append((mine, mine, sib, gi.at[l, pl.ds((1 - c) * hD, hD)]))
            mine = go.at[l, pl.ds(c * hR, hR)]
            out.append((mine, mine, sib, go.at[l, pl.ds((1 - c) * hR, hR)]))
        for l in wp_layers:
            mine = gp.at[l, 2 * x + y, pl.ds(c * hP, hP)]
            for peer in _peers7(x, y, c):
                out.append((mine, mine, peer, gp.at[l, 2 * peer[0] + peer[1], pl.ds(peer[2] * hP, hP)]))
        return out

    return plan


def gather_small(c8, wc, token):
    def body(c_ref, wc_ref, token_ref, call, wcall, ssem, rsem, lsem):
        x, y, c = _me()
        myc = 2 * x + y
        me_lin = 4 * x + 2 * y + c
        me = (x, y, c)
        local = [pltpu.make_async_copy(c_ref, call.at[me_lin], lsem.at[0]),
                 pltpu.make_async_copy(wc_ref, wcall.at[myc], lsem.at[1])]
        for cp in local:
            cp.start()
        sends, recvs = [], []
        for m, peer in enumerate(_peers7(x, y, c)):
            plin = 4 * peer[0] + 2 * peer[1] + peer[2]
            sends.append(_rcopy(c_ref, call.at[me_lin], ssem.at[m], rsem.at[m], peer))
            recvs.append(_rcopy(call.at[plin], call.at[plin], ssem.at[m], rsem.at[m], me))
        for j, (px, py) in enumerate([(1 - x, y), (x, 1 - y), (1 - x, 1 - y)]):
            pc = 2 * px + py
            sends.append(_rcopy(wc_ref, wcall.at[myc], ssem.at[7 + j], rsem.at[7 + j], (px, py, c)))
            recvs.append(_rcopy(wcall.at[pc], wcall.at[pc], ssem.at[7 + j], rsem.at[7 + j], me))
        for cp in sends:
            cp.start()
        for cp in recvs:
            cp.wait_recv()
        for cp in sends:
            cp.wait_send()
        for cp in local:
            cp.wait()

    return pl.pallas_call(
        body, name="gather_small",
        in_specs=[ANY] * 3, out_specs=[ANY] * 2,
        out_shape=[jax.ShapeDtypeStruct((N_DEV, SUBLANES, LANES), F32),
                   jax.ShapeDtypeStruct((N_CHIPS, wc.shape[0], 3, LANES), F32)],
        scratch_shapes=[pltpu.SemaphoreType.DMA((10,)), pltpu.SemaphoreType.DMA((10,)), pltpu.SemaphoreType.DMA((2,))],
        compiler_params=_params(n_grid=0),
    )(c8, wc, token)


def spread_now(gi, go, gp, layers, wp_layers):
    plan = plan_spread(layers, wp_layers)
    n = 2 * len(layers) + 7 * len(wp_layers)

    def body(gi_in, go_in, gp_in, gi, go, gp, ssem, rsem):
        copies = plan((gi, go, gp), *_me())
        me = _me()
        sends = [_rcopy(src, dst, ssem.at[k], rsem.at[k], peer) for k, (src, dst, peer, _) in enumerate(copies)]
        for cp in sends:
            cp.start()
        for k, (_, _, _, land) in enumerate(copies):
            _rcopy(land, land, ssem.at[k], rsem.at[k], me).wait_recv()
        for cp in sends:
            cp.wait_send()

    return pl.pallas_call(
        body, name="spread_now",
        in_specs=[ANY] * 3, out_specs=[ANY] * 3,
        out_shape=[jax.ShapeDtypeStruct(a.shape, a.dtype) for a in (gi, go, gp)],
        input_output_aliases={0: 0, 1: 1, 2: 2},
        scratch_shapes=[pltpu.SemaphoreType.DMA((n,)), pltpu.SemaphoreType.DMA((n,))],
        compiler_params=_params(n_grid=0),
    )(gi, go, gp)


def add_sibling(cidx, mine, sib):
    def body(c_ref, *refs):
        for a in range(3):
            m, s, o = refs[a], refs[3 + a], refs[6 + a]
            o[...] = (m[...].astype(F32) + s[...].astype(F32)).astype(BF16)

    def mine_spec(a):
        h = a.shape[1] // 2
        return pl.BlockSpec((None, h, a.shape[2]), lambda j, c_ref: (j, c_ref[0], 0))

    def sib_spec(a):
        return pl.BlockSpec((None,) + a.shape[1:], lambda j, c_ref: (j, 0, 0))

    return pl.pallas_call(
        body, name="add_sibling",
        grid_spec=pltpu.PrefetchScalarGridSpec(
            num_scalar_prefetch=1, grid=(N_CHIPS,),
            in_specs=[mine_spec(a) for a in mine] + [sib_spec(a) for a in sib],
            out_specs=[sib_spec(a) for a in sib]),
        out_shape=[jax.ShapeDtypeStruct(a.shape, BF16) for a in sib],
        compiler_params=_params(VMEM_BIG),
    )(cidx, *mine, *sib)


def sum_chips(pos, own, rb, acc, l, shapes):
    nq = 4
    n_in = 6 + (3 if acc is not None else 0)

    def body(pos_ref, *refs):
        for a in range(3):
            m, b, o = refs[a], refs[3 + a], refs[n_in + a]
            s = m[...].astype(F32)
            for j in range(3):
                s = s + b[j].astype(F32)
            o[...] = s

    def own_spec(a):
        return pl.BlockSpec((None, a.shape[1] // nq, a.shape[2]), lambda q, p: (p[1], q, 0))

    def rb_spec(a):
        return pl.BlockSpec((3, a.shape[1] // nq, a.shape[2]), lambda q, p: (0, q, 0))

    hi, ho, hp = own[0].shape[1] // nq, own[1].shape[1] // nq, own[2].shape[1] // nq
    out_specs = [pl.BlockSpec((None, hi, shapes[0][2]), lambda q, p: (l, p[0] * nq + q, 0)),
                 pl.BlockSpec((None, ho, shapes[1][2]), lambda q, p: (l, p[0] * nq + q, 0)),
                 pl.BlockSpec((None, None, hp, LANES), lambda q, p: (l, p[1], p[0] * nq + q, 0))]
    in_specs = [own_spec(a) for a in own] + [rb_spec(a) for a in rb]
    args = list(own) + list(rb)
    aliases = {}
    if acc is not None:
        in_specs += [ANY] * 3
        args += list(acc)
        aliases = {7: 0, 8: 1, 9: 2}
    return pl.pallas_call(
        body, name="sum_chips",
        grid_spec=pltpu.PrefetchScalarGridSpec(num_scalar_prefetch=1, grid=(nq,), in_specs=in_specs, out_specs=out_specs),
        out_shape=[jax.ShapeDtypeStruct(s, F32) for s in shapes],
        input_output_aliases=aliases,
        compiler_params=_params(VMEM_BIG),
    )(pos, *args)


def pack_small(pos, per_layer, loss_blk):
    L = len(per_layer)
    D = per_layer[0][0].shape[1]

    def body(pos_ref, *refs):
        o = refs[-1]
        lb = refs[-2]
        o[...] = jnp.zeros_like(o)
        for l in range(L):
            dgpre, dgpost, dsh, dsc, dgt, dps, dwc = refs[7 * l:7 * l + 7]
            base = SUBLANES * l
            for r, src in enumerate((dgpre, dgpost, dsh, dsc, dgt)):
                o[pl.ds(base + r, 1), :] = src[...]
            o[pl.ds(base + 5, 1), 0:dps.shape[1]] = dps[...]
            for j in range(dwc.shape[0]):
                for k in range(3):
                    idx = 3 * j + k
                    o[pl.ds(base + 6 + idx // 8, 1), (idx % 8) * LANES:(idx % 8 + 1) * LANES] = dwc[j, pl.ds(k, 1), :]
        o[pl.ds(5, 1), 4 * LANES:5 * LANES] = lb[pl.ds(0, 1), :]

    flat = [a for layer in per_layer for a in layer] + [loss_blk]

    def whole(a):
        return pl.BlockSpec(a.shape, lambda i, p: (0,) * a.ndim)

    return pl.pallas_call(
        body, name="pack_small",
        grid_spec=pltpu.PrefetchScalarGridSpec(
            num_scalar_prefetch=1, grid=(1,), in_specs=[whole(a) for a in flat],
            out_specs=pl.BlockSpec((None, L * SUBLANES, D), lambda i, p: (p[2], 0, 0))),
        out_shape=jax.ShapeDtypeStruct((N_DEV, L * SUBLANES, D), F32),
        compiler_params=_params(),
    )(pos, *flat)


def small_update(pos, packs, params, moments_m, moments_v):
    n = len(params)
    L, D = params[1].shape
    PS = params[3].shape[1]

    def body(pos_ref, p_ref, *refs):
        ws, ms, vs = refs[0:n], refs[n:2 * n], refs[2 * n:3 * n]
        loss_ref = refs[3 * n]
        outs = [refs[3 * n + 1 + 4 * t:3 * n + 5 + 4 * t] for t in range(n)]
        summed = refs[-1]
        s = p_ref[0]
        for d in range(1, N_DEV):
            s = s + p_ref[d]
        summed[...] = s
        loss_ref[...] = summed[pl.ds(5, 1), 4 * LANES:5 * LANES]
        chip = pos_ref[1]

        def update(t, idx, g):
            d, mm, vv = _adamw_math(ws[t][idx], g, ms[t][idx], vs[t][idx])
            g_ref, d_ref, mo_ref, vo_ref = outs[t]
            g_ref[idx] = g
            d_ref[idx] = d
            mo_ref[idx] = mm
            vo_ref[idx] = vv

        for l in range(L):
            base = SUBLANES * l
            row = pl.ds(l, 1)
            for k in range(3):
                update(0, (row, slice(k * D, (k + 1) * D)), summed[pl.ds(base + 2 + k, 1), :])
            update(1, (row, slice(None)), summed[pl.ds(base, 1), :])
            update(2, (row, slice(None)), summed[pl.ds(base + 1, 1), :])
            update(3, (row, slice(None)), summed[pl.ds(base + 5, 1), 0:PS])
            for k in range(3):
                g = None
                for j in range(N_CHIPS):
                    idx = 3 * j + k
                    cand = summed[pl.ds(base + 6 + idx // 8, 1), (idx % 8) * LANES:(idx % 8 + 1) * LANES]
                    g = cand if g is None else jnp.where(chip == j, cand, g)
                update(4, (l, pl.ds(k, 1), slice(None)), g)

    def whole(a):
        return pl.BlockSpec(a.shape, lambda i, p: (0,) * a.ndim)

    ins = [packs] + list(params) + list(moments_m) + list(moments_v)
    out_shape = [jax.ShapeDtypeStruct((1, LANES), F32)]
    for w in params:
        out_shape += [jax.ShapeDtypeStruct(w.shape, F32)] * 4
    outs = pl.pallas_call(
        body, name="small_update",
        grid_spec=pltpu.PrefetchScalarGridSpec(
            num_scalar_prefetch=1, grid=(1,), in_specs=[whole(a) for a in ins],
            out_specs=[whole(a) for a in out_shape],
            scratch_shapes=[pltpu.VMEM(packs.shape[1:], F32)]),
        out_shape=out_shape,
        compiler_params=_params(),
    )(pos, *ins)
    return outs[0], [outs[1 + 4 * t:5 + 4 * t] for t in range(n)]


def _adamw_math(w, g, m, v):
    m = ADAM_B1 * m + (1.0 - ADAM_B1) * g
    v = ADAM_B2 * v + (1.0 - ADAM_B2) * (g * g)
    m_hat = m / (1.0 - ADAM_B1 ** ADAM_STEP)
    v_hat = v / (1.0 - ADAM_B2 ** ADAM_STEP)
    delta = -ADAM_LR * (m_hat / (jnp.sqrt(v_hat) + ADAM_EPS) + ADAM_WD * w)
    return delta, m, v


def adamw(w, g, m, v, block, name, first=0, count=None, acc=None):
    grid = tuple(s // b for s, b in zip(w.shape, block))
    if count is not None:
        grid = (count,) + grid[1:]

    def body(w_ref, g_ref, m_ref, v_ref, *rest):
        go_ref, d_ref, mo_ref, vo_ref = rest[-4:]
        gv = g_ref[...]
        d, mm, vv = _adamw_math(w_ref[...], gv, m_ref[...], v_ref[...])
        go_ref[...] = gv
        d_ref[...] = d
        mo_ref[...] = mm
        vo_ref[...] = vv

    spec = pl.BlockSpec(block, lambda i, *rest: (first + i,) + rest)
    shape = jax.ShapeDtypeStruct(w.shape, F32)
    extra = [] if acc is None else list(acc)
    return pl.pallas_call(
        body, name=name, grid=grid,
        in_specs=[spec] * 4 + [ANY] * len(extra), out_specs=[spec] * 4, out_shape=[shape] * 4,
        input_output_aliases={4 + a: a for a in range(len(extra))},
        compiler_params=_params(VMEM_BIG, n_grid=len(grid)),
    )(w, g, m, v, *extra)


def ada_finish(c_all, dmod, w, m, v):
    L, D, CW = w.shape
    hD = D // 2

    def body(c_ref, d_ref, w_ref, m_ref, v_ref, g_ref, dl_ref, mo_ref, vo_ref):
        cv = c_ref[...]
        z = jnp.zeros_like(cv)
        ca = jnp.concatenate([cv * jax.nn.sigmoid(cv), z], axis=0).astype(BF16)
        dm = jnp.concatenate([d_ref[0], jnp.zeros_like(d_ref[0])], axis=0).astype(BF16)
        g = lax.dot_general(ca, dm, TN, preferred_element_type=F32)
        g_ref[0] = g
        d, mm, vv = _adamw_math(w_ref[0], g, m_ref[0], v_ref[0])
        dl_ref[0] = d
        mo_ref[0] = mm
        vo_ref[0] = vv

    big = pl.BlockSpec((1, hD, CW), lambda l, h: (l, h, 0))
    shape = jax.ShapeDtypeStruct(w.shape, F32)
    return pl.pallas_call(
        body, name="ada_finish", grid=(L, 2),
        in_specs=[pl.BlockSpec((N_DEV, hD), lambda l, h: (0, h)), pl.BlockSpec((1, N_DEV, CW), lambda l, h: (l, 0, 0)),
                  big, big, big],
        out_specs=[big] * 4, out_shape=[shape] * 4,
        compiler_params=_params(VMEM_BIG, n_grid=2),
    )(c_all, dmod, w, m, v)


def kernel(x, c, w_ada, b_ada, g_pre, w_in, w_conv, w_pool, pool_scale, w_out, g_post, loss_target, m_w_ada, m_b_ada, m_g_pre, m_w_in, m_w_conv, m_w_pool, m_pool_scale, m_w_out, m_g_post, v_w_ada, v_b_ada, v_g_pre, v_w_in, v_w_conv, v_w_pool, v_pool_scale, v_w_out, v_g_post):
    L, D, CW = w_in.shape
    RO = w_out.shape[1]
    T = x.shape[1]
    ix, iy, ic = _me()
    chip = 2 * ix + iy
    me_lin = 4 * ix + 2 * iy + ic

    pos = jnp.stack([ic, chip, me_lin]).astype(jnp.int32)
    g_pre3, g_post3 = g_pre.reshape(L, 1, D), g_post.reshape(L, 1, D)
    pscale3 = pool_scale.reshape(L, 1, pool_scale.shape[1])
    n_s, n_c = 3, 9

    def gather(bufs, after):
        ss, rs, bufs, tok = xchg_start("gather_start", bufs, 3 * len(bufs), plan_gather, after=after)
        return (ss, rs, bufs), tok

    def arrive(flight, after):
        ss, rs, bufs = flight
        bufs = xchg_wait("gather_wait", bufs, ss, rs, 3 * len(bufs), plan_gather, after)
        fss, frs, bufs, tok = xchg_start("forward_start", bufs, 3 * len(bufs), plan_forward, sibling_only=True)
        return (fss, frs, bufs), tok

    def ready(flight, after):
        fss, frs, bufs = flight
        return xchg_wait("forward_wait", bufs, fss, frs, 3 * len(bufs), plan_forward, after)

    def arrive_part(flight, which, after):
        ss, rs, bufs = flight
        sems = tuple(range(which, 3 * len(bufs), len(bufs)))
        (buf,) = xchg_wait("gather_wait", [bufs[which]], ss, rs, 3, plan_gather, after, sems=sems)
        fss, frs, (buf,), tok = xchg_start("forward_start", [buf], 3, plan_forward, sibling_only=True)
        return (fss, frs, [buf]), tok

    c_all3, wconv_all = gather_small(c.reshape(SUBLANES, LANES), w_conv, pos)
    c_all = c_all3.reshape(N_DEV, D)
    gi0, go0 = cast_weights(pos, w_in, w_out, 0, c_all3)
    fly_in0, token = gather([gi0], [])
    b_my = lax.dynamic_slice_in_dim(b_ada, chip * CW, CW, axis=1)
    m_ss, m_rs, mods, token = xchg_start("mod_start", [mod_part(pos, c_all, w_ada, b_my, token)], 3, plan_mod)
    fly_out0, token = gather([go0], [token])
    flying_w = [None] * L
    for l in range(1, L):
        flying_w[l], token = gather(list(cast_weights(pos, w_in, w_out, l, token)), [])
    fwd_in0, token = arrive(fly_in0, [token])
    (mod_all,) = xchg_wait("mod_wait", mods, m_ss, m_rs, 3, plan_mod, [token])
    mod = lax.dynamic_index_in_dim(mod_all, me_lin, axis=2, keepdims=False)
    mod4 = jnp.transpose(mod, (1, 0, 2)).reshape(L, 3, 1, D)

    xs, projs, yas, yps, ys = [x.reshape(T, D)], [], [], [], []
    wg_in, wg_out = [], []
    fwd_in = fwd_in0
    for l in range(L):
        (gi,) = ready(fwd_in, [mod4 if l == 0 else xs[l]])
        proj = proj_fwd(xs[l], mod4, g_pre3, gi, l)
        ya = conv_fwd(proj, wconv_all, l)
        yp = pool_fwd(proj, w_pool, pscale3, l)
        after = [ya, yp]
        if l == 0:
            fwd_out, token = arrive(fly_out0, after)
        else:
            fwd_out, token = arrive_part(flying_w[l], 1, after)
        after = [token]
        if l + 1 < L:
            fwd_in, token = arrive_part(flying_w[l + 1], 0, after)
            after = [token]
        (go,) = ready(fwd_out, after)
        wg_in.append(gi)
        wg_out.append(go.reshape(N_CHIPS * RO, D))
        projs.append(proj)
        yas.append(ya)
        yps.append(yp)
        if l + 1 < L:
            xn, yv = out_fwd(ya, yp, wg_out[l], xs[l], mod4, g_post3, l, after[0])
            xs.append(xn)
        else:
            dx, yv, loss_blk = out_fwd_loss(ya, yp, wg_out[l], xs[l], mod4, g_post3, l, loss_target.reshape(T, D))
        ys.append(yv)

    shapes = (w_in.shape, w_out.shape, w_pool.shape)
    smalls = [None] * L
    acc, flying, sib, token = None, None, None, loss_blk

    def to_chips(sib, after):
        sl, s_ss, s_rs, s_bufs = sib
        s_bufs = xchg_wait("sibling_wait", s_bufs, s_ss, s_rs, n_s, plan_sibling, after)
        chip_parts = add_sibling(pos, s_bufs[0:3], s_bufs[3:6])
        lands = [lax.empty((3,) + a.shape[1:], a.dtype) for a in chip_parts]
        c_ss, c_rs, c_bufs, ctoken = xchg_start("chip_start", list(chip_parts) + lands, n_c, plan_chip)
        return (sl, c_ss, c_rs, c_bufs), ctoken

    def landed(flying, acc, after):
        fl, f_ss, f_rs, f_bufs = flying
        f_bufs = xchg_wait("chip_wait", f_bufs, f_ss, f_rs, n_c, plan_chip, after)
        return sum_chips(pos, f_bufs[0:3], f_bufs[3:6], acc, fl, shapes)

    for l in reversed(range(L)):
        dya, dyp, dwo_l, dgate, dgpost = out_bwd(dx, ys[l], yas[l], yps[l], wg_out[l], mod4, g_post3, l, token)
        token = dya
        if sib is not None:
            arrived = flying
            flying, token = to_chips(sib, [dya])
            if arrived is not None:
                acc = landed(arrived, acc, [token])
                token = acc[0]
        du_p, dg_p, dwp_l, dps = pool_bwd(projs[l], dyp, w_pool, pscale3, l, token)
        du_a, db_a, dc_a, dg_a, dwc = conv_bwd(projs[l], dya, wconv_all, l)
        dx, dwi_l, dshift, dscale, dgpre = in_bwd([du_a, db_a, dc_a, dg_a, du_p, dg_p], wg_in[l], xs[l], dx,
                                                  mod4, g_pre3, l)
        smalls[l] = (dgpre, dgpost, dshift, dscale, dgate, dps, dwc)
        parts = [dwi_l, dwo_l.reshape(N_CHIPS, RO, D), dwp_l]
        s_lands = [lax.empty((a.shape[0], a.shape[1] // 2) + a.shape[2:], a.dtype) for a in parts]
        s_ss, s_rs, s_bufs, token = xchg_start("sibling_start", parts + s_lands, n_s, plan_sibling, sibling_only=True)
        sib = (l, s_ss, s_rs, s_bufs)
    grad_x = dx.reshape(1, T, D)

    p_ss, p_rs, packs, ptoken = xchg_start("pack_start", [pack_small(pos, smalls, loss_blk)], N_DEV - 1, plan_pack)
    acc = landed(flying, acc, [ptoken, token])
    n_sp = (2 + N_DEV - 1) * (L - 1)
    spread = plan_spread(tuple(range(1, L)), tuple(range(1, L)))
    sp_ss, sp_rs, acc, sp_token = xchg_start("spread_start", list(acc), n_sp, spread)
    flying, token = to_chips(sib, [sp_token])
    (packs_all,) = xchg_wait("pack_wait", packs, p_ss, p_rs, N_DEV - 1, plan_pack, [token])
    dmod_all = packs_all.reshape(N_DEV, L, SUBLANES, D)[:, :, 2:5].reshape(N_DEV, L, 3 * D)
    dmod_my = jnp.transpose(lax.dynamic_slice_in_dim(dmod_all, chip * CW, CW, axis=2), (1, 0, 2))

    g_w_ada, d_w_ada, nm_w_ada, nv_w_ada = ada_finish(c_all, dmod_my, w_ada, m_w_ada, v_w_ada)
    loss_row, upd = small_update(pos, packs_all, [b_ada, g_pre, g_post, pool_scale, w_conv],
                                 [m_b_ada, m_g_pre, m_g_post, m_pool_scale, m_w_conv],
                                 [v_b_ada, v_g_pre, v_g_post, v_pool_scale, v_w_conv])
    loss = loss_row[0, 0]
    (g_b_ada, d_b_ada, nm_b_ada, nv_b_ada), (g_g_pre, d_g_pre, nm_g_pre, nv_g_pre) = upd[0], upd[1]
    (g_g_post, d_g_post, nm_g_post, nv_g_post), (g_pscale, d_pscale, nm_pscale, nv_pscale) = upd[2], upd[3]
    g_w_conv, d_w_conv, nm_w_conv, nv_w_conv = upd[4]

    done = [nv_w_ada, nv_w_conv]
    g_w_in, g_w_out, g_w_pool = xchg_wait("spread_wait", acc, sp_ss, sp_rs, n_sp, spread, done)
    in_blk, out_blk = (1, D // 2, CW), (1, RO, D)
    upd_in = adamw(w_in, g_w_in, m_w_in, v_w_in, in_blk, "adamw_w_in", 1, L - 1)
    upd_out = adamw(w_out, g_w_out, m_w_out, v_w_out, out_blk, "adamw_w_out", 1, L - 1)

    acc = landed(flying, (g_w_in, g_w_out, g_w_pool), [upd_in[3], upd_out[3]])
    r_w_in, r_w_out, r_w_pool = spread_now(*acc, (0,), (0,))
    g_w_in, d_w_in, nm_w_in, nv_w_in = adamw(w_in, r_w_in, m_w_in, v_w_in, in_blk, "adamw_w_in", 0, 1, upd_in)
    g_w_out, d_w_out, nm_w_out, nv_w_out = adamw(w_out, r_w_out, m_w_out, v_w_out, out_blk, "adamw_w_out", 0, 1, upd_out)
    pshape = (L, N_CHIPS * LANES, LANES)
    upd_pool = adamw(w_pool.reshape(pshape), r_w_pool.reshape(pshape), m_w_pool.reshape(pshape),
                     v_w_pool.reshape(pshape), (1,) + pshape[1:], "adamw_w_pool")
    g_w_pool, d_w_pool, nm_w_pool, nv_w_pool = [a.reshape(w_pool.shape) for a in upd_pool]

    return (loss, grad_x,
            g_w_ada, g_b_ada, g_g_pre, g_w_in, g_w_conv, g_w_pool, g_pscale, g_w_out, g_g_post,
            d_w_ada, d_b_ada, d_g_pre, d_w_in, d_w_conv, d_w_pool, d_pscale, d_w_out, d_g_post,
            nm_w_ada, nm_b_ada, nm_g_pre, nm_w_in, nm_w_conv, nm_w_pool, nm_pscale, nm_w_out, nm_g_post,
            nv_w_ada, nv_b_ada, nv_g_pre, nv_w_in, nv_w_conv, nv_w_pool, nv_pscale, nv_w_out, nv_g_post)
```

```python
import functools

import jax
import jax.numpy as jnp
from jax import lax
from jax.experimental import pallas as pl
from jax.experimental.pallas import tpu as pltpu

F32 = jnp.float32
BF16 = jnp.bfloat16
MESH = pl.DeviceIdType.MESH
ANY = pl.BlockSpec(memory_space=pl.ANY)

NORM_EPS = 1e-6
POOL_WINDOWS = (2, 4, 8, 16)
ADAM_LR = 0.001
ADAM_B1 = 0.9
ADAM_B2 = 0.999
ADAM_EPS = 1e-08
ADAM_WD = 0.01
ADAM_STEP = 10

N_CHIPS = 4
N_DEV = 8
LANES = 128
SUBLANES = 8
VMEM_BIG = 56 * 1024 * 1024
HIST = 16
R_CONV = 64
R_POOL = 128

NT = (((1,), (1,)), ((), ()))
TN = (((0,), (0,)), ((), ()))


def _params(vmem=None, n_grid=1):
    kw = {}
    if n_grid:
        kw["dimension_semantics"] = ("arbitrary",) * n_grid
    if vmem is not None:
        kw["vmem_limit_bytes"] = vmem
    return pltpu.CompilerParams(**kw)


def _colsum8(v):
    n, d = v.shape
    return v.reshape(n // SUBLANES, SUBLANES, d).sum(axis=0)


def _rms(v):
    return lax.rsqrt(jnp.mean(v * v, axis=-1, keepdims=True) + NORM_EPS)


def _sigmoid(v):
    return 0.5 * jnp.tanh(0.5 * v) + 0.5


def _shift_down(ext, k, rows):
    if k == 0:
        return ext[HIST:HIST + rows]
    return pltpu.roll(ext, k, 0)[HIST:HIST + rows]


def _shift_up(ext, k, rows):
    if k == 0:
        return ext[0:rows]
    return pltpu.roll(ext, ext.shape[0] - k, 0)[0:rows]


def _load_ext(ref, r0, h0, first, rows):
    hist = ref[pl.ds(h0, HIST), :].astype(F32)
    hist = jnp.where(first, 0.0, hist)
    cur = ref[pl.ds(r0, rows), :].astype(F32)
    return jnp.concatenate([hist, cur], axis=0)


def _me():
    return lax.axis_index("x"), lax.axis_index("y"), lax.axis_index("c")


def cast_weights(pos, w_in, w_out, l, after):
    _, D, CW = w_in.shape
    RO = w_out.shape[1]

    def body(pos_ref, wi, wo, after_ref, oi, oo):
        oi[...] = wi[...].astype(BF16)
        oo[...] = wo[...].astype(BF16)

    return pl.pallas_call(
        body, name="cast_w",
        grid_spec=pltpu.PrefetchScalarGridSpec(
            num_scalar_prefetch=1, grid=(2,),
            in_specs=[pl.BlockSpec((None, D // 2, CW), lambda h, p: (l, h, 0)),
                      pl.BlockSpec((None, RO // 2, D), lambda h, p: (l, h, 0)), ANY],
            out_specs=[pl.BlockSpec((D // 2, CW), lambda h, p: (h, p[1])),
                       pl.BlockSpec((None, RO // 2, D), lambda h, p: (p[1], h, 0))]),
        out_shape=[jax.ShapeDtypeStruct((D, N_CHIPS * CW), BF16), jax.ShapeDtypeStruct((N_CHIPS, RO, D), BF16)],
        compiler_params=_params(),
    )(pos, w_in, w_out, after)


def mod_part(pos, c_all, w_ada, b_my, after):
    L, D, CW = w_ada.shape

    def body(pos_ref, c_ref, w_ref, b_ref, after_ref, o_ref):
        cv = c_ref[...]
        ca = (cv * jax.nn.sigmoid(cv)).astype(BF16)
        o_ref[...] = jnp.dot(ca, w_ref[0].astype(BF16), preferred_element_type=F32) + b_ref[0]

    return pl.pallas_call(
        body, name="mod_part",
        grid_spec=pltpu.PrefetchScalarGridSpec(
            num_scalar_prefetch=1, grid=(L,),
            in_specs=[pl.BlockSpec((N_DEV, D), lambda l, p: (0, 0)),
                      pl.BlockSpec((1, D, CW), lambda l, p: (l, 0, 0)),
                      pl.BlockSpec((1, 1, CW), lambda l, p: (l, 0, 0)), ANY],
            out_specs=pl.BlockSpec((None, None, N_DEV, CW), lambda l, p: (p[1], l, 0, 0))),
        out_shape=jax.ShapeDtypeStruct((N_CHIPS, L, N_DEV, CW), F32),
        compiler_params=_params(VMEM_BIG),
    )(pos, c_all, w_ada, b_my.reshape(L, 1, CW), after)


def _mod_row(l, k, D):
    return pl.BlockSpec((None, None, 1, D), lambda *_: (l, k, 0, 0))


def _layer_row(l, D):
    return pl.BlockSpec((None, 1, D), lambda *_: (l, 0, 0))


def proj_fwd(x, mod4, g_pre3, wg, l):
    T, D = x.shape
    NC = wg.shape[1]
    NB = N_CHIPS
    CW = NC // NB
    tm = 512

    def body(x_ref, sh_ref, sc_ref, g_ref, w_ref, o_ref):
        xv = x_ref[...]
        h = (xv * _rms(xv) * g_ref[...]) * (1.0 + sc_ref[...]) + sh_ref[...]
        hb = h.astype(BF16)
        for j in range(NB):
            cols = slice(j * CW, (j + 1) * CW)
            o_ref[:, cols] = jnp.dot(hb, w_ref[:, cols], preferred_element_type=F32).astype(BF16)

    return pl.pallas_call(
        body, name="proj_fwd", grid=(T // tm,),
        in_specs=[pl.BlockSpec((tm, D), lambda i: (i, 0)), _mod_row(l, 0, D), _mod_row(l, 1, D), _layer_row(l, D),
                  pl.BlockSpec((D, NC), lambda i: (0, 0))],
        out_specs=pl.BlockSpec((tm, NC), lambda i: (i, 0)),
        out_shape=jax.ShapeDtypeStruct((T, NC), BF16),
        compiler_params=_params(VMEM_BIG),
    )(x, mod4, mod4, g_pre3, wg)


N_MIX = 4


def _conv_fwd_block(u_ref, b_ref, c_ref, g_ref, w_ref, o_ref):
    T = u_ref.shape[0]
    R = R_CONV
    w0 = w_ref[pl.ds(0, 1), :]
    w1 = w_ref[pl.ds(1, 1), :]
    w2 = w_ref[pl.ds(2, 1), :]

    def chunk(i, carry):
        r0 = pl.multiple_of(i * R, R)
        h0 = pl.multiple_of(jnp.maximum(r0 - HIST, 0), HIST)
        first = i == 0
        ca = _load_ext(c_ref, r0, h0, first, R) * _load_ext(u_ref, r0, h0, first, R)
        conv = w2 * ca[HIST:] + w1 * _shift_down(ca, 1, R) + w0 * _shift_down(ca, 2, R)
        g = g_ref[pl.ds(r0, R), :].astype(F32)
        b = b_ref[pl.ds(r0, R), :].astype(F32)
        o_ref[pl.ds(r0, R), :] = (b * conv * (g * _sigmoid(g))).astype(BF16)
        return carry

    lax.fori_loop(0, T // R, chunk, 0)


def _conv_idx(j):
    return jnp.minimum(j, N_MIX - 1)


def _pool_idx(j):
    return jnp.maximum(j - N_MIX, 0)


def _proj_col(T, off, idx):
    return pl.BlockSpec((T, LANES), lambda j: (0, idx(j) + off))


def _causal_window_sum(ext, w):
    s, k = ext, 1
    while k < w:
        s = s + pltpu.roll(s, k, 0)
        k *= 2
    return s


def _anticausal_window_sum(ext, w):
    s, k = ext, 1
    n = ext.shape[0]
    while k < w:
        s = s + pltpu.roll(s, n - k, 0)
        k *= 2
    return s


def _count(r0, rows, w):
    t = r0 + lax.broadcasted_iota(jnp.int32, (rows, LANES), 0)
    return jnp.minimum(t + 1, w).astype(F32)


def _pooled_loop(p_ref, pooled_s, w, T):
    R = R_POOL

    def chunk(i, carry):
        r0 = pl.multiple_of(i * R, R)
        h0 = pl.multiple_of(jnp.maximum(r0 - HIST, 0), HIST)
        ext = _load_ext(p_ref, r0, h0, i == 0, R)
        ws = _causal_window_sum(ext, w)[HIST:]
        pooled_s[pl.ds(r0, R), :] = (ws / _count(r0, R, w) - ext[HIST:]).astype(BF16)
        return carry

    lax.fori_loop(0, T // R, chunk, 0)


def _conv_w_spec(l):
    return pl.BlockSpec((None, None, 3, LANES), lambda j: (_conv_idx(j), l, 0, 0))


def _pool_w_spec(l):
    return pl.BlockSpec((None, None, LANES, LANES), lambda j: (l, _pool_idx(j), 0, 0))


def _pool_s_spec(l):
    return pl.BlockSpec((None, 1, LANES), lambda j: (l, 0, _pool_idx(j)))


def _pool_fwd_group(p_ref, g_ref, w_ref, s_ref, o_ref, pooled_s, mixed_s, w):
    T = p_ref.shape[0]
    R = R_POOL
    _pooled_loop(p_ref, pooled_s, w, T)
    mixed_s[...] = jnp.dot(pooled_s[...], w_ref[...].astype(BF16), preferred_element_type=F32)
    sc = s_ref[...]

    def chunk(i, carry):
        r0 = pl.multiple_of(i * R, R)
        g = g_ref[pl.ds(r0, R), :].astype(F32)
        o_ref[pl.ds(r0, R), :] = (mixed_s[pl.ds(r0, R), :] * sc * (g * _sigmoid(g))).astype(BF16)
        return carry

    lax.fori_loop(0, T // R, chunk, 0)


def mix_fwd(proj, wconv, wpool, pscale3, l):
    T = proj.shape[0]

    def body(u_ref, b_ref, c_ref, g_ref, p_ref, gp_ref, wc_ref, wp_ref, s_ref, ya_ref, yp_ref, pooled_s, mixed_s):
        j = pl.program_id(0)
        pl.when(j < N_MIX)(functools.partial(_conv_fwd_block, u_ref, b_ref, c_ref, g_ref, wc_ref, ya_ref))
        for k, w in enumerate(POOL_WINDOWS):
            pl.when(j == N_MIX + k)(functools.partial(_pool_fwd_group, p_ref, gp_ref, wp_ref, s_ref, yp_ref,
                                                      pooled_s, mixed_s, w))

    half = jax.ShapeDtypeStruct((T, N_MIX * LANES), BF16)
    return pl.pallas_call(
        body, name="mix_fwd", grid=(2 * N_MIX,),
        in_specs=[_proj_col(T, 0, _conv_idx), _proj_col(T, 4, _conv_idx), _proj_col(T, 8, _conv_idx),
                  _proj_col(T, 12, _conv_idx), _proj_col(T, 16, _pool_idx), _proj_col(T, 20, _pool_idx),
                  _conv_w_spec(l), _pool_w_spec(l), _pool_s_spec(l)],
        out_specs=[pl.BlockSpec((T, LANES), lambda j: (0, _conv_idx(j))),
                   pl.BlockSpec((T, LANES), lambda j: (0, _pool_idx(j)))],
        out_shape=[half, half],
        scratch_shapes=[pltpu.VMEM((T, LANES), BF16), pltpu.VMEM((T, LANES), F32)],
        compiler_params=_params(),
    )(proj, proj, proj, proj, proj, proj, wconv, wpool, pscale3)


def out_fwd(ya, yp, wo, x, mod4, g_post3, l, after):
    T, D = x.shape
    H = ya.shape[1]
    tm = 512

    def body(ya_ref, yp_ref, wo_ref, x_ref, gt_ref, g_ref, after_ref, xn_ref, y_ref):
        y = (jnp.dot(ya_ref[...], wo_ref[0:H, :], preferred_element_type=F32)
             + jnp.dot(yp_ref[...], wo_ref[H:2 * H, :], preferred_element_type=F32))
        xn_ref[...] = x_ref[...] + gt_ref[...] * (y * _rms(y) * g_ref[...])
        y_ref[...] = y.astype(BF16)

    tile = pl.BlockSpec((tm, D), lambda i: (i, 0))
    half = pl.BlockSpec((tm, H), lambda i: (i, 0))
    return pl.pallas_call(
        body, name="out_fwd", grid=(T // tm,),
        in_specs=[half, half, pl.BlockSpec((2 * H, D), lambda i: (0, 0)), tile, _mod_row(l, 2, D), _layer_row(l, D),
                  ANY],
        out_specs=[tile, tile],
        out_shape=[jax.ShapeDtypeStruct((T, D), F32), jax.ShapeDtypeStruct((T, D), BF16)],
        compiler_params=_params(VMEM_BIG),
    )(ya, yp, wo, x, mod4, g_post3, after)


def out_fwd_loss(ya, yp, wo, x, mod4, g_post3, l, target):
    T, D = x.shape
    H = ya.shape[1]
    tm = 512
    nt = T // tm

    def body(ya_ref, yp_ref, wo_ref, x_ref, gt_ref, g_ref, t_ref, dx_ref, y_ref, l_ref, acc):
        i = pl.program_id(0)

        @pl.when(i == 0)
        def _():
            acc[...] = jnp.zeros_like(acc)

        y = (jnp.dot(ya_ref[...], wo_ref[0:H, :], preferred_element_type=F32)
             + jnp.dot(yp_ref[...], wo_ref[H:2 * H, :], preferred_element_type=F32))
        y_ref[...] = y.astype(BF16)
        d = (x_ref[...] + gt_ref[...] * (y * _rms(y) * g_ref[...])) - t_ref[...]
        dx_ref[...] = d * (1.0 / D)
        acc[...] += _colsum8(d * d)

        @pl.when(i == nt - 1)
        def _():
            l_ref[...] = jnp.zeros_like(l_ref) + jnp.sum(acc[...]) * (0.5 / D)

    tile = pl.BlockSpec((tm, D), lambda i: (i, 0))
    half = pl.BlockSpec((tm, H), lambda i: (i, 0))
    return pl.pallas_call(
        body, name="out_fwd_loss", grid=(nt,),
        in_specs=[half, half, pl.BlockSpec((2 * H, D), lambda i: (0, 0)), tile, _mod_row(l, 2, D), _layer_row(l, D),
                  tile],
        out_specs=[tile, tile, pl.BlockSpec((SUBLANES, LANES), lambda i: (0, 0))],
        out_shape=[jax.ShapeDtypeStruct((T, D), F32), jax.ShapeDtypeStruct((T, D), BF16),
                   jax.ShapeDtypeStruct((SUBLANES, LANES), F32)],
        scratch_shapes=[pltpu.VMEM((SUBLANES, D), F32)],
        compiler_params=_params(VMEM_BIG),
    )(ya, yp, wo, x, mod4, g_post3, target)


def out_bwd(dx, y, ya, yp, wo, mod4, g_post3, l, after):
    T, D = dx.shape
    H = ya.shape[1]
    tm = 512
    nt = T // tm

    def body(dx_ref, y_ref, ya_ref, yp_ref, wo_ref, gt_ref, g_ref, after_ref,
             dya_ref, dyp_ref, dwo_ref, dgt_ref, dg_ref, acc_w, acc_p):
        i = pl.program_id(0)

        @pl.when(i == 0)
        def _():
            acc_w[...] = jnp.zeros_like(acc_w)
            acc_p[...] = jnp.zeros_like(acc_p)

        yv = y_ref[...].astype(F32)
        dxv = dx_ref[...]
        gg = gt_ref[...] * g_ref[...]
        r = _rms(yv)
        yn = yv * r
        p = dxv * yn
        acc_p[...] += _colsum8(p)
        dy = r * (dxv * gg - yn * jnp.mean(p * gg, axis=-1, keepdims=True))
        dyb = dy.astype(BF16)
        dyc = lax.dot_general(dyb, wo_ref[...], NT, preferred_element_type=F32)
        dya_ref[...] = dyc[:, 0:H].astype(BF16)
        dyp_ref[...] = dyc[:, H:2 * H].astype(BF16)
        acc_w[0:H, :] += lax.dot_general(ya_ref[...], dyb, TN, preferred_element_type=F32)
        acc_w[H:2 * H, :] += lax.dot_general(yp_ref[...], dyb, TN, preferred_element_type=F32)

        @pl.when(i == nt - 1)
        def _():
            dwo_ref[...] = acc_w[...].astype(BF16)
            sp = jnp.sum(acc_p[...], axis=0, keepdims=True)
            dgt_ref[...] = g_ref[...] * sp
            dg_ref[...] = gt_ref[...] * sp

    row = pl.BlockSpec((1, D), lambda i: (0, 0))
    tile = pl.BlockSpec((tm, D), lambda i: (i, 0))
    half = pl.BlockSpec((tm, H), lambda i: (i, 0))
    full = pl.BlockSpec((2 * H, D), lambda i: (0, 0))
    return pl.pallas_call(
        body, name="out_bwd", grid=(nt,),
        in_specs=[tile, tile, half, half, full, _mod_row(l, 2, D), _layer_row(l, D), ANY],
        out_specs=[half, half, full, row, row],
        out_shape=[jax.ShapeDtypeStruct((T, H), BF16), jax.ShapeDtypeStruct((T, H), BF16),
                   jax.ShapeDtypeStruct((2 * H, D), BF16),
                   jax.ShapeDtypeStruct((1, D), F32), jax.ShapeDtypeStruct((1, D), F32)],
        scratch_shapes=[pltpu.VMEM((2 * H, D), F32), pltpu.VMEM((SUBLANES, D), F32)],
        compiler_params=_params(VMEM_BIG),
    )(dx, y, ya, yp, wo, mod4, g_post3, after)


def _conv_bwd_block(u_ref, b_ref, c_ref, g_ref, dy_ref, w_ref, du_ref, db_ref, dc_ref, dg_ref, dw_ref):
    T = u_ref.shape[0]
    R = R_CONV
    nchunk = T // R
    w0 = w_ref[pl.ds(0, 1), :]
    w1 = w_ref[pl.ds(1, 1), :]
    w2 = w_ref[pl.ds(2, 1), :]

    def chunk(k, carry):
        head, a0, a1, a2 = carry
        i = nchunk - 1 - k
        r0 = pl.multiple_of(i * R, R)
        h0 = pl.multiple_of(jnp.maximum(r0 - HIST, 0), HIST)
        first = i == 0
        ue = _load_ext(u_ref, r0, h0, first, R)
        ce = _load_ext(c_ref, r0, h0, first, R)
        ca = ce * ue
        ca0 = ca[HIST:]
        ca1 = _shift_down(ca, 1, R)
        ca2 = _shift_down(ca, 2, R)
        conv = w2 * ca0 + w1 * ca1 + w0 * ca2
        g = g_ref[pl.ds(r0, R), :].astype(F32)
        b = b_ref[pl.ds(r0, R), :].astype(F32)
        dy = dy_ref[pl.ds(r0, R), :].astype(F32)
        sg = _sigmoid(g)
        sl = g * sg
        t = dy * conv
        db_ref[pl.ds(r0, R), :] = (t * sl).astype(BF16)
        dg_ref[pl.ds(r0, R), :] = (t * b * (sg * (1.0 + g * (1.0 - sg)))).astype(BF16)
        dconv = dy * b * sl
        a2 = a2 + _colsum8(dconv * ca0)
        a1 = a1 + _colsum8(dconv * ca1)
        a0 = a0 + _colsum8(dconv * ca2)
        e = jnp.concatenate([dconv, head], axis=0)
        dca = w2 * dconv + w1 * _shift_up(e, 1, R) + w0 * _shift_up(e, 2, R)
        du_ref[pl.ds(r0, R), :] = (dca * ce[HIST:]).astype(BF16)
        dc_ref[pl.ds(r0, R), :] = (dca * ue[HIST:]).astype(BF16)
        return dconv[0:SUBLANES], a0, a1, a2

    z = jnp.zeros((SUBLANES, LANES), F32)
    _, a0, a1, a2 = lax.fori_loop(0, nchunk, chunk, (z, z, z, z))
    dw_ref[pl.ds(0, 1), :] = jnp.sum(a0, axis=0, keepdims=True)
    dw_ref[pl.ds(1, 1), :] = jnp.sum(a1, axis=0, keepdims=True)
    dw_ref[pl.ds(2, 1), :] = jnp.sum(a2, axis=0, keepdims=True)


def _pool_bwd_group(p_ref, g_ref, dy_ref, w_ref, s_ref, du_ref, dg_ref, dw_ref, ds_ref,
                    pooled_s, mixed_s, dmix_s, dpool_s, w):
    T = p_ref.shape[0]
    R = R_POOL
    nchunk = T // R
    wb = w_ref[...].astype(BF16)
    _pooled_loop(p_ref, pooled_s, w, T)
    mixed_s[...] = jnp.dot(pooled_s[...], wb, preferred_element_type=F32)
    sc = s_ref[...]

    def gate_chunk(i, acc):
        r0 = pl.multiple_of(i * R, R)
        g = g_ref[pl.ds(r0, R), :].astype(F32)
        dy = dy_ref[pl.ds(r0, R), :].astype(F32)
        mixed = mixed_s[pl.ds(r0, R), :]
        sg = _sigmoid(g)
        dg_ref[pl.ds(r0, R), :] = (dy * mixed * sc * (sg * (1.0 + g * (1.0 - sg)))).astype(BF16)
        dms = dy * (g * sg)
        dmix_s[pl.ds(r0, R), :] = (dms * sc).astype(BF16)
        return acc + _colsum8(dms * mixed)

    acc = lax.fori_loop(0, nchunk, gate_chunk, jnp.zeros((SUBLANES, LANES), F32))
    ds_ref[...] = jnp.sum(acc, axis=0, keepdims=True)
    dpool_s[pl.ds(0, T), :] = lax.dot_general(dmix_s[...], wb, NT, preferred_element_type=F32)
    dpool_s[pl.ds(T, HIST), :] = jnp.zeros((HIST, LANES), F32)
    dw_ref[...] = lax.dot_general(pooled_s[...], dmix_s[...], TN, preferred_element_type=F32).astype(BF16)

    def back_chunk(i, carry):
        r0 = pl.multiple_of(i * R, R)
        dpe = dpool_s[pl.ds(r0, R + HIST), :]
        e = dpe / _count(r0, R + HIST, w)
        du_ref[pl.ds(r0, R), :] = (_anticausal_window_sum(e, w)[0:R] - dpe[0:R]).astype(BF16)
        return carry

    lax.fori_loop(0, nchunk, back_chunk, 0)


def mix_bwd(proj, dya, dyp, wconv, wpool, pscale3, l, after):
    T = proj.shape[0]

    def body(u_ref, b_ref, c_ref, g_ref, p_ref, gp_ref, dya_ref, dyp_ref, wc_ref, wp_ref, s_ref, after_ref,
             dua_ref, dba_ref, dca_ref, dga_ref, dup_ref, dgp_ref, dwc_ref, dwp_ref, ds_ref,
             pooled_s, mixed_s, dmix_s, dpool_s):
        j = pl.program_id(0)
        pl.when(j < N_MIX)(functools.partial(_conv_bwd_block, u_ref, b_ref, c_ref, g_ref, dya_ref, wc_ref,
                                             dua_ref, dba_ref, dca_ref, dga_ref, dwc_ref))
        for k, w in enumerate(POOL_WINDOWS):
            pl.when(j == N_MIX + k)(functools.partial(_pool_bwd_group, p_ref, gp_ref, dyp_ref, wp_ref, s_ref,
                                                      dup_ref, dgp_ref, dwp_ref, ds_ref,
                                                      pooled_s, mixed_s, dmix_s, dpool_s, w))

    sec = jax.ShapeDtypeStruct((T, N_MIX * LANES), BF16)
    conv_col = pl.BlockSpec((T, LANES), lambda j: (0, _conv_idx(j)))
    pool_col = pl.BlockSpec((T, LANES), lambda j: (0, _pool_idx(j)))
    return pl.pallas_call(
        body, name="mix_bwd", grid=(2 * N_MIX,),
        in_specs=[_proj_col(T, 0, _conv_idx), _proj_col(T, 4, _conv_idx), _proj_col(T, 8, _conv_idx),
                  _proj_col(T, 12, _conv_idx), _proj_col(T, 16, _pool_idx), _proj_col(T, 20, _pool_idx),
                  conv_col, pool_col, _conv_w_spec(l), _pool_w_spec(l), _pool_s_spec(l), ANY],
        out_specs=[conv_col, conv_col, conv_col, conv_col, pool_col, pool_col,
                   pl.BlockSpec((None, 3, LANES), lambda j: (_conv_idx(j), 0, 0)),
                   pl.BlockSpec((None, LANES, LANES), lambda j: (_pool_idx(j), 0, 0)),
                   pl.BlockSpec((1, LANES), lambda j: (0, _pool_idx(j)))],
        out_shape=[sec] * 6 + [jax.ShapeDtypeStruct((N_MIX, 3, LANES), F32),
                               jax.ShapeDtypeStruct((N_MIX, LANES, LANES), BF16),
                               jax.ShapeDtypeStruct((1, N_MIX * LANES), F32)],
        scratch_shapes=[pltpu.VMEM((T, LANES), BF16), pltpu.VMEM((T, LANES), F32),
                        pltpu.VMEM((T, LANES), BF16), pltpu.VMEM((T + HIST, LANES), F32)],
        compiler_params=_params(),
    )(proj, proj, proj, proj, proj, proj, dya, dyp, wconv, wpool, pscale3, after)


def in_bwd(dsecs, wg, x, dxo, mod4, g_pre3, l):
    T, D = x.shape
    NB = N_CHIPS
    CW = wg.shape[1] // NB
    SW = dsecs[0].shape[1]
    nsec = len(dsecs)
    PW = 256
    assert SW % PW == 0 and CW % PW == 0
    tm = 256
    nt = T // tm

    def body(*refs):
        d_refs = refs[0:nsec]
        w_ref, x_ref, dxo_ref, sh_ref, sc_ref, g_ref = refs[nsec:nsec + 6]
        dxi_ref, dw_ref, dsh_ref, dsc_ref, dg_ref = refs[nsec + 6:nsec + 11]
        acc_w, acc_sh, acc_q = refs[nsec + 11:]
        i = pl.program_id(0)

        @pl.when(i == 0)
        def _():
            acc_w[...] = jnp.zeros_like(acc_w)
            acc_sh[...] = jnp.zeros_like(acc_sh)
            acc_q[...] = jnp.zeros_like(acc_q)

        xv = x_ref[...]
        r = _rms(xv)
        xh = xv * r
        sg = g_ref[...] * (1.0 + sc_ref[...])
        hb = (xh * sg + sh_ref[...]).astype(BF16)
        dh = lax.dot_general(d_refs[0][...], w_ref[:, 0:SW], NT, preferred_element_type=F32)
        for s in range(1, nsec):
            dh = dh + lax.dot_general(d_refs[s][...], w_ref[:, s * SW:(s + 1) * SW], NT, preferred_element_type=F32)
        for p in range(nsec * SW // PW):
            col = p * PW
            s, so = col // SW, col % SW
            j, jo = col // CW, col % CW
            acc_w[j, :, jo:jo + PW] += lax.dot_general(hb, d_refs[s][:, so:so + PW], TN, preferred_element_type=F32)
        q = dh * xh
        acc_sh[...] += _colsum8(dh)
        acc_q[...] += _colsum8(q)
        dxi_ref[...] = dxo_ref[...] + r * (dh * sg - xh * jnp.mean(q * sg, axis=-1, keepdims=True))

        @pl.when(i == nt - 1)
        def _():
            dw_ref[...] = acc_w[...].astype(BF16)
            sq = jnp.sum(acc_q[...], axis=0, keepdims=True)
            dsh_ref[...] = jnp.sum(acc_sh[...], axis=0, keepdims=True)
            dsc_ref[...] = g_ref[...] * sq
            dg_ref[...] = (1.0 + sc_ref[...]) * sq

    row = pl.BlockSpec((1, D), lambda i: (0, 0))
    tile = pl.BlockSpec((tm, D), lambda i: (i, 0))
    sect = pl.BlockSpec((tm, SW), lambda i: (i, 0))
    rowshape = jax.ShapeDtypeStruct((1, D), F32)
    return pl.pallas_call(
        body, name="in_bwd", grid=(nt,),
        in_specs=[sect] * nsec + [pl.BlockSpec((D, NB * CW), lambda i: (0, 0)), tile, tile,
                                  _mod_row(l, 0, D), _mod_row(l, 1, D), _layer_row(l, D)],
        out_specs=[tile, pl.BlockSpec((NB, D, CW), lambda i: (0, 0, 0)), row, row, row],
        out_shape=[jax.ShapeDtypeStruct((T, D), F32), jax.ShapeDtypeStruct((NB, D, CW), BF16),
                   rowshape, rowshape, rowshape],
        scratch_shapes=[pltpu.VMEM((NB, D, CW), F32),
                        pltpu.VMEM((SUBLANES, D), F32), pltpu.VMEM((SUBLANES, D), F32)],
        compiler_params=_params(VMEM_BIG),
    )(*dsecs, wg, x, dxo, mod4, mod4, g_pre3)


def _rcopy(src, dst, ssem, rsem, dev):
    return pltpu.make_async_remote_copy(src_ref=src, dst_ref=dst, send_sem=ssem, recv_sem=rsem,
                                        device_id=dev, device_id_type=MESH)


def _peers7(x, y, c):
    out = []
    for m in range(1, N_DEV):
        bx, by, bc = (m >> 2) & 1, (m >> 1) & 1, m & 1
        out.append(((1 - x) if bx else x, (1 - y) if by else y, (1 - c) if bc else c))
    return out


HBM = pl.BlockSpec(memory_space=pltpu.HBM)
SEM = pl.BlockSpec(memory_space=pltpu.SEMAPHORE)
SPLIT = pltpu.CompilerParams(has_side_effects=pltpu.SideEffectType.DATAFLOW_SIDE_EFFECTING)


def _hbm(a):
    return pltpu.with_memory_space_constraint(a, pltpu.HBM)


def _chips(x, y):
    return [(1 - x, y), (x, 1 - y), (1 - x, 1 - y)]


SIBLING_BARRIER_ID = 0


def xchg_start(name, bufs, n_copies, plan, sibling_only=False, after=()):
    n = len(bufs)
    after = list(after)

    def body(*refs):
        ssem, rsem, token = refs[n + len(after)], refs[n + len(after) + 1], refs[-1]
        x, y, c = _me()
        if sibling_only:
            barrier = pltpu.get_barrier_semaphore()
            pl.semaphore_signal(barrier, inc=1, device_id=(x, y, 1 - c), device_id_type=MESH)
            pl.semaphore_wait(barrier, 1)
        copies = plan(refs[0:n], x, y, c)
        assert len(copies) == n_copies
        for k, (src, dst, peer, _) in enumerate(copies):
            _rcopy(src, dst, ssem.at[k], rsem.at[k], peer).start()
        token[...] = jnp.zeros_like(token)

    params = dict(has_side_effects=pltpu.SideEffectType.DATAFLOW_SIDE_EFFECTING)
    if sibling_only:
        params["collective_id"] = SIBLING_BARRIER_ID
    outs = pl.pallas_call(
        body, name=name,
        in_specs=[HBM] * n + [ANY] * len(after),
        out_specs=[SEM, SEM] + [HBM] * n + [pl.BlockSpec(memory_space=pltpu.VMEM)],
        out_shape=([pltpu.SemaphoreType.DMA((n_copies,))] * 2 + [pltpu.HBM(b.shape, b.dtype) for b in bufs]
                   + [jax.ShapeDtypeStruct((SUBLANES, LANES), F32)]),
        input_output_aliases={a: 2 + a for a in range(n)},
        compiler_params=pltpu.CompilerParams(**params),
    )(*[_hbm(b) for b in bufs], *after)
    return outs[0], outs[1], list(outs[2:2 + n]), outs[-1]


def xchg_wait(name, bufs, ssem, rsem, n_copies, plan, after, sems=None):
    n = len(bufs)
    after = list(after)
    sems = tuple(range(n_copies)) if sems is None else tuple(sems)
    assert len(sems) == n_copies

    def body(*refs):
        ssem_ref, rsem_ref = refs[n], refs[n + 1]
        copies = plan(refs[0:n], *_me())
        assert len(copies) == n_copies
        for k, (src, _, peer, land) in zip(sems, copies):
            cp = _rcopy(src, land, ssem_ref.at[k], rsem_ref.at[k], peer)
            cp.wait_send()
            cp.wait_recv()

    outs = pl.pallas_call(
        body, name=name,
        in_specs=[HBM] * n + [SEM, SEM] + [ANY] * len(after), out_specs=[HBM] * n,
        out_shape=[pltpu.HBM(b.shape, b.dtype) for b in bufs],
        input_output_aliases={a: a for a in range(n)},
        compiler_params=SPLIT,
    )(*bufs, ssem, rsem, *after)
    return list(outs)


def _shard_half(buf, chip, half):
    if len(buf.shape) == 2:
        h, w = buf.shape[0] // 2, buf.shape[1] // N_CHIPS
        return buf.at[pl.ds(half * h, h), pl.ds(chip * w, w)]
    h = buf.shape[1] // 2
    return buf.at[chip, pl.ds(half * h, h)]


def plan_gather(refs, x, y, c):
    out = []
    for (px, py) in _chips(x, y):
        for buf in refs:
            own = _shard_half(buf, 2 * x + y, c)
            out.append((own, own, (px, py, c), _shard_half(buf, 2 * px + py, c)))
    return out


def plan_forward(refs, x, y, c):
    out = []
    for (px, py) in _chips(x, y):
        for buf in refs:
            landed = _shard_half(buf, 2 * px + py, c)
            out.append((landed, landed, (x, y, 1 - c), _shard_half(buf, 2 * px + py, 1 - c)))
    return out


def plan_sibling(refs, x, y, c):
    n = len(refs) // 2
    out = []
    for a in range(n):
        h = refs[a].shape[1] // 2
        out.append((refs[a].at[:, pl.ds((1 - c) * h, h)], refs[n + a], (x, y, 1 - c), refs[n + a]))
    return out


def plan_chip(refs, x, y, c):
    n = len(refs) // 2
    out = []
    for j, (px, py) in enumerate(_chips(x, y)):
        for a in range(n):
            out.append((refs[a].at[2 * px + py], refs[n + a].at[j], (px, py, c), refs[n + a].at[j]))
    return out


def plan_mod(refs, x, y, c):
    (mods,) = refs
    mine = mods.at[2 * x + y]
    return [(mine, mine, (px, py, c), mods.at[2 * px + py]) for (px, py) in _chips(x, y)]


def plan_pack(refs, x, y, c):
    (packs,) = refs
    mine = packs.at[4 * x + 2 * y + c]
    return [(mine, mine, peer, packs.at[4 * peer[0] + 2 * peer[1] + peer[2]]) for peer in _peers7(x, y, c)]


def plan_spread(layers, wp_layers):
    def plan(refs, x, y, c):
        gi, go, gp = refs
        hD, hR, hP = gi.shape[1] // 2, go.shape[1] // 2, gp.shape[2] // 2
        sib = (x, y, 1 - c)
        out = []
        for l in layers:
            mine = gi.at[l, pl.ds(c * hD, hD)]
            out.append((mine, mine, sib, gi.at[l, pl.ds((1 - c) * hD, hD)]))
            mine = go.at[l, pl.ds(c * hR, hR)]
            out.append((mine, mine, sib, go.at[l, pl.ds((1 - c) * hR, hR)]))
        for l in wp_layers:
            mine = gp.at[l, 2 * x + y, pl.ds(c * hP, hP)]
            for peer in _peers7(x, y, c):
                out.append((mine, mine, peer, gp.at[l, 2 * peer[0] + peer[1], pl.ds(peer[2] * hP, hP)]))
        return out

    return plan


def gather_small(c8, wc, token):
    def body(c_ref, wc_ref, token_ref, call, wcall, ssem, rsem, lsem):
        x, y, c = _me()
        myc = 2 * x + y
        me_lin = 4 * x + 2 * y + c
        me = (x, y, c)
        local = [pltpu.make_async_copy(c_ref, call.at[me_lin], lsem.at[0]),
                 pltpu.make_async_copy(wc_ref, wcall.at[myc], lsem.at[1])]
        for cp in local:
            cp.start()
        sends, recvs = [], []
        for m, peer in enumerate(_peers7(x, y, c)):
            plin = 4 * peer[0] + 2 * peer[1] + peer[2]
            sends.append(_rcopy(c_ref, call.at[me_lin], ssem.at[m], rsem.at[m], peer))
            recvs.append(_rcopy(call.at[plin], call.at[plin], ssem.at[m], rsem.at[m], me))
        for j, (px, py) in enumerate([(1 - x, y), (x, 1 - y), (1 - x, 1 - y)]):
            pc = 2 * px + py
            sends.append(_rcopy(wc_ref, wcall.at[myc], ssem.at[7 + j], rsem.at[7 + j], (px, py, c)))
            recvs.append(_rcopy(wcall.at[pc], wcall.at[pc], ssem.at[7 + j], rsem.at[7 + j], me))
        for cp in sends:
            cp.start()
        for cp in recvs:
            cp.wait_recv()
        for cp in sends:
            cp.wait_send()
        for cp in local:
            cp.wait()

    return pl.pallas_call(
        body, name="gather_small",
        in_specs=[ANY] * 3, out_specs=[ANY] * 2,
        out_shape=[jax.ShapeDtypeStruct((N_DEV, SUBLANES, LANES), F32),
                   jax.ShapeDtypeStruct((N_CHIPS, wc.shape[0], 3, LANES), F32)],
        scratch_shapes=[pltpu.SemaphoreType.DMA((10,)), pltpu.SemaphoreType.DMA((10,)), pltpu.SemaphoreType.DMA((2,))],
        compiler_params=_params(n_grid=0),
    )(c8, wc, token)


def spread_now(gi, go, gp, layers, wp_layers):
    plan = plan_spread(layers, wp_layers)
    n = 2 * len(layers) + 7 * len(wp_layers)

    def body(gi_in, go_in, gp_in, gi, go, gp, ssem, rsem):
        copies = plan((gi, go, gp), *_me())
        me = _me()
        sends = [_rcopy(src, dst, ssem.at[k], rsem.at[k], peer) for k, (src, dst, peer, _) in enumerate(copies)]
        for cp in sends:
            cp.start()
        for k, (_, _, _, land) in enumerate(copies):
            _rcopy(land, land, ssem.at[k], rsem.at[k], me).wait_recv()
        for cp in sends:
            cp.wait_send()

    return pl.pallas_call(
        body, name="spread_now",
        in_specs=[ANY] * 3, out_specs=[ANY] * 3,
        out_shape=[jax.ShapeDtypeStruct(a.shape, a.dtype) for a in (gi, go, gp)],
        input_output_aliases={0: 0, 1: 1, 2: 2},
        scratch_shapes=[pltpu.SemaphoreType.DMA((n,)), pltpu.SemaphoreType.DMA((n,))],
        compiler_params=_params(n_grid=0),
    )(gi, go, gp)


def add_sibling(cidx, mine, sib):
    def body(c_ref, *refs):
        for a in range(3):
            m, s, o = refs[a], refs[3 + a], refs[6 + a]
            o[...] = (m[...].astype(F32) + s[...].astype(F32)).astype(BF16)

    def mine_spec(a):
        h = a.shape[1] // 2
        return pl.BlockSpec((None, h, a.shape[2]), lambda j, c_ref: (j, c_ref[0], 0))

    def sib_spec(a):
        return pl.BlockSpec((None,) + a.shape[1:], lambda j, c_ref: (j, 0, 0))

    return pl.pallas_call(
        body, name="add_sibling",
        grid_spec=pltpu.PrefetchScalarGridSpec(
            num_scalar_prefetch=1, grid=(N_CHIPS,),
            in_specs=[mine_spec(a) for a in mine] + [sib_spec(a) for a in sib],
            out_specs=[sib_spec(a) for a in sib]),
        out_shape=[jax.ShapeDtypeStruct(a.shape, BF16) for a in sib],
        compiler_params=_params(VMEM_BIG),
    )(cidx, *mine, *sib)


def sum_chips(pos, own, rb, acc, l, shapes):
    nq = 4
    n_in = 6 + (3 if acc is not None else 0)

    def body(pos_ref, *refs):
        for a in range(3):
            m, b, o = refs[a], refs[3 + a], refs[n_in + a]
            s = m[...].astype(F32)
            for j in range(3):
                s = s + b[j].astype(F32)
            o[...] = s

    def own_spec(a):
        return pl.BlockSpec((None, a.shape[1] // nq, a.shape[2]), lambda q, p: (p[1], q, 0))

    def rb_spec(a):
        return pl.BlockSpec((3, a.shape[1] // nq, a.shape[2]), lambda q, p: (0, q, 0))

    hi, ho, hp = own[0].shape[1] // nq, own[1].shape[1] // nq, own[2].shape[1] // nq
    out_specs = [pl.BlockSpec((None, hi, shapes[0][2]), lambda q, p: (l, p[0] * nq + q, 0)),
                 pl.BlockSpec((None, ho, shapes[1][2]), lambda q, p: (l, p[0] * nq + q, 0)),
                 pl.BlockSpec((None, None, hp, LANES), lambda q, p: (l, p[1], p[0] * nq + q, 0))]
    in_specs = [own_spec(a) for a in own] + [rb_spec(a) for a in rb]
    args = list(own) + list(rb)
    aliases = {}
    if acc is not None:
        in_specs += [ANY] * 3
        args += list(acc)
        aliases = {7: 0, 8: 1, 9: 2}
    return pl.pallas_call(
        body, name="sum_chips",
        grid_spec=pltpu.PrefetchScalarGridSpec(num_scalar_prefetch=1, grid=(nq,), in_specs=in_specs, out_specs=out_specs),
        out_shape=[jax.ShapeDtypeStruct(s, F32) for s in shapes],
        input_output_aliases=aliases,
        compiler_params=_params(VMEM_BIG),
    )(pos, *args)


def pack_small(pos, per_layer, loss_blk):
    L = len(per_layer)
    D = per_layer[0][0].shape[1]

    def body(pos_ref, *refs):
        o = refs[-1]
        lb = refs[-2]
        o[...] = jnp.zeros_like(o)
        for l in range(L):
            dgpre, dgpost, dsh, dsc, dgt, dps, dwc = refs[7 * l:7 * l + 7]
            base = SUBLANES * l
            for r, src in enumerate((dgpre, dgpost, dsh, dsc, dgt)):
                o[pl.ds(base + r, 1), :] = src[...]
            o[pl.ds(base + 5, 1), 0:dps.shape[1]] = dps[...]
            for j in range(dwc.shape[0]):
                for k in range(3):
                    idx = 3 * j + k
                    o[pl.ds(base + 6 + idx // 8, 1), (idx % 8) * LANES:(idx % 8 + 1) * LANES] = dwc[j, pl.ds(k, 1), :]
        o[pl.ds(5, 1), 4 * LANES:5 * LANES] = lb[pl.ds(0, 1), :]

    flat = [a for layer in per_layer for a in layer] + [loss_blk]

    def whole(a):
        return pl.BlockSpec(a.shape, lambda i, p: (0,) * a.ndim)

    return pl.pallas_call(
        body, name="pack_small",
        grid_spec=pltpu.PrefetchScalarGridSpec(
            num_scalar_prefetch=1, grid=(1,), in_specs=[whole(a) for a in flat],
            out_specs=pl.BlockSpec((None, L * SUBLANES, D), lambda i, p: (p[2], 0, 0))),
        out_shape=jax.ShapeDtypeStruct((N_DEV, L * SUBLANES, D), F32),
        compiler_params=_params(),
    )(pos, *flat)


def small_update(pos, packs, params, moments_m, moments_v):
    n = len(params)
    L, D = params[1].shape
    PS = params[3].shape[1]

    def body(pos_ref, p_ref, *refs):
        ws, ms, vs = refs[0:n], refs[n:2 * n], refs[2 * n:3 * n]
        loss_ref = refs[3 * n]
        outs = [refs[3 * n + 1 + 4 * t:3 * n + 5 + 4 * t] for t in range(n)]
        summed = refs[-1]
        s = p_ref[0]
        for d in range(1, N_DEV):
            s = s + p_ref[d]
        summed[...] = s
        loss_ref[...] = summed[pl.ds(5, 1), 4 * LANES:5 * LANES]
        chip = pos_ref[1]

        def update(t, idx, g):
            d, mm, vv = _adamw_math(ws[t][idx], g, ms[t][idx], vs[t][idx])
            g_ref, d_ref, mo_ref, vo_ref = outs[t]
            g_ref[idx] = g
            d_ref[idx] = d
            mo_ref[idx] = mm
            vo_ref[idx] = vv

        for l in range(L):
            base = SUBLANES * l
            row = pl.ds(l, 1)
            for k in range(3):
                update(0, (row, slice(k * D, (k + 1) * D)), summed[pl.ds(base + 2 + k, 1), :])
            update(1, (row, slice(None)), summed[pl.ds(base, 1), :])
            update(2, (row, slice(None)), summed[pl.ds(base + 1, 1), :])
            update(3, (row, slice(None)), summed[pl.ds(base + 5, 1), 0:PS])
            for k in range(3):
                g = None
                for j in range(N_CHIPS):
                    idx = 3 * j + k
                    cand = summed[pl.ds(base + 6 + idx // 8, 1), (idx % 8) * LANES:(idx % 8 + 1) * LANES]
                    g = cand if g is None else jnp.where(chip == j, cand, g)
                update(4, (l, pl.ds(k, 1), slice(None)), g)

    def whole(a):
        return pl.BlockSpec(a.shape, lambda i, p: (0,) * a.ndim)

    ins = [packs] + list(params) + list(moments_m) + list(moments_v)
    out_shape = [jax.ShapeDtypeStruct((1, LANES), F32)]
    for w in params:
        out_shape += [jax.ShapeDtypeStruct(w.shape, F32)] * 4
    outs = pl.pallas_call(
        body, name="small_update",
        grid_spec=pltpu.PrefetchScalarGridSpec(
            num_scalar_prefetch=1, grid=(1,), in_specs=[whole(a) for a in ins],
            out_specs=[whole(a) for a in out_shape],
            scratch_shapes=[pltpu.VMEM(packs.shape[1:], F32)]),
        out_shape=out_shape,
        compiler_params=_params(),
    )(pos, *ins)
    return outs[0], [outs[1 + 4 * t:5 + 4 * t] for t in range(n)]


def _adamw_math(w, g, m, v):
    m = ADAM_B1 * m + (1.0 - ADAM_B1) * g
    v = ADAM_B2 * v + (1.0 - ADAM_B2) * (g * g)
    m_hat = m / (1.0 - ADAM_B1 ** ADAM_STEP)
    v_hat = v / (1.0 - ADAM_B2 ** ADAM_STEP)
    delta = -ADAM_LR * (m_hat / (jnp.sqrt(v_hat) + ADAM_EPS) + ADAM_WD * w)
    return delta, m, v


def adamw(w, g, m, v, block, name, first=0, count=None, acc=None):
    grid = tuple(s // b for s, b in zip(w.shape, block))
    if count is not None:
        grid = (count,) + grid[1:]

    def body(w_ref, g_ref, m_ref, v_ref, *rest):
        go_ref, d_ref, mo_ref, vo_ref = rest[-4:]
        gv = g_ref[...]
        d, mm, vv = _adamw_math(w_ref[...], gv, m_ref[...], v_ref[...])
        go_ref[...] = gv
        d_ref[...] = d
        mo_ref[...] = mm
        vo_ref[...] = vv

    spec = pl.BlockSpec(block, lambda i, *rest: (first + i,) + rest)
    shape = jax.ShapeDtypeStruct(w.shape, F32)
    extra = [] if acc is None else list(acc)
    return pl.pallas_call(
        body, name=name, grid=grid,
        in_specs=[spec] * 4 + [ANY] * len(extra), out_specs=[spec] * 4, out_shape=[shape] * 4,
        input_output_aliases={4 + a: a for a in range(len(extra))},
        compiler_params=_params(VMEM_BIG, n_grid=len(grid)),
    )(w, g, m, v, *extra)


def ada_finish(c_all, dmod, w, m, v):
    L, D, CW = w.shape
    hD = D // 2

    def body(c_ref, d_ref, w_ref, m_ref, v_ref, g_ref, dl_ref, mo_ref, vo_ref):
        cv = c_ref[...]
        z = jnp.zeros_like(cv)
        ca = jnp.concatenate([cv * jax.nn.sigmoid(cv), z], axis=0).astype(BF16)
        dm = jnp.concatenate([d_ref[0], jnp.zeros_like(d_ref[0])], axis=0).astype(BF16)
        g = lax.dot_general(ca, dm, TN, preferred_element_type=F32)
        g_ref[0] = g
        d, mm, vv = _adamw_math(w_ref[0], g, m_ref[0], v_ref[0])
        dl_ref[0] = d
        mo_ref[0] = mm
        vo_ref[0] = vv

    big = pl.BlockSpec((1, hD, CW), lambda l, h: (l, h, 0))
    shape = jax.ShapeDtypeStruct(w.shape, F32)
    return pl.pallas_call(
        body, name="ada_finish", grid=(L, 2),
        in_specs=[pl.BlockSpec((N_DEV, hD), lambda l, h: (0, h)), pl.BlockSpec((1, N_DEV, CW), lambda l, h: (l, 0, 0)),
                  big, big, big],
        out_specs=[big] * 4, out_shape=[shape] * 4,
        compiler_params=_params(VMEM_BIG, n_grid=2),
    )(c_all, dmod, w, m, v)


def kernel(x, c, w_ada, b_ada, g_pre, w_in, w_conv, w_pool, pool_scale, w_out, g_post, loss_target, m_w_ada, m_b_ada, m_g_pre, m_w_in, m_w_conv, m_w_pool, m_pool_scale, m_w_out, m_g_post, v_w_ada, v_b_ada, v_g_pre, v_w_in, v_w_conv, v_w_pool, v_pool_scale, v_w_out, v_g_post):
    L, D, CW = w_in.shape
    RO = w_out.shape[1]
    T = x.shape[1]
    ix, iy, ic = _me()
    chip = 2 * ix + iy
    me_lin = 4 * ix + 2 * iy + ic

    pos = jnp.stack([ic, chip, me_lin]).astype(jnp.int32)
    g_pre3, g_post3 = g_pre.reshape(L, 1, D), g_post.reshape(L, 1, D)
    pscale3 = pool_scale.reshape(L, 1, pool_scale.shape[1])
    n_s, n_c = 3, 9

    def gather(bufs, after):
        ss, rs, bufs, tok = xchg_start("gather_start", bufs, 3 * len(bufs), plan_gather, after=after)
        return (ss, rs, bufs), tok

    def arrive(flight, after):
        ss, rs, bufs = flight
        bufs = xchg_wait("gather_wait", bufs, ss, rs, 3 * len(bufs), plan_gather, after)
        fss, frs, bufs, tok = xchg_start("forward_start", bufs, 3 * len(bufs), plan_forward, sibling_only=True)
        return (fss, frs, bufs), tok

    def ready(flight, after):
        fss, frs, bufs = flight
        return xchg_wait("forward_wait", bufs, fss, frs, 3 * len(bufs), plan_forward, after)

    def arrive_part(flight, which, after):
        ss, rs, bufs = flight
        sems = tuple(range(which, 3 * len(bufs), len(bufs)))
        (buf,) = xchg_wait("gather_wait", [bufs[which]], ss, rs, 3, plan_gather, after, sems=sems)
        fss, frs, (buf,), tok = xchg_start("forward_start", [buf], 3, plan_forward, sibling_only=True)
        return (fss, frs, [buf]), tok

    c_all3, wconv_all = gather_small(c.reshape(SUBLANES, LANES), w_conv, pos)
    c_all = c_all3.reshape(N_DEV, D)
    gi0, go0 = cast_weights(pos, w_in, w_out, 0, c_all3)
    fly_in0, token = gather([gi0], [])
    b_my = lax.dynamic_slice_in_dim(b_ada, chip * CW, CW, axis=1)
    m_ss, m_rs, mods, token = xchg_start("mod_start", [mod_part(pos, c_all, w_ada, b_my, token)], 3, plan_mod)
    fly_out0, token = gather([go0], [token])
    flying_w = [None] * L
    for l in range(1, L):
        flying_w[l], token = gather(list(cast_weights(pos, w_in, w_out, l, token)), [])
    fwd_in0, token = arrive(fly_in0, [token])
    (mod_all,) = xchg_wait("mod_wait", mods, m_ss, m_rs, 3, plan_mod, [token])
    mod = lax.dynamic_index_in_dim(mod_all, me_lin, axis=2, keepdims=False)
    mod4 = jnp.transpose(mod, (1, 0, 2)).reshape(L, 3, 1, D)

    xs, projs, yas, yps, ys = [x.reshape(T, D)], [], [], [], []
    wg_in, wg_out = [], []
    fwd_in = fwd_in0
    for l in range(L):
        (gi,) = ready(fwd_in, [mod4 if l == 0 else xs[l]])
        proj = proj_fwd(xs[l], mod4, g_pre3, gi, l)
        ya, yp = mix_fwd(proj, wconv_all, w_pool, pscale3, l)
        after = [ya, yp]
        if l == 0:
            fwd_out, token = arrive(fly_out0, after)
        else:
            fwd_out, token = arrive_part(flying_w[l], 1, after)
        after = [token]
        if l + 1 < L:
            fwd_in, token = arrive_part(flying_w[l + 1], 0, after)
            after = [token]
        (go,) = ready(fwd_out, after)
        wg_in.append(gi)
        wg_out.append(go.reshape(N_CHIPS * RO, D))
        projs.append(proj)
        yas.append(ya)
        yps.append(yp)
        if l + 1 < L:
            xn, yv = out_fwd(ya, yp, wg_out[l], xs[l], mod4, g_post3, l, after[0])
            xs.append(xn)
        else:
            dx, yv, loss_blk = out_fwd_loss(ya, yp, wg_out[l], xs[l], mod4, g_post3, l, loss_target.reshape(T, D))
        ys.append(yv)

    shapes = (w_in.shape, w_out.shape, w_pool.shape)
    smalls = [None] * L
    acc, flying, sib, token = None, None, None, loss_blk

    def to_chips(sib, after):
        sl, s_ss, s_rs, s_bufs = sib
        s_bufs = xchg_wait("sibling_wait", s_bufs, s_ss, s_rs, n_s, plan_sibling, after)
        chip_parts = add_sibling(pos, s_bufs[0:3], s_bufs[3:6])
        lands = [lax.empty((3,) + a.shape[1:], a.dtype) for a in chip_parts]
        c_ss, c_rs, c_bufs, ctoken = xchg_start("chip_start", list(chip_parts) + lands, n_c, plan_chip)
        return (sl, c_ss, c_rs, c_bufs), ctoken

    def landed(flying, acc, after):
        fl, f_ss, f_rs, f_bufs = flying
        f_bufs = xchg_wait("chip_wait", f_bufs, f_ss, f_rs, n_c, plan_chip, after)
        return sum_chips(pos, f_bufs[0:3], f_bufs[3:6], acc, fl, shapes)

    for l in reversed(range(L)):
        dya, dyp, dwo_l, dgate, dgpost = out_bwd(dx, ys[l], yas[l], yps[l], wg_out[l], mod4, g_post3, l, token)
        token = dya
        if sib is not None:
            arrived = flying
            flying, token = to_chips(sib, [dya])
            if arrived is not None:
                acc = landed(arrived, acc, [token])
                token = acc[0]
        du_a, db_a, dc_a, dg_a, du_p, dg_p, dwc, dwp_l, dps = mix_bwd(projs[l], dya, dyp, wconv_all, w_pool, pscale3, l,
                                                                        token)
        dx, dwi_l, dshift, dscale, dgpre = in_bwd([du_a, db_a, dc_a, dg_a, du_p, dg_p], wg_in[l], xs[l], dx,
                                                  mod4, g_pre3, l)
        smalls[l] = (dgpre, dgpost, dshift, dscale, dgate, dps, dwc)
        parts = [dwi_l, dwo_l.reshape(N_CHIPS, RO, D), dwp_l]
        s_lands = [lax.empty((a.shape[0], a.shape[1] // 2) + a.shape[2:], a.dtype) for a in parts]
        s_ss, s_rs, s_bufs, token = xchg_start("sibling_start", parts + s_lands, n_s, plan_sibling, sibling_only=True)
        sib = (l, s_ss, s_rs, s_bufs)
    grad_x = dx.reshape(1, T, D)

    p_ss, p_rs, packs, ptoken = xchg_start("pack_start", [pack_small(pos, smalls, loss_blk)], N_DEV - 1, plan_pack)
    acc = landed(flying, acc, [ptoken, token])
    n_sp = (2 + N_DEV - 1) * (L - 1)
    spread = plan_spread(tuple(range(1, L)), tuple(range(1, L)))
    sp_ss, sp_rs, acc, sp_token = xchg_start("spread_start", list(acc), n_sp, spread)
    flying, token = to_chips(sib, [sp_token])
    (packs_all,) = xchg_wait("pack_wait", packs, p_ss, p_rs, N_DEV - 1, plan_pack, [token])
    dmod_all = packs_all.reshape(N_DEV, L, SUBLANES, D)[:, :, 2:5].reshape(N_DEV, L, 3 * D)
    dmod_my = jnp.transpose(lax.dynamic_slice_in_dim(dmod_all, chip * CW, CW, axis=2), (1, 0, 2))

    g_w_ada, d_w_ada, nm_w_ada, nv_w_ada = ada_finish(c_all, dmod_my, w_ada, m_w_ada, v_w_ada)
    loss_row, upd = small_update(pos, packs_all, [b_ada, g_pre, g_post, pool_scale, w_conv],
                                 [m_b_ada, m_g_pre, m_g_post, m_pool_scale, m_w_conv],
                                 [v_b_ada, v_g_pre, v_g_post, v_pool_scale, v_w_conv])
    loss = loss_row[0, 0]
    (g_b_ada, d_b_ada, nm_b_ada, nv_b_ada), (g_g_pre, d_g_pre, nm_g_pre, nv_g_pre) = upd[0], upd[1]
    (g_g_post, d_g_post, nm_g_post, nv_g_post), (g_pscale, d_pscale, nm_pscale, nv_pscale) = upd[2], upd[3]
    g_w_conv, d_w_conv, nm_w_conv, nv_w_conv = upd[4]

    done = [nv_w_ada, nv_w_conv]
    g_w_in, g_w_out, g_w_pool = xchg_wait("spread_wait", acc, sp_ss, sp_rs, n_sp, spread, done)
    in_blk, out_blk = (1, D // 2, CW), (1, RO, D)
    upd_in = adamw(w_in, g_w_in, m_w_in, v_w_in, in_blk, "adamw_w_in", 1, L - 1)
    upd_out = adamw(w_out, g_w_out, m_w_out, v_w_out, out_blk, "adamw_w_out", 1, L - 1)

    acc = landed(flying, (g_w_in, g_w_out, g_w_pool), [upd_in[3], upd_out[3]])
    r_w_in, r_w_out, r_w_pool = spread_now(*acc, (0,), (0,))
    g_w_in, d_w_in, nm_w_in, nv_w_in = adamw(w_in, r_w_in, m_w_in, v_w_in, in_blk, "adamw_w_in", 0, 1, upd_in)
    g_w_out, d_w_out, nm_w_out, nv_w_out = adamw(w_out, r_w_out, m_w_out, v_w_out, out_blk, "adamw_w_out", 0, 1, upd_out)
    pshape = (L, N_CHIPS * LANES, LANES)
    upd_pool = adamw(w_pool.reshape(pshape), r_w_pool.reshape(pshape), m_w_pool.reshape(pshape),
                     v_w_pool.reshape(pshape), (1,) + pshape[1:], "adamw_w_pool")
    g_w_pool, d_w_pool, nm_w_pool, nv_w_pool = [a.reshape(w_pool.shape) for a in upd_pool]

    return (loss, grad_x,
            g_w_ada, g_b_ada, g_g_pre, g_w_in, g_w_conv, g_w_pool, g_pscale, g_w_out, g_g_post,
            d_w_ada, d_b_ada, d_g_pre, d_w_in, d_w_conv, d_w_pool, d_pscale, d_w_out, d_g_post,
            nm_w_ada, nm_b_ada, nm_g_pre, nm_w_in, nm_w_conv, nm_w_pool, nm_pscale, nm_w_out, nm_g_post,
            nv_w_ada, nv_b_ada, nv_g_pre, nv_w_in, nv_w_conv, nv_w_pool, nv_pscale, nv_w_out, nv_g_post)
```

```python
import functools

import jax
import jax.numpy as jnp
from jax import lax
from jax.experimental import pallas as pl
from jax.experimental.pallas import tpu as pltpu

F32 = jnp.float32
BF16 = jnp.bfloat16
MESH = pl.DeviceIdType.MESH
ANY = pl.BlockSpec(memory_space=pl.ANY)

NORM_EPS = 1e-6
POOL_WINDOWS = (2, 4, 8, 16)
ADAM_LR = 0.001
ADAM_B1 = 0.9
ADAM_B2 = 0.999
ADAM_EPS = 1e-08
ADAM_WD = 0.01
ADAM_STEP = 10

N_CHIPS = 4
N_DEV = 8
LANES = 128
SUBLANES = 8
VMEM_BIG = 56 * 1024 * 1024
HIST = 16
R_CONV = 64
R_POOL = 128

NT = (((1,), (1,)), ((), ()))
TN = (((0,), (0,)), ((), ()))


def _params(vmem=None, n_grid=1):
    kw = {}
    if n_grid:
        kw["dimension_semantics"] = ("arbitrary",) * n_grid
    if vmem is not None:
        kw["vmem_limit_bytes"] = vmem
    return pltpu.CompilerParams(**kw)


def _colsum8(v):
    n, d = v.shape
    return v.reshape(n // SUBLANES, SUBLANES, d).sum(axis=0)


def _rms(v):
    return lax.rsqrt(jnp.mean(v * v, axis=-1, keepdims=True) + NORM_EPS)


def _sigmoid(v):
    return 0.5 * jnp.tanh(0.5 * v) + 0.5


def _shift_down(ext, k, rows):
    if k == 0:
        return ext[HIST:HIST + rows]
    return pltpu.roll(ext, k, 0)[HIST:HIST + rows]


def _shift_up(ext, k, rows):
    if k == 0:
        return ext[0:rows]
    return pltpu.roll(ext, ext.shape[0] - k, 0)[0:rows]


def _load_ext(ref, r0, h0, first, rows):
    hist = ref[pl.ds(h0, HIST), :].astype(F32)
    hist = jnp.where(first, 0.0, hist)
    cur = ref[pl.ds(r0, rows), :].astype(F32)
    return jnp.concatenate([hist, cur], axis=0)


def _me():
    return lax.axis_index("x"), lax.axis_index("y"), lax.axis_index("c")


def cast_weights(pos, w_in, w_out, l, after):
    _, D, CW = w_in.shape
    RO = w_out.shape[1]

    def body(pos_ref, wi, wo, after_ref, oi, oo):
        oi[...] = wi[...].astype(BF16)
        oo[...] = wo[...].astype(BF16)

    return pl.pallas_call(
        body, name="cast_w",
        grid_spec=pltpu.PrefetchScalarGridSpec(
            num_scalar_prefetch=1, grid=(2,),
            in_specs=[pl.BlockSpec((None, D // 2, CW), lambda h, p: (l, h, 0)),
                      pl.BlockSpec((None, RO // 2, D), lambda h, p: (l, h, 0)), ANY],
            out_specs=[pl.BlockSpec((D // 2, CW), lambda h, p: (h, p[1])),
                       pl.BlockSpec((None, RO // 2, D), lambda h, p: (p[1], h, 0))]),
        out_shape=[jax.ShapeDtypeStruct((D, N_CHIPS * CW), BF16), jax.ShapeDtypeStruct((N_CHIPS, RO, D), BF16)],
        compiler_params=_params(),
    )(pos, w_in, w_out, after)


def mod_part(pos, c_all, w_ada, b_my, after):
    L, D, CW = w_ada.shape

    def body(pos_ref, c_ref, w_ref, b_ref, after_ref, o_ref):
        cv = c_ref[...]
        ca = (cv * jax.nn.sigmoid(cv)).astype(BF16)
        o_ref[...] = jnp.dot(ca, w_ref[0].astype(BF16), preferred_element_type=F32) + b_ref[0]

    return pl.pallas_call(
        body, name="mod_part",
        grid_spec=pltpu.PrefetchScalarGridSpec(
            num_scalar_prefetch=1, grid=(L,),
            in_specs=[pl.BlockSpec((N_DEV, D), lambda l, p: (0, 0)),
                      pl.BlockSpec((1, D, CW), lambda l, p: (l, 0, 0)),
                      pl.BlockSpec((1, 1, CW), lambda l, p: (l, 0, 0)), ANY],
            out_specs=pl.BlockSpec((None, None, N_DEV, CW), lambda l, p: (p[1], l, 0, 0))),
        out_shape=jax.ShapeDtypeStruct((N_CHIPS, L, N_DEV, CW), F32),
        compiler_params=_params(VMEM_BIG),
    )(pos, c_all, w_ada, b_my.reshape(L, 1, CW), after)


def _mod_row(l, k, D):
    return pl.BlockSpec((None, None, 1, D), lambda *_: (l, k, 0, 0))


def _layer_row(l, D):
    return pl.BlockSpec((None, 1, D), lambda *_: (l, 0, 0))


def proj_fwd(x, mod4, g_pre3, wg, l):
    T, D = x.shape
    NC = wg.shape[1]
    NB = N_CHIPS
    CW = NC // NB
    tm = 512

    def body(x_ref, sh_ref, sc_ref, g_ref, w_ref, o_ref):
        xv = x_ref[...]
        h = (xv * _rms(xv) * g_ref[...]) * (1.0 + sc_ref[...]) + sh_ref[...]
        hb = h.astype(BF16)
        for j in range(NB):
            cols = slice(j * CW, (j + 1) * CW)
            o_ref[:, cols] = jnp.dot(hb, w_ref[:, cols], preferred_element_type=F32).astype(BF16)

    return pl.pallas_call(
        body, name="proj_fwd", grid=(T // tm,),
        in_specs=[pl.BlockSpec((tm, D), lambda i: (i, 0)), _mod_row(l, 0, D), _mod_row(l, 1, D), _layer_row(l, D),
                  pl.BlockSpec((D, NC), lambda i: (0, 0))],
        out_specs=pl.BlockSpec((tm, NC), lambda i: (i, 0)),
        out_shape=jax.ShapeDtypeStruct((T, NC), BF16),
        compiler_params=_params(VMEM_BIG),
    )(x, mod4, mod4, g_pre3, wg)


N_MIX = 4


def _conv_fwd_block(u_ref, b_ref, c_ref, g_ref, w_ref, o_ref):
    T = u_ref.shape[0]
    R = R_CONV
    w0 = w_ref[pl.ds(0, 1), :]
    w1 = w_ref[pl.ds(1, 1), :]
    w2 = w_ref[pl.ds(2, 1), :]

    def chunk(i, carry):
        r0 = pl.multiple_of(i * R, R)
        h0 = pl.multiple_of(jnp.maximum(r0 - HIST, 0), HIST)
        first = i == 0
        ca = _load_ext(c_ref, r0, h0, first, R) * _load_ext(u_ref, r0, h0, first, R)
        conv = w2 * ca[HIST:] + w1 * _shift_down(ca, 1, R) + w0 * _shift_down(ca, 2, R)
        g = g_ref[pl.ds(r0, R), :].astype(F32)
        b = b_ref[pl.ds(r0, R), :].astype(F32)
        o_ref[pl.ds(r0, R), :] = (b * conv * (g * _sigmoid(g))).astype(BF16)
        return carry

    lax.fori_loop(0, T // R, chunk, 0)


def _conv_idx(j):
    return jnp.minimum(j, N_MIX - 1)


def _pool_idx(j):
    return jnp.maximum(j - N_MIX, 0)


def _proj_col(T, off, idx):
    return pl.BlockSpec((T, LANES), lambda j: (0, idx(j) + off))


def _causal_window_sum(ext, w):
    s, k = ext, 1
    while k < w:
        s = s + pltpu.roll(s, k, 0)
        k *= 2
    return s


def _anticausal_window_sum(ext, w):
    s, k = ext, 1
    n = ext.shape[0]
    while k < w:
        s = s + pltpu.roll(s, n - k, 0)
        k *= 2
    return s


def _count(r0, rows, w):
    t = r0 + lax.broadcasted_iota(jnp.int32, (rows, LANES), 0)
    return jnp.minimum(t + 1, w).astype(F32)


def _pooled_loop(p_ref, pooled_s, w, T):
    R = R_POOL

    def chunk(i, carry):
        r0 = pl.multiple_of(i * R, R)
        h0 = pl.multiple_of(jnp.maximum(r0 - HIST, 0), HIST)
        ext = _load_ext(p_ref, r0, h0, i == 0, R)
        ws = _causal_window_sum(ext, w)[HIST:]
        pooled_s[pl.ds(r0, R), :] = (ws / _count(r0, R, w) - ext[HIST:]).astype(BF16)
        return carry

    lax.fori_loop(0, T // R, chunk, 0)


def _conv_w_spec(l):
    return pl.BlockSpec((None, None, 3, LANES), lambda j: (_conv_idx(j), l, 0, 0))


def _pool_w_spec(l):
    return pl.BlockSpec((None, None, LANES, LANES), lambda j: (l, _pool_idx(j), 0, 0))


def _pool_s_spec(l):
    return pl.BlockSpec((None, 1, LANES), lambda j: (l, 0, _pool_idx(j)))


def _pool_fwd_group(p_ref, g_ref, w_ref, s_ref, o_ref, pooled_s, mixed_s, w):
    T = p_ref.shape[0]
    R = R_POOL
    _pooled_loop(p_ref, pooled_s, w, T)
    mixed_s[...] = jnp.dot(pooled_s[...], w_ref[...].astype(BF16), preferred_element_type=F32)
    sc = s_ref[...]

    def chunk(i, carry):
        r0 = pl.multiple_of(i * R, R)
        g = g_ref[pl.ds(r0, R), :].astype(F32)
        o_ref[pl.ds(r0, R), :] = (mixed_s[pl.ds(r0, R), :] * sc * (g * _sigmoid(g))).astype(BF16)
        return carry

    lax.fori_loop(0, T // R, chunk, 0)


def mix_fwd(proj, wconv, wpool, pscale3, l):
    T = proj.shape[0]

    def body(u_ref, b_ref, c_ref, g_ref, p_ref, gp_ref, wc_ref, wp_ref, s_ref, ya_ref, yp_ref, pooled_s, mixed_s):
        j = pl.program_id(0)
        pl.when(j < N_MIX)(functools.partial(_conv_fwd_block, u_ref, b_ref, c_ref, g_ref, wc_ref, ya_ref))
        for k, w in enumerate(POOL_WINDOWS):
            pl.when(j == N_MIX + k)(functools.partial(_pool_fwd_group, p_ref, gp_ref, wp_ref, s_ref, yp_ref,
                                                      pooled_s, mixed_s, w))

    half = jax.ShapeDtypeStruct((T, N_MIX * LANES), BF16)
    return pl.pallas_call(
        body, name="mix_fwd", grid=(2 * N_MIX,),
        in_specs=[_proj_col(T, 0, _conv_idx), _proj_col(T, 4, _conv_idx), _proj_col(T, 8, _conv_idx),
                  _proj_col(T, 12, _conv_idx), _proj_col(T, 16, _pool_idx), _proj_col(T, 20, _pool_idx),
                  _conv_w_spec(l), _pool_w_spec(l), _pool_s_spec(l)],
        out_specs=[pl.BlockSpec((T, LANES), lambda j: (0, _conv_idx(j))),
                   pl.BlockSpec((T, LANES), lambda j: (0, _pool_idx(j)))],
        out_shape=[half, half],
        scratch_shapes=[pltpu.VMEM((T, LANES), BF16), pltpu.VMEM((T, LANES), F32)],
        compiler_params=_params(),
    )(proj, proj, proj, proj, proj, proj, wconv, wpool, pscale3)


def out_fwd(ya, yp, wo, x, mod4, g_post3, l, after):
    T, D = x.shape
    H = ya.shape[1]
    tm = 512

    def body(ya_ref, yp_ref, wo_ref, x_ref, gt_ref, g_ref, after_ref, xn_ref, y_ref):
        y = (jnp.dot(ya_ref[...], wo_ref[0:H, :], preferred_element_type=F32)
             + jnp.dot(yp_ref[...], wo_ref[H:2 * H, :], preferred_element_type=F32))
        xn_ref[...] = x_ref[...] + gt_ref[...] * (y * _rms(y) * g_ref[...])
        y_ref[...] = y.astype(BF16)

    tile = pl.BlockSpec((tm, D), lambda i: (i, 0))
    half = pl.BlockSpec((tm, H), lambda i: (i, 0))
    return pl.pallas_call(
        body, name="out_fwd", grid=(T // tm,),
        in_specs=[half, half, pl.BlockSpec((2 * H, D), lambda i: (0, 0)), tile, _mod_row(l, 2, D), _layer_row(l, D),
                  ANY],
        out_specs=[tile, tile],
        out_shape=[jax.ShapeDtypeStruct((T, D), F32), jax.ShapeDtypeStruct((T, D), BF16)],
        compiler_params=_params(VMEM_BIG),
    )(ya, yp, wo, x, mod4, g_post3, after)


def out_fwd_loss(ya, yp, wo, x, mod4, g_post3, l, target):
    T, D = x.shape
    H = ya.shape[1]
    tm = 512
    nt = T // tm

    def body(ya_ref, yp_ref, wo_ref, x_ref, gt_ref, g_ref, t_ref, dx_ref, y_ref, l_ref, acc):
        i = pl.program_id(0)

        @pl.when(i == 0)
        def _():
            acc[...] = jnp.zeros_like(acc)

        y = (jnp.dot(ya_ref[...], wo_ref[0:H, :], preferred_element_type=F32)
             + jnp.dot(yp_ref[...], wo_ref[H:2 * H, :], preferred_element_type=F32))
        y_ref[...] = y.astype(BF16)
        d = (x_ref[...] + gt_ref[...] * (y * _rms(y) * g_ref[...])) - t_ref[...]
        dx_ref[...] = d * (1.0 / D)
        acc[...] += _colsum8(d * d)

        @pl.when(i == nt - 1)
        def _():
            l_ref[...] = jnp.zeros_like(l_ref) + jnp.sum(acc[...]) * (0.5 / D)

    tile = pl.BlockSpec((tm, D), lambda i: (i, 0))
    half = pl.BlockSpec((tm, H), lambda i: (i, 0))
    return pl.pallas_call(
        body, name="out_fwd_loss", grid=(nt,),
        in_specs=[half, half, pl.BlockSpec((2 * H, D), lambda i: (0, 0)), tile, _mod_row(l, 2, D), _layer_row(l, D),
                  tile],
        out_specs=[tile, tile, pl.BlockSpec((SUBLANES, LANES), lambda i: (0, 0))],
        out_shape=[jax.ShapeDtypeStruct((T, D), F32), jax.ShapeDtypeStruct((T, D), BF16),
                   jax.ShapeDtypeStruct((SUBLANES, LANES), F32)],
        scratch_shapes=[pltpu.VMEM((SUBLANES, D), F32)],
        compiler_params=_params(VMEM_BIG),
    )(ya, yp, wo, x, mod4, g_post3, target)


def out_bwd(dx, y, ya, yp, wo, mod4, g_post3, l, after):
    T, D = dx.shape
    H = ya.shape[1]
    tm = 512
    nt = T // tm

    def body(dx_ref, y_ref, ya_ref, yp_ref, wo_ref, gt_ref, g_ref, after_ref,
             dya_ref, dyp_ref, dwo_ref, dgt_ref, dg_ref, acc_w, acc_p):
        i = pl.program_id(0)

        @pl.when(i == 0)
        def _():
            acc_w[...] = jnp.zeros_like(acc_w)
            acc_p[...] = jnp.zeros_like(acc_p)

        yv = y_ref[...].astype(F32)
        dxv = dx_ref[...]
        gg = gt_ref[...] * g_ref[...]
        r = _rms(yv)
        yn = yv * r
        p = dxv * yn
        acc_p[...] += _colsum8(p)
        dy = r * (dxv * gg - yn * jnp.mean(p * gg, axis=-1, keepdims=True))
        dyb = dy.astype(BF16)
        dyc = lax.dot_general(dyb, wo_ref[...], NT, preferred_element_type=F32)
        dya_ref[...] = dyc[:, 0:H].astype(BF16)
        dyp_ref[...] = dyc[:, H:2 * H].astype(BF16)
        acc_w[0:H, :] += lax.dot_general(ya_ref[...], dyb, TN, preferred_element_type=F32)
        acc_w[H:2 * H, :] += lax.dot_general(yp_ref[...], dyb, TN, preferred_element_type=F32)

        @pl.when(i == nt - 1)
        def _():
            dwo_ref[...] = acc_w[...].astype(BF16)
            sp = jnp.sum(acc_p[...], axis=0, keepdims=True)
            dgt_ref[...] = g_ref[...] * sp
            dg_ref[...] = gt_ref[...] * sp

    row = pl.BlockSpec((1, D), lambda i: (0, 0))
    tile = pl.BlockSpec((tm, D), lambda i: (i, 0))
    half = pl.BlockSpec((tm, H), lambda i: (i, 0))
    full = pl.BlockSpec((2 * H, D), lambda i: (0, 0))
    return pl.pallas_call(
        body, name="out_bwd", grid=(nt,),
        in_specs=[tile, tile, half, half, full, _mod_row(l, 2, D), _layer_row(l, D), ANY],
        out_specs=[half, half, full, row, row],
        out_shape=[jax.ShapeDtypeStruct((T, H), BF16), jax.ShapeDtypeStruct((T, H), BF16),
                   jax.ShapeDtypeStruct((2 * H, D), BF16),
                   jax.ShapeDtypeStruct((1, D), F32), jax.ShapeDtypeStruct((1, D), F32)],
        scratch_shapes=[pltpu.VMEM((2 * H, D), F32), pltpu.VMEM((SUBLANES, D), F32)],
        compiler_params=_params(VMEM_BIG),
    )(dx, y, ya, yp, wo, mod4, g_post3, after)


def _conv_bwd_block(u_ref, b_ref, c_ref, g_ref, dy_ref, w_ref, du_ref, db_ref, dc_ref, dg_ref, dw_ref):
    T = u_ref.shape[0]
    R = R_CONV
    nchunk = T // R
    w0 = w_ref[pl.ds(0, 1), :]
    w1 = w_ref[pl.ds(1, 1), :]
    w2 = w_ref[pl.ds(2, 1), :]

    def chunk(k, carry):
        head, a0, a1, a2 = carry
        i = nchunk - 1 - k
        r0 = pl.multiple_of(i * R, R)
        h0 = pl.multiple_of(jnp.maximum(r0 - HIST, 0), HIST)
        first = i == 0
        ue = _load_ext(u_ref, r0, h0, first, R)
        ce = _load_ext(c_ref, r0, h0, first, R)
        ca = ce * ue
        ca0 = ca[HIST:]
        ca1 = _shift_down(ca, 1, R)
        ca2 = _shift_down(ca, 2, R)
        conv = w2 * ca0 + w1 * ca1 + w0 * ca2
        g = g_ref[pl.ds(r0, R), :].astype(F32)
        b = b_ref[pl.ds(r0, R), :].astype(F32)
        dy = dy_ref[pl.ds(r0, R), :].astype(F32)
        sg = _sigmoid(g)
        sl = g * sg
        t = dy * conv
        db_ref[pl.ds(r0, R), :] = (t * sl).astype(BF16)
        dg_ref[pl.ds(r0, R), :] = (t * b * (sg * (1.0 + g * (1.0 - sg)))).astype(BF16)
        dconv = dy * b * sl
        a2 = a2 + _colsum8(dconv * ca0)
        a1 = a1 + _colsum8(dconv * ca1)
        a0 = a0 + _colsum8(dconv * ca2)
        e = jnp.concatenate([dconv, head], axis=0)
        dca = w2 * dconv + w1 * _shift_up(e, 1, R) + w0 * _shift_up(e, 2, R)
        du_ref[pl.ds(r0, R), :] = (dca * ce[HIST:]).astype(BF16)
        dc_ref[pl.ds(r0, R), :] = (dca * ue[HIST:]).astype(BF16)
        return dconv[0:SUBLANES], a0, a1, a2

    z = jnp.zeros((SUBLANES, LANES), F32)
    _, a0, a1, a2 = lax.fori_loop(0, nchunk, chunk, (z, z, z, z))
    dw_ref[pl.ds(0, 1), :] = jnp.sum(a0, axis=0, keepdims=True)
    dw_ref[pl.ds(1, 1), :] = jnp.sum(a1, axis=0, keepdims=True)
    dw_ref[pl.ds(2, 1), :] = jnp.sum(a2, axis=0, keepdims=True)


def _pool_bwd_group(p_ref, g_ref, dy_ref, w_ref, s_ref, du_ref, dg_ref, dw_ref, ds_ref,
                    pooled_s, mixed_s, dmix_s, dpool_s, w):
    T = p_ref.shape[0]
    R = R_POOL
    nchunk = T // R
    wb = w_ref[...].astype(BF16)
    _pooled_loop(p_ref, pooled_s, w, T)
    mixed_s[...] = jnp.dot(pooled_s[...], wb, preferred_element_type=F32)
    sc = s_ref[...]

    def gate_chunk(i, acc):
        r0 = pl.multiple_of(i * R, R)
        g = g_ref[pl.ds(r0, R), :].astype(F32)
        dy = dy_ref[pl.ds(r0, R), :].astype(F32)
        mixed = mixed_s[pl.ds(r0, R), :]
        sg = _sigmoid(g)
        dg_ref[pl.ds(r0, R), :] = (dy * mixed * sc * (sg * (1.0 + g * (1.0 - sg)))).astype(BF16)
        dms = dy * (g * sg)
        dmix_s[pl.ds(r0, R), :] = (dms * sc).astype(BF16)
        return acc + _colsum8(dms * mixed)

    acc = lax.fori_loop(0, nchunk, gate_chunk, jnp.zeros((SUBLANES, LANES), F32))
    ds_ref[...] = jnp.sum(acc, axis=0, keepdims=True)
    dpool_s[pl.ds(0, T), :] = lax.dot_general(dmix_s[...], wb, NT, preferred_element_type=F32)
    dpool_s[pl.ds(T, HIST), :] = jnp.zeros((HIST, LANES), F32)
    dw_ref[...] = lax.dot_general(pooled_s[...], dmix_s[...], TN, preferred_element_type=F32).astype(BF16)

    def back_chunk(i, carry):
        r0 = pl.multiple_of(i * R, R)
        dpe = dpool_s[pl.ds(r0, R + HIST), :]
        e = dpe / _count(r0, R + HIST, w)
        du_ref[pl.ds(r0, R), :] = (_anticausal_window_sum(e, w)[0:R] - dpe[0:R]).astype(BF16)
        return carry

    lax.fori_loop(0, nchunk, back_chunk, 0)


def mix_bwd(proj, dya, dyp, wconv, wpool, pscale3, l, after):
    T = proj.shape[0]

    def body(u_ref, b_ref, c_ref, g_ref, p_ref, gp_ref, dya_ref, dyp_ref, wc_ref, wp_ref, s_ref, after_ref,
             dua_ref, dba_ref, dca_ref, dga_ref, dup_ref, dgp_ref, dwc_ref, dwp_ref, ds_ref,
             pooled_s, mixed_s, dmix_s, dpool_s):
        j = pl.program_id(0)
        pl.when(j < N_MIX)(functools.partial(_conv_bwd_block, u_ref, b_ref, c_ref, g_ref, dya_ref, wc_ref,
                                             dua_ref, dba_ref, dca_ref, dga_ref, dwc_ref))
        for k, w in enumerate(POOL_WINDOWS):
            pl.when(j == N_MIX + k)(functools.partial(_pool_bwd_group, p_ref, gp_ref, dyp_ref, wp_ref, s_ref,
                                                      dup_ref, dgp_ref, dwp_ref, ds_ref,
                                                      pooled_s, mixed_s, dmix_s, dpool_s, w))

    sec = jax.ShapeDtypeStruct((T, N_MIX * LANES), BF16)
    conv_col = pl.BlockSpec((T, LANES), lambda j: (0, _conv_idx(j)))
    pool_col = pl.BlockSpec((T, LANES), lambda j: (0, _pool_idx(j)))
    return pl.pallas_call(
        body, name="mix_bwd", grid=(2 * N_MIX,),
        in_specs=[_proj_col(T, 0, _conv_idx), _proj_col(T, 4, _conv_idx), _proj_col(T, 8, _conv_idx),
                  _proj_col(T, 12, _conv_idx), _proj_col(T, 16, _pool_idx), _proj_col(T, 20, _pool_idx),
                  conv_col, pool_col, _conv_w_spec(l), _pool_w_spec(l), _pool_s_spec(l), ANY],
        out_specs=[conv_col, conv_col, conv_col, conv_col, pool_col, pool_col,
                   pl.BlockSpec((None, 3, LANES), lambda j: (_conv_idx(j), 0, 0)),
                   pl.BlockSpec((None, LANES, LANES), lambda j: (_pool_idx(j), 0, 0)),
                   pl.BlockSpec((1, LANES), lambda j: (0, _pool_idx(j)))],
        out_shape=[sec] * 6 + [jax.ShapeDtypeStruct((N_MIX, 3, LANES), F32),
                               jax.ShapeDtypeStruct((N_MIX, LANES, LANES), BF16),
                               jax.ShapeDtypeStruct((1, N_MIX * LANES), F32)],
        scratch_shapes=[pltpu.VMEM((T, LANES), BF16), pltpu.VMEM((T, LANES), F32),
                        pltpu.VMEM((T, LANES), BF16), pltpu.VMEM((T + HIST, LANES), F32)],
        compiler_params=_params(),
    )(proj, proj, proj, proj, proj, proj, dya, dyp, wconv, wpool, pscale3, after)


def in_bwd(dsecs, wg, x, dxo, mod4, g_pre3, l):
    T, D = x.shape
    NB = N_CHIPS
    CW = wg.shape[1] // NB
    SW = dsecs[0].shape[1]
    nsec = len(dsecs)
    PW = 256
    assert SW % PW == 0 and CW % PW == 0
    tm = 256
    nt = T // tm

    def body(*refs):
        d_refs = refs[0:nsec]
        w_ref, x_ref, dxo_ref, sh_ref, sc_ref, g_ref = refs[nsec:nsec + 6]
        dxi_ref, dw_ref, dsh_ref, dsc_ref, dg_ref = refs[nsec + 6:nsec + 11]
        acc_w, acc_sh, acc_q = refs[nsec + 11:]
        i = pl.program_id(0)

        @pl.when(i == 0)
        def _():
            acc_w[...] = jnp.zeros_like(acc_w)
            acc_sh[...] = jnp.zeros_like(acc_sh)
            acc_q[...] = jnp.zeros_like(acc_q)

        xv = x_ref[...]
        r = _rms(xv)
        xh = xv * r
        sg = g_ref[...] * (1.0 + sc_ref[...])
        hb = (xh * sg + sh_ref[...]).astype(BF16)
        dh = lax.dot_general(d_refs[0][...], w_ref[:, 0:SW], NT, preferred_element_type=F32)
        for s in range(1, nsec):
            dh = dh + lax.dot_general(d_refs[s][...], w_ref[:, s * SW:(s + 1) * SW], NT, preferred_element_type=F32)
        for p in range(nsec * SW // PW):
            col = p * PW
            s, so = col // SW, col % SW
            j, jo = col // CW, col % CW
            acc_w[j, :, jo:jo + PW] += lax.dot_general(hb, d_refs[s][:, so:so + PW], TN, preferred_element_type=F32)
        q = dh * xh
        acc_sh[...] += _colsum8(dh)
        acc_q[...] += _colsum8(q)
        dxi_ref[...] = dxo_ref[...] + r * (dh * sg - xh * jnp.mean(q * sg, axis=-1, keepdims=True))

        @pl.when(i == nt - 1)
        def _():
            dw_ref[...] = acc_w[...].astype(BF16)
            sq = jnp.sum(acc_q[...], axis=0, keepdims=True)
            dsh_ref[...] = jnp.sum(acc_sh[...], axis=0, keepdims=True)
            dsc_ref[...] = g_ref[...] * sq
            dg_ref[...] = (1.0 + sc_ref[...]) * sq

    row = pl.BlockSpec((1, D), lambda i: (0, 0))
    tile = pl.BlockSpec((tm, D), lambda i: (i, 0))
    sect = pl.BlockSpec((tm, SW), lambda i: (i, 0))
    rowshape = jax.ShapeDtypeStruct((1, D), F32)
    return pl.pallas_call(
        body, name="in_bwd", grid=(nt,),
        in_specs=[sect] * nsec + [pl.BlockSpec((D, NB * CW), lambda i: (0, 0)), tile, tile,
                                  _mod_row(l, 0, D), _mod_row(l, 1, D), _layer_row(l, D)],
        out_specs=[tile, pl.BlockSpec((NB, D, CW), lambda i: (0, 0, 0)), row, row, row],
        out_shape=[jax.ShapeDtypeStruct((T, D), F32), jax.ShapeDtypeStruct((NB, D, CW), BF16),
                   rowshape, rowshape, rowshape],
        scratch_shapes=[pltpu.VMEM((NB, D, CW), F32),
                        pltpu.VMEM((SUBLANES, D), F32), pltpu.VMEM((SUBLANES, D), F32)],
        compiler_params=_params(VMEM_BIG),
    )(*dsecs, wg, x, dxo, mod4, mod4, g_pre3)


def _rcopy(src, dst, ssem, rsem, dev):
    return pltpu.make_async_remote_copy(src_ref=src, dst_ref=dst, send_sem=ssem, recv_sem=rsem,
                                        device_id=dev, device_id_type=MESH)


def _peers7(x, y, c):
    out = []
    for m in range(1, N_DEV):
        bx, by, bc = (m >> 2) & 1, (m >> 1) & 1, m & 1
        out.append(((1 - x) if bx else x, (1 - y) if by else y, (1 - c) if bc else c))
    return out


HBM = pl.BlockSpec(memory_space=pltpu.HBM)
SEM = pl.BlockSpec(memory_space=pltpu.SEMAPHORE)
SPLIT = pltpu.CompilerParams(has_side_effects=pltpu.SideEffectType.DATAFLOW_SIDE_EFFECTING)


def _hbm(a):
    return pltpu.with_memory_space_constraint(a, pltpu.HBM)


def _chips(x, y):
    return [(1 - x, y), (x, 1 - y), (1 - x, 1 - y)]


SIBLING_BARRIER_ID = 0


def xchg_start(name, bufs, n_copies, plan, sibling_only=False, after=()):
    n = len(bufs)
    after = list(after)

    def body(*refs):
        ssem, rsem, token = refs[n + len(after)], refs[n + len(after) + 1], refs[-1]
        x, y, c = _me()
        if sibling_only:
            barrier = pltpu.get_barrier_semaphore()
            pl.semaphore_signal(barrier, inc=1, device_id=(x, y, 1 - c), device_id_type=MESH)
            pl.semaphore_wait(barrier, 1)
        copies = plan(refs[0:n], x, y, c)
        assert len(copies) == n_copies
        for k, (src, dst, peer, _) in enumerate(copies):
            _rcopy(src, dst, ssem.at[k], rsem.at[k], peer).start()
        token[...] = jnp.zeros_like(token)

    params = dict(has_side_effects=pltpu.SideEffectType.DATAFLOW_SIDE_EFFECTING)
    if sibling_only:
        params["collective_id"] = SIBLING_BARRIER_ID
    outs = pl.pallas_call(
        body, name=name,
        in_specs=[HBM] * n + [ANY] * len(after),
        out_specs=[SEM, SEM] + [HBM] * n + [pl.BlockSpec(memory_space=pltpu.VMEM)],
        out_shape=([pltpu.SemaphoreType.DMA((n_copies,))] * 2 + [pltpu.HBM(b.shape, b.dtype) for b in bufs]
                   + [jax.ShapeDtypeStruct((SUBLANES, LANES), F32)]),
        input_output_aliases={a: 2 + a for a in range(n)},
        compiler_params=pltpu.CompilerParams(**params),
    )(*[_hbm(b) for b in bufs], *after)
    return outs[0], outs[1], list(outs[2:2 + n]), outs[-1]


def xchg_wait(name, bufs, ssem, rsem, n_copies, plan, after, sems=None):
    n = len(bufs)
    after = list(after)
    sems = tuple(range(n_copies)) if sems is None else tuple(sems)
    assert len(sems) == n_copies

    def body(*refs):
        ssem_ref, rsem_ref = refs[n], refs[n + 1]
        copies = plan(refs[0:n], *_me())
        assert len(copies) == n_copies
        for k, (src, _, peer, land) in zip(sems, copies):
            cp = _rcopy(src, land, ssem_ref.at[k], rsem_ref.at[k], peer)
            cp.wait_send()
            cp.wait_recv()

    outs = pl.pallas_call(
        body, name=name,
        in_specs=[HBM] * n + [SEM, SEM] + [ANY] * len(after), out_specs=[HBM] * n,
        out_shape=[pltpu.HBM(b.shape, b.dtype) for b in bufs],
        input_output_aliases={a: a for a in range(n)},
        compiler_params=SPLIT,
    )(*bufs, ssem, rsem, *after)
    return list(outs)


def _shard_half(buf, chip, half):
    if len(buf.shape) == 2:
        h, w = buf.shape[0] // 2, buf.shape[1] // N_CHIPS
        return buf.at[pl.ds(half * h, h), pl.ds(chip * w, w)]
    h = buf.shape[1] // 2
    return buf.at[chip, pl.ds(half * h, h)]


def plan_gather(refs, x, y, c):
    out = []
    for (px, py) in _chips(x, y):
        for buf in refs:
            own = _shard_half(buf, 2 * x + y, c)
            out.append((own, own, (px, py, c), _shard_half(buf, 2 * px + py, c)))
    return out


def plan_forward(refs, x, y, c):
    out = []
    for (px, py) in _chips(x, y):
        for buf in refs:
            landed = _shard_half(buf, 2 * px + py, c)
            out.append((landed, landed, (x, y, 1 - c), _shard_half(buf, 2 * px + py, 1 - c)))
    return out


def plan_sibling(refs, x, y, c):
    n = len(refs) // 2
    out = []
    for a in range(n):
        h = refs[a].shape[1] // 2
        out.append((refs[a].at[:, pl.ds((1 - c) * h, h)], refs[n + a], (x, y, 1 - c), refs[n + a]))
    return out


def plan_chip(refs, x, y, c):
    n = len(refs) // 2
    out = []
    for j, (px, py) in enumerate(_chips(x, y)):
        for a in range(n):
            out.append((refs[a].at[2 * px + py], refs[n + a].at[j], (px, py, c), refs[n + a].at[j]))
    return out


def plan_mod(refs, x, y, c):
    (mods,) = refs
    mine = mods.at[2 * x + y]
    return [(mine, mine, (px, py, c), mods.at[2 * px + py]) for (px, py) in _chips(x, y)]


def plan_pack(refs, x, y, c):
    (packs,) = refs
    mine = packs.at[4 * x + 2 * y + c]
    return [(mine, mine, peer, packs.at[4 * peer[0] + 2 * peer[1] + peer[2]]) for peer in _peers7(x, y, c)]


def plan_spread(layers, wp_layers):
    def plan(refs, x, y, c):
        gi, go, gp = refs
        hD, hR, hP = gi.shape[1] // 2, go.shape[1] // 2, gp.shape[2] // 2
        sib = (x, y, 1 - c)
        out = []
        for l in layers:
            mine = gi.at[l, pl.ds(c * hD, hD)]
            out.append((mine, mine, sib, gi.at[l, pl.ds((1 - c) * hD, hD)]))
            mine = go.at[l, pl.ds(c * hR, hR)]
            out.append((mine, mine, sib, go.at[l, pl.ds((1 - c) * hR, hR)]))
        for l in wp_layers:
            mine = gp.at[l, 2 * x + y, pl.ds(c * hP, hP)]
            for peer in _peers7(x, y, c):
                out.append((mine, mine, peer, gp.at[l, 2 * peer[0] + peer[1], pl.ds(peer[2] * hP, hP)]))
        return out

    return plan


def gather_small(c8, wc, token):
    def body(c_ref, wc_ref, token_ref, call, wcall, ssem, rsem, lsem):
        x, y, c = _me()
        myc = 2 * x + y
        me_lin = 4 * x + 2 * y + c
        me = (x, y, c)
        local = [pltpu.make_async_copy(c_ref, call.at[me_lin], lsem.at[0]),
                 pltpu.make_async_copy(wc_ref, wcall.at[myc], lsem.at[1])]
        for cp in local:
            cp.start()
        sends, recvs = [], []
        for m, peer in enumerate(_peers7(x, y, c)):
            plin = 4 * peer[0] + 2 * peer[1] + peer[2]
            sends.append(_rcopy(c_ref, call.at[me_lin], ssem.at[m], rsem.at[m], peer))
            recvs.append(_rcopy(call.at[plin], call.at[plin], ssem.at[m], rsem.at[m], me))
        for j, (px, py) in enumerate([(1 - x, y), (x, 1 - y), (1 - x, 1 - y)]):
            pc = 2 * px + py
            sends.append(_rcopy(wc_ref, wcall.at[myc], ssem.at[7 + j], rsem.at[7 + j], (px, py, c)))
            recvs.append(_rcopy(wcall.at[pc], wcall.at[pc], ssem.at[7 + j], rsem.at[7 + j], me))
        for cp in sends:
            cp.start()
        for cp in recvs:
            cp.wait_recv()
        for cp in sends:
            cp.wait_send()
        for cp in local:
            cp.wait()

    return pl.pallas_call(
        body, name="gather_small",
        in_specs=[ANY] * 3, out_specs=[ANY] * 2,
        out_shape=[jax.ShapeDtypeStruct((N_DEV, SUBLANES, LANES), F32),
                   jax.ShapeDtypeStruct((N_CHIPS, wc.shape[0], 3, LANES), F32)],
        scratch_shapes=[pltpu.SemaphoreType.DMA((10,)), pltpu.SemaphoreType.DMA((10,)), pltpu.SemaphoreType.DMA((2,))],
        compiler_params=_params(n_grid=0),
    )(c8, wc, token)


def spread_now(gi, go, gp, layers, wp_layers):
    plan = plan_spread(layers, wp_layers)
    n = 2 * len(layers) + 7 * len(wp_layers)

    def body(gi_in, go_in, gp_in, gi, go, gp, ssem, rsem):
        copies = plan((gi, go, gp), *_me())
        me = _me()
        sends = [_rcopy(src, dst, ssem.at[k], rsem.at[k], peer) for k, (src, dst, peer, _) in enumerate(copies)]
        for cp in sends:
            cp.start()
        for k, (_, _, _, land) in enumerate(copies):
            _rcopy(land, land, ssem.at[k], rsem.at[k], me).wait_recv()
        for cp in sends:
            cp.wait_send()

    return pl.pallas_call(
        body, name="spread_now",
        in_specs=[ANY] * 3, out_specs=[ANY] * 3,
        out_shape=[jax.ShapeDtypeStruct(a.shape, a.dtype) for a in (gi, go, gp)],
        input_output_aliases={0: 0, 1: 1, 2: 2},
        scratch_shapes=[pltpu.SemaphoreType.DMA((n,)), pltpu.SemaphoreType.DMA((n,))],
        compiler_params=_params(n_grid=0),
    )(gi, go, gp)


def add_sibling(cidx, mine, sib):
    def body(c_ref, *refs):
        for a in range(3):
            m, s, o = refs[a], refs[3 + a], refs[6 + a]
            o[...] = (m[...].astype(F32) + s[...].astype(F32)).astype(BF16)

    per_step = 2

    def mine_spec(a):
        h = a.shape[1] // 2
        return pl.BlockSpec((per_step, h, a.shape[2]), lambda j, c_ref: (j, c_ref[0], 0))

    def sib_spec(a):
        return pl.BlockSpec((per_step,) + a.shape[1:], lambda j, c_ref: (j, 0, 0))

    return pl.pallas_call(
        body, name="add_sibling",
        grid_spec=pltpu.PrefetchScalarGridSpec(
            num_scalar_prefetch=1, grid=(N_CHIPS // per_step,),
            in_specs=[mine_spec(a) for a in mine] + [sib_spec(a) for a in sib],
            out_specs=[sib_spec(a) for a in sib]),
        out_shape=[jax.ShapeDtypeStruct(a.shape, BF16) for a in sib],
        compiler_params=_params(VMEM_BIG),
    )(cidx, *mine, *sib)


def sum_chips(pos, own, rb, acc, l, shapes):
    nq = 2
    n_in = 6 + (3 if acc is not None else 0)

    def body(pos_ref, *refs):
        for a in range(3):
            m, b, o = refs[a], refs[3 + a], refs[n_in + a]
            s = m[...].astype(F32)
            for j in range(3):
                s = s + b[j].astype(F32)
            o[...] = s

    def own_spec(a):
        return pl.BlockSpec((None, a.shape[1] // nq, a.shape[2]), lambda q, p: (p[1], q, 0))

    def rb_spec(a):
        return pl.BlockSpec((3, a.shape[1] // nq, a.shape[2]), lambda q, p: (0, q, 0))

    hi, ho, hp = own[0].shape[1] // nq, own[1].shape[1] // nq, own[2].shape[1] // nq
    out_specs = [pl.BlockSpec((None, hi, shapes[0][2]), lambda q, p: (l, p[0] * nq + q, 0)),
                 pl.BlockSpec((None, ho, shapes[1][2]), lambda q, p: (l, p[0] * nq + q, 0)),
                 pl.BlockSpec((None, None, hp, LANES), lambda q, p: (l, p[1], p[0] * nq + q, 0))]
    in_specs = [own_spec(a) for a in own] + [rb_spec(a) for a in rb]
    args = list(own) + list(rb)
    aliases = {}
    if acc is not None:
        in_specs += [ANY] * 3
        args += list(acc)
        aliases = {7: 0, 8: 1, 9: 2}
    return pl.pallas_call(
        body, name="sum_chips",
        grid_spec=pltpu.PrefetchScalarGridSpec(num_scalar_prefetch=1, grid=(nq,), in_specs=in_specs, out_specs=out_specs),
        out_shape=[jax.ShapeDtypeStruct(s, F32) for s in shapes],
        input_output_aliases=aliases,
        compiler_params=_params(VMEM_BIG),
    )(pos, *args)


def pack_small(pos, per_layer, loss_blk):
    L = len(per_layer)
    D = per_layer[0][0].shape[1]

    def body(pos_ref, *refs):
        o = refs[-1]
        lb = refs[-2]
        o[...] = jnp.zeros_like(o)
        for l in range(L):
            dgpre, dgpost, dsh, dsc, dgt, dps, dwc = refs[7 * l:7 * l + 7]
            base = SUBLANES * l
            for r, src in enumerate((dgpre, dgpost, dsh, dsc, dgt)):
                o[pl.ds(base + r, 1), :] = src[...]
            o[pl.ds(base + 5, 1), 0:dps.shape[1]] = dps[...]
            for j in range(dwc.shape[0]):
                for k in range(3):
                    idx = 3 * j + k
                    o[pl.ds(base + 6 + idx // 8, 1), (idx % 8) * LANES:(idx % 8 + 1) * LANES] = dwc[j, pl.ds(k, 1), :]
        o[pl.ds(5, 1), 4 * LANES:5 * LANES] = lb[pl.ds(0, 1), :]

    flat = [a for layer in per_layer for a in layer] + [loss_blk]

    def whole(a):
        return pl.BlockSpec(a.shape, lambda i, p: (0,) * a.ndim)

    return pl.pallas_call(
        body, name="pack_small",
        grid_spec=pltpu.PrefetchScalarGridSpec(
            num_scalar_prefetch=1, grid=(1,), in_specs=[whole(a) for a in flat],
            out_specs=pl.BlockSpec((None, L * SUBLANES, D), lambda i, p: (p[2], 0, 0))),
        out_shape=jax.ShapeDtypeStruct((N_DEV, L * SUBLANES, D), F32),
        compiler_params=_params(),
    )(pos, *flat)


def small_update(pos, packs, params, moments_m, moments_v):
    n = len(params)
    L, D = params[1].shape
    PS = params[3].shape[1]

    def body(pos_ref, p_ref, *refs):
        ws, ms, vs = refs[0:n], refs[n:2 * n], refs[2 * n:3 * n]
        loss_ref = refs[3 * n]
        outs = [refs[3 * n + 1 + 4 * t:3 * n + 5 + 4 * t] for t in range(n)]
        summed = refs[-1]
        s = p_ref[0]
        for d in range(1, N_DEV):
            s = s + p_ref[d]
        summed[...] = s
        loss_ref[...] = summed[pl.ds(5, 1), 4 * LANES:5 * LANES]
        chip = pos_ref[1]

        def update(t, idx, g):
            d, mm, vv = _adamw_math(ws[t][idx], g, ms[t][idx], vs[t][idx])
            g_ref, d_ref, mo_ref, vo_ref = outs[t]
            g_ref[idx] = g
            d_ref[idx] = d
            mo_ref[idx] = mm
            vo_ref[idx] = vv

        for l in range(L):
            base = SUBLANES * l
            row = pl.ds(l, 1)
            for k in range(3):
                update(0, (row, slice(k * D, (k + 1) * D)), summed[pl.ds(base + 2 + k, 1), :])
            update(1, (row, slice(None)), summed[pl.ds(base, 1), :])
            update(2, (row, slice(None)), summed[pl.ds(base + 1, 1), :])
            update(3, (row, slice(None)), summed[pl.ds(base + 5, 1), 0:PS])
            for k in range(3):
                g = None
                for j in range(N_CHIPS):
                    idx = 3 * j + k
                    cand = summed[pl.ds(base + 6 + idx // 8, 1), (idx % 8) * LANES:(idx % 8 + 1) * LANES]
                    g = cand if g is None else jnp.where(chip == j, cand, g)
                update(4, (l, pl.ds(k, 1), slice(None)), g)

    def whole(a):
        return pl.BlockSpec(a.shape, lambda i, p: (0,) * a.ndim)

    ins = [packs] + list(params) + list(moments_m) + list(moments_v)
    out_shape = [jax.ShapeDtypeStruct((1, LANES), F32)]
    for w in params:
        out_shape += [jax.ShapeDtypeStruct(w.shape, F32)] * 4
    outs = pl.pallas_call(
        body, name="small_update",
        grid_spec=pltpu.PrefetchScalarGridSpec(
            num_scalar_prefetch=1, grid=(1,), in_specs=[whole(a) for a in ins],
            out_specs=[whole(a) for a in out_shape],
            scratch_shapes=[pltpu.VMEM(packs.shape[1:], F32)]),
        out_shape=out_shape,
        compiler_params=_params(),
    )(pos, *ins)
    return outs[0], [outs[1 + 4 * t:5 + 4 * t] for t in range(n)]


def _adamw_math(w, g, m, v):
    m = ADAM_B1 * m + (1.0 - ADAM_B1) * g
    v = ADAM_B2 * v + (1.0 - ADAM_B2) * (g * g)
    m_hat = m / (1.0 - ADAM_B1 ** ADAM_STEP)
    v_hat = v / (1.0 - ADAM_B2 ** ADAM_STEP)
    delta = -ADAM_LR * (m_hat / (jnp.sqrt(v_hat) + ADAM_EPS) + ADAM_WD * w)
    return delta, m, v


def adamw(w, g, m, v, block, name, first=0, count=None, acc=None):
    grid = tuple(s // b for s, b in zip(w.shape, block))
    if count is not None:
        grid = (count,) + grid[1:]

    def body(w_ref, g_ref, m_ref, v_ref, *rest):
        go_ref, d_ref, mo_ref, vo_ref = rest[-4:]
        gv = g_ref[...]
        d, mm, vv = _adamw_math(w_ref[...], gv, m_ref[...], v_ref[...])
        go_ref[...] = gv
        d_ref[...] = d
        mo_ref[...] = mm
        vo_ref[...] = vv

    spec = pl.BlockSpec(block, lambda i, *rest: (first + i,) + rest)
    shape = jax.ShapeDtypeStruct(w.shape, F32)
    extra = [] if acc is None else list(acc)
    return pl.pallas_call(
        body, name=name, grid=grid,
        in_specs=[spec] * 4 + [ANY] * len(extra), out_specs=[spec] * 4, out_shape=[shape] * 4,
        input_output_aliases={4 + a: a for a in range(len(extra))},
        compiler_params=_params(VMEM_BIG, n_grid=len(grid)),
    )(w, g, m, v, *extra)


def ada_finish(c_all, dmod, w, m, v):
    L, D, CW = w.shape
    hD = D // 2

    def body(c_ref, d_ref, w_ref, m_ref, v_ref, g_ref, dl_ref, mo_ref, vo_ref):
        cv = c_ref[...]
        z = jnp.zeros_like(cv)
        ca = jnp.concatenate([cv * jax.nn.sigmoid(cv), z], axis=0).astype(BF16)
        dm = jnp.concatenate([d_ref[0], jnp.zeros_like(d_ref[0])], axis=0).astype(BF16)
        g = lax.dot_general(ca, dm, TN, preferred_element_type=F32)
        g_ref[0] = g
        d, mm, vv = _adamw_math(w_ref[0], g, m_ref[0], v_ref[0])
        dl_ref[0] = d
        mo_ref[0] = mm
        vo_ref[0] = vv

    big = pl.BlockSpec((1, hD, CW), lambda l, h: (l, h, 0))
    shape = jax.ShapeDtypeStruct(w.shape, F32)
    return pl.pallas_call(
        body, name="ada_finish", grid=(L, 2),
        in_specs=[pl.BlockSpec((N_DEV, hD), lambda l, h: (0, h)), pl.BlockSpec((1, N_DEV, CW), lambda l, h: (l, 0, 0)),
                  big, big, big],
        out_specs=[big] * 4, out_shape=[shape] * 4,
        compiler_params=_params(VMEM_BIG, n_grid=2),
    )(c_all, dmod, w, m, v)


def kernel(x, c, w_ada, b_ada, g_pre, w_in, w_conv, w_pool, pool_scale, w_out, g_post, loss_target, m_w_ada, m_b_ada, m_g_pre, m_w_in, m_w_conv, m_w_pool, m_pool_scale, m_w_out, m_g_post, v_w_ada, v_b_ada, v_g_pre, v_w_in, v_w_conv, v_w_pool, v_pool_scale, v_w_out, v_g_post):
    L, D, CW = w_in.shape
    RO = w_out.shape[1]
    T = x.shape[1]
    ix, iy, ic = _me()
    chip = 2 * ix + iy
    me_lin = 4 * ix + 2 * iy + ic

    pos = jnp.stack([ic, chip, me_lin]).astype(jnp.int32)
    g_pre3, g_post3 = g_pre.reshape(L, 1, D), g_post.reshape(L, 1, D)
    pscale3 = pool_scale.reshape(L, 1, pool_scale.shape[1])
    n_s, n_c = 3, 9

    def gather(bufs, after):
        ss, rs, bufs, tok = xchg_start("gather_start", bufs, 3 * len(bufs), plan_gather, after=after)
        return (ss, rs, bufs), tok

    def arrive(flight, after):
        ss, rs, bufs = flight
        bufs = xchg_wait("gather_wait", bufs, ss, rs, 3 * len(bufs), plan_gather, after)
        fss, frs, bufs, tok = xchg_start("forward_start", bufs, 3 * len(bufs), plan_forward, sibling_only=True)
        return (fss, frs, bufs), tok

    def ready(flight, after):
        fss, frs, bufs = flight
        return xchg_wait("forward_wait", bufs, fss, frs, 3 * len(bufs), plan_forward, after)

    def arrive_part(flight, which, after):
        ss, rs, bufs = flight
        sems = tuple(range(which, 3 * len(bufs), len(bufs)))
        (buf,) = xchg_wait("gather_wait", [bufs[which]], ss, rs, 3, plan_gather, after, sems=sems)
        fss, frs, (buf,), tok = xchg_start("forward_start", [buf], 3, plan_forward, sibling_only=True)
        return (fss, frs, [buf]), tok

    c_all3, wconv_all = gather_small(c.reshape(SUBLANES, LANES), w_conv, pos)
    c_all = c_all3.reshape(N_DEV, D)
    gi0, go0 = cast_weights(pos, w_in, w_out, 0, c_all3)
    fly_in0, token = gather([gi0], [])
    b_my = lax.dynamic_slice_in_dim(b_ada, chip * CW, CW, axis=1)
    m_ss, m_rs, mods, token = xchg_start("mod_start", [mod_part(pos, c_all, w_ada, b_my, token)], 3, plan_mod)
    fly_out0, token = gather([go0], [token])
    flying_w = [None] * L
    for l in range(1, L):
        flying_w[l], token = gather(list(cast_weights(pos, w_in, w_out, l, token)), [])
    fwd_in0, token = arrive(fly_in0, [token])
    (mod_all,) = xchg_wait("mod_wait", mods, m_ss, m_rs, 3, plan_mod, [token])
    mod = lax.dynamic_index_in_dim(mod_all, me_lin, axis=2, keepdims=False)
    mod4 = jnp.transpose(mod, (1, 0, 2)).reshape(L, 3, 1, D)

    xs, projs, yas, yps, ys = [x.reshape(T, D)], [], [], [], []
    wg_in, wg_out = [], []
    fwd_in = fwd_in0
    for l in range(L):
        (gi,) = ready(fwd_in, [mod4 if l == 0 else xs[l]])
        proj = proj_fwd(xs[l], mod4, g_pre3, gi, l)
        ya, yp = mix_fwd(proj, wconv_all, w_pool, pscale3, l)
        after = [ya, yp]
        if l == 0:
            fwd_out, token = arrive(fly_out0, after)
        else:
            fwd_out, token = arrive_part(flying_w[l], 1, after)
        after = [token]
        if l + 1 < L:
            fwd_in, token = arrive_part(flying_w[l + 1], 0, after)
            after = [token]
        (go,) = ready(fwd_out, after)
        wg_in.append(gi)
        wg_out.append(go.reshape(N_CHIPS * RO, D))
        projs.append(proj)
        yas.append(ya)
        yps.append(yp)
        if l + 1 < L:
            xn, yv = out_fwd(ya, yp, wg_out[l], xs[l], mod4, g_post3, l, after[0])
            xs.append(xn)
        else:
            dx, yv, loss_blk = out_fwd_loss(ya, yp, wg_out[l], xs[l], mod4, g_post3, l, loss_target.reshape(T, D))
        ys.append(yv)

    shapes = (w_in.shape, w_out.shape, w_pool.shape)
    smalls = [None] * L
    acc, flying, sib, token = None, None, None, loss_blk

    def to_chips(sib, after):
        sl, s_ss, s_rs, s_bufs = sib
        s_bufs = xchg_wait("sibling_wait", s_bufs, s_ss, s_rs, n_s, plan_sibling, after)
        chip_parts = add_sibling(pos, s_bufs[0:3], s_bufs[3:6])
        lands = [lax.empty((3,) + a.shape[1:], a.dtype) for a in chip_parts]
        c_ss, c_rs, c_bufs, ctoken = xchg_start("chip_start", list(chip_parts) + lands, n_c, plan_chip)
        return (sl, c_ss, c_rs, c_bufs), ctoken

    def landed(flying, acc, after):
        fl, f_ss, f_rs, f_bufs = flying
        f_bufs = xchg_wait("chip_wait", f_bufs, f_ss, f_rs, n_c, plan_chip, after)
        return sum_chips(pos, f_bufs[0:3], f_bufs[3:6], acc, fl, shapes)

    for l in reversed(range(L)):
        dya, dyp, dwo_l, dgate, dgpost = out_bwd(dx, ys[l], yas[l], yps[l], wg_out[l], mod4, g_post3, l, token)
        token = dya
        if sib is not None:
            arrived = flying
            flying, token = to_chips(sib, [dya])
            if arrived is not None:
                acc = landed(arrived, acc, [token])
                token = acc[0]
        du_a, db_a, dc_a, dg_a, du_p, dg_p, dwc, dwp_l, dps = mix_bwd(projs[l], dya, dyp, wconv_all, w_pool, pscale3, l,
                                                                        token)
        dx, dwi_l, dshift, dscale, dgpre = in_bwd([du_a, db_a, dc_a, dg_a, du_p, dg_p], wg_in[l], xs[l], dx,
                                                  mod4, g_pre3, l)
        smalls[l] = (dgpre, dgpost, dshift, dscale, dgate, dps, dwc)
        parts = [dwi_l, dwo_l.reshape(N_CHIPS, RO, D), dwp_l]
        s_lands = [lax.empty((a.shape[0], a.shape[1] // 2) + a.shape[2:], a.dtype) for a in parts]
        s_ss, s_rs, s_bufs, token = xchg_start("sibling_start", parts + s_lands, n_s, plan_sibling, sibling_only=True)
        sib = (l, s_ss, s_rs, s_bufs)
    grad_x = dx.reshape(1, T, D)

    p_ss, p_rs, packs, ptoken = xchg_start("pack_start", [pack_small(pos, smalls, loss_blk)], N_DEV - 1, plan_pack)
    acc = landed(flying, acc, [ptoken, token])
    n_sp = (2 + N_DEV - 1) * (L - 1)
    spread = plan_spread(tuple(range(1, L)), tuple(range(1, L)))
    sp_ss, sp_rs, acc, sp_token = xchg_start("spread_start", list(acc), n_sp, spread)
    flying, token = to_chips(sib, [sp_token])
    (packs_all,) = xchg_wait("pack_wait", packs, p_ss, p_rs, N_DEV - 1, plan_pack, [token])
    dmod_all = packs_all.reshape(N_DEV, L, SUBLANES, D)[:, :, 2:5].reshape(N_DEV, L, 3 * D)
    dmod_my = jnp.transpose(lax.dynamic_slice_in_dim(dmod_all, chip * CW, CW, axis=2), (1, 0, 2))

    g_w_ada, d_w_ada, nm_w_ada, nv_w_ada = ada_finish(c_all, dmod_my, w_ada, m_w_ada, v_w_ada)
    loss_row, upd = small_update(pos, packs_all, [b_ada, g_pre, g_post, pool_scale, w_conv],
                                 [m_b_ada, m_g_pre, m_g_post, m_pool_scale, m_w_conv],
                                 [v_b_ada, v_g_pre, v_g_post, v_pool_scale, v_w_conv])
    loss = loss_row[0, 0]
    (g_b_ada, d_b_ada, nm_b_ada, nv_b_ada), (g_g_pre, d_g_pre, nm_g_pre, nv_g_pre) = upd[0], upd[1]
    (g_g_post, d_g_post, nm_g_post, nv_g_post), (g_pscale, d_pscale, nm_pscale, nv_pscale) = upd[2], upd[3]
    g_w_conv, d_w_conv, nm_w_conv, nv_w_conv = upd[4]

    done = [nv_w_ada, nv_w_conv]
    g_w_in, g_w_out, g_w_pool = xchg_wait("spread_wait", acc, sp_ss, sp_rs, n_sp, spread, done)
    in_blk, out_blk = (1, D // 2, CW), (1, RO, D)
    upd_in = adamw(w_in, g_w_in, m_w_in, v_w_in, in_blk, "adamw_w_in", 1, L - 1)
    upd_out = adamw(w_out, g_w_out, m_w_out, v_w_out, out_blk, "adamw_w_out", 1, L - 1)

    acc = landed(flying, (g_w_in, g_w_out, g_w_pool), [upd_in[3], upd_out[3]])
    r_w_in, r_w_out, r_w_pool = spread_now(*acc, (0,), (0,))
    g_w_in, d_w_in, nm_w_in, nv_w_in = adamw(w_in, r_w_in, m_w_in, v_w_in, in_blk, "adamw_w_in", 0, 1, upd_in)
    g_w_out, d_w_out, nm_w_out, nv_w_out = adamw(w_out, r_w_out, m_w_out, v_w_out, out_blk, "adamw_w_out", 0, 1, upd_out)
    pshape = (L, N_CHIPS * LANES, LANES)
    upd_pool = adamw(w_pool.reshape(pshape), r_w_pool.reshape(pshape), m_w_pool.reshape(pshape),
                     v_w_pool.reshape(pshape), (1,) + pshape[1:], "adamw_w_pool")
    g_w_pool, d_w_pool, nm_w_pool, nv_w_pool = [a.reshape(w_pool.shape) for a in upd_pool]

    return (loss, grad_x,
            g_w_ada, g_b_ada, g_g_pre, g_w_in, g_w_conv, g_w_pool, g_pscale, g_w_out, g_g_post,
            d_w_ada, d_b_ada, d_g_pre, d_w_in, d_w_conv, d_w_pool, d_pscale, d_w_out, d_g_post,
            nm_w_ada, nm_b_ada, nm_g_pre, nm_w_in, nm_w_conv, nm_w_pool, nm_pscale, nm_w_out, nm_g_post,
            nv_w_ada, nv_b_ada, nv_g_pre, nv_w_in, nv_w_conv, nv_w_pool, nv_pscale, nv_w_out, nv_g_post)
```

```python
import functools

import jax
import jax.numpy as jnp
from jax import lax
from jax.experimental import pallas as pl
from jax.experimental.pallas import tpu as pltpu

F32 = jnp.float32
BF16 = jnp.bfloat16
MESH = pl.DeviceIdType.MESH
ANY = pl.BlockSpec(memory_space=pl.ANY)

NORM_EPS = 1e-6
POOL_WINDOWS = (2, 4, 8, 16)
ADAM_LR = 0.001
ADAM_B1 = 0.9
ADAM_B2 = 0.999
ADAM_EPS = 1e-08
ADAM_WD = 0.01
ADAM_STEP = 10

N_CHIPS = 4
N_DEV = 8
LANES = 128
SUBLANES = 8
VMEM_BIG = 56 * 1024 * 1024
HIST = 16
R_CONV = 64
R_POOL = 128

NT = (((1,), (1,)), ((), ()))
TN = (((0,), (0,)), ((), ()))


def _params(vmem=None, n_grid=1):
    kw = {}
    if n_grid:
        kw["dimension_semantics"] = ("arbitrary",) * n_grid
    if vmem is not None:
        kw["vmem_limit_bytes"] = vmem
    return pltpu.CompilerParams(**kw)


def _colsum8(v):
    n, d = v.shape
    return v.reshape(n // SUBLANES, SUBLANES, d).sum(axis=0)


def _rms(v):
    return lax.rsqrt(jnp.mean(v * v, axis=-1, keepdims=True) + NORM_EPS)


def _sigmoid(v):
    return 0.5 * jnp.tanh(0.5 * v) + 0.5


def _shift_down(ext, k, rows):
    if k == 0:
        return ext[HIST:HIST + rows]
    return pltpu.roll(ext, k, 0)[HIST:HIST + rows]


def _shift_up(ext, k, rows):
    if k == 0:
        return ext[0:rows]
    return pltpu.roll(ext, ext.shape[0] - k, 0)[0:rows]


def _load_ext(ref, r0, h0, first, rows):
    hist = ref[pl.ds(h0, HIST), :].astype(F32)
    hist = jnp.where(first, 0.0, hist)
    cur = ref[pl.ds(r0, rows), :].astype(F32)
    return jnp.concatenate([hist, cur], axis=0)


def _me():
    return lax.axis_index("x"), lax.axis_index("y"), lax.axis_index("c")


def cast_weights(pos, w_in, w_out, l, after):
    _, D, CW = w_in.shape
    RO = w_out.shape[1]

    def body(pos_ref, wi, wo, after_ref, oi, oo):
        oi[...] = wi[...].astype(BF16)
        oo[...] = wo[...].astype(BF16)

    return pl.pallas_call(
        body, name="cast_w",
        grid_spec=pltpu.PrefetchScalarGridSpec(
            num_scalar_prefetch=1, grid=(2,),
            in_specs=[pl.BlockSpec((None, D // 2, CW), lambda h, p: (l, h, 0)),
                      pl.BlockSpec((None, RO // 2, D), lambda h, p: (l, h, 0)), ANY],
            out_specs=[pl.BlockSpec((D // 2, CW), lambda h, p: (h, p[1])),
                       pl.BlockSpec((None, RO // 2, D), lambda h, p: (p[1], h, 0))]),
        out_shape=[jax.ShapeDtypeStruct((D, N_CHIPS * CW), BF16), jax.ShapeDtypeStruct((N_CHIPS, RO, D), BF16)],
        compiler_params=_params(),
    )(pos, w_in, w_out, after)


def mod_part(pos, c_all, w_ada, b_my, after):
    L, D, CW = w_ada.shape

    def body(pos_ref, c_ref, w_ref, b_ref, after_ref, o_ref):
        cv = c_ref[...]
        ca = (cv * jax.nn.sigmoid(cv)).astype(BF16)
        o_ref[...] = jnp.dot(ca, w_ref[0].astype(BF16), preferred_element_type=F32) + b_ref[0]

    return pl.pallas_call(
        body, name="mod_part",
        grid_spec=pltpu.PrefetchScalarGridSpec(
            num_scalar_prefetch=1, grid=(L,),
            in_specs=[pl.BlockSpec((N_DEV, D), lambda l, p: (0, 0)),
                      pl.BlockSpec((1, D, CW), lambda l, p: (l, 0, 0)),
                      pl.BlockSpec((1, 1, CW), lambda l, p: (l, 0, 0)), ANY],
            out_specs=pl.BlockSpec((None, None, N_DEV, CW), lambda l, p: (p[1], l, 0, 0))),
        out_shape=jax.ShapeDtypeStruct((N_CHIPS, L, N_DEV, CW), F32),
        compiler_params=_params(VMEM_BIG),
    )(pos, c_all, w_ada, b_my.reshape(L, 1, CW), after)


def _mod_row(l, k, D):
    return pl.BlockSpec((None, None, 1, D), lambda *_: (l, k, 0, 0))


def _layer_row(l, D):
    return pl.BlockSpec((None, 1, D), lambda *_: (l, 0, 0))


def proj_fwd(x, mod4, g_pre3, wg, l):
    T, D = x.shape
    NC = wg.shape[1]
    NB = N_CHIPS
    CW = NC // NB
    tm = 512

    def body(x_ref, sh_ref, sc_ref, g_ref, w_ref, o_ref):
        xv = x_ref[...]
        h = (xv * _rms(xv) * g_ref[...]) * (1.0 + sc_ref[...]) + sh_ref[...]
        hb = h.astype(BF16)
        for j in range(NB):
            cols = slice(j * CW, (j + 1) * CW)
            o_ref[:, cols] = jnp.dot(hb, w_ref[:, cols], preferred_element_type=F32).astype(BF16)

    return pl.pallas_call(
        body, name="proj_fwd", grid=(T // tm,),
        in_specs=[pl.BlockSpec((tm, D), lambda i: (i, 0)), _mod_row(l, 0, D), _mod_row(l, 1, D), _layer_row(l, D),
                  pl.BlockSpec((D, NC), lambda i: (0, 0))],
        out_specs=pl.BlockSpec((tm, NC), lambda i: (i, 0)),
        out_shape=jax.ShapeDtypeStruct((T, NC), BF16),
        compiler_params=_params(VMEM_BIG),
    )(x, mod4, mod4, g_pre3, wg)


N_MIX = 4


def _conv_fwd_block(u_ref, b_ref, c_ref, g_ref, w_ref, o_ref):
    T = u_ref.shape[0]
    R = 2 * R_CONV
    w0 = w_ref[pl.ds(0, 1), :]
    w1 = w_ref[pl.ds(1, 1), :]
    w2 = w_ref[pl.ds(2, 1), :]

    def chunk(i, carry):
        r0 = pl.multiple_of(i * R, R)
        h0 = pl.multiple_of(jnp.maximum(r0 - HIST, 0), HIST)
        first = i == 0
        ca = _load_ext(c_ref, r0, h0, first, R) * _load_ext(u_ref, r0, h0, first, R)
        conv = w2 * ca[HIST:] + w1 * _shift_down(ca, 1, R) + w0 * _shift_down(ca, 2, R)
        g = g_ref[pl.ds(r0, R), :].astype(F32)
        b = b_ref[pl.ds(r0, R), :].astype(F32)
        o_ref[pl.ds(r0, R), :] = (b * conv * (g * _sigmoid(g))).astype(BF16)
        return carry

    lax.fori_loop(0, T // R, chunk, 0)


def _conv_idx(j):
    return jnp.minimum(j, N_MIX - 1)


def _pool_idx(j):
    return jnp.maximum(j - N_MIX, 0)


def _proj_col(T, off, idx):
    return pl.BlockSpec((T, LANES), lambda j: (0, idx(j) + off))


def _causal_window_sum(ext, w):
    s, k = ext, 1
    while k < w:
        s = s + pltpu.roll(s, k, 0)
        k *= 2
    return s


def _anticausal_window_sum(ext, w):
    s, k = ext, 1
    n = ext.shape[0]
    while k < w:
        s = s + pltpu.roll(s, n - k, 0)
        k *= 2
    return s


def _count(r0, rows, w):
    t = r0 + lax.broadcasted_iota(jnp.int32, (rows, LANES), 0)
    return jnp.minimum(t + 1, w).astype(F32)


def _pooled_loop(p_ref, pooled_s, w, T):
    R = R_POOL

    def chunk(i, carry):
        r0 = pl.multiple_of(i * R, R)
        h0 = pl.multiple_of(jnp.maximum(r0 - HIST, 0), HIST)
        ext = _load_ext(p_ref, r0, h0, i == 0, R)
        ws = _causal_window_sum(ext, w)[HIST:]
        pooled_s[pl.ds(r0, R), :] = (ws / _count(r0, R, w) - ext[HIST:]).astype(BF16)
        return carry

    lax.fori_loop(0, T // R, chunk, 0)


def _conv_w_spec(l):
    return pl.BlockSpec((None, None, 3, LANES), lambda j: (_conv_idx(j), l, 0, 0))


def _pool_w_spec(l):
    return pl.BlockSpec((None, None, LANES, LANES), lambda j: (l, _pool_idx(j), 0, 0))


def _pool_s_spec(l):
    return pl.BlockSpec((None, 1, LANES), lambda j: (l, 0, _pool_idx(j)))


def _pool_fwd_group(p_ref, g_ref, w_ref, s_ref, o_ref, pooled_s, mixed_s, w):
    T = p_ref.shape[0]
    R = R_POOL
    _pooled_loop(p_ref, pooled_s, w, T)
    mixed_s[...] = jnp.dot(pooled_s[...], w_ref[...].astype(BF16), preferred_element_type=F32)
    sc = s_ref[...]

    def chunk(i, carry):
        r0 = pl.multiple_of(i * R, R)
        g = g_ref[pl.ds(r0, R), :].astype(F32)
        o_ref[pl.ds(r0, R), :] = (mixed_s[pl.ds(r0, R), :] * sc * (g * _sigmoid(g))).astype(BF16)
        return carry

    lax.fori_loop(0, T // R, chunk, 0)


def mix_fwd(proj, wconv, wpool, pscale3, l):
    T = proj.shape[0]

    def body(u_ref, b_ref, c_ref, g_ref, p_ref, gp_ref, wc_ref, wp_ref, s_ref, ya_ref, yp_ref, pooled_s, mixed_s):
        j = pl.program_id(0)
        pl.when(j < N_MIX)(functools.partial(_conv_fwd_block, u_ref, b_ref, c_ref, g_ref, wc_ref, ya_ref))
        for k, w in enumerate(POOL_WINDOWS):
            pl.when(j == N_MIX + k)(functools.partial(_pool_fwd_group, p_ref, gp_ref, wp_ref, s_ref, yp_ref,
                                                      pooled_s, mixed_s, w))

    half = jax.ShapeDtypeStruct((T, N_MIX * LANES), BF16)
    return pl.pallas_call(
        body, name="mix_fwd", grid=(2 * N_MIX,),
        in_specs=[_proj_col(T, 0, _conv_idx), _proj_col(T, 4, _conv_idx), _proj_col(T, 8, _conv_idx),
                  _proj_col(T, 12, _conv_idx), _proj_col(T, 16, _pool_idx), _proj_col(T, 20, _pool_idx),
                  _conv_w_spec(l), _pool_w_spec(l), _pool_s_spec(l)],
        out_specs=[pl.BlockSpec((T, LANES), lambda j: (0, _conv_idx(j))),
                   pl.BlockSpec((T, LANES), lambda j: (0, _pool_idx(j)))],
        out_shape=[half, half],
        scratch_shapes=[pltpu.VMEM((T, LANES), BF16), pltpu.VMEM((T, LANES), F32)],
        compiler_params=_params(),
    )(proj, proj, proj, proj, proj, proj, wconv, wpool, pscale3)


def out_fwd(ya, yp, wo, x, mod4, g_post3, l, after):
    T, D = x.shape
    H = ya.shape[1]
    tm = 512

    def body(ya_ref, yp_ref, wo_ref, x_ref, gt_ref, g_ref, after_ref, xn_ref, y_ref):
        y = (jnp.dot(ya_ref[...], wo_ref[0:H, :], preferred_element_type=F32)
             + jnp.dot(yp_ref[...], wo_ref[H:2 * H, :], preferred_element_type=F32))
        xn_ref[...] = x_ref[...] + gt_ref[...] * (y * _rms(y) * g_ref[...])
        y_ref[...] = y.astype(BF16)

    tile = pl.BlockSpec((tm, D), lambda i: (i, 0))
    half = pl.BlockSpec((tm, H), lambda i: (i, 0))
    return pl.pallas_call(
        body, name="out_fwd", grid=(T // tm,),
        in_specs=[half, half, pl.BlockSpec((2 * H, D), lambda i: (0, 0)), tile, _mod_row(l, 2, D), _layer_row(l, D),
                  ANY],
        out_specs=[tile, tile],
        out_shape=[jax.ShapeDtypeStruct((T, D), F32), jax.ShapeDtypeStruct((T, D), BF16)],
        compiler_params=_params(VMEM_BIG),
    )(ya, yp, wo, x, mod4, g_post3, after)


def out_fwd_loss(ya, yp, wo, x, mod4, g_post3, l, target):
    T, D = x.shape
    H = ya.shape[1]
    tm = 512
    nt = T // tm

    def body(ya_ref, yp_ref, wo_ref, x_ref, gt_ref, g_ref, t_ref, dx_ref, y_ref, l_ref, acc):
        i = pl.program_id(0)

        @pl.when(i == 0)
        def _():
            acc[...] = jnp.zeros_like(acc)

        y = (jnp.dot(ya_ref[...], wo_ref[0:H, :], preferred_element_type=F32)
             + jnp.dot(yp_ref[...], wo_ref[H:2 * H, :], preferred_element_type=F32))
        y_ref[...] = y.astype(BF16)
        d = (x_ref[...] + gt_ref[...] * (y * _rms(y) * g_ref[...])) - t_ref[...]
        dx_ref[...] = d * (1.0 / D)
        acc[...] += _colsum8(d * d)

        @pl.when(i == nt - 1)
        def _():
            l_ref[...] = jnp.zeros_like(l_ref) + jnp.sum(acc[...]) * (0.5 / D)

    tile = pl.BlockSpec((tm, D), lambda i: (i, 0))
    half = pl.BlockSpec((tm, H), lambda i: (i, 0))
    return pl.pallas_call(
        body, name="out_fwd_loss", grid=(nt,),
        in_specs=[half, half, pl.BlockSpec((2 * H, D), lambda i: (0, 0)), tile, _mod_row(l, 2, D), _layer_row(l, D),
                  tile],
        out_specs=[tile, tile, pl.BlockSpec((SUBLANES, LANES), lambda i: (0, 0))],
        out_shape=[jax.ShapeDtypeStruct((T, D), F32), jax.ShapeDtypeStruct((T, D), BF16),
                   jax.ShapeDtypeStruct((SUBLANES, LANES), F32)],
        scratch_shapes=[pltpu.VMEM((SUBLANES, D), F32)],
        compiler_params=_params(VMEM_BIG),
    )(ya, yp, wo, x, mod4, g_post3, target)


def out_bwd(dx, y, ya, yp, wo, mod4, g_post3, l, after):
    T, D = dx.shape
    H = ya.shape[1]
    tm = 512
    nt = T // tm

    def body(dx_ref, y_ref, ya_ref, yp_ref, wo_ref, gt_ref, g_ref, after_ref,
             dya_ref, dyp_ref, dwo_ref, dgt_ref, dg_ref, acc_w, acc_p):
        i = pl.program_id(0)

        @pl.when(i == 0)
        def _():
            acc_w[...] = jnp.zeros_like(acc_w)
            acc_p[...] = jnp.zeros_like(acc_p)

        yv = y_ref[...].astype(F32)
        dxv = dx_ref[...]
        gg = gt_ref[...] * g_ref[...]
        r = _rms(yv)
        yn = yv * r
        p = dxv * yn
        acc_p[...] += _colsum8(p)
        dy = r * (dxv * gg - yn * jnp.mean(p * gg, axis=-1, keepdims=True))
        dyb = dy.astype(BF16)
        dyc = lax.dot_general(dyb, wo_ref[...], NT, preferred_element_type=F32)
        dya_ref[...] = dyc[:, 0:H].astype(BF16)
        dyp_ref[...] = dyc[:, H:2 * H].astype(BF16)
        acc_w[0:H, :] += lax.dot_general(ya_ref[...], dyb, TN, preferred_element_type=F32)
        acc_w[H:2 * H, :] += lax.dot_general(yp_ref[...], dyb, TN, preferred_element_type=F32)

        @pl.when(i == nt - 1)
        def _():
            dwo_ref[...] = acc_w[...].astype(BF16)
            sp = jnp.sum(acc_p[...], axis=0, keepdims=True)
            dgt_ref[...] = g_ref[...] * sp
            dg_ref[...] = gt_ref[...] * sp

    row = pl.BlockSpec((1, D), lambda i: (0, 0))
    tile = pl.BlockSpec((tm, D), lambda i: (i, 0))
    half = pl.BlockSpec((tm, H), lambda i: (i, 0))
    full = pl.BlockSpec((2 * H, D), lambda i: (0, 0))
    return pl.pallas_call(
        body, name="out_bwd", grid=(nt,),
        in_specs=[tile, tile, half, half, full, _mod_row(l, 2, D), _layer_row(l, D), ANY],
        out_specs=[half, half, full, row, row],
        out_shape=[jax.ShapeDtypeStruct((T, H), BF16), jax.ShapeDtypeStruct((T, H), BF16),
                   jax.ShapeDtypeStruct((2 * H, D), BF16),
                   jax.ShapeDtypeStruct((1, D), F32), jax.ShapeDtypeStruct((1, D), F32)],
        scratch_shapes=[pltpu.VMEM((2 * H, D), F32), pltpu.VMEM((SUBLANES, D), F32)],
        compiler_params=_params(VMEM_BIG),
    )(dx, y, ya, yp, wo, mod4, g_post3, after)


def _conv_bwd_block(u_ref, b_ref, c_ref, g_ref, dy_ref, w_ref, du_ref, db_ref, dc_ref, dg_ref, dw_ref):
    T = u_ref.shape[0]
    R = R_CONV
    nchunk = T // R
    w0 = w_ref[pl.ds(0, 1), :]
    w1 = w_ref[pl.ds(1, 1), :]
    w2 = w_ref[pl.ds(2, 1), :]

    def chunk(k, carry):
        head, a0, a1, a2 = carry
        i = nchunk - 1 - k
        r0 = pl.multiple_of(i * R, R)
        h0 = pl.multiple_of(jnp.maximum(r0 - HIST, 0), HIST)
        first = i == 0
        ue = _load_ext(u_ref, r0, h0, first, R)
        ce = _load_ext(c_ref, r0, h0, first, R)
        ca = ce * ue
        ca0 = ca[HIST:]
        ca1 = _shift_down(ca, 1, R)
        ca2 = _shift_down(ca, 2, R)
        conv = w2 * ca0 + w1 * ca1 + w0 * ca2
        g = g_ref[pl.ds(r0, R), :].astype(F32)
        b = b_ref[pl.ds(r0, R), :].astype(F32)
        dy = dy_ref[pl.ds(r0, R), :].astype(F32)
        sg = _sigmoid(g)
        sl = g * sg
        t = dy * conv
        db_ref[pl.ds(r0, R), :] = (t * sl).astype(BF16)
        dg_ref[pl.ds(r0, R), :] = (t * b * (sg + sl * (1.0 - sg))).astype(BF16)
        dconv = dy * b * sl
        a2 = a2 + _colsum8(dconv * ca0)
        a1 = a1 + _colsum8(dconv * ca1)
        a0 = a0 + _colsum8(dconv * ca2)
        e = jnp.concatenate([dconv, head], axis=0)
        dca = w2 * dconv + w1 * _shift_up(e, 1, R) + w0 * _shift_up(e, 2, R)
        du_ref[pl.ds(r0, R), :] = (dca * ce[HIST:]).astype(BF16)
        dc_ref[pl.ds(r0, R), :] = (dca * ue[HIST:]).astype(BF16)
        return dconv[0:SUBLANES], a0, a1, a2

    z = jnp.zeros((SUBLANES, LANES), F32)
    _, a0, a1, a2 = lax.fori_loop(0, nchunk, chunk, (z, z, z, z))
    dw_ref[pl.ds(0, 1), :] = jnp.sum(a0, axis=0, keepdims=True)
    dw_ref[pl.ds(1, 1), :] = jnp.sum(a1, axis=0, keepdims=True)
    dw_ref[pl.ds(2, 1), :] = jnp.sum(a2, axis=0, keepdims=True)


def _pool_bwd_group(p_ref, g_ref, dy_ref, w_ref, s_ref, du_ref, dg_ref, dw_ref, ds_ref,
                    pooled_s, mixed_s, dmix_s, dpool_s, w):
    T = p_ref.shape[0]
    R = R_POOL
    nchunk = T // R
    wb = w_ref[...].astype(BF16)
    _pooled_loop(p_ref, pooled_s, w, T)
    mixed_s[...] = jnp.dot(pooled_s[...], wb, preferred_element_type=F32)
    sc = s_ref[...]

    def gate_chunk(i, acc):
        r0 = pl.multiple_of(i * R, R)
        g = g_ref[pl.ds(r0, R), :].astype(F32)
        dy = dy_ref[pl.ds(r0, R), :].astype(F32)
        mixed = mixed_s[pl.ds(r0, R), :]
        sg = _sigmoid(g)
        sl = g * sg
        dg_ref[pl.ds(r0, R), :] = (dy * mixed * sc * (sg + sl * (1.0 - sg))).astype(BF16)
        dms = dy * sl
        dmix_s[pl.ds(r0, R), :] = (dms * sc).astype(BF16)
        return acc + _colsum8(dms * mixed)

    acc = lax.fori_loop(0, nchunk, gate_chunk, jnp.zeros((SUBLANES, LANES), F32))
    ds_ref[...] = jnp.sum(acc, axis=0, keepdims=True)
    dpool_s[pl.ds(0, T), :] = lax.dot_general(dmix_s[...], wb, NT, preferred_element_type=F32)
    dpool_s[pl.ds(T, HIST), :] = jnp.zeros((HIST, LANES), F32)
    dw_ref[...] = lax.dot_general(pooled_s[...], dmix_s[...], TN, preferred_element_type=F32).astype(BF16)

    def back_chunk(i, carry):
        r0 = pl.multiple_of(i * R, R)
        dpe = dpool_s[pl.ds(r0, R + HIST), :]
        e = dpe / _count(r0, R + HIST, w)
        du_ref[pl.ds(r0, R), :] = (_anticausal_window_sum(e, w)[0:R] - dpe[0:R]).astype(BF16)
        return carry

    lax.fori_loop(0, nchunk, back_chunk, 0)


def mix_bwd(proj, dya, dyp, wconv, wpool, pscale3, l, after):
    T = proj.shape[0]

    def body(u_ref, b_ref, c_ref, g_ref, p_ref, gp_ref, dya_ref, dyp_ref, wc_ref, wp_ref, s_ref, after_ref,
             dua_ref, dba_ref, dca_ref, dga_ref, dup_ref, dgp_ref, dwc_ref, dwp_ref, ds_ref,
             pooled_s, mixed_s, dmix_s, dpool_s):
        j = pl.program_id(0)
        pl.when(j < N_MIX)(functools.partial(_conv_bwd_block, u_ref, b_ref, c_ref, g_ref, dya_ref, wc_ref,
                                             dua_ref, dba_ref, dca_ref, dga_ref, dwc_ref))
        for k, w in enumerate(POOL_WINDOWS):
            pl.when(j == N_MIX + k)(functools.partial(_pool_bwd_group, p_ref, gp_ref, dyp_ref, wp_ref, s_ref,
                                                      dup_ref, dgp_ref, dwp_ref, ds_ref,
                                                      pooled_s, mixed_s, dmix_s, dpool_s, w))

    sec = jax.ShapeDtypeStruct((T, N_MIX * LANES), BF16)
    conv_col = pl.BlockSpec((T, LANES), lambda j: (0, _conv_idx(j)))
    pool_col = pl.BlockSpec((T, LANES), lambda j: (0, _pool_idx(j)))
    return pl.pallas_call(
        body, name="mix_bwd", grid=(2 * N_MIX,),
        in_specs=[_proj_col(T, 0, _conv_idx), _proj_col(T, 4, _conv_idx), _proj_col(T, 8, _conv_idx),
                  _proj_col(T, 12, _conv_idx), _proj_col(T, 16, _pool_idx), _proj_col(T, 20, _pool_idx),
                  conv_col, pool_col, _conv_w_spec(l), _pool_w_spec(l), _pool_s_spec(l), ANY],
        out_specs=[conv_col, conv_col, conv_col, conv_col, pool_col, pool_col,
                   pl.BlockSpec((None, 3, LANES), lambda j: (_conv_idx(j), 0, 0)),
                   pl.BlockSpec((None, LANES, LANES), lambda j: (_pool_idx(j), 0, 0)),
                   pl.BlockSpec((1, LANES), lambda j: (0, _pool_idx(j)))],
        out_shape=[sec] * 6 + [jax.ShapeDtypeStruct((N_MIX, 3, LANES), F32),
                               jax.ShapeDtypeStruct((N_MIX, LANES, LANES), BF16),
                               jax.ShapeDtypeStruct((1, N_MIX * LANES), F32)],
        scratch_shapes=[pltpu.VMEM((T, LANES), BF16), pltpu.VMEM((T, LANES), F32),
                        pltpu.VMEM((T, LANES), BF16), pltpu.VMEM((T + HIST, LANES), F32)],
        compiler_params=_params(),
    )(proj, proj, proj, proj, proj, proj, dya, dyp, wconv, wpool, pscale3, after)


def in_bwd(dsecs, wg, x, dxo, mod4, g_pre3, l):
    T, D = x.shape
    NB = N_CHIPS
    CW = wg.shape[1] // NB
    SW = dsecs[0].shape[1]
    nsec = len(dsecs)
    PW = 256
    assert SW % PW == 0 and CW % PW == 0
    tm = 256
    nt = T // tm

    def body(*refs):
        d_refs = refs[0:nsec]
        w_ref, x_ref, dxo_ref, sh_ref, sc_ref, g_ref = refs[nsec:nsec + 6]
        dxi_ref, dw_ref, dsh_ref, dsc_ref, dg_ref = refs[nsec + 6:nsec + 11]
        acc_w, acc_sh, acc_q = refs[nsec + 11:]
        i = pl.program_id(0)

        @pl.when(i == 0)
        def _():
            acc_w[...] = jnp.zeros_like(acc_w)
            acc_sh[...] = jnp.zeros_like(acc_sh)
            acc_q[...] = jnp.zeros_like(acc_q)

        xv = x_ref[...]
        r = _rms(xv)
        xh = xv * r
        sg = g_ref[...] * (1.0 + sc_ref[...])
        hb = (xh * sg + sh_ref[...]).astype(BF16)
        dh = lax.dot_general(d_refs[0][...], w_ref[:, 0:SW], NT, preferred_element_type=F32)
        for s in range(1, nsec):
            dh = dh + lax.dot_general(d_refs[s][...], w_ref[:, s * SW:(s + 1) * SW], NT, preferred_element_type=F32)
        for p in range(nsec * SW // PW):
            col = p * PW
            s, so = col // SW, col % SW
            j, jo = col // CW, col % CW
            acc_w[j, :, jo:jo + PW] += lax.dot_general(hb, d_refs[s][:, so:so + PW], TN, preferred_element_type=F32)
        q = dh * xh
        acc_sh[...] += _colsum8(dh)
        acc_q[...] += _colsum8(q)
        dxi_ref[...] = dxo_ref[...] + r * (dh * sg - xh * jnp.mean(q * sg, axis=-1, keepdims=True))

        @pl.when(i == nt - 1)
        def _():
            dw_ref[...] = acc_w[...].astype(BF16)
            sq = jnp.sum(acc_q[...], axis=0, keepdims=True)
            dsh_ref[...] = jnp.sum(acc_sh[...], axis=0, keepdims=True)
            dsc_ref[...] = g_ref[...] * sq
            dg_ref[...] = (1.0 + sc_ref[...]) * sq

    row = pl.BlockSpec((1, D), lambda i: (0, 0))
    tile = pl.BlockSpec((tm, D), lambda i: (i, 0))
    sect = pl.BlockSpec((tm, SW), lambda i: (i, 0))
    rowshape = jax.ShapeDtypeStruct((1, D), F32)
    return pl.pallas_call(
        body, name="in_bwd", grid=(nt,),
        in_specs=[sect] * nsec + [pl.BlockSpec((D, NB * CW), lambda i: (0, 0)), tile, tile,
                                  _mod_row(l, 0, D), _mod_row(l, 1, D), _layer_row(l, D)],
        out_specs=[tile, pl.BlockSpec((NB, D, CW), lambda i: (0, 0, 0)), row, row, row],
        out_shape=[jax.ShapeDtypeStruct((T, D), F32), jax.ShapeDtypeStruct((NB, D, CW), BF16),
                   rowshape, rowshape, rowshape],
        scratch_shapes=[pltpu.VMEM((NB, D, CW), F32),
                        pltpu.VMEM((SUBLANES, D), F32), pltpu.VMEM((SUBLANES, D), F32)],
        compiler_params=_params(VMEM_BIG),
    )(*dsecs, wg, x, dxo, mod4, mod4, g_pre3)


def _rcopy(src, dst, ssem, rsem, dev):
    return pltpu.make_async_remote_copy(src_ref=src, dst_ref=dst, send_sem=ssem, recv_sem=rsem,
                                        device_id=dev, device_id_type=MESH)


def _peers7(x, y, c):
    out = []
    for m in range(1, N_DEV):
        bx, by, bc = (m >> 2) & 1, (m >> 1) & 1, m & 1
        out.append(((1 - x) if bx else x, (1 - y) if by else y, (1 - c) if bc else c))
    return out


HBM = pl.BlockSpec(memory_space=pltpu.HBM)
SEM = pl.BlockSpec(memory_space=pltpu.SEMAPHORE)
SPLIT = pltpu.CompilerParams(has_side_effects=pltpu.SideEffectType.DATAFLOW_SIDE_EFFECTING)


def _hbm(a):
    return pltpu.with_memory_space_constraint(a, pltpu.HBM)


def _chips(x, y):
    return [(1 - x, y), (x, 1 - y), (1 - x, 1 - y)]


SIBLING_BARRIER_ID = 0


def xchg_start(name, bufs, n_copies, plan, sibling_only=False, after=()):
    n = len(bufs)
    after = list(after)

    def body(*refs):
        ssem, rsem, token = refs[n + len(after)], refs[n + len(after) + 1], refs[-1]
        x, y, c = _me()
        if sibling_only:
            barrier = pltpu.get_barrier_semaphore()
            pl.semaphore_signal(barrier, inc=1, device_id=(x, y, 1 - c), device_id_type=MESH)
            pl.semaphore_wait(barrier, 1)
        copies = plan(refs[0:n], x, y, c)
        assert len(copies) == n_copies
        for k, (src, dst, peer, _) in enumerate(copies):
            _rcopy(src, dst, ssem.at[k], rsem.at[k], peer).start()
        token[...] = jnp.zeros_like(token)

    params = dict(has_side_effects=pltpu.SideEffectType.DATAFLOW_SIDE_EFFECTING)
    if sibling_only:
        params["collective_id"] = SIBLING_BARRIER_ID
    outs = pl.pallas_call(
        body, name=name,
        in_specs=[HBM] * n + [ANY] * len(after),
        out_specs=[SEM, SEM] + [HBM] * n + [pl.BlockSpec(memory_space=pltpu.VMEM)],
        out_shape=([pltpu.SemaphoreType.DMA((n_copies,))] * 2 + [pltpu.HBM(b.shape, b.dtype) for b in bufs]
                   + [jax.ShapeDtypeStruct((SUBLANES, LANES), F32)]),
        input_output_aliases={a: 2 + a for a in range(n)},
        compiler_params=pltpu.CompilerParams(**params),
    )(*[_hbm(b) for b in bufs], *after)
    return outs[0], outs[1], list(outs[2:2 + n]), outs[-1]


def xchg_wait(name, bufs, ssem, rsem, n_copies, plan, after, sems=None):
    n = len(bufs)
    after = list(after)
    sems = tuple(range(n_copies)) if sems is None else tuple(sems)
    assert len(sems) == n_copies

    def body(*refs):
        ssem_ref, rsem_ref = refs[n], refs[n + 1]
        copies = plan(refs[0:n], *_me())
        assert len(copies) == n_copies
        for k, (src, _, peer, land) in zip(sems, copies):
            cp = _rcopy(src, land, ssem_ref.at[k], rsem_ref.at[k], peer)
            cp.wait_send()
            cp.wait_recv()

    outs = pl.pallas_call(
        body, name=name,
        in_specs=[HBM] * n + [SEM, SEM] + [ANY] * len(after), out_specs=[HBM] * n,
        out_shape=[pltpu.HBM(b.shape, b.dtype) for b in bufs],
        input_output_aliases={a: a for a in range(n)},
        compiler_params=SPLIT,
    )(*bufs, ssem, rsem, *after)
    return list(outs)


def _shard_half(buf, chip, half):
    if len(buf.shape) == 2:
        h, w = buf.shape[0] // 2, buf.shape[1] // N_CHIPS
        return buf.at[pl.ds(half * h, h), pl.ds(chip * w, w)]
    h = buf.shape[1] // 2
    return buf.at[chip, pl.ds(half * h, h)]


def plan_gather(refs, x, y, c):
    out = []
    for (px, py) in _chips(x, y):
        for buf in refs:
            own = _shard_half(buf, 2 * x + y, c)
            out.append((own, own, (px, py, c), _shard_half(buf, 2 * px + py, c)))
    return out


def plan_forward(refs, x, y, c):
    out = []
    for (px, py) in _chips(x, y):
        for buf in refs:
            landed = _shard_half(buf, 2 * px + py, c)
            out.append((landed, landed, (x, y, 1 - c), _shard_half(buf, 2 * px + py, 1 - c)))
    return out


def plan_sibling(refs, x, y, c):
    n = len(refs) // 2
    out = []
    for a in range(n):
        h = refs[a].shape[1] // 2
        out.append((refs[a].at[:, pl.ds((1 - c) * h, h)], refs[n + a], (x, y, 1 - c), refs[n + a]))
    return out


def plan_chip(refs, x, y, c):
    n = len(refs) // 2
    out = []
    for j, (px, py) in enumerate(_chips(x, y)):
        for a in range(n):
            out.append((refs[a].at[2 * px + py], refs[n + a].at[j], (px, py, c), refs[n + a].at[j]))
    return out


def plan_mod(refs, x, y, c):
    (mods,) = refs
    mine = mods.at[2 * x + y]
    return [(mine, mine, (px, py, c), mods.at[2 * px + py]) for (px, py) in _chips(x, y)]


def plan_pack(refs, x, y, c):
    (packs,) = refs
    mine = packs.at[4 * x + 2 * y + c]
    return [(mine, mine, peer, packs.at[4 * peer[0] + 2 * peer[1] + peer[2]]) for peer in _peers7(x, y, c)]


def plan_spread(layers, wp_layers):
    def plan(refs, x, y, c):
        gi, go, gp = refs
        hD, hR, hP = gi.shape[1] // 2, go.shape[1] // 2, gp.shape[2] // 2
        sib = (x, y, 1 - c)
        out = []
        for l in layers:
            mine = gi.at[l, pl.ds(c * hD, hD)]
            out.append((mine, mine, sib, gi.at[l, pl.ds((1 - c) * hD, hD)]))
            mine = go.at[l, pl.ds(c * hR, hR)]
            out.append((mine, mine, sib, go.at[l, pl.ds((1 - c) * hR, hR)]))
        for l in wp_layers:
            mine = gp.at[l, 2 * x + y, pl.ds(c * hP, hP)]
            for peer in _peers7(x, y, c):
                out.append((mine, mine, peer, gp.at[l, 2 * peer[0] + peer[1], pl.ds(peer[2] * hP, hP)]))
        return out

    return plan


def gather_small(c8, wc, token):
    def body(c_ref, wc_ref, token_ref, call, wcall, ssem, rsem, lsem):
        x, y, c = _me()
        myc = 2 * x + y
        me_lin = 4 * x + 2 * y + c
        me = (x, y, c)
        local = [pltpu.make_async_copy(c_ref, call.at[me_lin], lsem.at[0]),
                 pltpu.make_async_copy(wc_ref, wcall.at[myc], lsem.at[1])]
        for cp in local:
            cp.start()
        sends, recvs = [], []
        for m, peer in enumerate(_peers7(x, y, c)):
            plin = 4 * peer[0] + 2 * peer[1] + peer[2]
            sends.append(_rcopy(c_ref, call.at[me_lin], ssem.at[m], rsem.at[m], peer))
            recvs.append(_rcopy(call.at[plin], call.at[plin], ssem.at[m], rsem.at[m], me))
        for j, (px, py) in enumerate([(1 - x, y), (x, 1 - y), (1 - x, 1 - y)]):
            pc = 2 * px + py
            sends.append(_rcopy(wc_ref, wcall.at[myc], ssem.at[7 + j], rsem.at[7 + j], (px, py, c)))
            recvs.append(_rcopy(wcall.at[pc], wcall.at[pc], ssem.at[7 + j], rsem.at[7 + j], me))
        for cp in sends:
            cp.start()
        for cp in recvs:
            cp.wait_recv()
        for cp in sends:
            cp.wait_send()
        for cp in local:
            cp.wait()

    return pl.pallas_call(
        body, name="gather_small",
        in_specs=[ANY] * 3, out_specs=[ANY] * 2,
        out_shape=[jax.ShapeDtypeStruct((N_DEV, SUBLANES, LANES), F32),
                   jax.ShapeDtypeStruct((N_CHIPS, wc.shape[0], 3, LANES), F32)],
        scratch_shapes=[pltpu.SemaphoreType.DMA((10,)), pltpu.SemaphoreType.DMA((10,)), pltpu.SemaphoreType.DMA((2,))],
        compiler_params=_params(n_grid=0),
    )(c8, wc, token)


def spread_now(gi, go, gp, layers, wp_layers):
    plan = plan_spread(layers, wp_layers)
    n = 2 * len(layers) + 7 * len(wp_layers)

    def body(gi_in, go_in, gp_in, gi, go, gp, ssem, rsem):
        copies = plan((gi, go, gp), *_me())
        me = _me()
        sends = [_rcopy(src, dst, ssem.at[k], rsem.at[k], peer) for k, (src, dst, peer, _) in enumerate(copies)]
        for cp in sends:
            cp.start()
        for k, (_, _, _, land) in enumerate(copies):
            _rcopy(land, land, ssem.at[k], rsem.at[k], me).wait_recv()
        for cp in sends:
            cp.wait_send()

    return pl.pallas_call(
        body, name="spread_now",
        in_specs=[ANY] * 3, out_specs=[ANY] * 3,
        out_shape=[jax.ShapeDtypeStruct(a.shape, a.dtype) for a in (gi, go, gp)],
        input_output_aliases={0: 0, 1: 1, 2: 2},
        scratch_shapes=[pltpu.SemaphoreType.DMA((n,)), pltpu.SemaphoreType.DMA((n,))],
        compiler_params=_params(n_grid=0),
    )(gi, go, gp)


def add_sibling(cidx, mine, sib):
    def body(c_ref, *refs):
        for a in range(3):
            m, s, o = refs[a], refs[3 + a], refs[6 + a]
            o[...] = (m[...].astype(F32) + s[...].astype(F32)).astype(BF16)

    per_step = 2

    def mine_spec(a):
        h = a.shape[1] // 2
        return pl.BlockSpec((per_step, h, a.shape[2]), lambda j, c_ref: (j, c_ref[0], 0))

    def sib_spec(a):
        return pl.BlockSpec((per_step,) + a.shape[1:], lambda j, c_ref: (j, 0, 0))

    return pl.pallas_call(
        body, name="add_sibling",
        grid_spec=pltpu.PrefetchScalarGridSpec(
            num_scalar_prefetch=1, grid=(N_CHIPS // per_step,),
            in_specs=[mine_spec(a) for a in mine] + [sib_spec(a) for a in sib],
            out_specs=[sib_spec(a) for a in sib]),
        out_shape=[jax.ShapeDtypeStruct(a.shape, BF16) for a in sib],
        compiler_params=_params(VMEM_BIG),
    )(cidx, *mine, *sib)


def sum_chips(pos, own, rb, acc, l, shapes):
    nq = 2
    n_in = 6 + (3 if acc is not None else 0)

    def body(pos_ref, *refs):
        for a in range(3):
            m, b, o = refs[a], refs[3 + a], refs[n_in + a]
            s = m[...].astype(F32)
            for j in range(3):
                s = s + b[j].astype(F32)
            o[...] = s

    def own_spec(a):
        return pl.BlockSpec((None, a.shape[1] // nq, a.shape[2]), lambda q, p: (p[1], q, 0))

    def rb_spec(a):
        return pl.BlockSpec((3, a.shape[1] // nq, a.shape[2]), lambda q, p: (0, q, 0))

    hi, ho, hp = own[0].shape[1] // nq, own[1].shape[1] // nq, own[2].shape[1] // nq
    out_specs = [pl.BlockSpec((None, hi, shapes[0][2]), lambda q, p: (l, p[0] * nq + q, 0)),
                 pl.BlockSpec((None, ho, shapes[1][2]), lambda q, p: (l, p[0] * nq + q, 0)),
                 pl.BlockSpec((None, None, hp, LANES), lambda q, p: (l, p[1], p[0] * nq + q, 0))]
    in_specs = [own_spec(a) for a in own] + [rb_spec(a) for a in rb]
    args = list(own) + list(rb)
    aliases = {}
    if acc is not None:
        in_specs += [ANY] * 3
        args += list(acc)
        aliases = {7: 0, 8: 1, 9: 2}
    return pl.pallas_call(
        body, name="sum_chips",
        grid_spec=pltpu.PrefetchScalarGridSpec(num_scalar_prefetch=1, grid=(nq,), in_specs=in_specs, out_specs=out_specs),
        out_shape=[jax.ShapeDtypeStruct(s, F32) for s in shapes],
        input_output_aliases=aliases,
        compiler_params=_params(VMEM_BIG),
    )(pos, *args)


def pack_small(pos, per_layer, loss_blk):
    L = len(per_layer)
    D = per_layer[0][0].shape[1]

    def body(pos_ref, *refs):
        o = refs[-1]
        lb = refs[-2]
        o[...] = jnp.zeros_like(o)
        for l in range(L):
            dgpre, dgpost, dsh, dsc, dgt, dps, dwc = refs[7 * l:7 * l + 7]
            base = SUBLANES * l
            for r, src in enumerate((dgpre, dgpost, dsh, dsc, dgt)):
                o[pl.ds(base + r, 1), :] = src[...]
            o[pl.ds(base + 5, 1), 0:dps.shape[1]] = dps[...]
            for j in range(dwc.shape[0]):
                for k in range(3):
                    idx = 3 * j + k
                    o[pl.ds(base + 6 + idx // 8, 1), (idx % 8) * LANES:(idx % 8 + 1) * LANES] = dwc[j, pl.ds(k, 1), :]
        o[pl.ds(5, 1), 4 * LANES:5 * LANES] = lb[pl.ds(0, 1), :]

    flat = [a for layer in per_layer for a in layer] + [loss_blk]

    def whole(a):
        return pl.BlockSpec(a.shape, lambda i, p: (0,) * a.ndim)

    return pl.pallas_call(
        body, name="pack_small",
        grid_spec=pltpu.PrefetchScalarGridSpec(
            num_scalar_prefetch=1, grid=(1,), in_specs=[whole(a) for a in flat],
            out_specs=pl.BlockSpec((None, L * SUBLANES, D), lambda i, p: (p[2], 0, 0))),
        out_shape=jax.ShapeDtypeStruct((N_DEV, L * SUBLANES, D), F32),
        compiler_params=_params(),
    )(pos, *flat)


def small_update(pos, packs, params, moments_m, moments_v):
    n = len(params)
    L, D = params[1].shape
    PS = params[3].shape[1]

    def body(pos_ref, p_ref, *refs):
        ws, ms, vs = refs[0:n], refs[n:2 * n], refs[2 * n:3 * n]
        loss_ref = refs[3 * n]
        outs = [refs[3 * n + 1 + 4 * t:3 * n + 5 + 4 * t] for t in range(n)]
        summed = refs[-1]
        s = p_ref[0]
        for d in range(1, N_DEV):
            s = s + p_ref[d]
        summed[...] = s
        loss_ref[...] = summed[pl.ds(5, 1), 4 * LANES:5 * LANES]
        chip = pos_ref[1]

        def update(t, idx, g):
            d, mm, vv = _adamw_math(ws[t][idx], g, ms[t][idx], vs[t][idx])
            g_ref, d_ref, mo_ref, vo_ref = outs[t]
            g_ref[idx] = g
            d_ref[idx] = d
            mo_ref[idx] = mm
            vo_ref[idx] = vv

        for l in range(L):
            base = SUBLANES * l
            row = pl.ds(l, 1)
            for k in range(3):
                update(0, (row, slice(k * D, (k + 1) * D)), summed[pl.ds(base + 2 + k, 1), :])
            update(1, (row, slice(None)), summed[pl.ds(base, 1), :])
            update(2, (row, slice(None)), summed[pl.ds(base + 1, 1), :])
            update(3, (row, slice(None)), summed[pl.ds(base + 5, 1), 0:PS])
            for k in range(3):
                g = None
                for j in range(N_CHIPS):
                    idx = 3 * j + k
                    cand = summed[pl.ds(base + 6 + idx // 8, 1), (idx % 8) * LANES:(idx % 8 + 1) * LANES]
                    g = cand if g is None else jnp.where(chip == j, cand, g)
                update(4, (l, pl.ds(k, 1), slice(None)), g)

    def whole(a):
        return pl.BlockSpec(a.shape, lambda i, p: (0,) * a.ndim)

    ins = [packs] + list(params) + list(moments_m) + list(moments_v)
    out_shape = [jax.ShapeDtypeStruct((1, LANES), F32)]
    for w in params:
        out_shape += [jax.ShapeDtypeStruct(w.shape, F32)] * 4
    outs = pl.pallas_call(
        body, name="small_update",
        grid_spec=pltpu.PrefetchScalarGridSpec(
            num_scalar_prefetch=1, grid=(1,), in_specs=[whole(a) for a in ins],
            out_specs=[whole(a) for a in out_shape],
            scratch_shapes=[pltpu.VMEM(packs.shape[1:], F32)]),
        out_shape=out_shape,
        compiler_params=_params(),
    )(pos, *ins)
    return outs[0], [outs[1 + 4 * t:5 + 4 * t] for t in range(n)]


def _adamw_math(w, g, m, v):
    m = ADAM_B1 * m + (1.0 - ADAM_B1) * g
    v = ADAM_B2 * v + (1.0 - ADAM_B2) * (g * g)
    m_hat = m / (1.0 - ADAM_B1 ** ADAM_STEP)
    v_hat = v / (1.0 - ADAM_B2 ** ADAM_STEP)
    delta = -ADAM_LR * (m_hat / (jnp.sqrt(v_hat) + ADAM_EPS) + ADAM_WD * w)
    return delta, m, v


def adamw(w, g, m, v, block, name, first=0, count=None, acc=None):
    grid = tuple(s // b for s, b in zip(w.shape, block))
    if count is not None:
        grid = (count,) + grid[1:]

    def body(w_ref, g_ref, m_ref, v_ref, *rest):
        go_ref, d_ref, mo_ref, vo_ref = rest[-4:]
        gv = g_ref[...]
        d, mm, vv = _adamw_math(w_ref[...], gv, m_ref[...], v_ref[...])
        go_ref[...] = gv
        d_ref[...] = d
        mo_ref[...] = mm
        vo_ref[...] = vv

    spec = pl.BlockSpec(block, lambda i, *rest: (first + i,) + rest)
    shape = jax.ShapeDtypeStruct(w.shape, F32)
    extra = [] if acc is None else list(acc)
    return pl.pallas_call(
        body, name=name, grid=grid,
        in_specs=[spec] * 4 + [ANY] * len(extra), out_specs=[spec] * 4, out_shape=[shape] * 4,
        input_output_aliases={4 + a: a for a in range(len(extra))},
        compiler_params=_params(VMEM_BIG, n_grid=len(grid)),
    )(w, g, m, v, *extra)


def ada_finish(c_all, dmod, w, m, v):
    L, D, CW = w.shape
    hD = D // 2

    def body(c_ref, d_ref, w_ref, m_ref, v_ref, g_ref, dl_ref, mo_ref, vo_ref):
        cv = c_ref[...]
        z = jnp.zeros_like(cv)
        ca = jnp.concatenate([cv * jax.nn.sigmoid(cv), z], axis=0).astype(BF16)
        dm = jnp.concatenate([d_ref[0], jnp.zeros_like(d_ref[0])], axis=0).astype(BF16)
        g = lax.dot_general(ca, dm, TN, preferred_element_type=F32)
        g_ref[0] = g
        d, mm, vv = _adamw_math(w_ref[0], g, m_ref[0], v_ref[0])
        dl_ref[0] = d
        mo_ref[0] = mm
        vo_ref[0] = vv

    big = pl.BlockSpec((1, hD, CW), lambda l, h: (l, h, 0))
    shape = jax.ShapeDtypeStruct(w.shape, F32)
    return pl.pallas_call(
        body, name="ada_finish", grid=(L, 2),
        in_specs=[pl.BlockSpec((N_DEV, hD), lambda l, h: (0, h)), pl.BlockSpec((1, N_DEV, CW), lambda l, h: (l, 0, 0)),
                  big, big, big],
        out_specs=[big] * 4, out_shape=[shape] * 4,
        compiler_params=_params(VMEM_BIG, n_grid=2),
    )(c_all, dmod, w, m, v)


def kernel(x, c, w_ada, b_ada, g_pre, w_in, w_conv, w_pool, pool_scale, w_out, g_post, loss_target, m_w_ada, m_b_ada, m_g_pre, m_w_in, m_w_conv, m_w_pool, m_pool_scale, m_w_out, m_g_post, v_w_ada, v_b_ada, v_g_pre, v_w_in, v_w_conv, v_w_pool, v_pool_scale, v_w_out, v_g_post):
    L, D, CW = w_in.shape
    RO = w_out.shape[1]
    T = x.shape[1]
    ix, iy, ic = _me()
    chip = 2 * ix + iy
    me_lin = 4 * ix + 2 * iy + ic

    pos = jnp.stack([ic, chip, me_lin]).astype(jnp.int32)
    g_pre3, g_post3 = g_pre.reshape(L, 1, D), g_post.reshape(L, 1, D)
    pscale3 = pool_scale.reshape(L, 1, pool_scale.shape[1])
    n_s, n_c = 3, 9

    def gather(bufs, after):
        ss, rs, bufs, tok = xchg_start("gather_start", bufs, 3 * len(bufs), plan_gather, after=after)
        return (ss, rs, bufs), tok

    def arrive(flight, after):
        ss, rs, bufs = flight
        bufs = xchg_wait("gather_wait", bufs, ss, rs, 3 * len(bufs), plan_gather, after)
        fss, frs, bufs, tok = xchg_start("forward_start", bufs, 3 * len(bufs), plan_forward, sibling_only=True)
        return (fss, frs, bufs), tok

    def ready(flight, after):
        fss, frs, bufs = flight
        return xchg_wait("forward_wait", bufs, fss, frs, 3 * len(bufs), plan_forward, after)

    def arrive_part(flight, which, after):
        ss, rs, bufs = flight
        sems = tuple(range(which, 3 * len(bufs), len(bufs)))
        (buf,) = xchg_wait("gather_wait", [bufs[which]], ss, rs, 3, plan_gather, after, sems=sems)
        fss, frs, (buf,), tok = xchg_start("forward_start", [buf], 3, plan_forward, sibling_only=True)
        return (fss, frs, [buf]), tok

    c_all3, wconv_all = gather_small(c.reshape(SUBLANES, LANES), w_conv, pos)
    c_all = c_all3.reshape(N_DEV, D)
    gi0, go0 = cast_weights(pos, w_in, w_out, 0, c_all3)
    fly_in0, token = gather([gi0], [])
    b_my = lax.dynamic_slice_in_dim(b_ada, chip * CW, CW, axis=1)
    m_ss, m_rs, mods, token = xchg_start("mod_start", [mod_part(pos, c_all, w_ada, b_my, token)], 3, plan_mod)
    fly_out0, token = gather([go0], [token])
    flying_w = [None] * L
    for l in range(1, L):
        flying_w[l], token = gather(list(cast_weights(pos, w_in, w_out, l, token)), [])
    fwd_in0, token = arrive(fly_in0, [token])
    (mod_all,) = xchg_wait("mod_wait", mods, m_ss, m_rs, 3, plan_mod, [token])
    mod = lax.dynamic_index_in_dim(mod_all, me_lin, axis=2, keepdims=False)
    mod4 = jnp.transpose(mod, (1, 0, 2)).reshape(L, 3, 1, D)

    xs, projs, yas, yps, ys = [x.reshape(T, D)], [], [], [], []
    wg_in, wg_out = [], []
    fwd_in = fwd_in0
    for l in range(L):
        (gi,) = ready(fwd_in, [mod4 if l == 0 else xs[l]])
        proj = proj_fwd(xs[l], mod4, g_pre3, gi, l)
        ya, yp = mix_fwd(proj, wconv_all, w_pool, pscale3, l)
        after = [ya, yp]
        if l == 0:
            fwd_out, token = arrive(fly_out0, after)
        else:
            fwd_out, token = arrive_part(flying_w[l], 1, after)
        after = [token]
        if l + 1 < L:
            fwd_in, token = arrive_part(flying_w[l + 1], 0, after)
            after = [token]
        (go,) = ready(fwd_out, after)
        wg_in.append(gi)
        wg_out.append(go.reshape(N_CHIPS * RO, D))
        projs.append(proj)
        yas.append(ya)
        yps.append(yp)
        if l + 1 < L:
            xn, yv = out_fwd(ya, yp, wg_out[l], xs[l], mod4, g_post3, l, after[0])
            xs.append(xn)
        else:
            dx, yv, loss_blk = out_fwd_loss(ya, yp, wg_out[l], xs[l], mod4, g_post3, l, loss_target.reshape(T, D))
        ys.append(yv)

    shapes = (w_in.shape, w_out.shape, w_pool.shape)
    smalls = [None] * L
    acc, flying, sib, token = None, None, None, loss_blk

    def to_chips(sib, after):
        sl, s_ss, s_rs, s_bufs = sib
        s_bufs = xchg_wait("sibling_wait", s_bufs, s_ss, s_rs, n_s, plan_sibling, after)
        chip_parts = add_sibling(pos, s_bufs[0:3], s_bufs[3:6])
        lands = [lax.empty((3,) + a.shape[1:], a.dtype) for a in chip_parts]
        c_ss, c_rs, c_bufs, ctoken = xchg_start("chip_start", list(chip_parts) + lands, n_c, plan_chip)
        return (sl, c_ss, c_rs, c_bufs), ctoken

    def landed(flying, acc, after):
        fl, f_ss, f_rs, f_bufs = flying
        f_bufs = xchg_wait("chip_wait", f_bufs, f_ss, f_rs, n_c, plan_chip, after)
        return sum_chips(pos, f_bufs[0:3], f_bufs[3:6], acc, fl, shapes)

    for l in reversed(range(L)):
        dya, dyp, dwo_l, dgate, dgpost = out_bwd(dx, ys[l], yas[l], yps[l], wg_out[l], mod4, g_post3, l, token)
        token = dya
        if sib is not None:
            arrived = flying
            flying, token = to_chips(sib, [dya])
            if arrived is not None:
                acc = landed(arrived, acc, [token])
                token = acc[0]
        du_a, db_a, dc_a, dg_a, du_p, dg_p, dwc, dwp_l, dps = mix_bwd(projs[l], dya, dyp, wconv_all, w_pool, pscale3, l,
                                                                        token)
        dx, dwi_l, dshift, dscale, dgpre = in_bwd([du_a, db_a, dc_a, dg_a, du_p, dg_p], wg_in[l], xs[l], dx,
                                                  mod4, g_pre3, l)
        smalls[l] = (dgpre, dgpost, dshift, dscale, dgate, dps, dwc)
        parts = [dwi_l, dwo_l.reshape(N_CHIPS, RO, D), dwp_l]
        s_lands = [lax.empty((a.shape[0], a.shape[1] // 2) + a.shape[2:], a.dtype) for a in parts]
        s_ss, s_rs, s_bufs, token = xchg_start("sibling_start", parts + s_lands, n_s, plan_sibling, sibling_only=True)
        sib = (l, s_ss, s_rs, s_bufs)
    grad_x = dx.reshape(1, T, D)

    p_ss, p_rs, packs, ptoken = xchg_start("pack_start", [pack_small(pos, smalls, loss_blk)], N_DEV - 1, plan_pack)
    acc = landed(flying, acc, [ptoken, token])
    n_sp = (2 + N_DEV - 1) * (L - 1)
    spread = plan_spread(tuple(range(1, L)), tuple(range(1, L)))
    sp_ss, sp_rs, acc, sp_token = xchg_start("spread_start", list(acc), n_sp, spread)
    flying, token = to_chips(sib, [sp_token])
    (packs_all,) = xchg_wait("pack_wait", packs, p_ss, p_rs, N_DEV - 1, plan_pack, [token])
    dmod_all = packs_all.reshape(N_DEV, L, SUBLANES, D)[:, :, 2:5].reshape(N_DEV, L, 3 * D)
    dmod_my = jnp.transpose(lax.dynamic_slice_in_dim(dmod_all, chip * CW, CW, axis=2), (1, 0, 2))

    g_w_ada, d_w_ada, nm_w_ada, nv_w_ada = ada_finish(c_all, dmod_my, w_ada, m_w_ada, v_w_ada)
    loss_row, upd = small_update(pos, packs_all, [b_ada, g_pre, g_post, pool_scale, w_conv],
                                 [m_b_ada, m_g_pre, m_g_post, m_pool_scale, m_w_conv],
                                 [v_b_ada, v_g_pre, v_g_post, v_pool_scale, v_w_conv])
    loss = loss_row[0, 0]
    (g_b_ada, d_b_ada, nm_b_ada, nv_b_ada), (g_g_pre, d_g_pre, nm_g_pre, nv_g_pre) = upd[0], upd[1]
    (g_g_post, d_g_post, nm_g_post, nv_g_post), (g_pscale, d_pscale, nm_pscale, nv_pscale) = upd[2], upd[3]
    g_w_conv, d_w_conv, nm_w_conv, nv_w_conv = upd[4]

    done = [nv_w_ada, nv_w_conv]
    g_w_in, g_w_out, g_w_pool = xchg_wait("spread_wait", acc, sp_ss, sp_rs, n_sp, spread, done)
    in_blk, out_blk = (1, D // 2, CW), (1, RO, D)
    upd_in = adamw(w_in, g_w_in, m_w_in, v_w_in, in_blk, "adamw_w_in", 1, L - 1)
    upd_out = adamw(w_out, g_w_out, m_w_out, v_w_out, out_blk, "adamw_w_out", 1, L - 1)

    acc = landed(flying, (g_w_in, g_w_out, g_w_pool), [upd_in[3], upd_out[3]])
    r_w_in, r_w_out, r_w_pool = spread_now(*acc, (0,), (0,))
    g_w_in, d_w_in, nm_w_in, nv_w_in = adamw(w_in, r_w_in, m_w_in, v_w_in, in_blk, "adamw_w_in", 0, 1, upd_in)
    g_w_out, d_w_out, nm_w_out, nv_w_out = adamw(w_out, r_w_out, m_w_out, v_w_out, out_blk, "adamw_w_out", 0, 1, upd_out)
    pshape = (L, N_CHIPS * LANES, LANES)
    upd_pool = adamw(w_pool.reshape(pshape), r_w_pool.reshape(pshape), m_w_pool.reshape(pshape),
                     v_w_pool.reshape(pshape), (1,) + pshape[1:], "adamw_w_pool")
    g_w_pool, d_w_pool, nm_w_pool, nv_w_pool = [a.reshape(w_pool.shape) for a in upd_pool]

    return (loss, grad_x,
            g_w_ada, g_b_ada, g_g_pre, g_w_in, g_w_conv, g_w_pool, g_pscale, g_w_out, g_g_post,
            d_w_ada, d_b_ada, d_g_pre, d_w_in, d_w_conv, d_w_pool, d_pscale, d_w_out, d_g_post,
            nm_w_ada, nm_b_ada, nm_g_pre, nm_w_in, nm_w_conv, nm_w_pool, nm_pscale, nm_w_out, nm_g_post,
            nv_w_ada, nv_b_ada, nv_g_pre, nv_w_in, nv_w_conv, nv_w_pool, nv_pscale, nv_w_out, nv_g_post)
```

```python
import functools

import jax
import jax.numpy as jnp
from jax import lax
from jax.experimental import pallas as pl
from jax.experimental.pallas import tpu as pltpu

F32 = jnp.float32
BF16 = jnp.bfloat16
MESH = pl.DeviceIdType.MESH
ANY = pl.BlockSpec(memory_space=pl.ANY)

NORM_EPS = 1e-6
POOL_WINDOWS = (2, 4, 8, 16)
ADAM_LR = 0.001
ADAM_B1 = 0.9
ADAM_B2 = 0.999
ADAM_EPS = 1e-08
ADAM_WD = 0.01
ADAM_STEP = 10

N_CHIPS = 4
N_DEV = 8
LANES = 128
SUBLANES = 8
VMEM_BIG = 56 * 1024 * 1024
HIST = 16
R_CONV = 64
R_POOL = 128

NT = (((1,), (1,)), ((), ()))
TN = (((0,), (0,)), ((), ()))


def _params(vmem=None, n_grid=1):
    kw = {}
    if n_grid:
        kw["dimension_semantics"] = ("arbitrary",) * n_grid
    if vmem is not None:
        kw["vmem_limit_bytes"] = vmem
    return pltpu.CompilerParams(**kw)


def _colsum8(v):
    n, d = v.shape
    return v.reshape(n // SUBLANES, SUBLANES, d).sum(axis=0)


def _rms(v):
    return lax.rsqrt(jnp.mean(v * v, axis=-1, keepdims=True) + NORM_EPS)


def _sigmoid(v):
    return 0.5 * jnp.tanh(0.5 * v) + 0.5


def _shift_down(ext, k, rows):
    if k == 0:
        return ext[HIST:HIST + rows]
    return pltpu.roll(ext, k, 0)[HIST:HIST + rows]


def _shift_up(ext, k, rows):
    if k == 0:
        return ext[0:rows]
    return pltpu.roll(ext, ext.shape[0] - k, 0)[0:rows]


def _load_ext(ref, r0, h0, first, rows):
    hist = ref[pl.ds(h0, HIST), :].astype(F32)
    hist = jnp.where(first, 0.0, hist)
    cur = ref[pl.ds(r0, rows), :].astype(F32)
    return jnp.concatenate([hist, cur], axis=0)


def _me():
    return lax.axis_index("x"), lax.axis_index("y"), lax.axis_index("c")


def cast_weights(pos, w_in, w_out, l, after):
    _, D, CW = w_in.shape
    RO = w_out.shape[1]

    def body(pos_ref, wi, wo, after_ref, oi, oo):
        oi[...] = wi[...].astype(BF16)
        oo[...] = wo[...].astype(BF16)

    return pl.pallas_call(
        body, name="cast_w",
        grid_spec=pltpu.PrefetchScalarGridSpec(
            num_scalar_prefetch=1, grid=(2,),
            in_specs=[pl.BlockSpec((None, D // 2, CW), lambda h, p: (l, h, 0)),
                      pl.BlockSpec((None, RO // 2, D), lambda h, p: (l, h, 0)), ANY],
            out_specs=[pl.BlockSpec((D // 2, CW), lambda h, p: (h, p[1])),
                       pl.BlockSpec((None, RO // 2, D), lambda h, p: (p[1], h, 0))]),
        out_shape=[jax.ShapeDtypeStruct((D, N_CHIPS * CW), BF16), jax.ShapeDtypeStruct((N_CHIPS, RO, D), BF16)],
        compiler_params=_params(),
    )(pos, w_in, w_out, after)


def mod_part(pos, c_all, w_ada, b_my, after):
    L, D, CW = w_ada.shape

    def body(pos_ref, c_ref, w_ref, b_ref, after_ref, o_ref):
        cv = c_ref[...]
        ca = (cv * jax.nn.sigmoid(cv)).astype(BF16)
        o_ref[...] = jnp.dot(ca, w_ref[0].astype(BF16), preferred_element_type=F32) + b_ref[0]

    return pl.pallas_call(
        body, name="mod_part",
        grid_spec=pltpu.PrefetchScalarGridSpec(
            num_scalar_prefetch=1, grid=(L,),
            in_specs=[pl.BlockSpec((N_DEV, D), lambda l, p: (0, 0)),
                      pl.BlockSpec((1, D, CW), lambda l, p: (l, 0, 0)),
                      pl.BlockSpec((1, 1, CW), lambda l, p: (l, 0, 0)), ANY],
            out_specs=pl.BlockSpec((None, None, N_DEV, CW), lambda l, p: (p[1], l, 0, 0))),
        out_shape=jax.ShapeDtypeStruct((N_CHIPS, L, N_DEV, CW), F32),
        compiler_params=_params(VMEM_BIG),
    )(pos, c_all, w_ada, b_my.reshape(L, 1, CW), after)


def _mod_row(l, k, D):
    return pl.BlockSpec((None, None, 1, D), lambda *_: (l, k, 0, 0))


def _layer_row(l, D):
    return pl.BlockSpec((None, 1, D), lambda *_: (l, 0, 0))


def proj_fwd(x, mod4, g_pre3, wg, l):
    T, D = x.shape
    NC = wg.shape[1]
    NB = N_CHIPS
    CW = NC // NB
    tm = 512

    def body(x_ref, sh_ref, sc_ref, g_ref, w_ref, o_ref):
        xv = x_ref[...]
        h = (xv * _rms(xv) * g_ref[...]) * (1.0 + sc_ref[...]) + sh_ref[...]
        hb = h.astype(BF16)
        for j in range(NB):
            cols = slice(j * CW, (j + 1) * CW)
            o_ref[:, cols] = jnp.dot(hb, w_ref[:, cols], preferred_element_type=F32).astype(BF16)

    return pl.pallas_call(
        body, name="proj_fwd", grid=(T // tm,),
        in_specs=[pl.BlockSpec((tm, D), lambda i: (i, 0)), _mod_row(l, 0, D), _mod_row(l, 1, D), _layer_row(l, D),
                  pl.BlockSpec((D, NC), lambda i: (0, 0))],
        out_specs=pl.BlockSpec((tm, NC), lambda i: (i, 0)),
        out_shape=jax.ShapeDtypeStruct((T, NC), BF16),
        compiler_params=_params(VMEM_BIG),
    )(x, mod4, mod4, g_pre3, wg)


N_MIX = 4


def _conv_fwd_block(u_ref, b_ref, c_ref, g_ref, w_ref, o_ref):
    T = u_ref.shape[0]
    R = 2 * R_CONV
    w0 = w_ref[pl.ds(0, 1), :]
    w1 = w_ref[pl.ds(1, 1), :]
    w2 = w_ref[pl.ds(2, 1), :]

    def chunk(i, carry):
        r0 = pl.multiple_of(i * R, R)
        h0 = pl.multiple_of(jnp.maximum(r0 - HIST, 0), HIST)
        first = i == 0
        ca = _load_ext(c_ref, r0, h0, first, R) * _load_ext(u_ref, r0, h0, first, R)
        conv = w2 * ca[HIST:] + w1 * _shift_down(ca, 1, R) + w0 * _shift_down(ca, 2, R)
        g = g_ref[pl.ds(r0, R), :].astype(F32)
        b = b_ref[pl.ds(r0, R), :].astype(F32)
        o_ref[pl.ds(r0, R), :] = (b * conv * (g * _sigmoid(g))).astype(BF16)
        return carry

    lax.fori_loop(0, T // R, chunk, 0)


def _conv_idx(j):
    return jnp.minimum(j, N_MIX - 1)


def _pool_idx(j):
    return jnp.maximum(j - N_MIX, 0)


def _proj_col(T, off, idx):
    return pl.BlockSpec((T, LANES), lambda j: (0, idx(j) + off))


def _causal_window_sum(ext, w):
    s, k = ext, 1
    while k < w:
        s = s + pltpu.roll(s, k, 0)
        k *= 2
    return s


def _anticausal_window_sum(ext, w):
    s, k = ext, 1
    n = ext.shape[0]
    while k < w:
        s = s + pltpu.roll(s, n - k, 0)
        k *= 2
    return s


def _count(r0, rows, w):
    t = r0 + lax.broadcasted_iota(jnp.int32, (rows, LANES), 0)
    return jnp.minimum(t + 1, w).astype(F32)


def _pooled_loop(p_ref, pooled_s, w, T):
    R = R_POOL

    def chunk(i, carry):
        r0 = pl.multiple_of(i * R, R)
        h0 = pl.multiple_of(jnp.maximum(r0 - HIST, 0), HIST)
        ext = _load_ext(p_ref, r0, h0, i == 0, R)
        ws = _causal_window_sum(ext, w)[HIST:]
        pooled_s[pl.ds(r0, R), :] = (ws / _count(r0, R, w) - ext[HIST:]).astype(BF16)
        return carry

    lax.fori_loop(0, T // R, chunk, 0)


def _conv_w_spec(l):
    return pl.BlockSpec((None, None, 3, LANES), lambda j: (_conv_idx(j), l, 0, 0))


def _pool_w_spec(l):
    return pl.BlockSpec((None, None, LANES, LANES), lambda j: (l, _pool_idx(j), 0, 0))


def _pool_s_spec(l):
    return pl.BlockSpec((None, 1, LANES), lambda j: (l, 0, _pool_idx(j)))


def _pool_fwd_group(p_ref, g_ref, w_ref, s_ref, o_ref, pooled_s, mixed_s, w):
    T = p_ref.shape[0]
    R = R_POOL
    _pooled_loop(p_ref, pooled_s, w, T)
    mixed_s[...] = jnp.dot(pooled_s[...], w_ref[...].astype(BF16), preferred_element_type=F32)
    sc = s_ref[...]

    def chunk(i, carry):
        r0 = pl.multiple_of(i * R, R)
        g = g_ref[pl.ds(r0, R), :].astype(F32)
        o_ref[pl.ds(r0, R), :] = (mixed_s[pl.ds(r0, R), :] * sc * (g * _sigmoid(g))).astype(BF16)
        return carry

    lax.fori_loop(0, T // R, chunk, 0)


def mix_fwd(proj, wconv, wpool, pscale3, l):
    T = proj.shape[0]

    def body(u_ref, b_ref, c_ref, g_ref, p_ref, gp_ref, wc_ref, wp_ref, s_ref, ya_ref, yp_ref, pooled_ref, mixed_s):
        j = pl.program_id(0)
        pl.when(j < N_MIX)(functools.partial(_conv_fwd_block, u_ref, b_ref, c_ref, g_ref, wc_ref, ya_ref))
        for k, w in enumerate(POOL_WINDOWS):
            pl.when(j == N_MIX + k)(functools.partial(_pool_fwd_group, p_ref, gp_ref, wp_ref, s_ref, yp_ref,
                                                      pooled_ref, mixed_s, w))

    half = jax.ShapeDtypeStruct((T, N_MIX * LANES), BF16)
    pool_col = pl.BlockSpec((T, LANES), lambda j: (0, _pool_idx(j)))
    return pl.pallas_call(
        body, name="mix_fwd", grid=(2 * N_MIX,),
        in_specs=[_proj_col(T, 0, _conv_idx), _proj_col(T, 4, _conv_idx), _proj_col(T, 8, _conv_idx),
                  _proj_col(T, 12, _conv_idx), _proj_col(T, 16, _pool_idx), _proj_col(T, 20, _pool_idx),
                  _conv_w_spec(l), _pool_w_spec(l), _pool_s_spec(l)],
        out_specs=[pl.BlockSpec((T, LANES), lambda j: (0, _conv_idx(j))), pool_col, pool_col],
        out_shape=[half, half, half],
        scratch_shapes=[pltpu.VMEM((T, LANES), F32)],
        compiler_params=_params(),
    )(proj, proj, proj, proj, proj, proj, wconv, wpool, pscale3)


def out_fwd(ya, yp, wo, x, mod4, g_post3, l, after):
    T, D = x.shape
    H = ya.shape[1]
    tm = 512

    def body(ya_ref, yp_ref, wo_ref, x_ref, gt_ref, g_ref, after_ref, xn_ref, y_ref):
        y = (jnp.dot(ya_ref[...], wo_ref[0:H, :], preferred_element_type=F32)
             + jnp.dot(yp_ref[...], wo_ref[H:2 * H, :], preferred_element_type=F32))
        xn_ref[...] = x_ref[...] + gt_ref[...] * (y * _rms(y) * g_ref[...])
        y_ref[...] = y.astype(BF16)

    tile = pl.BlockSpec((tm, D), lambda i: (i, 0))
    half = pl.BlockSpec((tm, H), lambda i: (i, 0))
    return pl.pallas_call(
        body, name="out_fwd", grid=(T // tm,),
        in_specs=[half, half, pl.BlockSpec((2 * H, D), lambda i: (0, 0)), tile, _mod_row(l, 2, D), _layer_row(l, D),
                  ANY],
        out_specs=[tile, tile],
        out_shape=[jax.ShapeDtypeStruct((T, D), F32), jax.ShapeDtypeStruct((T, D), BF16)],
        compiler_params=_params(VMEM_BIG),
    )(ya, yp, wo, x, mod4, g_post3, after)


def out_fwd_loss(ya, yp, wo, x, mod4, g_post3, l, target):
    T, D = x.shape
    H = ya.shape[1]
    tm = 512
    nt = T // tm

    def body(ya_ref, yp_ref, wo_ref, x_ref, gt_ref, g_ref, t_ref, dx_ref, y_ref, l_ref, acc):
        i = pl.program_id(0)

        @pl.when(i == 0)
        def _():
            acc[...] = jnp.zeros_like(acc)

        y = (jnp.dot(ya_ref[...], wo_ref[0:H, :], preferred_element_type=F32)
             + jnp.dot(yp_ref[...], wo_ref[H:2 * H, :], preferred_element_type=F32))
        y_ref[...] = y.astype(BF16)
        d = (x_ref[...] + gt_ref[...] * (y * _rms(y) * g_ref[...])) - t_ref[...]
        dx_ref[...] = d * (1.0 / D)
        acc[...] += _colsum8(d * d)

        @pl.when(i == nt - 1)
        def _():
            l_ref[...] = jnp.zeros_like(l_ref) + jnp.sum(acc[...]) * (0.5 / D)

    tile = pl.BlockSpec((tm, D), lambda i: (i, 0))
    half = pl.BlockSpec((tm, H), lambda i: (i, 0))
    return pl.pallas_call(
        body, name="out_fwd_loss", grid=(nt,),
        in_specs=[half, half, pl.BlockSpec((2 * H, D), lambda i: (0, 0)), tile, _mod_row(l, 2, D), _layer_row(l, D),
                  tile],
        out_specs=[tile, tile, pl.BlockSpec((SUBLANES, LANES), lambda i: (0, 0))],
        out_shape=[jax.ShapeDtypeStruct((T, D), F32), jax.ShapeDtypeStruct((T, D), BF16),
                   jax.ShapeDtypeStruct((SUBLANES, LANES), F32)],
        scratch_shapes=[pltpu.VMEM((SUBLANES, D), F32)],
        compiler_params=_params(VMEM_BIG),
    )(ya, yp, wo, x, mod4, g_post3, target)


def out_bwd(dx, y, ya, yp, wo, mod4, g_post3, l, after):
    T, D = dx.shape
    H = ya.shape[1]
    tm = 512
    nt = T // tm

    def body(dx_ref, y_ref, ya_ref, yp_ref, wo_ref, gt_ref, g_ref, after_ref,
             dya_ref, dyp_ref, dwo_ref, dgt_ref, dg_ref, acc_w, acc_p):
        i = pl.program_id(0)

        @pl.when(i == 0)
        def _():
            acc_w[...] = jnp.zeros_like(acc_w)
            acc_p[...] = jnp.zeros_like(acc_p)

        yv = y_ref[...].astype(F32)
        dxv = dx_ref[...]
        gg = gt_ref[...] * g_ref[...]
        r = _rms(yv)
        yn = yv * r
        p = dxv * yn
        acc_p[...] += _colsum8(p)
        dy = r * (dxv * gg - yn * jnp.mean(p * gg, axis=-1, keepdims=True))
        dyb = dy.astype(BF16)
        dyc = lax.dot_general(dyb, wo_ref[...], NT, preferred_element_type=F32)
        dya_ref[...] = dyc[:, 0:H].astype(BF16)
        dyp_ref[...] = dyc[:, H:2 * H].astype(BF16)
        acc_w[0:H, :] += lax.dot_general(ya_ref[...], dyb, TN, preferred_element_type=F32)
        acc_w[H:2 * H, :] += lax.dot_general(yp_ref[...], dyb, TN, preferred_element_type=F32)

        @pl.when(i == nt - 1)
        def _():
            dwo_ref[...] = acc_w[...].astype(BF16)
            sp = jnp.sum(acc_p[...], axis=0, keepdims=True)
            dgt_ref[...] = g_ref[...] * sp
            dg_ref[...] = gt_ref[...] * sp

    row = pl.BlockSpec((1, D), lambda i: (0, 0))
    tile = pl.BlockSpec((tm, D), lambda i: (i, 0))
    half = pl.BlockSpec((tm, H), lambda i: (i, 0))
    full = pl.BlockSpec((2 * H, D), lambda i: (0, 0))
    return pl.pallas_call(
        body, name="out_bwd", grid=(nt,),
        in_specs=[tile, tile, half, half, full, _mod_row(l, 2, D), _layer_row(l, D), ANY],
        out_specs=[half, half, full, row, row],
        out_shape=[jax.ShapeDtypeStruct((T, H), BF16), jax.ShapeDtypeStruct((T, H), BF16),
                   jax.ShapeDtypeStruct((2 * H, D), BF16),
                   jax.ShapeDtypeStruct((1, D), F32), jax.ShapeDtypeStruct((1, D), F32)],
        scratch_shapes=[pltpu.VMEM((2 * H, D), F32), pltpu.VMEM((SUBLANES, D), F32)],
        compiler_params=_params(VMEM_BIG),
    )(dx, y, ya, yp, wo, mod4, g_post3, after)


def _conv_bwd_block(u_ref, b_ref, c_ref, g_ref, dy_ref, w_ref, du_ref, db_ref, dc_ref, dg_ref, dw_ref):
    T = u_ref.shape[0]
    R = R_CONV
    nchunk = T // R
    w0 = w_ref[pl.ds(0, 1), :]
    w1 = w_ref[pl.ds(1, 1), :]
    w2 = w_ref[pl.ds(2, 1), :]

    def chunk(k, carry):
        head, a0, a1, a2 = carry
        i = nchunk - 1 - k
        r0 = pl.multiple_of(i * R, R)
        h0 = pl.multiple_of(jnp.maximum(r0 - HIST, 0), HIST)
        first = i == 0
        ue = _load_ext(u_ref, r0, h0, first, R)
        ce = _load_ext(c_ref, r0, h0, first, R)
        ca = ce * ue
        ca0 = ca[HIST:]
        ca1 = _shift_down(ca, 1, R)
        ca2 = _shift_down(ca, 2, R)
        conv = w2 * ca0 + w1 * ca1 + w0 * ca2
        g = g_ref[pl.ds(r0, R), :].astype(F32)
        b = b_ref[pl.ds(r0, R), :].astype(F32)
        dy = dy_ref[pl.ds(r0, R), :].astype(F32)
        sg = _sigmoid(g)
        sl = g * sg
        t = dy * conv
        db_ref[pl.ds(r0, R), :] = (t * sl).astype(BF16)
        dg_ref[pl.ds(r0, R), :] = (t * b * (sg + sl * (1.0 - sg))).astype(BF16)
        dconv = dy * b * sl
        a2 = a2 + _colsum8(dconv * ca0)
        a1 = a1 + _colsum8(dconv * ca1)
        a0 = a0 + _colsum8(dconv * ca2)
        e = jnp.concatenate([dconv, head], axis=0)
        dca = w2 * dconv + w1 * _shift_up(e, 1, R) + w0 * _shift_up(e, 2, R)
        du_ref[pl.ds(r0, R), :] = (dca * ce[HIST:]).astype(BF16)
        dc_ref[pl.ds(r0, R), :] = (dca * ue[HIST:]).astype(BF16)
        return dconv[0:SUBLANES], a0, a1, a2

    z = jnp.zeros((SUBLANES, LANES), F32)
    _, a0, a1, a2 = lax.fori_loop(0, nchunk, chunk, (z, z, z, z))
    dw_ref[pl.ds(0, 1), :] = jnp.sum(a0, axis=0, keepdims=True)
    dw_ref[pl.ds(1, 1), :] = jnp.sum(a1, axis=0, keepdims=True)
    dw_ref[pl.ds(2, 1), :] = jnp.sum(a2, axis=0, keepdims=True)


def _pool_bwd_group(pooled_s, g_ref, dy_ref, w_ref, s_ref, du_ref, dg_ref, dw_ref, ds_ref,
                    mixed_s, dmix_s, dpool_s, w):
    T = pooled_s.shape[0]
    R = R_POOL
    nchunk = T // R
    wb = w_ref[...].astype(BF16)
    mixed_s[...] = jnp.dot(pooled_s[...], wb, preferred_element_type=F32)
    sc = s_ref[...]

    def gate_chunk(i, acc):
        r0 = pl.multiple_of(i * R, R)
        g = g_ref[pl.ds(r0, R), :].astype(F32)
        dy = dy_ref[pl.ds(r0, R), :].astype(F32)
        mixed = mixed_s[pl.ds(r0, R), :]
        sg = _sigmoid(g)
        sl = g * sg
        dg_ref[pl.ds(r0, R), :] = (dy * mixed * sc * (sg + sl * (1.0 - sg))).astype(BF16)
        dms = dy * sl
        dmix_s[pl.ds(r0, R), :] = (dms * sc).astype(BF16)
        return acc + _colsum8(dms * mixed)

    acc = lax.fori_loop(0, nchunk, gate_chunk, jnp.zeros((SUBLANES, LANES), F32))
    ds_ref[...] = jnp.sum(acc, axis=0, keepdims=True)
    dpool_s[pl.ds(0, T), :] = lax.dot_general(dmix_s[...], wb, NT, preferred_element_type=F32)
    dpool_s[pl.ds(T, HIST), :] = jnp.zeros((HIST, LANES), F32)
    dw_ref[...] = lax.dot_general(pooled_s[...], dmix_s[...], TN, preferred_element_type=F32).astype(BF16)

    def back_chunk(i, carry):
        r0 = pl.multiple_of(i * R, R)
        dpe = dpool_s[pl.ds(r0, R + HIST), :]
        e = dpe / _count(r0, R + HIST, w)
        du_ref[pl.ds(r0, R), :] = (_anticausal_window_sum(e, w)[0:R] - dpe[0:R]).astype(BF16)
        return carry

    lax.fori_loop(0, nchunk, back_chunk, 0)


def mix_bwd(proj, pooled, dya, dyp, wconv, wpool, pscale3, l, after):
    T = proj.shape[0]

    def body(u_ref, b_ref, c_ref, g_ref, pooled_ref, gp_ref, dya_ref, dyp_ref, wc_ref, wp_ref, s_ref, after_ref,
             dua_ref, dba_ref, dca_ref, dga_ref, dup_ref, dgp_ref, dwc_ref, dwp_ref, ds_ref,
             mixed_s, dmix_s, dpool_s):
        j = pl.program_id(0)
        pl.when(j < N_MIX)(functools.partial(_conv_bwd_block, u_ref, b_ref, c_ref, g_ref, dya_ref, wc_ref,
                                             dua_ref, dba_ref, dca_ref, dga_ref, dwc_ref))
        for k, w in enumerate(POOL_WINDOWS):
            pl.when(j == N_MIX + k)(functools.partial(_pool_bwd_group, pooled_ref, gp_ref, dyp_ref, wp_ref, s_ref,
                                                      dup_ref, dgp_ref, dwp_ref, ds_ref,
                                                      mixed_s, dmix_s, dpool_s, w))

    sec = jax.ShapeDtypeStruct((T, N_MIX * LANES), BF16)
    conv_col = pl.BlockSpec((T, LANES), lambda j: (0, _conv_idx(j)))
    pool_col = pl.BlockSpec((T, LANES), lambda j: (0, _pool_idx(j)))
    return pl.pallas_call(
        body, name="mix_bwd", grid=(2 * N_MIX,),
        in_specs=[_proj_col(T, 0, _conv_idx), _proj_col(T, 4, _conv_idx), _proj_col(T, 8, _conv_idx),
                  _proj_col(T, 12, _conv_idx), pool_col, _proj_col(T, 20, _pool_idx),
                  conv_col, pool_col, _conv_w_spec(l), _pool_w_spec(l), _pool_s_spec(l), ANY],
        out_specs=[conv_col, conv_col, conv_col, conv_col, pool_col, pool_col,
                   pl.BlockSpec((None, 3, LANES), lambda j: (_conv_idx(j), 0, 0)),
                   pl.BlockSpec((None, LANES, LANES), lambda j: (_pool_idx(j), 0, 0)),
                   pl.BlockSpec((1, LANES), lambda j: (0, _pool_idx(j)))],
        out_shape=[sec] * 6 + [jax.ShapeDtypeStruct((N_MIX, 3, LANES), F32),
                               jax.ShapeDtypeStruct((N_MIX, LANES, LANES), BF16),
                               jax.ShapeDtypeStruct((1, N_MIX * LANES), F32)],
        scratch_shapes=[pltpu.VMEM((T, LANES), F32), pltpu.VMEM((T, LANES), BF16), pltpu.VMEM((T + HIST, LANES), F32)],
        compiler_params=_params(),
    )(proj, proj, proj, proj, pooled, proj, dya, dyp, wconv, wpool, pscale3, after)


def in_bwd(dsecs, wg, x, dxo, mod4, g_pre3, l):
    T, D = x.shape
    NB = N_CHIPS
    CW = wg.shape[1] // NB
    SW = dsecs[0].shape[1]
    nsec = len(dsecs)
    PW = 256
    assert SW % PW == 0 and CW % PW == 0
    tm = 256
    nt = T // tm

    def body(*refs):
        d_refs = refs[0:nsec]
        w_ref, x_ref, dxo_ref, sh_ref, sc_ref, g_ref = refs[nsec:nsec + 6]
        dxi_ref, dw_ref, dsh_ref, dsc_ref, dg_ref = refs[nsec + 6:nsec + 11]
        acc_w, acc_sh, acc_q = refs[nsec + 11:]
        i = pl.program_id(0)

        @pl.when(i == 0)
        def _():
            acc_w[...] = jnp.zeros_like(acc_w)
            acc_sh[...] = jnp.zeros_like(acc_sh)
            acc_q[...] = jnp.zeros_like(acc_q)

        xv = x_ref[...]
        r = _rms(xv)
        xh = xv * r
        sg = g_ref[...] * (1.0 + sc_ref[...])
        hb = (xh * sg + sh_ref[...]).astype(BF16)
        dh = lax.dot_general(d_refs[0][...], w_ref[:, 0:SW], NT, preferred_element_type=F32)
        for s in range(1, nsec):
            dh = dh + lax.dot_general(d_refs[s][...], w_ref[:, s * SW:(s + 1) * SW], NT, preferred_element_type=F32)
        for p in range(nsec * SW // PW):
            col = p * PW
            s, so = col // SW, col % SW
            j, jo = col // CW, col % CW
            acc_w[j, :, jo:jo + PW] += lax.dot_general(hb, d_refs[s][:, so:so + PW], TN, preferred_element_type=F32)
        q = dh * xh
        acc_sh[...] += _colsum8(dh)
        acc_q[...] += _colsum8(q)
        dxi_ref[...] = dxo_ref[...] + r * (dh * sg - xh * jnp.mean(q * sg, axis=-1, keepdims=True))

        @pl.when(i == nt - 1)
        def _():
            dw_ref[...] = acc_w[...].astype(BF16)
            sq = jnp.sum(acc_q[...], axis=0, keepdims=True)
            dsh_ref[...] = jnp.sum(acc_sh[...], axis=0, keepdims=True)
            dsc_ref[...] = g_ref[...] * sq
            dg_ref[...] = (1.0 + sc_ref[...]) * sq

    row = pl.BlockSpec((1, D), lambda i: (0, 0))
    tile = pl.BlockSpec((tm, D), lambda i: (i, 0))
    sect = pl.BlockSpec((tm, SW), lambda i: (i, 0))
    rowshape = jax.ShapeDtypeStruct((1, D), F32)
    return pl.pallas_call(
        body, name="in_bwd", grid=(nt,),
        in_specs=[sect] * nsec + [pl.BlockSpec((D, NB * CW), lambda i: (0, 0)), tile, tile,
                                  _mod_row(l, 0, D), _mod_row(l, 1, D), _layer_row(l, D)],
        out_specs=[tile, pl.BlockSpec((NB, D, CW), lambda i: (0, 0, 0)), row, row, row],
        out_shape=[jax.ShapeDtypeStruct((T, D), F32), jax.ShapeDtypeStruct((NB, D, CW), BF16),
                   rowshape, rowshape, rowshape],
        scratch_shapes=[pltpu.VMEM((NB, D, CW), F32),
                        pltpu.VMEM((SUBLANES, D), F32), pltpu.VMEM((SUBLANES, D), F32)],
        compiler_params=_params(VMEM_BIG),
    )(*dsecs, wg, x, dxo, mod4, mod4, g_pre3)


def _rcopy(src, dst, ssem, rsem, dev):
    return pltpu.make_async_remote_copy(src_ref=src, dst_ref=dst, send_sem=ssem, recv_sem=rsem,
                                        device_id=dev, device_id_type=MESH)


def _peers7(x, y, c):
    out = []
    for m in range(1, N_DEV):
        bx, by, bc = (m >> 2) & 1, (m >> 1) & 1, m & 1
        out.append(((1 - x) if bx else x, (1 - y) if by else y, (1 - c) if bc else c))
    return out


HBM = pl.BlockSpec(memory_space=pltpu.HBM)
SEM = pl.BlockSpec(memory_space=pltpu.SEMAPHORE)
SPLIT = pltpu.CompilerParams(has_side_effects=pltpu.SideEffectType.DATAFLOW_SIDE_EFFECTING)


def _hbm(a):
    return pltpu.with_memory_space_constraint(a, pltpu.HBM)


def _chips(x, y):
    return [(1 - x, y), (x, 1 - y), (1 - x, 1 - y)]


SIBLING_BARRIER_ID = 0


def xchg_start(name, bufs, n_copies, plan, sibling_only=False, after=()):
    n = len(bufs)
    after = list(after)

    def body(*refs):
        ssem, rsem, token = refs[n + len(after)], refs[n + len(after) + 1], refs[-1]
        x, y, c = _me()
        if sibling_only:
            barrier = pltpu.get_barrier_semaphore()
            pl.semaphore_signal(barrier, inc=1, device_id=(x, y, 1 - c), device_id_type=MESH)
            pl.semaphore_wait(barrier, 1)
        copies = plan(refs[0:n], x, y, c)
        assert len(copies) == n_copies
        for k, (src, dst, peer, _) in enumerate(copies):
            _rcopy(src, dst, ssem.at[k], rsem.at[k], peer).start()
        token[...] = jnp.zeros_like(token)

    params = dict(has_side_effects=pltpu.SideEffectType.DATAFLOW_SIDE_EFFECTING)
    if sibling_only:
        params["collective_id"] = SIBLING_BARRIER_ID
    outs = pl.pallas_call(
        body, name=name,
        in_specs=[HBM] * n + [ANY] * len(after),
        out_specs=[SEM, SEM] + [HBM] * n + [pl.BlockSpec(memory_space=pltpu.VMEM)],
        out_shape=([pltpu.SemaphoreType.DMA((n_copies,))] * 2 + [pltpu.HBM(b.shape, b.dtype) for b in bufs]
                   + [jax.ShapeDtypeStruct((SUBLANES, LANES), F32)]),
        input_output_aliases={a: 2 + a for a in range(n)},
        compiler_params=pltpu.CompilerParams(**params),
    )(*[_hbm(b) for b in bufs], *after)
    return outs[0], outs[1], list(outs[2:2 + n]), outs[-1]


def xchg_wait(name, bufs, ssem, rsem, n_copies, plan, after, sems=None):
    n = len(bufs)
    after = list(after)
    sems = tuple(range(n_copies)) if sems is None else tuple(sems)
    assert len(sems) == n_copies

    def body(*refs):
        ssem_ref, rsem_ref = refs[n], refs[n + 1]
        copies = plan(refs[0:n], *_me())
        assert len(copies) == n_copies
        for k, (src, _, peer, land) in zip(sems, copies):
            cp = _rcopy(src, land, ssem_ref.at[k], rsem_ref.at[k], peer)
            cp.wait_send()
            cp.wait_recv()

    outs = pl.pallas_call(
        body, name=name,
        in_specs=[HBM] * n + [SEM, SEM] + [ANY] * len(after), out_specs=[HBM] * n,
        out_shape=[pltpu.HBM(b.shape, b.dtype) for b in bufs],
        input_output_aliases={a: a for a in range(n)},
        compiler_params=SPLIT,
    )(*bufs, ssem, rsem, *after)
    return list(outs)


def _shard_half(buf, chip, half):
    if len(buf.shape) == 2:
        h, w = buf.shape[0] // 2, buf.shape[1] // N_CHIPS
        return buf.at[pl.ds(half * h, h), pl.ds(chip * w, w)]
    h = buf.shape[1] // 2
    return buf.at[chip, pl.ds(half * h, h)]


def plan_gather(refs, x, y, c):
    out = []
    for (px, py) in _chips(x, y):
        for buf in refs:
            own = _shard_half(buf, 2 * x + y, c)
            out.append((own, own, (px, py, c), _shard_half(buf, 2 * px + py, c)))
    return out


def plan_forward(refs, x, y, c):
    out = []
    for (px, py) in _chips(x, y):
        for buf in refs:
            landed = _shard_half(buf, 2 * px + py, c)
            out.append((landed, landed, (x, y, 1 - c), _shard_half(buf, 2 * px + py, 1 - c)))
    return out


def plan_sibling(refs, x, y, c):
    n = len(refs) // 2
    out = []
    for a in range(n):
        h = refs[a].shape[1] // 2
        out.append((refs[a].at[:, pl.ds((1 - c) * h, h)], refs[n + a], (x, y, 1 - c), refs[n + a]))
    return out


def plan_chip(refs, x, y, c):
    n = len(refs) // 2
    out = []
    for j, (px, py) in enumerate(_chips(x, y)):
        for a in range(n):
            out.append((refs[a].at[2 * px + py], refs[n + a].at[j], (px, py, c), refs[n + a].at[j]))
    return out


def plan_mod(refs, x, y, c):
    (mods,) = refs
    mine = mods.at[2 * x + y]
    return [(mine, mine, (px, py, c), mods.at[2 * px + py]) for (px, py) in _chips(x, y)]


def plan_pack(refs, x, y, c):
    (packs,) = refs
    mine = packs.at[4 * x + 2 * y + c]
    return [(mine, mine, peer, packs.at[4 * peer[0] + 2 * peer[1] + peer[2]]) for peer in _peers7(x, y, c)]


def plan_spread(layers, wp_layers):
    def plan(refs, x, y, c):
        gi, go, gp = refs
        hD, hR, hP = gi.shape[1] // 2, go.shape[1] // 2, gp.shape[2] // 2
        sib = (x, y, 1 - c)
        out = []
        for l in layers:
            mine = gi.at[l, pl.ds(c * hD, hD)]
            out.append((mine, mine, sib, gi.at[l, pl.ds((1 - c) * hD, hD)]))
            mine = go.at[l, pl.ds(c * hR, hR)]
            out.append((mine, mine, sib, go.at[l, pl.ds((1 - c) * hR, hR)]))
        for l in wp_layers:
            mine = gp.at[l, 2 * x + y, pl.ds(c * hP, hP)]
            for peer in _peers7(x, y, c):
                out.append((mine, mine, peer, gp.at[l, 2 * peer[0] + peer[1], pl.ds(peer[2] * hP, hP)]))
        return out

    return plan


def gather_small(c8, wc, token):
    def body(c_ref, wc_ref, token_ref, call, wcall, ssem, rsem, lsem):
        x, y, c = _me()
        myc = 2 * x + y
        me_lin = 4 * x + 2 * y + c
        me = (x, y, c)
        local = [pltpu.make_async_copy(c_ref, call.at[me_lin], lsem.at[0]),
                 pltpu.make_async_copy(wc_ref, wcall.at[myc], lsem.at[1])]
        for cp in local:
            cp.start()
        sends, recvs = [], []
        for m, peer in enumerate(_peers7(x, y, c)):
            plin = 4 * peer[0] + 2 * peer[1] + peer[2]
            sends.append(_rcopy(c_ref, call.at[me_lin], ssem.at[m], rsem.at[m], peer))
            recvs.append(_rcopy(call.at[plin], call.at[plin], ssem.at[m], rsem.at[m], me))
        for j, (px, py) in enumerate([(1 - x, y), (x, 1 - y), (1 - x, 1 - y)]):
            pc = 2 * px + py
            sends.append(_rcopy(wc_ref, wcall.at[myc], ssem.at[7 + j], rsem.at[7 + j], (px, py, c)))
            recvs.append(_rcopy(wcall.at[pc], wcall.at[pc], ssem.at[7 + j], rsem.at[7 + j], me))
        for cp in sends:
            cp.start()
        for cp in recvs:
            cp.wait_recv()
        for cp in sends:
            cp.wait_send()
        for cp in local:
            cp.wait()

    return pl.pallas_call(
        body, name="gather_small",
        in_specs=[ANY] * 3, out_specs=[ANY] * 2,
        out_shape=[jax.ShapeDtypeStruct((N_DEV, SUBLANES, LANES), F32),
                   jax.ShapeDtypeStruct((N_CHIPS, wc.shape[0], 3, LANES), F32)],
        scratch_shapes=[pltpu.SemaphoreType.DMA((10,)), pltpu.SemaphoreType.DMA((10,)), pltpu.SemaphoreType.DMA((2,))],
        compiler_params=_params(n_grid=0),
    )(c8, wc, token)


def spread_now(gi, go, gp, layers, wp_layers):
    plan = plan_spread(layers, wp_layers)
    n = 2 * len(layers) + 7 * len(wp_layers)

    def body(gi_in, go_in, gp_in, gi, go, gp, ssem, rsem):
        copies = plan((gi, go, gp), *_me())
        me = _me()
        sends = [_rcopy(src, dst, ssem.at[k], rsem.at[k], peer) for k, (src, dst, peer, _) in enumerate(copies)]
        for cp in sends:
            cp.start()
        for k, (_, _, _, land) in enumerate(copies):
            _rcopy(land, land, ssem.at[k], rsem.at[k], me).wait_recv()
        for cp in sends:
            cp.wait_send()

    return pl.pallas_call(
        body, name="spread_now",
        in_specs=[ANY] * 3, out_specs=[ANY] * 3,
        out_shape=[jax.ShapeDtypeStruct(a.shape, a.dtype) for a in (gi, go, gp)],
        input_output_aliases={0: 0, 1: 1, 2: 2},
        scratch_shapes=[pltpu.SemaphoreType.DMA((n,)), pltpu.SemaphoreType.DMA((n,))],
        compiler_params=_params(n_grid=0),
    )(gi, go, gp)


def add_sibling(cidx, mine, sib):
    def body(c_ref, *refs):
        for a in range(3):
            m, s, o = refs[a], refs[3 + a], refs[6 + a]
            o[...] = (m[...].astype(F32) + s[...].astype(F32)).astype(BF16)

    per_step = 2

    def mine_spec(a):
        h = a.shape[1] // 2
        return pl.BlockSpec((per_step, h, a.shape[2]), lambda j, c_ref: (j, c_ref[0], 0))

    def sib_spec(a):
        return pl.BlockSpec((per_step,) + a.shape[1:], lambda j, c_ref: (j, 0, 0))

    return pl.pallas_call(
        body, name="add_sibling",
        grid_spec=pltpu.PrefetchScalarGridSpec(
            num_scalar_prefetch=1, grid=(N_CHIPS // per_step,),
            in_specs=[mine_spec(a) for a in mine] + [sib_spec(a) for a in sib],
            out_specs=[sib_spec(a) for a in sib]),
        out_shape=[jax.ShapeDtypeStruct(a.shape, BF16) for a in sib],
        compiler_params=_params(VMEM_BIG),
    )(cidx, *mine, *sib)


def sum_chips(pos, own, rb, acc, l, shapes):
    nq = 2
    n_in = 6 + (3 if acc is not None else 0)

    def body(pos_ref, *refs):
        for a in range(3):
            m, b, o = refs[a], refs[3 + a], refs[n_in + a]
            s = m[...].astype(F32)
            for j in range(3):
                s = s + b[j].astype(F32)
            o[...] = s

    def own_spec(a):
        return pl.BlockSpec((None, a.shape[1] // nq, a.shape[2]), lambda q, p: (p[1], q, 0))

    def rb_spec(a):
        return pl.BlockSpec((3, a.shape[1] // nq, a.shape[2]), lambda q, p: (0, q, 0))

    hi, ho, hp = own[0].shape[1] // nq, own[1].shape[1] // nq, own[2].shape[1] // nq
    out_specs = [pl.BlockSpec((None, hi, shapes[0][2]), lambda q, p: (l, p[0] * nq + q, 0)),
                 pl.BlockSpec((None, ho, shapes[1][2]), lambda q, p: (l, p[0] * nq + q, 0)),
                 pl.BlockSpec((None, None, hp, LANES), lambda q, p: (l, p[1], p[0] * nq + q, 0))]
    in_specs = [own_spec(a) for a in own] + [rb_spec(a) for a in rb]
    args = list(own) + list(rb)
    aliases = {}
    if acc is not None:
        in_specs += [ANY] * 3
        args += list(acc)
        aliases = {7: 0, 8: 1, 9: 2}
    return pl.pallas_call(
        body, name="sum_chips",
        grid_spec=pltpu.PrefetchScalarGridSpec(num_scalar_prefetch=1, grid=(nq,), in_specs=in_specs, out_specs=out_specs),
        out_shape=[jax.ShapeDtypeStruct(s, F32) for s in shapes],
        input_output_aliases=aliases,
        compiler_params=_params(VMEM_BIG),
    )(pos, *args)


def pack_small(pos, per_layer, loss_blk):
    L = len(per_layer)
    D = per_layer[0][0].shape[1]

    def body(pos_ref, *refs):
        o = refs[-1]
        lb = refs[-2]
        o[...] = jnp.zeros_like(o)
        for l in range(L):
            dgpre, dgpost, dsh, dsc, dgt, dps, dwc = refs[7 * l:7 * l + 7]
            base = SUBLANES * l
            for r, src in enumerate((dgpre, dgpost, dsh, dsc, dgt)):
                o[pl.ds(base + r, 1), :] = src[...]
            o[pl.ds(base + 5, 1), 0:dps.shape[1]] = dps[...]
            for j in range(dwc.shape[0]):
                for k in range(3):
                    idx = 3 * j + k
                    o[pl.ds(base + 6 + idx // 8, 1), (idx % 8) * LANES:(idx % 8 + 1) * LANES] = dwc[j, pl.ds(k, 1), :]
        o[pl.ds(5, 1), 4 * LANES:5 * LANES] = lb[pl.ds(0, 1), :]

    flat = [a for layer in per_layer for a in layer] + [loss_blk]

    def whole(a):
        return pl.BlockSpec(a.shape, lambda i, p: (0,) * a.ndim)

    return pl.pallas_call(
        body, name="pack_small",
        grid_spec=pltpu.PrefetchScalarGridSpec(
            num_scalar_prefetch=1, grid=(1,), in_specs=[whole(a) for a in flat],
            out_specs=pl.BlockSpec((None, L * SUBLANES, D), lambda i, p: (p[2], 0, 0))),
        out_shape=jax.ShapeDtypeStruct((N_DEV, L * SUBLANES, D), F32),
        compiler_params=_params(),
    )(pos, *flat)


def small_update(pos, packs, params, moments_m, moments_v):
    n = len(params)
    L, D = params[1].shape
    PS = params[3].shape[1]

    def body(pos_ref, p_ref, *refs):
        ws, ms, vs = refs[0:n], refs[n:2 * n], refs[2 * n:3 * n]
        loss_ref = refs[3 * n]
        outs = [refs[3 * n + 1 + 4 * t:3 * n + 5 + 4 * t] for t in range(n)]
        summed = refs[-1]
        s = p_ref[0]
        for d in range(1, N_DEV):
            s = s + p_ref[d]
        summed[...] = s
        loss_ref[...] = summed[pl.ds(5, 1), 4 * LANES:5 * LANES]
        chip = pos_ref[1]

        def update(t, idx, g):
            d, mm, vv = _adamw_math(ws[t][idx], g, ms[t][idx], vs[t][idx])
            g_ref, d_ref, mo_ref, vo_ref = outs[t]
            g_ref[idx] = g
            d_ref[idx] = d
            mo_ref[idx] = mm
            vo_ref[idx] = vv

        for l in range(L):
            base = SUBLANES * l
            row = pl.ds(l, 1)
            for k in range(3):
                update(0, (row, slice(k * D, (k + 1) * D)), summed[pl.ds(base + 2 + k, 1), :])
            update(1, (row, slice(None)), summed[pl.ds(base, 1), :])
            update(2, (row, slice(None)), summed[pl.ds(base + 1, 1), :])
            update(3, (row, slice(None)), summed[pl.ds(base + 5, 1), 0:PS])
            for k in range(3):
                g = None
                for j in range(N_CHIPS):
                    idx = 3 * j + k
                    cand = summed[pl.ds(base + 6 + idx // 8, 1), (idx % 8) * LANES:(idx % 8 + 1) * LANES]
                    g = cand if g is None else jnp.where(chip == j, cand, g)
                update(4, (l, pl.ds(k, 1), slice(None)), g)

    def whole(a):
        return pl.BlockSpec(a.shape, lambda i, p: (0,) * a.ndim)

    ins = [packs] + list(params) + list(moments_m) + list(moments_v)
    out_shape = [jax.ShapeDtypeStruct((1, LANES), F32)]
    for w in params:
        out_shape += [jax.ShapeDtypeStruct(w.shape, F32)] * 4
    outs = pl.pallas_call(
        body, name="small_update",
        grid_spec=pltpu.PrefetchScalarGridSpec(
            num_scalar_prefetch=1, grid=(1,), in_specs=[whole(a) for a in ins],
            out_specs=[whole(a) for a in out_shape],
            scratch_shapes=[pltpu.VMEM(packs.shape[1:], F32)]),
        out_shape=out_shape,
        compiler_params=_params(),
    )(pos, *ins)
    return outs[0], [outs[1 + 4 * t:5 + 4 * t] for t in range(n)]


def _adamw_math(w, g, m, v):
    m = ADAM_B1 * m + (1.0 - ADAM_B1) * g
    v = ADAM_B2 * v + (1.0 - ADAM_B2) * (g * g)
    m_hat = m / (1.0 - ADAM_B1 ** ADAM_STEP)
    v_hat = v / (1.0 - ADAM_B2 ** ADAM_STEP)
    delta = -ADAM_LR * (m_hat / (jnp.sqrt(v_hat) + ADAM_EPS) + ADAM_WD * w)
    return delta, m, v


def adamw(w, g, m, v, block, name, first=0, count=None, acc=None):
    grid = tuple(s // b for s, b in zip(w.shape, block))
    if count is not None:
        grid = (count,) + grid[1:]

    def body(w_ref, g_ref, m_ref, v_ref, *rest):
        go_ref, d_ref, mo_ref, vo_ref = rest[-4:]
        gv = g_ref[...]
        d, mm, vv = _adamw_math(w_ref[...], gv, m_ref[...], v_ref[...])
        go_ref[...] = gv
        d_ref[...] = d
        mo_ref[...] = mm
        vo_ref[...] = vv

    spec = pl.BlockSpec(block, lambda i, *rest: (first + i,) + rest)
    shape = jax.ShapeDtypeStruct(w.shape, F32)
    extra = [] if acc is None else list(acc)
    return pl.pallas_call(
        body, name=name, grid=grid,
        in_specs=[spec] * 4 + [ANY] * len(extra), out_specs=[spec] * 4, out_shape=[shape] * 4,
        input_output_aliases={4 + a: a for a in range(len(extra))},
        compiler_params=_params(VMEM_BIG, n_grid=len(grid)),
    )(w, g, m, v, *extra)


def ada_finish(c_all, dmod, w, m, v):
    L, D, CW = w.shape
    hD = D // 2

    def body(c_ref, d_ref, w_ref, m_ref, v_ref, g_ref, dl_ref, mo_ref, vo_ref):
        cv = c_ref[...]
        z = jnp.zeros_like(cv)
        ca = jnp.concatenate([cv * jax.nn.sigmoid(cv), z], axis=0).astype(BF16)
        dm = jnp.concatenate([d_ref[0], jnp.zeros_like(d_ref[0])], axis=0).astype(BF16)
        g = lax.dot_general(ca, dm, TN, preferred_element_type=F32)
        g_ref[0] = g
        d, mm, vv = _adamw_math(w_ref[0], g, m_ref[0], v_ref[0])
        dl_ref[0] = d
        mo_ref[0] = mm
        vo_ref[0] = vv

    big = pl.BlockSpec((1, hD, CW), lambda l, h: (l, h, 0))
    shape = jax.ShapeDtypeStruct(w.shape, F32)
    return pl.pallas_call(
        body, name="ada_finish", grid=(L, 2),
        in_specs=[pl.BlockSpec((N_DEV, hD), lambda l, h: (0, h)), pl.BlockSpec((1, N_DEV, CW), lambda l, h: (l, 0, 0)),
                  big, big, big],
        out_specs=[big] * 4, out_shape=[shape] * 4,
        compiler_params=_params(VMEM_BIG, n_grid=2),
    )(c_all, dmod, w, m, v)


def kernel(x, c, w_ada, b_ada, g_pre, w_in, w_conv, w_pool, pool_scale, w_out, g_post, loss_target, m_w_ada, m_b_ada, m_g_pre, m_w_in, m_w_conv, m_w_pool, m_pool_scale, m_w_out, m_g_post, v_w_ada, v_b_ada, v_g_pre, v_w_in, v_w_conv, v_w_pool, v_pool_scale, v_w_out, v_g_post):
    L, D, CW = w_in.shape
    RO = w_out.shape[1]
    T = x.shape[1]
    ix, iy, ic = _me()
    chip = 2 * ix + iy
    me_lin = 4 * ix + 2 * iy + ic

    pos = jnp.stack([ic, chip, me_lin]).astype(jnp.int32)
    g_pre3, g_post3 = g_pre.reshape(L, 1, D), g_post.reshape(L, 1, D)
    pscale3 = pool_scale.reshape(L, 1, pool_scale.shape[1])
    n_s, n_c = 3, 9

    def gather(bufs, after):
        ss, rs, bufs, tok = xchg_start("gather_start", bufs, 3 * len(bufs), plan_gather, after=after)
        return (ss, rs, bufs), tok

    def arrive(flight, after):
        ss, rs, bufs = flight
        bufs = xchg_wait("gather_wait", bufs, ss, rs, 3 * len(bufs), plan_gather, after)
        fss, frs, bufs, tok = xchg_start("forward_start", bufs, 3 * len(bufs), plan_forward, sibling_only=True)
        return (fss, frs, bufs), tok

    def ready(flight, after):
        fss, frs, bufs = flight
        return xchg_wait("forward_wait", bufs, fss, frs, 3 * len(bufs), plan_forward, after)

    def arrive_part(flight, which, after):
        ss, rs, bufs = flight
        sems = tuple(range(which, 3 * len(bufs), len(bufs)))
        (buf,) = xchg_wait("gather_wait", [bufs[which]], ss, rs, 3, plan_gather, after, sems=sems)
        fss, frs, (buf,), tok = xchg_start("forward_start", [buf], 3, plan_forward, sibling_only=True)
        return (fss, frs, [buf]), tok

    c_all3, wconv_all = gather_small(c.reshape(SUBLANES, LANES), w_conv, pos)
    c_all = c_all3.reshape(N_DEV, D)
    gi0, go0 = cast_weights(pos, w_in, w_out, 0, c_all3)
    fly_in0, token = gather([gi0], [])
    b_my = lax.dynamic_slice_in_dim(b_ada, chip * CW, CW, axis=1)
    m_ss, m_rs, mods, token = xchg_start("mod_start", [mod_part(pos, c_all, w_ada, b_my, token)], 3, plan_mod)
    fly_out0, token = gather([go0], [token])
    flying_w = [None] * L
    for l in range(1, L):
        flying_w[l], token = gather(list(cast_weights(pos, w_in, w_out, l, token)), [])
    fwd_in0, token = arrive(fly_in0, [token])
    (mod_all,) = xchg_wait("mod_wait", mods, m_ss, m_rs, 3, plan_mod, [token])
    mod = lax.dynamic_index_in_dim(mod_all, me_lin, axis=2, keepdims=False)
    mod4 = jnp.transpose(mod, (1, 0, 2)).reshape(L, 3, 1, D)

    xs, projs, yas, yps, ys, pooleds = [x.reshape(T, D)], [], [], [], [], []
    wg_in, wg_out = [], []
    fwd_in = fwd_in0
    for l in range(L):
        (gi,) = ready(fwd_in, [mod4 if l == 0 else xs[l]])
        proj = proj_fwd(xs[l], mod4, g_pre3, gi, l)
        ya, yp, pooled = mix_fwd(proj, wconv_all, w_pool, pscale3, l)
        pooleds.append(pooled)
        after = [ya, yp]
        if l == 0:
            fwd_out, token = arrive(fly_out0, after)
        else:
            fwd_out, token = arrive_part(flying_w[l], 1, after)
        after = [token]
        if l + 1 < L:
            fwd_in, token = arrive_part(flying_w[l + 1], 0, after)
            after = [token]
        (go,) = ready(fwd_out, after)
        wg_in.append(gi)
        wg_out.append(go.reshape(N_CHIPS * RO, D))
        projs.append(proj)
        yas.append(ya)
        yps.append(yp)
        if l + 1 < L:
            xn, yv = out_fwd(ya, yp, wg_out[l], xs[l], mod4, g_post3, l, after[0])
            xs.append(xn)
        else:
            dx, yv, loss_blk = out_fwd_loss(ya, yp, wg_out[l], xs[l], mod4, g_post3, l, loss_target.reshape(T, D))
        ys.append(yv)

    shapes = (w_in.shape, w_out.shape, w_pool.shape)
    smalls = [None] * L
    acc, flying, sib, token = None, None, None, loss_blk

    def to_chips(sib, after):
        sl, s_ss, s_rs, s_bufs = sib
        s_bufs = xchg_wait("sibling_wait", s_bufs, s_ss, s_rs, n_s, plan_sibling, after)
        chip_parts = add_sibling(pos, s_bufs[0:3], s_bufs[3:6])
        lands = [lax.empty((3,) + a.shape[1:], a.dtype) for a in chip_parts]
        c_ss, c_rs, c_bufs, ctoken = xchg_start("chip_start", list(chip_parts) + lands, n_c, plan_chip)
        return (sl, c_ss, c_rs, c_bufs), ctoken

    def landed(flying, acc, after):
        fl, f_ss, f_rs, f_bufs = flying
        f_bufs = xchg_wait("chip_wait", f_bufs, f_ss, f_rs, n_c, plan_chip, after)
        return sum_chips(pos, f_bufs[0:3], f_bufs[3:6], acc, fl, shapes)

    for l in reversed(range(L)):
        dya, dyp, dwo_l, dgate, dgpost = out_bwd(dx, ys[l], yas[l], yps[l], wg_out[l], mod4, g_post3, l, token)
        token = dya
        if sib is not None:
            arrived = flying
            flying, token = to_chips(sib, [dya])
            if arrived is not None:
                acc = landed(arrived, acc, [token])
                token = acc[0]
        du_a, db_a, dc_a, dg_a, du_p, dg_p, dwc, dwp_l, dps = mix_bwd(projs[l], pooleds[l], dya, dyp, wconv_all, w_pool,
                                                                        pscale3, l, token)
        dx, dwi_l, dshift, dscale, dgpre = in_bwd([du_a, db_a, dc_a, dg_a, du_p, dg_p], wg_in[l], xs[l], dx,
                                                  mod4, g_pre3, l)
        smalls[l] = (dgpre, dgpost, dshift, dscale, dgate, dps, dwc)
        parts = [dwi_l, dwo_l.reshape(N_CHIPS, RO, D), dwp_l]
        s_lands = [lax.empty((a.shape[0], a.shape[1] // 2) + a.shape[2:], a.dtype) for a in parts]
        s_ss, s_rs, s_bufs, token = xchg_start("sibling_start", parts + s_lands, n_s, plan_sibling, sibling_only=True)
        sib = (l, s_ss, s_rs, s_bufs)
    grad_x = dx.reshape(1, T, D)

    p_ss, p_rs, packs, ptoken = xchg_start("pack_start", [pack_small(pos, smalls, loss_blk)], N_DEV - 1, plan_pack)
    acc = landed(flying, acc, [ptoken, token])
    n_sp = (2 + N_DEV - 1) * (L - 1)
    spread = plan_spread(tuple(range(1, L)), tuple(range(1, L)))
    sp_ss, sp_rs, acc, sp_token = xchg_start("spread_start", list(acc), n_sp, spread)
    flying, token = to_chips(sib, [sp_token])
    (packs_all,) = xchg_wait("pack_wait", packs, p_ss, p_rs, N_DEV - 1, plan_pack, [token])
    dmod_all = packs_all.reshape(N_DEV, L, SUBLANES, D)[:, :, 2:5].reshape(N_DEV, L, 3 * D)
    dmod_my = jnp.transpose(lax.dynamic_slice_in_dim(dmod_all, chip * CW, CW, axis=2), (1, 0, 2))

    g_w_ada, d_w_ada, nm_w_ada, nv_w_ada = ada_finish(c_all, dmod_my, w_ada, m_w_ada, v_w_ada)
    loss_row, upd = small_update(pos, packs_all, [b_ada, g_pre, g_post, pool_scale, w_conv],
                                 [m_b_ada, m_g_pre, m_g_post, m_pool_scale, m_w_conv],
                                 [v_b_ada, v_g_pre, v_g_post, v_pool_scale, v_w_conv])
    loss = loss_row[0, 0]
    (g_b_ada, d_b_ada, nm_b_ada, nv_b_ada), (g_g_pre, d_g_pre, nm_g_pre, nv_g_pre) = upd[0], upd[1]
    (g_g_post, d_g_post, nm_g_post, nv_g_post), (g_pscale, d_pscale, nm_pscale, nv_pscale) = upd[2], upd[3]
    g_w_conv, d_w_conv, nm_w_conv, nv_w_conv = upd[4]

    done = [nv_w_ada, nv_w_conv]
    g_w_in, g_w_out, g_w_pool = xchg_wait("spread_wait", acc, sp_ss, sp_rs, n_sp, spread, done)
    in_blk, out_blk = (1, D // 2, CW), (1, RO, D)
    upd_in = adamw(w_in, g_w_in, m_w_in, v_w_in, in_blk, "adamw_w_in", 1, L - 1)
    upd_out = adamw(w_out, g_w_out, m_w_out, v_w_out, out_blk, "adamw_w_out", 1, L - 1)

    acc = landed(flying, (g_w_in, g_w_out, g_w_pool), [upd_in[3], upd_out[3]])
    r_w_in, r_w_out, r_w_pool = spread_now(*acc, (0,), (0,))
    g_w_in, d_w_in, nm_w_in, nv_w_in = adamw(w_in, r_w_in, m_w_in, v_w_in, in_blk, "adamw_w_in", 0, 1, upd_in)
    g_w_out, d_w_out, nm_w_out, nv_w_out = adamw(w_out, r_w_out, m_w_out, v_w_out, out_blk, "adamw_w_out", 0, 1, upd_out)
    pshape = (L, N_CHIPS * LANES, LANES)
    upd_pool = adamw(w_pool.reshape(pshape), r_w_pool.reshape(pshape), m_w_pool.reshape(pshape),
                     v_w_pool.reshape(pshape), (1,) + pshape[1:], "adamw_w_pool")
    g_w_pool, d_w_pool, nm_w_pool, nv_w_pool = [a.reshape(w_pool.shape) for a in upd_pool]

    return (loss, grad_x,
            g_w_ada, g_b_ada, g_g_pre, g_w_in, g_w_conv, g_w_pool, g_pscale, g_w_out, g_g_post,
            d_w_ada, d_b_ada, d_g_pre, d_w_in, d_w_conv, d_w_pool, d_pscale, d_w_out, d_g_post,
            nm_w_ada, nm_b_ada, nm_g_pre, nm_w_in, nm_w_conv, nm_w_pool, nm_pscale, nm_w_out, nm_g_post,
            nv_w_ada, nv_b_ada, nv_g_pre, nv_w_in, nv_w_conv, nv_w_pool, nv_pscale, nv_w_out, nv_g_post)
```

```python
import functools

import jax
import jax.numpy as jnp
from jax import lax
from jax.experimental import pallas as pl
from jax.experimental.pallas import tpu as pltpu

F32 = jnp.float32
BF16 = jnp.bfloat16
MESH = pl.DeviceIdType.MESH
ANY = pl.BlockSpec(memory_space=pl.ANY)

NORM_EPS = 1e-6
POOL_WINDOWS = (2, 4, 8, 16)
ADAM_LR = 0.001
ADAM_B1 = 0.9
ADAM_B2 = 0.999
ADAM_EPS = 1e-08
ADAM_WD = 0.01
ADAM_STEP = 10

N_CHIPS = 4
N_DEV = 8
LANES = 128
SUBLANES = 8
VMEM_BIG = 56 * 1024 * 1024
HIST = 16
R_CONV = 64
R_POOL = 128

ROW_G_PRE, ROW_G_POST, ROW_MOD, ROW_PSCALE, ROW_WCONV = 0, 1, 2, 5, 6
LOSS_LANES = slice(4 * LANES, 5 * LANES)

NT = (((1,), (1,)), ((), ()))
TN = (((0,), (0,)), ((), ()))


def _params(vmem=None, n_grid=1):
    kw = {}
    if n_grid:
        kw["dimension_semantics"] = ("arbitrary",) * n_grid
    if vmem is not None:
        kw["vmem_limit_bytes"] = vmem
    return pltpu.CompilerParams(**kw)


def _colsum8(v):
    n, d = v.shape
    return v.reshape(n // SUBLANES, SUBLANES, d).sum(axis=0)


def _rms(v):
    return lax.rsqrt(jnp.mean(v * v, axis=-1, keepdims=True) + NORM_EPS)


def _sigmoid(v):
    return 0.5 * jnp.tanh(0.5 * v) + 0.5


def _shift_down(ext, k, rows):
    if k == 0:
        return ext[HIST:HIST + rows]
    return pltpu.roll(ext, k, 0)[HIST:HIST + rows]


def _shift_up(ext, k, rows):
    if k == 0:
        return ext[0:rows]
    return pltpu.roll(ext, ext.shape[0] - k, 0)[0:rows]


def _load_ext(ref, r0, h0, first, rows):
    hist = ref[pl.ds(h0, HIST), :].astype(F32)
    hist = jnp.where(first, 0.0, hist)
    cur = ref[pl.ds(r0, rows), :].astype(F32)
    return jnp.concatenate([hist, cur], axis=0)


def _me():
    return lax.axis_index("x"), lax.axis_index("y"), lax.axis_index("c")


def cast_weights(pos, w_in, w_out, l, after):
    _, D, CW = w_in.shape
    RO = w_out.shape[1]

    def body(pos_ref, wi, wo, after_ref, oi, oo):
        oi[...] = wi[...].astype(BF16)
        oo[...] = wo[...].astype(BF16)

    return pl.pallas_call(
        body, name="cast_w",
        grid_spec=pltpu.PrefetchScalarGridSpec(
            num_scalar_prefetch=1, grid=(2,),
            in_specs=[pl.BlockSpec((None, D // 2, CW), lambda h, p: (l, h, 0)),
                      pl.BlockSpec((None, RO // 2, D), lambda h, p: (l, h, 0)), ANY],
            out_specs=[pl.BlockSpec((D // 2, CW), lambda h, p: (h, p[1])),
                       pl.BlockSpec((None, RO // 2, D), lambda h, p: (p[1], h, 0))]),
        out_shape=[jax.ShapeDtypeStruct((D, N_CHIPS * CW), BF16), jax.ShapeDtypeStruct((N_CHIPS, RO, D), BF16)],
        compiler_params=_params(),
    )(pos, w_in, w_out, after)


def mod_part(pos, c_all, w_ada, b_my, after):
    L, D, CW = w_ada.shape

    def body(pos_ref, c_ref, w_ref, b_ref, after_ref, o_ref):
        cv = c_ref[...]
        ca = (cv * jax.nn.sigmoid(cv)).astype(BF16)
        o_ref[...] = jnp.dot(ca, w_ref[0].astype(BF16), preferred_element_type=F32) + b_ref[0]

    return pl.pallas_call(
        body, name="mod_part",
        grid_spec=pltpu.PrefetchScalarGridSpec(
            num_scalar_prefetch=1, grid=(L,),
            in_specs=[pl.BlockSpec((N_DEV, D), lambda l, p: (0, 0)),
                      pl.BlockSpec((1, D, CW), lambda l, p: (l, 0, 0)),
                      pl.BlockSpec((1, 1, CW), lambda l, p: (l, 0, 0)), ANY],
            out_specs=pl.BlockSpec((None, None, N_DEV, CW), lambda l, p: (p[1], l, 0, 0))),
        out_shape=jax.ShapeDtypeStruct((N_CHIPS, L, N_DEV, CW), F32),
        compiler_params=_params(VMEM_BIG),
    )(pos, c_all, w_ada, b_my.reshape(L, 1, CW), after)


def _mod_row(l, k, D):
    return pl.BlockSpec((None, None, 1, D), lambda *_: (l, k, 0, 0))


def _layer_row(l, D):
    return pl.BlockSpec((None, 1, D), lambda *_: (l, 0, 0))


def proj_fwd(x, mod4, g_pre3, wg, l):
    T, D = x.shape
    NC = wg.shape[1]
    NB = N_CHIPS
    CW = NC // NB
    tm = 512

    def body(x_ref, sh_ref, sc_ref, g_ref, w_ref, o_ref):
        xv = x_ref[...]
        h = (xv * _rms(xv)) * (g_ref[...] * (1.0 + sc_ref[...])) + sh_ref[...]
        hb = h.astype(BF16)
        for j in range(NB):
            cols = slice(j * CW, (j + 1) * CW)
            o_ref[:, cols] = jnp.dot(hb, w_ref[:, cols], preferred_element_type=F32).astype(BF16)

    return pl.pallas_call(
        body, name="proj_fwd", grid=(T // tm,),
        in_specs=[pl.BlockSpec((tm, D), lambda i: (i, 0)), _mod_row(l, 0, D), _mod_row(l, 1, D), _layer_row(l, D),
                  pl.BlockSpec((D, NC), lambda i: (0, 0))],
        out_specs=pl.BlockSpec((tm, NC), lambda i: (i, 0)),
        out_shape=jax.ShapeDtypeStruct((T, NC), BF16),
        compiler_params=_params(VMEM_BIG),
    )(x, mod4, mod4, g_pre3, wg)


N_MIX = 4


def _conv_fwd_block(u_ref, b_ref, c_ref, g_ref, w_ref, o_ref):
    T = u_ref.shape[0]
    R = 2 * R_CONV
    w0 = w_ref[pl.ds(0, 1), :]
    w1 = w_ref[pl.ds(1, 1), :]
    w2 = w_ref[pl.ds(2, 1), :]

    def chunk(i, carry):
        r0 = pl.multiple_of(i * R, R)
        h0 = pl.multiple_of(jnp.maximum(r0 - HIST, 0), HIST)
        first = i == 0
        ca = _load_ext(c_ref, r0, h0, first, R) * _load_ext(u_ref, r0, h0, first, R)
        conv = w2 * ca[HIST:] + w1 * _shift_down(ca, 1, R) + w0 * _shift_down(ca, 2, R)
        g = g_ref[pl.ds(r0, R), :].astype(F32)
        b = b_ref[pl.ds(r0, R), :].astype(F32)
        o_ref[pl.ds(r0, R), :] = (b * conv * (g * _sigmoid(g))).astype(BF16)
        return carry

    lax.fori_loop(0, T // R, chunk, 0)


def _conv_idx(j):
    return jnp.minimum(j, N_MIX - 1)


def _pool_idx(j):
    return jnp.maximum(j - N_MIX, 0)


def _proj_col(T, off, idx):
    return pl.BlockSpec((T, LANES), lambda j: (0, idx(j) + off))


def _causal_window_sum(ext, w):
    s, k = ext, 1
    while k < w:
        s = s + pltpu.roll(s, k, 0)
        k *= 2
    return s


def _anticausal_window_sum(ext, w):
    s, k = ext, 1
    n = ext.shape[0]
    while k < w:
        s = s + pltpu.roll(s, n - k, 0)
        k *= 2
    return s


def _count(r0, rows, w):
    t = r0 + lax.broadcasted_iota(jnp.int32, (rows, LANES), 0)
    return jnp.minimum(t + 1, w).astype(F32)


def _pooled_loop(p_ref, pooled_s, w, T):
    R = R_POOL

    def chunk(i, carry):
        r0 = pl.multiple_of(i * R, R)
        h0 = pl.multiple_of(jnp.maximum(r0 - HIST, 0), HIST)
        ext = _load_ext(p_ref, r0, h0, i == 0, R)
        ws = _causal_window_sum(ext, w)[HIST:]
        pooled_s[pl.ds(r0, R), :] = (ws / _count(r0, R, w) - ext[HIST:]).astype(BF16)
        return carry

    lax.fori_loop(0, T // R, chunk, 0)


def _conv_w_spec(l):
    return pl.BlockSpec((None, None, 3, LANES), lambda j: (_conv_idx(j), l, 0, 0))


def _pool_w_spec(l):
    return pl.BlockSpec((None, None, LANES, LANES), lambda j: (l, _pool_idx(j), 0, 0))


def _pool_s_spec(l):
    return pl.BlockSpec((None, 1, LANES), lambda j: (l, 0, _pool_idx(j)))


def _pool_fwd_group(p_ref, g_ref, w_ref, s_ref, o_ref, pooled_s, mixed_s, w):
    T = p_ref.shape[0]
    R = R_POOL
    _pooled_loop(p_ref, pooled_s, w, T)
    mixed_s[...] = jnp.dot(pooled_s[...], w_ref[...].astype(BF16), preferred_element_type=F32)
    sc = s_ref[...]

    def chunk(i, carry):
        r0 = pl.multiple_of(i * R, R)
        g = g_ref[pl.ds(r0, R), :].astype(F32)
        o_ref[pl.ds(r0, R), :] = (mixed_s[pl.ds(r0, R), :] * sc * (g * _sigmoid(g))).astype(BF16)
        return carry

    lax.fori_loop(0, T // R, chunk, 0)


def mix_fwd(proj, wconv, wpool, pscale3, l):
    T = proj.shape[0]

    def body(u_ref, b_ref, c_ref, g_ref, p_ref, gp_ref, wc_ref, wp_ref, s_ref, ya_ref, yp_ref, pooled_ref, mixed_s):
        j = pl.program_id(0)
        pl.when(j < N_MIX)(functools.partial(_conv_fwd_block, u_ref, b_ref, c_ref, g_ref, wc_ref, ya_ref))
        for k, w in enumerate(POOL_WINDOWS):
            pl.when(j == N_MIX + k)(functools.partial(_pool_fwd_group, p_ref, gp_ref, wp_ref, s_ref, yp_ref,
                                                      pooled_ref, mixed_s, w))

    half = jax.ShapeDtypeStruct((T, N_MIX * LANES), BF16)
    pool_col = pl.BlockSpec((T, LANES), lambda j: (0, _pool_idx(j)))
    return pl.pallas_call(
        body, name="mix_fwd", grid=(2 * N_MIX,),
        in_specs=[_proj_col(T, 0, _conv_idx), _proj_col(T, 4, _conv_idx), _proj_col(T, 8, _conv_idx),
                  _proj_col(T, 12, _conv_idx), _proj_col(T, 16, _pool_idx), _proj_col(T, 20, _pool_idx),
                  _conv_w_spec(l), _pool_w_spec(l), _pool_s_spec(l)],
        out_specs=[pl.BlockSpec((T, LANES), lambda j: (0, _conv_idx(j))), pool_col, pool_col],
        out_shape=[half, half, half],
        scratch_shapes=[pltpu.VMEM((T, LANES), F32)],
        compiler_params=_params(),
    )(proj, proj, proj, proj, proj, proj, wconv, wpool, pscale3)


def out_fwd(ya, yp, wo, x, mod4, g_post3, l, after):
    T, D = x.shape
    H = ya.shape[1]
    tm = 512

    def body(ya_ref, yp_ref, wo_ref, x_ref, gt_ref, g_ref, after_ref, xn_ref, y_ref):
        y = (jnp.dot(ya_ref[...], wo_ref[0:H, :], preferred_element_type=F32)
             + jnp.dot(yp_ref[...], wo_ref[H:2 * H, :], preferred_element_type=F32))
        xn_ref[...] = x_ref[...] + gt_ref[...] * (y * _rms(y) * g_ref[...])
        y_ref[...] = y.astype(BF16)

    tile = pl.BlockSpec((tm, D), lambda i: (i, 0))
    half = pl.BlockSpec((tm, H), lambda i: (i, 0))
    return pl.pallas_call(
        body, name="out_fwd", grid=(T // tm,),
        in_specs=[half, half, pl.BlockSpec((2 * H, D), lambda i: (0, 0)), tile, _mod_row(l, 2, D), _layer_row(l, D),
                  ANY],
        out_specs=[tile, tile],
        out_shape=[jax.ShapeDtypeStruct((T, D), F32), jax.ShapeDtypeStruct((T, D), BF16)],
        compiler_params=_params(VMEM_BIG),
    )(ya, yp, wo, x, mod4, g_post3, after)


def out_fwd_loss(ya, yp, wo, x, mod4, g_post3, l, target):
    T, D = x.shape
    H = ya.shape[1]
    tm = 512
    nt = T // tm

    def body(ya_ref, yp_ref, wo_ref, x_ref, gt_ref, g_ref, t_ref, dx_ref, y_ref, l_ref, acc):
        i = pl.program_id(0)

        @pl.when(i == 0)
        def _():
            acc[...] = jnp.zeros_like(acc)

        y = (jnp.dot(ya_ref[...], wo_ref[0:H, :], preferred_element_type=F32)
             + jnp.dot(yp_ref[...], wo_ref[H:2 * H, :], preferred_element_type=F32))
        y_ref[...] = y.astype(BF16)
        d = (x_ref[...] + gt_ref[...] * (y * _rms(y) * g_ref[...])) - t_ref[...]
        dx_ref[...] = d * (1.0 / D)
        acc[...] += _colsum8(d * d)

        @pl.when(i == nt - 1)
        def _():
            l_ref[...] = jnp.zeros_like(l_ref) + jnp.sum(acc[...]) * (0.5 / D)

    tile = pl.BlockSpec((tm, D), lambda i: (i, 0))
    half = pl.BlockSpec((tm, H), lambda i: (i, 0))
    return pl.pallas_call(
        body, name="out_fwd_loss", grid=(nt,),
        in_specs=[half, half, pl.BlockSpec((2 * H, D), lambda i: (0, 0)), tile, _mod_row(l, 2, D), _layer_row(l, D),
                  tile],
        out_specs=[tile, tile, pl.BlockSpec((SUBLANES, LANES), lambda i: (0, 0))],
        out_shape=[jax.ShapeDtypeStruct((T, D), F32), jax.ShapeDtypeStruct((T, D), BF16),
                   jax.ShapeDtypeStruct((SUBLANES, LANES), F32)],
        scratch_shapes=[pltpu.VMEM((SUBLANES, D), F32)],
        compiler_params=_params(VMEM_BIG),
    )(ya, yp, wo, x, mod4, g_post3, target)


def out_bwd(dx, y, ya, yp, wo, mod4, g_post3, l, after):
    T, D = dx.shape
    H = ya.shape[1]
    tm = 512
    nt = T // tm

    def body(dx_ref, y_ref, ya_ref, yp_ref, wo_ref, gt_ref, g_ref, after_ref,
             dya_ref, dyp_ref, dwo_ref, dgt_ref, dg_ref, acc_w, acc_p):
        i = pl.program_id(0)

        @pl.when(i == 0)
        def _():
            acc_w[...] = jnp.zeros_like(acc_w)
            acc_p[...] = jnp.zeros_like(acc_p)

        yv = y_ref[...].astype(F32)
        dxv = dx_ref[...]
        gg = gt_ref[...] * g_ref[...]
        r = _rms(yv)
        yn = yv * r
        p = dxv * yn
        acc_p[...] += _colsum8(p)
        dy = r * (dxv * gg - yn * jnp.mean(p * gg, axis=-1, keepdims=True))
        dyb = dy.astype(BF16)
        dyc = lax.dot_general(dyb, wo_ref[...], NT, preferred_element_type=F32)
        dya_ref[...] = dyc[:, 0:H].astype(BF16)
        dyp_ref[...] = dyc[:, H:2 * H].astype(BF16)
        acc_w[0:H, :] += lax.dot_general(ya_ref[...], dyb, TN, preferred_element_type=F32)
        acc_w[H:2 * H, :] += lax.dot_general(yp_ref[...], dyb, TN, preferred_element_type=F32)

        @pl.when(i == nt - 1)
        def _():
            dwo_ref[...] = acc_w[...].astype(BF16)
            sp = jnp.sum(acc_p[...], axis=0, keepdims=True)
            dgt_ref[...] = g_ref[...] * sp
            dg_ref[...] = gt_ref[...] * sp

    row = pl.BlockSpec((1, D), lambda i: (0, 0))
    tile = pl.BlockSpec((tm, D), lambda i: (i, 0))
    half = pl.BlockSpec((tm, H), lambda i: (i, 0))
    full = pl.BlockSpec((2 * H, D), lambda i: (0, 0))
    return pl.pallas_call(
        body, name="out_bwd", grid=(nt,),
        in_specs=[tile, tile, half, half, full, _mod_row(l, 2, D), _layer_row(l, D), ANY],
        out_specs=[half, half, full, row, row],
        out_shape=[jax.ShapeDtypeStruct((T, H), BF16), jax.ShapeDtypeStruct((T, H), BF16),
                   jax.ShapeDtypeStruct((2 * H, D), BF16),
                   jax.ShapeDtypeStruct((1, D), F32), jax.ShapeDtypeStruct((1, D), F32)],
        scratch_shapes=[pltpu.VMEM((2 * H, D), F32), pltpu.VMEM((SUBLANES, D), F32)],
        compiler_params=_params(VMEM_BIG),
    )(dx, y, ya, yp, wo, mod4, g_post3, after)


def _conv_bwd_block(u_ref, b_ref, c_ref, g_ref, dy_ref, w_ref, du_ref, db_ref, dc_ref, dg_ref, dw_ref):
    T = u_ref.shape[0]
    R = R_CONV
    nchunk = T // R
    w0 = w_ref[pl.ds(0, 1), :]
    w1 = w_ref[pl.ds(1, 1), :]
    w2 = w_ref[pl.ds(2, 1), :]

    def chunk(k, carry):
        head, a0, a1, a2 = carry
        i = nchunk - 1 - k
        r0 = pl.multiple_of(i * R, R)
        h0 = pl.multiple_of(jnp.maximum(r0 - HIST, 0), HIST)
        first = i == 0
        ue = _load_ext(u_ref, r0, h0, first, R)
        ce = _load_ext(c_ref, r0, h0, first, R)
        ca = ce * ue
        ca0 = ca[HIST:]
        ca1 = _shift_down(ca, 1, R)
        ca2 = _shift_down(ca, 2, R)
        conv = w2 * ca0 + w1 * ca1 + w0 * ca2
        g = g_ref[pl.ds(r0, R), :].astype(F32)
        b = b_ref[pl.ds(r0, R), :].astype(F32)
        dy = dy_ref[pl.ds(r0, R), :].astype(F32)
        sg = _sigmoid(g)
        sl = g * sg
        t = dy * conv
        db_ref[pl.ds(r0, R), :] = (t * sl).astype(BF16)
        dg_ref[pl.ds(r0, R), :] = (t * b * (sg + sl * (1.0 - sg))).astype(BF16)
        dconv = dy * b * sl
        a2 = a2 + _colsum8(dconv * ca0)
        a1 = a1 + _colsum8(dconv * ca1)
        a0 = a0 + _colsum8(dconv * ca2)
        e = jnp.concatenate([dconv, head], axis=0)
        dca = w2 * dconv + w1 * _shift_up(e, 1, R) + w0 * _shift_up(e, 2, R)
        du_ref[pl.ds(r0, R), :] = (dca * ce[HIST:]).astype(BF16)
        dc_ref[pl.ds(r0, R), :] = (dca * ue[HIST:]).astype(BF16)
        return dconv[0:SUBLANES], a0, a1, a2

    z = jnp.zeros((SUBLANES, LANES), F32)
    _, a0, a1, a2 = lax.fori_loop(0, nchunk, chunk, (z, z, z, z))
    dw_ref[pl.ds(0, 1), :] = jnp.sum(a0, axis=0, keepdims=True)
    dw_ref[pl.ds(1, 1), :] = jnp.sum(a1, axis=0, keepdims=True)
    dw_ref[pl.ds(2, 1), :] = jnp.sum(a2, axis=0, keepdims=True)


def _pool_bwd_group(pooled_s, g_ref, dy_ref, w_ref, s_ref, du_ref, dg_ref, dw_ref, ds_ref,
                    mixed_s, dmix_s, dpool_s, w):
    T = pooled_s.shape[0]
    R = R_POOL
    nchunk = T // R
    wb = w_ref[...].astype(BF16)
    mixed_s[...] = jnp.dot(pooled_s[...], wb, preferred_element_type=F32)
    sc = s_ref[...]

    def gate_chunk(i, acc):
        r0 = pl.multiple_of(i * R, R)
        g = g_ref[pl.ds(r0, R), :].astype(F32)
        dy = dy_ref[pl.ds(r0, R), :].astype(F32)
        mixed = mixed_s[pl.ds(r0, R), :]
        sg = _sigmoid(g)
        sl = g * sg
        dg_ref[pl.ds(r0, R), :] = (dy * mixed * sc * (sg + sl * (1.0 - sg))).astype(BF16)
        dms = dy * sl
        dmix_s[pl.ds(r0, R), :] = (dms * sc).astype(BF16)
        return acc + _colsum8(dms * mixed)

    acc = lax.fori_loop(0, nchunk, gate_chunk, jnp.zeros((SUBLANES, LANES), F32))
    ds_ref[...] = jnp.sum(acc, axis=0, keepdims=True)
    dpool_s[pl.ds(0, T), :] = lax.dot_general(dmix_s[...], wb, NT, preferred_element_type=F32)
    dpool_s[pl.ds(T, HIST), :] = jnp.zeros((HIST, LANES), F32)
    dw_ref[...] = lax.dot_general(pooled_s[...], dmix_s[...], TN, preferred_element_type=F32).astype(BF16)

    def back_chunk(i, carry):
        r0 = pl.multiple_of(i * R, R)
        dpe = dpool_s[pl.ds(r0, R + HIST), :]
        e = dpe / _count(r0, R + HIST, w)
        du_ref[pl.ds(r0, R), :] = (_anticausal_window_sum(e, w)[0:R] - dpe[0:R]).astype(BF16)
        return carry

    lax.fori_loop(0, nchunk, back_chunk, 0)


def mix_bwd(proj, pooled, dya, dyp, wconv, wpool, pscale3, l, after):
    T = proj.shape[0]

    def body(u_ref, b_ref, c_ref, g_ref, pooled_ref, gp_ref, dya_ref, dyp_ref, wc_ref, wp_ref, s_ref, after_ref,
             dua_ref, dba_ref, dca_ref, dga_ref, dup_ref, dgp_ref, dwc_ref, dwp_ref, ds_ref,
             mixed_s, dmix_s, dpool_s):
        j = pl.program_id(0)
        pl.when(j < N_MIX)(functools.partial(_conv_bwd_block, u_ref, b_ref, c_ref, g_ref, dya_ref, wc_ref,
                                             dua_ref, dba_ref, dca_ref, dga_ref, dwc_ref))
        for k, w in enumerate(POOL_WINDOWS):
            pl.when(j == N_MIX + k)(functools.partial(_pool_bwd_group, pooled_ref, gp_ref, dyp_ref, wp_ref, s_ref,
                                                      dup_ref, dgp_ref, dwp_ref, ds_ref,
                                                      mixed_s, dmix_s, dpool_s, w))

    sec = jax.ShapeDtypeStruct((T, N_MIX * LANES), BF16)
    conv_col = pl.BlockSpec((T, LANES), lambda j: (0, _conv_idx(j)))
    pool_col = pl.BlockSpec((T, LANES), lambda j: (0, _pool_idx(j)))
    return pl.pallas_call(
        body, name="mix_bwd", grid=(2 * N_MIX,),
        in_specs=[_proj_col(T, 0, _conv_idx), _proj_col(T, 4, _conv_idx), _proj_col(T, 8, _conv_idx),
                  _proj_col(T, 12, _conv_idx), pool_col, _proj_col(T, 20, _pool_idx),
                  conv_col, pool_col, _conv_w_spec(l), _pool_w_spec(l), _pool_s_spec(l), ANY],
        out_specs=[conv_col, conv_col, conv_col, conv_col, pool_col, pool_col,
                   pl.BlockSpec((None, 3, LANES), lambda j: (_conv_idx(j), 0, 0)),
                   pl.BlockSpec((None, LANES, LANES), lambda j: (_pool_idx(j), 0, 0)),
                   pl.BlockSpec((1, LANES), lambda j: (0, _pool_idx(j)))],
        out_shape=[sec] * 6 + [jax.ShapeDtypeStruct((N_MIX, 3, LANES), F32),
                               jax.ShapeDtypeStruct((N_MIX, LANES, LANES), BF16),
                               jax.ShapeDtypeStruct((1, N_MIX * LANES), F32)],
        scratch_shapes=[pltpu.VMEM((T, LANES), F32), pltpu.VMEM((T, LANES), BF16), pltpu.VMEM((T + HIST, LANES), F32)],
        compiler_params=_params(),
    )(proj, proj, proj, proj, pooled, proj, dya, dyp, wconv, wpool, pscale3, after)


def in_bwd(dsecs, wg, x, dxo, mod4, g_pre3, l):
    T, D = x.shape
    NB = N_CHIPS
    CW = wg.shape[1] // NB
    SW = dsecs[0].shape[1]
    nsec = len(dsecs)
    PW = 256
    assert SW % PW == 0 and CW % PW == 0
    tm = 256
    nt = T // tm

    def body(*refs):
        d_refs = refs[0:nsec]
        w_ref, x_ref, dxo_ref, sh_ref, sc_ref, g_ref = refs[nsec:nsec + 6]
        dxi_ref, dw_ref, dsh_ref, dsc_ref, dg_ref = refs[nsec + 6:nsec + 11]
        acc_w, acc_sh, acc_q = refs[nsec + 11:]
        i = pl.program_id(0)

        @pl.when(i == 0)
        def _():
            acc_w[...] = jnp.zeros_like(acc_w)
            acc_sh[...] = jnp.zeros_like(acc_sh)
            acc_q[...] = jnp.zeros_like(acc_q)

        xv = x_ref[...]
        r = _rms(xv)
        xh = xv * r
        sg = g_ref[...] * (1.0 + sc_ref[...])
        hb = (xh * sg + sh_ref[...]).astype(BF16)
        dh = lax.dot_general(d_refs[0][...], w_ref[:, 0:SW], NT, preferred_element_type=F32)
        for s in range(1, nsec):
            dh = dh + lax.dot_general(d_refs[s][...], w_ref[:, s * SW:(s + 1) * SW], NT, preferred_element_type=F32)
        for p in range(nsec * SW // PW):
            col = p * PW
            s, so = col // SW, col % SW
            j, jo = col // CW, col % CW
            acc_w[j, :, jo:jo + PW] += lax.dot_general(hb, d_refs[s][:, so:so + PW], TN, preferred_element_type=F32)
        q = dh * xh
        acc_sh[...] += _colsum8(dh)
        acc_q[...] += _colsum8(q)
        dxi_ref[...] = dxo_ref[...] + r * (dh * sg - xh * jnp.mean(q * sg, axis=-1, keepdims=True))

        @pl.when(i == nt - 1)
        def _():
            dw_ref[...] = acc_w[...].astype(BF16)
            sq = jnp.sum(acc_q[...], axis=0, keepdims=True)
            dsh_ref[...] = jnp.sum(acc_sh[...], axis=0, keepdims=True)
            dsc_ref[...] = g_ref[...] * sq
            dg_ref[...] = (1.0 + sc_ref[...]) * sq

    row = pl.BlockSpec((1, D), lambda i: (0, 0))
    tile = pl.BlockSpec((tm, D), lambda i: (i, 0))
    sect = pl.BlockSpec((tm, SW), lambda i: (i, 0))
    rowshape = jax.ShapeDtypeStruct((1, D), F32)
    return pl.pallas_call(
        body, name="in_bwd", grid=(nt,),
        in_specs=[sect] * nsec + [pl.BlockSpec((D, NB * CW), lambda i: (0, 0)), tile, tile,
                                  _mod_row(l, 0, D), _mod_row(l, 1, D), _layer_row(l, D)],
        out_specs=[tile, pl.BlockSpec((NB, D, CW), lambda i: (0, 0, 0)), row, row, row],
        out_shape=[jax.ShapeDtypeStruct((T, D), F32), jax.ShapeDtypeStruct((NB, D, CW), BF16),
                   rowshape, rowshape, rowshape],
        scratch_shapes=[pltpu.VMEM((NB, D, CW), F32),
                        pltpu.VMEM((SUBLANES, D), F32), pltpu.VMEM((SUBLANES, D), F32)],
        compiler_params=_params(VMEM_BIG),
    )(*dsecs, wg, x, dxo, mod4, mod4, g_pre3)


def _rcopy(src, dst, ssem, rsem, dev):
    return pltpu.make_async_remote_copy(src_ref=src, dst_ref=dst, send_sem=ssem, recv_sem=rsem,
                                        device_id=dev, device_id_type=MESH)


def _peers7(x, y, c):
    out = []
    for m in range(1, N_DEV):
        bx, by, bc = (m >> 2) & 1, (m >> 1) & 1, m & 1
        out.append(((1 - x) if bx else x, (1 - y) if by else y, (1 - c) if bc else c))
    return out


HBM = pl.BlockSpec(memory_space=pltpu.HBM)
SEM = pl.BlockSpec(memory_space=pltpu.SEMAPHORE)
SPLIT = pltpu.CompilerParams(has_side_effects=pltpu.SideEffectType.DATAFLOW_SIDE_EFFECTING)


def _hbm(a):
    return pltpu.with_memory_space_constraint(a, pltpu.HBM)


def _chips(x, y):
    return [(1 - x, y), (x, 1 - y), (1 - x, 1 - y)]


SIBLING_BARRIER_ID = 0


def xchg_start(name, bufs, n_copies, plan, sibling_only=False, after=()):
    n = len(bufs)
    after = list(after)

    def body(*refs):
        ssem, rsem, token = refs[n + len(after)], refs[n + len(after) + 1], refs[-1]
        x, y, c = _me()
        if sibling_only:
            barrier = pltpu.get_barrier_semaphore()
            pl.semaphore_signal(barrier, inc=1, device_id=(x, y, 1 - c), device_id_type=MESH)
            pl.semaphore_wait(barrier, 1)
        copies = plan(refs[0:n], x, y, c)
        assert len(copies) == n_copies
        for k, (src, dst, peer, _) in enumerate(copies):
            _rcopy(src, dst, ssem.at[k], rsem.at[k], peer).start()
        token[...] = jnp.zeros_like(token)

    params = dict(has_side_effects=pltpu.SideEffectType.DATAFLOW_SIDE_EFFECTING)
    if sibling_only:
        params["collective_id"] = SIBLING_BARRIER_ID
    outs = pl.pallas_call(
        body, name=name,
        in_specs=[HBM] * n + [ANY] * len(after),
        out_specs=[SEM, SEM] + [HBM] * n + [pl.BlockSpec(memory_space=pltpu.VMEM)],
        out_shape=([pltpu.SemaphoreType.DMA((n_copies,))] * 2 + [pltpu.HBM(b.shape, b.dtype) for b in bufs]
                   + [jax.ShapeDtypeStruct((SUBLANES, LANES), F32)]),
        input_output_aliases={a: 2 + a for a in range(n)},
        compiler_params=pltpu.CompilerParams(**params),
    )(*[_hbm(b) for b in bufs], *after)
    return outs[0], outs[1], list(outs[2:2 + n]), outs[-1]


def xchg_wait(name, bufs, ssem, rsem, n_copies, plan, after, sems=None):
    n = len(bufs)
    after = list(after)
    sems = tuple(range(n_copies)) if sems is None else tuple(sems)
    assert len(sems) == n_copies

    def body(*refs):
        ssem_ref, rsem_ref = refs[n], refs[n + 1]
        copies = plan(refs[0:n], *_me())
        assert len(copies) == n_copies
        for k, (src, _, peer, land) in zip(sems, copies):
            cp = _rcopy(src, land, ssem_ref.at[k], rsem_ref.at[k], peer)
            cp.wait_send()
            cp.wait_recv()

    outs = pl.pallas_call(
        body, name=name,
        in_specs=[HBM] * n + [SEM, SEM] + [ANY] * len(after), out_specs=[HBM] * n,
        out_shape=[pltpu.HBM(b.shape, b.dtype) for b in bufs],
        input_output_aliases={a: a for a in range(n)},
        compiler_params=SPLIT,
    )(*bufs, ssem, rsem, *after)
    return list(outs)


def _shard_half(buf, chip, half):
    if len(buf.shape) == 2:
        h, w = buf.shape[0] // 2, buf.shape[1] // N_CHIPS
        return buf.at[pl.ds(half * h, h), pl.ds(chip * w, w)]
    h = buf.shape[1] // 2
    return buf.at[chip, pl.ds(half * h, h)]


def plan_gather(refs, x, y, c):
    out = []
    for (px, py) in _chips(x, y):
        for buf in refs:
            own = _shard_half(buf, 2 * x + y, c)
            out.append((own, own, (px, py, c), _shard_half(buf, 2 * px + py, c)))
    return out


def plan_forward(refs, x, y, c):
    out = []
    for (px, py) in _chips(x, y):
        for buf in refs:
            landed = _shard_half(buf, 2 * px + py, c)
            out.append((landed, landed, (x, y, 1 - c), _shard_half(buf, 2 * px + py, 1 - c)))
    return out


def plan_sibling(refs, x, y, c):
    n = len(refs) // 2
    out = []
    for a in range(n):
        h = refs[a].shape[1] // 2
        out.append((refs[a].at[:, pl.ds((1 - c) * h, h)], refs[n + a], (x, y, 1 - c), refs[n + a]))
    return out


def plan_chip(refs, x, y, c):
    n = len(refs) // 2
    out = []
    for j, (px, py) in enumerate(_chips(x, y)):
        for a in range(n):
            out.append((refs[a].at[2 * px + py], refs[n + a].at[j], (px, py, c), refs[n + a].at[j]))
    return out


def plan_mod(refs, x, y, c):
    (mods,) = refs
    mine = mods.at[2 * x + y]
    return [(mine, mine, (px, py, c), mods.at[2 * px + py]) for (px, py) in _chips(x, y)]


def plan_pack(refs, x, y, c):
    (packs,) = refs
    mine = packs.at[4 * x + 2 * y + c]
    return [(mine, mine, peer, packs.at[4 * peer[0] + 2 * peer[1] + peer[2]]) for peer in _peers7(x, y, c)]


def plan_spread(layers, wp_layers):
    def plan(refs, x, y, c):
        gi, go, gp = refs
        hD, hR, hP = gi.shape[1] // 2, go.shape[1] // 2, gp.shape[2] // 2
        sib = (x, y, 1 - c)
        out = []
        for l in layers:
            mine = gi.at[l, pl.ds(c * hD, hD)]
            out.append((mine, mine, sib, gi.at[l, pl.ds((1 - c) * hD, hD)]))
            mine = go.at[l, pl.ds(c * hR, hR)]
            out.append((mine, mine, sib, go.at[l, pl.ds((1 - c) * hR, hR)]))
        for l in wp_layers:
            mine = gp.at[l, 2 * x + y, pl.ds(c * hP, hP)]
            for peer in _peers7(x, y, c):
                out.append((mine, mine, peer, gp.at[l, 2 * peer[0] + peer[1], pl.ds(peer[2] * hP, hP)]))
        return out

    return plan


def gather_small(c8, wc):
    def body(c_ref, wc_ref, call, wcall, ssem, rsem, lsem):
        x, y, c = _me()
        myc = 2 * x + y
        me_lin = 4 * x + 2 * y + c
        me = (x, y, c)
        local = [pltpu.make_async_copy(c_ref, call.at[me_lin], lsem.at[0]),
                 pltpu.make_async_copy(wc_ref, wcall.at[myc], lsem.at[1])]
        for cp in local:
            cp.start()
        sends, recvs = [], []
        for m, peer in enumerate(_peers7(x, y, c)):
            plin = 4 * peer[0] + 2 * peer[1] + peer[2]
            sends.append(_rcopy(c_ref, call.at[me_lin], ssem.at[m], rsem.at[m], peer))
            recvs.append(_rcopy(call.at[plin], call.at[plin], ssem.at[m], rsem.at[m], me))
        for j, (px, py) in enumerate([(1 - x, y), (x, 1 - y), (1 - x, 1 - y)]):
            pc = 2 * px + py
            sends.append(_rcopy(wc_ref, wcall.at[myc], ssem.at[7 + j], rsem.at[7 + j], (px, py, c)))
            recvs.append(_rcopy(wcall.at[pc], wcall.at[pc], ssem.at[7 + j], rsem.at[7 + j], me))
        for cp in sends:
            cp.start()
        for cp in recvs:
            cp.wait_recv()
        for cp in sends:
            cp.wait_send()
        for cp in local:
            cp.wait()

    return pl.pallas_call(
        body, name="gather_small",
        in_specs=[ANY] * 2, out_specs=[ANY] * 2,
        out_shape=[jax.ShapeDtypeStruct((N_DEV, SUBLANES, LANES), F32),
                   jax.ShapeDtypeStruct((N_CHIPS, wc.shape[0], 3, LANES), F32)],
        scratch_shapes=[pltpu.SemaphoreType.DMA((10,)), pltpu.SemaphoreType.DMA((10,)), pltpu.SemaphoreType.DMA((2,))],
        compiler_params=_params(n_grid=0),
    )(c8, wc)


def spread_now(gi, go, gp, layers, wp_layers):
    plan = plan_spread(layers, wp_layers)
    n = 2 * len(layers) + 7 * len(wp_layers)

    def body(gi_in, go_in, gp_in, gi, go, gp, ssem, rsem):
        copies = plan((gi, go, gp), *_me())
        me = _me()
        sends = [_rcopy(src, dst, ssem.at[k], rsem.at[k], peer) for k, (src, dst, peer, _) in enumerate(copies)]
        for cp in sends:
            cp.start()
        for k, (_, _, _, land) in enumerate(copies):
            _rcopy(land, land, ssem.at[k], rsem.at[k], me).wait_recv()
        for cp in sends:
            cp.wait_send()

    return pl.pallas_call(
        body, name="spread_now",
        in_specs=[ANY] * 3, out_specs=[ANY] * 3,
        out_shape=[jax.ShapeDtypeStruct(a.shape, a.dtype) for a in (gi, go, gp)],
        input_output_aliases={0: 0, 1: 1, 2: 2},
        scratch_shapes=[pltpu.SemaphoreType.DMA((n,)), pltpu.SemaphoreType.DMA((n,))],
        compiler_params=_params(n_grid=0),
    )(gi, go, gp)


def add_sibling(cidx, mine, sib):
    def body(c_ref, *refs):
        for a in range(3):
            m, s, o = refs[a], refs[3 + a], refs[6 + a]
            o[...] = (m[...].astype(F32) + s[...].astype(F32)).astype(BF16)

    per_step = 2

    def mine_spec(a):
        h = a.shape[1] // 2
        return pl.BlockSpec((per_step, h, a.shape[2]), lambda j, c_ref: (j, c_ref[0], 0))

    def sib_spec(a):
        return pl.BlockSpec((per_step,) + a.shape[1:], lambda j, c_ref: (j, 0, 0))

    return pl.pallas_call(
        body, name="add_sibling",
        grid_spec=pltpu.PrefetchScalarGridSpec(
            num_scalar_prefetch=1, grid=(N_CHIPS // per_step,),
            in_specs=[mine_spec(a) for a in mine] + [sib_spec(a) for a in sib],
            out_specs=[sib_spec(a) for a in sib]),
        out_shape=[jax.ShapeDtypeStruct(a.shape, BF16) for a in sib],
        compiler_params=_params(VMEM_BIG),
    )(cidx, *mine, *sib)


def sum_chips(pos, own, rb, acc, l, shapes):
    nq = 2
    n_in = 6 + (3 if acc is not None else 0)

    def body(pos_ref, *refs):
        for a in range(3):
            m, b, o = refs[a], refs[3 + a], refs[n_in + a]
            s = m[...].astype(F32)
            for j in range(3):
                s = s + b[j].astype(F32)
            o[...] = s

    def own_spec(a):
        return pl.BlockSpec((None, a.shape[1] // nq, a.shape[2]), lambda q, p: (p[1], q, 0))

    def rb_spec(a):
        return pl.BlockSpec((3, a.shape[1] // nq, a.shape[2]), lambda q, p: (0, q, 0))

    hi, ho, hp = own[0].shape[1] // nq, own[1].shape[1] // nq, own[2].shape[1] // nq
    out_specs = [pl.BlockSpec((None, hi, shapes[0][2]), lambda q, p: (l, p[0] * nq + q, 0)),
                 pl.BlockSpec((None, ho, shapes[1][2]), lambda q, p: (l, p[0] * nq + q, 0)),
                 pl.BlockSpec((None, None, hp, LANES), lambda q, p: (l, p[1], p[0] * nq + q, 0))]
    in_specs = [own_spec(a) for a in own] + [rb_spec(a) for a in rb]
    args = list(own) + list(rb)
    aliases = {}
    if acc is not None:
        in_specs += [ANY] * 3
        args += list(acc)
        aliases = {7: 0, 8: 1, 9: 2}
    return pl.pallas_call(
        body, name="sum_chips",
        grid_spec=pltpu.PrefetchScalarGridSpec(num_scalar_prefetch=1, grid=(nq,), in_specs=in_specs, out_specs=out_specs),
        out_shape=[jax.ShapeDtypeStruct(s, F32) for s in shapes],
        input_output_aliases=aliases,
        compiler_params=_params(VMEM_BIG),
    )(pos, *args)


def _wconv_slot(chip, tap):
    idx = 3 * chip + tap
    return ROW_WCONV + idx // SUBLANES, slice((idx % SUBLANES) * LANES, (idx % SUBLANES + 1) * LANES)


def pack_small(pos, per_layer, loss_blk):
    L = len(per_layer)
    D = per_layer[0][0].shape[1]

    def body(pos_ref, *refs):
        o = refs[-1]
        lb = refs[-2]
        o[...] = jnp.zeros_like(o)
        for l in range(L):
            dgpre, dgpost, dsh, dsc, dgt, dps, dwc = refs[7 * l:7 * l + 7]
            base = SUBLANES * l
            o[pl.ds(base + ROW_G_PRE, 1), :] = dgpre[...]
            o[pl.ds(base + ROW_G_POST, 1), :] = dgpost[...]
            for r, src in enumerate((dsh, dsc, dgt)):
                o[pl.ds(base + ROW_MOD + r, 1), :] = src[...]
            o[pl.ds(base + ROW_PSCALE, 1), 0:dps.shape[1]] = dps[...]
            for j in range(dwc.shape[0]):
                for k in range(3):
                    row, lanes = _wconv_slot(j, k)
                    o[pl.ds(base + row, 1), lanes] = dwc[j, pl.ds(k, 1), :]
        o[pl.ds(ROW_PSCALE, 1), LOSS_LANES] = lb[pl.ds(0, 1), :]

    flat = [a for layer in per_layer for a in layer] + [loss_blk]

    def whole(a):
        return pl.BlockSpec(a.shape, lambda i, p: (0,) * a.ndim)

    return pl.pallas_call(
        body, name="pack_small",
        grid_spec=pltpu.PrefetchScalarGridSpec(
            num_scalar_prefetch=1, grid=(1,), in_specs=[whole(a) for a in flat],
            out_specs=pl.BlockSpec((None, L * SUBLANES, D), lambda i, p: (p[2], 0, 0))),
        out_shape=jax.ShapeDtypeStruct((N_DEV, L * SUBLANES, D), F32),
        compiler_params=_params(),
    )(pos, *flat)


def small_update(pos, packs, params, moments_m, moments_v):
    n = len(params)
    L, D = params[1].shape
    PS = params[3].shape[1]

    def body(pos_ref, p_ref, *refs):
        ws, ms, vs = refs[0:n], refs[n:2 * n], refs[2 * n:3 * n]
        loss_ref = refs[3 * n]
        outs = [refs[3 * n + 1 + 4 * t:3 * n + 5 + 4 * t] for t in range(n)]
        summed = refs[-1]
        s = p_ref[0]
        for d in range(1, N_DEV):
            s = s + p_ref[d]
        summed[...] = s
        loss_ref[...] = summed[pl.ds(ROW_PSCALE, 1), LOSS_LANES]
        chip = pos_ref[1]

        def update(t, idx, g):
            d, mm, vv = _adamw_math(ws[t][idx], g, ms[t][idx], vs[t][idx])
            g_ref, d_ref, mo_ref, vo_ref = outs[t]
            g_ref[idx] = g
            d_ref[idx] = d
            mo_ref[idx] = mm
            vo_ref[idx] = vv

        for l in range(L):
            base = SUBLANES * l
            row = pl.ds(l, 1)
            for k in range(3):
                update(0, (row, slice(k * D, (k + 1) * D)), summed[pl.ds(base + ROW_MOD + k, 1), :])
            update(1, (row, slice(None)), summed[pl.ds(base + ROW_G_PRE, 1), :])
            update(2, (row, slice(None)), summed[pl.ds(base + ROW_G_POST, 1), :])
            update(3, (row, slice(None)), summed[pl.ds(base + ROW_PSCALE, 1), 0:PS])
            for k in range(3):
                g = None
                for j in range(N_CHIPS):
                    wrow, lanes = _wconv_slot(j, k)
                    cand = summed[pl.ds(base + wrow, 1), lanes]
                    g = cand if g is None else jnp.where(chip == j, cand, g)
                update(4, (l, pl.ds(k, 1), slice(None)), g)

    def whole(a):
        return pl.BlockSpec(a.shape, lambda i, p: (0,) * a.ndim)

    ins = [packs] + list(params) + list(moments_m) + list(moments_v)
    out_shape = [jax.ShapeDtypeStruct((1, LANES), F32)]
    for w in params:
        out_shape += [jax.ShapeDtypeStruct(w.shape, F32)] * 4
    outs = pl.pallas_call(
        body, name="small_update",
        grid_spec=pltpu.PrefetchScalarGridSpec(
            num_scalar_prefetch=1, grid=(1,), in_specs=[whole(a) for a in ins],
            out_specs=[whole(a) for a in out_shape],
            scratch_shapes=[pltpu.VMEM(packs.shape[1:], F32)]),
        out_shape=out_shape,
        compiler_params=_params(),
    )(pos, *ins)
    return outs[0], [outs[1 + 4 * t:5 + 4 * t] for t in range(n)]


def _adamw_math(w, g, m, v):
    m = ADAM_B1 * m + (1.0 - ADAM_B1) * g
    v = ADAM_B2 * v + (1.0 - ADAM_B2) * (g * g)
    m_hat = m / (1.0 - ADAM_B1 ** ADAM_STEP)
    v_hat = v / (1.0 - ADAM_B2 ** ADAM_STEP)
    delta = -ADAM_LR * (m_hat / (jnp.sqrt(v_hat) + ADAM_EPS) + ADAM_WD * w)
    return delta, m, v


def adamw(w, g, m, v, block, name, first=0, count=None, acc=None):
    grid = tuple(s // b for s, b in zip(w.shape, block))
    if count is not None:
        grid = (count,) + grid[1:]

    def body(w_ref, g_ref, m_ref, v_ref, *rest):
        go_ref, d_ref, mo_ref, vo_ref = rest[-4:]
        gv = g_ref[...]
        d, mm, vv = _adamw_math(w_ref[...], gv, m_ref[...], v_ref[...])
        go_ref[...] = gv
        d_ref[...] = d
        mo_ref[...] = mm
        vo_ref[...] = vv

    spec = pl.BlockSpec(block, lambda i, *rest: (first + i,) + rest)
    shape = jax.ShapeDtypeStruct(w.shape, F32)
    extra = [] if acc is None else list(acc)
    return pl.pallas_call(
        body, name=name, grid=grid,
        in_specs=[spec] * 4 + [ANY] * len(extra), out_specs=[spec] * 4, out_shape=[shape] * 4,
        input_output_aliases={4 + a: a for a in range(len(extra))},
        compiler_params=_params(VMEM_BIG, n_grid=len(grid)),
    )(w, g, m, v, *extra)


def ada_finish(c_all, dmod, w, m, v):
    L, D, CW = w.shape
    hD = D // 2

    def body(c_ref, d_ref, w_ref, m_ref, v_ref, g_ref, dl_ref, mo_ref, vo_ref):
        cv = c_ref[...]
        z = jnp.zeros_like(cv)
        ca = jnp.concatenate([cv * jax.nn.sigmoid(cv), z], axis=0).astype(BF16)
        dm = jnp.concatenate([d_ref[0], jnp.zeros_like(d_ref[0])], axis=0).astype(BF16)
        g = lax.dot_general(ca, dm, TN, preferred_element_type=F32)
        g_ref[0] = g
        d, mm, vv = _adamw_math(w_ref[0], g, m_ref[0], v_ref[0])
        dl_ref[0] = d
        mo_ref[0] = mm
        vo_ref[0] = vv

    big = pl.BlockSpec((1, hD, CW), lambda l, h: (l, h, 0))
    shape = jax.ShapeDtypeStruct(w.shape, F32)
    return pl.pallas_call(
        body, name="ada_finish", grid=(L, 2),
        in_specs=[pl.BlockSpec((N_DEV, hD), lambda l, h: (0, h)), pl.BlockSpec((1, N_DEV, CW), lambda l, h: (l, 0, 0)),
                  big, big, big],
        out_specs=[big] * 4, out_shape=[shape] * 4,
        compiler_params=_params(VMEM_BIG, n_grid=2),
    )(c_all, dmod, w, m, v)


def kernel(x, c, w_ada, b_ada, g_pre, w_in, w_conv, w_pool, pool_scale, w_out, g_post, loss_target, m_w_ada, m_b_ada, m_g_pre, m_w_in, m_w_conv, m_w_pool, m_pool_scale, m_w_out, m_g_post, v_w_ada, v_b_ada, v_g_pre, v_w_in, v_w_conv, v_w_pool, v_pool_scale, v_w_out, v_g_post):
    L, D, CW = w_in.shape
    RO = w_out.shape[1]
    T = x.shape[1]
    ix, iy, ic = _me()
    chip = 2 * ix + iy
    me_lin = 4 * ix + 2 * iy + ic

    pos = jnp.stack([ic, chip, me_lin]).astype(jnp.int32)
    g_pre3, g_post3 = g_pre.reshape(L, 1, D), g_post.reshape(L, 1, D)
    pscale3 = pool_scale.reshape(L, 1, pool_scale.shape[1])
    n_s, n_c = 3, 9

    def gather(bufs, after):
        ss, rs, bufs, tok = xchg_start("gather_start", bufs, 3 * len(bufs), plan_gather, after=after)
        return (ss, rs, bufs), tok

    def arrive(flight, after):
        ss, rs, bufs = flight
        bufs = xchg_wait("gather_wait", bufs, ss, rs, 3 * len(bufs), plan_gather, after)
        fss, frs, bufs, tok = xchg_start("forward_start", bufs, 3 * len(bufs), plan_forward, sibling_only=True)
        return (fss, frs, bufs), tok

    def ready(flight, after):
        fss, frs, bufs = flight
        return xchg_wait("forward_wait", bufs, fss, frs, 3 * len(bufs), plan_forward, after)

    def arrive_part(flight, which, after):
        ss, rs, bufs = flight
        sems = tuple(range(which, 3 * len(bufs), len(bufs)))
        (buf,) = xchg_wait("gather_wait", [bufs[which]], ss, rs, 3, plan_gather, after, sems=sems)
        fss, frs, (buf,), tok = xchg_start("forward_start", [buf], 3, plan_forward, sibling_only=True)
        return (fss, frs, [buf]), tok

    c_all3, wconv_all = gather_small(c.reshape(SUBLANES, LANES), w_conv)
    c_all = c_all3.reshape(N_DEV, D)
    gi0, go0 = cast_weights(pos, w_in, w_out, 0, c_all3)
    fly_in0, token = gather([gi0], [])
    b_my = lax.dynamic_slice_in_dim(b_ada, chip * CW, CW, axis=1)
    m_ss, m_rs, mods, token = xchg_start("mod_start", [mod_part(pos, c_all, w_ada, b_my, token)], 3, plan_mod)
    fly_out0, token = gather([go0], [token])
    flying_w = [None] * L
    for l in range(1, L):
        flying_w[l], token = gather(list(cast_weights(pos, w_in, w_out, l, token)), [])
    fwd_in0, token = arrive(fly_in0, [token])
    (mod_all,) = xchg_wait("mod_wait", mods, m_ss, m_rs, 3, plan_mod, [token])
    mod = lax.dynamic_index_in_dim(mod_all, me_lin, axis=2, keepdims=False)
    mod4 = jnp.transpose(mod, (1, 0, 2)).reshape(L, 3, 1, D)

    xs, projs, yas, yps, ys, pooleds = [x.reshape(T, D)], [], [], [], [], []
    wg_in, wg_out = [], []
    fwd_in = fwd_in0
    for l in range(L):
        (gi,) = ready(fwd_in, [mod4 if l == 0 else xs[l]])
        proj = proj_fwd(xs[l], mod4, g_pre3, gi, l)
        ya, yp, pooled = mix_fwd(proj, wconv_all, w_pool, pscale3, l)
        pooleds.append(pooled)
        after = [ya, yp]
        if l == 0:
            fwd_out, token = arrive(fly_out0, after)
        else:
            fwd_out, token = arrive_part(flying_w[l], 1, after)
        after = [token]
        if l + 1 < L:
            fwd_in, token = arrive_part(flying_w[l + 1], 0, after)
            after = [token]
        (go,) = ready(fwd_out, after)
        wg_in.append(gi)
        wg_out.append(go.reshape(N_CHIPS * RO, D))
        projs.append(proj)
        yas.append(ya)
        yps.append(yp)
        if l + 1 < L:
            xn, yv = out_fwd(ya, yp, wg_out[l], xs[l], mod4, g_post3, l, after[0])
            xs.append(xn)
        else:
            dx, yv, loss_blk = out_fwd_loss(ya, yp, wg_out[l], xs[l], mod4, g_post3, l, loss_target.reshape(T, D))
        ys.append(yv)

    shapes = (w_in.shape, w_out.shape, w_pool.shape)
    smalls = [None] * L
    acc, flying, sib, token = None, None, None, loss_blk

    def to_chips(sib, after):
        sl, s_ss, s_rs, s_bufs = sib
        s_bufs = xchg_wait("sibling_wait", s_bufs, s_ss, s_rs, n_s, plan_sibling, after)
        chip_parts = add_sibling(pos, s_bufs[0:3], s_bufs[3:6])
        lands = [lax.empty((3,) + a.shape[1:], a.dtype) for a in chip_parts]
        c_ss, c_rs, c_bufs, ctoken = xchg_start("chip_start", list(chip_parts) + lands, n_c, plan_chip)
        return (sl, c_ss, c_rs, c_bufs), ctoken

    def landed(flying, acc, after):
        fl, f_ss, f_rs, f_bufs = flying
        f_bufs = xchg_wait("chip_wait", f_bufs, f_ss, f_rs, n_c, plan_chip, after)
        return sum_chips(pos, f_bufs[0:3], f_bufs[3:6], acc, fl, shapes)

    for l in reversed(range(L)):
        dya, dyp, dwo_l, dgate, dgpost = out_bwd(dx, ys[l], yas[l], yps[l], wg_out[l], mod4, g_post3, l, token)
        token = dya
        if sib is not None:
            arrived = flying
            flying, token = to_chips(sib, [dya])
            if arrived is not None:
                acc = landed(arrived, acc, [token])
                token = acc[0]
        du_a, db_a, dc_a, dg_a, du_p, dg_p, dwc, dwp_l, dps = mix_bwd(projs[l], pooleds[l], dya, dyp, wconv_all, w_pool,
                                                                        pscale3, l, token)
        dx, dwi_l, dshift, dscale, dgpre = in_bwd([du_a, db_a, dc_a, dg_a, du_p, dg_p], wg_in[l], xs[l], dx,
                                                  mod4, g_pre3, l)
        smalls[l] = (dgpre, dgpost, dshift, dscale, dgate, dps, dwc)
        parts = [dwi_l, dwo_l.reshape(N_CHIPS, RO, D), dwp_l]
        s_lands = [lax.empty((a.shape[0], a.shape[1] // 2) + a.shape[2:], a.dtype) for a in parts]
        s_ss, s_rs, s_bufs, token = xchg_start("sibling_start", parts + s_lands, n_s, plan_sibling, sibling_only=True)
        sib = (l, s_ss, s_rs, s_bufs)
    grad_x = dx.reshape(1, T, D)

    p_ss, p_rs, packs, ptoken = xchg_start("pack_start", [pack_small(pos, smalls, loss_blk)], N_DEV - 1, plan_pack)
    acc = landed(flying, acc, [ptoken, token])
    n_sp = (2 + N_DEV - 1) * (L - 1)
    spread = plan_spread(tuple(range(1, L)), tuple(range(1, L)))
    sp_ss, sp_rs, acc, sp_token = xchg_start("spread_start", list(acc), n_sp, spread)
    flying, token = to_chips(sib, [sp_token])
    (packs_all,) = xchg_wait("pack_wait", packs, p_ss, p_rs, N_DEV - 1, plan_pack, [token])
    dmod_all = packs_all.reshape(N_DEV, L, SUBLANES, D)[:, :, ROW_MOD:ROW_MOD + 3].reshape(N_DEV, L, 3 * D)
    dmod_my = jnp.transpose(lax.dynamic_slice_in_dim(dmod_all, chip * CW, CW, axis=2), (1, 0, 2))

    g_w_ada, d_w_ada, nm_w_ada, nv_w_ada = ada_finish(c_all, dmod_my, w_ada, m_w_ada, v_w_ada)
    loss_row, upd = small_update(pos, packs_all, [b_ada, g_pre, g_post, pool_scale, w_conv],
                                 [m_b_ada, m_g_pre, m_g_post, m_pool_scale, m_w_conv],
                                 [v_b_ada, v_g_pre, v_g_post, v_pool_scale, v_w_conv])
    loss = loss_row[0, 0]
    (g_b_ada, d_b_ada, nm_b_ada, nv_b_ada), (g_g_pre, d_g_pre, nm_g_pre, nv_g_pre) = upd[0], upd[1]
    (g_g_post, d_g_post, nm_g_post, nv_g_post), (g_pscale, d_pscale, nm_pscale, nv_pscale) = upd[2], upd[3]
    g_w_conv, d_w_conv, nm_w_conv, nv_w_conv = upd[4]

    done = [nv_w_ada, nv_w_conv]
    g_w_in, g_w_out, g_w_pool = xchg_wait("spread_wait", acc, sp_ss, sp_rs, n_sp, spread, done)
    in_blk, out_blk = (1, D // 2, CW), (1, RO, D)
    upd_in = adamw(w_in, g_w_in, m_w_in, v_w_in, in_blk, "adamw_w_in", 1, L - 1)
    upd_out = adamw(w_out, g_w_out, m_w_out, v_w_out, out_blk, "adamw_w_out", 1, L - 1)

    acc = landed(flying, (g_w_in, g_w_out, g_w_pool), [upd_in[3], upd_out[3]])
    r_w_in, r_w_out, r_w_pool = spread_now(*acc, (0,), (0,))
    g_w_in, d_w_in, nm_w_in, nv_w_in = adamw(w_in, r_w_in, m_w_in, v_w_in, in_blk, "adamw_w_in", 0, 1, upd_in)
    g_w_out, d_w_out, nm_w_out, nv_w_out = adamw(w_out, r_w_out, m_w_out, v_w_out, out_blk, "adamw_w_out", 0, 1, upd_out)
    pshape = (L, N_CHIPS * LANES, LANES)
    upd_pool = adamw(w_pool.reshape(pshape), r_w_pool.reshape(pshape), m_w_pool.reshape(pshape),
                     v_w_pool.reshape(pshape), (1,) + pshape[1:], "adamw_w_pool")
    g_w_pool, d_w_pool, nm_w_pool, nv_w_pool = [a.reshape(w_pool.shape) for a in upd_pool]

    return (loss, grad_x,
            g_w_ada, g_b_ada, g_g_pre, g_w_in, g_w_conv, g_w_pool, g_pscale, g_w_out, g_g_post,
            d_w_ada, d_b_ada, d_g_pre, d_w_in, d_w_conv, d_w_pool, d_pscale, d_w_out, d_g_post,
            nm_w_ada, nm_b_ada, nm_g_pre, nm_w_in, nm_w_conv, nm_w_pool, nm_pscale, nm_w_out, nm_g_post,
            nv_w_ada, nv_b_ada, nv_g_pre, nv_w_in, nv_w_conv, nv_w_pool, nv_pscale, nv_w_out, nv_g_post)
```

```python
import functools

import jax
import jax.numpy as jnp
from jax import lax
from jax.experimental import pallas as pl
from jax.experimental.pallas import tpu as pltpu

F32 = jnp.float32
BF16 = jnp.bfloat16
MESH = pl.DeviceIdType.MESH
ANY = pl.BlockSpec(memory_space=pl.ANY)

NORM_EPS = 1e-6
POOL_WINDOWS = (2, 4, 8, 16)
ADAM_LR = 0.001
ADAM_B1 = 0.9
ADAM_B2 = 0.999
ADAM_EPS = 1e-08
ADAM_WD = 0.01
ADAM_STEP = 10

N_CHIPS = 4
N_DEV = 8
LANES = 128
SUBLANES = 8
VMEM_BIG = 56 * 1024 * 1024
HIST = 16
R_CONV = 64
R_POOL = 128

ROW_G_PRE, ROW_G_POST, ROW_MOD, ROW_PSCALE, ROW_WCONV = 0, 1, 2, 5, 6
LOSS_LANES = slice(4 * LANES, 5 * LANES)

NT = (((1,), (1,)), ((), ()))
TN = (((0,), (0,)), ((), ()))


def _params(vmem=None, n_grid=1):
    kw = {}
    if n_grid:
        kw["dimension_semantics"] = ("arbitrary",) * n_grid
    if vmem is not None:
        kw["vmem_limit_bytes"] = vmem
    return pltpu.CompilerParams(**kw)


def _colsum8(v):
    n, d = v.shape
    return v.reshape(n // SUBLANES, SUBLANES, d).sum(axis=0)


def _rms(v):
    return lax.rsqrt(jnp.mean(v * v, axis=-1, keepdims=True) + NORM_EPS)


def _sigmoid(v):
    return 0.5 * jnp.tanh(0.5 * v) + 0.5


def _shift_down(ext, k, rows):
    if k == 0:
        return ext[HIST:HIST + rows]
    return pltpu.roll(ext, k, 0)[HIST:HIST + rows]


def _shift_up(ext, k, rows):
    if k == 0:
        return ext[0:rows]
    return pltpu.roll(ext, ext.shape[0] - k, 0)[0:rows]


def _load_ext(ref, r0, h0, first, rows):
    hist = ref[pl.ds(h0, HIST), :].astype(F32)
    hist = jnp.where(first, 0.0, hist)
    cur = ref[pl.ds(r0, rows), :].astype(F32)
    return jnp.concatenate([hist, cur], axis=0)


def _me():
    return lax.axis_index("x"), lax.axis_index("y"), lax.axis_index("c")


def cast_weights(pos, w_in, w_out, l, after):
    _, D, CW = w_in.shape
    RO = w_out.shape[1]

    def body(pos_ref, wi, wo, after_ref, oi, oo):
        oi[...] = wi[...].astype(BF16)
        oo[...] = wo[...].astype(BF16)

    return pl.pallas_call(
        body, name="cast_w",
        grid_spec=pltpu.PrefetchScalarGridSpec(
            num_scalar_prefetch=1, grid=(2,),
            in_specs=[pl.BlockSpec((None, D // 2, CW), lambda h, p: (l, h, 0)),
                      pl.BlockSpec((None, RO // 2, D), lambda h, p: (l, h, 0)), ANY],
            out_specs=[pl.BlockSpec((D // 2, CW), lambda h, p: (h, p[1])),
                       pl.BlockSpec((None, RO // 2, D), lambda h, p: (p[1], h, 0))]),
        out_shape=[jax.ShapeDtypeStruct((D, N_CHIPS * CW), BF16), jax.ShapeDtypeStruct((N_CHIPS, RO, D), BF16)],
        compiler_params=_params(),
    )(pos, w_in, w_out, after)


def mod_part(pos, c_all, w_ada, b_my, after):
    L, D, CW = w_ada.shape

    def body(pos_ref, c_ref, w_ref, b_ref, after_ref, o_ref):
        cv = c_ref[...]
        ca = (cv * jax.nn.sigmoid(cv)).astype(BF16)
        o_ref[...] = jnp.dot(ca, w_ref[0].astype(BF16), preferred_element_type=F32) + b_ref[0]

    return pl.pallas_call(
        body, name="mod_part",
        grid_spec=pltpu.PrefetchScalarGridSpec(
            num_scalar_prefetch=1, grid=(L,),
            in_specs=[pl.BlockSpec((N_DEV, D), lambda l, p: (0, 0)),
                      pl.BlockSpec((1, D, CW), lambda l, p: (l, 0, 0)),
                      pl.BlockSpec((1, 1, CW), lambda l, p: (l, 0, 0)), ANY],
            out_specs=pl.BlockSpec((None, None, N_DEV, CW), lambda l, p: (p[1], l, 0, 0))),
        out_shape=jax.ShapeDtypeStruct((N_CHIPS, L, N_DEV, CW), F32),
        compiler_params=_params(VMEM_BIG),
    )(pos, c_all, w_ada, b_my.reshape(L, 1, CW), after)


def _mod_row(l, k, D):
    return pl.BlockSpec((None, None, 1, D), lambda *_: (l, k, 0, 0))


def _layer_row(l, D):
    return pl.BlockSpec((None, 1, D), lambda *_: (l, 0, 0))


def proj_fwd(x, mod4, g_pre3, wg, l):
    T, D = x.shape
    NC = wg.shape[1]
    NB = N_CHIPS
    CW = NC // NB
    tm = 512

    def body(x_ref, sh_ref, sc_ref, g_ref, w_ref, o_ref):
        xv = x_ref[...]
        h = (xv * _rms(xv)) * (g_ref[...] * (1.0 + sc_ref[...])) + sh_ref[...]
        hb = h.astype(BF16)
        for j in range(NB):
            cols = slice(j * CW, (j + 1) * CW)
            o_ref[:, cols] = jnp.dot(hb, w_ref[:, cols], preferred_element_type=F32).astype(BF16)

    return pl.pallas_call(
        body, name="proj_fwd", grid=(T // tm,),
        in_specs=[pl.BlockSpec((tm, D), lambda i: (i, 0)), _mod_row(l, 0, D), _mod_row(l, 1, D), _layer_row(l, D),
                  pl.BlockSpec((D, NC), lambda i: (0, 0))],
        out_specs=pl.BlockSpec((tm, NC), lambda i: (i, 0)),
        out_shape=jax.ShapeDtypeStruct((T, NC), BF16),
        compiler_params=_params(VMEM_BIG),
    )(x, mod4, mod4, g_pre3, wg)


N_MIX = 4


def _conv_fwd_block(u_ref, b_ref, c_ref, g_ref, w_ref, o_ref):
    T = u_ref.shape[0]
    R = 2 * R_CONV
    w0 = w_ref[pl.ds(0, 1), :]
    w1 = w_ref[pl.ds(1, 1), :]
    w2 = w_ref[pl.ds(2, 1), :]

    def chunk(i, carry):
        r0 = pl.multiple_of(i * R, R)
        h0 = pl.multiple_of(jnp.maximum(r0 - HIST, 0), HIST)
        first = i == 0
        ca = _load_ext(c_ref, r0, h0, first, R) * _load_ext(u_ref, r0, h0, first, R)
        conv = w2 * ca[HIST:] + w1 * _shift_down(ca, 1, R) + w0 * _shift_down(ca, 2, R)
        g = g_ref[pl.ds(r0, R), :].astype(F32)
        b = b_ref[pl.ds(r0, R), :].astype(F32)
        o_ref[pl.ds(r0, R), :] = (b * conv * (g * _sigmoid(g))).astype(BF16)
        return carry

    lax.fori_loop(0, T // R, chunk, 0)


def _conv_idx(j):
    return jnp.minimum(j, N_MIX - 1)


def _pool_idx(j):
    return jnp.maximum(j - N_MIX, 0)


def _proj_col(T, off, idx):
    return pl.BlockSpec((T, LANES), lambda j: (0, idx(j) + off))


def _causal_window_sum(ext, w):
    s, k = ext, 1
    while k < w:
        s = s + pltpu.roll(s, k, 0)
        k *= 2
    return s


def _anticausal_window_sum(ext, w):
    s, k = ext, 1
    n = ext.shape[0]
    while k < w:
        s = s + pltpu.roll(s, n - k, 0)
        k *= 2
    return s


def _count(r0, rows, w):
    t = r0 + lax.broadcasted_iota(jnp.int32, (rows, LANES), 0)
    return jnp.minimum(t + 1, w).astype(F32)


def _pooled_loop(p_ref, pooled_s, w, T):
    R = R_POOL

    def chunk(i, carry):
        r0 = pl.multiple_of(i * R, R)
        h0 = pl.multiple_of(jnp.maximum(r0 - HIST, 0), HIST)
        ext = _load_ext(p_ref, r0, h0, i == 0, R)
        ws = _causal_window_sum(ext, w)[HIST:]
        pooled_s[pl.ds(r0, R), :] = (ws / _count(r0, R, w) - ext[HIST:]).astype(BF16)
        return carry

    lax.fori_loop(0, T // R, chunk, 0)


def _conv_w_spec(l):
    return pl.BlockSpec((None, None, 3, LANES), lambda j: (_conv_idx(j), l, 0, 0))


def _pool_w_spec(l):
    return pl.BlockSpec((None, None, LANES, LANES), lambda j: (l, _pool_idx(j), 0, 0))


def _pool_s_spec(l):
    return pl.BlockSpec((None, 1, LANES), lambda j: (l, 0, _pool_idx(j)))


def _pool_fwd_group(p_ref, g_ref, w_ref, s_ref, o_ref, pooled_s, mixed_s, w):
    T = p_ref.shape[0]
    R = R_POOL
    _pooled_loop(p_ref, pooled_s, w, T)
    mixed_s[...] = jnp.dot(pooled_s[...], w_ref[...].astype(BF16), preferred_element_type=F32)
    sc = s_ref[...]

    def chunk(i, carry):
        r0 = pl.multiple_of(i * R, R)
        g = g_ref[pl.ds(r0, R), :].astype(F32)
        o_ref[pl.ds(r0, R), :] = (mixed_s[pl.ds(r0, R), :] * sc * (g * _sigmoid(g))).astype(BF16)
        return carry

    lax.fori_loop(0, T // R, chunk, 0)


def mix_fwd(proj, wconv, wpool, pscale3, l):
    T = proj.shape[0]

    def body(u_ref, b_ref, c_ref, g_ref, p_ref, gp_ref, wc_ref, wp_ref, s_ref, ya_ref, yp_ref, pooled_ref, mixed_s):
        j = pl.program_id(0)
        pl.when(j < N_MIX)(functools.partial(_conv_fwd_block, u_ref, b_ref, c_ref, g_ref, wc_ref, ya_ref))
        for k, w in enumerate(POOL_WINDOWS):
            pl.when(j == N_MIX + k)(functools.partial(_pool_fwd_group, p_ref, gp_ref, wp_ref, s_ref, yp_ref,
                                                      pooled_ref, mixed_s, w))

    half = jax.ShapeDtypeStruct((T, N_MIX * LANES), BF16)
    pool_col = pl.BlockSpec((T, LANES), lambda j: (0, _pool_idx(j)))
    return pl.pallas_call(
        body, name="mix_fwd", grid=(2 * N_MIX,),
        in_specs=[_proj_col(T, 0, _conv_idx), _proj_col(T, 4, _conv_idx), _proj_col(T, 8, _conv_idx),
                  _proj_col(T, 12, _conv_idx), _proj_col(T, 16, _pool_idx), _proj_col(T, 20, _pool_idx),
                  _conv_w_spec(l), _pool_w_spec(l), _pool_s_spec(l)],
        out_specs=[pl.BlockSpec((T, LANES), lambda j: (0, _conv_idx(j))), pool_col, pool_col],
        out_shape=[half, half, half],
        scratch_shapes=[pltpu.VMEM((T, LANES), F32)],
        compiler_params=_params(),
    )(proj, proj, proj, proj, proj, proj, wconv, wpool, pscale3)


def out_fwd(ya, yp, wo, x, mod4, g_post3, l, after):
    T, D = x.shape
    H = ya.shape[1]
    tm = 512

    def body(ya_ref, yp_ref, wo_ref, x_ref, gt_ref, g_ref, after_ref, xn_ref, y_ref):
        y = (jnp.dot(ya_ref[...], wo_ref[0:H, :], preferred_element_type=F32)
             + jnp.dot(yp_ref[...], wo_ref[H:2 * H, :], preferred_element_type=F32))
        xn_ref[...] = x_ref[...] + gt_ref[...] * (y * _rms(y) * g_ref[...])
        y_ref[...] = y.astype(BF16)

    tile = pl.BlockSpec((tm, D), lambda i: (i, 0))
    half = pl.BlockSpec((tm, H), lambda i: (i, 0))
    return pl.pallas_call(
        body, name="out_fwd", grid=(T // tm,),
        in_specs=[half, half, pl.BlockSpec((2 * H, D), lambda i: (0, 0)), tile, _mod_row(l, 2, D), _layer_row(l, D),
                  ANY],
        out_specs=[tile, tile],
        out_shape=[jax.ShapeDtypeStruct((T, D), F32), jax.ShapeDtypeStruct((T, D), BF16)],
        compiler_params=_params(VMEM_BIG),
    )(ya, yp, wo, x, mod4, g_post3, after)


def out_fwd_loss(ya, yp, wo, x, mod4, g_post3, l, target):
    T, D = x.shape
    H = ya.shape[1]
    tm = 512
    nt = T // tm

    def body(ya_ref, yp_ref, wo_ref, x_ref, gt_ref, g_ref, t_ref, dx_ref, y_ref, l_ref, acc):
        i = pl.program_id(0)

        @pl.when(i == 0)
        def _():
            acc[...] = jnp.zeros_like(acc)

        y = (jnp.dot(ya_ref[...], wo_ref[0:H, :], preferred_element_type=F32)
             + jnp.dot(yp_ref[...], wo_ref[H:2 * H, :], preferred_element_type=F32))
        y_ref[...] = y.astype(BF16)
        d = (x_ref[...] + gt_ref[...] * (y * _rms(y) * g_ref[...])) - t_ref[...]
        dx_ref[...] = d * (1.0 / D)
        acc[...] += _colsum8(d * d)

        @pl.when(i == nt - 1)
        def _():
            l_ref[...] = jnp.zeros_like(l_ref) + jnp.sum(acc[...]) * (0.5 / D)

    tile = pl.BlockSpec((tm, D), lambda i: (i, 0))
    half = pl.BlockSpec((tm, H), lambda i: (i, 0))
    return pl.pallas_call(
        body, name="out_fwd_loss", grid=(nt,),
        in_specs=[half, half, pl.BlockSpec((2 * H, D), lambda i: (0, 0)), tile, _mod_row(l, 2, D), _layer_row(l, D),
                  tile],
        out_specs=[tile, tile, pl.BlockSpec((SUBLANES, LANES), lambda i: (0, 0))],
        out_shape=[jax.ShapeDtypeStruct((T, D), F32), jax.ShapeDtypeStruct((T, D), BF16),
                   jax.ShapeDtypeStruct((SUBLANES, LANES), F32)],
        scratch_shapes=[pltpu.VMEM((SUBLANES, D), F32)],
        compiler_params=_params(VMEM_BIG),
    )(ya, yp, wo, x, mod4, g_post3, target)


def out_bwd(dx, y, ya, yp, wo, mod4, g_post3, l, after):
    T, D = dx.shape
    H = ya.shape[1]
    tm = 512
    nt = T // tm

    def body(dx_ref, y_ref, ya_ref, yp_ref, wo_ref, gt_ref, g_ref, after_ref,
             dya_ref, dyp_ref, dwo_ref, dgt_ref, dg_ref, acc_w, acc_p):
        i = pl.program_id(0)

        @pl.when(i == 0)
        def _():
            acc_w[...] = jnp.zeros_like(acc_w)
            acc_p[...] = jnp.zeros_like(acc_p)

        yv = y_ref[...].astype(F32)
        dxv = dx_ref[...]
        gg = gt_ref[...] * g_ref[...]
        r = _rms(yv)
        yn = yv * r
        p = dxv * yn
        acc_p[...] += _colsum8(p)
        dy = r * (dxv * gg - yn * jnp.mean(p * gg, axis=-1, keepdims=True))
        dyb = dy.astype(BF16)
        dyc = lax.dot_general(dyb, wo_ref[...], NT, preferred_element_type=F32)
        dya_ref[...] = dyc[:, 0:H].astype(BF16)
        dyp_ref[...] = dyc[:, H:2 * H].astype(BF16)
        acc_w[0:H, :] += lax.dot_general(ya_ref[...], dyb, TN, preferred_element_type=F32)
        acc_w[H:2 * H, :] += lax.dot_general(yp_ref[...], dyb, TN, preferred_element_type=F32)

        @pl.when(i == nt - 1)
        def _():
            dwo_ref[...] = acc_w[...].astype(BF16)
            sp = jnp.sum(acc_p[...], axis=0, keepdims=True)
            dgt_ref[...] = g_ref[...] * sp
            dg_ref[...] = gt_ref[...] * sp

    row = pl.BlockSpec((1, D), lambda i: (0, 0))
    tile = pl.BlockSpec((tm, D), lambda i: (i, 0))
    half = pl.BlockSpec((tm, H), lambda i: (i, 0))
    full = pl.BlockSpec((2 * H, D), lambda i: (0, 0))
    return pl.pallas_call(
        body, name="out_bwd", grid=(nt,),
        in_specs=[tile, tile, half, half, full, _mod_row(l, 2, D), _layer_row(l, D), ANY],
        out_specs=[half, half, full, row, row],
        out_shape=[jax.ShapeDtypeStruct((T, H), BF16), jax.ShapeDtypeStruct((T, H), BF16),
                   jax.ShapeDtypeStruct((2 * H, D), BF16),
                   jax.ShapeDtypeStruct((1, D), F32), jax.ShapeDtypeStruct((1, D), F32)],
        scratch_shapes=[pltpu.VMEM((2 * H, D), F32), pltpu.VMEM((SUBLANES, D), F32)],
        compiler_params=_params(VMEM_BIG),
    )(dx, y, ya, yp, wo, mod4, g_post3, after)


def _conv_bwd_block(u_ref, b_ref, c_ref, g_ref, dy_ref, w_ref, du_ref, db_ref, dc_ref, dg_ref, dw_ref):
    T = u_ref.shape[0]
    R = R_CONV
    nchunk = T // R
    w0 = w_ref[pl.ds(0, 1), :]
    w1 = w_ref[pl.ds(1, 1), :]
    w2 = w_ref[pl.ds(2, 1), :]

    def chunk(k, carry):
        head, a0, a1, a2 = carry
        i = nchunk - 1 - k
        r0 = pl.multiple_of(i * R, R)
        h0 = pl.multiple_of(jnp.maximum(r0 - HIST, 0), HIST)
        first = i == 0
        ue = _load_ext(u_ref, r0, h0, first, R)
        ce = _load_ext(c_ref, r0, h0, first, R)
        ca = ce * ue
        ca0 = ca[HIST:]
        ca1 = _shift_down(ca, 1, R)
        ca2 = _shift_down(ca, 2, R)
        conv = w2 * ca0 + w1 * ca1 + w0 * ca2
        g = g_ref[pl.ds(r0, R), :].astype(F32)
        b = b_ref[pl.ds(r0, R), :].astype(F32)
        dy = dy_ref[pl.ds(r0, R), :].astype(F32)
        sg = _sigmoid(g)
        sl = g * sg
        t = dy * conv
        db_ref[pl.ds(r0, R), :] = (t * sl).astype(BF16)
        dg_ref[pl.ds(r0, R), :] = (t * b * (sg + sl * (1.0 - sg))).astype(BF16)
        dconv = dy * b * sl
        a2 = a2 + _colsum8(dconv * ca0)
        a1 = a1 + _colsum8(dconv * ca1)
        a0 = a0 + _colsum8(dconv * ca2)
        e = jnp.concatenate([dconv, head], axis=0)
        dca = w2 * dconv + w1 * _shift_up(e, 1, R) + w0 * _shift_up(e, 2, R)
        du_ref[pl.ds(r0, R), :] = (dca * ce[HIST:]).astype(BF16)
        dc_ref[pl.ds(r0, R), :] = (dca * ue[HIST:]).astype(BF16)
        return dconv[0:SUBLANES], a0, a1, a2

    z = jnp.zeros((SUBLANES, LANES), F32)
    _, a0, a1, a2 = lax.fori_loop(0, nchunk, chunk, (z, z, z, z))
    dw_ref[pl.ds(0, 1), :] = jnp.sum(a0, axis=0, keepdims=True)
    dw_ref[pl.ds(1, 1), :] = jnp.sum(a1, axis=0, keepdims=True)
    dw_ref[pl.ds(2, 1), :] = jnp.sum(a2, axis=0, keepdims=True)


def _pool_bwd_group(pooled_s, g_ref, dy_ref, w_ref, s_ref, du_ref, dg_ref, dw_ref, ds_ref,
                    mixed_s, dmix_s, dpool_s, w):
    T = pooled_s.shape[0]
    R = R_POOL
    nchunk = T // R
    wb = w_ref[...].astype(BF16)
    mixed_s[...] = jnp.dot(pooled_s[...], wb, preferred_element_type=F32)
    sc = s_ref[...]

    def gate_chunk(i, acc):
        r0 = pl.multiple_of(i * R, R)
        g = g_ref[pl.ds(r0, R), :].astype(F32)
        dy = dy_ref[pl.ds(r0, R), :].astype(F32)
        mixed = mixed_s[pl.ds(r0, R), :]
        sg = _sigmoid(g)
        sl = g * sg
        dg_ref[pl.ds(r0, R), :] = (dy * mixed * sc * (sg + sl * (1.0 - sg))).astype(BF16)
        dms = dy * sl
        dmix_s[pl.ds(r0, R), :] = (dms * sc).astype(BF16)
        return acc + _colsum8(dms * mixed)

    acc = lax.fori_loop(0, nchunk, gate_chunk, jnp.zeros((SUBLANES, LANES), F32))
    ds_ref[...] = jnp.sum(acc, axis=0, keepdims=True)
    dpool_s[pl.ds(0, T), :] = lax.dot_general(dmix_s[...], wb, NT, preferred_element_type=F32)
    dpool_s[pl.ds(T, HIST), :] = jnp.zeros((HIST, LANES), F32)
    dw_ref[...] = lax.dot_general(pooled_s[...], dmix_s[...], TN, preferred_element_type=F32).astype(BF16)

    def back_chunk(i, carry):
        r0 = pl.multiple_of(i * R, R)
        dpe = dpool_s[pl.ds(r0, R + HIST), :]
        e = dpe / _count(r0, R + HIST, w)
        du_ref[pl.ds(r0, R), :] = (_anticausal_window_sum(e, w)[0:R] - dpe[0:R]).astype(BF16)
        return carry

    lax.fori_loop(0, nchunk, back_chunk, 0)


def mix_bwd(proj, pooled, dya, dyp, wconv, wpool, pscale3, l, after):
    T = proj.shape[0]

    def body(u_ref, b_ref, c_ref, g_ref, pooled_ref, gp_ref, dya_ref, dyp_ref, wc_ref, wp_ref, s_ref, after_ref,
             dua_ref, dba_ref, dca_ref, dga_ref, dup_ref, dgp_ref, dwc_ref, dwp_ref, ds_ref,
             mixed_s, dmix_s, dpool_s):
        j = pl.program_id(0)
        pl.when(j < N_MIX)(functools.partial(_conv_bwd_block, u_ref, b_ref, c_ref, g_ref, dya_ref, wc_ref,
                                             dua_ref, dba_ref, dca_ref, dga_ref, dwc_ref))
        for k, w in enumerate(POOL_WINDOWS):
            pl.when(j == N_MIX + k)(functools.partial(_pool_bwd_group, pooled_ref, gp_ref, dyp_ref, wp_ref, s_ref,
                                                      dup_ref, dgp_ref, dwp_ref, ds_ref,
                                                      mixed_s, dmix_s, dpool_s, w))

    sec = jax.ShapeDtypeStruct((T, N_MIX * LANES), BF16)
    conv_col = pl.BlockSpec((T, LANES), lambda j: (0, _conv_idx(j)))
    pool_col = pl.BlockSpec((T, LANES), lambda j: (0, _pool_idx(j)))
    return pl.pallas_call(
        body, name="mix_bwd", grid=(2 * N_MIX,),
        in_specs=[_proj_col(T, 0, _conv_idx), _proj_col(T, 4, _conv_idx), _proj_col(T, 8, _conv_idx),
                  _proj_col(T, 12, _conv_idx), pool_col, _proj_col(T, 20, _pool_idx),
                  conv_col, pool_col, _conv_w_spec(l), _pool_w_spec(l), _pool_s_spec(l), ANY],
        out_specs=[conv_col, conv_col, conv_col, conv_col, pool_col, pool_col,
                   pl.BlockSpec((None, 3, LANES), lambda j: (_conv_idx(j), 0, 0)),
                   pl.BlockSpec((None, LANES, LANES), lambda j: (_pool_idx(j), 0, 0)),
                   pl.BlockSpec((1, LANES), lambda j: (0, _pool_idx(j)))],
        out_shape=[sec] * 6 + [jax.ShapeDtypeStruct((N_MIX, 3, LANES), F32),
                               jax.ShapeDtypeStruct((N_MIX, LANES, LANES), BF16),
                               jax.ShapeDtypeStruct((1, N_MIX * LANES), F32)],
        scratch_shapes=[pltpu.VMEM((T, LANES), F32), pltpu.VMEM((T, LANES), BF16), pltpu.VMEM((T + HIST, LANES), F32)],
        compiler_params=_params(),
    )(proj, proj, proj, proj, pooled, proj, dya, dyp, wconv, wpool, pscale3, after)


def in_bwd(dsecs, wg, x, dxo, mod4, g_pre3, l):
    T, D = x.shape
    NB = N_CHIPS
    CW = wg.shape[1] // NB
    SW = dsecs[0].shape[1]
    nsec = len(dsecs)
    PW = 256
    assert SW % PW == 0 and CW % PW == 0
    tm = 256
    nt = T // tm

    def body(*refs):
        d_refs = refs[0:nsec]
        w_ref, x_ref, dxo_ref, sh_ref, sc_ref, g_ref = refs[nsec:nsec + 6]
        dxi_ref, dw_ref, dsh_ref, dsc_ref, dg_ref = refs[nsec + 6:nsec + 11]
        acc_w, acc_sh, acc_q = refs[nsec + 11:]
        i = pl.program_id(0)

        @pl.when(i == 0)
        def _():
            acc_w[...] = jnp.zeros_like(acc_w)
            acc_sh[...] = jnp.zeros_like(acc_sh)
            acc_q[...] = jnp.zeros_like(acc_q)

        xv = x_ref[...]
        r = _rms(xv)
        xh = xv * r
        sg = g_ref[...] * (1.0 + sc_ref[...])
        hb = (xh * sg + sh_ref[...]).astype(BF16)
        dh = lax.dot_general(d_refs[0][...], w_ref[:, 0:SW], NT, preferred_element_type=F32)
        for s in range(1, nsec):
            dh = dh + lax.dot_general(d_refs[s][...], w_ref[:, s * SW:(s + 1) * SW], NT, preferred_element_type=F32)
        for p in range(nsec * SW // PW):
            col = p * PW
            s, so = col // SW, col % SW
            j, jo = col // CW, col % CW
            acc_w[j, :, jo:jo + PW] += lax.dot_general(hb, d_refs[s][:, so:so + PW], TN, preferred_element_type=F32)
        q = dh * xh
        acc_sh[...] += _colsum8(dh)
        acc_q[...] += _colsum8(q)
        dxi_ref[...] = dxo_ref[...] + r * (dh * sg - xh * jnp.mean(q * sg, axis=-1, keepdims=True))

        @pl.when(i == nt - 1)
        def _():
            dw_ref[...] = acc_w[...].astype(BF16)
            sq = jnp.sum(acc_q[...], axis=0, keepdims=True)
            dsh_ref[...] = jnp.sum(acc_sh[...], axis=0, keepdims=True)
            dsc_ref[...] = g_ref[...] * sq
            dg_ref[...] = (1.0 + sc_ref[...]) * sq

    row = pl.BlockSpec((1, D), lambda i: (0, 0))
    tile = pl.BlockSpec((tm, D), lambda i: (i, 0))
    sect = pl.BlockSpec((tm, SW), lambda i: (i, 0))
    rowshape = jax.ShapeDtypeStruct((1, D), F32)
    return pl.pallas_call(
        body, name="in_bwd", grid=(nt,),
        in_specs=[sect] * nsec + [pl.BlockSpec((D, NB * CW), lambda i: (0, 0)), tile, tile,
                                  _mod_row(l, 0, D), _mod_row(l, 1, D), _layer_row(l, D)],
        out_specs=[tile, pl.BlockSpec((NB, D, CW), lambda i: (0, 0, 0)), row, row, row],
        out_shape=[jax.ShapeDtypeStruct((T, D), F32), jax.ShapeDtypeStruct((NB, D, CW), BF16),
                   rowshape, rowshape, rowshape],
        scratch_shapes=[pltpu.VMEM((NB, D, CW), F32),
                        pltpu.VMEM((SUBLANES, D), F32), pltpu.VMEM((SUBLANES, D), F32)],
        compiler_params=_params(VMEM_BIG),
    )(*dsecs, wg, x, dxo, mod4, mod4, g_pre3)


def _rcopy(src, dst, ssem, rsem, dev):
    return pltpu.make_async_remote_copy(src_ref=src, dst_ref=dst, send_sem=ssem, recv_sem=rsem,
                                        device_id=dev, device_id_type=MESH)


def _peers7(x, y, c):
    out = []
    for m in range(1, N_DEV):
        bx, by, bc = (m >> 2) & 1, (m >> 1) & 1, m & 1
        out.append(((1 - x) if bx else x, (1 - y) if by else y, (1 - c) if bc else c))
    return out


HBM = pl.BlockSpec(memory_space=pltpu.HBM)
SEM = pl.BlockSpec(memory_space=pltpu.SEMAPHORE)
SPLIT = pltpu.CompilerParams(has_side_effects=pltpu.SideEffectType.DATAFLOW_SIDE_EFFECTING)


def _hbm(a):
    return pltpu.with_memory_space_constraint(a, pltpu.HBM)


def _chips(x, y):
    return [(1 - x, y), (x, 1 - y), (1 - x, 1 - y)]


SIBLING_BARRIER_ID = 0


def xchg_start(name, bufs, n_copies, plan, sibling_only=False, after=()):
    n = len(bufs)
    after = list(after)

    def body(*refs):
        ssem, rsem, token = refs[n + len(after)], refs[n + len(after) + 1], refs[-1]
        x, y, c = _me()
        if sibling_only:
            barrier = pltpu.get_barrier_semaphore()
            pl.semaphore_signal(barrier, inc=1, device_id=(x, y, 1 - c), device_id_type=MESH)
            pl.semaphore_wait(barrier, 1)
        copies = plan(refs[0:n], x, y, c)
        assert len(copies) == n_copies
        for k, (src, dst, peer, _) in enumerate(copies):
            _rcopy(src, dst, ssem.at[k], rsem.at[k], peer).start()
        token[...] = jnp.zeros_like(token)

    params = dict(has_side_effects=pltpu.SideEffectType.DATAFLOW_SIDE_EFFECTING)
    if sibling_only:
        params["collective_id"] = SIBLING_BARRIER_ID
    outs = pl.pallas_call(
        body, name=name,
        in_specs=[HBM] * n + [ANY] * len(after),
        out_specs=[SEM, SEM] + [HBM] * n + [pl.BlockSpec(memory_space=pltpu.VMEM)],
        out_shape=([pltpu.SemaphoreType.DMA((n_copies,))] * 2 + [pltpu.HBM(b.shape, b.dtype) for b in bufs]
                   + [jax.ShapeDtypeStruct((SUBLANES, LANES), F32)]),
        input_output_aliases={a: 2 + a for a in range(n)},
        compiler_params=pltpu.CompilerParams(**params),
    )(*[_hbm(b) for b in bufs], *after)
    return outs[0], outs[1], list(outs[2:2 + n]), outs[-1]


def xchg_wait(name, bufs, ssem, rsem, n_copies, plan, after, sems=None):
    n = len(bufs)
    after = list(after)
    sems = tuple(range(n_copies)) if sems is None else tuple(sems)
    assert len(sems) == n_copies

    def body(*refs):
        ssem_ref, rsem_ref = refs[n], refs[n + 1]
        copies = plan(refs[0:n], *_me())
        assert len(copies) == n_copies
        for k, (src, _, peer, land) in zip(sems, copies):
            cp = _rcopy(src, land, ssem_ref.at[k], rsem_ref.at[k], peer)
            cp.wait_send()
            cp.wait_recv()

    outs = pl.pallas_call(
        body, name=name,
        in_specs=[HBM] * n + [SEM, SEM] + [ANY] * len(after), out_specs=[HBM] * n,
        out_shape=[pltpu.HBM(b.shape, b.dtype) for b in bufs],
        input_output_aliases={a: a for a in range(n)},
        compiler_params=SPLIT,
    )(*bufs, ssem, rsem, *after)
    return list(outs)


def _shard_half(buf, chip, half):
    if len(buf.shape) == 2:
        h, w = buf.shape[0] // 2, buf.shape[1] // N_CHIPS
        return buf.at[pl.ds(half * h, h), pl.ds(chip * w, w)]
    h = buf.shape[1] // 2
    return buf.at[chip, pl.ds(half * h, h)]


def plan_gather(refs, x, y, c):
    out = []
    for (px, py) in _chips(x, y):
        for buf in refs:
            own = _shard_half(buf, 2 * x + y, c)
            out.append((own, own, (px, py, c), _shard_half(buf, 2 * px + py, c)))
    return out


def plan_forward(refs, x, y, c):
    out = []
    for (px, py) in _chips(x, y):
        for buf in refs:
            landed = _shard_half(buf, 2 * px + py, c)
            out.append((landed, landed, (x, y, 1 - c), _shard_half(buf, 2 * px + py, 1 - c)))
    return out


def plan_sibling(refs, x, y, c):
    n = len(refs) // 2
    out = []
    for a in range(n):
        h = refs[a].shape[1] // 2
        out.append((refs[a].at[:, pl.ds((1 - c) * h, h)], refs[n + a], (x, y, 1 - c), refs[n + a]))
    return out


def plan_chip(refs, x, y, c):
    n = len(refs) // 2
    out = []
    for j, (px, py) in enumerate(_chips(x, y)):
        for a in range(n):
            out.append((refs[a].at[2 * px + py], refs[n + a].at[j], (px, py, c), refs[n + a].at[j]))
    return out


def plan_mod(refs, x, y, c):
    (mods,) = refs
    mine = mods.at[2 * x + y]
    return [(mine, mine, (px, py, c), mods.at[2 * px + py]) for (px, py) in _chips(x, y)]


def plan_pack(refs, x, y, c):
    (packs,) = refs
    mine = packs.at[4 * x + 2 * y + c]
    return [(mine, mine, peer, packs.at[4 * peer[0] + 2 * peer[1] + peer[2]]) for peer in _peers7(x, y, c)]


def plan_spread(layers, wp_layers):
    def plan(refs, x, y, c):
        gi, go, gp = refs
        hD, hR, hP = gi.shape[1] // 2, go.shape[1] // 2, gp.shape[2] // 2
        sib = (x, y, 1 - c)
        out = []
        for l in layers:
            mine = gi.at[l, pl.ds(c * hD, hD)]
            out.append((mine, mine, sib, gi.at[l, pl.ds((1 - c) * hD, hD)]))
            mine = go.at[l, pl.ds(c * hR, hR)]
            out.append((mine, mine, sib, go.at[l, pl.ds((1 - c) * hR, hR)]))
        for l in wp_layers:
            mine = gp.at[l, 2 * x + y, pl.ds(c * hP, hP)]
            for peer in _peers7(x, y, c):
                out.append((mine, mine, peer, gp.at[l, 2 * peer[0] + peer[1], pl.ds(peer[2] * hP, hP)]))
        return out

    return plan


def gather_small(c8, wc):
    def body(c_ref, wc_ref, call, wcall, ssem, rsem, lsem):
        x, y, c = _me()
        myc = 2 * x + y
        me_lin = 4 * x + 2 * y + c
        me = (x, y, c)
        local = [pltpu.make_async_copy(c_ref, call.at[me_lin], lsem.at[0]),
                 pltpu.make_async_copy(wc_ref, wcall.at[myc], lsem.at[1])]
        for cp in local:
            cp.start()
        sends, recvs = [], []
        for m, peer in enumerate(_peers7(x, y, c)):
            plin = 4 * peer[0] + 2 * peer[1] + peer[2]
            sends.append(_rcopy(c_ref, call.at[me_lin], ssem.at[m], rsem.at[m], peer))
            recvs.append(_rcopy(call.at[plin], call.at[plin], ssem.at[m], rsem.at[m], me))
        for j, (px, py) in enumerate([(1 - x, y), (x, 1 - y), (1 - x, 1 - y)]):
            pc = 2 * px + py
            sends.append(_rcopy(wc_ref, wcall.at[myc], ssem.at[7 + j], rsem.at[7 + j], (px, py, c)))
            recvs.append(_rcopy(wcall.at[pc], wcall.at[pc], ssem.at[7 + j], rsem.at[7 + j], me))
        for cp in sends:
            cp.start()
        for cp in recvs:
            cp.wait_recv()
        for cp in sends:
            cp.wait_send()
        for cp in local:
            cp.wait()

    return pl.pallas_call(
        body, name="gather_small",
        in_specs=[ANY] * 2, out_specs=[ANY] * 2,
        out_shape=[jax.ShapeDtypeStruct((N_DEV, SUBLANES, LANES), F32),
                   jax.ShapeDtypeStruct((N_CHIPS, wc.shape[0], 3, LANES), F32)],
        scratch_shapes=[pltpu.SemaphoreType.DMA((10,)), pltpu.SemaphoreType.DMA((10,)), pltpu.SemaphoreType.DMA((2,))],
        compiler_params=_params(n_grid=0),
    )(c8, wc)


def spread_now(gi, go, gp, layers, wp_layers):
    plan = plan_spread(layers, wp_layers)
    n = 2 * len(layers) + 7 * len(wp_layers)

    def body(gi_in, go_in, gp_in, gi, go, gp, ssem, rsem):
        copies = plan((gi, go, gp), *_me())
        me = _me()
        sends = [_rcopy(src, dst, ssem.at[k], rsem.at[k], peer) for k, (src, dst, peer, _) in enumerate(copies)]
        for cp in sends:
            cp.start()
        for k, (_, _, _, land) in enumerate(copies):
            _rcopy(land, land, ssem.at[k], rsem.at[k], me).wait_recv()
        for cp in sends:
            cp.wait_send()

    return pl.pallas_call(
        body, name="spread_now",
        in_specs=[ANY] * 3, out_specs=[ANY] * 3,
        out_shape=[jax.ShapeDtypeStruct(a.shape, a.dtype) for a in (gi, go, gp)],
        input_output_aliases={0: 0, 1: 1, 2: 2},
        scratch_shapes=[pltpu.SemaphoreType.DMA((n,)), pltpu.SemaphoreType.DMA((n,))],
        compiler_params=_params(n_grid=0),
    )(gi, go, gp)


def add_sibling(cidx, mine, sib):
    def body(c_ref, *refs):
        for a in range(3):
            m, s, o = refs[a], refs[3 + a], refs[6 + a]
            o[...] = (m[...].astype(F32) + s[...].astype(F32)).astype(BF16)

    per_step = 2

    def mine_spec(a):
        h = a.shape[1] // 2
        return pl.BlockSpec((per_step, h, a.shape[2]), lambda j, c_ref: (j, c_ref[0], 0))

    def sib_spec(a):
        return pl.BlockSpec((per_step,) + a.shape[1:], lambda j, c_ref: (j, 0, 0))

    return pl.pallas_call(
        body, name="add_sibling",
        grid_spec=pltpu.PrefetchScalarGridSpec(
            num_scalar_prefetch=1, grid=(N_CHIPS // per_step,),
            in_specs=[mine_spec(a) for a in mine] + [sib_spec(a) for a in sib],
            out_specs=[sib_spec(a) for a in sib]),
        out_shape=[jax.ShapeDtypeStruct(a.shape, BF16) for a in sib],
        compiler_params=_params(VMEM_BIG),
    )(cidx, *mine, *sib)


def sum_chips(pos, own, rb, acc, l, shapes):
    nq = 2
    n_in = 6 + (3 if acc is not None else 0)

    def body(pos_ref, *refs):
        for a in range(3):
            m, b, o = refs[a], refs[3 + a], refs[n_in + a]
            s = m[...].astype(F32)
            for j in range(3):
                s = s + b[j].astype(F32)
            o[...] = s

    def own_spec(a):
        return pl.BlockSpec((None, a.shape[1] // nq, a.shape[2]), lambda q, p: (p[1], q, 0))

    def rb_spec(a):
        return pl.BlockSpec((3, a.shape[1] // nq, a.shape[2]), lambda q, p: (0, q, 0))

    hi, ho, hp = own[0].shape[1] // nq, own[1].shape[1] // nq, own[2].shape[1] // nq
    out_specs = [pl.BlockSpec((None, hi, shapes[0][2]), lambda q, p: (l, p[0] * nq + q, 0)),
                 pl.BlockSpec((None, ho, shapes[1][2]), lambda q, p: (l, p[0] * nq + q, 0)),
                 pl.BlockSpec((None, None, hp, LANES), lambda q, p: (l, p[1], p[0] * nq + q, 0))]
    in_specs = [own_spec(a) for a in own] + [rb_spec(a) for a in rb]
    args = list(own) + list(rb)
    aliases = {}
    if acc is not None:
        in_specs += [ANY] * 3
        args += list(acc)
        aliases = {7: 0, 8: 1, 9: 2}
    return pl.pallas_call(
        body, name="sum_chips",
        grid_spec=pltpu.PrefetchScalarGridSpec(num_scalar_prefetch=1, grid=(nq,), in_specs=in_specs, out_specs=out_specs),
        out_shape=[jax.ShapeDtypeStruct(s, F32) for s in shapes],
        input_output_aliases=aliases,
        compiler_params=_params(VMEM_BIG),
    )(pos, *args)


def _wconv_slot(chip, tap):
    idx = 3 * chip + tap
    return ROW_WCONV + idx // SUBLANES, slice((idx % SUBLANES) * LANES, (idx % SUBLANES + 1) * LANES)


def pack_small(pos, per_layer, loss_blk):
    L = len(per_layer)
    D = per_layer[0][0].shape[1]

    def body(pos_ref, *refs):
        o = refs[-1]
        lb = refs[-2]
        o[...] = jnp.zeros_like(o)
        for l in range(L):
            dgpre, dgpost, dsh, dsc, dgt, dps, dwc = refs[7 * l:7 * l + 7]
            base = SUBLANES * l
            o[pl.ds(base + ROW_G_PRE, 1), :] = dgpre[...]
            o[pl.ds(base + ROW_G_POST, 1), :] = dgpost[...]
            for r, src in enumerate((dsh, dsc, dgt)):
                o[pl.ds(base + ROW_MOD + r, 1), :] = src[...]
            o[pl.ds(base + ROW_PSCALE, 1), 0:dps.shape[1]] = dps[...]
            for j in range(dwc.shape[0]):
                for k in range(3):
                    row, lanes = _wconv_slot(j, k)
                    o[pl.ds(base + row, 1), lanes] = dwc[j, pl.ds(k, 1), :]
        o[pl.ds(ROW_PSCALE, 1), LOSS_LANES] = lb[pl.ds(0, 1), :]

    flat = [a for layer in per_layer for a in layer] + [loss_blk]

    def whole(a):
        return pl.BlockSpec(a.shape, lambda i, p: (0,) * a.ndim)

    return pl.pallas_call(
        body, name="pack_small",
        grid_spec=pltpu.PrefetchScalarGridSpec(
            num_scalar_prefetch=1, grid=(1,), in_specs=[whole(a) for a in flat],
            out_specs=pl.BlockSpec((None, L * SUBLANES, D), lambda i, p: (p[2], 0, 0))),
        out_shape=jax.ShapeDtypeStruct((N_DEV, L * SUBLANES, D), F32),
        compiler_params=_params(),
    )(pos, *flat)


def small_update(pos, packs, params, moments_m, moments_v):
    n = len(params)
    L, D = params[1].shape
    PS = params[3].shape[1]

    def body(pos_ref, p_ref, *refs):
        ws, ms, vs = refs[0:n], refs[n:2 * n], refs[2 * n:3 * n]
        loss_ref = refs[3 * n]
        outs = [refs[3 * n + 1 + 4 * t:3 * n + 5 + 4 * t] for t in range(n)]
        summed = refs[-1]
        s = p_ref[0]
        for d in range(1, N_DEV):
            s = s + p_ref[d]
        summed[...] = s
        loss_ref[...] = summed[pl.ds(ROW_PSCALE, 1), LOSS_LANES]
        chip = pos_ref[1]

        def update(t, idx, g):
            d, mm, vv = _adamw_math(ws[t][idx], g, ms[t][idx], vs[t][idx])
            g_ref, d_ref, mo_ref, vo_ref = outs[t]
            g_ref[idx] = g
            d_ref[idx] = d
            mo_ref[idx] = mm
            vo_ref[idx] = vv

        for l in range(L):
            base = SUBLANES * l
            row = pl.ds(l, 1)
            for k in range(3):
                update(0, (row, slice(k * D, (k + 1) * D)), summed[pl.ds(base + ROW_MOD + k, 1), :])
            update(1, (row, slice(None)), summed[pl.ds(base + ROW_G_PRE, 1), :])
            update(2, (row, slice(None)), summed[pl.ds(base + ROW_G_POST, 1), :])
            update(3, (row, slice(None)), summed[pl.ds(base + ROW_PSCALE, 1), 0:PS])
            for k in range(3):
                g = None
                for j in range(N_CHIPS):
                    wrow, lanes = _wconv_slot(j, k)
                    cand = summed[pl.ds(base + wrow, 1), lanes]
                    g = cand if g is None else jnp.where(chip == j, cand, g)
                update(4, (l, pl.ds(k, 1), slice(None)), g)

    def whole(a):
        return pl.BlockSpec(a.shape, lambda i, p: (0,) * a.ndim)

    ins = [packs] + list(params) + list(moments_m) + list(moments_v)
    out_shape = [jax.ShapeDtypeStruct((1, LANES), F32)]
    for w in params:
        out_shape += [jax.ShapeDtypeStruct(w.shape, F32)] * 4
    outs = pl.pallas_call(
        body, name="small_update",
        grid_spec=pltpu.PrefetchScalarGridSpec(
            num_scalar_prefetch=1, grid=(1,), in_specs=[whole(a) for a in ins],
            out_specs=[whole(a) for a in out_shape],
            scratch_shapes=[pltpu.VMEM(packs.shape[1:], F32)]),
        out_shape=out_shape,
        compiler_params=_params(),
    )(pos, *ins)
    return outs[0], [outs[1 + 4 * t:5 + 4 * t] for t in range(n)]


def _adamw_math(w, g, m, v):
    m = ADAM_B1 * m + (1.0 - ADAM_B1) * g
    v = ADAM_B2 * v + (1.0 - ADAM_B2) * (g * g)
    m_hat = m / (1.0 - ADAM_B1 ** ADAM_STEP)
    v_hat = v / (1.0 - ADAM_B2 ** ADAM_STEP)
    delta = -ADAM_LR * (m_hat / (jnp.sqrt(v_hat) + ADAM_EPS) + ADAM_WD * w)
    return delta, m, v


def adamw(w, g, m, v, block, name, first=0, count=None, acc=None):
    grid = tuple(s // b for s, b in zip(w.shape, block))
    if count is not None:
        grid = (count,) + grid[1:]

    def body(w_ref, g_ref, m_ref, v_ref, *rest):
        go_ref, d_ref, mo_ref, vo_ref = rest[-4:]
        gv = g_ref[...]
        d, mm, vv = _adamw_math(w_ref[...], gv, m_ref[...], v_ref[...])
        go_ref[...] = gv
        d_ref[...] = d
        mo_ref[...] = mm
        vo_ref[...] = vv

    spec = pl.BlockSpec(block, lambda i, *rest: (first + i,) + rest)
    shape = jax.ShapeDtypeStruct(w.shape, F32)
    extra = [] if acc is None else list(acc)
    return pl.pallas_call(
        body, name=name, grid=grid,
        in_specs=[spec] * 4 + [ANY] * len(extra), out_specs=[spec] * 4, out_shape=[shape] * 4,
        input_output_aliases={4 + a: a for a in range(len(extra))},
        compiler_params=_params(VMEM_BIG, n_grid=len(grid)),
    )(w, g, m, v, *extra)


def ada_finish(c_all, dmod, w, m, v):
    L, D, CW = w.shape
    hD = D // 2

    def body(c_ref, d_ref, w_ref, m_ref, v_ref, g_ref, dl_ref, mo_ref, vo_ref):
        cv = c_ref[...]
        z = jnp.zeros_like(cv)
        ca = jnp.concatenate([cv * jax.nn.sigmoid(cv), z], axis=0).astype(BF16)
        dm = jnp.concatenate([d_ref[0], jnp.zeros_like(d_ref[0])], axis=0).astype(BF16)
        g = lax.dot_general(ca, dm, TN, preferred_element_type=F32)
        g_ref[0] = g
        d, mm, vv = _adamw_math(w_ref[0], g, m_ref[0], v_ref[0])
        dl_ref[0] = d
        mo_ref[0] = mm
        vo_ref[0] = vv

    big = pl.BlockSpec((1, hD, CW), lambda l, h: (l, h, 0))
    shape = jax.ShapeDtypeStruct(w.shape, F32)
    return pl.pallas_call(
        body, name="ada_finish", grid=(L, 2),
        in_specs=[pl.BlockSpec((N_DEV, hD), lambda l, h: (0, h)), pl.BlockSpec((1, N_DEV, CW), lambda l, h: (l, 0, 0)),
                  big, big, big],
        out_specs=[big] * 4, out_shape=[shape] * 4,
        compiler_params=_params(VMEM_BIG, n_grid=2),
    )(c_all, dmod, w, m, v)


def kernel(x, c, w_ada, b_ada, g_pre, w_in, w_conv, w_pool, pool_scale, w_out, g_post, loss_target, m_w_ada, m_b_ada, m_g_pre, m_w_in, m_w_conv, m_w_pool, m_pool_scale, m_w_out, m_g_post, v_w_ada, v_b_ada, v_g_pre, v_w_in, v_w_conv, v_w_pool, v_pool_scale, v_w_out, v_g_post):
    L, D, CW = w_in.shape
    RO = w_out.shape[1]
    T = x.shape[1]
    ix, iy, ic = _me()
    chip = 2 * ix + iy
    me_lin = 4 * ix + 2 * iy + ic

    pos = jnp.stack([ic, chip, me_lin]).astype(jnp.int32)
    g_pre3, g_post3 = g_pre.reshape(L, 1, D), g_post.reshape(L, 1, D)
    pscale3 = pool_scale.reshape(L, 1, pool_scale.shape[1])
    n_s, n_c = 3, 9

    def gather(bufs, after):
        ss, rs, bufs, tok = xchg_start("gather_start", bufs, 3 * len(bufs), plan_gather, after=after)
        return (ss, rs, bufs), tok

    def ready(flight, after):
        fss, frs, bufs = flight
        return xchg_wait("forward_wait", bufs, fss, frs, 3 * len(bufs), plan_forward, after)

    def arrive_part(flight, which, after):
        ss, rs, bufs = flight
        sems = tuple(range(which, 3 * len(bufs), len(bufs)))
        (buf,) = xchg_wait("gather_wait", [bufs[which]], ss, rs, 3, plan_gather, after, sems=sems)
        fss, frs, (buf,), tok = xchg_start("forward_start", [buf], 3, plan_forward, sibling_only=True)
        return (fss, frs, [buf]), tok

    c_all3, wconv_all = gather_small(c.reshape(SUBLANES, LANES), w_conv)
    c_all = c_all3.reshape(N_DEV, D)
    w_in_of, w_out_of = [None] * L, [None] * L
    gi0, go0 = cast_weights(pos, w_in, w_out, 0, c_all3)
    flight, token = gather([gi0], [])
    w_in_of[0] = (flight, 0)
    b_my = lax.dynamic_slice_in_dim(b_ada, chip * CW, CW, axis=1)
    m_ss, m_rs, mods, token = xchg_start("mod_start", [mod_part(pos, c_all, w_ada, b_my, token)], 3, plan_mod)
    gi1, go1 = cast_weights(pos, w_in, w_out, 1, token)
    flight, token = gather([gi1], [])
    w_in_of[1] = (flight, 0)
    flight, token = gather([go0, go1], [token])
    w_out_of[0], w_out_of[1] = (flight, 0), (flight, 1)
    for l in range(2, L):
        flight, token = gather(list(cast_weights(pos, w_in, w_out, l, token)), [])
        w_in_of[l], w_out_of[l] = (flight, 0), (flight, 1)
    fwd_in, token = arrive_part(*w_in_of[0], [token])
    (mod_all,) = xchg_wait("mod_wait", mods, m_ss, m_rs, 3, plan_mod, [token])
    mod = lax.dynamic_index_in_dim(mod_all, me_lin, axis=2, keepdims=False)
    mod4 = jnp.transpose(mod, (1, 0, 2)).reshape(L, 3, 1, D)

    xs, projs, yas, yps, ys, pooleds = [x.reshape(T, D)], [], [], [], [], []
    wg_in, wg_out = [], []
    for l in range(L):
        (gi,) = ready(fwd_in, [mod4 if l == 0 else xs[l]])
        proj = proj_fwd(xs[l], mod4, g_pre3, gi, l)
        ya, yp, pooled = mix_fwd(proj, wconv_all, w_pool, pscale3, l)
        pooleds.append(pooled)
        fwd_out, token = arrive_part(*w_out_of[l], [ya, yp])
        after = [token]
        if l + 1 < L:
            fwd_in, token = arrive_part(*w_in_of[l + 1], after)
            after = [token]
        (go,) = ready(fwd_out, after)
        wg_in.append(gi)
        wg_out.append(go.reshape(N_CHIPS * RO, D))
        projs.append(proj)
        yas.append(ya)
        yps.append(yp)
        if l + 1 < L:
            xn, yv = out_fwd(ya, yp, wg_out[l], xs[l], mod4, g_post3, l, after[0])
            xs.append(xn)
        else:
            dx, yv, loss_blk = out_fwd_loss(ya, yp, wg_out[l], xs[l], mod4, g_post3, l, loss_target.reshape(T, D))
        ys.append(yv)

    shapes = (w_in.shape, w_out.shape, w_pool.shape)
    smalls = [None] * L
    acc, flying, sib, token = None, None, None, loss_blk

    def to_chips(sib, after):
        sl, s_ss, s_rs, s_bufs = sib
        s_bufs = xchg_wait("sibling_wait", s_bufs, s_ss, s_rs, n_s, plan_sibling, after)
        chip_parts = add_sibling(pos, s_bufs[0:3], s_bufs[3:6])
        lands = [lax.empty((3,) + a.shape[1:], a.dtype) for a in chip_parts]
        c_ss, c_rs, c_bufs, ctoken = xchg_start("chip_start", list(chip_parts) + lands, n_c, plan_chip)
        return (sl, c_ss, c_rs, c_bufs), ctoken

    def landed(flying, acc, after):
        fl, f_ss, f_rs, f_bufs = flying
        f_bufs = xchg_wait("chip_wait", f_bufs, f_ss, f_rs, n_c, plan_chip, after)
        return sum_chips(pos, f_bufs[0:3], f_bufs[3:6], acc, fl, shapes)

    for l in reversed(range(L)):
        dya, dyp, dwo_l, dgate, dgpost = out_bwd(dx, ys[l], yas[l], yps[l], wg_out[l], mod4, g_post3, l, token)
        token = dya
        if sib is not None:
            arrived = flying
            flying, token = to_chips(sib, [dya])
            if arrived is not None:
                acc = landed(arrived, acc, [token])
                token = acc[0]
        du_a, db_a, dc_a, dg_a, du_p, dg_p, dwc, dwp_l, dps = mix_bwd(projs[l], pooleds[l], dya, dyp, wconv_all, w_pool,
                                                                        pscale3, l, token)
        dx, dwi_l, dshift, dscale, dgpre = in_bwd([du_a, db_a, dc_a, dg_a, du_p, dg_p], wg_in[l], xs[l], dx,
                                                  mod4, g_pre3, l)
        smalls[l] = (dgpre, dgpost, dshift, dscale, dgate, dps, dwc)
        parts = [dwi_l, dwo_l.reshape(N_CHIPS, RO, D), dwp_l]
        s_lands = [lax.empty((a.shape[0], a.shape[1] // 2) + a.shape[2:], a.dtype) for a in parts]
        s_ss, s_rs, s_bufs, token = xchg_start("sibling_start", parts + s_lands, n_s, plan_sibling, sibling_only=True)
        sib = (l, s_ss, s_rs, s_bufs)
    grad_x = dx.reshape(1, T, D)

    p_ss, p_rs, packs, ptoken = xchg_start("pack_start", [pack_small(pos, smalls, loss_blk)], N_DEV - 1, plan_pack)
    acc = landed(flying, acc, [ptoken, token])
    n_sp = (2 + N_DEV - 1) * (L - 1)
    spread = plan_spread(tuple(range(1, L)), tuple(range(1, L)))
    sp_ss, sp_rs, acc, sp_token = xchg_start("spread_start", list(acc), n_sp, spread)
    flying, token = to_chips(sib, [sp_token])
    (packs_all,) = xchg_wait("pack_wait", packs, p_ss, p_rs, N_DEV - 1, plan_pack, [token])
    dmod_all = packs_all.reshape(N_DEV, L, SUBLANES, D)[:, :, ROW_MOD:ROW_MOD + 3].reshape(N_DEV, L, 3 * D)
    dmod_my = jnp.transpose(lax.dynamic_slice_in_dim(dmod_all, chip * CW, CW, axis=2), (1, 0, 2))

    g_w_ada, d_w_ada, nm_w_ada, nv_w_ada = ada_finish(c_all, dmod_my, w_ada, m_w_ada, v_w_ada)
    loss_row, upd = small_update(pos, packs_all, [b_ada, g_pre, g_post, pool_scale, w_conv],
                                 [m_b_ada, m_g_pre, m_g_post, m_pool_scale, m_w_conv],
                                 [v_b_ada, v_g_pre, v_g_post, v_pool_scale, v_w_conv])
    loss = loss_row[0, 0]
    (g_b_ada, d_b_ada, nm_b_ada, nv_b_ada), (g_g_pre, d_g_pre, nm_g_pre, nv_g_pre) = upd[0], upd[1]
    (g_g_post, d_g_post, nm_g_post, nv_g_post), (g_pscale, d_pscale, nm_pscale, nv_pscale) = upd[2], upd[3]
    g_w_conv, d_w_conv, nm_w_conv, nv_w_conv = upd[4]

    done = [nv_w_ada, nv_w_conv]
    g_w_in, g_w_out, g_w_pool = xchg_wait("spread_wait", acc, sp_ss, sp_rs, n_sp, spread, done)
    in_blk, out_blk = (1, D // 2, CW), (1, RO, D)
    upd_in = adamw(w_in, g_w_in, m_w_in, v_w_in, in_blk, "adamw_w_in", 1, L - 1)
    upd_out = adamw(w_out, g_w_out, m_w_out, v_w_out, out_blk, "adamw_w_out", 1, L - 1)

    acc = landed(flying, (g_w_in, g_w_out, g_w_pool), [upd_in[3], upd_out[3]])
    r_w_in, r_w_out, r_w_pool = spread_now(*acc, (0,), (0,))
    g_w_in, d_w_in, nm_w_in, nv_w_in = adamw(w_in, r_w_in, m_w_in, v_w_in, in_blk, "adamw_w_in", 0, 1, upd_in)
    g_w_out, d_w_out, nm_w_out, nv_w_out = adamw(w_out, r_w_out, m_w_out, v_w_out, out_blk, "adamw_w_out", 0, 1, upd_out)
    pshape = (L, N_CHIPS * LANES, LANES)
    upd_pool = adamw(w_pool.reshape(pshape), r_w_pool.reshape(pshape), m_w_pool.reshape(pshape),
                     v_w_pool.reshape(pshape), (1,) + pshape[1:], "adamw_w_pool")
    g_w_pool, d_w_pool, nm_w_pool, nv_w_pool = [a.reshape(w_pool.shape) for a in upd_pool]

    return (loss, grad_x,
            g_w_ada, g_b_ada, g_g_pre, g_w_in, g_w_conv, g_w_pool, g_pscale, g_w_out, g_g_post,
            d_w_ada, d_b_ada, d_g_pre, d_w_in, d_w_conv, d_w_pool, d_pscale, d_w_out, d_g_post,
            nm_w_ada, nm_b_ada, nm_g_pre, nm_w_in, nm_w_conv, nm_w_pool, nm_pscale, nm_w_out, nm_g_post,
            nv_w_ada, nv_b_ada, nv_g_pre, nv_w_in, nv_w_conv, nv_w_pool, nv_pscale, nv_w_out, nv_g_post)
```

```python
import functools

import jax
import jax.numpy as jnp
from jax import lax
from jax.experimental import pallas as pl
from jax.experimental.pallas import tpu as pltpu

F32 = jnp.float32
BF16 = jnp.bfloat16
MESH = pl.DeviceIdType.MESH
ANY = pl.BlockSpec(memory_space=pl.ANY)

NORM_EPS = 1e-6
POOL_WINDOWS = (2, 4, 8, 16)
ADAM_LR = 0.001
ADAM_B1 = 0.9
ADAM_B2 = 0.999
ADAM_EPS = 1e-08
ADAM_WD = 0.01
ADAM_STEP = 10

N_CHIPS = 4
N_DEV = 8
LANES = 128
SUBLANES = 8
VMEM_BIG = 56 * 1024 * 1024
HIST = 16
R_CONV = 64
R_POOL = 128

ROW_G_PRE, ROW_G_POST, ROW_MOD, ROW_PSCALE, ROW_WCONV = 0, 1, 2, 5, 6
LOSS_LANES = slice(4 * LANES, 5 * LANES)

NT = (((1,), (1,)), ((), ()))
TN = (((0,), (0,)), ((), ()))


def _params(vmem=None, n_grid=1):
    kw = {}
    if n_grid:
        kw["dimension_semantics"] = ("arbitrary",) * n_grid
    if vmem is not None:
        kw["vmem_limit_bytes"] = vmem
    return pltpu.CompilerParams(**kw)


def _colsum8(v):
    n, d = v.shape
    return v.reshape(n // SUBLANES, SUBLANES, d).sum(axis=0)


def _rms(v):
    return lax.rsqrt(jnp.mean(v * v, axis=-1, keepdims=True) + NORM_EPS)


def _sigmoid(v):
    return 0.5 * jnp.tanh(0.5 * v) + 0.5


def _shift_down(ext, k, rows):
    if k == 0:
        return ext[HIST:HIST + rows]
    return pltpu.roll(ext, k, 0)[HIST:HIST + rows]


def _shift_up(ext, k, rows):
    if k == 0:
        return ext[0:rows]
    return pltpu.roll(ext, ext.shape[0] - k, 0)[0:rows]


def _load_ext(ref, r0, h0, first, rows):
    hist = ref[pl.ds(h0, HIST), :].astype(F32)
    hist = jnp.where(first, 0.0, hist)
    cur = ref[pl.ds(r0, rows), :].astype(F32)
    return jnp.concatenate([hist, cur], axis=0)


def _me():
    return lax.axis_index("x"), lax.axis_index("y"), lax.axis_index("c")


def cast_weights(pos, w_in, w_out, l, after):
    _, D, CW = w_in.shape
    RO = w_out.shape[1]

    def body(pos_ref, wi, wo, after_ref, oi, oo):
        oi[...] = wi[...].astype(BF16)
        oo[...] = wo[...].astype(BF16)

    return pl.pallas_call(
        body, name="cast_w",
        grid_spec=pltpu.PrefetchScalarGridSpec(
            num_scalar_prefetch=1, grid=(2,),
            in_specs=[pl.BlockSpec((None, D // 2, CW), lambda h, p: (l, h, 0)),
                      pl.BlockSpec((None, RO // 2, D), lambda h, p: (l, h, 0)), ANY],
            out_specs=[pl.BlockSpec((D // 2, CW), lambda h, p: (h, p[1])),
                       pl.BlockSpec((None, RO // 2, D), lambda h, p: (p[1], h, 0))]),
        out_shape=[jax.ShapeDtypeStruct((D, N_CHIPS * CW), BF16), jax.ShapeDtypeStruct((N_CHIPS, RO, D), BF16)],
        compiler_params=_params(),
    )(pos, w_in, w_out, after)


def mod_part(pos, c_all, w_ada, b_my, after):
    L, D, CW = w_ada.shape

    def body(pos_ref, c_ref, w_ref, b_ref, after_ref, o_ref):
        cv = c_ref[...]
        ca = (cv * jax.nn.sigmoid(cv)).astype(BF16)
        o_ref[...] = jnp.dot(ca, w_ref[0].astype(BF16), preferred_element_type=F32) + b_ref[0]

    return pl.pallas_call(
        body, name="mod_part",
        grid_spec=pltpu.PrefetchScalarGridSpec(
            num_scalar_prefetch=1, grid=(L,),
            in_specs=[pl.BlockSpec((N_DEV, D), lambda l, p: (0, 0)),
                      pl.BlockSpec((1, D, CW), lambda l, p: (l, 0, 0)),
                      pl.BlockSpec((1, 1, CW), lambda l, p: (l, 0, 0)), ANY],
            out_specs=pl.BlockSpec((None, None, N_DEV, CW), lambda l, p: (p[1], l, 0, 0))),
        out_shape=jax.ShapeDtypeStruct((N_CHIPS, L, N_DEV, CW), F32),
        compiler_params=_params(VMEM_BIG),
    )(pos, c_all, w_ada, b_my.reshape(L, 1, CW), after)


def _mod_row(l, k, D):
    return pl.BlockSpec((None, None, 1, D), lambda *_: (l, k, 0, 0))


def _layer_row(l, D):
    return pl.BlockSpec((None, 1, D), lambda *_: (l, 0, 0))


def proj_fwd(x, mod4, g_pre3, wg, l):
    T, D = x.shape
    NC = wg.shape[1]
    NB = N_CHIPS
    CW = NC // NB
    tm = 512

    def body(x_ref, sh_ref, sc_ref, g_ref, w_ref, o_ref):
        xv = x_ref[...]
        h = (xv * _rms(xv)) * (g_ref[...] * (1.0 + sc_ref[...])) + sh_ref[...]
        hb = h.astype(BF16)
        for j in range(NB):
            cols = slice(j * CW, (j + 1) * CW)
            o_ref[:, cols] = jnp.dot(hb, w_ref[:, cols], preferred_element_type=F32).astype(BF16)

    return pl.pallas_call(
        body, name="proj_fwd", grid=(T // tm,),
        in_specs=[pl.BlockSpec((tm, D), lambda i: (i, 0)), _mod_row(l, 0, D), _mod_row(l, 1, D), _layer_row(l, D),
                  pl.BlockSpec((D, NC), lambda i: (0, 0))],
        out_specs=pl.BlockSpec((tm, NC), lambda i: (i, 0)),
        out_shape=jax.ShapeDtypeStruct((T, NC), BF16),
        compiler_params=_params(VMEM_BIG),
    )(x, mod4, mod4, g_pre3, wg)


N_MIX = 4


def _conv_fwd_block(u_ref, b_ref, c_ref, g_ref, w_ref, o_ref):
    T = u_ref.shape[0]
    R = 2 * R_CONV
    w0 = w_ref[pl.ds(0, 1), :]
    w1 = w_ref[pl.ds(1, 1), :]
    w2 = w_ref[pl.ds(2, 1), :]

    def chunk(i, carry):
        r0 = pl.multiple_of(i * R, R)
        h0 = pl.multiple_of(jnp.maximum(r0 - HIST, 0), HIST)
        first = i == 0
        ca = _load_ext(c_ref, r0, h0, first, R) * _load_ext(u_ref, r0, h0, first, R)
        conv = w2 * ca[HIST:] + w1 * _shift_down(ca, 1, R) + w0 * _shift_down(ca, 2, R)
        g = g_ref[pl.ds(r0, R), :].astype(F32)
        b = b_ref[pl.ds(r0, R), :].astype(F32)
        o_ref[pl.ds(r0, R), :] = (b * conv * (g * _sigmoid(g))).astype(BF16)
        return carry

    lax.fori_loop(0, T // R, chunk, 0)


def _conv_idx(j):
    return jnp.minimum(j, N_MIX - 1)


def _pool_idx(j):
    return jnp.maximum(j - N_MIX, 0)


def _proj_col(T, off, idx):
    return pl.BlockSpec((T, LANES), lambda j: (0, idx(j) + off))


def _causal_window_sum(ext, w):
    s, k = ext, 1
    while k < w:
        s = s + pltpu.roll(s, k, 0)
        k *= 2
    return s


def _anticausal_window_sum(ext, w):
    s, k = ext, 1
    n = ext.shape[0]
    while k < w:
        s = s + pltpu.roll(s, n - k, 0)
        k *= 2
    return s


def _count(r0, rows, w):
    t = r0 + lax.broadcasted_iota(jnp.int32, (rows, LANES), 0)
    return jnp.minimum(t + 1, w).astype(F32)


def _pooled_loop(p_ref, pooled_s, w, T):
    R = R_POOL

    def chunk(i, carry):
        r0 = pl.multiple_of(i * R, R)
        h0 = pl.multiple_of(jnp.maximum(r0 - HIST, 0), HIST)
        ext = _load_ext(p_ref, r0, h0, i == 0, R)
        ws = _causal_window_sum(ext, w)[HIST:]
        pooled_s[pl.ds(r0, R), :] = (ws / _count(r0, R, w) - ext[HIST:]).astype(BF16)
        return carry

    lax.fori_loop(0, T // R, chunk, 0)


def _conv_w_spec(l):
    return pl.BlockSpec((None, None, 3, LANES), lambda j: (_conv_idx(j), l, 0, 0))


def _pool_w_spec(l):
    return pl.BlockSpec((None, None, LANES, LANES), lambda j: (l, _pool_idx(j), 0, 0))


def _pool_s_spec(l):
    return pl.BlockSpec((None, 1, LANES), lambda j: (l, 0, _pool_idx(j)))


def _pool_fwd_group(p_ref, g_ref, w_ref, s_ref, o_ref, pooled_s, mixed_s, w):
    T = p_ref.shape[0]
    R = R_POOL
    _pooled_loop(p_ref, pooled_s, w, T)
    mixed_s[...] = jnp.dot(pooled_s[...], w_ref[...].astype(BF16), preferred_element_type=F32)
    sc = s_ref[...]

    def chunk(i, carry):
        r0 = pl.multiple_of(i * R, R)
        g = g_ref[pl.ds(r0, R), :].astype(F32)
        o_ref[pl.ds(r0, R), :] = (mixed_s[pl.ds(r0, R), :] * sc * (g * _sigmoid(g))).astype(BF16)
        return carry

    lax.fori_loop(0, T // R, chunk, 0)


def mix_fwd(proj, wconv, wpool, pscale3, l):
    T = proj.shape[0]

    def body(u_ref, b_ref, c_ref, g_ref, p_ref, gp_ref, wc_ref, wp_ref, s_ref, ya_ref, yp_ref, pooled_ref, mixed_s):
        j = pl.program_id(0)
        pl.when(j < N_MIX)(functools.partial(_conv_fwd_block, u_ref, b_ref, c_ref, g_ref, wc_ref, ya_ref))
        for k, w in enumerate(POOL_WINDOWS):
            pl.when(j == N_MIX + k)(functools.partial(_pool_fwd_group, p_ref, gp_ref, wp_ref, s_ref, yp_ref,
                                                      pooled_ref, mixed_s, w))

    half = jax.ShapeDtypeStruct((T, N_MIX * LANES), BF16)
    pool_col = pl.BlockSpec((T, LANES), lambda j: (0, _pool_idx(j)))
    return pl.pallas_call(
        body, name="mix_fwd", grid=(2 * N_MIX,),
        in_specs=[_proj_col(T, 0, _conv_idx), _proj_col(T, 4, _conv_idx), _proj_col(T, 8, _conv_idx),
                  _proj_col(T, 12, _conv_idx), _proj_col(T, 16, _pool_idx), _proj_col(T, 20, _pool_idx),
                  _conv_w_spec(l), _pool_w_spec(l), _pool_s_spec(l)],
        out_specs=[pl.BlockSpec((T, LANES), lambda j: (0, _conv_idx(j))), pool_col, pool_col],
        out_shape=[half, half, half],
        scratch_shapes=[pltpu.VMEM((T, LANES), F32)],
        compiler_params=_params(),
    )(proj, proj, proj, proj, proj, proj, wconv, wpool, pscale3)


def out_fwd(ya, yp, wo, x, mod4, g_post3, l, after):
    T, D = x.shape
    H = ya.shape[1]
    tm = 512

    def body(ya_ref, yp_ref, wo_ref, x_ref, gt_ref, g_ref, after_ref, xn_ref, y_ref):
        y = (jnp.dot(ya_ref[...], wo_ref[0:H, :], preferred_element_type=F32)
             + jnp.dot(yp_ref[...], wo_ref[H:2 * H, :], preferred_element_type=F32))
        xn_ref[...] = x_ref[...] + gt_ref[...] * (y * _rms(y) * g_ref[...])
        y_ref[...] = y.astype(BF16)

    tile = pl.BlockSpec((tm, D), lambda i: (i, 0))
    half = pl.BlockSpec((tm, H), lambda i: (i, 0))
    return pl.pallas_call(
        body, name="out_fwd", grid=(T // tm,),
        in_specs=[half, half, pl.BlockSpec((2 * H, D), lambda i: (0, 0)), tile, _mod_row(l, 2, D), _layer_row(l, D),
                  ANY],
        out_specs=[tile, tile],
        out_shape=[jax.ShapeDtypeStruct((T, D), F32), jax.ShapeDtypeStruct((T, D), BF16)],
        compiler_params=_params(VMEM_BIG),
    )(ya, yp, wo, x, mod4, g_post3, after)


def out_fwd_loss(ya, yp, wo, x, mod4, g_post3, l, target):
    T, D = x.shape
    H = ya.shape[1]
    tm = 512
    nt = T // tm

    def body(ya_ref, yp_ref, wo_ref, x_ref, gt_ref, g_ref, t_ref, dx_ref, y_ref, l_ref, acc):
        i = pl.program_id(0)

        @pl.when(i == 0)
        def _():
            acc[...] = jnp.zeros_like(acc)

        y = (jnp.dot(ya_ref[...], wo_ref[0:H, :], preferred_element_type=F32)
             + jnp.dot(yp_ref[...], wo_ref[H:2 * H, :], preferred_element_type=F32))
        y_ref[...] = y.astype(BF16)
        d = (x_ref[...] + gt_ref[...] * (y * _rms(y) * g_ref[...])) - t_ref[...]
        dx_ref[...] = d * (1.0 / D)
        acc[...] += _colsum8(d * d)

        @pl.when(i == nt - 1)
        def _():
            l_ref[...] = jnp.zeros_like(l_ref) + jnp.sum(acc[...]) * (0.5 / D)

    tile = pl.BlockSpec((tm, D), lambda i: (i, 0))
    half = pl.BlockSpec((tm, H), lambda i: (i, 0))
    return pl.pallas_call(
        body, name="out_fwd_loss", grid=(nt,),
        in_specs=[half, half, pl.BlockSpec((2 * H, D), lambda i: (0, 0)), tile, _mod_row(l, 2, D), _layer_row(l, D),
                  tile],
        out_specs=[tile, tile, pl.BlockSpec((SUBLANES, LANES), lambda i: (0, 0))],
        out_shape=[jax.ShapeDtypeStruct((T, D), F32), jax.ShapeDtypeStruct((T, D), BF16),
                   jax.ShapeDtypeStruct((SUBLANES, LANES), F32)],
        scratch_shapes=[pltpu.VMEM((SUBLANES, D), F32)],
        compiler_params=_params(VMEM_BIG),
    )(ya, yp, wo, x, mod4, g_post3, target)


def out_bwd(dx, y, ya, yp, wo, mod4, g_post3, l, after):
    T, D = dx.shape
    H = ya.shape[1]
    tm = 512
    nt = T // tm

    def body(dx_ref, y_ref, ya_ref, yp_ref, wo_ref, gt_ref, g_ref, after_ref,
             dya_ref, dyp_ref, dwo_ref, dgt_ref, dg_ref, acc_w, acc_p):
        i = pl.program_id(0)

        @pl.when(i == 0)
        def _():
            acc_w[...] = jnp.zeros_like(acc_w)
            acc_p[...] = jnp.zeros_like(acc_p)

        yv = y_ref[...].astype(F32)
        dxv = dx_ref[...]
        gg = gt_ref[...] * g_ref[...]
        r = _rms(yv)
        yn = yv * r
        p = dxv * yn
        acc_p[...] += _colsum8(p)
        dy = r * (dxv * gg - yn * jnp.mean(p * gg, axis=-1, keepdims=True))
        dyb = dy.astype(BF16)
        dyc = lax.dot_general(dyb, wo_ref[...], NT, preferred_element_type=F32)
        dya_ref[...] = dyc[:, 0:H].astype(BF16)
        dyp_ref[...] = dyc[:, H:2 * H].astype(BF16)
        acc_w[0:H, :] += lax.dot_general(ya_ref[...], dyb, TN, preferred_element_type=F32)
        acc_w[H:2 * H, :] += lax.dot_general(yp_ref[...], dyb, TN, preferred_element_type=F32)

        @pl.when(i == nt - 1)
        def _():
            dwo_ref[...] = acc_w[...].astype(BF16)
            sp = jnp.sum(acc_p[...], axis=0, keepdims=True)
            dgt_ref[...] = g_ref[...] * sp
            dg_ref[...] = gt_ref[...] * sp

    row = pl.BlockSpec((1, D), lambda i: (0, 0))
    tile = pl.BlockSpec((tm, D), lambda i: (i, 0))
    half = pl.BlockSpec((tm, H), lambda i: (i, 0))
    full = pl.BlockSpec((2 * H, D), lambda i: (0, 0))
    return pl.pallas_call(
        body, name="out_bwd", grid=(nt,),
        in_specs=[tile, tile, half, half, full, _mod_row(l, 2, D), _layer_row(l, D), ANY],
        out_specs=[half, half, full, row, row],
        out_shape=[jax.ShapeDtypeStruct((T, H), BF16), jax.ShapeDtypeStruct((T, H), BF16),
                   jax.ShapeDtypeStruct((2 * H, D), BF16),
                   jax.ShapeDtypeStruct((1, D), F32), jax.ShapeDtypeStruct((1, D), F32)],
        scratch_shapes=[pltpu.VMEM((2 * H, D), F32), pltpu.VMEM((SUBLANES, D), F32)],
        compiler_params=_params(VMEM_BIG),
    )(dx, y, ya, yp, wo, mod4, g_post3, after)


def _conv_bwd_block(u_ref, b_ref, c_ref, g_ref, dy_ref, w_ref, du_ref, db_ref, dc_ref, dg_ref, dw_ref):
    T = u_ref.shape[0]
    R = R_CONV
    nchunk = T // R
    w0 = w_ref[pl.ds(0, 1), :]
    w1 = w_ref[pl.ds(1, 1), :]
    w2 = w_ref[pl.ds(2, 1), :]

    def chunk(k, carry):
        head, a0, a1, a2 = carry
        i = nchunk - 1 - k
        r0 = pl.multiple_of(i * R, R)
        h0 = pl.multiple_of(jnp.maximum(r0 - HIST, 0), HIST)
        first = i == 0
        ue = _load_ext(u_ref, r0, h0, first, R)
        ce = _load_ext(c_ref, r0, h0, first, R)
        ca = ce * ue
        ca0 = ca[HIST:]
        ca1 = _shift_down(ca, 1, R)
        ca2 = _shift_down(ca, 2, R)
        conv = w2 * ca0 + w1 * ca1 + w0 * ca2
        g = g_ref[pl.ds(r0, R), :].astype(F32)
        b = b_ref[pl.ds(r0, R), :].astype(F32)
        dy = dy_ref[pl.ds(r0, R), :].astype(F32)
        sg = _sigmoid(g)
        sl = g * sg
        t = dy * conv
        db_ref[pl.ds(r0, R), :] = (t * sl).astype(BF16)
        dg_ref[pl.ds(r0, R), :] = (t * b * (sg + sl * (1.0 - sg))).astype(BF16)
        dconv = dy * b * sl
        a2 = a2 + _colsum8(dconv * ca0)
        a1 = a1 + _colsum8(dconv * ca1)
        a0 = a0 + _colsum8(dconv * ca2)
        e = jnp.concatenate([dconv, head], axis=0)
        dca = w2 * dconv + w1 * _shift_up(e, 1, R) + w0 * _shift_up(e, 2, R)
        du_ref[pl.ds(r0, R), :] = (dca * ce[HIST:]).astype(BF16)
        dc_ref[pl.ds(r0, R), :] = (dca * ue[HIST:]).astype(BF16)
        return dconv[0:SUBLANES], a0, a1, a2

    z = jnp.zeros((SUBLANES, LANES), F32)
    _, a0, a1, a2 = lax.fori_loop(0, nchunk, chunk, (z, z, z, z))
    dw_ref[pl.ds(0, 1), :] = jnp.sum(a0, axis=0, keepdims=True)
    dw_ref[pl.ds(1, 1), :] = jnp.sum(a1, axis=0, keepdims=True)
    dw_ref[pl.ds(2, 1), :] = jnp.sum(a2, axis=0, keepdims=True)


def _pool_bwd_group(pooled_s, g_ref, dy_ref, w_ref, s_ref, du_ref, dg_ref, dw_ref, ds_ref,
                    mixed_s, dmix_s, dpool_s, w):
    T = pooled_s.shape[0]
    R = R_POOL
    nchunk = T // R
    wb = w_ref[...].astype(BF16)
    mixed_s[...] = jnp.dot(pooled_s[...], wb, preferred_element_type=F32)
    sc = s_ref[...]

    def gate_chunk(i, acc):
        r0 = pl.multiple_of(i * R, R)
        g = g_ref[pl.ds(r0, R), :].astype(F32)
        dy = dy_ref[pl.ds(r0, R), :].astype(F32)
        mixed = mixed_s[pl.ds(r0, R), :]
        sg = _sigmoid(g)
        sl = g * sg
        dg_ref[pl.ds(r0, R), :] = (dy * mixed * sc * (sg + sl * (1.0 - sg))).astype(BF16)
        dms = dy * sl
        dmix_s[pl.ds(r0, R), :] = (dms * sc).astype(BF16)
        return acc + _colsum8(dms * mixed)

    acc = lax.fori_loop(0, nchunk, gate_chunk, jnp.zeros((SUBLANES, LANES), F32))
    ds_ref[...] = jnp.sum(acc, axis=0, keepdims=True)
    dpool_s[pl.ds(0, T), :] = lax.dot_general(dmix_s[...], wb, NT, preferred_element_type=F32)
    dpool_s[pl.ds(T, HIST), :] = jnp.zeros((HIST, LANES), F32)
    dw_ref[...] = lax.dot_general(pooled_s[...], dmix_s[...], TN, preferred_element_type=F32).astype(BF16)

    def back_chunk(i, carry):
        r0 = pl.multiple_of(i * R, R)
        dpe = dpool_s[pl.ds(r0, R + HIST), :]
        e = dpe / _count(r0, R + HIST, w)
        du_ref[pl.ds(r0, R), :] = (_anticausal_window_sum(e, w)[0:R] - dpe[0:R]).astype(BF16)
        return carry

    lax.fori_loop(0, nchunk, back_chunk, 0)


def mix_bwd(proj, pooled, dya, dyp, wconv, wpool, pscale3, l, after):
    T = proj.shape[0]

    def body(u_ref, b_ref, c_ref, g_ref, pooled_ref, gp_ref, dya_ref, dyp_ref, wc_ref, wp_ref, s_ref, after_ref,
             dua_ref, dba_ref, dca_ref, dga_ref, dup_ref, dgp_ref, dwc_ref, dwp_ref, ds_ref,
             mixed_s, dmix_s, dpool_s):
        j = pl.program_id(0)
        pl.when(j < N_MIX)(functools.partial(_conv_bwd_block, u_ref, b_ref, c_ref, g_ref, dya_ref, wc_ref,
                                             dua_ref, dba_ref, dca_ref, dga_ref, dwc_ref))
        for k, w in enumerate(POOL_WINDOWS):
            pl.when(j == N_MIX + k)(functools.partial(_pool_bwd_group, pooled_ref, gp_ref, dyp_ref, wp_ref, s_ref,
                                                      dup_ref, dgp_ref, dwp_ref, ds_ref,
                                                      mixed_s, dmix_s, dpool_s, w))

    sec = jax.ShapeDtypeStruct((T, N_MIX * LANES), BF16)
    conv_col = pl.BlockSpec((T, LANES), lambda j: (0, _conv_idx(j)))
    pool_col = pl.BlockSpec((T, LANES), lambda j: (0, _pool_idx(j)))
    return pl.pallas_call(
        body, name="mix_bwd", grid=(2 * N_MIX,),
        in_specs=[_proj_col(T, 0, _conv_idx), _proj_col(T, 4, _conv_idx), _proj_col(T, 8, _conv_idx),
                  _proj_col(T, 12, _conv_idx), pool_col, _proj_col(T, 20, _pool_idx),
                  conv_col, pool_col, _conv_w_spec(l), _pool_w_spec(l), _pool_s_spec(l), ANY],
        out_specs=[conv_col, conv_col, conv_col, conv_col, pool_col, pool_col,
                   pl.BlockSpec((None, 3, LANES), lambda j: (_conv_idx(j), 0, 0)),
                   pl.BlockSpec((None, LANES, LANES), lambda j: (_pool_idx(j), 0, 0)),
                   pl.BlockSpec((1, LANES), lambda j: (0, _pool_idx(j)))],
        out_shape=[sec] * 6 + [jax.ShapeDtypeStruct((N_MIX, 3, LANES), F32),
                               jax.ShapeDtypeStruct((N_MIX, LANES, LANES), BF16),
                               jax.ShapeDtypeStruct((1, N_MIX * LANES), F32)],
        scratch_shapes=[pltpu.VMEM((T, LANES), F32), pltpu.VMEM((T, LANES), BF16), pltpu.VMEM((T + HIST, LANES), F32)],
        compiler_params=_params(),
    )(proj, proj, proj, proj, pooled, proj, dya, dyp, wconv, wpool, pscale3, after)


def in_bwd(dsecs, wg, x, dxo, mod4, g_pre3, l):
    T, D = x.shape
    NB = N_CHIPS
    CW = wg.shape[1] // NB
    SW = dsecs[0].shape[1]
    nsec = len(dsecs)
    PW = 256
    assert SW % PW == 0 and CW % PW == 0
    tm = 256
    nt = T // tm

    def body(*refs):
        d_refs = refs[0:nsec]
        w_ref, x_ref, dxo_ref, sh_ref, sc_ref, g_ref = refs[nsec:nsec + 6]
        dxi_ref, dw_ref, dsh_ref, dsc_ref, dg_ref = refs[nsec + 6:nsec + 11]
        acc_w, acc_sh, acc_q = refs[nsec + 11:]
        i = pl.program_id(0)

        @pl.when(i == 0)
        def _():
            acc_w[...] = jnp.zeros_like(acc_w)
            acc_sh[...] = jnp.zeros_like(acc_sh)
            acc_q[...] = jnp.zeros_like(acc_q)

        xv = x_ref[...]
        r = _rms(xv)
        xh = xv * r
        sg = g_ref[...] * (1.0 + sc_ref[...])
        hb = (xh * sg + sh_ref[...]).astype(BF16)
        dh = lax.dot_general(d_refs[0][...], w_ref[:, 0:SW], NT, preferred_element_type=F32)
        for s in range(1, nsec):
            dh = dh + lax.dot_general(d_refs[s][...], w_ref[:, s * SW:(s + 1) * SW], NT, preferred_element_type=F32)
        for p in range(nsec * SW // PW):
            col = p * PW
            s, so = col // SW, col % SW
            j, jo = col // CW, col % CW
            acc_w[j, :, jo:jo + PW] += lax.dot_general(hb, d_refs[s][:, so:so + PW], TN, preferred_element_type=F32)
        q = dh * xh
        acc_sh[...] += _colsum8(dh)
        acc_q[...] += _colsum8(q)
        dxi_ref[...] = dxo_ref[...] + r * (dh * sg - xh * jnp.mean(q * sg, axis=-1, keepdims=True))

        @pl.when(i == nt - 1)
        def _():
            dw_ref[...] = acc_w[...].astype(BF16)
            sq = jnp.sum(acc_q[...], axis=0, keepdims=True)
            dsh_ref[...] = jnp.sum(acc_sh[...], axis=0, keepdims=True)
            dsc_ref[...] = g_ref[...] * sq
            dg_ref[...] = (1.0 + sc_ref[...]) * sq

    row = pl.BlockSpec((1, D), lambda i: (0, 0))
    tile = pl.BlockSpec((tm, D), lambda i: (i, 0))
    sect = pl.BlockSpec((tm, SW), lambda i: (i, 0))
    rowshape = jax.ShapeDtypeStruct((1, D), F32)
    return pl.pallas_call(
        body, name="in_bwd", grid=(nt,),
        in_specs=[sect] * nsec + [pl.BlockSpec((D, NB * CW), lambda i: (0, 0)), tile, tile,
                                  _mod_row(l, 0, D), _mod_row(l, 1, D), _layer_row(l, D)],
        out_specs=[tile, pl.BlockSpec((NB, D, CW), lambda i: (0, 0, 0)), row, row, row],
        out_shape=[jax.ShapeDtypeStruct((T, D), F32), jax.ShapeDtypeStruct((NB, D, CW), BF16),
                   rowshape, rowshape, rowshape],
        scratch_shapes=[pltpu.VMEM((NB, D, CW), F32),
                        pltpu.VMEM((SUBLANES, D), F32), pltpu.VMEM((SUBLANES, D), F32)],
        compiler_params=_params(VMEM_BIG),
    )(*dsecs, wg, x, dxo, mod4, mod4, g_pre3)


def _rcopy(src, dst, ssem, rsem, dev):
    return pltpu.make_async_remote_copy(src_ref=src, dst_ref=dst, send_sem=ssem, recv_sem=rsem,
                                        device_id=dev, device_id_type=MESH)


def _peers7(x, y, c):
    out = []
    for m in range(1, N_DEV):
        bx, by, bc = (m >> 2) & 1, (m >> 1) & 1, m & 1
        out.append(((1 - x) if bx else x, (1 - y) if by else y, (1 - c) if bc else c))
    return out


HBM = pl.BlockSpec(memory_space=pltpu.HBM)
SEM = pl.BlockSpec(memory_space=pltpu.SEMAPHORE)
SPLIT = pltpu.CompilerParams(has_side_effects=pltpu.SideEffectType.DATAFLOW_SIDE_EFFECTING)


def _hbm(a):
    return pltpu.with_memory_space_constraint(a, pltpu.HBM)


def _chips(x, y):
    return [(1 - x, y), (x, 1 - y), (1 - x, 1 - y)]


SIBLING_BARRIER_ID = 0


def xchg_start(name, bufs, n_copies, plan, sibling_only=False, after=()):
    n = len(bufs)
    after = list(after)

    def body(*refs):
        ssem, rsem, token = refs[n + len(after)], refs[n + len(after) + 1], refs[-1]
        x, y, c = _me()
        if sibling_only:
            barrier = pltpu.get_barrier_semaphore()
            pl.semaphore_signal(barrier, inc=1, device_id=(x, y, 1 - c), device_id_type=MESH)
            pl.semaphore_wait(barrier, 1)
        copies = plan(refs[0:n], x, y, c)
        assert len(copies) == n_copies
        for k, (src, dst, peer, _) in enumerate(copies):
            _rcopy(src, dst, ssem.at[k], rsem.at[k], peer).start()
        token[...] = jnp.zeros_like(token)

    params = dict(has_side_effects=pltpu.SideEffectType.DATAFLOW_SIDE_EFFECTING)
    if sibling_only:
        params["collective_id"] = SIBLING_BARRIER_ID
    outs = pl.pallas_call(
        body, name=name,
        in_specs=[HBM] * n + [ANY] * len(after),
        out_specs=[SEM, SEM] + [HBM] * n + [pl.BlockSpec(memory_space=pltpu.VMEM)],
        out_shape=([pltpu.SemaphoreType.DMA((n_copies,))] * 2 + [pltpu.HBM(b.shape, b.dtype) for b in bufs]
                   + [jax.ShapeDtypeStruct((SUBLANES, LANES), F32)]),
        input_output_aliases={a: 2 + a for a in range(n)},
        compiler_params=pltpu.CompilerParams(**params),
    )(*[_hbm(b) for b in bufs], *after)
    return outs[0], outs[1], list(outs[2:2 + n]), outs[-1]


def xchg_wait(name, bufs, ssem, rsem, n_copies, plan, after, sems=None):
    n = len(bufs)
    after = list(after)
    sems = tuple(range(n_copies)) if sems is None else tuple(sems)
    assert len(sems) == n_copies

    def body(*refs):
        ssem_ref, rsem_ref = refs[n], refs[n + 1]
        copies = plan(refs[0:n], *_me())
        assert len(copies) == n_copies
        for k, (src, _, peer, land) in zip(sems, copies):
            cp = _rcopy(src, land, ssem_ref.at[k], rsem_ref.at[k], peer)
            cp.wait_send()
            cp.wait_recv()

    outs = pl.pallas_call(
        body, name=name,
        in_specs=[HBM] * n + [SEM, SEM] + [ANY] * len(after), out_specs=[HBM] * n,
        out_shape=[pltpu.HBM(b.shape, b.dtype) for b in bufs],
        input_output_aliases={a: a for a in range(n)},
        compiler_params=SPLIT,
    )(*bufs, ssem, rsem, *after)
    return list(outs)


def _shard_half(buf, chip, half):
    if len(buf.shape) == 2:
        h, w = buf.shape[0] // 2, buf.shape[1] // N_CHIPS
        return buf.at[pl.ds(half * h, h), pl.ds(chip * w, w)]
    h = buf.shape[1] // 2
    return buf.at[chip, pl.ds(half * h, h)]


def plan_gather(refs, x, y, c):
    out = []
    for (px, py) in _chips(x, y):
        for buf in refs:
            own = _shard_half(buf, 2 * x + y, c)
            out.append((own, own, (px, py, c), _shard_half(buf, 2 * px + py, c)))
    return out


def plan_forward(refs, x, y, c):
    out = []
    for (px, py) in _chips(x, y):
        for buf in refs:
            landed = _shard_half(buf, 2 * px + py, c)
            out.append((landed, landed, (x, y, 1 - c), _shard_half(buf, 2 * px + py, 1 - c)))
    return out


def plan_sibling(refs, x, y, c):
    n = len(refs) // 2
    out = []
    for a in range(n):
        h = refs[a].shape[1] // 2
        out.append((refs[a].at[:, pl.ds((1 - c) * h, h)], refs[n + a], (x, y, 1 - c), refs[n + a]))
    return out


def plan_chip(refs, x, y, c):
    n = len(refs) // 2
    out = []
    for j, (px, py) in enumerate(_chips(x, y)):
        for a in range(n):
            out.append((refs[a].at[2 * px + py], refs[n + a].at[j], (px, py, c), refs[n + a].at[j]))
    return out


def plan_mod(refs, x, y, c):
    (mods,) = refs
    mine = mods.at[2 * x + y]
    return [(mine, mine, (px, py, c), mods.at[2 * px + py]) for (px, py) in _chips(x, y)]


def plan_pack(refs, x, y, c):
    (packs,) = refs
    mine = packs.at[4 * x + 2 * y + c]
    return [(mine, mine, peer, packs.at[4 * peer[0] + 2 * peer[1] + peer[2]]) for peer in _peers7(x, y, c)]


def plan_spread(layers, wp_layers):
    def plan(refs, x, y, c):
        gi, go, gp = refs
        hD, hR, hP = gi.shape[1] // 2, go.shape[1] // 2, gp.shape[2] // 2
        sib = (x, y, 1 - c)
        out = []
        for l in layers:
            mine = gi.at[l, pl.ds(c * hD, hD)]
            out.append((mine, mine, sib, gi.at[l, pl.ds((1 - c) * hD, hD)]))
            mine = go.at[l, pl.ds(c * hR, hR)]
            out.append((mine, mine, sib, go.at[l, pl.ds((1 - c) * hR, hR)]))
        for l in wp_layers:
            mine = gp.at[l, 2 * x + y, pl.ds(c * hP, hP)]
            for peer in _peers7(x, y, c):
                out.append((mine, mine, peer, gp.at[l, 2 * peer[0] + peer[1], pl.ds(peer[2] * hP, hP)]))
        return out

    return plan


def gather_small(c8, wc):
    def body(c_ref, wc_ref, call, wcall, ssem, rsem, lsem):
        x, y, c = _me()
        myc = 2 * x + y
        me_lin = 4 * x + 2 * y + c
        me = (x, y, c)
        local = [pltpu.make_async_copy(c_ref, call.at[me_lin], lsem.at[0]),
                 pltpu.make_async_copy(wc_ref, wcall.at[myc], lsem.at[1])]
        for cp in local:
            cp.start()
        sends, recvs = [], []
        for m, peer in enumerate(_peers7(x, y, c)):
            plin = 4 * peer[0] + 2 * peer[1] + peer[2]
            sends.append(_rcopy(c_ref, call.at[me_lin], ssem.at[m], rsem.at[m], peer))
            recvs.append(_rcopy(call.at[plin], call.at[plin], ssem.at[m], rsem.at[m], me))
        for j, (px, py) in enumerate([(1 - x, y), (x, 1 - y), (1 - x, 1 - y)]):
            pc = 2 * px + py
            sends.append(_rcopy(wc_ref, wcall.at[myc], ssem.at[7 + j], rsem.at[7 + j], (px, py, c)))
            recvs.append(_rcopy(wcall.at[pc], wcall.at[pc], ssem.at[7 + j], rsem.at[7 + j], me))
        for cp in sends:
            cp.start()
        for cp in recvs:
            cp.wait_recv()
        for cp in sends:
            cp.wait_send()
        for cp in local:
            cp.wait()

    return pl.pallas_call(
        body, name="gather_small",
        in_specs=[ANY] * 2, out_specs=[ANY] * 2,
        out_shape=[jax.ShapeDtypeStruct((N_DEV, SUBLANES, LANES), F32),
                   jax.ShapeDtypeStruct((N_CHIPS, wc.shape[0], 3, LANES), F32)],
        scratch_shapes=[pltpu.SemaphoreType.DMA((10,)), pltpu.SemaphoreType.DMA((10,)), pltpu.SemaphoreType.DMA((2,))],
        compiler_params=_params(n_grid=0),
    )(c8, wc)


def add_sibling(cidx, mine, sib):
    def body(c_ref, *refs):
        for a in range(3):
            m, s, o = refs[a], refs[3 + a], refs[6 + a]
            o[...] = (m[...].astype(F32) + s[...].astype(F32)).astype(BF16)

    per_step = 2

    def mine_spec(a):
        h = a.shape[1] // 2
        return pl.BlockSpec((per_step, h, a.shape[2]), lambda j, c_ref: (j, c_ref[0], 0))

    def sib_spec(a):
        return pl.BlockSpec((per_step,) + a.shape[1:], lambda j, c_ref: (j, 0, 0))

    return pl.pallas_call(
        body, name="add_sibling",
        grid_spec=pltpu.PrefetchScalarGridSpec(
            num_scalar_prefetch=1, grid=(N_CHIPS // per_step,),
            in_specs=[mine_spec(a) for a in mine] + [sib_spec(a) for a in sib],
            out_specs=[sib_spec(a) for a in sib]),
        out_shape=[jax.ShapeDtypeStruct(a.shape, BF16) for a in sib],
        compiler_params=_params(VMEM_BIG),
    )(cidx, *mine, *sib)


def sum_chips(pos, own, rb, acc, l, shapes):
    nq = 2
    n_in = 6 + (3 if acc is not None else 0)

    def body(pos_ref, *refs):
        for a in range(3):
            m, b, o = refs[a], refs[3 + a], refs[n_in + a]
            s = m[...].astype(F32)
            for j in range(3):
                s = s + b[j].astype(F32)
            o[...] = s

    def own_spec(a):
        return pl.BlockSpec((None, a.shape[1] // nq, a.shape[2]), lambda q, p: (p[1], q, 0))

    def rb_spec(a):
        return pl.BlockSpec((3, a.shape[1] // nq, a.shape[2]), lambda q, p: (0, q, 0))

    hi, ho, hp = own[0].shape[1] // nq, own[1].shape[1] // nq, own[2].shape[1] // nq
    out_specs = [pl.BlockSpec((None, hi, shapes[0][2]), lambda q, p: (l, p[0] * nq + q, 0)),
                 pl.BlockSpec((None, ho, shapes[1][2]), lambda q, p: (l, p[0] * nq + q, 0)),
                 pl.BlockSpec((None, None, hp, LANES), lambda q, p: (l, p[1], p[0] * nq + q, 0))]
    in_specs = [own_spec(a) for a in own] + [rb_spec(a) for a in rb]
    args = list(own) + list(rb)
    aliases = {}
    if acc is not None:
        in_specs += [ANY] * 3
        args += list(acc)
        aliases = {7: 0, 8: 1, 9: 2}
    return pl.pallas_call(
        body, name="sum_chips",
        grid_spec=pltpu.PrefetchScalarGridSpec(num_scalar_prefetch=1, grid=(nq,), in_specs=in_specs, out_specs=out_specs),
        out_shape=[jax.ShapeDtypeStruct(s, F32) for s in shapes],
        input_output_aliases=aliases,
        compiler_params=_params(VMEM_BIG),
    )(pos, *args)


def _wconv_slot(chip, tap):
    idx = 3 * chip + tap
    return ROW_WCONV + idx // SUBLANES, slice((idx % SUBLANES) * LANES, (idx % SUBLANES + 1) * LANES)


def pack_small(pos, per_layer, loss_blk):
    L = len(per_layer)
    D = per_layer[0][0].shape[1]

    def body(pos_ref, *refs):
        o = refs[-1]
        lb = refs[-2]
        o[...] = jnp.zeros_like(o)
        for l in range(L):
            dgpre, dgpost, dsh, dsc, dgt, dps, dwc = refs[7 * l:7 * l + 7]
            base = SUBLANES * l
            o[pl.ds(base + ROW_G_PRE, 1), :] = dgpre[...]
            o[pl.ds(base + ROW_G_POST, 1), :] = dgpost[...]
            for r, src in enumerate((dsh, dsc, dgt)):
                o[pl.ds(base + ROW_MOD + r, 1), :] = src[...]
            o[pl.ds(base + ROW_PSCALE, 1), 0:dps.shape[1]] = dps[...]
            for j in range(dwc.shape[0]):
                for k in range(3):
                    row, lanes = _wconv_slot(j, k)
                    o[pl.ds(base + row, 1), lanes] = dwc[j, pl.ds(k, 1), :]
        o[pl.ds(ROW_PSCALE, 1), LOSS_LANES] = lb[pl.ds(0, 1), :]

    flat = [a for layer in per_layer for a in layer] + [loss_blk]

    def whole(a):
        return pl.BlockSpec(a.shape, lambda i, p: (0,) * a.ndim)

    return pl.pallas_call(
        body, name="pack_small",
        grid_spec=pltpu.PrefetchScalarGridSpec(
            num_scalar_prefetch=1, grid=(1,), in_specs=[whole(a) for a in flat],
            out_specs=pl.BlockSpec((None, L * SUBLANES, D), lambda i, p: (p[2], 0, 0))),
        out_shape=jax.ShapeDtypeStruct((N_DEV, L * SUBLANES, D), F32),
        compiler_params=_params(),
    )(pos, *flat)


def small_update(pos, packs, params, moments_m, moments_v):
    n = len(params)
    L, D = params[1].shape
    PS = params[3].shape[1]

    def body(pos_ref, p_ref, *refs):
        ws, ms, vs = refs[0:n], refs[n:2 * n], refs[2 * n:3 * n]
        loss_ref = refs[3 * n]
        outs = [refs[3 * n + 1 + 4 * t:3 * n + 5 + 4 * t] for t in range(n)]
        summed = refs[-1]
        s = p_ref[0]
        for d in range(1, N_DEV):
            s = s + p_ref[d]
        summed[...] = s
        loss_ref[...] = summed[pl.ds(ROW_PSCALE, 1), LOSS_LANES]
        chip = pos_ref[1]

        def update(t, idx, g):
            d, mm, vv = _adamw_math(ws[t][idx], g, ms[t][idx], vs[t][idx])
            g_ref, d_ref, mo_ref, vo_ref = outs[t]
            g_ref[idx] = g
            d_ref[idx] = d
            mo_ref[idx] = mm
            vo_ref[idx] = vv

        for l in range(L):
            base = SUBLANES * l
            row = pl.ds(l, 1)
            for k in range(3):
                update(0, (row, slice(k * D, (k + 1) * D)), summed[pl.ds(base + ROW_MOD + k, 1), :])
            update(1, (row, slice(None)), summed[pl.ds(base + ROW_G_PRE, 1), :])
            update(2, (row, slice(None)), summed[pl.ds(base + ROW_G_POST, 1), :])
            update(3, (row, slice(None)), summed[pl.ds(base + ROW_PSCALE, 1), 0:PS])
            for k in range(3):
                g = None
                for j in range(N_CHIPS):
                    wrow, lanes = _wconv_slot(j, k)
                    cand = summed[pl.ds(base + wrow, 1), lanes]
                    g = cand if g is None else jnp.where(chip == j, cand, g)
                update(4, (l, pl.ds(k, 1), slice(None)), g)

    def whole(a):
        return pl.BlockSpec(a.shape, lambda i, p: (0,) * a.ndim)

    ins = [packs] + list(params) + list(moments_m) + list(moments_v)
    out_shape = [jax.ShapeDtypeStruct((1, LANES), F32)]
    for w in params:
        out_shape += [jax.ShapeDtypeStruct(w.shape, F32)] * 4
    outs = pl.pallas_call(
        body, name="small_update",
        grid_spec=pltpu.PrefetchScalarGridSpec(
            num_scalar_prefetch=1, grid=(1,), in_specs=[whole(a) for a in ins],
            out_specs=[whole(a) for a in out_shape],
            scratch_shapes=[pltpu.VMEM(packs.shape[1:], F32)]),
        out_shape=out_shape,
        compiler_params=_params(),
    )(pos, *ins)
    return outs[0], [outs[1 + 4 * t:5 + 4 * t] for t in range(n)]


def _adamw_math(w, g, m, v):
    m = ADAM_B1 * m + (1.0 - ADAM_B1) * g
    v = ADAM_B2 * v + (1.0 - ADAM_B2) * (g * g)
    m_hat = m / (1.0 - ADAM_B1 ** ADAM_STEP)
    v_hat = v / (1.0 - ADAM_B2 ** ADAM_STEP)
    delta = -ADAM_LR * (m_hat / (jnp.sqrt(v_hat) + ADAM_EPS) + ADAM_WD * w)
    return delta, m, v


def adamw(w, g, m, v, block, name, first=0, count=None, acc=None):
    grid = tuple(s // b for s, b in zip(w.shape, block))
    if count is not None:
        grid = (count,) + grid[1:]

    def body(w_ref, g_ref, m_ref, v_ref, *rest):
        go_ref, d_ref, mo_ref, vo_ref = rest[-4:]
        gv = g_ref[...]
        d, mm, vv = _adamw_math(w_ref[...], gv, m_ref[...], v_ref[...])
        go_ref[...] = gv
        d_ref[...] = d
        mo_ref[...] = mm
        vo_ref[...] = vv

    spec = pl.BlockSpec(block, lambda i, *rest: (first + i,) + rest)
    shape = jax.ShapeDtypeStruct(w.shape, F32)
    extra = [] if acc is None else list(acc)
    return pl.pallas_call(
        body, name=name, grid=grid,
        in_specs=[spec] * 4 + [ANY] * len(extra), out_specs=[spec] * 4, out_shape=[shape] * 4,
        input_output_aliases={4 + a: a for a in range(len(extra))},
        compiler_params=_params(VMEM_BIG, n_grid=len(grid)),
    )(w, g, m, v, *extra)


def ada_finish(c_all, dmod, w, m, v):
    L, D, CW = w.shape
    hD = D // 2

    def body(c_ref, d_ref, w_ref, m_ref, v_ref, g_ref, dl_ref, mo_ref, vo_ref):
        cv = c_ref[...]
        z = jnp.zeros_like(cv)
        ca = jnp.concatenate([cv * jax.nn.sigmoid(cv), z], axis=0).astype(BF16)
        dm = jnp.concatenate([d_ref[0], jnp.zeros_like(d_ref[0])], axis=0).astype(BF16)
        g = lax.dot_general(ca, dm, TN, preferred_element_type=F32)
        g_ref[0] = g
        d, mm, vv = _adamw_math(w_ref[0], g, m_ref[0], v_ref[0])
        dl_ref[0] = d
        mo_ref[0] = mm
        vo_ref[0] = vv

    big = pl.BlockSpec((1, hD, CW), lambda l, h: (l, h, 0))
    shape = jax.ShapeDtypeStruct(w.shape, F32)
    return pl.pallas_call(
        body, name="ada_finish", grid=(L, 2),
        in_specs=[pl.BlockSpec((N_DEV, hD), lambda l, h: (0, h)), pl.BlockSpec((1, N_DEV, CW), lambda l, h: (l, 0, 0)),
                  big, big, big],
        out_specs=[big] * 4, out_shape=[shape] * 4,
        compiler_params=_params(VMEM_BIG, n_grid=2),
    )(c_all, dmod, w, m, v)


def kernel(x, c, w_ada, b_ada, g_pre, w_in, w_conv, w_pool, pool_scale, w_out, g_post, loss_target, m_w_ada, m_b_ada, m_g_pre, m_w_in, m_w_conv, m_w_pool, m_pool_scale, m_w_out, m_g_post, v_w_ada, v_b_ada, v_g_pre, v_w_in, v_w_conv, v_w_pool, v_pool_scale, v_w_out, v_g_post):
    L, D, CW = w_in.shape
    RO = w_out.shape[1]
    T = x.shape[1]
    ix, iy, ic = _me()
    chip = 2 * ix + iy
    me_lin = 4 * ix + 2 * iy + ic

    pos = jnp.stack([ic, chip, me_lin]).astype(jnp.int32)
    g_pre3, g_post3 = g_pre.reshape(L, 1, D), g_post.reshape(L, 1, D)
    pscale3 = pool_scale.reshape(L, 1, pool_scale.shape[1])
    n_s, n_c = 3, 9

    def gather(bufs, after):
        ss, rs, bufs, tok = xchg_start("gather_start", bufs, 3 * len(bufs), plan_gather, after=after)
        return (ss, rs, bufs), tok

    def ready(flight, after):
        fss, frs, bufs = flight
        return xchg_wait("forward_wait", bufs, fss, frs, 3 * len(bufs), plan_forward, after)

    def arrive_part(flight, which, after):
        ss, rs, bufs = flight
        sems = tuple(range(which, 3 * len(bufs), len(bufs)))
        (buf,) = xchg_wait("gather_wait", [bufs[which]], ss, rs, 3, plan_gather, after, sems=sems)
        fss, frs, (buf,), tok = xchg_start("forward_start", [buf], 3, plan_forward, sibling_only=True)
        return (fss, frs, [buf]), tok

    c_all3, wconv_all = gather_small(c.reshape(SUBLANES, LANES), w_conv)
    c_all = c_all3.reshape(N_DEV, D)
    w_in_of, w_out_of = [None] * L, [None] * L
    gi0, go0 = cast_weights(pos, w_in, w_out, 0, c_all3)
    flight, token = gather([gi0], [])
    w_in_of[0] = (flight, 0)
    b_my = lax.dynamic_slice_in_dim(b_ada, chip * CW, CW, axis=1)
    m_ss, m_rs, mods, token = xchg_start("mod_start", [mod_part(pos, c_all, w_ada, b_my, token)], 3, plan_mod)
    gi1, go1 = cast_weights(pos, w_in, w_out, 1, token)
    flight, token = gather([gi1], [])
    w_in_of[1] = (flight, 0)
    flight, token = gather([go0, go1], [token])
    w_out_of[0], w_out_of[1] = (flight, 0), (flight, 1)
    for l in range(2, L):
        flight, token = gather(list(cast_weights(pos, w_in, w_out, l, token)), [])
        w_in_of[l], w_out_of[l] = (flight, 0), (flight, 1)
    fwd_in, token = arrive_part(*w_in_of[0], [token])
    (mod_all,) = xchg_wait("mod_wait", mods, m_ss, m_rs, 3, plan_mod, [token])
    mod = lax.dynamic_index_in_dim(mod_all, me_lin, axis=2, keepdims=False)
    mod4 = jnp.transpose(mod, (1, 0, 2)).reshape(L, 3, 1, D)

    xs, projs, yas, yps, ys, pooleds = [x.reshape(T, D)], [], [], [], [], []
    wg_in, wg_out = [], []
    for l in range(L):
        (gi,) = ready(fwd_in, [mod4 if l == 0 else xs[l]])
        proj = proj_fwd(xs[l], mod4, g_pre3, gi, l)
        ya, yp, pooled = mix_fwd(proj, wconv_all, w_pool, pscale3, l)
        pooleds.append(pooled)
        fwd_out, token = arrive_part(*w_out_of[l], [ya, yp])
        after = [token]
        if l + 1 < L:
            fwd_in, token = arrive_part(*w_in_of[l + 1], after)
            after = [token]
        (go,) = ready(fwd_out, after)
        wg_in.append(gi)
        wg_out.append(go.reshape(N_CHIPS * RO, D))
        projs.append(proj)
        yas.append(ya)
        yps.append(yp)
        if l + 1 < L:
            xn, yv = out_fwd(ya, yp, wg_out[l], xs[l], mod4, g_post3, l, after[0])
            xs.append(xn)
        else:
            dx, yv, loss_blk = out_fwd_loss(ya, yp, wg_out[l], xs[l], mod4, g_post3, l, loss_target.reshape(T, D))
        ys.append(yv)

    shapes = (w_in.shape, w_out.shape, w_pool.shape)
    smalls = [None] * L
    acc, flying, sib, token = None, None, None, loss_blk

    def to_chips(sib, after):
        sl, s_ss, s_rs, s_bufs = sib
        s_bufs = xchg_wait("sibling_wait", s_bufs, s_ss, s_rs, n_s, plan_sibling, after)
        chip_parts = add_sibling(pos, s_bufs[0:3], s_bufs[3:6])
        lands = [lax.empty((3,) + a.shape[1:], a.dtype) for a in chip_parts]
        c_ss, c_rs, c_bufs, ctoken = xchg_start("chip_start", list(chip_parts) + lands, n_c, plan_chip)
        return (sl, c_ss, c_rs, c_bufs), ctoken

    def landed(flying, acc, after):
        fl, f_ss, f_rs, f_bufs = flying
        f_bufs = xchg_wait("chip_wait", f_bufs, f_ss, f_rs, n_c, plan_chip, after)
        return sum_chips(pos, f_bufs[0:3], f_bufs[3:6], acc, fl, shapes)

    for l in reversed(range(L)):
        dya, dyp, dwo_l, dgate, dgpost = out_bwd(dx, ys[l], yas[l], yps[l], wg_out[l], mod4, g_post3, l, token)
        token = dya
        if sib is not None:
            arrived = flying
            flying, token = to_chips(sib, [dya])
            if arrived is not None:
                acc = landed(arrived, acc, [token])
                token = acc[0]
        du_a, db_a, dc_a, dg_a, du_p, dg_p, dwc, dwp_l, dps = mix_bwd(projs[l], pooleds[l], dya, dyp, wconv_all, w_pool,
                                                                        pscale3, l, token)
        dx, dwi_l, dshift, dscale, dgpre = in_bwd([du_a, db_a, dc_a, dg_a, du_p, dg_p], wg_in[l], xs[l], dx,
                                                  mod4, g_pre3, l)
        smalls[l] = (dgpre, dgpost, dshift, dscale, dgate, dps, dwc)
        parts = [dwi_l, dwo_l.reshape(N_CHIPS, RO, D), dwp_l]
        s_lands = [lax.empty((a.shape[0], a.shape[1] // 2) + a.shape[2:], a.dtype) for a in parts]
        s_ss, s_rs, s_bufs, token = xchg_start("sibling_start", parts + s_lands, n_s, plan_sibling, sibling_only=True)
        sib = (l, s_ss, s_rs, s_bufs)
    grad_x = dx.reshape(1, T, D)

    p_ss, p_rs, packs, ptoken = xchg_start("pack_start", [pack_small(pos, smalls, loss_blk)], N_DEV - 1, plan_pack)
    acc = landed(flying, acc, [ptoken, token])
    n_sp = (2 + N_DEV - 1) * (L - 1)
    spread = plan_spread(tuple(range(1, L)), tuple(range(1, L)))
    sp_ss, sp_rs, acc, sp_token = xchg_start("spread_start", list(acc), n_sp, spread)
    flying, token = to_chips(sib, [sp_token])
    (packs_all,) = xchg_wait("pack_wait", packs, p_ss, p_rs, N_DEV - 1, plan_pack, [token])
    dmod_all = packs_all.reshape(N_DEV, L, SUBLANES, D)[:, :, ROW_MOD:ROW_MOD + 3].reshape(N_DEV, L, 3 * D)
    dmod_my = jnp.transpose(lax.dynamic_slice_in_dim(dmod_all, chip * CW, CW, axis=2), (1, 0, 2))

    g_w_ada, d_w_ada, nm_w_ada, nv_w_ada = ada_finish(c_all, dmod_my, w_ada, m_w_ada, v_w_ada)
    loss_row, upd = small_update(pos, packs_all, [b_ada, g_pre, g_post, pool_scale, w_conv],
                                 [m_b_ada, m_g_pre, m_g_post, m_pool_scale, m_w_conv],
                                 [v_b_ada, v_g_pre, v_g_post, v_pool_scale, v_w_conv])
    loss = loss_row[0, 0]
    (g_b_ada, d_b_ada, nm_b_ada, nv_b_ada), (g_g_pre, d_g_pre, nm_g_pre, nv_g_pre) = upd[0], upd[1]
    (g_g_post, d_g_post, nm_g_post, nv_g_post), (g_pscale, d_pscale, nm_pscale, nv_pscale) = upd[2], upd[3]
    g_w_conv, d_w_conv, nm_w_conv, nv_w_conv = upd[4]

    done = [nv_w_ada, nv_w_conv]
    g_w_in, g_w_out, g_w_pool = xchg_wait("spread_wait", acc, sp_ss, sp_rs, n_sp, spread, done)
    in_blk, out_blk = (1, D // 2, CW), (1, RO, D)
    upd_in = adamw(w_in, g_w_in, m_w_in, v_w_in, in_blk, "adamw_w_in", 1, L - 1)
    upd_out = adamw(w_out, g_w_out, m_w_out, v_w_out, out_blk, "adamw_w_out", 1, L - 1)

    acc = landed(flying, (g_w_in, g_w_out, g_w_pool), [upd_in[3], upd_out[3]])
    last = plan_spread((0,), (0,))
    n_last = 2 + N_DEV - 1
    l_ss, l_rs, acc, _ = xchg_start("spread_start", list(acc), n_last, last)
    pshape = (L, N_CHIPS * LANES, LANES)
    pool_blk = (1,) + pshape[1:]
    wp, mp, vp = w_pool.reshape(pshape), m_w_pool.reshape(pshape), v_w_pool.reshape(pshape)
    upd_pool = adamw(wp, acc[2].reshape(pshape), mp, vp, pool_blk, "adamw_w_pool", 1, L - 1)
    r_w_in, r_w_out, r_w_pool = xchg_wait("spread_wait", acc, l_ss, l_rs, n_last, last, [upd_pool[3]])
    g_w_in, d_w_in, nm_w_in, nv_w_in = adamw(w_in, r_w_in, m_w_in, v_w_in, in_blk, "adamw_w_in", 0, 1, upd_in)
    g_w_out, d_w_out, nm_w_out, nv_w_out = adamw(w_out, r_w_out, m_w_out, v_w_out, out_blk, "adamw_w_out", 0, 1, upd_out)
    upd_pool = adamw(wp, r_w_pool.reshape(pshape), mp, vp, pool_blk, "adamw_w_pool", 0, 1, upd_pool)
    g_w_pool, d_w_pool, nm_w_pool, nv_w_pool = [a.reshape(w_pool.shape) for a in upd_pool]

    return (loss, grad_x,
            g_w_ada, g_b_ada, g_g_pre, g_w_in, g_w_conv, g_w_pool, g_pscale, g_w_out, g_g_post,
            d_w_ada, d_b_ada, d_g_pre, d_w_in, d_w_conv, d_w_pool, d_pscale, d_w_out, d_g_post,
            nm_w_ada, nm_b_ada, nm_g_pre, nm_w_in, nm_w_conv, nm_w_pool, nm_pscale, nm_w_out, nm_g_post,
            nv_w_ada, nv_b_ada, nv_g_pre, nv_w_in, nv_w_conv, nv_w_pool, nv_pscale, nv_w_out, nv_g_post)
```

```python
import functools

import jax
import jax.numpy as jnp
from jax import lax
from jax.experimental import pallas as pl
from jax.experimental.pallas import tpu as pltpu

F32 = jnp.float32
BF16 = jnp.bfloat16
MESH = pl.DeviceIdType.MESH
ANY = pl.BlockSpec(memory_space=pl.ANY)

NORM_EPS = 1e-6
POOL_WINDOWS = (2, 4, 8, 16)
ADAM_LR = 0.001
ADAM_B1 = 0.9
ADAM_B2 = 0.999
ADAM_EPS = 1e-08
ADAM_WD = 0.01
ADAM_STEP = 10

N_CHIPS = 4
N_DEV = 8
LANES = 128
SUBLANES = 8
VMEM_BIG = 56 * 1024 * 1024
HIST = 16
R_CONV = 64
R_POOL = 128

ROW_G_PRE, ROW_G_POST, ROW_MOD, ROW_PSCALE, ROW_WCONV = 0, 1, 2, 5, 6
LOSS_LANES = slice(4 * LANES, 5 * LANES)

NT = (((1,), (1,)), ((), ()))
TN = (((0,), (0,)), ((), ()))


def _params(vmem=None, n_grid=1):
    kw = {}
    if n_grid:
        kw["dimension_semantics"] = ("arbitrary",) * n_grid
    if vmem is not None:
        kw["vmem_limit_bytes"] = vmem
    return pltpu.CompilerParams(**kw)


def _colsum8(v):
    n, d = v.shape
    return v.reshape(n // SUBLANES, SUBLANES, d).sum(axis=0)


def _rms(v):
    return lax.rsqrt(jnp.mean(v * v, axis=-1, keepdims=True) + NORM_EPS)


def _sigmoid(v):
    return 0.5 * jnp.tanh(0.5 * v) + 0.5


def _shift_down(ext, k, rows):
    if k == 0:
        return ext[HIST:HIST + rows]
    return pltpu.roll(ext, k, 0)[HIST:HIST + rows]


def _shift_up(ext, k, rows):
    if k == 0:
        return ext[0:rows]
    return pltpu.roll(ext, ext.shape[0] - k, 0)[0:rows]


def _load_ext(ref, r0, h0, first, rows):
    hist = ref[pl.ds(h0, HIST), :].astype(F32)
    hist = jnp.where(first, 0.0, hist)
    cur = ref[pl.ds(r0, rows), :].astype(F32)
    return jnp.concatenate([hist, cur], axis=0)


def _me():
    return lax.axis_index("x"), lax.axis_index("y"), lax.axis_index("c")


def cast_weights(pos, w_in, w_out, l, after):
    _, D, CW = w_in.shape
    RO = w_out.shape[1]

    def body(pos_ref, wi, wo, after_ref, oi, oo):
        oi[...] = wi[...].astype(BF16)
        oo[...] = wo[...].astype(BF16)

    return pl.pallas_call(
        body, name="cast_w",
        grid_spec=pltpu.PrefetchScalarGridSpec(
            num_scalar_prefetch=1, grid=(2,),
            in_specs=[pl.BlockSpec((None, D // 2, CW), lambda h, p: (l, h, 0)),
                      pl.BlockSpec((None, RO // 2, D), lambda h, p: (l, h, 0)), ANY],
            out_specs=[pl.BlockSpec((D // 2, CW), lambda h, p: (h, p[1])),
                       pl.BlockSpec((None, RO // 2, D), lambda h, p: (p[1], h, 0))]),
        out_shape=[jax.ShapeDtypeStruct((D, N_CHIPS * CW), BF16), jax.ShapeDtypeStruct((N_CHIPS, RO, D), BF16)],
        compiler_params=_params(),
    )(pos, w_in, w_out, after)


def mod_part(pos, c_all, w_ada, b_my, after):
    L, D, CW = w_ada.shape

    def body(pos_ref, c_ref, w_ref, b_ref, after_ref, o_ref):
        cv = c_ref[...]
        ca = (cv * jax.nn.sigmoid(cv)).astype(BF16)
        o_ref[...] = jnp.dot(ca, w_ref[0].astype(BF16), preferred_element_type=F32) + b_ref[0]

    return pl.pallas_call(
        body, name="mod_part",
        grid_spec=pltpu.PrefetchScalarGridSpec(
            num_scalar_prefetch=1, grid=(L,),
            in_specs=[pl.BlockSpec((N_DEV, D), lambda l, p: (0, 0)),
                      pl.BlockSpec((1, D, CW), lambda l, p: (l, 0, 0)),
                      pl.BlockSpec((1, 1, CW), lambda l, p: (l, 0, 0)), ANY],
            out_specs=pl.BlockSpec((None, None, N_DEV, CW), lambda l, p: (p[1], l, 0, 0))),
        out_shape=jax.ShapeDtypeStruct((N_CHIPS, L, N_DEV, CW), F32),
        compiler_params=_params(VMEM_BIG),
    )(pos, c_all, w_ada, b_my.reshape(L, 1, CW), after)


def _mod_row(l, k, D):
    return pl.BlockSpec((None, None, 1, D), lambda *_: (l, k, 0, 0))


def _layer_row(l, D):
    return pl.BlockSpec((None, 1, D), lambda *_: (l, 0, 0))


def proj_fwd(x, mod4, g_pre3, wg, l):
    T, D = x.shape
    NC = wg.shape[1]
    NB = N_CHIPS
    CW = NC // NB
    tm = 512

    def body(x_ref, sh_ref, sc_ref, g_ref, w_ref, o_ref):
        xv = x_ref[...]
        h = (xv * _rms(xv)) * (g_ref[...] * (1.0 + sc_ref[...])) + sh_ref[...]
        hb = h.astype(BF16)
        for j in range(NB):
            cols = slice(j * CW, (j + 1) * CW)
            o_ref[:, cols] = jnp.dot(hb, w_ref[:, cols], preferred_element_type=F32).astype(BF16)

    return pl.pallas_call(
        body, name="proj_fwd", grid=(T // tm,),
        in_specs=[pl.BlockSpec((tm, D), lambda i: (i, 0)), _mod_row(l, 0, D), _mod_row(l, 1, D), _layer_row(l, D),
                  pl.BlockSpec((D, NC), lambda i: (0, 0))],
        out_specs=pl.BlockSpec((tm, NC), lambda i: (i, 0)),
        out_shape=jax.ShapeDtypeStruct((T, NC), BF16),
        compiler_params=_params(VMEM_BIG),
    )(x, mod4, mod4, g_pre3, wg)


N_MIX = 4


def _conv_fwd_block(u_ref, b_ref, c_ref, g_ref, w_ref, o_ref):
    T = u_ref.shape[0]
    R = 2 * R_CONV
    w0 = w_ref[pl.ds(0, 1), :]
    w1 = w_ref[pl.ds(1, 1), :]
    w2 = w_ref[pl.ds(2, 1), :]

    def chunk(i, carry):
        r0 = pl.multiple_of(i * R, R)
        h0 = pl.multiple_of(jnp.maximum(r0 - HIST, 0), HIST)
        first = i == 0
        ca = _load_ext(c_ref, r0, h0, first, R) * _load_ext(u_ref, r0, h0, first, R)
        conv = w2 * ca[HIST:] + w1 * _shift_down(ca, 1, R) + w0 * _shift_down(ca, 2, R)
        g = g_ref[pl.ds(r0, R), :].astype(F32)
        b = b_ref[pl.ds(r0, R), :].astype(F32)
        o_ref[pl.ds(r0, R), :] = (b * conv * (g * _sigmoid(g))).astype(BF16)
        return carry

    lax.fori_loop(0, T // R, chunk, 0)


def _conv_idx(j):
    return jnp.minimum(j, N_MIX - 1)


def _pool_idx(j):
    return jnp.maximum(j - N_MIX, 0)


def _proj_col(T, off, idx):
    return pl.BlockSpec((T, LANES), lambda j: (0, idx(j) + off))


def _causal_window_sum(ext, w):
    s, k = ext, 1
    while k < w:
        s = s + pltpu.roll(s, k, 0)
        k *= 2
    return s


def _anticausal_window_sum(ext, w):
    s, k = ext, 1
    n = ext.shape[0]
    while k < w:
        s = s + pltpu.roll(s, n - k, 0)
        k *= 2
    return s


def _count(r0, rows, w):
    t = r0 + lax.broadcasted_iota(jnp.int32, (rows, LANES), 0)
    return jnp.minimum(t + 1, w).astype(F32)


def _pooled_loop(p_ref, pooled_s, w, T):
    R = R_POOL

    def chunk(i, carry):
        r0 = pl.multiple_of(i * R, R)
        h0 = pl.multiple_of(jnp.maximum(r0 - HIST, 0), HIST)
        ext = _load_ext(p_ref, r0, h0, i == 0, R)
        ws = _causal_window_sum(ext, w)[HIST:]
        pooled_s[pl.ds(r0, R), :] = (ws / _count(r0, R, w) - ext[HIST:]).astype(BF16)
        return carry

    lax.fori_loop(0, T // R, chunk, 0)


def _conv_w_spec(l):
    return pl.BlockSpec((None, None, 3, LANES), lambda j: (_conv_idx(j), l, 0, 0))


def _pool_w_spec(l):
    return pl.BlockSpec((None, None, LANES, LANES), lambda j: (l, _pool_idx(j), 0, 0))


def _pool_s_spec(l):
    return pl.BlockSpec((None, 1, LANES), lambda j: (l, 0, _pool_idx(j)))


def _pool_fwd_group(p_ref, g_ref, w_ref, s_ref, o_ref, pooled_s, mixed_s, w):
    T = p_ref.shape[0]
    R = R_POOL
    _pooled_loop(p_ref, pooled_s, w, T)
    mixed_s[...] = jnp.dot(pooled_s[...], w_ref[...].astype(BF16), preferred_element_type=F32)
    sc = s_ref[...]

    def chunk(i, carry):
        r0 = pl.multiple_of(i * R, R)
        g = g_ref[pl.ds(r0, R), :].astype(F32)
        o_ref[pl.ds(r0, R), :] = (mixed_s[pl.ds(r0, R), :] * sc * (g * _sigmoid(g))).astype(BF16)
        return carry

    lax.fori_loop(0, T // R, chunk, 0)


def mix_fwd(proj, wconv, wpool, pscale3, l):
    T = proj.shape[0]

    def body(u_ref, b_ref, c_ref, g_ref, p_ref, gp_ref, wc_ref, wp_ref, s_ref, ya_ref, yp_ref, pooled_ref, mixed_s):
        j = pl.program_id(0)
        pl.when(j < N_MIX)(functools.partial(_conv_fwd_block, u_ref, b_ref, c_ref, g_ref, wc_ref, ya_ref))
        for k, w in enumerate(POOL_WINDOWS):
            pl.when(j == N_MIX + k)(functools.partial(_pool_fwd_group, p_ref, gp_ref, wp_ref, s_ref, yp_ref,
                                                      pooled_ref, mixed_s, w))

    half = jax.ShapeDtypeStruct((T, N_MIX * LANES), BF16)
    pool_col = pl.BlockSpec((T, LANES), lambda j: (0, _pool_idx(j)))
    return pl.pallas_call(
        body, name="mix_fwd", grid=(2 * N_MIX,),
        in_specs=[_proj_col(T, 0, _conv_idx), _proj_col(T, 4, _conv_idx), _proj_col(T, 8, _conv_idx),
                  _proj_col(T, 12, _conv_idx), _proj_col(T, 16, _pool_idx), _proj_col(T, 20, _pool_idx),
                  _conv_w_spec(l), _pool_w_spec(l), _pool_s_spec(l)],
        out_specs=[pl.BlockSpec((T, LANES), lambda j: (0, _conv_idx(j))), pool_col, pool_col],
        out_shape=[half, half, half],
        scratch_shapes=[pltpu.VMEM((T, LANES), F32)],
        compiler_params=_params(),
    )(proj, proj, proj, proj, proj, proj, wconv, wpool, pscale3)


def out_fwd(ya, yp, wo, x, mod4, g_post3, l, after):
    T, D = x.shape
    H = ya.shape[1]
    tm = 512

    def body(ya_ref, yp_ref, wo_ref, x_ref, gt_ref, g_ref, after_ref, xn_ref, y_ref):
        y = (jnp.dot(ya_ref[...], wo_ref[0:H, :], preferred_element_type=F32)
             + jnp.dot(yp_ref[...], wo_ref[H:2 * H, :], preferred_element_type=F32))
        xn_ref[...] = x_ref[...] + gt_ref[...] * (y * _rms(y) * g_ref[...])
        y_ref[...] = y.astype(BF16)

    tile = pl.BlockSpec((tm, D), lambda i: (i, 0))
    half = pl.BlockSpec((tm, H), lambda i: (i, 0))
    return pl.pallas_call(
        body, name="out_fwd", grid=(T // tm,),
        in_specs=[half, half, pl.BlockSpec((2 * H, D), lambda i: (0, 0)), tile, _mod_row(l, 2, D), _layer_row(l, D),
                  ANY],
        out_specs=[tile, tile],
        out_shape=[jax.ShapeDtypeStruct((T, D), F32), jax.ShapeDtypeStruct((T, D), BF16)],
        compiler_params=_params(VMEM_BIG),
    )(ya, yp, wo, x, mod4, g_post3, after)


def out_fwd_loss(ya, yp, wo, x, mod4, g_post3, l, target):
    T, D = x.shape
    H = ya.shape[1]
    tm = 512
    nt = T // tm

    def body(ya_ref, yp_ref, wo_ref, x_ref, gt_ref, g_ref, t_ref, dx_ref, y_ref, l_ref, acc):
        i = pl.program_id(0)

        @pl.when(i == 0)
        def _():
            acc[...] = jnp.zeros_like(acc)

        y = (jnp.dot(ya_ref[...], wo_ref[0:H, :], preferred_element_type=F32)
             + jnp.dot(yp_ref[...], wo_ref[H:2 * H, :], preferred_element_type=F32))
        y_ref[...] = y.astype(BF16)
        d = (x_ref[...] + gt_ref[...] * (y * _rms(y) * g_ref[...])) - t_ref[...]
        dx_ref[...] = d * (1.0 / D)
        acc[...] += _colsum8(d * d)

        @pl.when(i == nt - 1)
        def _():
            l_ref[...] = jnp.zeros_like(l_ref) + jnp.sum(acc[...]) * (0.5 / D)

    tile = pl.BlockSpec((tm, D), lambda i: (i, 0))
    half = pl.BlockSpec((tm, H), lambda i: (i, 0))
    return pl.pallas_call(
        body, name="out_fwd_loss", grid=(nt,),
        in_specs=[half, half, pl.BlockSpec((2 * H, D), lambda i: (0, 0)), tile, _mod_row(l, 2, D), _layer_row(l, D),
                  tile],
        out_specs=[tile, tile, pl.BlockSpec((SUBLANES, LANES), lambda i: (0, 0))],
        out_shape=[jax.ShapeDtypeStruct((T, D), F32), jax.ShapeDtypeStruct((T, D), BF16),
                   jax.ShapeDtypeStruct((SUBLANES, LANES), F32)],
        scratch_shapes=[pltpu.VMEM((SUBLANES, D), F32)],
        compiler_params=_params(VMEM_BIG),
    )(ya, yp, wo, x, mod4, g_post3, target)


def out_bwd(dx, y, ya, yp, wo, mod4, g_post3, l, after):
    T, D = dx.shape
    H = ya.shape[1]
    tm = 512
    nt = T // tm

    def body(dx_ref, y_ref, ya_ref, yp_ref, wo_ref, gt_ref, g_ref, after_ref,
             dya_ref, dyp_ref, dwo_ref, dgt_ref, dg_ref, acc_w, acc_p):
        i = pl.program_id(0)

        @pl.when(i == 0)
        def _():
            acc_w[...] = jnp.zeros_like(acc_w)
            acc_p[...] = jnp.zeros_like(acc_p)

        yv = y_ref[...].astype(F32)
        dxv = dx_ref[...]
        gg = gt_ref[...] * g_ref[...]
        r = _rms(yv)
        yn = yv * r
        p = dxv * yn
        acc_p[...] += _colsum8(p)
        dy = r * (dxv * gg - yn * jnp.mean(p * gg, axis=-1, keepdims=True))
        dyb = dy.astype(BF16)
        dyc = lax.dot_general(dyb, wo_ref[...], NT, preferred_element_type=F32)
        dya_ref[...] = dyc[:, 0:H].astype(BF16)
        dyp_ref[...] = dyc[:, H:2 * H].astype(BF16)
        acc_w[0:H, :] += lax.dot_general(ya_ref[...], dyb, TN, preferred_element_type=F32)
        acc_w[H:2 * H, :] += lax.dot_general(yp_ref[...], dyb, TN, preferred_element_type=F32)

        @pl.when(i == nt - 1)
        def _():
            dwo_ref[...] = acc_w[...].astype(BF16)
            sp = jnp.sum(acc_p[...], axis=0, keepdims=True)
            dgt_ref[...] = g_ref[...] * sp
            dg_ref[...] = gt_ref[...] * sp

    row = pl.BlockSpec((1, D), lambda i: (0, 0))
    tile = pl.BlockSpec((tm, D), lambda i: (i, 0))
    half = pl.BlockSpec((tm, H), lambda i: (i, 0))
    full = pl.BlockSpec((2 * H, D), lambda i: (0, 0))
    return pl.pallas_call(
        body, name="out_bwd", grid=(nt,),
        in_specs=[tile, tile, half, half, full, _mod_row(l, 2, D), _layer_row(l, D), ANY],
        out_specs=[half, half, full, row, row],
        out_shape=[jax.ShapeDtypeStruct((T, H), BF16), jax.ShapeDtypeStruct((T, H), BF16),
                   jax.ShapeDtypeStruct((2 * H, D), BF16),
                   jax.ShapeDtypeStruct((1, D), F32), jax.ShapeDtypeStruct((1, D), F32)],
        scratch_shapes=[pltpu.VMEM((2 * H, D), F32), pltpu.VMEM((SUBLANES, D), F32)],
        compiler_params=_params(VMEM_BIG),
    )(dx, y, ya, yp, wo, mod4, g_post3, after)


def _conv_bwd_block(u_ref, b_ref, c_ref, g_ref, dy_ref, w_ref, du_ref, db_ref, dc_ref, dg_ref, dw_ref):
    T = u_ref.shape[0]
    R = R_CONV
    nchunk = T // R
    w0 = w_ref[pl.ds(0, 1), :]
    w1 = w_ref[pl.ds(1, 1), :]
    w2 = w_ref[pl.ds(2, 1), :]

    def chunk(k, carry):
        head, a0, a1, a2 = carry
        i = nchunk - 1 - k
        r0 = pl.multiple_of(i * R, R)
        h0 = pl.multiple_of(jnp.maximum(r0 - HIST, 0), HIST)
        first = i == 0
        ue = _load_ext(u_ref, r0, h0, first, R)
        ce = _load_ext(c_ref, r0, h0, first, R)
        ca = ce * ue
        ca0 = ca[HIST:]
        ca1 = _shift_down(ca, 1, R)
        ca2 = _shift_down(ca, 2, R)
        conv = w2 * ca0 + w1 * ca1 + w0 * ca2
        g = g_ref[pl.ds(r0, R), :].astype(F32)
        b = b_ref[pl.ds(r0, R), :].astype(F32)
        dy = dy_ref[pl.ds(r0, R), :].astype(F32)
        sg = _sigmoid(g)
        sl = g * sg
        t = dy * conv
        db_ref[pl.ds(r0, R), :] = (t * sl).astype(BF16)
        dg_ref[pl.ds(r0, R), :] = (t * b * (sg + sl * (1.0 - sg))).astype(BF16)
        dconv = dy * b * sl
        a2 = a2 + _colsum8(dconv * ca0)
        a1 = a1 + _colsum8(dconv * ca1)
        a0 = a0 + _colsum8(dconv * ca2)
        e = jnp.concatenate([dconv, head], axis=0)
        dca = w2 * dconv + w1 * _shift_up(e, 1, R) + w0 * _shift_up(e, 2, R)
        du_ref[pl.ds(r0, R), :] = (dca * ce[HIST:]).astype(BF16)
        dc_ref[pl.ds(r0, R), :] = (dca * ue[HIST:]).astype(BF16)
        return dconv[0:SUBLANES], a0, a1, a2

    z = jnp.zeros((SUBLANES, LANES), F32)
    _, a0, a1, a2 = lax.fori_loop(0, nchunk, chunk, (z, z, z, z))
    dw_ref[pl.ds(0, 1), :] = jnp.sum(a0, axis=0, keepdims=True)
    dw_ref[pl.ds(1, 1), :] = jnp.sum(a1, axis=0, keepdims=True)
    dw_ref[pl.ds(2, 1), :] = jnp.sum(a2, axis=0, keepdims=True)


def _pool_bwd_group(pooled_s, g_ref, dy_ref, w_ref, s_ref, du_ref, dg_ref, dw_ref, ds_ref,
                    mixed_s, dmix_s, dpool_s, w):
    T = pooled_s.shape[0]
    R = R_POOL
    nchunk = T // R
    wb = w_ref[...].astype(BF16)
    mixed_s[...] = jnp.dot(pooled_s[...], wb, preferred_element_type=F32)
    sc = s_ref[...]

    def gate_chunk(i, acc):
        r0 = pl.multiple_of(i * R, R)
        g = g_ref[pl.ds(r0, R), :].astype(F32)
        dy = dy_ref[pl.ds(r0, R), :].astype(F32)
        mixed = mixed_s[pl.ds(r0, R), :]
        sg = _sigmoid(g)
        sl = g * sg
        dg_ref[pl.ds(r0, R), :] = (dy * mixed * sc * (sg + sl * (1.0 - sg))).astype(BF16)
        dms = dy * sl
        dmix_s[pl.ds(r0, R), :] = (dms * sc).astype(BF16)
        return acc + _colsum8(dms * mixed)

    acc = lax.fori_loop(0, nchunk, gate_chunk, jnp.zeros((SUBLANES, LANES), F32))
    ds_ref[...] = jnp.sum(acc, axis=0, keepdims=True)
    dpool_s[pl.ds(0, T), :] = lax.dot_general(dmix_s[...], wb, NT, preferred_element_type=F32)
    dpool_s[pl.ds(T, HIST), :] = jnp.zeros((HIST, LANES), F32)
    dw_ref[...] = lax.dot_general(pooled_s[...], dmix_s[...], TN, preferred_element_type=F32).astype(BF16)

    def back_chunk(i, carry):
        r0 = pl.multiple_of(i * R, R)
        dpe = dpool_s[pl.ds(r0, R + HIST), :]
        e = dpe / _count(r0, R + HIST, w)
        du_ref[pl.ds(r0, R), :] = (_anticausal_window_sum(e, w)[0:R] - dpe[0:R]).astype(BF16)
        return carry

    lax.fori_loop(0, nchunk, back_chunk, 0)


def mix_bwd(proj, pooled, dya, dyp, wconv, wpool, pscale3, l, after):
    T = proj.shape[0]

    def body(u_ref, b_ref, c_ref, g_ref, pooled_ref, gp_ref, dya_ref, dyp_ref, wc_ref, wp_ref, s_ref, after_ref,
             dua_ref, dba_ref, dca_ref, dga_ref, dup_ref, dgp_ref, dwc_ref, dwp_ref, ds_ref,
             mixed_s, dmix_s, dpool_s):
        j = pl.program_id(0)
        pl.when(j < N_MIX)(functools.partial(_conv_bwd_block, u_ref, b_ref, c_ref, g_ref, dya_ref, wc_ref,
                                             dua_ref, dba_ref, dca_ref, dga_ref, dwc_ref))
        for k, w in enumerate(POOL_WINDOWS):
            pl.when(j == N_MIX + k)(functools.partial(_pool_bwd_group, pooled_ref, gp_ref, dyp_ref, wp_ref, s_ref,
                                                      dup_ref, dgp_ref, dwp_ref, ds_ref,
                                                      mixed_s, dmix_s, dpool_s, w))

    sec = jax.ShapeDtypeStruct((T, N_MIX * LANES), BF16)
    conv_col = pl.BlockSpec((T, LANES), lambda j: (0, _conv_idx(j)))
    pool_col = pl.BlockSpec((T, LANES), lambda j: (0, _pool_idx(j)))
    return pl.pallas_call(
        body, name="mix_bwd", grid=(2 * N_MIX,),
        in_specs=[_proj_col(T, 0, _conv_idx), _proj_col(T, 4, _conv_idx), _proj_col(T, 8, _conv_idx),
                  _proj_col(T, 12, _conv_idx), pool_col, _proj_col(T, 20, _pool_idx),
                  conv_col, pool_col, _conv_w_spec(l), _pool_w_spec(l), _pool_s_spec(l), ANY],
        out_specs=[conv_col, conv_col, conv_col, conv_col, pool_col, pool_col,
                   pl.BlockSpec((None, 3, LANES), lambda j: (_conv_idx(j), 0, 0)),
                   pl.BlockSpec((None, LANES, LANES), lambda j: (_pool_idx(j), 0, 0)),
                   pl.BlockSpec((1, LANES), lambda j: (0, _pool_idx(j)))],
        out_shape=[sec] * 6 + [jax.ShapeDtypeStruct((N_MIX, 3, LANES), F32),
                               jax.ShapeDtypeStruct((N_MIX, LANES, LANES), BF16),
                               jax.ShapeDtypeStruct((1, N_MIX * LANES), F32)],
        scratch_shapes=[pltpu.VMEM((T, LANES), F32), pltpu.VMEM((T, LANES), BF16), pltpu.VMEM((T + HIST, LANES), F32)],
        compiler_params=_params(),
    )(proj, proj, proj, proj, pooled, proj, dya, dyp, wconv, wpool, pscale3, after)


def in_bwd(dsecs, wg, x, dxo, mod4, g_pre3, l):
    T, D = x.shape
    NB = N_CHIPS
    CW = wg.shape[1] // NB
    SW = dsecs[0].shape[1]
    nsec = len(dsecs)
    PW = 256
    assert SW % PW == 0 and CW % PW == 0
    tm = 256
    nt = T // tm

    def body(*refs):
        d_refs = refs[0:nsec]
        w_ref, x_ref, dxo_ref, sh_ref, sc_ref, g_ref = refs[nsec:nsec + 6]
        dxi_ref, dw_ref, dsh_ref, dsc_ref, dg_ref = refs[nsec + 6:nsec + 11]
        acc_w, acc_sh, acc_q = refs[nsec + 11:]
        i = pl.program_id(0)

        @pl.when(i == 0)
        def _():
            acc_w[...] = jnp.zeros_like(acc_w)
            acc_sh[...] = jnp.zeros_like(acc_sh)
            acc_q[...] = jnp.zeros_like(acc_q)

        xv = x_ref[...]
        r = _rms(xv)
        xh = xv * r
        sg = g_ref[...] * (1.0 + sc_ref[...])
        hb = (xh * sg + sh_ref[...]).astype(BF16)
        dh = lax.dot_general(d_refs[0][...], w_ref[:, 0:SW], NT, preferred_element_type=F32)
        for s in range(1, nsec):
            dh = dh + lax.dot_general(d_refs[s][...], w_ref[:, s * SW:(s + 1) * SW], NT, preferred_element_type=F32)
        for p in range(nsec * SW // PW):
            col = p * PW
            s, so = col // SW, col % SW
            j, jo = col // CW, col % CW
            acc_w[j, :, jo:jo + PW] += lax.dot_general(hb, d_refs[s][:, so:so + PW], TN, preferred_element_type=F32)
        q = dh * xh
        acc_sh[...] += _colsum8(dh)
        acc_q[...] += _colsum8(q)
        dxi_ref[...] = dxo_ref[...] + r * (dh * sg - xh * jnp.mean(q * sg, axis=-1, keepdims=True))

        @pl.when(i == nt - 1)
        def _():
            dw_ref[...] = acc_w[...].astype(BF16)
            sq = jnp.sum(acc_q[...], axis=0, keepdims=True)
            dsh_ref[...] = jnp.sum(acc_sh[...], axis=0, keepdims=True)
            dsc_ref[...] = g_ref[...] * sq
            dg_ref[...] = (1.0 + sc_ref[...]) * sq

    row = pl.BlockSpec((1, D), lambda i: (0, 0))
    tile = pl.BlockSpec((tm, D), lambda i: (i, 0))
    sect = pl.BlockSpec((tm, SW), lambda i: (i, 0))
    rowshape = jax.ShapeDtypeStruct((1, D), F32)
    return pl.pallas_call(
        body, name="in_bwd", grid=(nt,),
        in_specs=[sect] * nsec + [pl.BlockSpec((D, NB * CW), lambda i: (0, 0)), tile, tile,
                                  _mod_row(l, 0, D), _mod_row(l, 1, D), _layer_row(l, D)],
        out_specs=[tile, pl.BlockSpec((NB, D, CW), lambda i: (0, 0, 0)), row, row, row],
        out_shape=[jax.ShapeDtypeStruct((T, D), F32), jax.ShapeDtypeStruct((NB, D, CW), BF16),
                   rowshape, rowshape, rowshape],
        scratch_shapes=[pltpu.VMEM((NB, D, CW), F32),
                        pltpu.VMEM((SUBLANES, D), F32), pltpu.VMEM((SUBLANES, D), F32)],
        compiler_params=_params(VMEM_BIG),
    )(*dsecs, wg, x, dxo, mod4, mod4, g_pre3)


def _rcopy(src, dst, ssem, rsem, dev):
    return pltpu.make_async_remote_copy(src_ref=src, dst_ref=dst, send_sem=ssem, recv_sem=rsem,
                                        device_id=dev, device_id_type=MESH)


def _peers7(x, y, c):
    out = []
    for m in range(1, N_DEV):
        bx, by, bc = (m >> 2) & 1, (m >> 1) & 1, m & 1
        out.append(((1 - x) if bx else x, (1 - y) if by else y, (1 - c) if bc else c))
    return out


HBM = pl.BlockSpec(memory_space=pltpu.HBM)
SEM = pl.BlockSpec(memory_space=pltpu.SEMAPHORE)
SPLIT = pltpu.CompilerParams(has_side_effects=pltpu.SideEffectType.DATAFLOW_SIDE_EFFECTING)


def _hbm(a):
    return pltpu.with_memory_space_constraint(a, pltpu.HBM)


def _chips(x, y):
    return [(1 - x, y), (x, 1 - y), (1 - x, 1 - y)]


SIBLING_BARRIER_ID = 0


def xchg_start(name, bufs, n_copies, plan, sibling_only=False, after=()):
    n = len(bufs)
    after = list(after)

    def body(*refs):
        ssem, rsem, token = refs[n + len(after)], refs[n + len(after) + 1], refs[-1]
        x, y, c = _me()
        if sibling_only:
            barrier = pltpu.get_barrier_semaphore()
            pl.semaphore_signal(barrier, inc=1, device_id=(x, y, 1 - c), device_id_type=MESH)
            pl.semaphore_wait(barrier, 1)
        copies = plan(refs[0:n], x, y, c)
        assert len(copies) == n_copies
        for k, (src, dst, peer, _) in enumerate(copies):
            _rcopy(src, dst, ssem.at[k], rsem.at[k], peer).start()
        token[...] = jnp.zeros_like(token)

    params = dict(has_side_effects=pltpu.SideEffectType.DATAFLOW_SIDE_EFFECTING)
    if sibling_only:
        params["collective_id"] = SIBLING_BARRIER_ID
    outs = pl.pallas_call(
        body, name=name,
        in_specs=[HBM] * n + [ANY] * len(after),
        out_specs=[SEM, SEM] + [HBM] * n + [pl.BlockSpec(memory_space=pltpu.VMEM)],
        out_shape=([pltpu.SemaphoreType.DMA((n_copies,))] * 2 + [pltpu.HBM(b.shape, b.dtype) for b in bufs]
                   + [jax.ShapeDtypeStruct((SUBLANES, LANES), F32)]),
        input_output_aliases={a: 2 + a for a in range(n)},
        compiler_params=pltpu.CompilerParams(**params),
    )(*[_hbm(b) for b in bufs], *after)
    return outs[0], outs[1], list(outs[2:2 + n]), outs[-1]


def xchg_wait(name, bufs, ssem, rsem, n_copies, plan, after, sems=None):
    n = len(bufs)
    after = list(after)
    sems = tuple(range(n_copies)) if sems is None else tuple(sems)
    assert len(sems) == n_copies

    def body(*refs):
        ssem_ref, rsem_ref = refs[n], refs[n + 1]
        copies = plan(refs[0:n], *_me())
        assert len(copies) == n_copies
        for k, (src, _, peer, land) in zip(sems, copies):
            cp = _rcopy(src, land, ssem_ref.at[k], rsem_ref.at[k], peer)
            cp.wait_send()
            cp.wait_recv()

    outs = pl.pallas_call(
        body, name=name,
        in_specs=[HBM] * n + [SEM, SEM] + [ANY] * len(after), out_specs=[HBM] * n,
        out_shape=[pltpu.HBM(b.shape, b.dtype) for b in bufs],
        input_output_aliases={a: a for a in range(n)},
        compiler_params=SPLIT,
    )(*bufs, ssem, rsem, *after)
    return list(outs)


def _shard_half(buf, chip, half):
    if len(buf.shape) == 2:
        h, w = buf.shape[0] // 2, buf.shape[1] // N_CHIPS
        return buf.at[pl.ds(half * h, h), pl.ds(chip * w, w)]
    h = buf.shape[1] // 2
    return buf.at[chip, pl.ds(half * h, h)]


def plan_gather(refs, x, y, c):
    out = []
    for (px, py) in _chips(x, y):
        for buf in refs:
            own = _shard_half(buf, 2 * x + y, c)
            out.append((own, own, (px, py, c), _shard_half(buf, 2 * px + py, c)))
    return out


def plan_forward(refs, x, y, c):
    out = []
    for (px, py) in _chips(x, y):
        for buf in refs:
            landed = _shard_half(buf, 2 * px + py, c)
            out.append((landed, landed, (x, y, 1 - c), _shard_half(buf, 2 * px + py, 1 - c)))
    return out


def plan_sibling(refs, x, y, c):
    n = len(refs) // 2
    out = []
    for a in range(n):
        h = refs[a].shape[1] // 2
        out.append((refs[a].at[:, pl.ds((1 - c) * h, h)], refs[n + a], (x, y, 1 - c), refs[n + a]))
    return out


def plan_chip(refs, x, y, c):
    n = len(refs) // 2
    out = []
    for j, (px, py) in enumerate(_chips(x, y)):
        for a in range(n):
            out.append((refs[a].at[2 * px + py], refs[n + a].at[j], (px, py, c), refs[n + a].at[j]))
    return out


def plan_mod(refs, x, y, c):
    (mods,) = refs
    mine = mods.at[2 * x + y]
    return [(mine, mine, (px, py, c), mods.at[2 * px + py]) for (px, py) in _chips(x, y)]


def plan_pack(refs, x, y, c):
    (packs,) = refs
    mine = packs.at[4 * x + 2 * y + c]
    return [(mine, mine, peer, packs.at[4 * peer[0] + 2 * peer[1] + peer[2]]) for peer in _peers7(x, y, c)]


def plan_spread(layers, wp_layers):
    def plan(refs, x, y, c):
        gi, go, gp = refs
        hD, hR, hP = gi.shape[1] // 2, go.shape[1] // 2, gp.shape[2] // 2
        sib = (x, y, 1 - c)
        out = []
        for l in layers:
            mine = gi.at[l, pl.ds(c * hD, hD)]
            out.append((mine, mine, sib, gi.at[l, pl.ds((1 - c) * hD, hD)]))
            mine = go.at[l, pl.ds(c * hR, hR)]
            out.append((mine, mine, sib, go.at[l, pl.ds((1 - c) * hR, hR)]))
        for l in wp_layers:
            mine = gp.at[l, 2 * x + y, pl.ds(c * hP, hP)]
            for peer in _peers7(x, y, c):
                out.append((mine, mine, peer, gp.at[l, 2 * peer[0] + peer[1], pl.ds(peer[2] * hP, hP)]))
        return out

    return plan


def place_small(pos, c8, wc):
    L = wc.shape[0]

    def body(pos_ref, c_ref, wc_ref, call_ref, wcall_ref):
        call_ref[...] = c_ref[...]
        wcall_ref[...] = wc_ref[...]

    return pl.pallas_call(
        body, name="place_small",
        grid_spec=pltpu.PrefetchScalarGridSpec(
            num_scalar_prefetch=1, grid=(1,),
            in_specs=[pl.BlockSpec((SUBLANES, LANES), lambda i, p: (0, 0)),
                      pl.BlockSpec((L, 3, LANES), lambda i, p: (0, 0, 0))],
            out_specs=[pl.BlockSpec((None, SUBLANES, LANES), lambda i, p: (p[2], 0, 0)),
                       pl.BlockSpec((None, L, 3, LANES), lambda i, p: (p[1], 0, 0, 0))]),
        out_shape=[jax.ShapeDtypeStruct((N_DEV, SUBLANES, LANES), F32),
                   jax.ShapeDtypeStruct((N_CHIPS, L, 3, LANES), F32)],
        compiler_params=_params(),
    )(pos, c8, wc)


def plan_small(refs, x, y, c):
    call, wcall = refs
    mine = call.at[4 * x + 2 * y + c]
    out = [(mine, mine, peer, call.at[4 * peer[0] + 2 * peer[1] + peer[2]]) for peer in _peers7(x, y, c)]
    mine = wcall.at[2 * x + y]
    out += [(mine, mine, (px, py, c), wcall.at[2 * px + py]) for (px, py) in _chips(x, y)]
    return out


def add_sibling(cidx, mine, sib):
    def body(c_ref, *refs):
        for a in range(3):
            m, s, o = refs[a], refs[3 + a], refs[6 + a]
            o[...] = (m[...].astype(F32) + s[...].astype(F32)).astype(BF16)

    per_step = 2

    def mine_spec(a):
        h = a.shape[1] // 2
        return pl.BlockSpec((per_step, h, a.shape[2]), lambda j, c_ref: (j, c_ref[0], 0))

    def sib_spec(a):
        return pl.BlockSpec((per_step,) + a.shape[1:], lambda j, c_ref: (j, 0, 0))

    return pl.pallas_call(
        body, name="add_sibling",
        grid_spec=pltpu.PrefetchScalarGridSpec(
            num_scalar_prefetch=1, grid=(N_CHIPS // per_step,),
            in_specs=[mine_spec(a) for a in mine] + [sib_spec(a) for a in sib],
            out_specs=[sib_spec(a) for a in sib]),
        out_shape=[jax.ShapeDtypeStruct(a.shape, BF16) for a in sib],
        compiler_params=_params(VMEM_BIG),
    )(cidx, *mine, *sib)


def sum_chips(pos, own, rb, acc, l, shapes):
    nq = 2
    n_in = 6 + (3 if acc is not None else 0)

    def body(pos_ref, *refs):
        for a in range(3):
            m, b, o = refs[a], refs[3 + a], refs[n_in + a]
            s = m[...].astype(F32)
            for j in range(3):
                s = s + b[j].astype(F32)
            o[...] = s

    def own_spec(a):
        return pl.BlockSpec((None, a.shape[1] // nq, a.shape[2]), lambda q, p: (p[1], q, 0))

    def rb_spec(a):
        return pl.BlockSpec((3, a.shape[1] // nq, a.shape[2]), lambda q, p: (0, q, 0))

    hi, ho, hp = own[0].shape[1] // nq, own[1].shape[1] // nq, own[2].shape[1] // nq
    out_specs = [pl.BlockSpec((None, hi, shapes[0][2]), lambda q, p: (l, p[0] * nq + q, 0)),
                 pl.BlockSpec((None, ho, shapes[1][2]), lambda q, p: (l, p[0] * nq + q, 0)),
                 pl.BlockSpec((None, None, hp, LANES), lambda q, p: (l, p[1], p[0] * nq + q, 0))]
    in_specs = [own_spec(a) for a in own] + [rb_spec(a) for a in rb]
    args = list(own) + list(rb)
    aliases = {}
    if acc is not None:
        in_specs += [ANY] * 3
        args += list(acc)
        aliases = {7: 0, 8: 1, 9: 2}
    return pl.pallas_call(
        body, name="sum_chips",
        grid_spec=pltpu.PrefetchScalarGridSpec(num_scalar_prefetch=1, grid=(nq,), in_specs=in_specs, out_specs=out_specs),
        out_shape=[jax.ShapeDtypeStruct(s, F32) for s in shapes],
        input_output_aliases=aliases,
        compiler_params=_params(VMEM_BIG),
    )(pos, *args)


def _wconv_slot(chip, tap):
    idx = 3 * chip + tap
    return ROW_WCONV + idx // SUBLANES, slice((idx % SUBLANES) * LANES, (idx % SUBLANES + 1) * LANES)


def pack_small(pos, per_layer, loss_blk):
    L = len(per_layer)
    D = per_layer[0][0].shape[1]

    def body(pos_ref, *refs):
        o = refs[-1]
        lb = refs[-2]
        o[...] = jnp.zeros_like(o)
        for l in range(L):
            dgpre, dgpost, dsh, dsc, dgt, dps, dwc = refs[7 * l:7 * l + 7]
            base = SUBLANES * l
            o[pl.ds(base + ROW_G_PRE, 1), :] = dgpre[...]
            o[pl.ds(base + ROW_G_POST, 1), :] = dgpost[...]
            for r, src in enumerate((dsh, dsc, dgt)):
                o[pl.ds(base + ROW_MOD + r, 1), :] = src[...]
            o[pl.ds(base + ROW_PSCALE, 1), 0:dps.shape[1]] = dps[...]
            for j in range(dwc.shape[0]):
                for k in range(3):
                    row, lanes = _wconv_slot(j, k)
                    o[pl.ds(base + row, 1), lanes] = dwc[j, pl.ds(k, 1), :]
        o[pl.ds(ROW_PSCALE, 1), LOSS_LANES] = lb[pl.ds(0, 1), :]

    flat = [a for layer in per_layer for a in layer] + [loss_blk]

    def whole(a):
        return pl.BlockSpec(a.shape, lambda i, p: (0,) * a.ndim)

    return pl.pallas_call(
        body, name="pack_small",
        grid_spec=pltpu.PrefetchScalarGridSpec(
            num_scalar_prefetch=1, grid=(1,), in_specs=[whole(a) for a in flat],
            out_specs=pl.BlockSpec((None, L * SUBLANES, D), lambda i, p: (p[2], 0, 0))),
        out_shape=jax.ShapeDtypeStruct((N_DEV, L * SUBLANES, D), F32),
        compiler_params=_params(),
    )(pos, *flat)


def small_update(pos, packs, params, moments_m, moments_v):
    n = len(params)
    L, D = params[1].shape
    PS = params[3].shape[1]

    def body(pos_ref, p_ref, *refs):
        ws, ms, vs = refs[0:n], refs[n:2 * n], refs[2 * n:3 * n]
        loss_ref = refs[3 * n]
        outs = [refs[3 * n + 1 + 4 * t:3 * n + 5 + 4 * t] for t in range(n)]
        summed = refs[-1]
        s = p_ref[0]
        for d in range(1, N_DEV):
            s = s + p_ref[d]
        summed[...] = s
        loss_ref[...] = summed[pl.ds(ROW_PSCALE, 1), LOSS_LANES]
        chip = pos_ref[1]

        def update(t, idx, g):
            d, mm, vv = _adamw_math(ws[t][idx], g, ms[t][idx], vs[t][idx])
            g_ref, d_ref, mo_ref, vo_ref = outs[t]
            g_ref[idx] = g
            d_ref[idx] = d
            mo_ref[idx] = mm
            vo_ref[idx] = vv

        for l in range(L):
            base = SUBLANES * l
            row = pl.ds(l, 1)
            for k in range(3):
                update(0, (row, slice(k * D, (k + 1) * D)), summed[pl.ds(base + ROW_MOD + k, 1), :])
            update(1, (row, slice(None)), summed[pl.ds(base + ROW_G_PRE, 1), :])
            update(2, (row, slice(None)), summed[pl.ds(base + ROW_G_POST, 1), :])
            update(3, (row, slice(None)), summed[pl.ds(base + ROW_PSCALE, 1), 0:PS])
            for k in range(3):
                g = None
                for j in range(N_CHIPS):
                    wrow, lanes = _wconv_slot(j, k)
                    cand = summed[pl.ds(base + wrow, 1), lanes]
                    g = cand if g is None else jnp.where(chip == j, cand, g)
                update(4, (l, pl.ds(k, 1), slice(None)), g)

    def whole(a):
        return pl.BlockSpec(a.shape, lambda i, p: (0,) * a.ndim)

    ins = [packs] + list(params) + list(moments_m) + list(moments_v)
    out_shape = [jax.ShapeDtypeStruct((1, LANES), F32)]
    for w in params:
        out_shape += [jax.ShapeDtypeStruct(w.shape, F32)] * 4
    outs = pl.pallas_call(
        body, name="small_update",
        grid_spec=pltpu.PrefetchScalarGridSpec(
            num_scalar_prefetch=1, grid=(1,), in_specs=[whole(a) for a in ins],
            out_specs=[whole(a) for a in out_shape],
            scratch_shapes=[pltpu.VMEM(packs.shape[1:], F32)]),
        out_shape=out_shape,
        compiler_params=_params(),
    )(pos, *ins)
    return outs[0], [outs[1 + 4 * t:5 + 4 * t] for t in range(n)]


def _adamw_math(w, g, m, v):
    m = ADAM_B1 * m + (1.0 - ADAM_B1) * g
    v = ADAM_B2 * v + (1.0 - ADAM_B2) * (g * g)
    m_hat = m / (1.0 - ADAM_B1 ** ADAM_STEP)
    v_hat = v / (1.0 - ADAM_B2 ** ADAM_STEP)
    delta = -ADAM_LR * (m_hat / (jnp.sqrt(v_hat) + ADAM_EPS) + ADAM_WD * w)
    return delta, m, v


def adamw(w, g, m, v, block, name, first=0, count=None, acc=None):
    grid = tuple(s // b for s, b in zip(w.shape, block))
    if count is not None:
        grid = (count,) + grid[1:]

    def body(w_ref, g_ref, m_ref, v_ref, *rest):
        go_ref, d_ref, mo_ref, vo_ref = rest[-4:]
        gv = g_ref[...]
        d, mm, vv = _adamw_math(w_ref[...], gv, m_ref[...], v_ref[...])
        go_ref[...] = gv
        d_ref[...] = d
        mo_ref[...] = mm
        vo_ref[...] = vv

    spec = pl.BlockSpec(block, lambda i, *rest: (first + i,) + rest)
    shape = jax.ShapeDtypeStruct(w.shape, F32)
    extra = [] if acc is None else list(acc)
    return pl.pallas_call(
        body, name=name, grid=grid,
        in_specs=[spec] * 4 + [ANY] * len(extra), out_specs=[spec] * 4, out_shape=[shape] * 4,
        input_output_aliases={4 + a: a for a in range(len(extra))},
        compiler_params=_params(VMEM_BIG, n_grid=len(grid)),
    )(w, g, m, v, *extra)


def ada_finish(c_all, dmod, w, m, v):
    L, D, CW = w.shape
    hD = D // 2

    def body(c_ref, d_ref, w_ref, m_ref, v_ref, g_ref, dl_ref, mo_ref, vo_ref):
        cv = c_ref[...]
        z = jnp.zeros_like(cv)
        ca = jnp.concatenate([cv * jax.nn.sigmoid(cv), z], axis=0).astype(BF16)
        dm = jnp.concatenate([d_ref[0], jnp.zeros_like(d_ref[0])], axis=0).astype(BF16)
        g = lax.dot_general(ca, dm, TN, preferred_element_type=F32)
        g_ref[0] = g
        d, mm, vv = _adamw_math(w_ref[0], g, m_ref[0], v_ref[0])
        dl_ref[0] = d
        mo_ref[0] = mm
        vo_ref[0] = vv

    big = pl.BlockSpec((1, hD, CW), lambda l, h: (l, h, 0))
    shape = jax.ShapeDtypeStruct(w.shape, F32)
    return pl.pallas_call(
        body, name="ada_finish", grid=(L, 2),
        in_specs=[pl.BlockSpec((N_DEV, hD), lambda l, h: (0, h)), pl.BlockSpec((1, N_DEV, CW), lambda l, h: (l, 0, 0)),
                  big, big, big],
        out_specs=[big] * 4, out_shape=[shape] * 4,
        compiler_params=_params(VMEM_BIG, n_grid=2),
    )(c_all, dmod, w, m, v)


def kernel(x, c, w_ada, b_ada, g_pre, w_in, w_conv, w_pool, pool_scale, w_out, g_post, loss_target, m_w_ada, m_b_ada, m_g_pre, m_w_in, m_w_conv, m_w_pool, m_pool_scale, m_w_out, m_g_post, v_w_ada, v_b_ada, v_g_pre, v_w_in, v_w_conv, v_w_pool, v_pool_scale, v_w_out, v_g_post):
    L, D, CW = w_in.shape
    RO = w_out.shape[1]
    T = x.shape[1]
    ix, iy, ic = _me()
    chip = 2 * ix + iy
    me_lin = 4 * ix + 2 * iy + ic

    pos = jnp.stack([ic, chip, me_lin]).astype(jnp.int32)
    g_pre3, g_post3 = g_pre.reshape(L, 1, D), g_post.reshape(L, 1, D)
    pscale3 = pool_scale.reshape(L, 1, pool_scale.shape[1])
    n_s, n_c = 3, 9

    def gather(bufs, after):
        ss, rs, bufs, tok = xchg_start("gather_start", bufs, 3 * len(bufs), plan_gather, after=after)
        return (ss, rs, bufs), tok

    def ready(flight, after):
        fss, frs, bufs = flight
        return xchg_wait("forward_wait", bufs, fss, frs, 3 * len(bufs), plan_forward, after)

    def arrive_part(flight, which, after):
        ss, rs, bufs = flight
        sems = tuple(range(which, 3 * len(bufs), len(bufs)))
        (buf,) = xchg_wait("gather_wait", [bufs[which]], ss, rs, 3, plan_gather, after, sems=sems)
        fss, frs, (buf,), tok = xchg_start("forward_start", [buf], 3, plan_forward, sibling_only=True)
        return (fss, frs, [buf]), tok

    n_small = N_DEV - 1 + N_CHIPS - 1
    s_ss, s_rs, smalls_in, token = xchg_start("small_start", list(place_small(pos, c.reshape(SUBLANES, LANES), w_conv)),
                                              n_small, plan_small)
    w_in_of, w_out_of = [None] * L, [None] * L
    gi0, go0 = cast_weights(pos, w_in, w_out, 0, token)
    flight, token = gather([gi0], [])
    w_in_of[0] = (flight, 0)
    c_all3, wconv_all = xchg_wait("small_wait", smalls_in, s_ss, s_rs, n_small, plan_small, [token])
    c_all = c_all3.reshape(N_DEV, D)
    b_my = lax.dynamic_slice_in_dim(b_ada, chip * CW, CW, axis=1)
    m_ss, m_rs, mods, token = xchg_start("mod_start", [mod_part(pos, c_all, w_ada, b_my, token)], 3, plan_mod)
    gi1, go1 = cast_weights(pos, w_in, w_out, 1, token)
    flight, token = gather([gi1], [])
    w_in_of[1] = (flight, 0)
    flight, token = gather([go0, go1], [token])
    w_out_of[0], w_out_of[1] = (flight, 0), (flight, 1)
    for l in range(2, L):
        flight, token = gather(list(cast_weights(pos, w_in, w_out, l, token)), [])
        w_in_of[l], w_out_of[l] = (flight, 0), (flight, 1)
    fwd_in, token = arrive_part(*w_in_of[0], [token])
    (mod_all,) = xchg_wait("mod_wait", mods, m_ss, m_rs, 3, plan_mod, [token])
    mod = lax.dynamic_index_in_dim(mod_all, me_lin, axis=2, keepdims=False)
    mod4 = jnp.transpose(mod, (1, 0, 2)).reshape(L, 3, 1, D)

    xs, projs, yas, yps, ys, pooleds = [x.reshape(T, D)], [], [], [], [], []
    wg_in, wg_out = [], []
    for l in range(L):
        (gi,) = ready(fwd_in, [mod4 if l == 0 else xs[l]])
        proj = proj_fwd(xs[l], mod4, g_pre3, gi, l)
        ya, yp, pooled = mix_fwd(proj, wconv_all, w_pool, pscale3, l)
        pooleds.append(pooled)
        fwd_out, token = arrive_part(*w_out_of[l], [ya, yp])
        after = [token]
        if l + 1 < L:
            fwd_in, token = arrive_part(*w_in_of[l + 1], after)
            after = [token]
        (go,) = ready(fwd_out, after)
        wg_in.append(gi)
        wg_out.append(go.reshape(N_CHIPS * RO, D))
        projs.append(proj)
        yas.append(ya)
        yps.append(yp)
        if l + 1 < L:
            xn, yv = out_fwd(ya, yp, wg_out[l], xs[l], mod4, g_post3, l, after[0])
            xs.append(xn)
        else:
            dx, yv, loss_blk = out_fwd_loss(ya, yp, wg_out[l], xs[l], mod4, g_post3, l, loss_target.reshape(T, D))
        ys.append(yv)

    shapes = (w_in.shape, w_out.shape, w_pool.shape)
    smalls = [None] * L
    acc, flying, sib, token = None, None, None, loss_blk

    def to_chips(sib, after):
        sl, s_ss, s_rs, s_bufs = sib
        s_bufs = xchg_wait("sibling_wait", s_bufs, s_ss, s_rs, n_s, plan_sibling, after)
        chip_parts = add_sibling(pos, s_bufs[0:3], s_bufs[3:6])
        lands = [lax.empty((3,) + a.shape[1:], a.dtype) for a in chip_parts]
        c_ss, c_rs, c_bufs, ctoken = xchg_start("chip_start", list(chip_parts) + lands, n_c, plan_chip)
        return (sl, c_ss, c_rs, c_bufs), ctoken

    def landed(flying, acc, after):
        fl, f_ss, f_rs, f_bufs = flying
        f_bufs = xchg_wait("chip_wait", f_bufs, f_ss, f_rs, n_c, plan_chip, after)
        return sum_chips(pos, f_bufs[0:3], f_bufs[3:6], acc, fl, shapes)

    for l in reversed(range(L)):
        dya, dyp, dwo_l, dgate, dgpost = out_bwd(dx, ys[l], yas[l], yps[l], wg_out[l], mod4, g_post3, l, token)
        token = dya
        if sib is not None:
            arrived = flying
            flying, token = to_chips(sib, [dya])
            if arrived is not None:
                acc = landed(arrived, acc, [token])
                token = acc[0]
        du_a, db_a, dc_a, dg_a, du_p, dg_p, dwc, dwp_l, dps = mix_bwd(projs[l], pooleds[l], dya, dyp, wconv_all, w_pool,
                                                                        pscale3, l, token)
        dx, dwi_l, dshift, dscale, dgpre = in_bwd([du_a, db_a, dc_a, dg_a, du_p, dg_p], wg_in[l], xs[l], dx,
                                                  mod4, g_pre3, l)
        smalls[l] = (dgpre, dgpost, dshift, dscale, dgate, dps, dwc)
        parts = [dwi_l, dwo_l.reshape(N_CHIPS, RO, D), dwp_l]
        s_lands = [lax.empty((a.shape[0], a.shape[1] // 2) + a.shape[2:], a.dtype) for a in parts]
        s_ss, s_rs, s_bufs, token = xchg_start("sibling_start", parts + s_lands, n_s, plan_sibling, sibling_only=True)
        sib = (l, s_ss, s_rs, s_bufs)
    grad_x = dx.reshape(1, T, D)

    p_ss, p_rs, packs, ptoken = xchg_start("pack_start", [pack_small(pos, smalls, loss_blk)], N_DEV - 1, plan_pack)
    acc = landed(flying, acc, [ptoken, token])
    n_sp = (2 + N_DEV - 1) * (L - 1)
    spread = plan_spread(tuple(range(1, L)), tuple(range(1, L)))
    sp_ss, sp_rs, acc, sp_token = xchg_start("spread_start", list(acc), n_sp, spread)
    flying, token = to_chips(sib, [sp_token])
    (packs_all,) = xchg_wait("pack_wait", packs, p_ss, p_rs, N_DEV - 1, plan_pack, [token])
    dmod_all = packs_all.reshape(N_DEV, L, SUBLANES, D)[:, :, ROW_MOD:ROW_MOD + 3].reshape(N_DEV, L, 3 * D)
    dmod_my = jnp.transpose(lax.dynamic_slice_in_dim(dmod_all, chip * CW, CW, axis=2), (1, 0, 2))

    g_w_ada, d_w_ada, nm_w_ada, nv_w_ada = ada_finish(c_all, dmod_my, w_ada, m_w_ada, v_w_ada)
    loss_row, upd = small_update(pos, packs_all, [b_ada, g_pre, g_post, pool_scale, w_conv],
                                 [m_b_ada, m_g_pre, m_g_post, m_pool_scale, m_w_conv],
                                 [v_b_ada, v_g_pre, v_g_post, v_pool_scale, v_w_conv])
    loss = loss_row[0, 0]
    (g_b_ada, d_b_ada, nm_b_ada, nv_b_ada), (g_g_pre, d_g_pre, nm_g_pre, nv_g_pre) = upd[0], upd[1]
    (g_g_post, d_g_post, nm_g_post, nv_g_post), (g_pscale, d_pscale, nm_pscale, nv_pscale) = upd[2], upd[3]
    g_w_conv, d_w_conv, nm_w_conv, nv_w_conv = upd[4]

    done = [nv_w_ada, nv_w_conv]
    g_w_in, g_w_out, g_w_pool = xchg_wait("spread_wait", acc, sp_ss, sp_rs, n_sp, spread, done)
    in_blk, out_blk = (1, D // 2, CW), (1, RO, D)
    upd_in = adamw(w_in, g_w_in, m_w_in, v_w_in, in_blk, "adamw_w_in", 1, L - 1)
    upd_out = adamw(w_out, g_w_out, m_w_out, v_w_out, out_blk, "adamw_w_out", 1, L - 1)

    acc = landed(flying, (g_w_in, g_w_out, g_w_pool), [upd_in[3], upd_out[3]])
    last = plan_spread((0,), (0,))
    n_last = 2 + N_DEV - 1
    l_ss, l_rs, acc, _ = xchg_start("spread_start", list(acc), n_last, last)
    pshape = (L, N_CHIPS * LANES, LANES)
    pool_blk = (1,) + pshape[1:]
    wp, mp, vp = w_pool.reshape(pshape), m_w_pool.reshape(pshape), v_w_pool.reshape(pshape)
    upd_pool = adamw(wp, acc[2].reshape(pshape), mp, vp, pool_blk, "adamw_w_pool", 1, L - 1)
    r_w_in, r_w_out, r_w_pool = xchg_wait("spread_wait", acc, l_ss, l_rs, n_last, last, [upd_pool[3]])
    g_w_in, d_w_in, nm_w_in, nv_w_in = adamw(w_in, r_w_in, m_w_in, v_w_in, in_blk, "adamw_w_in", 0, 1, upd_in)
    g_w_out, d_w_out, nm_w_out, nv_w_out = adamw(w_out, r_w_out, m_w_out, v_w_out, out_blk, "adamw_w_out", 0, 1, upd_out)
    upd_pool = adamw(wp, r_w_pool.reshape(pshape), mp, vp, pool_blk, "adamw_w_pool", 0, 1, upd_pool)
    g_w_pool, d_w_pool, nm_w_pool, nv_w_pool = [a.reshape(w_pool.shape) for a in upd_pool]

    return (loss, grad_x,
            g_w_ada, g_b_ada, g_g_pre, g_w_in, g_w_conv, g_w_pool, g_pscale, g_w_out, g_g_post,
            d_w_ada, d_b_ada, d_g_pre, d_w_in, d_w_conv, d_w_pool, d_pscale, d_w_out, d_g_post,
            nm_w_ada, nm_b_ada, nm_g_pre, nm_w_in, nm_w_conv, nm_w_pool, nm_pscale, nm_w_out, nm_g_post,
            nv_w_ada, nv_b_ada, nv_g_pre, nv_w_in, nv_w_conv, nv_w_pool, nv_pscale, nv_w_out, nv_g_post)
```

```python
import functools

import jax
import jax.numpy as jnp
from jax import lax
from jax.experimental import pallas as pl
from jax.experimental.pallas import tpu as pltpu

F32 = jnp.float32
BF16 = jnp.bfloat16
MESH = pl.DeviceIdType.MESH
ANY = pl.BlockSpec(memory_space=pl.ANY)

NORM_EPS = 1e-6
POOL_WINDOWS = (2, 4, 8, 16)
ADAM_LR = 0.001
ADAM_B1 = 0.9
ADAM_B2 = 0.999
ADAM_EPS = 1e-08
ADAM_WD = 0.01
ADAM_STEP = 10

N_CHIPS = 4
N_DEV = 8
LANES = 128
SUBLANES = 8
VMEM_BIG = 56 * 1024 * 1024
HIST = 16
R_CONV = 64
R_POOL = 128

ROW_G_PRE, ROW_G_POST, ROW_MOD, ROW_PSCALE, ROW_WCONV = 0, 1, 2, 5, 6
LOSS_LANES = slice(4 * LANES, 5 * LANES)

NT = (((1,), (1,)), ((), ()))
TN = (((0,), (0,)), ((), ()))


def _params(vmem=None, n_grid=1):
    kw = {}
    if n_grid:
        kw["dimension_semantics"] = ("arbitrary",) * n_grid
    if vmem is not None:
        kw["vmem_limit_bytes"] = vmem
    return pltpu.CompilerParams(**kw)


def _colsum8(v):
    n, d = v.shape
    return v.reshape(n // SUBLANES, SUBLANES, d).sum(axis=0)


def _rms(v):
    return lax.rsqrt(jnp.mean(v * v, axis=-1, keepdims=True) + NORM_EPS)


def _sigmoid(v):
    return 0.5 * jnp.tanh(0.5 * v) + 0.5


def _shift_down(ext, k, rows):
    if k == 0:
        return ext[HIST:HIST + rows]
    return pltpu.roll(ext, k, 0)[HIST:HIST + rows]


def _shift_up(ext, k, rows):
    if k == 0:
        return ext[0:rows]
    return pltpu.roll(ext, ext.shape[0] - k, 0)[0:rows]


def _load_ext(ref, r0, h0, first, rows):
    hist = ref[pl.ds(h0, HIST), :].astype(F32)
    hist = jnp.where(first, 0.0, hist)
    cur = ref[pl.ds(r0, rows), :].astype(F32)
    return jnp.concatenate([hist, cur], axis=0)


def _me():
    return lax.axis_index("x"), lax.axis_index("y"), lax.axis_index("c")


def cast_weights(pos, w_in, w_out, l, after):
    _, D, CW = w_in.shape
    RO = w_out.shape[1]

    def body(pos_ref, wi, wo, after_ref, oi, oo):
        oi[...] = wi[...].astype(BF16)
        oo[...] = wo[...].astype(BF16)

    return pl.pallas_call(
        body, name="cast_w",
        grid_spec=pltpu.PrefetchScalarGridSpec(
            num_scalar_prefetch=1, grid=(2,),
            in_specs=[pl.BlockSpec((None, D // 2, CW), lambda h, p: (l, h, 0)),
                      pl.BlockSpec((None, RO // 2, D), lambda h, p: (l, h, 0)), ANY],
            out_specs=[pl.BlockSpec((D // 2, CW), lambda h, p: (h, p[1])),
                       pl.BlockSpec((None, RO // 2, D), lambda h, p: (p[1], h, 0))]),
        out_shape=[jax.ShapeDtypeStruct((D, N_CHIPS * CW), BF16), jax.ShapeDtypeStruct((N_CHIPS, RO, D), BF16)],
        compiler_params=_params(),
    )(pos, w_in, w_out, after)


def mod_part(pos, c_all, w_ada, b_my, after):
    L, D, CW = w_ada.shape

    def body(pos_ref, c_ref, w_ref, b_ref, after_ref, o_ref):
        cv = c_ref[...]
        ca = (cv * jax.nn.sigmoid(cv)).astype(BF16)
        o_ref[...] = jnp.dot(ca, w_ref[0].astype(BF16), preferred_element_type=F32) + b_ref[0]

    return pl.pallas_call(
        body, name="mod_part",
        grid_spec=pltpu.PrefetchScalarGridSpec(
            num_scalar_prefetch=1, grid=(L,),
            in_specs=[pl.BlockSpec((N_DEV, D), lambda l, p: (0, 0)),
                      pl.BlockSpec((1, D, CW), lambda l, p: (l, 0, 0)),
                      pl.BlockSpec((1, 1, CW), lambda l, p: (l, 0, 0)), ANY],
            out_specs=pl.BlockSpec((None, None, N_DEV, CW), lambda l, p: (p[1], l, 0, 0))),
        out_shape=jax.ShapeDtypeStruct((N_CHIPS, L, N_DEV, CW), F32),
        compiler_params=_params(VMEM_BIG),
    )(pos, c_all, w_ada, b_my.reshape(L, 1, CW), after)


def _mod_row(l, k, D):
    return pl.BlockSpec((None, None, 1, D), lambda *_: (l, k, 0, 0))


def _layer_row(l, D):
    return pl.BlockSpec((None, 1, D), lambda *_: (l, 0, 0))


def proj_fwd(x, mod4, g_pre3, wg, l):
    T, D = x.shape
    NC = wg.shape[1]
    NB = N_CHIPS
    CW = NC // NB
    tm = 512

    def body(x_ref, sh_ref, sc_ref, g_ref, w_ref, o_ref):
        xv = x_ref[...]
        h = (xv * _rms(xv)) * (g_ref[...] * (1.0 + sc_ref[...])) + sh_ref[...]
        hb = h.astype(BF16)
        for j in range(NB):
            cols = slice(j * CW, (j + 1) * CW)
            o_ref[:, cols] = jnp.dot(hb, w_ref[:, cols], preferred_element_type=F32).astype(BF16)

    return pl.pallas_call(
        body, name="proj_fwd", grid=(T // tm,),
        in_specs=[pl.BlockSpec((tm, D), lambda i: (i, 0)), _mod_row(l, 0, D), _mod_row(l, 1, D), _layer_row(l, D),
                  pl.BlockSpec((D, NC), lambda i: (0, 0))],
        out_specs=pl.BlockSpec((tm, NC), lambda i: (i, 0)),
        out_shape=jax.ShapeDtypeStruct((T, NC), BF16),
        compiler_params=_params(VMEM_BIG),
    )(x, mod4, mod4, g_pre3, wg)


N_MIX = 4


def _conv_fwd_block(u_ref, b_ref, c_ref, g_ref, w_ref, o_ref):
    T = u_ref.shape[0]
    R = 2 * R_CONV
    w0 = w_ref[pl.ds(0, 1), :]
    w1 = w_ref[pl.ds(1, 1), :]
    w2 = w_ref[pl.ds(2, 1), :]

    def chunk(i, carry):
        r0 = pl.multiple_of(i * R, R)
        h0 = pl.multiple_of(jnp.maximum(r0 - HIST, 0), HIST)
        first = i == 0
        ca = _load_ext(c_ref, r0, h0, first, R) * _load_ext(u_ref, r0, h0, first, R)
        conv = w2 * ca[HIST:] + w1 * _shift_down(ca, 1, R) + w0 * _shift_down(ca, 2, R)
        g = g_ref[pl.ds(r0, R), :].astype(F32)
        b = b_ref[pl.ds(r0, R), :].astype(F32)
        o_ref[pl.ds(r0, R), :] = (b * conv * (g * _sigmoid(g))).astype(BF16)
        return carry

    lax.fori_loop(0, T // R, chunk, 0)


def _conv_idx(j):
    return jnp.minimum(j, N_MIX - 1)


def _pool_idx(j):
    return jnp.maximum(j - N_MIX, 0)


def _proj_col(T, off, idx):
    return pl.BlockSpec((T, LANES), lambda j: (0, idx(j) + off))


def _causal_window_sum(ext, w):
    s, k = ext, 1
    while k < w:
        s = s + pltpu.roll(s, k, 0)
        k *= 2
    return s


def _anticausal_window_sum(ext, w):
    s, k = ext, 1
    n = ext.shape[0]
    while k < w:
        s = s + pltpu.roll(s, n - k, 0)
        k *= 2
    return s


def _count(r0, rows, w):
    t = r0 + lax.broadcasted_iota(jnp.int32, (rows, LANES), 0)
    return jnp.minimum(t + 1, w).astype(F32)


def _pooled_loop(p_ref, pooled_s, w, T):
    R = R_POOL

    def chunk(i, carry):
        r0 = pl.multiple_of(i * R, R)
        h0 = pl.multiple_of(jnp.maximum(r0 - HIST, 0), HIST)
        ext = _load_ext(p_ref, r0, h0, i == 0, R)
        ws = _causal_window_sum(ext, w)[HIST:]
        pooled_s[pl.ds(r0, R), :] = (ws / _count(r0, R, w) - ext[HIST:]).astype(BF16)
        return carry

    lax.fori_loop(0, T // R, chunk, 0)


def _conv_w_spec(l):
    return pl.BlockSpec((None, None, 3, LANES), lambda j: (_conv_idx(j), l, 0, 0))


def _pool_w_spec(l):
    return pl.BlockSpec((None, None, LANES, LANES), lambda j: (l, _pool_idx(j), 0, 0))


def _pool_s_spec(l):
    return pl.BlockSpec((None, 1, LANES), lambda j: (l, 0, _pool_idx(j)))


def _pool_fwd_group(p_ref, g_ref, w_ref, s_ref, o_ref, pooled_s, mixed_s, w):
    T = p_ref.shape[0]
    R = R_POOL
    _pooled_loop(p_ref, pooled_s, w, T)
    mixed_s[...] = jnp.dot(pooled_s[...], w_ref[...].astype(BF16), preferred_element_type=F32)
    sc = s_ref[...]

    def chunk(i, carry):
        r0 = pl.multiple_of(i * R, R)
        g = g_ref[pl.ds(r0, R), :].astype(F32)
        o_ref[pl.ds(r0, R), :] = (mixed_s[pl.ds(r0, R), :] * sc * (g * _sigmoid(g))).astype(BF16)
        return carry

    lax.fori_loop(0, T // R, chunk, 0)


def mix_fwd(proj, wconv, wpool, pscale3, l):
    T = proj.shape[0]

    def body(u_ref, b_ref, c_ref, g_ref, p_ref, gp_ref, wc_ref, wp_ref, s_ref, ya_ref, yp_ref, pooled_ref, mixed_s):
        j = pl.program_id(0)
        pl.when(j < N_MIX)(functools.partial(_conv_fwd_block, u_ref, b_ref, c_ref, g_ref, wc_ref, ya_ref))
        for k, w in enumerate(POOL_WINDOWS):
            pl.when(j == N_MIX + k)(functools.partial(_pool_fwd_group, p_ref, gp_ref, wp_ref, s_ref, yp_ref,
                                                      pooled_ref, mixed_s, w))

    half = jax.ShapeDtypeStruct((T, N_MIX * LANES), BF16)
    pool_col = pl.BlockSpec((T, LANES), lambda j: (0, _pool_idx(j)))
    return pl.pallas_call(
        body, name="mix_fwd", grid=(2 * N_MIX,),
        in_specs=[_proj_col(T, 0, _conv_idx), _proj_col(T, 4, _conv_idx), _proj_col(T, 8, _conv_idx),
                  _proj_col(T, 12, _conv_idx), _proj_col(T, 16, _pool_idx), _proj_col(T, 20, _pool_idx),
                  _conv_w_spec(l), _pool_w_spec(l), _pool_s_spec(l)],
        out_specs=[pl.BlockSpec((T, LANES), lambda j: (0, _conv_idx(j))), pool_col, pool_col],
        out_shape=[half, half, half],
        scratch_shapes=[pltpu.VMEM((T, LANES), F32)],
        compiler_params=_params(),
    )(proj, proj, proj, proj, proj, proj, wconv, wpool, pscale3)


def out_fwd(ya, yp, wo, x, mod4, g_post3, l, after):
    T, D = x.shape
    H = ya.shape[1]
    tm = 512

    def body(ya_ref, yp_ref, wo_ref, x_ref, gt_ref, g_ref, after_ref, xn_ref, y_ref):
        y = (jnp.dot(ya_ref[...], wo_ref[0:H, :], preferred_element_type=F32)
             + jnp.dot(yp_ref[...], wo_ref[H:2 * H, :], preferred_element_type=F32))
        xn_ref[...] = x_ref[...] + gt_ref[...] * (y * _rms(y) * g_ref[...])
        y_ref[...] = y.astype(BF16)

    tile = pl.BlockSpec((tm, D), lambda i: (i, 0))
    half = pl.BlockSpec((tm, H), lambda i: (i, 0))
    return pl.pallas_call(
        body, name="out_fwd", grid=(T // tm,),
        in_specs=[half, half, pl.BlockSpec((2 * H, D), lambda i: (0, 0)), tile, _mod_row(l, 2, D), _layer_row(l, D),
                  ANY],
        out_specs=[tile, tile],
        out_shape=[jax.ShapeDtypeStruct((T, D), F32), jax.ShapeDtypeStruct((T, D), BF16)],
        compiler_params=_params(VMEM_BIG),
    )(ya, yp, wo, x, mod4, g_post3, after)


def out_fwd_loss(ya, yp, wo, x, mod4, g_post3, l, target):
    T, D = x.shape
    H = ya.shape[1]
    tm = 512
    nt = T // tm

    def body(ya_ref, yp_ref, wo_ref, x_ref, gt_ref, g_ref, t_ref, dx_ref, y_ref, l_ref, acc):
        i = pl.program_id(0)

        @pl.when(i == 0)
        def _():
            acc[...] = jnp.zeros_like(acc)

        y = (jnp.dot(ya_ref[...], wo_ref[0:H, :], preferred_element_type=F32)
             + jnp.dot(yp_ref[...], wo_ref[H:2 * H, :], preferred_element_type=F32))
        y_ref[...] = y.astype(BF16)
        d = (x_ref[...] + gt_ref[...] * (y * _rms(y) * g_ref[...])) - t_ref[...]
        dx_ref[...] = d * (1.0 / D)
        acc[...] += _colsum8(d * d)

        @pl.when(i == nt - 1)
        def _():
            l_ref[...] = jnp.zeros_like(l_ref) + jnp.sum(acc[...]) * (0.5 / D)

    tile = pl.BlockSpec((tm, D), lambda i: (i, 0))
    half = pl.BlockSpec((tm, H), lambda i: (i, 0))
    return pl.pallas_call(
        body, name="out_fwd_loss", grid=(nt,),
        in_specs=[half, half, pl.BlockSpec((2 * H, D), lambda i: (0, 0)), tile, _mod_row(l, 2, D), _layer_row(l, D),
                  tile],
        out_specs=[tile, tile, pl.BlockSpec((SUBLANES, LANES), lambda i: (0, 0))],
        out_shape=[jax.ShapeDtypeStruct((T, D), F32), jax.ShapeDtypeStruct((T, D), BF16),
                   jax.ShapeDtypeStruct((SUBLANES, LANES), F32)],
        scratch_shapes=[pltpu.VMEM((SUBLANES, D), F32)],
        compiler_params=_params(VMEM_BIG),
    )(ya, yp, wo, x, mod4, g_post3, target)


def out_bwd(dx, y, ya, yp, wo, mod4, g_post3, l, after):
    T, D = dx.shape
    H = ya.shape[1]
    tm = 512
    nt = T // tm

    def body(dx_ref, y_ref, ya_ref, yp_ref, wo_ref, gt_ref, g_ref, after_ref,
             dya_ref, dyp_ref, dwo_ref, dgt_ref, dg_ref, acc_w, acc_p):
        i = pl.program_id(0)

        @pl.when(i == 0)
        def _():
            acc_w[...] = jnp.zeros_like(acc_w)
            acc_p[...] = jnp.zeros_like(acc_p)

        yv = y_ref[...].astype(F32)
        dxv = dx_ref[...]
        gg = gt_ref[...] * g_ref[...]
        r = _rms(yv)
        yn = yv * r
        p = dxv * yn
        acc_p[...] += _colsum8(p)
        dy = r * (dxv * gg - yn * jnp.mean(p * gg, axis=-1, keepdims=True))
        dyb = dy.astype(BF16)
        dyc = lax.dot_general(dyb, wo_ref[...], NT, preferred_element_type=F32)
        dya_ref[...] = dyc[:, 0:H].astype(BF16)
        dyp_ref[...] = dyc[:, H:2 * H].astype(BF16)
        acc_w[0:H, :] += lax.dot_general(ya_ref[...], dyb, TN, preferred_element_type=F32)
        acc_w[H:2 * H, :] += lax.dot_general(yp_ref[...], dyb, TN, preferred_element_type=F32)

        @pl.when(i == nt - 1)
        def _():
            dwo_ref[...] = acc_w[...].astype(BF16)
            sp = jnp.sum(acc_p[...], axis=0, keepdims=True)
            dgt_ref[...] = g_ref[...] * sp
            dg_ref[...] = gt_ref[...] * sp

    row = pl.BlockSpec((1, D), lambda i: (0, 0))
    tile = pl.BlockSpec((tm, D), lambda i: (i, 0))
    half = pl.BlockSpec((tm, H), lambda i: (i, 0))
    full = pl.BlockSpec((2 * H, D), lambda i: (0, 0))
    return pl.pallas_call(
        body, name="out_bwd", grid=(nt,),
        in_specs=[tile, tile, half, half, full, _mod_row(l, 2, D), _layer_row(l, D), ANY],
        out_specs=[half, half, full, row, row],
        out_shape=[jax.ShapeDtypeStruct((T, H), BF16), jax.ShapeDtypeStruct((T, H), BF16),
                   jax.ShapeDtypeStruct((2 * H, D), BF16),
                   jax.ShapeDtypeStruct((1, D), F32), jax.ShapeDtypeStruct((1, D), F32)],
        scratch_shapes=[pltpu.VMEM((2 * H, D), F32), pltpu.VMEM((SUBLANES, D), F32)],
        compiler_params=_params(VMEM_BIG),
    )(dx, y, ya, yp, wo, mod4, g_post3, after)


def _conv_bwd_block(u_ref, b_ref, c_ref, g_ref, dy_ref, w_ref, du_ref, db_ref, dc_ref, dg_ref, dw_ref):
    T = u_ref.shape[0]
    R = R_CONV
    nchunk = T // R
    w0 = w_ref[pl.ds(0, 1), :]
    w1 = w_ref[pl.ds(1, 1), :]
    w2 = w_ref[pl.ds(2, 1), :]

    def chunk(k, carry):
        head, a0, a1, a2 = carry
        i = nchunk - 1 - k
        r0 = pl.multiple_of(i * R, R)
        h0 = pl.multiple_of(jnp.maximum(r0 - HIST, 0), HIST)
        first = i == 0
        ue = _load_ext(u_ref, r0, h0, first, R)
        ce = _load_ext(c_ref, r0, h0, first, R)
        ca = ce * ue
        ca0 = ca[HIST:]
        ca1 = _shift_down(ca, 1, R)
        ca2 = _shift_down(ca, 2, R)
        conv = w2 * ca0 + w1 * ca1 + w0 * ca2
        g = g_ref[pl.ds(r0, R), :].astype(F32)
        b = b_ref[pl.ds(r0, R), :].astype(F32)
        dy = dy_ref[pl.ds(r0, R), :].astype(F32)
        sg = _sigmoid(g)
        sl = g * sg
        t = dy * conv
        db_ref[pl.ds(r0, R), :] = (t * sl).astype(BF16)
        dg_ref[pl.ds(r0, R), :] = (t * b * (sg + sl * (1.0 - sg))).astype(BF16)
        dconv = dy * b * sl
        a2 = a2 + _colsum8(dconv * ca0)
        a1 = a1 + _colsum8(dconv * ca1)
        a0 = a0 + _colsum8(dconv * ca2)
        e = jnp.concatenate([dconv, head], axis=0)
        dca = w2 * dconv + w1 * _shift_up(e, 1, R) + w0 * _shift_up(e, 2, R)
        du_ref[pl.ds(r0, R), :] = (dca * ce[HIST:]).astype(BF16)
        dc_ref[pl.ds(r0, R), :] = (dca * ue[HIST:]).astype(BF16)
        return dconv[0:SUBLANES], a0, a1, a2

    z = jnp.zeros((SUBLANES, LANES), F32)
    _, a0, a1, a2 = lax.fori_loop(0, nchunk, chunk, (z, z, z, z))
    dw_ref[pl.ds(0, 1), :] = jnp.sum(a0, axis=0, keepdims=True)
    dw_ref[pl.ds(1, 1), :] = jnp.sum(a1, axis=0, keepdims=True)
    dw_ref[pl.ds(2, 1), :] = jnp.sum(a2, axis=0, keepdims=True)


def _pool_bwd_group(pooled_s, g_ref, dy_ref, w_ref, s_ref, du_ref, dg_ref, dw_ref, ds_ref,
                    mixed_s, dmix_s, dpool_s, w):
    T = pooled_s.shape[0]
    R = R_POOL
    nchunk = T // R
    wb = w_ref[...].astype(BF16)
    mixed_s[...] = jnp.dot(pooled_s[...], wb, preferred_element_type=F32)
    sc = s_ref[...]

    def gate_chunk(i, acc):
        r0 = pl.multiple_of(i * R, R)
        g = g_ref[pl.ds(r0, R), :].astype(F32)
        dy = dy_ref[pl.ds(r0, R), :].astype(F32)
        mixed = mixed_s[pl.ds(r0, R), :]
        sg = _sigmoid(g)
        sl = g * sg
        dg_ref[pl.ds(r0, R), :] = (dy * mixed * sc * (sg + sl * (1.0 - sg))).astype(BF16)
        dms = dy * sl
        dmix_s[pl.ds(r0, R), :] = (dms * sc).astype(BF16)
        return acc + _colsum8(dms * mixed)

    acc = lax.fori_loop(0, nchunk, gate_chunk, jnp.zeros((SUBLANES, LANES), F32))
    ds_ref[...] = jnp.sum(acc, axis=0, keepdims=True)
    dpool_s[pl.ds(0, T), :] = lax.dot_general(dmix_s[...], wb, NT, preferred_element_type=F32)
    dpool_s[pl.ds(T, HIST), :] = jnp.zeros((HIST, LANES), F32)
    dw_ref[...] = lax.dot_general(pooled_s[...], dmix_s[...], TN, preferred_element_type=F32).astype(BF16)

    def back_chunk(i, carry):
        r0 = pl.multiple_of(i * R, R)
        dpe = dpool_s[pl.ds(r0, R + HIST), :]
        e = dpe / _count(r0, R + HIST, w)
        du_ref[pl.ds(r0, R), :] = (_anticausal_window_sum(e, w)[0:R] - dpe[0:R]).astype(BF16)
        return carry

    lax.fori_loop(0, nchunk, back_chunk, 0)


def mix_bwd(proj, pooled, dya, dyp, wconv, wpool, pscale3, l, after):
    T = proj.shape[0]

    def body(u_ref, b_ref, c_ref, g_ref, pooled_ref, gp_ref, dya_ref, dyp_ref, wc_ref, wp_ref, s_ref, after_ref,
             dua_ref, dba_ref, dca_ref, dga_ref, dup_ref, dgp_ref, dwc_ref, dwp_ref, ds_ref,
             mixed_s, dmix_s, dpool_s):
        j = pl.program_id(0)
        pl.when(j < N_MIX)(functools.partial(_conv_bwd_block, u_ref, b_ref, c_ref, g_ref, dya_ref, wc_ref,
                                             dua_ref, dba_ref, dca_ref, dga_ref, dwc_ref))
        for k, w in enumerate(POOL_WINDOWS):
            pl.when(j == N_MIX + k)(functools.partial(_pool_bwd_group, pooled_ref, gp_ref, dyp_ref, wp_ref, s_ref,
                                                      dup_ref, dgp_ref, dwp_ref, ds_ref,
                                                      mixed_s, dmix_s, dpool_s, w))

    sec = jax.ShapeDtypeStruct((T, N_MIX * LANES), BF16)
    conv_col = pl.BlockSpec((T, LANES), lambda j: (0, _conv_idx(j)))
    pool_col = pl.BlockSpec((T, LANES), lambda j: (0, _pool_idx(j)))
    return pl.pallas_call(
        body, name="mix_bwd", grid=(2 * N_MIX,),
        in_specs=[_proj_col(T, 0, _conv_idx), _proj_col(T, 4, _conv_idx), _proj_col(T, 8, _conv_idx),
                  _proj_col(T, 12, _conv_idx), pool_col, _proj_col(T, 20, _pool_idx),
                  conv_col, pool_col, _conv_w_spec(l), _pool_w_spec(l), _pool_s_spec(l), ANY],
        out_specs=[conv_col, conv_col, conv_col, conv_col, pool_col, pool_col,
                   pl.BlockSpec((None, 3, LANES), lambda j: (_conv_idx(j), 0, 0)),
                   pl.BlockSpec((None, LANES, LANES), lambda j: (_pool_idx(j), 0, 0)),
                   pl.BlockSpec((1, LANES), lambda j: (0, _pool_idx(j)))],
        out_shape=[sec] * 6 + [jax.ShapeDtypeStruct((N_MIX, 3, LANES), F32),
                               jax.ShapeDtypeStruct((N_MIX, LANES, LANES), BF16),
                               jax.ShapeDtypeStruct((1, N_MIX * LANES), F32)],
        scratch_shapes=[pltpu.VMEM((T, LANES), F32), pltpu.VMEM((T, LANES), BF16), pltpu.VMEM((T + HIST, LANES), F32)],
        compiler_params=_params(),
    )(proj, proj, proj, proj, pooled, proj, dya, dyp, wconv, wpool, pscale3, after)


def in_bwd(dsecs, wg, x, dxo, mod4, g_pre3, l, update=None):
    T, D = x.shape
    NB = N_CHIPS
    CW = wg.shape[1] // NB
    SW = dsecs[0].shape[1]
    nsec = len(dsecs)
    PW = 256
    assert SW % PW == 0 and CW % PW == 0
    tm = 256
    nt = T // tm
    n_in = nsec + 6
    n_upd = 0 if update is None else 8
    n_acc = 0 if update is None or update[3] is None else 8

    def body(*refs):
        d_refs = refs[0:nsec]
        w_ref, x_ref, dxo_ref, sh_ref, sc_ref, g_ref = refs[nsec:n_in]
        outs = refs[n_in + n_upd + n_acc:]
        dxi_ref, dw_ref, dsh_ref, dsc_ref, dg_ref = outs[0:5]
        acc_w, acc_sh, acc_q = outs[5 + n_upd:]
        i = pl.program_id(0)
        for k in range(0, n_upd, 4):
            _adamw_block(refs[n_in + k:n_in + k + 4], outs[5 + k:5 + k + 4])

        @pl.when(i == 0)
        def _():
            acc_w[...] = jnp.zeros_like(acc_w)
            acc_sh[...] = jnp.zeros_like(acc_sh)
            acc_q[...] = jnp.zeros_like(acc_q)

        xv = x_ref[...]
        r = _rms(xv)
        xh = xv * r
        sg = g_ref[...] * (1.0 + sc_ref[...])
        hb = (xh * sg + sh_ref[...]).astype(BF16)
        dh = lax.dot_general(d_refs[0][...], w_ref[:, 0:SW], NT, preferred_element_type=F32)
        for s in range(1, nsec):
            dh = dh + lax.dot_general(d_refs[s][...], w_ref[:, s * SW:(s + 1) * SW], NT, preferred_element_type=F32)
        for p in range(nsec * SW // PW):
            col = p * PW
            s, so = col // SW, col % SW
            j, jo = col // CW, col % CW
            acc_w[j, :, jo:jo + PW] += lax.dot_general(hb, d_refs[s][:, so:so + PW], TN, preferred_element_type=F32)
        q = dh * xh
        acc_sh[...] += _colsum8(dh)
        acc_q[...] += _colsum8(q)
        dxi_ref[...] = dxo_ref[...] + r * (dh * sg - xh * jnp.mean(q * sg, axis=-1, keepdims=True))

        @pl.when(i == nt - 1)
        def _():
            dw_ref[...] = acc_w[...].astype(BF16)
            sq = jnp.sum(acc_q[...], axis=0, keepdims=True)
            dsh_ref[...] = jnp.sum(acc_sh[...], axis=0, keepdims=True)
            dsc_ref[...] = g_ref[...] * sq
            dg_ref[...] = (1.0 + sc_ref[...]) * sq

    row = pl.BlockSpec((1, D), lambda i: (0, 0))
    tile = pl.BlockSpec((tm, D), lambda i: (i, 0))
    sect = pl.BlockSpec((tm, SW), lambda i: (i, 0))
    rowshape = jax.ShapeDtypeStruct((1, D), F32)
    in_specs = [sect] * nsec + [pl.BlockSpec((D, NB * CW), lambda i: (0, 0)), tile, tile,
                                _mod_row(l, 0, D), _mod_row(l, 1, D), _layer_row(l, D)]
    out_specs = [tile, pl.BlockSpec((NB, D, CW), lambda i: (0, 0, 0)), row, row, row]
    out_shape = [jax.ShapeDtypeStruct((T, D), F32), jax.ShapeDtypeStruct((NB, D, CW), BF16), rowshape, rowshape, rowshape]
    args = [*dsecs, wg, x, dxo, mod4, mod4, g_pre3]
    aliases = {}
    if update is not None:
        layer, of_w_in, of_w_out, acc = update
        for group in (of_w_in, of_w_out):
            _, rows, cols = group[0].shape
            spec = pl.BlockSpec((None, rows // nt, cols), lambda i: (layer, i, 0))
            in_specs += [spec] * 4
            out_specs += [spec] * 4
            out_shape += [jax.ShapeDtypeStruct(group[0].shape, F32)] * 4
            args += list(group)
        if acc is not None:
            aliases = {len(args) + a: 5 + a for a in range(n_acc)}
            in_specs += [ANY] * n_acc
            args += list(acc)
    return pl.pallas_call(
        body, name="in_bwd", grid=(nt,),
        in_specs=in_specs, out_specs=out_specs, out_shape=out_shape, input_output_aliases=aliases,
        scratch_shapes=[pltpu.VMEM((NB, D, CW), F32),
                        pltpu.VMEM((SUBLANES, D), F32), pltpu.VMEM((SUBLANES, D), F32)],
        compiler_params=_params(VMEM_BIG),
    )(*args)


def _rcopy(src, dst, ssem, rsem, dev):
    return pltpu.make_async_remote_copy(src_ref=src, dst_ref=dst, send_sem=ssem, recv_sem=rsem,
                                        device_id=dev, device_id_type=MESH)


def _peers7(x, y, c):
    out = []
    for m in range(1, N_DEV):
        bx, by, bc = (m >> 2) & 1, (m >> 1) & 1, m & 1
        out.append(((1 - x) if bx else x, (1 - y) if by else y, (1 - c) if bc else c))
    return out


HBM = pl.BlockSpec(memory_space=pltpu.HBM)
SEM = pl.BlockSpec(memory_space=pltpu.SEMAPHORE)
SPLIT = pltpu.CompilerParams(has_side_effects=pltpu.SideEffectType.DATAFLOW_SIDE_EFFECTING)


def _hbm(a):
    return pltpu.with_memory_space_constraint(a, pltpu.HBM)


def _chips(x, y):
    return [(1 - x, y), (x, 1 - y), (1 - x, 1 - y)]


SIBLING_BARRIER_ID = 0


def xchg_start(name, bufs, n_copies, plan, sibling_only=False, after=()):
    n = len(bufs)
    after = list(after)

    def body(*refs):
        ssem, rsem, token = refs[n + len(after)], refs[n + len(after) + 1], refs[-1]
        x, y, c = _me()
        if sibling_only:
            barrier = pltpu.get_barrier_semaphore()
            pl.semaphore_signal(barrier, inc=1, device_id=(x, y, 1 - c), device_id_type=MESH)
            pl.semaphore_wait(barrier, 1)
        copies = plan(refs[0:n], x, y, c)
        assert len(copies) == n_copies
        for k, (src, dst, peer, _) in enumerate(copies):
            _rcopy(src, dst, ssem.at[k], rsem.at[k], peer).start()
        token[...] = jnp.zeros_like(token)

    params = dict(has_side_effects=pltpu.SideEffectType.DATAFLOW_SIDE_EFFECTING)
    if sibling_only:
        params["collective_id"] = SIBLING_BARRIER_ID
    outs = pl.pallas_call(
        body, name=name,
        in_specs=[HBM] * n + [ANY] * len(after),
        out_specs=[SEM, SEM] + [HBM] * n + [pl.BlockSpec(memory_space=pltpu.VMEM)],
        out_shape=([pltpu.SemaphoreType.DMA((n_copies,))] * 2 + [pltpu.HBM(b.shape, b.dtype) for b in bufs]
                   + [jax.ShapeDtypeStruct((SUBLANES, LANES), F32)]),
        input_output_aliases={a: 2 + a for a in range(n)},
        compiler_params=pltpu.CompilerParams(**params),
    )(*[_hbm(b) for b in bufs], *after)
    return outs[0], outs[1], list(outs[2:2 + n]), outs[-1]


def xchg_wait(name, bufs, ssem, rsem, n_copies, plan, after, sems=None):
    n = len(bufs)
    after = list(after)
    sems = tuple(range(n_copies)) if sems is None else tuple(sems)
    assert len(sems) == n_copies

    def body(*refs):
        ssem_ref, rsem_ref = refs[n], refs[n + 1]
        copies = plan(refs[0:n], *_me())
        assert len(copies) == n_copies
        for k, (src, _, peer, land) in zip(sems, copies):
            cp = _rcopy(src, land, ssem_ref.at[k], rsem_ref.at[k], peer)
            cp.wait_send()
            cp.wait_recv()

    outs = pl.pallas_call(
        body, name=name,
        in_specs=[HBM] * n + [SEM, SEM] + [ANY] * len(after), out_specs=[HBM] * n,
        out_shape=[pltpu.HBM(b.shape, b.dtype) for b in bufs],
        input_output_aliases={a: a for a in range(n)},
        compiler_params=SPLIT,
    )(*bufs, ssem, rsem, *after)
    return list(outs)


def _shard_half(buf, chip, half):
    if len(buf.shape) == 2:
        h, w = buf.shape[0] // 2, buf.shape[1] // N_CHIPS
        return buf.at[pl.ds(half * h, h), pl.ds(chip * w, w)]
    h = buf.shape[1] // 2
    return buf.at[chip, pl.ds(half * h, h)]


def plan_gather(refs, x, y, c):
    out = []
    for (px, py) in _chips(x, y):
        for buf in refs:
            own = _shard_half(buf, 2 * x + y, c)
            out.append((own, own, (px, py, c), _shard_half(buf, 2 * px + py, c)))
    return out


def plan_forward(refs, x, y, c):
    out = []
    for (px, py) in _chips(x, y):
        for buf in refs:
            landed = _shard_half(buf, 2 * px + py, c)
            out.append((landed, landed, (x, y, 1 - c), _shard_half(buf, 2 * px + py, 1 - c)))
    return out


def plan_sibling(refs, x, y, c):
    n = len(refs) // 2
    out = []
    for a in range(n):
        h = refs[a].shape[1] // 2
        out.append((refs[a].at[:, pl.ds((1 - c) * h, h)], refs[n + a], (x, y, 1 - c), refs[n + a]))
    return out


def plan_chip(refs, x, y, c):
    n = len(refs) // 2
    out = []
    for j, (px, py) in enumerate(_chips(x, y)):
        for a in range(n):
            out.append((refs[a].at[2 * px + py], refs[n + a].at[j], (px, py, c), refs[n + a].at[j]))
    return out


def plan_mod(refs, x, y, c):
    (mods,) = refs
    mine = mods.at[2 * x + y]
    return [(mine, mine, (px, py, c), mods.at[2 * px + py]) for (px, py) in _chips(x, y)]


def plan_pack(refs, x, y, c):
    (packs,) = refs
    mine = packs.at[4 * x + 2 * y + c]
    return [(mine, mine, peer, packs.at[4 * peer[0] + 2 * peer[1] + peer[2]]) for peer in _peers7(x, y, c)]


def plan_spread(layers, wp_layers):
    def plan(refs, x, y, c):
        gi, go, gp = refs
        hD, hR, hP = gi.shape[1] // 2, go.shape[1] // 2, gp.shape[2] // 2
        sib = (x, y, 1 - c)
        out = []
        for l in layers:
            mine = gi.at[l, pl.ds(c * hD, hD)]
            out.append((mine, mine, sib, gi.at[l, pl.ds((1 - c) * hD, hD)]))
            mine = go.at[l, pl.ds(c * hR, hR)]
            out.append((mine, mine, sib, go.at[l, pl.ds((1 - c) * hR, hR)]))
        for l in wp_layers:
            mine = gp.at[l, 2 * x + y, pl.ds(c * hP, hP)]
            for peer in _peers7(x, y, c):
                out.append((mine, mine, peer, gp.at[l, 2 * peer[0] + peer[1], pl.ds(peer[2] * hP, hP)]))
        return out

    return plan


def place_small(pos, c8, wc):
    L = wc.shape[0]

    def body(pos_ref, c_ref, wc_ref, call_ref, wcall_ref):
        call_ref[...] = c_ref[...]
        wcall_ref[...] = wc_ref[...]

    return pl.pallas_call(
        body, name="place_small",
        grid_spec=pltpu.PrefetchScalarGridSpec(
            num_scalar_prefetch=1, grid=(1,),
            in_specs=[pl.BlockSpec((SUBLANES, LANES), lambda i, p: (0, 0)),
                      pl.BlockSpec((L, 3, LANES), lambda i, p: (0, 0, 0))],
            out_specs=[pl.BlockSpec((None, SUBLANES, LANES), lambda i, p: (p[2], 0, 0)),
                       pl.BlockSpec((None, L, 3, LANES), lambda i, p: (p[1], 0, 0, 0))]),
        out_shape=[jax.ShapeDtypeStruct((N_DEV, SUBLANES, LANES), F32),
                   jax.ShapeDtypeStruct((N_CHIPS, L, 3, LANES), F32)],
        compiler_params=_params(),
    )(pos, c8, wc)


def plan_small(refs, x, y, c):
    call, wcall = refs
    mine = call.at[4 * x + 2 * y + c]
    out = [(mine, mine, peer, call.at[4 * peer[0] + 2 * peer[1] + peer[2]]) for peer in _peers7(x, y, c)]
    mine = wcall.at[2 * x + y]
    out += [(mine, mine, (px, py, c), wcall.at[2 * px + py]) for (px, py) in _chips(x, y)]
    return out


def add_sibling(cidx, mine, sib):
    def body(c_ref, *refs):
        for a in range(3):
            m, s, o = refs[a], refs[3 + a], refs[6 + a]
            o[...] = (m[...].astype(F32) + s[...].astype(F32)).astype(BF16)

    per_step = 2

    def mine_spec(a):
        h = a.shape[1] // 2
        return pl.BlockSpec((per_step, h, a.shape[2]), lambda j, c_ref: (j, c_ref[0], 0))

    def sib_spec(a):
        return pl.BlockSpec((per_step,) + a.shape[1:], lambda j, c_ref: (j, 0, 0))

    return pl.pallas_call(
        body, name="add_sibling",
        grid_spec=pltpu.PrefetchScalarGridSpec(
            num_scalar_prefetch=1, grid=(N_CHIPS // per_step,),
            in_specs=[mine_spec(a) for a in mine] + [sib_spec(a) for a in sib],
            out_specs=[sib_spec(a) for a in sib]),
        out_shape=[jax.ShapeDtypeStruct(a.shape, BF16) for a in sib],
        compiler_params=_params(VMEM_BIG),
    )(cidx, *mine, *sib)


def sum_chips(pos, own, rb, acc, l, shapes):
    nq = 2
    n_in = 6 + (3 if acc is not None else 0)

    def body(pos_ref, *refs):
        for a in range(3):
            m, b, o = refs[a], refs[3 + a], refs[n_in + a]
            s = m[...].astype(F32)
            for j in range(3):
                s = s + b[j].astype(F32)
            o[...] = s

    def own_spec(a):
        return pl.BlockSpec((None, a.shape[1] // nq, a.shape[2]), lambda q, p: (p[1], q, 0))

    def rb_spec(a):
        return pl.BlockSpec((3, a.shape[1] // nq, a.shape[2]), lambda q, p: (0, q, 0))

    hi, ho, hp = own[0].shape[1] // nq, own[1].shape[1] // nq, own[2].shape[1] // nq
    out_specs = [pl.BlockSpec((None, hi, shapes[0][2]), lambda q, p: (l, p[0] * nq + q, 0)),
                 pl.BlockSpec((None, ho, shapes[1][2]), lambda q, p: (l, p[0] * nq + q, 0)),
                 pl.BlockSpec((None, None, hp, LANES), lambda q, p: (l, p[1], p[0] * nq + q, 0))]
    in_specs = [own_spec(a) for a in own] + [rb_spec(a) for a in rb]
    args = list(own) + list(rb)
    aliases = {}
    if acc is not None:
        in_specs += [ANY] * 3
        args += list(acc)
        aliases = {7: 0, 8: 1, 9: 2}
    return pl.pallas_call(
        body, name="sum_chips",
        grid_spec=pltpu.PrefetchScalarGridSpec(num_scalar_prefetch=1, grid=(nq,), in_specs=in_specs, out_specs=out_specs),
        out_shape=[jax.ShapeDtypeStruct(s, F32) for s in shapes],
        input_output_aliases=aliases,
        compiler_params=_params(VMEM_BIG),
    )(pos, *args)


def _wconv_slot(chip, tap):
    idx = 3 * chip + tap
    return ROW_WCONV + idx // SUBLANES, slice((idx % SUBLANES) * LANES, (idx % SUBLANES + 1) * LANES)


def pack_small(pos, per_layer, loss_blk):
    L = len(per_layer)
    D = per_layer[0][0].shape[1]

    def body(pos_ref, *refs):
        o = refs[-1]
        lb = refs[-2]
        o[...] = jnp.zeros_like(o)
        for l in range(L):
            dgpre, dgpost, dsh, dsc, dgt, dps, dwc = refs[7 * l:7 * l + 7]
            base = SUBLANES * l
            o[pl.ds(base + ROW_G_PRE, 1), :] = dgpre[...]
            o[pl.ds(base + ROW_G_POST, 1), :] = dgpost[...]
            for r, src in enumerate((dsh, dsc, dgt)):
                o[pl.ds(base + ROW_MOD + r, 1), :] = src[...]
            o[pl.ds(base + ROW_PSCALE, 1), 0:dps.shape[1]] = dps[...]
            for j in range(dwc.shape[0]):
                for k in range(3):
                    row, lanes = _wconv_slot(j, k)
                    o[pl.ds(base + row, 1), lanes] = dwc[j, pl.ds(k, 1), :]
        o[pl.ds(ROW_PSCALE, 1), LOSS_LANES] = lb[pl.ds(0, 1), :]

    flat = [a for layer in per_layer for a in layer] + [loss_blk]

    def whole(a):
        return pl.BlockSpec(a.shape, lambda i, p: (0,) * a.ndim)

    return pl.pallas_call(
        body, name="pack_small",
        grid_spec=pltpu.PrefetchScalarGridSpec(
            num_scalar_prefetch=1, grid=(1,), in_specs=[whole(a) for a in flat],
            out_specs=pl.BlockSpec((None, L * SUBLANES, D), lambda i, p: (p[2], 0, 0))),
        out_shape=jax.ShapeDtypeStruct((N_DEV, L * SUBLANES, D), F32),
        compiler_params=_params(),
    )(pos, *flat)


def small_update(pos, packs, params, moments_m, moments_v):
    n = len(params)
    L, D = params[1].shape
    PS = params[3].shape[1]

    def body(pos_ref, p_ref, *refs):
        ws, ms, vs = refs[0:n], refs[n:2 * n], refs[2 * n:3 * n]
        loss_ref = refs[3 * n]
        outs = [refs[3 * n + 1 + 4 * t:3 * n + 5 + 4 * t] for t in range(n)]
        summed = refs[-1]
        s = p_ref[0]
        for d in range(1, N_DEV):
            s = s + p_ref[d]
        summed[...] = s
        loss_ref[...] = summed[pl.ds(ROW_PSCALE, 1), LOSS_LANES]
        chip = pos_ref[1]

        def update(t, idx, g):
            d, mm, vv = _adamw_math(ws[t][idx], g, ms[t][idx], vs[t][idx])
            g_ref, d_ref, mo_ref, vo_ref = outs[t]
            g_ref[idx] = g
            d_ref[idx] = d
            mo_ref[idx] = mm
            vo_ref[idx] = vv

        for l in range(L):
            base = SUBLANES * l
            row = pl.ds(l, 1)
            for k in range(3):
                update(0, (row, slice(k * D, (k + 1) * D)), summed[pl.ds(base + ROW_MOD + k, 1), :])
            update(1, (row, slice(None)), summed[pl.ds(base + ROW_G_PRE, 1), :])
            update(2, (row, slice(None)), summed[pl.ds(base + ROW_G_POST, 1), :])
            update(3, (row, slice(None)), summed[pl.ds(base + ROW_PSCALE, 1), 0:PS])
            for k in range(3):
                g = None
                for j in range(N_CHIPS):
                    wrow, lanes = _wconv_slot(j, k)
                    cand = summed[pl.ds(base + wrow, 1), lanes]
                    g = cand if g is None else jnp.where(chip == j, cand, g)
                update(4, (l, pl.ds(k, 1), slice(None)), g)

    def whole(a):
        return pl.BlockSpec(a.shape, lambda i, p: (0,) * a.ndim)

    ins = [packs] + list(params) + list(moments_m) + list(moments_v)
    out_shape = [jax.ShapeDtypeStruct((1, LANES), F32)]
    for w in params:
        out_shape += [jax.ShapeDtypeStruct(w.shape, F32)] * 4
    outs = pl.pallas_call(
        body, name="small_update",
        grid_spec=pltpu.PrefetchScalarGridSpec(
            num_scalar_prefetch=1, grid=(1,), in_specs=[whole(a) for a in ins],
            out_specs=[whole(a) for a in out_shape],
            scratch_shapes=[pltpu.VMEM(packs.shape[1:], F32)]),
        out_shape=out_shape,
        compiler_params=_params(),
    )(pos, *ins)
    return outs[0], [outs[1 + 4 * t:5 + 4 * t] for t in range(n)]


def _adamw_math(w, g, m, v):
    m = ADAM_B1 * m + (1.0 - ADAM_B1) * g
    v = ADAM_B2 * v + (1.0 - ADAM_B2) * (g * g)
    m_hat = m / (1.0 - ADAM_B1 ** ADAM_STEP)
    v_hat = v / (1.0 - ADAM_B2 ** ADAM_STEP)
    delta = -ADAM_LR * (m_hat / (jnp.sqrt(v_hat) + ADAM_EPS) + ADAM_WD * w)
    return delta, m, v


def _adamw_block(ins, outs):
    w_ref, g_ref, m_ref, v_ref = ins
    go_ref, d_ref, mo_ref, vo_ref = outs
    gv = g_ref[...]
    d, mm, vv = _adamw_math(w_ref[...], gv, m_ref[...], v_ref[...])
    go_ref[...] = gv
    d_ref[...] = d
    mo_ref[...] = mm
    vo_ref[...] = vv


def adamw(w, g, m, v, block, name, first=0, count=None, acc=None):
    grid = tuple(s // b for s, b in zip(w.shape, block))
    if count is not None:
        grid = (count,) + grid[1:]

    def body(*refs):
        _adamw_block(refs[0:4], refs[-4:])

    spec = pl.BlockSpec(block, lambda i, *rest: (first + i,) + rest)
    shape = jax.ShapeDtypeStruct(w.shape, F32)
    extra = [] if acc is None else list(acc)
    return pl.pallas_call(
        body, name=name, grid=grid,
        in_specs=[spec] * 4 + [ANY] * len(extra), out_specs=[spec] * 4, out_shape=[shape] * 4,
        input_output_aliases={4 + a: a for a in range(len(extra))},
        compiler_params=_params(VMEM_BIG, n_grid=len(grid)),
    )(w, g, m, v, *extra)


def ada_finish(c_all, dmod, w, m, v):
    L, D, CW = w.shape
    hD = D // 2

    def body(c_ref, d_ref, w_ref, m_ref, v_ref, g_ref, dl_ref, mo_ref, vo_ref):
        cv = c_ref[...]
        z = jnp.zeros_like(cv)
        ca = jnp.concatenate([cv * jax.nn.sigmoid(cv), z], axis=0).astype(BF16)
        dm = jnp.concatenate([d_ref[0], jnp.zeros_like(d_ref[0])], axis=0).astype(BF16)
        g = lax.dot_general(ca, dm, TN, preferred_element_type=F32)
        g_ref[0] = g
        d, mm, vv = _adamw_math(w_ref[0], g, m_ref[0], v_ref[0])
        dl_ref[0] = d
        mo_ref[0] = mm
        vo_ref[0] = vv

    big = pl.BlockSpec((1, hD, CW), lambda l, h: (l, h, 0))
    shape = jax.ShapeDtypeStruct(w.shape, F32)
    return pl.pallas_call(
        body, name="ada_finish", grid=(L, 2),
        in_specs=[pl.BlockSpec((N_DEV, hD), lambda l, h: (0, h)), pl.BlockSpec((1, N_DEV, CW), lambda l, h: (l, 0, 0)),
                  big, big, big],
        out_specs=[big] * 4, out_shape=[shape] * 4,
        compiler_params=_params(VMEM_BIG, n_grid=2),
    )(c_all, dmod, w, m, v)


def kernel(x, c, w_ada, b_ada, g_pre, w_in, w_conv, w_pool, pool_scale, w_out, g_post, loss_target, m_w_ada, m_b_ada, m_g_pre, m_w_in, m_w_conv, m_w_pool, m_pool_scale, m_w_out, m_g_post, v_w_ada, v_b_ada, v_g_pre, v_w_in, v_w_conv, v_w_pool, v_pool_scale, v_w_out, v_g_post):
    L, D, CW = w_in.shape
    RO = w_out.shape[1]
    T = x.shape[1]
    ix, iy, ic = _me()
    chip = 2 * ix + iy
    me_lin = 4 * ix + 2 * iy + ic

    pos = jnp.stack([ic, chip, me_lin]).astype(jnp.int32)
    g_pre3, g_post3 = g_pre.reshape(L, 1, D), g_post.reshape(L, 1, D)
    pscale3 = pool_scale.reshape(L, 1, pool_scale.shape[1])
    n_s, n_c = 3, 9

    def gather(bufs, after):
        ss, rs, bufs, tok = xchg_start("gather_start", bufs, 3 * len(bufs), plan_gather, after=after)
        return (ss, rs, bufs), tok

    def ready(flight, after):
        fss, frs, bufs = flight
        return xchg_wait("forward_wait", bufs, fss, frs, 3 * len(bufs), plan_forward, after)

    def arrive_part(flight, which, after):
        ss, rs, bufs = flight
        sems = tuple(range(which, 3 * len(bufs), len(bufs)))
        (buf,) = xchg_wait("gather_wait", [bufs[which]], ss, rs, 3, plan_gather, after, sems=sems)
        fss, frs, (buf,), tok = xchg_start("forward_start", [buf], 3, plan_forward, sibling_only=True)
        return (fss, frs, [buf]), tok

    n_small = N_DEV - 1 + N_CHIPS - 1
    s_ss, s_rs, smalls_in, token = xchg_start("small_start", list(place_small(pos, c.reshape(SUBLANES, LANES), w_conv)),
                                              n_small, plan_small)
    w_in_of, w_out_of = [None] * L, [None] * L
    gi0, go0 = cast_weights(pos, w_in, w_out, 0, token)
    flight, token = gather([gi0], [])
    w_in_of[0] = (flight, 0)
    c_all3, wconv_all = xchg_wait("small_wait", smalls_in, s_ss, s_rs, n_small, plan_small, [token])
    c_all = c_all3.reshape(N_DEV, D)
    b_my = lax.dynamic_slice_in_dim(b_ada, chip * CW, CW, axis=1)
    m_ss, m_rs, mods, token = xchg_start("mod_start", [mod_part(pos, c_all, w_ada, b_my, token)], 3, plan_mod)
    gi1, go1 = cast_weights(pos, w_in, w_out, 1, token)
    flight, token = gather([gi1], [])
    w_in_of[1] = (flight, 0)
    flight, token = gather([go0, go1], [token])
    w_out_of[0], w_out_of[1] = (flight, 0), (flight, 1)
    for l in range(2, L):
        flight, token = gather(list(cast_weights(pos, w_in, w_out, l, token)), [])
        w_in_of[l], w_out_of[l] = (flight, 0), (flight, 1)
    fwd_in, token = arrive_part(*w_in_of[0], [token])
    (mod_all,) = xchg_wait("mod_wait", mods, m_ss, m_rs, 3, plan_mod, [token])
    mod = lax.dynamic_index_in_dim(mod_all, me_lin, axis=2, keepdims=False)
    mod4 = jnp.transpose(mod, (1, 0, 2)).reshape(L, 3, 1, D)

    xs, projs, yas, yps, ys, pooleds = [x.reshape(T, D)], [], [], [], [], []
    wg_in, wg_out = [], []
    for l in range(L):
        (gi,) = ready(fwd_in, [mod4 if l == 0 else xs[l]])
        proj = proj_fwd(xs[l], mod4, g_pre3, gi, l)
        ya, yp, pooled = mix_fwd(proj, wconv_all, w_pool, pscale3, l)
        pooleds.append(pooled)
        fwd_out, token = arrive_part(*w_out_of[l], [ya, yp])
        after = [token]
        if l + 1 < L:
            fwd_in, token = arrive_part(*w_in_of[l + 1], after)
            after = [token]
        (go,) = ready(fwd_out, after)
        wg_in.append(gi)
        wg_out.append(go.reshape(N_CHIPS * RO, D))
        projs.append(proj)
        yas.append(ya)
        yps.append(yp)
        if l + 1 < L:
            xn, yv = out_fwd(ya, yp, wg_out[l], xs[l], mod4, g_post3, l, after[0])
            xs.append(xn)
        else:
            dx, yv, loss_blk = out_fwd_loss(ya, yp, wg_out[l], xs[l], mod4, g_post3, l, loss_target.reshape(T, D))
        ys.append(yv)

    shapes = (w_in.shape, w_out.shape, w_pool.shape)
    smalls = [None] * L
    acc, flying, sib, token = None, None, None, loss_blk

    def to_chips(sib, after):
        sl, s_ss, s_rs, s_bufs = sib
        s_bufs = xchg_wait("sibling_wait", s_bufs, s_ss, s_rs, n_s, plan_sibling, after)
        chip_parts = add_sibling(pos, s_bufs[0:3], s_bufs[3:6])
        lands = [lax.empty((3,) + a.shape[1:], a.dtype) for a in chip_parts]
        c_ss, c_rs, c_bufs, ctoken = xchg_start("chip_start", list(chip_parts) + lands, n_c, plan_chip)
        return (sl, c_ss, c_rs, c_bufs), ctoken

    def landed(flying, acc, after):
        fl, f_ss, f_rs, f_bufs = flying
        f_bufs = xchg_wait("chip_wait", f_bufs, f_ss, f_rs, n_c, plan_chip, after)
        return sum_chips(pos, f_bufs[0:3], f_bufs[3:6], acc, fl, shapes)

    early = None
    for l in reversed(range(L)):
        dya, dyp, dwo_l, dgate, dgpost = out_bwd(dx, ys[l], yas[l], yps[l], wg_out[l], mod4, g_post3, l, token)
        token = dya
        spreading = None
        if sib is not None:
            arrived = flying
            flying, token = to_chips(sib, [dya])
            if arrived is not None:
                acc = landed(arrived, acc, [token])
                spreading = plan_spread((arrived[0],), ())
                sp_ss, sp_rs, acc, token = xchg_start("spread_start", list(acc), 2, spreading, sibling_only=True)
        du_a, db_a, dc_a, dg_a, du_p, dg_p, dwc, dwp_l, dps = mix_bwd(projs[l], pooleds[l], dya, dyp, wconv_all, w_pool,
                                                                        pscale3, l, token)
        update = None
        if spreading is not None:
            acc = xchg_wait("spread_wait", acc, sp_ss, sp_rs, 2, spreading, [du_a])
            update = (arrived[0], [w_in, acc[0], m_w_in, v_w_in], [w_out, acc[1], m_w_out, v_w_out], early)
        dx, dwi_l, dshift, dscale, dgpre, *rest = in_bwd([du_a, db_a, dc_a, dg_a, du_p, dg_p], wg_in[l], xs[l], dx,
                                                         mod4, g_pre3, l, update)
        early = rest if rest else early
        smalls[l] = (dgpre, dgpost, dshift, dscale, dgate, dps, dwc)
        parts = [dwi_l, dwo_l.reshape(N_CHIPS, RO, D), dwp_l]
        s_lands = [lax.empty((a.shape[0], a.shape[1] // 2) + a.shape[2:], a.dtype) for a in parts]
        s_ss, s_rs, s_bufs, token = xchg_start("sibling_start", parts + s_lands, n_s, plan_sibling, sibling_only=True)
        sib = (l, s_ss, s_rs, s_bufs)
    grad_x = dx.reshape(1, T, D)

    p_ss, p_rs, packs, ptoken = xchg_start("pack_start", [pack_small(pos, smalls, loss_blk)], N_DEV - 1, plan_pack)
    acc = landed(flying, acc, [ptoken, token])
    n_sp = 2 + (N_DEV - 1) * (L - 1)
    spread = plan_spread((1,), tuple(range(1, L)))
    sp_ss, sp_rs, acc, sp_token = xchg_start("spread_start", list(acc), n_sp, spread)
    flying, token = to_chips(sib, [sp_token])
    (packs_all,) = xchg_wait("pack_wait", packs, p_ss, p_rs, N_DEV - 1, plan_pack, [token])
    dmod_all = packs_all.reshape(N_DEV, L, SUBLANES, D)[:, :, ROW_MOD:ROW_MOD + 3].reshape(N_DEV, L, 3 * D)
    dmod_my = jnp.transpose(lax.dynamic_slice_in_dim(dmod_all, chip * CW, CW, axis=2), (1, 0, 2))

    g_w_ada, d_w_ada, nm_w_ada, nv_w_ada = ada_finish(c_all, dmod_my, w_ada, m_w_ada, v_w_ada)
    loss_row, upd = small_update(pos, packs_all, [b_ada, g_pre, g_post, pool_scale, w_conv],
                                 [m_b_ada, m_g_pre, m_g_post, m_pool_scale, m_w_conv],
                                 [v_b_ada, v_g_pre, v_g_post, v_pool_scale, v_w_conv])
    loss = loss_row[0, 0]
    (g_b_ada, d_b_ada, nm_b_ada, nv_b_ada), (g_g_pre, d_g_pre, nm_g_pre, nv_g_pre) = upd[0], upd[1]
    (g_g_post, d_g_post, nm_g_post, nv_g_post), (g_pscale, d_pscale, nm_pscale, nv_pscale) = upd[2], upd[3]
    g_w_conv, d_w_conv, nm_w_conv, nv_w_conv = upd[4]

    done = [nv_w_ada, nv_w_conv]
    g_w_in, g_w_out, g_w_pool = xchg_wait("spread_wait", acc, sp_ss, sp_rs, n_sp, spread, done)
    in_blk, out_blk = (1, D // 2, CW), (1, RO, D)
    upd_in = adamw(w_in, g_w_in, m_w_in, v_w_in, in_blk, "adamw_w_in", 1, 1, early[0:4])
    upd_out = adamw(w_out, g_w_out, m_w_out, v_w_out, out_blk, "adamw_w_out", 1, 1, early[4:8])

    acc = landed(flying, (g_w_in, g_w_out, g_w_pool), [upd_in[3], upd_out[3]])
    last = plan_spread((0,), (0,))
    n_last = 2 + N_DEV - 1
    l_ss, l_rs, acc, _ = xchg_start("spread_start", list(acc), n_last, last)
    pshape = (L, N_CHIPS * LANES, LANES)
    pool_blk = (1,) + pshape[1:]
    wp, mp, vp = w_pool.reshape(pshape), m_w_pool.reshape(pshape), v_w_pool.reshape(pshape)
    upd_pool = adamw(wp, acc[2].reshape(pshape), mp, vp, pool_blk, "adamw_w_pool", 1, L - 1)
    r_w_in, r_w_out, r_w_pool = xchg_wait("spread_wait", acc, l_ss, l_rs, n_last, last, [upd_pool[3]])
    g_w_in, d_w_in, nm_w_in, nv_w_in = adamw(w_in, r_w_in, m_w_in, v_w_in, in_blk, "adamw_w_in", 0, 1, upd_in)
    g_w_out, d_w_out, nm_w_out, nv_w_out = adamw(w_out, r_w_out, m_w_out, v_w_out, out_blk, "adamw_w_out", 0, 1, upd_out)
    upd_pool = adamw(wp, r_w_pool.reshape(pshape), mp, vp, pool_blk, "adamw_w_pool", 0, 1, upd_pool)
    g_w_pool, d_w_pool, nm_w_pool, nv_w_pool = [a.reshape(w_pool.shape) for a in upd_pool]

    return (loss, grad_x,
            g_w_ada, g_b_ada, g_g_pre, g_w_in, g_w_conv, g_w_pool, g_pscale, g_w_out, g_g_post,
            d_w_ada, d_b_ada, d_g_pre, d_w_in, d_w_conv, d_w_pool, d_pscale, d_w_out, d_g_post,
            nm_w_ada, nm_b_ada, nm_g_pre, nm_w_in, nm_w_conv, nm_w_pool, nm_pscale, nm_w_out, nm_g_post,
            nv_w_ada, nv_b_ada, nv_g_pre, nv_w_in, nv_w_conv, nv_w_pool, nv_pscale, nv_w_out, nv_g_post)
```

```python
import functools

import jax
import jax.numpy as jnp
from jax import lax
from jax.experimental import pallas as pl
from jax.experimental.pallas import tpu as pltpu

F32 = jnp.float32
BF16 = jnp.bfloat16
MESH = pl.DeviceIdType.MESH
ANY = pl.BlockSpec(memory_space=pl.ANY)

NORM_EPS = 1e-6
POOL_WINDOWS = (2, 4, 8, 16)
ADAM_LR = 0.001
ADAM_B1 = 0.9
ADAM_B2 = 0.999
ADAM_EPS = 1e-08
ADAM_WD = 0.01
ADAM_STEP = 10

N_CHIPS = 4
N_DEV = 8
LANES = 128
SUBLANES = 8
VMEM_BIG = 56 * 1024 * 1024
HIST = 16
R_CONV = 64
R_POOL = 128

ROW_G_PRE, ROW_G_POST, ROW_MOD, ROW_PSCALE, ROW_WCONV = 0, 1, 2, 5, 6
LOSS_LANES = slice(4 * LANES, 5 * LANES)

NT = (((1,), (1,)), ((), ()))
TN = (((0,), (0,)), ((), ()))


def _params(vmem=None, n_grid=1):
    kw = {}
    if n_grid:
        kw["dimension_semantics"] = ("arbitrary",) * n_grid
    if vmem is not None:
        kw["vmem_limit_bytes"] = vmem
    return pltpu.CompilerParams(**kw)


def _colsum8(v):
    n, d = v.shape
    return v.reshape(n // SUBLANES, SUBLANES, d).sum(axis=0)


def _rms(v):
    return lax.rsqrt(jnp.mean(v * v, axis=-1, keepdims=True) + NORM_EPS)


def _sigmoid(v):
    return 0.5 * jnp.tanh(0.5 * v) + 0.5


def _shift_down(ext, k, rows):
    if k == 0:
        return ext[HIST:HIST + rows]
    return pltpu.roll(ext, k, 0)[HIST:HIST + rows]


def _shift_up(ext, k, rows):
    if k == 0:
        return ext[0:rows]
    return pltpu.roll(ext, ext.shape[0] - k, 0)[0:rows]


def _load_ext(ref, r0, h0, first, rows):
    hist = ref[pl.ds(h0, HIST), :].astype(F32)
    hist = jnp.where(first, 0.0, hist)
    cur = ref[pl.ds(r0, rows), :].astype(F32)
    return jnp.concatenate([hist, cur], axis=0)


def _me():
    return lax.axis_index("x"), lax.axis_index("y"), lax.axis_index("c")


def cast_weights(pos, w_in, w_out, l, after):
    _, D, CW = w_in.shape
    RO = w_out.shape[1]

    def body(pos_ref, wi, wo, after_ref, oi, oo):
        oi[...] = wi[...].astype(BF16)
        oo[...] = wo[...].astype(BF16)

    return pl.pallas_call(
        body, name="cast_w",
        grid_spec=pltpu.PrefetchScalarGridSpec(
            num_scalar_prefetch=1, grid=(2,),
            in_specs=[pl.BlockSpec((None, D // 2, CW), lambda h, p: (l, h, 0)),
                      pl.BlockSpec((None, RO // 2, D), lambda h, p: (l, h, 0)), ANY],
            out_specs=[pl.BlockSpec((D // 2, CW), lambda h, p: (h, p[1])),
                       pl.BlockSpec((None, RO // 2, D), lambda h, p: (p[1], h, 0))]),
        out_shape=[jax.ShapeDtypeStruct((D, N_CHIPS * CW), BF16), jax.ShapeDtypeStruct((N_CHIPS, RO, D), BF16)],
        compiler_params=_params(),
    )(pos, w_in, w_out, after)


def mod_part(pos, c_all, w_ada, b_my, after):
    L, D, CW = w_ada.shape

    def body(pos_ref, c_ref, w_ref, b_ref, after_ref, o_ref):
        cv = c_ref[...]
        ca = (cv * jax.nn.sigmoid(cv)).astype(BF16)
        o_ref[...] = jnp.dot(ca, w_ref[0].astype(BF16), preferred_element_type=F32) + b_ref[0]

    return pl.pallas_call(
        body, name="mod_part",
        grid_spec=pltpu.PrefetchScalarGridSpec(
            num_scalar_prefetch=1, grid=(L,),
            in_specs=[pl.BlockSpec((N_DEV, D), lambda l, p: (0, 0)),
                      pl.BlockSpec((1, D, CW), lambda l, p: (l, 0, 0)),
                      pl.BlockSpec((1, 1, CW), lambda l, p: (l, 0, 0)), ANY],
            out_specs=pl.BlockSpec((None, None, N_DEV, CW), lambda l, p: (p[1], l, 0, 0))),
        out_shape=jax.ShapeDtypeStruct((N_CHIPS, L, N_DEV, CW), F32),
        compiler_params=_params(VMEM_BIG),
    )(pos, c_all, w_ada, b_my.reshape(L, 1, CW), after)


def _mod_row(l, k, D):
    return pl.BlockSpec((None, None, 1, D), lambda *_: (l, k, 0, 0))


def _layer_row(l, D):
    return pl.BlockSpec((None, 1, D), lambda *_: (l, 0, 0))


def proj_fwd(x, mod4, g_pre3, wg, l):
    T, D = x.shape
    NC = wg.shape[1]
    NB = N_CHIPS
    CW = NC // NB
    tm = 512

    def body(x_ref, sh_ref, sc_ref, g_ref, w_ref, o_ref):
        xv = x_ref[...]
        h = (xv * _rms(xv)) * (g_ref[...] * (1.0 + sc_ref[...])) + sh_ref[...]
        hb = h.astype(BF16)
        for j in range(NB):
            cols = slice(j * CW, (j + 1) * CW)
            o_ref[:, cols] = jnp.dot(hb, w_ref[:, cols], preferred_element_type=F32).astype(BF16)

    return pl.pallas_call(
        body, name="proj_fwd", grid=(T // tm,),
        in_specs=[pl.BlockSpec((tm, D), lambda i: (i, 0)), _mod_row(l, 0, D), _mod_row(l, 1, D), _layer_row(l, D),
                  pl.BlockSpec((D, NC), lambda i: (0, 0))],
        out_specs=pl.BlockSpec((tm, NC), lambda i: (i, 0)),
        out_shape=jax.ShapeDtypeStruct((T, NC), BF16),
        compiler_params=_params(VMEM_BIG),
    )(x, mod4, mod4, g_pre3, wg)


N_MIX = 4


def _conv_fwd_block(u_ref, b_ref, c_ref, g_ref, w_ref, o_ref):
    T = u_ref.shape[0]
    R = 2 * R_CONV
    w0 = w_ref[pl.ds(0, 1), :]
    w1 = w_ref[pl.ds(1, 1), :]
    w2 = w_ref[pl.ds(2, 1), :]

    def chunk(i, carry):
        r0 = pl.multiple_of(i * R, R)
        h0 = pl.multiple_of(jnp.maximum(r0 - HIST, 0), HIST)
        first = i == 0
        ca = _load_ext(c_ref, r0, h0, first, R) * _load_ext(u_ref, r0, h0, first, R)
        conv = w2 * ca[HIST:] + w1 * _shift_down(ca, 1, R) + w0 * _shift_down(ca, 2, R)
        g = g_ref[pl.ds(r0, R), :].astype(F32)
        b = b_ref[pl.ds(r0, R), :].astype(F32)
        o_ref[pl.ds(r0, R), :] = (b * conv * (g * _sigmoid(g))).astype(BF16)
        return carry

    lax.fori_loop(0, T // R, chunk, 0)


def _conv_idx(j):
    return jnp.minimum(j, N_MIX - 1)


def _pool_idx(j):
    return jnp.maximum(j - N_MIX, 0)


def _proj_col(T, off, idx):
    return pl.BlockSpec((T, LANES), lambda j: (0, idx(j) + off))


def _causal_window_sum(ext, w):
    s, k = ext, 1
    while k < w:
        s = s + pltpu.roll(s, k, 0)
        k *= 2
    return s


def _anticausal_window_sum(ext, w):
    s, k = ext, 1
    n = ext.shape[0]
    while k < w:
        s = s + pltpu.roll(s, n - k, 0)
        k *= 2
    return s


def _count(r0, rows, w):
    t = r0 + lax.broadcasted_iota(jnp.int32, (rows, LANES), 0)
    return jnp.minimum(t + 1, w).astype(F32)


def _pooled_loop(p_ref, pooled_s, w, T):
    R = R_POOL

    def chunk(i, carry):
        r0 = pl.multiple_of(i * R, R)
        h0 = pl.multiple_of(jnp.maximum(r0 - HIST, 0), HIST)
        ext = _load_ext(p_ref, r0, h0, i == 0, R)
        ws = _causal_window_sum(ext, w)[HIST:]
        pooled_s[pl.ds(r0, R), :] = (ws / _count(r0, R, w) - ext[HIST:]).astype(BF16)
        return carry

    lax.fori_loop(0, T // R, chunk, 0)


def _conv_w_spec(l):
    return pl.BlockSpec((None, None, 3, LANES), lambda j: (_conv_idx(j), l, 0, 0))


def _pool_w_spec(l):
    return pl.BlockSpec((None, None, LANES, LANES), lambda j: (l, _pool_idx(j), 0, 0))


def _pool_s_spec(l):
    return pl.BlockSpec((None, 1, LANES), lambda j: (l, 0, _pool_idx(j)))


def _pool_fwd_group(p_ref, g_ref, w_ref, s_ref, o_ref, pooled_s, mixed_s, w):
    T = p_ref.shape[0]
    R = R_POOL
    _pooled_loop(p_ref, pooled_s, w, T)
    mixed_s[...] = jnp.dot(pooled_s[...], w_ref[...].astype(BF16), preferred_element_type=F32)
    sc = s_ref[...]

    def chunk(i, carry):
        r0 = pl.multiple_of(i * R, R)
        g = g_ref[pl.ds(r0, R), :].astype(F32)
        o_ref[pl.ds(r0, R), :] = (mixed_s[pl.ds(r0, R), :] * sc * (g * _sigmoid(g))).astype(BF16)
        return carry

    lax.fori_loop(0, T // R, chunk, 0)


def mix_fwd(proj, wconv, wpool, pscale3, l):
    T = proj.shape[0]

    def body(u_ref, b_ref, c_ref, g_ref, p_ref, gp_ref, wc_ref, wp_ref, s_ref, ya_ref, yp_ref, pooled_ref, mixed_s):
        j = pl.program_id(0)
        pl.when(j < N_MIX)(functools.partial(_conv_fwd_block, u_ref, b_ref, c_ref, g_ref, wc_ref, ya_ref))
        for k, w in enumerate(POOL_WINDOWS):
            pl.when(j == N_MIX + k)(functools.partial(_pool_fwd_group, p_ref, gp_ref, wp_ref, s_ref, yp_ref,
                                                      pooled_ref, mixed_s, w))

    half = jax.ShapeDtypeStruct((T, N_MIX * LANES), BF16)
    pool_col = pl.BlockSpec((T, LANES), lambda j: (0, _pool_idx(j)))
    return pl.pallas_call(
        body, name="mix_fwd", grid=(2 * N_MIX,),
        in_specs=[_proj_col(T, 0, _conv_idx), _proj_col(T, 4, _conv_idx), _proj_col(T, 8, _conv_idx),
                  _proj_col(T, 12, _conv_idx), _proj_col(T, 16, _pool_idx), _proj_col(T, 20, _pool_idx),
                  _conv_w_spec(l), _pool_w_spec(l), _pool_s_spec(l)],
        out_specs=[pl.BlockSpec((T, LANES), lambda j: (0, _conv_idx(j))), pool_col, pool_col],
        out_shape=[half, half, half],
        scratch_shapes=[pltpu.VMEM((T, LANES), F32)],
        compiler_params=_params(),
    )(proj, proj, proj, proj, proj, proj, wconv, wpool, pscale3)


def out_fwd(ya, yp, wo, x, mod4, g_post3, l, after):
    T, D = x.shape
    H = ya.shape[1]
    tm = 512

    def body(ya_ref, yp_ref, wo_ref, x_ref, gt_ref, g_ref, after_ref, xn_ref, y_ref):
        y = (jnp.dot(ya_ref[...], wo_ref[0:H, :], preferred_element_type=F32)
             + jnp.dot(yp_ref[...], wo_ref[H:2 * H, :], preferred_element_type=F32))
        xn_ref[...] = x_ref[...] + gt_ref[...] * (y * _rms(y) * g_ref[...])
        y_ref[...] = y.astype(BF16)

    tile = pl.BlockSpec((tm, D), lambda i: (i, 0))
    half = pl.BlockSpec((tm, H), lambda i: (i, 0))
    return pl.pallas_call(
        body, name="out_fwd", grid=(T // tm,),
        in_specs=[half, half, pl.BlockSpec((2 * H, D), lambda i: (0, 0)), tile, _mod_row(l, 2, D), _layer_row(l, D),
                  ANY],
        out_specs=[tile, tile],
        out_shape=[jax.ShapeDtypeStruct((T, D), F32), jax.ShapeDtypeStruct((T, D), BF16)],
        compiler_params=_params(VMEM_BIG),
    )(ya, yp, wo, x, mod4, g_post3, after)


def out_fwd_loss(ya, yp, wo, x, mod4, g_post3, l, target):
    T, D = x.shape
    H = ya.shape[1]
    tm = 512
    nt = T // tm

    def body(ya_ref, yp_ref, wo_ref, x_ref, gt_ref, g_ref, t_ref, dx_ref, y_ref, l_ref, acc):
        i = pl.program_id(0)

        @pl.when(i == 0)
        def _():
            acc[...] = jnp.zeros_like(acc)

        y = (jnp.dot(ya_ref[...], wo_ref[0:H, :], preferred_element_type=F32)
             + jnp.dot(yp_ref[...], wo_ref[H:2 * H, :], preferred_element_type=F32))
        y_ref[...] = y.astype(BF16)
        d = (x_ref[...] + gt_ref[...] * (y * _rms(y) * g_ref[...])) - t_ref[...]
        dx_ref[...] = d * (1.0 / D)
        acc[...] += _colsum8(d * d)

        @pl.when(i == nt - 1)
        def _():
            l_ref[...] = jnp.zeros_like(l_ref) + jnp.sum(acc[...]) * (0.5 / D)

    tile = pl.BlockSpec((tm, D), lambda i: (i, 0))
    half = pl.BlockSpec((tm, H), lambda i: (i, 0))
    return pl.pallas_call(
        body, name="out_fwd_loss", grid=(nt,),
        in_specs=[half, half, pl.BlockSpec((2 * H, D), lambda i: (0, 0)), tile, _mod_row(l, 2, D), _layer_row(l, D),
                  tile],
        out_specs=[tile, tile, pl.BlockSpec((SUBLANES, LANES), lambda i: (0, 0))],
        out_shape=[jax.ShapeDtypeStruct((T, D), F32), jax.ShapeDtypeStruct((T, D), BF16),
                   jax.ShapeDtypeStruct((SUBLANES, LANES), F32)],
        scratch_shapes=[pltpu.VMEM((SUBLANES, D), F32)],
        compiler_params=_params(VMEM_BIG),
    )(ya, yp, wo, x, mod4, g_post3, target)


def out_bwd(dx, y, ya, yp, wo, mod4, g_post3, l, after):
    T, D = dx.shape
    H = ya.shape[1]
    tm = 512
    nt = T // tm

    def body(dx_ref, y_ref, ya_ref, yp_ref, wo_ref, gt_ref, g_ref, after_ref,
             dya_ref, dyp_ref, dwo_ref, dgt_ref, dg_ref, acc_w, acc_p):
        i = pl.program_id(0)

        @pl.when(i == 0)
        def _():
            acc_w[...] = jnp.zeros_like(acc_w)
            acc_p[...] = jnp.zeros_like(acc_p)

        yv = y_ref[...].astype(F32)
        dxv = dx_ref[...]
        gg = gt_ref[...] * g_ref[...]
        r = _rms(yv)
        yn = yv * r
        p = dxv * yn
        acc_p[...] += _colsum8(p)
        dy = r * (dxv * gg - yn * jnp.mean(p * gg, axis=-1, keepdims=True))
        dyb = dy.astype(BF16)
        dyc = lax.dot_general(dyb, wo_ref[...], NT, preferred_element_type=F32)
        dya_ref[...] = dyc[:, 0:H].astype(BF16)
        dyp_ref[...] = dyc[:, H:2 * H].astype(BF16)
        acc_w[0:H, :] += lax.dot_general(ya_ref[...], dyb, TN, preferred_element_type=F32)
        acc_w[H:2 * H, :] += lax.dot_general(yp_ref[...], dyb, TN, preferred_element_type=F32)

        @pl.when(i == nt - 1)
        def _():
            dwo_ref[...] = acc_w[...].astype(BF16)
            sp = jnp.sum(acc_p[...], axis=0, keepdims=True)
            dgt_ref[...] = g_ref[...] * sp
            dg_ref[...] = gt_ref[...] * sp

    row = pl.BlockSpec((1, D), lambda i: (0, 0))
    tile = pl.BlockSpec((tm, D), lambda i: (i, 0))
    half = pl.BlockSpec((tm, H), lambda i: (i, 0))
    full = pl.BlockSpec((2 * H, D), lambda i: (0, 0))
    return pl.pallas_call(
        body, name="out_bwd", grid=(nt,),
        in_specs=[tile, tile, half, half, full, _mod_row(l, 2, D), _layer_row(l, D), ANY],
        out_specs=[half, half, full, row, row],
        out_shape=[jax.ShapeDtypeStruct((T, H), BF16), jax.ShapeDtypeStruct((T, H), BF16),
                   jax.ShapeDtypeStruct((2 * H, D), BF16),
                   jax.ShapeDtypeStruct((1, D), F32), jax.ShapeDtypeStruct((1, D), F32)],
        scratch_shapes=[pltpu.VMEM((2 * H, D), F32), pltpu.VMEM((SUBLANES, D), F32)],
        compiler_params=_params(VMEM_BIG),
    )(dx, y, ya, yp, wo, mod4, g_post3, after)


def _conv_bwd_block(u_ref, b_ref, c_ref, g_ref, dy_ref, w_ref, du_ref, db_ref, dc_ref, dg_ref, dw_ref):
    T = u_ref.shape[0]
    R = R_CONV
    nchunk = T // R
    w0 = w_ref[pl.ds(0, 1), :]
    w1 = w_ref[pl.ds(1, 1), :]
    w2 = w_ref[pl.ds(2, 1), :]

    def chunk(k, carry):
        head, a0, a1, a2 = carry
        i = nchunk - 1 - k
        r0 = pl.multiple_of(i * R, R)
        h0 = pl.multiple_of(jnp.maximum(r0 - HIST, 0), HIST)
        first = i == 0
        ue = _load_ext(u_ref, r0, h0, first, R)
        ce = _load_ext(c_ref, r0, h0, first, R)
        ca = ce * ue
        ca0 = ca[HIST:]
        ca1 = _shift_down(ca, 1, R)
        ca2 = _shift_down(ca, 2, R)
        conv = w2 * ca0 + w1 * ca1 + w0 * ca2
        g = g_ref[pl.ds(r0, R), :].astype(F32)
        b = b_ref[pl.ds(r0, R), :].astype(F32)
        dy = dy_ref[pl.ds(r0, R), :].astype(F32)
        sg = _sigmoid(g)
        sl = g * sg
        t = dy * conv
        db_ref[pl.ds(r0, R), :] = (t * sl).astype(BF16)
        dg_ref[pl.ds(r0, R), :] = (t * b * (sg + sl * (1.0 - sg))).astype(BF16)
        dconv = dy * b * sl
        a2 = a2 + _colsum8(dconv * ca0)
        a1 = a1 + _colsum8(dconv * ca1)
        a0 = a0 + _colsum8(dconv * ca2)
        e = jnp.concatenate([dconv, head], axis=0)
        dca = w2 * dconv + w1 * _shift_up(e, 1, R) + w0 * _shift_up(e, 2, R)
        du_ref[pl.ds(r0, R), :] = (dca * ce[HIST:]).astype(BF16)
        dc_ref[pl.ds(r0, R), :] = (dca * ue[HIST:]).astype(BF16)
        return dconv[0:SUBLANES], a0, a1, a2

    z = jnp.zeros((SUBLANES, LANES), F32)
    _, a0, a1, a2 = lax.fori_loop(0, nchunk, chunk, (z, z, z, z))
    dw_ref[pl.ds(0, 1), :] = jnp.sum(a0, axis=0, keepdims=True)
    dw_ref[pl.ds(1, 1), :] = jnp.sum(a1, axis=0, keepdims=True)
    dw_ref[pl.ds(2, 1), :] = jnp.sum(a2, axis=0, keepdims=True)


def _pool_bwd_group(pooled_s, g_ref, dy_ref, w_ref, s_ref, du_ref, dg_ref, dw_ref, ds_ref,
                    mixed_s, dmix_s, dpool_s, w):
    T = pooled_s.shape[0]
    R = R_POOL
    nchunk = T // R
    wb = w_ref[...].astype(BF16)
    mixed_s[...] = jnp.dot(pooled_s[...], wb, preferred_element_type=F32)
    sc = s_ref[...]

    def gate_chunk(i, acc):
        r0 = pl.multiple_of(i * R, R)
        g = g_ref[pl.ds(r0, R), :].astype(F32)
        dy = dy_ref[pl.ds(r0, R), :].astype(F32)
        mixed = mixed_s[pl.ds(r0, R), :]
        sg = _sigmoid(g)
        sl = g * sg
        dg_ref[pl.ds(r0, R), :] = (dy * mixed * sc * (sg + sl * (1.0 - sg))).astype(BF16)
        dms = dy * sl
        dmix_s[pl.ds(r0, R), :] = (dms * sc).astype(BF16)
        return acc + _colsum8(dms * mixed)

    acc = lax.fori_loop(0, nchunk, gate_chunk, jnp.zeros((SUBLANES, LANES), F32))
    ds_ref[...] = jnp.sum(acc, axis=0, keepdims=True)
    dpool_s[pl.ds(0, T), :] = lax.dot_general(dmix_s[...], wb, NT, preferred_element_type=F32)
    dpool_s[pl.ds(T, HIST), :] = jnp.zeros((HIST, LANES), F32)
    dw_ref[...] = lax.dot_general(pooled_s[...], dmix_s[...], TN, preferred_element_type=F32).astype(BF16)

    def back_chunk(i, carry):
        r0 = pl.multiple_of(i * R, R)
        dpe = dpool_s[pl.ds(r0, R + HIST), :]
        e = dpe / _count(r0, R + HIST, w)
        du_ref[pl.ds(r0, R), :] = (_anticausal_window_sum(e, w)[0:R] - dpe[0:R]).astype(BF16)
        return carry

    lax.fori_loop(0, nchunk, back_chunk, 0)


def mix_bwd(proj, pooled, dya, dyp, wconv, wpool, pscale3, l, after):
    T = proj.shape[0]

    def body(u_ref, b_ref, c_ref, g_ref, pooled_ref, gp_ref, dya_ref, dyp_ref, wc_ref, wp_ref, s_ref, after_ref,
             dua_ref, dba_ref, dca_ref, dga_ref, dup_ref, dgp_ref, dwc_ref, dwp_ref, ds_ref,
             mixed_s, dmix_s, dpool_s):
        j = pl.program_id(0)
        pl.when(j < N_MIX)(functools.partial(_conv_bwd_block, u_ref, b_ref, c_ref, g_ref, dya_ref, wc_ref,
                                             dua_ref, dba_ref, dca_ref, dga_ref, dwc_ref))
        for k, w in enumerate(POOL_WINDOWS):
            pl.when(j == N_MIX + k)(functools.partial(_pool_bwd_group, pooled_ref, gp_ref, dyp_ref, wp_ref, s_ref,
                                                      dup_ref, dgp_ref, dwp_ref, ds_ref,
                                                      mixed_s, dmix_s, dpool_s, w))

    sec = jax.ShapeDtypeStruct((T, N_MIX * LANES), BF16)
    conv_col = pl.BlockSpec((T, LANES), lambda j: (0, _conv_idx(j)))
    pool_col = pl.BlockSpec((T, LANES), lambda j: (0, _pool_idx(j)))
    return pl.pallas_call(
        body, name="mix_bwd", grid=(2 * N_MIX,),
        in_specs=[_proj_col(T, 0, _conv_idx), _proj_col(T, 4, _conv_idx), _proj_col(T, 8, _conv_idx),
                  _proj_col(T, 12, _conv_idx), pool_col, _proj_col(T, 20, _pool_idx),
                  conv_col, pool_col, _conv_w_spec(l), _pool_w_spec(l), _pool_s_spec(l), ANY],
        out_specs=[conv_col, conv_col, conv_col, conv_col, pool_col, pool_col,
                   pl.BlockSpec((None, 3, LANES), lambda j: (_conv_idx(j), 0, 0)),
                   pl.BlockSpec((None, LANES, LANES), lambda j: (_pool_idx(j), 0, 0)),
                   pl.BlockSpec((1, LANES), lambda j: (0, _pool_idx(j)))],
        out_shape=[sec] * 6 + [jax.ShapeDtypeStruct((N_MIX, 3, LANES), F32),
                               jax.ShapeDtypeStruct((N_MIX, LANES, LANES), BF16),
                               jax.ShapeDtypeStruct((1, N_MIX * LANES), F32)],
        scratch_shapes=[pltpu.VMEM((T, LANES), F32), pltpu.VMEM((T, LANES), BF16), pltpu.VMEM((T + HIST, LANES), F32)],
        compiler_params=_params(),
    )(proj, proj, proj, proj, pooled, proj, dya, dyp, wconv, wpool, pscale3, after)


def in_bwd(dsecs, wg, x, dxo, mod4, g_pre3, l, update=None):
    T, D = x.shape
    NB = N_CHIPS
    CW = wg.shape[1] // NB
    SW = dsecs[0].shape[1]
    nsec = len(dsecs)
    PW = 256
    assert SW % PW == 0 and CW % PW == 0
    tm = 256
    nt = T // tm
    n_in = nsec + 6
    n_upd = 0 if update is None else 8
    n_acc = 0 if update is None or update[3] is None else 8

    def body(*refs):
        d_refs = refs[0:nsec]
        w_ref, x_ref, dxo_ref, sh_ref, sc_ref, g_ref = refs[nsec:n_in]
        outs = refs[n_in + n_upd + n_acc:]
        dxi_ref, dw_ref, dsh_ref, dsc_ref, dg_ref = outs[0:5]
        acc_w, acc_sh, acc_q = outs[5 + n_upd:]
        i = pl.program_id(0)
        for k in range(0, n_upd, 4):
            _adamw_block(refs[n_in + k:n_in + k + 4], outs[5 + k:5 + k + 4])

        @pl.when(i == 0)
        def _():
            acc_w[...] = jnp.zeros_like(acc_w)
            acc_sh[...] = jnp.zeros_like(acc_sh)
            acc_q[...] = jnp.zeros_like(acc_q)

        xv = x_ref[...]
        r = _rms(xv)
        xh = xv * r
        sg = g_ref[...] * (1.0 + sc_ref[...])
        hb = (xh * sg + sh_ref[...]).astype(BF16)
        dh = lax.dot_general(d_refs[0][...], w_ref[:, 0:SW], NT, preferred_element_type=F32)
        for s in range(1, nsec):
            dh = dh + lax.dot_general(d_refs[s][...], w_ref[:, s * SW:(s + 1) * SW], NT, preferred_element_type=F32)
        for p in range(nsec * SW // PW):
            col = p * PW
            s, so = col // SW, col % SW
            j, jo = col // CW, col % CW
            acc_w[j, :, jo:jo + PW] += lax.dot_general(hb, d_refs[s][:, so:so + PW], TN, preferred_element_type=F32)
        q = dh * xh
        acc_sh[...] += _colsum8(dh)
        acc_q[...] += _colsum8(q)
        dxi_ref[...] = dxo_ref[...] + r * (dh * sg - xh * jnp.mean(q * sg, axis=-1, keepdims=True))

        @pl.when(i == nt - 1)
        def _():
            dw_ref[...] = acc_w[...].astype(BF16)
            sq = jnp.sum(acc_q[...], axis=0, keepdims=True)
            dsh_ref[...] = jnp.sum(acc_sh[...], axis=0, keepdims=True)
            dsc_ref[...] = g_ref[...] * sq
            dg_ref[...] = (1.0 + sc_ref[...]) * sq

    row = pl.BlockSpec((1, D), lambda i: (0, 0))
    tile = pl.BlockSpec((tm, D), lambda i: (i, 0))
    sect = pl.BlockSpec((tm, SW), lambda i: (i, 0))
    rowshape = jax.ShapeDtypeStruct((1, D), F32)
    in_specs = [sect] * nsec + [pl.BlockSpec((D, NB * CW), lambda i: (0, 0)), tile, tile,
                                _mod_row(l, 0, D), _mod_row(l, 1, D), _layer_row(l, D)]
    out_specs = [tile, pl.BlockSpec((NB, D, CW), lambda i: (0, 0, 0)), row, row, row]
    out_shape = [jax.ShapeDtypeStruct((T, D), F32), jax.ShapeDtypeStruct((NB, D, CW), BF16), rowshape, rowshape, rowshape]
    args = [*dsecs, wg, x, dxo, mod4, mod4, g_pre3]
    aliases = {}
    if update is not None:
        layer, of_w_in, of_w_out, acc = update
        for group in (of_w_in, of_w_out):
            _, rows, cols = group[0].shape
            spec = pl.BlockSpec((None, rows // nt, cols), lambda i: (layer, i, 0))
            in_specs += [spec] * 4
            out_specs += [spec] * 4
            out_shape += [jax.ShapeDtypeStruct(group[0].shape, F32)] * 4
            args += list(group)
        if acc is not None:
            aliases = {len(args) + a: 5 + a for a in range(n_acc)}
            in_specs += [ANY] * n_acc
            args += list(acc)
    return pl.pallas_call(
        body, name="in_bwd", grid=(nt,),
        in_specs=in_specs, out_specs=out_specs, out_shape=out_shape, input_output_aliases=aliases,
        scratch_shapes=[pltpu.VMEM((NB, D, CW), F32),
                        pltpu.VMEM((SUBLANES, D), F32), pltpu.VMEM((SUBLANES, D), F32)],
        compiler_params=_params(VMEM_BIG),
    )(*args)


def _rcopy(src, dst, ssem, rsem, dev):
    return pltpu.make_async_remote_copy(src_ref=src, dst_ref=dst, send_sem=ssem, recv_sem=rsem,
                                        device_id=dev, device_id_type=MESH)


def _peers7(x, y, c):
    out = []
    for m in range(1, N_DEV):
        bx, by, bc = (m >> 2) & 1, (m >> 1) & 1, m & 1
        out.append(((1 - x) if bx else x, (1 - y) if by else y, (1 - c) if bc else c))
    return out


HBM = pl.BlockSpec(memory_space=pltpu.HBM)
SEM = pl.BlockSpec(memory_space=pltpu.SEMAPHORE)
SPLIT = pltpu.CompilerParams(has_side_effects=pltpu.SideEffectType.DATAFLOW_SIDE_EFFECTING)


def _hbm(a):
    return pltpu.with_memory_space_constraint(a, pltpu.HBM)


def _chips(x, y):
    return [(1 - x, y), (x, 1 - y), (1 - x, 1 - y)]


SIBLING_BARRIER_ID = 0


def xchg_start(name, bufs, n_copies, plan, sibling_only=False, after=()):
    n = len(bufs)
    after = list(after)

    def body(*refs):
        ssem, rsem, token = refs[n + len(after)], refs[n + len(after) + 1], refs[-1]
        x, y, c = _me()
        if sibling_only:
            barrier = pltpu.get_barrier_semaphore()
            pl.semaphore_signal(barrier, inc=1, device_id=(x, y, 1 - c), device_id_type=MESH)
            pl.semaphore_wait(barrier, 1)
        copies = plan(refs[0:n], x, y, c)
        assert len(copies) == n_copies
        for k, (src, dst, peer, _) in enumerate(copies):
            _rcopy(src, dst, ssem.at[k], rsem.at[k], peer).start()
        token[...] = jnp.zeros_like(token)

    params = dict(has_side_effects=pltpu.SideEffectType.DATAFLOW_SIDE_EFFECTING)
    if sibling_only:
        params["collective_id"] = SIBLING_BARRIER_ID
    outs = pl.pallas_call(
        body, name=name,
        in_specs=[HBM] * n + [ANY] * len(after),
        out_specs=[SEM, SEM] + [HBM] * n + [pl.BlockSpec(memory_space=pltpu.VMEM)],
        out_shape=([pltpu.SemaphoreType.DMA((n_copies,))] * 2 + [pltpu.HBM(b.shape, b.dtype) for b in bufs]
                   + [jax.ShapeDtypeStruct((SUBLANES, LANES), F32)]),
        input_output_aliases={a: 2 + a for a in range(n)},
        compiler_params=pltpu.CompilerParams(**params),
    )(*[_hbm(b) for b in bufs], *after)
    return outs[0], outs[1], list(outs[2:2 + n]), outs[-1]


def xchg_wait(name, bufs, ssem, rsem, n_copies, plan, after, sems=None):
    n = len(bufs)
    after = list(after)
    sems = tuple(range(n_copies)) if sems is None else tuple(sems)
    assert len(sems) == n_copies

    def body(*refs):
        ssem_ref, rsem_ref = refs[n], refs[n + 1]
        copies = plan(refs[0:n], *_me())
        assert len(copies) == n_copies
        for k, (src, _, peer, land) in zip(sems, copies):
            cp = _rcopy(src, land, ssem_ref.at[k], rsem_ref.at[k], peer)
            cp.wait_send()
            cp.wait_recv()

    outs = pl.pallas_call(
        body, name=name,
        in_specs=[HBM] * n + [SEM, SEM] + [ANY] * len(after), out_specs=[HBM] * n,
        out_shape=[pltpu.HBM(b.shape, b.dtype) for b in bufs],
        input_output_aliases={a: a for a in range(n)},
        compiler_params=SPLIT,
    )(*bufs, ssem, rsem, *after)
    return list(outs)


def _shard_half(buf, chip, half):
    if len(buf.shape) == 2:
        h, w = buf.shape[0] // 2, buf.shape[1] // N_CHIPS
        return buf.at[pl.ds(half * h, h), pl.ds(chip * w, w)]
    h = buf.shape[1] // 2
    return buf.at[chip, pl.ds(half * h, h)]


def plan_gather(refs, x, y, c):
    out = []
    for buf in refs:
        own = _shard_half(buf, 2 * x + y, c)
        for (px, py) in _chips(x, y):
            out.append((own, own, (px, py, c), _shard_half(buf, 2 * px + py, c)))
    return out


def plan_forward(refs, x, y, c):
    out = []
    for (px, py) in _chips(x, y):
        for buf in refs:
            landed = _shard_half(buf, 2 * px + py, c)
            out.append((landed, landed, (x, y, 1 - c), _shard_half(buf, 2 * px + py, 1 - c)))
    return out


def plan_sibling(refs, x, y, c):
    n = len(refs) // 2
    out = []
    for a in range(n):
        h = refs[a].shape[1] // 2
        out.append((refs[a].at[:, pl.ds((1 - c) * h, h)], refs[n + a], (x, y, 1 - c), refs[n + a]))
    return out


def plan_chip(refs, x, y, c):
    n = len(refs) // 2
    out = []
    for j, (px, py) in enumerate(_chips(x, y)):
        for a in range(n):
            out.append((refs[a].at[2 * px + py], refs[n + a].at[j], (px, py, c), refs[n + a].at[j]))
    return out


def plan_mod(refs, x, y, c):
    (mods,) = refs
    mine = mods.at[2 * x + y]
    return [(mine, mine, (px, py, c), mods.at[2 * px + py]) for (px, py) in _chips(x, y)]


def plan_pack(refs, x, y, c):
    (packs,) = refs
    mine = packs.at[4 * x + 2 * y + c]
    return [(mine, mine, peer, packs.at[4 * peer[0] + 2 * peer[1] + peer[2]]) for peer in _peers7(x, y, c)]


def plan_spread(layers, wp_layers):
    def plan(refs, x, y, c):
        gi, go, gp = refs
        hD, hR, hP = gi.shape[1] // 2, go.shape[1] // 2, gp.shape[2] // 2
        sib = (x, y, 1 - c)
        out = []
        for l in layers:
            mine = gi.at[l, pl.ds(c * hD, hD)]
            out.append((mine, mine, sib, gi.at[l, pl.ds((1 - c) * hD, hD)]))
            mine = go.at[l, pl.ds(c * hR, hR)]
            out.append((mine, mine, sib, go.at[l, pl.ds((1 - c) * hR, hR)]))
        for l in wp_layers:
            mine = gp.at[l, 2 * x + y, pl.ds(c * hP, hP)]
            for peer in _peers7(x, y, c):
                out.append((mine, mine, peer, gp.at[l, 2 * peer[0] + peer[1], pl.ds(peer[2] * hP, hP)]))
        return out

    return plan


def place_small(pos, c8, wc):
    L = wc.shape[0]

    def body(pos_ref, c_ref, wc_ref, call_ref, wcall_ref):
        call_ref[...] = c_ref[...]
        wcall_ref[...] = wc_ref[...]

    return pl.pallas_call(
        body, name="place_small",
        grid_spec=pltpu.PrefetchScalarGridSpec(
            num_scalar_prefetch=1, grid=(1,),
            in_specs=[pl.BlockSpec((SUBLANES, LANES), lambda i, p: (0, 0)),
                      pl.BlockSpec((L, 3, LANES), lambda i, p: (0, 0, 0))],
            out_specs=[pl.BlockSpec((None, SUBLANES, LANES), lambda i, p: (p[2], 0, 0)),
                       pl.BlockSpec((None, L, 3, LANES), lambda i, p: (p[1], 0, 0, 0))]),
        out_shape=[jax.ShapeDtypeStruct((N_DEV, SUBLANES, LANES), F32),
                   jax.ShapeDtypeStruct((N_CHIPS, L, 3, LANES), F32)],
        compiler_params=_params(),
    )(pos, c8, wc)


def plan_small(refs, x, y, c):
    call, wcall = refs
    mine = call.at[4 * x + 2 * y + c]
    out = [(mine, mine, peer, call.at[4 * peer[0] + 2 * peer[1] + peer[2]]) for peer in _peers7(x, y, c)]
    mine = wcall.at[2 * x + y]
    out += [(mine, mine, (px, py, c), wcall.at[2 * px + py]) for (px, py) in _chips(x, y)]
    return out


def add_sibling(cidx, mine, sib):
    def body(c_ref, *refs):
        for a in range(3):
            m, s, o = refs[a], refs[3 + a], refs[6 + a]
            o[...] = (m[...].astype(F32) + s[...].astype(F32)).astype(BF16)

    per_step = 2

    def mine_spec(a):
        h = a.shape[1] // 2
        return pl.BlockSpec((per_step, h, a.shape[2]), lambda j, c_ref: (j, c_ref[0], 0))

    def sib_spec(a):
        return pl.BlockSpec((per_step,) + a.shape[1:], lambda j, c_ref: (j, 0, 0))

    return pl.pallas_call(
        body, name="add_sibling",
        grid_spec=pltpu.PrefetchScalarGridSpec(
            num_scalar_prefetch=1, grid=(N_CHIPS // per_step,),
            in_specs=[mine_spec(a) for a in mine] + [sib_spec(a) for a in sib],
            out_specs=[sib_spec(a) for a in sib]),
        out_shape=[jax.ShapeDtypeStruct(a.shape, BF16) for a in sib],
        compiler_params=_params(VMEM_BIG),
    )(cidx, *mine, *sib)


def sum_chips(pos, own, rb, acc, l, shapes):
    nq = 2
    n_in = 6 + (3 if acc is not None else 0)

    def body(pos_ref, *refs):
        for a in range(3):
            m, b, o = refs[a], refs[3 + a], refs[n_in + a]
            s = m[...].astype(F32)
            for j in range(3):
                s = s + b[j].astype(F32)
            o[...] = s

    def own_spec(a):
        return pl.BlockSpec((None, a.shape[1] // nq, a.shape[2]), lambda q, p: (p[1], q, 0))

    def rb_spec(a):
        return pl.BlockSpec((3, a.shape[1] // nq, a.shape[2]), lambda q, p: (0, q, 0))

    hi, ho, hp = own[0].shape[1] // nq, own[1].shape[1] // nq, own[2].shape[1] // nq
    out_specs = [pl.BlockSpec((None, hi, shapes[0][2]), lambda q, p: (l, p[0] * nq + q, 0)),
                 pl.BlockSpec((None, ho, shapes[1][2]), lambda q, p: (l, p[0] * nq + q, 0)),
                 pl.BlockSpec((None, None, hp, LANES), lambda q, p: (l, p[1], p[0] * nq + q, 0))]
    in_specs = [own_spec(a) for a in own] + [rb_spec(a) for a in rb]
    args = list(own) + list(rb)
    aliases = {}
    if acc is not None:
        in_specs += [ANY] * 3
        args += list(acc)
        aliases = {7: 0, 8: 1, 9: 2}
    return pl.pallas_call(
        body, name="sum_chips",
        grid_spec=pltpu.PrefetchScalarGridSpec(num_scalar_prefetch=1, grid=(nq,), in_specs=in_specs, out_specs=out_specs),
        out_shape=[jax.ShapeDtypeStruct(s, F32) for s in shapes],
        input_output_aliases=aliases,
        compiler_params=_params(VMEM_BIG),
    )(pos, *args)


def _wconv_slot(chip, tap):
    idx = 3 * chip + tap
    return ROW_WCONV + idx // SUBLANES, slice((idx % SUBLANES) * LANES, (idx % SUBLANES + 1) * LANES)


def pack_small(pos, per_layer, loss_blk):
    L = len(per_layer)
    D = per_layer[0][0].shape[1]

    def body(pos_ref, *refs):
        o = refs[-1]
        lb = refs[-2]
        o[...] = jnp.zeros_like(o)
        for l in range(L):
            dgpre, dgpost, dsh, dsc, dgt, dps, dwc = refs[7 * l:7 * l + 7]
            base = SUBLANES * l
            o[pl.ds(base + ROW_G_PRE, 1), :] = dgpre[...]
            o[pl.ds(base + ROW_G_POST, 1), :] = dgpost[...]
            for r, src in enumerate((dsh, dsc, dgt)):
                o[pl.ds(base + ROW_MOD + r, 1), :] = src[...]
            o[pl.ds(base + ROW_PSCALE, 1), 0:dps.shape[1]] = dps[...]
            for j in range(dwc.shape[0]):
                for k in range(3):
                    row, lanes = _wconv_slot(j, k)
                    o[pl.ds(base + row, 1), lanes] = dwc[j, pl.ds(k, 1), :]
        o[pl.ds(ROW_PSCALE, 1), LOSS_LANES] = lb[pl.ds(0, 1), :]

    flat = [a for layer in per_layer for a in layer] + [loss_blk]

    def whole(a):
        return pl.BlockSpec(a.shape, lambda i, p: (0,) * a.ndim)

    return pl.pallas_call(
        body, name="pack_small",
        grid_spec=pltpu.PrefetchScalarGridSpec(
            num_scalar_prefetch=1, grid=(1,), in_specs=[whole(a) for a in flat],
            out_specs=pl.BlockSpec((None, L * SUBLANES, D), lambda i, p: (p[2], 0, 0))),
        out_shape=jax.ShapeDtypeStruct((N_DEV, L * SUBLANES, D), F32),
        compiler_params=_params(),
    )(pos, *flat)


def small_update(pos, packs, params, moments_m, moments_v):
    n = len(params)
    L, D = params[1].shape
    PS = params[3].shape[1]

    def body(pos_ref, p_ref, *refs):
        ws, ms, vs = refs[0:n], refs[n:2 * n], refs[2 * n:3 * n]
        loss_ref = refs[3 * n]
        outs = [refs[3 * n + 1 + 4 * t:3 * n + 5 + 4 * t] for t in range(n)]
        summed = refs[-1]
        s = p_ref[0]
        for d in range(1, N_DEV):
            s = s + p_ref[d]
        summed[...] = s
        loss_ref[...] = summed[pl.ds(ROW_PSCALE, 1), LOSS_LANES]
        chip = pos_ref[1]

        def update(t, idx, g):
            d, mm, vv = _adamw_math(ws[t][idx], g, ms[t][idx], vs[t][idx])
            g_ref, d_ref, mo_ref, vo_ref = outs[t]
            g_ref[idx] = g
            d_ref[idx] = d
            mo_ref[idx] = mm
            vo_ref[idx] = vv

        for l in range(L):
            base = SUBLANES * l
            row = pl.ds(l, 1)
            for k in range(3):
                update(0, (row, slice(k * D, (k + 1) * D)), summed[pl.ds(base + ROW_MOD + k, 1), :])
            update(1, (row, slice(None)), summed[pl.ds(base + ROW_G_PRE, 1), :])
            update(2, (row, slice(None)), summed[pl.ds(base + ROW_G_POST, 1), :])
            update(3, (row, slice(None)), summed[pl.ds(base + ROW_PSCALE, 1), 0:PS])
            for k in range(3):
                g = None
                for j in range(N_CHIPS):
                    wrow, lanes = _wconv_slot(j, k)
                    cand = summed[pl.ds(base + wrow, 1), lanes]
                    g = cand if g is None else jnp.where(chip == j, cand, g)
                update(4, (l, pl.ds(k, 1), slice(None)), g)

    def whole(a):
        return pl.BlockSpec(a.shape, lambda i, p: (0,) * a.ndim)

    ins = [packs] + list(params) + list(moments_m) + list(moments_v)
    out_shape = [jax.ShapeDtypeStruct((1, LANES), F32)]
    for w in params:
        out_shape += [jax.ShapeDtypeStruct(w.shape, F32)] * 4
    outs = pl.pallas_call(
        body, name="small_update",
        grid_spec=pltpu.PrefetchScalarGridSpec(
            num_scalar_prefetch=1, grid=(1,), in_specs=[whole(a) for a in ins],
            out_specs=[whole(a) for a in out_shape],
            scratch_shapes=[pltpu.VMEM(packs.shape[1:], F32)]),
        out_shape=out_shape,
        compiler_params=_params(),
    )(pos, *ins)
    return outs[0], [outs[1 + 4 * t:5 + 4 * t] for t in range(n)]


def _adamw_math(w, g, m, v):
    m = ADAM_B1 * m + (1.0 - ADAM_B1) * g
    v = ADAM_B2 * v + (1.0 - ADAM_B2) * (g * g)
    m_hat = m / (1.0 - ADAM_B1 ** ADAM_STEP)
    v_hat = v / (1.0 - ADAM_B2 ** ADAM_STEP)
    delta = -ADAM_LR * (m_hat / (jnp.sqrt(v_hat) + ADAM_EPS) + ADAM_WD * w)
    return delta, m, v


def _adamw_block(ins, outs):
    w_ref, g_ref, m_ref, v_ref = ins
    go_ref, d_ref, mo_ref, vo_ref = outs
    gv = g_ref[...]
    d, mm, vv = _adamw_math(w_ref[...], gv, m_ref[...], v_ref[...])
    go_ref[...] = gv
    d_ref[...] = d
    mo_ref[...] = mm
    vo_ref[...] = vv


def adamw(groups, name, first, count, steps, acc=None):
    n = len(groups)

    def body(*refs):
        outs = refs[len(refs) - 4 * n:]
        for k in range(n):
            _adamw_block(refs[4 * k:4 * k + 4], outs[4 * k:4 * k + 4])

    specs, out_shape, args = [], [], []
    for group in groups:
        shape = group[0].shape
        spec = pl.BlockSpec((1, shape[1] // steps) + shape[2:],
                            lambda i, s, rest=(0,) * (len(shape) - 2): (first + i, s) + rest)
        specs += [spec] * 4
        out_shape += [jax.ShapeDtypeStruct(shape, F32)] * 4
        args += list(group)
    extra = [] if acc is None else list(acc)
    return pl.pallas_call(
        body, name=name, grid=(count, steps),
        in_specs=specs + [ANY] * len(extra), out_specs=specs, out_shape=out_shape,
        input_output_aliases={4 * n + a: a for a in range(len(extra))},
        compiler_params=_params(VMEM_BIG, n_grid=2),
    )(*args, *extra)


def ada_finish(c_all, dmod, w, m, v):
    L, D, CW = w.shape
    hD = D // 2

    def body(c_ref, d_ref, w_ref, m_ref, v_ref, g_ref, dl_ref, mo_ref, vo_ref):
        cv = c_ref[...]
        z = jnp.zeros_like(cv)
        ca = jnp.concatenate([cv * jax.nn.sigmoid(cv), z], axis=0).astype(BF16)
        dm = jnp.concatenate([d_ref[0], jnp.zeros_like(d_ref[0])], axis=0).astype(BF16)
        g = lax.dot_general(ca, dm, TN, preferred_element_type=F32)
        g_ref[0] = g
        d, mm, vv = _adamw_math(w_ref[0], g, m_ref[0], v_ref[0])
        dl_ref[0] = d
        mo_ref[0] = mm
        vo_ref[0] = vv

    big = pl.BlockSpec((1, hD, CW), lambda l, h: (l, h, 0))
    shape = jax.ShapeDtypeStruct(w.shape, F32)
    return pl.pallas_call(
        body, name="ada_finish", grid=(L, 2),
        in_specs=[pl.BlockSpec((N_DEV, hD), lambda l, h: (0, h)), pl.BlockSpec((1, N_DEV, CW), lambda l, h: (l, 0, 0)),
                  big, big, big],
        out_specs=[big] * 4, out_shape=[shape] * 4,
        compiler_params=_params(VMEM_BIG, n_grid=2),
    )(c_all, dmod, w, m, v)


def kernel(x, c, w_ada, b_ada, g_pre, w_in, w_conv, w_pool, pool_scale, w_out, g_post, loss_target, m_w_ada, m_b_ada, m_g_pre, m_w_in, m_w_conv, m_w_pool, m_pool_scale, m_w_out, m_g_post, v_w_ada, v_b_ada, v_g_pre, v_w_in, v_w_conv, v_w_pool, v_pool_scale, v_w_out, v_g_post):
    L, D, CW = w_in.shape
    RO = w_out.shape[1]
    T = x.shape[1]
    ix, iy, ic = _me()
    chip = 2 * ix + iy
    me_lin = 4 * ix + 2 * iy + ic

    pos = jnp.stack([ic, chip, me_lin]).astype(jnp.int32)
    g_pre3, g_post3 = g_pre.reshape(L, 1, D), g_post.reshape(L, 1, D)
    pscale3 = pool_scale.reshape(L, 1, pool_scale.shape[1])
    n_s, n_c = 3, 9

    def gather(bufs, after):
        ss, rs, bufs, tok = xchg_start("gather_start", bufs, 3 * len(bufs), plan_gather, after=after)
        return (ss, rs, bufs), tok

    def ready(flight, after):
        fss, frs, bufs = flight
        return xchg_wait("forward_wait", bufs, fss, frs, 3 * len(bufs), plan_forward, after)

    def arrive_part(flight, which, after):
        ss, rs, bufs = flight
        sems = tuple(range(3 * which, 3 * which + 3))
        (buf,) = xchg_wait("gather_wait", [bufs[which]], ss, rs, 3, plan_gather, after, sems=sems)
        fss, frs, (buf,), tok = xchg_start("forward_start", [buf], 3, plan_forward, sibling_only=True)
        return (fss, frs, [buf]), tok

    n_small = N_DEV - 1 + N_CHIPS - 1
    s_ss, s_rs, smalls_in, token = xchg_start("small_start", list(place_small(pos, c.reshape(SUBLANES, LANES), w_conv)),
                                              n_small, plan_small)
    w_in_of, w_out_of = [None] * L, [None] * L
    gi0, go0 = cast_weights(pos, w_in, w_out, 0, token)
    flight, token = gather([gi0], [])
    w_in_of[0] = (flight, 0)
    c_all3, wconv_all = xchg_wait("small_wait", smalls_in, s_ss, s_rs, n_small, plan_small, [token])
    c_all = c_all3.reshape(N_DEV, D)
    b_my = lax.dynamic_slice_in_dim(b_ada, chip * CW, CW, axis=1)
    m_ss, m_rs, mods, token = xchg_start("mod_start", [mod_part(pos, c_all, w_ada, b_my, token)], 3, plan_mod)
    gi1, go1 = cast_weights(pos, w_in, w_out, 1, token)
    flight, token = gather([gi1, go0, go1], [])
    w_in_of[1], w_out_of[0], w_out_of[1] = (flight, 0), (flight, 1), (flight, 2)
    late = []
    for l in range(2, L):
        late += list(cast_weights(pos, w_in, w_out, l, token))
    flight, token = gather(late, [])
    for l in range(2, L):
        w_in_of[l], w_out_of[l] = (flight, 2 * (l - 2)), (flight, 2 * (l - 2) + 1)
    fwd_in, token = arrive_part(*w_in_of[0], [token])
    (mod_all,) = xchg_wait("mod_wait", mods, m_ss, m_rs, 3, plan_mod, [token])
    mod = lax.dynamic_index_in_dim(mod_all, me_lin, axis=2, keepdims=False)
    mod4 = jnp.transpose(mod, (1, 0, 2)).reshape(L, 3, 1, D)

    xs, projs, yas, yps, ys, pooleds = [x.reshape(T, D)], [], [], [], [], []
    wg_in, wg_out = [], []
    for l in range(L):
        (gi,) = ready(fwd_in, [mod4 if l == 0 else xs[l]])
        proj = proj_fwd(xs[l], mod4, g_pre3, gi, l)
        ya, yp, pooled = mix_fwd(proj, wconv_all, w_pool, pscale3, l)
        pooleds.append(pooled)
        fwd_out, token = arrive_part(*w_out_of[l], [ya, yp])
        after = [token]
        if l + 1 < L:
            fwd_in, token = arrive_part(*w_in_of[l + 1], after)
            after = [token]
        (go,) = ready(fwd_out, after)
        wg_in.append(gi)
        wg_out.append(go.reshape(N_CHIPS * RO, D))
        projs.append(proj)
        yas.append(ya)
        yps.append(yp)
        if l + 1 < L:
            xn, yv = out_fwd(ya, yp, wg_out[l], xs[l], mod4, g_post3, l, after[0])
            xs.append(xn)
        else:
            dx, yv, loss_blk = out_fwd_loss(ya, yp, wg_out[l], xs[l], mod4, g_post3, l, loss_target.reshape(T, D))
        ys.append(yv)

    shapes = (w_in.shape, w_out.shape, w_pool.shape)
    smalls = [None] * L
    acc, flying, sib, token = None, None, None, loss_blk

    def to_chips(sib, after):
        sl, s_ss, s_rs, s_bufs = sib
        s_bufs = xchg_wait("sibling_wait", s_bufs, s_ss, s_rs, n_s, plan_sibling, after)
        chip_parts = add_sibling(pos, s_bufs[0:3], s_bufs[3:6])
        lands = [lax.empty((3,) + a.shape[1:], a.dtype) for a in chip_parts]
        c_ss, c_rs, c_bufs, ctoken = xchg_start("chip_start", list(chip_parts) + lands, n_c, plan_chip)
        return (sl, c_ss, c_rs, c_bufs), ctoken

    def landed(flying, acc, after):
        fl, f_ss, f_rs, f_bufs = flying
        f_bufs = xchg_wait("chip_wait", f_bufs, f_ss, f_rs, n_c, plan_chip, after)
        return sum_chips(pos, f_bufs[0:3], f_bufs[3:6], acc, fl, shapes)

    early = None
    for l in reversed(range(L)):
        dya, dyp, dwo_l, dgate, dgpost = out_bwd(dx, ys[l], yas[l], yps[l], wg_out[l], mod4, g_post3, l, token)
        token = dya
        spreading = None
        if sib is not None:
            arrived = flying
            flying, token = to_chips(sib, [dya])
            if arrived is not None:
                acc = landed(arrived, acc, [token])
                spreading = plan_spread((arrived[0],), ())
                sp_ss, sp_rs, acc, token = xchg_start("spread_start", list(acc), 2, spreading, sibling_only=True)
        du_a, db_a, dc_a, dg_a, du_p, dg_p, dwc, dwp_l, dps = mix_bwd(projs[l], pooleds[l], dya, dyp, wconv_all, w_pool,
                                                                        pscale3, l, token)
        update = None
        if spreading is not None:
            acc = xchg_wait("spread_wait", acc, sp_ss, sp_rs, 2, spreading, [du_a])
            update = (arrived[0], [w_in, acc[0], m_w_in, v_w_in], [w_out, acc[1], m_w_out, v_w_out], early)
        dx, dwi_l, dshift, dscale, dgpre, *rest = in_bwd([du_a, db_a, dc_a, dg_a, du_p, dg_p], wg_in[l], xs[l], dx,
                                                         mod4, g_pre3, l, update)
        early = rest if rest else early
        smalls[l] = (dgpre, dgpost, dshift, dscale, dgate, dps, dwc)
        parts = [dwi_l, dwo_l.reshape(N_CHIPS, RO, D), dwp_l]
        s_lands = [lax.empty((a.shape[0], a.shape[1] // 2) + a.shape[2:], a.dtype) for a in parts]
        s_ss, s_rs, s_bufs, token = xchg_start("sibling_start", parts + s_lands, n_s, plan_sibling, sibling_only=True)
        sib = (l, s_ss, s_rs, s_bufs)
    grad_x = dx.reshape(1, T, D)

    p_ss, p_rs, packs, ptoken = xchg_start("pack_start", [pack_small(pos, smalls, loss_blk)], N_DEV - 1, plan_pack)
    acc = landed(flying, acc, [ptoken, token])
    n_sp = 2 + (N_DEV - 1) * (L - 1)
    spread = plan_spread((1,), tuple(range(1, L)))
    sp_ss, sp_rs, acc, sp_token = xchg_start("spread_start", list(acc), n_sp, spread)
    flying, token = to_chips(sib, [sp_token])
    (packs_all,) = xchg_wait("pack_wait", packs, p_ss, p_rs, N_DEV - 1, plan_pack, [token])
    dmod_all = packs_all.reshape(N_DEV, L, SUBLANES, D)[:, :, ROW_MOD:ROW_MOD + 3].reshape(N_DEV, L, 3 * D)
    dmod_my = jnp.transpose(lax.dynamic_slice_in_dim(dmod_all, chip * CW, CW, axis=2), (1, 0, 2))

    g_w_ada, d_w_ada, nm_w_ada, nv_w_ada = ada_finish(c_all, dmod_my, w_ada, m_w_ada, v_w_ada)
    loss_row, upd = small_update(pos, packs_all, [b_ada, g_pre, g_post, pool_scale, w_conv],
                                 [m_b_ada, m_g_pre, m_g_post, m_pool_scale, m_w_conv],
                                 [v_b_ada, v_g_pre, v_g_post, v_pool_scale, v_w_conv])
    loss = loss_row[0, 0]
    (g_b_ada, d_b_ada, nm_b_ada, nv_b_ada), (g_g_pre, d_g_pre, nm_g_pre, nv_g_pre) = upd[0], upd[1]
    (g_g_post, d_g_post, nm_g_post, nv_g_post), (g_pscale, d_pscale, nm_pscale, nv_pscale) = upd[2], upd[3]
    g_w_conv, d_w_conv, nm_w_conv, nv_w_conv = upd[4]

    done = [nv_w_ada, nv_w_conv]
    g_w_in, g_w_out, g_w_pool = xchg_wait("spread_wait", acc, sp_ss, sp_rs, n_sp, spread, done)
    early = adamw([[w_in, g_w_in, m_w_in, v_w_in], [w_out, g_w_out, m_w_out, v_w_out]], "adamw_layer", 1, 1, 2, early)

    acc = landed(flying, (g_w_in, g_w_out, g_w_pool), [early[3], early[7]])
    last = plan_spread((0,), (0,))
    n_last = 2 + N_DEV - 1
    l_ss, l_rs, acc, _ = xchg_start("spread_start", list(acc), n_last, last)
    upd_pool = adamw([[w_pool, acc[2], m_w_pool, v_w_pool]], "adamw_w_pool", 1, L - 1, 1)
    r_w_in, r_w_out, r_w_pool = xchg_wait("spread_wait", acc, l_ss, l_rs, n_last, last, [upd_pool[3]])
    (g_w_in, d_w_in, nm_w_in, nv_w_in, g_w_out, d_w_out, nm_w_out, nv_w_out,
     g_w_pool, d_w_pool, nm_w_pool, nv_w_pool) = adamw(
         [[w_in, r_w_in, m_w_in, v_w_in], [w_out, r_w_out, m_w_out, v_w_out], [w_pool, r_w_pool, m_w_pool, v_w_pool]],
         "adamw_layer", 0, 1, 2, list(early) + list(upd_pool))

    return (loss, grad_x,
            g_w_ada, g_b_ada, g_g_pre, g_w_in, g_w_conv, g_w_pool, g_pscale, g_w_out, g_g_post,
            d_w_ada, d_b_ada, d_g_pre, d_w_in, d_w_conv, d_w_pool, d_pscale, d_w_out, d_g_post,
            nm_w_ada, nm_b_ada, nm_g_pre, nm_w_in, nm_w_conv, nm_w_pool, nm_pscale, nm_w_out, nm_g_post,
            nv_w_ada, nv_b_ada, nv_g_pre, nv_w_in, nv_w_conv, nv_w_pool, nv_pscale, nv_w_out, nv_g_post)
```

```python
import functools

import jax
import jax.numpy as jnp
from jax import lax
from jax.experimental import pallas as pl
from jax.experimental.pallas import tpu as pltpu

F32 = jnp.float32
BF16 = jnp.bfloat16
MESH = pl.DeviceIdType.MESH
ANY = pl.BlockSpec(memory_space=pl.ANY)

NORM_EPS = 1e-6
POOL_WINDOWS = (2, 4, 8, 16)
ADAM_LR = 0.001
ADAM_B1 = 0.9
ADAM_B2 = 0.999
ADAM_EPS = 1e-08
ADAM_WD = 0.01
ADAM_STEP = 10

N_CHIPS = 4
N_DEV = 8
LANES = 128
SUBLANES = 8
VMEM_BIG = 56 * 1024 * 1024
HIST = 16
R_CONV = 64
R_POOL = 128

ROW_G_PRE, ROW_G_POST, ROW_MOD, ROW_PSCALE, ROW_WCONV = 0, 1, 2, 5, 6
LOSS_LANES = slice(4 * LANES, 5 * LANES)

NT = (((1,), (1,)), ((), ()))
TN = (((0,), (0,)), ((), ()))


def _params(vmem=None, n_grid=1):
    kw = {}
    if n_grid:
        kw["dimension_semantics"] = ("arbitrary",) * n_grid
    if vmem is not None:
        kw["vmem_limit_bytes"] = vmem
    return pltpu.CompilerParams(**kw)


def _colsum8(v):
    n, d = v.shape
    return v.reshape(n // SUBLANES, SUBLANES, d).sum(axis=0)


def _rms(v):
    return lax.rsqrt(jnp.mean(v * v, axis=-1, keepdims=True) + NORM_EPS)


def _sigmoid(v):
    return 0.5 * jnp.tanh(0.5 * v) + 0.5


def _shift_down(ext, k, rows):
    if k == 0:
        return ext[HIST:HIST + rows]
    return pltpu.roll(ext, k, 0)[HIST:HIST + rows]


def _shift_up(ext, k, rows):
    if k == 0:
        return ext[0:rows]
    return pltpu.roll(ext, ext.shape[0] - k, 0)[0:rows]


def _load_ext(ref, r0, h0, first, rows):
    hist = ref[pl.ds(h0, HIST), :].astype(F32)
    hist = jnp.where(first, 0.0, hist)
    cur = ref[pl.ds(r0, rows), :].astype(F32)
    return jnp.concatenate([hist, cur], axis=0)


def _me():
    return lax.axis_index("x"), lax.axis_index("y"), lax.axis_index("c")


def cast_weights(pos, w_in, w_out, l, after):
    _, D, CW = w_in.shape
    RO = w_out.shape[1]

    def body(pos_ref, wi, wo, after_ref, oi, oo):
        oi[...] = wi[...].astype(BF16)
        oo[...] = wo[...].astype(BF16)

    return pl.pallas_call(
        body, name="cast_w",
        grid_spec=pltpu.PrefetchScalarGridSpec(
            num_scalar_prefetch=1, grid=(2,),
            in_specs=[pl.BlockSpec((None, D // 2, CW), lambda h, p: (l, h, 0)),
                      pl.BlockSpec((None, RO // 2, D), lambda h, p: (l, h, 0)), ANY],
            out_specs=[pl.BlockSpec((D // 2, CW), lambda h, p: (h, p[1])),
                       pl.BlockSpec((None, RO // 2, D), lambda h, p: (p[1], h, 0))]),
        out_shape=[jax.ShapeDtypeStruct((D, N_CHIPS * CW), BF16), jax.ShapeDtypeStruct((N_CHIPS, RO, D), BF16)],
        compiler_params=_params(),
    )(pos, w_in, w_out, after)


def mod_part(pos, c_all, w_ada, b_my, after):
    L, D, CW = w_ada.shape

    def body(pos_ref, c_ref, w_ref, b_ref, after_ref, o_ref):
        cv = c_ref[...]
        ca = (cv * jax.nn.sigmoid(cv)).astype(BF16)
        o_ref[...] = jnp.dot(ca, w_ref[0].astype(BF16), preferred_element_type=F32) + b_ref[0]

    return pl.pallas_call(
        body, name="mod_part",
        grid_spec=pltpu.PrefetchScalarGridSpec(
            num_scalar_prefetch=1, grid=(L,),
            in_specs=[pl.BlockSpec((N_DEV, D), lambda l, p: (0, 0)),
                      pl.BlockSpec((1, D, CW), lambda l, p: (l, 0, 0)),
                      pl.BlockSpec((1, 1, CW), lambda l, p: (l, 0, 0)), ANY],
            out_specs=pl.BlockSpec((None, None, N_DEV, CW), lambda l, p: (p[1], l, 0, 0))),
        out_shape=jax.ShapeDtypeStruct((N_CHIPS, L, N_DEV, CW), F32),
        compiler_params=_params(VMEM_BIG),
    )(pos, c_all, w_ada, b_my.reshape(L, 1, CW), after)


def _mod_row(l, k, D):
    return pl.BlockSpec((None, None, 1, D), lambda *_: (l, k, 0, 0))


def _layer_row(l, D):
    return pl.BlockSpec((None, 1, D), lambda *_: (l, 0, 0))


def proj_fwd(x, mod4, g_pre3, wg, l):
    T, D = x.shape
    NC = wg.shape[1]
    NB = N_CHIPS
    CW = NC // NB
    tm = 512

    def body(x_ref, sh_ref, sc_ref, g_ref, w_ref, o_ref):
        xv = x_ref[...]
        h = (xv * _rms(xv)) * (g_ref[...] * (1.0 + sc_ref[...])) + sh_ref[...]
        hb = h.astype(BF16)
        for j in range(NB):
            cols = slice(j * CW, (j + 1) * CW)
            o_ref[:, cols] = jnp.dot(hb, w_ref[:, cols], preferred_element_type=F32).astype(BF16)

    return pl.pallas_call(
        body, name="proj_fwd", grid=(T // tm,),
        in_specs=[pl.BlockSpec((tm, D), lambda i: (i, 0)), _mod_row(l, 0, D), _mod_row(l, 1, D), _layer_row(l, D),
                  pl.BlockSpec((D, NC), lambda i: (0, 0))],
        out_specs=pl.BlockSpec((tm, NC), lambda i: (i, 0)),
        out_shape=jax.ShapeDtypeStruct((T, NC), BF16),
        compiler_params=_params(VMEM_BIG),
    )(x, mod4, mod4, g_pre3, wg)


N_MIX = 4


def _conv_fwd_block(u_ref, b_ref, c_ref, g_ref, w_ref, o_ref):
    T = u_ref.shape[0]
    R = 2 * R_CONV
    w0 = w_ref[pl.ds(0, 1), :]
    w1 = w_ref[pl.ds(1, 1), :]
    w2 = w_ref[pl.ds(2, 1), :]

    def chunk(i, carry):
        r0 = pl.multiple_of(i * R, R)
        h0 = pl.multiple_of(jnp.maximum(r0 - HIST, 0), HIST)
        first = i == 0
        ca = _load_ext(c_ref, r0, h0, first, R) * _load_ext(u_ref, r0, h0, first, R)
        conv = w2 * ca[HIST:] + w1 * _shift_down(ca, 1, R) + w0 * _shift_down(ca, 2, R)
        g = g_ref[pl.ds(r0, R), :].astype(F32)
        b = b_ref[pl.ds(r0, R), :].astype(F32)
        o_ref[pl.ds(r0, R), :] = (b * conv * (g * _sigmoid(g))).astype(BF16)
        return carry

    lax.fori_loop(0, T // R, chunk, 0)


def _conv_idx(j):
    return jnp.minimum(j, N_MIX - 1)


def _pool_idx(j):
    return jnp.maximum(j - N_MIX, 0)


def _proj_col(T, off, idx):
    return pl.BlockSpec((T, LANES), lambda j: (0, idx(j) + off))


def _causal_window_sum(ext, w):
    s, k = ext, 1
    while k < w:
        s = s + pltpu.roll(s, k, 0)
        k *= 2
    return s


def _anticausal_window_sum(ext, w):
    s, k = ext, 1
    n = ext.shape[0]
    while k < w:
        s = s + pltpu.roll(s, n - k, 0)
        k *= 2
    return s


def _count(r0, rows, w):
    t = r0 + lax.broadcasted_iota(jnp.int32, (rows, LANES), 0)
    return jnp.minimum(t + 1, w).astype(F32)


def _pooled_loop(p_ref, pooled_s, w, T):
    R = R_POOL

    def chunk(i, carry):
        r0 = pl.multiple_of(i * R, R)
        h0 = pl.multiple_of(jnp.maximum(r0 - HIST, 0), HIST)
        ext = _load_ext(p_ref, r0, h0, i == 0, R)
        ws = _causal_window_sum(ext, w)[HIST:]
        pooled_s[pl.ds(r0, R), :] = (ws / _count(r0, R, w) - ext[HIST:]).astype(BF16)
        return carry

    lax.fori_loop(0, T // R, chunk, 0)


def _conv_w_spec(l):
    return pl.BlockSpec((None, None, 3, LANES), lambda j: (_conv_idx(j), l, 0, 0))


def _pool_w_spec(l):
    return pl.BlockSpec((None, None, LANES, LANES), lambda j: (l, _pool_idx(j), 0, 0))


def _pool_s_spec(l):
    return pl.BlockSpec((None, 1, LANES), lambda j: (l, 0, _pool_idx(j)))


def _pool_fwd_group(p_ref, g_ref, w_ref, s_ref, o_ref, pooled_s, mixed_s, w):
    T = p_ref.shape[0]
    R = R_POOL
    _pooled_loop(p_ref, pooled_s, w, T)
    mixed_s[...] = jnp.dot(pooled_s[...], w_ref[...].astype(BF16), preferred_element_type=F32)
    sc = s_ref[...]

    def chunk(i, carry):
        r0 = pl.multiple_of(i * R, R)
        g = g_ref[pl.ds(r0, R), :].astype(F32)
        o_ref[pl.ds(r0, R), :] = (mixed_s[pl.ds(r0, R), :] * sc * (g * _sigmoid(g))).astype(BF16)
        return carry

    lax.fori_loop(0, T // R, chunk, 0)


def mix_fwd(proj, wconv, wpool, pscale3, l):
    T = proj.shape[0]

    def body(u_ref, b_ref, c_ref, g_ref, p_ref, gp_ref, wc_ref, wp_ref, s_ref, ya_ref, yp_ref, pooled_ref, mixed_s):
        j = pl.program_id(0)
        pl.when(j < N_MIX)(functools.partial(_conv_fwd_block, u_ref, b_ref, c_ref, g_ref, wc_ref, ya_ref))
        for k, w in enumerate(POOL_WINDOWS):
            pl.when(j == N_MIX + k)(functools.partial(_pool_fwd_group, p_ref, gp_ref, wp_ref, s_ref, yp_ref,
                                                      pooled_ref, mixed_s, w))

    half = jax.ShapeDtypeStruct((T, N_MIX * LANES), BF16)
    pool_col = pl.BlockSpec((T, LANES), lambda j: (0, _pool_idx(j)))
    return pl.pallas_call(
        body, name="mix_fwd", grid=(2 * N_MIX,),
        in_specs=[_proj_col(T, 0, _conv_idx), _proj_col(T, 4, _conv_idx), _proj_col(T, 8, _conv_idx),
                  _proj_col(T, 12, _conv_idx), _proj_col(T, 16, _pool_idx), _proj_col(T, 20, _pool_idx),
                  _conv_w_spec(l), _pool_w_spec(l), _pool_s_spec(l)],
        out_specs=[pl.BlockSpec((T, LANES), lambda j: (0, _conv_idx(j))), pool_col, pool_col],
        out_shape=[half, half, half],
        scratch_shapes=[pltpu.VMEM((T, LANES), F32)],
        compiler_params=_params(),
    )(proj, proj, proj, proj, proj, proj, wconv, wpool, pscale3)


def out_fwd(ya, yp, wo, x, mod4, g_post3, l, after):
    T, D = x.shape
    H = ya.shape[1]
    tm = 512

    def body(ya_ref, yp_ref, wo_ref, x_ref, gt_ref, g_ref, after_ref, xn_ref, y_ref):
        y = (jnp.dot(ya_ref[...], wo_ref[0:H, :], preferred_element_type=F32)
             + jnp.dot(yp_ref[...], wo_ref[H:2 * H, :], preferred_element_type=F32))
        xn_ref[...] = x_ref[...] + gt_ref[...] * (y * _rms(y) * g_ref[...])
        y_ref[...] = y.astype(BF16)

    tile = pl.BlockSpec((tm, D), lambda i: (i, 0))
    half = pl.BlockSpec((tm, H), lambda i: (i, 0))
    return pl.pallas_call(
        body, name="out_fwd", grid=(T // tm,),
        in_specs=[half, half, pl.BlockSpec((2 * H, D), lambda i: (0, 0)), tile, _mod_row(l, 2, D), _layer_row(l, D),
                  ANY],
        out_specs=[tile, tile],
        out_shape=[jax.ShapeDtypeStruct((T, D), F32), jax.ShapeDtypeStruct((T, D), BF16)],
        compiler_params=_params(VMEM_BIG),
    )(ya, yp, wo, x, mod4, g_post3, after)


def out_fwd_loss(ya, yp, wo, x, mod4, g_post3, l, target):
    T, D = x.shape
    H = ya.shape[1]
    tm = 512
    nt = T // tm

    def body(ya_ref, yp_ref, wo_ref, x_ref, gt_ref, g_ref, t_ref, dx_ref, y_ref, l_ref, acc):
        i = pl.program_id(0)

        @pl.when(i == 0)
        def _():
            acc[...] = jnp.zeros_like(acc)

        y = (jnp.dot(ya_ref[...], wo_ref[0:H, :], preferred_element_type=F32)
             + jnp.dot(yp_ref[...], wo_ref[H:2 * H, :], preferred_element_type=F32))
        y_ref[...] = y.astype(BF16)
        d = (x_ref[...] + gt_ref[...] * (y * _rms(y) * g_ref[...])) - t_ref[...]
        dx_ref[...] = d * (1.0 / D)
        acc[...] += _colsum8(d * d)

        @pl.when(i == nt - 1)
        def _():
            l_ref[...] = jnp.zeros_like(l_ref) + jnp.sum(acc[...]) * (0.5 / D)

    tile = pl.BlockSpec((tm, D), lambda i: (i, 0))
    half = pl.BlockSpec((tm, H), lambda i: (i, 0))
    return pl.pallas_call(
        body, name="out_fwd_loss", grid=(nt,),
        in_specs=[half, half, pl.BlockSpec((2 * H, D), lambda i: (0, 0)), tile, _mod_row(l, 2, D), _layer_row(l, D),
                  tile],
        out_specs=[tile, tile, pl.BlockSpec((SUBLANES, LANES), lambda i: (0, 0))],
        out_shape=[jax.ShapeDtypeStruct((T, D), F32), jax.ShapeDtypeStruct((T, D), BF16),
                   jax.ShapeDtypeStruct((SUBLANES, LANES), F32)],
        scratch_shapes=[pltpu.VMEM((SUBLANES, D), F32)],
        compiler_params=_params(VMEM_BIG),
    )(ya, yp, wo, x, mod4, g_post3, target)


def out_bwd(dx, y, ya, yp, wo, mod4, g_post3, l, after):
    T, D = dx.shape
    H = ya.shape[1]
    tm = 512
    nt = T // tm

    def body(dx_ref, y_ref, ya_ref, yp_ref, wo_ref, gt_ref, g_ref, after_ref,
             dya_ref, dyp_ref, dwo_ref, dgt_ref, dg_ref, acc_w, acc_p):
        i = pl.program_id(0)

        @pl.when(i == 0)
        def _():
            acc_w[...] = jnp.zeros_like(acc_w)
            acc_p[...] = jnp.zeros_like(acc_p)

        yv = y_ref[...].astype(F32)
        dxv = dx_ref[...]
        gg = gt_ref[...] * g_ref[...]
        r = _rms(yv)
        yn = yv * r
        p = dxv * yn
        acc_p[...] += _colsum8(p)
        dy = r * (dxv * gg - yn * jnp.mean(p * gg, axis=-1, keepdims=True))
        dyb = dy.astype(BF16)
        dyc = lax.dot_general(dyb, wo_ref[...], NT, preferred_element_type=F32)
        dya_ref[...] = dyc[:, 0:H].astype(BF16)
        dyp_ref[...] = dyc[:, H:2 * H].astype(BF16)
        acc_w[0:H, :] += lax.dot_general(ya_ref[...], dyb, TN, preferred_element_type=F32)
        acc_w[H:2 * H, :] += lax.dot_general(yp_ref[...], dyb, TN, preferred_element_type=F32)

        @pl.when(i == nt - 1)
        def _():
            dwo_ref[...] = acc_w[...].astype(BF16)
            sp = jnp.sum(acc_p[...], axis=0, keepdims=True)
            dgt_ref[...] = g_ref[...] * sp
            dg_ref[...] = gt_ref[...] * sp

    row = pl.BlockSpec((1, D), lambda i: (0, 0))
    tile = pl.BlockSpec((tm, D), lambda i: (i, 0))
    half = pl.BlockSpec((tm, H), lambda i: (i, 0))
    full = pl.BlockSpec((2 * H, D), lambda i: (0, 0))
    return pl.pallas_call(
        body, name="out_bwd", grid=(nt,),
        in_specs=[tile, tile, half, half, full, _mod_row(l, 2, D), _layer_row(l, D), ANY],
        out_specs=[half, half, full, row, row],
        out_shape=[jax.ShapeDtypeStruct((T, H), BF16), jax.ShapeDtypeStruct((T, H), BF16),
                   jax.ShapeDtypeStruct((2 * H, D), BF16),
                   jax.ShapeDtypeStruct((1, D), F32), jax.ShapeDtypeStruct((1, D), F32)],
        scratch_shapes=[pltpu.VMEM((2 * H, D), F32), pltpu.VMEM((SUBLANES, D), F32)],
        compiler_params=_params(VMEM_BIG),
    )(dx, y, ya, yp, wo, mod4, g_post3, after)


def _conv_bwd_block(u_ref, b_ref, c_ref, g_ref, dy_ref, w_ref, du_ref, db_ref, dc_ref, dg_ref, dw_ref):
    T = u_ref.shape[0]
    R = R_CONV
    nchunk = T // R
    w0 = w_ref[pl.ds(0, 1), :]
    w1 = w_ref[pl.ds(1, 1), :]
    w2 = w_ref[pl.ds(2, 1), :]

    def chunk(k, carry):
        head, a0, a1, a2 = carry
        i = nchunk - 1 - k
        r0 = pl.multiple_of(i * R, R)
        h0 = pl.multiple_of(jnp.maximum(r0 - HIST, 0), HIST)
        first = i == 0
        ue = _load_ext(u_ref, r0, h0, first, R)
        ce = _load_ext(c_ref, r0, h0, first, R)
        ca = ce * ue
        ca0 = ca[HIST:]
        ca1 = _shift_down(ca, 1, R)
        ca2 = _shift_down(ca, 2, R)
        conv = w2 * ca0 + w1 * ca1 + w0 * ca2
        g = g_ref[pl.ds(r0, R), :].astype(F32)
        b = b_ref[pl.ds(r0, R), :].astype(F32)
        dy = dy_ref[pl.ds(r0, R), :].astype(F32)
        sg = _sigmoid(g)
        sl = g * sg
        t = dy * conv
        db_ref[pl.ds(r0, R), :] = (t * sl).astype(BF16)
        dg_ref[pl.ds(r0, R), :] = (t * b * (sg + sl * (1.0 - sg))).astype(BF16)
        dconv = dy * b * sl
        a2 = a2 + _colsum8(dconv * ca0)
        a1 = a1 + _colsum8(dconv * ca1)
        a0 = a0 + _colsum8(dconv * ca2)
        e = jnp.concatenate([dconv, head], axis=0)
        dca = w2 * dconv + w1 * _shift_up(e, 1, R) + w0 * _shift_up(e, 2, R)
        du_ref[pl.ds(r0, R), :] = (dca * ce[HIST:]).astype(BF16)
        dc_ref[pl.ds(r0, R), :] = (dca * ue[HIST:]).astype(BF16)
        return dconv[0:SUBLANES], a0, a1, a2

    z = jnp.zeros((SUBLANES, LANES), F32)
    _, a0, a1, a2 = lax.fori_loop(0, nchunk, chunk, (z, z, z, z))
    dw_ref[pl.ds(0, 1), :] = jnp.sum(a0, axis=0, keepdims=True)
    dw_ref[pl.ds(1, 1), :] = jnp.sum(a1, axis=0, keepdims=True)
    dw_ref[pl.ds(2, 1), :] = jnp.sum(a2, axis=0, keepdims=True)


def _pool_bwd_group(pooled_s, g_ref, dy_ref, w_ref, s_ref, du_ref, dg_ref, dw_ref, ds_ref,
                    mixed_s, dmix_s, dpool_s, w):
    T = pooled_s.shape[0]
    R = R_POOL
    nchunk = T // R
    wb = w_ref[...].astype(BF16)
    mixed_s[...] = jnp.dot(pooled_s[...], wb, preferred_element_type=F32)
    sc = s_ref[...]

    def gate_chunk(i, acc):
        r0 = pl.multiple_of(i * R, R)
        g = g_ref[pl.ds(r0, R), :].astype(F32)
        dy = dy_ref[pl.ds(r0, R), :].astype(F32)
        mixed = mixed_s[pl.ds(r0, R), :]
        sg = _sigmoid(g)
        sl = g * sg
        dg_ref[pl.ds(r0, R), :] = (dy * mixed * sc * (sg + sl * (1.0 - sg))).astype(BF16)
        dms = dy * sl
        dmix_s[pl.ds(r0, R), :] = (dms * sc).astype(BF16)
        return acc + _colsum8(dms * mixed)

    acc = lax.fori_loop(0, nchunk, gate_chunk, jnp.zeros((SUBLANES, LANES), F32))
    ds_ref[...] = jnp.sum(acc, axis=0, keepdims=True)
    dpool_s[pl.ds(0, T), :] = lax.dot_general(dmix_s[...], wb, NT, preferred_element_type=F32)
    dpool_s[pl.ds(T, HIST), :] = jnp.zeros((HIST, LANES), F32)
    dw_ref[...] = lax.dot_general(pooled_s[...], dmix_s[...], TN, preferred_element_type=F32).astype(BF16)

    def back_chunk(i, carry):
        r0 = pl.multiple_of(i * R, R)
        dpe = dpool_s[pl.ds(r0, R + HIST), :]
        e = dpe / _count(r0, R + HIST, w)
        du_ref[pl.ds(r0, R), :] = (_anticausal_window_sum(e, w)[0:R] - dpe[0:R]).astype(BF16)
        return carry

    lax.fori_loop(0, nchunk, back_chunk, 0)


def mix_bwd(proj, pooled, dya, dyp, wconv, wpool, pscale3, l, after):
    T = proj.shape[0]

    def body(u_ref, b_ref, c_ref, g_ref, pooled_ref, gp_ref, dya_ref, dyp_ref, wc_ref, wp_ref, s_ref, after_ref,
             dua_ref, dba_ref, dca_ref, dga_ref, dup_ref, dgp_ref, dwc_ref, dwp_ref, ds_ref,
             mixed_s, dmix_s, dpool_s):
        j = pl.program_id(0)
        pl.when(j < N_MIX)(functools.partial(_conv_bwd_block, u_ref, b_ref, c_ref, g_ref, dya_ref, wc_ref,
                                             dua_ref, dba_ref, dca_ref, dga_ref, dwc_ref))
        for k, w in enumerate(POOL_WINDOWS):
            pl.when(j == N_MIX + k)(functools.partial(_pool_bwd_group, pooled_ref, gp_ref, dyp_ref, wp_ref, s_ref,
                                                      dup_ref, dgp_ref, dwp_ref, ds_ref,
                                                      mixed_s, dmix_s, dpool_s, w))

    sec = jax.ShapeDtypeStruct((T, N_MIX * LANES), BF16)
    conv_col = pl.BlockSpec((T, LANES), lambda j: (0, _conv_idx(j)))
    pool_col = pl.BlockSpec((T, LANES), lambda j: (0, _pool_idx(j)))
    return pl.pallas_call(
        body, name="mix_bwd", grid=(2 * N_MIX,),
        in_specs=[_proj_col(T, 0, _conv_idx), _proj_col(T, 4, _conv_idx), _proj_col(T, 8, _conv_idx),
                  _proj_col(T, 12, _conv_idx), pool_col, _proj_col(T, 20, _pool_idx),
                  conv_col, pool_col, _conv_w_spec(l), _pool_w_spec(l), _pool_s_spec(l), ANY],
        out_specs=[conv_col, conv_col, conv_col, conv_col, pool_col, pool_col,
                   pl.BlockSpec((None, 3, LANES), lambda j: (_conv_idx(j), 0, 0)),
                   pl.BlockSpec((None, LANES, LANES), lambda j: (_pool_idx(j), 0, 0)),
                   pl.BlockSpec((1, LANES), lambda j: (0, _pool_idx(j)))],
        out_shape=[sec] * 6 + [jax.ShapeDtypeStruct((N_MIX, 3, LANES), F32),
                               jax.ShapeDtypeStruct((N_MIX, LANES, LANES), BF16),
                               jax.ShapeDtypeStruct((1, N_MIX * LANES), F32)],
        scratch_shapes=[pltpu.VMEM((T, LANES), F32), pltpu.VMEM((T, LANES), BF16), pltpu.VMEM((T + HIST, LANES), F32)],
        compiler_params=_params(),
    )(proj, proj, proj, proj, pooled, proj, dya, dyp, wconv, wpool, pscale3, after)


def in_bwd(dsecs, wg, x, dxo, mod4, g_pre3, l, update=None):
    T, D = x.shape
    NB = N_CHIPS
    CW = wg.shape[1] // NB
    SW = dsecs[0].shape[1]
    nsec = len(dsecs)
    PW = 256
    assert SW % PW == 0 and CW % PW == 0
    tm = 256
    nt = T // tm
    n_in = nsec + 6
    n_upd = 0 if update is None else 8
    n_acc = 0 if update is None or update[3] is None else 8

    def body(*refs):
        d_refs = refs[0:nsec]
        w_ref, x_ref, dxo_ref, sh_ref, sc_ref, g_ref = refs[nsec:n_in]
        outs = refs[n_in + n_upd + n_acc:]
        dxi_ref, dw_ref, dsh_ref, dsc_ref, dg_ref = outs[0:5]
        acc_w, acc_sh, acc_q = outs[5 + n_upd:]
        i = pl.program_id(0)
        for k in range(0, n_upd, 4):
            _adamw_block(refs[n_in + k:n_in + k + 4], outs[5 + k:5 + k + 4])

        @pl.when(i == 0)
        def _():
            acc_w[...] = jnp.zeros_like(acc_w)
            acc_sh[...] = jnp.zeros_like(acc_sh)
            acc_q[...] = jnp.zeros_like(acc_q)

        xv = x_ref[...]
        r = _rms(xv)
        xh = xv * r
        sg = g_ref[...] * (1.0 + sc_ref[...])
        hb = (xh * sg + sh_ref[...]).astype(BF16)
        dh = lax.dot_general(d_refs[0][...], w_ref[:, 0:SW], NT, preferred_element_type=F32)
        for s in range(1, nsec):
            dh = dh + lax.dot_general(d_refs[s][...], w_ref[:, s * SW:(s + 1) * SW], NT, preferred_element_type=F32)
        for p in range(nsec * SW // PW):
            col = p * PW
            s, so = col // SW, col % SW
            j, jo = col // CW, col % CW
            acc_w[j, :, jo:jo + PW] += lax.dot_general(hb, d_refs[s][:, so:so + PW], TN, preferred_element_type=F32)
        q = dh * xh
        acc_sh[...] += _colsum8(dh)
        acc_q[...] += _colsum8(q)
        dxi_ref[...] = dxo_ref[...] + r * (dh * sg - xh * jnp.mean(q * sg, axis=-1, keepdims=True))

        @pl.when(i == nt - 1)
        def _():
            dw_ref[...] = acc_w[...].astype(BF16)
            sq = jnp.sum(acc_q[...], axis=0, keepdims=True)
            dsh_ref[...] = jnp.sum(acc_sh[...], axis=0, keepdims=True)
            dsc_ref[...] = g_ref[...] * sq
            dg_ref[...] = (1.0 + sc_ref[...]) * sq

    row = pl.BlockSpec((1, D), lambda i: (0, 0))
    tile = pl.BlockSpec((tm, D), lambda i: (i, 0))
    sect = pl.BlockSpec((tm, SW), lambda i: (i, 0))
    rowshape = jax.ShapeDtypeStruct((1, D), F32)
    in_specs = [sect] * nsec + [pl.BlockSpec((D, NB * CW), lambda i: (0, 0)), tile, tile,
                                _mod_row(l, 0, D), _mod_row(l, 1, D), _layer_row(l, D)]
    out_specs = [tile, pl.BlockSpec((NB, D, CW), lambda i: (0, 0, 0)), row, row, row]
    out_shape = [jax.ShapeDtypeStruct((T, D), F32), jax.ShapeDtypeStruct((NB, D, CW), BF16), rowshape, rowshape, rowshape]
    args = [*dsecs, wg, x, dxo, mod4, mod4, g_pre3]
    aliases = {}
    if update is not None:
        layer, of_w_in, of_w_out, acc = update
        for group in (of_w_in, of_w_out):
            _, rows, cols = group[0].shape
            spec = pl.BlockSpec((None, rows // nt, cols), lambda i: (layer, i, 0))
            in_specs += [spec] * 4
            out_specs += [spec] * 4
            out_shape += [jax.ShapeDtypeStruct(group[0].shape, F32)] * 4
            args += list(group)
        if acc is not None:
            aliases = {len(args) + a: 5 + a for a in range(n_acc)}
            in_specs += [ANY] * n_acc
            args += list(acc)
    return pl.pallas_call(
        body, name="in_bwd", grid=(nt,),
        in_specs=in_specs, out_specs=out_specs, out_shape=out_shape, input_output_aliases=aliases,
        scratch_shapes=[pltpu.VMEM((NB, D, CW), F32),
                        pltpu.VMEM((SUBLANES, D), F32), pltpu.VMEM((SUBLANES, D), F32)],
        compiler_params=_params(VMEM_BIG),
    )(*args)


def _rcopy(src, dst, ssem, rsem, dev):
    return pltpu.make_async_remote_copy(src_ref=src, dst_ref=dst, send_sem=ssem, recv_sem=rsem,
                                        device_id=dev, device_id_type=MESH)


def _peers7(x, y, c):
    out = []
    for m in range(1, N_DEV):
        bx, by, bc = (m >> 2) & 1, (m >> 1) & 1, m & 1
        out.append(((1 - x) if bx else x, (1 - y) if by else y, (1 - c) if bc else c))
    return out


HBM = pl.BlockSpec(memory_space=pltpu.HBM)
SEM = pl.BlockSpec(memory_space=pltpu.SEMAPHORE)
SPLIT = pltpu.CompilerParams(has_side_effects=pltpu.SideEffectType.DATAFLOW_SIDE_EFFECTING)


def _hbm(a):
    return pltpu.with_memory_space_constraint(a, pltpu.HBM)


def _chips(x, y):
    return [(1 - x, y), (x, 1 - y), (1 - x, 1 - y)]


SIBLING_BARRIER_ID = 0


def xchg_start(name, bufs, n_copies, plan, sibling_only=False, after=()):
    n = len(bufs)
    after = list(after)

    def body(*refs):
        ssem, rsem, token = refs[n + len(after)], refs[n + len(after) + 1], refs[-1]
        x, y, c = _me()
        if sibling_only:
            barrier = pltpu.get_barrier_semaphore()
            pl.semaphore_signal(barrier, inc=1, device_id=(x, y, 1 - c), device_id_type=MESH)
            pl.semaphore_wait(barrier, 1)
        copies = plan(refs[0:n], x, y, c)
        assert len(copies) == n_copies
        for k, (src, dst, peer, _) in enumerate(copies):
            _rcopy(src, dst, ssem.at[k], rsem.at[k], peer).start()
        token[...] = jnp.zeros_like(token)

    params = dict(has_side_effects=pltpu.SideEffectType.DATAFLOW_SIDE_EFFECTING)
    if sibling_only:
        params["collective_id"] = SIBLING_BARRIER_ID
    outs = pl.pallas_call(
        body, name=name,
        in_specs=[HBM] * n + [ANY] * len(after),
        out_specs=[SEM, SEM] + [HBM] * n + [pl.BlockSpec(memory_space=pltpu.VMEM)],
        out_shape=([pltpu.SemaphoreType.DMA((n_copies,))] * 2 + [pltpu.HBM(b.shape, b.dtype) for b in bufs]
                   + [jax.ShapeDtypeStruct((SUBLANES, LANES), F32)]),
        input_output_aliases={a: 2 + a for a in range(n)},
        compiler_params=pltpu.CompilerParams(**params),
    )(*[_hbm(b) for b in bufs], *after)
    return outs[0], outs[1], list(outs[2:2 + n]), outs[-1]


def xchg_wait(name, bufs, ssem, rsem, n_copies, plan, after, sems=None):
    n = len(bufs)
    after = list(after)
    sems = tuple(range(n_copies)) if sems is None else tuple(sems)
    assert len(sems) == n_copies

    def body(*refs):
        ssem_ref, rsem_ref = refs[n], refs[n + 1]
        copies = plan(refs[0:n], *_me())
        assert len(copies) == n_copies
        for k, (src, _, peer, land) in zip(sems, copies):
            cp = _rcopy(src, land, ssem_ref.at[k], rsem_ref.at[k], peer)
            cp.wait_send()
            cp.wait_recv()

    outs = pl.pallas_call(
        body, name=name,
        in_specs=[HBM] * n + [SEM, SEM] + [ANY] * len(after), out_specs=[HBM] * n,
        out_shape=[pltpu.HBM(b.shape, b.dtype) for b in bufs],
        input_output_aliases={a: a for a in range(n)},
        compiler_params=SPLIT,
    )(*bufs, ssem, rsem, *after)
    return list(outs)


def _shard_half(buf, chip, half):
    if len(buf.shape) == 2:
        h, w = buf.shape[0] // 2, buf.shape[1] // N_CHIPS
        return buf.at[pl.ds(half * h, h), pl.ds(chip * w, w)]
    h = buf.shape[1] // 2
    return buf.at[chip, pl.ds(half * h, h)]


def plan_gather(refs, x, y, c):
    out = []
    for buf in refs:
        own = _shard_half(buf, 2 * x + y, c)
        for (px, py) in _chips(x, y):
            out.append((own, own, (px, py, c), _shard_half(buf, 2 * px + py, c)))
    return out


def plan_forward(refs, x, y, c):
    out = []
    for (px, py) in _chips(x, y):
        for buf in refs:
            landed = _shard_half(buf, 2 * px + py, c)
            out.append((landed, landed, (x, y, 1 - c), _shard_half(buf, 2 * px + py, 1 - c)))
    return out


def plan_sibling(refs, x, y, c):
    n = len(refs) // 2
    out = []
    for a in range(n):
        h = refs[a].shape[1] // 2
        out.append((refs[a].at[:, pl.ds((1 - c) * h, h)], refs[n + a], (x, y, 1 - c), refs[n + a]))
    return out


def plan_chip(refs, x, y, c):
    n = len(refs) // 2
    out = []
    for j, (px, py) in enumerate(_chips(x, y)):
        for a in range(n):
            out.append((refs[a].at[2 * px + py], refs[n + a].at[j], (px, py, c), refs[n + a].at[j]))
    return out


def plan_mod(refs, x, y, c):
    (mods,) = refs
    mine = mods.at[2 * x + y]
    return [(mine, mine, (px, py, c), mods.at[2 * px + py]) for (px, py) in _chips(x, y)]


def plan_pack(refs, x, y, c):
    (packs,) = refs
    mine = packs.at[4 * x + 2 * y + c]
    return [(mine, mine, peer, packs.at[4 * peer[0] + 2 * peer[1] + peer[2]]) for peer in _peers7(x, y, c)]


def plan_spread(layers, wp_layers):
    def plan(refs, x, y, c):
        gi, go, gp = refs
        hD, hR, hP = gi.shape[1] // 2, go.shape[1] // 2, gp.shape[2] // 2
        sib = (x, y, 1 - c)
        out = []
        for l in layers:
            mine = gi.at[l, pl.ds(c * hD, hD)]
            out.append((mine, mine, sib, gi.at[l, pl.ds((1 - c) * hD, hD)]))
            mine = go.at[l, pl.ds(c * hR, hR)]
            out.append((mine, mine, sib, go.at[l, pl.ds((1 - c) * hR, hR)]))
        for l in wp_layers:
            mine = gp.at[l, 2 * x + y, pl.ds(c * hP, hP)]
            for peer in _peers7(x, y, c):
                out.append((mine, mine, peer, gp.at[l, 2 * peer[0] + peer[1], pl.ds(peer[2] * hP, hP)]))
        return out

    return plan


def place_small(pos, c8, wc):
    L = wc.shape[0]

    def body(pos_ref, c_ref, wc_ref, call_ref, wcall_ref):
        call_ref[...] = c_ref[...]
        wcall_ref[...] = wc_ref[...]

    return pl.pallas_call(
        body, name="place_small",
        grid_spec=pltpu.PrefetchScalarGridSpec(
            num_scalar_prefetch=1, grid=(1,),
            in_specs=[pl.BlockSpec((SUBLANES, LANES), lambda i, p: (0, 0)),
                      pl.BlockSpec((L, 3, LANES), lambda i, p: (0, 0, 0))],
            out_specs=[pl.BlockSpec((None, SUBLANES, LANES), lambda i, p: (p[2], 0, 0)),
                       pl.BlockSpec((None, L, 3, LANES), lambda i, p: (p[1], 0, 0, 0))]),
        out_shape=[jax.ShapeDtypeStruct((N_DEV, SUBLANES, LANES), F32),
                   jax.ShapeDtypeStruct((N_CHIPS, L, 3, LANES), F32)],
        compiler_params=_params(),
    )(pos, c8, wc)


def plan_small(refs, x, y, c):
    call, wcall = refs
    mine = call.at[4 * x + 2 * y + c]
    out = [(mine, mine, peer, call.at[4 * peer[0] + 2 * peer[1] + peer[2]]) for peer in _peers7(x, y, c)]
    mine = wcall.at[2 * x + y]
    out += [(mine, mine, (px, py, c), wcall.at[2 * px + py]) for (px, py) in _chips(x, y)]
    return out


def add_sibling(cidx, mine, sib):
    def body(c_ref, *refs):
        for a in range(3):
            m, s, o = refs[a], refs[3 + a], refs[6 + a]
            o[...] = (m[...].astype(F32) + s[...].astype(F32)).astype(BF16)

    per_step = 2

    def mine_spec(a):
        h = a.shape[1] // 2
        return pl.BlockSpec((per_step, h, a.shape[2]), lambda j, c_ref: (j, c_ref[0], 0))

    def sib_spec(a):
        return pl.BlockSpec((per_step,) + a.shape[1:], lambda j, c_ref: (j, 0, 0))

    return pl.pallas_call(
        body, name="add_sibling",
        grid_spec=pltpu.PrefetchScalarGridSpec(
            num_scalar_prefetch=1, grid=(N_CHIPS // per_step,),
            in_specs=[mine_spec(a) for a in mine] + [sib_spec(a) for a in sib],
            out_specs=[sib_spec(a) for a in sib]),
        out_shape=[jax.ShapeDtypeStruct(a.shape, BF16) for a in sib],
        compiler_params=_params(VMEM_BIG),
    )(cidx, *mine, *sib)


def sum_chips(pos, own, rb, acc, l, shapes):
    nq = 2
    n_in = 6 + (3 if acc is not None else 0)

    def body(pos_ref, *refs):
        for a in range(3):
            m, b, o = refs[a], refs[3 + a], refs[n_in + a]
            s = m[...].astype(F32)
            for j in range(3):
                s = s + b[j].astype(F32)
            o[...] = s

    def own_spec(a):
        return pl.BlockSpec((None, a.shape[1] // nq, a.shape[2]), lambda q, p: (p[1], q, 0))

    def rb_spec(a):
        return pl.BlockSpec((3, a.shape[1] // nq, a.shape[2]), lambda q, p: (0, q, 0))

    hi, ho, hp = own[0].shape[1] // nq, own[1].shape[1] // nq, own[2].shape[1] // nq
    out_specs = [pl.BlockSpec((None, hi, shapes[0][2]), lambda q, p: (l, p[0] * nq + q, 0)),
                 pl.BlockSpec((None, ho, shapes[1][2]), lambda q, p: (l, p[0] * nq + q, 0)),
                 pl.BlockSpec((None, None, hp, LANES), lambda q, p: (l, p[1], p[0] * nq + q, 0))]
    in_specs = [own_spec(a) for a in own] + [rb_spec(a) for a in rb]
    args = list(own) + list(rb)
    aliases = {}
    if acc is not None:
        in_specs += [ANY] * 3
        args += list(acc)
        aliases = {7: 0, 8: 1, 9: 2}
    return pl.pallas_call(
        body, name="sum_chips",
        grid_spec=pltpu.PrefetchScalarGridSpec(num_scalar_prefetch=1, grid=(nq,), in_specs=in_specs, out_specs=out_specs),
        out_shape=[jax.ShapeDtypeStruct(s, F32) for s in shapes],
        input_output_aliases=aliases,
        compiler_params=_params(VMEM_BIG),
    )(pos, *args)


def _wconv_slot(chip, tap):
    idx = 3 * chip + tap
    return ROW_WCONV + idx // SUBLANES, slice((idx % SUBLANES) * LANES, (idx % SUBLANES + 1) * LANES)


def pack_small(pos, per_layer, loss_blk):
    L = len(per_layer)
    D = per_layer[0][0].shape[1]

    def body(pos_ref, *refs):
        o = refs[-1]
        lb = refs[-2]
        o[...] = jnp.zeros_like(o)
        for l in range(L):
            dgpre, dgpost, dsh, dsc, dgt, dps, dwc = refs[7 * l:7 * l + 7]
            base = SUBLANES * l
            o[pl.ds(base + ROW_G_PRE, 1), :] = dgpre[...]
            o[pl.ds(base + ROW_G_POST, 1), :] = dgpost[...]
            for r, src in enumerate((dsh, dsc, dgt)):
                o[pl.ds(base + ROW_MOD + r, 1), :] = src[...]
            o[pl.ds(base + ROW_PSCALE, 1), 0:dps.shape[1]] = dps[...]
            for j in range(dwc.shape[0]):
                for k in range(3):
                    row, lanes = _wconv_slot(j, k)
                    o[pl.ds(base + row, 1), lanes] = dwc[j, pl.ds(k, 1), :]
        o[pl.ds(ROW_PSCALE, 1), LOSS_LANES] = lb[pl.ds(0, 1), :]

    flat = [a for layer in per_layer for a in layer] + [loss_blk]

    def whole(a):
        return pl.BlockSpec(a.shape, lambda i, p: (0,) * a.ndim)

    return pl.pallas_call(
        body, name="pack_small",
        grid_spec=pltpu.PrefetchScalarGridSpec(
            num_scalar_prefetch=1, grid=(1,), in_specs=[whole(a) for a in flat],
            out_specs=pl.BlockSpec((None, L * SUBLANES, D), lambda i, p: (p[2], 0, 0))),
        out_shape=jax.ShapeDtypeStruct((N_DEV, L * SUBLANES, D), F32),
        compiler_params=_params(),
    )(pos, *flat)


def small_update(pos, packs, params, moments_m, moments_v):
    n = len(params)
    L, D = params[1].shape
    PS = params[3].shape[1]

    def body(pos_ref, p_ref, *refs):
        ws, ms, vs = refs[0:n], refs[n:2 * n], refs[2 * n:3 * n]
        loss_ref = refs[3 * n]
        outs = [refs[3 * n + 1 + 4 * t:3 * n + 5 + 4 * t] for t in range(n)]
        summed = refs[-1]
        s = p_ref[0]
        for d in range(1, N_DEV):
            s = s + p_ref[d]
        summed[...] = s
        loss_ref[...] = summed[pl.ds(ROW_PSCALE, 1), LOSS_LANES]
        chip = pos_ref[1]

        def update(t, idx, g):
            d, mm, vv = _adamw_math(ws[t][idx], g, ms[t][idx], vs[t][idx])
            g_ref, d_ref, mo_ref, vo_ref = outs[t]
            g_ref[idx] = g
            d_ref[idx] = d
            mo_ref[idx] = mm
            vo_ref[idx] = vv

        for l in range(L):
            base = SUBLANES * l
            row = pl.ds(l, 1)
            for k in range(3):
                update(0, (row, slice(k * D, (k + 1) * D)), summed[pl.ds(base + ROW_MOD + k, 1), :])
            update(1, (row, slice(None)), summed[pl.ds(base + ROW_G_PRE, 1), :])
            update(2, (row, slice(None)), summed[pl.ds(base + ROW_G_POST, 1), :])
            update(3, (row, slice(None)), summed[pl.ds(base + ROW_PSCALE, 1), 0:PS])
            for k in range(3):
                g = None
                for j in range(N_CHIPS):
                    wrow, lanes = _wconv_slot(j, k)
                    cand = summed[pl.ds(base + wrow, 1), lanes]
                    g = cand if g is None else jnp.where(chip == j, cand, g)
                update(4, (l, pl.ds(k, 1), slice(None)), g)

    def whole(a):
        return pl.BlockSpec(a.shape, lambda i, p: (0,) * a.ndim)

    ins = [packs] + list(params) + list(moments_m) + list(moments_v)
    out_shape = [jax.ShapeDtypeStruct((1, LANES), F32)]
    for w in params:
        out_shape += [jax.ShapeDtypeStruct(w.shape, F32)] * 4
    outs = pl.pallas_call(
        body, name="small_update",
        grid_spec=pltpu.PrefetchScalarGridSpec(
            num_scalar_prefetch=1, grid=(1,), in_specs=[whole(a) for a in ins],
            out_specs=[whole(a) for a in out_shape],
            scratch_shapes=[pltpu.VMEM(packs.shape[1:], F32)]),
        out_shape=out_shape,
        compiler_params=_params(),
    )(pos, *ins)
    return outs[0], [outs[1 + 4 * t:5 + 4 * t] for t in range(n)]


def _adamw_math(w, g, m, v):
    m = ADAM_B1 * m + (1.0 - ADAM_B1) * g
    v = ADAM_B2 * v + (1.0 - ADAM_B2) * (g * g)
    m_hat = m / (1.0 - ADAM_B1 ** ADAM_STEP)
    v_hat = v / (1.0 - ADAM_B2 ** ADAM_STEP)
    delta = -ADAM_LR * (m_hat / (jnp.sqrt(v_hat) + ADAM_EPS) + ADAM_WD * w)
    return delta, m, v


def _adamw_block(ins, outs):
    w_ref, g_ref, m_ref, v_ref = ins
    go_ref, d_ref, mo_ref, vo_ref = outs
    gv = g_ref[...]
    d, mm, vv = _adamw_math(w_ref[...], gv, m_ref[...], v_ref[...])
    go_ref[...] = gv
    d_ref[...] = d
    mo_ref[...] = mm
    vo_ref[...] = vv


def adamw(groups, name, first, count, steps, acc=None):
    n = len(groups)

    def body(*refs):
        outs = refs[len(refs) - 4 * n:]
        for k in range(n):
            _adamw_block(refs[4 * k:4 * k + 4], outs[4 * k:4 * k + 4])

    specs, out_shape, args = [], [], []
    for group in groups:
        shape = group[0].shape
        spec = pl.BlockSpec((1, shape[1] // steps) + shape[2:],
                            lambda i, s, rest=(0,) * (len(shape) - 2): (first + i, s) + rest)
        specs += [spec] * 4
        out_shape += [jax.ShapeDtypeStruct(shape, F32)] * 4
        args += list(group)
    extra = [] if acc is None else list(acc)
    return pl.pallas_call(
        body, name=name, grid=(count, steps),
        in_specs=specs + [ANY] * len(extra), out_specs=specs, out_shape=out_shape,
        input_output_aliases={4 * n + a: a for a in range(len(extra))},
        compiler_params=_params(VMEM_BIG, n_grid=2),
    )(*args, *extra)


def ada_finish(c_all, dmod, w, m, v):
    L, D, CW = w.shape
    hD = D // 2

    def body(c_ref, d_ref, w_ref, m_ref, v_ref, g_ref, dl_ref, mo_ref, vo_ref):
        cv = c_ref[...]
        z = jnp.zeros_like(cv)
        ca = jnp.concatenate([cv * jax.nn.sigmoid(cv), z], axis=0).astype(BF16)
        dm = jnp.concatenate([d_ref[0], jnp.zeros_like(d_ref[0])], axis=0).astype(BF16)
        g = lax.dot_general(ca, dm, TN, preferred_element_type=F32)
        g_ref[0] = g
        d, mm, vv = _adamw_math(w_ref[0], g, m_ref[0], v_ref[0])
        dl_ref[0] = d
        mo_ref[0] = mm
        vo_ref[0] = vv

    big = pl.BlockSpec((1, hD, CW), lambda l, h: (l, h, 0))
    shape = jax.ShapeDtypeStruct(w.shape, F32)
    return pl.pallas_call(
        body, name="ada_finish", grid=(L, 2),
        in_specs=[pl.BlockSpec((N_DEV, hD), lambda l, h: (0, h)), pl.BlockSpec((1, N_DEV, CW), lambda l, h: (l, 0, 0)),
                  big, big, big],
        out_specs=[big] * 4, out_shape=[shape] * 4,
        compiler_params=_params(VMEM_BIG, n_grid=2),
    )(c_all, dmod, w, m, v)


def kernel(x, c, w_ada, b_ada, g_pre, w_in, w_conv, w_pool, pool_scale, w_out, g_post, loss_target, m_w_ada, m_b_ada, m_g_pre, m_w_in, m_w_conv, m_w_pool, m_pool_scale, m_w_out, m_g_post, v_w_ada, v_b_ada, v_g_pre, v_w_in, v_w_conv, v_w_pool, v_pool_scale, v_w_out, v_g_post):
    L, D, CW = w_in.shape
    RO = w_out.shape[1]
    T = x.shape[1]
    ix, iy, ic = _me()
    chip = 2 * ix + iy
    me_lin = 4 * ix + 2 * iy + ic

    pos = jnp.stack([ic, chip, me_lin]).astype(jnp.int32)
    n_s, n_c = 3, 9

    def gather(bufs, after):
        ss, rs, bufs, tok = xchg_start("gather_start", bufs, 3 * len(bufs), plan_gather, after=after)
        return (ss, rs, bufs), tok

    def ready(flight, after):
        fss, frs, bufs = flight
        return xchg_wait("forward_wait", bufs, fss, frs, 3 * len(bufs), plan_forward, after)

    def arrive_part(flight, which, after):
        ss, rs, bufs = flight
        sems = tuple(range(3 * which, 3 * which + 3))
        (buf,) = xchg_wait("gather_wait", [bufs[which]], ss, rs, 3, plan_gather, after, sems=sems)
        fss, frs, (buf,), tok = xchg_start("forward_start", [buf], 3, plan_forward, sibling_only=True)
        return (fss, frs, [buf]), tok

    n_small = N_DEV - 1 + N_CHIPS - 1
    s_ss, s_rs, smalls_in, token = xchg_start("small_start", list(place_small(pos, c.reshape(SUBLANES, LANES), w_conv)),
                                              n_small, plan_small)
    w_in_of, w_out_of = [None] * L, [None] * L
    gi0, go0 = cast_weights(pos, w_in, w_out, 0, token)
    flight, token = gather([gi0], [])
    w_in_of[0] = (flight, 0)
    g_pre_l, g_post_l, pscale_l, b_ada_l, token = lax.optimization_barrier((g_pre, g_post, pool_scale, b_ada, token))
    g_pre3, g_post3 = g_pre_l.reshape(L, 1, D), g_post_l.reshape(L, 1, D)
    pscale3 = pscale_l.reshape(L, 1, pool_scale.shape[1])
    c_all3, wconv_all = xchg_wait("small_wait", smalls_in, s_ss, s_rs, n_small, plan_small, [token])
    c_all = c_all3.reshape(N_DEV, D)
    b_my = lax.dynamic_slice_in_dim(b_ada_l, chip * CW, CW, axis=1)
    m_ss, m_rs, mods, token = xchg_start("mod_start", [mod_part(pos, c_all, w_ada, b_my, token)], 3, plan_mod)
    gi1, go1 = cast_weights(pos, w_in, w_out, 1, token)
    flight, token = gather([gi1, go0, go1], [])
    w_in_of[1], w_out_of[0], w_out_of[1] = (flight, 0), (flight, 1), (flight, 2)
    late = []
    for l in range(2, L):
        late += list(cast_weights(pos, w_in, w_out, l, token))
    flight, token = gather(late, [])
    for l in range(2, L):
        w_in_of[l], w_out_of[l] = (flight, 2 * (l - 2)), (flight, 2 * (l - 2) + 1)
    fwd_in, token = arrive_part(*w_in_of[0], [token])
    (mod_all,) = xchg_wait("mod_wait", mods, m_ss, m_rs, 3, plan_mod, [token])
    mod = lax.dynamic_index_in_dim(mod_all, me_lin, axis=2, keepdims=False)
    mod4 = jnp.transpose(mod, (1, 0, 2)).reshape(L, 3, 1, D)

    xs, projs, yas, yps, ys, pooleds = [x.reshape(T, D)], [], [], [], [], []
    wg_in, wg_out = [], []
    for l in range(L):
        (gi,) = ready(fwd_in, [mod4 if l == 0 else xs[l]])
        proj = proj_fwd(xs[l], mod4, g_pre3, gi, l)
        ya, yp, pooled = mix_fwd(proj, wconv_all, w_pool, pscale3, l)
        pooleds.append(pooled)
        fwd_out, token = arrive_part(*w_out_of[l], [ya, yp])
        after = [token]
        if l + 1 < L:
            fwd_in, token = arrive_part(*w_in_of[l + 1], after)
            after = [token]
        (go,) = ready(fwd_out, after)
        wg_in.append(gi)
        wg_out.append(go.reshape(N_CHIPS * RO, D))
        projs.append(proj)
        yas.append(ya)
        yps.append(yp)
        if l + 1 < L:
            xn, yv = out_fwd(ya, yp, wg_out[l], xs[l], mod4, g_post3, l, after[0])
            xs.append(xn)
        else:
            dx, yv, loss_blk = out_fwd_loss(ya, yp, wg_out[l], xs[l], mod4, g_post3, l, loss_target.reshape(T, D))
        ys.append(yv)

    shapes = (w_in.shape, w_out.shape, w_pool.shape)
    smalls = [None] * L
    acc, flying, sib, token = None, None, None, loss_blk

    def to_chips(sib, after):
        sl, s_ss, s_rs, s_bufs = sib
        s_bufs = xchg_wait("sibling_wait", s_bufs, s_ss, s_rs, n_s, plan_sibling, after)
        chip_parts = add_sibling(pos, s_bufs[0:3], s_bufs[3:6])
        lands = [lax.empty((3,) + a.shape[1:], a.dtype) for a in chip_parts]
        c_ss, c_rs, c_bufs, ctoken = xchg_start("chip_start", list(chip_parts) + lands, n_c, plan_chip)
        return (sl, c_ss, c_rs, c_bufs), ctoken

    def landed(flying, acc, after):
        fl, f_ss, f_rs, f_bufs = flying
        f_bufs = xchg_wait("chip_wait", f_bufs, f_ss, f_rs, n_c, plan_chip, after)
        return sum_chips(pos, f_bufs[0:3], f_bufs[3:6], acc, fl, shapes)

    early = None
    for l in reversed(range(L)):
        dya, dyp, dwo_l, dgate, dgpost = out_bwd(dx, ys[l], yas[l], yps[l], wg_out[l], mod4, g_post3, l, token)
        token = dya
        spreading = None
        if sib is not None:
            arrived = flying
            flying, token = to_chips(sib, [dya])
            if arrived is not None:
                acc = landed(arrived, acc, [token])
                spreading = plan_spread((arrived[0],), ())
                sp_ss, sp_rs, acc, token = xchg_start("spread_start", list(acc), 2, spreading, sibling_only=True)
        du_a, db_a, dc_a, dg_a, du_p, dg_p, dwc, dwp_l, dps = mix_bwd(projs[l], pooleds[l], dya, dyp, wconv_all, w_pool,
                                                                        pscale3, l, token)
        update = None
        if spreading is not None:
            acc = xchg_wait("spread_wait", acc, sp_ss, sp_rs, 2, spreading, [du_a])
            update = (arrived[0], [w_in, acc[0], m_w_in, v_w_in], [w_out, acc[1], m_w_out, v_w_out], early)
        dx, dwi_l, dshift, dscale, dgpre, *rest = in_bwd([du_a, db_a, dc_a, dg_a, du_p, dg_p], wg_in[l], xs[l], dx,
                                                         mod4, g_pre3, l, update)
        early = rest if rest else early
        smalls[l] = (dgpre, dgpost, dshift, dscale, dgate, dps, dwc)
        parts = [dwi_l, dwo_l.reshape(N_CHIPS, RO, D), dwp_l]
        s_lands = [lax.empty((a.shape[0], a.shape[1] // 2) + a.shape[2:], a.dtype) for a in parts]
        s_ss, s_rs, s_bufs, token = xchg_start("sibling_start", parts + s_lands, n_s, plan_sibling, sibling_only=True)
        sib = (l, s_ss, s_rs, s_bufs)
    grad_x = dx.reshape(1, T, D)

    p_ss, p_rs, packs, ptoken = xchg_start("pack_start", [pack_small(pos, smalls, loss_blk)], N_DEV - 1, plan_pack)
    acc = landed(flying, acc, [ptoken, token])
    n_sp = 2 + (N_DEV - 1) * (L - 1)
    spread = plan_spread((1,), tuple(range(1, L)))
    sp_ss, sp_rs, acc, sp_token = xchg_start("spread_start", list(acc), n_sp, spread)
    flying, token = to_chips(sib, [sp_token])
    (packs_all,) = xchg_wait("pack_wait", packs, p_ss, p_rs, N_DEV - 1, plan_pack, [token])
    dmod_all = packs_all.reshape(N_DEV, L, SUBLANES, D)[:, :, ROW_MOD:ROW_MOD + 3].reshape(N_DEV, L, 3 * D)
    dmod_my = jnp.transpose(lax.dynamic_slice_in_dim(dmod_all, chip * CW, CW, axis=2), (1, 0, 2))

    g_w_ada, d_w_ada, nm_w_ada, nv_w_ada = ada_finish(c_all, dmod_my, w_ada, m_w_ada, v_w_ada)
    m_w_conv_l, v_w_conv_l, _ = lax.optimization_barrier((m_w_conv, v_w_conv, yas[0]))
    loss_row, upd = small_update(pos, packs_all, [b_ada, g_pre, g_post, pool_scale, w_conv],
                                 [m_b_ada, m_g_pre, m_g_post, m_pool_scale, m_w_conv_l],
                                 [v_b_ada, v_g_pre, v_g_post, v_pool_scale, v_w_conv_l])
    loss = loss_row[0, 0]
    (g_b_ada, d_b_ada, nm_b_ada, nv_b_ada), (g_g_pre, d_g_pre, nm_g_pre, nv_g_pre) = upd[0], upd[1]
    (g_g_post, d_g_post, nm_g_post, nv_g_post), (g_pscale, d_pscale, nm_pscale, nv_pscale) = upd[2], upd[3]
    g_w_conv, d_w_conv, nm_w_conv, nv_w_conv = upd[4]

    done = [nv_w_ada, nv_w_conv]
    g_w_in, g_w_out, g_w_pool = xchg_wait("spread_wait", acc, sp_ss, sp_rs, n_sp, spread, done)
    early = adamw([[w_in, g_w_in, m_w_in, v_w_in], [w_out, g_w_out, m_w_out, v_w_out]], "adamw_layer", 1, 1, 2, early)

    acc = landed(flying, (g_w_in, g_w_out, g_w_pool), [early[3], early[7]])
    last = plan_spread((0,), (0,))
    n_last = 2 + N_DEV - 1
    l_ss, l_rs, acc, _ = xchg_start("spread_start", list(acc), n_last, last)
    upd_pool = adamw([[w_pool, acc[2], m_w_pool, v_w_pool]], "adamw_w_pool", 1, L - 1, 1)
    r_w_in, r_w_out, r_w_pool = xchg_wait("spread_wait", acc, l_ss, l_rs, n_last, last, [upd_pool[3]])
    (g_w_in, d_w_in, nm_w_in, nv_w_in, g_w_out, d_w_out, nm_w_out, nv_w_out,
     g_w_pool, d_w_pool, nm_w_pool, nv_w_pool) = adamw(
         [[w_in, r_w_in, m_w_in, v_w_in], [w_out, r_w_out, m_w_out, v_w_out], [w_pool, r_w_pool, m_w_pool, v_w_pool]],
         "adamw_layer", 0, 1, 2, list(early) + list(upd_pool))

    return (loss, grad_x,
            g_w_ada, g_b_ada, g_g_pre, g_w_in, g_w_conv, g_w_pool, g_pscale, g_w_out, g_g_post,
            d_w_ada, d_b_ada, d_g_pre, d_w_in, d_w_conv, d_w_pool, d_pscale, d_w_out, d_g_post,
            nm_w_ada, nm_b_ada, nm_g_pre, nm_w_in, nm_w_conv, nm_w_pool, nm_pscale, nm_w_out, nm_g_post,
            nv_w_ada, nv_b_ada, nv_g_pre, nv_w_in, nv_w_conv, nv_w_pool, nv_pscale, nv_w_out, nv_g_post)
```

```python
import functools

import jax
import jax.numpy as jnp
from jax import lax
from jax.experimental import pallas as pl
from jax.experimental.pallas import tpu as pltpu

F32 = jnp.float32
BF16 = jnp.bfloat16
MESH = pl.DeviceIdType.MESH
ANY = pl.BlockSpec(memory_space=pl.ANY)

NORM_EPS = 1e-6
POOL_WINDOWS = (2, 4, 8, 16)
ADAM_LR = 0.001
ADAM_B1 = 0.9
ADAM_B2 = 0.999
ADAM_EPS = 1e-08
ADAM_WD = 0.01
ADAM_STEP = 10

N_CHIPS = 4
N_DEV = 8
LANES = 128
SUBLANES = 8
VMEM_BIG = 56 * 1024 * 1024
HIST = 16
R_CONV = 64
R_POOL = 128

ROW_G_PRE, ROW_G_POST, ROW_MOD, ROW_PSCALE, ROW_WCONV = 0, 1, 2, 5, 6
LOSS_LANES = slice(4 * LANES, 5 * LANES)

NT = (((1,), (1,)), ((), ()))
TN = (((0,), (0,)), ((), ()))


def _params(vmem=None, n_grid=1):
    kw = {}
    if n_grid:
        kw["dimension_semantics"] = ("arbitrary",) * n_grid
    if vmem is not None:
        kw["vmem_limit_bytes"] = vmem
    return pltpu.CompilerParams(**kw)


def _colsum8(v):
    n, d = v.shape
    return v.reshape(n // SUBLANES, SUBLANES, d).sum(axis=0)


def _rms(v):
    return lax.rsqrt(jnp.mean(v * v, axis=-1, keepdims=True) + NORM_EPS)


def _sigmoid(v):
    return 0.5 * jnp.tanh(0.5 * v) + 0.5


def _shift_down(ext, k, rows):
    if k == 0:
        return ext[HIST:HIST + rows]
    return pltpu.roll(ext, k, 0)[HIST:HIST + rows]


def _shift_up(ext, k, rows):
    if k == 0:
        return ext[0:rows]
    return pltpu.roll(ext, ext.shape[0] - k, 0)[0:rows]


def _load_ext(ref, r0, h0, first, rows):
    hist = ref[pl.ds(h0, HIST), :].astype(F32)
    hist = jnp.where(first, 0.0, hist)
    cur = ref[pl.ds(r0, rows), :].astype(F32)
    return jnp.concatenate([hist, cur], axis=0)


def _me():
    return lax.axis_index("x"), lax.axis_index("y"), lax.axis_index("c")


def cast_weights(pos, w_in, w_out, l, after):
    _, D, CW = w_in.shape
    RO = w_out.shape[1]

    def body(pos_ref, wi, wo, after_ref, oi, oo):
        oi[...] = wi[...].astype(BF16)
        oo[...] = wo[...].astype(BF16)

    return pl.pallas_call(
        body, name="cast_w",
        grid_spec=pltpu.PrefetchScalarGridSpec(
            num_scalar_prefetch=1, grid=(2,),
            in_specs=[pl.BlockSpec((None, D // 2, CW), lambda h, p: (l, h, 0)),
                      pl.BlockSpec((None, RO // 2, D), lambda h, p: (l, h, 0)), ANY],
            out_specs=[pl.BlockSpec((D // 2, CW), lambda h, p: (h, p[1])),
                       pl.BlockSpec((None, RO // 2, D), lambda h, p: (p[1], h, 0))]),
        out_shape=[jax.ShapeDtypeStruct((D, N_CHIPS * CW), BF16), jax.ShapeDtypeStruct((N_CHIPS, RO, D), BF16)],
        compiler_params=_params(),
    )(pos, w_in, w_out, after)


def mod_part(pos, c_all, w_ada, b_my, after):
    L, D, CW = w_ada.shape

    def body(pos_ref, c_ref, w_ref, b_ref, after_ref, o_ref):
        cv = c_ref[...]
        ca = (cv * jax.nn.sigmoid(cv)).astype(BF16)
        o_ref[...] = jnp.dot(ca, w_ref[0].astype(BF16), preferred_element_type=F32) + b_ref[0]

    return pl.pallas_call(
        body, name="mod_part",
        grid_spec=pltpu.PrefetchScalarGridSpec(
            num_scalar_prefetch=1, grid=(L,),
            in_specs=[pl.BlockSpec((N_DEV, D), lambda l, p: (0, 0)),
                      pl.BlockSpec((1, D, CW), lambda l, p: (l, 0, 0)),
                      pl.BlockSpec((1, 1, CW), lambda l, p: (l, 0, 0)), ANY],
            out_specs=pl.BlockSpec((None, None, N_DEV, CW), lambda l, p: (p[1], l, 0, 0))),
        out_shape=jax.ShapeDtypeStruct((N_CHIPS, L, N_DEV, CW), F32),
        compiler_params=_params(VMEM_BIG),
    )(pos, c_all, w_ada, b_my.reshape(L, 1, CW), after)


def _mod_row(l, k, D):
    return pl.BlockSpec((None, None, 1, D), lambda *_: (l, k, 0, 0))


def _layer_row(l, D):
    return pl.BlockSpec((None, 1, D), lambda *_: (l, 0, 0))


def proj_fwd(x, mod4, g_pre3, wg, l):
    T, D = x.shape
    NC = wg.shape[1]
    NB = N_CHIPS
    CW = NC // NB
    tm = 512

    def body(x_ref, sh_ref, sc_ref, g_ref, w_ref, o_ref):
        xv = x_ref[...]
        h = (xv * _rms(xv)) * (g_ref[...] * (1.0 + sc_ref[...])) + sh_ref[...]
        hb = h.astype(BF16)
        for j in range(NB):
            cols = slice(j * CW, (j + 1) * CW)
            o_ref[:, cols] = jnp.dot(hb, w_ref[:, cols], preferred_element_type=F32).astype(BF16)

    return pl.pallas_call(
        body, name="proj_fwd", grid=(T // tm,),
        in_specs=[pl.BlockSpec((tm, D), lambda i: (i, 0)), _mod_row(l, 0, D), _mod_row(l, 1, D), _layer_row(l, D),
                  pl.BlockSpec((D, NC), lambda i: (0, 0))],
        out_specs=pl.BlockSpec((tm, NC), lambda i: (i, 0)),
        out_shape=jax.ShapeDtypeStruct((T, NC), BF16),
        compiler_params=_params(VMEM_BIG),
    )(x, mod4, mod4, g_pre3, wg)


N_MIX = 4


def _conv_fwd_block(u_ref, b_ref, c_ref, g_ref, w_ref, o_ref):
    T = u_ref.shape[0]
    R = 2 * R_CONV
    w0 = w_ref[pl.ds(0, 1), :]
    w1 = w_ref[pl.ds(1, 1), :]
    w2 = w_ref[pl.ds(2, 1), :]

    def chunk(i, carry):
        r0 = pl.multiple_of(i * R, R)
        h0 = pl.multiple_of(jnp.maximum(r0 - HIST, 0), HIST)
        first = i == 0
        ca = _load_ext(c_ref, r0, h0, first, R) * _load_ext(u_ref, r0, h0, first, R)
        conv = w2 * ca[HIST:] + w1 * _shift_down(ca, 1, R) + w0 * _shift_down(ca, 2, R)
        g = g_ref[pl.ds(r0, R), :].astype(F32)
        b = b_ref[pl.ds(r0, R), :].astype(F32)
        o_ref[pl.ds(r0, R), :] = (b * conv * (g * _sigmoid(g))).astype(BF16)
        return carry

    lax.fori_loop(0, T // R, chunk, 0)


def _conv_idx(j):
    return jnp.minimum(j, N_MIX - 1)


def _pool_idx(j):
    return jnp.maximum(j - N_MIX, 0)


def _proj_col(T, off, idx):
    return pl.BlockSpec((T, LANES), lambda j: (0, idx(j) + off))


def _causal_window_sum(ext, w):
    s, k = ext, 1
    while k < w:
        s = s + pltpu.roll(s, k, 0)
        k *= 2
    return s


def _anticausal_window_sum(ext, w):
    s, k = ext, 1
    n = ext.shape[0]
    while k < w:
        s = s + pltpu.roll(s, n - k, 0)
        k *= 2
    return s


def _count(r0, rows, w):
    t = r0 + lax.broadcasted_iota(jnp.int32, (rows, LANES), 0)
    return jnp.minimum(t + 1, w).astype(F32)


def _pooled_loop(p_ref, pooled_s, w, T):
    R = R_POOL

    def chunk(i, carry):
        r0 = pl.multiple_of(i * R, R)
        h0 = pl.multiple_of(jnp.maximum(r0 - HIST, 0), HIST)
        ext = _load_ext(p_ref, r0, h0, i == 0, R)
        ws = _causal_window_sum(ext, w)[HIST:]
        pooled_s[pl.ds(r0, R), :] = (ws / _count(r0, R, w) - ext[HIST:]).astype(BF16)
        return carry

    lax.fori_loop(0, T // R, chunk, 0)


def _conv_w_spec(l):
    return pl.BlockSpec((None, None, 3, LANES), lambda j: (_conv_idx(j), l, 0, 0))


def _pool_w_spec(l):
    return pl.BlockSpec((None, None, LANES, LANES), lambda j: (l, _pool_idx(j), 0, 0))


def _pool_s_spec(l):
    return pl.BlockSpec((None, 1, LANES), lambda j: (l, 0, _pool_idx(j)))


def _pool_fwd_group(p_ref, g_ref, w_ref, s_ref, o_ref, pooled_s, mixed_s, w):
    T = p_ref.shape[0]
    R = R_POOL
    _pooled_loop(p_ref, pooled_s, w, T)
    mixed_s[...] = jnp.dot(pooled_s[...], w_ref[...].astype(BF16), preferred_element_type=F32)
    sc = s_ref[...]

    def chunk(i, carry):
        r0 = pl.multiple_of(i * R, R)
        g = g_ref[pl.ds(r0, R), :].astype(F32)
        o_ref[pl.ds(r0, R), :] = (mixed_s[pl.ds(r0, R), :] * sc * (g * _sigmoid(g))).astype(BF16)
        return carry

    lax.fori_loop(0, T // R, chunk, 0)


def mix_fwd(proj, wconv, wpool, pscale3, l):
    T = proj.shape[0]

    def body(u_ref, b_ref, c_ref, g_ref, p_ref, gp_ref, wc_ref, wp_ref, s_ref, ya_ref, yp_ref, pooled_ref, mixed_s):
        j = pl.program_id(0)
        pl.when(j < N_MIX)(functools.partial(_conv_fwd_block, u_ref, b_ref, c_ref, g_ref, wc_ref, ya_ref))
        for k, w in enumerate(POOL_WINDOWS):
            pl.when(j == N_MIX + k)(functools.partial(_pool_fwd_group, p_ref, gp_ref, wp_ref, s_ref, yp_ref,
                                                      pooled_ref, mixed_s, w))

    half = jax.ShapeDtypeStruct((T, N_MIX * LANES), BF16)
    pool_col = pl.BlockSpec((T, LANES), lambda j: (0, _pool_idx(j)))
    return pl.pallas_call(
        body, name="mix_fwd", grid=(2 * N_MIX,),
        in_specs=[_proj_col(T, 0, _conv_idx), _proj_col(T, 4, _conv_idx), _proj_col(T, 8, _conv_idx),
                  _proj_col(T, 12, _conv_idx), _proj_col(T, 16, _pool_idx), _proj_col(T, 20, _pool_idx),
                  _conv_w_spec(l), _pool_w_spec(l), _pool_s_spec(l)],
        out_specs=[pl.BlockSpec((T, LANES), lambda j: (0, _conv_idx(j))), pool_col, pool_col],
        out_shape=[half, half, half],
        scratch_shapes=[pltpu.VMEM((T, LANES), F32)],
        compiler_params=_params(),
    )(proj, proj, proj, proj, proj, proj, wconv, wpool, pscale3)


def out_fwd(ya, yp, wo, x, mod4, g_post3, l, after):
    T, D = x.shape
    H = ya.shape[1]
    tm = 512

    def body(ya_ref, yp_ref, wo_ref, x_ref, gt_ref, g_ref, after_ref, xn_ref, y_ref):
        y = (jnp.dot(ya_ref[...], wo_ref[0:H, :], preferred_element_type=F32)
             + jnp.dot(yp_ref[...], wo_ref[H:2 * H, :], preferred_element_type=F32))
        xn_ref[...] = x_ref[...] + gt_ref[...] * (y * _rms(y) * g_ref[...])
        y_ref[...] = y.astype(BF16)

    tile = pl.BlockSpec((tm, D), lambda i: (i, 0))
    half = pl.BlockSpec((tm, H), lambda i: (i, 0))
    return pl.pallas_call(
        body, name="out_fwd", grid=(T // tm,),
        in_specs=[half, half, pl.BlockSpec((2 * H, D), lambda i: (0, 0)), tile, _mod_row(l, 2, D), _layer_row(l, D),
                  ANY],
        out_specs=[tile, tile],
        out_shape=[jax.ShapeDtypeStruct((T, D), F32), jax.ShapeDtypeStruct((T, D), BF16)],
        compiler_params=_params(VMEM_BIG),
    )(ya, yp, wo, x, mod4, g_post3, after)


def out_fwd_loss(ya, yp, wo, x, mod4, g_post3, l, target):
    T, D = x.shape
    H = ya.shape[1]
    tm = 512
    nt = T // tm

    def body(ya_ref, yp_ref, wo_ref, x_ref, gt_ref, g_ref, t_ref, dx_ref, y_ref, l_ref, acc):
        i = pl.program_id(0)

        @pl.when(i == 0)
        def _():
            acc[...] = jnp.zeros_like(acc)

        y = (jnp.dot(ya_ref[...], wo_ref[0:H, :], preferred_element_type=F32)
             + jnp.dot(yp_ref[...], wo_ref[H:2 * H, :], preferred_element_type=F32))
        y_ref[...] = y.astype(BF16)
        d = (x_ref[...] + gt_ref[...] * (y * _rms(y) * g_ref[...])) - t_ref[...]
        dx_ref[...] = d * (1.0 / D)
        acc[...] += _colsum8(d * d)

        @pl.when(i == nt - 1)
        def _():
            l_ref[...] = jnp.zeros_like(l_ref) + jnp.sum(acc[...]) * (0.5 / D)

    tile = pl.BlockSpec((tm, D), lambda i: (i, 0))
    half = pl.BlockSpec((tm, H), lambda i: (i, 0))
    return pl.pallas_call(
        body, name="out_fwd_loss", grid=(nt,),
        in_specs=[half, half, pl.BlockSpec((2 * H, D), lambda i: (0, 0)), tile, _mod_row(l, 2, D), _layer_row(l, D),
                  tile],
        out_specs=[tile, tile, pl.BlockSpec((SUBLANES, LANES), lambda i: (0, 0))],
        out_shape=[jax.ShapeDtypeStruct((T, D), F32), jax.ShapeDtypeStruct((T, D), BF16),
                   jax.ShapeDtypeStruct((SUBLANES, LANES), F32)],
        scratch_shapes=[pltpu.VMEM((SUBLANES, D), F32)],
        compiler_params=_params(VMEM_BIG),
    )(ya, yp, wo, x, mod4, g_post3, target)


def out_bwd(dx, y, ya, yp, wo, mod4, g_post3, l, after):
    T, D = dx.shape
    H = ya.shape[1]
    tm = 512
    nt = T // tm

    def body(dx_ref, y_ref, ya_ref, yp_ref, wo_ref, gt_ref, g_ref, after_ref,
             dya_ref, dyp_ref, dwo_ref, dgt_ref, dg_ref, acc_w, acc_p):
        i = pl.program_id(0)

        @pl.when(i == 0)
        def _():
            acc_w[...] = jnp.zeros_like(acc_w)
            acc_p[...] = jnp.zeros_like(acc_p)

        yv = y_ref[...].astype(F32)
        dxv = dx_ref[...]
        gg = gt_ref[...] * g_ref[...]
        r = _rms(yv)
        yn = yv * r
        p = dxv * yn
        acc_p[...] += _colsum8(p)
        dy = r * (dxv * gg - yn * jnp.mean(p * gg, axis=-1, keepdims=True))
        dyb = dy.astype(BF16)
        dyc = lax.dot_general(dyb, wo_ref[...], NT, preferred_element_type=F32)
        dya_ref[...] = dyc[:, 0:H].astype(BF16)
        dyp_ref[...] = dyc[:, H:2 * H].astype(BF16)
        acc_w[0:H, :] += lax.dot_general(ya_ref[...], dyb, TN, preferred_element_type=F32)
        acc_w[H:2 * H, :] += lax.dot_general(yp_ref[...], dyb, TN, preferred_element_type=F32)

        @pl.when(i == nt - 1)
        def _():
            dwo_ref[...] = acc_w[...].astype(BF16)
            sp = jnp.sum(acc_p[...], axis=0, keepdims=True)
            dgt_ref[...] = g_ref[...] * sp
            dg_ref[...] = gt_ref[...] * sp

    row = pl.BlockSpec((1, D), lambda i: (0, 0))
    tile = pl.BlockSpec((tm, D), lambda i: (i, 0))
    half = pl.BlockSpec((tm, H), lambda i: (i, 0))
    full = pl.BlockSpec((2 * H, D), lambda i: (0, 0))
    return pl.pallas_call(
        body, name="out_bwd", grid=(nt,),
        in_specs=[tile, tile, half, half, full, _mod_row(l, 2, D), _layer_row(l, D), ANY],
        out_specs=[half, half, full, row, row],
        out_shape=[jax.ShapeDtypeStruct((T, H), BF16), jax.ShapeDtypeStruct((T, H), BF16),
                   jax.ShapeDtypeStruct((2 * H, D), BF16),
                   jax.ShapeDtypeStruct((1, D), F32), jax.ShapeDtypeStruct((1, D), F32)],
        scratch_shapes=[pltpu.VMEM((2 * H, D), F32), pltpu.VMEM((SUBLANES, D), F32)],
        compiler_params=_params(VMEM_BIG),
    )(dx, y, ya, yp, wo, mod4, g_post3, after)


def _conv_bwd_block(u_ref, b_ref, c_ref, g_ref, dy_ref, w_ref, du_ref, db_ref, dc_ref, dg_ref, dw_ref):
    T = u_ref.shape[0]
    R = R_CONV
    nchunk = T // R
    w0 = w_ref[pl.ds(0, 1), :]
    w1 = w_ref[pl.ds(1, 1), :]
    w2 = w_ref[pl.ds(2, 1), :]

    def chunk(k, carry):
        head, a0, a1, a2 = carry
        i = nchunk - 1 - k
        r0 = pl.multiple_of(i * R, R)
        h0 = pl.multiple_of(jnp.maximum(r0 - HIST, 0), HIST)
        first = i == 0
        ue = _load_ext(u_ref, r0, h0, first, R)
        ce = _load_ext(c_ref, r0, h0, first, R)
        ca = ce * ue
        ca0 = ca[HIST:]
        ca1 = _shift_down(ca, 1, R)
        ca2 = _shift_down(ca, 2, R)
        conv = w2 * ca0 + w1 * ca1 + w0 * ca2
        g = g_ref[pl.ds(r0, R), :].astype(F32)
        b = b_ref[pl.ds(r0, R), :].astype(F32)
        dy = dy_ref[pl.ds(r0, R), :].astype(F32)
        sg = _sigmoid(g)
        sl = g * sg
        t = dy * conv
        db_ref[pl.ds(r0, R), :] = (t * sl).astype(BF16)
        dg_ref[pl.ds(r0, R), :] = (t * b * (sg + sl * (1.0 - sg))).astype(BF16)
        dconv = dy * b * sl
        a2 = a2 + _colsum8(dconv * ca0)
        a1 = a1 + _colsum8(dconv * ca1)
        a0 = a0 + _colsum8(dconv * ca2)
        e = jnp.concatenate([dconv, head], axis=0)
        dca = w2 * dconv + w1 * _shift_up(e, 1, R) + w0 * _shift_up(e, 2, R)
        du_ref[pl.ds(r0, R), :] = (dca * ce[HIST:]).astype(BF16)
        dc_ref[pl.ds(r0, R), :] = (dca * ue[HIST:]).astype(BF16)
        return dconv[0:SUBLANES], a0, a1, a2

    z = jnp.zeros((SUBLANES, LANES), F32)
    _, a0, a1, a2 = lax.fori_loop(0, nchunk, chunk, (z, z, z, z))
    dw_ref[pl.ds(0, 1), :] = jnp.sum(a0, axis=0, keepdims=True)
    dw_ref[pl.ds(1, 1), :] = jnp.sum(a1, axis=0, keepdims=True)
    dw_ref[pl.ds(2, 1), :] = jnp.sum(a2, axis=0, keepdims=True)


def _pool_bwd_group(pooled_s, g_ref, dy_ref, w_ref, s_ref, du_ref, dg_ref, dw_ref, ds_ref,
                    mixed_s, dmix_s, dpool_s, w):
    T = pooled_s.shape[0]
    R = R_POOL
    nchunk = T // R
    wb = w_ref[...].astype(BF16)
    mixed_s[...] = jnp.dot(pooled_s[...], wb, preferred_element_type=F32)
    sc = s_ref[...]

    def gate_chunk(i, acc):
        r0 = pl.multiple_of(i * R, R)
        g = g_ref[pl.ds(r0, R), :].astype(F32)
        dy = dy_ref[pl.ds(r0, R), :].astype(F32)
        mixed = mixed_s[pl.ds(r0, R), :]
        sg = _sigmoid(g)
        sl = g * sg
        dg_ref[pl.ds(r0, R), :] = (dy * mixed * sc * (sg + sl * (1.0 - sg))).astype(BF16)
        dms = dy * sl
        dmix_s[pl.ds(r0, R), :] = (dms * sc).astype(BF16)
        return acc + _colsum8(dms * mixed)

    acc = lax.fori_loop(0, nchunk, gate_chunk, jnp.zeros((SUBLANES, LANES), F32))
    ds_ref[...] = jnp.sum(acc, axis=0, keepdims=True)
    dpool_s[pl.ds(0, T), :] = lax.dot_general(dmix_s[...], wb, NT, preferred_element_type=F32)
    dpool_s[pl.ds(T, HIST), :] = jnp.zeros((HIST, LANES), F32)
    dw_ref[...] = lax.dot_general(pooled_s[...], dmix_s[...], TN, preferred_element_type=F32).astype(BF16)

    def back_chunk(i, carry):
        r0 = pl.multiple_of(i * R, R)
        dpe = dpool_s[pl.ds(r0, R + HIST), :]
        e = dpe / _count(r0, R + HIST, w)
        du_ref[pl.ds(r0, R), :] = (_anticausal_window_sum(e, w)[0:R] - dpe[0:R]).astype(BF16)
        return carry

    lax.fori_loop(0, nchunk, back_chunk, 0)


def mix_bwd(proj, pooled, dya, dyp, wconv, wpool, pscale3, l, after):
    T = proj.shape[0]

    def body(u_ref, b_ref, c_ref, g_ref, pooled_ref, gp_ref, dya_ref, dyp_ref, wc_ref, wp_ref, s_ref, after_ref,
             dua_ref, dba_ref, dca_ref, dga_ref, dup_ref, dgp_ref, dwc_ref, dwp_ref, ds_ref,
             mixed_s, dmix_s, dpool_s):
        j = pl.program_id(0)
        pl.when(j < N_MIX)(functools.partial(_conv_bwd_block, u_ref, b_ref, c_ref, g_ref, dya_ref, wc_ref,
                                             dua_ref, dba_ref, dca_ref, dga_ref, dwc_ref))
        for k, w in enumerate(POOL_WINDOWS):
            pl.when(j == N_MIX + k)(functools.partial(_pool_bwd_group, pooled_ref, gp_ref, dyp_ref, wp_ref, s_ref,
                                                      dup_ref, dgp_ref, dwp_ref, ds_ref,
                                                      mixed_s, dmix_s, dpool_s, w))

    sec = jax.ShapeDtypeStruct((T, N_MIX * LANES), BF16)
    conv_col = pl.BlockSpec((T, LANES), lambda j: (0, _conv_idx(j)))
    pool_col = pl.BlockSpec((T, LANES), lambda j: (0, _pool_idx(j)))
    return pl.pallas_call(
        body, name="mix_bwd", grid=(2 * N_MIX,),
        in_specs=[_proj_col(T, 0, _conv_idx), _proj_col(T, 4, _conv_idx), _proj_col(T, 8, _conv_idx),
                  _proj_col(T, 12, _conv_idx), pool_col, _proj_col(T, 20, _pool_idx),
                  conv_col, pool_col, _conv_w_spec(l), _pool_w_spec(l), _pool_s_spec(l), ANY],
        out_specs=[conv_col, conv_col, conv_col, conv_col, pool_col, pool_col,
                   pl.BlockSpec((None, 3, LANES), lambda j: (_conv_idx(j), 0, 0)),
                   pl.BlockSpec((None, LANES, LANES), lambda j: (_pool_idx(j), 0, 0)),
                   pl.BlockSpec((1, LANES), lambda j: (0, _pool_idx(j)))],
        out_shape=[sec] * 6 + [jax.ShapeDtypeStruct((N_MIX, 3, LANES), F32),
                               jax.ShapeDtypeStruct((N_MIX, LANES, LANES), BF16),
                               jax.ShapeDtypeStruct((1, N_MIX * LANES), F32)],
        scratch_shapes=[pltpu.VMEM((T, LANES), F32), pltpu.VMEM((T, LANES), BF16), pltpu.VMEM((T + HIST, LANES), F32)],
        compiler_params=_params(),
    )(proj, proj, proj, proj, pooled, proj, dya, dyp, wconv, wpool, pscale3, after)


def in_bwd(dsecs, wg, x, dxo, mod4, g_pre3, l, update=None):
    T, D = x.shape
    NB = N_CHIPS
    CW = wg.shape[1] // NB
    SW = dsecs[0].shape[1]
    nsec = len(dsecs)
    PW = 256
    assert SW % PW == 0 and CW % PW == 0
    tm = 256
    nt = T // tm
    n_in = nsec + 6
    n_upd = 0 if update is None else 8
    n_acc = 0 if update is None or update[3] is None else 8

    def body(*refs):
        d_refs = refs[0:nsec]
        w_ref, x_ref, dxo_ref, sh_ref, sc_ref, g_ref = refs[nsec:n_in]
        outs = refs[n_in + n_upd + n_acc:]
        dxi_ref, dw_ref, dsh_ref, dsc_ref, dg_ref = outs[0:5]
        acc_w, acc_sh, acc_q = outs[5 + n_upd:]
        i = pl.program_id(0)
        for k in range(0, n_upd, 4):
            _adamw_block(refs[n_in + k:n_in + k + 4], outs[5 + k:5 + k + 4])

        @pl.when(i == 0)
        def _():
            acc_w[...] = jnp.zeros_like(acc_w)
            acc_sh[...] = jnp.zeros_like(acc_sh)
            acc_q[...] = jnp.zeros_like(acc_q)

        xv = x_ref[...]
        r = _rms(xv)
        xh = xv * r
        sg = g_ref[...] * (1.0 + sc_ref[...])
        hb = (xh * sg + sh_ref[...]).astype(BF16)
        dh = lax.dot_general(d_refs[0][...], w_ref[:, 0:SW], NT, preferred_element_type=F32)
        for s in range(1, nsec):
            dh = dh + lax.dot_general(d_refs[s][...], w_ref[:, s * SW:(s + 1) * SW], NT, preferred_element_type=F32)
        for p in range(nsec * SW // PW):
            col = p * PW
            s, so = col // SW, col % SW
            j, jo = col // CW, col % CW
            acc_w[j, :, jo:jo + PW] += lax.dot_general(hb, d_refs[s][:, so:so + PW], TN, preferred_element_type=F32)
        q = dh * xh
        acc_sh[...] += _colsum8(dh)
        acc_q[...] += _colsum8(q)
        dxi_ref[...] = dxo_ref[...] + r * (dh * sg - xh * jnp.mean(q * sg, axis=-1, keepdims=True))

        @pl.when(i == nt - 1)
        def _():
            dw_ref[...] = acc_w[...].astype(BF16)
            sq = jnp.sum(acc_q[...], axis=0, keepdims=True)
            dsh_ref[...] = jnp.sum(acc_sh[...], axis=0, keepdims=True)
            dsc_ref[...] = g_ref[...] * sq
            dg_ref[...] = (1.0 + sc_ref[...]) * sq

    row = pl.BlockSpec((1, D), lambda i: (0, 0))
    tile = pl.BlockSpec((tm, D), lambda i: (i, 0))
    sect = pl.BlockSpec((tm, SW), lambda i: (i, 0))
    rowshape = jax.ShapeDtypeStruct((1, D), F32)
    in_specs = [sect] * nsec + [pl.BlockSpec((D, NB * CW), lambda i: (0, 0)), tile, tile,
                                _mod_row(l, 0, D), _mod_row(l, 1, D), _layer_row(l, D)]
    out_specs = [tile, pl.BlockSpec((NB, D, CW), lambda i: (0, 0, 0)), row, row, row]
    out_shape = [jax.ShapeDtypeStruct((T, D), F32), jax.ShapeDtypeStruct((NB, D, CW), BF16), rowshape, rowshape, rowshape]
    args = [*dsecs, wg, x, dxo, mod4, mod4, g_pre3]
    aliases = {}
    if update is not None:
        layer, of_w_in, of_w_out, acc = update
        for group in (of_w_in, of_w_out):
            _, rows, cols = group[0].shape
            spec = pl.BlockSpec((None, rows // nt, cols), lambda i: (layer, i, 0))
            in_specs += [spec] * 4
            out_specs += [spec] * 4
            out_shape += [jax.ShapeDtypeStruct(group[0].shape, F32)] * 4
            args += list(group)
        if acc is not None:
            aliases = {len(args) + a: 5 + a for a in range(n_acc)}
            in_specs += [ANY] * n_acc
            args += list(acc)
    return pl.pallas_call(
        body, name="in_bwd", grid=(nt,),
        in_specs=in_specs, out_specs=out_specs, out_shape=out_shape, input_output_aliases=aliases,
        scratch_shapes=[pltpu.VMEM((NB, D, CW), F32),
                        pltpu.VMEM((SUBLANES, D), F32), pltpu.VMEM((SUBLANES, D), F32)],
        compiler_params=_params(VMEM_BIG),
    )(*args)


def _rcopy(src, dst, ssem, rsem, dev):
    return pltpu.make_async_remote_copy(src_ref=src, dst_ref=dst, send_sem=ssem, recv_sem=rsem,
                                        device_id=dev, device_id_type=MESH)


def _peers7(x, y, c):
    out = []
    for m in range(1, N_DEV):
        bx, by, bc = (m >> 2) & 1, (m >> 1) & 1, m & 1
        out.append(((1 - x) if bx else x, (1 - y) if by else y, (1 - c) if bc else c))
    return out


HBM = pl.BlockSpec(memory_space=pltpu.HBM)
SEM = pl.BlockSpec(memory_space=pltpu.SEMAPHORE)
SPLIT = pltpu.CompilerParams(has_side_effects=pltpu.SideEffectType.DATAFLOW_SIDE_EFFECTING)


def _hbm(a):
    return pltpu.with_memory_space_constraint(a, pltpu.HBM)


def _chips(x, y):
    return [(1 - x, y), (x, 1 - y), (1 - x, 1 - y)]


SIBLING_BARRIER_ID = 0


def xchg_start(name, bufs, n_copies, plan, sibling_only=False, after=()):
    n = len(bufs)
    after = list(after)

    def body(*refs):
        ssem, rsem, token = refs[n + len(after)], refs[n + len(after) + 1], refs[-1]
        x, y, c = _me()
        if sibling_only:
            barrier = pltpu.get_barrier_semaphore()
            pl.semaphore_signal(barrier, inc=1, device_id=(x, y, 1 - c), device_id_type=MESH)
            pl.semaphore_wait(barrier, 1)
        copies = plan(refs[0:n], x, y, c)
        assert len(copies) == n_copies
        for k, (src, dst, peer, _) in enumerate(copies):
            _rcopy(src, dst, ssem.at[k], rsem.at[k], peer).start()
        token[...] = jnp.zeros_like(token)

    params = dict(has_side_effects=pltpu.SideEffectType.DATAFLOW_SIDE_EFFECTING)
    if sibling_only:
        params["collective_id"] = SIBLING_BARRIER_ID
    outs = pl.pallas_call(
        body, name=name,
        in_specs=[HBM] * n + [ANY] * len(after),
        out_specs=[SEM, SEM] + [HBM] * n + [pl.BlockSpec(memory_space=pltpu.VMEM)],
        out_shape=([pltpu.SemaphoreType.DMA((n_copies,))] * 2 + [pltpu.HBM(b.shape, b.dtype) for b in bufs]
                   + [jax.ShapeDtypeStruct((SUBLANES, LANES), F32)]),
        input_output_aliases={a: 2 + a for a in range(n)},
        compiler_params=pltpu.CompilerParams(**params),
    )(*[_hbm(b) for b in bufs], *after)
    return outs[0], outs[1], list(outs[2:2 + n]), outs[-1]


def xchg_wait(name, bufs, ssem, rsem, n_copies, plan, after, sems=None):
    n = len(bufs)
    after = list(after)
    sems = tuple(range(n_copies)) if sems is None else tuple(sems)
    assert len(sems) == n_copies

    def body(*refs):
        ssem_ref, rsem_ref = refs[n], refs[n + 1]
        copies = plan(refs[0:n], *_me())
        assert len(copies) == n_copies
        for k, (src, _, peer, land) in zip(sems, copies):
            cp = _rcopy(src, land, ssem_ref.at[k], rsem_ref.at[k], peer)
            cp.wait_send()
            cp.wait_recv()

    outs = pl.pallas_call(
        body, name=name,
        in_specs=[HBM] * n + [SEM, SEM] + [ANY] * len(after), out_specs=[HBM] * n,
        out_shape=[pltpu.HBM(b.shape, b.dtype) for b in bufs],
        input_output_aliases={a: a for a in range(n)},
        compiler_params=SPLIT,
    )(*bufs, ssem, rsem, *after)
    return list(outs)


def _shard_half(buf, chip, half):
    if len(buf.shape) == 2:
        h, w = buf.shape[0] // 2, buf.shape[1] // N_CHIPS
        return buf.at[pl.ds(half * h, h), pl.ds(chip * w, w)]
    h = buf.shape[1] // 2
    return buf.at[chip, pl.ds(half * h, h)]


def plan_gather(refs, x, y, c):
    out = []
    for buf in refs:
        own = _shard_half(buf, 2 * x + y, c)
        for (px, py) in _chips(x, y):
            out.append((own, own, (px, py, c), _shard_half(buf, 2 * px + py, c)))
    return out


def plan_forward(refs, x, y, c):
    out = []
    for (px, py) in _chips(x, y):
        for buf in refs:
            landed = _shard_half(buf, 2 * px + py, c)
            out.append((landed, landed, (x, y, 1 - c), _shard_half(buf, 2 * px + py, 1 - c)))
    return out


def plan_sibling(refs, x, y, c):
    n = len(refs) // 2
    out = []
    for a in range(n):
        h = refs[a].shape[1] // 2
        out.append((refs[a].at[:, pl.ds((1 - c) * h, h)], refs[n + a], (x, y, 1 - c), refs[n + a]))
    return out


def plan_chip(refs, x, y, c):
    n = len(refs) // 2
    out = []
    for j, (px, py) in enumerate(_chips(x, y)):
        for a in range(n):
            out.append((refs[a].at[2 * px + py], refs[n + a].at[j], (px, py, c), refs[n + a].at[j]))
    return out


def plan_mod(refs, x, y, c):
    (mods,) = refs
    mine = mods.at[2 * x + y]
    return [(mine, mine, (px, py, c), mods.at[2 * px + py]) for (px, py) in _chips(x, y)]


def plan_pack(refs, x, y, c):
    (packs,) = refs
    mine = packs.at[4 * x + 2 * y + c]
    return [(mine, mine, peer, packs.at[4 * peer[0] + 2 * peer[1] + peer[2]]) for peer in _peers7(x, y, c)]


def plan_spread(layers, wp_layers):
    def plan(refs, x, y, c):
        gi, go, gp = refs
        hD, hR, hP = gi.shape[1] // 2, go.shape[1] // 2, gp.shape[2] // 2
        sib = (x, y, 1 - c)
        out = []
        for l in layers:
            mine = gi.at[l, pl.ds(c * hD, hD)]
            out.append((mine, mine, sib, gi.at[l, pl.ds((1 - c) * hD, hD)]))
            mine = go.at[l, pl.ds(c * hR, hR)]
            out.append((mine, mine, sib, go.at[l, pl.ds((1 - c) * hR, hR)]))
        for l in wp_layers:
            mine = gp.at[l, 2 * x + y, pl.ds(c * hP, hP)]
            for peer in _peers7(x, y, c):
                out.append((mine, mine, peer, gp.at[l, 2 * peer[0] + peer[1], pl.ds(peer[2] * hP, hP)]))
        return out

    return plan


def place_small(pos, c8, wc):
    L = wc.shape[0]

    def body(pos_ref, c_ref, wc_ref, call_ref, wcall_ref):
        call_ref[...] = c_ref[...]
        wcall_ref[...] = wc_ref[...]

    return pl.pallas_call(
        body, name="place_small",
        grid_spec=pltpu.PrefetchScalarGridSpec(
            num_scalar_prefetch=1, grid=(1,),
            in_specs=[pl.BlockSpec((SUBLANES, LANES), lambda i, p: (0, 0)),
                      pl.BlockSpec((L, 3, LANES), lambda i, p: (0, 0, 0))],
            out_specs=[pl.BlockSpec((None, SUBLANES, LANES), lambda i, p: (p[2], 0, 0)),
                       pl.BlockSpec((None, L, 3, LANES), lambda i, p: (p[1], 0, 0, 0))]),
        out_shape=[jax.ShapeDtypeStruct((N_DEV, SUBLANES, LANES), F32),
                   jax.ShapeDtypeStruct((N_CHIPS, L, 3, LANES), F32)],
        compiler_params=_params(),
    )(pos, c8, wc)


def plan_small(refs, x, y, c):
    call, wcall = refs
    mine = call.at[4 * x + 2 * y + c]
    out = [(mine, mine, peer, call.at[4 * peer[0] + 2 * peer[1] + peer[2]]) for peer in _peers7(x, y, c)]
    mine = wcall.at[2 * x + y]
    out += [(mine, mine, (px, py, c), wcall.at[2 * px + py]) for (px, py) in _chips(x, y)]
    return out


def add_sibling(cidx, mine, sib):
    def body(c_ref, *refs):
        for a in range(3):
            m, s, o = refs[a], refs[3 + a], refs[6 + a]
            o[...] = (m[...].astype(F32) + s[...].astype(F32)).astype(BF16)

    per_step = 2

    def mine_spec(a):
        h = a.shape[1] // 2
        return pl.BlockSpec((per_step, h, a.shape[2]), lambda j, c_ref: (j, c_ref[0], 0))

    def sib_spec(a):
        return pl.BlockSpec((per_step,) + a.shape[1:], lambda j, c_ref: (j, 0, 0))

    return pl.pallas_call(
        body, name="add_sibling",
        grid_spec=pltpu.PrefetchScalarGridSpec(
            num_scalar_prefetch=1, grid=(N_CHIPS // per_step,),
            in_specs=[mine_spec(a) for a in mine] + [sib_spec(a) for a in sib],
            out_specs=[sib_spec(a) for a in sib]),
        out_shape=[jax.ShapeDtypeStruct(a.shape, BF16) for a in sib],
        compiler_params=_params(VMEM_BIG),
    )(cidx, *mine, *sib)


def sum_chips(pos, own, rb, acc, l, shapes):
    nq = 2
    n_in = 6 + (3 if acc is not None else 0)

    def body(pos_ref, *refs):
        for a in range(3):
            m, b, o = refs[a], refs[3 + a], refs[n_in + a]
            s = m[...].astype(F32)
            for j in range(3):
                s = s + b[j].astype(F32)
            o[...] = s

    def own_spec(a):
        return pl.BlockSpec((None, a.shape[1] // nq, a.shape[2]), lambda q, p: (p[1], q, 0))

    def rb_spec(a):
        return pl.BlockSpec((3, a.shape[1] // nq, a.shape[2]), lambda q, p: (0, q, 0))

    hi, ho, hp = own[0].shape[1] // nq, own[1].shape[1] // nq, own[2].shape[1] // nq
    out_specs = [pl.BlockSpec((None, hi, shapes[0][2]), lambda q, p: (l, p[0] * nq + q, 0)),
                 pl.BlockSpec((None, ho, shapes[1][2]), lambda q, p: (l, p[0] * nq + q, 0)),
                 pl.BlockSpec((None, None, hp, LANES), lambda q, p: (l, p[1], p[0] * nq + q, 0))]
    in_specs = [own_spec(a) for a in own] + [rb_spec(a) for a in rb]
    args = list(own) + list(rb)
    aliases = {}
    if acc is not None:
        in_specs += [ANY] * 3
        args += list(acc)
        aliases = {7: 0, 8: 1, 9: 2}
    return pl.pallas_call(
        body, name="sum_chips",
        grid_spec=pltpu.PrefetchScalarGridSpec(num_scalar_prefetch=1, grid=(nq,), in_specs=in_specs, out_specs=out_specs),
        out_shape=[jax.ShapeDtypeStruct(s, F32) for s in shapes],
        input_output_aliases=aliases,
        compiler_params=_params(VMEM_BIG),
    )(pos, *args)


def _wconv_slot(chip, tap):
    idx = 3 * chip + tap
    return ROW_WCONV + idx // SUBLANES, slice((idx % SUBLANES) * LANES, (idx % SUBLANES + 1) * LANES)


def pack_small(pos, per_layer, loss_blk):
    L = len(per_layer)
    D = per_layer[0][0].shape[1]

    def body(pos_ref, *refs):
        o = refs[-1]
        lb = refs[-2]
        o[...] = jnp.zeros_like(o)
        for l in range(L):
            dgpre, dgpost, dsh, dsc, dgt, dps, dwc = refs[7 * l:7 * l + 7]
            base = SUBLANES * l
            o[pl.ds(base + ROW_G_PRE, 1), :] = dgpre[...]
            o[pl.ds(base + ROW_G_POST, 1), :] = dgpost[...]
            for r, src in enumerate((dsh, dsc, dgt)):
                o[pl.ds(base + ROW_MOD + r, 1), :] = src[...]
            o[pl.ds(base + ROW_PSCALE, 1), 0:dps.shape[1]] = dps[...]
            for j in range(dwc.shape[0]):
                for k in range(3):
                    row, lanes = _wconv_slot(j, k)
                    o[pl.ds(base + row, 1), lanes] = dwc[j, pl.ds(k, 1), :]
        o[pl.ds(ROW_PSCALE, 1), LOSS_LANES] = lb[pl.ds(0, 1), :]

    flat = [a for layer in per_layer for a in layer] + [loss_blk]

    def whole(a):
        return pl.BlockSpec(a.shape, lambda i, p: (0,) * a.ndim)

    return pl.pallas_call(
        body, name="pack_small",
        grid_spec=pltpu.PrefetchScalarGridSpec(
            num_scalar_prefetch=1, grid=(1,), in_specs=[whole(a) for a in flat],
            out_specs=pl.BlockSpec((None, L * SUBLANES, D), lambda i, p: (p[2], 0, 0))),
        out_shape=jax.ShapeDtypeStruct((N_DEV, L * SUBLANES, D), F32),
        compiler_params=_params(),
    )(pos, *flat)


def small_update(pos, packs, params, moments_m, moments_v):
    n = len(params)
    L, D = params[1].shape
    PS = params[3].shape[1]

    def body(pos_ref, p_ref, *refs):
        ws, ms, vs = refs[0:n], refs[n:2 * n], refs[2 * n:3 * n]
        loss_ref = refs[3 * n]
        outs = [refs[3 * n + 1 + 4 * t:3 * n + 5 + 4 * t] for t in range(n)]
        summed = refs[-1]
        s = p_ref[0]
        for d in range(1, N_DEV):
            s = s + p_ref[d]
        summed[...] = s
        loss_ref[...] = summed[pl.ds(ROW_PSCALE, 1), LOSS_LANES]
        chip = pos_ref[1]

        def update(t, idx, g):
            d, mm, vv = _adamw_math(ws[t][idx], g, ms[t][idx], vs[t][idx])
            g_ref, d_ref, mo_ref, vo_ref = outs[t]
            g_ref[idx] = g
            d_ref[idx] = d
            mo_ref[idx] = mm
            vo_ref[idx] = vv

        for l in range(L):
            base = SUBLANES * l
            row = pl.ds(l, 1)
            for k in range(3):
                update(0, (row, slice(k * D, (k + 1) * D)), summed[pl.ds(base + ROW_MOD + k, 1), :])
            update(1, (row, slice(None)), summed[pl.ds(base + ROW_G_PRE, 1), :])
            update(2, (row, slice(None)), summed[pl.ds(base + ROW_G_POST, 1), :])
            update(3, (row, slice(None)), summed[pl.ds(base + ROW_PSCALE, 1), 0:PS])
            for k in range(3):
                g = None
                for j in range(N_CHIPS):
                    wrow, lanes = _wconv_slot(j, k)
                    cand = summed[pl.ds(base + wrow, 1), lanes]
                    g = cand if g is None else jnp.where(chip == j, cand, g)
                update(4, (l, pl.ds(k, 1), slice(None)), g)

    def whole(a):
        return pl.BlockSpec(a.shape, lambda i, p: (0,) * a.ndim)

    ins = [packs] + list(params) + list(moments_m) + list(moments_v)
    out_shape = [jax.ShapeDtypeStruct((1, LANES), F32)]
    for w in params:
        out_shape += [jax.ShapeDtypeStruct(w.shape, F32)] * 4
    outs = pl.pallas_call(
        body, name="small_update",
        grid_spec=pltpu.PrefetchScalarGridSpec(
            num_scalar_prefetch=1, grid=(1,), in_specs=[whole(a) for a in ins],
            out_specs=[whole(a) for a in out_shape],
            scratch_shapes=[pltpu.VMEM(packs.shape[1:], F32)]),
        out_shape=out_shape,
        compiler_params=_params(),
    )(pos, *ins)
    return outs[0], [outs[1 + 4 * t:5 + 4 * t] for t in range(n)]


def _adamw_math(w, g, m, v):
    m = ADAM_B1 * m + (1.0 - ADAM_B1) * g
    v = ADAM_B2 * v + (1.0 - ADAM_B2) * (g * g)
    m_hat = m / (1.0 - ADAM_B1 ** ADAM_STEP)
    v_hat = v / (1.0 - ADAM_B2 ** ADAM_STEP)
    delta = -ADAM_LR * (m_hat / (jnp.sqrt(v_hat) + ADAM_EPS) + ADAM_WD * w)
    return delta, m, v


def _adamw_block(ins, outs):
    w_ref, g_ref, m_ref, v_ref = ins
    go_ref, d_ref, mo_ref, vo_ref = outs
    gv = g_ref[...]
    d, mm, vv = _adamw_math(w_ref[...], gv, m_ref[...], v_ref[...])
    go_ref[...] = gv
    d_ref[...] = d
    mo_ref[...] = mm
    vo_ref[...] = vv


def adamw(groups, name, first, count, steps, acc=None):
    n = len(groups)

    def body(*refs):
        outs = refs[len(refs) - 4 * n:]
        for k in range(n):
            _adamw_block(refs[4 * k:4 * k + 4], outs[4 * k:4 * k + 4])

    specs, out_shape, args = [], [], []
    for group in groups:
        shape = group[0].shape
        spec = pl.BlockSpec((1, shape[1] // steps) + shape[2:],
                            lambda i, s, rest=(0,) * (len(shape) - 2): (first + i, s) + rest)
        specs += [spec] * 4
        out_shape += [jax.ShapeDtypeStruct(shape, F32)] * 4
        args += list(group)
    extra = [] if acc is None else list(acc)
    return pl.pallas_call(
        body, name=name, grid=(count, steps),
        in_specs=specs + [ANY] * len(extra), out_specs=specs, out_shape=out_shape,
        input_output_aliases={4 * n + a: a for a in range(len(extra))},
        compiler_params=_params(VMEM_BIG, n_grid=2),
    )(*args, *extra)


def ada_finish(c_all, dmod, w, m, v):
    L, D, CW = w.shape
    hD = D // 2

    def body(c_ref, d_ref, w_ref, m_ref, v_ref, g_ref, dl_ref, mo_ref, vo_ref):
        cv = c_ref[...]
        z = jnp.zeros_like(cv)
        ca = jnp.concatenate([cv * jax.nn.sigmoid(cv), z], axis=0).astype(BF16)
        dm = jnp.concatenate([d_ref[0], jnp.zeros_like(d_ref[0])], axis=0).astype(BF16)
        g = lax.dot_general(ca, dm, TN, preferred_element_type=F32)
        g_ref[0] = g
        d, mm, vv = _adamw_math(w_ref[0], g, m_ref[0], v_ref[0])
        dl_ref[0] = d
        mo_ref[0] = mm
        vo_ref[0] = vv

    big = pl.BlockSpec((1, hD, CW), lambda l, h: (l, h, 0))
    shape = jax.ShapeDtypeStruct(w.shape, F32)
    return pl.pallas_call(
        body, name="ada_finish", grid=(L, 2),
        in_specs=[pl.BlockSpec((N_DEV, hD), lambda l, h: (0, h)), pl.BlockSpec((1, N_DEV, CW), lambda l, h: (l, 0, 0)),
                  big, big, big],
        out_specs=[big] * 4, out_shape=[shape] * 4,
        compiler_params=_params(VMEM_BIG, n_grid=2),
    )(c_all, dmod, w, m, v)


def kernel(x, c, w_ada, b_ada, g_pre, w_in, w_conv, w_pool, pool_scale, w_out, g_post, loss_target, m_w_ada, m_b_ada, m_g_pre, m_w_in, m_w_conv, m_w_pool, m_pool_scale, m_w_out, m_g_post, v_w_ada, v_b_ada, v_g_pre, v_w_in, v_w_conv, v_w_pool, v_pool_scale, v_w_out, v_g_post):
    L, D, CW = w_in.shape
    RO = w_out.shape[1]
    T = x.shape[1]
    ix, iy, ic = _me()
    chip = 2 * ix + iy
    me_lin = 4 * ix + 2 * iy + ic

    pos = jnp.stack([ic, chip, me_lin]).astype(jnp.int32)
    n_s, n_c = 3, 9

    def gather(bufs, after):
        ss, rs, bufs, tok = xchg_start("gather_start", bufs, 3 * len(bufs), plan_gather, after=after)
        return (ss, rs, bufs), tok

    def ready(flight, after):
        fss, frs, bufs = flight
        return xchg_wait("forward_wait", bufs, fss, frs, 3 * len(bufs), plan_forward, after)

    def arrive_part(flight, which, after, base=0):
        ss, rs, bufs = flight
        sems = tuple(range(base + 3 * which, base + 3 * which + 3))
        (buf,) = xchg_wait("gather_wait", [bufs[which]], ss, rs, 3, plan_gather, after, sems=sems)
        fss, frs, (buf,), tok = xchg_start("forward_start", [buf], 3, plan_forward, sibling_only=True)
        return (fss, frs, [buf]), tok

    n_small = N_DEV - 1 + N_CHIPS - 1
    w_in_of, w_out_of = [None] * L, [None] * L
    gi0, go0 = cast_weights(pos, w_in, w_out, 0, pos)

    def plan_first(refs, x, y, c):
        return plan_small(refs[0:2], x, y, c) + plan_gather(refs[2:3], x, y, c)

    placed = list(place_small(pos, c.reshape(SUBLANES, LANES), w_conv))
    s_ss, s_rs, firsts, token = xchg_start("first_start", placed + [gi0], n_small + 3, plan_first)
    smalls_in = firsts[0:2]
    w_in_of[0] = ((s_ss, s_rs, firsts[2:3]), 0)
    g_pre_l, g_post_l, pscale_l, b_ada_l, token = lax.optimization_barrier((g_pre, g_post, pool_scale, b_ada, token))
    g_pre3, g_post3 = g_pre_l.reshape(L, 1, D), g_post_l.reshape(L, 1, D)
    pscale3 = pscale_l.reshape(L, 1, pool_scale.shape[1])
    c_all3, wconv_all = xchg_wait("small_wait", smalls_in, s_ss, s_rs, n_small, plan_small, [token])
    c_all = c_all3.reshape(N_DEV, D)
    b_my = lax.dynamic_slice_in_dim(b_ada_l, chip * CW, CW, axis=1)
    m_ss, m_rs, mods, token = xchg_start("mod_start", [mod_part(pos, c_all, w_ada, b_my, token)], 3, plan_mod)
    gi1, go1 = cast_weights(pos, w_in, w_out, 1, token)
    flight, token = gather([gi1, go0, go1], [])
    w_in_of[1], w_out_of[0], w_out_of[1] = (flight, 0), (flight, 1), (flight, 2)
    late = []
    for l in range(2, L):
        late += list(cast_weights(pos, w_in, w_out, l, token))
    flight, token = gather(late, [])
    for l in range(2, L):
        w_in_of[l], w_out_of[l] = (flight, 2 * (l - 2)), (flight, 2 * (l - 2) + 1)
    fwd_in, token = arrive_part(*w_in_of[0], [token], base=n_small)
    (mod_all,) = xchg_wait("mod_wait", mods, m_ss, m_rs, 3, plan_mod, [token])
    mod = lax.dynamic_index_in_dim(mod_all, me_lin, axis=2, keepdims=False)
    mod4 = jnp.transpose(mod, (1, 0, 2)).reshape(L, 3, 1, D)

    xs, projs, yas, yps, ys, pooleds = [x.reshape(T, D)], [], [], [], [], []
    wg_in, wg_out = [], []
    for l in range(L):
        (gi,) = ready(fwd_in, [mod4 if l == 0 else xs[l]])
        proj = proj_fwd(xs[l], mod4, g_pre3, gi, l)
        ya, yp, pooled = mix_fwd(proj, wconv_all, w_pool, pscale3, l)
        pooleds.append(pooled)
        fwd_out, token = arrive_part(*w_out_of[l], [ya, yp])
        after = [token]
        if l + 1 < L:
            fwd_in, token = arrive_part(*w_in_of[l + 1], after)
            after = [token]
        (go,) = ready(fwd_out, after)
        wg_in.append(gi)
        wg_out.append(go.reshape(N_CHIPS * RO, D))
        projs.append(proj)
        yas.append(ya)
        yps.append(yp)
        if l + 1 < L:
            xn, yv = out_fwd(ya, yp, wg_out[l], xs[l], mod4, g_post3, l, after[0])
            xs.append(xn)
        else:
            dx, yv, loss_blk = out_fwd_loss(ya, yp, wg_out[l], xs[l], mod4, g_post3, l, loss_target.reshape(T, D))
        ys.append(yv)

    shapes = (w_in.shape, w_out.shape, w_pool.shape)
    smalls = [None] * L
    acc, flying, sib, token = None, None, None, loss_blk

    def to_chips(sib, after):
        sl, s_ss, s_rs, s_bufs = sib
        s_bufs = xchg_wait("sibling_wait", s_bufs, s_ss, s_rs, n_s, plan_sibling, after)
        chip_parts = add_sibling(pos, s_bufs[0:3], s_bufs[3:6])
        lands = [lax.empty((3,) + a.shape[1:], a.dtype) for a in chip_parts]
        c_ss, c_rs, c_bufs, ctoken = xchg_start("chip_start", list(chip_parts) + lands, n_c, plan_chip)
        return (sl, c_ss, c_rs, c_bufs), ctoken

    def landed(flying, acc, after):
        fl, f_ss, f_rs, f_bufs = flying
        f_bufs = xchg_wait("chip_wait", f_bufs, f_ss, f_rs, n_c, plan_chip, after)
        return sum_chips(pos, f_bufs[0:3], f_bufs[3:6], acc, fl, shapes)

    early = None
    for l in reversed(range(L)):
        dya, dyp, dwo_l, dgate, dgpost = out_bwd(dx, ys[l], yas[l], yps[l], wg_out[l], mod4, g_post3, l, token)
        token = dya
        spreading = None
        if sib is not None:
            arrived = flying
            flying, token = to_chips(sib, [dya])
            if arrived is not None:
                acc = landed(arrived, acc, [token])
                spreading = plan_spread((arrived[0],), ())
                sp_ss, sp_rs, acc, token = xchg_start("spread_start", list(acc), 2, spreading, sibling_only=True)
        du_a, db_a, dc_a, dg_a, du_p, dg_p, dwc, dwp_l, dps = mix_bwd(projs[l], pooleds[l], dya, dyp, wconv_all, w_pool,
                                                                        pscale3, l, token)
        update = None
        if spreading is not None:
            acc = xchg_wait("spread_wait", acc, sp_ss, sp_rs, 2, spreading, [du_a])
            update = (arrived[0], [w_in, acc[0], m_w_in, v_w_in], [w_out, acc[1], m_w_out, v_w_out], early)
        dx, dwi_l, dshift, dscale, dgpre, *rest = in_bwd([du_a, db_a, dc_a, dg_a, du_p, dg_p], wg_in[l], xs[l], dx,
                                                         mod4, g_pre3, l, update)
        early = rest if rest else early
        smalls[l] = (dgpre, dgpost, dshift, dscale, dgate, dps, dwc)
        parts = [dwi_l, dwo_l.reshape(N_CHIPS, RO, D), dwp_l]
        s_lands = [lax.empty((a.shape[0], a.shape[1] // 2) + a.shape[2:], a.dtype) for a in parts]
        s_ss, s_rs, s_bufs, token = xchg_start("sibling_start", parts + s_lands, n_s, plan_sibling, sibling_only=True)
        sib = (l, s_ss, s_rs, s_bufs)
    grad_x = dx.reshape(1, T, D)

    p_ss, p_rs, packs, ptoken = xchg_start("pack_start", [pack_small(pos, smalls, loss_blk)], N_DEV - 1, plan_pack)
    acc = landed(flying, acc, [ptoken, token])
    n_sp = 2 + (N_DEV - 1) * (L - 1)
    spread = plan_spread((1,), tuple(range(1, L)))
    sp_ss, sp_rs, acc, sp_token = xchg_start("spread_start", list(acc), n_sp, spread)
    flying, token = to_chips(sib, [sp_token])
    (packs_all,) = xchg_wait("pack_wait", packs, p_ss, p_rs, N_DEV - 1, plan_pack, [token])
    dmod_all = packs_all.reshape(N_DEV, L, SUBLANES, D)[:, :, ROW_MOD:ROW_MOD + 3].reshape(N_DEV, L, 3 * D)
    dmod_my = jnp.transpose(lax.dynamic_slice_in_dim(dmod_all, chip * CW, CW, axis=2), (1, 0, 2))

    g_w_ada, d_w_ada, nm_w_ada, nv_w_ada = ada_finish(c_all, dmod_my, w_ada, m_w_ada, v_w_ada)
    m_w_conv_l, v_w_conv_l, _ = lax.optimization_barrier((m_w_conv, v_w_conv, yas[0]))
    loss_row, upd = small_update(pos, packs_all, [b_ada, g_pre, g_post, pool_scale, w_conv],
                                 [m_b_ada, m_g_pre, m_g_post, m_pool_scale, m_w_conv_l],
                                 [v_b_ada, v_g_pre, v_g_post, v_pool_scale, v_w_conv_l])
    loss = loss_row[0, 0]
    (g_b_ada, d_b_ada, nm_b_ada, nv_b_ada), (g_g_pre, d_g_pre, nm_g_pre, nv_g_pre) = upd[0], upd[1]
    (g_g_post, d_g_post, nm_g_post, nv_g_post), (g_pscale, d_pscale, nm_pscale, nv_pscale) = upd[2], upd[3]
    g_w_conv, d_w_conv, nm_w_conv, nv_w_conv = upd[4]

    done = [nv_w_ada, nv_w_conv]
    g_w_in, g_w_out, g_w_pool = xchg_wait("spread_wait", acc, sp_ss, sp_rs, n_sp, spread, done)
    early = adamw([[w_in, g_w_in, m_w_in, v_w_in], [w_out, g_w_out, m_w_out, v_w_out]], "adamw_layer", 1, 1, 2, early)

    acc = landed(flying, (g_w_in, g_w_out, g_w_pool), [early[3], early[7]])
    last = plan_spread((0,), (0,))
    n_last = 2 + N_DEV - 1
    l_ss, l_rs, acc, _ = xchg_start("spread_start", list(acc), n_last, last)
    upd_pool = adamw([[w_pool, acc[2], m_w_pool, v_w_pool]], "adamw_w_pool", 1, L - 1, 1)
    r_w_in, r_w_out, r_w_pool = xchg_wait("spread_wait", acc, l_ss, l_rs, n_last, last, [upd_pool[3]])
    (g_w_in, d_w_in, nm_w_in, nv_w_in, g_w_out, d_w_out, nm_w_out, nv_w_out,
     g_w_pool, d_w_pool, nm_w_pool, nv_w_pool) = adamw(
         [[w_in, r_w_in, m_w_in, v_w_in], [w_out, r_w_out, m_w_out, v_w_out], [w_pool, r_w_pool, m_w_pool, v_w_pool]],
         "adamw_layer", 0, 1, 2, list(early) + list(upd_pool))

    return (loss, grad_x,
            g_w_ada, g_b_ada, g_g_pre, g_w_in, g_w_conv, g_w_pool, g_pscale, g_w_out, g_g_post,
            d_w_ada, d_b_ada, d_g_pre, d_w_in, d_w_conv, d_w_pool, d_pscale, d_w_out, d_g_post,
            nm_w_ada, nm_b_ada, nm_g_pre, nm_w_in, nm_w_conv, nm_w_pool, nm_pscale, nm_w_out, nm_g_post,
            nv_w_ada, nv_b_ada, nv_g_pre, nv_w_in, nv_w_conv, nv_w_pool, nv_pscale, nv_w_out, nv_g_post)
```

```python
import functools

import jax
import jax.numpy as jnp
from jax import lax
from jax.experimental import pallas as pl
from jax.experimental.pallas import tpu as pltpu

F32 = jnp.float32
BF16 = jnp.bfloat16
MESH = pl.DeviceIdType.MESH
ANY = pl.BlockSpec(memory_space=pl.ANY)

NORM_EPS = 1e-6
POOL_WINDOWS = (2, 4, 8, 16)
ADAM_LR = 0.001
ADAM_B1 = 0.9
ADAM_B2 = 0.999
ADAM_EPS = 1e-08
ADAM_WD = 0.01
ADAM_STEP = 10

N_CHIPS = 4
N_DEV = 8
LANES = 128
SUBLANES = 8
VMEM_BIG = 56 * 1024 * 1024
HIST = 16
R_CONV = 64
R_POOL = 128

ROW_G_PRE, ROW_G_POST, ROW_MOD, ROW_PSCALE, ROW_WCONV = 0, 1, 2, 5, 6
LOSS_LANES = slice(4 * LANES, 5 * LANES)

NT = (((1,), (1,)), ((), ()))
TN = (((0,), (0,)), ((), ()))


def _params(vmem=None, n_grid=1):
    kw = {}
    if n_grid:
        kw["dimension_semantics"] = ("arbitrary",) * n_grid
    if vmem is not None:
        kw["vmem_limit_bytes"] = vmem
    return pltpu.CompilerParams(**kw)


def _colsum8(v):
    n, d = v.shape
    return v.reshape(n // SUBLANES, SUBLANES, d).sum(axis=0)


def _rms(v):
    return lax.rsqrt(jnp.mean(v * v, axis=-1, keepdims=True) + NORM_EPS)


def _sigmoid(v):
    return 0.5 * jnp.tanh(0.5 * v) + 0.5


def _shift_down(ext, k, rows):
    if k == 0:
        return ext[HIST:HIST + rows]
    return pltpu.roll(ext, k, 0)[HIST:HIST + rows]


def _shift_up(ext, k, rows):
    if k == 0:
        return ext[0:rows]
    return pltpu.roll(ext, ext.shape[0] - k, 0)[0:rows]


def _load_ext(ref, r0, h0, first, rows):
    hist = ref[pl.ds(h0, HIST), :].astype(F32)
    hist = jnp.where(first, 0.0, hist)
    cur = ref[pl.ds(r0, rows), :].astype(F32)
    return jnp.concatenate([hist, cur], axis=0)


def _me():
    return lax.axis_index("x"), lax.axis_index("y"), lax.axis_index("c")


def cast_weights(pos, w_in, w_out, l, after):
    _, D, CW = w_in.shape
    RO = w_out.shape[1]

    def body(pos_ref, wi, wo, after_ref, oi, oo):
        oi[...] = wi[...].astype(BF16)
        oo[...] = wo[...].astype(BF16)

    return pl.pallas_call(
        body, name="cast_w",
        grid_spec=pltpu.PrefetchScalarGridSpec(
            num_scalar_prefetch=1, grid=(2,),
            in_specs=[pl.BlockSpec((None, D // 2, CW), lambda h, p: (l, h, 0)),
                      pl.BlockSpec((None, RO // 2, D), lambda h, p: (l, h, 0)), ANY],
            out_specs=[pl.BlockSpec((D // 2, CW), lambda h, p: (h, p[1])),
                       pl.BlockSpec((None, RO // 2, D), lambda h, p: (p[1], h, 0))]),
        out_shape=[jax.ShapeDtypeStruct((D, N_CHIPS * CW), BF16), jax.ShapeDtypeStruct((N_CHIPS, RO, D), BF16)],
        compiler_params=_params(),
    )(pos, w_in, w_out, after)


def mod_part(pos, c_all, w_ada, b_my, after):
    L, D, CW = w_ada.shape

    def body(pos_ref, c_ref, w_ref, b_ref, after_ref, o_ref):
        cv = c_ref[...]
        ca = (cv * jax.nn.sigmoid(cv)).astype(BF16)
        o_ref[...] = jnp.dot(ca, w_ref[0].astype(BF16), preferred_element_type=F32) + b_ref[0]

    return pl.pallas_call(
        body, name="mod_part",
        grid_spec=pltpu.PrefetchScalarGridSpec(
            num_scalar_prefetch=1, grid=(L,),
            in_specs=[pl.BlockSpec((N_DEV, D), lambda l, p: (0, 0)),
                      pl.BlockSpec((1, D, CW), lambda l, p: (l, 0, 0)),
                      pl.BlockSpec((1, 1, CW), lambda l, p: (l, 0, 0)), ANY],
            out_specs=pl.BlockSpec((None, None, N_DEV, CW), lambda l, p: (p[1], l, 0, 0))),
        out_shape=jax.ShapeDtypeStruct((N_CHIPS, L, N_DEV, CW), F32),
        compiler_params=_params(VMEM_BIG),
    )(pos, c_all, w_ada, b_my.reshape(L, 1, CW), after)


def _mod_row(l, k, D):
    return pl.BlockSpec((None, None, 1, D), lambda *_: (l, k, 0, 0))


def _layer_row(l, D):
    return pl.BlockSpec((None, 1, D), lambda *_: (l, 0, 0))


def proj_fwd(x, mod4, g_pre3, wg, l):
    T, D = x.shape
    NC = wg.shape[1]
    NB = N_CHIPS
    CW = NC // NB
    tm = 512

    def body(x_ref, sh_ref, sc_ref, g_ref, w_ref, o_ref):
        xv = x_ref[...]
        h = (xv * _rms(xv)) * (g_ref[...] * (1.0 + sc_ref[...])) + sh_ref[...]
        hb = h.astype(BF16)
        for j in range(NB):
            cols = slice(j * CW, (j + 1) * CW)
            o_ref[:, cols] = jnp.dot(hb, w_ref[:, cols], preferred_element_type=F32).astype(BF16)

    return pl.pallas_call(
        body, name="proj_fwd", grid=(T // tm,),
        in_specs=[pl.BlockSpec((tm, D), lambda i: (i, 0)), _mod_row(l, 0, D), _mod_row(l, 1, D), _layer_row(l, D),
                  pl.BlockSpec((D, NC), lambda i: (0, 0))],
        out_specs=pl.BlockSpec((tm, NC), lambda i: (i, 0)),
        out_shape=jax.ShapeDtypeStruct((T, NC), BF16),
        compiler_params=_params(VMEM_BIG),
    )(x, mod4, mod4, g_pre3, wg)


N_MIX = 4


def _conv_fwd_block(u_ref, b_ref, c_ref, g_ref, w_ref, o_ref):
    T = u_ref.shape[0]
    R = 2 * R_CONV
    w0 = w_ref[pl.ds(0, 1), :]
    w1 = w_ref[pl.ds(1, 1), :]
    w2 = w_ref[pl.ds(2, 1), :]

    def chunk(i, carry):
        r0 = pl.multiple_of(i * R, R)
        h0 = pl.multiple_of(jnp.maximum(r0 - HIST, 0), HIST)
        first = i == 0
        ca = _load_ext(c_ref, r0, h0, first, R) * _load_ext(u_ref, r0, h0, first, R)
        conv = w2 * ca[HIST:] + w1 * _shift_down(ca, 1, R) + w0 * _shift_down(ca, 2, R)
        g = g_ref[pl.ds(r0, R), :].astype(F32)
        b = b_ref[pl.ds(r0, R), :].astype(F32)
        o_ref[pl.ds(r0, R), :] = (b * conv * (g * _sigmoid(g))).astype(BF16)
        return carry

    lax.fori_loop(0, T // R, chunk, 0)


def _conv_idx(j):
    return jnp.minimum(j, N_MIX - 1)


def _pool_idx(j):
    return jnp.maximum(j - N_MIX, 0)


def _proj_col(T, off, idx):
    return pl.BlockSpec((T, LANES), lambda j: (0, idx(j) + off))


def _causal_window_sum(ext, w):
    s, k = ext, 1
    while k < w:
        s = s + pltpu.roll(s, k, 0)
        k *= 2
    return s


def _anticausal_window_sum(ext, w):
    s, k = ext, 1
    n = ext.shape[0]
    while k < w:
        s = s + pltpu.roll(s, n - k, 0)
        k *= 2
    return s


def _count(r0, rows, w):
    t = r0 + lax.broadcasted_iota(jnp.int32, (rows, LANES), 0)
    return jnp.minimum(t + 1, w).astype(F32)


def _pooled_loop(p_ref, pooled_s, w, T):
    R = R_POOL

    def chunk(i, carry):
        r0 = pl.multiple_of(i * R, R)
        h0 = pl.multiple_of(jnp.maximum(r0 - HIST, 0), HIST)
        ext = _load_ext(p_ref, r0, h0, i == 0, R)
        ws = _causal_window_sum(ext, w)[HIST:]
        pooled_s[pl.ds(r0, R), :] = (ws / _count(r0, R, w) - ext[HIST:]).astype(BF16)
        return carry

    lax.fori_loop(0, T // R, chunk, 0)


def _conv_w_spec(l):
    return pl.BlockSpec((None, None, 3, LANES), lambda j: (_conv_idx(j), l, 0, 0))


def _pool_w_spec(l):
    return pl.BlockSpec((None, None, LANES, LANES), lambda j: (l, _pool_idx(j), 0, 0))


def _pool_s_spec(l):
    return pl.BlockSpec((None, 1, LANES), lambda j: (l, 0, _pool_idx(j)))


def _pool_fwd_group(p_ref, g_ref, w_ref, s_ref, o_ref, pooled_s, mixed_s, w):
    T = p_ref.shape[0]
    R = R_POOL
    _pooled_loop(p_ref, pooled_s, w, T)
    mixed_s[...] = jnp.dot(pooled_s[...], w_ref[...].astype(BF16), preferred_element_type=F32)
    sc = s_ref[...]

    def chunk(i, carry):
        r0 = pl.multiple_of(i * R, R)
        g = g_ref[pl.ds(r0, R), :].astype(F32)
        o_ref[pl.ds(r0, R), :] = (mixed_s[pl.ds(r0, R), :] * sc * (g * _sigmoid(g))).astype(BF16)
        return carry

    lax.fori_loop(0, T // R, chunk, 0)


def mix_fwd(proj, wconv, wpool, pscale3, l):
    T = proj.shape[0]

    def body(u_ref, b_ref, c_ref, g_ref, p_ref, gp_ref, wc_ref, wp_ref, s_ref, ya_ref, yp_ref, pooled_ref, mixed_s):
        j = pl.program_id(0)
        pl.when(j < N_MIX)(functools.partial(_conv_fwd_block, u_ref, b_ref, c_ref, g_ref, wc_ref, ya_ref))
        for k, w in enumerate(POOL_WINDOWS):
            pl.when(j == N_MIX + k)(functools.partial(_pool_fwd_group, p_ref, gp_ref, wp_ref, s_ref, yp_ref,
                                                      pooled_ref, mixed_s, w))

    half = jax.ShapeDtypeStruct((T, N_MIX * LANES), BF16)
    pool_col = pl.BlockSpec((T, LANES), lambda j: (0, _pool_idx(j)))
    return pl.pallas_call(
        body, name="mix_fwd", grid=(2 * N_MIX,),
        in_specs=[_proj_col(T, 0, _conv_idx), _proj_col(T, 4, _conv_idx), _proj_col(T, 8, _conv_idx),
                  _proj_col(T, 12, _conv_idx), _proj_col(T, 16, _pool_idx), _proj_col(T, 20, _pool_idx),
                  _conv_w_spec(l), _pool_w_spec(l), _pool_s_spec(l)],
        out_specs=[pl.BlockSpec((T, LANES), lambda j: (0, _conv_idx(j))), pool_col, pool_col],
        out_shape=[half, half, half],
        scratch_shapes=[pltpu.VMEM((T, LANES), F32)],
        compiler_params=_params(),
    )(proj, proj, proj, proj, proj, proj, wconv, wpool, pscale3)


def out_fwd(ya, yp, wo, x, mod4, g_post3, l, after):
    T, D = x.shape
    H = ya.shape[1]
    tm = 512

    def body(ya_ref, yp_ref, wo_ref, x_ref, gt_ref, g_ref, after_ref, xn_ref, y_ref):
        y = (jnp.dot(ya_ref[...], wo_ref[0:H, :], preferred_element_type=F32)
             + jnp.dot(yp_ref[...], wo_ref[H:2 * H, :], preferred_element_type=F32))
        xn_ref[...] = x_ref[...] + gt_ref[...] * (y * _rms(y) * g_ref[...])
        y_ref[...] = y.astype(BF16)

    tile = pl.BlockSpec((tm, D), lambda i: (i, 0))
    half = pl.BlockSpec((tm, H), lambda i: (i, 0))
    return pl.pallas_call(
        body, name="out_fwd", grid=(T // tm,),
        in_specs=[half, half, pl.BlockSpec((2 * H, D), lambda i: (0, 0)), tile, _mod_row(l, 2, D), _layer_row(l, D),
                  ANY],
        out_specs=[tile, tile],
        out_shape=[jax.ShapeDtypeStruct((T, D), F32), jax.ShapeDtypeStruct((T, D), BF16)],
        compiler_params=_params(VMEM_BIG),
    )(ya, yp, wo, x, mod4, g_post3, after)


def out_fwd_loss(ya, yp, wo, x, mod4, g_post3, l, target):
    T, D = x.shape
    H = ya.shape[1]
    tm = 512
    nt = T // tm

    def body(ya_ref, yp_ref, wo_ref, x_ref, gt_ref, g_ref, t_ref, dx_ref, y_ref, l_ref, acc):
        i = pl.program_id(0)

        @pl.when(i == 0)
        def _():
            acc[...] = jnp.zeros_like(acc)

        y = (jnp.dot(ya_ref[...], wo_ref[0:H, :], preferred_element_type=F32)
             + jnp.dot(yp_ref[...], wo_ref[H:2 * H, :], preferred_element_type=F32))
        y_ref[...] = y.astype(BF16)
        d = (x_ref[...] + gt_ref[...] * (y * _rms(y) * g_ref[...])) - t_ref[...]
        dx_ref[...] = d * (1.0 / D)
        acc[...] += _colsum8(d * d)

        @pl.when(i == nt - 1)
        def _():
            l_ref[...] = jnp.zeros_like(l_ref) + jnp.sum(acc[...]) * (0.5 / D)

    tile = pl.BlockSpec((tm, D), lambda i: (i, 0))
    half = pl.BlockSpec((tm, H), lambda i: (i, 0))
    return pl.pallas_call(
        body, name="out_fwd_loss", grid=(nt,),
        in_specs=[half, half, pl.BlockSpec((2 * H, D), lambda i: (0, 0)), tile, _mod_row(l, 2, D), _layer_row(l, D),
                  tile],
        out_specs=[tile, tile, pl.BlockSpec((SUBLANES, LANES), lambda i: (0, 0))],
        out_shape=[jax.ShapeDtypeStruct((T, D), F32), jax.ShapeDtypeStruct((T, D), BF16),
                   jax.ShapeDtypeStruct((SUBLANES, LANES), F32)],
        scratch_shapes=[pltpu.VMEM((SUBLANES, D), F32)],
        compiler_params=_params(VMEM_BIG),
    )(ya, yp, wo, x, mod4, g_post3, target)


def out_bwd(dx, y, ya, yp, wo, mod4, g_post3, l, after):
    T, D = dx.shape
    H = ya.shape[1]
    tm = 512
    nt = T // tm

    def body(dx_ref, y_ref, ya_ref, yp_ref, wo_ref, gt_ref, g_ref, after_ref,
             dya_ref, dyp_ref, dwo_ref, dgt_ref, dg_ref, acc_w, acc_p):
        i = pl.program_id(0)

        @pl.when(i == 0)
        def _():
            acc_w[...] = jnp.zeros_like(acc_w)
            acc_p[...] = jnp.zeros_like(acc_p)

        yv = y_ref[...].astype(F32)
        dxv = dx_ref[...]
        gg = gt_ref[...] * g_ref[...]
        r = _rms(yv)
        yn = yv * r
        p = dxv * yn
        acc_p[...] += _colsum8(p)
        dy = r * (dxv * gg - yn * jnp.mean(p * gg, axis=-1, keepdims=True))
        dyb = dy.astype(BF16)
        dyc = lax.dot_general(dyb, wo_ref[...], NT, preferred_element_type=F32)
        dya_ref[...] = dyc[:, 0:H].astype(BF16)
        dyp_ref[...] = dyc[:, H:2 * H].astype(BF16)
        acc_w[0:H, :] += lax.dot_general(ya_ref[...], dyb, TN, preferred_element_type=F32)
        acc_w[H:2 * H, :] += lax.dot_general(yp_ref[...], dyb, TN, preferred_element_type=F32)

        @pl.when(i == nt - 1)
        def _():
            dwo_ref[...] = acc_w[...].astype(BF16)
            sp = jnp.sum(acc_p[...], axis=0, keepdims=True)
            dgt_ref[...] = g_ref[...] * sp
            dg_ref[...] = gt_ref[...] * sp

    row = pl.BlockSpec((1, D), lambda i: (0, 0))
    tile = pl.BlockSpec((tm, D), lambda i: (i, 0))
    half = pl.BlockSpec((tm, H), lambda i: (i, 0))
    full = pl.BlockSpec((2 * H, D), lambda i: (0, 0))
    return pl.pallas_call(
        body, name="out_bwd", grid=(nt,),
        in_specs=[tile, tile, half, half, full, _mod_row(l, 2, D), _layer_row(l, D), ANY],
        out_specs=[half, half, full, row, row],
        out_shape=[jax.ShapeDtypeStruct((T, H), BF16), jax.ShapeDtypeStruct((T, H), BF16),
                   jax.ShapeDtypeStruct((2 * H, D), BF16),
                   jax.ShapeDtypeStruct((1, D), F32), jax.ShapeDtypeStruct((1, D), F32)],
        scratch_shapes=[pltpu.VMEM((2 * H, D), F32), pltpu.VMEM((SUBLANES, D), F32)],
        compiler_params=_params(VMEM_BIG),
    )(dx, y, ya, yp, wo, mod4, g_post3, after)


def _conv_bwd_block(u_ref, b_ref, c_ref, g_ref, dy_ref, w_ref, du_ref, db_ref, dc_ref, dg_ref, dw_ref):
    T = u_ref.shape[0]
    R = R_CONV
    nchunk = T // R
    w0 = w_ref[pl.ds(0, 1), :]
    w1 = w_ref[pl.ds(1, 1), :]
    w2 = w_ref[pl.ds(2, 1), :]

    def chunk(k, carry):
        head, a0, a1, a2 = carry
        i = nchunk - 1 - k
        r0 = pl.multiple_of(i * R, R)
        h0 = pl.multiple_of(jnp.maximum(r0 - HIST, 0), HIST)
        first = i == 0
        ue = _load_ext(u_ref, r0, h0, first, R)
        ce = _load_ext(c_ref, r0, h0, first, R)
        ca = ce * ue
        ca0 = ca[HIST:]
        ca1 = _shift_down(ca, 1, R)
        ca2 = _shift_down(ca, 2, R)
        conv = w2 * ca0 + w1 * ca1 + w0 * ca2
        g = g_ref[pl.ds(r0, R), :].astype(F32)
        b = b_ref[pl.ds(r0, R), :].astype(F32)
        dy = dy_ref[pl.ds(r0, R), :].astype(F32)
        sg = _sigmoid(g)
        sl = g * sg
        t = dy * conv
        db_ref[pl.ds(r0, R), :] = (t * sl).astype(BF16)
        dg_ref[pl.ds(r0, R), :] = (t * b * (sg + sl * (1.0 - sg))).astype(BF16)
        dconv = dy * b * sl
        a2 = a2 + _colsum8(dconv * ca0)
        a1 = a1 + _colsum8(dconv * ca1)
        a0 = a0 + _colsum8(dconv * ca2)
        e = jnp.concatenate([dconv, head], axis=0)
        dca = w2 * dconv + w1 * _shift_up(e, 1, R) + w0 * _shift_up(e, 2, R)
        du_ref[pl.ds(r0, R), :] = (dca * ce[HIST:]).astype(BF16)
        dc_ref[pl.ds(r0, R), :] = (dca * ue[HIST:]).astype(BF16)
        return dconv[0:SUBLANES], a0, a1, a2

    z = jnp.zeros((SUBLANES, LANES), F32)
    _, a0, a1, a2 = lax.fori_loop(0, nchunk, chunk, (z, z, z, z))
    dw_ref[pl.ds(0, 1), :] = jnp.sum(a0, axis=0, keepdims=True)
    dw_ref[pl.ds(1, 1), :] = jnp.sum(a1, axis=0, keepdims=True)
    dw_ref[pl.ds(2, 1), :] = jnp.sum(a2, axis=0, keepdims=True)


def _pool_bwd_group(pooled_s, g_ref, dy_ref, w_ref, s_ref, du_ref, dg_ref, dw_ref, ds_ref,
                    mixed_s, dmix_s, dpool_s, w):
    T = pooled_s.shape[0]
    R = R_POOL
    nchunk = T // R
    wb = w_ref[...].astype(BF16)
    mixed_s[...] = jnp.dot(pooled_s[...], wb, preferred_element_type=F32)
    sc = s_ref[...]

    def gate_chunk(i, acc):
        r0 = pl.multiple_of(i * R, R)
        g = g_ref[pl.ds(r0, R), :].astype(F32)
        dy = dy_ref[pl.ds(r0, R), :].astype(F32)
        mixed = mixed_s[pl.ds(r0, R), :]
        sg = _sigmoid(g)
        sl = g * sg
        dg_ref[pl.ds(r0, R), :] = (dy * mixed * sc * (sg + sl * (1.0 - sg))).astype(BF16)
        dms = dy * sl
        dmix_s[pl.ds(r0, R), :] = (dms * sc).astype(BF16)
        return acc + _colsum8(dms * mixed)

    acc = lax.fori_loop(0, nchunk, gate_chunk, jnp.zeros((SUBLANES, LANES), F32))
    ds_ref[...] = jnp.sum(acc, axis=0, keepdims=True)
    dpool_s[pl.ds(0, T), :] = lax.dot_general(dmix_s[...], wb, NT, preferred_element_type=F32)
    dpool_s[pl.ds(T, HIST), :] = jnp.zeros((HIST, LANES), F32)
    dw_ref[...] = lax.dot_general(pooled_s[...], dmix_s[...], TN, preferred_element_type=F32).astype(BF16)

    def back_chunk(i, carry):
        r0 = pl.multiple_of(i * R, R)
        dpe = dpool_s[pl.ds(r0, R + HIST), :]
        e = dpe / _count(r0, R + HIST, w)
        du_ref[pl.ds(r0, R), :] = (_anticausal_window_sum(e, w)[0:R] - dpe[0:R]).astype(BF16)
        return carry

    lax.fori_loop(0, nchunk, back_chunk, 0)


def mix_bwd(proj, pooled, dya, dyp, wconv, wpool, pscale3, l, after):
    T = proj.shape[0]

    def body(u_ref, b_ref, c_ref, g_ref, pooled_ref, gp_ref, dya_ref, dyp_ref, wc_ref, wp_ref, s_ref, after_ref,
             dua_ref, dba_ref, dca_ref, dga_ref, dup_ref, dgp_ref, dwc_ref, dwp_ref, ds_ref,
             mixed_s, dmix_s, dpool_s):
        j = pl.program_id(0)
        pl.when(j < N_MIX)(functools.partial(_conv_bwd_block, u_ref, b_ref, c_ref, g_ref, dya_ref, wc_ref,
                                             dua_ref, dba_ref, dca_ref, dga_ref, dwc_ref))
        for k, w in enumerate(POOL_WINDOWS):
            pl.when(j == N_MIX + k)(functools.partial(_pool_bwd_group, pooled_ref, gp_ref, dyp_ref, wp_ref, s_ref,
                                                      dup_ref, dgp_ref, dwp_ref, ds_ref,
                                                      mixed_s, dmix_s, dpool_s, w))

    sec = jax.ShapeDtypeStruct((T, N_MIX * LANES), BF16)
    conv_col = pl.BlockSpec((T, LANES), lambda j: (0, _conv_idx(j)))
    pool_col = pl.BlockSpec((T, LANES), lambda j: (0, _pool_idx(j)))
    return pl.pallas_call(
        body, name="mix_bwd", grid=(2 * N_MIX,),
        in_specs=[_proj_col(T, 0, _conv_idx), _proj_col(T, 4, _conv_idx), _proj_col(T, 8, _conv_idx),
                  _proj_col(T, 12, _conv_idx), pool_col, _proj_col(T, 20, _pool_idx),
                  conv_col, pool_col, _conv_w_spec(l), _pool_w_spec(l), _pool_s_spec(l), ANY],
        out_specs=[conv_col, conv_col, conv_col, conv_col, pool_col, pool_col,
                   pl.BlockSpec((None, 3, LANES), lambda j: (_conv_idx(j), 0, 0)),
                   pl.BlockSpec((None, LANES, LANES), lambda j: (_pool_idx(j), 0, 0)),
                   pl.BlockSpec((1, LANES), lambda j: (0, _pool_idx(j)))],
        out_shape=[sec] * 6 + [jax.ShapeDtypeStruct((N_MIX, 3, LANES), F32),
                               jax.ShapeDtypeStruct((N_MIX, LANES, LANES), BF16),
                               jax.ShapeDtypeStruct((1, N_MIX * LANES), F32)],
        scratch_shapes=[pltpu.VMEM((T, LANES), F32), pltpu.VMEM((T, LANES), BF16), pltpu.VMEM((T + HIST, LANES), F32)],
        compiler_params=_params(),
    )(proj, proj, proj, proj, pooled, proj, dya, dyp, wconv, wpool, pscale3, after)


def in_bwd(dsecs, wg, x, dxo, mod4, g_pre3, l, update=None):
    T, D = x.shape
    NB = N_CHIPS
    CW = wg.shape[1] // NB
    SW = dsecs[0].shape[1]
    nsec = len(dsecs)
    PW = 256
    assert SW % PW == 0 and CW % PW == 0
    tm = 256
    nt = T // tm
    n_in = nsec + 6
    n_upd = 0 if update is None else 8
    n_acc = 0 if update is None or update[3] is None else 8

    def body(*refs):
        d_refs = refs[0:nsec]
        w_ref, x_ref, dxo_ref, sh_ref, sc_ref, g_ref = refs[nsec:n_in]
        outs = refs[n_in + n_upd + n_acc:]
        dxi_ref, dw_ref, dsh_ref, dsc_ref, dg_ref = outs[0:5]
        acc_w, acc_sh, acc_q = outs[5 + n_upd:]
        i = pl.program_id(0)
        for k in range(0, n_upd, 4):
            _adamw_block(refs[n_in + k:n_in + k + 4], outs[5 + k:5 + k + 4])

        @pl.when(i == 0)
        def _():
            acc_w[...] = jnp.zeros_like(acc_w)
            acc_sh[...] = jnp.zeros_like(acc_sh)
            acc_q[...] = jnp.zeros_like(acc_q)

        xv = x_ref[...]
        r = _rms(xv)
        xh = xv * r
        sg = g_ref[...] * (1.0 + sc_ref[...])
        hb = (xh * sg + sh_ref[...]).astype(BF16)
        dh = lax.dot_general(d_refs[0][...], w_ref[:, 0:SW], NT, preferred_element_type=F32)
        for s in range(1, nsec):
            dh = dh + lax.dot_general(d_refs[s][...], w_ref[:, s * SW:(s + 1) * SW], NT, preferred_element_type=F32)
        for p in range(nsec * SW // PW):
            col = p * PW
            s, so = col // SW, col % SW
            j, jo = col // CW, col % CW
            acc_w[j, :, jo:jo + PW] += lax.dot_general(hb, d_refs[s][:, so:so + PW], TN, preferred_element_type=F32)
        q = dh * xh
        acc_sh[...] += _colsum8(dh)
        acc_q[...] += _colsum8(q)
        dxi_ref[...] = dxo_ref[...] + r * (dh * sg - xh * jnp.mean(q * sg, axis=-1, keepdims=True))

        @pl.when(i == nt - 1)
        def _():
            dw_ref[...] = acc_w[...].astype(BF16)
            sq = jnp.sum(acc_q[...], axis=0, keepdims=True)
            dsh_ref[...] = jnp.sum(acc_sh[...], axis=0, keepdims=True)
            dsc_ref[...] = g_ref[...] * sq
            dg_ref[...] = (1.0 + sc_ref[...]) * sq

    row = pl.BlockSpec((1, D), lambda i: (0, 0))
    tile = pl.BlockSpec((tm, D), lambda i: (i, 0))
    sect = pl.BlockSpec((tm, SW), lambda i: (i, 0))
    rowshape = jax.ShapeDtypeStruct((1, D), F32)
    in_specs = [sect] * nsec + [pl.BlockSpec((D, NB * CW), lambda i: (0, 0)), tile, tile,
                                _mod_row(l, 0, D), _mod_row(l, 1, D), _layer_row(l, D)]
    out_specs = [tile, pl.BlockSpec((NB, D, CW), lambda i: (0, 0, 0)), row, row, row]
    out_shape = [jax.ShapeDtypeStruct((T, D), F32), jax.ShapeDtypeStruct((NB, D, CW), BF16), rowshape, rowshape, rowshape]
    args = [*dsecs, wg, x, dxo, mod4, mod4, g_pre3]
    aliases = {}
    if update is not None:
        layer, of_w_in, of_w_out, acc = update
        for group in (of_w_in, of_w_out):
            _, rows, cols = group[0].shape
            spec = pl.BlockSpec((None, rows // nt, cols), lambda i: (layer, i, 0))
            in_specs += [spec] * 4
            out_specs += [spec] * 4
            out_shape += [jax.ShapeDtypeStruct(group[0].shape, F32)] * 4
            args += list(group)
        if acc is not None:
            aliases = {len(args) + a: 5 + a for a in range(n_acc)}
            in_specs += [ANY] * n_acc
            args += list(acc)
    return pl.pallas_call(
        body, name="in_bwd", grid=(nt,),
        in_specs=in_specs, out_specs=out_specs, out_shape=out_shape, input_output_aliases=aliases,
        scratch_shapes=[pltpu.VMEM((NB, D, CW), F32),
                        pltpu.VMEM((SUBLANES, D), F32), pltpu.VMEM((SUBLANES, D), F32)],
        compiler_params=_params(VMEM_BIG),
    )(*args)


def _rcopy(src, dst, ssem, rsem, dev):
    return pltpu.make_async_remote_copy(src_ref=src, dst_ref=dst, send_sem=ssem, recv_sem=rsem,
                                        device_id=dev, device_id_type=MESH)


def _peers7(x, y, c):
    out = []
    for m in range(1, N_DEV):
        bx, by, bc = (m >> 2) & 1, (m >> 1) & 1, m & 1
        out.append(((1 - x) if bx else x, (1 - y) if by else y, (1 - c) if bc else c))
    return out


HBM = pl.BlockSpec(memory_space=pltpu.HBM)
SEM = pl.BlockSpec(memory_space=pltpu.SEMAPHORE)
SPLIT = pltpu.CompilerParams(has_side_effects=pltpu.SideEffectType.DATAFLOW_SIDE_EFFECTING)


def _hbm(a):
    return pltpu.with_memory_space_constraint(a, pltpu.HBM)


def _chips(x, y):
    return [(1 - x, y), (x, 1 - y), (1 - x, 1 - y)]


SIBLING_BARRIER_ID = 0


def xchg_start(name, bufs, n_copies, plan, sibling_only=False, after=()):
    n = len(bufs)
    after = list(after)

    def body(*refs):
        ssem, rsem, token = refs[n + len(after)], refs[n + len(after) + 1], refs[-1]
        x, y, c = _me()
        if sibling_only:
            barrier = pltpu.get_barrier_semaphore()
            pl.semaphore_signal(barrier, inc=1, device_id=(x, y, 1 - c), device_id_type=MESH)
            pl.semaphore_wait(barrier, 1)
        copies = plan(refs[0:n], x, y, c)
        assert len(copies) == n_copies
        for k, (src, dst, peer, _) in enumerate(copies):
            _rcopy(src, dst, ssem.at[k], rsem.at[k], peer).start()
        token[...] = jnp.zeros_like(token)

    params = dict(has_side_effects=pltpu.SideEffectType.DATAFLOW_SIDE_EFFECTING)
    if sibling_only:
        params["collective_id"] = SIBLING_BARRIER_ID
    outs = pl.pallas_call(
        body, name=name,
        in_specs=[HBM] * n + [ANY] * len(after),
        out_specs=[SEM, SEM] + [HBM] * n + [pl.BlockSpec(memory_space=pltpu.VMEM)],
        out_shape=([pltpu.SemaphoreType.DMA((n_copies,))] * 2 + [pltpu.HBM(b.shape, b.dtype) for b in bufs]
                   + [jax.ShapeDtypeStruct((SUBLANES, LANES), F32)]),
        input_output_aliases={a: 2 + a for a in range(n)},
        compiler_params=pltpu.CompilerParams(**params),
    )(*[_hbm(b) for b in bufs], *after)
    return outs[0], outs[1], list(outs[2:2 + n]), outs[-1]


def xchg_wait(name, bufs, ssem, rsem, n_copies, plan, after, sems=None):
    n = len(bufs)
    after = list(after)
    sems = tuple(range(n_copies)) if sems is None else tuple(sems)
    assert len(sems) == n_copies

    def body(*refs):
        ssem_ref, rsem_ref = refs[n], refs[n + 1]
        copies = plan(refs[0:n], *_me())
        assert len(copies) == n_copies
        for k, (src, _, peer, land) in zip(sems, copies):
            cp = _rcopy(src, land, ssem_ref.at[k], rsem_ref.at[k], peer)
            cp.wait_send()
            cp.wait_recv()

    outs = pl.pallas_call(
        body, name=name,
        in_specs=[HBM] * n + [SEM, SEM] + [ANY] * len(after), out_specs=[HBM] * n,
        out_shape=[pltpu.HBM(b.shape, b.dtype) for b in bufs],
        input_output_aliases={a: a for a in range(n)},
        compiler_params=SPLIT,
    )(*bufs, ssem, rsem, *after)
    return list(outs)


def _shard_half(buf, chip, half):
    if len(buf.shape) == 2:
        h, w = buf.shape[0] // 2, buf.shape[1] // N_CHIPS
        return buf.at[pl.ds(half * h, h), pl.ds(chip * w, w)]
    h = buf.shape[1] // 2
    return buf.at[chip, pl.ds(half * h, h)]


def plan_gather(refs, x, y, c):
    out = []
    for buf in refs:
        own = _shard_half(buf, 2 * x + y, c)
        for (px, py) in _chips(x, y):
            out.append((own, own, (px, py, c), _shard_half(buf, 2 * px + py, c)))
    return out


def plan_forward(refs, x, y, c):
    out = []
    for (px, py) in _chips(x, y):
        for buf in refs:
            landed = _shard_half(buf, 2 * px + py, c)
            out.append((landed, landed, (x, y, 1 - c), _shard_half(buf, 2 * px + py, 1 - c)))
    return out


def plan_sibling(refs, x, y, c):
    n = len(refs) // 2
    out = []
    for a in range(n):
        h = refs[a].shape[1] // 2
        out.append((refs[a].at[:, pl.ds((1 - c) * h, h)], refs[n + a], (x, y, 1 - c), refs[n + a]))
    return out


def plan_chip(refs, x, y, c):
    n = len(refs) // 2
    out = []
    for j, (px, py) in enumerate(_chips(x, y)):
        for a in range(n):
            out.append((refs[a].at[2 * px + py], refs[n + a].at[j], (px, py, c), refs[n + a].at[j]))
    return out


def plan_mod(refs, x, y, c):
    (mods,) = refs
    mine = mods.at[2 * x + y]
    return [(mine, mine, (px, py, c), mods.at[2 * px + py]) for (px, py) in _chips(x, y)]


def plan_pack(refs, x, y, c):
    (packs,) = refs
    mine = packs.at[4 * x + 2 * y + c]
    return [(mine, mine, peer, packs.at[4 * peer[0] + 2 * peer[1] + peer[2]]) for peer in _peers7(x, y, c)]


def plan_spread(layers, wp_layers):
    def plan(refs, x, y, c):
        gi, go, gp = refs
        hD, hR, hP = gi.shape[1] // 2, go.shape[1] // 2, gp.shape[2] // 2
        sib = (x, y, 1 - c)
        out = []
        for l in layers:
            mine = gi.at[l, pl.ds(c * hD, hD)]
            out.append((mine, mine, sib, gi.at[l, pl.ds((1 - c) * hD, hD)]))
            mine = go.at[l, pl.ds(c * hR, hR)]
            out.append((mine, mine, sib, go.at[l, pl.ds((1 - c) * hR, hR)]))
        for l in wp_layers:
            mine = gp.at[l, 2 * x + y, pl.ds(c * hP, hP)]
            for peer in _peers7(x, y, c):
                out.append((mine, mine, peer, gp.at[l, 2 * peer[0] + peer[1], pl.ds(peer[2] * hP, hP)]))
        return out

    return plan


def place_small(pos, c8, wc):
    L = wc.shape[0]

    def body(pos_ref, c_ref, wc_ref, call_ref, wcall_ref):
        call_ref[...] = c_ref[...]
        wcall_ref[...] = wc_ref[...]

    return pl.pallas_call(
        body, name="place_small",
        grid_spec=pltpu.PrefetchScalarGridSpec(
            num_scalar_prefetch=1, grid=(1,),
            in_specs=[pl.BlockSpec((SUBLANES, LANES), lambda i, p: (0, 0)),
                      pl.BlockSpec((L, 3, LANES), lambda i, p: (0, 0, 0))],
            out_specs=[pl.BlockSpec((None, SUBLANES, LANES), lambda i, p: (p[2], 0, 0)),
                       pl.BlockSpec((None, L, 3, LANES), lambda i, p: (p[1], 0, 0, 0))]),
        out_shape=[jax.ShapeDtypeStruct((N_DEV, SUBLANES, LANES), F32),
                   jax.ShapeDtypeStruct((N_CHIPS, L, 3, LANES), F32)],
        compiler_params=_params(),
    )(pos, c8, wc)


def plan_small(refs, x, y, c):
    call, wcall = refs
    mine = call.at[4 * x + 2 * y + c]
    out = [(mine, mine, peer, call.at[4 * peer[0] + 2 * peer[1] + peer[2]]) for peer in _peers7(x, y, c)]
    mine = wcall.at[2 * x + y]
    out += [(mine, mine, (px, py, c), wcall.at[2 * px + py]) for (px, py) in _chips(x, y)]
    return out


def add_sibling(cidx, mine, sib):
    def body(c_ref, *refs):
        for a in range(3):
            m, s, o = refs[a], refs[3 + a], refs[6 + a]
            o[...] = (m[...].astype(F32) + s[...].astype(F32)).astype(BF16)

    per_step = 2

    def mine_spec(a):
        h = a.shape[1] // 2
        return pl.BlockSpec((per_step, h, a.shape[2]), lambda j, c_ref: (j, c_ref[0], 0))

    def sib_spec(a):
        return pl.BlockSpec((per_step,) + a.shape[1:], lambda j, c_ref: (j, 0, 0))

    return pl.pallas_call(
        body, name="add_sibling",
        grid_spec=pltpu.PrefetchScalarGridSpec(
            num_scalar_prefetch=1, grid=(N_CHIPS // per_step,),
            in_specs=[mine_spec(a) for a in mine] + [sib_spec(a) for a in sib],
            out_specs=[sib_spec(a) for a in sib]),
        out_shape=[jax.ShapeDtypeStruct(a.shape, BF16) for a in sib],
        compiler_params=_params(VMEM_BIG),
    )(cidx, *mine, *sib)


def sum_chips(pos, own, rb, acc, l, shapes):
    nq = 2
    n_in = 6 + (3 if acc is not None else 0)

    def body(pos_ref, *refs):
        for a in range(3):
            m, b, o = refs[a], refs[3 + a], refs[n_in + a]
            s = m[...].astype(F32)
            for j in range(3):
                s = s + b[j].astype(F32)
            o[...] = s

    def own_spec(a):
        return pl.BlockSpec((None, a.shape[1] // nq, a.shape[2]), lambda q, p: (p[1], q, 0))

    def rb_spec(a):
        return pl.BlockSpec((3, a.shape[1] // nq, a.shape[2]), lambda q, p: (0, q, 0))

    hi, ho, hp = own[0].shape[1] // nq, own[1].shape[1] // nq, own[2].shape[1] // nq
    out_specs = [pl.BlockSpec((None, hi, shapes[0][2]), lambda q, p: (l, p[0] * nq + q, 0)),
                 pl.BlockSpec((None, ho, shapes[1][2]), lambda q, p: (l, p[0] * nq + q, 0)),
                 pl.BlockSpec((None, None, hp, LANES), lambda q, p: (l, p[1], p[0] * nq + q, 0))]
    in_specs = [own_spec(a) for a in own] + [rb_spec(a) for a in rb]
    args = list(own) + list(rb)
    aliases = {}
    if acc is not None:
        in_specs += [ANY] * 3
        args += list(acc)
        aliases = {7: 0, 8: 1, 9: 2}
    return pl.pallas_call(
        body, name="sum_chips",
        grid_spec=pltpu.PrefetchScalarGridSpec(num_scalar_prefetch=1, grid=(nq,), in_specs=in_specs, out_specs=out_specs),
        out_shape=[jax.ShapeDtypeStruct(s, F32) for s in shapes],
        input_output_aliases=aliases,
        compiler_params=_params(VMEM_BIG),
    )(pos, *args)


def _wconv_slot(chip, tap):
    idx = 3 * chip + tap
    return ROW_WCONV + idx // SUBLANES, slice((idx % SUBLANES) * LANES, (idx % SUBLANES + 1) * LANES)


def pack_small(pos, per_layer, loss_blk):
    L = len(per_layer)
    D = per_layer[0][0].shape[1]

    def body(pos_ref, *refs):
        o = refs[-1]
        lb = refs[-2]
        o[...] = jnp.zeros_like(o)
        for l in range(L):
            dgpre, dgpost, dsh, dsc, dgt, dps, dwc = refs[7 * l:7 * l + 7]
            base = SUBLANES * l
            o[pl.ds(base + ROW_G_PRE, 1), :] = dgpre[...]
            o[pl.ds(base + ROW_G_POST, 1), :] = dgpost[...]
            for r, src in enumerate((dsh, dsc, dgt)):
                o[pl.ds(base + ROW_MOD + r, 1), :] = src[...]
            o[pl.ds(base + ROW_PSCALE, 1), 0:dps.shape[1]] = dps[...]
            for j in range(dwc.shape[0]):
                for k in range(3):
                    row, lanes = _wconv_slot(j, k)
                    o[pl.ds(base + row, 1), lanes] = dwc[j, pl.ds(k, 1), :]
        o[pl.ds(ROW_PSCALE, 1), LOSS_LANES] = lb[pl.ds(0, 1), :]

    flat = [a for layer in per_layer for a in layer] + [loss_blk]

    def whole(a):
        return pl.BlockSpec(a.shape, lambda i, p: (0,) * a.ndim)

    return pl.pallas_call(
        body, name="pack_small",
        grid_spec=pltpu.PrefetchScalarGridSpec(
            num_scalar_prefetch=1, grid=(1,), in_specs=[whole(a) for a in flat],
            out_specs=pl.BlockSpec((None, L * SUBLANES, D), lambda i, p: (p[2], 0, 0))),
        out_shape=jax.ShapeDtypeStruct((N_DEV, L * SUBLANES, D), F32),
        compiler_params=_params(),
    )(pos, *flat)


def small_update(pos, packs, params, moments_m, moments_v):
    n = len(params)
    L, D = params[1].shape
    PS = params[3].shape[1]

    def body(pos_ref, p_ref, *refs):
        ws, ms, vs = refs[0:n], refs[n:2 * n], refs[2 * n:3 * n]
        loss_ref = refs[3 * n]
        outs = [refs[3 * n + 1 + 4 * t:3 * n + 5 + 4 * t] for t in range(n)]
        summed = refs[-1]
        s = p_ref[0]
        for d in range(1, N_DEV):
            s = s + p_ref[d]
        summed[...] = s
        loss_ref[...] = summed[pl.ds(ROW_PSCALE, 1), LOSS_LANES]
        chip = pos_ref[1]

        def update(t, idx, g):
            d, mm, vv = _adamw_math(ws[t][idx], g, ms[t][idx], vs[t][idx])
            g_ref, d_ref, mo_ref, vo_ref = outs[t]
            g_ref[idx] = g
            d_ref[idx] = d
            mo_ref[idx] = mm
            vo_ref[idx] = vv

        for l in range(L):
            base = SUBLANES * l
            row = pl.ds(l, 1)
            for k in range(3):
                update(0, (row, slice(k * D, (k + 1) * D)), summed[pl.ds(base + ROW_MOD + k, 1), :])
            update(1, (row, slice(None)), summed[pl.ds(base + ROW_G_PRE, 1), :])
            update(2, (row, slice(None)), summed[pl.ds(base + ROW_G_POST, 1), :])
            update(3, (row, slice(None)), summed[pl.ds(base + ROW_PSCALE, 1), 0:PS])
            for k in range(3):
                g = None
                for j in range(N_CHIPS):
                    wrow, lanes = _wconv_slot(j, k)
                    cand = summed[pl.ds(base + wrow, 1), lanes]
                    g = cand if g is None else jnp.where(chip == j, cand, g)
                update(4, (l, pl.ds(k, 1), slice(None)), g)

    def whole(a):
        return pl.BlockSpec(a.shape, lambda i, p: (0,) * a.ndim)

    ins = [packs] + list(params) + list(moments_m) + list(moments_v)
    out_shape = [jax.ShapeDtypeStruct((1, LANES), F32)]
    for w in params:
        out_shape += [jax.ShapeDtypeStruct(w.shape, F32)] * 4
    outs = pl.pallas_call(
        body, name="small_update",
        grid_spec=pltpu.PrefetchScalarGridSpec(
            num_scalar_prefetch=1, grid=(1,), in_specs=[whole(a) for a in ins],
            out_specs=[whole(a) for a in out_shape],
            scratch_shapes=[pltpu.VMEM(packs.shape[1:], F32)]),
        out_shape=out_shape,
        compiler_params=_params(),
    )(pos, *ins)
    return outs[0], [outs[1 + 4 * t:5 + 4 * t] for t in range(n)]


def _adamw_math(w, g, m, v):
    m = ADAM_B1 * m + (1.0 - ADAM_B1) * g
    v = ADAM_B2 * v + (1.0 - ADAM_B2) * (g * g)
    m_hat = m / (1.0 - ADAM_B1 ** ADAM_STEP)
    v_hat = v / (1.0 - ADAM_B2 ** ADAM_STEP)
    delta = -ADAM_LR * (m_hat / (jnp.sqrt(v_hat) + ADAM_EPS) + ADAM_WD * w)
    return delta, m, v


def _adamw_block(ins, outs):
    w_ref, g_ref, m_ref, v_ref = ins
    go_ref, d_ref, mo_ref, vo_ref = outs
    gv = g_ref[...]
    d, mm, vv = _adamw_math(w_ref[...], gv, m_ref[...], v_ref[...])
    go_ref[...] = gv
    d_ref[...] = d
    mo_ref[...] = mm
    vo_ref[...] = vv


def adamw(groups, name, first, count, steps, acc=None):
    n = len(groups)

    def body(*refs):
        outs = refs[len(refs) - 4 * n:]
        for k in range(n):
            _adamw_block(refs[4 * k:4 * k + 4], outs[4 * k:4 * k + 4])

    specs, out_shape, args = [], [], []
    for group in groups:
        shape = group[0].shape
        spec = pl.BlockSpec((1, shape[1] // steps) + shape[2:],
                            lambda i, s, rest=(0,) * (len(shape) - 2): (first + i, s) + rest)
        specs += [spec] * 4
        out_shape += [jax.ShapeDtypeStruct(shape, F32)] * 4
        args += list(group)
    extra = [] if acc is None else list(acc)
    return pl.pallas_call(
        body, name=name, grid=(count, steps),
        in_specs=specs + [ANY] * len(extra), out_specs=specs, out_shape=out_shape,
        input_output_aliases={4 * n + a: a for a in range(len(extra))},
        compiler_params=_params(VMEM_BIG, n_grid=2),
    )(*args, *extra)


def ada_finish(c_all, dmod, w, m, v):
    L, D, CW = w.shape
    hD = D // 2

    def body(c_ref, d_ref, w_ref, m_ref, v_ref, g_ref, dl_ref, mo_ref, vo_ref):
        cv = c_ref[...]
        z = jnp.zeros_like(cv)
        ca = jnp.concatenate([cv * jax.nn.sigmoid(cv), z], axis=0).astype(BF16)
        dm = jnp.concatenate([d_ref[0], jnp.zeros_like(d_ref[0])], axis=0).astype(BF16)
        g = lax.dot_general(ca, dm, TN, preferred_element_type=F32)
        g_ref[0] = g
        d, mm, vv = _adamw_math(w_ref[0], g, m_ref[0], v_ref[0])
        dl_ref[0] = d
        mo_ref[0] = mm
        vo_ref[0] = vv

    big = pl.BlockSpec((1, hD, CW), lambda l, h: (l, h, 0))
    shape = jax.ShapeDtypeStruct(w.shape, F32)
    return pl.pallas_call(
        body, name="ada_finish", grid=(L, 2),
        in_specs=[pl.BlockSpec((N_DEV, hD), lambda l, h: (0, h)), pl.BlockSpec((1, N_DEV, CW), lambda l, h: (l, 0, 0)),
                  big, big, big],
        out_specs=[big] * 4, out_shape=[shape] * 4,
        compiler_params=_params(VMEM_BIG, n_grid=2),
    )(c_all, dmod, w, m, v)


def kernel(x, c, w_ada, b_ada, g_pre, w_in, w_conv, w_pool, pool_scale, w_out, g_post, loss_target, m_w_ada, m_b_ada, m_g_pre, m_w_in, m_w_conv, m_w_pool, m_pool_scale, m_w_out, m_g_post, v_w_ada, v_b_ada, v_g_pre, v_w_in, v_w_conv, v_w_pool, v_pool_scale, v_w_out, v_g_post):
    L, D, CW = w_in.shape
    RO = w_out.shape[1]
    T = x.shape[1]
    ix, iy, ic = _me()
    chip = 2 * ix + iy
    me_lin = 4 * ix + 2 * iy + ic

    pos = jnp.stack([ic, chip, me_lin]).astype(jnp.int32)
    n_s, n_c = 3, 9

    def gather(bufs, after):
        ss, rs, bufs, tok = xchg_start("gather_start", bufs, 3 * len(bufs), plan_gather, after=after)
        return (ss, rs, bufs), tok

    def ready(flight, after):
        fss, frs, bufs = flight
        return xchg_wait("forward_wait", bufs, fss, frs, 3 * len(bufs), plan_forward, after)

    def arrive_part(flight, which, after, base=0):
        ss, rs, bufs = flight
        sems = tuple(range(base + 3 * which, base + 3 * which + 3))
        (buf,) = xchg_wait("gather_wait", [bufs[which]], ss, rs, 3, plan_gather, after, sems=sems)
        fss, frs, (buf,), tok = xchg_start("forward_start", [buf], 3, plan_forward, sibling_only=True)
        return (fss, frs, [buf]), tok

    n_small = N_DEV - 1 + N_CHIPS - 1
    w_in_of, w_out_of = [None] * L, [None] * L
    gi0, go0 = cast_weights(pos, w_in, w_out, 0, pos)

    def plan_first(refs, x, y, c):
        return plan_small(refs[0:2], x, y, c) + plan_gather(refs[2:3], x, y, c)

    placed = list(place_small(pos, c.reshape(SUBLANES, LANES), w_conv))
    s_ss, s_rs, firsts, token = xchg_start("first_start", placed + [gi0], n_small + 3, plan_first)
    smalls_in = firsts[0:2]
    w_in_of[0] = ((s_ss, s_rs, firsts[2:3]), 0)
    g_pre_l, g_post_l, pscale_l, b_ada_l, token = lax.optimization_barrier((g_pre, g_post, pool_scale, b_ada, token))
    g_pre3, g_post3 = g_pre_l.reshape(L, 1, D), g_post_l.reshape(L, 1, D)
    pscale3 = pscale_l.reshape(L, 1, pool_scale.shape[1])
    c_all3, wconv_all = xchg_wait("small_wait", smalls_in, s_ss, s_rs, n_small, plan_small, [token])
    c_all = c_all3.reshape(N_DEV, D)
    b_my = lax.dynamic_slice_in_dim(b_ada_l, chip * CW, CW, axis=1)
    m_ss, m_rs, mods, token = xchg_start("mod_start", [mod_part(pos, c_all, w_ada, b_my, token)], 3, plan_mod)
    gi1, go1 = cast_weights(pos, w_in, w_out, 1, token)
    flight, token = gather([go0, gi1, go1], [])
    w_out_of[0], w_in_of[1], w_out_of[1] = (flight, 0), (flight, 1), (flight, 2)
    late = []
    for l in range(2, L):
        late += list(cast_weights(pos, w_in, w_out, l, token))
    flight, token = gather(late, [])
    for l in range(2, L):
        w_in_of[l], w_out_of[l] = (flight, 2 * (l - 2)), (flight, 2 * (l - 2) + 1)
    fwd_in, token = arrive_part(*w_in_of[0], [token], base=n_small)
    (mod_all,) = xchg_wait("mod_wait", mods, m_ss, m_rs, 3, plan_mod, [token])
    mod = lax.dynamic_index_in_dim(mod_all, me_lin, axis=2, keepdims=False)
    mod4 = jnp.transpose(mod, (1, 0, 2)).reshape(L, 3, 1, D)

    xs, projs, yas, yps, ys, pooleds = [x.reshape(T, D)], [], [], [], [], []
    wg_in, wg_out = [], []
    for l in range(L):
        (gi,) = ready(fwd_in, [mod4 if l == 0 else xs[l]])
        proj = proj_fwd(xs[l], mod4, g_pre3, gi, l)
        ya, yp, pooled = mix_fwd(proj, wconv_all, w_pool, pscale3, l)
        pooleds.append(pooled)
        fwd_out, token = arrive_part(*w_out_of[l], [ya, yp])
        after = [token]
        if 0 < l < L - 1:
            fwd_in, token = arrive_part(*w_in_of[l + 1], after)
            after = [token]
        (go,) = ready(fwd_out, after)
        wg_in.append(gi)
        wg_out.append(go.reshape(N_CHIPS * RO, D))
        projs.append(proj)
        yas.append(ya)
        yps.append(yp)
        if l + 1 < L:
            xn, yv = out_fwd(ya, yp, wg_out[l], xs[l], mod4, g_post3, l, after[0])
            xs.append(xn)
            if l == 0:
                fwd_in, token = arrive_part(*w_in_of[1], [xn])
        else:
            dx, yv, loss_blk = out_fwd_loss(ya, yp, wg_out[l], xs[l], mod4, g_post3, l, loss_target.reshape(T, D))
        ys.append(yv)

    shapes = (w_in.shape, w_out.shape, w_pool.shape)
    smalls = [None] * L
    acc, flying, sib, token = None, None, None, loss_blk

    def to_chips(sib, after):
        sl, s_ss, s_rs, s_bufs = sib
        s_bufs = xchg_wait("sibling_wait", s_bufs, s_ss, s_rs, n_s, plan_sibling, after)
        chip_parts = add_sibling(pos, s_bufs[0:3], s_bufs[3:6])
        lands = [lax.empty((3,) + a.shape[1:], a.dtype) for a in chip_parts]
        c_ss, c_rs, c_bufs, ctoken = xchg_start("chip_start", list(chip_parts) + lands, n_c, plan_chip)
        return (sl, c_ss, c_rs, c_bufs), ctoken

    def landed(flying, acc, after):
        fl, f_ss, f_rs, f_bufs = flying
        f_bufs = xchg_wait("chip_wait", f_bufs, f_ss, f_rs, n_c, plan_chip, after)
        return sum_chips(pos, f_bufs[0:3], f_bufs[3:6], acc, fl, shapes)

    early = None
    for l in reversed(range(L)):
        dya, dyp, dwo_l, dgate, dgpost = out_bwd(dx, ys[l], yas[l], yps[l], wg_out[l], mod4, g_post3, l, token)
        token = dya
        spreading = None
        if sib is not None:
            arrived = flying
            flying, token = to_chips(sib, [dya])
            if arrived is not None:
                acc = landed(arrived, acc, [token])
                spreading = plan_spread((arrived[0],), ())
                sp_ss, sp_rs, acc, token = xchg_start("spread_start", list(acc), 2, spreading, sibling_only=True)
        du_a, db_a, dc_a, dg_a, du_p, dg_p, dwc, dwp_l, dps = mix_bwd(projs[l], pooleds[l], dya, dyp, wconv_all, w_pool,
                                                                        pscale3, l, token)
        update = None
        if spreading is not None:
            acc = xchg_wait("spread_wait", acc, sp_ss, sp_rs, 2, spreading, [du_a])
            update = (arrived[0], [w_in, acc[0], m_w_in, v_w_in], [w_out, acc[1], m_w_out, v_w_out], early)
        dx, dwi_l, dshift, dscale, dgpre, *rest = in_bwd([du_a, db_a, dc_a, dg_a, du_p, dg_p], wg_in[l], xs[l], dx,
                                                         mod4, g_pre3, l, update)
        early = rest if rest else early
        smalls[l] = (dgpre, dgpost, dshift, dscale, dgate, dps, dwc)
        parts = [dwi_l, dwo_l.reshape(N_CHIPS, RO, D), dwp_l]
        s_lands = [lax.empty((a.shape[0], a.shape[1] // 2) + a.shape[2:], a.dtype) for a in parts]
        s_ss, s_rs, s_bufs, token = xchg_start("sibling_start", parts + s_lands, n_s, plan_sibling, sibling_only=True)
        sib = (l, s_ss, s_rs, s_bufs)
    grad_x = dx.reshape(1, T, D)

    p_ss, p_rs, packs, ptoken = xchg_start("pack_start", [pack_small(pos, smalls, loss_blk)], N_DEV - 1, plan_pack)
    acc = landed(flying, acc, [ptoken, token])
    n_sp = 2 + (N_DEV - 1) * (L - 1)
    spread = plan_spread((1,), tuple(range(1, L)))
    sp_ss, sp_rs, acc, sp_token = xchg_start("spread_start", list(acc), n_sp, spread)
    flying, token = to_chips(sib, [sp_token])
    (packs_all,) = xchg_wait("pack_wait", packs, p_ss, p_rs, N_DEV - 1, plan_pack, [token])
    dmod_all = packs_all.reshape(N_DEV, L, SUBLANES, D)[:, :, ROW_MOD:ROW_MOD + 3].reshape(N_DEV, L, 3 * D)
    dmod_my = jnp.transpose(lax.dynamic_slice_in_dim(dmod_all, chip * CW, CW, axis=2), (1, 0, 2))

    g_w_ada, d_w_ada, nm_w_ada, nv_w_ada = ada_finish(c_all, dmod_my, w_ada, m_w_ada, v_w_ada)
    m_w_conv_l, v_w_conv_l, _ = lax.optimization_barrier((m_w_conv, v_w_conv, yas[0]))
    loss_row, upd = small_update(pos, packs_all, [b_ada, g_pre, g_post, pool_scale, w_conv],
                                 [m_b_ada, m_g_pre, m_g_post, m_pool_scale, m_w_conv_l],
                                 [v_b_ada, v_g_pre, v_g_post, v_pool_scale, v_w_conv_l])
    loss = loss_row[0, 0]
    (g_b_ada, d_b_ada, nm_b_ada, nv_b_ada), (g_g_pre, d_g_pre, nm_g_pre, nv_g_pre) = upd[0], upd[1]
    (g_g_post, d_g_post, nm_g_post, nv_g_post), (g_pscale, d_pscale, nm_pscale, nv_pscale) = upd[2], upd[3]
    g_w_conv, d_w_conv, nm_w_conv, nv_w_conv = upd[4]

    done = [nv_w_ada, nv_w_conv]
    g_w_in, g_w_out, g_w_pool = xchg_wait("spread_wait", acc, sp_ss, sp_rs, n_sp, spread, done)
    early = adamw([[w_in, g_w_in, m_w_in, v_w_in], [w_out, g_w_out, m_w_out, v_w_out]], "adamw_layer", 1, 1, 2, early)

    acc = landed(flying, (g_w_in, g_w_out, g_w_pool), [early[3], early[7]])
    last = plan_spread((0,), (0,))
    n_last = 2 + N_DEV - 1
    l_ss, l_rs, acc, _ = xchg_start("spread_start", list(acc), n_last, last)
    upd_pool = adamw([[w_pool, acc[2], m_w_pool, v_w_pool]], "adamw_w_pool", 1, L - 1, 1)
    r_w_in, r_w_out, r_w_pool = xchg_wait("spread_wait", acc, l_ss, l_rs, n_last, last, [upd_pool[3]])
    (g_w_in, d_w_in, nm_w_in, nv_w_in, g_w_out, d_w_out, nm_w_out, nv_w_out,
     g_w_pool, d_w_pool, nm_w_pool, nv_w_pool) = adamw(
         [[w_in, r_w_in, m_w_in, v_w_in], [w_out, r_w_out, m_w_out, v_w_out], [w_pool, r_w_pool, m_w_pool, v_w_pool]],
         "adamw_layer", 0, 1, 2, list(early) + list(upd_pool))

    return (loss, grad_x,
            g_w_ada, g_b_ada, g_g_pre, g_w_in, g_w_conv, g_w_pool, g_pscale, g_w_out, g_g_post,
            d_w_ada, d_b_ada, d_g_pre, d_w_in, d_w_conv, d_w_pool, d_pscale, d_w_out, d_g_post,
            nm_w_ada, nm_b_ada, nm_g_pre, nm_w_in, nm_w_conv, nm_w_pool, nm_pscale, nm_w_out, nm_g_post,
            nv_w_ada, nv_b_ada, nv_g_pre, nv_w_in, nv_w_conv, nv_w_pool, nv_pscale, nv_w_out, nv_g_post)
```

```python
import functools

import jax
import jax.numpy as jnp
from jax import lax
from jax.experimental import pallas as pl
from jax.experimental.pallas import tpu as pltpu

F32 = jnp.float32
BF16 = jnp.bfloat16
MESH = pl.DeviceIdType.MESH
ANY = pl.BlockSpec(memory_space=pl.ANY)

NORM_EPS = 1e-6
POOL_WINDOWS = (2, 4, 8, 16)
ADAM_LR = 0.001
ADAM_B1 = 0.9
ADAM_B2 = 0.999
ADAM_EPS = 1e-08
ADAM_WD = 0.01
ADAM_STEP = 10

N_CHIPS = 4
N_DEV = 8
LANES = 128
SUBLANES = 8
VMEM_BIG = 56 * 1024 * 1024
HIST = 16
R_CONV = 64
R_POOL = 128

ROW_G_PRE, ROW_G_POST, ROW_MOD, ROW_PSCALE, ROW_WCONV = 0, 1, 2, 5, 6
LOSS_LANES = slice(4 * LANES, 5 * LANES)

NT = (((1,), (1,)), ((), ()))
TN = (((0,), (0,)), ((), ()))


def _params(vmem=None, n_grid=1):
    kw = {}
    if n_grid:
        kw["dimension_semantics"] = ("arbitrary",) * n_grid
    if vmem is not None:
        kw["vmem_limit_bytes"] = vmem
    return pltpu.CompilerParams(**kw)


def _colsum8(v):
    n, d = v.shape
    return v.reshape(n // SUBLANES, SUBLANES, d).sum(axis=0)


def _rms(v):
    return lax.rsqrt(jnp.mean(v * v, axis=-1, keepdims=True) + NORM_EPS)


def _sigmoid(v):
    return 0.5 * jnp.tanh(0.5 * v) + 0.5


def _shift_down(ext, k, rows):
    if k == 0:
        return ext[HIST:HIST + rows]
    return pltpu.roll(ext, k, 0)[HIST:HIST + rows]


def _shift_up(ext, k, rows):
    if k == 0:
        return ext[0:rows]
    return pltpu.roll(ext, ext.shape[0] - k, 0)[0:rows]


def _load_ext(ref, r0, h0, first, rows):
    hist = ref[pl.ds(h0, HIST), :].astype(F32)
    hist = jnp.where(first, 0.0, hist)
    cur = ref[pl.ds(r0, rows), :].astype(F32)
    return jnp.concatenate([hist, cur], axis=0)


def _me():
    return lax.axis_index("x"), lax.axis_index("y"), lax.axis_index("c")


def cast_weights(pos, w_in, w_out, l, after):
    _, D, CW = w_in.shape
    RO = w_out.shape[1]

    def body(pos_ref, wi, wo, after_ref, oi, oo):
        oi[...] = wi[...].astype(BF16)
        oo[...] = wo[...].astype(BF16)

    return pl.pallas_call(
        body, name="cast_w",
        grid_spec=pltpu.PrefetchScalarGridSpec(
            num_scalar_prefetch=1, grid=(2,),
            in_specs=[pl.BlockSpec((None, D // 2, CW), lambda h, p: (l, h, 0)),
                      pl.BlockSpec((None, RO // 2, D), lambda h, p: (l, h, 0)), ANY],
            out_specs=[pl.BlockSpec((D // 2, CW), lambda h, p: (h, p[1])),
                       pl.BlockSpec((None, RO // 2, D), lambda h, p: (p[1], h, 0))]),
        out_shape=[jax.ShapeDtypeStruct((D, N_CHIPS * CW), BF16), jax.ShapeDtypeStruct((N_CHIPS, RO, D), BF16)],
        compiler_params=_params(),
    )(pos, w_in, w_out, after)


def mod_part(pos, c_all, w_ada, b_my, after):
    L, D, CW = w_ada.shape

    def body(pos_ref, c_ref, w_ref, b_ref, after_ref, o_ref):
        cv = c_ref[...]
        ca = (cv * jax.nn.sigmoid(cv)).astype(BF16)
        o_ref[...] = jnp.dot(ca, w_ref[0].astype(BF16), preferred_element_type=F32) + b_ref[0]

    return pl.pallas_call(
        body, name="mod_part",
        grid_spec=pltpu.PrefetchScalarGridSpec(
            num_scalar_prefetch=1, grid=(L,),
            in_specs=[pl.BlockSpec((N_DEV, D), lambda l, p: (0, 0)),
                      pl.BlockSpec((1, D, CW), lambda l, p: (l, 0, 0)),
                      pl.BlockSpec((1, 1, CW), lambda l, p: (l, 0, 0)), ANY],
            out_specs=pl.BlockSpec((None, None, N_DEV, CW), lambda l, p: (p[1], l, 0, 0))),
        out_shape=jax.ShapeDtypeStruct((N_CHIPS, L, N_DEV, CW), F32),
        compiler_params=_params(VMEM_BIG),
    )(pos, c_all, w_ada, b_my.reshape(L, 1, CW), after)


def _mod_row(l, k, D):
    return pl.BlockSpec((None, None, 1, D), lambda *_: (l, k, 0, 0))


def _layer_row(l, D):
    return pl.BlockSpec((None, 1, D), lambda *_: (l, 0, 0))


def proj_fwd(x, mod4, g_pre3, wg, l):
    T, D = x.shape
    NC = wg.shape[1]
    NB = N_CHIPS
    CW = NC // NB
    tm = 512

    def body(x_ref, sh_ref, sc_ref, g_ref, w_ref, o_ref):
        xv = x_ref[...]
        h = (xv * _rms(xv)) * (g_ref[...] * (1.0 + sc_ref[...])) + sh_ref[...]
        hb = h.astype(BF16)
        for j in range(NB):
            cols = slice(j * CW, (j + 1) * CW)
            o_ref[:, cols] = jnp.dot(hb, w_ref[:, cols], preferred_element_type=F32).astype(BF16)

    return pl.pallas_call(
        body, name="proj_fwd", grid=(T // tm,),
        in_specs=[pl.BlockSpec((tm, D), lambda i: (i, 0)), _mod_row(l, 0, D), _mod_row(l, 1, D), _layer_row(l, D),
                  pl.BlockSpec((D, NC), lambda i: (0, 0))],
        out_specs=pl.BlockSpec((tm, NC), lambda i: (i, 0)),
        out_shape=jax.ShapeDtypeStruct((T, NC), BF16),
        compiler_params=_params(VMEM_BIG),
    )(x, mod4, mod4, g_pre3, wg)


N_MIX = 4


def _conv_fwd_block(u_ref, b_ref, c_ref, g_ref, w_ref, o_ref):
    T = u_ref.shape[0]
    R = 2 * R_CONV
    w0 = w_ref[pl.ds(0, 1), :]
    w1 = w_ref[pl.ds(1, 1), :]
    w2 = w_ref[pl.ds(2, 1), :]

    def chunk(i, carry):
        r0 = pl.multiple_of(i * R, R)
        h0 = pl.multiple_of(jnp.maximum(r0 - HIST, 0), HIST)
        first = i == 0
        ca = _load_ext(c_ref, r0, h0, first, R) * _load_ext(u_ref, r0, h0, first, R)
        conv = w2 * ca[HIST:] + w1 * _shift_down(ca, 1, R) + w0 * _shift_down(ca, 2, R)
        g = g_ref[pl.ds(r0, R), :].astype(F32)
        b = b_ref[pl.ds(r0, R), :].astype(F32)
        o_ref[pl.ds(r0, R), :] = (b * conv * (g * _sigmoid(g))).astype(BF16)
        return carry

    lax.fori_loop(0, T // R, chunk, 0)


def _conv_idx(j):
    return jnp.minimum(j, N_MIX - 1)


def _pool_idx(j):
    return jnp.maximum(j - N_MIX, 0)


def _proj_col(T, off, idx):
    return pl.BlockSpec((T, LANES), lambda j: (0, idx(j) + off))


def _causal_window_sum(ext, w):
    s, k = ext, 1
    while k < w:
        s = s + pltpu.roll(s, k, 0)
        k *= 2
    return s


def _anticausal_window_sum(ext, w):
    s, k = ext, 1
    n = ext.shape[0]
    while k < w:
        s = s + pltpu.roll(s, n - k, 0)
        k *= 2
    return s


def _count(r0, rows, w):
    t = r0 + lax.broadcasted_iota(jnp.int32, (rows, LANES), 0)
    return jnp.minimum(t + 1, w).astype(F32)


def _pooled_loop(p_ref, pooled_s, w, T):
    R = R_POOL

    def chunk(i, carry):
        r0 = pl.multiple_of(i * R, R)
        h0 = pl.multiple_of(jnp.maximum(r0 - HIST, 0), HIST)
        ext = _load_ext(p_ref, r0, h0, i == 0, R)
        ws = _causal_window_sum(ext, w)[HIST:]
        pooled_s[pl.ds(r0, R), :] = (ws / _count(r0, R, w) - ext[HIST:]).astype(BF16)
        return carry

    lax.fori_loop(0, T // R, chunk, 0)


def _conv_w_spec(l):
    return pl.BlockSpec((None, None, 3, LANES), lambda j: (_conv_idx(j), l, 0, 0))


def _pool_w_spec(l):
    return pl.BlockSpec((None, None, LANES, LANES), lambda j: (l, _pool_idx(j), 0, 0))


def _pool_s_spec(l):
    return pl.BlockSpec((None, 1, LANES), lambda j: (l, 0, _pool_idx(j)))


def _pool_fwd_group(p_ref, g_ref, w_ref, s_ref, o_ref, pooled_s, mixed_s, w):
    T = p_ref.shape[0]
    R = R_POOL
    _pooled_loop(p_ref, pooled_s, w, T)
    mixed_s[...] = jnp.dot(pooled_s[...], w_ref[...].astype(BF16), preferred_element_type=F32)
    sc = s_ref[...]

    def chunk(i, carry):
        r0 = pl.multiple_of(i * R, R)
        g = g_ref[pl.ds(r0, R), :].astype(F32)
        o_ref[pl.ds(r0, R), :] = (mixed_s[pl.ds(r0, R), :] * sc * (g * _sigmoid(g))).astype(BF16)
        return carry

    lax.fori_loop(0, T // R, chunk, 0)


def mix_fwd(proj, wconv, wpool, pscale3, l):
    T = proj.shape[0]

    def body(u_ref, b_ref, c_ref, g_ref, p_ref, gp_ref, wc_ref, wp_ref, s_ref, ya_ref, yp_ref, pooled_ref, mixed_s):
        j = pl.program_id(0)
        pl.when(j < N_MIX)(functools.partial(_conv_fwd_block, u_ref, b_ref, c_ref, g_ref, wc_ref, ya_ref))
        for k, w in enumerate(POOL_WINDOWS):
            pl.when(j == N_MIX + k)(functools.partial(_pool_fwd_group, p_ref, gp_ref, wp_ref, s_ref, yp_ref,
                                                      pooled_ref, mixed_s, w))

    half = jax.ShapeDtypeStruct((T, N_MIX * LANES), BF16)
    pool_col = pl.BlockSpec((T, LANES), lambda j: (0, _pool_idx(j)))
    return pl.pallas_call(
        body, name="mix_fwd", grid=(2 * N_MIX,),
        in_specs=[_proj_col(T, 0, _conv_idx), _proj_col(T, 4, _conv_idx), _proj_col(T, 8, _conv_idx),
                  _proj_col(T, 12, _conv_idx), _proj_col(T, 16, _pool_idx), _proj_col(T, 20, _pool_idx),
                  _conv_w_spec(l), _pool_w_spec(l), _pool_s_spec(l)],
        out_specs=[pl.BlockSpec((T, LANES), lambda j: (0, _conv_idx(j))), pool_col, pool_col],
        out_shape=[half, half, half],
        scratch_shapes=[pltpu.VMEM((T, LANES), F32)],
        compiler_params=_params(),
    )(proj, proj, proj, proj, proj, proj, wconv, wpool, pscale3)


def out_fwd(ya, yp, wo, x, mod4, g_post3, l, after):
    T, D = x.shape
    H = ya.shape[1]
    tm = 512

    def body(ya_ref, yp_ref, wo_ref, x_ref, gt_ref, g_ref, after_ref, xn_ref, y_ref):
        y = (jnp.dot(ya_ref[...], wo_ref[0:H, :], preferred_element_type=F32)
             + jnp.dot(yp_ref[...], wo_ref[H:2 * H, :], preferred_element_type=F32))
        xn_ref[...] = x_ref[...] + gt_ref[...] * (y * _rms(y) * g_ref[...])
        y_ref[...] = y.astype(BF16)

    tile = pl.BlockSpec((tm, D), lambda i: (i, 0))
    half = pl.BlockSpec((tm, H), lambda i: (i, 0))
    return pl.pallas_call(
        body, name="out_fwd", grid=(T // tm,),
        in_specs=[half, half, pl.BlockSpec((2 * H, D), lambda i: (0, 0)), tile, _mod_row(l, 2, D), _layer_row(l, D),
                  ANY],
        out_specs=[tile, tile],
        out_shape=[jax.ShapeDtypeStruct((T, D), F32), jax.ShapeDtypeStruct((T, D), BF16)],
        compiler_params=_params(VMEM_BIG),
    )(ya, yp, wo, x, mod4, g_post3, after)


def out_fwd_loss(ya, yp, wo, x, mod4, g_post3, l, target):
    T, D = x.shape
    H = ya.shape[1]
    tm = 512
    nt = T // tm

    def body(ya_ref, yp_ref, wo_ref, x_ref, gt_ref, g_ref, t_ref, dx_ref, y_ref, l_ref, acc):
        i = pl.program_id(0)

        @pl.when(i == 0)
        def _():
            acc[...] = jnp.zeros_like(acc)

        y = (jnp.dot(ya_ref[...], wo_ref[0:H, :], preferred_element_type=F32)
             + jnp.dot(yp_ref[...], wo_ref[H:2 * H, :], preferred_element_type=F32))
        y_ref[...] = y.astype(BF16)
        d = (x_ref[...] + gt_ref[...] * (y * _rms(y) * g_ref[...])) - t_ref[...]
        dx_ref[...] = d * (1.0 / D)
        acc[...] += _colsum8(d * d)

        @pl.when(i == nt - 1)
        def _():
            l_ref[...] = jnp.zeros_like(l_ref) + jnp.sum(acc[...]) * (0.5 / D)

    tile = pl.BlockSpec((tm, D), lambda i: (i, 0))
    half = pl.BlockSpec((tm, H), lambda i: (i, 0))
    return pl.pallas_call(
        body, name="out_fwd_loss", grid=(nt,),
        in_specs=[half, half, pl.BlockSpec((2 * H, D), lambda i: (0, 0)), tile, _mod_row(l, 2, D), _layer_row(l, D),
                  tile],
        out_specs=[tile, tile, pl.BlockSpec((SUBLANES, LANES), lambda i: (0, 0))],
        out_shape=[jax.ShapeDtypeStruct((T, D), F32), jax.ShapeDtypeStruct((T, D), BF16),
                   jax.ShapeDtypeStruct((SUBLANES, LANES), F32)],
        scratch_shapes=[pltpu.VMEM((SUBLANES, D), F32)],
        compiler_params=_params(VMEM_BIG),
    )(ya, yp, wo, x, mod4, g_post3, target)


def out_bwd(dx, y, ya, yp, wo, mod4, g_post3, l, after):
    T, D = dx.shape
    H = ya.shape[1]
    tm = 512
    nt = T // tm

    def body(dx_ref, y_ref, ya_ref, yp_ref, wo_ref, gt_ref, g_ref, after_ref,
             dya_ref, dyp_ref, dwo_ref, dgt_ref, dg_ref, acc_w, acc_p):
        i = pl.program_id(0)

        @pl.when(i == 0)
        def _():
            acc_w[...] = jnp.zeros_like(acc_w)
            acc_p[...] = jnp.zeros_like(acc_p)

        yv = y_ref[...].astype(F32)
        dxv = dx_ref[...]
        gg = gt_ref[...] * g_ref[...]
        r = _rms(yv)
        yn = yv * r
        p = dxv * yn
        acc_p[...] += _colsum8(p)
        dy = r * (dxv * gg - yn * jnp.mean(p * gg, axis=-1, keepdims=True))
        dyb = dy.astype(BF16)
        dyc = lax.dot_general(dyb, wo_ref[...], NT, preferred_element_type=F32)
        dya_ref[...] = dyc[:, 0:H].astype(BF16)
        dyp_ref[...] = dyc[:, H:2 * H].astype(BF16)
        acc_w[0:H, :] += lax.dot_general(ya_ref[...], dyb, TN, preferred_element_type=F32)
        acc_w[H:2 * H, :] += lax.dot_general(yp_ref[...], dyb, TN, preferred_element_type=F32)

        @pl.when(i == nt - 1)
        def _():
            dwo_ref[...] = acc_w[...].astype(BF16)
            sp = jnp.sum(acc_p[...], axis=0, keepdims=True)
            dgt_ref[...] = g_ref[...] * sp
            dg_ref[...] = gt_ref[...] * sp

    row = pl.BlockSpec((1, D), lambda i: (0, 0))
    tile = pl.BlockSpec((tm, D), lambda i: (i, 0))
    half = pl.BlockSpec((tm, H), lambda i: (i, 0))
    full = pl.BlockSpec((2 * H, D), lambda i: (0, 0))
    return pl.pallas_call(
        body, name="out_bwd", grid=(nt,),
        in_specs=[tile, tile, half, half, full, _mod_row(l, 2, D), _layer_row(l, D), ANY],
        out_specs=[half, half, full, row, row],
        out_shape=[jax.ShapeDtypeStruct((T, H), BF16), jax.ShapeDtypeStruct((T, H), BF16),
                   jax.ShapeDtypeStruct((2 * H, D), BF16),
                   jax.ShapeDtypeStruct((1, D), F32), jax.ShapeDtypeStruct((1, D), F32)],
        scratch_shapes=[pltpu.VMEM((2 * H, D), F32), pltpu.VMEM((SUBLANES, D), F32)],
        compiler_params=_params(VMEM_BIG),
    )(dx, y, ya, yp, wo, mod4, g_post3, after)


def _conv_bwd_block(u_ref, b_ref, c_ref, g_ref, dy_ref, w_ref, du_ref, db_ref, dc_ref, dg_ref, dw_ref):
    T = u_ref.shape[0]
    R = R_CONV
    nchunk = T // R
    w0 = w_ref[pl.ds(0, 1), :]
    w1 = w_ref[pl.ds(1, 1), :]
    w2 = w_ref[pl.ds(2, 1), :]

    def chunk(k, carry):
        head, a0, a1, a2 = carry
        i = nchunk - 1 - k
        r0 = pl.multiple_of(i * R, R)
        h0 = pl.multiple_of(jnp.maximum(r0 - HIST, 0), HIST)
        first = i == 0
        ue = _load_ext(u_ref, r0, h0, first, R)
        ce = _load_ext(c_ref, r0, h0, first, R)
        ca = ce * ue
        ca0 = ca[HIST:]
        ca1 = _shift_down(ca, 1, R)
        ca2 = _shift_down(ca, 2, R)
        conv = w2 * ca0 + w1 * ca1 + w0 * ca2
        g = g_ref[pl.ds(r0, R), :].astype(F32)
        b = b_ref[pl.ds(r0, R), :].astype(F32)
        dy = dy_ref[pl.ds(r0, R), :].astype(F32)
        sg = _sigmoid(g)
        sl = g * sg
        t = dy * conv
        db_ref[pl.ds(r0, R), :] = (t * sl).astype(BF16)
        dg_ref[pl.ds(r0, R), :] = (t * b * (sg + sl * (1.0 - sg))).astype(BF16)
        dconv = dy * b * sl
        a2 = a2 + _colsum8(dconv * ca0)
        a1 = a1 + _colsum8(dconv * ca1)
        a0 = a0 + _colsum8(dconv * ca2)
        e = jnp.concatenate([dconv, head], axis=0)
        dca = w2 * dconv + w1 * _shift_up(e, 1, R) + w0 * _shift_up(e, 2, R)
        du_ref[pl.ds(r0, R), :] = (dca * ce[HIST:]).astype(BF16)
        dc_ref[pl.ds(r0, R), :] = (dca * ue[HIST:]).astype(BF16)
        return dconv[0:SUBLANES], a0, a1, a2

    z = jnp.zeros((SUBLANES, LANES), F32)
    _, a0, a1, a2 = lax.fori_loop(0, nchunk, chunk, (z, z, z, z))
    dw_ref[pl.ds(0, 1), :] = jnp.sum(a0, axis=0, keepdims=True)
    dw_ref[pl.ds(1, 1), :] = jnp.sum(a1, axis=0, keepdims=True)
    dw_ref[pl.ds(2, 1), :] = jnp.sum(a2, axis=0, keepdims=True)


def _pool_bwd_group(pooled_s, g_ref, dy_ref, w_ref, s_ref, du_ref, dg_ref, dw_ref, ds_ref,
                    mixed_s, dmix_s, dpool_s, w):
    T = pooled_s.shape[0]
    R = R_POOL
    nchunk = T // R
    wb = w_ref[...].astype(BF16)
    mixed_s[...] = jnp.dot(pooled_s[...], wb, preferred_element_type=F32)
    sc = s_ref[...]

    def gate_chunk(i, acc):
        r0 = pl.multiple_of(i * R, R)
        g = g_ref[pl.ds(r0, R), :].astype(F32)
        dy = dy_ref[pl.ds(r0, R), :].astype(F32)
        mixed = mixed_s[pl.ds(r0, R), :]
        sg = _sigmoid(g)
        sl = g * sg
        dg_ref[pl.ds(r0, R), :] = (dy * mixed * sc * (sg + sl * (1.0 - sg))).astype(BF16)
        dms = dy * sl
        dmix_s[pl.ds(r0, R), :] = (dms * sc).astype(BF16)
        return acc + _colsum8(dms * mixed)

    acc = lax.fori_loop(0, nchunk, gate_chunk, jnp.zeros((SUBLANES, LANES), F32))
    ds_ref[...] = jnp.sum(acc, axis=0, keepdims=True)
    dpool_s[pl.ds(0, T), :] = lax.dot_general(dmix_s[...], wb, NT, preferred_element_type=F32)
    dpool_s[pl.ds(T, HIST), :] = jnp.zeros((HIST, LANES), F32)
    dw_ref[...] = lax.dot_general(pooled_s[...], dmix_s[...], TN, preferred_element_type=F32).astype(BF16)

    def back_chunk(i, carry):
        r0 = pl.multiple_of(i * R, R)
        dpe = dpool_s[pl.ds(r0, R + HIST), :]
        e = dpe / _count(r0, R + HIST, w)
        du_ref[pl.ds(r0, R), :] = (_anticausal_window_sum(e, w)[0:R] - dpe[0:R]).astype(BF16)
        return carry

    lax.fori_loop(0, nchunk, back_chunk, 0)


def mix_bwd(proj, pooled, dya, dyp, wconv, wpool, pscale3, l, after):
    T = proj.shape[0]

    def body(u_ref, b_ref, c_ref, g_ref, pooled_ref, gp_ref, dya_ref, dyp_ref, wc_ref, wp_ref, s_ref, after_ref,
             dua_ref, dba_ref, dca_ref, dga_ref, dup_ref, dgp_ref, dwc_ref, dwp_ref, ds_ref,
             mixed_s, dmix_s, dpool_s):
        j = pl.program_id(0)
        pl.when(j < N_MIX)(functools.partial(_conv_bwd_block, u_ref, b_ref, c_ref, g_ref, dya_ref, wc_ref,
                                             dua_ref, dba_ref, dca_ref, dga_ref, dwc_ref))
        for k, w in enumerate(POOL_WINDOWS):
            pl.when(j == N_MIX + k)(functools.partial(_pool_bwd_group, pooled_ref, gp_ref, dyp_ref, wp_ref, s_ref,
                                                      dup_ref, dgp_ref, dwp_ref, ds_ref,
                                                      mixed_s, dmix_s, dpool_s, w))

    sec = jax.ShapeDtypeStruct((T, N_MIX * LANES), BF16)
    conv_col = pl.BlockSpec((T, LANES), lambda j: (0, _conv_idx(j)))
    pool_col = pl.BlockSpec((T, LANES), lambda j: (0, _pool_idx(j)))
    return pl.pallas_call(
        body, name="mix_bwd", grid=(2 * N_MIX,),
        in_specs=[_proj_col(T, 0, _conv_idx), _proj_col(T, 4, _conv_idx), _proj_col(T, 8, _conv_idx),
                  _proj_col(T, 12, _conv_idx), pool_col, _proj_col(T, 20, _pool_idx),
                  conv_col, pool_col, _conv_w_spec(l), _pool_w_spec(l), _pool_s_spec(l), ANY],
        out_specs=[conv_col, conv_col, conv_col, conv_col, pool_col, pool_col,
                   pl.BlockSpec((None, 3, LANES), lambda j: (_conv_idx(j), 0, 0)),
                   pl.BlockSpec((None, LANES, LANES), lambda j: (_pool_idx(j), 0, 0)),
                   pl.BlockSpec((1, LANES), lambda j: (0, _pool_idx(j)))],
        out_shape=[sec] * 6 + [jax.ShapeDtypeStruct((N_MIX, 3, LANES), F32),
                               jax.ShapeDtypeStruct((N_MIX, LANES, LANES), BF16),
                               jax.ShapeDtypeStruct((1, N_MIX * LANES), F32)],
        scratch_shapes=[pltpu.VMEM((T, LANES), F32), pltpu.VMEM((T, LANES), BF16), pltpu.VMEM((T + HIST, LANES), F32)],
        compiler_params=_params(),
    )(proj, proj, proj, proj, pooled, proj, dya, dyp, wconv, wpool, pscale3, after)


def in_bwd(dsecs, wg, x, dxo, mod4, g_pre3, l, update=None):
    T, D = x.shape
    NB = N_CHIPS
    CW = wg.shape[1] // NB
    SW = dsecs[0].shape[1]
    nsec = len(dsecs)
    PW = 256
    assert SW % PW == 0 and CW % PW == 0
    tm = 256
    nt = T // tm
    n_in = nsec + 6
    n_upd = 0 if update is None else 8
    n_acc = 0 if update is None or update[3] is None else 8

    def body(*refs):
        d_refs = refs[0:nsec]
        w_ref, x_ref, dxo_ref, sh_ref, sc_ref, g_ref = refs[nsec:n_in]
        outs = refs[n_in + n_upd + n_acc:]
        dxi_ref, dw_ref, dsh_ref, dsc_ref, dg_ref = outs[0:5]
        acc_w, acc_sh, acc_q = outs[5 + n_upd:]
        i = pl.program_id(0)
        for k in range(0, n_upd, 4):
            _adamw_block(refs[n_in + k:n_in + k + 4], outs[5 + k:5 + k + 4])

        @pl.when(i == 0)
        def _():
            acc_w[...] = jnp.zeros_like(acc_w)
            acc_sh[...] = jnp.zeros_like(acc_sh)
            acc_q[...] = jnp.zeros_like(acc_q)

        xv = x_ref[...]
        r = _rms(xv)
        xh = xv * r
        sg = g_ref[...] * (1.0 + sc_ref[...])
        hb = (xh * sg + sh_ref[...]).astype(BF16)
        dh = lax.dot_general(d_refs[0][...], w_ref[:, 0:SW], NT, preferred_element_type=F32)
        for s in range(1, nsec):
            dh = dh + lax.dot_general(d_refs[s][...], w_ref[:, s * SW:(s + 1) * SW], NT, preferred_element_type=F32)
        for p in range(nsec * SW // PW):
            col = p * PW
            s, so = col // SW, col % SW
            j, jo = col // CW, col % CW
            acc_w[j, :, jo:jo + PW] += lax.dot_general(hb, d_refs[s][:, so:so + PW], TN, preferred_element_type=F32)
        q = dh * xh
        acc_sh[...] += _colsum8(dh)
        acc_q[...] += _colsum8(q)
        dxi_ref[...] = dxo_ref[...] + r * (dh * sg - xh * jnp.mean(q * sg, axis=-1, keepdims=True))

        @pl.when(i == nt - 1)
        def _():
            dw_ref[...] = acc_w[...].astype(BF16)
            sq = jnp.sum(acc_q[...], axis=0, keepdims=True)
            dsh_ref[...] = jnp.sum(acc_sh[...], axis=0, keepdims=True)
            dsc_ref[...] = g_ref[...] * sq
            dg_ref[...] = (1.0 + sc_ref[...]) * sq

    row = pl.BlockSpec((1, D), lambda i: (0, 0))
    tile = pl.BlockSpec((tm, D), lambda i: (i, 0))
    sect = pl.BlockSpec((tm, SW), lambda i: (i, 0))
    rowshape = jax.ShapeDtypeStruct((1, D), F32)
    in_specs = [sect] * nsec + [pl.BlockSpec((D, NB * CW), lambda i: (0, 0)), tile, tile,
                                _mod_row(l, 0, D), _mod_row(l, 1, D), _layer_row(l, D)]
    out_specs = [tile, pl.BlockSpec((NB, D, CW), lambda i: (0, 0, 0)), row, row, row]
    out_shape = [jax.ShapeDtypeStruct((T, D), F32), jax.ShapeDtypeStruct((NB, D, CW), BF16), rowshape, rowshape, rowshape]
    args = [*dsecs, wg, x, dxo, mod4, mod4, g_pre3]
    aliases = {}
    if update is not None:
        layer, of_w_in, of_w_out, acc = update
        for group in (of_w_in, of_w_out):
            _, rows, cols = group[0].shape
            spec = pl.BlockSpec((None, rows // nt, cols), lambda i: (layer, i, 0))
            in_specs += [spec] * 4
            out_specs += [spec] * 4
            out_shape += [jax.ShapeDtypeStruct(group[0].shape, F32)] * 4
            args += list(group)
        if acc is not None:
            aliases = {len(args) + a: 5 + a for a in range(n_acc)}
            in_specs += [ANY] * n_acc
            args += list(acc)
    return pl.pallas_call(
        body, name="in_bwd", grid=(nt,),
        in_specs=in_specs, out_specs=out_specs, out_shape=out_shape, input_output_aliases=aliases,
        scratch_shapes=[pltpu.VMEM((NB, D, CW), F32),
                        pltpu.VMEM((SUBLANES, D), F32), pltpu.VMEM((SUBLANES, D), F32)],
        compiler_params=_params(VMEM_BIG),
    )(*args)


def _rcopy(src, dst, ssem, rsem, dev):
    return pltpu.make_async_remote_copy(src_ref=src, dst_ref=dst, send_sem=ssem, recv_sem=rsem,
                                        device_id=dev, device_id_type=MESH)


def _peers7(x, y, c):
    out = []
    for m in range(1, N_DEV):
        bx, by, bc = (m >> 2) & 1, (m >> 1) & 1, m & 1
        out.append(((1 - x) if bx else x, (1 - y) if by else y, (1 - c) if bc else c))
    return out


HBM = pl.BlockSpec(memory_space=pltpu.HBM)
SEM = pl.BlockSpec(memory_space=pltpu.SEMAPHORE)
SPLIT = pltpu.CompilerParams(has_side_effects=pltpu.SideEffectType.DATAFLOW_SIDE_EFFECTING)


def _hbm(a):
    return pltpu.with_memory_space_constraint(a, pltpu.HBM)


def _chips(x, y):
    return [(1 - x, y), (x, 1 - y), (1 - x, 1 - y)]


SIBLING_BARRIER_ID = 0


def xchg_start(name, bufs, n_copies, plan, sibling_only=False, after=()):
    n = len(bufs)
    after = list(after)

    def body(*refs):
        ssem, rsem, token = refs[n + len(after)], refs[n + len(after) + 1], refs[-1]
        x, y, c = _me()
        if sibling_only:
            barrier = pltpu.get_barrier_semaphore()
            pl.semaphore_signal(barrier, inc=1, device_id=(x, y, 1 - c), device_id_type=MESH)
            pl.semaphore_wait(barrier, 1)
        copies = plan(refs[0:n], x, y, c)
        assert len(copies) == n_copies
        for k, (src, dst, peer, _) in enumerate(copies):
            _rcopy(src, dst, ssem.at[k], rsem.at[k], peer).start()
        token[...] = jnp.zeros_like(token)

    params = dict(has_side_effects=pltpu.SideEffectType.DATAFLOW_SIDE_EFFECTING)
    if sibling_only:
        params["collective_id"] = SIBLING_BARRIER_ID
    outs = pl.pallas_call(
        body, name=name,
        in_specs=[HBM] * n + [ANY] * len(after),
        out_specs=[SEM, SEM] + [HBM] * n + [pl.BlockSpec(memory_space=pltpu.VMEM)],
        out_shape=([pltpu.SemaphoreType.DMA((n_copies,))] * 2 + [pltpu.HBM(b.shape, b.dtype) for b in bufs]
                   + [jax.ShapeDtypeStruct((SUBLANES, LANES), F32)]),
        input_output_aliases={a: 2 + a for a in range(n)},
        compiler_params=pltpu.CompilerParams(**params),
    )(*[_hbm(b) for b in bufs], *after)
    return outs[0], outs[1], list(outs[2:2 + n]), outs[-1]


def xchg_wait(name, bufs, ssem, rsem, n_copies, plan, after, sems=None):
    n = len(bufs)
    after = list(after)
    sems = tuple(range(n_copies)) if sems is None else tuple(sems)
    assert len(sems) == n_copies

    def body(*refs):
        ssem_ref, rsem_ref = refs[n], refs[n + 1]
        copies = plan(refs[0:n], *_me())
        assert len(copies) == n_copies
        for k, (src, _, peer, land) in zip(sems, copies):
            cp = _rcopy(src, land, ssem_ref.at[k], rsem_ref.at[k], peer)
            cp.wait_send()
            cp.wait_recv()

    outs = pl.pallas_call(
        body, name=name,
        in_specs=[HBM] * n + [SEM, SEM] + [ANY] * len(after), out_specs=[HBM] * n,
        out_shape=[pltpu.HBM(b.shape, b.dtype) for b in bufs],
        input_output_aliases={a: a for a in range(n)},
        compiler_params=SPLIT,
    )(*bufs, ssem, rsem, *after)
    return list(outs)


def _shard_half(buf, chip, half):
    if len(buf.shape) == 2:
        h, w = buf.shape[0] // 2, buf.shape[1] // N_CHIPS
        return buf.at[pl.ds(half * h, h), pl.ds(chip * w, w)]
    h = buf.shape[1] // 2
    return buf.at[chip, pl.ds(half * h, h)]


def plan_gather(refs, x, y, c):
    out = []
    for buf in refs:
        own = _shard_half(buf, 2 * x + y, c)
        for (px, py) in _chips(x, y):
            out.append((own, own, (px, py, c), _shard_half(buf, 2 * px + py, c)))
    return out


def plan_forward(refs, x, y, c):
    out = []
    for (px, py) in _chips(x, y):
        for buf in refs:
            landed = _shard_half(buf, 2 * px + py, c)
            out.append((landed, landed, (x, y, 1 - c), _shard_half(buf, 2 * px + py, 1 - c)))
    return out


def plan_sibling(refs, x, y, c):
    n = len(refs) // 2
    out = []
    for a in range(n):
        h = refs[a].shape[1] // 2
        out.append((refs[a].at[:, pl.ds((1 - c) * h, h)], refs[n + a], (x, y, 1 - c), refs[n + a]))
    return out


def plan_chip(refs, x, y, c):
    n = len(refs) // 2
    out = []
    for j, (px, py) in enumerate(_chips(x, y)):
        for a in range(n):
            out.append((refs[a].at[2 * px + py], refs[n + a].at[j], (px, py, c), refs[n + a].at[j]))
    return out


def plan_mod(refs, x, y, c):
    (mods,) = refs
    mine = mods.at[2 * x + y]
    return [(mine, mine, (px, py, c), mods.at[2 * px + py]) for (px, py) in _chips(x, y)]


def plan_pack(refs, x, y, c):
    (packs,) = refs
    mine = packs.at[4 * x + 2 * y + c]
    return [(mine, mine, peer, packs.at[4 * peer[0] + 2 * peer[1] + peer[2]]) for peer in _peers7(x, y, c)]


def plan_spread(layers, wp_layers):
    def plan(refs, x, y, c):
        gi, go, gp = refs
        hD, hR, hP = gi.shape[1] // 2, go.shape[1] // 2, gp.shape[2] // 2
        sib = (x, y, 1 - c)
        out = []
        for l in layers:
            mine = gi.at[l, pl.ds(c * hD, hD)]
            out.append((mine, mine, sib, gi.at[l, pl.ds((1 - c) * hD, hD)]))
            mine = go.at[l, pl.ds(c * hR, hR)]
            out.append((mine, mine, sib, go.at[l, pl.ds((1 - c) * hR, hR)]))
        for l in wp_layers:
            mine = gp.at[l, 2 * x + y, pl.ds(c * hP, hP)]
            for peer in _peers7(x, y, c):
                out.append((mine, mine, peer, gp.at[l, 2 * peer[0] + peer[1], pl.ds(peer[2] * hP, hP)]))
        return out

    return plan


def place_small(pos, c8, wc):
    L = wc.shape[0]

    def body(pos_ref, c_ref, wc_ref, call_ref, wcall_ref):
        call_ref[...] = c_ref[...]
        wcall_ref[...] = wc_ref[...]

    return pl.pallas_call(
        body, name="place_small",
        grid_spec=pltpu.PrefetchScalarGridSpec(
            num_scalar_prefetch=1, grid=(1,),
            in_specs=[pl.BlockSpec((SUBLANES, LANES), lambda i, p: (0, 0)),
                      pl.BlockSpec((L, 3, LANES), lambda i, p: (0, 0, 0))],
            out_specs=[pl.BlockSpec((None, SUBLANES, LANES), lambda i, p: (p[2], 0, 0)),
                       pl.BlockSpec((None, L, 3, LANES), lambda i, p: (p[1], 0, 0, 0))]),
        out_shape=[jax.ShapeDtypeStruct((N_DEV, SUBLANES, LANES), F32),
                   jax.ShapeDtypeStruct((N_CHIPS, L, 3, LANES), F32)],
        compiler_params=_params(),
    )(pos, c8, wc)


def plan_small(refs, x, y, c):
    call, wcall = refs
    mine = call.at[4 * x + 2 * y + c]
    out = [(mine, mine, peer, call.at[4 * peer[0] + 2 * peer[1] + peer[2]]) for peer in _peers7(x, y, c)]
    mine = wcall.at[2 * x + y]
    out += [(mine, mine, (px, py, c), wcall.at[2 * px + py]) for (px, py) in _chips(x, y)]
    return out


def add_sibling(cidx, mine, sib):
    def body(c_ref, *refs):
        for a in range(3):
            m, s, o = refs[a], refs[3 + a], refs[6 + a]
            o[...] = (m[...].astype(F32) + s[...].astype(F32)).astype(BF16)

    per_step = 2

    def mine_spec(a):
        h = a.shape[1] // 2
        return pl.BlockSpec((per_step, h, a.shape[2]), lambda j, c_ref: (j, c_ref[0], 0))

    def sib_spec(a):
        return pl.BlockSpec((per_step,) + a.shape[1:], lambda j, c_ref: (j, 0, 0))

    return pl.pallas_call(
        body, name="add_sibling",
        grid_spec=pltpu.PrefetchScalarGridSpec(
            num_scalar_prefetch=1, grid=(N_CHIPS // per_step,),
            in_specs=[mine_spec(a) for a in mine] + [sib_spec(a) for a in sib],
            out_specs=[sib_spec(a) for a in sib]),
        out_shape=[jax.ShapeDtypeStruct(a.shape, BF16) for a in sib],
        compiler_params=_params(VMEM_BIG),
    )(cidx, *mine, *sib)


def sum_chips(pos, own, rb, acc, l, shapes):
    nq = 2
    n_in = 6 + (3 if acc is not None else 0)

    def body(pos_ref, *refs):
        for a in range(3):
            m, b, o = refs[a], refs[3 + a], refs[n_in + a]
            s = m[...].astype(F32)
            for j in range(3):
                s = s + b[j].astype(F32)
            o[...] = s

    def own_spec(a):
        return pl.BlockSpec((None, a.shape[1] // nq, a.shape[2]), lambda q, p: (p[1], q, 0))

    def rb_spec(a):
        return pl.BlockSpec((3, a.shape[1] // nq, a.shape[2]), lambda q, p: (0, q, 0))

    hi, ho, hp = own[0].shape[1] // nq, own[1].shape[1] // nq, own[2].shape[1] // nq
    out_specs = [pl.BlockSpec((None, hi, shapes[0][2]), lambda q, p: (l, p[0] * nq + q, 0)),
                 pl.BlockSpec((None, ho, shapes[1][2]), lambda q, p: (l, p[0] * nq + q, 0)),
                 pl.BlockSpec((None, None, hp, LANES), lambda q, p: (l, p[1], p[0] * nq + q, 0))]
    in_specs = [own_spec(a) for a in own] + [rb_spec(a) for a in rb]
    args = list(own) + list(rb)
    aliases = {}
    if acc is not None:
        in_specs += [ANY] * 3
        args += list(acc)
        aliases = {7: 0, 8: 1, 9: 2}
    return pl.pallas_call(
        body, name="sum_chips",
        grid_spec=pltpu.PrefetchScalarGridSpec(num_scalar_prefetch=1, grid=(nq,), in_specs=in_specs, out_specs=out_specs),
        out_shape=[jax.ShapeDtypeStruct(s, F32) for s in shapes],
        input_output_aliases=aliases,
        compiler_params=_params(VMEM_BIG),
    )(pos, *args)


def _wconv_slot(chip, tap):
    idx = 3 * chip + tap
    return ROW_WCONV + idx // SUBLANES, slice((idx % SUBLANES) * LANES, (idx % SUBLANES + 1) * LANES)


def pack_small(pos, per_layer, loss_blk):
    L = len(per_layer)
    D = per_layer[0][0].shape[1]

    def body(pos_ref, *refs):
        o = refs[-1]
        lb = refs[-2]
        o[...] = jnp.zeros_like(o)
        for l in range(L):
            dgpre, dgpost, dsh, dsc, dgt, dps, dwc = refs[7 * l:7 * l + 7]
            base = SUBLANES * l
            o[pl.ds(base + ROW_G_PRE, 1), :] = dgpre[...]
            o[pl.ds(base + ROW_G_POST, 1), :] = dgpost[...]
            for r, src in enumerate((dsh, dsc, dgt)):
                o[pl.ds(base + ROW_MOD + r, 1), :] = src[...]
            o[pl.ds(base + ROW_PSCALE, 1), 0:dps.shape[1]] = dps[...]
            for j in range(dwc.shape[0]):
                for k in range(3):
                    row, lanes = _wconv_slot(j, k)
                    o[pl.ds(base + row, 1), lanes] = dwc[j, pl.ds(k, 1), :]
        o[pl.ds(ROW_PSCALE, 1), LOSS_LANES] = lb[pl.ds(0, 1), :]

    flat = [a for layer in per_layer for a in layer] + [loss_blk]

    def whole(a):
        return pl.BlockSpec(a.shape, lambda i, p: (0,) * a.ndim)

    return pl.pallas_call(
        body, name="pack_small",
        grid_spec=pltpu.PrefetchScalarGridSpec(
            num_scalar_prefetch=1, grid=(1,), in_specs=[whole(a) for a in flat],
            out_specs=pl.BlockSpec((None, L * SUBLANES, D), lambda i, p: (p[2], 0, 0))),
        out_shape=jax.ShapeDtypeStruct((N_DEV, L * SUBLANES, D), F32),
        compiler_params=_params(),
    )(pos, *flat)


def small_update(pos, packs, params, moments_m, moments_v):
    n = len(params)
    L, D = params[1].shape
    PS = params[3].shape[1]

    def body(pos_ref, p_ref, *refs):
        ws, ms, vs = refs[0:n], refs[n:2 * n], refs[2 * n:3 * n]
        loss_ref = refs[3 * n]
        outs = [refs[3 * n + 1 + 4 * t:3 * n + 5 + 4 * t] for t in range(n)]
        summed = refs[-1]
        s = p_ref[0]
        for d in range(1, N_DEV):
            s = s + p_ref[d]
        summed[...] = s
        loss_ref[...] = summed[pl.ds(ROW_PSCALE, 1), LOSS_LANES]
        chip = pos_ref[1]

        def update(t, idx, g):
            d, mm, vv = _adamw_math(ws[t][idx], g, ms[t][idx], vs[t][idx])
            g_ref, d_ref, mo_ref, vo_ref = outs[t]
            g_ref[idx] = g
            d_ref[idx] = d
            mo_ref[idx] = mm
            vo_ref[idx] = vv

        for l in range(L):
            base = SUBLANES * l
            row = pl.ds(l, 1)
            for k in range(3):
                update(0, (row, slice(k * D, (k + 1) * D)), summed[pl.ds(base + ROW_MOD + k, 1), :])
            update(1, (row, slice(None)), summed[pl.ds(base + ROW_G_PRE, 1), :])
            update(2, (row, slice(None)), summed[pl.ds(base + ROW_G_POST, 1), :])
            update(3, (row, slice(None)), summed[pl.ds(base + ROW_PSCALE, 1), 0:PS])
            for k in range(3):
                g = None
                for j in range(N_CHIPS):
                    wrow, lanes = _wconv_slot(j, k)
                    cand = summed[pl.ds(base + wrow, 1), lanes]
                    g = cand if g is None else jnp.where(chip == j, cand, g)
                update(4, (l, pl.ds(k, 1), slice(None)), g)

    def whole(a):
        return pl.BlockSpec(a.shape, lambda i, p: (0,) * a.ndim)

    ins = [packs] + list(params) + list(moments_m) + list(moments_v)
    out_shape = [jax.ShapeDtypeStruct((1, LANES), F32)]
    for w in params:
        out_shape += [jax.ShapeDtypeStruct(w.shape, F32)] * 4
    outs = pl.pallas_call(
        body, name="small_update",
        grid_spec=pltpu.PrefetchScalarGridSpec(
            num_scalar_prefetch=1, grid=(1,), in_specs=[whole(a) for a in ins],
            out_specs=[whole(a) for a in out_shape],
            scratch_shapes=[pltpu.VMEM(packs.shape[1:], F32)]),
        out_shape=out_shape,
        compiler_params=_params(),
    )(pos, *ins)
    return outs[0], [outs[1 + 4 * t:5 + 4 * t] for t in range(n)]


def _adamw_math(w, g, m, v):
    m = ADAM_B1 * m + (1.0 - ADAM_B1) * g
    v = ADAM_B2 * v + (1.0 - ADAM_B2) * (g * g)
    m_hat = m / (1.0 - ADAM_B1 ** ADAM_STEP)
    v_hat = v / (1.0 - ADAM_B2 ** ADAM_STEP)
    delta = -ADAM_LR * (m_hat / (jnp.sqrt(v_hat) + ADAM_EPS) + ADAM_WD * w)
    return delta, m, v


def _adamw_block(ins, outs):
    w_ref, g_ref, m_ref, v_ref = ins
    go_ref, d_ref, mo_ref, vo_ref = outs
    gv = g_ref[...]
    d, mm, vv = _adamw_math(w_ref[...], gv, m_ref[...], v_ref[...])
    go_ref[...] = gv
    d_ref[...] = d
    mo_ref[...] = mm
    vo_ref[...] = vv


def adamw(groups, name, first, count, steps, acc=None):
    n = len(groups)

    def body(*refs):
        outs = refs[len(refs) - 4 * n:]
        for k in range(n):
            _adamw_block(refs[4 * k:4 * k + 4], outs[4 * k:4 * k + 4])

    specs, out_shape, args = [], [], []
    for group in groups:
        shape = group[0].shape
        spec = pl.BlockSpec((1, shape[1] // steps) + shape[2:],
                            lambda i, s, rest=(0,) * (len(shape) - 2): (first + i, s) + rest)
        specs += [spec] * 4
        out_shape += [jax.ShapeDtypeStruct(shape, F32)] * 4
        args += list(group)
    extra = [] if acc is None else list(acc)
    return pl.pallas_call(
        body, name=name, grid=(count, steps),
        in_specs=specs + [ANY] * len(extra), out_specs=specs, out_shape=out_shape,
        input_output_aliases={4 * n + a: a for a in range(len(extra))},
        compiler_params=_params(VMEM_BIG, n_grid=2),
    )(*args, *extra)


def ada_finish(c_all, dmod, w, m, v):
    L, D, CW = w.shape
    hD = D // 2

    def body(c_ref, d_ref, w_ref, m_ref, v_ref, g_ref, dl_ref, mo_ref, vo_ref):
        cv = c_ref[...]
        z = jnp.zeros_like(cv)
        ca = jnp.concatenate([cv * jax.nn.sigmoid(cv), z], axis=0).astype(BF16)
        dm = jnp.concatenate([d_ref[0], jnp.zeros_like(d_ref[0])], axis=0).astype(BF16)
        g = lax.dot_general(ca, dm, TN, preferred_element_type=F32)
        g_ref[0] = g
        d, mm, vv = _adamw_math(w_ref[0], g, m_ref[0], v_ref[0])
        dl_ref[0] = d
        mo_ref[0] = mm
        vo_ref[0] = vv

    big = pl.BlockSpec((1, hD, CW), lambda l, h: (l, h, 0))
    shape = jax.ShapeDtypeStruct(w.shape, F32)
    return pl.pallas_call(
        body, name="ada_finish", grid=(L, 2),
        in_specs=[pl.BlockSpec((N_DEV, hD), lambda l, h: (0, h)), pl.BlockSpec((1, N_DEV, CW), lambda l, h: (l, 0, 0)),
                  big, big, big],
        out_specs=[big] * 4, out_shape=[shape] * 4,
        compiler_params=_params(VMEM_BIG, n_grid=2),
    )(c_all, dmod, w, m, v)


def kernel(x, c, w_ada, b_ada, g_pre, w_in, w_conv, w_pool, pool_scale, w_out, g_post, loss_target, m_w_ada, m_b_ada, m_g_pre, m_w_in, m_w_conv, m_w_pool, m_pool_scale, m_w_out, m_g_post, v_w_ada, v_b_ada, v_g_pre, v_w_in, v_w_conv, v_w_pool, v_pool_scale, v_w_out, v_g_post):
    L, D, CW = w_in.shape
    RO = w_out.shape[1]
    T = x.shape[1]
    ix, iy, ic = _me()
    chip = 2 * ix + iy
    me_lin = 4 * ix + 2 * iy + ic

    pos = jnp.stack([ic, chip, me_lin]).astype(jnp.int32)
    n_s, n_c = 3, 9

    def gather(bufs, after):
        ss, rs, bufs, tok = xchg_start("gather_start", bufs, 3 * len(bufs), plan_gather, after=after)
        return (ss, rs, bufs), tok

    def ready(flight, after):
        fss, frs, bufs = flight
        return xchg_wait("forward_wait", bufs, fss, frs, 3 * len(bufs), plan_forward, after)

    def arrive_part(flight, which, after, base=0):
        ss, rs, bufs = flight
        sems = tuple(range(base + 3 * which, base + 3 * which + 3))
        (buf,) = xchg_wait("gather_wait", [bufs[which]], ss, rs, 3, plan_gather, after, sems=sems)
        fss, frs, (buf,), tok = xchg_start("forward_start", [buf], 3, plan_forward, sibling_only=True)
        return (fss, frs, [buf]), tok

    n_small = N_DEV - 1 + N_CHIPS - 1
    w_in_of, w_out_of = [None] * L, [None] * L
    gi0, go0 = cast_weights(pos, w_in, w_out, 0, pos)

    def plan_first(refs, x, y, c):
        return plan_small(refs[0:2], x, y, c) + plan_gather(refs[2:3], x, y, c)

    placed = list(place_small(pos, c.reshape(SUBLANES, LANES), w_conv))
    s_ss, s_rs, firsts, token = xchg_start("first_start", placed + [gi0], n_small + 3, plan_first)
    smalls_in = firsts[0:2]
    w_in_of[0] = ((s_ss, s_rs, firsts[2:3]), 0)
    g_pre_l, g_post_l, pscale_l, b_ada_l, token = lax.optimization_barrier((g_pre, g_post, pool_scale, b_ada, token))
    g_pre3, g_post3 = g_pre_l.reshape(L, 1, D), g_post_l.reshape(L, 1, D)
    pscale3 = pscale_l.reshape(L, 1, pool_scale.shape[1])
    c_all3, wconv_all = xchg_wait("small_wait", smalls_in, s_ss, s_rs, n_small, plan_small, [token])
    c_all = c_all3.reshape(N_DEV, D)
    b_my = lax.dynamic_slice_in_dim(b_ada_l, chip * CW, CW, axis=1)
    m_ss, m_rs, mods, token = xchg_start("mod_start", [mod_part(pos, c_all, w_ada, b_my, token)], 3, plan_mod)
    gi1, go1 = cast_weights(pos, w_in, w_out, 1, token)
    flight, token = gather([go0, gi1, go1], [])
    w_out_of[0], w_in_of[1], w_out_of[1] = (flight, 0), (flight, 1), (flight, 2)
    late = []
    for l in range(2, L):
        late += list(cast_weights(pos, w_in, w_out, l, token))
    flight, token = gather(late, [])
    for l in range(2, L):
        w_in_of[l], w_out_of[l] = (flight, 2 * (l - 2)), (flight, 2 * (l - 2) + 1)
    fwd_in, token = arrive_part(*w_in_of[0], [token], base=n_small)
    m_w_conv_l, v_w_conv_l, token = lax.optimization_barrier((m_w_conv, v_w_conv, token))
    (mod_all,) = xchg_wait("mod_wait", mods, m_ss, m_rs, 3, plan_mod, [token])
    mod = lax.dynamic_index_in_dim(mod_all, me_lin, axis=2, keepdims=False)
    mod4 = jnp.transpose(mod, (1, 0, 2)).reshape(L, 3, 1, D)

    xs, projs, yas, yps, ys, pooleds = [x.reshape(T, D)], [], [], [], [], []
    wg_in, wg_out = [], []
    for l in range(L):
        (gi,) = ready(fwd_in, [mod4 if l == 0 else xs[l]])
        proj = proj_fwd(xs[l], mod4, g_pre3, gi, l)
        ya, yp, pooled = mix_fwd(proj, wconv_all, w_pool, pscale3, l)
        pooleds.append(pooled)
        fwd_out, token = arrive_part(*w_out_of[l], [ya, yp])
        after = [token]
        if 0 < l < L - 1:
            fwd_in, token = arrive_part(*w_in_of[l + 1], after)
            after = [token]
        (go,) = ready(fwd_out, after)
        wg_in.append(gi)
        wg_out.append(go.reshape(N_CHIPS * RO, D))
        projs.append(proj)
        yas.append(ya)
        yps.append(yp)
        if l + 1 < L:
            xn, yv = out_fwd(ya, yp, wg_out[l], xs[l], mod4, g_post3, l, after[0])
            xs.append(xn)
            if l == 0:
                fwd_in, token = arrive_part(*w_in_of[1], [xn])
        else:
            dx, yv, loss_blk = out_fwd_loss(ya, yp, wg_out[l], xs[l], mod4, g_post3, l, loss_target.reshape(T, D))
        ys.append(yv)

    shapes = (w_in.shape, w_out.shape, w_pool.shape)
    smalls = [None] * L
    acc, flying, sib, token = None, None, None, loss_blk

    def to_chips(sib, after):
        sl, s_ss, s_rs, s_bufs = sib
        s_bufs = xchg_wait("sibling_wait", s_bufs, s_ss, s_rs, n_s, plan_sibling, after)
        chip_parts = add_sibling(pos, s_bufs[0:3], s_bufs[3:6])
        lands = [lax.empty((3,) + a.shape[1:], a.dtype) for a in chip_parts]
        c_ss, c_rs, c_bufs, ctoken = xchg_start("chip_start", list(chip_parts) + lands, n_c, plan_chip)
        return (sl, c_ss, c_rs, c_bufs), ctoken

    def landed(flying, acc, after):
        fl, f_ss, f_rs, f_bufs = flying
        f_bufs = xchg_wait("chip_wait", f_bufs, f_ss, f_rs, n_c, plan_chip, after)
        return sum_chips(pos, f_bufs[0:3], f_bufs[3:6], acc, fl, shapes)

    early = None
    for l in reversed(range(L)):
        dya, dyp, dwo_l, dgate, dgpost = out_bwd(dx, ys[l], yas[l], yps[l], wg_out[l], mod4, g_post3, l, token)
        token = dya
        spreading = None
        if sib is not None:
            arrived = flying
            flying, token = to_chips(sib, [dya])
            if arrived is not None:
                acc = landed(arrived, acc, [token])
                spreading = plan_spread((arrived[0],), ())
                sp_ss, sp_rs, acc, token = xchg_start("spread_start", list(acc), 2, spreading, sibling_only=True)
        du_a, db_a, dc_a, dg_a, du_p, dg_p, dwc, dwp_l, dps = mix_bwd(projs[l], pooleds[l], dya, dyp, wconv_all, w_pool,
                                                                        pscale3, l, token)
        update = None
        if spreading is not None:
            acc = xchg_wait("spread_wait", acc, sp_ss, sp_rs, 2, spreading, [du_a])
            update = (arrived[0], [w_in, acc[0], m_w_in, v_w_in], [w_out, acc[1], m_w_out, v_w_out], early)
        dx, dwi_l, dshift, dscale, dgpre, *rest = in_bwd([du_a, db_a, dc_a, dg_a, du_p, dg_p], wg_in[l], xs[l], dx,
                                                         mod4, g_pre3, l, update)
        early = rest if rest else early
        smalls[l] = (dgpre, dgpost, dshift, dscale, dgate, dps, dwc)
        parts = [dwi_l, dwo_l.reshape(N_CHIPS, RO, D), dwp_l]
        s_lands = [lax.empty((a.shape[0], a.shape[1] // 2) + a.shape[2:], a.dtype) for a in parts]
        s_ss, s_rs, s_bufs, token = xchg_start("sibling_start", parts + s_lands, n_s, plan_sibling, sibling_only=True)
        sib = (l, s_ss, s_rs, s_bufs)
    grad_x = dx.reshape(1, T, D)

    p_ss, p_rs, packs, ptoken = xchg_start("pack_start", [pack_small(pos, smalls, loss_blk)], N_DEV - 1, plan_pack)
    acc = landed(flying, acc, [ptoken, token])
    n_sp = 2 + (N_DEV - 1) * (L - 1)
    spread = plan_spread((1,), tuple(range(1, L)))
    sp_ss, sp_rs, acc, sp_token = xchg_start("spread_start", list(acc), n_sp, spread)
    flying, token = to_chips(sib, [sp_token])
    (packs_all,) = xchg_wait("pack_wait", packs, p_ss, p_rs, N_DEV - 1, plan_pack, [token])
    dmod_all = packs_all.reshape(N_DEV, L, SUBLANES, D)[:, :, ROW_MOD:ROW_MOD + 3].reshape(N_DEV, L, 3 * D)
    dmod_my = jnp.transpose(lax.dynamic_slice_in_dim(dmod_all, chip * CW, CW, axis=2), (1, 0, 2))

    g_w_ada, d_w_ada, nm_w_ada, nv_w_ada = ada_finish(c_all, dmod_my, w_ada, m_w_ada, v_w_ada)
    packs_late, _ = lax.optimization_barrier((packs_all, nv_w_ada))
    loss_row, upd = small_update(pos, packs_late, [b_ada, g_pre, g_post, pool_scale, w_conv],
                                 [m_b_ada, m_g_pre, m_g_post, m_pool_scale, m_w_conv_l],
                                 [v_b_ada, v_g_pre, v_g_post, v_pool_scale, v_w_conv_l])
    loss = loss_row[0, 0]
    (g_b_ada, d_b_ada, nm_b_ada, nv_b_ada), (g_g_pre, d_g_pre, nm_g_pre, nv_g_pre) = upd[0], upd[1]
    (g_g_post, d_g_post, nm_g_post, nv_g_post), (g_pscale, d_pscale, nm_pscale, nv_pscale) = upd[2], upd[3]
    g_w_conv, d_w_conv, nm_w_conv, nv_w_conv = upd[4]

    done = [nv_w_ada, nv_w_conv]
    g_w_in, g_w_out, g_w_pool = xchg_wait("spread_wait", acc, sp_ss, sp_rs, n_sp, spread, done)
    early = adamw([[w_in, g_w_in, m_w_in, v_w_in], [w_out, g_w_out, m_w_out, v_w_out]], "adamw_layer", 1, 1, 2, early)

    acc = landed(flying, (g_w_in, g_w_out, g_w_pool), [early[3], early[7]])
    last = plan_spread((0,), (0,))
    n_last = 2 + N_DEV - 1
    l_ss, l_rs, acc, _ = xchg_start("spread_start", list(acc), n_last, last)
    upd_pool = adamw([[w_pool, acc[2], m_w_pool, v_w_pool]], "adamw_w_pool", 1, L - 1, 1)
    r_w_in, r_w_out, r_w_pool = xchg_wait("spread_wait", acc, l_ss, l_rs, n_last, last, [upd_pool[3]])
    (g_w_in, d_w_in, nm_w_in, nv_w_in, g_w_out, d_w_out, nm_w_out, nv_w_out,
     g_w_pool, d_w_pool, nm_w_pool, nv_w_pool) = adamw(
         [[w_in, r_w_in, m_w_in, v_w_in], [w_out, r_w_out, m_w_out, v_w_out], [w_pool, r_w_pool, m_w_pool, v_w_pool]],
         "adamw_layer", 0, 1, 2, list(early) + list(upd_pool))

    return (loss, grad_x,
            g_w_ada, g_b_ada, g_g_pre, g_w_in, g_w_conv, g_w_pool, g_pscale, g_w_out, g_g_post,
            d_w_ada, d_b_ada, d_g_pre, d_w_in, d_w_conv, d_w_pool, d_pscale, d_w_out, d_g_post,
            nm_w_ada, nm_b_ada, nm_g_pre, nm_w_in, nm_w_conv, nm_w_pool, nm_pscale, nm_w_out, nm_g_post,
            nv_w_ada, nv_b_ada, nv_g_pre, nv_w_in, nv_w_conv, nv_w_pool, nv_pscale, nv_w_out, nv_g_post)
```

```python
import functools

import jax
import jax.numpy as jnp
from jax import lax
from jax.experimental import pallas as pl
from jax.experimental.pallas import tpu as pltpu

F32 = jnp.float32
BF16 = jnp.bfloat16
MESH = pl.DeviceIdType.MESH
ANY = pl.BlockSpec(memory_space=pl.ANY)

NORM_EPS = 1e-6
POOL_WINDOWS = (2, 4, 8, 16)
ADAM_LR = 0.001
ADAM_B1 = 0.9
ADAM_B2 = 0.999
ADAM_EPS = 1e-08
ADAM_WD = 0.01
ADAM_STEP = 10

N_CHIPS = 4
N_DEV = 8
LANES = 128
SUBLANES = 8
VMEM_BIG = 56 * 1024 * 1024
HIST = 16
R_CONV = 64
R_POOL = 128

ROW_G_PRE, ROW_G_POST, ROW_MOD, ROW_PSCALE, ROW_WCONV = 0, 1, 2, 5, 6
LOSS_LANES = slice(4 * LANES, 5 * LANES)

NT = (((1,), (1,)), ((), ()))
TN = (((0,), (0,)), ((), ()))


def _params(vmem=None, n_grid=1):
    kw = {}
    if n_grid:
        kw["dimension_semantics"] = ("arbitrary",) * n_grid
    if vmem is not None:
        kw["vmem_limit_bytes"] = vmem
    return pltpu.CompilerParams(**kw)


def _colsum8(v):
    n, d = v.shape
    return v.reshape(n // SUBLANES, SUBLANES, d).sum(axis=0)


def _rms(v):
    return lax.rsqrt(jnp.mean(v * v, axis=-1, keepdims=True) + NORM_EPS)


def _sigmoid(v):
    return 0.5 * jnp.tanh(0.5 * v) + 0.5


def _shift_down(ext, k, rows):
    if k == 0:
        return ext[HIST:HIST + rows]
    return pltpu.roll(ext, k, 0)[HIST:HIST + rows]


def _shift_up(ext, k, rows):
    if k == 0:
        return ext[0:rows]
    return pltpu.roll(ext, ext.shape[0] - k, 0)[0:rows]


def _load_ext(ref, r0, h0, first, rows):
    hist = ref[pl.ds(h0, HIST), :].astype(F32)
    hist = jnp.where(first, 0.0, hist)
    cur = ref[pl.ds(r0, rows), :].astype(F32)
    return jnp.concatenate([hist, cur], axis=0)


def _me():
    return lax.axis_index("x"), lax.axis_index("y"), lax.axis_index("c")


def cast_weights(pos, w_in, w_out, l, after):
    _, D, CW = w_in.shape
    RO = w_out.shape[1]

    def body(pos_ref, wi, wo, after_ref, oi, oo):
        oi[...] = wi[...].astype(BF16)
        oo[...] = wo[...].astype(BF16)

    return pl.pallas_call(
        body, name="cast_w",
        grid_spec=pltpu.PrefetchScalarGridSpec(
            num_scalar_prefetch=1, grid=(2,),
            in_specs=[pl.BlockSpec((None, D // 2, CW), lambda h, p: (l, h, 0)),
                      pl.BlockSpec((None, RO // 2, D), lambda h, p: (l, h, 0)), ANY],
            out_specs=[pl.BlockSpec((D // 2, CW), lambda h, p: (h, p[1])),
                       pl.BlockSpec((None, RO // 2, D), lambda h, p: (p[1], h, 0))]),
        out_shape=[jax.ShapeDtypeStruct((D, N_CHIPS * CW), BF16), jax.ShapeDtypeStruct((N_CHIPS, RO, D), BF16)],
        compiler_params=_params(),
    )(pos, w_in, w_out, after)


def mod_part(pos, c_all, w_ada, b_my, after):
    L, D, CW = w_ada.shape

    def body(pos_ref, c_ref, w_ref, b_ref, after_ref, o_ref):
        cv = c_ref[...]
        ca = (cv * jax.nn.sigmoid(cv)).astype(BF16)
        o_ref[...] = jnp.dot(ca, w_ref[0].astype(BF16), preferred_element_type=F32) + b_ref[0]

    return pl.pallas_call(
        body, name="mod_part",
        grid_spec=pltpu.PrefetchScalarGridSpec(
            num_scalar_prefetch=1, grid=(L,),
            in_specs=[pl.BlockSpec((N_DEV, D), lambda l, p: (0, 0)),
                      pl.BlockSpec((1, D, CW), lambda l, p: (l, 0, 0)),
                      pl.BlockSpec((1, 1, CW), lambda l, p: (l, 0, 0)), ANY],
            out_specs=pl.BlockSpec((None, None, N_DEV, CW), lambda l, p: (p[1], l, 0, 0))),
        out_shape=jax.ShapeDtypeStruct((N_CHIPS, L, N_DEV, CW), F32),
        compiler_params=_params(VMEM_BIG),
    )(pos, c_all, w_ada, b_my.reshape(L, 1, CW), after)


def _mod_row(l, k, D):
    return pl.BlockSpec((None, None, 1, D), lambda *_: (l, k, 0, 0))


def _layer_row(l, D):
    return pl.BlockSpec((None, 1, D), lambda *_: (l, 0, 0))


def proj_fwd(x, mod4, g_pre3, wg, l):
    T, D = x.shape
    NC = wg.shape[1]
    NB = N_CHIPS
    CW = NC // NB
    tm = 512

    def body(x_ref, sh_ref, sc_ref, g_ref, w_ref, o_ref):
        xv = x_ref[...]
        h = (xv * _rms(xv)) * (g_ref[...] * (1.0 + sc_ref[...])) + sh_ref[...]
        hb = h.astype(BF16)
        for j in range(NB):
            cols = slice(j * CW, (j + 1) * CW)
            o_ref[:, cols] = jnp.dot(hb, w_ref[:, cols], preferred_element_type=F32).astype(BF16)

    return pl.pallas_call(
        body, name="proj_fwd", grid=(T // tm,),
        in_specs=[pl.BlockSpec((tm, D), lambda i: (i, 0)), _mod_row(l, 0, D), _mod_row(l, 1, D), _layer_row(l, D),
                  pl.BlockSpec((D, NC), lambda i: (0, 0))],
        out_specs=pl.BlockSpec((tm, NC), lambda i: (i, 0)),
        out_shape=jax.ShapeDtypeStruct((T, NC), BF16),
        compiler_params=_params(VMEM_BIG),
    )(x, mod4, mod4, g_pre3, wg)


N_MIX = 4


def _conv_fwd_block(u_ref, b_ref, c_ref, g_ref, w_ref, o_ref):
    T = u_ref.shape[0]
    R = 2 * R_CONV
    w0 = w_ref[pl.ds(0, 1), :]
    w1 = w_ref[pl.ds(1, 1), :]
    w2 = w_ref[pl.ds(2, 1), :]

    def chunk(i, carry):
        r0 = pl.multiple_of(i * R, R)
        h0 = pl.multiple_of(jnp.maximum(r0 - HIST, 0), HIST)
        first = i == 0
        ca = _load_ext(c_ref, r0, h0, first, R) * _load_ext(u_ref, r0, h0, first, R)
        conv = w2 * ca[HIST:] + w1 * _shift_down(ca, 1, R) + w0 * _shift_down(ca, 2, R)
        g = g_ref[pl.ds(r0, R), :].astype(F32)
        b = b_ref[pl.ds(r0, R), :].astype(F32)
        o_ref[pl.ds(r0, R), :] = (b * conv * (g * _sigmoid(g))).astype(BF16)
        return carry

    lax.fori_loop(0, T // R, chunk, 0)


def _conv_idx(j):
    return jnp.minimum(j, N_MIX - 1)


def _pool_idx(j):
    return jnp.maximum(j - N_MIX, 0)


def _proj_col(T, off, idx):
    return pl.BlockSpec((T, LANES), lambda j: (0, idx(j) + off))


def _causal_window_sum(ext, w):
    s, k = ext, 1
    while k < w:
        s = s + pltpu.roll(s, k, 0)
        k *= 2
    return s


def _anticausal_window_sum(ext, w):
    s, k = ext, 1
    n = ext.shape[0]
    while k < w:
        s = s + pltpu.roll(s, n - k, 0)
        k *= 2
    return s


def _count(r0, rows, w):
    t = r0 + lax.broadcasted_iota(jnp.int32, (rows, LANES), 0)
    return jnp.minimum(t + 1, w).astype(F32)


def _pooled_loop(p_ref, pooled_s, w, T):
    R = R_POOL

    def chunk(i, carry):
        r0 = pl.multiple_of(i * R, R)
        h0 = pl.multiple_of(jnp.maximum(r0 - HIST, 0), HIST)
        ext = _load_ext(p_ref, r0, h0, i == 0, R)
        ws = _causal_window_sum(ext, w)[HIST:]
        pooled_s[pl.ds(r0, R), :] = (ws / _count(r0, R, w) - ext[HIST:]).astype(BF16)
        return carry

    lax.fori_loop(0, T // R, chunk, 0)


def _conv_w_spec(l):
    return pl.BlockSpec((None, None, 3, LANES), lambda j: (_conv_idx(j), l, 0, 0))


def _pool_w_spec(l):
    return pl.BlockSpec((None, None, LANES, LANES), lambda j: (l, _pool_idx(j), 0, 0))


def _pool_s_spec(l):
    return pl.BlockSpec((None, 1, LANES), lambda j: (l, 0, _pool_idx(j)))


def _pool_fwd_group(p_ref, g_ref, w_ref, s_ref, o_ref, pooled_s, mixed_s, w):
    T = p_ref.shape[0]
    R = R_POOL
    _pooled_loop(p_ref, pooled_s, w, T)
    mixed_s[...] = jnp.dot(pooled_s[...], w_ref[...].astype(BF16), preferred_element_type=F32)
    sc = s_ref[...]

    def chunk(i, carry):
        r0 = pl.multiple_of(i * R, R)
        g = g_ref[pl.ds(r0, R), :].astype(F32)
        o_ref[pl.ds(r0, R), :] = (mixed_s[pl.ds(r0, R), :] * sc * (g * _sigmoid(g))).astype(BF16)
        return carry

    lax.fori_loop(0, T // R, chunk, 0)


def mix_fwd(proj, wconv, wpool, pscale3, l):
    T = proj.shape[0]

    def body(u_ref, b_ref, c_ref, g_ref, p_ref, gp_ref, wc_ref, wp_ref, s_ref, ya_ref, yp_ref, pooled_ref, mixed_s):
        j = pl.program_id(0)
        pl.when(j < N_MIX)(functools.partial(_conv_fwd_block, u_ref, b_ref, c_ref, g_ref, wc_ref, ya_ref))
        for k, w in enumerate(POOL_WINDOWS):
            pl.when(j == N_MIX + k)(functools.partial(_pool_fwd_group, p_ref, gp_ref, wp_ref, s_ref, yp_ref,
                                                      pooled_ref, mixed_s, w))

    half = jax.ShapeDtypeStruct((T, N_MIX * LANES), BF16)
    pool_col = pl.BlockSpec((T, LANES), lambda j: (0, _pool_idx(j)))
    return pl.pallas_call(
        body, name="mix_fwd", grid=(2 * N_MIX,),
        in_specs=[_proj_col(T, 0, _conv_idx), _proj_col(T, 4, _conv_idx), _proj_col(T, 8, _conv_idx),
                  _proj_col(T, 12, _conv_idx), _proj_col(T, 16, _pool_idx), _proj_col(T, 20, _pool_idx),
                  _conv_w_spec(l), _pool_w_spec(l), _pool_s_spec(l)],
        out_specs=[pl.BlockSpec((T, LANES), lambda j: (0, _conv_idx(j))), pool_col, pool_col],
        out_shape=[half, half, half],
        scratch_shapes=[pltpu.VMEM((T, LANES), F32)],
        compiler_params=_params(),
    )(proj, proj, proj, proj, proj, proj, wconv, wpool, pscale3)


def out_fwd(ya, yp, wo, x, mod4, g_post3, l, after):
    T, D = x.shape
    H = ya.shape[1]
    tm = 512

    def body(ya_ref, yp_ref, wo_ref, x_ref, gt_ref, g_ref, after_ref, xn_ref, y_ref):
        y = (jnp.dot(ya_ref[...], wo_ref[0:H, :], preferred_element_type=F32)
             + jnp.dot(yp_ref[...], wo_ref[H:2 * H, :], preferred_element_type=F32))
        xn_ref[...] = x_ref[...] + gt_ref[...] * (y * _rms(y) * g_ref[...])
        y_ref[...] = y.astype(BF16)

    tile = pl.BlockSpec((tm, D), lambda i: (i, 0))
    half = pl.BlockSpec((tm, H), lambda i: (i, 0))
    return pl.pallas_call(
        body, name="out_fwd", grid=(T // tm,),
        in_specs=[half, half, pl.BlockSpec((2 * H, D), lambda i: (0, 0)), tile, _mod_row(l, 2, D), _layer_row(l, D),
                  ANY],
        out_specs=[tile, tile],
        out_shape=[jax.ShapeDtypeStruct((T, D), F32), jax.ShapeDtypeStruct((T, D), BF16)],
        compiler_params=_params(VMEM_BIG),
    )(ya, yp, wo, x, mod4, g_post3, after)


def out_fwd_loss(ya, yp, wo, x, mod4, g_post3, l, target):
    T, D = x.shape
    H = ya.shape[1]
    tm = 512
    nt = T // tm

    def body(ya_ref, yp_ref, wo_ref, x_ref, gt_ref, g_ref, t_ref, dx_ref, y_ref, l_ref, acc):
        i = pl.program_id(0)

        @pl.when(i == 0)
        def _():
            acc[...] = jnp.zeros_like(acc)

        y = (jnp.dot(ya_ref[...], wo_ref[0:H, :], preferred_element_type=F32)
             + jnp.dot(yp_ref[...], wo_ref[H:2 * H, :], preferred_element_type=F32))
        y_ref[...] = y.astype(BF16)
        d = (x_ref[...] + gt_ref[...] * (y * _rms(y) * g_ref[...])) - t_ref[...]
        dx_ref[...] = d * (1.0 / D)
        acc[...] += _colsum8(d * d)

        @pl.when(i == nt - 1)
        def _():
            l_ref[...] = jnp.zeros_like(l_ref) + jnp.sum(acc[...]) * (0.5 / D)

    tile = pl.BlockSpec((tm, D), lambda i: (i, 0))
    half = pl.BlockSpec((tm, H), lambda i: (i, 0))
    return pl.pallas_call(
        body, name="out_fwd_loss", grid=(nt,),
        in_specs=[half, half, pl.BlockSpec((2 * H, D), lambda i: (0, 0)), tile, _mod_row(l, 2, D), _layer_row(l, D),
                  tile],
        out_specs=[tile, tile, pl.BlockSpec((SUBLANES, LANES), lambda i: (0, 0))],
        out_shape=[jax.ShapeDtypeStruct((T, D), F32), jax.ShapeDtypeStruct((T, D), BF16),
                   jax.ShapeDtypeStruct((SUBLANES, LANES), F32)],
        scratch_shapes=[pltpu.VMEM((SUBLANES, D), F32)],
        compiler_params=_params(VMEM_BIG),
    )(ya, yp, wo, x, mod4, g_post3, target)


def out_bwd(dx, y, ya, yp, wo, mod4, g_post3, l, after):
    T, D = dx.shape
    H = ya.shape[1]
    tm = 512
    nt = T // tm

    def body(dx_ref, y_ref, ya_ref, yp_ref, wo_ref, gt_ref, g_ref, after_ref,
             dya_ref, dyp_ref, dwo_ref, dgt_ref, dg_ref, acc_w, acc_p):
        i = pl.program_id(0)

        @pl.when(i == 0)
        def _():
            acc_w[...] = jnp.zeros_like(acc_w)
            acc_p[...] = jnp.zeros_like(acc_p)

        yv = y_ref[...].astype(F32)
        dxv = dx_ref[...]
        gg = gt_ref[...] * g_ref[...]
        r = _rms(yv)
        yn = yv * r
        p = dxv * yn
        acc_p[...] += _colsum8(p)
        dy = r * (dxv * gg - yn * jnp.mean(p * gg, axis=-1, keepdims=True))
        dyb = dy.astype(BF16)
        dyc = lax.dot_general(dyb, wo_ref[...], NT, preferred_element_type=F32)
        dya_ref[...] = dyc[:, 0:H].astype(BF16)
        dyp_ref[...] = dyc[:, H:2 * H].astype(BF16)
        acc_w[0:H, :] += lax.dot_general(ya_ref[...], dyb, TN, preferred_element_type=F32)
        acc_w[H:2 * H, :] += lax.dot_general(yp_ref[...], dyb, TN, preferred_element_type=F32)

        @pl.when(i == nt - 1)
        def _():
            dwo_ref[...] = acc_w[...].astype(BF16)
            sp = jnp.sum(acc_p[...], axis=0, keepdims=True)
            dgt_ref[...] = g_ref[...] * sp
            dg_ref[...] = gt_ref[...] * sp

    row = pl.BlockSpec((1, D), lambda i: (0, 0))
    tile = pl.BlockSpec((tm, D), lambda i: (i, 0))
    half = pl.BlockSpec((tm, H), lambda i: (i, 0))
    full = pl.BlockSpec((2 * H, D), lambda i: (0, 0))
    return pl.pallas_call(
        body, name="out_bwd", grid=(nt,),
        in_specs=[tile, tile, half, half, full, _mod_row(l, 2, D), _layer_row(l, D), ANY],
        out_specs=[half, half, full, row, row],
        out_shape=[jax.ShapeDtypeStruct((T, H), BF16), jax.ShapeDtypeStruct((T, H), BF16),
                   jax.ShapeDtypeStruct((2 * H, D), BF16),
                   jax.ShapeDtypeStruct((1, D), F32), jax.ShapeDtypeStruct((1, D), F32)],
        scratch_shapes=[pltpu.VMEM((2 * H, D), F32), pltpu.VMEM((SUBLANES, D), F32)],
        compiler_params=_params(VMEM_BIG),
    )(dx, y, ya, yp, wo, mod4, g_post3, after)


def _conv_bwd_block(u_ref, b_ref, c_ref, g_ref, dy_ref, w_ref, du_ref, db_ref, dc_ref, dg_ref, dw_ref):
    T = u_ref.shape[0]
    R = R_CONV
    nchunk = T // R
    w0 = w_ref[pl.ds(0, 1), :]
    w1 = w_ref[pl.ds(1, 1), :]
    w2 = w_ref[pl.ds(2, 1), :]

    def chunk(k, carry):
        head, a0, a1, a2 = carry
        i = nchunk - 1 - k
        r0 = pl.multiple_of(i * R, R)
        h0 = pl.multiple_of(jnp.maximum(r0 - HIST, 0), HIST)
        first = i == 0
        ue = _load_ext(u_ref, r0, h0, first, R)
        ce = _load_ext(c_ref, r0, h0, first, R)
        ca = ce * ue
        ca0 = ca[HIST:]
        ca1 = _shift_down(ca, 1, R)
        ca2 = _shift_down(ca, 2, R)
        conv = w2 * ca0 + w1 * ca1 + w0 * ca2
        g = g_ref[pl.ds(r0, R), :].astype(F32)
        b = b_ref[pl.ds(r0, R), :].astype(F32)
        dy = dy_ref[pl.ds(r0, R), :].astype(F32)
        sg = _sigmoid(g)
        sl = g * sg
        t = dy * conv
        db_ref[pl.ds(r0, R), :] = (t * sl).astype(BF16)
        dg_ref[pl.ds(r0, R), :] = (t * b * (sg + sl * (1.0 - sg))).astype(BF16)
        dconv = dy * b * sl
        a2 = a2 + _colsum8(dconv * ca0)
        a1 = a1 + _colsum8(dconv * ca1)
        a0 = a0 + _colsum8(dconv * ca2)
        e = jnp.concatenate([dconv, head], axis=0)
        dca = w2 * dconv + w1 * _shift_up(e, 1, R) + w0 * _shift_up(e, 2, R)
        du_ref[pl.ds(r0, R), :] = (dca * ce[HIST:]).astype(BF16)
        dc_ref[pl.ds(r0, R), :] = (dca * ue[HIST:]).astype(BF16)
        return dconv[0:SUBLANES], a0, a1, a2

    z = jnp.zeros((SUBLANES, LANES), F32)
    _, a0, a1, a2 = lax.fori_loop(0, nchunk, chunk, (z, z, z, z))
    dw_ref[pl.ds(0, 1), :] = jnp.sum(a0, axis=0, keepdims=True)
    dw_ref[pl.ds(1, 1), :] = jnp.sum(a1, axis=0, keepdims=True)
    dw_ref[pl.ds(2, 1), :] = jnp.sum(a2, axis=0, keepdims=True)


def _pool_bwd_group(pooled_s, g_ref, dy_ref, w_ref, s_ref, du_ref, dg_ref, dw_ref, ds_ref,
                    mixed_s, dmix_s, dpool_s, w):
    T = pooled_s.shape[0]
    R = R_POOL
    nchunk = T // R
    wb = w_ref[...].astype(BF16)
    mixed_s[...] = jnp.dot(pooled_s[...], wb, preferred_element_type=F32)
    sc = s_ref[...]

    def gate_chunk(i, acc):
        r0 = pl.multiple_of(i * R, R)
        g = g_ref[pl.ds(r0, R), :].astype(F32)
        dy = dy_ref[pl.ds(r0, R), :].astype(F32)
        mixed = mixed_s[pl.ds(r0, R), :]
        sg = _sigmoid(g)
        sl = g * sg
        dg_ref[pl.ds(r0, R), :] = (dy * mixed * sc * (sg + sl * (1.0 - sg))).astype(BF16)
        dms = dy * sl
        dmix_s[pl.ds(r0, R), :] = (dms * sc).astype(BF16)
        return acc + _colsum8(dms * mixed)

    acc = lax.fori_loop(0, nchunk, gate_chunk, jnp.zeros((SUBLANES, LANES), F32))
    ds_ref[...] = jnp.sum(acc, axis=0, keepdims=True)
    dpool_s[pl.ds(0, T), :] = lax.dot_general(dmix_s[...], wb, NT, preferred_element_type=F32)
    dpool_s[pl.ds(T, HIST), :] = jnp.zeros((HIST, LANES), F32)
    dw_ref[...] = lax.dot_general(pooled_s[...], dmix_s[...], TN, preferred_element_type=F32).astype(BF16)

    def back_chunk(i, carry):
        r0 = pl.multiple_of(i * R, R)
        dpe = dpool_s[pl.ds(r0, R + HIST), :]
        e = dpe / _count(r0, R + HIST, w)
        du_ref[pl.ds(r0, R), :] = (_anticausal_window_sum(e, w)[0:R] - dpe[0:R]).astype(BF16)
        return carry

    lax.fori_loop(0, nchunk, back_chunk, 0)


def mix_bwd(proj, pooled, dya, dyp, wconv, wpool, pscale3, l, after):
    T = proj.shape[0]

    def body(u_ref, b_ref, c_ref, g_ref, pooled_ref, gp_ref, dya_ref, dyp_ref, wc_ref, wp_ref, s_ref, after_ref,
             dua_ref, dba_ref, dca_ref, dga_ref, dup_ref, dgp_ref, dwc_ref, dwp_ref, ds_ref,
             mixed_s, dmix_s, dpool_s):
        j = pl.program_id(0)
        pl.when(j < N_MIX)(functools.partial(_conv_bwd_block, u_ref, b_ref, c_ref, g_ref, dya_ref, wc_ref,
                                             dua_ref, dba_ref, dca_ref, dga_ref, dwc_ref))
        for k, w in enumerate(POOL_WINDOWS):
            pl.when(j == N_MIX + k)(functools.partial(_pool_bwd_group, pooled_ref, gp_ref, dyp_ref, wp_ref, s_ref,
                                                      dup_ref, dgp_ref, dwp_ref, ds_ref,
                                                      mixed_s, dmix_s, dpool_s, w))

    sec = jax.ShapeDtypeStruct((T, N_MIX * LANES), BF16)
    conv_col = pl.BlockSpec((T, LANES), lambda j: (0, _conv_idx(j)))
    pool_col = pl.BlockSpec((T, LANES), lambda j: (0, _pool_idx(j)))
    return pl.pallas_call(
        body, name="mix_bwd", grid=(2 * N_MIX,),
        in_specs=[_proj_col(T, 0, _conv_idx), _proj_col(T, 4, _conv_idx), _proj_col(T, 8, _conv_idx),
                  _proj_col(T, 12, _conv_idx), pool_col, _proj_col(T, 20, _pool_idx),
                  conv_col, pool_col, _conv_w_spec(l), _pool_w_spec(l), _pool_s_spec(l), ANY],
        out_specs=[conv_col, conv_col, conv_col, conv_col, pool_col, pool_col,
                   pl.BlockSpec((None, 3, LANES), lambda j: (_conv_idx(j), 0, 0)),
                   pl.BlockSpec((None, LANES, LANES), lambda j: (_pool_idx(j), 0, 0)),
                   pl.BlockSpec((1, LANES), lambda j: (0, _pool_idx(j)))],
        out_shape=[sec] * 6 + [jax.ShapeDtypeStruct((N_MIX, 3, LANES), F32),
                               jax.ShapeDtypeStruct((N_MIX, LANES, LANES), BF16),
                               jax.ShapeDtypeStruct((1, N_MIX * LANES), F32)],
        scratch_shapes=[pltpu.VMEM((T, LANES), F32), pltpu.VMEM((T, LANES), BF16), pltpu.VMEM((T + HIST, LANES), F32)],
        compiler_params=_params(),
    )(proj, proj, proj, proj, pooled, proj, dya, dyp, wconv, wpool, pscale3, after)


def in_bwd(dsecs, wg, x, dxo, mod4, g_pre3, l, update=None):
    T, D = x.shape
    NB = N_CHIPS
    CW = wg.shape[1] // NB
    SW = dsecs[0].shape[1]
    nsec = len(dsecs)
    PW = 256
    assert SW % PW == 0 and CW % PW == 0
    tm = 256
    nt = T // tm
    n_in = nsec + 6
    n_upd = 0 if update is None else 8
    n_acc = 0 if update is None or update[3] is None else 8

    def body(*refs):
        d_refs = refs[0:nsec]
        w_ref, x_ref, dxo_ref, sh_ref, sc_ref, g_ref = refs[nsec:n_in]
        outs = refs[n_in + n_upd + n_acc:]
        dxi_ref, dw_ref, dsh_ref, dsc_ref, dg_ref = outs[0:5]
        acc_w, acc_sh, acc_q = outs[5 + n_upd:]
        i = pl.program_id(0)
        for k in range(0, n_upd, 4):
            _adamw_block(refs[n_in + k:n_in + k + 4], outs[5 + k:5 + k + 4])

        @pl.when(i == 0)
        def _():
            acc_w[...] = jnp.zeros_like(acc_w)
            acc_sh[...] = jnp.zeros_like(acc_sh)
            acc_q[...] = jnp.zeros_like(acc_q)

        xv = x_ref[...]
        r = _rms(xv)
        xh = xv * r
        sg = g_ref[...] * (1.0 + sc_ref[...])
        hb = (xh * sg + sh_ref[...]).astype(BF16)
        dh = lax.dot_general(d_refs[0][...], w_ref[:, 0:SW], NT, preferred_element_type=F32)
        for s in range(1, nsec):
            dh = dh + lax.dot_general(d_refs[s][...], w_ref[:, s * SW:(s + 1) * SW], NT, preferred_element_type=F32)
        for p in range(nsec * SW // PW):
            col = p * PW
            s, so = col // SW, col % SW
            j, jo = col // CW, col % CW
            acc_w[j, :, jo:jo + PW] += lax.dot_general(hb, d_refs[s][:, so:so + PW], TN, preferred_element_type=F32)
        q = dh * xh
        acc_sh[...] += _colsum8(dh)
        acc_q[...] += _colsum8(q)
        dxi_ref[...] = dxo_ref[...] + r * (dh * sg - xh * jnp.mean(q * sg, axis=-1, keepdims=True))

        @pl.when(i == nt - 1)
        def _():
            dw_ref[...] = acc_w[...].astype(BF16)
            sq = jnp.sum(acc_q[...], axis=0, keepdims=True)
            dsh_ref[...] = jnp.sum(acc_sh[...], axis=0, keepdims=True)
            dsc_ref[...] = g_ref[...] * sq
            dg_ref[...] = (1.0 + sc_ref[...]) * sq

    row = pl.BlockSpec((1, D), lambda i: (0, 0))
    tile = pl.BlockSpec((tm, D), lambda i: (i, 0))
    sect = pl.BlockSpec((tm, SW), lambda i: (i, 0))
    rowshape = jax.ShapeDtypeStruct((1, D), F32)
    in_specs = [sect] * nsec + [pl.BlockSpec((D, NB * CW), lambda i: (0, 0)), tile, tile,
                                _mod_row(l, 0, D), _mod_row(l, 1, D), _layer_row(l, D)]
    out_specs = [tile, pl.BlockSpec((NB, D, CW), lambda i: (0, 0, 0)), row, row, row]
    out_shape = [jax.ShapeDtypeStruct((T, D), F32), jax.ShapeDtypeStruct((NB, D, CW), BF16), rowshape, rowshape, rowshape]
    args = [*dsecs, wg, x, dxo, mod4, mod4, g_pre3]
    aliases = {}
    if update is not None:
        layer, of_w_in, of_w_out, acc = update
        for group in (of_w_in, of_w_out):
            _, rows, cols = group[0].shape
            spec = pl.BlockSpec((None, rows // nt, cols), lambda i: (layer, i, 0))
            in_specs += [spec] * 4
            out_specs += [spec] * 4
            out_shape += [jax.ShapeDtypeStruct(group[0].shape, F32)] * 4
            args += list(group)
        if acc is not None:
            aliases = {len(args) + a: 5 + a for a in range(n_acc)}
            in_specs += [ANY] * n_acc
            args += list(acc)
    return pl.pallas_call(
        body, name="in_bwd", grid=(nt,),
        in_specs=in_specs, out_specs=out_specs, out_shape=out_shape, input_output_aliases=aliases,
        scratch_shapes=[pltpu.VMEM((NB, D, CW), F32),
                        pltpu.VMEM((SUBLANES, D), F32), pltpu.VMEM((SUBLANES, D), F32)],
        compiler_params=_params(VMEM_BIG),
    )(*args)


def _rcopy(src, dst, ssem, rsem, dev):
    return pltpu.make_async_remote_copy(src_ref=src, dst_ref=dst, send_sem=ssem, recv_sem=rsem,
                                        device_id=dev, device_id_type=MESH)


def _peers7(x, y, c):
    out = []
    for m in range(1, N_DEV):
        bx, by, bc = (m >> 2) & 1, (m >> 1) & 1, m & 1
        out.append(((1 - x) if bx else x, (1 - y) if by else y, (1 - c) if bc else c))
    return out


HBM = pl.BlockSpec(memory_space=pltpu.HBM)
SEM = pl.BlockSpec(memory_space=pltpu.SEMAPHORE)
SPLIT = pltpu.CompilerParams(has_side_effects=pltpu.SideEffectType.DATAFLOW_SIDE_EFFECTING)


def _hbm(a):
    return pltpu.with_memory_space_constraint(a, pltpu.HBM)


def _chips(x, y):
    return [(1 - x, y), (x, 1 - y), (1 - x, 1 - y)]


SIBLING_BARRIER_ID = 0


def xchg_start(name, bufs, n_copies, plan, sibling_only=False, after=()):
    n = len(bufs)
    after = list(after)

    def body(*refs):
        ssem, rsem, token = refs[n + len(after)], refs[n + len(after) + 1], refs[-1]
        x, y, c = _me()
        if sibling_only:
            barrier = pltpu.get_barrier_semaphore()
            pl.semaphore_signal(barrier, inc=1, device_id=(x, y, 1 - c), device_id_type=MESH)
            pl.semaphore_wait(barrier, 1)
        copies = plan(refs[0:n], x, y, c)
        assert len(copies) == n_copies
        for k, (src, dst, peer, _) in enumerate(copies):
            _rcopy(src, dst, ssem.at[k], rsem.at[k], peer).start()
        token[...] = jnp.zeros_like(token)

    params = dict(has_side_effects=pltpu.SideEffectType.DATAFLOW_SIDE_EFFECTING)
    if sibling_only:
        params["collective_id"] = SIBLING_BARRIER_ID
    outs = pl.pallas_call(
        body, name=name,
        in_specs=[HBM] * n + [ANY] * len(after),
        out_specs=[SEM, SEM] + [HBM] * n + [pl.BlockSpec(memory_space=pltpu.VMEM)],
        out_shape=([pltpu.SemaphoreType.DMA((n_copies,))] * 2 + [pltpu.HBM(b.shape, b.dtype) for b in bufs]
                   + [jax.ShapeDtypeStruct((SUBLANES, LANES), F32)]),
        input_output_aliases={a: 2 + a for a in range(n)},
        compiler_params=pltpu.CompilerParams(**params),
    )(*[_hbm(b) for b in bufs], *after)
    return outs[0], outs[1], list(outs[2:2 + n]), outs[-1]


def xchg_wait(name, bufs, ssem, rsem, n_copies, plan, after, sems=None):
    n = len(bufs)
    after = list(after)
    sems = tuple(range(n_copies)) if sems is None else tuple(sems)
    assert len(sems) == n_copies

    def body(*refs):
        ssem_ref, rsem_ref = refs[n], refs[n + 1]
        copies = plan(refs[0:n], *_me())
        assert len(copies) == n_copies
        for k, (src, _, peer, land) in zip(sems, copies):
            cp = _rcopy(src, land, ssem_ref.at[k], rsem_ref.at[k], peer)
            cp.wait_send()
            cp.wait_recv()

    outs = pl.pallas_call(
        body, name=name,
        in_specs=[HBM] * n + [SEM, SEM] + [ANY] * len(after), out_specs=[HBM] * n,
        out_shape=[pltpu.HBM(b.shape, b.dtype) for b in bufs],
        input_output_aliases={a: a for a in range(n)},
        compiler_params=SPLIT,
    )(*bufs, ssem, rsem, *after)
    return list(outs)


def _shard_half(buf, chip, half):
    if len(buf.shape) == 2:
        h, w = buf.shape[0] // 2, buf.shape[1] // N_CHIPS
        return buf.at[pl.ds(half * h, h), pl.ds(chip * w, w)]
    h = buf.shape[1] // 2
    return buf.at[chip, pl.ds(half * h, h)]


def plan_gather(refs, x, y, c):
    out = []
    for buf in refs:
        own = _shard_half(buf, 2 * x + y, c)
        for (px, py) in _chips(x, y):
            out.append((own, own, (px, py, c), _shard_half(buf, 2 * px + py, c)))
    return out


def plan_forward(refs, x, y, c):
    out = []
    for (px, py) in _chips(x, y):
        for buf in refs:
            landed = _shard_half(buf, 2 * px + py, c)
            out.append((landed, landed, (x, y, 1 - c), _shard_half(buf, 2 * px + py, 1 - c)))
    return out


def plan_sibling(refs, x, y, c):
    n = len(refs) // 2
    out = []
    for a in range(n):
        h = refs[a].shape[1] // 2
        out.append((refs[a].at[:, pl.ds((1 - c) * h, h)], refs[n + a], (x, y, 1 - c), refs[n + a]))
    return out


def plan_chip(refs, x, y, c):
    n = len(refs) // 2
    out = []
    for j, (px, py) in enumerate(_chips(x, y)):
        for a in range(n):
            out.append((refs[a].at[2 * px + py], refs[n + a].at[j], (px, py, c), refs[n + a].at[j]))
    return out


def plan_mod(refs, x, y, c):
    (mods,) = refs
    mine = mods.at[2 * x + y]
    return [(mine, mine, (px, py, c), mods.at[2 * px + py]) for (px, py) in _chips(x, y)]


def plan_pack(refs, x, y, c):
    (packs,) = refs
    mine = packs.at[4 * x + 2 * y + c]
    return [(mine, mine, peer, packs.at[4 * peer[0] + 2 * peer[1] + peer[2]]) for peer in _peers7(x, y, c)]


def plan_spread(layers, wp_layers):
    def plan(refs, x, y, c):
        gi, go, gp = refs
        hD, hR, hP = gi.shape[1] // 2, go.shape[1] // 2, gp.shape[2] // 2
        sib = (x, y, 1 - c)
        out = []
        for l in layers:
            mine = gi.at[l, pl.ds(c * hD, hD)]
            out.append((mine, mine, sib, gi.at[l, pl.ds((1 - c) * hD, hD)]))
            mine = go.at[l, pl.ds(c * hR, hR)]
            out.append((mine, mine, sib, go.at[l, pl.ds((1 - c) * hR, hR)]))
        for l in wp_layers:
            mine = gp.at[l, 2 * x + y, pl.ds(c * hP, hP)]
            for peer in _peers7(x, y, c):
                out.append((mine, mine, peer, gp.at[l, 2 * peer[0] + peer[1], pl.ds(peer[2] * hP, hP)]))
        return out

    return plan


def place_small(pos, c8, wc):
    L = wc.shape[0]

    def body(pos_ref, c_ref, wc_ref, call_ref, wcall_ref):
        call_ref[...] = c_ref[...]
        wcall_ref[...] = wc_ref[...]

    return pl.pallas_call(
        body, name="place_small",
        grid_spec=pltpu.PrefetchScalarGridSpec(
            num_scalar_prefetch=1, grid=(1,),
            in_specs=[pl.BlockSpec((SUBLANES, LANES), lambda i, p: (0, 0)),
                      pl.BlockSpec((L, 3, LANES), lambda i, p: (0, 0, 0))],
            out_specs=[pl.BlockSpec((None, SUBLANES, LANES), lambda i, p: (p[2], 0, 0)),
                       pl.BlockSpec((None, L, 3, LANES), lambda i, p: (p[1], 0, 0, 0))]),
        out_shape=[jax.ShapeDtypeStruct((N_DEV, SUBLANES, LANES), F32),
                   jax.ShapeDtypeStruct((N_CHIPS, L, 3, LANES), F32)],
        compiler_params=_params(),
    )(pos, c8, wc)


def plan_small(refs, x, y, c):
    call, wcall = refs
    mine = call.at[4 * x + 2 * y + c]
    out = [(mine, mine, peer, call.at[4 * peer[0] + 2 * peer[1] + peer[2]]) for peer in _peers7(x, y, c)]
    mine = wcall.at[2 * x + y]
    out += [(mine, mine, (px, py, c), wcall.at[2 * px + py]) for (px, py) in _chips(x, y)]
    return out


def add_sibling(cidx, mine, sib):
    def body(c_ref, *refs):
        for a in range(3):
            m, s, o = refs[a], refs[3 + a], refs[6 + a]
            o[...] = (m[...].astype(F32) + s[...].astype(F32)).astype(BF16)

    per_step = 2

    def mine_spec(a):
        h = a.shape[1] // 2
        return pl.BlockSpec((per_step, h, a.shape[2]), lambda j, c_ref: (j, c_ref[0], 0))

    def sib_spec(a):
        return pl.BlockSpec((per_step,) + a.shape[1:], lambda j, c_ref: (j, 0, 0))

    return pl.pallas_call(
        body, name="add_sibling",
        grid_spec=pltpu.PrefetchScalarGridSpec(
            num_scalar_prefetch=1, grid=(N_CHIPS // per_step,),
            in_specs=[mine_spec(a) for a in mine] + [sib_spec(a) for a in sib],
            out_specs=[sib_spec(a) for a in sib]),
        out_shape=[jax.ShapeDtypeStruct(a.shape, BF16) for a in sib],
        compiler_params=_params(VMEM_BIG),
    )(cidx, *mine, *sib)


def sum_chips(pos, own, rb, acc, l, shapes):
    nq = 2
    n_in = 6 + (3 if acc is not None else 0)

    def body(pos_ref, *refs):
        for a in range(3):
            m, b, o = refs[a], refs[3 + a], refs[n_in + a]
            s = m[...].astype(F32)
            for j in range(3):
                s = s + b[j].astype(F32)
            o[...] = s

    def own_spec(a):
        return pl.BlockSpec((None, a.shape[1] // nq, a.shape[2]), lambda q, p: (p[1], q, 0))

    def rb_spec(a):
        return pl.BlockSpec((3, a.shape[1] // nq, a.shape[2]), lambda q, p: (0, q, 0))

    hi, ho, hp = own[0].shape[1] // nq, own[1].shape[1] // nq, own[2].shape[1] // nq
    out_specs = [pl.BlockSpec((None, hi, shapes[0][2]), lambda q, p: (l, p[0] * nq + q, 0)),
                 pl.BlockSpec((None, ho, shapes[1][2]), lambda q, p: (l, p[0] * nq + q, 0)),
                 pl.BlockSpec((None, None, hp, LANES), lambda q, p: (l, p[1], p[0] * nq + q, 0))]
    in_specs = [own_spec(a) for a in own] + [rb_spec(a) for a in rb]
    args = list(own) + list(rb)
    aliases = {}
    if acc is not None:
        in_specs += [ANY] * 3
        args += list(acc)
        aliases = {7: 0, 8: 1, 9: 2}
    return pl.pallas_call(
        body, name="sum_chips",
        grid_spec=pltpu.PrefetchScalarGridSpec(num_scalar_prefetch=1, grid=(nq,), in_specs=in_specs, out_specs=out_specs),
        out_shape=[jax.ShapeDtypeStruct(s, F32) for s in shapes],
        input_output_aliases=aliases,
        compiler_params=_params(VMEM_BIG),
    )(pos, *args)


def _wconv_slot(chip, tap):
    idx = 3 * chip + tap
    return ROW_WCONV + idx // SUBLANES, slice((idx % SUBLANES) * LANES, (idx % SUBLANES + 1) * LANES)


def pack_small(pos, per_layer, loss_blk):
    L = len(per_layer)
    D = per_layer[0][0].shape[1]

    def body(pos_ref, *refs):
        o = refs[-1]
        lb = refs[-2]
        o[...] = jnp.zeros_like(o)
        for l in range(L):
            dgpre, dgpost, dsh, dsc, dgt, dps, dwc = refs[7 * l:7 * l + 7]
            base = SUBLANES * l
            o[pl.ds(base + ROW_G_PRE, 1), :] = dgpre[...]
            o[pl.ds(base + ROW_G_POST, 1), :] = dgpost[...]
            for r, src in enumerate((dsh, dsc, dgt)):
                o[pl.ds(base + ROW_MOD + r, 1), :] = src[...]
            o[pl.ds(base + ROW_PSCALE, 1), 0:dps.shape[1]] = dps[...]
            for j in range(dwc.shape[0]):
                for k in range(3):
                    row, lanes = _wconv_slot(j, k)
                    o[pl.ds(base + row, 1), lanes] = dwc[j, pl.ds(k, 1), :]
        o[pl.ds(ROW_PSCALE, 1), LOSS_LANES] = lb[pl.ds(0, 1), :]

    flat = [a for layer in per_layer for a in layer] + [loss_blk]

    def whole(a):
        return pl.BlockSpec(a.shape, lambda i, p: (0,) * a.ndim)

    return pl.pallas_call(
        body, name="pack_small",
        grid_spec=pltpu.PrefetchScalarGridSpec(
            num_scalar_prefetch=1, grid=(1,), in_specs=[whole(a) for a in flat],
            out_specs=pl.BlockSpec((None, L * SUBLANES, D), lambda i, p: (p[2], 0, 0))),
        out_shape=jax.ShapeDtypeStruct((N_DEV, L * SUBLANES, D), F32),
        compiler_params=_params(),
    )(pos, *flat)


def small_update(pos, packs, params, moments_m, moments_v):
    n = len(params)
    L, D = params[1].shape
    PS = params[3].shape[1]

    def body(pos_ref, p_ref, *refs):
        ws, ms, vs = refs[0:n], refs[n:2 * n], refs[2 * n:3 * n]
        loss_ref = refs[3 * n]
        outs = [refs[3 * n + 1 + 4 * t:3 * n + 5 + 4 * t] for t in range(n)]
        summed = refs[-1]
        s = p_ref[0]
        for d in range(1, N_DEV):
            s = s + p_ref[d]
        summed[...] = s
        loss_ref[...] = summed[pl.ds(ROW_PSCALE, 1), LOSS_LANES]
        chip = pos_ref[1]

        def update(t, idx, g):
            d, mm, vv = _adamw_math(ws[t][idx], g, ms[t][idx], vs[t][idx])
            g_ref, d_ref, mo_ref, vo_ref = outs[t]
            g_ref[idx] = g
            d_ref[idx] = d
            mo_ref[idx] = mm
            vo_ref[idx] = vv

        for l in range(L):
            base = SUBLANES * l
            row = pl.ds(l, 1)
            for k in range(3):
                update(0, (row, slice(k * D, (k + 1) * D)), summed[pl.ds(base + ROW_MOD + k, 1), :])
            update(1, (row, slice(None)), summed[pl.ds(base + ROW_G_PRE, 1), :])
            update(2, (row, slice(None)), summed[pl.ds(base + ROW_G_POST, 1), :])
            update(3, (row, slice(None)), summed[pl.ds(base + ROW_PSCALE, 1), 0:PS])
            for k in range(3):
                g = None
                for j in range(N_CHIPS):
                    wrow, lanes = _wconv_slot(j, k)
                    cand = summed[pl.ds(base + wrow, 1), lanes]
                    g = cand if g is None else jnp.where(chip == j, cand, g)
                update(4, (l, pl.ds(k, 1), slice(None)), g)

    def whole(a):
        return pl.BlockSpec(a.shape, lambda i, p: (0,) * a.ndim)

    ins = [packs] + list(params) + list(moments_m) + list(moments_v)
    out_shape = [jax.ShapeDtypeStruct((1, LANES), F32)]
    for w in params:
        out_shape += [jax.ShapeDtypeStruct(w.shape, F32)] * 4
    outs = pl.pallas_call(
        body, name="small_update",
        grid_spec=pltpu.PrefetchScalarGridSpec(
            num_scalar_prefetch=1, grid=(1,), in_specs=[whole(a) for a in ins],
            out_specs=[whole(a) for a in out_shape],
            scratch_shapes=[pltpu.VMEM(packs.shape[1:], F32)]),
        out_shape=out_shape,
        compiler_params=_params(),
    )(pos, *ins)
    return outs[0], [outs[1 + 4 * t:5 + 4 * t] for t in range(n)]


def _adamw_math(w, g, m, v):
    m = ADAM_B1 * m + (1.0 - ADAM_B1) * g
    v = ADAM_B2 * v + (1.0 - ADAM_B2) * (g * g)
    m_hat = m / (1.0 - ADAM_B1 ** ADAM_STEP)
    v_hat = v / (1.0 - ADAM_B2 ** ADAM_STEP)
    delta = -ADAM_LR * (m_hat / (jnp.sqrt(v_hat) + ADAM_EPS) + ADAM_WD * w)
    return delta, m, v


def _adamw_block(ins, outs):
    w_ref, g_ref, m_ref, v_ref = ins
    go_ref, d_ref, mo_ref, vo_ref = outs
    gv = g_ref[...]
    d, mm, vv = _adamw_math(w_ref[...], gv, m_ref[...], v_ref[...])
    go_ref[...] = gv
    d_ref[...] = d
    mo_ref[...] = mm
    vo_ref[...] = vv


def adamw(groups, name, first, count, steps, acc=None):
    n = len(groups)

    def body(*refs):
        outs = refs[len(refs) - 4 * n:]
        for k in range(n):
            _adamw_block(refs[4 * k:4 * k + 4], outs[4 * k:4 * k + 4])

    specs, out_shape, args = [], [], []
    for group in groups:
        shape = group[0].shape
        spec = pl.BlockSpec((1, shape[1] // steps) + shape[2:],
                            lambda i, s, rest=(0,) * (len(shape) - 2): (first + i, s) + rest)
        specs += [spec] * 4
        out_shape += [jax.ShapeDtypeStruct(shape, F32)] * 4
        args += list(group)
    extra = [] if acc is None else list(acc)
    return pl.pallas_call(
        body, name=name, grid=(count, steps),
        in_specs=specs + [ANY] * len(extra), out_specs=specs, out_shape=out_shape,
        input_output_aliases={4 * n + a: a for a in range(len(extra))},
        compiler_params=_params(VMEM_BIG, n_grid=2),
    )(*args, *extra)


def ada_finish(c_all, dmod, w, m, v):
    L, D, CW = w.shape
    hD = D // 2

    def body(c_ref, d_ref, w_ref, m_ref, v_ref, g_ref, dl_ref, mo_ref, vo_ref):
        cv = c_ref[...]
        z = jnp.zeros_like(cv)
        ca = jnp.concatenate([cv * jax.nn.sigmoid(cv), z], axis=0).astype(BF16)
        dm = jnp.concatenate([d_ref[0], jnp.zeros_like(d_ref[0])], axis=0).astype(BF16)
        g = lax.dot_general(ca, dm, TN, preferred_element_type=F32)
        g_ref[0] = g
        d, mm, vv = _adamw_math(w_ref[0], g, m_ref[0], v_ref[0])
        dl_ref[0] = d
        mo_ref[0] = mm
        vo_ref[0] = vv

    big = pl.BlockSpec((1, hD, CW), lambda l, h: (l, h, 0))
    shape = jax.ShapeDtypeStruct(w.shape, F32)
    return pl.pallas_call(
        body, name="ada_finish", grid=(L, 2),
        in_specs=[pl.BlockSpec((N_DEV, hD), lambda l, h: (0, h)), pl.BlockSpec((1, N_DEV, CW), lambda l, h: (l, 0, 0)),
                  big, big, big],
        out_specs=[big] * 4, out_shape=[shape] * 4,
        compiler_params=_params(VMEM_BIG, n_grid=2),
    )(c_all, dmod, w, m, v)


def kernel(x, c, w_ada, b_ada, g_pre, w_in, w_conv, w_pool, pool_scale, w_out, g_post, loss_target, m_w_ada, m_b_ada, m_g_pre, m_w_in, m_w_conv, m_w_pool, m_pool_scale, m_w_out, m_g_post, v_w_ada, v_b_ada, v_g_pre, v_w_in, v_w_conv, v_w_pool, v_pool_scale, v_w_out, v_g_post):
    L, D, CW = w_in.shape
    RO = w_out.shape[1]
    T = x.shape[1]
    ix, iy, ic = _me()
    chip = 2 * ix + iy
    me_lin = 4 * ix + 2 * iy + ic

    pos = jnp.stack([ic, chip, me_lin]).astype(jnp.int32)
    n_s, n_c = 3, 9

    def gather(bufs, after):
        ss, rs, bufs, tok = xchg_start("gather_start", bufs, 3 * len(bufs), plan_gather, after=after)
        return (ss, rs, bufs), tok

    def ready(flight, after):
        fss, frs, bufs = flight
        return xchg_wait("forward_wait", bufs, fss, frs, 3 * len(bufs), plan_forward, after)

    def arrive_part(flight, which, after, base=0):
        ss, rs, bufs = flight
        sems = tuple(range(base + 3 * which, base + 3 * which + 3))
        (buf,) = xchg_wait("gather_wait", [bufs[which]], ss, rs, 3, plan_gather, after, sems=sems)
        fss, frs, (buf,), tok = xchg_start("forward_start", [buf], 3, plan_forward, sibling_only=True)
        return (fss, frs, [buf]), tok

    n_small = N_DEV - 1 + N_CHIPS - 1
    w_in_of, w_out_of = [None] * L, [None] * L
    gi0, go0 = cast_weights(pos, w_in, w_out, 0, pos)

    def plan_first(refs, x, y, c):
        return plan_small(refs[0:2], x, y, c) + plan_gather(refs[2:3], x, y, c)

    placed = list(place_small(pos, c.reshape(SUBLANES, LANES), w_conv))
    s_ss, s_rs, firsts, token = xchg_start("first_start", placed + [gi0], n_small + 3, plan_first)
    smalls_in = firsts[0:2]
    w_in_of[0] = ((s_ss, s_rs, firsts[2:3]), 0)
    g_pre_l, g_post_l, pscale_l, b_ada_l, m_w_conv_l, v_w_conv_l, token = lax.optimization_barrier(
        (g_pre, g_post, pool_scale, b_ada, m_w_conv, v_w_conv, token))
    g_pre3, g_post3 = g_pre_l.reshape(L, 1, D), g_post_l.reshape(L, 1, D)
    pscale3 = pscale_l.reshape(L, 1, pool_scale.shape[1])
    c_all3, wconv_all = xchg_wait("small_wait", smalls_in, s_ss, s_rs, n_small, plan_small, [token])
    c_all = c_all3.reshape(N_DEV, D)
    b_my = lax.dynamic_slice_in_dim(b_ada_l, chip * CW, CW, axis=1)
    m_ss, m_rs, mods, token = xchg_start("mod_start", [mod_part(pos, c_all, w_ada, b_my, token)], 3, plan_mod)
    gi1, go1 = cast_weights(pos, w_in, w_out, 1, token)
    flight, token = gather([go0, gi1, go1], [])
    w_out_of[0], w_in_of[1], w_out_of[1] = (flight, 0), (flight, 1), (flight, 2)
    late = []
    for l in range(2, L):
        late += list(cast_weights(pos, w_in, w_out, l, token))
    flight, token = gather(late, [])
    for l in range(2, L):
        w_in_of[l], w_out_of[l] = (flight, 2 * (l - 2)), (flight, 2 * (l - 2) + 1)
    fwd_in, token = arrive_part(*w_in_of[0], [token], base=n_small)
    (mod_all,) = xchg_wait("mod_wait", mods, m_ss, m_rs, 3, plan_mod, [token])
    mod = lax.dynamic_index_in_dim(mod_all, me_lin, axis=2, keepdims=False)
    mod4 = jnp.transpose(mod, (1, 0, 2)).reshape(L, 3, 1, D)

    xs, projs, yas, yps, ys, pooleds = [x.reshape(T, D)], [], [], [], [], []
    wg_in, wg_out = [], []
    for l in range(L):
        (gi,) = ready(fwd_in, [mod4 if l == 0 else xs[l]])
        proj = proj_fwd(xs[l], mod4, g_pre3, gi, l)
        ya, yp, pooled = mix_fwd(proj, wconv_all, w_pool, pscale3, l)
        pooleds.append(pooled)
        fwd_out, token = arrive_part(*w_out_of[l], [ya, yp])
        after = [token]
        if 0 < l < L - 1:
            fwd_in, token = arrive_part(*w_in_of[l + 1], after)
            after = [token]
        (go,) = ready(fwd_out, after)
        wg_in.append(gi)
        wg_out.append(go.reshape(N_CHIPS * RO, D))
        projs.append(proj)
        yas.append(ya)
        yps.append(yp)
        if l + 1 < L:
            xn, yv = out_fwd(ya, yp, wg_out[l], xs[l], mod4, g_post3, l, after[0])
            xs.append(xn)
            if l == 0:
                fwd_in, token = arrive_part(*w_in_of[1], [xn])
        else:
            dx, yv, loss_blk = out_fwd_loss(ya, yp, wg_out[l], xs[l], mod4, g_post3, l, loss_target.reshape(T, D))
        ys.append(yv)

    shapes = (w_in.shape, w_out.shape, w_pool.shape)
    smalls = [None] * L
    acc, flying, sib, token = None, None, None, loss_blk

    def to_chips(sib, after):
        sl, s_ss, s_rs, s_bufs = sib
        s_bufs = xchg_wait("sibling_wait", s_bufs, s_ss, s_rs, n_s, plan_sibling, after)
        chip_parts = add_sibling(pos, s_bufs[0:3], s_bufs[3:6])
        lands = [lax.empty((3,) + a.shape[1:], a.dtype) for a in chip_parts]
        c_ss, c_rs, c_bufs, ctoken = xchg_start("chip_start", list(chip_parts) + lands, n_c, plan_chip)
        return (sl, c_ss, c_rs, c_bufs), ctoken

    def landed(flying, acc, after):
        fl, f_ss, f_rs, f_bufs = flying
        f_bufs = xchg_wait("chip_wait", f_bufs, f_ss, f_rs, n_c, plan_chip, after)
        return sum_chips(pos, f_bufs[0:3], f_bufs[3:6], acc, fl, shapes)

    early = None
    for l in reversed(range(L)):
        dya, dyp, dwo_l, dgate, dgpost = out_bwd(dx, ys[l], yas[l], yps[l], wg_out[l], mod4, g_post3, l, token)
        token = dya
        spreading = None
        if sib is not None:
            arrived = flying
            flying, token = to_chips(sib, [dya])
            if arrived is not None:
                acc = landed(arrived, acc, [token])
                spreading = plan_spread((arrived[0],), ())
                sp_ss, sp_rs, acc, token = xchg_start("spread_start", list(acc), 2, spreading, sibling_only=True)
        du_a, db_a, dc_a, dg_a, du_p, dg_p, dwc, dwp_l, dps = mix_bwd(projs[l], pooleds[l], dya, dyp, wconv_all, w_pool,
                                                                        pscale3, l, token)
        update = None
        if spreading is not None:
            acc = xchg_wait("spread_wait", acc, sp_ss, sp_rs, 2, spreading, [du_a])
            update = (arrived[0], [w_in, acc[0], m_w_in, v_w_in], [w_out, acc[1], m_w_out, v_w_out], early)
        dx, dwi_l, dshift, dscale, dgpre, *rest = in_bwd([du_a, db_a, dc_a, dg_a, du_p, dg_p], wg_in[l], xs[l], dx,
                                                         mod4, g_pre3, l, update)
        early = rest if rest else early
        smalls[l] = (dgpre, dgpost, dshift, dscale, dgate, dps, dwc)
        parts = [dwi_l, dwo_l.reshape(N_CHIPS, RO, D), dwp_l]
        s_lands = [lax.empty((a.shape[0], a.shape[1] // 2) + a.shape[2:], a.dtype) for a in parts]
        s_ss, s_rs, s_bufs, token = xchg_start("sibling_start", parts + s_lands, n_s, plan_sibling, sibling_only=True)
        sib = (l, s_ss, s_rs, s_bufs)
    grad_x = dx.reshape(1, T, D)

    p_ss, p_rs, packs, ptoken = xchg_start("pack_start", [pack_small(pos, smalls, loss_blk)], N_DEV - 1, plan_pack)
    acc = landed(flying, acc, [ptoken, token])
    n_sp = 2 + (N_DEV - 1) * (L - 1)
    spread = plan_spread((1,), tuple(range(1, L)))
    sp_ss, sp_rs, acc, sp_token = xchg_start("spread_start", list(acc), n_sp, spread)
    flying, token = to_chips(sib, [sp_token])
    (packs_all,) = xchg_wait("pack_wait", packs, p_ss, p_rs, N_DEV - 1, plan_pack, [token])
    dmod_all = packs_all.reshape(N_DEV, L, SUBLANES, D)[:, :, ROW_MOD:ROW_MOD + 3].reshape(N_DEV, L, 3 * D)
    dmod_my = jnp.transpose(lax.dynamic_slice_in_dim(dmod_all, chip * CW, CW, axis=2), (1, 0, 2))

    g_w_ada, d_w_ada, nm_w_ada, nv_w_ada = ada_finish(c_all, dmod_my, w_ada, m_w_ada, v_w_ada)
    packs_late, _ = lax.optimization_barrier((packs_all, nv_w_ada))
    loss_row, upd = small_update(pos, packs_late, [b_ada, g_pre, g_post, pool_scale, w_conv],
                                 [m_b_ada, m_g_pre, m_g_post, m_pool_scale, m_w_conv_l],
                                 [v_b_ada, v_g_pre, v_g_post, v_pool_scale, v_w_conv_l])
    loss = loss_row[0, 0]
    (g_b_ada, d_b_ada, nm_b_ada, nv_b_ada), (g_g_pre, d_g_pre, nm_g_pre, nv_g_pre) = upd[0], upd[1]
    (g_g_post, d_g_post, nm_g_post, nv_g_post), (g_pscale, d_pscale, nm_pscale, nv_pscale) = upd[2], upd[3]
    g_w_conv, d_w_conv, nm_w_conv, nv_w_conv = upd[4]

    done = [nv_w_ada, nv_w_conv]
    g_w_in, g_w_out, g_w_pool = xchg_wait("spread_wait", acc, sp_ss, sp_rs, n_sp, spread, done)
    early = adamw([[w_in, g_w_in, m_w_in, v_w_in], [w_out, g_w_out, m_w_out, v_w_out]], "adamw_layer", 1, 1, 2, early)

    acc = landed(flying, (g_w_in, g_w_out, g_w_pool), [early[3], early[7]])
    last = plan_spread((0,), (0,))
    n_last = 2 + N_DEV - 1
    l_ss, l_rs, acc, _ = xchg_start("spread_start", list(acc), n_last, last)
    upd_pool = adamw([[w_pool, acc[2], m_w_pool, v_w_pool]], "adamw_w_pool", 1, L - 1, 1)
    r_w_in, r_w_out, r_w_pool = xchg_wait("spread_wait", acc, l_ss, l_rs, n_last, last, [upd_pool[3]])
    (g_w_in, d_w_in, nm_w_in, nv_w_in, g_w_out, d_w_out, nm_w_out, nv_w_out,
     g_w_pool, d_w_pool, nm_w_pool, nv_w_pool) = adamw(
         [[w_in, r_w_in, m_w_in, v_w_in], [w_out, r_w_out, m_w_out, v_w_out], [w_pool, r_w_pool, m_w_pool, v_w_pool]],
         "adamw_layer", 0, 1, 2, list(early) + list(upd_pool))

    return (loss, grad_x,
            g_w_ada, g_b_ada, g_g_pre, g_w_in, g_w_conv, g_w_pool, g_pscale, g_w_out, g_g_post,
            d_w_ada, d_b_ada, d_g_pre, d_w_in, d_w_conv, d_w_pool, d_pscale, d_w_out, d_g_post,
            nm_w_ada, nm_b_ada, nm_g_pre, nm_w_in, nm_w_conv, nm_w_pool, nm_pscale, nm_w_out, nm_g_post,
            nv_w_ada, nv_b_ada, nv_g_pre, nv_w_in, nv_w_conv, nv_w_pool, nv_pscale, nv_w_out, nv_g_post)
```

```python
import functools

import jax
import jax.numpy as jnp
from jax import lax
from jax.experimental import pallas as pl
from jax.experimental.pallas import tpu as pltpu

F32 = jnp.float32
BF16 = jnp.bfloat16
MESH = pl.DeviceIdType.MESH
ANY = pl.BlockSpec(memory_space=pl.ANY)

NORM_EPS = 1e-6
POOL_WINDOWS = (2, 4, 8, 16)
ADAM_LR = 0.001
ADAM_B1 = 0.9
ADAM_B2 = 0.999
ADAM_EPS = 1e-08
ADAM_WD = 0.01
ADAM_STEP = 10

N_CHIPS = 4
N_DEV = 8
LANES = 128
SUBLANES = 8
VMEM_BIG = 56 * 1024 * 1024
HIST = 16
R_CONV = 64
R_POOL = 128

ROW_G_PRE, ROW_G_POST, ROW_MOD, ROW_PSCALE, ROW_WCONV = 0, 1, 2, 5, 6
LOSS_LANES = slice(4 * LANES, 5 * LANES)

NT = (((1,), (1,)), ((), ()))
TN = (((0,), (0,)), ((), ()))


def _params(vmem=None, n_grid=1):
    kw = {}
    if n_grid:
        kw["dimension_semantics"] = ("arbitrary",) * n_grid
    if vmem is not None:
        kw["vmem_limit_bytes"] = vmem
    return pltpu.CompilerParams(**kw)


def _colsum8(v):
    n, d = v.shape
    return v.reshape(n // SUBLANES, SUBLANES, d).sum(axis=0)


def _rms(v):
    return lax.rsqrt(jnp.mean(v * v, axis=-1, keepdims=True) + NORM_EPS)


def _sigmoid(v):
    return 0.5 * jnp.tanh(0.5 * v) + 0.5


def _shift_down(ext, k, rows):
    if k == 0:
        return ext[HIST:HIST + rows]
    return pltpu.roll(ext, k, 0)[HIST:HIST + rows]


def _shift_up(ext, k, rows):
    if k == 0:
        return ext[0:rows]
    return pltpu.roll(ext, ext.shape[0] - k, 0)[0:rows]


def _load_ext(ref, r0, h0, first, rows):
    hist = ref[pl.ds(h0, HIST), :].astype(F32)
    hist = jnp.where(first, 0.0, hist)
    cur = ref[pl.ds(r0, rows), :].astype(F32)
    return jnp.concatenate([hist, cur], axis=0)


def _me():
    return lax.axis_index("x"), lax.axis_index("y"), lax.axis_index("c")


def cast_weights(pos, w_in, w_out, l, after):
    _, D, CW = w_in.shape
    RO = w_out.shape[1]

    def body(pos_ref, wi, wo, after_ref, oi, oo):
        oi[...] = wi[...].astype(BF16)
        oo[...] = wo[...].astype(BF16)

    return pl.pallas_call(
        body, name="cast_w",
        grid_spec=pltpu.PrefetchScalarGridSpec(
            num_scalar_prefetch=1, grid=(2,),
            in_specs=[pl.BlockSpec((None, D // 2, CW), lambda h, p: (l, h, 0)),
                      pl.BlockSpec((None, RO // 2, D), lambda h, p: (l, h, 0)), ANY],
            out_specs=[pl.BlockSpec((D // 2, CW), lambda h, p: (h, p[1])),
                       pl.BlockSpec((None, RO // 2, D), lambda h, p: (p[1], h, 0))]),
        out_shape=[jax.ShapeDtypeStruct((D, N_CHIPS * CW), BF16), jax.ShapeDtypeStruct((N_CHIPS, RO, D), BF16)],
        compiler_params=_params(),
    )(pos, w_in, w_out, after)


def mod_part(pos, c_all, w_ada, b_my, after):
    L, D, CW = w_ada.shape

    def body(pos_ref, c_ref, w_ref, b_ref, after_ref, o_ref):
        cv = c_ref[...]
        ca = (cv * jax.nn.sigmoid(cv)).astype(BF16)
        o_ref[...] = jnp.dot(ca, w_ref[0].astype(BF16), preferred_element_type=F32) + b_ref[0]

    return pl.pallas_call(
        body, name="mod_part",
        grid_spec=pltpu.PrefetchScalarGridSpec(
            num_scalar_prefetch=1, grid=(L,),
            in_specs=[pl.BlockSpec((N_DEV, D), lambda l, p: (0, 0)),
                      pl.BlockSpec((1, D, CW), lambda l, p: (l, 0, 0)),
                      pl.BlockSpec((1, 1, CW), lambda l, p: (l, 0, 0)), ANY],
            out_specs=pl.BlockSpec((None, None, N_DEV, CW), lambda l, p: (p[1], l, 0, 0))),
        out_shape=jax.ShapeDtypeStruct((N_CHIPS, L, N_DEV, CW), F32),
        compiler_params=_params(VMEM_BIG),
    )(pos, c_all, w_ada, b_my.reshape(L, 1, CW), after)


def _mod_row(l, k, D):
    return pl.BlockSpec((None, None, 1, D), lambda *_: (l, k, 0, 0))


def _layer_row(l, D):
    return pl.BlockSpec((None, 1, D), lambda *_: (l, 0, 0))


def proj_fwd(x, mod4, g_pre3, wg, l):
    T, D = x.shape
    NC = wg.shape[1]
    NB = N_CHIPS
    CW = NC // NB
    tm = 512

    def body(x_ref, sh_ref, sc_ref, g_ref, w_ref, o_ref):
        xv = x_ref[...]
        h = (xv * _rms(xv)) * (g_ref[...] * (1.0 + sc_ref[...])) + sh_ref[...]
        hb = h.astype(BF16)
        for j in range(NB):
            cols = slice(j * CW, (j + 1) * CW)
            o_ref[:, cols] = jnp.dot(hb, w_ref[:, cols], preferred_element_type=F32).astype(BF16)

    return pl.pallas_call(
        body, name="proj_fwd", grid=(T // tm,),
        in_specs=[pl.BlockSpec((tm, D), lambda i: (i, 0)), _mod_row(l, 0, D), _mod_row(l, 1, D), _layer_row(l, D),
                  pl.BlockSpec((D, NC), lambda i: (0, 0))],
        out_specs=pl.BlockSpec((tm, NC), lambda i: (i, 0)),
        out_shape=jax.ShapeDtypeStruct((T, NC), BF16),
        compiler_params=_params(VMEM_BIG),
    )(x, mod4, mod4, g_pre3, wg)


N_MIX = 4


def _conv_fwd_block(u_ref, b_ref, c_ref, g_ref, w_ref, o_ref):
    T = u_ref.shape[0]
    R = 2 * R_CONV
    w0 = w_ref[pl.ds(0, 1), :]
    w1 = w_ref[pl.ds(1, 1), :]
    w2 = w_ref[pl.ds(2, 1), :]

    def chunk(i, carry):
        r0 = pl.multiple_of(i * R, R)
        h0 = pl.multiple_of(jnp.maximum(r0 - HIST, 0), HIST)
        first = i == 0
        ca = _load_ext(c_ref, r0, h0, first, R) * _load_ext(u_ref, r0, h0, first, R)
        conv = w2 * ca[HIST:] + w1 * _shift_down(ca, 1, R) + w0 * _shift_down(ca, 2, R)
        g = g_ref[pl.ds(r0, R), :].astype(F32)
        b = b_ref[pl.ds(r0, R), :].astype(F32)
        o_ref[pl.ds(r0, R), :] = (b * conv * (g * _sigmoid(g))).astype(BF16)
        return carry

    lax.fori_loop(0, T // R, chunk, 0)


def _conv_idx(j):
    return jnp.minimum(j, N_MIX - 1)


def _pool_idx(j):
    return jnp.maximum(j - N_MIX, 0)


def _proj_col(T, off, idx):
    return pl.BlockSpec((T, LANES), lambda j: (0, idx(j) + off))


def _causal_window_sum(ext, w):
    s, k = ext, 1
    while k < w:
        s = s + pltpu.roll(s, k, 0)
        k *= 2
    return s


def _anticausal_window_sum(ext, w):
    s, k = ext, 1
    n = ext.shape[0]
    while k < w:
        s = s + pltpu.roll(s, n - k, 0)
        k *= 2
    return s


def _count(r0, rows, w):
    t = r0 + lax.broadcasted_iota(jnp.int32, (rows, LANES), 0)
    return jnp.minimum(t + 1, w).astype(F32)


def _pooled_loop(p_ref, pooled_s, w, T):
    R = R_POOL

    def chunk(i, carry):
        r0 = pl.multiple_of(i * R, R)
        h0 = pl.multiple_of(jnp.maximum(r0 - HIST, 0), HIST)
        ext = _load_ext(p_ref, r0, h0, i == 0, R)
        ws = _causal_window_sum(ext, w)[HIST:]
        pooled_s[pl.ds(r0, R), :] = (ws / _count(r0, R, w) - ext[HIST:]).astype(BF16)
        return carry

    lax.fori_loop(0, T // R, chunk, 0)


def _conv_w_spec(l):
    return pl.BlockSpec((None, None, 3, LANES), lambda j: (_conv_idx(j), l, 0, 0))


def _pool_w_spec(l):
    return pl.BlockSpec((None, None, LANES, LANES), lambda j: (l, _pool_idx(j), 0, 0))


def _pool_s_spec(l):
    return pl.BlockSpec((None, 1, LANES), lambda j: (l, 0, _pool_idx(j)))


def _pool_fwd_group(p_ref, g_ref, w_ref, s_ref, o_ref, pooled_s, mixed_s, w):
    T = p_ref.shape[0]
    R = R_POOL
    _pooled_loop(p_ref, pooled_s, w, T)
    mixed_s[...] = jnp.dot(pooled_s[...], w_ref[...].astype(BF16), preferred_element_type=F32)
    sc = s_ref[...]

    def chunk(i, carry):
        r0 = pl.multiple_of(i * R, R)
        g = g_ref[pl.ds(r0, R), :].astype(F32)
        o_ref[pl.ds(r0, R), :] = (mixed_s[pl.ds(r0, R), :] * sc * (g * _sigmoid(g))).astype(BF16)
        return carry

    lax.fori_loop(0, T // R, chunk, 0)


def mix_fwd(proj, wconv, wpool, pscale3, l):
    T = proj.shape[0]

    def body(u_ref, b_ref, c_ref, g_ref, p_ref, gp_ref, wc_ref, wp_ref, s_ref, ya_ref, yp_ref, pooled_ref, mixed_s):
        j = pl.program_id(0)
        pl.when(j < N_MIX)(functools.partial(_conv_fwd_block, u_ref, b_ref, c_ref, g_ref, wc_ref, ya_ref))
        for k, w in enumerate(POOL_WINDOWS):
            pl.when(j == N_MIX + k)(functools.partial(_pool_fwd_group, p_ref, gp_ref, wp_ref, s_ref, yp_ref,
                                                      pooled_ref, mixed_s, w))

    half = jax.ShapeDtypeStruct((T, N_MIX * LANES), BF16)
    pool_col = pl.BlockSpec((T, LANES), lambda j: (0, _pool_idx(j)))
    return pl.pallas_call(
        body, name="mix_fwd", grid=(2 * N_MIX,),
        in_specs=[_proj_col(T, 0, _conv_idx), _proj_col(T, 4, _conv_idx), _proj_col(T, 8, _conv_idx),
                  _proj_col(T, 12, _conv_idx), _proj_col(T, 16, _pool_idx), _proj_col(T, 20, _pool_idx),
                  _conv_w_spec(l), _pool_w_spec(l), _pool_s_spec(l)],
        out_specs=[pl.BlockSpec((T, LANES), lambda j: (0, _conv_idx(j))), pool_col, pool_col],
        out_shape=[half, half, half],
        scratch_shapes=[pltpu.VMEM((T, LANES), F32)],
        compiler_params=_params(),
    )(proj, proj, proj, proj, proj, proj, wconv, wpool, pscale3)


def out_fwd(ya, yp, wo, x, mod4, g_post3, l, after):
    T, D = x.shape
    H = ya.shape[1]
    tm = 512

    def body(ya_ref, yp_ref, wo_ref, x_ref, gt_ref, g_ref, after_ref, xn_ref, y_ref):
        y = (jnp.dot(ya_ref[...], wo_ref[0:H, :], preferred_element_type=F32)
             + jnp.dot(yp_ref[...], wo_ref[H:2 * H, :], preferred_element_type=F32))
        xn_ref[...] = x_ref[...] + gt_ref[...] * (y * _rms(y) * g_ref[...])
        y_ref[...] = y.astype(BF16)

    tile = pl.BlockSpec((tm, D), lambda i: (i, 0))
    half = pl.BlockSpec((tm, H), lambda i: (i, 0))
    return pl.pallas_call(
        body, name="out_fwd", grid=(T // tm,),
        in_specs=[half, half, pl.BlockSpec((2 * H, D), lambda i: (0, 0)), tile, _mod_row(l, 2, D), _layer_row(l, D),
                  ANY],
        out_specs=[tile, tile],
        out_shape=[jax.ShapeDtypeStruct((T, D), F32), jax.ShapeDtypeStruct((T, D), BF16)],
        compiler_params=_params(VMEM_BIG),
    )(ya, yp, wo, x, mod4, g_post3, after)


def out_fwd_loss(ya, yp, wo, x, mod4, g_post3, l, target):
    T, D = x.shape
    H = ya.shape[1]
    tm = 512
    nt = T // tm

    def body(ya_ref, yp_ref, wo_ref, x_ref, gt_ref, g_ref, t_ref, dx_ref, y_ref, l_ref, acc):
        i = pl.program_id(0)

        @pl.when(i == 0)
        def _():
            acc[...] = jnp.zeros_like(acc)

        y = (jnp.dot(ya_ref[...], wo_ref[0:H, :], preferred_element_type=F32)
             + jnp.dot(yp_ref[...], wo_ref[H:2 * H, :], preferred_element_type=F32))
        y_ref[...] = y.astype(BF16)
        d = (x_ref[...] + gt_ref[...] * (y * _rms(y) * g_ref[...])) - t_ref[...]
        dx_ref[...] = d * (1.0 / D)
        acc[...] += _colsum8(d * d)

        @pl.when(i == nt - 1)
        def _():
            l_ref[...] = jnp.zeros_like(l_ref) + jnp.sum(acc[...]) * (0.5 / D)

    tile = pl.BlockSpec((tm, D), lambda i: (i, 0))
    half = pl.BlockSpec((tm, H), lambda i: (i, 0))
    return pl.pallas_call(
        body, name="out_fwd_loss", grid=(nt,),
        in_specs=[half, half, pl.BlockSpec((2 * H, D), lambda i: (0, 0)), tile, _mod_row(l, 2, D), _layer_row(l, D),
                  tile],
        out_specs=[tile, tile, pl.BlockSpec((SUBLANES, LANES), lambda i: (0, 0))],
        out_shape=[jax.ShapeDtypeStruct((T, D), F32), jax.ShapeDtypeStruct((T, D), BF16),
                   jax.ShapeDtypeStruct((SUBLANES, LANES), F32)],
        scratch_shapes=[pltpu.VMEM((SUBLANES, D), F32)],
        compiler_params=_params(VMEM_BIG),
    )(ya, yp, wo, x, mod4, g_post3, target)


def out_bwd(dx, y, ya, yp, wo, mod4, g_post3, l, after):
    T, D = dx.shape
    H = ya.shape[1]
    tm = 512
    nt = T // tm

    def body(dx_ref, y_ref, ya_ref, yp_ref, wo_ref, gt_ref, g_ref, after_ref,
             dya_ref, dyp_ref, dwo_ref, dgt_ref, dg_ref, acc_w, acc_p):
        i = pl.program_id(0)

        @pl.when(i == 0)
        def _():
            acc_w[...] = jnp.zeros_like(acc_w)
            acc_p[...] = jnp.zeros_like(acc_p)

        yv = y_ref[...].astype(F32)
        dxv = dx_ref[...]
        gg = gt_ref[...] * g_ref[...]
        r = _rms(yv)
        yn = yv * r
        p = dxv * yn
        acc_p[...] += _colsum8(p)
        dy = r * (dxv * gg - yn * jnp.mean(p * gg, axis=-1, keepdims=True))
        dyb = dy.astype(BF16)
        dyc = lax.dot_general(dyb, wo_ref[...], NT, preferred_element_type=F32)
        dya_ref[...] = dyc[:, 0:H].astype(BF16)
        dyp_ref[...] = dyc[:, H:2 * H].astype(BF16)
        acc_w[0:H, :] += lax.dot_general(ya_ref[...], dyb, TN, preferred_element_type=F32)
        acc_w[H:2 * H, :] += lax.dot_general(yp_ref[...], dyb, TN, preferred_element_type=F32)

        @pl.when(i == nt - 1)
        def _():
            dwo_ref[...] = acc_w[...].astype(BF16)
            sp = jnp.sum(acc_p[...], axis=0, keepdims=True)
            dgt_ref[...] = g_ref[...] * sp
            dg_ref[...] = gt_ref[...] * sp

    row = pl.BlockSpec((1, D), lambda i: (0, 0))
    tile = pl.BlockSpec((tm, D), lambda i: (i, 0))
    half = pl.BlockSpec((tm, H), lambda i: (i, 0))
    full = pl.BlockSpec((2 * H, D), lambda i: (0, 0))
    return pl.pallas_call(
        body, name="out_bwd", grid=(nt,),
        in_specs=[tile, tile, half, half, full, _mod_row(l, 2, D), _layer_row(l, D), ANY],
        out_specs=[half, half, full, row, row],
        out_shape=[jax.ShapeDtypeStruct((T, H), BF16), jax.ShapeDtypeStruct((T, H), BF16),
                   jax.ShapeDtypeStruct((2 * H, D), BF16),
                   jax.ShapeDtypeStruct((1, D), F32), jax.ShapeDtypeStruct((1, D), F32)],
        scratch_shapes=[pltpu.VMEM((2 * H, D), F32), pltpu.VMEM((SUBLANES, D), F32)],
        compiler_params=_params(VMEM_BIG),
    )(dx, y, ya, yp, wo, mod4, g_post3, after)


def _conv_bwd_block(u_ref, b_ref, c_ref, g_ref, dy_ref, w_ref, du_ref, db_ref, dc_ref, dg_ref, dw_ref):
    T = u_ref.shape[0]
    R = R_CONV
    nchunk = T // R
    w0 = w_ref[pl.ds(0, 1), :]
    w1 = w_ref[pl.ds(1, 1), :]
    w2 = w_ref[pl.ds(2, 1), :]

    def chunk(k, carry):
        head, a0, a1, a2 = carry
        i = nchunk - 1 - k
        r0 = pl.multiple_of(i * R, R)
        h0 = pl.multiple_of(jnp.maximum(r0 - HIST, 0), HIST)
        first = i == 0
        ue = _load_ext(u_ref, r0, h0, first, R)
        ce = _load_ext(c_ref, r0, h0, first, R)
        ca = ce * ue
        ca0 = ca[HIST:]
        ca1 = _shift_down(ca, 1, R)
        ca2 = _shift_down(ca, 2, R)
        conv = w2 * ca0 + w1 * ca1 + w0 * ca2
        g = g_ref[pl.ds(r0, R), :].astype(F32)
        b = b_ref[pl.ds(r0, R), :].astype(F32)
        dy = dy_ref[pl.ds(r0, R), :].astype(F32)
        sg = _sigmoid(g)
        sl = g * sg
        t = dy * conv
        db_ref[pl.ds(r0, R), :] = (t * sl).astype(BF16)
        dg_ref[pl.ds(r0, R), :] = (t * b * (sg + sl * (1.0 - sg))).astype(BF16)
        dconv = dy * b * sl
        a2 = a2 + _colsum8(dconv * ca0)
        a1 = a1 + _colsum8(dconv * ca1)
        a0 = a0 + _colsum8(dconv * ca2)
        e = jnp.concatenate([dconv, head], axis=0)
        dca = w2 * dconv + w1 * _shift_up(e, 1, R) + w0 * _shift_up(e, 2, R)
        du_ref[pl.ds(r0, R), :] = (dca * ce[HIST:]).astype(BF16)
        dc_ref[pl.ds(r0, R), :] = (dca * ue[HIST:]).astype(BF16)
        return dconv[0:SUBLANES], a0, a1, a2

    z = jnp.zeros((SUBLANES, LANES), F32)
    _, a0, a1, a2 = lax.fori_loop(0, nchunk, chunk, (z, z, z, z))
    dw_ref[pl.ds(0, 1), :] = jnp.sum(a0, axis=0, keepdims=True)
    dw_ref[pl.ds(1, 1), :] = jnp.sum(a1, axis=0, keepdims=True)
    dw_ref[pl.ds(2, 1), :] = jnp.sum(a2, axis=0, keepdims=True)


def _pool_bwd_group(pooled_s, g_ref, dy_ref, w_ref, s_ref, du_ref, dg_ref, dw_ref, ds_ref,
                    mixed_s, dmix_s, dpool_s, w):
    T = pooled_s.shape[0]
    R = R_POOL
    nchunk = T // R
    wb = w_ref[...].astype(BF16)
    mixed_s[...] = jnp.dot(pooled_s[...], wb, preferred_element_type=F32)
    sc = s_ref[...]

    def gate_chunk(i, acc):
        r0 = pl.multiple_of(i * R, R)
        g = g_ref[pl.ds(r0, R), :].astype(F32)
        dy = dy_ref[pl.ds(r0, R), :].astype(F32)
        mixed = mixed_s[pl.ds(r0, R), :]
        sg = _sigmoid(g)
        sl = g * sg
        dg_ref[pl.ds(r0, R), :] = (dy * mixed * sc * (sg + sl * (1.0 - sg))).astype(BF16)
        dms = dy * sl
        dmix_s[pl.ds(r0, R), :] = (dms * sc).astype(BF16)
        return acc + _colsum8(dms * mixed)

    acc = lax.fori_loop(0, nchunk, gate_chunk, jnp.zeros((SUBLANES, LANES), F32))
    ds_ref[...] = jnp.sum(acc, axis=0, keepdims=True)
    dpool_s[pl.ds(0, T), :] = lax.dot_general(dmix_s[...], wb, NT, preferred_element_type=F32)
    dpool_s[pl.ds(T, HIST), :] = jnp.zeros((HIST, LANES), F32)
    dw_ref[...] = lax.dot_general(pooled_s[...], dmix_s[...], TN, preferred_element_type=F32).astype(BF16)

    def back_chunk(i, carry):
        r0 = pl.multiple_of(i * R, R)
        dpe = dpool_s[pl.ds(r0, R + HIST), :]
        e = dpe / _count(r0, R + HIST, w)
        du_ref[pl.ds(r0, R), :] = (_anticausal_window_sum(e, w)[0:R] - dpe[0:R]).astype(BF16)
        return carry

    lax.fori_loop(0, nchunk, back_chunk, 0)


def mix_bwd(proj, pooled, dya, dyp, wconv, wpool, pscale3, l, after):
    T = proj.shape[0]

    def body(u_ref, b_ref, c_ref, g_ref, pooled_ref, gp_ref, dya_ref, dyp_ref, wc_ref, wp_ref, s_ref, after_ref,
             dua_ref, dba_ref, dca_ref, dga_ref, dup_ref, dgp_ref, dwc_ref, dwp_ref, ds_ref,
             mixed_s, dmix_s, dpool_s):
        j = pl.program_id(0)
        pl.when(j < N_MIX)(functools.partial(_conv_bwd_block, u_ref, b_ref, c_ref, g_ref, dya_ref, wc_ref,
                                             dua_ref, dba_ref, dca_ref, dga_ref, dwc_ref))
        for k, w in enumerate(POOL_WINDOWS):
            pl.when(j == N_MIX + k)(functools.partial(_pool_bwd_group, pooled_ref, gp_ref, dyp_ref, wp_ref, s_ref,
                                                      dup_ref, dgp_ref, dwp_ref, ds_ref,
                                                      mixed_s, dmix_s, dpool_s, w))

    sec = jax.ShapeDtypeStruct((T, N_MIX * LANES), BF16)
    conv_col = pl.BlockSpec((T, LANES), lambda j: (0, _conv_idx(j)))
    pool_col = pl.BlockSpec((T, LANES), lambda j: (0, _pool_idx(j)))
    return pl.pallas_call(
        body, name="mix_bwd", grid=(2 * N_MIX,),
        in_specs=[_proj_col(T, 0, _conv_idx), _proj_col(T, 4, _conv_idx), _proj_col(T, 8, _conv_idx),
                  _proj_col(T, 12, _conv_idx), pool_col, _proj_col(T, 20, _pool_idx),
                  conv_col, pool_col, _conv_w_spec(l), _pool_w_spec(l), _pool_s_spec(l), ANY],
        out_specs=[conv_col, conv_col, conv_col, conv_col, pool_col, pool_col,
                   pl.BlockSpec((None, 3, LANES), lambda j: (_conv_idx(j), 0, 0)),
                   pl.BlockSpec((None, LANES, LANES), lambda j: (_pool_idx(j), 0, 0)),
                   pl.BlockSpec((1, LANES), lambda j: (0, _pool_idx(j)))],
        out_shape=[sec] * 6 + [jax.ShapeDtypeStruct((N_MIX, 3, LANES), F32),
                               jax.ShapeDtypeStruct((N_MIX, LANES, LANES), BF16),
                               jax.ShapeDtypeStruct((1, N_MIX * LANES), F32)],
        scratch_shapes=[pltpu.VMEM((T, LANES), F32), pltpu.VMEM((T, LANES), BF16), pltpu.VMEM((T + HIST, LANES), F32)],
        compiler_params=_params(),
    )(proj, proj, proj, proj, pooled, proj, dya, dyp, wconv, wpool, pscale3, after)


def in_bwd(dsecs, wg, x, dxo, mod4, g_pre3, l, update=None):
    T, D = x.shape
    NB = N_CHIPS
    CW = wg.shape[1] // NB
    SW = dsecs[0].shape[1]
    nsec = len(dsecs)
    PW = 256
    assert SW % PW == 0 and CW % PW == 0
    tm = 256
    nt = T // tm
    n_in = nsec + 6
    n_upd = 0 if update is None else 8
    n_acc = 0 if update is None or update[3] is None else 8

    def body(*refs):
        d_refs = refs[0:nsec]
        w_ref, x_ref, dxo_ref, sh_ref, sc_ref, g_ref = refs[nsec:n_in]
        outs = refs[n_in + n_upd + n_acc:]
        dxi_ref, dw_ref, dsh_ref, dsc_ref, dg_ref = outs[0:5]
        acc_w, acc_sh, acc_q = outs[5 + n_upd:]
        i = pl.program_id(0)
        for k in range(0, n_upd, 4):
            _adamw_block(refs[n_in + k:n_in + k + 4], outs[5 + k:5 + k + 4])

        @pl.when(i == 0)
        def _():
            acc_w[...] = jnp.zeros_like(acc_w)
            acc_sh[...] = jnp.zeros_like(acc_sh)
            acc_q[...] = jnp.zeros_like(acc_q)

        xv = x_ref[...]
        r = _rms(xv)
        xh = xv * r
        sg = g_ref[...] * (1.0 + sc_ref[...])
        hb = (xh * sg + sh_ref[...]).astype(BF16)
        dh = lax.dot_general(d_refs[0][...], w_ref[:, 0:SW], NT, preferred_element_type=F32)
        for s in range(1, nsec):
            dh = dh + lax.dot_general(d_refs[s][...], w_ref[:, s * SW:(s + 1) * SW], NT, preferred_element_type=F32)
        for p in range(nsec * SW // PW):
            col = p * PW
            s, so = col // SW, col % SW
            j, jo = col // CW, col % CW
            acc_w[j, :, jo:jo + PW] += lax.dot_general(hb, d_refs[s][:, so:so + PW], TN, preferred_element_type=F32)
        q = dh * xh
        acc_sh[...] += _colsum8(dh)
        acc_q[...] += _colsum8(q)
        dxi_ref[...] = dxo_ref[...] + r * (dh * sg - xh * jnp.mean(q * sg, axis=-1, keepdims=True))

        @pl.when(i == nt - 1)
        def _():
            dw_ref[...] = acc_w[...].astype(BF16)
            sq = jnp.sum(acc_q[...], axis=0, keepdims=True)
            dsh_ref[...] = jnp.sum(acc_sh[...], axis=0, keepdims=True)
            dsc_ref[...] = g_ref[...] * sq
            dg_ref[...] = (1.0 + sc_ref[...]) * sq

    row = pl.BlockSpec((1, D), lambda i: (0, 0))
    tile = pl.BlockSpec((tm, D), lambda i: (i, 0))
    sect = pl.BlockSpec((tm, SW), lambda i: (i, 0))
    rowshape = jax.ShapeDtypeStruct((1, D), F32)
    in_specs = [sect] * nsec + [pl.BlockSpec((D, NB * CW), lambda i: (0, 0)), tile, tile,
                                _mod_row(l, 0, D), _mod_row(l, 1, D), _layer_row(l, D)]
    out_specs = [tile, pl.BlockSpec((NB, D, CW), lambda i: (0, 0, 0)), row, row, row]
    out_shape = [jax.ShapeDtypeStruct((T, D), F32), jax.ShapeDtypeStruct((NB, D, CW), BF16), rowshape, rowshape, rowshape]
    args = [*dsecs, wg, x, dxo, mod4, mod4, g_pre3]
    aliases = {}
    if update is not None:
        layer, of_w_in, of_w_out, acc = update
        for group in (of_w_in, of_w_out):
            _, rows, cols = group[0].shape
            spec = pl.BlockSpec((None, rows // nt, cols), lambda i: (layer, i, 0))
            in_specs += [spec] * 4
            out_specs += [spec] * 4
            out_shape += [jax.ShapeDtypeStruct(group[0].shape, F32)] * 4
            args += list(group)
        if acc is not None:
            aliases = {len(args) + a: 5 + a for a in range(n_acc)}
            in_specs += [ANY] * n_acc
            args += list(acc)
    return pl.pallas_call(
        body, name="in_bwd", grid=(nt,),
        in_specs=in_specs, out_specs=out_specs, out_shape=out_shape, input_output_aliases=aliases,
        scratch_shapes=[pltpu.VMEM((NB, D, CW), F32),
                        pltpu.VMEM((SUBLANES, D), F32), pltpu.VMEM((SUBLANES, D), F32)],
        compiler_params=_params(VMEM_BIG),
    )(*args)


def _rcopy(src, dst, ssem, rsem, dev):
    return pltpu.make_async_remote_copy(src_ref=src, dst_ref=dst, send_sem=ssem, recv_sem=rsem,
                                        device_id=dev, device_id_type=MESH)


def _peers7(x, y, c):
    out = []
    for m in range(1, N_DEV):
        bx, by, bc = (m >> 2) & 1, (m >> 1) & 1, m & 1
        out.append(((1 - x) if bx else x, (1 - y) if by else y, (1 - c) if bc else c))
    return out


HBM = pl.BlockSpec(memory_space=pltpu.HBM)
SEM = pl.BlockSpec(memory_space=pltpu.SEMAPHORE)
SPLIT = pltpu.CompilerParams(has_side_effects=pltpu.SideEffectType.DATAFLOW_SIDE_EFFECTING)


def _hbm(a):
    return pltpu.with_memory_space_constraint(a, pltpu.HBM)


def _chips(x, y):
    return [(1 - x, y), (x, 1 - y), (1 - x, 1 - y)]


SIBLING_BARRIER_ID = 0


def xchg_start(name, bufs, n_copies, plan, sibling_only=False, after=()):
    n = len(bufs)
    after = list(after)

    def body(*refs):
        ssem, rsem, token = refs[n + len(after)], refs[n + len(after) + 1], refs[-1]
        x, y, c = _me()
        if sibling_only:
            barrier = pltpu.get_barrier_semaphore()
            pl.semaphore_signal(barrier, inc=1, device_id=(x, y, 1 - c), device_id_type=MESH)
            pl.semaphore_wait(barrier, 1)
        copies = plan(refs[0:n], x, y, c)
        assert len(copies) == n_copies
        for k, (src, dst, peer, _) in enumerate(copies):
            _rcopy(src, dst, ssem.at[k], rsem.at[k], peer).start()
        token[...] = jnp.zeros_like(token)

    params = dict(has_side_effects=pltpu.SideEffectType.DATAFLOW_SIDE_EFFECTING)
    if sibling_only:
        params["collective_id"] = SIBLING_BARRIER_ID
    outs = pl.pallas_call(
        body, name=name,
        in_specs=[HBM] * n + [ANY] * len(after),
        out_specs=[SEM, SEM] + [HBM] * n + [pl.BlockSpec(memory_space=pltpu.VMEM)],
        out_shape=([pltpu.SemaphoreType.DMA((n_copies,))] * 2 + [pltpu.HBM(b.shape, b.dtype) for b in bufs]
                   + [jax.ShapeDtypeStruct((SUBLANES, LANES), F32)]),
        input_output_aliases={a: 2 + a for a in range(n)},
        compiler_params=pltpu.CompilerParams(**params),
    )(*[_hbm(b) for b in bufs], *after)
    return outs[0], outs[1], list(outs[2:2 + n]), outs[-1]


def xchg_wait(name, bufs, ssem, rsem, n_copies, plan, after, sems=None):
    n = len(bufs)
    after = list(after)
    sems = tuple(range(n_copies)) if sems is None else tuple(sems)
    assert len(sems) == n_copies

    def body(*refs):
        ssem_ref, rsem_ref = refs[n], refs[n + 1]
        copies = plan(refs[0:n], *_me())
        assert len(copies) == n_copies
        for k, (src, _, peer, land) in zip(sems, copies):
            cp = _rcopy(src, land, ssem_ref.at[k], rsem_ref.at[k], peer)
            cp.wait_send()
            cp.wait_recv()

    outs = pl.pallas_call(
        body, name=name,
        in_specs=[HBM] * n + [SEM, SEM] + [ANY] * len(after), out_specs=[HBM] * n,
        out_shape=[pltpu.HBM(b.shape, b.dtype) for b in bufs],
        input_output_aliases={a: a for a in range(n)},
        compiler_params=SPLIT,
    )(*bufs, ssem, rsem, *after)
    return list(outs)


def _shard_half(buf, chip, half):
    if len(buf.shape) == 2:
        h, w = buf.shape[0] // 2, buf.shape[1] // N_CHIPS
        return buf.at[pl.ds(half * h, h), pl.ds(chip * w, w)]
    h = buf.shape[1] // 2
    return buf.at[chip, pl.ds(half * h, h)]


def plan_gather(refs, x, y, c):
    out = []
    for buf in refs:
        own = _shard_half(buf, 2 * x + y, c)
        for (px, py) in _chips(x, y):
            out.append((own, own, (px, py, c), _shard_half(buf, 2 * px + py, c)))
    return out


def plan_forward(refs, x, y, c):
    out = []
    for (px, py) in _chips(x, y):
        for buf in refs:
            landed = _shard_half(buf, 2 * px + py, c)
            out.append((landed, landed, (x, y, 1 - c), _shard_half(buf, 2 * px + py, 1 - c)))
    return out


def plan_sibling(refs, x, y, c):
    n = len(refs) // 2
    out = []
    for a in range(n):
        h = refs[a].shape[1] // 2
        out.append((refs[a].at[:, pl.ds((1 - c) * h, h)], refs[n + a], (x, y, 1 - c), refs[n + a]))
    return out


def plan_chip(refs, x, y, c):
    n = len(refs) // 2
    out = []
    for j, (px, py) in enumerate(_chips(x, y)):
        for a in range(n):
            out.append((refs[a].at[2 * px + py], refs[n + a].at[j], (px, py, c), refs[n + a].at[j]))
    return out


def plan_mod(refs, x, y, c):
    (mods,) = refs
    mine = mods.at[2 * x + y]
    return [(mine, mine, (px, py, c), mods.at[2 * px + py]) for (px, py) in _chips(x, y)]


def plan_pack(refs, x, y, c):
    (packs,) = refs
    mine = packs.at[4 * x + 2 * y + c]
    return [(mine, mine, peer, packs.at[4 * peer[0] + 2 * peer[1] + peer[2]]) for peer in _peers7(x, y, c)]


def plan_spread(layers, wp_layers):
    def plan(refs, x, y, c):
        gi, go, gp = refs
        hD, hR, hP = gi.shape[1] // 2, go.shape[1] // 2, gp.shape[2] // 2
        sib = (x, y, 1 - c)
        out = []
        for l in layers:
            mine = gi.at[l, pl.ds(c * hD, hD)]
            out.append((mine, mine, sib, gi.at[l, pl.ds((1 - c) * hD, hD)]))
            mine = go.at[l, pl.ds(c * hR, hR)]
            out.append((mine, mine, sib, go.at[l, pl.ds((1 - c) * hR, hR)]))
        for l in wp_layers:
            mine = gp.at[l, 2 * x + y, pl.ds(c * hP, hP)]
            for peer in _peers7(x, y, c):
                out.append((mine, mine, peer, gp.at[l, 2 * peer[0] + peer[1], pl.ds(peer[2] * hP, hP)]))
        return out

    return plan


def place_small(pos, c8, wc):
    L = wc.shape[0]

    def body(pos_ref, c_ref, wc_ref, call_ref, wcall_ref):
        call_ref[...] = c_ref[...]
        wcall_ref[...] = wc_ref[...]

    return pl.pallas_call(
        body, name="place_small",
        grid_spec=pltpu.PrefetchScalarGridSpec(
            num_scalar_prefetch=1, grid=(1,),
            in_specs=[pl.BlockSpec((SUBLANES, LANES), lambda i, p: (0, 0)),
                      pl.BlockSpec((L, 3, LANES), lambda i, p: (0, 0, 0))],
            out_specs=[pl.BlockSpec((None, SUBLANES, LANES), lambda i, p: (p[2], 0, 0)),
                       pl.BlockSpec((None, L, 3, LANES), lambda i, p: (p[1], 0, 0, 0))]),
        out_shape=[jax.ShapeDtypeStruct((N_DEV, SUBLANES, LANES), F32),
                   jax.ShapeDtypeStruct((N_CHIPS, L, 3, LANES), F32)],
        compiler_params=_params(),
    )(pos, c8, wc)


def plan_small(refs, x, y, c):
    call, wcall = refs
    mine = call.at[4 * x + 2 * y + c]
    out = [(mine, mine, peer, call.at[4 * peer[0] + 2 * peer[1] + peer[2]]) for peer in _peers7(x, y, c)]
    mine = wcall.at[2 * x + y]
    out += [(mine, mine, (px, py, c), wcall.at[2 * px + py]) for (px, py) in _chips(x, y)]
    return out


def add_sibling(cidx, mine, sib):
    def body(c_ref, *refs):
        for a in range(3):
            m, s, o = refs[a], refs[3 + a], refs[6 + a]
            o[...] = (m[...].astype(F32) + s[...].astype(F32)).astype(BF16)

    per_step = 2

    def mine_spec(a):
        h = a.shape[1] // 2
        return pl.BlockSpec((per_step, h, a.shape[2]), lambda j, c_ref: (j, c_ref[0], 0))

    def sib_spec(a):
        return pl.BlockSpec((per_step,) + a.shape[1:], lambda j, c_ref: (j, 0, 0))

    return pl.pallas_call(
        body, name="add_sibling",
        grid_spec=pltpu.PrefetchScalarGridSpec(
            num_scalar_prefetch=1, grid=(N_CHIPS // per_step,),
            in_specs=[mine_spec(a) for a in mine] + [sib_spec(a) for a in sib],
            out_specs=[sib_spec(a) for a in sib]),
        out_shape=[jax.ShapeDtypeStruct(a.shape, BF16) for a in sib],
        compiler_params=_params(VMEM_BIG),
    )(cidx, *mine, *sib)


def sum_chips(pos, own, rb, acc, l, shapes):
    nq = 2
    n_in = 6 + (3 if acc is not None else 0)

    def body(pos_ref, *refs):
        for a in range(3):
            m, b, o = refs[a], refs[3 + a], refs[n_in + a]
            s = m[...].astype(F32)
            for j in range(3):
                s = s + b[j].astype(F32)
            o[...] = s

    def own_spec(a):
        return pl.BlockSpec((None, a.shape[1] // nq, a.shape[2]), lambda q, p: (p[1], q, 0))

    def rb_spec(a):
        return pl.BlockSpec((3, a.shape[1] // nq, a.shape[2]), lambda q, p: (0, q, 0))

    hi, ho, hp = own[0].shape[1] // nq, own[1].shape[1] // nq, own[2].shape[1] // nq
    out_specs = [pl.BlockSpec((None, hi, shapes[0][2]), lambda q, p: (l, p[0] * nq + q, 0)),
                 pl.BlockSpec((None, ho, shapes[1][2]), lambda q, p: (l, p[0] * nq + q, 0)),
                 pl.BlockSpec((None, None, hp, LANES), lambda q, p: (l, p[1], p[0] * nq + q, 0))]
    in_specs = [own_spec(a) for a in own] + [rb_spec(a) for a in rb]
    args = list(own) + list(rb)
    aliases = {}
    if acc is not None:
        in_specs += [ANY] * 3
        args += list(acc)
        aliases = {7: 0, 8: 1, 9: 2}
    return pl.pallas_call(
        body, name="sum_chips",
        grid_spec=pltpu.PrefetchScalarGridSpec(num_scalar_prefetch=1, grid=(nq,), in_specs=in_specs, out_specs=out_specs),
        out_shape=[jax.ShapeDtypeStruct(s, F32) for s in shapes],
        input_output_aliases=aliases,
        compiler_params=_params(VMEM_BIG),
    )(pos, *args)


def _wconv_slot(chip, tap):
    idx = 3 * chip + tap
    return ROW_WCONV + idx // SUBLANES, slice((idx % SUBLANES) * LANES, (idx % SUBLANES + 1) * LANES)


def pack_small(pos, per_layer, loss_blk):
    L = len(per_layer)
    D = per_layer[0][0].shape[1]

    def body(pos_ref, *refs):
        o = refs[-1]
        lb = refs[-2]
        o[...] = jnp.zeros_like(o)
        for l in range(L):
            dgpre, dgpost, dsh, dsc, dgt, dps, dwc = refs[7 * l:7 * l + 7]
            base = SUBLANES * l
            o[pl.ds(base + ROW_G_PRE, 1), :] = dgpre[...]
            o[pl.ds(base + ROW_G_POST, 1), :] = dgpost[...]
            for r, src in enumerate((dsh, dsc, dgt)):
                o[pl.ds(base + ROW_MOD + r, 1), :] = src[...]
            o[pl.ds(base + ROW_PSCALE, 1), 0:dps.shape[1]] = dps[...]
            for j in range(dwc.shape[0]):
                for k in range(3):
                    row, lanes = _wconv_slot(j, k)
                    o[pl.ds(base + row, 1), lanes] = dwc[j, pl.ds(k, 1), :]
        o[pl.ds(ROW_PSCALE, 1), LOSS_LANES] = lb[pl.ds(0, 1), :]

    flat = [a for layer in per_layer for a in layer] + [loss_blk]

    def whole(a):
        return pl.BlockSpec(a.shape, lambda i, p: (0,) * a.ndim)

    return pl.pallas_call(
        body, name="pack_small",
        grid_spec=pltpu.PrefetchScalarGridSpec(
            num_scalar_prefetch=1, grid=(1,), in_specs=[whole(a) for a in flat],
            out_specs=pl.BlockSpec((None, L * SUBLANES, D), lambda i, p: (p[2], 0, 0))),
        out_shape=jax.ShapeDtypeStruct((N_DEV, L * SUBLANES, D), F32),
        compiler_params=_params(),
    )(pos, *flat)


def small_update(pos, packs, params, moments_m, moments_v):
    n = len(params)
    L, D = params[1].shape
    PS = params[3].shape[1]

    def body(pos_ref, p_ref, *refs):
        ws, ms, vs = refs[0:n], refs[n:2 * n], refs[2 * n:3 * n]
        loss_ref = refs[3 * n]
        outs = [refs[3 * n + 1 + 4 * t:3 * n + 5 + 4 * t] for t in range(n)]
        summed = refs[-1]
        s = p_ref[0]
        for d in range(1, N_DEV):
            s = s + p_ref[d]
        summed[...] = s
        loss_ref[...] = summed[pl.ds(ROW_PSCALE, 1), LOSS_LANES]
        chip = pos_ref[1]

        def update(t, idx, g):
            d, mm, vv = _adamw_math(ws[t][idx], g, ms[t][idx], vs[t][idx])
            g_ref, d_ref, mo_ref, vo_ref = outs[t]
            g_ref[idx] = g
            d_ref[idx] = d
            mo_ref[idx] = mm
            vo_ref[idx] = vv

        for l in range(L):
            base = SUBLANES * l
            row = pl.ds(l, 1)
            for k in range(3):
                update(0, (row, slice(k * D, (k + 1) * D)), summed[pl.ds(base + ROW_MOD + k, 1), :])
            update(1, (row, slice(None)), summed[pl.ds(base + ROW_G_PRE, 1), :])
            update(2, (row, slice(None)), summed[pl.ds(base + ROW_G_POST, 1), :])
            update(3, (row, slice(None)), summed[pl.ds(base + ROW_PSCALE, 1), 0:PS])
            for k in range(3):
                g = None
                for j in range(N_CHIPS):
                    wrow, lanes = _wconv_slot(j, k)
                    cand = summed[pl.ds(base + wrow, 1), lanes]
                    g = cand if g is None else jnp.where(chip == j, cand, g)
                update(4, (l, pl.ds(k, 1), slice(None)), g)

    def whole(a):
        return pl.BlockSpec(a.shape, lambda i, p: (0,) * a.ndim)

    ins = [packs] + list(params) + list(moments_m) + list(moments_v)
    out_shape = [jax.ShapeDtypeStruct((1, LANES), F32)]
    for w in params:
        out_shape += [jax.ShapeDtypeStruct(w.shape, F32)] * 4
    outs = pl.pallas_call(
        body, name="small_update",
        grid_spec=pltpu.PrefetchScalarGridSpec(
            num_scalar_prefetch=1, grid=(1,), in_specs=[whole(a) for a in ins],
            out_specs=[whole(a) for a in out_shape],
            scratch_shapes=[pltpu.VMEM(packs.shape[1:], F32)]),
        out_shape=out_shape,
        compiler_params=_params(),
    )(pos, *ins)
    return outs[0], [outs[1 + 4 * t:5 + 4 * t] for t in range(n)]


def _adamw_math(w, g, m, v):
    m = ADAM_B1 * m + (1.0 - ADAM_B1) * g
    v = ADAM_B2 * v + (1.0 - ADAM_B2) * (g * g)
    m_hat = m / (1.0 - ADAM_B1 ** ADAM_STEP)
    v_hat = v / (1.0 - ADAM_B2 ** ADAM_STEP)
    delta = -ADAM_LR * (m_hat / (jnp.sqrt(v_hat) + ADAM_EPS) + ADAM_WD * w)
    return delta, m, v


def _adamw_block(ins, outs):
    w_ref, g_ref, m_ref, v_ref = ins
    go_ref, d_ref, mo_ref, vo_ref = outs
    gv = g_ref[...]
    d, mm, vv = _adamw_math(w_ref[...], gv, m_ref[...], v_ref[...])
    go_ref[...] = gv
    d_ref[...] = d
    mo_ref[...] = mm
    vo_ref[...] = vv


def adamw(groups, name, first, count, steps, acc=None):
    n = len(groups)

    def body(*refs):
        outs = refs[len(refs) - 4 * n:]
        for k in range(n):
            _adamw_block(refs[4 * k:4 * k + 4], outs[4 * k:4 * k + 4])

    specs, out_shape, args = [], [], []
    for group in groups:
        shape = group[0].shape
        spec = pl.BlockSpec((1, shape[1] // steps) + shape[2:],
                            lambda i, s, rest=(0,) * (len(shape) - 2): (first + i, s) + rest)
        specs += [spec] * 4
        out_shape += [jax.ShapeDtypeStruct(shape, F32)] * 4
        args += list(group)
    extra = [] if acc is None else list(acc)
    return pl.pallas_call(
        body, name=name, grid=(count, steps),
        in_specs=specs + [ANY] * len(extra), out_specs=specs, out_shape=out_shape,
        input_output_aliases={4 * n + a: a for a in range(len(extra))},
        compiler_params=_params(VMEM_BIG, n_grid=2),
    )(*args, *extra)


def ada_finish(c_all, dmod, w, m, v):
    L, D, CW = w.shape
    hD = D // 2

    def body(c_ref, d_ref, w_ref, m_ref, v_ref, g_ref, dl_ref, mo_ref, vo_ref):
        cv = c_ref[...]
        z = jnp.zeros_like(cv)
        ca = jnp.concatenate([cv * jax.nn.sigmoid(cv), z], axis=0).astype(BF16)
        dm = jnp.concatenate([d_ref[0], jnp.zeros_like(d_ref[0])], axis=0).astype(BF16)
        g = lax.dot_general(ca, dm, TN, preferred_element_type=F32)
        g_ref[0] = g
        d, mm, vv = _adamw_math(w_ref[0], g, m_ref[0], v_ref[0])
        dl_ref[0] = d
        mo_ref[0] = mm
        vo_ref[0] = vv

    big = pl.BlockSpec((1, hD, CW), lambda l, h: (l, h, 0))
    shape = jax.ShapeDtypeStruct(w.shape, F32)
    return pl.pallas_call(
        body, name="ada_finish", grid=(L, 2),
        in_specs=[pl.BlockSpec((N_DEV, hD), lambda l, h: (0, h)), pl.BlockSpec((1, N_DEV, CW), lambda l, h: (l, 0, 0)),
                  big, big, big],
        out_specs=[big] * 4, out_shape=[shape] * 4,
        compiler_params=_params(VMEM_BIG, n_grid=2),
    )(c_all, dmod, w, m, v)


def kernel(x, c, w_ada, b_ada, g_pre, w_in, w_conv, w_pool, pool_scale, w_out, g_post, loss_target, m_w_ada, m_b_ada, m_g_pre, m_w_in, m_w_conv, m_w_pool, m_pool_scale, m_w_out, m_g_post, v_w_ada, v_b_ada, v_g_pre, v_w_in, v_w_conv, v_w_pool, v_pool_scale, v_w_out, v_g_post):
    L, D, CW = w_in.shape
    RO = w_out.shape[1]
    T = x.shape[1]
    ix, iy, ic = _me()
    chip = 2 * ix + iy
    me_lin = 4 * ix + 2 * iy + ic

    pos = jnp.stack([ic, chip, me_lin]).astype(jnp.int32)
    n_s, n_c = 3, 9

    def gather(bufs, after):
        ss, rs, bufs, tok = xchg_start("gather_start", bufs, 3 * len(bufs), plan_gather, after=after)
        return (ss, rs, bufs), tok

    def ready(flight, after):
        fss, frs, bufs = flight
        return xchg_wait("forward_wait", bufs, fss, frs, 3 * len(bufs), plan_forward, after)

    def arrive_part(flight, which, after, base=0):
        ss, rs, bufs = flight
        sems = tuple(range(base + 3 * which, base + 3 * which + 3))
        (buf,) = xchg_wait("gather_wait", [bufs[which]], ss, rs, 3, plan_gather, after, sems=sems)
        fss, frs, (buf,), tok = xchg_start("forward_start", [buf], 3, plan_forward, sibling_only=True)
        return (fss, frs, [buf]), tok

    n_small = N_DEV - 1 + N_CHIPS - 1
    w_in_of, w_out_of = [None] * L, [None] * L
    gi0, go0 = cast_weights(pos, w_in, w_out, 0, pos)

    def plan_first(refs, x, y, c):
        return plan_small(refs[0:2], x, y, c) + plan_gather(refs[2:3], x, y, c)

    placed = list(place_small(pos, c.reshape(SUBLANES, LANES), w_conv))
    s_ss, s_rs, firsts, token = xchg_start("first_start", placed + [gi0], n_small + 3, plan_first)
    smalls_in = firsts[0:2]
    w_in_of[0] = ((s_ss, s_rs, firsts[2:3]), 0)
    g_pre_l, g_post_l, pscale_l, b_ada_l, m_w_conv_l, v_w_conv_l, token = lax.optimization_barrier(
        (g_pre, g_post, pool_scale, b_ada, m_w_conv, v_w_conv, token))
    g_pre3, g_post3 = g_pre_l.reshape(L, 1, D), g_post_l.reshape(L, 1, D)
    pscale3 = pscale_l.reshape(L, 1, pool_scale.shape[1])
    c_all3, wconv_all = xchg_wait("small_wait", smalls_in, s_ss, s_rs, n_small, plan_small, [token])
    c_all = c_all3.reshape(N_DEV, D)
    b_my = lax.dynamic_slice_in_dim(b_ada_l, chip * CW, CW, axis=1)
    m_ss, m_rs, mods, token = xchg_start("mod_start", [mod_part(pos, c_all, w_ada, b_my, token)], 3, plan_mod)
    gi1, go1 = cast_weights(pos, w_in, w_out, 1, token)
    flight, token = gather([go0, gi1, go1], [])
    w_out_of[0], w_in_of[1], w_out_of[1] = (flight, 0), (flight, 1), (flight, 2)
    late = []
    for l in range(2, L):
        late += list(cast_weights(pos, w_in, w_out, l, token))
    flight, token = gather(late, [])
    for l in range(2, L):
        w_in_of[l], w_out_of[l] = (flight, 2 * (l - 2)), (flight, 2 * (l - 2) + 1)
    fwd_in, token = arrive_part(*w_in_of[0], [token], base=n_small)
    (mod_all,) = xchg_wait("mod_wait", mods, m_ss, m_rs, 3, plan_mod, [token])
    mod = lax.dynamic_index_in_dim(mod_all, me_lin, axis=2, keepdims=False)
    mod4 = jnp.transpose(mod, (1, 0, 2)).reshape(L, 3, 1, D)

    xs, projs, yas, yps, ys, pooleds = [x.reshape(T, D)], [], [], [], [], []
    wg_in, wg_out = [], []
    for l in range(L):
        (gi,) = ready(fwd_in, [mod4 if l == 0 else xs[l]])
        proj = proj_fwd(xs[l], mod4, g_pre3, gi, l)
        ya, yp, pooled = mix_fwd(proj, wconv_all, w_pool, pscale3, l)
        pooleds.append(pooled)
        fwd_out, token = arrive_part(*w_out_of[l], [ya, yp])
        after = [token]
        if 0 < l < L - 1:
            fwd_in, token = arrive_part(*w_in_of[l + 1], after)
            after = [token]
        (go,) = ready(fwd_out, after)
        wg_in.append(gi)
        wg_out.append(go.reshape(N_CHIPS * RO, D))
        projs.append(proj)
        yas.append(ya)
        yps.append(yp)
        if l + 1 < L:
            xn, yv = out_fwd(ya, yp, wg_out[l], xs[l], mod4, g_post3, l, after[0])
            xs.append(xn)
            if l == 0:
                fwd_in, token = arrive_part(*w_in_of[1], [xn])
        else:
            dx, yv, loss_blk = out_fwd_loss(ya, yp, wg_out[l], xs[l], mod4, g_post3, l, loss_target.reshape(T, D))
        ys.append(yv)

    shapes = (w_in.shape, w_out.shape, w_pool.shape)
    smalls = [None] * L
    acc, flying, sib, token = None, None, None, loss_blk

    def to_chips(sib, after):
        sl, s_ss, s_rs, s_bufs = sib
        s_bufs = xchg_wait("sibling_wait", s_bufs, s_ss, s_rs, n_s, plan_sibling, after)
        chip_parts = add_sibling(pos, s_bufs[0:3], s_bufs[3:6])
        lands = [lax.empty((3,) + a.shape[1:], a.dtype) for a in chip_parts]
        c_ss, c_rs, c_bufs, ctoken = xchg_start("chip_start", list(chip_parts) + lands, n_c, plan_chip)
        return (sl, c_ss, c_rs, c_bufs), ctoken

    def landed(flying, acc, after):
        fl, f_ss, f_rs, f_bufs = flying
        f_bufs = xchg_wait("chip_wait", f_bufs, f_ss, f_rs, n_c, plan_chip, after)
        return sum_chips(pos, f_bufs[0:3], f_bufs[3:6], acc, fl, shapes)

    early = None
    for l in reversed(range(L)):
        dya, dyp, dwo_l, dgate, dgpost = out_bwd(dx, ys[l], yas[l], yps[l], wg_out[l], mod4, g_post3, l, token)
        token = dya
        spreading = None
        if sib is not None:
            arrived = flying
            flying, token = to_chips(sib, [dya])
            if arrived is not None:
                acc = landed(arrived, acc, [token])
                spreading = plan_spread((arrived[0],), ())
                sp_ss, sp_rs, acc, token = xchg_start("spread_start", list(acc), 2, spreading, sibling_only=True)
        du_a, db_a, dc_a, dg_a, du_p, dg_p, dwc, dwp_l, dps = mix_bwd(projs[l], pooleds[l], dya, dyp, wconv_all, w_pool,
                                                                        pscale3, l, token)
        update = None
        if spreading is not None:
            acc = xchg_wait("spread_wait", acc, sp_ss, sp_rs, 2, spreading, [du_a])
            update = (arrived[0], [w_in, acc[0], m_w_in, v_w_in], [w_out, acc[1], m_w_out, v_w_out], early)
        dx, dwi_l, dshift, dscale, dgpre, *rest = in_bwd([du_a, db_a, dc_a, dg_a, du_p, dg_p], wg_in[l], xs[l], dx,
                                                         mod4, g_pre3, l, update)
        early = rest if rest else early
        smalls[l] = (dgpre, dgpost, dshift, dscale, dgate, dps, dwc)
        parts = [dwi_l, dwo_l.reshape(N_CHIPS, RO, D), dwp_l]
        s_lands = [lax.empty((a.shape[0], a.shape[1] // 2) + a.shape[2:], a.dtype) for a in parts]
        s_ss, s_rs, s_bufs, token = xchg_start("sibling_start", parts + s_lands, n_s, plan_sibling, sibling_only=True)
        sib = (l, s_ss, s_rs, s_bufs)
    grad_x = dx.reshape(1, T, D)

    acc = landed(flying, acc, [token])
    n_p, n_sp = N_DEV - 1, 2 + (N_DEV - 1) * (L - 1)
    spread = plan_spread((1,), tuple(range(1, L)))

    def plan_tail(refs, x, y, c):
        return plan_pack(refs[0:1], x, y, c) + spread(refs[1:4], x, y, c)

    sp_ss, sp_rs, both, sp_token = xchg_start("spread_start", [pack_small(pos, smalls, loss_blk)] + list(acc),
                                              n_p + n_sp, plan_tail)
    packs, acc = both[0:1], both[1:4]
    flying, token = to_chips(sib, [sp_token])
    (packs_all,) = xchg_wait("pack_wait", packs, sp_ss, sp_rs, n_p, plan_pack, [token])
    dmod_all = packs_all.reshape(N_DEV, L, SUBLANES, D)[:, :, ROW_MOD:ROW_MOD + 3].reshape(N_DEV, L, 3 * D)
    dmod_my = jnp.transpose(lax.dynamic_slice_in_dim(dmod_all, chip * CW, CW, axis=2), (1, 0, 2))

    g_w_ada, d_w_ada, nm_w_ada, nv_w_ada = ada_finish(c_all, dmod_my, w_ada, m_w_ada, v_w_ada)
    packs_late, _ = lax.optimization_barrier((packs_all, nv_w_ada))
    loss_row, upd = small_update(pos, packs_late, [b_ada, g_pre, g_post, pool_scale, w_conv],
                                 [m_b_ada, m_g_pre, m_g_post, m_pool_scale, m_w_conv_l],
                                 [v_b_ada, v_g_pre, v_g_post, v_pool_scale, v_w_conv_l])
    loss = loss_row[0, 0]
    (g_b_ada, d_b_ada, nm_b_ada, nv_b_ada), (g_g_pre, d_g_pre, nm_g_pre, nv_g_pre) = upd[0], upd[1]
    (g_g_post, d_g_post, nm_g_post, nv_g_post), (g_pscale, d_pscale, nm_pscale, nv_pscale) = upd[2], upd[3]
    g_w_conv, d_w_conv, nm_w_conv, nv_w_conv = upd[4]

    done = [nv_w_ada, nv_w_conv]
    g_w_in, g_w_out, g_w_pool = xchg_wait("spread_wait", acc, sp_ss, sp_rs, n_sp, spread, done,
                                          sems=range(n_p, n_p + n_sp))
    early = adamw([[w_in, g_w_in, m_w_in, v_w_in], [w_out, g_w_out, m_w_out, v_w_out]], "adamw_layer", 1, 1, 2, early)

    acc = landed(flying, (g_w_in, g_w_out, g_w_pool), [early[3], early[7]])
    last = plan_spread((0,), (0,))
    n_last = 2 + N_DEV - 1
    l_ss, l_rs, acc, _ = xchg_start("spread_start", list(acc), n_last, last)
    upd_pool = adamw([[w_pool, acc[2], m_w_pool, v_w_pool]], "adamw_w_pool", 1, L - 1, 1)
    r_w_in, r_w_out, r_w_pool = xchg_wait("spread_wait", acc, l_ss, l_rs, n_last, last, [upd_pool[3]])
    (g_w_in, d_w_in, nm_w_in, nv_w_in, g_w_out, d_w_out, nm_w_out, nv_w_out,
     g_w_pool, d_w_pool, nm_w_pool, nv_w_pool) = adamw(
         [[w_in, r_w_in, m_w_in, v_w_in], [w_out, r_w_out, m_w_out, v_w_out], [w_pool, r_w_pool, m_w_pool, v_w_pool]],
         "adamw_layer", 0, 1, 2, list(early) + list(upd_pool))

    return (loss, grad_x,
            g_w_ada, g_b_ada, g_g_pre, g_w_in, g_w_conv, g_w_pool, g_pscale, g_w_out, g_g_post,
            d_w_ada, d_b_ada, d_g_pre, d_w_in, d_w_conv, d_w_pool, d_pscale, d_w_out, d_g_post,
            nm_w_ada, nm_b_ada, nm_g_pre, nm_w_in, nm_w_conv, nm_w_pool, nm_pscale, nm_w_out, nm_g_post,
            nv_w_ada, nv_b_ada, nv_g_pre, nv_w_in, nv_w_conv, nv_w_pool, nv_pscale, nv_w_out, nv_g_post)
```

```python
import functools

import jax
import jax.numpy as jnp
from jax import lax
from jax.experimental import pallas as pl
from jax.experimental.pallas import tpu as pltpu

F32 = jnp.float32
BF16 = jnp.bfloat16
MESH = pl.DeviceIdType.MESH
ANY = pl.BlockSpec(memory_space=pl.ANY)

NORM_EPS = 1e-6
POOL_WINDOWS = (2, 4, 8, 16)
ADAM_LR = 0.001
ADAM_B1 = 0.9
ADAM_B2 = 0.999
ADAM_EPS = 1e-08
ADAM_WD = 0.01
ADAM_STEP = 10

N_CHIPS = 4
N_DEV = 8
LANES = 128
SUBLANES = 8
VMEM_BIG = 56 * 1024 * 1024
HIST = 16
R_CONV = 64
R_POOL = 128

ROW_G_PRE, ROW_G_POST, ROW_MOD, ROW_PSCALE, ROW_WCONV = 0, 1, 2, 5, 6
LOSS_LANES = slice(4 * LANES, 5 * LANES)

NT = (((1,), (1,)), ((), ()))
TN = (((0,), (0,)), ((), ()))


def _params(vmem=None, n_grid=1):
    kw = {}
    if n_grid:
        kw["dimension_semantics"] = ("arbitrary",) * n_grid
    if vmem is not None:
        kw["vmem_limit_bytes"] = vmem
    return pltpu.CompilerParams(**kw)


def _colsum8(v):
    n, d = v.shape
    return v.reshape(n // SUBLANES, SUBLANES, d).sum(axis=0)


def _rms(v):
    return lax.rsqrt(jnp.mean(v * v, axis=-1, keepdims=True) + NORM_EPS)


def _sigmoid(v):
    return 0.5 * jnp.tanh(0.5 * v) + 0.5


def _shift_down(ext, k, rows):
    if k == 0:
        return ext[HIST:HIST + rows]
    return pltpu.roll(ext, k, 0)[HIST:HIST + rows]


def _shift_up(ext, k, rows):
    if k == 0:
        return ext[0:rows]
    return pltpu.roll(ext, ext.shape[0] - k, 0)[0:rows]


def _load_ext(ref, r0, h0, first, rows):
    hist = ref[pl.ds(h0, HIST), :].astype(F32)
    hist = jnp.where(first, 0.0, hist)
    cur = ref[pl.ds(r0, rows), :].astype(F32)
    return jnp.concatenate([hist, cur], axis=0)


def _me():
    return lax.axis_index("x"), lax.axis_index("y"), lax.axis_index("c")


def cast_weights(pos, w_in, w_out, l, after):
    _, D, CW = w_in.shape
    RO = w_out.shape[1]

    def body(pos_ref, wi, wo, after_ref, oi, oo):
        oi[...] = wi[...].astype(BF16)
        oo[...] = wo[...].astype(BF16)

    return pl.pallas_call(
        body, name="cast_w",
        grid_spec=pltpu.PrefetchScalarGridSpec(
            num_scalar_prefetch=1, grid=(2,),
            in_specs=[pl.BlockSpec((None, D // 2, CW), lambda h, p: (l, h, 0)),
                      pl.BlockSpec((None, RO // 2, D), lambda h, p: (l, h, 0)), ANY],
            out_specs=[pl.BlockSpec((D // 2, CW), lambda h, p: (h, p[1])),
                       pl.BlockSpec((None, RO // 2, D), lambda h, p: (p[1], h, 0))]),
        out_shape=[jax.ShapeDtypeStruct((D, N_CHIPS * CW), BF16), jax.ShapeDtypeStruct((N_CHIPS, RO, D), BF16)],
        compiler_params=_params(),
    )(pos, w_in, w_out, after)


def mod_part(pos, c_all, w_ada, b_my, after):
    L, D, CW = w_ada.shape

    def body(pos_ref, c_ref, w_ref, b_ref, after_ref, o_ref):
        cv = c_ref[...]
        ca = (cv * jax.nn.sigmoid(cv)).astype(BF16)
        o_ref[...] = jnp.dot(ca, w_ref[0].astype(BF16), preferred_element_type=F32) + b_ref[0]

    return pl.pallas_call(
        body, name="mod_part",
        grid_spec=pltpu.PrefetchScalarGridSpec(
            num_scalar_prefetch=1, grid=(L,),
            in_specs=[pl.BlockSpec((N_DEV, D), lambda l, p: (0, 0)),
                      pl.BlockSpec((1, D, CW), lambda l, p: (l, 0, 0)),
                      pl.BlockSpec((1, 1, CW), lambda l, p: (l, 0, 0)), ANY],
            out_specs=pl.BlockSpec((None, None, N_DEV, CW), lambda l, p: (p[1], l, 0, 0))),
        out_shape=jax.ShapeDtypeStruct((N_CHIPS, L, N_DEV, CW), F32),
        compiler_params=_params(VMEM_BIG),
    )(pos, c_all, w_ada, b_my.reshape(L, 1, CW), after)


def _mod_row(l, k, D):
    return pl.BlockSpec((None, None, 1, D), lambda *_: (l, k, 0, 0))


def _layer_row(l, D):
    return pl.BlockSpec((None, 1, D), lambda *_: (l, 0, 0))


def proj_fwd(x, mod4, g_pre3, wg, l):
    T, D = x.shape
    NC = wg.shape[1]
    NB = N_CHIPS
    CW = NC // NB
    tm = 512

    def body(x_ref, sh_ref, sc_ref, g_ref, w_ref, o_ref):
        xv = x_ref[...]
        h = (xv * _rms(xv)) * (g_ref[...] * (1.0 + sc_ref[...])) + sh_ref[...]
        hb = h.astype(BF16)
        for j in range(NB):
            cols = slice(j * CW, (j + 1) * CW)
            o_ref[:, cols] = jnp.dot(hb, w_ref[:, cols], preferred_element_type=F32).astype(BF16)

    return pl.pallas_call(
        body, name="proj_fwd", grid=(T // tm,),
        in_specs=[pl.BlockSpec((tm, D), lambda i: (i, 0)), _mod_row(l, 0, D), _mod_row(l, 1, D), _layer_row(l, D),
                  pl.BlockSpec((D, NC), lambda i: (0, 0))],
        out_specs=pl.BlockSpec((tm, NC), lambda i: (i, 0)),
        out_shape=jax.ShapeDtypeStruct((T, NC), BF16),
        compiler_params=_params(VMEM_BIG),
    )(x, mod4, mod4, g_pre3, wg)


N_MIX = 4


def _conv_fwd_block(u_ref, b_ref, c_ref, g_ref, w_ref, o_ref):
    T = u_ref.shape[0]
    R = 2 * R_CONV
    w0 = w_ref[pl.ds(0, 1), :]
    w1 = w_ref[pl.ds(1, 1), :]
    w2 = w_ref[pl.ds(2, 1), :]

    def chunk(i, carry):
        r0 = pl.multiple_of(i * R, R)
        h0 = pl.multiple_of(jnp.maximum(r0 - HIST, 0), HIST)
        first = i == 0
        ca = _load_ext(c_ref, r0, h0, first, R) * _load_ext(u_ref, r0, h0, first, R)
        conv = w2 * ca[HIST:] + w1 * _shift_down(ca, 1, R) + w0 * _shift_down(ca, 2, R)
        g = g_ref[pl.ds(r0, R), :].astype(F32)
        b = b_ref[pl.ds(r0, R), :].astype(F32)
        o_ref[pl.ds(r0, R), :] = (b * conv * (g * _sigmoid(g))).astype(BF16)
        return carry

    lax.fori_loop(0, T // R, chunk, 0)


def _conv_idx(j):
    return jnp.minimum(j, N_MIX - 1)


def _pool_idx(j):
    return jnp.maximum(j - N_MIX, 0)


def _proj_col(T, off, idx):
    return pl.BlockSpec((T, LANES), lambda j: (0, idx(j) + off))


def _causal_window_sum(ext, w):
    s, k = ext, 1
    while k < w:
        s = s + pltpu.roll(s, k, 0)
        k *= 2
    return s


def _anticausal_window_sum(ext, w):
    s, k = ext, 1
    n = ext.shape[0]
    while k < w:
        s = s + pltpu.roll(s, n - k, 0)
        k *= 2
    return s


def _count(r0, rows, w):
    t = r0 + lax.broadcasted_iota(jnp.int32, (rows, LANES), 0)
    return jnp.minimum(t + 1, w).astype(F32)


def _pooled_loop(p_ref, pooled_s, w, T):
    R = R_POOL

    def chunk(i, carry):
        r0 = pl.multiple_of(i * R, R)
        h0 = pl.multiple_of(jnp.maximum(r0 - HIST, 0), HIST)
        ext = _load_ext(p_ref, r0, h0, i == 0, R)
        ws = _causal_window_sum(ext, w)[HIST:]
        pooled_s[pl.ds(r0, R), :] = (ws / _count(r0, R, w) - ext[HIST:]).astype(BF16)
        return carry

    lax.fori_loop(0, T // R, chunk, 0)


def _conv_w_spec(l):
    return pl.BlockSpec((None, None, 3, LANES), lambda j: (_conv_idx(j), l, 0, 0))


def _pool_w_spec(l):
    return pl.BlockSpec((None, None, LANES, LANES), lambda j: (l, _pool_idx(j), 0, 0))


def _pool_s_spec(l):
    return pl.BlockSpec((None, 1, LANES), lambda j: (l, 0, _pool_idx(j)))


def _pool_fwd_group(p_ref, g_ref, w_ref, s_ref, o_ref, pooled_s, mixed_s, w):
    T = p_ref.shape[0]
    R = R_POOL
    _pooled_loop(p_ref, pooled_s, w, T)
    mixed_s[...] = jnp.dot(pooled_s[...], w_ref[...].astype(BF16), preferred_element_type=F32)
    sc = s_ref[...]

    def chunk(i, carry):
        r0 = pl.multiple_of(i * R, R)
        g = g_ref[pl.ds(r0, R), :].astype(F32)
        o_ref[pl.ds(r0, R), :] = (mixed_s[pl.ds(r0, R), :] * sc * (g * _sigmoid(g))).astype(BF16)
        return carry

    lax.fori_loop(0, T // R, chunk, 0)


def mix_fwd(proj, wconv, wpool, pscale3, l):
    T = proj.shape[0]

    def body(u_ref, b_ref, c_ref, g_ref, p_ref, gp_ref, wc_ref, wp_ref, s_ref, ya_ref, yp_ref, pooled_ref, mixed_s):
        j = pl.program_id(0)
        pl.when(j < N_MIX)(functools.partial(_conv_fwd_block, u_ref, b_ref, c_ref, g_ref, wc_ref, ya_ref))
        for k, w in enumerate(POOL_WINDOWS):
            pl.when(j == N_MIX + k)(functools.partial(_pool_fwd_group, p_ref, gp_ref, wp_ref, s_ref, yp_ref,
                                                      pooled_ref, mixed_s, w))

    half = jax.ShapeDtypeStruct((T, N_MIX * LANES), BF16)
    pool_col = pl.BlockSpec((T, LANES), lambda j: (0, _pool_idx(j)))
    return pl.pallas_call(
        body, name="mix_fwd", grid=(2 * N_MIX,),
        in_specs=[_proj_col(T, 0, _conv_idx), _proj_col(T, 4, _conv_idx), _proj_col(T, 8, _conv_idx),
                  _proj_col(T, 12, _conv_idx), _proj_col(T, 16, _pool_idx), _proj_col(T, 20, _pool_idx),
                  _conv_w_spec(l), _pool_w_spec(l), _pool_s_spec(l)],
        out_specs=[pl.BlockSpec((T, LANES), lambda j: (0, _conv_idx(j))), pool_col, pool_col],
        out_shape=[half, half, half],
        scratch_shapes=[pltpu.VMEM((T, LANES), F32)],
        compiler_params=_params(),
    )(proj, proj, proj, proj, proj, proj, wconv, wpool, pscale3)


def out_fwd(ya, yp, wo, x, mod4, g_post3, l, after):
    T, D = x.shape
    H = ya.shape[1]
    tm = 512

    def body(ya_ref, yp_ref, wo_ref, x_ref, gt_ref, g_ref, after_ref, xn_ref, y_ref):
        y = (jnp.dot(ya_ref[...], wo_ref[0:H, :], preferred_element_type=F32)
             + jnp.dot(yp_ref[...], wo_ref[H:2 * H, :], preferred_element_type=F32))
        xn_ref[...] = x_ref[...] + gt_ref[...] * (y * _rms(y) * g_ref[...])
        y_ref[...] = y.astype(BF16)

    tile = pl.BlockSpec((tm, D), lambda i: (i, 0))
    half = pl.BlockSpec((tm, H), lambda i: (i, 0))
    return pl.pallas_call(
        body, name="out_fwd", grid=(T // tm,),
        in_specs=[half, half, pl.BlockSpec((2 * H, D), lambda i: (0, 0)), tile, _mod_row(l, 2, D), _layer_row(l, D),
                  ANY],
        out_specs=[tile, tile],
        out_shape=[jax.ShapeDtypeStruct((T, D), F32), jax.ShapeDtypeStruct((T, D), BF16)],
        compiler_params=_params(VMEM_BIG),
    )(ya, yp, wo, x, mod4, g_post3, after)


def out_fwd_loss(ya, yp, wo, x, mod4, g_post3, l, target):
    T, D = x.shape
    H = ya.shape[1]
    tm = 512
    nt = T // tm

    def body(ya_ref, yp_ref, wo_ref, x_ref, gt_ref, g_ref, t_ref, dx_ref, y_ref, l_ref, acc):
        i = pl.program_id(0)

        @pl.when(i == 0)
        def _():
            acc[...] = jnp.zeros_like(acc)

        y = (jnp.dot(ya_ref[...], wo_ref[0:H, :], preferred_element_type=F32)
             + jnp.dot(yp_ref[...], wo_ref[H:2 * H, :], preferred_element_type=F32))
        y_ref[...] = y.astype(BF16)
        d = (x_ref[...] + gt_ref[...] * (y * _rms(y) * g_ref[...])) - t_ref[...]
        dx_ref[...] = d * (1.0 / D)
        acc[...] += _colsum8(d * d)

        @pl.when(i == nt - 1)
        def _():
            l_ref[...] = jnp.zeros_like(l_ref) + jnp.sum(acc[...]) * (0.5 / D)

    tile = pl.BlockSpec((tm, D), lambda i: (i, 0))
    half = pl.BlockSpec((tm, H), lambda i: (i, 0))
    return pl.pallas_call(
        body, name="out_fwd_loss", grid=(nt,),
        in_specs=[half, half, pl.BlockSpec((2 * H, D), lambda i: (0, 0)), tile, _mod_row(l, 2, D), _layer_row(l, D),
                  tile],
        out_specs=[tile, tile, pl.BlockSpec((SUBLANES, LANES), lambda i: (0, 0))],
        out_shape=[jax.ShapeDtypeStruct((T, D), F32), jax.ShapeDtypeStruct((T, D), BF16),
                   jax.ShapeDtypeStruct((SUBLANES, LANES), F32)],
        scratch_shapes=[pltpu.VMEM((SUBLANES, D), F32)],
        compiler_params=_params(VMEM_BIG),
    )(ya, yp, wo, x, mod4, g_post3, target)


def out_bwd(dx, y, ya, yp, wo, mod4, g_post3, l, after):
    T, D = dx.shape
    H = ya.shape[1]
    tm = 512
    nt = T // tm

    def body(dx_ref, y_ref, ya_ref, yp_ref, wo_ref, gt_ref, g_ref, after_ref,
             dya_ref, dyp_ref, dwo_ref, dgt_ref, dg_ref, acc_w, acc_p):
        i = pl.program_id(0)

        @pl.when(i == 0)
        def _():
            acc_w[...] = jnp.zeros_like(acc_w)
            acc_p[...] = jnp.zeros_like(acc_p)

        yv = y_ref[...].astype(F32)
        dxv = dx_ref[...]
        gg = gt_ref[...] * g_ref[...]
        r = _rms(yv)
        yn = yv * r
        p = dxv * yn
        acc_p[...] += _colsum8(p)
        dy = r * (dxv * gg - yn * jnp.mean(p * gg, axis=-1, keepdims=True))
        dyb = dy.astype(BF16)
        dyc = lax.dot_general(dyb, wo_ref[...], NT, preferred_element_type=F32)
        dya_ref[...] = dyc[:, 0:H].astype(BF16)
        dyp_ref[...] = dyc[:, H:2 * H].astype(BF16)
        acc_w[0:H, :] += lax.dot_general(ya_ref[...], dyb, TN, preferred_element_type=F32)
        acc_w[H:2 * H, :] += lax.dot_general(yp_ref[...], dyb, TN, preferred_element_type=F32)

        @pl.when(i == nt - 1)
        def _():
            dwo_ref[...] = acc_w[...].astype(BF16)
            sp = jnp.sum(acc_p[...], axis=0, keepdims=True)
            dgt_ref[...] = g_ref[...] * sp
            dg_ref[...] = gt_ref[...] * sp

    row = pl.BlockSpec((1, D), lambda i: (0, 0))
    tile = pl.BlockSpec((tm, D), lambda i: (i, 0))
    half = pl.BlockSpec((tm, H), lambda i: (i, 0))
    full = pl.BlockSpec((2 * H, D), lambda i: (0, 0))
    return pl.pallas_call(
        body, name="out_bwd", grid=(nt,),
        in_specs=[tile, tile, half, half, full, _mod_row(l, 2, D), _layer_row(l, D), ANY],
        out_specs=[half, half, full, row, row],
        out_shape=[jax.ShapeDtypeStruct((T, H), BF16), jax.ShapeDtypeStruct((T, H), BF16),
                   jax.ShapeDtypeStruct((2 * H, D), BF16),
                   jax.ShapeDtypeStruct((1, D), F32), jax.ShapeDtypeStruct((1, D), F32)],
        scratch_shapes=[pltpu.VMEM((2 * H, D), F32), pltpu.VMEM((SUBLANES, D), F32)],
        compiler_params=_params(VMEM_BIG),
    )(dx, y, ya, yp, wo, mod4, g_post3, after)


def _conv_bwd_block(u_ref, b_ref, c_ref, g_ref, dy_ref, w_ref, du_ref, db_ref, dc_ref, dg_ref, dw_ref):
    T = u_ref.shape[0]
    R = R_CONV
    nchunk = T // R
    w0 = w_ref[pl.ds(0, 1), :]
    w1 = w_ref[pl.ds(1, 1), :]
    w2 = w_ref[pl.ds(2, 1), :]

    def chunk(k, carry):
        head, a0, a1, a2 = carry
        i = nchunk - 1 - k
        r0 = pl.multiple_of(i * R, R)
        h0 = pl.multiple_of(jnp.maximum(r0 - HIST, 0), HIST)
        first = i == 0
        ue = _load_ext(u_ref, r0, h0, first, R)
        ce = _load_ext(c_ref, r0, h0, first, R)
        ca = ce * ue
        ca0 = ca[HIST:]
        ca1 = _shift_down(ca, 1, R)
        ca2 = _shift_down(ca, 2, R)
        conv = w2 * ca0 + w1 * ca1 + w0 * ca2
        g = g_ref[pl.ds(r0, R), :].astype(F32)
        b = b_ref[pl.ds(r0, R), :].astype(F32)
        dy = dy_ref[pl.ds(r0, R), :].astype(F32)
        sg = _sigmoid(g)
        sl = g * sg
        t = dy * conv
        db_ref[pl.ds(r0, R), :] = (t * sl).astype(BF16)
        dg_ref[pl.ds(r0, R), :] = (t * b * (sg + sl * (1.0 - sg))).astype(BF16)
        dconv = dy * b * sl
        a2 = a2 + _colsum8(dconv * ca0)
        a1 = a1 + _colsum8(dconv * ca1)
        a0 = a0 + _colsum8(dconv * ca2)
        e = jnp.concatenate([dconv, head], axis=0)
        dca = w2 * dconv + w1 * _shift_up(e, 1, R) + w0 * _shift_up(e, 2, R)
        du_ref[pl.ds(r0, R), :] = (dca * ce[HIST:]).astype(BF16)
        dc_ref[pl.ds(r0, R), :] = (dca * ue[HIST:]).astype(BF16)
        return dconv[0:SUBLANES], a0, a1, a2

    z = jnp.zeros((SUBLANES, LANES), F32)
    _, a0, a1, a2 = lax.fori_loop(0, nchunk, chunk, (z, z, z, z))
    dw_ref[pl.ds(0, 1), :] = jnp.sum(a0, axis=0, keepdims=True)
    dw_ref[pl.ds(1, 1), :] = jnp.sum(a1, axis=0, keepdims=True)
    dw_ref[pl.ds(2, 1), :] = jnp.sum(a2, axis=0, keepdims=True)


def _pool_bwd_group(pooled_s, g_ref, dy_ref, w_ref, s_ref, du_ref, dg_ref, dw_ref, ds_ref,
                    mixed_s, dmix_s, dpool_s, w):
    T = pooled_s.shape[0]
    R = R_POOL
    nchunk = T // R
    wb = w_ref[...].astype(BF16)
    mixed_s[...] = jnp.dot(pooled_s[...], wb, preferred_element_type=F32)
    sc = s_ref[...]

    def gate_chunk(i, acc):
        r0 = pl.multiple_of(i * R, R)
        g = g_ref[pl.ds(r0, R), :].astype(F32)
        dy = dy_ref[pl.ds(r0, R), :].astype(F32)
        mixed = mixed_s[pl.ds(r0, R), :]
        sg = _sigmoid(g)
        sl = g * sg
        dg_ref[pl.ds(r0, R), :] = (dy * mixed * sc * (sg + sl * (1.0 - sg))).astype(BF16)
        dms = dy * sl
        dmix_s[pl.ds(r0, R), :] = (dms * sc).astype(BF16)
        return acc + _colsum8(dms * mixed)

    acc = lax.fori_loop(0, nchunk, gate_chunk, jnp.zeros((SUBLANES, LANES), F32))
    ds_ref[...] = jnp.sum(acc, axis=0, keepdims=True)
    dpool_s[pl.ds(0, T), :] = lax.dot_general(dmix_s[...], wb, NT, preferred_element_type=F32)
    dpool_s[pl.ds(T, HIST), :] = jnp.zeros((HIST, LANES), F32)
    dw_ref[...] = lax.dot_general(pooled_s[...], dmix_s[...], TN, preferred_element_type=F32).astype(BF16)

    def back_chunk(i, carry):
        r0 = pl.multiple_of(i * R, R)
        dpe = dpool_s[pl.ds(r0, R + HIST), :]
        e = dpe / _count(r0, R + HIST, w)
        du_ref[pl.ds(r0, R), :] = (_anticausal_window_sum(e, w)[0:R] - dpe[0:R]).astype(BF16)
        return carry

    lax.fori_loop(0, nchunk, back_chunk, 0)


def mix_bwd(proj, pooled, dya, dyp, wconv, wpool, pscale3, l, after):
    T = proj.shape[0]

    def body(u_ref, b_ref, c_ref, g_ref, pooled_ref, gp_ref, dya_ref, dyp_ref, wc_ref, wp_ref, s_ref, after_ref,
             dua_ref, dba_ref, dca_ref, dga_ref, dup_ref, dgp_ref, dwc_ref, dwp_ref, ds_ref,
             mixed_s, dmix_s, dpool_s):
        j = pl.program_id(0)
        pl.when(j < N_MIX)(functools.partial(_conv_bwd_block, u_ref, b_ref, c_ref, g_ref, dya_ref, wc_ref,
                                             dua_ref, dba_ref, dca_ref, dga_ref, dwc_ref))
        for k, w in enumerate(POOL_WINDOWS):
            pl.when(j == N_MIX + k)(functools.partial(_pool_bwd_group, pooled_ref, gp_ref, dyp_ref, wp_ref, s_ref,
                                                      dup_ref, dgp_ref, dwp_ref, ds_ref,
                                                      mixed_s, dmix_s, dpool_s, w))

    sec = jax.ShapeDtypeStruct((T, N_MIX * LANES), BF16)
    conv_col = pl.BlockSpec((T, LANES), lambda j: (0, _conv_idx(j)))
    pool_col = pl.BlockSpec((T, LANES), lambda j: (0, _pool_idx(j)))
    return pl.pallas_call(
        body, name="mix_bwd", grid=(2 * N_MIX,),
        in_specs=[_proj_col(T, 0, _conv_idx), _proj_col(T, 4, _conv_idx), _proj_col(T, 8, _conv_idx),
                  _proj_col(T, 12, _conv_idx), pool_col, _proj_col(T, 20, _pool_idx),
                  conv_col, pool_col, _conv_w_spec(l), _pool_w_spec(l), _pool_s_spec(l), ANY],
        out_specs=[conv_col, conv_col, conv_col, conv_col, pool_col, pool_col,
                   pl.BlockSpec((None, 3, LANES), lambda j: (_conv_idx(j), 0, 0)),
                   pl.BlockSpec((None, LANES, LANES), lambda j: (_pool_idx(j), 0, 0)),
                   pl.BlockSpec((1, LANES), lambda j: (0, _pool_idx(j)))],
        out_shape=[sec] * 6 + [jax.ShapeDtypeStruct((N_MIX, 3, LANES), F32),
                               jax.ShapeDtypeStruct((N_MIX, LANES, LANES), BF16),
                               jax.ShapeDtypeStruct((1, N_MIX * LANES), F32)],
        scratch_shapes=[pltpu.VMEM((T, LANES), F32), pltpu.VMEM((T, LANES), BF16), pltpu.VMEM((T + HIST, LANES), F32)],
        compiler_params=_params(),
    )(proj, proj, proj, proj, pooled, proj, dya, dyp, wconv, wpool, pscale3, after)


def in_bwd(dsecs, wg, x, dxo, mod4, g_pre3, l, update=None):
    T, D = x.shape
    NB = N_CHIPS
    CW = wg.shape[1] // NB
    SW = dsecs[0].shape[1]
    nsec = len(dsecs)
    PW = 256
    assert SW % PW == 0 and CW % PW == 0
    tm = 256
    nt = T // tm
    n_in = nsec + 6
    n_upd = 0 if update is None else 8
    n_acc = 0 if update is None or update[3] is None else 8

    def body(*refs):
        d_refs = refs[0:nsec]
        w_ref, x_ref, dxo_ref, sh_ref, sc_ref, g_ref = refs[nsec:n_in]
        outs = refs[n_in + n_upd + n_acc:]
        dxi_ref, dw_ref, dsh_ref, dsc_ref, dg_ref = outs[0:5]
        acc_w, acc_sh, acc_q = outs[5 + n_upd:]
        i = pl.program_id(0)
        for k in range(0, n_upd, 4):
            _adamw_block(refs[n_in + k:n_in + k + 4], outs[5 + k:5 + k + 4])

        @pl.when(i == 0)
        def _():
            acc_w[...] = jnp.zeros_like(acc_w)
            acc_sh[...] = jnp.zeros_like(acc_sh)
            acc_q[...] = jnp.zeros_like(acc_q)

        xv = x_ref[...]
        r = _rms(xv)
        xh = xv * r
        sg = g_ref[...] * (1.0 + sc_ref[...])
        hb = (xh * sg + sh_ref[...]).astype(BF16)
        dh = lax.dot_general(d_refs[0][...], w_ref[:, 0:SW], NT, preferred_element_type=F32)
        for s in range(1, nsec):
            dh = dh + lax.dot_general(d_refs[s][...], w_ref[:, s * SW:(s + 1) * SW], NT, preferred_element_type=F32)
        for p in range(nsec * SW // PW):
            col = p * PW
            s, so = col // SW, col % SW
            j, jo = col // CW, col % CW
            acc_w[j, :, jo:jo + PW] += lax.dot_general(hb, d_refs[s][:, so:so + PW], TN, preferred_element_type=F32)
        q = dh * xh
        acc_sh[...] += _colsum8(dh)
        acc_q[...] += _colsum8(q)
        dxi_ref[...] = dxo_ref[...] + r * (dh * sg - xh * jnp.mean(q * sg, axis=-1, keepdims=True))

        @pl.when(i == nt - 1)
        def _():
            dw_ref[...] = acc_w[...].astype(BF16)
            sq = jnp.sum(acc_q[...], axis=0, keepdims=True)
            dsh_ref[...] = jnp.sum(acc_sh[...], axis=0, keepdims=True)
            dsc_ref[...] = g_ref[...] * sq
            dg_ref[...] = (1.0 + sc_ref[...]) * sq

    row = pl.BlockSpec((1, D), lambda i: (0, 0))
    tile = pl.BlockSpec((tm, D), lambda i: (i, 0))
    sect = pl.BlockSpec((tm, SW), lambda i: (i, 0))
    rowshape = jax.ShapeDtypeStruct((1, D), F32)
    in_specs = [sect] * nsec + [pl.BlockSpec((D, NB * CW), lambda i: (0, 0)), tile, tile,
                                _mod_row(l, 0, D), _mod_row(l, 1, D), _layer_row(l, D)]
    out_specs = [tile, pl.BlockSpec((NB, D, CW), lambda i: (0, 0, 0)), row, row, row]
    out_shape = [jax.ShapeDtypeStruct((T, D), F32), jax.ShapeDtypeStruct((NB, D, CW), BF16), rowshape, rowshape, rowshape]
    args = [*dsecs, wg, x, dxo, mod4, mod4, g_pre3]
    aliases = {}
    if update is not None:
        layer, of_w_in, of_w_out, acc = update
        for group in (of_w_in, of_w_out):
            _, rows, cols = group[0].shape
            spec = pl.BlockSpec((None, rows // nt, cols), lambda i: (layer, i, 0))
            in_specs += [spec] * 4
            out_specs += [spec] * 4
            out_shape += [jax.ShapeDtypeStruct(group[0].shape, F32)] * 4
            args += list(group)
        if acc is not None:
            aliases = {len(args) + a: 5 + a for a in range(n_acc)}
            in_specs += [ANY] * n_acc
            args += list(acc)
    return pl.pallas_call(
        body, name="in_bwd", grid=(nt,),
        in_specs=in_specs, out_specs=out_specs, out_shape=out_shape, input_output_aliases=aliases,
        scratch_shapes=[pltpu.VMEM((NB, D, CW), F32),
                        pltpu.VMEM((SUBLANES, D), F32), pltpu.VMEM((SUBLANES, D), F32)],
        compiler_params=_params(VMEM_BIG),
    )(*args)


def _rcopy(src, dst, ssem, rsem, dev):
    return pltpu.make_async_remote_copy(src_ref=src, dst_ref=dst, send_sem=ssem, recv_sem=rsem,
                                        device_id=dev, device_id_type=MESH)


def _peers7(x, y, c):
    out = []
    for m in range(1, N_DEV):
        bx, by, bc = (m >> 2) & 1, (m >> 1) & 1, m & 1
        out.append(((1 - x) if bx else x, (1 - y) if by else y, (1 - c) if bc else c))
    return out


HBM = pl.BlockSpec(memory_space=pltpu.HBM)
SEM = pl.BlockSpec(memory_space=pltpu.SEMAPHORE)
SPLIT = pltpu.CompilerParams(has_side_effects=pltpu.SideEffectType.DATAFLOW_SIDE_EFFECTING)


def _hbm(a):
    return pltpu.with_memory_space_constraint(a, pltpu.HBM)


def _chips(x, y):
    return [(1 - x, y), (x, 1 - y), (1 - x, 1 - y)]


SIBLING_BARRIER_ID = 0


def xchg_start(name, bufs, n_copies, plan, sibling_only=False, after=()):
    n = len(bufs)
    after = list(after)

    def body(*refs):
        ssem, rsem, token = refs[n + len(after)], refs[n + len(after) + 1], refs[-1]
        x, y, c = _me()
        if sibling_only:
            barrier = pltpu.get_barrier_semaphore()
            pl.semaphore_signal(barrier, inc=1, device_id=(x, y, 1 - c), device_id_type=MESH)
            pl.semaphore_wait(barrier, 1)
        copies = plan(refs[0:n], x, y, c)
        assert len(copies) == n_copies
        for k, (src, dst, peer, _) in enumerate(copies):
            _rcopy(src, dst, ssem.at[k], rsem.at[k], peer).start()
        token[...] = jnp.zeros_like(token)

    params = dict(has_side_effects=pltpu.SideEffectType.DATAFLOW_SIDE_EFFECTING)
    if sibling_only:
        params["collective_id"] = SIBLING_BARRIER_ID
    outs = pl.pallas_call(
        body, name=name,
        in_specs=[HBM] * n + [ANY] * len(after),
        out_specs=[SEM, SEM] + [HBM] * n + [pl.BlockSpec(memory_space=pltpu.VMEM)],
        out_shape=([pltpu.SemaphoreType.DMA((n_copies,))] * 2 + [pltpu.HBM(b.shape, b.dtype) for b in bufs]
                   + [jax.ShapeDtypeStruct((SUBLANES, LANES), F32)]),
        input_output_aliases={a: 2 + a for a in range(n)},
        compiler_params=pltpu.CompilerParams(**params),
    )(*[_hbm(b) for b in bufs], *after)
    return outs[0], outs[1], list(outs[2:2 + n]), outs[-1]


def xchg_wait(name, bufs, ssem, rsem, n_copies, plan, after, sems=None):
    n = len(bufs)
    after = list(after)
    sems = tuple(range(n_copies)) if sems is None else tuple(sems)
    assert len(sems) == n_copies

    def body(*refs):
        ssem_ref, rsem_ref = refs[n], refs[n + 1]
        copies = plan(refs[0:n], *_me())
        assert len(copies) == n_copies
        for k, (src, _, peer, land) in zip(sems, copies):
            cp = _rcopy(src, land, ssem_ref.at[k], rsem_ref.at[k], peer)
            cp.wait_send()
            cp.wait_recv()

    outs = pl.pallas_call(
        body, name=name,
        in_specs=[HBM] * n + [SEM, SEM] + [ANY] * len(after), out_specs=[HBM] * n,
        out_shape=[pltpu.HBM(b.shape, b.dtype) for b in bufs],
        input_output_aliases={a: a for a in range(n)},
        compiler_params=SPLIT,
    )(*bufs, ssem, rsem, *after)
    return list(outs)


def _shard_half(buf, chip, half):
    if len(buf.shape) == 2:
        h, w = buf.shape[0] // 2, buf.shape[1] // N_CHIPS
        return buf.at[pl.ds(half * h, h), pl.ds(chip * w, w)]
    h = buf.shape[1] // 2
    return buf.at[chip, pl.ds(half * h, h)]


def plan_gather(refs, x, y, c):
    out = []
    for buf in refs:
        own = _shard_half(buf, 2 * x + y, c)
        for (px, py) in _chips(x, y):
            out.append((own, own, (px, py, c), _shard_half(buf, 2 * px + py, c)))
    return out


def plan_forward(refs, x, y, c):
    out = []
    for (px, py) in _chips(x, y):
        for buf in refs:
            landed = _shard_half(buf, 2 * px + py, c)
            out.append((landed, landed, (x, y, 1 - c), _shard_half(buf, 2 * px + py, 1 - c)))
    return out


def plan_sibling(refs, x, y, c):
    n = len(refs) // 2
    out = []
    for a in range(n):
        h = refs[a].shape[1] // 2
        out.append((refs[a].at[:, pl.ds((1 - c) * h, h)], refs[n + a], (x, y, 1 - c), refs[n + a]))
    return out


def plan_chip(refs, x, y, c):
    n = len(refs) // 2
    out = []
    for j, (px, py) in enumerate(_chips(x, y)):
        for a in range(n):
            out.append((refs[a].at[2 * px + py], refs[n + a].at[j], (px, py, c), refs[n + a].at[j]))
    return out


def plan_mod(refs, x, y, c):
    (mods,) = refs
    mine = mods.at[2 * x + y]
    return [(mine, mine, (px, py, c), mods.at[2 * px + py]) for (px, py) in _chips(x, y)]


def plan_pack(refs, x, y, c):
    (packs,) = refs
    mine = packs.at[4 * x + 2 * y + c]
    return [(mine, mine, peer, packs.at[4 * peer[0] + 2 * peer[1] + peer[2]]) for peer in _peers7(x, y, c)]


def plan_spread(layers, wp_layers):
    def plan(refs, x, y, c):
        gi, go, gp = refs
        hD, hR, hP = gi.shape[1] // 2, go.shape[1] // 2, gp.shape[2] // 2
        sib = (x, y, 1 - c)
        out = []
        for l in layers:
            mine = gi.at[l, pl.ds(c * hD, hD)]
            out.append((mine, mine, sib, gi.at[l, pl.ds((1 - c) * hD, hD)]))
            mine = go.at[l, pl.ds(c * hR, hR)]
            out.append((mine, mine, sib, go.at[l, pl.ds((1 - c) * hR, hR)]))
        for l in wp_layers:
            mine = gp.at[l, 2 * x + y, pl.ds(c * hP, hP)]
            for peer in _peers7(x, y, c):
                out.append((mine, mine, peer, gp.at[l, 2 * peer[0] + peer[1], pl.ds(peer[2] * hP, hP)]))
        return out

    return plan


def place_small(pos, c8, wc):
    L = wc.shape[0]

    def body(pos_ref, c_ref, wc_ref, call_ref, wcall_ref):
        call_ref[...] = c_ref[...]
        wcall_ref[...] = wc_ref[...]

    return pl.pallas_call(
        body, name="place_small",
        grid_spec=pltpu.PrefetchScalarGridSpec(
            num_scalar_prefetch=1, grid=(1,),
            in_specs=[pl.BlockSpec((SUBLANES, LANES), lambda i, p: (0, 0)),
                      pl.BlockSpec((L, 3, LANES), lambda i, p: (0, 0, 0))],
            out_specs=[pl.BlockSpec((None, SUBLANES, LANES), lambda i, p: (p[2], 0, 0)),
                       pl.BlockSpec((None, L, 3, LANES), lambda i, p: (p[1], 0, 0, 0))]),
        out_shape=[jax.ShapeDtypeStruct((N_DEV, SUBLANES, LANES), F32),
                   jax.ShapeDtypeStruct((N_CHIPS, L, 3, LANES), F32)],
        compiler_params=_params(),
    )(pos, c8, wc)


def plan_small(refs, x, y, c):
    call, wcall = refs
    mine = call.at[4 * x + 2 * y + c]
    out = [(mine, mine, peer, call.at[4 * peer[0] + 2 * peer[1] + peer[2]]) for peer in _peers7(x, y, c)]
    mine = wcall.at[2 * x + y]
    out += [(mine, mine, (px, py, c), wcall.at[2 * px + py]) for (px, py) in _chips(x, y)]
    return out


def add_sibling(cidx, mine, sib):
    def body(c_ref, *refs):
        for a in range(3):
            m, s, o = refs[a], refs[3 + a], refs[6 + a]
            o[...] = (m[...].astype(F32) + s[...].astype(F32)).astype(BF16)

    per_step = 2

    def mine_spec(a):
        h = a.shape[1] // 2
        return pl.BlockSpec((per_step, h, a.shape[2]), lambda j, c_ref: (j, c_ref[0], 0))

    def sib_spec(a):
        return pl.BlockSpec((per_step,) + a.shape[1:], lambda j, c_ref: (j, 0, 0))

    return pl.pallas_call(
        body, name="add_sibling",
        grid_spec=pltpu.PrefetchScalarGridSpec(
            num_scalar_prefetch=1, grid=(N_CHIPS // per_step,),
            in_specs=[mine_spec(a) for a in mine] + [sib_spec(a) for a in sib],
            out_specs=[sib_spec(a) for a in sib]),
        out_shape=[jax.ShapeDtypeStruct(a.shape, BF16) for a in sib],
        compiler_params=_params(VMEM_BIG),
    )(cidx, *mine, *sib)


def sum_chips(pos, own, rb, acc, l, shapes):
    nq = 2
    n_in = 6 + (3 if acc is not None else 0)

    def body(pos_ref, *refs):
        for a in range(3):
            m, b, o = refs[a], refs[3 + a], refs[n_in + a]
            s = m[...].astype(F32)
            for j in range(3):
                s = s + b[j].astype(F32)
            o[...] = s

    def own_spec(a):
        return pl.BlockSpec((None, a.shape[1] // nq, a.shape[2]), lambda q, p: (p[1], q, 0))

    def rb_spec(a):
        return pl.BlockSpec((3, a.shape[1] // nq, a.shape[2]), lambda q, p: (0, q, 0))

    hi, ho, hp = own[0].shape[1] // nq, own[1].shape[1] // nq, own[2].shape[1] // nq
    out_specs = [pl.BlockSpec((None, hi, shapes[0][2]), lambda q, p: (l, p[0] * nq + q, 0)),
                 pl.BlockSpec((None, ho, shapes[1][2]), lambda q, p: (l, p[0] * nq + q, 0)),
                 pl.BlockSpec((None, None, hp, LANES), lambda q, p: (l, p[1], p[0] * nq + q, 0))]
    in_specs = [own_spec(a) for a in own] + [rb_spec(a) for a in rb]
    args = list(own) + list(rb)
    aliases = {}
    if acc is not None:
        in_specs += [ANY] * 3
        args += list(acc)
        aliases = {7: 0, 8: 1, 9: 2}
    return pl.pallas_call(
        body, name="sum_chips",
        grid_spec=pltpu.PrefetchScalarGridSpec(num_scalar_prefetch=1, grid=(nq,), in_specs=in_specs, out_specs=out_specs),
        out_shape=[jax.ShapeDtypeStruct(s, F32) for s in shapes],
        input_output_aliases=aliases,
        compiler_params=_params(VMEM_BIG),
    )(pos, *args)


def _wconv_slot(chip, tap):
    idx = 3 * chip + tap
    return ROW_WCONV + idx // SUBLANES, slice((idx % SUBLANES) * LANES, (idx % SUBLANES + 1) * LANES)


def pack_small(pos, per_layer, loss_blk):
    L = len(per_layer)
    D = per_layer[0][0].shape[1]

    def body(pos_ref, *refs):
        o = refs[-1]
        lb = refs[-2]
        o[...] = jnp.zeros_like(o)
        for l in range(L):
            dgpre, dgpost, dsh, dsc, dgt, dps, dwc = refs[7 * l:7 * l + 7]
            base = SUBLANES * l
            o[pl.ds(base + ROW_G_PRE, 1), :] = dgpre[...]
            o[pl.ds(base + ROW_G_POST, 1), :] = dgpost[...]
            for r, src in enumerate((dsh, dsc, dgt)):
                o[pl.ds(base + ROW_MOD + r, 1), :] = src[...]
            o[pl.ds(base + ROW_PSCALE, 1), 0:dps.shape[1]] = dps[...]
            for j in range(dwc.shape[0]):
                for k in range(3):
                    row, lanes = _wconv_slot(j, k)
                    o[pl.ds(base + row, 1), lanes] = dwc[j, pl.ds(k, 1), :]
        o[pl.ds(ROW_PSCALE, 1), LOSS_LANES] = lb[pl.ds(0, 1), :]

    flat = [a for layer in per_layer for a in layer] + [loss_blk]

    def whole(a):
        return pl.BlockSpec(a.shape, lambda i, p: (0,) * a.ndim)

    return pl.pallas_call(
        body, name="pack_small",
        grid_spec=pltpu.PrefetchScalarGridSpec(
            num_scalar_prefetch=1, grid=(1,), in_specs=[whole(a) for a in flat],
            out_specs=pl.BlockSpec((None, L * SUBLANES, D), lambda i, p: (p[2], 0, 0))),
        out_shape=jax.ShapeDtypeStruct((N_DEV, L * SUBLANES, D), F32),
        compiler_params=_params(),
    )(pos, *flat)


def small_update(pos, packs, params, moments_m, moments_v):
    n = len(params)
    L, D = params[1].shape
    PS = params[3].shape[1]

    def body(pos_ref, p_ref, *refs):
        ws, ms, vs = refs[0:n], refs[n:2 * n], refs[2 * n:3 * n]
        loss_ref = refs[3 * n]
        outs = [refs[3 * n + 1 + 4 * t:3 * n + 5 + 4 * t] for t in range(n)]
        summed = refs[-1]
        s = p_ref[0]
        for d in range(1, N_DEV):
            s = s + p_ref[d]
        summed[...] = s
        loss_ref[...] = summed[pl.ds(ROW_PSCALE, 1), LOSS_LANES]
        chip = pos_ref[1]

        def update(t, idx, g):
            d, mm, vv = _adamw_math(ws[t][idx], g, ms[t][idx], vs[t][idx])
            g_ref, d_ref, mo_ref, vo_ref = outs[t]
            g_ref[idx] = g
            d_ref[idx] = d
            mo_ref[idx] = mm
            vo_ref[idx] = vv

        for l in range(L):
            base = SUBLANES * l
            row = pl.ds(l, 1)
            for k in range(3):
                update(0, (row, slice(k * D, (k + 1) * D)), summed[pl.ds(base + ROW_MOD + k, 1), :])
            update(1, (row, slice(None)), summed[pl.ds(base + ROW_G_PRE, 1), :])
            update(2, (row, slice(None)), summed[pl.ds(base + ROW_G_POST, 1), :])
            update(3, (row, slice(None)), summed[pl.ds(base + ROW_PSCALE, 1), 0:PS])
            for k in range(3):
                g = None
                for j in range(N_CHIPS):
                    wrow, lanes = _wconv_slot(j, k)
                    cand = summed[pl.ds(base + wrow, 1), lanes]
                    g = cand if g is None else jnp.where(chip == j, cand, g)
                update(4, (l, pl.ds(k, 1), slice(None)), g)

    def whole(a):
        return pl.BlockSpec(a.shape, lambda i, p: (0,) * a.ndim)

    ins = [packs] + list(params) + list(moments_m) + list(moments_v)
    out_shape = [jax.ShapeDtypeStruct((1, LANES), F32)]
    for w in params:
        out_shape += [jax.ShapeDtypeStruct(w.shape, F32)] * 4
    outs = pl.pallas_call(
        body, name="small_update",
        grid_spec=pltpu.PrefetchScalarGridSpec(
            num_scalar_prefetch=1, grid=(1,), in_specs=[whole(a) for a in ins],
            out_specs=[whole(a) for a in out_shape],
            scratch_shapes=[pltpu.VMEM(packs.shape[1:], F32)]),
        out_shape=out_shape,
        compiler_params=_params(),
    )(pos, *ins)
    return outs[0], [outs[1 + 4 * t:5 + 4 * t] for t in range(n)]


def _adamw_math(w, g, m, v):
    m = ADAM_B1 * m + (1.0 - ADAM_B1) * g
    v = ADAM_B2 * v + (1.0 - ADAM_B2) * (g * g)
    m_hat = m / (1.0 - ADAM_B1 ** ADAM_STEP)
    v_hat = v / (1.0 - ADAM_B2 ** ADAM_STEP)
    delta = -ADAM_LR * (m_hat / (jnp.sqrt(v_hat) + ADAM_EPS) + ADAM_WD * w)
    return delta, m, v


def _adamw_block(ins, outs):
    w_ref, g_ref, m_ref, v_ref = ins
    go_ref, d_ref, mo_ref, vo_ref = outs
    gv = g_ref[...]
    d, mm, vv = _adamw_math(w_ref[...], gv, m_ref[...], v_ref[...])
    go_ref[...] = gv
    d_ref[...] = d
    mo_ref[...] = mm
    vo_ref[...] = vv


def adamw(groups, name, first, count, steps, acc=None):
    n = len(groups)

    def body(*refs):
        outs = refs[len(refs) - 4 * n:]
        for k in range(n):
            _adamw_block(refs[4 * k:4 * k + 4], outs[4 * k:4 * k + 4])

    specs, out_shape, args = [], [], []
    for group in groups:
        shape = group[0].shape
        spec = pl.BlockSpec((1, shape[1] // steps) + shape[2:],
                            lambda i, s, rest=(0,) * (len(shape) - 2): (first + i, s) + rest)
        specs += [spec] * 4
        out_shape += [jax.ShapeDtypeStruct(shape, F32)] * 4
        args += list(group)
    extra = [] if acc is None else list(acc)
    return pl.pallas_call(
        body, name=name, grid=(count, steps),
        in_specs=specs + [ANY] * len(extra), out_specs=specs, out_shape=out_shape,
        input_output_aliases={4 * n + a: a for a in range(len(extra))},
        compiler_params=_params(VMEM_BIG, n_grid=2),
    )(*args, *extra)


def ada_finish(c_all, dmod, w, m, v):
    L, D, CW = w.shape
    hD = D // 2

    def body(c_ref, d_ref, w_ref, m_ref, v_ref, g_ref, dl_ref, mo_ref, vo_ref):
        cv = c_ref[...]
        z = jnp.zeros_like(cv)
        ca = jnp.concatenate([cv * jax.nn.sigmoid(cv), z], axis=0).astype(BF16)
        dm = jnp.concatenate([d_ref[0], jnp.zeros_like(d_ref[0])], axis=0).astype(BF16)
        g = lax.dot_general(ca, dm, TN, preferred_element_type=F32)
        g_ref[0] = g
        d, mm, vv = _adamw_math(w_ref[0], g, m_ref[0], v_ref[0])
        dl_ref[0] = d
        mo_ref[0] = mm
        vo_ref[0] = vv

    big = pl.BlockSpec((1, hD, CW), lambda l, h: (l, h, 0))
    shape = jax.ShapeDtypeStruct(w.shape, F32)
    return pl.pallas_call(
        body, name="ada_finish", grid=(L, 2),
        in_specs=[pl.BlockSpec((N_DEV, hD), lambda l, h: (0, h)), pl.BlockSpec((1, N_DEV, CW), lambda l, h: (l, 0, 0)),
                  big, big, big],
        out_specs=[big] * 4, out_shape=[shape] * 4,
        compiler_params=_params(VMEM_BIG, n_grid=2),
    )(c_all, dmod, w, m, v)


def kernel(x, c, w_ada, b_ada, g_pre, w_in, w_conv, w_pool, pool_scale, w_out, g_post, loss_target, m_w_ada, m_b_ada, m_g_pre, m_w_in, m_w_conv, m_w_pool, m_pool_scale, m_w_out, m_g_post, v_w_ada, v_b_ada, v_g_pre, v_w_in, v_w_conv, v_w_pool, v_pool_scale, v_w_out, v_g_post):
    L, D, CW = w_in.shape
    RO = w_out.shape[1]
    T = x.shape[1]
    ix, iy, ic = _me()
    chip = 2 * ix + iy
    me_lin = 4 * ix + 2 * iy + ic

    pos = jnp.stack([ic, chip, me_lin]).astype(jnp.int32)
    n_s, n_c = 3, 9

    def gather(bufs, after):
        ss, rs, bufs, tok = xchg_start("gather_start", bufs, 3 * len(bufs), plan_gather, after=after)
        return (ss, rs, bufs), tok

    def ready(flight, after):
        fss, frs, bufs = flight
        return xchg_wait("forward_wait", bufs, fss, frs, 3 * len(bufs), plan_forward, after)

    def arrive_part(flight, which, after, base=0):
        ss, rs, bufs = flight
        sems = tuple(range(base + 3 * which, base + 3 * which + 3))
        (buf,) = xchg_wait("gather_wait", [bufs[which]], ss, rs, 3, plan_gather, after, sems=sems)
        fss, frs, (buf,), tok = xchg_start("forward_start", [buf], 3, plan_forward, sibling_only=True)
        return (fss, frs, [buf]), tok

    n_small = N_DEV - 1 + N_CHIPS - 1
    w_in_of, w_out_of = [None] * L, [None] * L
    gi0, go0 = cast_weights(pos, w_in, w_out, 0, pos)

    def plan_first(refs, x, y, c):
        return plan_small(refs[0:2], x, y, c) + plan_gather(refs[2:3], x, y, c)

    placed = list(place_small(pos, c.reshape(SUBLANES, LANES), w_conv))
    s_ss, s_rs, firsts, token = xchg_start("first_start", placed + [gi0], n_small + 3, plan_first)
    smalls_in = firsts[0:2]
    w_in_of[0] = ((s_ss, s_rs, firsts[2:3]), 0)
    g_pre_l, g_post_l, pscale_l, b_ada_l, m_w_conv_l, v_w_conv_l, token = lax.optimization_barrier(
        (g_pre, g_post, pool_scale, b_ada, m_w_conv, v_w_conv, token))
    g_pre3, g_post3 = g_pre_l.reshape(L, 1, D), g_post_l.reshape(L, 1, D)
    pscale3 = pscale_l.reshape(L, 1, pool_scale.shape[1])
    gi1, go1 = cast_weights(pos, w_in, w_out, 1, token)
    c_all3, wconv_all = xchg_wait("small_wait", smalls_in, s_ss, s_rs, n_small, plan_small, [token, go1])
    c_all = c_all3.reshape(N_DEV, D)
    b_my = lax.dynamic_slice_in_dim(b_ada_l, chip * CW, CW, axis=1)
    m_ss, m_rs, mods, token = xchg_start("mod_start", [mod_part(pos, c_all, w_ada, b_my, token)], 3, plan_mod)
    flight, token = gather([go0, gi1, go1], [token])
    w_out_of[0], w_in_of[1], w_out_of[1] = (flight, 0), (flight, 1), (flight, 2)
    late = []
    for l in range(2, L):
        late += list(cast_weights(pos, w_in, w_out, l, token))
    flight, token = gather(late, [])
    for l in range(2, L):
        w_in_of[l], w_out_of[l] = (flight, 2 * (l - 2)), (flight, 2 * (l - 2) + 1)
    fwd_in, token = arrive_part(*w_in_of[0], [token], base=n_small)
    (mod_all,) = xchg_wait("mod_wait", mods, m_ss, m_rs, 3, plan_mod, [token])
    mod = lax.dynamic_index_in_dim(mod_all, me_lin, axis=2, keepdims=False)
    mod4 = jnp.transpose(mod, (1, 0, 2)).reshape(L, 3, 1, D)

    xs, projs, yas, yps, ys, pooleds = [x.reshape(T, D)], [], [], [], [], []
    wg_in, wg_out = [], []
    for l in range(L):
        (gi,) = ready(fwd_in, [mod4 if l == 0 else xs[l]])
        proj = proj_fwd(xs[l], mod4, g_pre3, gi, l)
        ya, yp, pooled = mix_fwd(proj, wconv_all, w_pool, pscale3, l)
        pooleds.append(pooled)
        fwd_out, token = arrive_part(*w_out_of[l], [ya, yp])
        after = [token]
        if 0 < l < L - 1:
            fwd_in, token = arrive_part(*w_in_of[l + 1], after)
            after = [token]
        (go,) = ready(fwd_out, after)
        wg_in.append(gi)
        wg_out.append(go.reshape(N_CHIPS * RO, D))
        projs.append(proj)
        yas.append(ya)
        yps.append(yp)
        if l + 1 < L:
            xn, yv = out_fwd(ya, yp, wg_out[l], xs[l], mod4, g_post3, l, after[0])
            xs.append(xn)
            if l == 0:
                fwd_in, token = arrive_part(*w_in_of[1], [xn])
        else:
            dx, yv, loss_blk = out_fwd_loss(ya, yp, wg_out[l], xs[l], mod4, g_post3, l, loss_target.reshape(T, D))
        ys.append(yv)

    shapes = (w_in.shape, w_out.shape, w_pool.shape)
    smalls = [None] * L
    acc, flying, sib, token = None, None, None, loss_blk

    def to_chips(sib, after):
        sl, s_ss, s_rs, s_bufs = sib
        s_bufs = xchg_wait("sibling_wait", s_bufs, s_ss, s_rs, n_s, plan_sibling, after)
        chip_parts = add_sibling(pos, s_bufs[0:3], s_bufs[3:6])
        lands = [lax.empty((3,) + a.shape[1:], a.dtype) for a in chip_parts]
        c_ss, c_rs, c_bufs, ctoken = xchg_start("chip_start", list(chip_parts) + lands, n_c, plan_chip)
        return (sl, c_ss, c_rs, c_bufs), ctoken

    def landed(flying, acc, after):
        fl, f_ss, f_rs, f_bufs = flying
        f_bufs = xchg_wait("chip_wait", f_bufs, f_ss, f_rs, n_c, plan_chip, after)
        return sum_chips(pos, f_bufs[0:3], f_bufs[3:6], acc, fl, shapes)

    early = None
    for l in reversed(range(L)):
        dya, dyp, dwo_l, dgate, dgpost = out_bwd(dx, ys[l], yas[l], yps[l], wg_out[l], mod4, g_post3, l, token)
        token = dya
        spreading = None
        if sib is not None:
            arrived = flying
            flying, token = to_chips(sib, [dya])
            if arrived is not None:
                acc = landed(arrived, acc, [token])
                spreading = plan_spread((arrived[0],), ())
                sp_ss, sp_rs, acc, token = xchg_start("spread_start", list(acc), 2, spreading, sibling_only=True)
        du_a, db_a, dc_a, dg_a, du_p, dg_p, dwc, dwp_l, dps = mix_bwd(projs[l], pooleds[l], dya, dyp, wconv_all, w_pool,
                                                                        pscale3, l, token)
        update = None
        if spreading is not None:
            acc = xchg_wait("spread_wait", acc, sp_ss, sp_rs, 2, spreading, [du_a])
            update = (arrived[0], [w_in, acc[0], m_w_in, v_w_in], [w_out, acc[1], m_w_out, v_w_out], early)
        dx, dwi_l, dshift, dscale, dgpre, *rest = in_bwd([du_a, db_a, dc_a, dg_a, du_p, dg_p], wg_in[l], xs[l], dx,
                                                         mod4, g_pre3, l, update)
        early = rest if rest else early
        smalls[l] = (dgpre, dgpost, dshift, dscale, dgate, dps, dwc)
        parts = [dwi_l, dwo_l.reshape(N_CHIPS, RO, D), dwp_l]
        s_lands = [lax.empty((a.shape[0], a.shape[1] // 2) + a.shape[2:], a.dtype) for a in parts]
        s_ss, s_rs, s_bufs, token = xchg_start("sibling_start", parts + s_lands, n_s, plan_sibling, sibling_only=True)
        sib = (l, s_ss, s_rs, s_bufs)
    grad_x = dx.reshape(1, T, D)

    p_ss, p_rs, packs, ptoken = xchg_start("pack_start", [pack_small(pos, smalls, loss_blk)], N_DEV - 1, plan_pack)
    acc = landed(flying, acc, [ptoken, token])
    n_sp = 2 + (N_DEV - 1) * (L - 1)
    spread = plan_spread((1,), tuple(range(1, L)))
    sp_ss, sp_rs, acc, sp_token = xchg_start("spread_start", list(acc), n_sp, spread)
    flying, token = to_chips(sib, [sp_token])
    (packs_all,) = xchg_wait("pack_wait", packs, p_ss, p_rs, N_DEV - 1, plan_pack, [token])
    dmod_all = packs_all.reshape(N_DEV, L, SUBLANES, D)[:, :, ROW_MOD:ROW_MOD + 3].reshape(N_DEV, L, 3 * D)
    dmod_my = jnp.transpose(lax.dynamic_slice_in_dim(dmod_all, chip * CW, CW, axis=2), (1, 0, 2))

    g_w_ada, d_w_ada, nm_w_ada, nv_w_ada = ada_finish(c_all, dmod_my, w_ada, m_w_ada, v_w_ada)
    packs_late, _ = lax.optimization_barrier((packs_all, nv_w_ada))
    loss_row, upd = small_update(pos, packs_late, [b_ada, g_pre, g_post, pool_scale, w_conv],
                                 [m_b_ada, m_g_pre, m_g_post, m_pool_scale, m_w_conv_l],
                                 [v_b_ada, v_g_pre, v_g_post, v_pool_scale, v_w_conv_l])
    loss = loss_row[0, 0]
    (g_b_ada, d_b_ada, nm_b_ada, nv_b_ada), (g_g_pre, d_g_pre, nm_g_pre, nv_g_pre) = upd[0], upd[1]
    (g_g_post, d_g_post, nm_g_post, nv_g_post), (g_pscale, d_pscale, nm_pscale, nv_pscale) = upd[2], upd[3]
    g_w_conv, d_w_conv, nm_w_conv, nv_w_conv = upd[4]

    done = [nv_w_ada, nv_w_conv]
    g_w_in, g_w_out, g_w_pool = xchg_wait("spread_wait", acc, sp_ss, sp_rs, n_sp, spread, done)
    early = adamw([[w_in, g_w_in, m_w_in, v_w_in], [w_out, g_w_out, m_w_out, v_w_out]], "adamw_layer", 1, 1, 2, early)

    acc = landed(flying, (g_w_in, g_w_out, g_w_pool), [early[3], early[7]])
    last = plan_spread((0,), (0,))
    n_last = 2 + N_DEV - 1
    l_ss, l_rs, acc, _ = xchg_start("spread_start", list(acc), n_last, last)
    upd_pool = adamw([[w_pool, acc[2], m_w_pool, v_w_pool]], "adamw_w_pool", 1, L - 1, 1)
    r_w_in, r_w_out, r_w_pool = xchg_wait("spread_wait", acc, l_ss, l_rs, n_last, last, [upd_pool[3]])
    (g_w_in, d_w_in, nm_w_in, nv_w_in, g_w_out, d_w_out, nm_w_out, nv_w_out,
     g_w_pool, d_w_pool, nm_w_pool, nv_w_pool) = adamw(
         [[w_in, r_w_in, m_w_in, v_w_in], [w_out, r_w_out, m_w_out, v_w_out], [w_pool, r_w_pool, m_w_pool, v_w_pool]],
         "adamw_layer", 0, 1, 2, list(early) + list(upd_pool))

    return (loss, grad_x,
            g_w_ada, g_b_ada, g_g_pre, g_w_in, g_w_conv, g_w_pool, g_pscale, g_w_out, g_g_post,
            d_w_ada, d_b_ada, d_g_pre, d_w_in, d_w_conv, d_w_pool, d_pscale, d_w_out, d_g_post,
            nm_w_ada, nm_b_ada, nm_g_pre, nm_w_in, nm_w_conv, nm_w_pool, nm_pscale, nm_w_out, nm_g_post,
            nv_w_ada, nv_b_ada, nv_g_pre, nv_w_in, nv_w_conv, nv_w_pool, nv_pscale, nv_w_out, nv_g_post)
```

```python
import functools

import jax
import jax.numpy as jnp
from jax import lax
from jax.experimental import pallas as pl
from jax.experimental.pallas import tpu as pltpu

F32 = jnp.float32
BF16 = jnp.bfloat16
MESH = pl.DeviceIdType.MESH
ANY = pl.BlockSpec(memory_space=pl.ANY)

NORM_EPS = 1e-6
POOL_WINDOWS = (2, 4, 8, 16)
ADAM_LR = 0.001
ADAM_B1 = 0.9
ADAM_B2 = 0.999
ADAM_EPS = 1e-08
ADAM_WD = 0.01
ADAM_STEP = 10

N_CHIPS = 4
N_DEV = 8
LANES = 128
SUBLANES = 8
VMEM_BIG = 56 * 1024 * 1024
HIST = 16
R_CONV = 64
R_POOL = 128

ROW_G_PRE, ROW_G_POST, ROW_MOD, ROW_PSCALE, ROW_WCONV = 0, 1, 2, 5, 6
LOSS_LANES = slice(4 * LANES, 5 * LANES)

NT = (((1,), (1,)), ((), ()))
TN = (((0,), (0,)), ((), ()))


def _params(vmem=None, n_grid=1):
    kw = {}
    if n_grid:
        kw["dimension_semantics"] = ("arbitrary",) * n_grid
    if vmem is not None:
        kw["vmem_limit_bytes"] = vmem
    return pltpu.CompilerParams(**kw)


def _colsum8(v):
    n, d = v.shape
    return v.reshape(n // SUBLANES, SUBLANES, d).sum(axis=0)


def _rms(v):
    return lax.rsqrt(jnp.mean(v * v, axis=-1, keepdims=True) + NORM_EPS)


def _sigmoid(v):
    return 0.5 * jnp.tanh(0.5 * v) + 0.5


def _shift_down(ext, k, rows):
    if k == 0:
        return ext[HIST:HIST + rows]
    return pltpu.roll(ext, k, 0)[HIST:HIST + rows]


def _shift_up(ext, k, rows):
    if k == 0:
        return ext[0:rows]
    return pltpu.roll(ext, ext.shape[0] - k, 0)[0:rows]


def _load_ext(ref, r0, h0, first, rows):
    hist = ref[pl.ds(h0, HIST), :].astype(F32)
    hist = jnp.where(first, 0.0, hist)
    cur = ref[pl.ds(r0, rows), :].astype(F32)
    return jnp.concatenate([hist, cur], axis=0)


def _me():
    return lax.axis_index("x"), lax.axis_index("y"), lax.axis_index("c")


def cast_weights(pos, w_in, w_out, l, after):
    _, D, CW = w_in.shape
    RO = w_out.shape[1]

    def body(pos_ref, wi, wo, after_ref, oi, oo):
        oi[...] = wi[...].astype(BF16)
        oo[...] = wo[...].astype(BF16)

    return pl.pallas_call(
        body, name="cast_w",
        grid_spec=pltpu.PrefetchScalarGridSpec(
            num_scalar_prefetch=1, grid=(2,),
            in_specs=[pl.BlockSpec((None, D // 2, CW), lambda h, p: (l, h, 0)),
                      pl.BlockSpec((None, RO // 2, D), lambda h, p: (l, h, 0)), ANY],
            out_specs=[pl.BlockSpec((D // 2, CW), lambda h, p: (h, p[1])),
                       pl.BlockSpec((None, RO // 2, D), lambda h, p: (p[1], h, 0))]),
        out_shape=[jax.ShapeDtypeStruct((D, N_CHIPS * CW), BF16), jax.ShapeDtypeStruct((N_CHIPS, RO, D), BF16)],
        compiler_params=_params(),
    )(pos, w_in, w_out, after)


def mod_part(pos, c_all, w_ada, b_my, after):
    L, D, CW = w_ada.shape

    def body(pos_ref, c_ref, w_ref, b_ref, after_ref, o_ref):
        cv = c_ref[...]
        ca = (cv * jax.nn.sigmoid(cv)).astype(BF16)
        o_ref[...] = jnp.dot(ca, w_ref[0].astype(BF16), preferred_element_type=F32) + b_ref[0]

    return pl.pallas_call(
        body, name="mod_part",
        grid_spec=pltpu.PrefetchScalarGridSpec(
            num_scalar_prefetch=1, grid=(L,),
            in_specs=[pl.BlockSpec((N_DEV, D), lambda l, p: (0, 0)),
                      pl.BlockSpec((1, D, CW), lambda l, p: (l, 0, 0)),
                      pl.BlockSpec((1, 1, CW), lambda l, p: (l, 0, 0)), ANY],
            out_specs=pl.BlockSpec((None, None, N_DEV, CW), lambda l, p: (p[1], l, 0, 0))),
        out_shape=jax.ShapeDtypeStruct((N_CHIPS, L, N_DEV, CW), F32),
        compiler_params=_params(VMEM_BIG),
    )(pos, c_all, w_ada, b_my.reshape(L, 1, CW), after)


def _mod_row(l, k, D):
    return pl.BlockSpec((None, None, 1, D), lambda *_: (l, k, 0, 0))


def _layer_row(l, D):
    return pl.BlockSpec((None, 1, D), lambda *_: (l, 0, 0))


def proj_fwd(x, mod4, g_pre3, wg, l):
    T, D = x.shape
    NC = wg.shape[1]
    NB = N_CHIPS
    CW = NC // NB
    tm = 512

    def body(x_ref, sh_ref, sc_ref, g_ref, w_ref, o_ref):
        xv = x_ref[...]
        h = (xv * _rms(xv)) * (g_ref[...] * (1.0 + sc_ref[...])) + sh_ref[...]
        hb = h.astype(BF16)
        for j in range(NB):
            cols = slice(j * CW, (j + 1) * CW)
            o_ref[:, cols] = jnp.dot(hb, w_ref[:, cols], preferred_element_type=F32).astype(BF16)

    return pl.pallas_call(
        body, name="proj_fwd", grid=(T // tm,),
        in_specs=[pl.BlockSpec((tm, D), lambda i: (i, 0)), _mod_row(l, 0, D), _mod_row(l, 1, D), _layer_row(l, D),
                  pl.BlockSpec((D, NC), lambda i: (0, 0))],
        out_specs=pl.BlockSpec((tm, NC), lambda i: (i, 0)),
        out_shape=jax.ShapeDtypeStruct((T, NC), BF16),
        compiler_params=_params(VMEM_BIG),
    )(x, mod4, mod4, g_pre3, wg)


N_MIX = 4


def _conv_fwd_block(u_ref, b_ref, c_ref, g_ref, w_ref, o_ref):
    T = u_ref.shape[0]
    R = 2 * R_CONV
    w0 = w_ref[pl.ds(0, 1), :]
    w1 = w_ref[pl.ds(1, 1), :]
    w2 = w_ref[pl.ds(2, 1), :]

    def chunk(i, carry):
        r0 = pl.multiple_of(i * R, R)
        h0 = pl.multiple_of(jnp.maximum(r0 - HIST, 0), HIST)
        first = i == 0
        ca = _load_ext(c_ref, r0, h0, first, R) * _load_ext(u_ref, r0, h0, first, R)
        conv = w2 * ca[HIST:] + w1 * _shift_down(ca, 1, R) + w0 * _shift_down(ca, 2, R)
        g = g_ref[pl.ds(r0, R), :].astype(F32)
        b = b_ref[pl.ds(r0, R), :].astype(F32)
        o_ref[pl.ds(r0, R), :] = (b * conv * (g * _sigmoid(g))).astype(BF16)
        return carry

    lax.fori_loop(0, T // R, chunk, 0)


def _conv_idx(j):
    return jnp.minimum(j, N_MIX - 1)


def _pool_idx(j):
    return jnp.maximum(j - N_MIX, 0)


def _proj_col(T, off, idx):
    return pl.BlockSpec((T, LANES), lambda j: (0, idx(j) + off))


def _causal_window_sum(ext, w):
    s, k = ext, 1
    while k < w:
        s = s + pltpu.roll(s, k, 0)
        k *= 2
    return s


def _anticausal_window_sum(ext, w):
    s, k = ext, 1
    n = ext.shape[0]
    while k < w:
        s = s + pltpu.roll(s, n - k, 0)
        k *= 2
    return s


def _count(r0, rows, w):
    t = r0 + lax.broadcasted_iota(jnp.int32, (rows, LANES), 0)
    return jnp.minimum(t + 1, w).astype(F32)


def _pooled_loop(p_ref, pooled_s, w, T):
    R = R_POOL

    def chunk(i, carry):
        r0 = pl.multiple_of(i * R, R)
        h0 = pl.multiple_of(jnp.maximum(r0 - HIST, 0), HIST)
        ext = _load_ext(p_ref, r0, h0, i == 0, R)
        ws = _causal_window_sum(ext, w)[HIST:]
        pooled_s[pl.ds(r0, R), :] = (ws / _count(r0, R, w) - ext[HIST:]).astype(BF16)
        return carry

    lax.fori_loop(0, T // R, chunk, 0)


def _conv_w_spec(l):
    return pl.BlockSpec((None, None, 3, LANES), lambda j: (_conv_idx(j), l, 0, 0))


def _pool_w_spec(l):
    return pl.BlockSpec((None, None, LANES, LANES), lambda j: (l, _pool_idx(j), 0, 0))


def _pool_s_spec(l):
    return pl.BlockSpec((None, 1, LANES), lambda j: (l, 0, _pool_idx(j)))


def _pool_fwd_group(p_ref, g_ref, w_ref, s_ref, o_ref, pooled_s, mixed_s, w):
    T = p_ref.shape[0]
    R = R_POOL
    _pooled_loop(p_ref, pooled_s, w, T)
    mixed_s[...] = jnp.dot(pooled_s[...], w_ref[...].astype(BF16), preferred_element_type=F32)
    sc = s_ref[...]

    def chunk(i, carry):
        r0 = pl.multiple_of(i * R, R)
        g = g_ref[pl.ds(r0, R), :].astype(F32)
        o_ref[pl.ds(r0, R), :] = (mixed_s[pl.ds(r0, R), :] * sc * (g * _sigmoid(g))).astype(BF16)
        return carry

    lax.fori_loop(0, T // R, chunk, 0)


def mix_fwd(proj, wconv, wpool, pscale3, l):
    T = proj.shape[0]

    def body(u_ref, b_ref, c_ref, g_ref, p_ref, gp_ref, wc_ref, wp_ref, s_ref, ya_ref, yp_ref, pooled_ref, mixed_s):
        j = pl.program_id(0)
        pl.when(j < N_MIX)(functools.partial(_conv_fwd_block, u_ref, b_ref, c_ref, g_ref, wc_ref, ya_ref))
        for k, w in enumerate(POOL_WINDOWS):
            pl.when(j == N_MIX + k)(functools.partial(_pool_fwd_group, p_ref, gp_ref, wp_ref, s_ref, yp_ref,
                                                      pooled_ref, mixed_s, w))

    half = jax.ShapeDtypeStruct((T, N_MIX * LANES), BF16)
    pool_col = pl.BlockSpec((T, LANES), lambda j: (0, _pool_idx(j)))
    return pl.pallas_call(
        body, name="mix_fwd", grid=(2 * N_MIX,),
        in_specs=[_proj_col(T, 0, _conv_idx), _proj_col(T, 4, _conv_idx), _proj_col(T, 8, _conv_idx),
                  _proj_col(T, 12, _conv_idx), _proj_col(T, 16, _pool_idx), _proj_col(T, 20, _pool_idx),
                  _conv_w_spec(l), _pool_w_spec(l), _pool_s_spec(l)],
        out_specs=[pl.BlockSpec((T, LANES), lambda j: (0, _conv_idx(j))), pool_col, pool_col],
        out_shape=[half, half, half],
        scratch_shapes=[pltpu.VMEM((T, LANES), F32)],
        compiler_params=_params(),
    )(proj, proj, proj, proj, proj, proj, wconv, wpool, pscale3)


def out_fwd(ya, yp, wo, x, mod4, g_post3, l, after):
    T, D = x.shape
    H = ya.shape[1]
    tm = 512

    def body(ya_ref, yp_ref, wo_ref, x_ref, gt_ref, g_ref, after_ref, xn_ref, y_ref):
        y = (jnp.dot(ya_ref[...], wo_ref[0:H, :], preferred_element_type=F32)
             + jnp.dot(yp_ref[...], wo_ref[H:2 * H, :], preferred_element_type=F32))
        xn_ref[...] = x_ref[...] + gt_ref[...] * (y * _rms(y) * g_ref[...])
        y_ref[...] = y.astype(BF16)

    tile = pl.BlockSpec((tm, D), lambda i: (i, 0))
    half = pl.BlockSpec((tm, H), lambda i: (i, 0))
    return pl.pallas_call(
        body, name="out_fwd", grid=(T // tm,),
        in_specs=[half, half, pl.BlockSpec((2 * H, D), lambda i: (0, 0)), tile, _mod_row(l, 2, D), _layer_row(l, D),
                  ANY],
        out_specs=[tile, tile],
        out_shape=[jax.ShapeDtypeStruct((T, D), F32), jax.ShapeDtypeStruct((T, D), BF16)],
        compiler_params=_params(VMEM_BIG),
    )(ya, yp, wo, x, mod4, g_post3, after)


def out_fwd_loss(ya, yp, wo, x, mod4, g_post3, l, target):
    T, D = x.shape
    H = ya.shape[1]
    tm = 512
    nt = T // tm

    def body(ya_ref, yp_ref, wo_ref, x_ref, gt_ref, g_ref, t_ref, dx_ref, y_ref, l_ref, acc):
        i = pl.program_id(0)

        @pl.when(i == 0)
        def _():
            acc[...] = jnp.zeros_like(acc)

        y = (jnp.dot(ya_ref[...], wo_ref[0:H, :], preferred_element_type=F32)
             + jnp.dot(yp_ref[...], wo_ref[H:2 * H, :], preferred_element_type=F32))
        y_ref[...] = y.astype(BF16)
        d = (x_ref[...] + gt_ref[...] * (y * _rms(y) * g_ref[...])) - t_ref[...]
        dx_ref[...] = d * (1.0 / D)
        acc[...] += _colsum8(d * d)

        @pl.when(i == nt - 1)
        def _():
            l_ref[...] = jnp.zeros_like(l_ref) + jnp.sum(acc[...]) * (0.5 / D)

    tile = pl.BlockSpec((tm, D), lambda i: (i, 0))
    half = pl.BlockSpec((tm, H), lambda i: (i, 0))
    return pl.pallas_call(
        body, name="out_fwd_loss", grid=(nt,),
        in_specs=[half, half, pl.BlockSpec((2 * H, D), lambda i: (0, 0)), tile, _mod_row(l, 2, D), _layer_row(l, D),
                  tile],
        out_specs=[tile, tile, pl.BlockSpec((SUBLANES, LANES), lambda i: (0, 0))],
        out_shape=[jax.ShapeDtypeStruct((T, D), F32), jax.ShapeDtypeStruct((T, D), BF16),
                   jax.ShapeDtypeStruct((SUBLANES, LANES), F32)],
        scratch_shapes=[pltpu.VMEM((SUBLANES, D), F32)],
        compiler_params=_params(VMEM_BIG),
    )(ya, yp, wo, x, mod4, g_post3, target)


def out_bwd(dx, y, ya, yp, wo, mod4, g_post3, l, after):
    T, D = dx.shape
    H = ya.shape[1]
    tm = 512
    nt = T // tm

    def body(dx_ref, y_ref, ya_ref, yp_ref, wo_ref, gt_ref, g_ref, after_ref,
             dya_ref, dyp_ref, dwo_ref, dgt_ref, dg_ref, acc_w, acc_p):
        i = pl.program_id(0)

        @pl.when(i == 0)
        def _():
            acc_w[...] = jnp.zeros_like(acc_w)
            acc_p[...] = jnp.zeros_like(acc_p)

        yv = y_ref[...].astype(F32)
        dxv = dx_ref[...]
        gg = gt_ref[...] * g_ref[...]
        r = _rms(yv)
        yn = yv * r
        p = dxv * yn
        acc_p[...] += _colsum8(p)
        dy = r * (dxv * gg - yn * jnp.mean(p * gg, axis=-1, keepdims=True))
        dyb = dy.astype(BF16)
        dyc = lax.dot_general(dyb, wo_ref[...], NT, preferred_element_type=F32)
        dya_ref[...] = dyc[:, 0:H].astype(BF16)
        dyp_ref[...] = dyc[:, H:2 * H].astype(BF16)
        acc_w[0:H, :] += lax.dot_general(ya_ref[...], dyb, TN, preferred_element_type=F32)
        acc_w[H:2 * H, :] += lax.dot_general(yp_ref[...], dyb, TN, preferred_element_type=F32)

        @pl.when(i == nt - 1)
        def _():
            dwo_ref[...] = acc_w[...].astype(BF16)
            sp = jnp.sum(acc_p[...], axis=0, keepdims=True)
            dgt_ref[...] = g_ref[...] * sp
            dg_ref[...] = gt_ref[...] * sp

    row = pl.BlockSpec((1, D), lambda i: (0, 0))
    tile = pl.BlockSpec((tm, D), lambda i: (i, 0))
    half = pl.BlockSpec((tm, H), lambda i: (i, 0))
    full = pl.BlockSpec((2 * H, D), lambda i: (0, 0))
    return pl.pallas_call(
        body, name="out_bwd", grid=(nt,),
        in_specs=[tile, tile, half, half, full, _mod_row(l, 2, D), _layer_row(l, D), ANY],
        out_specs=[half, half, full, row, row],
        out_shape=[jax.ShapeDtypeStruct((T, H), BF16), jax.ShapeDtypeStruct((T, H), BF16),
                   jax.ShapeDtypeStruct((2 * H, D), BF16),
                   jax.ShapeDtypeStruct((1, D), F32), jax.ShapeDtypeStruct((1, D), F32)],
        scratch_shapes=[pltpu.VMEM((2 * H, D), F32), pltpu.VMEM((SUBLANES, D), F32)],
        compiler_params=_params(VMEM_BIG),
    )(dx, y, ya, yp, wo, mod4, g_post3, after)


def _conv_bwd_block(u_ref, b_ref, c_ref, g_ref, dy_ref, w_ref, du_ref, db_ref, dc_ref, dg_ref, dw_ref):
    T = u_ref.shape[0]
    R = R_CONV
    nchunk = T // R
    w0 = w_ref[pl.ds(0, 1), :]
    w1 = w_ref[pl.ds(1, 1), :]
    w2 = w_ref[pl.ds(2, 1), :]

    def chunk(k, carry):
        head, a0, a1, a2 = carry
        i = nchunk - 1 - k
        r0 = pl.multiple_of(i * R, R)
        h0 = pl.multiple_of(jnp.maximum(r0 - HIST, 0), HIST)
        first = i == 0
        ue = _load_ext(u_ref, r0, h0, first, R)
        ce = _load_ext(c_ref, r0, h0, first, R)
        ca = ce * ue
        ca0 = ca[HIST:]
        ca1 = _shift_down(ca, 1, R)
        ca2 = _shift_down(ca, 2, R)
        conv = w2 * ca0 + w1 * ca1 + w0 * ca2
        g = g_ref[pl.ds(r0, R), :].astype(F32)
        b = b_ref[pl.ds(r0, R), :].astype(F32)
        dy = dy_ref[pl.ds(r0, R), :].astype(F32)
        sg = _sigmoid(g)
        sl = g * sg
        t = dy * conv
        db_ref[pl.ds(r0, R), :] = (t * sl).astype(BF16)
        dg_ref[pl.ds(r0, R), :] = (t * b * (sg + sl * (1.0 - sg))).astype(BF16)
        dconv = dy * b * sl
        a2 = a2 + _colsum8(dconv * ca0)
        a1 = a1 + _colsum8(dconv * ca1)
        a0 = a0 + _colsum8(dconv * ca2)
        e = jnp.concatenate([dconv, head], axis=0)
        dca = w2 * dconv + w1 * _shift_up(e, 1, R) + w0 * _shift_up(e, 2, R)
        du_ref[pl.ds(r0, R), :] = (dca * ce[HIST:]).astype(BF16)
        dc_ref[pl.ds(r0, R), :] = (dca * ue[HIST:]).astype(BF16)
        return dconv[0:SUBLANES], a0, a1, a2

    z = jnp.zeros((SUBLANES, LANES), F32)
    _, a0, a1, a2 = lax.fori_loop(0, nchunk, chunk, (z, z, z, z))
    dw_ref[pl.ds(0, 1), :] = jnp.sum(a0, axis=0, keepdims=True)
    dw_ref[pl.ds(1, 1), :] = jnp.sum(a1, axis=0, keepdims=True)
    dw_ref[pl.ds(2, 1), :] = jnp.sum(a2, axis=0, keepdims=True)


def _pool_bwd_group(pooled_s, g_ref, dy_ref, w_ref, s_ref, du_ref, dg_ref, dw_ref, ds_ref,
                    mixed_s, dmix_s, dpool_s, w):
    T = pooled_s.shape[0]
    R = R_POOL
    nchunk = T // R
    wb = w_ref[...].astype(BF16)
    mixed_s[...] = jnp.dot(pooled_s[...], wb, preferred_element_type=F32)
    sc = s_ref[...]

    def gate_chunk(i, acc):
        r0 = pl.multiple_of(i * R, R)
        g = g_ref[pl.ds(r0, R), :].astype(F32)
        dy = dy_ref[pl.ds(r0, R), :].astype(F32)
        mixed = mixed_s[pl.ds(r0, R), :]
        sg = _sigmoid(g)
        sl = g * sg
        dg_ref[pl.ds(r0, R), :] = (dy * mixed * sc * (sg + sl * (1.0 - sg))).astype(BF16)
        dms = dy * sl
        dmix_s[pl.ds(r0, R), :] = (dms * sc).astype(BF16)
        return acc + _colsum8(dms * mixed)

    acc = lax.fori_loop(0, nchunk, gate_chunk, jnp.zeros((SUBLANES, LANES), F32))
    ds_ref[...] = jnp.sum(acc, axis=0, keepdims=True)
    dpool_s[pl.ds(0, T), :] = lax.dot_general(dmix_s[...], wb, NT, preferred_element_type=F32)
    dpool_s[pl.ds(T, HIST), :] = jnp.zeros((HIST, LANES), F32)
    dw_ref[...] = lax.dot_general(pooled_s[...], dmix_s[...], TN, preferred_element_type=F32).astype(BF16)

    def back_chunk(i, carry):
        r0 = pl.multiple_of(i * R, R)
        dpe = dpool_s[pl.ds(r0, R + HIST), :]
        e = dpe / _count(r0, R + HIST, w)
        du_ref[pl.ds(r0, R), :] = (_anticausal_window_sum(e, w)[0:R] - dpe[0:R]).astype(BF16)
        return carry

    lax.fori_loop(0, nchunk, back_chunk, 0)


def mix_bwd(proj, pooled, dya, dyp, wconv, wpool, pscale3, l, after):
    T = proj.shape[0]

    def body(u_ref, b_ref, c_ref, g_ref, pooled_ref, gp_ref, dya_ref, dyp_ref, wc_ref, wp_ref, s_ref, after_ref,
             dua_ref, dba_ref, dca_ref, dga_ref, dup_ref, dgp_ref, dwc_ref, dwp_ref, ds_ref,
             mixed_s, dmix_s, dpool_s):
        j = pl.program_id(0)
        pl.when(j < N_MIX)(functools.partial(_conv_bwd_block, u_ref, b_ref, c_ref, g_ref, dya_ref, wc_ref,
                                             dua_ref, dba_ref, dca_ref, dga_ref, dwc_ref))
        for k, w in enumerate(POOL_WINDOWS):
            pl.when(j == N_MIX + k)(functools.partial(_pool_bwd_group, pooled_ref, gp_ref, dyp_ref, wp_ref, s_ref,
                                                      dup_ref, dgp_ref, dwp_ref, ds_ref,
                                                      mixed_s, dmix_s, dpool_s, w))

    sec = jax.ShapeDtypeStruct((T, N_MIX * LANES), BF16)
    conv_col = pl.BlockSpec((T, LANES), lambda j: (0, _conv_idx(j)))
    pool_col = pl.BlockSpec((T, LANES), lambda j: (0, _pool_idx(j)))
    return pl.pallas_call(
        body, name="mix_bwd", grid=(2 * N_MIX,),
        in_specs=[_proj_col(T, 0, _conv_idx), _proj_col(T, 4, _conv_idx), _proj_col(T, 8, _conv_idx),
                  _proj_col(T, 12, _conv_idx), pool_col, _proj_col(T, 20, _pool_idx),
                  conv_col, pool_col, _conv_w_spec(l), _pool_w_spec(l), _pool_s_spec(l), ANY],
        out_specs=[conv_col, conv_col, conv_col, conv_col, pool_col, pool_col,
                   pl.BlockSpec((None, 3, LANES), lambda j: (_conv_idx(j), 0, 0)),
                   pl.BlockSpec((None, LANES, LANES), lambda j: (_pool_idx(j), 0, 0)),
                   pl.BlockSpec((1, LANES), lambda j: (0, _pool_idx(j)))],
        out_shape=[sec] * 6 + [jax.ShapeDtypeStruct((N_MIX, 3, LANES), F32),
                               jax.ShapeDtypeStruct((N_MIX, LANES, LANES), BF16),
                               jax.ShapeDtypeStruct((1, N_MIX * LANES), F32)],
        scratch_shapes=[pltpu.VMEM((T, LANES), F32), pltpu.VMEM((T, LANES), BF16), pltpu.VMEM((T + HIST, LANES), F32)],
        compiler_params=_params(),
    )(proj, proj, proj, proj, pooled, proj, dya, dyp, wconv, wpool, pscale3, after)


def in_bwd(dsecs, wg, x, dxo, mod4, g_pre3, l, update=None):
    T, D = x.shape
    NB = N_CHIPS
    CW = wg.shape[1] // NB
    SW = dsecs[0].shape[1]
    nsec = len(dsecs)
    PW = 256
    assert SW % PW == 0 and CW % PW == 0
    tm = 256
    nt = T // tm
    n_in = nsec + 6
    n_upd = 0 if update is None else 8
    n_acc = 0 if update is None or update[3] is None else 8

    def body(*refs):
        d_refs = refs[0:nsec]
        w_ref, x_ref, dxo_ref, sh_ref, sc_ref, g_ref = refs[nsec:n_in]
        outs = refs[n_in + n_upd + n_acc:]
        dxi_ref, dw_ref, dsh_ref, dsc_ref, dg_ref = outs[0:5]
        acc_w, acc_sh, acc_q = outs[5 + n_upd:]
        i = pl.program_id(0)

        @pl.when(i == 0)
        def _():
            acc_w[...] = jnp.zeros_like(acc_w)
            acc_sh[...] = jnp.zeros_like(acc_sh)
            acc_q[...] = jnp.zeros_like(acc_q)

        xv = x_ref[...]
        r = _rms(xv)
        xh = xv * r
        sg = g_ref[...] * (1.0 + sc_ref[...])
        hb = (xh * sg + sh_ref[...]).astype(BF16)
        dh = lax.dot_general(d_refs[0][...], w_ref[:, 0:SW], NT, preferred_element_type=F32)
        for s in range(1, nsec):
            dh = dh + lax.dot_general(d_refs[s][...], w_ref[:, s * SW:(s + 1) * SW], NT, preferred_element_type=F32)
        for p in range(nsec * SW // PW):
            col = p * PW
            s, so = col // SW, col % SW
            j, jo = col // CW, col % CW
            acc_w[j, :, jo:jo + PW] += lax.dot_general(hb, d_refs[s][:, so:so + PW], TN, preferred_element_type=F32)
            if n_upd:
                k, n_pieces, q = (4, 4, p // 3) if p % 3 == 2 else (0, 8, p // 3 * 2 + p % 3)
                piece = refs[n_in + k].shape[0] // n_pieces
                _adamw_block(refs[n_in + k:n_in + k + 4], outs[5 + k:5 + k + 4], pl.ds(q * piece, piece))
        q = dh * xh
        acc_sh[...] += _colsum8(dh)
        acc_q[...] += _colsum8(q)
        dxi_ref[...] = dxo_ref[...] + r * (dh * sg - xh * jnp.mean(q * sg, axis=-1, keepdims=True))

        @pl.when(i == nt - 1)
        def _():
            dw_ref[...] = acc_w[...].astype(BF16)
            sq = jnp.sum(acc_q[...], axis=0, keepdims=True)
            dsh_ref[...] = jnp.sum(acc_sh[...], axis=0, keepdims=True)
            dsc_ref[...] = g_ref[...] * sq
            dg_ref[...] = (1.0 + sc_ref[...]) * sq

    row = pl.BlockSpec((1, D), lambda i: (0, 0))
    tile = pl.BlockSpec((tm, D), lambda i: (i, 0))
    sect = pl.BlockSpec((tm, SW), lambda i: (i, 0))
    rowshape = jax.ShapeDtypeStruct((1, D), F32)
    in_specs = [sect] * nsec + [pl.BlockSpec((D, NB * CW), lambda i: (0, 0)), tile, tile,
                                _mod_row(l, 0, D), _mod_row(l, 1, D), _layer_row(l, D)]
    out_specs = [tile, pl.BlockSpec((NB, D, CW), lambda i: (0, 0, 0)), row, row, row]
    out_shape = [jax.ShapeDtypeStruct((T, D), F32), jax.ShapeDtypeStruct((NB, D, CW), BF16), rowshape, rowshape, rowshape]
    args = [*dsecs, wg, x, dxo, mod4, mod4, g_pre3]
    aliases = {}
    if update is not None:
        layer, of_w_in, of_w_out, acc = update
        for group in (of_w_in, of_w_out):
            _, rows, cols = group[0].shape
            spec = pl.BlockSpec((None, rows // nt, cols), lambda i: (layer, i, 0))
            in_specs += [spec] * 4
            out_specs += [spec] * 4
            out_shape += [jax.ShapeDtypeStruct(group[0].shape, F32)] * 4
            args += list(group)
        if acc is not None:
            aliases = {len(args) + a: 5 + a for a in range(n_acc)}
            in_specs += [ANY] * n_acc
            args += list(acc)
    return pl.pallas_call(
        body, name="in_bwd", grid=(nt,),
        in_specs=in_specs, out_specs=out_specs, out_shape=out_shape, input_output_aliases=aliases,
        scratch_shapes=[pltpu.VMEM((NB, D, CW), F32),
                        pltpu.VMEM((SUBLANES, D), F32), pltpu.VMEM((SUBLANES, D), F32)],
        compiler_params=_params(VMEM_BIG),
    )(*args)


def _rcopy(src, dst, ssem, rsem, dev):
    return pltpu.make_async_remote_copy(src_ref=src, dst_ref=dst, send_sem=ssem, recv_sem=rsem,
                                        device_id=dev, device_id_type=MESH)


def _peers7(x, y, c):
    out = []
    for m in range(1, N_DEV):
        bx, by, bc = (m >> 2) & 1, (m >> 1) & 1, m & 1
        out.append(((1 - x) if bx else x, (1 - y) if by else y, (1 - c) if bc else c))
    return out


HBM = pl.BlockSpec(memory_space=pltpu.HBM)
SEM = pl.BlockSpec(memory_space=pltpu.SEMAPHORE)
SPLIT = pltpu.CompilerParams(has_side_effects=pltpu.SideEffectType.DATAFLOW_SIDE_EFFECTING)


def _hbm(a):
    return pltpu.with_memory_space_constraint(a, pltpu.HBM)


def _chips(x, y):
    return [(1 - x, y), (x, 1 - y), (1 - x, 1 - y)]


SIBLING_BARRIER_ID = 0


def xchg_start(name, bufs, n_copies, plan, sibling_only=False, after=()):
    n = len(bufs)
    after = list(after)

    def body(*refs):
        ssem, rsem, token = refs[n + len(after)], refs[n + len(after) + 1], refs[-1]
        x, y, c = _me()
        if sibling_only:
            barrier = pltpu.get_barrier_semaphore()
            pl.semaphore_signal(barrier, inc=1, device_id=(x, y, 1 - c), device_id_type=MESH)
            pl.semaphore_wait(barrier, 1)
        copies = plan(refs[0:n], x, y, c)
        assert len(copies) == n_copies
        for k, (src, dst, peer, _) in enumerate(copies):
            _rcopy(src, dst, ssem.at[k], rsem.at[k], peer).start()
        token[...] = jnp.zeros_like(token)

    params = dict(has_side_effects=pltpu.SideEffectType.DATAFLOW_SIDE_EFFECTING)
    if sibling_only:
        params["collective_id"] = SIBLING_BARRIER_ID
    outs = pl.pallas_call(
        body, name=name,
        in_specs=[HBM] * n + [ANY] * len(after),
        out_specs=[SEM, SEM] + [HBM] * n + [pl.BlockSpec(memory_space=pltpu.VMEM)],
        out_shape=([pltpu.SemaphoreType.DMA((n_copies,))] * 2 + [pltpu.HBM(b.shape, b.dtype) for b in bufs]
                   + [jax.ShapeDtypeStruct((SUBLANES, LANES), F32)]),
        input_output_aliases={a: 2 + a for a in range(n)},
        compiler_params=pltpu.CompilerParams(**params),
    )(*[_hbm(b) for b in bufs], *after)
    return outs[0], outs[1], list(outs[2:2 + n]), outs[-1]


def xchg_wait(name, bufs, ssem, rsem, n_copies, plan, after, sems=None):
    n = len(bufs)
    after = list(after)
    sems = tuple(range(n_copies)) if sems is None else tuple(sems)
    assert len(sems) == n_copies

    def body(*refs):
        ssem_ref, rsem_ref = refs[n], refs[n + 1]
        copies = plan(refs[0:n], *_me())
        assert len(copies) == n_copies
        for k, (src, _, peer, land) in zip(sems, copies):
            cp = _rcopy(src, land, ssem_ref.at[k], rsem_ref.at[k], peer)
            cp.wait_send()
            cp.wait_recv()

    outs = pl.pallas_call(
        body, name=name,
        in_specs=[HBM] * n + [SEM, SEM] + [ANY] * len(after), out_specs=[HBM] * n,
        out_shape=[pltpu.HBM(b.shape, b.dtype) for b in bufs],
        input_output_aliases={a: a for a in range(n)},
        compiler_params=SPLIT,
    )(*bufs, ssem, rsem, *after)
    return list(outs)


def _shard_half(buf, chip, half):
    if len(buf.shape) == 2:
        h, w = buf.shape[0] // 2, buf.shape[1] // N_CHIPS
        return buf.at[pl.ds(half * h, h), pl.ds(chip * w, w)]
    h = buf.shape[1] // 2
    return buf.at[chip, pl.ds(half * h, h)]


def plan_gather(refs, x, y, c):
    out = []
    for buf in refs:
        own = _shard_half(buf, 2 * x + y, c)
        for (px, py) in _chips(x, y):
            out.append((own, own, (px, py, c), _shard_half(buf, 2 * px + py, c)))
    return out


def plan_forward(refs, x, y, c):
    out = []
    for (px, py) in _chips(x, y):
        for buf in refs:
            landed = _shard_half(buf, 2 * px + py, c)
            out.append((landed, landed, (x, y, 1 - c), _shard_half(buf, 2 * px + py, 1 - c)))
    return out


def plan_sibling(refs, x, y, c):
    n = len(refs) // 2
    out = []
    for a in range(n):
        h = refs[a].shape[1] // 2
        out.append((refs[a].at[:, pl.ds((1 - c) * h, h)], refs[n + a], (x, y, 1 - c), refs[n + a]))
    return out


def plan_chip(refs, x, y, c):
    n = len(refs) // 2
    out = []
    for j, (px, py) in enumerate(_chips(x, y)):
        for a in range(n):
            out.append((refs[a].at[2 * px + py], refs[n + a].at[j], (px, py, c), refs[n + a].at[j]))
    return out


def plan_mod(refs, x, y, c):
    (mods,) = refs
    mine = mods.at[2 * x + y]
    return [(mine, mine, (px, py, c), mods.at[2 * px + py]) for (px, py) in _chips(x, y)]


def plan_pack(refs, x, y, c):
    (packs,) = refs
    mine = packs.at[4 * x + 2 * y + c]
    return [(mine, mine, peer, packs.at[4 * peer[0] + 2 * peer[1] + peer[2]]) for peer in _peers7(x, y, c)]


def plan_spread(layers, wp_layers):
    def plan(refs, x, y, c):
        gi, go, gp = refs
        hD, hR, hP = gi.shape[1] // 2, go.shape[1] // 2, gp.shape[2] // 2
        sib = (x, y, 1 - c)
        out = []
        for l in layers:
            mine = gi.at[l, pl.ds(c * hD, hD)]
            out.append((mine, mine, sib, gi.at[l, pl.ds((1 - c) * hD, hD)]))
            mine = go.at[l, pl.ds(c * hR, hR)]
            out.append((mine, mine, sib, go.at[l, pl.ds((1 - c) * hR, hR)]))
        for l in wp_layers:
            mine = gp.at[l, 2 * x + y, pl.ds(c * hP, hP)]
            for peer in _peers7(x, y, c):
                out.append((mine, mine, peer, gp.at[l, 2 * peer[0] + peer[1], pl.ds(peer[2] * hP, hP)]))
        return out

    return plan


def place_small(pos, c8, wc):
    L = wc.shape[0]

    def body(pos_ref, c_ref, wc_ref, call_ref, wcall_ref):
        call_ref[...] = c_ref[...]
        wcall_ref[...] = wc_ref[...]

    return pl.pallas_call(
        body, name="place_small",
        grid_spec=pltpu.PrefetchScalarGridSpec(
            num_scalar_prefetch=1, grid=(1,),
            in_specs=[pl.BlockSpec((SUBLANES, LANES), lambda i, p: (0, 0)),
                      pl.BlockSpec((L, 3, LANES), lambda i, p: (0, 0, 0))],
            out_specs=[pl.BlockSpec((None, SUBLANES, LANES), lambda i, p: (p[2], 0, 0)),
                       pl.BlockSpec((None, L, 3, LANES), lambda i, p: (p[1], 0, 0, 0))]),
        out_shape=[jax.ShapeDtypeStruct((N_DEV, SUBLANES, LANES), F32),
                   jax.ShapeDtypeStruct((N_CHIPS, L, 3, LANES), F32)],
        compiler_params=_params(),
    )(pos, c8, wc)


def plan_small(refs, x, y, c):
    call, wcall = refs
    mine = call.at[4 * x + 2 * y + c]
    out = [(mine, mine, peer, call.at[4 * peer[0] + 2 * peer[1] + peer[2]]) for peer in _peers7(x, y, c)]
    mine = wcall.at[2 * x + y]
    out += [(mine, mine, (px, py, c), wcall.at[2 * px + py]) for (px, py) in _chips(x, y)]
    return out


def add_sibling(cidx, mine, sib):
    def body(c_ref, *refs):
        for a in range(3):
            m, s, o = refs[a], refs[3 + a], refs[6 + a]
            o[...] = (m[...].astype(F32) + s[...].astype(F32)).astype(BF16)

    per_step = 2

    def mine_spec(a):
        h = a.shape[1] // 2
        return pl.BlockSpec((per_step, h, a.shape[2]), lambda j, c_ref: (j, c_ref[0], 0))

    def sib_spec(a):
        return pl.BlockSpec((per_step,) + a.shape[1:], lambda j, c_ref: (j, 0, 0))

    return pl.pallas_call(
        body, name="add_sibling",
        grid_spec=pltpu.PrefetchScalarGridSpec(
            num_scalar_prefetch=1, grid=(N_CHIPS // per_step,),
            in_specs=[mine_spec(a) for a in mine] + [sib_spec(a) for a in sib],
            out_specs=[sib_spec(a) for a in sib]),
        out_shape=[jax.ShapeDtypeStruct(a.shape, BF16) for a in sib],
        compiler_params=_params(VMEM_BIG),
    )(cidx, *mine, *sib)


def sum_chips(pos, own, rb, acc, l, shapes):
    nq = 2
    n_in = 6 + (3 if acc is not None else 0)

    def body(pos_ref, *refs):
        for a in range(3):
            m, b, o = refs[a], refs[3 + a], refs[n_in + a]
            s = m[...].astype(F32)
            for j in range(3):
                s = s + b[j].astype(F32)
            o[...] = s

    def own_spec(a):
        return pl.BlockSpec((None, a.shape[1] // nq, a.shape[2]), lambda q, p: (p[1], q, 0))

    def rb_spec(a):
        return pl.BlockSpec((3, a.shape[1] // nq, a.shape[2]), lambda q, p: (0, q, 0))

    hi, ho, hp = own[0].shape[1] // nq, own[1].shape[1] // nq, own[2].shape[1] // nq
    out_specs = [pl.BlockSpec((None, hi, shapes[0][2]), lambda q, p: (l, p[0] * nq + q, 0)),
                 pl.BlockSpec((None, ho, shapes[1][2]), lambda q, p: (l, p[0] * nq + q, 0)),
                 pl.BlockSpec((None, None, hp, LANES), lambda q, p: (l, p[1], p[0] * nq + q, 0))]
    in_specs = [own_spec(a) for a in own] + [rb_spec(a) for a in rb]
    args = list(own) + list(rb)
    aliases = {}
    if acc is not None:
        in_specs += [ANY] * 3
        args += list(acc)
        aliases = {7: 0, 8: 1, 9: 2}
    return pl.pallas_call(
        body, name="sum_chips",
        grid_spec=pltpu.PrefetchScalarGridSpec(num_scalar_prefetch=1, grid=(nq,), in_specs=in_specs, out_specs=out_specs),
        out_shape=[jax.ShapeDtypeStruct(s, F32) for s in shapes],
        input_output_aliases=aliases,
        compiler_params=_params(VMEM_BIG),
    )(pos, *args)


def _wconv_slot(chip, tap):
    idx = 3 * chip + tap
    return ROW_WCONV + idx // SUBLANES, slice((idx % SUBLANES) * LANES, (idx % SUBLANES + 1) * LANES)


def pack_small(pos, per_layer, loss_blk):
    L = len(per_layer)
    D = per_layer[0][0].shape[1]

    def body(pos_ref, *refs):
        o = refs[-1]
        lb = refs[-2]
        o[...] = jnp.zeros_like(o)
        for l in range(L):
            dgpre, dgpost, dsh, dsc, dgt, dps, dwc = refs[7 * l:7 * l + 7]
            base = SUBLANES * l
            o[pl.ds(base + ROW_G_PRE, 1), :] = dgpre[...]
            o[pl.ds(base + ROW_G_POST, 1), :] = dgpost[...]
            for r, src in enumerate((dsh, dsc, dgt)):
                o[pl.ds(base + ROW_MOD + r, 1), :] = src[...]
            o[pl.ds(base + ROW_PSCALE, 1), 0:dps.shape[1]] = dps[...]
            for j in range(dwc.shape[0]):
                for k in range(3):
                    row, lanes = _wconv_slot(j, k)
                    o[pl.ds(base + row, 1), lanes] = dwc[j, pl.ds(k, 1), :]
        o[pl.ds(ROW_PSCALE, 1), LOSS_LANES] = lb[pl.ds(0, 1), :]

    flat = [a for layer in per_layer for a in layer] + [loss_blk]

    def whole(a):
        return pl.BlockSpec(a.shape, lambda i, p: (0,) * a.ndim)

    return pl.pallas_call(
        body, name="pack_small",
        grid_spec=pltpu.PrefetchScalarGridSpec(
            num_scalar_prefetch=1, grid=(1,), in_specs=[whole(a) for a in flat],
            out_specs=pl.BlockSpec((None, L * SUBLANES, D), lambda i, p: (p[2], 0, 0))),
        out_shape=jax.ShapeDtypeStruct((N_DEV, L * SUBLANES, D), F32),
        compiler_params=_params(),
    )(pos, *flat)


def small_update(pos, packs, params, moments_m, moments_v):
    n = len(params)
    L, D = params[1].shape
    PS = params[3].shape[1]

    def body(pos_ref, p_ref, *refs):
        ws, ms, vs = refs[0:n], refs[n:2 * n], refs[2 * n:3 * n]
        loss_ref = refs[3 * n]
        outs = [refs[3 * n + 1 + 4 * t:3 * n + 5 + 4 * t] for t in range(n)]
        summed = refs[-1]
        s = p_ref[0]
        for d in range(1, N_DEV):
            s = s + p_ref[d]
        summed[...] = s
        loss_ref[...] = summed[pl.ds(ROW_PSCALE, 1), LOSS_LANES]
        chip = pos_ref[1]

        def update(t, idx, g):
            d, mm, vv = _adamw_math(ws[t][idx], g, ms[t][idx], vs[t][idx])
            g_ref, d_ref, mo_ref, vo_ref = outs[t]
            g_ref[idx] = g
            d_ref[idx] = d
            mo_ref[idx] = mm
            vo_ref[idx] = vv

        for l in range(L):
            base = SUBLANES * l
            row = pl.ds(l, 1)
            for k in range(3):
                update(0, (row, slice(k * D, (k + 1) * D)), summed[pl.ds(base + ROW_MOD + k, 1), :])
            update(1, (row, slice(None)), summed[pl.ds(base + ROW_G_PRE, 1), :])
            update(2, (row, slice(None)), summed[pl.ds(base + ROW_G_POST, 1), :])
            update(3, (row, slice(None)), summed[pl.ds(base + ROW_PSCALE, 1), 0:PS])
            for k in range(3):
                g = None
                for j in range(N_CHIPS):
                    wrow, lanes = _wconv_slot(j, k)
                    cand = summed[pl.ds(base + wrow, 1), lanes]
                    g = cand if g is None else jnp.where(chip == j, cand, g)
                update(4, (l, pl.ds(k, 1), slice(None)), g)

    def whole(a):
        return pl.BlockSpec(a.shape, lambda i, p: (0,) * a.ndim)

    ins = [packs] + list(params) + list(moments_m) + list(moments_v)
    out_shape = [jax.ShapeDtypeStruct((1, LANES), F32)]
    for w in params:
        out_shape += [jax.ShapeDtypeStruct(w.shape, F32)] * 4
    outs = pl.pallas_call(
        body, name="small_update",
        grid_spec=pltpu.PrefetchScalarGridSpec(
            num_scalar_prefetch=1, grid=(1,), in_specs=[whole(a) for a in ins],
            out_specs=[whole(a) for a in out_shape],
            scratch_shapes=[pltpu.VMEM(packs.shape[1:], F32)]),
        out_shape=out_shape,
        compiler_params=_params(),
    )(pos, *ins)
    return outs[0], [outs[1 + 4 * t:5 + 4 * t] for t in range(n)]


def _adamw_math(w, g, m, v):
    m = ADAM_B1 * m + (1.0 - ADAM_B1) * g
    v = ADAM_B2 * v + (1.0 - ADAM_B2) * (g * g)
    m_hat = m / (1.0 - ADAM_B1 ** ADAM_STEP)
    v_hat = v / (1.0 - ADAM_B2 ** ADAM_STEP)
    delta = -ADAM_LR * (m_hat / (jnp.sqrt(v_hat) + ADAM_EPS) + ADAM_WD * w)
    return delta, m, v


def _adamw_block(ins, outs, rows=Ellipsis):
    w_ref, g_ref, m_ref, v_ref = ins
    go_ref, d_ref, mo_ref, vo_ref = outs
    gv = g_ref[rows]
    d, mm, vv = _adamw_math(w_ref[rows], gv, m_ref[rows], v_ref[rows])
    go_ref[rows] = gv
    d_ref[rows] = d
    mo_ref[rows] = mm
    vo_ref[rows] = vv


def adamw(groups, name, first, count, steps, acc=None):
    n = len(groups)

    def body(*refs):
        outs = refs[len(refs) - 4 * n:]
        for k in range(n):
            _adamw_block(refs[4 * k:4 * k + 4], outs[4 * k:4 * k + 4])

    specs, out_shape, args = [], [], []
    for group in groups:
        shape = group[0].shape
        spec = pl.BlockSpec((1, shape[1] // steps) + shape[2:],
                            lambda i, s, rest=(0,) * (len(shape) - 2): (first + i, s) + rest)
        specs += [spec] * 4
        out_shape += [jax.ShapeDtypeStruct(shape, F32)] * 4
        args += list(group)
    extra = [] if acc is None else list(acc)
    return pl.pallas_call(
        body, name=name, grid=(count, steps),
        in_specs=specs + [ANY] * len(extra), out_specs=specs, out_shape=out_shape,
        input_output_aliases={4 * n + a: a for a in range(len(extra))},
        compiler_params=_params(VMEM_BIG, n_grid=2),
    )(*args, *extra)


def ada_finish(c_all, dmod, w, m, v):
    L, D, CW = w.shape
    hD = D // 2

    def body(c_ref, d_ref, w_ref, m_ref, v_ref, g_ref, dl_ref, mo_ref, vo_ref):
        cv = c_ref[...]
        z = jnp.zeros_like(cv)
        ca = jnp.concatenate([cv * jax.nn.sigmoid(cv), z], axis=0).astype(BF16)
        dm = jnp.concatenate([d_ref[0], jnp.zeros_like(d_ref[0])], axis=0).astype(BF16)
        g = lax.dot_general(ca, dm, TN, preferred_element_type=F32)
        g_ref[0] = g
        d, mm, vv = _adamw_math(w_ref[0], g, m_ref[0], v_ref[0])
        dl_ref[0] = d
        mo_ref[0] = mm
        vo_ref[0] = vv

    big = pl.BlockSpec((1, hD, CW), lambda l, h: (l, h, 0))
    shape = jax.ShapeDtypeStruct(w.shape, F32)
    return pl.pallas_call(
        body, name="ada_finish", grid=(L, 2),
        in_specs=[pl.BlockSpec((N_DEV, hD), lambda l, h: (0, h)), pl.BlockSpec((1, N_DEV, CW), lambda l, h: (l, 0, 0)),
                  big, big, big],
        out_specs=[big] * 4, out_shape=[shape] * 4,
        compiler_params=_params(VMEM_BIG, n_grid=2),
    )(c_all, dmod, w, m, v)


def kernel(x, c, w_ada, b_ada, g_pre, w_in, w_conv, w_pool, pool_scale, w_out, g_post, loss_target, m_w_ada, m_b_ada, m_g_pre, m_w_in, m_w_conv, m_w_pool, m_pool_scale, m_w_out, m_g_post, v_w_ada, v_b_ada, v_g_pre, v_w_in, v_w_conv, v_w_pool, v_pool_scale, v_w_out, v_g_post):
    L, D, CW = w_in.shape
    RO = w_out.shape[1]
    T = x.shape[1]
    ix, iy, ic = _me()
    chip = 2 * ix + iy
    me_lin = 4 * ix + 2 * iy + ic

    pos = jnp.stack([ic, chip, me_lin]).astype(jnp.int32)
    n_s, n_c = 3, 9

    def gather(bufs, after):
        ss, rs, bufs, tok = xchg_start("gather_start", bufs, 3 * len(bufs), plan_gather, after=after)
        return (ss, rs, bufs), tok

    def ready(flight, after):
        fss, frs, bufs = flight
        return xchg_wait("forward_wait", bufs, fss, frs, 3 * len(bufs), plan_forward, after)

    def arrive_part(flight, which, after, base=0):
        ss, rs, bufs = flight
        sems = tuple(range(base + 3 * which, base + 3 * which + 3))
        (buf,) = xchg_wait("gather_wait", [bufs[which]], ss, rs, 3, plan_gather, after, sems=sems)
        fss, frs, (buf,), tok = xchg_start("forward_start", [buf], 3, plan_forward, sibling_only=True)
        return (fss, frs, [buf]), tok

    n_small = N_DEV - 1 + N_CHIPS - 1
    w_in_of, w_out_of = [None] * L, [None] * L
    gi0, go0 = cast_weights(pos, w_in, w_out, 0, pos)

    def plan_first(refs, x, y, c):
        return plan_small(refs[0:2], x, y, c) + plan_gather(refs[2:3], x, y, c)

    placed = list(place_small(pos, c.reshape(SUBLANES, LANES), w_conv))
    s_ss, s_rs, firsts, token = xchg_start("first_start", placed + [gi0], n_small + 3, plan_first)
    smalls_in = firsts[0:2]
    w_in_of[0] = ((s_ss, s_rs, firsts[2:3]), 0)
    g_pre_l, g_post_l, pscale_l, b_ada_l, m_w_conv_l, v_w_conv_l, token = lax.optimization_barrier(
        (g_pre, g_post, pool_scale, b_ada, m_w_conv, v_w_conv, token))
    g_pre3, g_post3 = g_pre_l.reshape(L, 1, D), g_post_l.reshape(L, 1, D)
    pscale3 = pscale_l.reshape(L, 1, pool_scale.shape[1])
    gi1, go1 = cast_weights(pos, w_in, w_out, 1, token)
    c_all3, wconv_all = xchg_wait("small_wait", smalls_in, s_ss, s_rs, n_small, plan_small, [token, go1])
    c_all = c_all3.reshape(N_DEV, D)
    b_my = lax.dynamic_slice_in_dim(b_ada_l, chip * CW, CW, axis=1)
    m_ss, m_rs, mods, token = xchg_start("mod_start", [mod_part(pos, c_all, w_ada, b_my, token)], 3, plan_mod)
    flight, token = gather([go0, gi1, go1], [token])
    w_out_of[0], w_in_of[1], w_out_of[1] = (flight, 0), (flight, 1), (flight, 2)
    late = []
    for l in range(2, L):
        late += list(cast_weights(pos, w_in, w_out, l, token))
    flight, token = gather(late, [])
    for l in range(2, L):
        w_in_of[l], w_out_of[l] = (flight, 2 * (l - 2)), (flight, 2 * (l - 2) + 1)
    fwd_in, token = arrive_part(*w_in_of[0], [token], base=n_small)
    (mod_all,) = xchg_wait("mod_wait", mods, m_ss, m_rs, 3, plan_mod, [token])
    mod = lax.dynamic_index_in_dim(mod_all, me_lin, axis=2, keepdims=False)
    mod4 = jnp.transpose(mod, (1, 0, 2)).reshape(L, 3, 1, D)

    xs, projs, yas, yps, ys, pooleds = [x.reshape(T, D)], [], [], [], [], []
    wg_in, wg_out = [], []
    for l in range(L):
        (gi,) = ready(fwd_in, [mod4 if l == 0 else xs[l]])
        proj = proj_fwd(xs[l], mod4, g_pre3, gi, l)
        ya, yp, pooled = mix_fwd(proj, wconv_all, w_pool, pscale3, l)
        pooleds.append(pooled)
        fwd_out, token = arrive_part(*w_out_of[l], [ya, yp])
        after = [token]
        if 0 < l < L - 1:
            fwd_in, token = arrive_part(*w_in_of[l + 1], after)
            after = [token]
        (go,) = ready(fwd_out, after)
        wg_in.append(gi)
        wg_out.append(go.reshape(N_CHIPS * RO, D))
        projs.append(proj)
        yas.append(ya)
        yps.append(yp)
        if l + 1 < L:
            xn, yv = out_fwd(ya, yp, wg_out[l], xs[l], mod4, g_post3, l, after[0])
            xs.append(xn)
            if l == 0:
                fwd_in, token = arrive_part(*w_in_of[1], [xn])
        else:
            dx, yv, loss_blk = out_fwd_loss(ya, yp, wg_out[l], xs[l], mod4, g_post3, l, loss_target.reshape(T, D))
        ys.append(yv)

    shapes = (w_in.shape, w_out.shape, w_pool.shape)
    smalls = [None] * L
    acc, flying, sib, token = None, None, None, loss_blk

    def to_chips(sib, after):
        sl, s_ss, s_rs, s_bufs = sib
        s_bufs = xchg_wait("sibling_wait", s_bufs, s_ss, s_rs, n_s, plan_sibling, after)
        chip_parts = add_sibling(pos, s_bufs[0:3], s_bufs[3:6])
        lands = [lax.empty((3,) + a.shape[1:], a.dtype) for a in chip_parts]
        c_ss, c_rs, c_bufs, ctoken = xchg_start("chip_start", list(chip_parts) + lands, n_c, plan_chip)
        return (sl, c_ss, c_rs, c_bufs), ctoken

    def landed(flying, acc, after):
        fl, f_ss, f_rs, f_bufs = flying
        f_bufs = xchg_wait("chip_wait", f_bufs, f_ss, f_rs, n_c, plan_chip, after)
        return sum_chips(pos, f_bufs[0:3], f_bufs[3:6], acc, fl, shapes)

    early = None
    for l in reversed(range(L)):
        dya, dyp, dwo_l, dgate, dgpost = out_bwd(dx, ys[l], yas[l], yps[l], wg_out[l], mod4, g_post3, l, token)
        token = dya
        spreading = None
        if sib is not None:
            arrived = flying
            flying, token = to_chips(sib, [dya])
            if arrived is not None:
                acc = landed(arrived, acc, [token])
                spreading = plan_spread((arrived[0],), ())
                sp_ss, sp_rs, acc, token = xchg_start("spread_start", list(acc), 2, spreading, sibling_only=True)
        du_a, db_a, dc_a, dg_a, du_p, dg_p, dwc, dwp_l, dps = mix_bwd(projs[l], pooleds[l], dya, dyp, wconv_all, w_pool,
                                                                        pscale3, l, token)
        update = None
        if spreading is not None:
            acc = xchg_wait("spread_wait", acc, sp_ss, sp_rs, 2, spreading, [du_a])
            update = (arrived[0], [w_in, acc[0], m_w_in, v_w_in], [w_out, acc[1], m_w_out, v_w_out], early)
        dx, dwi_l, dshift, dscale, dgpre, *rest = in_bwd([du_a, db_a, dc_a, dg_a, du_p, dg_p], wg_in[l], xs[l], dx,
                                                         mod4, g_pre3, l, update)
        early = rest if rest else early
        smalls[l] = (dgpre, dgpost, dshift, dscale, dgate, dps, dwc)
        parts = [dwi_l, dwo_l.reshape(N_CHIPS, RO, D), dwp_l]
        s_lands = [lax.empty((a.shape[0], a.shape[1] // 2) + a.shape[2:], a.dtype) for a in parts]
        s_ss, s_rs, s_bufs, token = xchg_start("sibling_start", parts + s_lands, n_s, plan_sibling, sibling_only=True)
        sib = (l, s_ss, s_rs, s_bufs)
    grad_x = dx.reshape(1, T, D)

    p_ss, p_rs, packs, ptoken = xchg_start("pack_start", [pack_small(pos, smalls, loss_blk)], N_DEV - 1, plan_pack)
    acc = landed(flying, acc, [ptoken, token])
    n_sp = 2 + (N_DEV - 1) * (L - 1)
    spread = plan_spread((1,), tuple(range(1, L)))
    sp_ss, sp_rs, acc, sp_token = xchg_start("spread_start", list(acc), n_sp, spread)
    flying, token = to_chips(sib, [sp_token])
    (packs_all,) = xchg_wait("pack_wait", packs, p_ss, p_rs, N_DEV - 1, plan_pack, [token])
    dmod_all = packs_all.reshape(N_DEV, L, SUBLANES, D)[:, :, ROW_MOD:ROW_MOD + 3].reshape(N_DEV, L, 3 * D)
    dmod_my = jnp.transpose(lax.dynamic_slice_in_dim(dmod_all, chip * CW, CW, axis=2), (1, 0, 2))

    g_w_ada, d_w_ada, nm_w_ada, nv_w_ada = ada_finish(c_all, dmod_my, w_ada, m_w_ada, v_w_ada)
    packs_late, _ = lax.optimization_barrier((packs_all, nv_w_ada))
    loss_row, upd = small_update(pos, packs_late, [b_ada, g_pre, g_post, pool_scale, w_conv],
                                 [m_b_ada, m_g_pre, m_g_post, m_pool_scale, m_w_conv_l],
                                 [v_b_ada, v_g_pre, v_g_post, v_pool_scale, v_w_conv_l])
    loss = loss_row[0, 0]
    (g_b_ada, d_b_ada, nm_b_ada, nv_b_ada), (g_g_pre, d_g_pre, nm_g_pre, nv_g_pre) = upd[0], upd[1]
    (g_g_post, d_g_post, nm_g_post, nv_g_post), (g_pscale, d_pscale, nm_pscale, nv_pscale) = upd[2], upd[3]
    g_w_conv, d_w_conv, nm_w_conv, nv_w_conv = upd[4]

    done = [nv_w_ada, nv_w_conv]
    g_w_in, g_w_out, g_w_pool = xchg_wait("spread_wait", acc, sp_ss, sp_rs, n_sp, spread, done)
    early = adamw([[w_in, g_w_in, m_w_in, v_w_in], [w_out, g_w_out, m_w_out, v_w_out]], "adamw_layer", 1, 1, 2, early)

    acc = landed(flying, (g_w_in, g_w_out, g_w_pool), [early[3], early[7]])
    last = plan_spread((0,), (0,))
    n_last = 2 + N_DEV - 1
    l_ss, l_rs, acc, _ = xchg_start("spread_start", list(acc), n_last, last)
    upd_pool = adamw([[w_pool, acc[2], m_w_pool, v_w_pool]], "adamw_w_pool", 1, L - 1, 1)
    r_w_in, r_w_out, r_w_pool = xchg_wait("spread_wait", acc, l_ss, l_rs, n_last, last, [upd_pool[3]])
    (g_w_in, d_w_in, nm_w_in, nv_w_in, g_w_out, d_w_out, nm_w_out, nv_w_out,
     g_w_pool, d_w_pool, nm_w_pool, nv_w_pool) = adamw(
         [[w_in, r_w_in, m_w_in, v_w_in], [w_out, r_w_out, m_w_out, v_w_out], [w_pool, r_w_pool, m_w_pool, v_w_pool]],
         "adamw_layer", 0, 1, 2, list(early) + list(upd_pool))

    return (loss, grad_x,
            g_w_ada, g_b_ada, g_g_pre, g_w_in, g_w_conv, g_w_pool, g_pscale, g_w_out, g_g_post,
            d_w_ada, d_b_ada, d_g_pre, d_w_in, d_w_conv, d_w_pool, d_pscale, d_w_out, d_g_post,
            nm_w_ada, nm_b_ada, nm_g_pre, nm_w_in, nm_w_conv, nm_w_pool, nm_pscale, nm_w_out, nm_g_post,
            nv_w_ada, nv_b_ada, nv_g_pre, nv_w_in, nv_w_conv, nv_w_pool, nv_pscale, nv_w_out, nv_g_post)
```
